```python
import jax, jax.numpy as jnp
from jax import lax
import numpy as np

D_MODEL = 1024
BATCH = 8
SEQ = 4096
DEPTH = 2

N_A_LAYERS = DEPTH // 2
N_B_LAYERS = DEPTH - N_A_LAYERS

A_KEY_DIM = 128
A_HEADS = D_MODEL // A_KEY_DIM
A_VAL_DIM = D_MODEL // A_HEADS
A_DK = A_HEADS * A_KEY_DIM
A_DV = A_HEADS * A_VAL_DIM
A_CHUNK = 64

B_HEAD_DIM = 128
B_HEADS = D_MODEL // B_HEAD_DIM
B_WIDTH = B_HEADS * B_HEAD_DIM
B_Q_BLOCK = 128

D_FF = ((8 * D_MODEL // 3 + 127) // 128) * 128
CONV_W = 3
EPS = 1e-6
NEG_INF = -1e30

kernel_name = "yoco_hgrn2_fox_adaln_convffn"


def rms_norm(x):
    xf = x.astype(jnp.float32)
    y = xf * lax.rsqrt(jnp.mean(xf * xf, axis=-1, keepdims=True) + EPS)
    return y.astype(x.dtype)


def modulate(x, shift, scale):
    return rms_norm(x) * (1 + scale[:, None, :]) + shift[:, None, :]


def hgrn2_mixer(h, w_in, lb, norm_g, w_out):
    bsz, seq, _ = h.shape
    n_chunks = seq // A_CHUNK
    proj = (h @ w_in).astype(jnp.float32)
    q, f, i, g = jnp.split(proj, [A_DK, 2 * A_DK, 2 * A_DK + A_DV], axis=-1)
    q = jax.nn.silu(q)
    fg = lb + (1 - lb) * jax.nn.sigmoid(f)
    log_f = jnp.log(fg)
    k = 1 - fg

    def to_chunks(t, d):
        return t.reshape(bsz, n_chunks, A_CHUNK, A_HEADS, d).transpose(0, 3, 1, 2, 4)

    q, k, log_f = to_chunks(q, A_KEY_DIM), to_chunks(k, A_KEY_DIM), to_chunks(log_f, A_KEY_DIM)
    v = to_chunks(i, A_VAL_DIM)
    b = jnp.cumsum(log_f, axis=3)
    b_mid = b[:, :, :, A_CHUNK // 2:A_CHUNK // 2 + 1, :]
    q_intra = q * jnp.exp(b - b_mid)
    k_intra = k * jnp.exp(b_mid - b)
    scores = jnp.einsum('bhnck,bhnsk->bhncs', q_intra, k_intra)
    causal = jnp.tril(jnp.ones((A_CHUNK, A_CHUNK), dtype=bool))
    scores = jnp.where(causal, scores, 0.0)
    o_intra = jnp.einsum('bhncs,bhnsv->bhncv', scores, v)
    b_last = b[:, :, :, -1:, :]
    q_inter = q * jnp.exp(b)
    k_state = k * jnp.exp(b_last - b)
    chunk_decay = jnp.exp(b_last[:, :, :, 0, :])

    def step(state, xs):
        qc, kc, vc, dc = xs
        o = jnp.einsum('bhck,bhkv->bhcv', qc, state)
        state = dc[..., None] * state + jnp.einsum('bhck,bhcv->bhkv', kc, vc)
        return state, o

    xs = (jnp.moveaxis(q_inter, 2, 0), jnp.moveaxis(k_state, 2, 0),
          jnp.moveaxis(v, 2, 0), jnp.moveaxis(chunk_decay, 2, 0))
    state0 = jnp.zeros((bsz, A_HEADS, A_KEY_DIM, A_VAL_DIM), jnp.float32)
    _, o_inter = lax.scan(step, state0, xs)
    o = o_intra + jnp.moveaxis(o_inter, 0, 2)
    o = o.transpose(0, 2, 3, 1, 4).reshape(bsz, seq, A_HEADS, A_VAL_DIM)
    gate = jax.nn.silu(g.reshape(bsz, seq, A_HEADS, A_VAL_DIM))
    o = rms_norm(o) * norm_g * gate
    return o.reshape(bsz, seq, A_DV).astype(h.dtype) @ w_out


def shared_kv(x, c, kv_ada_w, kv_ada_b, kv_w, kv_b_f, k_norm_g):
    bsz, seq, _ = x.shape
    shift, scale = jnp.split(jax.nn.silu(c) @ kv_ada_w + kv_ada_b, 2, axis=-1)
    h = modulate(x, shift, scale)
    proj = h @ kv_w
    k, v, f_logit = jnp.split(proj, [B_WIDTH, 2 * B_WIDTH], axis=-1)
    k = rms_norm(k.reshape(bsz, seq, B_HEADS, B_HEAD_DIM)) * k_norm_g
    k = k.transpose(0, 2, 1, 3)
    v = v.reshape(bsz, seq, B_HEADS, B_HEAD_DIM).transpose(0, 2, 1, 3)
    log_f = jax.nn.log_sigmoid((f_logit + kv_b_f).astype(jnp.float32))
    cum_log_f = jnp.cumsum(log_f.transpose(0, 2, 1), axis=-1)
    return k, v, cum_log_f


def fox_mixer(h, k, v, cum_log_f, w_q, q_norm_g, w_out):
    bsz, seq, _ = h.shape
    n_blocks = seq // B_Q_BLOCK
    q, og = jnp.split(h @ w_q, 2, axis=-1)
    q = rms_norm(q.reshape(bsz, seq, B_HEADS, B_HEAD_DIM)) * q_norm_g * (B_HEAD_DIM ** -0.5)
    q_blocks = q.reshape(bsz, n_blocks, B_Q_BLOCK, B_HEADS, B_HEAD_DIM).transpose(1, 0, 3, 2, 4)
    f_q = cum_log_f.reshape(bsz, B_HEADS, n_blocks, B_Q_BLOCK).transpose(2, 0, 1, 3)
    starts = jnp.arange(n_blocks) * B_Q_BLOCK
    key_pos = jnp.arange(seq)

    def attend_block(args):
        qb, fq, start = args
        q_pos = start + jnp.arange(B_Q_BLOCK)
        logits = jnp.einsum('bhqd,bhkd->bhqk', qb, k).astype(jnp.float32)
        logits = logits + (fq[..., :, None] - cum_log_f[:, :, None, :])
        logits = jnp.where(key_pos[None, :] <= q_pos[:, None], logits, NEG_INF)
        p = jax.nn.softmax(logits, axis=-1)
        return jnp.einsum('bhqk,bhkd->bhqd', p.astype(v.dtype), v)

    o = lax.map(attend_block, (q_blocks, f_q, starts))
    o = o.transpose(1, 0, 3, 2, 4).reshape(bsz, seq, B_WIDTH)
    o = o * jax.nn.sigmoid(og)
    return o @ w_out


def conv_glu_ffn(h, w_up, conv_w, conv_b, w_down):
    u = h @ w_up
    u = lax.conv_general_dilated(
        u, conv_w[:, None, :], window_strides=(1,), padding=[(CONV_W - 1, 0)],
        dimension_numbers=('NWC', 'WIO', 'NWC'), feature_group_count=2 * D_FF) + conv_b
    gate, val = jnp.split(u, 2, axis=-1)
    return (jax.nn.silu(gate) * val) @ w_down


def _fwd_setup_inputs(seed: int = 0) -> dict:
    key = jax.random.key(seed)
    ks = jax.random.split(key, 20)
    f32 = jnp.float32
    D = D_MODEL

    def nrm(k, shape, scale):
        return jax.random.normal(k, shape, f32) * scale

    return {
        "x": nrm(ks[0], (BATCH, SEQ, D), 1.0),
        "c": nrm(ks[1], (BATCH, D), 1.0),
        "ada_w": nrm(ks[2], (DEPTH, D, 6 * D), 0.5 * D ** -0.5),
        "ada_b": nrm(ks[3], (DEPTH, 6 * D), 0.02),
        "a_w_in": nrm(ks[4], (N_A_LAYERS, D, 2 * A_DK + 2 * A_DV), D ** -0.5),
        "a_lb_logits": nrm(ks[5], (N_A_LAYERS + 1, A_DK), 0.1),
        "a_norm_g": 1.0 + nrm(ks[6], (N_A_LAYERS, A_VAL_DIM), 0.02),
        "a_w_out": nrm(ks[7], (N_A_LAYERS, A_DV, D), A_DV ** -0.5),
        "kv_ada_w": nrm(ks[8], (D, 2 * D), 0.5 * D ** -0.5),
        "kv_ada_b": nrm(ks[9], (2 * D,), 0.02),
        "kv_w": nrm(ks[10], (D, 2 * B_WIDTH + B_HEADS), D ** -0.5),
        "kv_b_f": 3.0 + nrm(ks[11], (B_HEADS,), 0.5),
        "k_norm_g": 1.0 + nrm(ks[12], (B_HEAD_DIM,), 0.02),
        "b_w_q": nrm(ks[13], (N_B_LAYERS, D, 2 * B_WIDTH), D ** -0.5),
        "q_norm_g": 1.0 + nrm(ks[14], (N_B_LAYERS, B_HEAD_DIM), 0.02),
        "b_w_out": nrm(ks[15], (N_B_LAYERS, B_WIDTH, D), B_WIDTH ** -0.5),
        "ffn_w_up": nrm(ks[16], (DEPTH, D, 2 * D_FF), D ** -0.5),
        "ffn_conv_w": nrm(ks[17], (DEPTH, CONV_W, 2 * D_FF), CONV_W ** -0.5),
        "ffn_conv_b": nrm(ks[18], (DEPTH, 2 * D_FF), 0.02),
        "ffn_w_down": nrm(ks[19], (DEPTH, D_FF, D), D_FF ** -0.5),
    }


def _fwd_reference(x, c, ada_w, ada_b, a_w_in, a_lb_logits, a_norm_g, a_w_out,
              kv_ada_w, kv_ada_b, kv_w, kv_b_f, k_norm_g,
              b_w_q, q_norm_g, b_w_out,
              ffn_w_up, ffn_conv_w, ffn_conv_b, ffn_w_down):
    lb_all = jnp.cumsum(jax.nn.softmax(a_lb_logits.astype(jnp.float32), axis=0), axis=0)[:N_A_LAYERS]
    c_act = jax.nn.silu(c)
    k_sh = v_sh = cum_log_f = None
    for l in range(DEPTH):
        mod = c_act @ ada_w[l] + ada_b[l]
        sh1, sc1, g1, sh2, sc2, g2 = jnp.split(mod, 6, axis=-1)
        if l == N_A_LAYERS:
            k_sh, v_sh, cum_log_f = shared_kv(x, c, kv_ada_w, kv_ada_b, kv_w, kv_b_f, k_norm_g)
        h = modulate(x, sh1, sc1)
        if l < N_A_LAYERS:
            y = hgrn2_mixer(h, a_w_in[l], lb_all[l], a_norm_g[l], a_w_out[l])
        else:
            j = l - N_A_LAYERS
            y = fox_mixer(h, k_sh, v_sh, cum_log_f, b_w_q[j], q_norm_g[j], b_w_out[j])
        x = x + g1[:, None, :] * y
        h = modulate(x, sh2, sc2)
        x = x + g2[:, None, :] * conv_glu_ffn(h, ffn_w_up[l], ffn_conv_w[l], ffn_conv_b[l], ffn_w_down[l])
    return x


import jax as _jax
import jax.numpy as _jnp

TWIN_FORMAT = 'train_step'
FWD_PARAMS = ['x', 'c', 'ada_w', 'ada_b', 'a_w_in', 'a_lb_logits', 'a_norm_g', 'a_w_out', 'kv_ada_w', 'kv_ada_b', 'kv_w', 'kv_b_f', 'k_norm_g', 'b_w_q', 'q_norm_g', 'b_w_out', 'ffn_w_up', 'ffn_conv_w', 'ffn_conv_b', 'ffn_w_down']
TWIN_WEIGHTS = ['ada_w', 'ada_b', 'a_w_in', 'a_lb_logits', 'a_norm_g', 'a_w_out', 'kv_ada_w', 'kv_ada_b', 'kv_w', 'kv_b_f', 'k_norm_g', 'b_w_q', 'q_norm_g', 'b_w_out', 'ffn_w_up', 'ffn_conv_w', 'ffn_conv_b', 'ffn_w_down']
TWIN_DIFF_INPUT = 'x'
TWIN_INPUTS = ['x', 'c', 'ada_w', 'ada_b', 'a_w_in', 'a_lb_logits', 'a_norm_g', 'a_w_out', 'kv_ada_w', 'kv_ada_b', 'kv_w', 'kv_b_f', 'k_norm_g', 'b_w_q', 'q_norm_g', 'b_w_out', 'ffn_w_up', 'ffn_conv_w', 'ffn_conv_b', 'ffn_w_down', 'loss_target', 'm_ada_w', 'm_ada_b', 'm_a_w_in', 'm_a_lb_logits', 'm_a_norm_g', 'm_a_w_out', 'm_kv_ada_w', 'm_kv_ada_b', 'm_kv_w', 'm_kv_b_f', 'm_k_norm_g', 'm_b_w_q', 'm_q_norm_g', 'm_b_w_out', 'm_ffn_w_up', 'm_ffn_conv_w', 'm_ffn_conv_b', 'm_ffn_w_down', 'v_ada_w', 'v_ada_b', 'v_a_w_in', 'v_a_lb_logits', 'v_a_norm_g', 'v_a_w_out', 'v_kv_ada_w', 'v_kv_ada_b', 'v_kv_w', 'v_kv_b_f', 'v_k_norm_g', 'v_b_w_q', 'v_q_norm_g', 'v_b_w_out', 'v_ffn_w_up', 'v_ffn_conv_w', 'v_ffn_conv_b', 'v_ffn_w_down']
TWIN_OUTPUTS = ['loss', 'grad_x', 'grad_ada_w', 'grad_ada_b', 'grad_a_w_in', 'grad_a_lb_logits', 'grad_a_norm_g', 'grad_a_w_out', 'grad_kv_ada_w', 'grad_kv_ada_b', 'grad_kv_w', 'grad_kv_b_f', 'grad_k_norm_g', 'grad_b_w_q', 'grad_q_norm_g', 'grad_b_w_out', 'grad_ffn_w_up', 'grad_ffn_conv_w', 'grad_ffn_conv_b', 'grad_ffn_w_down', 'delta_ada_w', 'delta_ada_b', 'delta_a_w_in', 'delta_a_lb_logits', 'delta_a_norm_g', 'delta_a_w_out', 'delta_kv_ada_w', 'delta_kv_ada_b', 'delta_kv_w', 'delta_kv_b_f', 'delta_k_norm_g', 'delta_b_w_q', 'delta_q_norm_g', 'delta_b_w_out', 'delta_ffn_w_up', 'delta_ffn_conv_w', 'delta_ffn_conv_b', 'delta_ffn_w_down', 'new_m_ada_w', 'new_m_ada_b', 'new_m_a_w_in', 'new_m_a_lb_logits', 'new_m_a_norm_g', 'new_m_a_w_out', 'new_m_kv_ada_w', 'new_m_kv_ada_b', 'new_m_kv_w', 'new_m_kv_b_f', 'new_m_k_norm_g', 'new_m_b_w_q', 'new_m_q_norm_g', 'new_m_b_w_out', 'new_m_ffn_w_up', 'new_m_ffn_conv_w', 'new_m_ffn_conv_b', 'new_m_ffn_w_down', 'new_v_ada_w', 'new_v_ada_b', 'new_v_a_w_in', 'new_v_a_lb_logits', 'new_v_a_norm_g', 'new_v_a_w_out', 'new_v_kv_ada_w', 'new_v_kv_ada_b', 'new_v_kv_w', 'new_v_kv_b_f', 'new_v_k_norm_g', 'new_v_b_w_q', 'new_v_q_norm_g', 'new_v_b_w_out', 'new_v_ffn_w_up', 'new_v_ffn_conv_w', 'new_v_ffn_conv_b', 'new_v_ffn_w_down']
TWIN_LEAF_KINDS = {'loss': 'loss', 'grad_x': 'grad_x', 'grad_ada_w': 'grad_w', 'grad_ada_b': 'grad_w', 'grad_a_w_in': 'grad_w', 'grad_a_lb_logits': 'grad_w', 'grad_a_norm_g': 'grad_w', 'grad_a_w_out': 'grad_w', 'grad_kv_ada_w': 'grad_w', 'grad_kv_ada_b': 'grad_w', 'grad_kv_w': 'grad_w', 'grad_kv_b_f': 'grad_w', 'grad_k_norm_g': 'grad_w', 'grad_b_w_q': 'grad_w', 'grad_q_norm_g': 'grad_w', 'grad_b_w_out': 'grad_w', 'grad_ffn_w_up': 'grad_w', 'grad_ffn_conv_w': 'grad_w', 'grad_ffn_conv_b': 'grad_w', 'grad_ffn_w_down': 'grad_w', 'delta_ada_w': 'delta_w', 'delta_ada_b': 'delta_w', 'delta_a_w_in': 'delta_w', 'delta_a_lb_logits': 'delta_w', 'delta_a_norm_g': 'delta_w', 'delta_a_w_out': 'delta_w', 'delta_kv_ada_w': 'delta_w', 'delta_kv_ada_b': 'delta_w', 'delta_kv_w': 'delta_w', 'delta_kv_b_f': 'delta_w', 'delta_k_norm_g': 'delta_w', 'delta_b_w_q': 'delta_w', 'delta_q_norm_g': 'delta_w', 'delta_b_w_out': 'delta_w', 'delta_ffn_w_up': 'delta_w', 'delta_ffn_conv_w': 'delta_w', 'delta_ffn_conv_b': 'delta_w', 'delta_ffn_w_down': 'delta_w', 'new_m_ada_w': 'new_m', 'new_m_ada_b': 'new_m', 'new_m_a_w_in': 'new_m', 'new_m_a_lb_logits': 'new_m', 'new_m_a_norm_g': 'new_m', 'new_m_a_w_out': 'new_m', 'new_m_kv_ada_w': 'new_m', 'new_m_kv_ada_b': 'new_m', 'new_m_kv_w': 'new_m', 'new_m_kv_b_f': 'new_m', 'new_m_k_norm_g': 'new_m', 'new_m_b_w_q': 'new_m', 'new_m_q_norm_g': 'new_m', 'new_m_b_w_out': 'new_m', 'new_m_ffn_w_up': 'new_m', 'new_m_ffn_conv_w': 'new_m', 'new_m_ffn_conv_b': 'new_m', 'new_m_ffn_w_down': 'new_m', 'new_v_ada_w': 'new_v', 'new_v_ada_b': 'new_v', 'new_v_a_w_in': 'new_v', 'new_v_a_lb_logits': 'new_v', 'new_v_a_norm_g': 'new_v', 'new_v_a_w_out': 'new_v', 'new_v_kv_ada_w': 'new_v', 'new_v_kv_ada_b': 'new_v', 'new_v_kv_w': 'new_v', 'new_v_kv_b_f': 'new_v', 'new_v_k_norm_g': 'new_v', 'new_v_b_w_q': 'new_v', 'new_v_q_norm_g': 'new_v', 'new_v_b_w_out': 'new_v', 'new_v_ffn_w_up': 'new_v', 'new_v_ffn_conv_w': 'new_v', 'new_v_ffn_conv_b': 'new_v', 'new_v_ffn_w_down': 'new_v'}


def _forward(args):
    return _fwd_reference(*[args[k] for k in FWD_PARAMS])


def _output_shape():
    out = _jax.eval_shape(lambda: _forward(_fwd_setup_inputs(0)))
    return out.shape, out.dtype

N_MICROBATCH = 1
ADAM_LR = 0.001
ADAM_B1 = 0.9
ADAM_B2 = 0.999
ADAM_EPS = 1e-08
ADAM_WD = 0.01
ADAM_STEP = 10
PER_EXAMPLE_BATCH_AXIS = {'x': 0, 'c': 0, 'loss_target': 0}
SHARED_INPUTS = []
_WEIGHT_DTYPES = {'ada_w': _jnp.float32, 'ada_b': _jnp.float32, 'a_w_in': _jnp.float32, 'a_lb_logits': _jnp.float32, 'a_norm_g': _jnp.float32, 'a_w_out': _jnp.float32, 'kv_ada_w': _jnp.float32, 'kv_ada_b': _jnp.float32, 'kv_w': _jnp.float32, 'kv_b_f': _jnp.float32, 'k_norm_g': _jnp.float32, 'b_w_q': _jnp.float32, 'q_norm_g': _jnp.float32, 'b_w_out': _jnp.float32, 'ffn_w_up': _jnp.float32, 'ffn_conv_w': _jnp.float32, 'ffn_conv_b': _jnp.float32, 'ffn_w_down': _jnp.float32}
MOMENT_SCALE = {'ada_w': 9.403169e-01, 'ada_b': 2.069659e+00, 'a_w_in': 1.127818e-01, 'a_lb_logits': 6.763894e-03, 'a_norm_g': 1.063938e+01, 'a_w_out': 1.157708e-01, 'kv_ada_w': 1.774269e-01, 'kv_ada_b': 4.087711e-01, 'kv_w': 7.425871e-02, 'kv_b_f': 3.478652e+00, 'k_norm_g': 2.906478e-01, 'b_w_q': 1.281269e-02, 'q_norm_g': 2.925523e-01, 'b_w_out': 7.507123e-02, 'ffn_w_up': 1.036974e-01, 'ffn_conv_w': 5.014719e-01, 'ffn_conv_b': 4.116130e-01, 'ffn_w_down': 8.260221e-02}


def _to_microbatches(a, axis):
    t = _jnp.moveaxis(a, axis, 0)
    t = t.reshape((N_MICROBATCH, t.shape[0] // N_MICROBATCH) + t.shape[1:])
    return _jnp.moveaxis(t, 1, axis + 1)


def setup_inputs(seed: int = 0) -> dict:
    inp = _fwd_setup_inputs(seed)
    key = _jax.random.fold_in(_jax.random.key(seed), 7919)
    shape, _ = _output_shape()
    out = dict(inp)
    out["loss_target"] = _jax.random.normal(_jax.random.fold_in(key, 0), shape, _jnp.float32)
    for i, name in enumerate(TWIN_WEIGHTS):
        w = inp[name].astype(_jnp.float32)
        if MOMENT_SCALE is None:
            s = _jnp.sqrt(_jnp.mean(_jnp.square(w)) + 1e-30)
        else:
            s = MOMENT_SCALE[name]
        km, kv = _jax.random.split(_jax.random.fold_in(key, i + 1))
        out[name] = w
        out["m_" + name] = s * _jax.random.normal(km, w.shape, _jnp.float32)
        out["v_" + name] = (s * s) * _jax.random.uniform(kv, w.shape, _jnp.float32, 0.5, 1.5)
    if N_MICROBATCH > 1:
        for name, axis in PER_EXAMPLE_BATCH_AXIS.items():
            out[name] = _to_microbatches(out[name], axis)
    return {'x': out['x'], 'c': out['c'], 'ada_w': out['ada_w'], 'ada_b': out['ada_b'], 'a_w_in': out['a_w_in'], 'a_lb_logits': out['a_lb_logits'], 'a_norm_g': out['a_norm_g'], 'a_w_out': out['a_w_out'], 'kv_ada_w': out['kv_ada_w'], 'kv_ada_b': out['kv_ada_b'], 'kv_w': out['kv_w'], 'kv_b_f': out['kv_b_f'], 'k_norm_g': out['k_norm_g'], 'b_w_q': out['b_w_q'], 'q_norm_g': out['q_norm_g'], 'b_w_out': out['b_w_out'], 'ffn_w_up': out['ffn_w_up'], 'ffn_conv_w': out['ffn_conv_w'], 'ffn_conv_b': out['ffn_conv_b'], 'ffn_w_down': out['ffn_w_down'], 'loss_target': out['loss_target'], 'm_ada_w': out['m_ada_w'], 'm_ada_b': out['m_ada_b'], 'm_a_w_in': out['m_a_w_in'], 'm_a_lb_logits': out['m_a_lb_logits'], 'm_a_norm_g': out['m_a_norm_g'], 'm_a_w_out': out['m_a_w_out'], 'm_kv_ada_w': out['m_kv_ada_w'], 'm_kv_ada_b': out['m_kv_ada_b'], 'm_kv_w': out['m_kv_w'], 'm_kv_b_f': out['m_kv_b_f'], 'm_k_norm_g': out['m_k_norm_g'], 'm_b_w_q': out['m_b_w_q'], 'm_q_norm_g': out['m_q_norm_g'], 'm_b_w_out': out['m_b_w_out'], 'm_ffn_w_up': out['m_ffn_w_up'], 'm_ffn_conv_w': out['m_ffn_conv_w'], 'm_ffn_conv_b': out['m_ffn_conv_b'], 'm_ffn_w_down': out['m_ffn_w_down'], 'v_ada_w': out['v_ada_w'], 'v_ada_b': out['v_ada_b'], 'v_a_w_in': out['v_a_w_in'], 'v_a_lb_logits': out['v_a_lb_logits'], 'v_a_norm_g': out['v_a_norm_g'], 'v_a_w_out': out['v_a_w_out'], 'v_kv_ada_w': out['v_kv_ada_w'], 'v_kv_ada_b': out['v_kv_ada_b'], 'v_kv_w': out['v_kv_w'], 'v_kv_b_f': out['v_kv_b_f'], 'v_k_norm_g': out['v_k_norm_g'], 'v_b_w_q': out['v_b_w_q'], 'v_q_norm_g': out['v_q_norm_g'], 'v_b_w_out': out['v_b_w_out'], 'v_ffn_w_up': out['v_ffn_w_up'], 'v_ffn_conv_w': out['v_ffn_conv_w'], 'v_ffn_conv_b': out['v_ffn_conv_b'], 'v_ffn_w_down': out['v_ffn_w_down']}


def _loss(weights, diff, rest, loss_target):
    with _jax.named_scope("forward"):
        args = {**rest, TWIN_DIFF_INPUT: diff, **{k: w.astype(_WEIGHT_DTYPES[k]) for k, w in weights.items()}}
        y = _forward(args)
    with _jax.named_scope("loss_head"):
        err = _jnp.square(y.astype(_jnp.float32) - loss_target)
        return 0.5 * _jnp.sum(_jnp.mean(err, axis=-1)) if err.ndim else 0.5 * err


def _adamw(w, g, m, v):
    m = ADAM_B1 * m + (1.0 - ADAM_B1) * g
    v = ADAM_B2 * v + (1.0 - ADAM_B2) * _jnp.square(g)
    m_hat = m / (1.0 - ADAM_B1 ** ADAM_STEP)
    v_hat = v / (1.0 - ADAM_B2 ** ADAM_STEP)
    delta = -ADAM_LR * (m_hat / (_jnp.sqrt(v_hat) + ADAM_EPS) + ADAM_WD * w)
    return delta, m, v


def reference(x, c, ada_w, ada_b, a_w_in, a_lb_logits, a_norm_g, a_w_out, kv_ada_w, kv_ada_b, kv_w, kv_b_f, k_norm_g, b_w_q, q_norm_g, b_w_out, ffn_w_up, ffn_conv_w, ffn_conv_b, ffn_w_down, loss_target, m_ada_w, m_ada_b, m_a_w_in, m_a_lb_logits, m_a_norm_g, m_a_w_out, m_kv_ada_w, m_kv_ada_b, m_kv_w, m_kv_b_f, m_k_norm_g, m_b_w_q, m_q_norm_g, m_b_w_out, m_ffn_w_up, m_ffn_conv_w, m_ffn_conv_b, m_ffn_w_down, v_ada_w, v_ada_b, v_a_w_in, v_a_lb_logits, v_a_norm_g, v_a_w_out, v_kv_ada_w, v_kv_ada_b, v_kv_w, v_kv_b_f, v_k_norm_g, v_b_w_q, v_q_norm_g, v_b_w_out, v_ffn_w_up, v_ffn_conv_w, v_ffn_conv_b, v_ffn_w_down):
    given = dict(x=x, c=c, ada_w=ada_w, ada_b=ada_b, a_w_in=a_w_in, a_lb_logits=a_lb_logits, a_norm_g=a_norm_g, a_w_out=a_w_out, kv_ada_w=kv_ada_w, kv_ada_b=kv_ada_b, kv_w=kv_w, kv_b_f=kv_b_f, k_norm_g=k_norm_g, b_w_q=b_w_q, q_norm_g=q_norm_g, b_w_out=b_w_out, ffn_w_up=ffn_w_up, ffn_conv_w=ffn_conv_w, ffn_conv_b=ffn_conv_b, ffn_w_down=ffn_w_down, loss_target=loss_target, m_ada_w=m_ada_w, m_ada_b=m_ada_b, m_a_w_in=m_a_w_in, m_a_lb_logits=m_a_lb_logits, m_a_norm_g=m_a_norm_g, m_a_w_out=m_a_w_out, m_kv_ada_w=m_kv_ada_w, m_kv_ada_b=m_kv_ada_b, m_kv_w=m_kv_w, m_kv_b_f=m_kv_b_f, m_k_norm_g=m_k_norm_g, m_b_w_q=m_b_w_q, m_q_norm_g=m_q_norm_g, m_b_w_out=m_b_w_out, m_ffn_w_up=m_ffn_w_up, m_ffn_conv_w=m_ffn_conv_w, m_ffn_conv_b=m_ffn_conv_b, m_ffn_w_down=m_ffn_w_down, v_ada_w=v_ada_w, v_ada_b=v_ada_b, v_a_w_in=v_a_w_in, v_a_lb_logits=v_a_lb_logits, v_a_norm_g=v_a_norm_g, v_a_w_out=v_a_w_out, v_kv_ada_w=v_kv_ada_w, v_kv_ada_b=v_kv_ada_b, v_kv_w=v_kv_w, v_kv_b_f=v_kv_b_f, v_k_norm_g=v_k_norm_g, v_b_w_q=v_b_w_q, v_q_norm_g=v_q_norm_g, v_b_w_out=v_b_w_out, v_ffn_w_up=v_ffn_w_up, v_ffn_conv_w=v_ffn_conv_w, v_ffn_conv_b=v_ffn_conv_b, v_ffn_w_down=v_ffn_w_down)
    weights = {n: given[n] for n in TWIN_WEIGHTS}
    shared = {n: given[n] for n in SHARED_INPUTS}
    per_example = {n: given[n] for n in ['x', 'c']}
    grad_fn = _jax.value_and_grad(_loss, argnums=(0, 1))

    def one_microbatch(ex, loss_target):
        ex = dict(ex)
        diff = ex.pop(TWIN_DIFF_INPUT)
        return grad_fn(weights, diff, {**shared, **ex}, loss_target)

    if N_MICROBATCH == 1:
        loss, (grad_w, grad_x) = one_microbatch(per_example, given["loss_target"])
    else:
        def body(carry, xs):
            loss_sum, grad_sum = carry
            l_k, (gw_k, gx_k) = one_microbatch(xs[0], xs[1])
            with _jax.named_scope("update"):
                return (loss_sum + l_k, _jax.tree.map(_jnp.add, grad_sum, gw_k)), gx_k

        init = (_jnp.zeros((), _jnp.float32), _jax.tree.map(_jnp.zeros_like, weights))
        (loss, grad_w), grad_x = _jax.lax.scan(body, init, (per_example, given["loss_target"]))
    with _jax.named_scope("update"):
        delta_w, new_m, new_v = {}, {}, {}
        for n in TWIN_WEIGHTS:
            delta_w[n], new_m[n], new_v[n] = _adamw(weights[n], grad_w[n], given["m_" + n], given["v_" + n])
    return (loss, grad_x, *[grad_w[n] for n in TWIN_WEIGHTS], *[delta_w[n] for n in TWIN_WEIGHTS],
            *[new_m[n] for n in TWIN_WEIGHTS], *[new_v[n] for n in TWIN_WEIGHTS])
```

```python
import functools

import jax
import jax.numpy as jnp
from jax import lax
from jax.experimental import pallas as pl
from jax.experimental.pallas import tpu as pltpu

F32 = jnp.float32
BF16 = jnp.bfloat16

NDEV = 8
NCHIP = 4
HEAD = 128
A_CHUNK = 64
CONV_TAPS = 3
EPS = 1e-6
NEG_INF = -1e30
LANES = 128
PACK_W = 1024
VMEM_LIMIT = 48 * 1024 * 1024

ADAM_LR = 0.001
ADAM_B1 = 0.9
ADAM_B2 = 0.999
ADAM_EPS = 1e-08
ADAM_WD = 0.01
ADAM_STEP = 10

_NN = (((1,), (0,)), ((), ()))
_NT = (((1,), (1,)), ((), ()))
_TN = (((0,), (0,)), ((), ()))
_MESH = pl.DeviceIdType.MESH


def _cparams(**kw):
    return pltpu.CompilerParams(vmem_limit_bytes=VMEM_LIMIT, **kw)


def _divisor_tile(n, pref, unit=LANES):
    if n <= pref:
        return n
    best = None
    for t in range(unit, pref + 1, unit):
        if n % t == 0:
            best = t
    assert best is not None, (n, pref)
    return best


def _bdot_raw(a, b, dims):
    return lax.dot_general(a.astype(BF16), b.astype(BF16), dims, preferred_element_type=F32)


@jax.custom_vjp
def _dot_nn(a, b):
    return _bdot_raw(a, b, _NN)


@jax.custom_vjp
def _dot_nt(a, b):
    return _bdot_raw(a, b, _NT)


@jax.custom_vjp
def _dot_tn(a, b):
    return _bdot_raw(a, b, _TN)


_dot_nn.defvjp(lambda a, b: (_bdot_raw(a, b, _NN), (a, b)),
               lambda r, g: (_dot_nt(g, r[1]), _dot_tn(r[0], g)))
_dot_nt.defvjp(lambda a, b: (_bdot_raw(a, b, _NT), (a, b)),
               lambda r, g: (_dot_nn(g, r[1]), _dot_tn(g, r[0])))
_dot_tn.defvjp(lambda a, b: (_bdot_raw(a, b, _TN), (a, b)),
               lambda r, g: (_dot_nt(r[1], g), _dot_nn(r[0], g)))


def _f32dot(a, b):
    return lax.dot_general(a, b, _NN, precision=lax.Precision.HIGHEST, preferred_element_type=F32)


@jax.custom_vjp
def _tri_cumsum(x, tri, tri_t):
    return _f32dot(tri, x)


_tri_cumsum.defvjp(lambda x, tri, tri_t: (_f32dot(tri, x), (tri, tri_t)),
                   lambda r, g: (_f32dot(r[1], g), jnp.zeros_like(r[0]), jnp.zeros_like(r[1])))


def _sigmoid(x):
    return jax.nn.sigmoid(x)


def _silu(x):
    return x * jax.nn.sigmoid(x)


def _rms(x):
    return x * lax.rsqrt(jnp.mean(x * x, axis=-1, keepdims=True) + EPS)


def _modulate(x, sh, sc):
    return _rms(x) * (1.0 + sc) + sh


def _mm(a, b, mode, out_dtype, name, tm=1024, tn=1024, tk=2048):
    if mode == "nn":
        (m, k), (k2, n) = a.shape, b.shape
    elif mode == "nt":
        (m, k), (n, k2) = a.shape, b.shape
    else:
        (k, m), (k2, n) = a.shape, b.shape
    assert k == k2, (a.shape, b.shape, mode)
    tm = _divisor_tile(m, tm)
    tn = _divisor_tile(n, tn)
    tk = _divisor_tile(k, tk)
    nk = k // tk
    dims = {"nn": _NN, "nt": _NT, "tn": _TN}[mode]

    def body(a_ref, b_ref, o_ref, *acc):
        p = lax.dot_general(a_ref[...].astype(BF16), b_ref[...].astype(BF16), dims,
                            preferred_element_type=F32)
        if nk == 1:
            o_ref[...] = p.astype(o_ref.dtype)
        else:
            kk = pl.program_id(2)

            @pl.when(kk == 0)
            def _():
                acc[0][...] = p

            @pl.when(kk > 0)
            def _():
                acc[0][...] += p

            @pl.when(kk == nk - 1)
            def _():
                o_ref[...] = acc[0][...].astype(o_ref.dtype)

    if mode == "tn":
        a_spec = pl.BlockSpec((tk, tm), lambda i, j, kk: (kk, i))
    else:
        a_spec = pl.BlockSpec((tm, tk), lambda i, j, kk: (i, kk))
    if mode == "nt":
        b_spec = pl.BlockSpec((tn, tk), lambda i, j, kk: (j, kk))
    else:
        b_spec = pl.BlockSpec((tk, tn), lambda i, j, kk: (kk, j))
    return pl.pallas_call(
        body, name=name,
        grid=(m // tm, n // tn, nk),
        in_specs=[a_spec, b_spec],
        out_specs=pl.BlockSpec((tm, tn), lambda i, j, kk: (i, j)),
        out_shape=jax.ShapeDtypeStruct((m, n), out_dtype),
        scratch_shapes=[pltpu.VMEM((tm, tn), F32)] if nk > 1 else [],
        compiler_params=_cparams(dimension_semantics=("parallel", "parallel", "arbitrary")),
    )(a, b)


def _row_specs(rows, params, tb):
    specs = []
    for (_, cw, off) in rows:
        specs.append(pl.BlockSpec((tb, cw), functools.partial(lambda j, i, off: (i, off + j), off=off)))
    for (_, cw, per_col) in params:
        if per_col:
            specs.append(pl.BlockSpec((1, cw), lambda j, i: (0, j)))
        else:
            specs.append(pl.BlockSpec((1, cw), lambda j, i: (0, 0)))
    return specs


def _row_fwd(f, rows, params, out_dtypes, *, ncol, tb, name):
    t = rows[0][0].shape[0]
    tb = min(tb, t)
    n_in = len(rows) + len(params)
    blk = [jax.ShapeDtypeStruct((tb, cw), F32) for (_, cw, _) in rows]
    blk += [jax.ShapeDtypeStruct((1, cw), F32) for (_, cw, _) in params]
    out_avals = jax.eval_shape(f, *blk)

    def body(*refs):
        vals = [r[...].astype(F32) for r in refs[:n_in]]
        outs = f(*vals)
        for o_ref, o in zip(refs[n_in:], outs):
            o_ref[...] = o.astype(o_ref.dtype)

    return pl.pallas_call(
        body, name=name,
        grid=(ncol, t // tb),
        in_specs=_row_specs(rows, params, tb),
        out_specs=[pl.BlockSpec((tb, av.shape[1]), lambda j, i: (i, j)) for av in out_avals],
        out_shape=[jax.ShapeDtypeStruct((t, ncol * av.shape[1]), dt) for av, dt in zip(out_avals, out_dtypes)],
        compiler_params=_cparams(dimension_semantics=("parallel", "parallel")),
    )(*[r[0] for r in rows], *[p[0] for p in params])


def _row_bwd(f, rows, params, cots, row_grad_dtypes, *, ncol, tb, name):
    t = rows[0][0].shape[0]
    tb = min(tb, t)
    n_r, n_p, n_c = len(rows), len(params), len(cots)
    want = [j for j in range(n_r) if row_grad_dtypes[j] is not None]

    def body(*refs):
        j, i = pl.program_id(0), pl.program_id(1)
        vals = [r[...].astype(F32) for r in refs[:n_r + n_p]]
        cvals = tuple(r[...].astype(F32) for r in refs[n_r + n_p:n_r + n_p + n_c])
        outs = refs[n_r + n_p + n_c:]
        _, vjp_fn = jax.vjp(f, *vals)
        grads = vjp_fn(cvals)
        for o_ref, jr in zip(outs[:len(want)], want):
            o_ref[...] = grads[jr].astype(o_ref.dtype)
        for o_ref, g, (_, _, per_col) in zip(outs[len(want):], grads[n_r:], params):
            first = (i == 0) if per_col else jnp.logical_and(i == 0, j == 0)

            @pl.when(first)
            def _():
                o_ref[...] = g

            @pl.when(jnp.logical_not(first))
            def _():
                o_ref[...] += g

    out_specs, out_shape = [], []
    for jr in want:
        cw = rows[jr][1]
        out_specs.append(pl.BlockSpec((tb, cw), lambda j, i: (i, j)))
        out_shape.append(jax.ShapeDtypeStruct((t, ncol * cw), row_grad_dtypes[jr]))
    for (arr, cw, per_col) in params:
        if per_col:
            out_specs.append(pl.BlockSpec((1, cw), lambda j, i: (0, j)))
        else:
            out_specs.append(pl.BlockSpec((1, cw), lambda j, i: (0, 0)))
        out_shape.append(jax.ShapeDtypeStruct(arr.shape, F32))
    res = pl.pallas_call(
        body, name=name,
        grid=(ncol, t // tb),
        in_specs=_row_specs(rows, params, tb) + _row_specs(cots, [], tb),
        out_specs=out_specs, out_shape=out_shape,
        compiler_params=_cparams(dimension_semantics=("arbitrary", "arbitrary")),
    )(*[r[0] for r in rows], *[p[0] for p in params], *[c[0] for c in cots])
    return res[:len(want)], res[len(want):]


def _f_mod(x, sh, sc):
    return (_modulate(x, sh, sc),)


def _f_res_mod(x, y, g, sh, sc):
    x1 = x + g * y
    return x1, _modulate(x1, sh, sc)


def _f_res_mod2(x, y, g, sh_a, sc_a, sh_b, sc_b):
    x1 = x + g * y
    return x1, _modulate(x1, sh_a, sc_a), _modulate(x1, sh_b, sc_b)


def _f_qnorm(p, g):
    return (_rms(p) * g * (HEAD ** -0.5),)


def _f_knorm(p, g):
    return (_rms(p) * g,)


def _f_outgate(o, og):
    return (o * _sigmoid(og),)


def _loss_call(x3, f, g2, target, tb):
    t, d = x3.shape
    tb = min(tb, t)

    def body(x_ref, f_ref, g_ref, t_ref, loss_ref, dx_ref, df_ref, dg_ref):
        i = pl.program_id(0)
        fv = f_ref[...]
        g = g_ref[...]
        e = x_ref[...] + g * fv - t_ref[...]
        dx = e * (1.0 / d)
        part = 0.5 * jnp.sum(jnp.sum(e * dx, axis=1, keepdims=True), axis=0, keepdims=True)
        dx_ref[...] = dx
        df_ref[...] = (g * dx).astype(df_ref.dtype)
        dg = jnp.sum(dx * fv, axis=0, keepdims=True)

        @pl.when(i == 0)
        def _():
            loss_ref[...] = jnp.broadcast_to(part, loss_ref.shape)
            dg_ref[...] = dg

        @pl.when(i > 0)
        def _():
            loss_ref[...] += jnp.broadcast_to(part, loss_ref.shape)
            dg_ref[...] += dg

    row = pl.BlockSpec((tb, d), lambda i: (i, 0))
    vec = pl.BlockSpec((1, d), lambda i: (0, 0))
    return pl.pallas_call(
        body, name="loss_head",
        grid=(t // tb,),
        in_specs=[row, row, vec, row],
        out_specs=[pl.BlockSpec((1, LANES), lambda i: (0, 0)), row, row, vec],
        out_shape=[jax.ShapeDtypeStruct((1, LANES), F32), jax.ShapeDtypeStruct((t, d), F32),
                   jax.ShapeDtypeStruct((t, d), BF16), jax.ShapeDtypeStruct((1, d), F32)],
        compiler_params=_cparams(dimension_semantics=("arbitrary",)),
    )(x3, f, g2, target)


def _tri_consts(c):
    r = lax.broadcasted_iota(jnp.int32, (c, c), 0)
    s = lax.broadcasted_iota(jnp.int32, (c, c), 1)
    tri = (s <= r).astype(F32)
    tri_t = (r <= s).astype(F32)
    return tri, tri_t, s <= r


def _hg_chunk(qp, fp, ip, gp, lb, ng, st, tri, tri_t, causal):
    c = qp.shape[0]
    q = _silu(qp)
    fg = lb + (1.0 - lb) * _sigmoid(fp)
    logf = jnp.log(fg)
    k = 1.0 - fg
    b = _tri_cumsum(logf, tri, tri_t)
    row = lax.broadcasted_iota(jnp.int32, (c, 1), 0)
    b_mid = lax.stop_gradient(jnp.sum(jnp.where(row == c // 2, b, 0.0), axis=0, keepdims=True))
    b_last = jnp.sum(jnp.where(row == c - 1, b, 0.0), axis=0, keepdims=True)
    scores = _dot_nt(q * jnp.exp(b - b_mid), k * jnp.exp(b_mid - b))
    scores = jnp.where(causal, scores, 0.0)
    o = _dot_nn(scores, ip) + _dot_nt(q * jnp.exp(b), st)
    st_new = st * jnp.exp(b_last) + _dot_tn(ip, k * jnp.exp(b_last - b))
    y = _rms(o) * ng * _silu(gp)
    return y, st_new


def _hg_specs(tb, nh, rev_nb=None):
    def row(off):
        if rev_nb is None:
            return pl.BlockSpec((tb, HEAD), functools.partial(lambda h, i, off: (i, off + h), off=off))
        return pl.BlockSpec((tb, HEAD), functools.partial(lambda h, i, off: (rev_nb - 1 - i, off + h), off=off))
    return [row(0), row(nh), row(2 * nh), row(3 * nh),
            pl.BlockSpec((1, HEAD), lambda h, i: (0, h)), pl.BlockSpec((1, HEAD), lambda h, i: (0, 0))]


def _hgrn2_fwd(proj, lb, ng, tb):
    t = proj.shape[0]
    nh = proj.shape[1] // (4 * HEAD)
    tb = min(tb, t)
    nb, ncb = t // tb, tb // A_CHUNK

    def body(q_ref, f_ref, i_ref, g_ref, lb_ref, ng_ref, y_ref, s_ref, st):
        i = pl.program_id(1)

        @pl.when(i == 0)
        def _():
            st[...] = jnp.zeros_like(st)

        s_ref[0, 0] = st[...]
        tri, tri_t, causal = _tri_consts(A_CHUNK)
        lbv, ngv = lb_ref[...], ng_ref[...]

        def chunk(cidx, carry):
            sl = pl.ds(pl.multiple_of(cidx * A_CHUNK, A_CHUNK), A_CHUNK)
            y, st_new = _hg_chunk(q_ref[sl, :], f_ref[sl, :], i_ref[sl, :], g_ref[sl, :], lbv, ngv,
                                  st[...], tri, tri_t, causal)
            y_ref[sl, :] = y.astype(y_ref.dtype)
            st[...] = st_new
            return carry

        lax.fori_loop(0, ncb, chunk, 0)

    return pl.pallas_call(
        body, name="hgrn2_fwd",
        grid=(nh, nb),
        in_specs=_hg_specs(tb, nh),
        out_specs=[pl.BlockSpec((tb, HEAD), lambda h, i: (i, h)),
                   pl.BlockSpec((1, 1, HEAD, HEAD), lambda h, i: (h, i, 0, 0))],
        out_shape=[jax.ShapeDtypeStruct((t, nh * HEAD), BF16),
                   jax.ShapeDtypeStruct((nh, nb, HEAD, HEAD), F32)],
        scratch_shapes=[pltpu.VMEM((HEAD, HEAD), F32)],
        compiler_params=_cparams(dimension_semantics=("parallel", "arbitrary")),
    )(proj, proj, proj, proj, lb, ng)


def _hgrn2_bwd(proj, lb, ng, states, dy, tb):
    t = proj.shape[0]
    nh = proj.shape[1] // (4 * HEAD)
    tb = min(tb, t)
    nb, ncb = t // tb, tb // A_CHUNK

    def body(q_ref, f_ref, i_ref, g_ref, lb_ref, ng_ref, s_ref, dy_ref,
             dq_ref, df_ref, di_ref, dg_ref, dlb_ref, dng_ref, sts, dst, dlb_acc, dng_acc):
        h, i = pl.program_id(0), pl.program_id(1)
        tri, tri_t, causal = _tri_consts(A_CHUNK)
        lbv, ngv = lb_ref[...], ng_ref[...]

        @pl.when(i == 0)
        def _():
            dst[...] = jnp.zeros_like(dst)
            dlb_acc[...] = jnp.zeros_like(dlb_acc)

        @pl.when(jnp.logical_and(i == 0, h == 0))
        def _():
            dng_acc[...] = jnp.zeros_like(dng_acc)

        def fn(qp, fp, ip, gp, lbx, ngx, stx):
            return _hg_chunk(qp, fp, ip, gp, lbx, ngx, stx, tri, tri_t, causal)

        def recompute(cidx, st):
            sl = pl.ds(pl.multiple_of(cidx * A_CHUNK, A_CHUNK), A_CHUNK)
            sts[cidx] = st
            _, st_new = fn(q_ref[sl, :], f_ref[sl, :], i_ref[sl, :], g_ref[sl, :], lbv, ngv, st)
            return st_new

        lax.fori_loop(0, ncb, recompute, s_ref[0, 0])

        def back(r, carry):
            cidx = ncb - 1 - r
            sl = pl.ds(pl.multiple_of(cidx * A_CHUNK, A_CHUNK), A_CHUNK)
            _, vjp_fn = jax.vjp(fn, q_ref[sl, :], f_ref[sl, :], i_ref[sl, :], g_ref[sl, :], lbv, ngv, sts[cidx])
            gq, gf, gi, gg, glb, gng, gst = vjp_fn((dy_ref[sl, :].astype(F32), dst[...]))
            dq_ref[sl, :] = gq.astype(dq_ref.dtype)
            df_ref[sl, :] = gf.astype(df_ref.dtype)
            di_ref[sl, :] = gi.astype(di_ref.dtype)
            dg_ref[sl, :] = gg.astype(dg_ref.dtype)
            dlb_acc[...] += glb
            dng_acc[...] += gng
            dst[...] = gst
            return carry

        lax.fori_loop(0, ncb, back, 0)
        dlb_ref[...] = dlb_acc[...]
        dng_ref[...] = dng_acc[...]

    rev = lambda h, i: (nb - 1 - i, h)
    slab = jax.ShapeDtypeStruct((t, nh * HEAD), BF16)
    return pl.pallas_call(
        body, name="hgrn2_bwd",
        grid=(nh, nb),
        in_specs=_hg_specs(tb, nh, rev_nb=nb) + [
            pl.BlockSpec((1, 1, HEAD, HEAD), lambda h, i: (h, nb - 1 - i, 0, 0)),
            pl.BlockSpec((tb, HEAD), rev)],
        out_specs=[pl.BlockSpec((tb, HEAD), rev)] * 4 + [
            pl.BlockSpec((1, HEAD), lambda h, i: (0, h)), pl.BlockSpec((1, HEAD), lambda h, i: (0, 0))],
        out_shape=[slab, slab, slab, slab,
                   jax.ShapeDtypeStruct((1, nh * HEAD), F32), jax.ShapeDtypeStruct((1, HEAD), F32)],
        scratch_shapes=[pltpu.VMEM((ncb, HEAD, HEAD), F32), pltpu.VMEM((HEAD, HEAD), F32),
                        pltpu.VMEM((1, HEAD), F32), pltpu.VMEM((1, HEAD), F32)],
        compiler_params=_cparams(dimension_semantics=("arbitrary", "arbitrary")),
    )(proj, proj, proj, proj, lb, ng, states, dy)


def _fgate_consts(cb):
    r = lax.broadcasted_iota(jnp.int32, (cb, cb), 0)
    s = lax.broadcasted_iota(jnp.int32, (cb, cb), 1)
    return (r <= s).astype(F32), (r >= s).astype(F32)


def _fgate_fwd(xt, bias, cb=512):
    nh, t = xt.shape
    cb = min(cb, t)

    def body(x_ref, b_ref, o_ref):
        upper, _ = _fgate_consts(cb)
        carry = jnp.zeros((nh, 1), F32)
        for blk in range(t // cb):
            z = x_ref[:, blk * cb:(blk + 1) * cb] + b_ref[...]
            logf = jnp.minimum(z, 0.0) - jnp.log(1.0 + jnp.exp(-jnp.abs(z)))
            cs = _f32dot(logf, upper) + carry
            o_ref[:, blk * cb:(blk + 1) * cb] = cs
            carry = cs[:, cb - 1:cb]

    vm = pl.BlockSpec(memory_space=pltpu.VMEM)
    return pl.pallas_call(
        body, name="fgate_fwd", in_specs=[vm, vm], out_specs=vm,
        out_shape=jax.ShapeDtypeStruct((nh, t), F32), compiler_params=_cparams(),
    )(xt, bias)


def _fgate_bwd(xt, bias, dft, cb=512):
    nh, t = xt.shape
    cb = min(cb, t)
    nblk = t // cb

    def body(x_ref, b_ref, d_ref, dx_ref, db_ref):
        _, lower = _fgate_consts(cb)
        carry = jnp.zeros((nh, 1), F32)
        db = jnp.zeros((nh, 1), F32)
        for blk in range(nblk - 1, -1, -1):
            sl = slice(blk * cb, (blk + 1) * cb)
            dlogf = _f32dot(d_ref[:, sl], lower) + carry
            carry = dlogf[:, 0:1]
            z = x_ref[:, sl] + b_ref[...]
            dz = dlogf * (1.0 - _sigmoid(z))
            dx_ref[:, sl] = dz
            db = db + jnp.sum(dz, axis=1, keepdims=True)
        db_ref[...] = db

    vm = pl.BlockSpec(memory_space=pltpu.VMEM)
    return pl.pallas_call(
        body, name="fgate_bwd", in_specs=[vm, vm, vm], out_specs=[vm, vm],
        out_shape=[jax.ShapeDtypeStruct((nh, t), F32), jax.ShapeDtypeStruct((nh, 1), F32)],
        compiler_params=_cparams(),
    )(xt, bias, dft)


def _attn_mask(i, j, blk):
    rows = i * blk + lax.broadcasted_iota(jnp.int32, (blk, blk), 0)
    cols = j * blk + lax.broadcasted_iota(jnp.int32, (blk, blk), 1)
    return cols <= rows


def _attn_fwd(q, k, v, f_col, f_row, blk):
    t, width = q.shape
    nh = width // HEAD
    nq = t // blk

    def body(q_ref, k_ref, v_ref, fc_ref, fr_ref, o_ref, lse_ref):
        i = pl.program_id(0)
        for h in range(nh):
            cs = slice(h * HEAD, (h + 1) * HEAD)
            qh = q_ref[:, cs]
            fq = fc_ref[:, h:h + 1]

            def step(j, carry):
                m, l, acc = carry
                rs = pl.ds(pl.multiple_of(j * blk, blk), blk)
                s = _bdot_raw(qh, k_ref[rs, cs], _NT) + fq - fr_ref[j, h:h + 1, :]
                s = jnp.where(_attn_mask(i, j, blk), s, NEG_INF)
                m_new = jnp.maximum(m, jnp.max(s, axis=1, keepdims=True))
                p = jnp.exp(s - m_new)
                alpha = jnp.exp(m - m_new)
                l_new = alpha * l + jnp.sum(p, axis=1, keepdims=True)
                acc_new = alpha * acc + _bdot_raw(p, v_ref[rs, cs], _NN)
                return m_new, l_new, acc_new

            init = (jnp.full((blk, 1), NEG_INF, F32), jnp.zeros((blk, 1), F32), jnp.zeros((blk, HEAD), F32))
            m, l, acc = lax.fori_loop(0, i + 1, step, init)
            o_ref[:, cs] = acc / l
            lse_ref[:, h:h + 1] = m + jnp.log(l)

    vm = pl.BlockSpec(memory_space=pltpu.VMEM)
    return pl.pallas_call(
        body, name="fox_attn_fwd",
        grid=(nq,),
        in_specs=[pl.BlockSpec((blk, width), lambda i: (i, 0)), vm, vm,
                  pl.BlockSpec((blk, nh), lambda i: (i, 0)), vm],
        out_specs=[pl.BlockSpec((blk, width), lambda i: (i, 0)), pl.BlockSpec((blk, nh), lambda i: (i, 0))],
        out_shape=[jax.ShapeDtypeStruct((t, width), F32), jax.ShapeDtypeStruct((t, nh), F32)],
        compiler_params=_cparams(dimension_semantics=("parallel",)),
    )(q, k, v, f_col, f_row)


def _attn_bwd_dq(q, k, v, f_col, f_row, o, do, lse, blk):
    t, width = q.shape
    nh = width // HEAD
    nq = t // blk

    def body(q_ref, k_ref, v_ref, fc_ref, fr_ref, o_ref, do_ref, lse_ref, dq_ref, dfc_ref, dl_ref):
        i = pl.program_id(0)
        for h in range(nh):
            cs = slice(h * HEAD, (h + 1) * HEAD)
            qh = q_ref[:, cs]
            doh = do_ref[:, cs]
            fq = fc_ref[:, h:h + 1]
            lse_h = lse_ref[:, h:h + 1]
            delta = jnp.sum(doh.astype(F32) * o_ref[:, cs], axis=1, keepdims=True)

            def step(j, carry):
                dq, dfq = carry
                rs = pl.ds(pl.multiple_of(j * blk, blk), blk)
                kj = k_ref[rs, cs]
                s = _bdot_raw(qh, kj, _NT) + fq - fr_ref[j, h:h + 1, :]
                p = jnp.where(_attn_mask(i, j, blk), jnp.exp(s - lse_h), 0.0)
                dp = _bdot_raw(doh, v_ref[rs, cs], _NT)
                ds = p * (dp - delta)
                return dq + _bdot_raw(ds, kj, _NN), dfq + jnp.sum(ds, axis=1, keepdims=True)

            dq, dfq = lax.fori_loop(0, i + 1, step, (jnp.zeros((blk, HEAD), F32), jnp.zeros((blk, 1), F32)))
            dq_ref[:, cs] = dq
            dfc_ref[:, h:h + 1] = dfq
            dl_ref[:, h:h + 1] = delta

    vm = pl.BlockSpec(memory_space=pltpu.VMEM)
    wide = pl.BlockSpec((blk, width), lambda i: (i, 0))
    thin = pl.BlockSpec((blk, nh), lambda i: (i, 0))
    return pl.pallas_call(
        body, name="fox_attn_bwd_dq",
        grid=(nq,),
        in_specs=[wide, vm, vm, thin, vm, wide, wide, thin],
        out_specs=[wide, thin, thin],
        out_shape=[jax.ShapeDtypeStruct((t, width), F32), jax.ShapeDtypeStruct((t, nh), F32),
                   jax.ShapeDtypeStruct((t, nh), F32)],
        compiler_params=_cparams(dimension_semantics=("parallel",)),
    )(q, k, v, f_col, f_row, o, do, lse)


def _attn_bwd_dkv(q, k, v, f_col, f_row, do, lse, delta, blk):
    t, width = q.shape
    nh = width // HEAD
    nq = t // blk

    def body(q_ref, k_ref, v_ref, fc_ref, fr_ref, do_ref, lse_ref, dl_ref, dk_ref, dv_ref, dfr_ref):
        j = pl.program_id(0)
        for h in range(nh):
            cs = slice(h * HEAD, (h + 1) * HEAD)
            kj = k_ref[:, cs]
            vj = v_ref[:, cs]
            fs = fr_ref[0, h:h + 1, :]

            def step(i, carry):
                dk, dv, dfs = carry
                rs = pl.ds(pl.multiple_of(i * blk, blk), blk)
                qi = q_ref[rs, cs]
                doi = do_ref[rs, cs]
                s = _bdot_raw(qi, kj, _NT) + fc_ref[rs, h:h + 1] - fs
                p = jnp.where(_attn_mask(i, j, blk), jnp.exp(s - lse_ref[rs, h:h + 1]), 0.0)
                dp = _bdot_raw(doi, vj, _NT)
                ds = p * (dp - dl_ref[rs, h:h + 1])
                return (dk + _bdot_raw(ds, qi, _TN), dv + _bdot_raw(p, doi, _TN),
                        dfs - jnp.sum(ds, axis=0, keepdims=True))

            init = (jnp.zeros((blk, HEAD), F32), jnp.zeros((blk, HEAD), F32), jnp.zeros((1, blk), F32))
            dk, dv, dfs = lax.fori_loop(j, nq, step, init)
            dk_ref[:, cs] = dk
            dv_ref[:, cs] = dv
            dfr_ref[0, h:h + 1, :] = dfs

    vm = pl.BlockSpec(memory_space=pltpu.VMEM)
    wide = pl.BlockSpec((blk, width), lambda j: (j, 0))
    frow = pl.BlockSpec((1, nh, blk), lambda j: (j, 0, 0))
    return pl.pallas_call(
        body, name="fox_attn_bwd_dkv",
        grid=(nq,),
        in_specs=[vm, wide, wide, vm, frow, vm, vm, vm],
        out_specs=[wide, wide, frow],
        out_shape=[jax.ShapeDtypeStruct((t, width), F32), jax.ShapeDtypeStruct((t, width), F32),
                   jax.ShapeDtypeStruct((nq, nh, blk), F32)],
        compiler_params=_cparams(dimension_semantics=("parallel",)),
    )(q, k, v, f_col, f_row, do, lse, delta)


def _shift_down(u, n):
    row = lax.broadcasted_iota(jnp.int32, u.shape, 0)
    return jnp.where(row < n, 0.0, pltpu.roll(u, n, 0))


def _shift_up(u, n):
    t = u.shape[0]
    row = lax.broadcasted_iota(jnp.int32, u.shape, 0)
    return jnp.where(row >= t - n, 0.0, pltpu.roll(u, t - n, 0))


def _conv3(u, w, b):
    return w[0:1] * _shift_down(u, 2) + w[1:2] * _shift_down(u, 1) + w[2:3] * u + b


def _convglu_fwd(u, cw, cb):
    t, two_f = u.shape
    nb = two_f // (2 * LANES)

    def body(u_ref, w_ref, b_ref, a_ref):
        c = _conv3(u_ref[...], w_ref[...], b_ref[...])
        a_ref[...] = (_silu(c[:, :LANES]) * c[:, LANES:]).astype(a_ref.dtype)

    return pl.pallas_call(
        body, name="convglu_fwd",
        grid=(nb,),
        in_specs=[pl.BlockSpec((t, 2 * LANES), lambda j: (0, j)),
                  pl.BlockSpec((CONV_TAPS, 2 * LANES), lambda j: (0, j)),
                  pl.BlockSpec((1, 2 * LANES), lambda j: (0, j))],
        out_specs=pl.BlockSpec((t, LANES), lambda j: (0, j)),
        out_shape=jax.ShapeDtypeStruct((t, two_f // 2), BF16),
        compiler_params=_cparams(dimension_semantics=("parallel",)),
    )(u, cw, cb)


def _convglu_bwd(u, cw, cb, da):
    t, two_f = u.shape
    nb = two_f // (2 * LANES)

    def body(u_ref, w_ref, b_ref, da_ref, du_ref, dw_ref, db_ref):
        uv = u_ref[...]
        w = w_ref[...]
        u1 = _shift_down(uv, 1)
        u2 = _shift_down(uv, 2)
        c = w[0:1] * u2 + w[1:2] * u1 + w[2:3] * uv + b_ref[...]
        gc, vc = c[:, :LANES], c[:, LANES:]
        sg = _sigmoid(gc)
        dav = da_ref[...].astype(F32)
        dgc = dav * vc * (sg * (1.0 + gc * (1.0 - sg)))
        dvc = dav * (gc * sg)
        dc = jnp.concatenate([dgc, dvc], axis=1)
        du = w[2:3] * dc + w[1:2] * _shift_up(dc, 1) + w[0:1] * _shift_up(dc, 2)
        du_ref[...] = du.astype(du_ref.dtype)
        dw_ref[...] = jnp.concatenate([jnp.sum(dc * u2, axis=0, keepdims=True),
                                       jnp.sum(dc * u1, axis=0, keepdims=True),
                                       jnp.sum(dc * uv, axis=0, keepdims=True)], axis=0)
        db_ref[...] = jnp.sum(dc, axis=0, keepdims=True)

    pair = pl.BlockSpec((t, 2 * LANES), lambda j: (0, j))
    return pl.pallas_call(
        body, name="convglu_bwd",
        grid=(nb,),
        in_specs=[pair, pl.BlockSpec((CONV_TAPS, 2 * LANES), lambda j: (0, j)),
                  pl.BlockSpec((1, 2 * LANES), lambda j: (0, j)),
                  pl.BlockSpec((t, LANES), lambda j: (0, j))],
        out_specs=[pair, pl.BlockSpec((CONV_TAPS, 2 * LANES), lambda j: (0, j)),
                   pl.BlockSpec((1, 2 * LANES), lambda j: (0, j))],
        out_shape=[jax.ShapeDtypeStruct((t, two_f), BF16), jax.ShapeDtypeStruct((CONV_TAPS, two_f), F32),
                   jax.ShapeDtypeStruct((1, two_f), F32)],
        compiler_params=_cparams(dimension_semantics=("parallel",)),
    )(u, cw, cb, da)


def _interleave_cols(w, two_f):
    nb = two_f // (2 * LANES)
    lead = w.shape[:-1]
    return w.reshape(*lead, 2, nb, LANES).swapaxes(-3, -2).reshape(*lead, two_f)


def _deinterleave_cols(w, two_f):
    nb = two_f // (2 * LANES)
    lead = w.shape[:-1]
    return w.reshape(*lead, nb, 2, LANES).swapaxes(-3, -2).reshape(*lead, two_f)


def _local_step(x, target, mods, lb, wts, small, *, tb=512, attn_blk=256):
    t, d = x.shape
    nh = d // HEAD
    two_f = wts["up0"].shape[1]
    one = lambda a: (a, d, False)

    def ffn_fwd(h2, l):
        u = _mm(h2, wts[f"up{l}"], "nn", F32, f"ffn{l}_up", tn=1408)
        a = _convglu_fwd(u, small[f"conv_w{l}"], small[f"conv_b{l}"])
        f = _mm(a, wts[f"down{l}"], "nn", F32, f"ffn{l}_down", tk=2816)
        return u, a, f

    def ffn_bwd(df, h2, u, a, l):
        da = _mm(df, wts[f"down{l}"], "nt", BF16, f"ffn{l}_down_dx", tn=1408)
        dwd = _mm(a, df, "tn", BF16, f"ffn{l}_down_dw", tm=1408, tk=1024)
        du, dcw, dcb = _convglu_bwd(u, small[f"conv_w{l}"], small[f"conv_b{l}"], da)
        dh2 = _mm(du, wts[f"up{l}"], "nt", F32, f"ffn{l}_up_dx", tk=2816)
        dwu = _mm(h2, du, "tn", BF16, f"ffn{l}_up_dw", tn=1408, tk=1024)
        return dh2, dwu, dwd, dcw, dcb

    (h_a,) = _row_fwd(_f_mod, [(x, d, 0)], [one(mods["sh1_0"]), one(mods["sc1_0"])], [BF16],
                      ncol=1, tb=tb, name="l0_mod1")
    proj_a = _mm(h_a, wts["a_in"], "nn", F32, "a_in")
    ypre, states = _hgrn2_fwd(proj_a, lb, small["a_norm_g"], tb)
    y_a = _mm(ypre, wts["a_out"], "nn", F32, "a_out")
    x1, h2_0 = _row_fwd(_f_res_mod, [(x, d, 0), (y_a, d, 0)],
                        [one(mods["g1_0"]), one(mods["sh2_0"]), one(mods["sc2_0"])], [F32, BF16],
                        ncol=1, tb=tb, name="l0_res_mod2")
    u0, a0, f0 = ffn_fwd(h2_0, 0)
    x2, h_kv, h_q = _row_fwd(_f_res_mod2, [(x1, d, 0), (f0, d, 0)],
                             [one(mods["g2_0"]), one(mods["kv_sh"]), one(mods["kv_sc"]),
                              one(mods["sh1_1"]), one(mods["sc1_1"])], [F32, BF16, BF16],
                             ncol=1, tb=tb, name="l0_res_kvmod_qmod")
    proj_kv = _mm(h_kv, wts["kv"], "nn", F32, "kv_proj")
    proj_f = _mm(h_kv, wts["kv_f"], "nn", F32, "kv_fproj")
    (k_n,) = _row_fwd(_f_knorm, [(proj_kv, HEAD, 0)], [(small["k_norm_g"], HEAD, False)], [BF16],
                      ncol=nh, tb=tb, name="k_norm")
    v_b = proj_kv[:, d:].astype(BF16)
    f_logit_t = proj_f[:, :nh].T
    f_bias = small["kv_b_f"].reshape(nh, 1)
    f_t = _fgate_fwd(f_logit_t, f_bias)
    f_col = f_t.T
    f_row = f_t.reshape(nh, t // attn_blk, attn_blk).transpose(1, 0, 2)
    proj_q = _mm(h_q, wts["b_q"], "nn", F32, "b_q")
    (q_n,) = _row_fwd(_f_qnorm, [(proj_q, HEAD, 0)], [(small["q_norm_g"], HEAD, False)], [BF16],
                      ncol=nh, tb=tb, name="q_norm")
    o_att, lse = _attn_fwd(q_n, k_n, v_b, f_col, f_row, attn_blk)
    (z,) = _row_fwd(_f_outgate, [(o_att, HEAD, 0), (proj_q, HEAD, nh)], [], [BF16],
                    ncol=nh, tb=tb, name="out_gate")
    y_b = _mm(z, wts["b_out"], "nn", F32, "b_out")
    x3, h2_1 = _row_fwd(_f_res_mod, [(x2, d, 0), (y_b, d, 0)],
                        [one(mods["g1_1"]), one(mods["sh2_1"]), one(mods["sc2_1"])], [F32, BF16],
                        ncol=1, tb=tb, name="l1_res_mod2")
    u1, a1, f1 = ffn_fwd(h2_1, 1)
    loss, dx4, df1, dg2_1 = _loss_call(x3, f1, mods["g2_1"], target, tb)

    g = {}
    dmods = {"g2_1": dg2_1}
    dh2, g["up1"], g["down1"], g["conv_w1"], g["conv_b1"] = ffn_bwd(df1, h2_1, u1, a1, 1)
    (dx2, dy_b), (dmods["g1_1"], dmods["sh2_1"], dmods["sc2_1"]) = _row_bwd(
        _f_res_mod, [(x2, d, 0), (y_b, d, 0)],
        [one(mods["g1_1"]), one(mods["sh2_1"]), one(mods["sc2_1"])],
        [(dx4, d, 0), (dh2, d, 0)], [F32, BF16], ncol=1, tb=tb, name="l1_res_mod2_bwd")
    dz = _mm(dy_b, wts["b_out"], "nt", F32, "b_out_dx")
    g["b_out"] = _mm(z, dy_b, "tn", BF16, "b_out_dw", tk=1024)
    (do_att, dog), _ = _row_bwd(_f_outgate, [(o_att, HEAD, 0), (proj_q, HEAD, nh)], [], [(dz, HEAD, 0)],
                                [BF16, BF16], ncol=nh, tb=tb, name="out_gate_bwd")
    dq_n, dfc_q, delta = _attn_bwd_dq(q_n, k_n, v_b, f_col, f_row, o_att, do_att, lse, attn_blk)
    dk_n, dv, dfr_k = _attn_bwd_dkv(q_n, k_n, v_b, f_col, f_row, do_att, lse, delta, attn_blk)
    (dpq,), (g["q_norm_g"],) = _row_bwd(_f_qnorm, [(proj_q, HEAD, 0)], [(small["q_norm_g"], HEAD, False)],
                                        [(dq_n, HEAD, 0)], [BF16], ncol=nh, tb=tb, name="q_norm_bwd")
    dproj_q = jnp.concatenate([dpq, dog], axis=1)
    dh_q = _mm(dproj_q, wts["b_q"], "nt", F32, "b_q_dx")
    g["b_q"] = _mm(h_q, dproj_q, "tn", BF16, "b_q_dw", tk=1024)
    (dpk,), (g["k_norm_g"],) = _row_bwd(_f_knorm, [(proj_kv, HEAD, 0)], [(small["k_norm_g"], HEAD, False)],
                                        [(dk_n, HEAD, 0)], [BF16], ncol=nh, tb=tb, name="k_norm_bwd")
    dproj_kv = jnp.concatenate([dpk, dv.astype(BF16)], axis=1)
    df_t = dfc_q.T + dfr_k.transpose(1, 0, 2).reshape(nh, t)
    dflogit_t, g["kv_b_f"] = _fgate_bwd(f_logit_t, f_bias, df_t)
    dproj_f = jnp.pad(dflogit_t.T, ((0, 0), (0, LANES - nh))).astype(BF16)
    dh_kv = _mm(dproj_kv, wts["kv"], "nt", F32, "kv_proj_dx") + _mm(dproj_f, wts["kv_f"], "nt", F32, "kv_fproj_dx")
    g["kv"] = _mm(h_kv, dproj_kv, "tn", BF16, "kv_proj_dw", tk=1024)
    g["kv_f"] = _mm(h_kv, dproj_f, "tn", F32, "kv_fproj_dw", tk=1024)
    (dx1, df0), (dmods["g2_0"], dmods["kv_sh"], dmods["kv_sc"], dmods["sh1_1"], dmods["sc1_1"]) = _row_bwd(
        _f_res_mod2, [(x1, d, 0), (f0, d, 0)],
        [one(mods["g2_0"]), one(mods["kv_sh"]), one(mods["kv_sc"]), one(mods["sh1_1"]), one(mods["sc1_1"])],
        [(dx2, d, 0), (dh_kv, d, 0), (dh_q, d, 0)], [F32, BF16], ncol=1, tb=tb, name="l0_res_kvmod_qmod_bwd")
    dh2, g["up0"], g["down0"], g["conv_w0"], g["conv_b0"] = ffn_bwd(df0, h2_0, u0, a0, 0)
    (dx0, dy_a), (dmods["g1_0"], dmods["sh2_0"], dmods["sc2_0"]) = _row_bwd(
        _f_res_mod, [(x, d, 0), (y_a, d, 0)],
        [one(mods["g1_0"]), one(mods["sh2_0"]), one(mods["sc2_0"])],
        [(dx1, d, 0), (dh2, d, 0)], [F32, BF16], ncol=1, tb=tb, name="l0_res_mod2_bwd")
    dypre = _mm(dy_a, wts["a_out"], "nt", BF16, "a_out_dx")
    g["a_out"] = _mm(ypre, dy_a, "tn", BF16, "a_out_dw", tk=1024)
    dpa_q, dpa_f, dpa_i, dpa_g, dlb, g["a_norm_g"] = _hgrn2_bwd(proj_a, lb, small["a_norm_g"], states, dypre, tb)
    dproj_a = jnp.concatenate([dpa_q, dpa_f, dpa_i, dpa_g], axis=1)
    dh_a = _mm(dproj_a, wts["a_in"], "nt", F32, "a_in_dx")
    g["a_in"] = _mm(h_a, dproj_a, "tn", BF16, "a_in_dw", tk=1024)
    (dx_mod,), (dmods["sh1_0"], dmods["sc1_0"]) = _row_bwd(
        _f_mod, [(x, d, 0)], [one(mods["sh1_0"]), one(mods["sc1_0"])], [(dh_a, d, 0)], [F32],
        ncol=1, tb=tb, name="l0_mod1_bwd")
    grad_x = _add2(dx0, dx_mod, tb)
    return loss, grad_x, dmods, dlb, g


def _add2(a, b, tb):
    t, d = a.shape
    tb = min(tb, t)

    def body(a_ref, b_ref, o_ref):
        o_ref[...] = a_ref[...] + b_ref[...]

    row = pl.BlockSpec((tb, d), lambda i: (i, 0))
    return pl.pallas_call(body, name="grad_x_sum", grid=(t // tb,), in_specs=[row, row], out_specs=row,
                          out_shape=jax.ShapeDtypeStruct((t, d), F32),
                          compiler_params=_cparams(dimension_semantics=("parallel",)))(a, b)


def _position():
    return lax.axis_index("x"), lax.axis_index("y"), lax.axis_index("c")


def _all_gather(xs, name):
    r, c = xs.shape

    def body(x_ref, out_ref, send_sems, recv_sems, local_sem):
        x, y, cc = _position()
        me, sibling = (x, y, cc), (x, y, 1 - cc)
        chips = [(1 - x, y), (x, 1 - y), (1 - x, 1 - y)]

        def slot(px, py, pc):
            return out_ref.at[4 * px + 2 * py + pc]

        def copy(k, block, to, src=None):
            return pltpu.make_async_remote_copy(
                src_ref=slot(*block) if src is None else src, dst_ref=slot(*block),
                send_sem=send_sems.at[k], recv_sem=recv_sems.at[k], device_id=to, device_id_type=_MESH)

        mine = pltpu.make_async_copy(x_ref, slot(*me), local_sem)
        mine.start()
        first = [copy(0, me, sibling, src=x_ref)]
        first += [copy(1 + j, me, (*chip, cc), src=x_ref) for j, chip in enumerate(chips)]
        for cp in first:
            cp.start()
        passed = [copy(4 + j, (*chip, cc), sibling) for j, chip in enumerate(chips)]
        for j, chip in enumerate(chips):
            copy(1 + j, (*chip, cc), me).wait_recv()
            passed[j].start()
        copy(0, sibling, me).wait_recv()
        for j, chip in enumerate(chips):
            copy(4 + j, (*chip, 1 - cc), me).wait_recv()
        for cp in first + passed:
            cp.wait_send()
        mine.wait()

    return pl.pallas_call(
        body, name=name,
        out_shape=jax.ShapeDtypeStruct((NDEV, r, c), xs.dtype),
        in_specs=[pl.BlockSpec(memory_space=pl.ANY)],
        out_specs=pl.BlockSpec(memory_space=pl.ANY),
        scratch_shapes=[pltpu.SemaphoreType.DMA((7,)), pltpu.SemaphoreType.DMA((7,)), pltpu.SemaphoreType.DMA(())],
    )(xs)


def _rs_sibling_exchange(g):
    _, _, r, c = g.shape

    def body(g_ref, recv_ref, send_sems, recv_sems):
        x, y, cc = _position()
        copies = [pltpu.make_async_remote_copy(
            src_ref=g_ref.at[q, 1 - cc], dst_ref=recv_ref.at[q], send_sem=send_sems.at[q],
            recv_sem=recv_sems.at[q], device_id=(x, y, 1 - cc), device_id_type=_MESH) for q in range(NCHIP)]
        for cp in copies:
            cp.start()
        for cp in copies:
            cp.wait()

    return pl.pallas_call(
        body, name="rs_sibling_exchange",
        out_shape=jax.ShapeDtypeStruct((NCHIP, r, c), g.dtype),
        in_specs=[pl.BlockSpec(memory_space=pl.ANY)],
        out_specs=pl.BlockSpec(memory_space=pl.ANY),
        scratch_shapes=[pltpu.SemaphoreType.DMA((NCHIP,)), pltpu.SemaphoreType.DMA((NCHIP,))],
    )(g)


def _rs_chip_exchange(part):
    _, r, c = part.shape

    def body(p_ref, recv_ref, send_sems, recv_sems, local_sem):
        x, y, cc = _position()
        myq = 2 * x + y
        chips = [(1 - x, y), (x, 1 - y), (1 - x, 1 - y)]

        def copy(k, px, py, src_q, dst_q):
            return pltpu.make_async_remote_copy(
                src_ref=p_ref.at[src_q], dst_ref=recv_ref.at[dst_q], send_sem=send_sems.at[k],
                recv_sem=recv_sems.at[k], device_id=(px, py, cc), device_id_type=_MESH)

        mine = pltpu.make_async_copy(p_ref.at[myq], recv_ref.at[myq], local_sem)
        mine.start()
        sends = [copy(k, px, py, 2 * px + py, myq) for k, (px, py) in enumerate(chips)]
        for cp in sends:
            cp.start()
        for k, (px, py) in enumerate(chips):
            copy(k, px, py, myq, 2 * px + py).wait_recv()
        for cp in sends:
            cp.wait_send()
        mine.wait()

    return pl.pallas_call(
        body, name="rs_chip_exchange",
        out_shape=jax.ShapeDtypeStruct((NCHIP, r, c), part.dtype),
        in_specs=[pl.BlockSpec(memory_space=pl.ANY)],
        out_specs=pl.BlockSpec(memory_space=pl.ANY),
        scratch_shapes=[pltpu.SemaphoreType.DMA((3,)), pltpu.SemaphoreType.DMA((3,)), pltpu.SemaphoreType.DMA(())],
    )(part)


def _pair_sum(own, got, tr):
    n, r, c = own.shape

    def body(a_ref, b_ref, o_ref):
        o_ref[...] = (a_ref[...].astype(F32) + b_ref[...].astype(F32)).astype(o_ref.dtype)

    spec = pl.BlockSpec((1, tr, c), lambda q, i: (q, i, 0))
    return pl.pallas_call(body, name="rs_pair_sum", grid=(n, r // tr), in_specs=[spec, spec], out_specs=spec,
                          out_shape=jax.ShapeDtypeStruct((n, r, c), own.dtype),
                          compiler_params=_cparams(dimension_semantics=("parallel", "parallel")))(own, got)


def _slab_sum(slabs, tr):
    n, r, c = slabs.shape

    def body(s_ref, o_ref):
        acc = s_ref[0].astype(F32)
        for q in range(1, n):
            acc = acc + s_ref[q].astype(F32)
        o_ref[...] = acc

    return pl.pallas_call(body, name=f"slab_sum_{n}x{r}x{c}", grid=(r // tr,),
                          in_specs=[pl.BlockSpec((n, tr, c), lambda i: (0, i, 0))],
                          out_specs=pl.BlockSpec((tr, c), lambda i: (i, 0)),
                          out_shape=jax.ShapeDtypeStruct((r, c), F32),
                          compiler_params=_cparams(dimension_semantics=("parallel",)))(slabs)


def _ada_fwd(c_all, ada_w, kv_ada_w, logits):
    rows, d = c_all.shape
    n0, nkv = ada_w.shape[2], kv_ada_w.shape[1]

    def body(c_ref, w_ref, kw_ref, lg_ref, part_ref, cact_ref, lb_ref):
        ca = _silu(c_ref[...])
        cact_ref[...] = ca
        part_ref[:, 0:n0] = _bdot_raw(ca, w_ref[0], _NN)
        part_ref[:, n0:2 * n0] = _bdot_raw(ca, w_ref[1], _NN)
        part_ref[:, 2 * n0:2 * n0 + nkv] = _bdot_raw(ca, kw_ref[...], _NN)
        lb_ref[...] = _sigmoid(lg_ref[0:1, :] - lg_ref[1:2, :])

    vm = pl.BlockSpec(memory_space=pltpu.VMEM)
    return pl.pallas_call(
        body, name="ada_fwd", in_specs=[vm, vm, vm, vm], out_specs=[vm, vm, vm],
        out_shape=[jax.ShapeDtypeStruct((rows, 2 * n0 + nkv), F32), jax.ShapeDtypeStruct((rows, d), F32),
                   jax.ShapeDtypeStruct((1, d), F32)],
        compiler_params=_cparams(),
    )(c_all, ada_w, kv_ada_w, logits)


def _ada_bwd(c_act, dm0, dm1, dkv, lb, dlb):
    rows, d = c_act.shape

    def body(c_ref, d0_ref, d1_ref, dk_ref, lb_ref, dlb_ref, dw_ref, dkw_ref, dlg_ref):
        ca = c_ref[...]
        dw_ref[0] = _bdot_raw(ca, d0_ref[...], _TN)
        dw_ref[1] = _bdot_raw(ca, d1_ref[...], _TN)
        dkw_ref[...] = _bdot_raw(ca, dk_ref[...], _TN)
        lbv = lb_ref[...]
        dl0 = dlb_ref[...] * lbv * (1.0 - lbv)
        dlg_ref[0:1, :] = dl0
        dlg_ref[1:2, :] = -dl0

    vm = pl.BlockSpec(memory_space=pltpu.VMEM)
    return pl.pallas_call(
        body, name="ada_bwd", in_specs=[vm] * 6, out_specs=[vm, vm, vm],
        out_shape=[jax.ShapeDtypeStruct((2, d, dm0.shape[1]), F32), jax.ShapeDtypeStruct((d, dkv.shape[1]), F32),
                   jax.ShapeDtypeStruct((2, d), F32)],
        compiler_params=_cparams(),
    )(c_act, dm0, dm1, dkv, lb, dlb)


def _adamw(w, g, m, v, name, tr=512):
    r, c = w.shape
    tr = _divisor_tile(r, tr, unit=8)
    c1 = 1.0 - ADAM_B1 ** ADAM_STEP
    c2 = 1.0 - ADAM_B2 ** ADAM_STEP

    def body(w_ref, g_ref, m_ref, v_ref, d_ref, mo_ref, vo_ref):
        gv = g_ref[...]
        mn = ADAM_B1 * m_ref[...] + (1.0 - ADAM_B1) * gv
        vn = ADAM_B2 * v_ref[...] + (1.0 - ADAM_B2) * (gv * gv)
        d_ref[...] = -ADAM_LR * ((mn / c1) / (jnp.sqrt(vn / c2) + ADAM_EPS) + ADAM_WD * w_ref[...])
        mo_ref[...] = mn
        vo_ref[...] = vn

    spec = pl.BlockSpec((tr, c), lambda i: (i, 0))
    out = jax.ShapeDtypeStruct((r, c), F32)
    return pl.pallas_call(body, name=name, grid=(r // tr,), in_specs=[spec] * 4, out_specs=[spec] * 3,
                          out_shape=[out, out, out],
                          compiler_params=_cparams(dimension_semantics=("parallel",)))(w, g, m, v)


def _pad_rows(a, rows):
    return jnp.pad(a, ((0, rows - a.shape[0]), (0, 0)))


def _pack_small(parts, lanes=LANES, row_unit=8):
    flat = jnp.concatenate([p.reshape(-1).astype(F32) for p in parts])
    rows = -(-flat.shape[0] // lanes)
    rows = -(-rows // row_unit) * row_unit
    return jnp.pad(flat, (0, rows * lanes - flat.shape[0])).reshape(rows, lanes)


def _unpack_small(flat, shapes):
    out, off = [], 0
    for s in shapes:
        n = 1
        for k in s:
            n *= k
        out.append(flat[off:off + n].reshape(s))
        off += n
    return out


def kernel(x, c, ada_w, ada_b, a_w_in, a_lb_logits, a_norm_g, a_w_out, kv_ada_w, kv_ada_b, kv_w, kv_b_f, k_norm_g, b_w_q, q_norm_g, b_w_out, ffn_w_up, ffn_conv_w, ffn_conv_b, ffn_w_down, loss_target, m_ada_w, m_ada_b, m_a_w_in, m_a_lb_logits, m_a_norm_g, m_a_w_out, m_kv_ada_w, m_kv_ada_b, m_kv_w, m_kv_b_f, m_k_norm_g, m_b_w_q, m_q_norm_g, m_b_w_out, m_ffn_w_up, m_ffn_conv_w, m_ffn_conv_b, m_ffn_w_down, v_ada_w, v_ada_b, v_a_w_in, v_a_lb_logits, v_a_norm_g, v_a_w_out, v_kv_ada_w, v_kv_ada_b, v_kv_w, v_kv_b_f, v_k_norm_g, v_b_w_q, v_q_norm_g, v_b_w_out, v_ffn_w_up, v_ffn_conv_w, v_ffn_conv_b, v_ffn_w_down):
    t, d = x.shape[1], x.shape[2]
    nh = d // HEAD
    two_f = ffn_w_up.shape[2] * NDEV
    ff = two_f // 2
    me = 4 * lax.axis_index("x") + 2 * lax.axis_index("y") + lax.axis_index("c")
    weights = dict(ada_w=ada_w, ada_b=ada_b, a_w_in=a_w_in, a_lb_logits=a_lb_logits, a_norm_g=a_norm_g,
                   a_w_out=a_w_out, kv_ada_w=kv_ada_w, kv_ada_b=kv_ada_b, kv_w=kv_w, kv_b_f=kv_b_f,
                   k_norm_g=k_norm_g, b_w_q=b_w_q, q_norm_g=q_norm_g, b_w_out=b_w_out, ffn_w_up=ffn_w_up,
                   ffn_conv_w=ffn_conv_w, ffn_conv_b=ffn_conv_b, ffn_w_down=ffn_w_down)
    m_in = dict(ada_w=m_ada_w, ada_b=m_ada_b, a_w_in=m_a_w_in, a_lb_logits=m_a_lb_logits, a_norm_g=m_a_norm_g,
                a_w_out=m_a_w_out, kv_ada_w=m_kv_ada_w, kv_ada_b=m_kv_ada_b, kv_w=m_kv_w, kv_b_f=m_kv_b_f,
                k_norm_g=m_k_norm_g, b_w_q=m_b_w_q, q_norm_g=m_q_norm_g, b_w_out=m_b_w_out, ffn_w_up=m_ffn_w_up,
                ffn_conv_w=m_ffn_conv_w, ffn_conv_b=m_ffn_conv_b, ffn_w_down=m_ffn_w_down)
    v_in = dict(ada_w=v_ada_w, ada_b=v_ada_b, a_w_in=v_a_w_in, a_lb_logits=v_a_lb_logits, a_norm_g=v_a_norm_g,
                a_w_out=v_a_w_out, kv_ada_w=v_kv_ada_w, kv_ada_b=v_kv_ada_b, kv_w=v_kv_w, kv_b_f=v_kv_b_f,
                k_norm_g=v_k_norm_g, b_w_q=v_b_w_q, q_norm_g=v_q_norm_g, b_w_out=v_b_w_out, ffn_w_up=v_ffn_w_up,
                ffn_conv_w=v_ffn_conv_w, ffn_conv_b=v_ffn_conv_b, ffn_w_down=v_ffn_w_down)
    order = list(weights)

    big_names = ["a_w_in", "a_w_out", "kv_w", "b_w_q", "b_w_out", "ffn_w_up", "ffn_w_down"]
    big_rows = [weights[n].size // PACK_W for n in big_names]
    r_used = sum(big_rows)
    r_pack = -(-r_used // 384) * 384
    shard = jnp.concatenate([weights[n].astype(BF16).reshape(-1, PACK_W) for n in big_names], axis=0)
    gathered_w = _all_gather(_pad_rows(shard, r_pack), "gather_weights")

    pre = _pack_small([c, a_lb_logits, ffn_conv_w])
    pre_all = _all_gather(pre, "gather_small_inputs").reshape(NDEV, -1)
    c_all = pre_all[:, :d]
    logits = pre_all[:, d:d + 2 * HEAD].reshape(NDEV, 2, HEAD).transpose(1, 0, 2).reshape(2, d)
    ncw = two_f // NDEV
    conv_w_full = pre_all[:, d + 2 * HEAD:d + 2 * HEAD + 2 * CONV_TAPS * ncw]
    conv_w_full = conv_w_full.reshape(NDEV, 2, CONV_TAPS, ncw).transpose(1, 2, 0, 3).reshape(2, CONV_TAPS, two_f)

    part, c_act, lb = _ada_fwd(_pad_rows(c_all, 2 * NDEV), ada_w, kv_ada_w, logits)
    part_all = _all_gather(part[:NDEV], "gather_adaln")
    mine = lax.dynamic_index_in_dim(part_all, me, axis=1, keepdims=False)
    n0, nkv = ada_w.shape[2], kv_ada_w.shape[1]
    mod_names = ["sh1", "sc1", "g1", "sh2", "sc2", "g2"]
    mods = {}
    for l in range(2):
        row = mine[:, l * n0:(l + 1) * n0].reshape(-1) + ada_b[l]
        for k, nm in enumerate(mod_names):
            mods[f"{nm}_{l}"] = row[k * d:(k + 1) * d].reshape(1, d)
    kvrow = mine[:, 2 * n0:2 * n0 + nkv].reshape(-1) + kv_ada_b
    mods["kv_sh"], mods["kv_sc"] = kvrow[:d].reshape(1, d), kvrow[d:].reshape(1, d)

    def cols(off, rows, k, n_loc):
        return gathered_w[:, off:off + rows].reshape(NDEV, k, n_loc).transpose(1, 0, 2).reshape(k, NDEV * n_loc)

    offs = [0]
    for rws in big_rows:
        offs.append(offs[-1] + rws)
    kv_full = cols(offs[2], big_rows[2], d, kv_w.shape[1])
    up_all = gathered_w[:, offs[5]:offs[6]].reshape(NDEV, 2, d, ncw)
    down_all = gathered_w[:, offs[6]:offs[7]].reshape(NDEV, 2, ff // NDEV, d)
    wts = {
        "a_in": cols(offs[0], big_rows[0], d, a_w_in.shape[2]),
        "a_out": gathered_w[:, offs[1]:offs[2]].reshape(d, d),
        "kv": kv_full[:, :2 * d],
        "kv_f": jnp.pad(kv_full[:, 2 * d:], ((0, 0), (0, LANES - nh))),
        "b_q": cols(offs[3], big_rows[3], d, b_w_q.shape[2]),
        "b_out": gathered_w[:, offs[4]:offs[5]].reshape(d, d),
    }
    small = {"a_norm_g": a_norm_g, "k_norm_g": k_norm_g.reshape(1, HEAD), "q_norm_g": q_norm_g, "kv_b_f": kv_b_f}
    for l in range(2):
        up = up_all[:, l].transpose(1, 0, 2).reshape(d, two_f)
        wts[f"up{l}"] = _interleave_cols(up, two_f)
        wts[f"down{l}"] = down_all[:, l].reshape(ff, d)
        small[f"conv_w{l}"] = _interleave_cols(conv_w_full[l], two_f)
        small[f"conv_b{l}"] = _interleave_cols(ffn_conv_b[l].reshape(1, two_f), two_f)

    loss_v, grad_x, dmods, dlb, g = _local_step(x[0], loss_target[0], mods, lb, wts, small)
    loss = lax.psum(loss_v[0, 0], ("x", "y", "c"))

    def by_cols(a, n_loc):
        k = a.shape[0]
        return a.reshape(k, NDEV, n_loc).transpose(1, 0, 2).reshape(NDEV, -1, PACK_W)

    g_kv = jnp.concatenate([g["kv"], g["kv_f"][:, :nh].astype(BF16)], axis=1)
    g_up = jnp.stack([_deinterleave_cols(g[f"up{l}"], two_f).reshape(d, NDEV, ncw).transpose(1, 0, 2)
                      for l in range(2)], axis=1)
    g_down = jnp.stack([g[f"down{l}"].reshape(NDEV, ff // NDEV, d) for l in range(2)], axis=1)
    g_pack = jnp.concatenate([
        by_cols(g["a_in"], a_w_in.shape[2]), g["a_out"].reshape(NDEV, -1, PACK_W), by_cols(g_kv, kv_w.shape[1]),
        by_cols(g["b_q"], b_w_q.shape[2]), g["b_out"].reshape(NDEV, -1, PACK_W),
        g_up.reshape(NDEV, -1, PACK_W), g_down.reshape(NDEV, -1, PACK_W)], axis=1)
    g_pack = jnp.pad(g_pack, ((0, 0), (0, r_pack - r_used), (0, 0))).reshape(NCHIP, 2, r_pack, PACK_W)
    from_sibling = _rs_sibling_exchange(g_pack)
    own = lax.dynamic_index_in_dim(g_pack, lax.axis_index("c"), axis=1, keepdims=False)
    chip_part = _pair_sum(own, from_sibling, r_pack // 3)
    from_chips = _rs_chip_exchange(chip_part)
    g_big = _slab_sum(from_chips, r_pack // 3)

    dmod_vec = [dmods[f"{nm}_{l}"] for l in range(2) for nm in mod_names] + [dmods["kv_sh"], dmods["kv_sc"]]
    post = _pack_small(dmod_vec + [dlb, g["a_norm_g"], g["k_norm_g"], g["q_norm_g"],
                                   jnp.pad(g["kv_b_f"].reshape(-1), (0, LANES - nh)),
                                   _deinterleave_cols(g["conv_w0"], two_f), _deinterleave_cols(g["conv_w1"], two_f),
                                   _deinterleave_cols(g["conv_b0"], two_f), _deinterleave_cols(g["conv_b1"], two_f)])
    post_all = _all_gather(post, "gather_small_grads")
    tot = _slab_sum(post_all, post.shape[0]).reshape(-1)
    nmod = 14 * d
    (t_mod, t_lb, t_ang, t_kng, t_qng, t_bf, t_cw, t_cb) = _unpack_small(
        tot, [(nmod,), (1, d), (1, HEAD), (HEAD,), (1, HEAD), (LANES,), (2, CONV_TAPS, two_f), (2, two_f)])
    dm_all = post_all.reshape(NDEV, -1)[:, :nmod]
    dm0 = lax.dynamic_slice_in_dim(dm_all[:, :6 * d], me * n0, n0, axis=1)
    dm1 = lax.dynamic_slice_in_dim(dm_all[:, 6 * d:12 * d], me * n0, n0, axis=1)
    dkv = lax.dynamic_slice_in_dim(dm_all[:, 12 * d:], me * nkv, nkv, axis=1)
    g_ada_w, g_kv_ada_w, g_logits = _ada_bwd(c_act, _pad_rows(dm0, 2 * NDEV), _pad_rows(dm1, 2 * NDEV),
                                              _pad_rows(dkv, 2 * NDEV), lb, t_lb)

    grads = {
        "ada_w": g_ada_w,
        "ada_b": t_mod[:12 * d].reshape(2, 6 * d),
        "a_w_in": g_big[offs[0]:offs[1]].reshape(a_w_in.shape),
        "a_lb_logits": lax.dynamic_slice_in_dim(g_logits, me * HEAD, HEAD, axis=1),
        "a_norm_g": t_ang,
        "a_w_out": g_big[offs[1]:offs[2]].reshape(a_w_out.shape),
        "kv_ada_w": g_kv_ada_w,
        "kv_ada_b": t_mod[12 * d:],
        "kv_w": g_big[offs[2]:offs[3]].reshape(kv_w.shape),
        "kv_b_f": t_bf[:nh],
        "k_norm_g": t_kng,
        "b_w_q": g_big[offs[3]:offs[4]].reshape(b_w_q.shape),
        "q_norm_g": t_qng,
        "b_w_out": g_big[offs[4]:offs[5]].reshape(b_w_out.shape),
        "ffn_w_up": g_big[offs[5]:offs[6]].reshape(ffn_w_up.shape),
        "ffn_conv_w": lax.dynamic_slice_in_dim(t_cw, me * ncw, ncw, axis=2),
        "ffn_conv_b": t_cb,
        "ffn_w_down": g_big[offs[6]:offs[7]].reshape(ffn_w_down.shape),
    }

    big_adam = ["ada_w", "a_w_in", "a_w_out", "kv_ada_w", "kv_w", "b_w_q", "b_w_out", "ffn_w_up", "ffn_w_down"]
    small_adam = [n for n in order if n not in big_adam]
    delta, new_m, new_v = {}, {}, {}
    for n in big_adam:
        shp = weights[n].shape
        two_d = lambda a: a.reshape(-1, shp[-1])
        dl, mn, vn = _adamw(two_d(weights[n]), two_d(grads[n]), two_d(m_in[n]), two_d(v_in[n]), f"adamw_{n}")
        delta[n], new_m[n], new_v[n] = dl.reshape(shp), mn.reshape(shp), vn.reshape(shp)
    packs = [_pack_small([src[n] for n in small_adam]) for src in (weights, grads, m_in, v_in)]
    outs = _adamw(*packs, "adamw_small", tr=packs[0].shape[0])
    shapes = [weights[n].shape for n in small_adam]
    for dst, o in zip((delta, new_m, new_v), outs):
        for n, a in zip(small_adam, _unpack_small(o.reshape(-1), shapes)):
            dst[n] = a

    return (loss, grad_x.reshape(x.shape), *[grads[n] for n in order], *[delta[n] for n in order],
            *[new_m[n] for n in order], *[new_v[n] for n in order])
```

```python
import functools

import jax
import jax.numpy as jnp
from jax import lax
from jax.experimental import pallas as pl
from jax.experimental.pallas import tpu as pltpu

F32 = jnp.float32
BF16 = jnp.bfloat16

NDEV = 8
NCHIP = 4
HEAD = 128
A_CHUNK = 64
CONV_TAPS = 3
EPS = 1e-6
NEG_INF = -1e30
LANES = 128
VMEM_LIMIT = 48 * 1024 * 1024

ADAM_LR = 0.001
ADAM_B1 = 0.9
ADAM_B2 = 0.999
ADAM_EPS = 1e-08
ADAM_WD = 0.01
ADAM_STEP = 10

_NN = (((1,), (0,)), ((), ()))
_NT = (((1,), (1,)), ((), ()))
_TN = (((0,), (0,)), ((), ()))
_MESH = pl.DeviceIdType.MESH


def _cparams(**kw):
    return pltpu.CompilerParams(vmem_limit_bytes=VMEM_LIMIT, **kw)


def _divisor_tile(n, pref, unit=LANES):
    if n <= pref:
        return n
    best = None
    for t in range(unit, pref + 1, unit):
        if n % t == 0:
            best = t
    assert best is not None, (n, pref)
    return best


def _round_up(n, unit):
    return -(-n // unit) * unit


def _bdot_raw(a, b, dims):
    return lax.dot_general(a.astype(BF16), b.astype(BF16), dims, preferred_element_type=F32)


@jax.custom_vjp
def _dot_nn(a, b):
    return _bdot_raw(a, b, _NN)


@jax.custom_vjp
def _dot_nt(a, b):
    return _bdot_raw(a, b, _NT)


@jax.custom_vjp
def _dot_tn(a, b):
    return _bdot_raw(a, b, _TN)


_dot_nn.defvjp(lambda a, b: (_bdot_raw(a, b, _NN), (a, b)),
               lambda r, g: (_dot_nt(g, r[1]), _dot_tn(r[0], g)))
_dot_nt.defvjp(lambda a, b: (_bdot_raw(a, b, _NT), (a, b)),
               lambda r, g: (_dot_nn(g, r[1]), _dot_tn(g, r[0])))
_dot_tn.defvjp(lambda a, b: (_bdot_raw(a, b, _TN), (a, b)),
               lambda r, g: (_dot_nt(r[1], g), _dot_nn(r[0], g)))


def _f32dot(a, b):
    return lax.dot_general(a, b, _NN, precision=lax.Precision.HIGHEST, preferred_element_type=F32)


@jax.custom_vjp
def _tri_cumsum(x, tri, tri_t):
    return _f32dot(tri, x)


_tri_cumsum.defvjp(lambda x, tri, tri_t: (_f32dot(tri, x), (tri, tri_t)),
                   lambda r, g: (_f32dot(r[1], g), jnp.zeros_like(r[0]), jnp.zeros_like(r[1])))


def _sigmoid(x):
    return jax.nn.sigmoid(x)


def _silu(x):
    return x * jax.nn.sigmoid(x)


def _rms(x):
    return x * lax.rsqrt(jnp.mean(x * x, axis=-1, keepdims=True) + EPS)


def _modulate(x, sh, sc):
    return _rms(x) * (1.0 + sc) + sh


def _mm_call(a, b, dims, a_spec, b_spec, o_spec, o_shape, grid, acc_tile, name):
    nk = grid[2]

    def body(a_ref, b_ref, o_ref, *acc):
        p = lax.dot_general(a_ref[...].astype(BF16), b_ref[...].astype(BF16), dims,
                            preferred_element_type=F32)
        if nk == 1:
            o_ref[...] = p.astype(o_ref.dtype)
        else:
            kk = pl.program_id(2)

            @pl.when(kk == 0)
            def _():
                acc[0][...] = p

            @pl.when(kk > 0)
            def _():
                acc[0][...] += p

            @pl.when(kk == nk - 1)
            def _():
                o_ref[...] = acc[0][...].astype(o_ref.dtype)

    return pl.pallas_call(
        body, name=name, grid=grid, in_specs=[a_spec, b_spec], out_specs=o_spec, out_shape=o_shape,
        scratch_shapes=[pltpu.VMEM(acc_tile, F32)] if nk > 1 else [],
        compiler_params=_cparams(dimension_semantics=("parallel", "parallel", "arbitrary")),
    )(a, b)


def _mm(a, b, mode, out_dtype, name, tm=1024, tn=1024, tk=2048):
    if mode == "nn":
        (m, k), (k2, n) = a.shape, b.shape
    elif mode == "nt":
        (m, k), (n, k2) = a.shape, b.shape
    else:
        (k, m), (k2, n) = a.shape, b.shape
    assert k == k2, (a.shape, b.shape, mode)
    tm, tn, tk = _divisor_tile(m, tm), _divisor_tile(n, tn), _divisor_tile(k, tk)
    if mode == "tn":
        a_spec = pl.BlockSpec((tk, tm), lambda i, j, kk: (kk, i))
    else:
        a_spec = pl.BlockSpec((tm, tk), lambda i, j, kk: (i, kk))
    if mode == "nt":
        b_spec = pl.BlockSpec((tn, tk), lambda i, j, kk: (j, kk))
    else:
        b_spec = pl.BlockSpec((tk, tn), lambda i, j, kk: (kk, j))
    return _mm_call(a, b, {"nn": _NN, "nt": _NT, "tn": _TN}[mode], a_spec, b_spec,
                    pl.BlockSpec((tm, tn), lambda i, j, kk: (i, j)), jax.ShapeDtypeStruct((m, n), out_dtype),
                    (m // tm, n // tn, k // tk), (tm, tn), name)


def _mm_wblk(a, wb, out_dtype, name, *, row_off=0, split=1, tm=1024):
    m, k = a.shape
    nb, _, nl = wb.shape
    tm = _divisor_tile(m, tm)
    per = nb // split
    if split == 1:
        o_spec = pl.BlockSpec((tm, nl), lambda i, j, kk: (i, j))
        o_shape = jax.ShapeDtypeStruct((m, nb * nl), out_dtype)
    else:
        o_spec = pl.BlockSpec((None, tm, nl), lambda i, j, kk: (j // per, i, j % per))
        o_shape = jax.ShapeDtypeStruct((split, m, per * nl), out_dtype)
    return _mm_call(a, wb, _NN, pl.BlockSpec((tm, k), lambda i, j, kk: (i, 0)),
                    pl.BlockSpec((None, k, nl), lambda i, j, kk: (j, row_off, 0)),
                    o_spec, o_shape, (m // tm, nb, 1), (tm, nl), name)


def _mm_wblk_dx(dy, wb, out_dtype, name, *, k, row_off=0, split=1, tm=1024):
    nb, _, nl = wb.shape
    m = dy.shape[-2]
    tm = _divisor_tile(m, tm)
    per = nb // split
    if split == 1:
        a_spec = pl.BlockSpec((tm, nl), lambda i, j, kk: (i, kk))
    else:
        a_spec = pl.BlockSpec((None, tm, nl), lambda i, j, kk: (kk // per, i, kk % per))
    return _mm_call(dy, wb, _NT, a_spec, pl.BlockSpec((None, k, nl), lambda i, j, kk: (kk, row_off, 0)),
                    pl.BlockSpec((tm, k), lambda i, j, kk: (i, 0)), jax.ShapeDtypeStruct((m, k), out_dtype),
                    (m // tm, 1, nb), (tm, k), name)


def _mm_wblk_dw(x, dy, name, *, nb, split=1, tk=1024):
    t, k = x.shape
    nl = dy.shape[-1] * split // nb
    tk = _divisor_tile(t, tk)
    per = nb // split
    if split == 1:
        b_spec = pl.BlockSpec((tk, nl), lambda i, j, kk: (kk, j))
    else:
        b_spec = pl.BlockSpec((None, tk, nl), lambda i, j, kk: (j // per, kk, j % per))
    return _mm_call(x, dy, _TN, pl.BlockSpec((tk, k), lambda i, j, kk: (kk, 0)), b_spec,
                    pl.BlockSpec((None, k, nl), lambda i, j, kk: (j, 0, 0)),
                    jax.ShapeDtypeStruct((nb, k, nl), BF16), (1, nb, t // tk), (k, nl), name)


def _row_specs(rows, tb, nsub):
    return [pl.BlockSpec((tb, nsub * cw), functools.partial(lambda i, off: (i, off), off=off))
            for (_, cw, off) in rows]


def _vec_specs(params):
    return [pl.BlockSpec(p.shape, lambda i: (0, 0)) for p in params]


def _row_fwd(f, rows, params, out_dtypes, *, nsub=1, tb, name):
    t = rows[0][0].shape[0]
    tb = min(tb, t)
    n_r, n_p = len(rows), len(params)
    blk = [jax.ShapeDtypeStruct((tb, cw), F32) for (_, cw, _) in rows]
    blk += [jax.ShapeDtypeStruct(p.shape, F32) for p in params]
    out_avals = jax.eval_shape(f, *blk)

    def body(*refs):
        pv = [r[...] for r in refs[n_r:n_r + n_p]]
        for s in range(nsub):
            vals = [r[:, s * cw:(s + 1) * cw].astype(F32) for r, (_, cw, _) in zip(refs[:n_r], rows)]
            outs = f(*vals, *pv)
            for o_ref, o in zip(refs[n_r + n_p:], outs):
                w = o.shape[1]
                o_ref[:, s * w:(s + 1) * w] = o.astype(o_ref.dtype)

    return pl.pallas_call(
        body, name=name,
        grid=(t // tb,),
        in_specs=_row_specs(rows, tb, nsub) + _vec_specs(params),
        out_specs=[pl.BlockSpec((tb, nsub * av.shape[1]), lambda i: (i, 0)) for av in out_avals],
        out_shape=[jax.ShapeDtypeStruct((t, nsub * av.shape[1]), dt) for av, dt in zip(out_avals, out_dtypes)],
        compiler_params=_cparams(dimension_semantics=("parallel",)),
    )(*[r[0] for r in rows], *params)


def _row_bwd(f, rows, params, cots, row_grad_dtypes, *, nsub=1, tb, name, add_to=None):
    t = rows[0][0].shape[0]
    tb = min(tb, t)
    n_r, n_p, n_c = len(rows), len(params), len(cots)
    want = [j for j in range(n_r) if row_grad_dtypes[j] is not None]
    extra = [] if add_to is None else [(add_to[1], rows[add_to[0]][1], 0)]

    def body(*refs):
        i = pl.program_id(0)
        r_in, p_in = refs[:n_r], refs[n_r:n_r + n_p]
        c_in = refs[n_r + n_p:n_r + n_p + n_c]
        e_in = refs[n_r + n_p + n_c:n_r + n_p + n_c + len(extra)]
        outs = refs[n_r + n_p + n_c + len(extra):]
        pv = [r[...] for r in p_in]
        psum = [None] * n_p
        for s in range(nsub):
            vals = [r[:, s * cw:(s + 1) * cw].astype(F32) for r, (_, cw, _) in zip(r_in, rows)]
            cvals = tuple(r[:, s * cw:(s + 1) * cw].astype(F32) for r, (_, cw, _) in zip(c_in, cots))
            _, vjp_fn = jax.vjp(f, *vals, *pv)
            grads = vjp_fn(cvals)
            for o_ref, jr in zip(outs[:len(want)], want):
                cw = rows[jr][1]
                gr = grads[jr]
                if add_to is not None and jr == add_to[0]:
                    gr = gr + e_in[0][:, s * cw:(s + 1) * cw]
                o_ref[:, s * cw:(s + 1) * cw] = gr.astype(o_ref.dtype)
            for jp in range(n_p):
                psum[jp] = grads[n_r + jp] if psum[jp] is None else psum[jp] + grads[n_r + jp]
        for o_ref, g in zip(outs[len(want):], psum):
            @pl.when(i == 0)
            def _():
                o_ref[...] = g

            @pl.when(i > 0)
            def _():
                o_ref[...] += g

    out_specs = [pl.BlockSpec((tb, nsub * rows[jr][1]), lambda i: (i, 0)) for jr in want]
    out_shape = [jax.ShapeDtypeStruct((t, nsub * rows[jr][1]), row_grad_dtypes[jr]) for jr in want]
    out_specs += _vec_specs(params)
    out_shape += [jax.ShapeDtypeStruct(p.shape, F32) for p in params]
    res = pl.pallas_call(
        body, name=name,
        grid=(t // tb,),
        in_specs=_row_specs(rows, tb, nsub) + _vec_specs(params) + _row_specs(cots, tb, nsub)
        + _row_specs(extra, tb, nsub),
        out_specs=out_specs, out_shape=out_shape,
        compiler_params=_cparams(dimension_semantics=("arbitrary",)),
    )(*[r[0] for r in rows], *params, *[c[0] for c in cots], *[e[0] for e in extra])
    return res[:len(want)], res[len(want):]


def _f_mod(x, sh, sc):
    return (_modulate(x, sh, sc),)


def _f_res_mod(x, y, g, sh, sc):
    x1 = x + g * y
    return x1, _modulate(x1, sh, sc)


def _f_res_mod2(x, y, g, sh_a, sc_a, sh_b, sc_b):
    x1 = x + g * y
    return x1, _modulate(x1, sh_a, sc_a), _modulate(x1, sh_b, sc_b)


def _f_qnorm(p, g):
    return (_rms(p) * g * (HEAD ** -0.5),)


def _f_knorm(p, g):
    return (_rms(p) * g,)


def _f_outgate(o, og):
    return (o * _sigmoid(og),)


def _loss_call(x3, f, g2, target, tb):
    t, d = x3.shape
    tb = min(tb, t)

    def body(x_ref, f_ref, g_ref, t_ref, loss_ref, dx_ref, df_ref, dg_ref):
        i = pl.program_id(0)
        fv = f_ref[...]
        g = g_ref[...]
        e = x_ref[...] + g * fv - t_ref[...]
        dx = e * (1.0 / d)
        part = 0.5 * jnp.sum(jnp.sum(e * dx, axis=1, keepdims=True), axis=0, keepdims=True)
        dx_ref[...] = dx
        df_ref[...] = (g * dx).astype(df_ref.dtype)
        dg = jnp.sum(dx * fv, axis=0, keepdims=True)

        @pl.when(i == 0)
        def _():
            loss_ref[...] = jnp.broadcast_to(part, loss_ref.shape)
            dg_ref[...] = dg

        @pl.when(i > 0)
        def _():
            loss_ref[...] += jnp.broadcast_to(part, loss_ref.shape)
            dg_ref[...] += dg

    row = pl.BlockSpec((tb, d), lambda i: (i, 0))
    vec = pl.BlockSpec((1, d), lambda i: (0, 0))
    return pl.pallas_call(
        body, name="loss_head",
        grid=(t // tb,),
        in_specs=[row, row, vec, row],
        out_specs=[pl.BlockSpec((1, LANES), lambda i: (0, 0)), row, row, vec],
        out_shape=[jax.ShapeDtypeStruct((1, LANES), F32), jax.ShapeDtypeStruct((t, d), F32),
                   jax.ShapeDtypeStruct((t, d), BF16), jax.ShapeDtypeStruct((1, d), F32)],
        compiler_params=_cparams(dimension_semantics=("arbitrary",)),
    )(x3, f, g2, target)


def _tri_consts(c):
    r = lax.broadcasted_iota(jnp.int32, (c, c), 0)
    s = lax.broadcasted_iota(jnp.int32, (c, c), 1)
    tri = (s <= r).astype(F32)
    tri_t = (r <= s).astype(F32)
    return tri, tri_t, s <= r


def _hg_chunk(qp, fp, ip, gp, lb, ng, st, tri, tri_t, causal):
    c = qp.shape[0]
    q = _silu(qp)
    fg = lb + (1.0 - lb) * _sigmoid(fp)
    logf = jnp.log(fg)
    k = 1.0 - fg
    b = _tri_cumsum(logf, tri, tri_t)
    row = lax.broadcasted_iota(jnp.int32, (c, 1), 0)
    b_mid = lax.stop_gradient(jnp.sum(jnp.where(row == c // 2, b, 0.0), axis=0, keepdims=True))
    b_last = jnp.sum(jnp.where(row == c - 1, b, 0.0), axis=0, keepdims=True)
    scores = _dot_nt(q * jnp.exp(b - b_mid), k * jnp.exp(b_mid - b))
    scores = jnp.where(causal, scores, 0.0)
    o = _dot_nn(scores, ip) + _dot_nt(q * jnp.exp(b), st)
    st_new = st * jnp.exp(b_last) + _dot_tn(ip, k * jnp.exp(b_last - b))
    y = _rms(o) * ng * _silu(gp)
    return y, st_new


def _hg_specs(tb, nh, rev_nb=None):
    def row(off):
        if rev_nb is None:
            return pl.BlockSpec((tb, HEAD), functools.partial(lambda h, i, off: (i, off + h), off=off))
        return pl.BlockSpec((tb, HEAD), functools.partial(lambda h, i, off: (rev_nb - 1 - i, off + h), off=off))
    return [row(0), row(nh), row(2 * nh), row(3 * nh),
            pl.BlockSpec((1, HEAD), lambda h, i: (0, h)), pl.BlockSpec((1, HEAD), lambda h, i: (0, 0))]


def _hgrn2_fwd(proj, lb, ng, tb):
    t = proj.shape[0]
    nh = proj.shape[1] // (4 * HEAD)
    tb = min(tb, t)
    nb, ncb = t // tb, tb // A_CHUNK

    def body(q_ref, f_ref, i_ref, g_ref, lb_ref, ng_ref, y_ref, s_ref, st_ref):
        i = pl.program_id(1)

        @pl.when(i == 0)
        def _():
            st_ref[...] = jnp.zeros_like(st_ref)

        st = st_ref[...]
        s_ref[0, 0] = st
        tri, tri_t, causal = _tri_consts(A_CHUNK)
        lbv, ngv = lb_ref[...], ng_ref[...]
        for cidx in range(ncb):
            sl = slice(cidx * A_CHUNK, (cidx + 1) * A_CHUNK)
            y, st = _hg_chunk(q_ref[sl, :], f_ref[sl, :], i_ref[sl, :], g_ref[sl, :], lbv, ngv,
                              st, tri, tri_t, causal)
            y_ref[sl, :] = y.astype(y_ref.dtype)
        st_ref[...] = st

    return pl.pallas_call(
        body, name="hgrn2_fwd",
        grid=(nh, nb),
        in_specs=_hg_specs(tb, nh),
        out_specs=[pl.BlockSpec((tb, HEAD), lambda h, i: (i, h)),
                   pl.BlockSpec((1, 1, HEAD, HEAD), lambda h, i: (h, i, 0, 0))],
        out_shape=[jax.ShapeDtypeStruct((t, nh * HEAD), BF16),
                   jax.ShapeDtypeStruct((nh, nb, HEAD, HEAD), F32)],
        scratch_shapes=[pltpu.VMEM((HEAD, HEAD), F32)],
        compiler_params=_cparams(dimension_semantics=("parallel", "arbitrary")),
    )(proj, proj, proj, proj, lb, ng)


def _hgrn2_bwd(proj, lb, ng, states, dy, tb):
    t = proj.shape[0]
    nh = proj.shape[1] // (4 * HEAD)
    tb = min(tb, t)
    nb, ncb = t // tb, tb // A_CHUNK

    def body(q_ref, f_ref, i_ref, g_ref, lb_ref, ng_ref, s_ref, dy_ref,
             dq_ref, df_ref, di_ref, dg_ref, dlb_ref, dng_ref, dst_ref):
        h, i = pl.program_id(0), pl.program_id(1)
        tri, tri_t, causal = _tri_consts(A_CHUNK)
        lbv, ngv = lb_ref[...], ng_ref[...]

        @pl.when(i == 0)
        def _():
            dst_ref[...] = jnp.zeros_like(dst_ref)
            dlb_ref[...] = jnp.zeros_like(dlb_ref)

        @pl.when(jnp.logical_and(i == 0, h == 0))
        def _():
            dng_ref[...] = jnp.zeros_like(dng_ref)

        def fn(qp, fp, ip, gp, lbx, ngx, stx):
            return _hg_chunk(qp, fp, ip, gp, lbx, ngx, stx, tri, tri_t, causal)

        sts = [s_ref[0, 0]]
        for cidx in range(ncb - 1):
            sl = slice(cidx * A_CHUNK, (cidx + 1) * A_CHUNK)
            _, st_new = fn(q_ref[sl, :], f_ref[sl, :], i_ref[sl, :], g_ref[sl, :], lbv, ngv, sts[-1])
            sts.append(st_new)
        dst = dst_ref[...]
        dlb = jnp.zeros((1, HEAD), F32)
        dng = jnp.zeros((1, HEAD), F32)
        for cidx in range(ncb - 1, -1, -1):
            sl = slice(cidx * A_CHUNK, (cidx + 1) * A_CHUNK)
            _, vjp_fn = jax.vjp(fn, q_ref[sl, :], f_ref[sl, :], i_ref[sl, :], g_ref[sl, :], lbv, ngv, sts[cidx])
            gq, gf, gi, gg, glb, gng, dst = vjp_fn((dy_ref[sl, :].astype(F32), dst))
            dq_ref[sl, :] = gq.astype(dq_ref.dtype)
            df_ref[sl, :] = gf.astype(df_ref.dtype)
            di_ref[sl, :] = gi.astype(di_ref.dtype)
            dg_ref[sl, :] = gg.astype(dg_ref.dtype)
            dlb = dlb + glb
            dng = dng + gng
        dst_ref[...] = dst
        dlb_ref[...] += dlb
        dng_ref[...] += dng

    rev = lambda h, i: (nb - 1 - i, h)
    slab = jax.ShapeDtypeStruct((t, nh * HEAD), BF16)
    return pl.pallas_call(
        body, name="hgrn2_bwd",
        grid=(nh, nb),
        in_specs=_hg_specs(tb, nh, rev_nb=nb) + [
            pl.BlockSpec((1, 1, HEAD, HEAD), lambda h, i: (h, nb - 1 - i, 0, 0)),
            pl.BlockSpec((tb, HEAD), rev)],
        out_specs=[pl.BlockSpec((tb, HEAD), rev)] * 4 + [
            pl.BlockSpec((1, HEAD), lambda h, i: (0, h)), pl.BlockSpec((1, HEAD), lambda h, i: (0, 0))],
        out_shape=[slab, slab, slab, slab,
                   jax.ShapeDtypeStruct((1, nh * HEAD), F32), jax.ShapeDtypeStruct((1, HEAD), F32)],
        scratch_shapes=[pltpu.VMEM((HEAD, HEAD), F32)],
        compiler_params=_cparams(dimension_semantics=("arbitrary", "arbitrary")),
    )(proj, proj, proj, proj, lb, ng, states, dy)


def _fgate_consts(cb):
    r = lax.broadcasted_iota(jnp.int32, (cb, cb), 0)
    s = lax.broadcasted_iota(jnp.int32, (cb, cb), 1)
    return (r <= s).astype(F32), (r >= s).astype(F32)


def _fgate_fwd(xt, bias, cb=512):
    nh, t = xt.shape
    cb = min(cb, t)

    def body(x_ref, b_ref, o_ref):
        upper, _ = _fgate_consts(cb)
        carry = jnp.zeros((nh, 1), F32)
        for blk in range(t // cb):
            z = x_ref[:, blk * cb:(blk + 1) * cb] + b_ref[...]
            logf = jnp.minimum(z, 0.0) - jnp.log(1.0 + jnp.exp(-jnp.abs(z)))
            cs = _f32dot(logf, upper) + carry
            o_ref[:, blk * cb:(blk + 1) * cb] = cs
            carry = cs[:, cb - 1:cb]

    vm = pl.BlockSpec(memory_space=pltpu.VMEM)
    return pl.pallas_call(
        body, name="fgate_fwd", in_specs=[vm, vm], out_specs=vm,
        out_shape=jax.ShapeDtypeStruct((nh, t), F32), compiler_params=_cparams(),
    )(xt, bias)


def _fgate_bwd(xt, bias, dft, cb=512):
    nh, t = xt.shape
    cb = min(cb, t)
    nblk = t // cb

    def body(x_ref, b_ref, d_ref, dx_ref, db_ref):
        _, lower = _fgate_consts(cb)
        carry = jnp.zeros((nh, 1), F32)
        db = jnp.zeros((nh, 1), F32)
        for blk in range(nblk - 1, -1, -1):
            sl = slice(blk * cb, (blk + 1) * cb)
            dlogf = _f32dot(d_ref[:, sl], lower) + carry
            carry = dlogf[:, 0:1]
            z = x_ref[:, sl] + b_ref[...]
            dz = dlogf * (1.0 - _sigmoid(z))
            dx_ref[:, sl] = dz
            db = db + jnp.sum(dz, axis=1, keepdims=True)
        db_ref[...] = db

    vm = pl.BlockSpec(memory_space=pltpu.VMEM)
    return pl.pallas_call(
        body, name="fgate_bwd", in_specs=[vm, vm, vm], out_specs=[vm, vm],
        out_shape=[jax.ShapeDtypeStruct((nh, t), F32), jax.ShapeDtypeStruct((nh, 1), F32)],
        compiler_params=_cparams(),
    )(xt, bias, dft)


def _lower_tri(blk):
    return (lax.broadcasted_iota(jnp.int32, (blk, blk), 1) <= lax.broadcasted_iota(jnp.int32, (blk, blk), 0))


def _attn_fwd(q, k, v, f_col, f_row, blk):
    t, width = q.shape
    nh = width // HEAD
    nq = t // blk

    def body(q_ref, k_ref, v_ref, fc_ref, fr_ref, o_ref, lse_ref):
        i = pl.program_id(0)
        tri = _lower_tri(blk)
        for h in range(nh):
            cs = slice(h * HEAD, (h + 1) * HEAD)
            qh = q_ref[:, cs]
            fq = fc_ref[:, h:h + 1]

            def tile(j, carry, masked):
                m, l, acc = carry
                rs = pl.ds(pl.multiple_of(j * blk, blk), blk)
                s = _bdot_raw(qh, k_ref[rs, cs], _NT) + (fq - fr_ref[j, h:h + 1, :])
                if masked:
                    s = jnp.where(tri, s, NEG_INF)
                m_new = jnp.maximum(m, jnp.max(s, axis=1, keepdims=True))
                p = jnp.exp(s - m_new)
                alpha = jnp.exp(m - m_new)
                l_new = alpha * l + jnp.sum(p, axis=1, keepdims=True)
                acc_new = alpha * acc + _bdot_raw(p, v_ref[rs, cs], _NN)
                return m_new, l_new, acc_new

            init = (jnp.full((blk, 1), NEG_INF, F32), jnp.zeros((blk, 1), F32), jnp.zeros((blk, HEAD), F32))
            carry = lax.fori_loop(0, i, lambda j, c: tile(j, c, False), init)
            m, l, acc = tile(i, carry, True)
            o_ref[:, cs] = acc / l
            lse_ref[:, h:h + 1] = m + jnp.log(l)

    vm = pl.BlockSpec(memory_space=pltpu.VMEM)
    return pl.pallas_call(
        body, name="fox_attn_fwd",
        grid=(nq,),
        in_specs=[pl.BlockSpec((blk, width), lambda i: (i, 0)), vm, vm,
                  pl.BlockSpec((blk, nh), lambda i: (i, 0)), vm],
        out_specs=[pl.BlockSpec((blk, width), lambda i: (i, 0)), pl.BlockSpec((blk, nh), lambda i: (i, 0))],
        out_shape=[jax.ShapeDtypeStruct((t, width), F32), jax.ShapeDtypeStruct((t, nh), F32)],
        compiler_params=_cparams(dimension_semantics=("parallel",)),
    )(q, k, v, f_col, f_row)


def _attn_bwd_dq(q, k, v, f_col, f_row, o, do, lse, blk):
    t, width = q.shape
    nh = width // HEAD
    nq = t // blk

    def body(q_ref, k_ref, v_ref, fc_ref, fr_ref, o_ref, do_ref, lse_ref, dq_ref, dfc_ref, dl_ref):
        i = pl.program_id(0)
        tri = _lower_tri(blk)
        for h in range(nh):
            cs = slice(h * HEAD, (h + 1) * HEAD)
            qh = q_ref[:, cs]
            doh = do_ref[:, cs]
            fq = fc_ref[:, h:h + 1]
            lse_h = lse_ref[:, h:h + 1]
            delta = jnp.sum(doh.astype(F32) * o_ref[:, cs], axis=1, keepdims=True)

            def tile(j, carry, masked):
                dq, dfq = carry
                rs = pl.ds(pl.multiple_of(j * blk, blk), blk)
                kj = k_ref[rs, cs]
                s = _bdot_raw(qh, kj, _NT) + (fq - lse_h - fr_ref[j, h:h + 1, :])
                p = jnp.exp(s)
                if masked:
                    p = jnp.where(tri, p, 0.0)
                ds = p * (_bdot_raw(doh, v_ref[rs, cs], _NT) - delta)
                return dq + _bdot_raw(ds, kj, _NN), dfq + jnp.sum(ds, axis=1, keepdims=True)

            carry = lax.fori_loop(0, i, lambda j, c: tile(j, c, False),
                                  (jnp.zeros((blk, HEAD), F32), jnp.zeros((blk, 1), F32)))
            dq, dfq = tile(i, carry, True)
            dq_ref[:, cs] = dq
            dfc_ref[:, h:h + 1] = dfq
            dl_ref[:, h:h + 1] = delta

    vm = pl.BlockSpec(memory_space=pltpu.VMEM)
    wide = pl.BlockSpec((blk, width), lambda i: (i, 0))
    thin = pl.BlockSpec((blk, nh), lambda i: (i, 0))
    return pl.pallas_call(
        body, name="fox_attn_bwd_dq",
        grid=(nq,),
        in_specs=[wide, vm, vm, thin, vm, wide, wide, thin],
        out_specs=[wide, thin, thin],
        out_shape=[jax.ShapeDtypeStruct((t, width), F32), jax.ShapeDtypeStruct((t, nh), F32),
                   jax.ShapeDtypeStruct((t, nh), F32)],
        compiler_params=_cparams(dimension_semantics=("parallel",)),
    )(q, k, v, f_col, f_row, o, do, lse)


def _attn_bwd_dkv(q, k, v, f_col, f_row, do, lse, delta, blk):
    t, width = q.shape
    nh = width // HEAD
    nq = t // blk

    def body(q_ref, k_ref, v_ref, fc_ref, fr_ref, do_ref, lse_ref, dl_ref, dk_ref, dv_ref, dfr_ref):
        j = pl.program_id(0)
        tri = _lower_tri(blk)
        for h in range(nh):
            cs = slice(h * HEAD, (h + 1) * HEAD)
            kj = k_ref[:, cs]
            vj = v_ref[:, cs]
            fs = fr_ref[0, h:h + 1, :]

            def tile(i, carry, masked):
                dk, dv, dfs = carry
                rs = pl.ds(pl.multiple_of(i * blk, blk), blk)
                qi = q_ref[rs, cs]
                doi = do_ref[rs, cs]
                s = _bdot_raw(qi, kj, _NT) + (fc_ref[rs, h:h + 1] - lse_ref[rs, h:h + 1] - fs)
                p = jnp.exp(s)
                if masked:
                    p = jnp.where(tri, p, 0.0)
                ds = p * (_bdot_raw(doi, vj, _NT) - dl_ref[rs, h:h + 1])
                return (dk + _bdot_raw(ds, qi, _TN), dv + _bdot_raw(p, doi, _TN),
                        dfs - jnp.sum(ds, axis=0, keepdims=True))

            init = (jnp.zeros((blk, HEAD), F32), jnp.zeros((blk, HEAD), F32), jnp.zeros((1, blk), F32))
            carry = tile(j, init, True)
            dk, dv, dfs = lax.fori_loop(j + 1, nq, lambda i, c: tile(i, c, False), carry)
            dk_ref[:, cs] = dk
            dv_ref[:, cs] = dv
            dfr_ref[0, h:h + 1, :] = dfs

    vm = pl.BlockSpec(memory_space=pltpu.VMEM)
    wide = pl.BlockSpec((blk, width), lambda j: (j, 0))
    frow = pl.BlockSpec((1, nh, blk), lambda j: (j, 0, 0))
    return pl.pallas_call(
        body, name="fox_attn_bwd_dkv",
        grid=(nq,),
        in_specs=[vm, wide, wide, vm, frow, vm, vm, vm],
        out_specs=[wide, wide, frow],
        out_shape=[jax.ShapeDtypeStruct((t, width), F32), jax.ShapeDtypeStruct((t, width), F32),
                   jax.ShapeDtypeStruct((nq, nh, blk), F32)],
        compiler_params=_cparams(dimension_semantics=("parallel",)),
    )(q, k, v, f_col, f_row, do, lse, delta)


def _shift_down(u, n):
    row = lax.broadcasted_iota(jnp.int32, u.shape, 0)
    return jnp.where(row < n, 0.0, pltpu.roll(u, n, 0))


def _shift_up(u, n):
    t = u.shape[0]
    row = lax.broadcasted_iota(jnp.int32, u.shape, 0)
    return jnp.where(row >= t - n, 0.0, pltpu.roll(u, t - n, 0))


def _convglu_specs(t):
    return [pl.BlockSpec((2, t, LANES), lambda j: (0, 0, j)),
            pl.BlockSpec((2, CONV_TAPS, LANES), lambda j: (0, 0, j)),
            pl.BlockSpec((2, 1, LANES), lambda j: (0, 0, j))]


def _convglu_fwd(u, cw, cb):
    _, t, fp = u.shape

    def body(u_ref, w_ref, b_ref, a_ref):
        c = []
        for hf in range(2):
            uv, w = u_ref[hf], w_ref[hf]
            c.append(w[0:1] * _shift_down(uv, 2) + w[1:2] * _shift_down(uv, 1) + w[2:3] * uv + b_ref[hf])
        a_ref[...] = (_silu(c[0]) * c[1]).astype(a_ref.dtype)

    return pl.pallas_call(
        body, name="convglu_fwd",
        grid=(fp // LANES,),
        in_specs=_convglu_specs(t),
        out_specs=pl.BlockSpec((t, LANES), lambda j: (0, j)),
        out_shape=jax.ShapeDtypeStruct((t, fp), BF16),
        compiler_params=_cparams(dimension_semantics=("parallel",)),
    )(u, cw, cb)


def _convglu_bwd(u, cw, cb, da):
    _, t, fp = u.shape

    def body(u_ref, w_ref, b_ref, da_ref, du_ref, dw_ref, db_ref):
        us, c = [], []
        for hf in range(2):
            uv, w = u_ref[hf], w_ref[hf]
            u1, u2 = _shift_down(uv, 1), _shift_down(uv, 2)
            us.append((uv, u1, u2))
            c.append(w[0:1] * u2 + w[1:2] * u1 + w[2:3] * uv + b_ref[hf])
        gc, vc = c
        sg = _sigmoid(gc)
        dav = da_ref[...].astype(F32)
        dcs = [dav * vc * (sg * (1.0 + gc * (1.0 - sg))), dav * (gc * sg)]
        for hf in range(2):
            dc, w = dcs[hf], w_ref[hf]
            uv, u1, u2 = us[hf]
            du = w[2:3] * dc + w[1:2] * _shift_up(dc, 1) + w[0:1] * _shift_up(dc, 2)
            du_ref[hf] = du.astype(du_ref.dtype)
            dw_ref[hf, 0:1, :] = jnp.sum(dc * u2, axis=0, keepdims=True)
            dw_ref[hf, 1:2, :] = jnp.sum(dc * u1, axis=0, keepdims=True)
            dw_ref[hf, 2:3, :] = jnp.sum(dc * uv, axis=0, keepdims=True)
            db_ref[hf] = jnp.sum(dc, axis=0, keepdims=True)

    specs = _convglu_specs(t)
    return pl.pallas_call(
        body, name="convglu_bwd",
        grid=(fp // LANES,),
        in_specs=specs + [pl.BlockSpec((t, LANES), lambda j: (0, j))],
        out_specs=specs,
        out_shape=[jax.ShapeDtypeStruct((2, t, fp), BF16), jax.ShapeDtypeStruct((2, CONV_TAPS, fp), F32),
                   jax.ShapeDtypeStruct((2, 1, fp), F32)],
        compiler_params=_cparams(dimension_semantics=("parallel",)),
    )(u, cw, cb, da)


def _local_step(x, target, mods, lb, wts, small, *, tb=512, attn_blk=512):
    t, d = x.shape
    nh = d // HEAD
    nb = wts["a_in"].shape[0]
    vec = lambda *names: [mods[n] for n in names]

    def ffn_fwd(h2, l):
        u = _mm_wblk(h2, wts["up"], F32, f"ffn{l}_up", row_off=l, split=2)
        a = _convglu_fwd(u, small[f"conv_w{l}"], small[f"conv_b{l}"])
        f = _mm(a, wts[f"down{l}"], "nn", F32, f"ffn{l}_down", tk=4096)
        return u, a, f

    def ffn_bwd(df, h2, u, a, l):
        da = _mm(df, wts[f"down{l}"], "nt", BF16, f"ffn{l}_down_dx", tn=1536)
        dwd = _mm(a, df, "tn", BF16, f"ffn{l}_down_dw", tm=1536, tk=1024)
        du, dcw, dcb = _convglu_bwd(u, small[f"conv_w{l}"], small[f"conv_b{l}"], da)
        dh2 = _mm_wblk_dx(du, wts["up"], F32, f"ffn{l}_up_dx", k=d, row_off=l, split=2)
        dwu = _mm_wblk_dw(h2, du, f"ffn{l}_up_dw", nb=nb, split=2)
        return dh2, dwu, dwd, dcw, dcb

    (h_a,) = _row_fwd(_f_mod, [(x, d, 0)], vec("sh1_0", "sc1_0"), [BF16], tb=tb, name="l0_mod1")
    proj_a = _mm_wblk(h_a, wts["a_in"], F32, "a_in")
    ypre, states = _hgrn2_fwd(proj_a, lb, small["a_norm_g"], tb)
    y_a = _mm(ypre, wts["a_out"], "nn", F32, "a_out")
    x1, h2_0 = _row_fwd(_f_res_mod, [(x, d, 0), (y_a, d, 0)], vec("g1_0", "sh2_0", "sc2_0"), [F32, BF16],
                        tb=tb, name="l0_res_mod2")
    u0, a0, f0 = ffn_fwd(h2_0, 0)
    x2, h_kv, h_q = _row_fwd(_f_res_mod2, [(x1, d, 0), (f0, d, 0)],
                             vec("g2_0", "kv_sh", "kv_sc", "sh1_1", "sc1_1"), [F32, BF16, BF16],
                             tb=tb, name="l0_res_kvmod_qmod")
    proj_kv = _mm(h_kv, wts["kv"], "nn", F32, "kv_proj")
    proj_f = _mm(h_kv, wts["kv_f"], "nn", F32, "kv_fproj")
    (k_n,) = _row_fwd(_f_knorm, [(proj_kv, HEAD, 0)], [small["k_norm_g"]], [BF16], nsub=nh, tb=tb, name="k_norm")
    v_b = proj_kv[:, d:].astype(BF16)
    f_logit_t = proj_f[:, :nh].T
    f_bias = small["kv_b_f"].reshape(nh, 1)
    f_t = _fgate_fwd(f_logit_t, f_bias)
    f_col = f_t.T
    f_row = f_t.reshape(nh, t // attn_blk, attn_blk).transpose(1, 0, 2)
    proj_q = _mm_wblk(h_q, wts["b_q"], F32, "b_q")
    (q_n,) = _row_fwd(_f_qnorm, [(proj_q, HEAD, 0)], [small["q_norm_g"]], [BF16], nsub=nh, tb=tb, name="q_norm")
    o_att, lse = _attn_fwd(q_n, k_n, v_b, f_col, f_row, attn_blk)
    (z,) = _row_fwd(_f_outgate, [(o_att, HEAD, 0), (proj_q, HEAD, 1)], [], [BF16], nsub=nh, tb=tb, name="out_gate")
    y_b = _mm(z, wts["b_out"], "nn", F32, "b_out")
    x3, h2_1 = _row_fwd(_f_res_mod, [(x2, d, 0), (y_b, d, 0)], vec("g1_1", "sh2_1", "sc2_1"), [F32, BF16],
                        tb=tb, name="l1_res_mod2")
    u1, a1, f1 = ffn_fwd(h2_1, 1)
    loss, dx4, df1, dg2_1 = _loss_call(x3, f1, mods["g2_1"], target, tb)

    g = {}
    dmods = {"g2_1": dg2_1}
    dh2, g["up1"], g["down1"], g["conv_w1"], g["conv_b1"] = ffn_bwd(df1, h2_1, u1, a1, 1)
    (dx2, dy_b), (dmods["g1_1"], dmods["sh2_1"], dmods["sc2_1"]) = _row_bwd(
        _f_res_mod, [(x2, d, 0), (y_b, d, 0)], vec("g1_1", "sh2_1", "sc2_1"),
        [(dx4, d, 0), (dh2, d, 0)], [F32, BF16], tb=tb, name="l1_res_mod2_bwd")
    dz = _mm(dy_b, wts["b_out"], "nt", F32, "b_out_dx")
    g["b_out"] = _mm(z, dy_b, "tn", BF16, "b_out_dw", tk=1024)
    (do_att, dog), _ = _row_bwd(_f_outgate, [(o_att, HEAD, 0), (proj_q, HEAD, 1)], [], [(dz, HEAD, 0)],
                                [BF16, BF16], nsub=nh, tb=tb, name="out_gate_bwd")
    dq_n, dfc_q, delta = _attn_bwd_dq(q_n, k_n, v_b, f_col, f_row, o_att, do_att, lse, attn_blk)
    dk_n, dv, dfr_k = _attn_bwd_dkv(q_n, k_n, v_b, f_col, f_row, do_att, lse, delta, attn_blk)
    (dpq,), (g["q_norm_g"],) = _row_bwd(_f_qnorm, [(proj_q, HEAD, 0)], [small["q_norm_g"]],
                                        [(dq_n, HEAD, 0)], [BF16], nsub=nh, tb=tb, name="q_norm_bwd")
    dproj_q = jnp.concatenate([dpq, dog], axis=1)
    dh_q = _mm_wblk_dx(dproj_q, wts["b_q"], F32, "b_q_dx", k=d)
    g["b_q"] = _mm_wblk_dw(h_q, dproj_q, "b_q_dw", nb=nb)
    (dpk,), (g["k_norm_g"],) = _row_bwd(_f_knorm, [(proj_kv, HEAD, 0)], [small["k_norm_g"]],
                                        [(dk_n, HEAD, 0)], [BF16], nsub=nh, tb=tb, name="k_norm_bwd")
    dproj_kv = jnp.concatenate([dpk, dv.astype(BF16)], axis=1)
    df_t = dfc_q.T + dfr_k.transpose(1, 0, 2).reshape(nh, t)
    dflogit_t, g["kv_b_f"] = _fgate_bwd(f_logit_t, f_bias, df_t)
    dproj_f = jnp.pad(dflogit_t.T, ((0, 0), (0, LANES - nh))).astype(BF16)
    dh_kv = _mm(dproj_kv, wts["kv"], "nt", F32, "kv_proj_dx") + _mm(dproj_f, wts["kv_f"], "nt", F32, "kv_fproj_dx")
    g["kv"] = _mm(h_kv, dproj_kv, "tn", BF16, "kv_proj_dw", tk=1024)
    g["kv_f"] = _mm(h_kv, dproj_f, "tn", F32, "kv_fproj_dw", tk=1024)
    (dx1, df0), (dmods["g2_0"], dmods["kv_sh"], dmods["kv_sc"], dmods["sh1_1"], dmods["sc1_1"]) = _row_bwd(
        _f_res_mod2, [(x1, d, 0), (f0, d, 0)], vec("g2_0", "kv_sh", "kv_sc", "sh1_1", "sc1_1"),
        [(dx2, d, 0), (dh_kv, d, 0), (dh_q, d, 0)], [F32, BF16], tb=tb, name="l0_res_kvmod_qmod_bwd")
    dh2, g["up0"], g["down0"], g["conv_w0"], g["conv_b0"] = ffn_bwd(df0, h2_0, u0, a0, 0)
    (dx0, dy_a), (dmods["g1_0"], dmods["sh2_0"], dmods["sc2_0"]) = _row_bwd(
        _f_res_mod, [(x, d, 0), (y_a, d, 0)], vec("g1_0", "sh2_0", "sc2_0"),
        [(dx1, d, 0), (dh2, d, 0)], [F32, BF16], tb=tb, name="l0_res_mod2_bwd")
    dypre = _mm(dy_a, wts["a_out"], "nt", BF16, "a_out_dx")
    g["a_out"] = _mm(ypre, dy_a, "tn", BF16, "a_out_dw", tk=1024)
    dpa_q, dpa_f, dpa_i, dpa_g, dlb, g["a_norm_g"] = _hgrn2_bwd(proj_a, lb, small["a_norm_g"], states, dypre, tb)
    dproj_a = jnp.concatenate([dpa_q, dpa_f, dpa_i, dpa_g], axis=1)
    dh_a = _mm_wblk_dx(dproj_a, wts["a_in"], F32, "a_in_dx", k=d)
    g["a_in"] = _mm_wblk_dw(h_a, dproj_a, "a_in_dw", nb=nb)
    (grad_x,), (dmods["sh1_0"], dmods["sc1_0"]) = _row_bwd(
        _f_mod, [(x, d, 0)], vec("sh1_0", "sc1_0"), [(dh_a, d, 0)], [F32], tb=tb, name="l0_mod1_bwd",
        add_to=(0, dx0))
    return loss, grad_x, dmods, dlb, g


def _position():
    return lax.axis_index("x"), lax.axis_index("y"), lax.axis_index("c")


def _hbm_specs(n):
    return [pl.BlockSpec(memory_space=pl.ANY)] * n


def _all_gather(arrs, name):
    n = len(arrs)

    def body(*refs):
        x_refs, out_refs = refs[:n], refs[n:2 * n]
        send_sems, recv_sems, local_sems = refs[2 * n:]
        x, y, cc = _position()
        me, sibling = (x, y, cc), (x, y, 1 - cc)
        chips = [(1 - x, y), (x, 1 - y), (1 - x, 1 - y)]

        def copy(a, k, block, to, src=None):
            slot = out_refs[a].at[4 * block[0] + 2 * block[1] + block[2]]
            return pltpu.make_async_remote_copy(
                src_ref=slot if src is None else src, dst_ref=slot,
                send_sem=send_sems.at[7 * a + k], recv_sem=recv_sems.at[7 * a + k],
                device_id=to, device_id_type=_MESH)

        local = [pltpu.make_async_copy(x_refs[a], out_refs[a].at[4 * x + 2 * y + cc], local_sems.at[a])
                 for a in range(n)]
        for cp in local:
            cp.start()
        first = []
        for a in range(n):
            first.append(copy(a, 0, me, sibling, src=x_refs[a]))
            first += [copy(a, 1 + j, me, (*chip, cc), src=x_refs[a]) for j, chip in enumerate(chips)]
        for cp in first:
            cp.start()
        passed = []
        for j, chip in enumerate(chips):
            for a in range(n):
                copy(a, 1 + j, (*chip, cc), me).wait_recv()
                fwd = copy(a, 4 + j, (*chip, cc), sibling)
                fwd.start()
                passed.append(fwd)
        for a in range(n):
            copy(a, 0, sibling, me).wait_recv()
        for j, chip in enumerate(chips):
            for a in range(n):
                copy(a, 4 + j, (*chip, 1 - cc), me).wait_recv()
        for cp in first + passed:
            cp.wait_send()
        for cp in local:
            cp.wait()

    return pl.pallas_call(
        body, name=name,
        out_shape=[jax.ShapeDtypeStruct((NDEV, *a.shape), a.dtype) for a in arrs],
        in_specs=_hbm_specs(n), out_specs=_hbm_specs(n),
        scratch_shapes=[pltpu.SemaphoreType.DMA((7 * n,)), pltpu.SemaphoreType.DMA((7 * n,)),
                        pltpu.SemaphoreType.DMA((n,))],
    )(*arrs)


def _rs_sibling_exchange(gs):
    n = len(gs)

    def body(*refs):
        g_refs, recv_refs = refs[:n], refs[n:2 * n]
        send_sems, recv_sems = refs[2 * n:]
        x, y, cc = _position()
        copies = [pltpu.make_async_remote_copy(
            src_ref=g_refs[a].at[q, 1 - cc], dst_ref=recv_refs[a].at[q], send_sem=send_sems.at[NCHIP * a + q],
            recv_sem=recv_sems.at[NCHIP * a + q], device_id=(x, y, 1 - cc), device_id_type=_MESH)
            for a in range(n) for q in range(NCHIP)]
        for cp in copies:
            cp.start()
        for cp in copies:
            cp.wait()

    return pl.pallas_call(
        body, name="rs_sibling_exchange",
        out_shape=[jax.ShapeDtypeStruct((NCHIP, *g.shape[2:]), g.dtype) for g in gs],
        in_specs=_hbm_specs(n), out_specs=_hbm_specs(n),
        scratch_shapes=[pltpu.SemaphoreType.DMA((NCHIP * n,)), pltpu.SemaphoreType.DMA((NCHIP * n,))],
    )(*gs)


def _rs_chip_exchange(parts):
    n = len(parts)

    def body(*refs):
        p_refs, recv_refs = refs[:n], refs[n:2 * n]
        send_sems, recv_sems, local_sems = refs[2 * n:]
        x, y, cc = _position()
        myq = 2 * x + y
        chips = [(1 - x, y), (x, 1 - y), (1 - x, 1 - y)]

        def copy(a, k, px, py, src_q, dst_q):
            return pltpu.make_async_remote_copy(
                src_ref=p_refs[a].at[src_q], dst_ref=recv_refs[a].at[dst_q], send_sem=send_sems.at[3 * a + k],
                recv_sem=recv_sems.at[3 * a + k], device_id=(px, py, cc), device_id_type=_MESH)

        local = [pltpu.make_async_copy(p_refs[a].at[myq], recv_refs[a].at[myq], local_sems.at[a]) for a in range(n)]
        for cp in local:
            cp.start()
        sends = [copy(a, k, px, py, 2 * px + py, myq) for a in range(n) for k, (px, py) in enumerate(chips)]
        for cp in sends:
            cp.start()
        for a in range(n):
            for k, (px, py) in enumerate(chips):
                copy(a, k, px, py, myq, 2 * px + py).wait_recv()
        for cp in sends:
            cp.wait_send()
        for cp in local:
            cp.wait()

    return pl.pallas_call(
        body, name="rs_chip_exchange",
        out_shape=[jax.ShapeDtypeStruct(p.shape, p.dtype) for p in parts],
        in_specs=_hbm_specs(n), out_specs=_hbm_specs(n),
        scratch_shapes=[pltpu.SemaphoreType.DMA((3 * n,)), pltpu.SemaphoreType.DMA((3 * n,)),
                        pltpu.SemaphoreType.DMA((n,))],
    )(*parts)


def _pair_sum(own, got, name):
    n, r, c = own.shape

    def body(a_ref, b_ref, o_ref):
        o_ref[...] = (a_ref[...].astype(F32) + b_ref[...].astype(F32)).astype(o_ref.dtype)

    spec = pl.BlockSpec((1, r, c), lambda q: (q, 0, 0))
    return pl.pallas_call(body, name=name, grid=(n,), in_specs=[spec, spec], out_specs=spec,
                          out_shape=jax.ShapeDtypeStruct((n, r, c), own.dtype),
                          compiler_params=_cparams(dimension_semantics=("parallel",)))(own, got)


def _slab_sum(slabs, name, tr=None):
    n, r, c = slabs.shape
    tr = r if tr is None else tr

    def body(s_ref, o_ref):
        acc = s_ref[0].astype(F32)
        for q in range(1, n):
            acc = acc + s_ref[q].astype(F32)
        o_ref[...] = acc

    return pl.pallas_call(body, name=name, grid=(r // tr,),
                          in_specs=[pl.BlockSpec((n, tr, c), lambda i: (0, i, 0))],
                          out_specs=pl.BlockSpec((tr, c), lambda i: (i, 0)),
                          out_shape=jax.ShapeDtypeStruct((r, c), F32),
                          compiler_params=_cparams(dimension_semantics=("parallel",)))(slabs)


def _ada_fwd(c_all, ada_w, kv_ada_w, logits):
    rows, d = c_all.shape
    n0, nkv = ada_w.shape[2], kv_ada_w.shape[1]

    def body(c_ref, w_ref, kw_ref, lg_ref, part_ref, cact_ref, lb_ref):
        ca = _silu(c_ref[...])
        cact_ref[...] = ca
        part_ref[:, 0:n0] = _bdot_raw(ca, w_ref[0], _NN)
        part_ref[:, n0:2 * n0] = _bdot_raw(ca, w_ref[1], _NN)
        part_ref[:, 2 * n0:2 * n0 + nkv] = _bdot_raw(ca, kw_ref[...], _NN)
        lb_ref[...] = _sigmoid(lg_ref[0:1, :] - lg_ref[1:2, :])

    vm = pl.BlockSpec(memory_space=pltpu.VMEM)
    return pl.pallas_call(
        body, name="ada_fwd", in_specs=[vm, vm, vm, vm], out_specs=[vm, vm, vm],
        out_shape=[jax.ShapeDtypeStruct((rows, 2 * n0 + nkv), F32), jax.ShapeDtypeStruct((rows, d), F32),
                   jax.ShapeDtypeStruct((1, d), F32)],
        compiler_params=_cparams(),
    )(c_all, ada_w, kv_ada_w, logits)


def _ada_bwd(c_act, dm0, dm1, dkv, lb, dlb):
    rows, d = c_act.shape

    def body(c_ref, d0_ref, d1_ref, dk_ref, lb_ref, dlb_ref, dw_ref, dkw_ref, dlg_ref):
        ca = c_ref[...]
        dw_ref[0] = _bdot_raw(ca, d0_ref[...], _TN)
        dw_ref[1] = _bdot_raw(ca, d1_ref[...], _TN)
        dkw_ref[...] = _bdot_raw(ca, dk_ref[...], _TN)
        lbv = lb_ref[...]
        dl0 = dlb_ref[...] * lbv * (1.0 - lbv)
        dlg_ref[0:1, :] = dl0
        dlg_ref[1:2, :] = -dl0

    vm = pl.BlockSpec(memory_space=pltpu.VMEM)
    return pl.pallas_call(
        body, name="ada_bwd", in_specs=[vm] * 6, out_specs=[vm, vm, vm],
        out_shape=[jax.ShapeDtypeStruct((2, d, dm0.shape[1]), F32), jax.ShapeDtypeStruct((d, dkv.shape[1]), F32),
                   jax.ShapeDtypeStruct((2, d), F32)],
        compiler_params=_cparams(),
    )(c_act, dm0, dm1, dkv, lb, dlb)


def _adamw(w, g, m, v, name, tr=512):
    r, c = w.shape
    tr = _divisor_tile(r, tr, unit=8)
    c1 = 1.0 - ADAM_B1 ** ADAM_STEP
    c2 = 1.0 - ADAM_B2 ** ADAM_STEP

    def body(w_ref, g_ref, m_ref, v_ref, d_ref, mo_ref, vo_ref):
        gv = g_ref[...]
        mn = ADAM_B1 * m_ref[...] + (1.0 - ADAM_B1) * gv
        vn = ADAM_B2 * v_ref[...] + (1.0 - ADAM_B2) * (gv * gv)
        d_ref[...] = -ADAM_LR * ((mn / c1) / (jnp.sqrt(vn / c2) + ADAM_EPS) + ADAM_WD * w_ref[...])
        mo_ref[...] = mn
        vo_ref[...] = vn

    spec = pl.BlockSpec((tr, c), lambda i: (i, 0))
    out = jax.ShapeDtypeStruct((r, c), F32)
    return pl.pallas_call(body, name=name, grid=(r // tr,), in_specs=[spec] * 4, out_specs=[spec] * 3,
                          out_shape=[out, out, out],
                          compiler_params=_cparams(dimension_semantics=("parallel",)))(w, g, m, v)


def _pad_rows(a, rows):
    return jnp.pad(a, ((0, rows - a.shape[0]), (0, 0)))


def _pack_small(parts, lanes=LANES, row_unit=8):
    flat = jnp.concatenate([p.reshape(-1).astype(F32) for p in parts])
    rows = _round_up(-(-flat.shape[0] // lanes), row_unit)
    return jnp.pad(flat, (0, rows * lanes - flat.shape[0])).reshape(rows, lanes)


def _unpack_small(flat, shapes):
    out, off = [], 0
    for s in shapes:
        n = 1
        for k in s:
            n *= k
        out.append(flat[off:off + n].reshape(s))
        off += n
    return out


def _pad_shard_cols(a, n_loc, n_pad):
    lead = a.shape[:-1]
    a = a.reshape(*lead, NDEV, n_loc)
    a = jnp.pad(a, [(0, 0)] * (len(lead) + 1) + [(0, n_pad - n_loc)])
    return a.reshape(*lead, NDEV * n_pad)


def _unpad_shard_cols(a, n_loc, n_pad):
    lead = a.shape[:-1]
    return a.reshape(*lead, NDEV, n_pad)[..., :n_loc].reshape(*lead, NDEV * n_loc)


def kernel(x, c, ada_w, ada_b, a_w_in, a_lb_logits, a_norm_g, a_w_out, kv_ada_w, kv_ada_b, kv_w, kv_b_f, k_norm_g, b_w_q, q_norm_g, b_w_out, ffn_w_up, ffn_conv_w, ffn_conv_b, ffn_w_down, loss_target, m_ada_w, m_ada_b, m_a_w_in, m_a_lb_logits, m_a_norm_g, m_a_w_out, m_kv_ada_w, m_kv_ada_b, m_kv_w, m_kv_b_f, m_k_norm_g, m_b_w_q, m_q_norm_g, m_b_w_out, m_ffn_w_up, m_ffn_conv_w, m_ffn_conv_b, m_ffn_w_down, v_ada_w, v_ada_b, v_a_w_in, v_a_lb_logits, v_a_norm_g, v_a_w_out, v_kv_ada_w, v_kv_ada_b, v_kv_w, v_kv_b_f, v_k_norm_g, v_b_w_q, v_q_norm_g, v_b_w_out, v_ffn_w_up, v_ffn_conv_w, v_ffn_conv_b, v_ffn_w_down):
    t, d = x.shape[1], x.shape[2]
    nh = d // HEAD
    ncw = ffn_w_up.shape[2]
    ncp = _round_up(ncw, LANES)
    two_f = ncw * NDEV
    ff = two_f // 2
    fp = ncp * NDEV // 2
    rd = ffn_w_down.shape[1]
    me = 4 * lax.axis_index("x") + 2 * lax.axis_index("y") + lax.axis_index("c")
    weights = dict(ada_w=ada_w, ada_b=ada_b, a_w_in=a_w_in, a_lb_logits=a_lb_logits, a_norm_g=a_norm_g,
                   a_w_out=a_w_out, kv_ada_w=kv_ada_w, kv_ada_b=kv_ada_b, kv_w=kv_w, kv_b_f=kv_b_f,
                   k_norm_g=k_norm_g, b_w_q=b_w_q, q_norm_g=q_norm_g, b_w_out=b_w_out, ffn_w_up=ffn_w_up,
                   ffn_conv_w=ffn_conv_w, ffn_conv_b=ffn_conv_b, ffn_w_down=ffn_w_down)
    m_in = dict(ada_w=m_ada_w, ada_b=m_ada_b, a_w_in=m_a_w_in, a_lb_logits=m_a_lb_logits, a_norm_g=m_a_norm_g,
                a_w_out=m_a_w_out, kv_ada_w=m_kv_ada_w, kv_ada_b=m_kv_ada_b, kv_w=m_kv_w, kv_b_f=m_kv_b_f,
                k_norm_g=m_k_norm_g, b_w_q=m_b_w_q, q_norm_g=m_q_norm_g, b_w_out=m_b_w_out, ffn_w_up=m_ffn_w_up,
                ffn_conv_w=m_ffn_conv_w, ffn_conv_b=m_ffn_conv_b, ffn_w_down=m_ffn_w_down)
    v_in = dict(ada_w=v_ada_w, ada_b=v_ada_b, a_w_in=v_a_w_in, a_lb_logits=v_a_lb_logits, a_norm_g=v_a_norm_g,
                a_w_out=v_a_w_out, kv_ada_w=v_kv_ada_w, kv_ada_b=v_kv_ada_b, kv_w=v_kv_w, kv_b_f=v_kv_b_f,
                k_norm_g=v_k_norm_g, b_w_q=v_b_w_q, q_norm_g=v_q_norm_g, b_w_out=v_b_w_out, ffn_w_up=v_ffn_w_up,
                ffn_conv_w=v_ffn_conv_w, ffn_conv_b=v_ffn_conv_b, ffn_w_down=v_ffn_w_down)
    order = list(weights)

    up_loc = jnp.pad(ffn_w_up, ((0, 0), (0, 0), (0, ncp - ncw))).astype(BF16).reshape(2 * d, ncp)
    g_ain, g_aout, g_kv, g_bq, g_bout, g_up, g_down = _all_gather(
        [a_w_in[0].astype(BF16), a_w_out[0].astype(BF16), kv_w.astype(BF16), b_w_q[0].astype(BF16),
         b_w_out[0].astype(BF16), up_loc, ffn_w_down.astype(BF16).reshape(2 * rd, d)], "gather_weights")

    pre = _pack_small([c, a_lb_logits, ffn_conv_w])
    (pre_all,) = _all_gather([pre], "gather_small_inputs")
    pre_all = pre_all.reshape(NDEV, -1)
    c_all = pre_all[:, :d]
    logits = pre_all[:, d:d + 2 * HEAD].reshape(NDEV, 2, HEAD).transpose(1, 0, 2).reshape(2, d)
    conv_w_full = pre_all[:, d + 2 * HEAD:d + 2 * HEAD + 2 * CONV_TAPS * ncw]
    conv_w_full = conv_w_full.reshape(NDEV, 2, CONV_TAPS, ncw).transpose(1, 2, 0, 3).reshape(2, CONV_TAPS, two_f)

    part, c_act, lb = _ada_fwd(_pad_rows(c_all, 2 * NDEV), ada_w, kv_ada_w, logits)
    (part_all,) = _all_gather([part[:NDEV]], "gather_adaln")
    mine = lax.dynamic_index_in_dim(part_all, me, axis=1, keepdims=False)
    n0, nkv = ada_w.shape[2], kv_ada_w.shape[1]
    mod_names = ["sh1", "sc1", "g1", "sh2", "sc2", "g2"]
    mods = {}
    for l in range(2):
        row = mine[:, l * n0:(l + 1) * n0].reshape(-1) + ada_b[l]
        for k, nm in enumerate(mod_names):
            mods[f"{nm}_{l}"] = row[k * d:(k + 1) * d].reshape(1, d)
    kvrow = mine[:, 2 * n0:2 * n0 + nkv].reshape(-1) + kv_ada_b
    mods["kv_sh"], mods["kv_sc"] = kvrow[:d].reshape(1, d), kvrow[d:].reshape(1, d)

    kv_full = g_kv.transpose(1, 0, 2).reshape(d, NDEV * kv_w.shape[1])
    wts = {
        "a_in": g_ain, "a_out": g_aout.reshape(d, d), "kv": kv_full[:, :2 * d],
        "kv_f": jnp.pad(kv_full[:, 2 * d:], ((0, 0), (0, LANES - nh))),
        "b_q": g_bq, "b_out": g_bout.reshape(d, d), "up": g_up,
    }
    small = {"a_norm_g": a_norm_g, "k_norm_g": k_norm_g.reshape(1, HEAD), "q_norm_g": q_norm_g, "kv_b_f": kv_b_f}
    down_all = g_down.reshape(NDEV, 2, rd, d)
    for l in range(2):
        dn = down_all[:, l].reshape(NCHIP, ff // NCHIP, d)
        wts[f"down{l}"] = jnp.pad(dn, ((0, 0), (0, ncp - ncw), (0, 0))).reshape(fp, d)
        small[f"conv_w{l}"] = _pad_shard_cols(conv_w_full[l], ncw, ncp).reshape(CONV_TAPS, 2, fp).transpose(1, 0, 2)
        small[f"conv_b{l}"] = _pad_shard_cols(ffn_conv_b[l], ncw, ncp).reshape(2, 1, fp)

    loss_v, grad_x, dmods, dlb, g = _local_step(x[0], loss_target[0], mods, lb, wts, small)
    loss = lax.psum(loss_v[0, 0], ("x", "y", "c"))

    g_kvw = jnp.concatenate([g["kv"], g["kv_f"][:, :nh].astype(BF16)], axis=1)
    g_kvw = g_kvw.reshape(d, NDEV, kv_w.shape[1]).transpose(1, 0, 2)
    g_dn = [g[f"down{l}"].reshape(NCHIP, ncp, d)[:, :ncw].reshape(NDEV, rd, d) for l in range(2)]
    rs_names = ["a_w_in", "a_w_out", "kv_w", "b_w_q", "b_w_out", "up0", "up1", "down0", "down1"]
    rs_in = [g["a_in"], g["a_out"].reshape(NDEV, d // NDEV, d), g_kvw, g["b_q"],
             g["b_out"].reshape(NDEV, d // NDEV, d), g["up0"], g["up1"], g_dn[0], g_dn[1]]
    rs_in = [a.reshape(NCHIP, 2, *a.shape[1:]) for a in rs_in]
    from_sibling = _rs_sibling_exchange(rs_in)
    cc = lax.axis_index("c")
    chip_parts = [_pair_sum(lax.dynamic_index_in_dim(a, cc, axis=1, keepdims=False), b, f"rs_pair_sum_{nm}")
                  for a, b, nm in zip(rs_in, from_sibling, rs_names)]
    from_chips = _rs_chip_exchange(chip_parts)
    g_sum = {nm: _slab_sum(a, f"rs_slab_sum_{nm}") for a, nm in zip(from_chips, rs_names)}

    def conv_w_grad(a):
        return _unpad_shard_cols(a.transpose(1, 0, 2).reshape(CONV_TAPS, 2 * fp), ncw, ncp)

    def conv_b_grad(a):
        return _unpad_shard_cols(a.reshape(2 * fp), ncw, ncp)

    dmod_vec = [dmods[f"{nm}_{l}"] for l in range(2) for nm in mod_names] + [dmods["kv_sh"], dmods["kv_sc"]]
    post = _pack_small(dmod_vec + [dlb, g["a_norm_g"], g["k_norm_g"], g["q_norm_g"],
                                   jnp.pad(g["kv_b_f"].reshape(-1), (0, LANES - nh)),
                                   conv_w_grad(g["conv_w0"]), conv_w_grad(g["conv_w1"]),
                                   conv_b_grad(g["conv_b0"]), conv_b_grad(g["conv_b1"])])
    (post_all,) = _all_gather([post], "gather_small_grads")
    tot = _slab_sum(post_all, "small_grad_sum").reshape(-1)
    nmod = 14 * d
    (t_mod, t_lb, t_ang, t_kng, t_qng, t_bf, t_cw, t_cb) = _unpack_small(
        tot, [(nmod,), (1, d), (1, HEAD), (HEAD,), (1, HEAD), (LANES,), (2, CONV_TAPS, two_f), (2, two_f)])
    dm_all = post_all.reshape(NDEV, -1)[:, :nmod]
    dm0 = lax.dynamic_slice_in_dim(dm_all[:, :6 * d], me * n0, n0, axis=1)
    dm1 = lax.dynamic_slice_in_dim(dm_all[:, 6 * d:12 * d], me * n0, n0, axis=1)
    dkv = lax.dynamic_slice_in_dim(dm_all[:, 12 * d:], me * nkv, nkv, axis=1)
    g_ada_w, g_kv_ada_w, g_logits = _ada_bwd(c_act, _pad_rows(dm0, 2 * NDEV), _pad_rows(dm1, 2 * NDEV),
                                              _pad_rows(dkv, 2 * NDEV), lb, t_lb)

    grads = {
        "ada_w": g_ada_w,
        "ada_b": t_mod[:12 * d].reshape(2, 6 * d),
        "a_w_in": g_sum["a_w_in"].reshape(a_w_in.shape),
        "a_lb_logits": lax.dynamic_slice_in_dim(g_logits, me * HEAD, HEAD, axis=1),
        "a_norm_g": t_ang,
        "a_w_out": g_sum["a_w_out"].reshape(a_w_out.shape),
        "kv_ada_w": g_kv_ada_w,
        "kv_ada_b": t_mod[12 * d:],
        "kv_w": g_sum["kv_w"],
        "kv_b_f": t_bf[:nh],
        "k_norm_g": t_kng,
        "b_w_q": g_sum["b_w_q"].reshape(b_w_q.shape),
        "q_norm_g": t_qng,
        "b_w_out": g_sum["b_w_out"].reshape(b_w_out.shape),
        "ffn_w_up": jnp.stack([g_sum["up0"][:, :ncw], g_sum["up1"][:, :ncw]]),
        "ffn_conv_w": lax.dynamic_slice_in_dim(t_cw, me * ncw, ncw, axis=2),
        "ffn_conv_b": t_cb,
        "ffn_w_down": jnp.stack([g_sum["down0"], g_sum["down1"]]),
    }

    big_adam = ["ada_w", "a_w_in", "a_w_out", "kv_ada_w", "kv_w", "b_w_q", "b_w_out", "ffn_w_up", "ffn_w_down"]
    small_adam = [n for n in order if n not in big_adam]
    delta, new_m, new_v = {}, {}, {}
    for n in big_adam:
        shp = weights[n].shape
        two_d = lambda a: a.reshape(-1, shp[-1])
        dl, mn, vn = _adamw(two_d(weights[n]), two_d(grads[n]), two_d(m_in[n]), two_d(v_in[n]), f"adamw_{n}")
        delta[n], new_m[n], new_v[n] = dl.reshape(shp), mn.reshape(shp), vn.reshape(shp)
    packs = [_pack_small([src[n] for n in small_adam]) for src in (weights, grads, m_in, v_in)]
    outs = _adamw(*packs, "adamw_small", tr=packs[0].shape[0])
    shapes = [weights[n].shape for n in small_adam]
    for dst, o in zip((delta, new_m, new_v), outs):
        for n, a in zip(small_adam, _unpack_small(o.reshape(-1), shapes)):
            dst[n] = a

    return (loss, grad_x.reshape(x.shape), *[grads[n] for n in order], *[delta[n] for n in order],
            *[new_m[n] for n in order], *[new_v[n] for n in order])
```

```python
import functools

import jax
import jax.numpy as jnp
from jax import lax
from jax.experimental import pallas as pl
from jax.experimental.pallas import tpu as pltpu

F32 = jnp.float32
BF16 = jnp.bfloat16

NDEV = 8
NCHIP = 4
HEAD = 128
A_CHUNK = 64
CONV_TAPS = 3
EPS = 1e-6
NEG_INF = -1e30
LANES = 128
VMEM_LIMIT = 48 * 1024 * 1024

ADAM_LR = 0.001
ADAM_B1 = 0.9
ADAM_B2 = 0.999
ADAM_EPS = 1e-08
ADAM_WD = 0.01
ADAM_STEP = 10

_NN = (((1,), (0,)), ((), ()))
_NT = (((1,), (1,)), ((), ()))
_TN = (((0,), (0,)), ((), ()))
_MESH = pl.DeviceIdType.MESH


def _cparams(**kw):
    return pltpu.CompilerParams(vmem_limit_bytes=VMEM_LIMIT, **kw)


def _divisor_tile(n, pref, unit=LANES):
    if n <= pref:
        return n
    best = None
    for t in range(unit, pref + 1, unit):
        if n % t == 0:
            best = t
    assert best is not None, (n, pref)
    return best


def _round_up(n, unit):
    return -(-n // unit) * unit


def _bdot_raw(a, b, dims):
    return lax.dot_general(a.astype(BF16), b.astype(BF16), dims, preferred_element_type=F32)


@jax.custom_vjp
def _dot_nn(a, b):
    return _bdot_raw(a, b, _NN)


@jax.custom_vjp
def _dot_nt(a, b):
    return _bdot_raw(a, b, _NT)


@jax.custom_vjp
def _dot_tn(a, b):
    return _bdot_raw(a, b, _TN)


_dot_nn.defvjp(lambda a, b: (_bdot_raw(a, b, _NN), (a, b)),
               lambda r, g: (_dot_nt(g, r[1]), _dot_tn(r[0], g)))
_dot_nt.defvjp(lambda a, b: (_bdot_raw(a, b, _NT), (a, b)),
               lambda r, g: (_dot_nn(g, r[1]), _dot_tn(g, r[0])))
_dot_tn.defvjp(lambda a, b: (_bdot_raw(a, b, _TN), (a, b)),
               lambda r, g: (_dot_nt(r[1], g), _dot_nn(r[0], g)))


def _f32dot(a, b):
    return lax.dot_general(a, b, _NN, precision=lax.Precision.HIGHEST, preferred_element_type=F32)


def _sigmoid(x):
    return jax.nn.sigmoid(x)


def _silu(x):
    return x * jax.nn.sigmoid(x)


def _rms(x):
    return x * lax.rsqrt(jnp.mean(x * x, axis=-1, keepdims=True) + EPS)


def _modulate(x, sh, sc):
    return _rms(x) * (1.0 + sc) + sh


def _mm_call(a, b, dims, a_spec, b_spec, o_spec, o_shape, grid, acc_tile, name):
    nk = grid[2]

    def body(a_ref, b_ref, o_ref, *acc):
        p = lax.dot_general(a_ref[...].astype(BF16), b_ref[...].astype(BF16), dims,
                            preferred_element_type=F32)
        if nk == 1:
            o_ref[...] = p.astype(o_ref.dtype)
        else:
            kk = pl.program_id(2)

            @pl.when(kk == 0)
            def _():
                acc[0][...] = p

            @pl.when(kk > 0)
            def _():
                acc[0][...] += p

            @pl.when(kk == nk - 1)
            def _():
                o_ref[...] = acc[0][...].astype(o_ref.dtype)

    return pl.pallas_call(
        body, name=name, grid=grid, in_specs=[a_spec, b_spec], out_specs=o_spec, out_shape=o_shape,
        scratch_shapes=[pltpu.VMEM(acc_tile, F32)] if nk > 1 else [],
        compiler_params=_cparams(dimension_semantics=("parallel", "parallel", "arbitrary")),
    )(a, b)


def _mm(a, b, mode, out_dtype, name, tm=1024, tn=1024, tk=2048):
    if mode == "nn":
        (m, k), (k2, n) = a.shape, b.shape
    elif mode == "nt":
        (m, k), (n, k2) = a.shape, b.shape
    else:
        (k, m), (k2, n) = a.shape, b.shape
    assert k == k2, (a.shape, b.shape, mode)
    tm, tn, tk = _divisor_tile(m, tm), _divisor_tile(n, tn), _divisor_tile(k, tk)
    if mode == "tn":
        a_spec = pl.BlockSpec((tk, tm), lambda i, j, kk: (kk, i))
    else:
        a_spec = pl.BlockSpec((tm, tk), lambda i, j, kk: (i, kk))
    if mode == "nt":
        b_spec = pl.BlockSpec((tn, tk), lambda i, j, kk: (j, kk))
    else:
        b_spec = pl.BlockSpec((tk, tn), lambda i, j, kk: (kk, j))
    return _mm_call(a, b, {"nn": _NN, "nt": _NT, "tn": _TN}[mode], a_spec, b_spec,
                    pl.BlockSpec((tm, tn), lambda i, j, kk: (i, j)), jax.ShapeDtypeStruct((m, n), out_dtype),
                    (m // tm, n // tn, k // tk), (tm, tn), name)


def _wblk_act_spec(rows, gb, nl, split, row_axis, blk_axis):
    if split == 1:
        return pl.BlockSpec((rows, gb * nl), lambda *g: (g[row_axis], g[blk_axis]))
    return pl.BlockSpec((None, rows, gb * nl), lambda *g: (g[blk_axis], g[row_axis], 0))


def _mm_wblk(a, wb, out_dtype, name, *, gb, row_off=0, split=1, tm=1024):
    m, k = a.shape
    nb, _, nl = wb.shape
    assert nb % gb == 0 and (split == 1 or gb * split == nb)
    tm = _divisor_tile(m, tm)

    def body(a_ref, b_ref, o_ref):
        av = a_ref[...].astype(BF16)
        for s in range(gb):
            o_ref[:, s * nl:(s + 1) * nl] = lax.dot_general(
                av, b_ref[s].astype(BF16), _NN, preferred_element_type=F32).astype(o_ref.dtype)

    o_shape = (m, nb * nl) if split == 1 else (split, m, gb * nl)
    return pl.pallas_call(
        body, name=name, grid=(m // tm, nb // gb),
        in_specs=[pl.BlockSpec((tm, k), lambda i, j: (i, 0)),
                  pl.BlockSpec((gb, k, nl), lambda i, j: (j, row_off, 0))],
        out_specs=_wblk_act_spec(tm, gb, nl, split, 0, 1),
        out_shape=jax.ShapeDtypeStruct(o_shape, out_dtype),
        compiler_params=_cparams(dimension_semantics=("parallel", "parallel")),
    )(a, wb)


def _mm_wblk_dx(dy, wb, out_dtype, name, *, k, gb, row_off=0, split=1, tm=1024):
    nb, _, nl = wb.shape
    assert nb % gb == 0 and (split == 1 or gb * split == nb)
    m = dy.shape[-2]
    tm = _divisor_tile(m, tm)
    nk = nb // gb

    def body(a_ref, b_ref, o_ref, *acc):
        p = None
        for s in range(gb):
            q = lax.dot_general(a_ref[:, s * nl:(s + 1) * nl].astype(BF16), b_ref[s].astype(BF16), _NT,
                                preferred_element_type=F32)
            p = q if p is None else p + q
        if nk == 1:
            o_ref[...] = p.astype(o_ref.dtype)
        else:
            kk = pl.program_id(1)

            @pl.when(kk == 0)
            def _():
                acc[0][...] = p

            @pl.when(kk > 0)
            def _():
                acc[0][...] += p

            @pl.when(kk == nk - 1)
            def _():
                o_ref[...] = acc[0][...].astype(o_ref.dtype)

    return pl.pallas_call(
        body, name=name, grid=(m // tm, nk),
        in_specs=[_wblk_act_spec(tm, gb, nl, split, 0, 1),
                  pl.BlockSpec((gb, k, nl), lambda i, kk: (kk, row_off, 0))],
        out_specs=pl.BlockSpec((tm, k), lambda i, kk: (i, 0)),
        out_shape=jax.ShapeDtypeStruct((m, k), out_dtype),
        scratch_shapes=[pltpu.VMEM((tm, k), F32)] if nk > 1 else [],
        compiler_params=_cparams(dimension_semantics=("parallel", "arbitrary")),
    )(dy, wb)


def _mm_wblk_dw(x, dy, name, *, nb, gb, split=1, tk=1024):
    t, k = x.shape
    assert nb % gb == 0 and (split == 1 or gb * split == nb)
    nl = dy.shape[-1] * split // nb
    tk = _divisor_tile(t, tk)
    nk = t // tk

    def body(a_ref, b_ref, o_ref, acc):
        kk = pl.program_id(1)
        av = a_ref[...].astype(BF16)
        for s in range(gb):
            p = lax.dot_general(av, b_ref[:, s * nl:(s + 1) * nl].astype(BF16), _TN, preferred_element_type=F32)

            @pl.when(kk == 0)
            def _():
                acc[s] = p

            @pl.when(kk > 0)
            def _():
                acc[s] += p

        @pl.when(kk == nk - 1)
        def _():
            o_ref[...] = acc[...].astype(o_ref.dtype)

    return pl.pallas_call(
        body, name=name, grid=(nb // gb, nk),
        in_specs=[pl.BlockSpec((tk, k), lambda j, kk: (kk, 0)), _wblk_act_spec(tk, gb, nl, split, 1, 0)],
        out_specs=pl.BlockSpec((gb, k, nl), lambda j, kk: (j, 0, 0)),
        out_shape=jax.ShapeDtypeStruct((nb, k, nl), BF16),
        scratch_shapes=[pltpu.VMEM((gb, k, nl), F32)],
        compiler_params=_cparams(dimension_semantics=("parallel", "arbitrary")),
    )(x, dy)


def _row_specs(rows, tb, nsub):
    return [pl.BlockSpec((tb, nsub * cw), functools.partial(lambda i, off: (i, off), off=off))
            for (_, cw, off) in rows]


def _vec_specs(params):
    return [pl.BlockSpec(p.shape, lambda i: (0, 0)) for p in params]


def _row_fwd(f, rows, params, out_dtypes, *, nsub=1, tb, name):
    t = rows[0][0].shape[0]
    tb = min(tb, t)
    n_r, n_p = len(rows), len(params)
    blk = [jax.ShapeDtypeStruct((tb, cw), F32) for (_, cw, _) in rows]
    blk += [jax.ShapeDtypeStruct(p.shape, F32) for p in params]
    out_avals = jax.eval_shape(f, *blk)

    def body(*refs):
        pv = [r[...] for r in refs[n_r:n_r + n_p]]
        for s in range(nsub):
            vals = [r[:, s * cw:(s + 1) * cw].astype(F32) for r, (_, cw, _) in zip(refs[:n_r], rows)]
            outs = f(*vals, *pv)
            for o_ref, o in zip(refs[n_r + n_p:], outs):
                w = o.shape[1]
                o_ref[:, s * w:(s + 1) * w] = o.astype(o_ref.dtype)

    return pl.pallas_call(
        body, name=name,
        grid=(t // tb,),
        in_specs=_row_specs(rows, tb, nsub) + _vec_specs(params),
        out_specs=[pl.BlockSpec((tb, nsub * av.shape[1]), lambda i: (i, 0)) for av in out_avals],
        out_shape=[jax.ShapeDtypeStruct((t, nsub * av.shape[1]), dt) for av, dt in zip(out_avals, out_dtypes)],
        compiler_params=_cparams(dimension_semantics=("parallel",)),
    )(*[r[0] for r in rows], *params)


def _row_bwd(f, rows, params, cots, row_grad_dtypes, *, nsub=1, tb, name, add_to=None):
    t = rows[0][0].shape[0]
    tb = min(tb, t)
    n_r, n_p, n_c = len(rows), len(params), len(cots)
    want = [j for j in range(n_r) if row_grad_dtypes[j] is not None]
    extra = [] if add_to is None else [(add_to[1], rows[add_to[0]][1], 0)]

    def body(*refs):
        i = pl.program_id(0)
        r_in, p_in = refs[:n_r], refs[n_r:n_r + n_p]
        c_in = refs[n_r + n_p:n_r + n_p + n_c]
        e_in = refs[n_r + n_p + n_c:n_r + n_p + n_c + len(extra)]
        outs = refs[n_r + n_p + n_c + len(extra):]
        pv = [r[...] for r in p_in]
        psum = [None] * n_p
        for s in range(nsub):
            vals = [r[:, s * cw:(s + 1) * cw].astype(F32) for r, (_, cw, _) in zip(r_in, rows)]
            cvals = tuple(r[:, s * cw:(s + 1) * cw].astype(F32) for r, (_, cw, _) in zip(c_in, cots))
            _, vjp_fn = jax.vjp(f, *vals, *pv)
            grads = vjp_fn(cvals)
            for o_ref, jr in zip(outs[:len(want)], want):
                cw = rows[jr][1]
                gr = grads[jr]
                if add_to is not None and jr == add_to[0]:
                    gr = gr + e_in[0][:, s * cw:(s + 1) * cw]
                o_ref[:, s * cw:(s + 1) * cw] = gr.astype(o_ref.dtype)
            for jp in range(n_p):
                psum[jp] = grads[n_r + jp] if psum[jp] is None else psum[jp] + grads[n_r + jp]
        for o_ref, g in zip(outs[len(want):], psum):
            @pl.when(i == 0)
            def _():
                o_ref[...] = g

            @pl.when(i > 0)
            def _():
                o_ref[...] += g

    out_specs = [pl.BlockSpec((tb, nsub * rows[jr][1]), lambda i: (i, 0)) for jr in want]
    out_shape = [jax.ShapeDtypeStruct((t, nsub * rows[jr][1]), row_grad_dtypes[jr]) for jr in want]
    out_specs += _vec_specs(params)
    out_shape += [jax.ShapeDtypeStruct(p.shape, F32) for p in params]
    res = pl.pallas_call(
        body, name=name,
        grid=(t // tb,),
        in_specs=_row_specs(rows, tb, nsub) + _vec_specs(params) + _row_specs(cots, tb, nsub)
        + _row_specs(extra, tb, nsub),
        out_specs=out_specs, out_shape=out_shape,
        compiler_params=_cparams(dimension_semantics=("arbitrary",)),
    )(*[r[0] for r in rows], *params, *[c[0] for c in cots], *[e[0] for e in extra])
    return res[:len(want)], res[len(want):]


def _f_mod(x, sh, sc):
    return (_modulate(x, sh, sc),)


def _f_res_mod(x, y, g, sh, sc):
    x1 = x + g * y
    return x1, _modulate(x1, sh, sc)


def _f_res_mod2(x, y, g, sh_a, sc_a, sh_b, sc_b):
    x1 = x + g * y
    return x1, _modulate(x1, sh_a, sc_a), _modulate(x1, sh_b, sc_b)


def _f_qnorm(p, g):
    return (_rms(p) * g * (HEAD ** -0.5),)


def _f_knorm(p, g):
    return (_rms(p) * g,)


def _f_outgate(o, og):
    return (o * _sigmoid(og),)


def _loss_call(x3, f, g2, target, tb):
    t, d = x3.shape
    tb = min(tb, t)

    def body(x_ref, f_ref, g_ref, t_ref, loss_ref, dx_ref, df_ref, dg_ref):
        i = pl.program_id(0)
        fv = f_ref[...]
        g = g_ref[...]
        e = x_ref[...] + g * fv - t_ref[...]
        dx = e * (1.0 / d)
        part = 0.5 * jnp.sum(jnp.sum(e * dx, axis=1, keepdims=True), axis=0, keepdims=True)
        dx_ref[...] = dx
        df_ref[...] = (g * dx).astype(df_ref.dtype)
        dg = jnp.sum(dx * fv, axis=0, keepdims=True)

        @pl.when(i == 0)
        def _():
            loss_ref[...] = jnp.broadcast_to(part, loss_ref.shape)
            dg_ref[...] = dg

        @pl.when(i > 0)
        def _():
            loss_ref[...] += jnp.broadcast_to(part, loss_ref.shape)
            dg_ref[...] += dg

    row = pl.BlockSpec((tb, d), lambda i: (i, 0))
    vec = pl.BlockSpec((1, d), lambda i: (0, 0))
    return pl.pallas_call(
        body, name="loss_head",
        grid=(t // tb,),
        in_specs=[row, row, vec, row],
        out_specs=[pl.BlockSpec((1, LANES), lambda i: (0, 0)), row, row, vec],
        out_shape=[jax.ShapeDtypeStruct((1, LANES), F32), jax.ShapeDtypeStruct((t, d), F32),
                   jax.ShapeDtypeStruct((t, d), BF16), jax.ShapeDtypeStruct((1, d), F32)],
        compiler_params=_cparams(dimension_semantics=("arbitrary",)),
    )(x3, f, g2, target)


def _hg_consts(tb):
    c = A_CHUNK
    r = lax.broadcasted_iota(jnp.int32, (c, c), 0)
    s = lax.broadcasted_iota(jnp.int32, (c, c), 1)
    br = lax.broadcasted_iota(jnp.int32, (tb, tb), 0)
    bs = lax.broadcasted_iota(jnp.int32, (tb, tb), 1)
    shift = c.bit_length() - 1
    same_chunk = jnp.right_shift(br, shift) == jnp.right_shift(bs, shift)
    return (s <= r).astype(F32), (r <= s).astype(F32), jnp.logical_and(same_chunk, bs <= br)


def _chunk_apply(mat, x):
    c = mat.shape[0]
    return jnp.concatenate([_f32dot(mat, x[i * c:(i + 1) * c]) for i in range(x.shape[0] // c)], axis=0)


@jax.custom_vjp
def _chunk_cumsum(x, tri, tri_t):
    return _chunk_apply(tri, x)


_chunk_cumsum.defvjp(lambda x, tri, tri_t: (_chunk_apply(tri, x), (tri, tri_t)),
                     lambda r, g: (_chunk_apply(r[1], g), jnp.zeros_like(r[0]), jnp.zeros_like(r[1])))


def _per_chunk(a, b, dims):
    return jnp.stack([_bdot_raw(a[i], b[i], dims) for i in range(a.shape[0])])


@jax.custom_vjp
def _chunk_tn(a, b):
    return _per_chunk(a, b, _TN)


@jax.custom_vjp
def _chunk_nt(a, b):
    return _per_chunk(a, b, _NT)


@jax.custom_vjp
def _chunk_nn(a, b):
    return _per_chunk(a, b, _NN)


_chunk_tn.defvjp(lambda a, b: (_per_chunk(a, b, _TN), (a, b)),
                 lambda r, g: (_chunk_nt(r[1], g), _chunk_nn(r[0], g)))
_chunk_nt.defvjp(lambda a, b: (_per_chunk(a, b, _NT), (a, b)),
                 lambda r, g: (_chunk_nn(g, r[1]), _chunk_tn(g, r[0])))
_chunk_nn.defvjp(lambda a, b: (_per_chunk(a, b, _NN), (a, b)),
                 lambda r, g: (_chunk_nt(g, r[1]), _chunk_tn(r[0], g)))


def _scan_states(decay, m, st):
    sts = []
    for i in range(m.shape[0]):
        sts.append(st)
        st = st * decay[i] + m[i]
    return jnp.stack(sts), st


@jax.custom_vjp
def _state_scan(decay, m, st):
    return _scan_states(decay, m, st)


def _state_scan_fwd(decay, m, st):
    sts, st_out = _scan_states(decay, m, st)
    return (sts, st_out), (decay, sts)


def _state_scan_bwd(res, cts):
    decay, sts = res
    d_sts, g = cts
    d_decay, d_m = [], []
    for i in range(sts.shape[0] - 1, -1, -1):
        d_m.append(g)
        d_decay.append(jnp.sum(g * sts[i], axis=0, keepdims=True))
        g = g * decay[i] + d_sts[i]
    return jnp.stack(d_decay[::-1]), jnp.stack(d_m[::-1]), g


_state_scan.defvjp(_state_scan_fwd, _state_scan_bwd)


def _hg_block(qp, fp, ip, gp, lb, ng, st, tri, tri_t, bd_causal):
    tb = qp.shape[0]
    c = A_CHUNK
    n = tb // c
    q = _silu(qp)
    fg = lb + (1.0 - lb) * _sigmoid(fp)
    logf = jnp.log(fg)
    k = 1.0 - fg
    b3 = _chunk_cumsum(logf, tri, tri_t).reshape(n, c, HEAD)
    pos = lax.broadcasted_iota(jnp.int32, (1, c, 1), 1)
    b_mid = lax.stop_gradient(jnp.sum(jnp.where(pos == c // 2, b3, 0.0), axis=1, keepdims=True))
    b_last = jnp.sum(jnp.where(pos == c - 1, b3, 0.0), axis=1, keepdims=True)
    q3, k3, v3 = q.reshape(n, c, HEAD), k.reshape(n, c, HEAD), ip.reshape(n, c, HEAD)
    scores = _dot_nt((q3 * jnp.exp(b3 - b_mid)).reshape(tb, HEAD), (k3 * jnp.exp(b_mid - b3)).reshape(tb, HEAD))
    o_intra = _dot_nn(jnp.where(bd_causal, scores, 0.0), ip)
    states, st_new = _state_scan(jnp.exp(b_last), _chunk_tn(v3, k3 * jnp.exp(b_last - b3)), st)
    o = o_intra + _chunk_nt(q3 * jnp.exp(b3), states).reshape(tb, HEAD)
    y = _rms(o) * ng * _silu(gp)
    return y, st_new


def _hg_specs(tb, nh, rev_nb=None):
    def row(off):
        if rev_nb is None:
            return pl.BlockSpec((tb, HEAD), functools.partial(lambda h, i, off: (i, off + h), off=off))
        return pl.BlockSpec((tb, HEAD), functools.partial(lambda h, i, off: (rev_nb - 1 - i, off + h), off=off))
    return [row(0), row(nh), row(2 * nh), row(3 * nh),
            pl.BlockSpec((1, HEAD), lambda h, i: (0, h)), pl.BlockSpec((1, HEAD), lambda h, i: (0, 0))]


def _hgrn2_fwd(proj, lb, ng, tb):
    t = proj.shape[0]
    nh = proj.shape[1] // (4 * HEAD)
    tb = min(tb, t)
    nb = t // tb

    def body(q_ref, f_ref, i_ref, g_ref, lb_ref, ng_ref, y_ref, s_ref, st_ref):
        i = pl.program_id(1)

        @pl.when(i == 0)
        def _():
            st_ref[...] = jnp.zeros_like(st_ref)

        st = st_ref[...]
        s_ref[0, 0] = st
        y, st_new = _hg_block(q_ref[...], f_ref[...], i_ref[...], g_ref[...], lb_ref[...], ng_ref[...], st,
                              *_hg_consts(tb))
        y_ref[...] = y.astype(y_ref.dtype)
        st_ref[...] = st_new

    return pl.pallas_call(
        body, name="hgrn2_fwd",
        grid=(nh, nb),
        in_specs=_hg_specs(tb, nh),
        out_specs=[pl.BlockSpec((tb, HEAD), lambda h, i: (i, h)),
                   pl.BlockSpec((1, 1, HEAD, HEAD), lambda h, i: (h, i, 0, 0))],
        out_shape=[jax.ShapeDtypeStruct((t, nh * HEAD), BF16),
                   jax.ShapeDtypeStruct((nh, nb, HEAD, HEAD), F32)],
        scratch_shapes=[pltpu.VMEM((HEAD, HEAD), F32)],
        compiler_params=_cparams(dimension_semantics=("parallel", "arbitrary")),
    )(proj, proj, proj, proj, lb, ng)


def _hgrn2_bwd(proj, lb, ng, states, dy, tb):
    t = proj.shape[0]
    nh = proj.shape[1] // (4 * HEAD)
    tb = min(tb, t)
    nb = t // tb

    def body(q_ref, f_ref, i_ref, g_ref, lb_ref, ng_ref, s_ref, dy_ref,
             dq_ref, df_ref, di_ref, dg_ref, dlb_ref, dng_ref, dst_ref):
        h, i = pl.program_id(0), pl.program_id(1)
        consts = _hg_consts(tb)

        @pl.when(i == 0)
        def _():
            dst_ref[...] = jnp.zeros_like(dst_ref)
            dlb_ref[...] = jnp.zeros_like(dlb_ref)

        @pl.when(jnp.logical_and(i == 0, h == 0))
        def _():
            dng_ref[...] = jnp.zeros_like(dng_ref)

        def fn(qp, fp, ip, gp, lbx, ngx, stx):
            return _hg_block(qp, fp, ip, gp, lbx, ngx, stx, *consts)

        _, vjp_fn = jax.vjp(fn, q_ref[...], f_ref[...], i_ref[...], g_ref[...], lb_ref[...], ng_ref[...],
                            s_ref[0, 0])
        gq, gf, gi, gg, glb, gng, dst = vjp_fn((dy_ref[...].astype(F32), dst_ref[...]))
        dq_ref[...] = gq.astype(dq_ref.dtype)
        df_ref[...] = gf.astype(df_ref.dtype)
        di_ref[...] = gi.astype(di_ref.dtype)
        dg_ref[...] = gg.astype(dg_ref.dtype)
        dst_ref[...] = dst
        dlb_ref[...] += glb
        dng_ref[...] += gng

    rev = lambda h, i: (nb - 1 - i, h)
    slab = jax.ShapeDtypeStruct((t, nh * HEAD), BF16)
    return pl.pallas_call(
        body, name="hgrn2_bwd",
        grid=(nh, nb),
        in_specs=_hg_specs(tb, nh, rev_nb=nb) + [
            pl.BlockSpec((1, 1, HEAD, HEAD), lambda h, i: (h, nb - 1 - i, 0, 0)),
            pl.BlockSpec((tb, HEAD), rev)],
        out_specs=[pl.BlockSpec((tb, HEAD), rev)] * 4 + [
            pl.BlockSpec((1, HEAD), lambda h, i: (0, h)), pl.BlockSpec((1, HEAD), lambda h, i: (0, 0))],
        out_shape=[slab, slab, slab, slab,
                   jax.ShapeDtypeStruct((1, nh * HEAD), F32), jax.ShapeDtypeStruct((1, HEAD), F32)],
        scratch_shapes=[pltpu.VMEM((HEAD, HEAD), F32)],
        compiler_params=_cparams(dimension_semantics=("arbitrary", "arbitrary")),
    )(proj, proj, proj, proj, lb, ng, states, dy)


def _fgate_consts(cb):
    r = lax.broadcasted_iota(jnp.int32, (cb, cb), 0)
    s = lax.broadcasted_iota(jnp.int32, (cb, cb), 1)
    return (r <= s).astype(F32), (r >= s).astype(F32)


def _fgate_fwd(xt, bias, cb=512):
    nh, t = xt.shape
    cb = min(cb, t)

    def body(x_ref, b_ref, o_ref):
        upper, _ = _fgate_consts(cb)
        carry = jnp.zeros((nh, 1), F32)
        for blk in range(t // cb):
            z = x_ref[:, blk * cb:(blk + 1) * cb] + b_ref[...]
            logf = jnp.minimum(z, 0.0) - jnp.log(1.0 + jnp.exp(-jnp.abs(z)))
            cs = _f32dot(logf, upper) + carry
            o_ref[:, blk * cb:(blk + 1) * cb] = cs
            carry = cs[:, cb - 1:cb]

    vm = pl.BlockSpec(memory_space=pltpu.VMEM)
    return pl.pallas_call(
        body, name="fgate_fwd", in_specs=[vm, vm], out_specs=vm,
        out_shape=jax.ShapeDtypeStruct((nh, t), F32), compiler_params=_cparams(),
    )(xt, bias)


def _fgate_bwd(xt, bias, dft, cb=512):
    nh, t = xt.shape
    cb = min(cb, t)
    nblk = t // cb

    def body(x_ref, b_ref, d_ref, dx_ref, db_ref):
        _, lower = _fgate_consts(cb)
        carry = jnp.zeros((nh, 1), F32)
        db = jnp.zeros((nh, 1), F32)
        for blk in range(nblk - 1, -1, -1):
            sl = slice(blk * cb, (blk + 1) * cb)
            dlogf = _f32dot(d_ref[:, sl], lower) + carry
            carry = dlogf[:, 0:1]
            z = x_ref[:, sl] + b_ref[...]
            dz = dlogf * (1.0 - _sigmoid(z))
            dx_ref[:, sl] = dz
            db = db + jnp.sum(dz, axis=1, keepdims=True)
        db_ref[...] = db

    vm = pl.BlockSpec(memory_space=pltpu.VMEM)
    return pl.pallas_call(
        body, name="fgate_bwd", in_specs=[vm, vm, vm], out_specs=[vm, vm],
        out_shape=[jax.ShapeDtypeStruct((nh, t), F32), jax.ShapeDtypeStruct((nh, 1), F32)],
        compiler_params=_cparams(),
    )(xt, bias, dft)


ATTN_ROWS = 32


def _row_chunks(blk, fn):
    for r in range(blk // ATTN_ROWS):
        fn(pl.ds(r * ATTN_ROWS, ATTN_ROWS), r * ATTN_ROWS)


def _chunk_causal(first_row, blk):
    rows = first_row + lax.broadcasted_iota(jnp.int32, (ATTN_ROWS, blk), 0)
    return lax.broadcasted_iota(jnp.int32, (ATTN_ROWS, blk), 1) <= rows


def _attn_fwd(q, k, v, f_col, f_row, blk):
    t, width = q.shape
    nh = width // HEAD
    nq = t // blk

    def body(q_ref, k_ref, v_ref, fc_ref, fr_ref, o_ref, lse_ref):
        i = pl.program_id(0)
        tri = (lax.broadcasted_iota(jnp.int32, (blk, blk), 1) <= lax.broadcasted_iota(jnp.int32, (blk, blk), 0))
        for h in range(nh):
            cs = slice(h * HEAD, (h + 1) * HEAD)
            qh = q_ref[:, cs]
            fq = fc_ref[:, h:h + 1]

            def tile(j, carry, masked):
                m, l, acc = carry
                rs = pl.ds(pl.multiple_of(j * blk, blk), blk)
                s = _bdot_raw(qh, k_ref[rs, cs], _NT) + (fq - fr_ref[j, h:h + 1, :])
                if masked:
                    s = jnp.where(tri, s, NEG_INF)
                m_new = jnp.maximum(m, jnp.max(s, axis=1, keepdims=True))
                p = jnp.exp(s - m_new)
                alpha = jnp.exp(m - m_new)
                l_new = alpha * l + jnp.sum(p, axis=1, keepdims=True)
                acc_new = alpha * acc + _bdot_raw(p, v_ref[rs, cs], _NN)
                return m_new, l_new, acc_new

            init = (jnp.full((blk, 1), NEG_INF, F32), jnp.zeros((blk, 1), F32), jnp.zeros((blk, HEAD), F32))
            carry = lax.fori_loop(0, i, lambda j, c: tile(j, c, False), init)
            m, l, acc = tile(i, carry, True)
            o_ref[:, cs] = acc / l
            lse_ref[:, h:h + 1] = m + jnp.log(l)

    vm = pl.BlockSpec(memory_space=pltpu.VMEM)
    return pl.pallas_call(
        body, name="fox_attn_fwd",
        grid=(nq,),
        in_specs=[pl.BlockSpec((blk, width), lambda i: (i, 0)), vm, vm,
                  pl.BlockSpec((blk, nh), lambda i: (i, 0)), vm],
        out_specs=[pl.BlockSpec((blk, width), lambda i: (i, 0)), pl.BlockSpec((blk, nh), lambda i: (i, 0))],
        out_shape=[jax.ShapeDtypeStruct((t, width), F32), jax.ShapeDtypeStruct((t, nh), F32)],
        compiler_params=_cparams(dimension_semantics=("parallel",)),
    )(q, k, v, f_col, f_row)


def _attn_bwd_dq(q, k, v, f_col, f_row, o, do, lse, blk):
    t, width = q.shape
    nh = width // HEAD
    nq = t // blk

    def body(q_ref, k_ref, v_ref, fc_ref, fr_ref, o_ref, do_ref, lse_ref, dq_ref, dfc_ref, dl_ref,
             s_ref, dp_ref, ds_ref, bias_ref, delta_ref, dfq_ref, acc_ref):
        i = pl.program_id(0)
        for h in range(nh):
            cs = slice(h * HEAD, (h + 1) * HEAD)
            bias_ref[...] = fc_ref[:, h:h + 1] - lse_ref[:, h:h + 1]
            delta_ref[...] = jnp.sum(do_ref[:, cs].astype(F32) * o_ref[:, cs], axis=1, keepdims=True)
            dfq_ref[...] = jnp.zeros_like(dfq_ref)
            acc_ref[...] = jnp.zeros_like(acc_ref)

            def tile(j, masked):
                rs = pl.ds(pl.multiple_of(j * blk, blk), blk)
                s_ref[...] = _bdot_raw(q_ref[:, cs], k_ref[rs, cs], _NT)
                dp_ref[...] = _bdot_raw(do_ref[:, cs], v_ref[rs, cs], _NT)
                fs = fr_ref[j, h:h + 1, :]

                def rows(rr, first_row):
                    p = jnp.exp(s_ref[rr, :] + (bias_ref[rr, :] - fs))
                    if masked:
                        p = jnp.where(_chunk_causal(first_row, blk), p, 0.0)
                    ds = p * (dp_ref[rr, :] - delta_ref[rr, :])
                    dfq_ref[rr, :] += jnp.sum(ds, axis=1, keepdims=True)
                    ds_ref[rr, :] = ds.astype(ds_ref.dtype)

                _row_chunks(blk, rows)
                acc_ref[...] += _bdot_raw(ds_ref[...], k_ref[rs, cs], _NN)

            def off_diagonal(j, carry):
                tile(j, False)
                return carry

            lax.fori_loop(0, i, off_diagonal, 0)
            tile(i, True)
            dq_ref[:, cs] = acc_ref[...]
            dfc_ref[:, h:h + 1] = dfq_ref[...]
            dl_ref[:, h:h + 1] = delta_ref[...]

    vm = pl.BlockSpec(memory_space=pltpu.VMEM)
    wide = pl.BlockSpec((blk, width), lambda i: (i, 0))
    thin = pl.BlockSpec((blk, nh), lambda i: (i, 0))
    col = pltpu.VMEM((blk, 1), F32)
    return pl.pallas_call(
        body, name="fox_attn_bwd_dq",
        grid=(nq,),
        in_specs=[wide, vm, vm, thin, vm, wide, wide, thin],
        out_specs=[wide, thin, thin],
        out_shape=[jax.ShapeDtypeStruct((t, width), F32), jax.ShapeDtypeStruct((t, nh), F32),
                   jax.ShapeDtypeStruct((t, nh), F32)],
        scratch_shapes=[pltpu.VMEM((blk, blk), F32), pltpu.VMEM((blk, blk), F32), pltpu.VMEM((blk, blk), BF16),
                        col, col, col, pltpu.VMEM((blk, HEAD), F32)],
        compiler_params=_cparams(dimension_semantics=("parallel",)),
    )(q, k, v, f_col, f_row, o, do, lse)


def _attn_bwd_dkv(q, k, v, f_col, f_row, do, lse, delta, blk):
    t, width = q.shape
    nh = width // HEAD
    nq = t // blk

    def body(q_ref, k_ref, v_ref, fc_ref, fr_ref, do_ref, lse_ref, dl_ref, dk_ref, dv_ref, dfr_ref,
             s_ref, dp_ref, p_ref, ds_ref, dfs_ref, dk_acc, dv_acc):
        j = pl.program_id(0)
        for h in range(nh):
            cs = slice(h * HEAD, (h + 1) * HEAD)
            fs = fr_ref[0, h:h + 1, :]
            dfs_ref[...] = jnp.zeros_like(dfs_ref)
            dk_acc[...] = jnp.zeros_like(dk_acc)
            dv_acc[...] = jnp.zeros_like(dv_acc)

            def tile(i, masked):
                base = pl.multiple_of(i * blk, blk)
                rs = pl.ds(base, blk)
                s_ref[...] = _bdot_raw(q_ref[rs, cs], k_ref[:, cs], _NT)
                dp_ref[...] = _bdot_raw(do_ref[rs, cs], v_ref[:, cs], _NT)

                def rows(rr, first_row):
                    gr = pl.ds(pl.multiple_of(base + first_row, ATTN_ROWS), ATTN_ROWS)
                    bias = fc_ref[gr, h:h + 1] - lse_ref[gr, h:h + 1]
                    p = jnp.exp(s_ref[rr, :] + (bias - fs))
                    if masked:
                        p = jnp.where(_chunk_causal(first_row, blk), p, 0.0)
                    ds = p * (dp_ref[rr, :] - dl_ref[gr, h:h + 1])
                    dfs_ref[...] -= jnp.sum(ds, axis=0, keepdims=True)
                    p_ref[rr, :] = p.astype(p_ref.dtype)
                    ds_ref[rr, :] = ds.astype(ds_ref.dtype)

                _row_chunks(blk, rows)
                dv_acc[...] += _bdot_raw(p_ref[...], do_ref[rs, cs], _TN)
                dk_acc[...] += _bdot_raw(ds_ref[...], q_ref[rs, cs], _TN)

            def off_diagonal(i, carry):
                tile(i, False)
                return carry

            tile(j, True)
            lax.fori_loop(j + 1, nq, off_diagonal, 0)
            dk_ref[:, cs] = dk_acc[...]
            dv_ref[:, cs] = dv_acc[...]
            dfr_ref[0, h:h + 1, :] = dfs_ref[...]

    vm = pl.BlockSpec(memory_space=pltpu.VMEM)
    wide = pl.BlockSpec((blk, width), lambda j: (j, 0))
    frow = pl.BlockSpec((1, nh, blk), lambda j: (j, 0, 0))
    tile_f32, tile_b16 = pltpu.VMEM((blk, blk), F32), pltpu.VMEM((blk, blk), BF16)
    return pl.pallas_call(
        body, name="fox_attn_bwd_dkv",
        grid=(nq,),
        in_specs=[vm, wide, wide, vm, frow, vm, vm, vm],
        out_specs=[wide, wide, frow],
        out_shape=[jax.ShapeDtypeStruct((t, width), F32), jax.ShapeDtypeStruct((t, width), F32),
                   jax.ShapeDtypeStruct((nq, nh, blk), F32)],
        scratch_shapes=[tile_f32, tile_f32, tile_b16, tile_b16, pltpu.VMEM((1, blk), F32),
                        pltpu.VMEM((blk, HEAD), F32), pltpu.VMEM((blk, HEAD), F32)],
        compiler_params=_cparams(dimension_semantics=("parallel",)),
    )(q, k, v, f_col, f_row, do, lse, delta)


def _shift_down(u, n):
    row = lax.broadcasted_iota(jnp.int32, u.shape, 0)
    return jnp.where(row < n, 0.0, pltpu.roll(u, n, 0))


def _shift_up(u, n):
    t = u.shape[0]
    row = lax.broadcasted_iota(jnp.int32, u.shape, 0)
    return jnp.where(row >= t - n, 0.0, pltpu.roll(u, t - n, 0))


def _convglu_specs(t):
    return [pl.BlockSpec((2, t, LANES), lambda j: (0, 0, j)),
            pl.BlockSpec((2, CONV_TAPS, LANES), lambda j: (0, 0, j)),
            pl.BlockSpec((2, 1, LANES), lambda j: (0, 0, j))]


def _convglu_fwd(u, cw, cb):
    _, t, fp = u.shape

    def body(u_ref, w_ref, b_ref, a_ref):
        c = []
        for hf in range(2):
            uv, w = u_ref[hf], w_ref[hf]
            c.append(w[0:1] * _shift_down(uv, 2) + w[1:2] * _shift_down(uv, 1) + w[2:3] * uv + b_ref[hf])
        a_ref[...] = (_silu(c[0]) * c[1]).astype(a_ref.dtype)

    return pl.pallas_call(
        body, name="convglu_fwd",
        grid=(fp // LANES,),
        in_specs=_convglu_specs(t),
        out_specs=pl.BlockSpec((t, LANES), lambda j: (0, j)),
        out_shape=jax.ShapeDtypeStruct((t, fp), BF16),
        compiler_params=_cparams(dimension_semantics=("parallel",)),
    )(u, cw, cb)


def _convglu_bwd(u, cw, cb, da):
    _, t, fp = u.shape

    def body(u_ref, w_ref, b_ref, da_ref, du_ref, dw_ref, db_ref):
        us, c = [], []
        for hf in range(2):
            uv, w = u_ref[hf], w_ref[hf]
            u1, u2 = _shift_down(uv, 1), _shift_down(uv, 2)
            us.append((uv, u1, u2))
            c.append(w[0:1] * u2 + w[1:2] * u1 + w[2:3] * uv + b_ref[hf])
        gc, vc = c
        sg = _sigmoid(gc)
        dav = da_ref[...].astype(F32)
        dcs = [dav * vc * (sg * (1.0 + gc * (1.0 - sg))), dav * (gc * sg)]
        for hf in range(2):
            dc, w = dcs[hf], w_ref[hf]
            uv, u1, u2 = us[hf]
            du = w[2:3] * dc + w[1:2] * _shift_up(dc, 1) + w[0:1] * _shift_up(dc, 2)
            du_ref[hf] = du.astype(du_ref.dtype)
            dw_ref[hf, 0:1, :] = jnp.sum(dc * u2, axis=0, keepdims=True)
            dw_ref[hf, 1:2, :] = jnp.sum(dc * u1, axis=0, keepdims=True)
            dw_ref[hf, 2:3, :] = jnp.sum(dc * uv, axis=0, keepdims=True)
            db_ref[hf] = jnp.sum(dc, axis=0, keepdims=True)

    specs = _convglu_specs(t)
    return pl.pallas_call(
        body, name="convglu_bwd",
        grid=(fp // LANES,),
        in_specs=specs + [pl.BlockSpec((t, LANES), lambda j: (0, j))],
        out_specs=specs,
        out_shape=[jax.ShapeDtypeStruct((2, t, fp), BF16), jax.ShapeDtypeStruct((2, CONV_TAPS, fp), F32),
                   jax.ShapeDtypeStruct((2, 1, fp), F32)],
        compiler_params=_cparams(dimension_semantics=("parallel",)),
    )(u, cw, cb, da)


def _local_step(x, target, mods, lb, wts, small, *, tb=512, attn_blk=512):
    t, d = x.shape
    nh = d // HEAD
    nb = wts["a_in"].shape[0]
    vec = lambda *names: [mods[n] for n in names]

    def ffn_fwd(h2, l):
        u = _mm_wblk(h2, wts["up"], F32, f"ffn{l}_up", gb=nb // 2, row_off=l, split=2, tm=512)
        a = _convglu_fwd(u, small[f"conv_w{l}"], small[f"conv_b{l}"])
        f = _mm(a, wts[f"down{l}"], "nn", F32, f"ffn{l}_down", tk=4096)
        return u, a, f

    def ffn_bwd(df, h2, u, a, l):
        da = _mm(df, wts[f"down{l}"], "nt", BF16, f"ffn{l}_down_dx", tn=1536)
        dwd = _mm(a, df, "tn", BF16, f"ffn{l}_down_dw", tm=1536, tk=1024)
        du, dcw, dcb = _convglu_bwd(u, small[f"conv_w{l}"], small[f"conv_b{l}"], da)
        dh2 = _mm_wblk_dx(du, wts["up"], F32, f"ffn{l}_up_dx", k=d, gb=nb // 2, row_off=l, split=2, tm=512)
        dwu = _mm_wblk_dw(h2, du, f"ffn{l}_up_dw", nb=nb, gb=nb // 2, split=2, tk=512)
        return dh2, dwu, dwd, dcw, dcb

    (h_a,) = _row_fwd(_f_mod, [(x, d, 0)], vec("sh1_0", "sc1_0"), [BF16], tb=tb, name="l0_mod1")
    proj_a = _mm_wblk(h_a, wts["a_in"], F32, "a_in", gb=nb // 2)
    ypre, states = _hgrn2_fwd(proj_a, lb, small["a_norm_g"], tb)
    y_a = _mm(ypre, wts["a_out"], "nn", F32, "a_out")
    x1, h2_0 = _row_fwd(_f_res_mod, [(x, d, 0), (y_a, d, 0)], vec("g1_0", "sh2_0", "sc2_0"), [F32, BF16],
                        tb=tb, name="l0_res_mod2")
    u0, a0, f0 = ffn_fwd(h2_0, 0)
    x2, h_kv, h_q = _row_fwd(_f_res_mod2, [(x1, d, 0), (f0, d, 0)],
                             vec("g2_0", "kv_sh", "kv_sc", "sh1_1", "sc1_1"), [F32, BF16, BF16],
                             tb=tb, name="l0_res_kvmod_qmod")
    proj_kv = _mm(h_kv, wts["kv"], "nn", F32, "kv_proj")
    proj_f = _mm(h_kv, wts["kv_f"], "nn", F32, "kv_fproj")
    (k_n,) = _row_fwd(_f_knorm, [(proj_kv, HEAD, 0)], [small["k_norm_g"]], [BF16], nsub=nh, tb=tb, name="k_norm")
    v_b = proj_kv[:, d:].astype(BF16)
    f_logit_t = proj_f[:, :nh].T
    f_bias = small["kv_b_f"].reshape(nh, 1)
    f_t = _fgate_fwd(f_logit_t, f_bias)
    f_col = f_t.T
    f_row = f_t.reshape(nh, t // attn_blk, attn_blk).transpose(1, 0, 2)
    proj_q = _mm_wblk(h_q, wts["b_q"], F32, "b_q", gb=nb)
    (q_n,) = _row_fwd(_f_qnorm, [(proj_q, HEAD, 0)], [small["q_norm_g"]], [BF16], nsub=nh, tb=tb, name="q_norm")
    o_att, lse = _attn_fwd(q_n, k_n, v_b, f_col, f_row, attn_blk)
    (z,) = _row_fwd(_f_outgate, [(o_att, HEAD, 0), (proj_q, HEAD, 1)], [], [BF16], nsub=nh, tb=tb, name="out_gate")
    y_b = _mm(z, wts["b_out"], "nn", F32, "b_out")
    x3, h2_1 = _row_fwd(_f_res_mod, [(x2, d, 0), (y_b, d, 0)], vec("g1_1", "sh2_1", "sc2_1"), [F32, BF16],
                        tb=tb, name="l1_res_mod2")
    u1, a1, f1 = ffn_fwd(h2_1, 1)
    loss, dx4, df1, dg2_1 = _loss_call(x3, f1, mods["g2_1"], target, tb)

    g = {}
    dmods = {"g2_1": dg2_1}
    dh2, g["up1"], g["down1"], g["conv_w1"], g["conv_b1"] = ffn_bwd(df1, h2_1, u1, a1, 1)
    (dx2, dy_b), (dmods["g1_1"], dmods["sh2_1"], dmods["sc2_1"]) = _row_bwd(
        _f_res_mod, [(x2, d, 0), (y_b, d, 0)], vec("g1_1", "sh2_1", "sc2_1"),
        [(dx4, d, 0), (dh2, d, 0)], [F32, BF16], tb=tb, name="l1_res_mod2_bwd")
    dz = _mm(dy_b, wts["b_out"], "nt", F32, "b_out_dx")
    g["b_out"] = _mm(z, dy_b, "tn", BF16, "b_out_dw", tk=1024)
    (do_att, dog), _ = _row_bwd(_f_outgate, [(o_att, HEAD, 0), (proj_q, HEAD, 1)], [], [(dz, HEAD, 0)],
                                [BF16, BF16], nsub=nh, tb=tb, name="out_gate_bwd")
    dq_n, dfc_q, delta = _attn_bwd_dq(q_n, k_n, v_b, f_col, f_row, o_att, do_att, lse, attn_blk)
    dk_n, dv, dfr_k = _attn_bwd_dkv(q_n, k_n, v_b, f_col, f_row, do_att, lse, delta, attn_blk)
    (dpq,), (g["q_norm_g"],) = _row_bwd(_f_qnorm, [(proj_q, HEAD, 0)], [small["q_norm_g"]],
                                        [(dq_n, HEAD, 0)], [BF16], nsub=nh, tb=tb, name="q_norm_bwd")
    dproj_q = jnp.concatenate([dpq, dog], axis=1)
    dh_q = _mm_wblk_dx(dproj_q, wts["b_q"], F32, "b_q_dx", k=d, gb=nb)
    g["b_q"] = _mm_wblk_dw(h_q, dproj_q, "b_q_dw", nb=nb, gb=nb)
    (dpk,), (g["k_norm_g"],) = _row_bwd(_f_knorm, [(proj_kv, HEAD, 0)], [small["k_norm_g"]],
                                        [(dk_n, HEAD, 0)], [BF16], nsub=nh, tb=tb, name="k_norm_bwd")
    dproj_kv = jnp.concatenate([dpk, dv.astype(BF16)], axis=1)
    df_t = dfc_q.T + dfr_k.transpose(1, 0, 2).reshape(nh, t)
    dflogit_t, g["kv_b_f"] = _fgate_bwd(f_logit_t, f_bias, df_t)
    dproj_f = jnp.pad(dflogit_t.T, ((0, 0), (0, LANES - nh))).astype(BF16)
    dh_kv = _mm(dproj_kv, wts["kv"], "nt", F32, "kv_proj_dx") + _mm(dproj_f, wts["kv_f"], "nt", F32, "kv_fproj_dx")
    g["kv"] = _mm(h_kv, dproj_kv, "tn", BF16, "kv_proj_dw", tk=1024)
    g["kv_f"] = _mm(h_kv, dproj_f, "tn", F32, "kv_fproj_dw", tk=1024)
    (dx1, df0), (dmods["g2_0"], dmods["kv_sh"], dmods["kv_sc"], dmods["sh1_1"], dmods["sc1_1"]) = _row_bwd(
        _f_res_mod2, [(x1, d, 0), (f0, d, 0)], vec("g2_0", "kv_sh", "kv_sc", "sh1_1", "sc1_1"),
        [(dx2, d, 0), (dh_kv, d, 0), (dh_q, d, 0)], [F32, BF16], tb=tb, name="l0_res_kvmod_qmod_bwd")
    dh2, g["up0"], g["down0"], g["conv_w0"], g["conv_b0"] = ffn_bwd(df0, h2_0, u0, a0, 0)
    (dx0, dy_a), (dmods["g1_0"], dmods["sh2_0"], dmods["sc2_0"]) = _row_bwd(
        _f_res_mod, [(x, d, 0), (y_a, d, 0)], vec("g1_0", "sh2_0", "sc2_0"),
        [(dx1, d, 0), (dh2, d, 0)], [F32, BF16], tb=tb, name="l0_res_mod2_bwd")
    dypre = _mm(dy_a, wts["a_out"], "nt", BF16, "a_out_dx")
    g["a_out"] = _mm(ypre, dy_a, "tn", BF16, "a_out_dw", tk=1024)
    dpa_q, dpa_f, dpa_i, dpa_g, dlb, g["a_norm_g"] = _hgrn2_bwd(proj_a, lb, small["a_norm_g"], states, dypre, tb)
    dproj_a = jnp.concatenate([dpa_q, dpa_f, dpa_i, dpa_g], axis=1)
    dh_a = _mm_wblk_dx(dproj_a, wts["a_in"], F32, "a_in_dx", k=d, gb=nb, tm=512)
    g["a_in"] = _mm_wblk_dw(h_a, dproj_a, "a_in_dw", nb=nb, gb=nb // 2)
    (grad_x,), (dmods["sh1_0"], dmods["sc1_0"]) = _row_bwd(
        _f_mod, [(x, d, 0)], vec("sh1_0", "sc1_0"), [(dh_a, d, 0)], [F32], tb=tb, name="l0_mod1_bwd",
        add_to=(0, dx0))
    return loss, grad_x, dmods, dlb, g


def _position():
    return lax.axis_index("x"), lax.axis_index("y"), lax.axis_index("c")


def _hbm_specs(n):
    return [pl.BlockSpec(memory_space=pl.ANY)] * n


def _all_gather(arrs, name):
    n = len(arrs)

    def body(*refs):
        x_refs, out_refs = refs[:n], refs[n:2 * n]
        send_sems, recv_sems, local_sems = refs[2 * n:]
        x, y, cc = _position()
        me, sibling = (x, y, cc), (x, y, 1 - cc)
        chips = [(1 - x, y), (x, 1 - y), (1 - x, 1 - y)]

        def copy(a, k, block, to, src=None):
            slot = out_refs[a].at[4 * block[0] + 2 * block[1] + block[2]]
            return pltpu.make_async_remote_copy(
                src_ref=slot if src is None else src, dst_ref=slot,
                send_sem=send_sems.at[7 * a + k], recv_sem=recv_sems.at[7 * a + k],
                device_id=to, device_id_type=_MESH)

        local = [pltpu.make_async_copy(x_refs[a], out_refs[a].at[4 * x + 2 * y + cc], local_sems.at[a])
                 for a in range(n)]
        for cp in local:
            cp.start()
        first = []
        for a in range(n):
            first.append(copy(a, 0, me, sibling, src=x_refs[a]))
            first += [copy(a, 1 + j, me, (*chip, cc), src=x_refs[a]) for j, chip in enumerate(chips)]
        for cp in first:
            cp.start()
        passed = []
        for j, chip in enumerate(chips):
            for a in range(n):
                copy(a, 1 + j, (*chip, cc), me).wait_recv()
                fwd = copy(a, 4 + j, (*chip, cc), sibling)
                fwd.start()
                passed.append(fwd)
        for a in range(n):
            copy(a, 0, sibling, me).wait_recv()
        for j, chip in enumerate(chips):
            for a in range(n):
                copy(a, 4 + j, (*chip, 1 - cc), me).wait_recv()
        for cp in first + passed:
            cp.wait_send()
        for cp in local:
            cp.wait()

    return pl.pallas_call(
        body, name=name,
        out_shape=[jax.ShapeDtypeStruct((NDEV, *a.shape), a.dtype) for a in arrs],
        in_specs=_hbm_specs(n), out_specs=_hbm_specs(n),
        scratch_shapes=[pltpu.SemaphoreType.DMA((7 * n,)), pltpu.SemaphoreType.DMA((7 * n,)),
                        pltpu.SemaphoreType.DMA((n,))],
    )(*arrs)


def _rs_sibling_exchange(gs):
    n = len(gs)

    def body(*refs):
        g_refs, recv_refs = refs[:n], refs[n:2 * n]
        send_sems, recv_sems = refs[2 * n:]
        x, y, cc = _position()
        copies = [pltpu.make_async_remote_copy(
            src_ref=g_refs[a].at[q, 1 - cc], dst_ref=recv_refs[a].at[q], send_sem=send_sems.at[NCHIP * a + q],
            recv_sem=recv_sems.at[NCHIP * a + q], device_id=(x, y, 1 - cc), device_id_type=_MESH)
            for a in range(n) for q in range(NCHIP)]
        for cp in copies:
            cp.start()
        for cp in copies:
            cp.wait()

    return pl.pallas_call(
        body, name="rs_sibling_exchange",
        out_shape=[jax.ShapeDtypeStruct((NCHIP, *g.shape[2:]), g.dtype) for g in gs],
        in_specs=_hbm_specs(n), out_specs=_hbm_specs(n),
        scratch_shapes=[pltpu.SemaphoreType.DMA((NCHIP * n,)), pltpu.SemaphoreType.DMA((NCHIP * n,))],
    )(*gs)


def _rs_chip_exchange(parts):
    n = len(parts)

    def body(*refs):
        p_refs, recv_refs = refs[:n], refs[n:2 * n]
        send_sems, recv_sems, local_sems = refs[2 * n:]
        x, y, cc = _position()
        myq = 2 * x + y
        chips = [(1 - x, y), (x, 1 - y), (1 - x, 1 - y)]

        def copy(a, k, px, py, src_q, dst_q):
            return pltpu.make_async_remote_copy(
                src_ref=p_refs[a].at[src_q], dst_ref=recv_refs[a].at[dst_q], send_sem=send_sems.at[3 * a + k],
                recv_sem=recv_sems.at[3 * a + k], device_id=(px, py, cc), device_id_type=_MESH)

        local = [pltpu.make_async_copy(p_refs[a].at[myq], recv_refs[a].at[myq], local_sems.at[a]) for a in range(n)]
        for cp in local:
            cp.start()
        sends = [copy(a, k, px, py, 2 * px + py, myq) for a in range(n) for k, (px, py) in enumerate(chips)]
        for cp in sends:
            cp.start()
        for a in range(n):
            for k, (px, py) in enumerate(chips):
                copy(a, k, px, py, myq, 2 * px + py).wait_recv()
        for cp in sends:
            cp.wait_send()
        for cp in local:
            cp.wait()

    return pl.pallas_call(
        body, name="rs_chip_exchange",
        out_shape=[jax.ShapeDtypeStruct(p.shape, p.dtype) for p in parts],
        in_specs=_hbm_specs(n), out_specs=_hbm_specs(n),
        scratch_shapes=[pltpu.SemaphoreType.DMA((3 * n,)), pltpu.SemaphoreType.DMA((3 * n,)),
                        pltpu.SemaphoreType.DMA((n,))],
    )(*parts)


def _pair_sum(own, got, name):
    n, r, c = own.shape

    def body(a_ref, b_ref, o_ref):
        o_ref[...] = (a_ref[...].astype(F32) + b_ref[...].astype(F32)).astype(o_ref.dtype)

    spec = pl.BlockSpec((1, r, c), lambda q: (q, 0, 0))
    return pl.pallas_call(body, name=name, grid=(n,), in_specs=[spec, spec], out_specs=spec,
                          out_shape=jax.ShapeDtypeStruct((n, r, c), own.dtype),
                          compiler_params=_cparams(dimension_semantics=("parallel",)))(own, got)


def _slab_sum(slabs, name, tr=None):
    n, r, c = slabs.shape
    tr = r if tr is None else tr

    def body(s_ref, o_ref):
        acc = s_ref[0].astype(F32)
        for q in range(1, n):
            acc = acc + s_ref[q].astype(F32)
        o_ref[...] = acc

    return pl.pallas_call(body, name=name, grid=(r // tr,),
                          in_specs=[pl.BlockSpec((n, tr, c), lambda i: (0, i, 0))],
                          out_specs=pl.BlockSpec((tr, c), lambda i: (i, 0)),
                          out_shape=jax.ShapeDtypeStruct((r, c), F32),
                          compiler_params=_cparams(dimension_semantics=("parallel",)))(slabs)


def _ada_fwd(c_all, ada_w, kv_ada_w, logits):
    rows, d = c_all.shape
    n0, nkv = ada_w.shape[2], kv_ada_w.shape[1]

    def body(c_ref, w_ref, kw_ref, lg_ref, part_ref, cact_ref, lb_ref):
        ca = _silu(c_ref[...])
        cact_ref[...] = ca
        part_ref[:, 0:n0] = _bdot_raw(ca, w_ref[0], _NN)
        part_ref[:, n0:2 * n0] = _bdot_raw(ca, w_ref[1], _NN)
        part_ref[:, 2 * n0:2 * n0 + nkv] = _bdot_raw(ca, kw_ref[...], _NN)
        lb_ref[...] = _sigmoid(lg_ref[0:1, :] - lg_ref[1:2, :])

    vm = pl.BlockSpec(memory_space=pltpu.VMEM)
    return pl.pallas_call(
        body, name="ada_fwd", in_specs=[vm, vm, vm, vm], out_specs=[vm, vm, vm],
        out_shape=[jax.ShapeDtypeStruct((rows, 2 * n0 + nkv), F32), jax.ShapeDtypeStruct((rows, d), F32),
                   jax.ShapeDtypeStruct((1, d), F32)],
        compiler_params=_cparams(),
    )(c_all, ada_w, kv_ada_w, logits)


def _ada_bwd(c_act, dm0, dm1, dkv, lb, dlb):
    rows, d = c_act.shape

    def body(c_ref, d0_ref, d1_ref, dk_ref, lb_ref, dlb_ref, dw_ref, dkw_ref, dlg_ref):
        ca = c_ref[...]
        dw_ref[0] = _bdot_raw(ca, d0_ref[...], _TN)
        dw_ref[1] = _bdot_raw(ca, d1_ref[...], _TN)
        dkw_ref[...] = _bdot_raw(ca, dk_ref[...], _TN)
        lbv = lb_ref[...]
        dl0 = dlb_ref[...] * lbv * (1.0 - lbv)
        dlg_ref[0:1, :] = dl0
        dlg_ref[1:2, :] = -dl0

    vm = pl.BlockSpec(memory_space=pltpu.VMEM)
    return pl.pallas_call(
        body, name="ada_bwd", in_specs=[vm] * 6, out_specs=[vm, vm, vm],
        out_shape=[jax.ShapeDtypeStruct((2, d, dm0.shape[1]), F32), jax.ShapeDtypeStruct((d, dkv.shape[1]), F32),
                   jax.ShapeDtypeStruct((2, d), F32)],
        compiler_params=_cparams(),
    )(c_act, dm0, dm1, dkv, lb, dlb)


def _adamw(w, g, m, v, name, tr=512):
    r, c = w.shape
    tr = _divisor_tile(r, tr, unit=8)
    c1 = 1.0 - ADAM_B1 ** ADAM_STEP
    c2 = 1.0 - ADAM_B2 ** ADAM_STEP

    def body(w_ref, g_ref, m_ref, v_ref, d_ref, mo_ref, vo_ref):
        gv = g_ref[...]
        mn = ADAM_B1 * m_ref[...] + (1.0 - ADAM_B1) * gv
        vn = ADAM_B2 * v_ref[...] + (1.0 - ADAM_B2) * (gv * gv)
        d_ref[...] = -ADAM_LR * ((mn / c1) / (jnp.sqrt(vn / c2) + ADAM_EPS) + ADAM_WD * w_ref[...])
        mo_ref[...] = mn
        vo_ref[...] = vn

    spec = pl.BlockSpec((tr, c), lambda i: (i, 0))
    out = jax.ShapeDtypeStruct((r, c), F32)
    return pl.pallas_call(body, name=name, grid=(r // tr,), in_specs=[spec] * 4, out_specs=[spec] * 3,
                          out_shape=[out, out, out],
                          compiler_params=_cparams(dimension_semantics=("parallel",)))(w, g, m, v)


def _pad_rows(a, rows):
    return jnp.pad(a, ((0, rows - a.shape[0]), (0, 0)))


def _pack_small(parts, lanes=LANES, row_unit=8):
    flat = jnp.concatenate([p.reshape(-1).astype(F32) for p in parts])
    rows = _round_up(-(-flat.shape[0] // lanes), row_unit)
    return jnp.pad(flat, (0, rows * lanes - flat.shape[0])).reshape(rows, lanes)


def _unpack_small(flat, shapes):
    out, off = [], 0
    for s in shapes:
        n = 1
        for k in s:
            n *= k
        out.append(flat[off:off + n].reshape(s))
        off += n
    return out


def _pad_shard_cols(a, n_loc, n_pad):
    lead = a.shape[:-1]
    a = a.reshape(*lead, NDEV, n_loc)
    a = jnp.pad(a, [(0, 0)] * (len(lead) + 1) + [(0, n_pad - n_loc)])
    return a.reshape(*lead, NDEV * n_pad)


def _unpad_shard_cols(a, n_loc, n_pad):
    lead = a.shape[:-1]
    return a.reshape(*lead, NDEV, n_pad)[..., :n_loc].reshape(*lead, NDEV * n_loc)


def kernel(x, c, ada_w, ada_b, a_w_in, a_lb_logits, a_norm_g, a_w_out, kv_ada_w, kv_ada_b, kv_w, kv_b_f, k_norm_g, b_w_q, q_norm_g, b_w_out, ffn_w_up, ffn_conv_w, ffn_conv_b, ffn_w_down, loss_target, m_ada_w, m_ada_b, m_a_w_in, m_a_lb_logits, m_a_norm_g, m_a_w_out, m_kv_ada_w, m_kv_ada_b, m_kv_w, m_kv_b_f, m_k_norm_g, m_b_w_q, m_q_norm_g, m_b_w_out, m_ffn_w_up, m_ffn_conv_w, m_ffn_conv_b, m_ffn_w_down, v_ada_w, v_ada_b, v_a_w_in, v_a_lb_logits, v_a_norm_g, v_a_w_out, v_kv_ada_w, v_kv_ada_b, v_kv_w, v_kv_b_f, v_k_norm_g, v_b_w_q, v_q_norm_g, v_b_w_out, v_ffn_w_up, v_ffn_conv_w, v_ffn_conv_b, v_ffn_w_down):
    t, d = x.shape[1], x.shape[2]
    nh = d // HEAD
    ncw = ffn_w_up.shape[2]
    ncp = _round_up(ncw, LANES)
    two_f = ncw * NDEV
    ff = two_f // 2
    fp = ncp * NDEV // 2
    rd = ffn_w_down.shape[1]
    me = 4 * lax.axis_index("x") + 2 * lax.axis_index("y") + lax.axis_index("c")
    weights = dict(ada_w=ada_w, ada_b=ada_b, a_w_in=a_w_in, a_lb_logits=a_lb_logits, a_norm_g=a_norm_g,
                   a_w_out=a_w_out, kv_ada_w=kv_ada_w, kv_ada_b=kv_ada_b, kv_w=kv_w, kv_b_f=kv_b_f,
                   k_norm_g=k_norm_g, b_w_q=b_w_q, q_norm_g=q_norm_g, b_w_out=b_w_out, ffn_w_up=ffn_w_up,
                   ffn_conv_w=ffn_conv_w, ffn_conv_b=ffn_conv_b, ffn_w_down=ffn_w_down)
    m_in = dict(ada_w=m_ada_w, ada_b=m_ada_b, a_w_in=m_a_w_in, a_lb_logits=m_a_lb_logits, a_norm_g=m_a_norm_g,
                a_w_out=m_a_w_out, kv_ada_w=m_kv_ada_w, kv_ada_b=m_kv_ada_b, kv_w=m_kv_w, kv_b_f=m_kv_b_f,
                k_norm_g=m_k_norm_g, b_w_q=m_b_w_q, q_norm_g=m_q_norm_g, b_w_out=m_b_w_out, ffn_w_up=m_ffn_w_up,
                ffn_conv_w=m_ffn_conv_w, ffn_conv_b=m_ffn_conv_b, ffn_w_down=m_ffn_w_down)
    v_in = dict(ada_w=v_ada_w, ada_b=v_ada_b, a_w_in=v_a_w_in, a_lb_logits=v_a_lb_logits, a_norm_g=v_a_norm_g,
                a_w_out=v_a_w_out, kv_ada_w=v_kv_ada_w, kv_ada_b=v_kv_ada_b, kv_w=v_kv_w, kv_b_f=v_kv_b_f,
                k_norm_g=v_k_norm_g, b_w_q=v_b_w_q, q_norm_g=v_q_norm_g, b_w_out=v_b_w_out, ffn_w_up=v_ffn_w_up,
                ffn_conv_w=v_ffn_conv_w, ffn_conv_b=v_ffn_conv_b, ffn_w_down=v_ffn_w_down)
    order = list(weights)

    up_loc = jnp.pad(ffn_w_up, ((0, 0), (0, 0), (0, ncp - ncw))).astype(BF16).reshape(2 * d, ncp)
    g_ain, g_aout, g_kv, g_bq, g_bout, g_up, g_down = _all_gather(
        [a_w_in[0].astype(BF16), a_w_out[0].astype(BF16), kv_w.astype(BF16), b_w_q[0].astype(BF16),
         b_w_out[0].astype(BF16), up_loc, ffn_w_down.astype(BF16).reshape(2 * rd, d)], "gather_weights")

    pre = _pack_small([c, a_lb_logits, ffn_conv_w])
    (pre_all,) = _all_gather([pre], "gather_small_inputs")
    pre_all = pre_all.reshape(NDEV, -1)
    c_all = pre_all[:, :d]
    logits = pre_all[:, d:d + 2 * HEAD].reshape(NDEV, 2, HEAD).transpose(1, 0, 2).reshape(2, d)
    conv_w_full = pre_all[:, d + 2 * HEAD:d + 2 * HEAD + 2 * CONV_TAPS * ncw]
    conv_w_full = conv_w_full.reshape(NDEV, 2, CONV_TAPS, ncw).transpose(1, 2, 0, 3).reshape(2, CONV_TAPS, two_f)

    part, c_act, lb = _ada_fwd(_pad_rows(c_all, 2 * NDEV), ada_w, kv_ada_w, logits)
    (part_all,) = _all_gather([part[:NDEV]], "gather_adaln")
    mine = lax.dynamic_index_in_dim(part_all, me, axis=1, keepdims=False)
    n0, nkv = ada_w.shape[2], kv_ada_w.shape[1]
    mod_names = ["sh1", "sc1", "g1", "sh2", "sc2", "g2"]
    mods = {}
    for l in range(2):
        row = mine[:, l * n0:(l + 1) * n0].reshape(-1) + ada_b[l]
        for k, nm in enumerate(mod_names):
            mods[f"{nm}_{l}"] = row[k * d:(k + 1) * d].reshape(1, d)
    kvrow = mine[:, 2 * n0:2 * n0 + nkv].reshape(-1) + kv_ada_b
    mods["kv_sh"], mods["kv_sc"] = kvrow[:d].reshape(1, d), kvrow[d:].reshape(1, d)

    kv_full = g_kv.transpose(1, 0, 2).reshape(d, NDEV * kv_w.shape[1])
    wts = {
        "a_in": g_ain, "a_out": g_aout.reshape(d, d), "kv": kv_full[:, :2 * d],
        "kv_f": jnp.pad(kv_full[:, 2 * d:], ((0, 0), (0, LANES - nh))),
        "b_q": g_bq, "b_out": g_bout.reshape(d, d), "up": g_up,
    }
    small = {"a_norm_g": a_norm_g, "k_norm_g": k_norm_g.reshape(1, HEAD), "q_norm_g": q_norm_g, "kv_b_f": kv_b_f}
    down_all = g_down.reshape(NDEV, 2, rd, d)
    for l in range(2):
        dn = down_all[:, l].reshape(NCHIP, ff // NCHIP, d)
        wts[f"down{l}"] = jnp.pad(dn, ((0, 0), (0, ncp - ncw), (0, 0))).reshape(fp, d)
        small[f"conv_w{l}"] = _pad_shard_cols(conv_w_full[l], ncw, ncp).reshape(CONV_TAPS, 2, fp).transpose(1, 0, 2)
        small[f"conv_b{l}"] = _pad_shard_cols(ffn_conv_b[l], ncw, ncp).reshape(2, 1, fp)

    loss_v, grad_x, dmods, dlb, g = _local_step(x[0], loss_target[0], mods, lb, wts, small)
    loss = lax.psum(loss_v[0, 0], ("x", "y", "c"))

    g_kvw = jnp.concatenate([g["kv"], g["kv_f"][:, :nh].astype(BF16)], axis=1)
    g_kvw = g_kvw.reshape(d, NDEV, kv_w.shape[1]).transpose(1, 0, 2)
    g_dn = [g[f"down{l}"].reshape(NCHIP, ncp, d)[:, :ncw].reshape(NDEV, rd, d) for l in range(2)]
    rs_names = ["a_w_in", "a_w_out", "kv_w", "b_w_q", "b_w_out", "up0", "up1", "down0", "down1"]
    rs_in = [g["a_in"], g["a_out"].reshape(NDEV, d // NDEV, d), g_kvw, g["b_q"],
             g["b_out"].reshape(NDEV, d // NDEV, d), g["up0"], g["up1"], g_dn[0], g_dn[1]]
    rs_in = [a.reshape(NCHIP, 2, *a.shape[1:]) for a in rs_in]
    from_sibling = _rs_sibling_exchange(rs_in)
    cc = lax.axis_index("c")
    chip_parts = [_pair_sum(lax.dynamic_index_in_dim(a, cc, axis=1, keepdims=False), b, f"rs_pair_sum_{nm}")
                  for a, b, nm in zip(rs_in, from_sibling, rs_names)]
    from_chips = _rs_chip_exchange(chip_parts)
    g_sum = {nm: _slab_sum(a, f"rs_slab_sum_{nm}") for a, nm in zip(from_chips, rs_names)}

    def conv_w_grad(a):
        return _unpad_shard_cols(a.transpose(1, 0, 2).reshape(CONV_TAPS, 2 * fp), ncw, ncp)

    def conv_b_grad(a):
        return _unpad_shard_cols(a.reshape(2 * fp), ncw, ncp)

    dmod_vec = [dmods[f"{nm}_{l}"] for l in range(2) for nm in mod_names] + [dmods["kv_sh"], dmods["kv_sc"]]
    post = _pack_small(dmod_vec + [dlb, g["a_norm_g"], g["k_norm_g"], g["q_norm_g"],
                                   jnp.pad(g["kv_b_f"].reshape(-1), (0, LANES - nh)),
                                   conv_w_grad(g["conv_w0"]), conv_w_grad(g["conv_w1"]),
                                   conv_b_grad(g["conv_b0"]), conv_b_grad(g["conv_b1"])])
    (post_all,) = _all_gather([post], "gather_small_grads")
    tot = _slab_sum(post_all, "small_grad_sum").reshape(-1)
    nmod = 14 * d
    (t_mod, t_lb, t_ang, t_kng, t_qng, t_bf, t_cw, t_cb) = _unpack_small(
        tot, [(nmod,), (1, d), (1, HEAD), (HEAD,), (1, HEAD), (LANES,), (2, CONV_TAPS, two_f), (2, two_f)])
    dm_all = post_all.reshape(NDEV, -1)[:, :nmod]
    dm0 = lax.dynamic_slice_in_dim(dm_all[:, :6 * d], me * n0, n0, axis=1)
    dm1 = lax.dynamic_slice_in_dim(dm_all[:, 6 * d:12 * d], me * n0, n0, axis=1)
    dkv = lax.dynamic_slice_in_dim(dm_all[:, 12 * d:], me * nkv, nkv, axis=1)
    g_ada_w, g_kv_ada_w, g_logits = _ada_bwd(c_act, _pad_rows(dm0, 2 * NDEV), _pad_rows(dm1, 2 * NDEV),
                                              _pad_rows(dkv, 2 * NDEV), lb, t_lb)

    grads = {
        "ada_w": g_ada_w,
        "ada_b": t_mod[:12 * d].reshape(2, 6 * d),
        "a_w_in": g_sum["a_w_in"].reshape(a_w_in.shape),
        "a_lb_logits": lax.dynamic_slice_in_dim(g_logits, me * HEAD, HEAD, axis=1),
        "a_norm_g": t_ang,
        "a_w_out": g_sum["a_w_out"].reshape(a_w_out.shape),
        "kv_ada_w": g_kv_ada_w,
        "kv_ada_b": t_mod[12 * d:],
        "kv_w": g_sum["kv_w"],
        "kv_b_f": t_bf[:nh],
        "k_norm_g": t_kng,
        "b_w_q": g_sum["b_w_q"].reshape(b_w_q.shape),
        "q_norm_g": t_qng,
        "b_w_out": g_sum["b_w_out"].reshape(b_w_out.shape),
        "ffn_w_up": jnp.stack([g_sum["up0"][:, :ncw], g_sum["up1"][:, :ncw]]),
        "ffn_conv_w": lax.dynamic_slice_in_dim(t_cw, me * ncw, ncw, axis=2),
        "ffn_conv_b": t_cb,
        "ffn_w_down": jnp.stack([g_sum["down0"], g_sum["down1"]]),
    }

    big_adam = ["ada_w", "a_w_in", "a_w_out", "kv_ada_w", "kv_w", "b_w_q", "b_w_out", "ffn_w_up", "ffn_w_down"]
    small_adam = [n for n in order if n not in big_adam]
    delta, new_m, new_v = {}, {}, {}
    for n in big_adam:
        shp = weights[n].shape
        two_d = lambda a: a.reshape(-1, shp[-1])
        dl, mn, vn = _adamw(two_d(weights[n]), two_d(grads[n]), two_d(m_in[n]), two_d(v_in[n]), f"adamw_{n}")
        delta[n], new_m[n], new_v[n] = dl.reshape(shp), mn.reshape(shp), vn.reshape(shp)
    packs = [_pack_small([src[n] for n in small_adam]) for src in (weights, grads, m_in, v_in)]
    outs = _adamw(*packs, "adamw_small", tr=packs[0].shape[0])
    shapes = [weights[n].shape for n in small_adam]
    for dst, o in zip((delta, new_m, new_v), outs):
        for n, a in zip(small_adam, _unpack_small(o.reshape(-1), shapes)):
            dst[n] = a

    return (loss, grad_x.reshape(x.shape), *[grads[n] for n in order], *[delta[n] for n in order],
            *[new_m[n] for n in order], *[new_v[n] for n in order])
```

```python
import functools

import jax
import jax.numpy as jnp
from jax import lax
from jax.experimental import pallas as pl
from jax.experimental.pallas import tpu as pltpu

F32 = jnp.float32
BF16 = jnp.bfloat16

NDEV = 8
NCHIP = 4
HEAD = 128
A_CHUNK = 64
CONV_TAPS = 3
EPS = 1e-6
NEG_INF = -1e30
LANES = 128
VMEM_LIMIT = 48 * 1024 * 1024

ADAM_LR = 0.001
ADAM_B1 = 0.9
ADAM_B2 = 0.999
ADAM_EPS = 1e-08
ADAM_WD = 0.01
ADAM_STEP = 10

_NN = (((1,), (0,)), ((), ()))
_NT = (((1,), (1,)), ((), ()))
_TN = (((0,), (0,)), ((), ()))
_MESH = pl.DeviceIdType.MESH


def _cparams(**kw):
    return pltpu.CompilerParams(vmem_limit_bytes=VMEM_LIMIT, **kw)


def _divisor_tile(n, pref, unit=LANES):
    if n <= pref:
        return n
    best = None
    for t in range(unit, pref + 1, unit):
        if n % t == 0:
            best = t
    assert best is not None, (n, pref)
    return best


def _round_up(n, unit):
    return -(-n // unit) * unit


def _bdot_raw(a, b, dims):
    return lax.dot_general(a.astype(BF16), b.astype(BF16), dims, preferred_element_type=F32)


@jax.custom_vjp
def _dot_nn(a, b):
    return _bdot_raw(a, b, _NN)


@jax.custom_vjp
def _dot_nt(a, b):
    return _bdot_raw(a, b, _NT)


@jax.custom_vjp
def _dot_tn(a, b):
    return _bdot_raw(a, b, _TN)


_dot_nn.defvjp(lambda a, b: (_bdot_raw(a, b, _NN), (a, b)),
               lambda r, g: (_dot_nt(g, r[1]), _dot_tn(r[0], g)))
_dot_nt.defvjp(lambda a, b: (_bdot_raw(a, b, _NT), (a, b)),
               lambda r, g: (_dot_nn(g, r[1]), _dot_tn(g, r[0])))
_dot_tn.defvjp(lambda a, b: (_bdot_raw(a, b, _TN), (a, b)),
               lambda r, g: (_dot_nt(r[1], g), _dot_nn(r[0], g)))


def _f32dot(a, b):
    return lax.dot_general(a, b, _NN, precision=lax.Precision.HIGHEST, preferred_element_type=F32)


def _sigmoid(x):
    return jax.nn.sigmoid(x)


def _silu(x):
    return x * jax.nn.sigmoid(x)


def _rms(x):
    return x * lax.rsqrt(jnp.mean(x * x, axis=-1, keepdims=True) + EPS)


def _modulate(x, sh, sc):
    return _rms(x) * (1.0 + sc) + sh


def _mm_call(a, b, dims, a_spec, b_spec, o_spec, o_shape, grid, acc_tile, name):
    nk = grid[2]

    def body(a_ref, b_ref, o_ref, *acc):
        p = lax.dot_general(a_ref[...].astype(BF16), b_ref[...].astype(BF16), dims,
                            preferred_element_type=F32)
        if nk == 1:
            o_ref[...] = p.astype(o_ref.dtype)
        else:
            kk = pl.program_id(2)

            @pl.when(kk == 0)
            def _():
                acc[0][...] = p

            @pl.when(kk > 0)
            def _():
                acc[0][...] += p

            @pl.when(kk == nk - 1)
            def _():
                o_ref[...] = acc[0][...].astype(o_ref.dtype)

    return pl.pallas_call(
        body, name=name, grid=grid, in_specs=[a_spec, b_spec], out_specs=o_spec, out_shape=o_shape,
        scratch_shapes=[pltpu.VMEM(acc_tile, F32)] if nk > 1 else [],
        compiler_params=_cparams(dimension_semantics=("parallel", "parallel", "arbitrary")),
    )(a, b)


def _mm(a, b, mode, out_dtype, name, tm=1024, tn=1024, tk=2048):
    if mode == "nn":
        (m, k), (k2, n) = a.shape, b.shape
    elif mode == "nt":
        (m, k), (n, k2) = a.shape, b.shape
    else:
        (k, m), (k2, n) = a.shape, b.shape
    assert k == k2, (a.shape, b.shape, mode)
    tm, tn, tk = _divisor_tile(m, tm), _divisor_tile(n, tn), _divisor_tile(k, tk)
    if mode == "tn":
        a_spec = pl.BlockSpec((tk, tm), lambda i, j, kk: (kk, i))
    else:
        a_spec = pl.BlockSpec((tm, tk), lambda i, j, kk: (i, kk))
    if mode == "nt":
        b_spec = pl.BlockSpec((tn, tk), lambda i, j, kk: (j, kk))
    else:
        b_spec = pl.BlockSpec((tk, tn), lambda i, j, kk: (kk, j))
    return _mm_call(a, b, {"nn": _NN, "nt": _NT, "tn": _TN}[mode], a_spec, b_spec,
                    pl.BlockSpec((tm, tn), lambda i, j, kk: (i, j)), jax.ShapeDtypeStruct((m, n), out_dtype),
                    (m // tm, n // tn, k // tk), (tm, tn), name)


def _wblk_act_spec(rows, gb, nl, split, nb, row_axis, blk_axis):
    if split == 1:
        return pl.BlockSpec((rows, gb * nl), lambda *g: (g[row_axis], g[blk_axis]))
    groups = nb // split // gb
    return pl.BlockSpec((None, rows, gb * nl),
                        lambda *g: (g[blk_axis] // groups, g[row_axis], g[blk_axis] % groups))


def _mm_wblk(a, wb, out_dtype, name, *, gb, row_off=0, split=1, tm=1024):
    m, k = a.shape
    nb, _, nl = wb.shape
    assert (nb // split) % gb == 0
    tm = _divisor_tile(m, tm)

    def body(a_ref, b_ref, o_ref):
        av = a_ref[...].astype(BF16)
        for s in range(gb):
            o_ref[:, s * nl:(s + 1) * nl] = lax.dot_general(
                av, b_ref[s].astype(BF16), _NN, preferred_element_type=F32).astype(o_ref.dtype)

    o_shape = (m, nb * nl) if split == 1 else (split, m, nb // split * nl)
    return pl.pallas_call(
        body, name=name, grid=(m // tm, nb // gb),
        in_specs=[pl.BlockSpec((tm, k), lambda i, j: (i, 0)),
                  pl.BlockSpec((gb, k, nl), lambda i, j: (j, row_off, 0))],
        out_specs=_wblk_act_spec(tm, gb, nl, split, nb, 0, 1),
        out_shape=jax.ShapeDtypeStruct(o_shape, out_dtype),
        compiler_params=_cparams(dimension_semantics=("parallel", "parallel")),
    )(a, wb)


def _mm_wblk_dx(dy, wb, out_dtype, name, *, k, gb, row_off=0, split=1, tm=1024):
    nb, _, nl = wb.shape
    assert (nb // split) % gb == 0
    m = dy.shape[-2]
    tm = _divisor_tile(m, tm)
    nk = nb // gb

    def body(a_ref, b_ref, o_ref, *acc):
        p = None
        for s in range(gb):
            q = lax.dot_general(a_ref[:, s * nl:(s + 1) * nl].astype(BF16), b_ref[s].astype(BF16), _NT,
                                preferred_element_type=F32)
            p = q if p is None else p + q
        if nk == 1:
            o_ref[...] = p.astype(o_ref.dtype)
        else:
            kk = pl.program_id(1)

            @pl.when(kk == 0)
            def _():
                acc[0][...] = p

            @pl.when(kk > 0)
            def _():
                acc[0][...] += p

            @pl.when(kk == nk - 1)
            def _():
                o_ref[...] = acc[0][...].astype(o_ref.dtype)

    return pl.pallas_call(
        body, name=name, grid=(m // tm, nk),
        in_specs=[_wblk_act_spec(tm, gb, nl, split, nb, 0, 1),
                  pl.BlockSpec((gb, k, nl), lambda i, kk: (kk, row_off, 0))],
        out_specs=pl.BlockSpec((tm, k), lambda i, kk: (i, 0)),
        out_shape=jax.ShapeDtypeStruct((m, k), out_dtype),
        scratch_shapes=[pltpu.VMEM((tm, k), F32)] if nk > 1 else [],
        compiler_params=_cparams(dimension_semantics=("parallel", "arbitrary")),
    )(dy, wb)


def _mm_wblk_dw(x, dy, name, *, nb, gb, split=1, tk=1024):
    t, k = x.shape
    assert (nb // split) % gb == 0
    nl = dy.shape[-1] * split // nb
    tk = _divisor_tile(t, tk)
    nk = t // tk

    def body(a_ref, b_ref, o_ref, acc):
        kk = pl.program_id(1)
        av = a_ref[...].astype(BF16)
        for s in range(gb):
            p = lax.dot_general(av, b_ref[:, s * nl:(s + 1) * nl].astype(BF16), _TN, preferred_element_type=F32)

            @pl.when(kk == 0)
            def _():
                acc[s] = p

            @pl.when(kk > 0)
            def _():
                acc[s] += p

        @pl.when(kk == nk - 1)
        def _():
            o_ref[...] = acc[...].astype(o_ref.dtype)

    return pl.pallas_call(
        body, name=name, grid=(nb // gb, nk),
        in_specs=[pl.BlockSpec((tk, k), lambda j, kk: (kk, 0)), _wblk_act_spec(tk, gb, nl, split, nb, 1, 0)],
        out_specs=pl.BlockSpec((gb, k, nl), lambda j, kk: (j, 0, 0)),
        out_shape=jax.ShapeDtypeStruct((nb, k, nl), BF16),
        scratch_shapes=[pltpu.VMEM((gb, k, nl), F32)],
        compiler_params=_cparams(dimension_semantics=("parallel", "arbitrary")),
    )(x, dy)


def _row_specs(rows, tb, nsub):
    return [pl.BlockSpec((tb, nsub * cw), functools.partial(lambda i, off: (i, off), off=off))
            for (_, cw, off) in rows]


def _vec_specs(params):
    return [pl.BlockSpec(p.shape, lambda i: (0, 0)) for p in params]


def _row_fwd(f, rows, params, out_dtypes, *, nsub=1, tb, name):
    t = rows[0][0].shape[0]
    tb = min(tb, t)
    n_r, n_p = len(rows), len(params)
    blk = [jax.ShapeDtypeStruct((tb, cw), F32) for (_, cw, _) in rows]
    blk += [jax.ShapeDtypeStruct(p.shape, F32) for p in params]
    out_avals = jax.eval_shape(f, *blk)

    def body(*refs):
        pv = [r[...] for r in refs[n_r:n_r + n_p]]
        for s in range(nsub):
            vals = [r[:, s * cw:(s + 1) * cw].astype(F32) for r, (_, cw, _) in zip(refs[:n_r], rows)]
            outs = f(*vals, *pv)
            for o_ref, o in zip(refs[n_r + n_p:], outs):
                w = o.shape[1]
                o_ref[:, s * w:(s + 1) * w] = o.astype(o_ref.dtype)

    return pl.pallas_call(
        body, name=name,
        grid=(t // tb,),
        in_specs=_row_specs(rows, tb, nsub) + _vec_specs(params),
        out_specs=[pl.BlockSpec((tb, nsub * av.shape[1]), lambda i: (i, 0)) for av in out_avals],
        out_shape=[jax.ShapeDtypeStruct((t, nsub * av.shape[1]), dt) for av, dt in zip(out_avals, out_dtypes)],
        compiler_params=_cparams(dimension_semantics=("parallel",)),
    )(*[r[0] for r in rows], *params)


def _row_bwd(f, rows, params, cots, row_grad_dtypes, *, nsub=1, tb, name, add_to=None):
    t = rows[0][0].shape[0]
    tb = min(tb, t)
    n_r, n_p, n_c = len(rows), len(params), len(cots)
    want = [j for j in range(n_r) if row_grad_dtypes[j] is not None]
    extra = [] if add_to is None else [(add_to[1], rows[add_to[0]][1], 0)]

    def body(*refs):
        i = pl.program_id(0)
        r_in, p_in = refs[:n_r], refs[n_r:n_r + n_p]
        c_in = refs[n_r + n_p:n_r + n_p + n_c]
        e_in = refs[n_r + n_p + n_c:n_r + n_p + n_c + len(extra)]
        outs = refs[n_r + n_p + n_c + len(extra):]
        pv = [r[...] for r in p_in]
        psum = [None] * n_p
        for s in range(nsub):
            vals = [r[:, s * cw:(s + 1) * cw].astype(F32) for r, (_, cw, _) in zip(r_in, rows)]
            cvals = tuple(r[:, s * cw:(s + 1) * cw].astype(F32) for r, (_, cw, _) in zip(c_in, cots))
            _, vjp_fn = jax.vjp(f, *vals, *pv)
            grads = vjp_fn(cvals)
            for o_ref, jr in zip(outs[:len(want)], want):
                cw = rows[jr][1]
                gr = grads[jr]
                if add_to is not None and jr == add_to[0]:
                    gr = gr + e_in[0][:, s * cw:(s + 1) * cw]
                o_ref[:, s * cw:(s + 1) * cw] = gr.astype(o_ref.dtype)
            for jp in range(n_p):
                psum[jp] = grads[n_r + jp] if psum[jp] is None else psum[jp] + grads[n_r + jp]
        for o_ref, g in zip(outs[len(want):], psum):
            @pl.when(i == 0)
            def _():
                o_ref[...] = g

            @pl.when(i > 0)
            def _():
                o_ref[...] += g

    out_specs = [pl.BlockSpec((tb, nsub * rows[jr][1]), lambda i: (i, 0)) for jr in want]
    out_shape = [jax.ShapeDtypeStruct((t, nsub * rows[jr][1]), row_grad_dtypes[jr]) for jr in want]
    out_specs += _vec_specs(params)
    out_shape += [jax.ShapeDtypeStruct(p.shape, F32) for p in params]
    res = pl.pallas_call(
        body, name=name,
        grid=(t // tb,),
        in_specs=_row_specs(rows, tb, nsub) + _vec_specs(params) + _row_specs(cots, tb, nsub)
        + _row_specs(extra, tb, nsub),
        out_specs=out_specs, out_shape=out_shape,
        compiler_params=_cparams(dimension_semantics=("arbitrary",)),
    )(*[r[0] for r in rows], *params, *[c[0] for c in cots], *[e[0] for e in extra])
    return res[:len(want)], res[len(want):]


def _f_mod(x, sh, sc):
    return (_modulate(x, sh, sc),)


def _f_res_mod(x, y, g, sh, sc):
    x1 = x + g * y
    return x1, _modulate(x1, sh, sc)


def _f_res_mod2(x, y, g, sh_a, sc_a, sh_b, sc_b):
    x1 = x + g * y
    return x1, _modulate(x1, sh_a, sc_a), _modulate(x1, sh_b, sc_b)


def _f_qnorm(p, g):
    return (_rms(p) * g * (HEAD ** -0.5),)


def _f_knorm(p, g):
    return (_rms(p) * g,)


def _f_outgate(o, og):
    return (o * _sigmoid(og),)


def _loss_call(x3, f, g2, target, tb):
    t, d = x3.shape
    tb = min(tb, t)

    def body(x_ref, f_ref, g_ref, t_ref, loss_ref, dx_ref, df_ref, dg_ref):
        i = pl.program_id(0)
        fv = f_ref[...]
        g = g_ref[...]
        e = x_ref[...] + g * fv - t_ref[...]
        dx = e * (1.0 / d)
        part = 0.5 * jnp.sum(jnp.sum(e * dx, axis=1, keepdims=True), axis=0, keepdims=True)
        dx_ref[...] = dx
        df_ref[...] = (g * dx).astype(df_ref.dtype)
        dg = jnp.sum(dx * fv, axis=0, keepdims=True)

        @pl.when(i == 0)
        def _():
            loss_ref[...] = jnp.broadcast_to(part, loss_ref.shape)
            dg_ref[...] = dg

        @pl.when(i > 0)
        def _():
            loss_ref[...] += jnp.broadcast_to(part, loss_ref.shape)
            dg_ref[...] += dg

    row = pl.BlockSpec((tb, d), lambda i: (i, 0))
    vec = pl.BlockSpec((1, d), lambda i: (0, 0))
    return pl.pallas_call(
        body, name="loss_head",
        grid=(t // tb,),
        in_specs=[row, row, vec, row],
        out_specs=[pl.BlockSpec((1, LANES), lambda i: (0, 0)), row, row, vec],
        out_shape=[jax.ShapeDtypeStruct((1, LANES), F32), jax.ShapeDtypeStruct((t, d), F32),
                   jax.ShapeDtypeStruct((t, d), BF16), jax.ShapeDtypeStruct((1, d), F32)],
        compiler_params=_cparams(dimension_semantics=("arbitrary",)),
    )(x3, f, g2, target)


def _hg_consts(tb):
    c = A_CHUNK
    r = lax.broadcasted_iota(jnp.int32, (c, c), 0)
    s = lax.broadcasted_iota(jnp.int32, (c, c), 1)
    br = lax.broadcasted_iota(jnp.int32, (tb, tb), 0)
    bs = lax.broadcasted_iota(jnp.int32, (tb, tb), 1)
    shift = c.bit_length() - 1
    same_chunk = jnp.right_shift(br, shift) == jnp.right_shift(bs, shift)
    return (s <= r).astype(F32), (r <= s).astype(F32), jnp.logical_and(same_chunk, bs <= br)


def _chunk_apply(mat, x):
    c = mat.shape[0]
    return jnp.concatenate([_f32dot(mat, x[i * c:(i + 1) * c]) for i in range(x.shape[0] // c)], axis=0)


@jax.custom_vjp
def _chunk_cumsum(x, tri, tri_t):
    return _chunk_apply(tri, x)


_chunk_cumsum.defvjp(lambda x, tri, tri_t: (_chunk_apply(tri, x), (tri, tri_t)),
                     lambda r, g: (_chunk_apply(r[1], g), jnp.zeros_like(r[0]), jnp.zeros_like(r[1])))


def _per_chunk(a, b, dims):
    return jnp.stack([_bdot_raw(a[i], b[i], dims) for i in range(a.shape[0])])


@jax.custom_vjp
def _chunk_tn(a, b):
    return _per_chunk(a, b, _TN)


@jax.custom_vjp
def _chunk_nt(a, b):
    return _per_chunk(a, b, _NT)


@jax.custom_vjp
def _chunk_nn(a, b):
    return _per_chunk(a, b, _NN)


_chunk_tn.defvjp(lambda a, b: (_per_chunk(a, b, _TN), (a, b)),
                 lambda r, g: (_chunk_nt(r[1], g), _chunk_nn(r[0], g)))
_chunk_nt.defvjp(lambda a, b: (_per_chunk(a, b, _NT), (a, b)),
                 lambda r, g: (_chunk_nn(g, r[1]), _chunk_tn(g, r[0])))
_chunk_nn.defvjp(lambda a, b: (_per_chunk(a, b, _NN), (a, b)),
                 lambda r, g: (_chunk_nt(g, r[1]), _chunk_tn(r[0], g)))


def _scan_states(decay, m, st):
    sts = []
    for i in range(m.shape[0]):
        sts.append(st)
        st = st * decay[i] + m[i]
    return jnp.stack(sts), st


@jax.custom_vjp
def _state_scan(decay, m, st):
    return _scan_states(decay, m, st)


def _state_scan_fwd(decay, m, st):
    sts, st_out = _scan_states(decay, m, st)
    return (sts, st_out), (decay, sts)


def _state_scan_bwd(res, cts):
    decay, sts = res
    d_sts, g = cts
    d_decay, d_m = [], []
    for i in range(sts.shape[0] - 1, -1, -1):
        d_m.append(g)
        d_decay.append(jnp.sum(g * sts[i], axis=0, keepdims=True))
        g = g * decay[i] + d_sts[i]
    return jnp.stack(d_decay[::-1]), jnp.stack(d_m[::-1]), g


_state_scan.defvjp(_state_scan_fwd, _state_scan_bwd)


def _hg_block(qp, fp, ip, gp, lb, ng, st, tri, tri_t, bd_causal):
    tb = qp.shape[0]
    c = A_CHUNK
    n = tb // c
    q = _silu(qp)
    fg = lb + (1.0 - lb) * _sigmoid(fp)
    logf = jnp.log(fg)
    k = 1.0 - fg
    b3 = _chunk_cumsum(logf, tri, tri_t).reshape(n, c, HEAD)
    pos = lax.broadcasted_iota(jnp.int32, (1, c, 1), 1)
    b_mid = lax.stop_gradient(jnp.sum(jnp.where(pos == c // 2, b3, 0.0), axis=1, keepdims=True))
    b_last = jnp.sum(jnp.where(pos == c - 1, b3, 0.0), axis=1, keepdims=True)
    q3, k3, v3 = q.reshape(n, c, HEAD), k.reshape(n, c, HEAD), ip.reshape(n, c, HEAD)
    scores = _dot_nt((q3 * jnp.exp(b3 - b_mid)).reshape(tb, HEAD), (k3 * jnp.exp(b_mid - b3)).reshape(tb, HEAD))
    o_intra = _dot_nn(jnp.where(bd_causal, scores, 0.0), ip)
    states, st_new = _state_scan(jnp.exp(b_last), _chunk_tn(v3, k3 * jnp.exp(b_last - b3)), st)
    o = o_intra + _chunk_nt(q3 * jnp.exp(b3), states).reshape(tb, HEAD)
    y = _rms(o) * ng * _silu(gp)
    return y, st_new


def _hg_specs(tb, nh, rev_nb=None):
    def row(off):
        if rev_nb is None:
            return pl.BlockSpec((tb, HEAD), functools.partial(lambda h, i, off: (i, off + h), off=off))
        return pl.BlockSpec((tb, HEAD), functools.partial(lambda h, i, off: (rev_nb - 1 - i, off + h), off=off))
    return [row(0), row(nh), row(2 * nh), row(3 * nh),
            pl.BlockSpec((1, HEAD), lambda h, i: (0, h)), pl.BlockSpec((1, HEAD), lambda h, i: (0, 0))]


def _hgrn2_fwd(proj, lb, ng, tb):
    t = proj.shape[0]
    nh = proj.shape[1] // (4 * HEAD)
    tb = min(tb, t)
    nb = t // tb

    def body(q_ref, f_ref, i_ref, g_ref, lb_ref, ng_ref, y_ref, s_ref, st_ref):
        i = pl.program_id(1)

        @pl.when(i == 0)
        def _():
            st_ref[...] = jnp.zeros_like(st_ref)

        st = st_ref[...]
        s_ref[0, 0] = st
        y, st_new = _hg_block(q_ref[...], f_ref[...], i_ref[...], g_ref[...], lb_ref[...], ng_ref[...], st,
                              *_hg_consts(tb))
        y_ref[...] = y.astype(y_ref.dtype)
        st_ref[...] = st_new

    return pl.pallas_call(
        body, name="hgrn2_fwd",
        grid=(nh, nb),
        in_specs=_hg_specs(tb, nh),
        out_specs=[pl.BlockSpec((tb, HEAD), lambda h, i: (i, h)),
                   pl.BlockSpec((1, 1, HEAD, HEAD), lambda h, i: (h, i, 0, 0))],
        out_shape=[jax.ShapeDtypeStruct((t, nh * HEAD), BF16),
                   jax.ShapeDtypeStruct((nh, nb, HEAD, HEAD), F32)],
        scratch_shapes=[pltpu.VMEM((HEAD, HEAD), F32)],
        compiler_params=_cparams(dimension_semantics=("parallel", "arbitrary")),
    )(proj, proj, proj, proj, lb, ng)


def _hgrn2_bwd(proj, lb, ng, states, dy, tb):
    t = proj.shape[0]
    nh = proj.shape[1] // (4 * HEAD)
    tb = min(tb, t)
    nb = t // tb

    def body(q_ref, f_ref, i_ref, g_ref, lb_ref, ng_ref, s_ref, dy_ref,
             dq_ref, df_ref, di_ref, dg_ref, dlb_ref, dng_ref, dst_ref):
        h, i = pl.program_id(0), pl.program_id(1)
        consts = _hg_consts(tb)

        @pl.when(i == 0)
        def _():
            dst_ref[...] = jnp.zeros_like(dst_ref)
            dlb_ref[...] = jnp.zeros_like(dlb_ref)

        @pl.when(jnp.logical_and(i == 0, h == 0))
        def _():
            dng_ref[...] = jnp.zeros_like(dng_ref)

        def fn(qp, fp, ip, gp, lbx, ngx, stx):
            return _hg_block(qp, fp, ip, gp, lbx, ngx, stx, *consts)

        _, vjp_fn = jax.vjp(fn, q_ref[...], f_ref[...], i_ref[...], g_ref[...], lb_ref[...], ng_ref[...],
                            s_ref[0, 0])
        gq, gf, gi, gg, glb, gng, dst = vjp_fn((dy_ref[...].astype(F32), dst_ref[...]))
        dq_ref[...] = gq.astype(dq_ref.dtype)
        df_ref[...] = gf.astype(df_ref.dtype)
        di_ref[...] = gi.astype(di_ref.dtype)
        dg_ref[...] = gg.astype(dg_ref.dtype)
        dst_ref[...] = dst
        dlb_ref[...] += glb
        dng_ref[...] += gng

    rev = lambda h, i: (nb - 1 - i, h)
    slab = jax.ShapeDtypeStruct((t, nh * HEAD), BF16)
    return pl.pallas_call(
        body, name="hgrn2_bwd",
        grid=(nh, nb),
        in_specs=_hg_specs(tb, nh, rev_nb=nb) + [
            pl.BlockSpec((1, 1, HEAD, HEAD), lambda h, i: (h, nb - 1 - i, 0, 0)),
            pl.BlockSpec((tb, HEAD), rev)],
        out_specs=[pl.BlockSpec((tb, HEAD), rev)] * 4 + [
            pl.BlockSpec((1, HEAD), lambda h, i: (0, h)), pl.BlockSpec((1, HEAD), lambda h, i: (0, 0))],
        out_shape=[slab, slab, slab, slab,
                   jax.ShapeDtypeStruct((1, nh * HEAD), F32), jax.ShapeDtypeStruct((1, HEAD), F32)],
        scratch_shapes=[pltpu.VMEM((HEAD, HEAD), F32)],
        compiler_params=_cparams(dimension_semantics=("arbitrary", "arbitrary")),
    )(proj, proj, proj, proj, lb, ng, states, dy)


def _fgate_consts(cb):
    r = lax.broadcasted_iota(jnp.int32, (cb, cb), 0)
    s = lax.broadcasted_iota(jnp.int32, (cb, cb), 1)
    return (r <= s).astype(F32), (r >= s).astype(F32)


def _fgate_fwd(xt, bias, cb=512):
    nh, t = xt.shape
    cb = min(cb, t)

    def body(x_ref, b_ref, o_ref):
        upper, _ = _fgate_consts(cb)
        carry = jnp.zeros((nh, 1), F32)
        for blk in range(t // cb):
            z = x_ref[:, blk * cb:(blk + 1) * cb] + b_ref[...]
            logf = jnp.minimum(z, 0.0) - jnp.log(1.0 + jnp.exp(-jnp.abs(z)))
            cs = _f32dot(logf, upper) + carry
            o_ref[:, blk * cb:(blk + 1) * cb] = cs
            carry = cs[:, cb - 1:cb]

    vm = pl.BlockSpec(memory_space=pltpu.VMEM)
    return pl.pallas_call(
        body, name="fgate_fwd", in_specs=[vm, vm], out_specs=vm,
        out_shape=jax.ShapeDtypeStruct((nh, t), F32), compiler_params=_cparams(),
    )(xt, bias)


def _fgate_bwd(xt, bias, dft, cb=512):
    nh, t = xt.shape
    cb = min(cb, t)
    nblk = t // cb

    def body(x_ref, b_ref, d_ref, dx_ref, db_ref):
        _, lower = _fgate_consts(cb)
        carry = jnp.zeros((nh, 1), F32)
        db = jnp.zeros((nh, 1), F32)
        for blk in range(nblk - 1, -1, -1):
            sl = slice(blk * cb, (blk + 1) * cb)
            dlogf = _f32dot(d_ref[:, sl], lower) + carry
            carry = dlogf[:, 0:1]
            z = x_ref[:, sl] + b_ref[...]
            dz = dlogf * (1.0 - _sigmoid(z))
            dx_ref[:, sl] = dz
            db = db + jnp.sum(dz, axis=1, keepdims=True)
        db_ref[...] = db

    vm = pl.BlockSpec(memory_space=pltpu.VMEM)
    return pl.pallas_call(
        body, name="fgate_bwd", in_specs=[vm, vm, vm], out_specs=[vm, vm],
        out_shape=[jax.ShapeDtypeStruct((nh, t), F32), jax.ShapeDtypeStruct((nh, 1), F32)],
        compiler_params=_cparams(),
    )(xt, bias, dft)


ATTN_ROWS = 32


def _row_chunks(blk, fn):
    for r in range(blk // ATTN_ROWS):
        fn(pl.ds(r * ATTN_ROWS, ATTN_ROWS), r * ATTN_ROWS)


def _chunk_causal(first_row, blk):
    rows = first_row + lax.broadcasted_iota(jnp.int32, (ATTN_ROWS, blk), 0)
    return lax.broadcasted_iota(jnp.int32, (ATTN_ROWS, blk), 1) <= rows


def _attn_fwd(q, k, v, f_col, f_row, blk):
    t, width = q.shape
    nh = width // HEAD
    nq = t // blk

    def body(q_ref, k_ref, v_ref, fc_ref, fr_ref, o_ref, lse_ref):
        i = pl.program_id(0)
        tri = (lax.broadcasted_iota(jnp.int32, (blk, blk), 1) <= lax.broadcasted_iota(jnp.int32, (blk, blk), 0))
        for h in range(nh):
            cs = slice(h * HEAD, (h + 1) * HEAD)
            qh = q_ref[:, cs]
            fq = fc_ref[:, h:h + 1]

            def tile(j, carry, masked):
                m, l, acc = carry
                rs = pl.ds(pl.multiple_of(j * blk, blk), blk)
                s = _bdot_raw(qh, k_ref[rs, cs], _NT) + (fq - fr_ref[j, h:h + 1, :])
                if masked:
                    s = jnp.where(tri, s, NEG_INF)
                m_new = jnp.maximum(m, jnp.max(s, axis=1, keepdims=True))
                p = jnp.exp(s - m_new)
                alpha = jnp.exp(m - m_new)
                l_new = alpha * l + jnp.sum(p, axis=1, keepdims=True)
                acc_new = alpha * acc + _bdot_raw(p, v_ref[rs, cs], _NN)
                return m_new, l_new, acc_new

            init = (jnp.full((blk, 1), NEG_INF, F32), jnp.zeros((blk, 1), F32), jnp.zeros((blk, HEAD), F32))
            carry = lax.fori_loop(0, i, lambda j, c: tile(j, c, False), init)
            m, l, acc = tile(i, carry, True)
            o_ref[:, cs] = acc / l
            lse_ref[:, h:h + 1] = m + jnp.log(l)

    vm = pl.BlockSpec(memory_space=pltpu.VMEM)
    return pl.pallas_call(
        body, name="fox_attn_fwd",
        grid=(nq,),
        in_specs=[pl.BlockSpec((blk, width), lambda i: (i, 0)), vm, vm,
                  pl.BlockSpec((blk, nh), lambda i: (i, 0)), vm],
        out_specs=[pl.BlockSpec((blk, width), lambda i: (i, 0)), pl.BlockSpec((blk, nh), lambda i: (i, 0))],
        out_shape=[jax.ShapeDtypeStruct((t, width), F32), jax.ShapeDtypeStruct((t, nh), F32)],
        compiler_params=_cparams(dimension_semantics=("parallel",)),
    )(q, k, v, f_col, f_row)


def _attn_bwd_dq(q, k, v, f_col, f_row, o, do, lse, blk):
    t, width = q.shape
    nh = width // HEAD
    nq = t // blk

    def body(q_ref, k_ref, v_ref, fc_ref, fr_ref, o_ref, do_ref, lse_ref, dq_ref, dfc_ref, dl_ref):
        i = pl.program_id(0)
        tri = (lax.broadcasted_iota(jnp.int32, (blk, blk), 1) <= lax.broadcasted_iota(jnp.int32, (blk, blk), 0))
        for h in range(nh):
            cs = slice(h * HEAD, (h + 1) * HEAD)
            qh = q_ref[:, cs]
            doh = do_ref[:, cs]
            bias = fc_ref[:, h:h + 1] - lse_ref[:, h:h + 1]
            delta = jnp.sum(doh.astype(F32) * o_ref[:, cs], axis=1, keepdims=True)

            def tile(j, carry, masked):
                dq, dfq = carry
                rs = pl.ds(pl.multiple_of(j * blk, blk), blk)
                kj = k_ref[rs, cs]
                p = jnp.exp(_bdot_raw(qh, kj, _NT) + (bias - fr_ref[j, h:h + 1, :]))
                if masked:
                    p = jnp.where(tri, p, 0.0)
                ds = p * (_bdot_raw(doh, v_ref[rs, cs], _NT) - delta)
                return dq + _bdot_raw(ds, kj, _NN), dfq + jnp.sum(ds, axis=1, keepdims=True)

            carry = lax.fori_loop(0, i, lambda j, c: tile(j, c, False),
                                  (jnp.zeros((blk, HEAD), F32), jnp.zeros((blk, 1), F32)))
            dq, dfq = tile(i, carry, True)
            dq_ref[:, cs] = dq
            dfc_ref[:, h:h + 1] = dfq
            dl_ref[:, h:h + 1] = delta

    vm = pl.BlockSpec(memory_space=pltpu.VMEM)
    wide = pl.BlockSpec((blk, width), lambda i: (i, 0))
    thin = pl.BlockSpec((blk, nh), lambda i: (i, 0))
    return pl.pallas_call(
        body, name="fox_attn_bwd_dq",
        grid=(nq,),
        in_specs=[wide, vm, vm, thin, vm, wide, wide, thin],
        out_specs=[wide, thin, thin],
        out_shape=[jax.ShapeDtypeStruct((t, width), F32), jax.ShapeDtypeStruct((t, nh), F32),
                   jax.ShapeDtypeStruct((t, nh), F32)],
        compiler_params=_cparams(dimension_semantics=("parallel",)),
    )(q, k, v, f_col, f_row, o, do, lse)


def _attn_bwd_dkv(q, k, v, f_col, f_row, do, lse, delta, blk):
    t, width = q.shape
    nh = width // HEAD
    nq = t // blk

    def body(q_ref, k_ref, v_ref, fc_ref, fr_ref, do_ref, lse_ref, dl_ref, dk_ref, dv_ref, dfr_ref,
             s_ref, dp_ref, p_ref, ds_ref, dfs_ref, dk_acc, dv_acc):
        j = pl.program_id(0)
        for h in range(nh):
            cs = slice(h * HEAD, (h + 1) * HEAD)
            fs = fr_ref[0, h:h + 1, :]
            dfs_ref[...] = jnp.zeros_like(dfs_ref)
            dk_acc[...] = jnp.zeros_like(dk_acc)
            dv_acc[...] = jnp.zeros_like(dv_acc)

            def tile(i, masked):
                base = pl.multiple_of(i * blk, blk)
                rs = pl.ds(base, blk)
                s_ref[...] = _bdot_raw(q_ref[rs, cs], k_ref[:, cs], _NT)
                dp_ref[...] = _bdot_raw(do_ref[rs, cs], v_ref[:, cs], _NT)

                def rows(rr, first_row):
                    gr = pl.ds(pl.multiple_of(base + first_row, ATTN_ROWS), ATTN_ROWS)
                    bias = fc_ref[gr, h:h + 1] - lse_ref[gr, h:h + 1]
                    p = jnp.exp(s_ref[rr, :] + (bias - fs))
                    if masked:
                        p = jnp.where(_chunk_causal(first_row, blk), p, 0.0)
                    ds = p * (dp_ref[rr, :] - dl_ref[gr, h:h + 1])
                    dfs_ref[...] -= jnp.sum(ds, axis=0, keepdims=True)
                    p_ref[rr, :] = p.astype(p_ref.dtype)
                    ds_ref[rr, :] = ds.astype(ds_ref.dtype)

                _row_chunks(blk, rows)
                dv_acc[...] += _bdot_raw(p_ref[...], do_ref[rs, cs], _TN)
                dk_acc[...] += _bdot_raw(ds_ref[...], q_ref[rs, cs], _TN)

            def off_diagonal(i, carry):
                tile(i, False)
                return carry

            tile(j, True)
            lax.fori_loop(j + 1, nq, off_diagonal, 0)
            dk_ref[:, cs] = dk_acc[...]
            dv_ref[:, cs] = dv_acc[...]
            dfr_ref[0, h:h + 1, :] = dfs_ref[...]

    vm = pl.BlockSpec(memory_space=pltpu.VMEM)
    wide = pl.BlockSpec((blk, width), lambda j: (j, 0))
    frow = pl.BlockSpec((1, nh, blk), lambda j: (j, 0, 0))
    tile_f32, tile_b16 = pltpu.VMEM((blk, blk), F32), pltpu.VMEM((blk, blk), BF16)
    return pl.pallas_call(
        body, name="fox_attn_bwd_dkv",
        grid=(nq,),
        in_specs=[vm, wide, wide, vm, frow, vm, vm, vm],
        out_specs=[wide, wide, frow],
        out_shape=[jax.ShapeDtypeStruct((t, width), F32), jax.ShapeDtypeStruct((t, width), F32),
                   jax.ShapeDtypeStruct((nq, nh, blk), F32)],
        scratch_shapes=[tile_f32, tile_f32, tile_b16, tile_b16, pltpu.VMEM((1, blk), F32),
                        pltpu.VMEM((blk, HEAD), F32), pltpu.VMEM((blk, HEAD), F32)],
        compiler_params=_cparams(dimension_semantics=("parallel",)),
    )(q, k, v, f_col, f_row, do, lse, delta)


def _shift_down(u, n):
    row = lax.broadcasted_iota(jnp.int32, u.shape, 0)
    return jnp.where(row < n, 0.0, pltpu.roll(u, n, 0))


def _shift_up(u, n):
    t = u.shape[0]
    row = lax.broadcasted_iota(jnp.int32, u.shape, 0)
    return jnp.where(row >= t - n, 0.0, pltpu.roll(u, t - n, 0))


def _convglu_specs(t):
    return [pl.BlockSpec((2, t, LANES), lambda j: (0, 0, j)),
            pl.BlockSpec((2, CONV_TAPS, LANES), lambda j: (0, 0, j)),
            pl.BlockSpec((2, 1, LANES), lambda j: (0, 0, j))]


def _convglu_fwd(u, cw, cb):
    _, t, fp = u.shape

    def body(u_ref, w_ref, b_ref, a_ref):
        c = []
        for hf in range(2):
            uv, w = u_ref[hf], w_ref[hf]
            c.append(w[0:1] * _shift_down(uv, 2) + w[1:2] * _shift_down(uv, 1) + w[2:3] * uv + b_ref[hf])
        a_ref[...] = (_silu(c[0]) * c[1]).astype(a_ref.dtype)

    return pl.pallas_call(
        body, name="convglu_fwd",
        grid=(fp // LANES,),
        in_specs=_convglu_specs(t),
        out_specs=pl.BlockSpec((t, LANES), lambda j: (0, j)),
        out_shape=jax.ShapeDtypeStruct((t, fp), BF16),
        compiler_params=_cparams(dimension_semantics=("parallel",)),
    )(u, cw, cb)


def _convglu_bwd(u, cw, cb, da):
    _, t, fp = u.shape

    def body(u_ref, w_ref, b_ref, da_ref, du_ref, dw_ref, db_ref):
        us, c = [], []
        for hf in range(2):
            uv, w = u_ref[hf], w_ref[hf]
            u1, u2 = _shift_down(uv, 1), _shift_down(uv, 2)
            us.append((uv, u1, u2))
            c.append(w[0:1] * u2 + w[1:2] * u1 + w[2:3] * uv + b_ref[hf])
        gc, vc = c
        sg = _sigmoid(gc)
        dav = da_ref[...].astype(F32)
        dcs = [dav * vc * (sg * (1.0 + gc * (1.0 - sg))), dav * (gc * sg)]
        for hf in range(2):
            dc, w = dcs[hf], w_ref[hf]
            uv, u1, u2 = us[hf]
            du = w[2:3] * dc + w[1:2] * _shift_up(dc, 1) + w[0:1] * _shift_up(dc, 2)
            du_ref[hf] = du.astype(du_ref.dtype)
            dw_ref[hf, 0:1, :] = jnp.sum(dc * u2, axis=0, keepdims=True)
            dw_ref[hf, 1:2, :] = jnp.sum(dc * u1, axis=0, keepdims=True)
            dw_ref[hf, 2:3, :] = jnp.sum(dc * uv, axis=0, keepdims=True)
            db_ref[hf] = jnp.sum(dc, axis=0, keepdims=True)

    specs = _convglu_specs(t)
    return pl.pallas_call(
        body, name="convglu_bwd",
        grid=(fp // LANES,),
        in_specs=specs + [pl.BlockSpec((t, LANES), lambda j: (0, j))],
        out_specs=specs,
        out_shape=[jax.ShapeDtypeStruct((2, t, fp), BF16), jax.ShapeDtypeStruct((2, CONV_TAPS, fp), F32),
                   jax.ShapeDtypeStruct((2, 1, fp), F32)],
        compiler_params=_cparams(dimension_semantics=("parallel",)),
    )(u, cw, cb, da)


def _local_step(x, target, mods, lb, small, get_w, put_g, *, tb=512, attn_blk=512):
    t, d = x.shape
    nh = d // HEAD
    nb = NDEV
    wts = {}
    vec = lambda *names: [mods[n] for n in names]

    def ffn_fwd(h2, l):
        u = _mm_wblk(h2, wts[f"up{l}"], F32, f"ffn{l}_up", gb=nb // 2, split=2, tm=512)
        a = _convglu_fwd(u, small[f"conv_w{l}"], small[f"conv_b{l}"])
        f = _mm(a, wts[f"down{l}"], "nn", F32, f"ffn{l}_down", tk=4096)
        return u, a, f

    def ffn_bwd(df, h2, u, a, l):
        da = _mm(df, wts[f"down{l}"], "nt", BF16, f"ffn{l}_down_dx", tn=1536)
        dwd = _mm(a, df, "tn", BF16, f"ffn{l}_down_dw", tm=1536, tk=1024)
        du, dcw, dcb = _convglu_bwd(u, small[f"conv_w{l}"], small[f"conv_b{l}"], da)
        dh2 = _mm_wblk_dx(du, wts[f"up{l}"], F32, f"ffn{l}_up_dx", k=d, gb=nb // 2, split=2, tm=512)
        dwu = _mm_wblk_dw(h2, du, f"ffn{l}_up_dw", nb=nb, gb=nb // 4, split=2, tk=2048)
        return dh2, dwu, dwd, dcw, dcb

    (h_a,) = _row_fwd(_f_mod, [(x, d, 0)], vec("sh1_0", "sc1_0"), [BF16], tb=tb, name="l0_mod1")
    wts.update(get_w("l0a", h_a))
    proj_a = _mm_wblk(h_a, wts["a_in"], F32, "a_in", gb=nb // 2)
    ypre, states = _hgrn2_fwd(proj_a, lb, small["a_norm_g"], tb)
    wts.update(get_w("l0b", ypre))
    y_a = _mm(ypre, wts["a_out"], "nn", F32, "a_out")
    x1, h2_0 = _row_fwd(_f_res_mod, [(x, d, 0), (y_a, d, 0)], vec("g1_0", "sh2_0", "sc2_0"), [F32, BF16],
                        tb=tb, name="l0_res_mod2")
    u0, a0, f0 = ffn_fwd(h2_0, 0)
    x2, h_kv, h_q = _row_fwd(_f_res_mod2, [(x1, d, 0), (f0, d, 0)],
                             vec("g2_0", "kv_sh", "kv_sc", "sh1_1", "sc1_1"), [F32, BF16, BF16],
                             tb=tb, name="l0_res_kvmod_qmod")
    wts.update(get_w("l1", h_kv))
    proj_kv = _mm(h_kv, wts["kv"], "nn", F32, "kv_proj")
    proj_f = _mm(h_kv, wts["kv_f"], "nn", F32, "kv_fproj")
    (k_n,) = _row_fwd(_f_knorm, [(proj_kv, HEAD, 0)], [small["k_norm_g"]], [BF16], nsub=nh, tb=tb, name="k_norm")
    v_b = proj_kv[:, d:].astype(BF16)
    f_logit_t = proj_f[:, :nh].T
    f_bias = small["kv_b_f"].reshape(nh, 1)
    f_t = _fgate_fwd(f_logit_t, f_bias)
    f_col = f_t.T
    f_row = f_t.reshape(nh, t // attn_blk, attn_blk).transpose(1, 0, 2)
    proj_q = _mm_wblk(h_q, wts["b_q"], F32, "b_q", gb=nb)
    (q_n,) = _row_fwd(_f_qnorm, [(proj_q, HEAD, 0)], [small["q_norm_g"]], [BF16], nsub=nh, tb=tb, name="q_norm")
    o_att, lse = _attn_fwd(q_n, k_n, v_b, f_col, f_row, attn_blk)
    (z,) = _row_fwd(_f_outgate, [(o_att, HEAD, 0), (proj_q, HEAD, 1)], [], [BF16], nsub=nh, tb=tb, name="out_gate")
    y_b = _mm(z, wts["b_out"], "nn", F32, "b_out")
    x3, h2_1 = _row_fwd(_f_res_mod, [(x2, d, 0), (y_b, d, 0)], vec("g1_1", "sh2_1", "sc2_1"), [F32, BF16],
                        tb=tb, name="l1_res_mod2")
    u1, a1, f1 = ffn_fwd(h2_1, 1)
    loss, dx4, df1, dg2_1 = _loss_call(x3, f1, mods["g2_1"], target, tb)

    g = {}
    dmods = {"g2_1": dg2_1}
    dh2, g["up1"], g["down1"], g["conv_w1"], g["conv_b1"] = ffn_bwd(df1, h2_1, u1, a1, 1)
    (dx2, dy_b), (dmods["g1_1"], dmods["sh2_1"], dmods["sc2_1"]) = _row_bwd(
        _f_res_mod, [(x2, d, 0), (y_b, d, 0)], vec("g1_1", "sh2_1", "sc2_1"),
        [(dx4, d, 0), (dh2, d, 0)], [F32, BF16], tb=tb, name="l1_res_mod2_bwd")
    dz = _mm(dy_b, wts["b_out"], "nt", F32, "b_out_dx")
    g["b_out"] = _mm(z, dy_b, "tn", BF16, "b_out_dw", tk=1024)
    (do_att, dog), _ = _row_bwd(_f_outgate, [(o_att, HEAD, 0), (proj_q, HEAD, 1)], [], [(dz, HEAD, 0)],
                                [BF16, BF16], nsub=nh, tb=tb, name="out_gate_bwd")
    dq_n, dfc_q, delta = _attn_bwd_dq(q_n, k_n, v_b, f_col, f_row, o_att, do_att, lse, attn_blk)
    dk_n, dv, dfr_k = _attn_bwd_dkv(q_n, k_n, v_b, f_col, f_row, do_att, lse, delta, attn_blk)
    (dpq,), (g["q_norm_g"],) = _row_bwd(_f_qnorm, [(proj_q, HEAD, 0)], [small["q_norm_g"]],
                                        [(dq_n, HEAD, 0)], [BF16], nsub=nh, tb=tb, name="q_norm_bwd")
    dproj_q = jnp.concatenate([dpq, dog], axis=1)
    dh_q = _mm_wblk_dx(dproj_q, wts["b_q"], F32, "b_q_dx", k=d, gb=nb)
    g["b_q"] = _mm_wblk_dw(h_q, dproj_q, "b_q_dw", nb=nb, gb=nb)
    (dpk,), (g["k_norm_g"],) = _row_bwd(_f_knorm, [(proj_kv, HEAD, 0)], [small["k_norm_g"]],
                                        [(dk_n, HEAD, 0)], [BF16], nsub=nh, tb=tb, name="k_norm_bwd")
    dproj_kv = jnp.concatenate([dpk, dv.astype(BF16)], axis=1)
    df_t = dfc_q.T + dfr_k.transpose(1, 0, 2).reshape(nh, t)
    dflogit_t, g["kv_b_f"] = _fgate_bwd(f_logit_t, f_bias, df_t)
    dproj_f = jnp.pad(dflogit_t.T, ((0, 0), (0, LANES - nh))).astype(BF16)
    dh_kv = _mm(dproj_kv, wts["kv"], "nt", F32, "kv_proj_dx") + _mm(dproj_f, wts["kv_f"], "nt", F32, "kv_fproj_dx")
    g["kv"] = _mm(h_kv, dproj_kv, "tn", BF16, "kv_proj_dw", tk=1024)
    g["kv_f"] = _mm(h_kv, dproj_f, "tn", F32, "kv_fproj_dw", tk=1024)
    sent = put_g("l1", {n: g.pop(n) for n in ("b_out", "b_q", "kv", "kv_f", "up1", "down1")})
    (dx1, df0), (dmods["g2_0"], dmods["kv_sh"], dmods["kv_sc"], dmods["sh1_1"], dmods["sc1_1"]) = _row_bwd(
        _f_res_mod2, [(x1, d, 0), (f0, d, 0)], [mods["g2_0"] + sent] + vec("kv_sh", "kv_sc", "sh1_1", "sc1_1"),
        [(dx2, d, 0), (dh_kv, d, 0), (dh_q, d, 0)], [F32, BF16], tb=tb, name="l0_res_kvmod_qmod_bwd")
    dh2, g["up0"], g["down0"], g["conv_w0"], g["conv_b0"] = ffn_bwd(df0, h2_0, u0, a0, 0)
    (dx0, dy_a), (dmods["g1_0"], dmods["sh2_0"], dmods["sc2_0"]) = _row_bwd(
        _f_res_mod, [(x, d, 0), (y_a, d, 0)], vec("g1_0", "sh2_0", "sc2_0"),
        [(dx1, d, 0), (dh2, d, 0)], [F32, BF16], tb=tb, name="l0_res_mod2_bwd")
    dypre = _mm(dy_a, wts["a_out"], "nt", BF16, "a_out_dx")
    g["a_out"] = _mm(ypre, dy_a, "tn", BF16, "a_out_dw", tk=1024)
    sent = put_g("l0b", {n: g.pop(n) for n in ("a_out", "up0", "down0")})
    dpa_q, dpa_f, dpa_i, dpa_g, dlb, g["a_norm_g"] = _hgrn2_bwd(proj_a, lb + sent, small["a_norm_g"], states,
                                                               dypre, tb)
    dproj_a = jnp.concatenate([dpa_q, dpa_f, dpa_i, dpa_g], axis=1)
    dh_a = _mm_wblk_dx(dproj_a, wts["a_in"], F32, "a_in_dx", k=d, gb=nb, tm=512)
    put_g("l0a", {"a_in": _mm_wblk_dw(h_a, dproj_a, "a_in_dw", nb=nb, gb=nb // 4, tk=2048)})
    (grad_x,), (dmods["sh1_0"], dmods["sc1_0"]) = _row_bwd(
        _f_mod, [(x, d, 0)], vec("sh1_0", "sc1_0"), [(dh_a, d, 0)], [F32], tb=tb, name="l0_mod1_bwd",
        add_to=(0, dx0))
    return loss, grad_x, dmods, dlb, g


def _position():
    return lax.axis_index("x"), lax.axis_index("y"), lax.axis_index("c")


def _hbm_specs(n):
    return [pl.BlockSpec(memory_space=pl.ANY)] * n


def _all_gather(arrs, name):
    n = len(arrs)

    def body(*refs):
        x_refs, out_refs = refs[:n], refs[n:2 * n]
        send_sems, recv_sems, local_sems = refs[2 * n:]
        x, y, cc = _position()
        me, sibling = (x, y, cc), (x, y, 1 - cc)
        chips = [(1 - x, y), (x, 1 - y), (1 - x, 1 - y)]

        def copy(a, k, block, to, src=None):
            slot = out_refs[a].at[4 * block[0] + 2 * block[1] + block[2]]
            return pltpu.make_async_remote_copy(
                src_ref=slot if src is None else src, dst_ref=slot,
                send_sem=send_sems.at[7 * a + k], recv_sem=recv_sems.at[7 * a + k],
                device_id=to, device_id_type=_MESH)

        local = [pltpu.make_async_copy(x_refs[a], out_refs[a].at[4 * x + 2 * y + cc], local_sems.at[a])
                 for a in range(n)]
        for cp in local:
            cp.start()
        first = []
        for a in range(n):
            first.append(copy(a, 0, me, sibling, src=x_refs[a]))
            first += [copy(a, 1 + j, me, (*chip, cc), src=x_refs[a]) for j, chip in enumerate(chips)]
        for cp in first:
            cp.start()
        passed = []
        for j, chip in enumerate(chips):
            for a in range(n):
                copy(a, 1 + j, (*chip, cc), me).wait_recv()
                fwd = copy(a, 4 + j, (*chip, cc), sibling)
                fwd.start()
                passed.append(fwd)
        for a in range(n):
            copy(a, 0, sibling, me).wait_recv()
        for j, chip in enumerate(chips):
            for a in range(n):
                copy(a, 4 + j, (*chip, 1 - cc), me).wait_recv()
        for cp in first + passed:
            cp.wait_send()
        for cp in local:
            cp.wait()

    return pl.pallas_call(
        body, name=name,
        out_shape=[jax.ShapeDtypeStruct((NDEV, *a.shape), a.dtype) for a in arrs],
        in_specs=_hbm_specs(n), out_specs=_hbm_specs(n),
        scratch_shapes=[pltpu.SemaphoreType.DMA((7 * n,)), pltpu.SemaphoreType.DMA((7 * n,)),
                        pltpu.SemaphoreType.DMA((n,))],
    )(*arrs)


_XCHG_EFFECT = pltpu.SideEffectType.DATAFLOW_SIDE_EFFECTING


def _xchg_copies(src_refs, land_refs, send_sems, recv_sems, local_sems, scatter):
    x, y, cc = _position()
    me = 4 * x + 2 * y + cc
    remote, local = [], []
    for a, (src, land) in enumerate(zip(src_refs, land_refs)):
        local.append(pltpu.make_async_copy(src.at[me] if scatter else src, land.at[me], local_sems.at[a]))
        for rel in range(1, NDEV):
            px = 1 - x if rel & 4 else x
            py = 1 - y if rel & 2 else y
            pc = 1 - cc if rel & 1 else cc
            k = (NDEV - 1) * a + rel - 1
            remote.append(pltpu.make_async_remote_copy(
                src_ref=src.at[4 * px + 2 * py + pc] if scatter else src, dst_ref=land.at[me],
                send_sem=send_sems.at[k], recv_sem=recv_sems.at[k], device_id=(px, py, pc), device_id_type=_MESH))
    return remote, local


def _xchg_start(srcs, scatter, name):
    n = len(srcs)
    lands = [lax.empty(s.shape if scatter else (NDEV, *s.shape), s.dtype) for s in srcs]

    def body(*refs):
        remote, local = _xchg_copies(refs[:n], refs[n:2 * n], *refs[2 * n:2 * n + 3], scatter)
        for cp in local + remote:
            cp.start()
        token = refs[-1]
        token[...] = jnp.zeros_like(token)

    hbm = pl.BlockSpec(memory_space=pltpu.HBM)
    sem = pl.BlockSpec(memory_space=pltpu.SEMAPHORE)
    out = pl.pallas_call(
        body, name=name,
        out_shape=(pltpu.SemaphoreType.DMA(((NDEV - 1) * n,)), pltpu.SemaphoreType.DMA(((NDEV - 1) * n,)),
                   pltpu.SemaphoreType.DMA((n,)),
                   *[pltpu.HBM(a.shape, a.dtype) for a in srcs + lands], jax.ShapeDtypeStruct((8, LANES), F32)),
        in_specs=[hbm] * (2 * n),
        out_specs=(sem, sem, sem, *[hbm] * (2 * n), pl.BlockSpec(memory_space=pltpu.VMEM)),
        input_output_aliases={i: 3 + i for i in range(2 * n)},
        compiler_params=pltpu.CompilerParams(has_side_effects=_XCHG_EFFECT),
    )(*[pltpu.with_memory_space_constraint(a, pltpu.HBM) for a in srcs + lands])
    return out[:-1], out[-1][0, 0]


def _xchg_wait(handles, after, scatter, name):
    n = (len(handles) - 3) // 2

    def body(*refs):
        remote, local = _xchg_copies(refs[:n], refs[n:2 * n], *refs[2 * n:2 * n + 3], scatter)
        for cp in remote:
            cp.wait_send()
            cp.wait_recv()
        for cp in local:
            cp.wait()

    hbm = pl.BlockSpec(memory_space=pltpu.HBM)
    sem = pl.BlockSpec(memory_space=pltpu.SEMAPHORE)
    thru = list(handles[3:])
    out = pl.pallas_call(
        body, name=name,
        out_shape=tuple(pltpu.HBM(a.shape, a.dtype) for a in thru),
        in_specs=[hbm] * (2 * n) + [sem, sem, sem, pl.BlockSpec(memory_space=pl.ANY)],
        out_specs=tuple([hbm] * (2 * n)),
        input_output_aliases={i: i for i in range(2 * n)},
        compiler_params=pltpu.CompilerParams(has_side_effects=_XCHG_EFFECT),
    )(*thru, *handles[:3], after)
    return list(out[n:])


def _rs_sibling_exchange(gs):
    n = len(gs)

    def body(*refs):
        g_refs, recv_refs = refs[:n], refs[n:2 * n]
        send_sems, recv_sems = refs[2 * n:]
        x, y, cc = _position()
        copies = [pltpu.make_async_remote_copy(
            src_ref=g_refs[a].at[q, 1 - cc], dst_ref=recv_refs[a].at[q], send_sem=send_sems.at[NCHIP * a + q],
            recv_sem=recv_sems.at[NCHIP * a + q], device_id=(x, y, 1 - cc), device_id_type=_MESH)
            for a in range(n) for q in range(NCHIP)]
        for cp in copies:
            cp.start()
        for cp in copies:
            cp.wait()

    return pl.pallas_call(
        body, name="rs_sibling_exchange",
        out_shape=[jax.ShapeDtypeStruct((NCHIP, *g.shape[2:]), g.dtype) for g in gs],
        in_specs=_hbm_specs(n), out_specs=_hbm_specs(n),
        scratch_shapes=[pltpu.SemaphoreType.DMA((NCHIP * n,)), pltpu.SemaphoreType.DMA((NCHIP * n,))],
    )(*gs)


def _rs_chip_exchange(parts):
    n = len(parts)

    def body(*refs):
        p_refs, recv_refs = refs[:n], refs[n:2 * n]
        send_sems, recv_sems, local_sems = refs[2 * n:]
        x, y, cc = _position()
        myq = 2 * x + y
        chips = [(1 - x, y), (x, 1 - y), (1 - x, 1 - y)]

        def copy(a, k, px, py, src_q, dst_q):
            return pltpu.make_async_remote_copy(
                src_ref=p_refs[a].at[src_q], dst_ref=recv_refs[a].at[dst_q], send_sem=send_sems.at[3 * a + k],
                recv_sem=recv_sems.at[3 * a + k], device_id=(px, py, cc), device_id_type=_MESH)

        local = [pltpu.make_async_copy(p_refs[a].at[myq], recv_refs[a].at[myq], local_sems.at[a]) for a in range(n)]
        for cp in local:
            cp.start()
        sends = [copy(a, k, px, py, 2 * px + py, myq) for a in range(n) for k, (px, py) in enumerate(chips)]
        for cp in sends:
            cp.start()
        for a in range(n):
            for k, (px, py) in enumerate(chips):
                copy(a, k, px, py, myq, 2 * px + py).wait_recv()
        for cp in sends:
            cp.wait_send()
        for cp in local:
            cp.wait()

    return pl.pallas_call(
        body, name="rs_chip_exchange",
        out_shape=[jax.ShapeDtypeStruct(p.shape, p.dtype) for p in parts],
        in_specs=_hbm_specs(n), out_specs=_hbm_specs(n),
        scratch_shapes=[pltpu.SemaphoreType.DMA((3 * n,)), pltpu.SemaphoreType.DMA((3 * n,)),
                        pltpu.SemaphoreType.DMA((n,))],
    )(*parts)


def _pair_sum(own, got, name):
    n, r, c = own.shape

    def body(a_ref, b_ref, o_ref):
        o_ref[...] = (a_ref[...].astype(F32) + b_ref[...].astype(F32)).astype(o_ref.dtype)

    spec = pl.BlockSpec((1, r, c), lambda q: (q, 0, 0))
    return pl.pallas_call(body, name=name, grid=(n,), in_specs=[spec, spec], out_specs=spec,
                          out_shape=jax.ShapeDtypeStruct((n, r, c), own.dtype),
                          compiler_params=_cparams(dimension_semantics=("parallel",)))(own, got)


def _slab_sum(slabs, name, tr=None):
    n, r, c = slabs.shape
    tr = r if tr is None else tr

    def body(s_ref, o_ref):
        acc = s_ref[0].astype(F32)
        for q in range(1, n):
            acc = acc + s_ref[q].astype(F32)
        o_ref[...] = acc

    return pl.pallas_call(body, name=name, grid=(r // tr,),
                          in_specs=[pl.BlockSpec((n, tr, c), lambda i: (0, i, 0))],
                          out_specs=pl.BlockSpec((tr, c), lambda i: (i, 0)),
                          out_shape=jax.ShapeDtypeStruct((r, c), F32),
                          compiler_params=_cparams(dimension_semantics=("parallel",)))(slabs)


def _ada_fwd(c_all, ada_w, kv_ada_w, logits):
    rows, d = c_all.shape
    n0, nkv = ada_w.shape[2], kv_ada_w.shape[1]

    def body(c_ref, w_ref, kw_ref, lg_ref, part_ref, cact_ref, lb_ref):
        ca = _silu(c_ref[...])
        cact_ref[...] = ca
        part_ref[:, 0:n0] = _bdot_raw(ca, w_ref[0], _NN)
        part_ref[:, n0:2 * n0] = _bdot_raw(ca, w_ref[1], _NN)
        part_ref[:, 2 * n0:2 * n0 + nkv] = _bdot_raw(ca, kw_ref[...], _NN)
        lb_ref[...] = _sigmoid(lg_ref[0:1, :] - lg_ref[1:2, :])

    vm = pl.BlockSpec(memory_space=pltpu.VMEM)
    return pl.pallas_call(
        body, name="ada_fwd", in_specs=[vm, vm, vm, vm], out_specs=[vm, vm, vm],
        out_shape=[jax.ShapeDtypeStruct((rows, 2 * n0 + nkv), F32), jax.ShapeDtypeStruct((rows, d), F32),
                   jax.ShapeDtypeStruct((1, d), F32)],
        compiler_params=_cparams(),
    )(c_all, ada_w, kv_ada_w, logits)


def _ada_bwd(c_act, dm0, dm1, dkv, lb, dlb):
    rows, d = c_act.shape

    def body(c_ref, d0_ref, d1_ref, dk_ref, lb_ref, dlb_ref, dw_ref, dkw_ref, dlg_ref):
        ca = c_ref[...]
        dw_ref[0] = _bdot_raw(ca, d0_ref[...], _TN)
        dw_ref[1] = _bdot_raw(ca, d1_ref[...], _TN)
        dkw_ref[...] = _bdot_raw(ca, dk_ref[...], _TN)
        lbv = lb_ref[...]
        dl0 = dlb_ref[...] * lbv * (1.0 - lbv)
        dlg_ref[0:1, :] = dl0
        dlg_ref[1:2, :] = -dl0

    vm = pl.BlockSpec(memory_space=pltpu.VMEM)
    return pl.pallas_call(
        body, name="ada_bwd", in_specs=[vm] * 6, out_specs=[vm, vm, vm],
        out_shape=[jax.ShapeDtypeStruct((2, d, dm0.shape[1]), F32), jax.ShapeDtypeStruct((d, dkv.shape[1]), F32),
                   jax.ShapeDtypeStruct((2, d), F32)],
        compiler_params=_cparams(),
    )(c_act, dm0, dm1, dkv, lb, dlb)


def _adamw(w, g, m, v, name, tr=512):
    r, c = w.shape
    tr = _divisor_tile(r, tr, unit=8)
    c1 = 1.0 - ADAM_B1 ** ADAM_STEP
    c2 = 1.0 - ADAM_B2 ** ADAM_STEP

    def body(w_ref, g_ref, m_ref, v_ref, d_ref, mo_ref, vo_ref):
        gv = g_ref[...]
        mn = ADAM_B1 * m_ref[...] + (1.0 - ADAM_B1) * gv
        vn = ADAM_B2 * v_ref[...] + (1.0 - ADAM_B2) * (gv * gv)
        d_ref[...] = -ADAM_LR * ((mn / c1) / (jnp.sqrt(vn / c2) + ADAM_EPS) + ADAM_WD * w_ref[...])
        mo_ref[...] = mn
        vo_ref[...] = vn

    spec = pl.BlockSpec((tr, c), lambda i: (i, 0))
    out = jax.ShapeDtypeStruct((r, c), F32)
    return pl.pallas_call(body, name=name, grid=(r // tr,), in_specs=[spec] * 4, out_specs=[spec] * 3,
                          out_shape=[out, out, out],
                          compiler_params=_cparams(dimension_semantics=("parallel",)))(w, g, m, v)


def _pad_rows(a, rows):
    return jnp.pad(a, ((0, rows - a.shape[0]), (0, 0)))


def _pack_small(parts, lanes=LANES, row_unit=8):
    flat = jnp.concatenate([p.reshape(-1).astype(F32) for p in parts])
    rows = _round_up(-(-flat.shape[0] // lanes), row_unit)
    return jnp.pad(flat, (0, rows * lanes - flat.shape[0])).reshape(rows, lanes)


def _unpack_small(flat, shapes):
    out, off = [], 0
    for s in shapes:
        n = 1
        for k in s:
            n *= k
        out.append(flat[off:off + n].reshape(s))
        off += n
    return out


def _pad_shard_cols(a, n_loc, n_pad):
    lead = a.shape[:-1]
    a = a.reshape(*lead, NDEV, n_loc)
    a = jnp.pad(a, [(0, 0)] * (len(lead) + 1) + [(0, n_pad - n_loc)])
    return a.reshape(*lead, NDEV * n_pad)


def _unpad_shard_cols(a, n_loc, n_pad):
    lead = a.shape[:-1]
    return a.reshape(*lead, NDEV, n_pad)[..., :n_loc].reshape(*lead, NDEV * n_loc)


def kernel(x, c, ada_w, ada_b, a_w_in, a_lb_logits, a_norm_g, a_w_out, kv_ada_w, kv_ada_b, kv_w, kv_b_f, k_norm_g, b_w_q, q_norm_g, b_w_out, ffn_w_up, ffn_conv_w, ffn_conv_b, ffn_w_down, loss_target, m_ada_w, m_ada_b, m_a_w_in, m_a_lb_logits, m_a_norm_g, m_a_w_out, m_kv_ada_w, m_kv_ada_b, m_kv_w, m_kv_b_f, m_k_norm_g, m_b_w_q, m_q_norm_g, m_b_w_out, m_ffn_w_up, m_ffn_conv_w, m_ffn_conv_b, m_ffn_w_down, v_ada_w, v_ada_b, v_a_w_in, v_a_lb_logits, v_a_norm_g, v_a_w_out, v_kv_ada_w, v_kv_ada_b, v_kv_w, v_kv_b_f, v_k_norm_g, v_b_w_q, v_q_norm_g, v_b_w_out, v_ffn_w_up, v_ffn_conv_w, v_ffn_conv_b, v_ffn_w_down):
    t, d = x.shape[1], x.shape[2]
    nh = d // HEAD
    ncw = ffn_w_up.shape[2]
    ncp = _round_up(ncw, LANES)
    two_f = ncw * NDEV
    ff = two_f // 2
    fp = ncp * NDEV // 2
    rd = ffn_w_down.shape[1]
    me = 4 * lax.axis_index("x") + 2 * lax.axis_index("y") + lax.axis_index("c")
    weights = dict(ada_w=ada_w, ada_b=ada_b, a_w_in=a_w_in, a_lb_logits=a_lb_logits, a_norm_g=a_norm_g,
                   a_w_out=a_w_out, kv_ada_w=kv_ada_w, kv_ada_b=kv_ada_b, kv_w=kv_w, kv_b_f=kv_b_f,
                   k_norm_g=k_norm_g, b_w_q=b_w_q, q_norm_g=q_norm_g, b_w_out=b_w_out, ffn_w_up=ffn_w_up,
                   ffn_conv_w=ffn_conv_w, ffn_conv_b=ffn_conv_b, ffn_w_down=ffn_w_down)
    m_in = dict(ada_w=m_ada_w, ada_b=m_ada_b, a_w_in=m_a_w_in, a_lb_logits=m_a_lb_logits, a_norm_g=m_a_norm_g,
                a_w_out=m_a_w_out, kv_ada_w=m_kv_ada_w, kv_ada_b=m_kv_ada_b, kv_w=m_kv_w, kv_b_f=m_kv_b_f,
                k_norm_g=m_k_norm_g, b_w_q=m_b_w_q, q_norm_g=m_q_norm_g, b_w_out=m_b_w_out, ffn_w_up=m_ffn_w_up,
                ffn_conv_w=m_ffn_conv_w, ffn_conv_b=m_ffn_conv_b, ffn_w_down=m_ffn_w_down)
    v_in = dict(ada_w=v_ada_w, ada_b=v_ada_b, a_w_in=v_a_w_in, a_lb_logits=v_a_lb_logits, a_norm_g=v_a_norm_g,
                a_w_out=v_a_w_out, kv_ada_w=v_kv_ada_w, kv_ada_b=v_kv_ada_b, kv_w=v_kv_w, kv_b_f=v_kv_b_f,
                k_norm_g=v_k_norm_g, b_w_q=v_b_w_q, q_norm_g=v_q_norm_g, b_w_out=v_b_w_out, ffn_w_up=v_ffn_w_up,
                ffn_conv_w=v_ffn_conv_w, ffn_conv_b=v_ffn_conv_b, ffn_w_down=v_ffn_w_down)
    order = list(weights)

    up_loc = jnp.pad(ffn_w_up, ((0, 0), (0, 0), (0, ncp - ncw))).astype(BF16)
    down_loc = ffn_w_down.astype(BF16)
    gather_names = {"l0a": ["a_in"], "l0b": ["a_out", "up0", "down0"], "l1": ["kv", "b_q", "b_out", "up1", "down1"]}
    shards = {"a_in": a_w_in[0].astype(BF16), "a_out": a_w_out[0].astype(BF16), "up0": up_loc[0],
              "down0": down_loc[0], "kv": kv_w.astype(BF16), "b_q": b_w_q[0].astype(BF16),
              "b_out": b_w_out[0].astype(BF16), "up1": up_loc[1], "down1": down_loc[1]}
    in_flight, zero = {}, jnp.zeros((), F32)
    for grp in ("l0a", "l0b", "l1"):
        srcs = [shards[n] for n in gather_names[grp]]
        srcs[0] = srcs[0] + zero.astype(BF16)
        in_flight[grp], zero = _xchg_start(srcs, False, f"gather_{grp}_start")

    pre = _pack_small([c, a_lb_logits, ffn_conv_w]) + zero
    (pre_all,) = _all_gather([pre], "gather_small_inputs")
    pre_all = pre_all.reshape(NDEV, -1)
    c_all = pre_all[:, :d]
    logits = pre_all[:, d:d + 2 * HEAD].reshape(NDEV, 2, HEAD).transpose(1, 0, 2).reshape(2, d)
    conv_w_full = pre_all[:, d + 2 * HEAD:d + 2 * HEAD + 2 * CONV_TAPS * ncw]
    conv_w_full = conv_w_full.reshape(NDEV, 2, CONV_TAPS, ncw).transpose(1, 2, 0, 3).reshape(2, CONV_TAPS, two_f)

    part, c_act, lb = _ada_fwd(_pad_rows(c_all, 2 * NDEV), ada_w, kv_ada_w, logits)
    (part_all,) = _all_gather([part[:NDEV]], "gather_adaln")
    mine = lax.dynamic_index_in_dim(part_all, me, axis=1, keepdims=False)
    n0, nkv = ada_w.shape[2], kv_ada_w.shape[1]
    mod_names = ["sh1", "sc1", "g1", "sh2", "sc2", "g2"]
    mods = {}
    for l in range(2):
        row = mine[:, l * n0:(l + 1) * n0].reshape(-1) + ada_b[l]
        for k, nm in enumerate(mod_names):
            mods[f"{nm}_{l}"] = row[k * d:(k + 1) * d].reshape(1, d)
    kvrow = mine[:, 2 * n0:2 * n0 + nkv].reshape(-1) + kv_ada_b
    mods["kv_sh"], mods["kv_sc"] = kvrow[:d].reshape(1, d), kvrow[d:].reshape(1, d)

    small = {"a_norm_g": a_norm_g, "k_norm_g": k_norm_g.reshape(1, HEAD), "q_norm_g": q_norm_g, "kv_b_f": kv_b_f}
    for l in range(2):
        small[f"conv_w{l}"] = _pad_shard_cols(conv_w_full[l], ncw, ncp).reshape(CONV_TAPS, 2, fp).transpose(1, 0, 2)
        small[f"conv_b{l}"] = _pad_shard_cols(ffn_conv_b[l], ncw, ncp).reshape(2, 1, fp)

    def get_w(grp, after):
        got = dict(zip(gather_names[grp], _xchg_wait(in_flight[grp], after, False, f"gather_{grp}_wait")))
        out = {}
        for n, a in got.items():
            if n in ("a_out", "b_out"):
                out[n] = a.reshape(d, d)
            elif n in ("down0", "down1"):
                dn = a.reshape(NCHIP, ff // NCHIP, d)
                out[n] = jnp.pad(dn, ((0, 0), (0, ncp - ncw), (0, 0))).reshape(fp, d)
            elif n == "kv":
                kv_full = a.transpose(1, 0, 2).reshape(d, NDEV * kv_w.shape[1])
                out["kv"] = kv_full[:, :2 * d]
                out["kv_f"] = jnp.pad(kv_full[:, 2 * d:], ((0, 0), (0, LANES - nh)))
            else:
                out[n] = a
        return out

    scatter_flight, g_last = {}, {}

    def put_g(grp, gr):
        if grp == "l0a":
            g_last.update(gr)
            return zero
        if grp == "l1":
            g_kvw = jnp.concatenate([gr["kv"], gr["kv_f"][:, :nh].astype(BF16)], axis=1)
            arrs = {"kv_w": g_kvw.reshape(d, NDEV, kv_w.shape[1]).transpose(1, 0, 2), "b_w_q": gr["b_q"],
                    "b_w_out": gr["b_out"].reshape(NDEV, d // NDEV, d), "up1": gr["up1"],
                    "down1": gr["down1"].reshape(NCHIP, ncp, d)[:, :ncw].reshape(NDEV, rd, d)}
        else:
            arrs = {"a_w_out": gr["a_out"].reshape(NDEV, d // NDEV, d), "up0": gr["up0"],
                    "down0": gr["down0"].reshape(NCHIP, ncp, d)[:, :ncw].reshape(NDEV, rd, d)}
        handles, sent = _xchg_start(list(arrs.values()), True, f"scatter_{grp}_start")
        scatter_flight[grp] = (list(arrs), handles)
        return sent

    loss_v, grad_x, dmods, dlb, g = _local_step(x[0], loss_target[0], mods, lb, small, get_w, put_g)
    loss = lax.psum(loss_v[0, 0], ("x", "y", "c"))

    g_sum = {}
    for grp in ("l1", "l0b"):
        names, handles = scatter_flight[grp]
        for nm, a in zip(names, _xchg_wait(handles, grad_x, True, f"scatter_{grp}_wait")):
            g_sum[nm] = _slab_sum(a, f"rs_slab_sum_{nm}")
    a_in_parts = g_last["a_in"].reshape(NCHIP, 2, *g_last["a_in"].shape[1:])
    (from_sibling,) = _rs_sibling_exchange([a_in_parts])
    own = lax.dynamic_index_in_dim(a_in_parts, lax.axis_index("c"), axis=1, keepdims=False)
    (from_chips,) = _rs_chip_exchange([_pair_sum(own, from_sibling, "rs_pair_sum_a_w_in")])
    g_sum["a_w_in"] = _slab_sum(from_chips, "rs_slab_sum_a_w_in")

    def conv_w_grad(a):
        return _unpad_shard_cols(a.transpose(1, 0, 2).reshape(CONV_TAPS, 2 * fp), ncw, ncp)

    def conv_b_grad(a):
        return _unpad_shard_cols(a.reshape(2 * fp), ncw, ncp)

    dmod_vec = [dmods[f"{nm}_{l}"] for l in range(2) for nm in mod_names] + [dmods["kv_sh"], dmods["kv_sc"]]
    post = _pack_small(dmod_vec + [dlb, g["a_norm_g"], g["k_norm_g"], g["q_norm_g"],
                                   jnp.pad(g["kv_b_f"].reshape(-1), (0, LANES - nh)),
                                   conv_w_grad(g["conv_w0"]), conv_w_grad(g["conv_w1"]),
                                   conv_b_grad(g["conv_b0"]), conv_b_grad(g["conv_b1"])])
    (post_all,) = _all_gather([post], "gather_small_grads")
    tot = _slab_sum(post_all, "small_grad_sum").reshape(-1)
    nmod = 14 * d
    (t_mod, t_lb, t_ang, t_kng, t_qng, t_bf, t_cw, t_cb) = _unpack_small(
        tot, [(nmod,), (1, d), (1, HEAD), (HEAD,), (1, HEAD), (LANES,), (2, CONV_TAPS, two_f), (2, two_f)])
    dm_all = post_all.reshape(NDEV, -1)[:, :nmod]
    dm0 = lax.dynamic_slice_in_dim(dm_all[:, :6 * d], me * n0, n0, axis=1)
    dm1 = lax.dynamic_slice_in_dim(dm_all[:, 6 * d:12 * d], me * n0, n0, axis=1)
    dkv = lax.dynamic_slice_in_dim(dm_all[:, 12 * d:], me * nkv, nkv, axis=1)
    g_ada_w, g_kv_ada_w, g_logits = _ada_bwd(c_act, _pad_rows(dm0, 2 * NDEV), _pad_rows(dm1, 2 * NDEV),
                                              _pad_rows(dkv, 2 * NDEV), lb, t_lb)

    grads = {
        "ada_w": g_ada_w,
        "ada_b": t_mod[:12 * d].reshape(2, 6 * d),
        "a_w_in": g_sum["a_w_in"].reshape(a_w_in.shape),
        "a_lb_logits": lax.dynamic_slice_in_dim(g_logits, me * HEAD, HEAD, axis=1),
        "a_norm_g": t_ang,
        "a_w_out": g_sum["a_w_out"].reshape(a_w_out.shape),
        "kv_ada_w": g_kv_ada_w,
        "kv_ada_b": t_mod[12 * d:],
        "kv_w": g_sum["kv_w"],
        "kv_b_f": t_bf[:nh],
        "k_norm_g": t_kng,
        "b_w_q": g_sum["b_w_q"].reshape(b_w_q.shape),
        "q_norm_g": t_qng,
        "b_w_out": g_sum["b_w_out"].reshape(b_w_out.shape),
        "ffn_w_up": jnp.stack([g_sum["up0"][:, :ncw], g_sum["up1"][:, :ncw]]),
        "ffn_conv_w": lax.dynamic_slice_in_dim(t_cw, me * ncw, ncw, axis=2),
        "ffn_conv_b": t_cb,
        "ffn_w_down": jnp.stack([g_sum["down0"], g_sum["down1"]]),
    }

    big_adam = ["ada_w", "a_w_in", "a_w_out", "kv_ada_w", "kv_w", "b_w_q", "b_w_out", "ffn_w_up", "ffn_w_down"]
    small_adam = [n for n in order if n not in big_adam]
    delta, new_m, new_v = {}, {}, {}
    for n in big_adam:
        shp = weights[n].shape
        two_d = lambda a: a.reshape(-1, shp[-1])
        dl, mn, vn = _adamw(two_d(weights[n]), two_d(grads[n]), two_d(m_in[n]), two_d(v_in[n]), f"adamw_{n}")
        delta[n], new_m[n], new_v[n] = dl.reshape(shp), mn.reshape(shp), vn.reshape(shp)
    packs = [_pack_small([src[n] for n in small_adam]) for src in (weights, grads, m_in, v_in)]
    outs = _adamw(*packs, "adamw_small", tr=packs[0].shape[0])
    shapes = [weights[n].shape for n in small_adam]
    for dst, o in zip((delta, new_m, new_v), outs):
        for n, a in zip(small_adam, _unpack_small(o.reshape(-1), shapes)):
            dst[n] = a

    return (loss, grad_x.reshape(x.shape), *[grads[n] for n in order], *[delta[n] for n in order],
            *[new_m[n] for n in order], *[new_v[n] for n in order])
```

```python
import functools

import jax
import jax.numpy as jnp
from jax import lax
from jax.experimental import pallas as pl
from jax.experimental.pallas import tpu as pltpu

F32 = jnp.float32
BF16 = jnp.bfloat16

NDEV = 8
NCHIP = 4
HEAD = 128
A_CHUNK = 64
CONV_TAPS = 3
EPS = 1e-6
NEG_INF = -1e30
LANES = 128
VMEM_LIMIT = 48 * 1024 * 1024

ADAM_LR = 0.001
ADAM_B1 = 0.9
ADAM_B2 = 0.999
ADAM_EPS = 1e-08
ADAM_WD = 0.01
ADAM_STEP = 10

_NN = (((1,), (0,)), ((), ()))
_NT = (((1,), (1,)), ((), ()))
_TN = (((0,), (0,)), ((), ()))
_MESH = pl.DeviceIdType.MESH


def _cparams(**kw):
    return pltpu.CompilerParams(vmem_limit_bytes=VMEM_LIMIT, **kw)


def _divisor_tile(n, pref, unit=LANES):
    if n <= pref:
        return n
    best = None
    for t in range(unit, pref + 1, unit):
        if n % t == 0:
            best = t
    assert best is not None, (n, pref)
    return best


def _round_up(n, unit):
    return -(-n // unit) * unit


def _bdot_raw(a, b, dims):
    return lax.dot_general(a.astype(BF16), b.astype(BF16), dims, preferred_element_type=F32)


@jax.custom_vjp
def _dot_nn(a, b):
    return _bdot_raw(a, b, _NN)


@jax.custom_vjp
def _dot_nt(a, b):
    return _bdot_raw(a, b, _NT)


@jax.custom_vjp
def _dot_tn(a, b):
    return _bdot_raw(a, b, _TN)


_dot_nn.defvjp(lambda a, b: (_bdot_raw(a, b, _NN), (a, b)),
               lambda r, g: (_dot_nt(g, r[1]), _dot_tn(r[0], g)))
_dot_nt.defvjp(lambda a, b: (_bdot_raw(a, b, _NT), (a, b)),
               lambda r, g: (_dot_nn(g, r[1]), _dot_tn(g, r[0])))
_dot_tn.defvjp(lambda a, b: (_bdot_raw(a, b, _TN), (a, b)),
               lambda r, g: (_dot_nt(r[1], g), _dot_nn(r[0], g)))


def _f32dot(a, b):
    return lax.dot_general(a, b, _NN, precision=lax.Precision.HIGHEST, preferred_element_type=F32)


def _sigmoid(x):
    return jax.nn.sigmoid(x)


def _silu(x):
    return x * jax.nn.sigmoid(x)


def _rms(x):
    return x * lax.rsqrt(jnp.mean(x * x, axis=-1, keepdims=True) + EPS)


def _modulate(x, sh, sc):
    return _rms(x) * (1.0 + sc) + sh


def _mm_call(a, b, dims, a_spec, b_spec, o_spec, o_shape, grid, acc_tile, name):
    nk = grid[2]

    def body(a_ref, b_ref, o_ref, *acc):
        p = lax.dot_general(a_ref[...].astype(BF16), b_ref[...].astype(BF16), dims,
                            preferred_element_type=F32)
        if nk == 1:
            o_ref[...] = p.astype(o_ref.dtype)
        else:
            kk = pl.program_id(2)

            @pl.when(kk == 0)
            def _():
                acc[0][...] = p

            @pl.when(kk > 0)
            def _():
                acc[0][...] += p

            @pl.when(kk == nk - 1)
            def _():
                o_ref[...] = acc[0][...].astype(o_ref.dtype)

    return pl.pallas_call(
        body, name=name, grid=grid, in_specs=[a_spec, b_spec], out_specs=o_spec, out_shape=o_shape,
        scratch_shapes=[pltpu.VMEM(acc_tile, F32)] if nk > 1 else [],
        compiler_params=_cparams(dimension_semantics=("parallel", "parallel", "arbitrary")),
    )(a, b)


def _mm(a, b, mode, out_dtype, name, tm=1024, tn=1024, tk=2048):
    if mode == "nn":
        (m, k), (k2, n) = a.shape, b.shape
    elif mode == "nt":
        (m, k), (n, k2) = a.shape, b.shape
    else:
        (k, m), (k2, n) = a.shape, b.shape
    assert k == k2, (a.shape, b.shape, mode)
    tm, tn, tk = _divisor_tile(m, tm), _divisor_tile(n, tn), _divisor_tile(k, tk)
    if mode == "tn":
        a_spec = pl.BlockSpec((tk, tm), lambda i, j, kk: (kk, i))
    else:
        a_spec = pl.BlockSpec((tm, tk), lambda i, j, kk: (i, kk))
    if mode == "nt":
        b_spec = pl.BlockSpec((tn, tk), lambda i, j, kk: (j, kk))
    else:
        b_spec = pl.BlockSpec((tk, tn), lambda i, j, kk: (kk, j))
    return _mm_call(a, b, {"nn": _NN, "nt": _NT, "tn": _TN}[mode], a_spec, b_spec,
                    pl.BlockSpec((tm, tn), lambda i, j, kk: (i, j)), jax.ShapeDtypeStruct((m, n), out_dtype),
                    (m // tm, n // tn, k // tk), (tm, tn), name)


def _wblk_act_spec(rows, gb, nl, split, nb, row_axis, blk_axis):
    if split == 1:
        return pl.BlockSpec((rows, gb * nl), lambda *g: (g[row_axis], g[blk_axis]))
    groups = nb // split // gb
    return pl.BlockSpec((None, rows, gb * nl),
                        lambda *g: (g[blk_axis] // groups, g[row_axis], g[blk_axis] % groups))


def _mm_wblk(a, wb, out_dtype, name, *, gb, row_off=0, split=1, tm=1024):
    m, k = a.shape
    nb, _, nl = wb.shape
    assert (nb // split) % gb == 0
    tm = _divisor_tile(m, tm)

    def body(a_ref, b_ref, o_ref):
        av = a_ref[...].astype(BF16)
        for s in range(gb):
            o_ref[:, s * nl:(s + 1) * nl] = lax.dot_general(
                av, b_ref[s].astype(BF16), _NN, preferred_element_type=F32).astype(o_ref.dtype)

    o_shape = (m, nb * nl) if split == 1 else (split, m, nb // split * nl)
    return pl.pallas_call(
        body, name=name, grid=(m // tm, nb // gb),
        in_specs=[pl.BlockSpec((tm, k), lambda i, j: (i, 0)),
                  pl.BlockSpec((gb, k, nl), lambda i, j: (j, row_off, 0))],
        out_specs=_wblk_act_spec(tm, gb, nl, split, nb, 0, 1),
        out_shape=jax.ShapeDtypeStruct(o_shape, out_dtype),
        compiler_params=_cparams(dimension_semantics=("parallel", "parallel")),
    )(a, wb)


def _mm_wblk_dx(dy, wb, out_dtype, name, *, k, gb, row_off=0, split=1, tm=1024):
    nb, _, nl = wb.shape
    assert (nb // split) % gb == 0
    m = dy.shape[-2]
    tm = _divisor_tile(m, tm)
    nk = nb // gb

    def body(a_ref, b_ref, o_ref, *acc):
        p = None
        for s in range(gb):
            q = lax.dot_general(a_ref[:, s * nl:(s + 1) * nl].astype(BF16), b_ref[s].astype(BF16), _NT,
                                preferred_element_type=F32)
            p = q if p is None else p + q
        if nk == 1:
            o_ref[...] = p.astype(o_ref.dtype)
        else:
            kk = pl.program_id(1)

            @pl.when(kk == 0)
            def _():
                acc[0][...] = p

            @pl.when(kk > 0)
            def _():
                acc[0][...] += p

            @pl.when(kk == nk - 1)
            def _():
                o_ref[...] = acc[0][...].astype(o_ref.dtype)

    return pl.pallas_call(
        body, name=name, grid=(m // tm, nk),
        in_specs=[_wblk_act_spec(tm, gb, nl, split, nb, 0, 1),
                  pl.BlockSpec((gb, k, nl), lambda i, kk: (kk, row_off, 0))],
        out_specs=pl.BlockSpec((tm, k), lambda i, kk: (i, 0)),
        out_shape=jax.ShapeDtypeStruct((m, k), out_dtype),
        scratch_shapes=[pltpu.VMEM((tm, k), F32)] if nk > 1 else [],
        compiler_params=_cparams(dimension_semantics=("parallel", "arbitrary")),
    )(dy, wb)


def _mm_wblk_dw(x, dy, name, *, nb, gb, split=1, tk=1024):
    t, k = x.shape
    assert (nb // split) % gb == 0
    nl = dy.shape[-1] * split // nb
    tk = _divisor_tile(t, tk)
    nk = t // tk

    def body(a_ref, b_ref, o_ref, acc):
        kk = pl.program_id(1)
        av = a_ref[...].astype(BF16)
        for s in range(gb):
            p = lax.dot_general(av, b_ref[:, s * nl:(s + 1) * nl].astype(BF16), _TN, preferred_element_type=F32)

            @pl.when(kk == 0)
            def _():
                acc[s] = p

            @pl.when(kk > 0)
            def _():
                acc[s] += p

        @pl.when(kk == nk - 1)
        def _():
            o_ref[...] = acc[...].astype(o_ref.dtype)

    return pl.pallas_call(
        body, name=name, grid=(nb // gb, nk),
        in_specs=[pl.BlockSpec((tk, k), lambda j, kk: (kk, 0)), _wblk_act_spec(tk, gb, nl, split, nb, 1, 0)],
        out_specs=pl.BlockSpec((gb, k, nl), lambda j, kk: (j, 0, 0)),
        out_shape=jax.ShapeDtypeStruct((nb, k, nl), BF16),
        scratch_shapes=[pltpu.VMEM((gb, k, nl), F32)],
        compiler_params=_cparams(dimension_semantics=("parallel", "arbitrary")),
    )(x, dy)


def _row_specs(rows, tb, nsub):
    return [pl.BlockSpec((tb, nsub * cw), functools.partial(lambda i, off: (i, off), off=off))
            for (_, cw, off) in rows]


def _vec_specs(params):
    return [pl.BlockSpec(p.shape, lambda i: (0, 0)) for p in params]


def _row_fwd(f, rows, params, out_dtypes, *, nsub=1, tb, name):
    t = rows[0][0].shape[0]
    tb = min(tb, t)
    n_r, n_p = len(rows), len(params)
    blk = [jax.ShapeDtypeStruct((tb, cw), F32) for (_, cw, _) in rows]
    blk += [jax.ShapeDtypeStruct(p.shape, F32) for p in params]
    out_avals = jax.eval_shape(f, *blk)

    def body(*refs):
        pv = [r[...] for r in refs[n_r:n_r + n_p]]
        for s in range(nsub):
            vals = [r[:, s * cw:(s + 1) * cw].astype(F32) for r, (_, cw, _) in zip(refs[:n_r], rows)]
            outs = f(*vals, *pv)
            for o_ref, o in zip(refs[n_r + n_p:], outs):
                w = o.shape[1]
                o_ref[:, s * w:(s + 1) * w] = o.astype(o_ref.dtype)

    return pl.pallas_call(
        body, name=name,
        grid=(t // tb,),
        in_specs=_row_specs(rows, tb, nsub) + _vec_specs(params),
        out_specs=[pl.BlockSpec((tb, nsub * av.shape[1]), lambda i: (i, 0)) for av in out_avals],
        out_shape=[jax.ShapeDtypeStruct((t, nsub * av.shape[1]), dt) for av, dt in zip(out_avals, out_dtypes)],
        compiler_params=_cparams(dimension_semantics=("parallel",)),
    )(*[r[0] for r in rows], *params)


def _row_bwd(f, rows, params, cots, row_grad_dtypes, *, nsub=1, tb, name, add_to=None):
    t = rows[0][0].shape[0]
    tb = min(tb, t)
    n_r, n_p, n_c = len(rows), len(params), len(cots)
    want = [j for j in range(n_r) if row_grad_dtypes[j] is not None]
    extra = [] if add_to is None else [(add_to[1], rows[add_to[0]][1], 0)]

    def body(*refs):
        i = pl.program_id(0)
        r_in, p_in = refs[:n_r], refs[n_r:n_r + n_p]
        c_in = refs[n_r + n_p:n_r + n_p + n_c]
        e_in = refs[n_r + n_p + n_c:n_r + n_p + n_c + len(extra)]
        outs = refs[n_r + n_p + n_c + len(extra):]
        pv = [r[...] for r in p_in]
        psum = [None] * n_p
        for s in range(nsub):
            vals = [r[:, s * cw:(s + 1) * cw].astype(F32) for r, (_, cw, _) in zip(r_in, rows)]
            cvals = tuple(r[:, s * cw:(s + 1) * cw].astype(F32) for r, (_, cw, _) in zip(c_in, cots))
            _, vjp_fn = jax.vjp(f, *vals, *pv)
            grads = vjp_fn(cvals)
            for o_ref, jr in zip(outs[:len(want)], want):
                cw = rows[jr][1]
                gr = grads[jr]
                if add_to is not None and jr == add_to[0]:
                    gr = gr + e_in[0][:, s * cw:(s + 1) * cw]
                o_ref[:, s * cw:(s + 1) * cw] = gr.astype(o_ref.dtype)
            for jp in range(n_p):
                psum[jp] = grads[n_r + jp] if psum[jp] is None else psum[jp] + grads[n_r + jp]
        for o_ref, g in zip(outs[len(want):], psum):
            @pl.when(i == 0)
            def _():
                o_ref[...] = g

            @pl.when(i > 0)
            def _():
                o_ref[...] += g

    out_specs = [pl.BlockSpec((tb, nsub * rows[jr][1]), lambda i: (i, 0)) for jr in want]
    out_shape = [jax.ShapeDtypeStruct((t, nsub * rows[jr][1]), row_grad_dtypes[jr]) for jr in want]
    out_specs += _vec_specs(params)
    out_shape += [jax.ShapeDtypeStruct(p.shape, F32) for p in params]
    res = pl.pallas_call(
        body, name=name,
        grid=(t // tb,),
        in_specs=_row_specs(rows, tb, nsub) + _vec_specs(params) + _row_specs(cots, tb, nsub)
        + _row_specs(extra, tb, nsub),
        out_specs=out_specs, out_shape=out_shape,
        compiler_params=_cparams(dimension_semantics=("arbitrary",)),
    )(*[r[0] for r in rows], *params, *[c[0] for c in cots], *[e[0] for e in extra])
    return res[:len(want)], res[len(want):]


def _f_mod(x, sh, sc):
    return (_modulate(x, sh, sc),)


def _f_res_mod(x, y, g, sh, sc):
    x1 = x + g * y
    return x1, _modulate(x1, sh, sc)


def _f_res_mod2(x, y, g, sh_a, sc_a, sh_b, sc_b):
    x1 = x + g * y
    return x1, _modulate(x1, sh_a, sc_a), _modulate(x1, sh_b, sc_b)


def _f_qnorm(p, g):
    return (_rms(p) * g * (HEAD ** -0.5),)


def _f_knorm(p, g):
    return (_rms(p) * g,)


def _f_outgate(o, og):
    return (o * _sigmoid(og),)


def _loss_call(x3, f, g2, target, tb):
    t, d = x3.shape
    tb = min(tb, t)

    def body(x_ref, f_ref, g_ref, t_ref, loss_ref, dx_ref, df_ref, dg_ref):
        i = pl.program_id(0)
        fv = f_ref[...]
        g = g_ref[...]
        e = x_ref[...] + g * fv - t_ref[...]
        dx = e * (1.0 / d)
        part = 0.5 * jnp.sum(jnp.sum(e * dx, axis=1, keepdims=True), axis=0, keepdims=True)
        dx_ref[...] = dx
        df_ref[...] = (g * dx).astype(df_ref.dtype)
        dg = jnp.sum(dx * fv, axis=0, keepdims=True)

        @pl.when(i == 0)
        def _():
            loss_ref[...] = jnp.broadcast_to(part, loss_ref.shape)
            dg_ref[...] = dg

        @pl.when(i > 0)
        def _():
            loss_ref[...] += jnp.broadcast_to(part, loss_ref.shape)
            dg_ref[...] += dg

    row = pl.BlockSpec((tb, d), lambda i: (i, 0))
    vec = pl.BlockSpec((1, d), lambda i: (0, 0))
    return pl.pallas_call(
        body, name="loss_head",
        grid=(t // tb,),
        in_specs=[row, row, vec, row],
        out_specs=[pl.BlockSpec((1, LANES), lambda i: (0, 0)), row, row, vec],
        out_shape=[jax.ShapeDtypeStruct((1, LANES), F32), jax.ShapeDtypeStruct((t, d), F32),
                   jax.ShapeDtypeStruct((t, d), BF16), jax.ShapeDtypeStruct((1, d), F32)],
        compiler_params=_cparams(dimension_semantics=("arbitrary",)),
    )(x3, f, g2, target)


def _hg_consts(tb):
    c = A_CHUNK
    r = lax.broadcasted_iota(jnp.int32, (c, c), 0)
    s = lax.broadcasted_iota(jnp.int32, (c, c), 1)
    br = lax.broadcasted_iota(jnp.int32, (tb, tb), 0)
    bs = lax.broadcasted_iota(jnp.int32, (tb, tb), 1)
    shift = c.bit_length() - 1
    same_chunk = jnp.right_shift(br, shift) == jnp.right_shift(bs, shift)
    return (s <= r).astype(F32), (r <= s).astype(F32), jnp.logical_and(same_chunk, bs <= br)


def _chunk_apply(mat, x):
    c = mat.shape[0]
    return jnp.concatenate([_f32dot(mat, x[i * c:(i + 1) * c]) for i in range(x.shape[0] // c)], axis=0)


@jax.custom_vjp
def _chunk_cumsum(x, tri, tri_t):
    return _chunk_apply(tri, x)


_chunk_cumsum.defvjp(lambda x, tri, tri_t: (_chunk_apply(tri, x), (tri, tri_t)),
                     lambda r, g: (_chunk_apply(r[1], g), jnp.zeros_like(r[0]), jnp.zeros_like(r[1])))


def _per_chunk(a, b, dims):
    return jnp.stack([_bdot_raw(a[i], b[i], dims) for i in range(a.shape[0])])


@jax.custom_vjp
def _chunk_tn(a, b):
    return _per_chunk(a, b, _TN)


@jax.custom_vjp
def _chunk_nt(a, b):
    return _per_chunk(a, b, _NT)


@jax.custom_vjp
def _chunk_nn(a, b):
    return _per_chunk(a, b, _NN)


_chunk_tn.defvjp(lambda a, b: (_per_chunk(a, b, _TN), (a, b)),
                 lambda r, g: (_chunk_nt(r[1], g), _chunk_nn(r[0], g)))
_chunk_nt.defvjp(lambda a, b: (_per_chunk(a, b, _NT), (a, b)),
                 lambda r, g: (_chunk_nn(g, r[1]), _chunk_tn(g, r[0])))
_chunk_nn.defvjp(lambda a, b: (_per_chunk(a, b, _NN), (a, b)),
                 lambda r, g: (_chunk_nt(g, r[1]), _chunk_tn(r[0], g)))


def _scan_states(decay, m, st):
    sts = []
    for i in range(m.shape[0]):
        sts.append(st)
        st = st * decay[i] + m[i]
    return jnp.stack(sts), st


@jax.custom_vjp
def _state_scan(decay, m, st):
    return _scan_states(decay, m, st)


def _state_scan_fwd(decay, m, st):
    sts, st_out = _scan_states(decay, m, st)
    return (sts, st_out), (decay, sts)


def _state_scan_bwd(res, cts):
    decay, sts = res
    d_sts, g = cts
    d_decay, d_m = [], []
    for i in range(sts.shape[0] - 1, -1, -1):
        d_m.append(g)
        d_decay.append(jnp.sum(g * sts[i], axis=0, keepdims=True))
        g = g * decay[i] + d_sts[i]
    return jnp.stack(d_decay[::-1]), jnp.stack(d_m[::-1]), g


_state_scan.defvjp(_state_scan_fwd, _state_scan_bwd)


def _hg_block(qp, fp, ip, gp, lb, ng, st, tri, tri_t, bd_causal):
    tb = qp.shape[0]
    c = A_CHUNK
    n = tb // c
    q = _silu(qp)
    fg = lb + (1.0 - lb) * _sigmoid(fp)
    logf = jnp.log(fg)
    k = 1.0 - fg
    b3 = _chunk_cumsum(logf, tri, tri_t).reshape(n, c, HEAD)
    pos = lax.broadcasted_iota(jnp.int32, (1, c, 1), 1)
    b_mid = lax.stop_gradient(jnp.sum(jnp.where(pos == c // 2, b3, 0.0), axis=1, keepdims=True))
    b_last = jnp.sum(jnp.where(pos == c - 1, b3, 0.0), axis=1, keepdims=True)
    q3, k3, v3 = q.reshape(n, c, HEAD), k.reshape(n, c, HEAD), ip.reshape(n, c, HEAD)
    scores = _dot_nt((q3 * jnp.exp(b3 - b_mid)).reshape(tb, HEAD), (k3 * jnp.exp(b_mid - b3)).reshape(tb, HEAD))
    o_intra = _dot_nn(jnp.where(bd_causal, scores, 0.0), ip)
    states, st_new = _state_scan(jnp.exp(b_last), _chunk_tn(v3, k3 * jnp.exp(b_last - b3)), st)
    o = o_intra + _chunk_nt(q3 * jnp.exp(b3), states).reshape(tb, HEAD)
    y = _rms(o) * ng * _silu(gp)
    return y, st_new


def _hg_specs(tb, nh, rev_nb=None):
    def row(off):
        if rev_nb is None:
            return pl.BlockSpec((tb, HEAD), functools.partial(lambda h, i, off: (i, off + h), off=off))
        return pl.BlockSpec((tb, HEAD), functools.partial(lambda h, i, off: (rev_nb - 1 - i, off + h), off=off))
    return [row(0), row(nh), row(2 * nh), row(3 * nh),
            pl.BlockSpec((1, HEAD), lambda h, i: (0, h)), pl.BlockSpec((1, HEAD), lambda h, i: (0, 0))]


def _hgrn2_fwd(proj, lb, ng, tb):
    t = proj.shape[0]
    nh = proj.shape[1] // (4 * HEAD)
    tb = min(tb, t)
    nb = t // tb

    def body(q_ref, f_ref, i_ref, g_ref, lb_ref, ng_ref, y_ref, s_ref, st_ref):
        i = pl.program_id(1)

        @pl.when(i == 0)
        def _():
            st_ref[...] = jnp.zeros_like(st_ref)

        st = st_ref[...]
        s_ref[0, 0] = st
        y, st_new = _hg_block(q_ref[...], f_ref[...], i_ref[...], g_ref[...], lb_ref[...], ng_ref[...], st,
                              *_hg_consts(tb))
        y_ref[...] = y.astype(y_ref.dtype)
        st_ref[...] = st_new

    return pl.pallas_call(
        body, name="hgrn2_fwd",
        grid=(nh, nb),
        in_specs=_hg_specs(tb, nh),
        out_specs=[pl.BlockSpec((tb, HEAD), lambda h, i: (i, h)),
                   pl.BlockSpec((1, 1, HEAD, HEAD), lambda h, i: (h, i, 0, 0))],
        out_shape=[jax.ShapeDtypeStruct((t, nh * HEAD), BF16),
                   jax.ShapeDtypeStruct((nh, nb, HEAD, HEAD), F32)],
        scratch_shapes=[pltpu.VMEM((HEAD, HEAD), F32)],
        compiler_params=_cparams(dimension_semantics=("parallel", "arbitrary")),
    )(proj, proj, proj, proj, lb, ng)


def _hgrn2_bwd(proj, lb, ng, states, dy, tb):
    t = proj.shape[0]
    nh = proj.shape[1] // (4 * HEAD)
    tb = min(tb, t)
    nb = t // tb

    def body(q_ref, f_ref, i_ref, g_ref, lb_ref, ng_ref, s_ref, dy_ref,
             dq_ref, df_ref, di_ref, dg_ref, dlb_ref, dng_ref, dst_ref):
        h, i = pl.program_id(0), pl.program_id(1)
        consts = _hg_consts(tb)

        @pl.when(i == 0)
        def _():
            dst_ref[...] = jnp.zeros_like(dst_ref)
            dlb_ref[...] = jnp.zeros_like(dlb_ref)

        @pl.when(jnp.logical_and(i == 0, h == 0))
        def _():
            dng_ref[...] = jnp.zeros_like(dng_ref)

        def fn(qp, fp, ip, gp, lbx, ngx, stx):
            return _hg_block(qp, fp, ip, gp, lbx, ngx, stx, *consts)

        _, vjp_fn = jax.vjp(fn, q_ref[...], f_ref[...], i_ref[...], g_ref[...], lb_ref[...], ng_ref[...],
                            s_ref[0, 0])
        gq, gf, gi, gg, glb, gng, dst = vjp_fn((dy_ref[...].astype(F32), dst_ref[...]))
        dq_ref[...] = gq.astype(dq_ref.dtype)
        df_ref[...] = gf.astype(df_ref.dtype)
        di_ref[...] = gi.astype(di_ref.dtype)
        dg_ref[...] = gg.astype(dg_ref.dtype)
        dst_ref[...] = dst
        dlb_ref[...] += glb
        dng_ref[...] += gng

    rev = lambda h, i: (nb - 1 - i, h)
    slab = jax.ShapeDtypeStruct((t, nh * HEAD), BF16)
    return pl.pallas_call(
        body, name="hgrn2_bwd",
        grid=(nh, nb),
        in_specs=_hg_specs(tb, nh, rev_nb=nb) + [
            pl.BlockSpec((1, 1, HEAD, HEAD), lambda h, i: (h, nb - 1 - i, 0, 0)),
            pl.BlockSpec((tb, HEAD), rev)],
        out_specs=[pl.BlockSpec((tb, HEAD), rev)] * 4 + [
            pl.BlockSpec((1, HEAD), lambda h, i: (0, h)), pl.BlockSpec((1, HEAD), lambda h, i: (0, 0))],
        out_shape=[slab, slab, slab, slab,
                   jax.ShapeDtypeStruct((1, nh * HEAD), F32), jax.ShapeDtypeStruct((1, HEAD), F32)],
        scratch_shapes=[pltpu.VMEM((HEAD, HEAD), F32)],
        compiler_params=_cparams(dimension_semantics=("arbitrary", "arbitrary")),
    )(proj, proj, proj, proj, lb, ng, states, dy)


def _fgate_consts(cb):
    r = lax.broadcasted_iota(jnp.int32, (cb, cb), 0)
    s = lax.broadcasted_iota(jnp.int32, (cb, cb), 1)
    return (r <= s).astype(F32), (r >= s).astype(F32)


def _fgate_fwd(xt, bias, cb=512):
    nh, t = xt.shape
    cb = min(cb, t)

    def body(x_ref, b_ref, o_ref):
        upper, _ = _fgate_consts(cb)
        carry = jnp.zeros((nh, 1), F32)
        for blk in range(t // cb):
            z = x_ref[:, blk * cb:(blk + 1) * cb] + b_ref[...]
            logf = jnp.minimum(z, 0.0) - jnp.log(1.0 + jnp.exp(-jnp.abs(z)))
            cs = _f32dot(logf, upper) + carry
            o_ref[:, blk * cb:(blk + 1) * cb] = cs
            carry = cs[:, cb - 1:cb]

    vm = pl.BlockSpec(memory_space=pltpu.VMEM)
    return pl.pallas_call(
        body, name="fgate_fwd", in_specs=[vm, vm], out_specs=vm,
        out_shape=jax.ShapeDtypeStruct((nh, t), F32), compiler_params=_cparams(),
    )(xt, bias)


def _fgate_bwd(xt, bias, dft, cb=512):
    nh, t = xt.shape
    cb = min(cb, t)
    nblk = t // cb

    def body(x_ref, b_ref, d_ref, dx_ref, db_ref):
        _, lower = _fgate_consts(cb)
        carry = jnp.zeros((nh, 1), F32)
        db = jnp.zeros((nh, 1), F32)
        for blk in range(nblk - 1, -1, -1):
            sl = slice(blk * cb, (blk + 1) * cb)
            dlogf = _f32dot(d_ref[:, sl], lower) + carry
            carry = dlogf[:, 0:1]
            z = x_ref[:, sl] + b_ref[...]
            dz = dlogf * (1.0 - _sigmoid(z))
            dx_ref[:, sl] = dz
            db = db + jnp.sum(dz, axis=1, keepdims=True)
        db_ref[...] = db

    vm = pl.BlockSpec(memory_space=pltpu.VMEM)
    return pl.pallas_call(
        body, name="fgate_bwd", in_specs=[vm, vm, vm], out_specs=[vm, vm],
        out_shape=[jax.ShapeDtypeStruct((nh, t), F32), jax.ShapeDtypeStruct((nh, 1), F32)],
        compiler_params=_cparams(),
    )(xt, bias, dft)


ATTN_ROWS = 32


def _row_chunks(blk, fn):
    for r in range(blk // ATTN_ROWS):
        fn(pl.ds(r * ATTN_ROWS, ATTN_ROWS), r * ATTN_ROWS)


def _chunk_causal(first_row, blk):
    rows = first_row + lax.broadcasted_iota(jnp.int32, (ATTN_ROWS, blk), 0)
    return lax.broadcasted_iota(jnp.int32, (ATTN_ROWS, blk), 1) <= rows


def _attn_fwd(q, k, v, f_col, f_row, blk):
    t, width = q.shape
    nh = width // HEAD
    nq = t // blk

    def body(q_ref, k_ref, v_ref, fc_ref, fr_ref, o_ref, lse_ref):
        i = pl.program_id(0)
        tri = (lax.broadcasted_iota(jnp.int32, (blk, blk), 1) <= lax.broadcasted_iota(jnp.int32, (blk, blk), 0))
        for h in range(nh):
            cs = slice(h * HEAD, (h + 1) * HEAD)
            qh = q_ref[:, cs]
            fq = fc_ref[:, h:h + 1]

            def tile(j, carry, masked):
                m, l, acc = carry
                rs = pl.ds(pl.multiple_of(j * blk, blk), blk)
                s = _bdot_raw(qh, k_ref[rs, cs], _NT) + (fq - fr_ref[j, h:h + 1, :])
                if masked:
                    s = jnp.where(tri, s, NEG_INF)
                m_new = jnp.maximum(m, jnp.max(s, axis=1, keepdims=True))
                p = jnp.exp(s - m_new)
                alpha = jnp.exp(m - m_new)
                l_new = alpha * l + jnp.sum(p, axis=1, keepdims=True)
                acc_new = alpha * acc + _bdot_raw(p, v_ref[rs, cs], _NN)
                return m_new, l_new, acc_new

            init = (jnp.full((blk, 1), NEG_INF, F32), jnp.zeros((blk, 1), F32), jnp.zeros((blk, HEAD), F32))
            carry = lax.fori_loop(0, i, lambda j, c: tile(j, c, False), init)
            m, l, acc = tile(i, carry, True)
            o_ref[:, cs] = acc / l
            lse_ref[:, h:h + 1] = m + jnp.log(l)

    vm = pl.BlockSpec(memory_space=pltpu.VMEM)
    return pl.pallas_call(
        body, name="fox_attn_fwd",
        grid=(nq,),
        in_specs=[pl.BlockSpec((blk, width), lambda i: (i, 0)), vm, vm,
                  pl.BlockSpec((blk, nh), lambda i: (i, 0)), vm],
        out_specs=[pl.BlockSpec((blk, width), lambda i: (i, 0)), pl.BlockSpec((blk, nh), lambda i: (i, 0))],
        out_shape=[jax.ShapeDtypeStruct((t, width), F32), jax.ShapeDtypeStruct((t, nh), F32)],
        compiler_params=_cparams(dimension_semantics=("parallel",)),
    )(q, k, v, f_col, f_row)


def _attn_bwd_dq(q, k, v, f_col, f_row, o, do, lse, blk):
    t, width = q.shape
    nh = width // HEAD
    nq = t // blk

    def body(q_ref, k_ref, v_ref, fc_ref, fr_ref, o_ref, do_ref, lse_ref, dq_ref, dfc_ref, dl_ref):
        i = pl.program_id(0)
        tri = (lax.broadcasted_iota(jnp.int32, (blk, blk), 1) <= lax.broadcasted_iota(jnp.int32, (blk, blk), 0))
        for h in range(nh):
            cs = slice(h * HEAD, (h + 1) * HEAD)
            qh = q_ref[:, cs]
            doh = do_ref[:, cs]
            bias = fc_ref[:, h:h + 1] - lse_ref[:, h:h + 1]
            delta = jnp.sum(doh.astype(F32) * o_ref[:, cs], axis=1, keepdims=True)

            def tile(j, carry, masked):
                dq, dfq = carry
                rs = pl.ds(pl.multiple_of(j * blk, blk), blk)
                kj = k_ref[rs, cs]
                p = jnp.exp(_bdot_raw(qh, kj, _NT) + (bias - fr_ref[j, h:h + 1, :]))
                if masked:
                    p = jnp.where(tri, p, 0.0)
                ds = p * (_bdot_raw(doh, v_ref[rs, cs], _NT) - delta)
                return dq + _bdot_raw(ds, kj, _NN), dfq + jnp.sum(ds, axis=1, keepdims=True)

            carry = lax.fori_loop(0, i, lambda j, c: tile(j, c, False),
                                  (jnp.zeros((blk, HEAD), F32), jnp.zeros((blk, 1), F32)))
            dq, dfq = tile(i, carry, True)
            dq_ref[:, cs] = dq
            dfc_ref[:, h:h + 1] = dfq
            dl_ref[:, h:h + 1] = delta

    vm = pl.BlockSpec(memory_space=pltpu.VMEM)
    wide = pl.BlockSpec((blk, width), lambda i: (i, 0))
    thin = pl.BlockSpec((blk, nh), lambda i: (i, 0))
    return pl.pallas_call(
        body, name="fox_attn_bwd_dq",
        grid=(nq,),
        in_specs=[wide, vm, vm, thin, vm, wide, wide, thin],
        out_specs=[wide, thin, thin],
        out_shape=[jax.ShapeDtypeStruct((t, width), F32), jax.ShapeDtypeStruct((t, nh), F32),
                   jax.ShapeDtypeStruct((t, nh), F32)],
        compiler_params=_cparams(dimension_semantics=("parallel",)),
    )(q, k, v, f_col, f_row, o, do, lse)


def _attn_bwd_dkv(q, k, v, f_col, f_row, do, lse, delta, blk):
    t, width = q.shape
    nh = width // HEAD
    nq = t // blk

    def body(q_ref, k_ref, v_ref, fc_ref, fr_ref, do_ref, lse_ref, dl_ref, dk_ref, dv_ref, dfr_ref,
             s_ref, dp_ref, p_ref, ds_ref, dfs_ref, dk_acc, dv_acc):
        j = pl.program_id(0)
        for h in range(nh):
            cs = slice(h * HEAD, (h + 1) * HEAD)
            fs = fr_ref[0, h:h + 1, :]
            dfs_ref[...] = jnp.zeros_like(dfs_ref)
            dk_acc[...] = jnp.zeros_like(dk_acc)
            dv_acc[...] = jnp.zeros_like(dv_acc)

            def tile(i, masked):
                base = pl.multiple_of(i * blk, blk)
                rs = pl.ds(base, blk)
                s_ref[...] = _bdot_raw(q_ref[rs, cs], k_ref[:, cs], _NT)
                dp_ref[...] = _bdot_raw(do_ref[rs, cs], v_ref[:, cs], _NT)

                def rows(rr, first_row):
                    gr = pl.ds(pl.multiple_of(base + first_row, ATTN_ROWS), ATTN_ROWS)
                    bias = fc_ref[gr, h:h + 1] - lse_ref[gr, h:h + 1]
                    p = jnp.exp(s_ref[rr, :] + (bias - fs))
                    if masked:
                        p = jnp.where(_chunk_causal(first_row, blk), p, 0.0)
                    ds = p * (dp_ref[rr, :] - dl_ref[gr, h:h + 1])
                    dfs_ref[...] -= jnp.sum(ds, axis=0, keepdims=True)
                    p_ref[rr, :] = p.astype(p_ref.dtype)
                    ds_ref[rr, :] = ds.astype(ds_ref.dtype)

                _row_chunks(blk, rows)
                dv_acc[...] += _bdot_raw(p_ref[...], do_ref[rs, cs], _TN)
                dk_acc[...] += _bdot_raw(ds_ref[...], q_ref[rs, cs], _TN)

            def off_diagonal(i, carry):
                tile(i, False)
                return carry

            tile(j, True)
            lax.fori_loop(j + 1, nq, off_diagonal, 0)
            dk_ref[:, cs] = dk_acc[...]
            dv_ref[:, cs] = dv_acc[...]
            dfr_ref[0, h:h + 1, :] = dfs_ref[...]

    vm = pl.BlockSpec(memory_space=pltpu.VMEM)
    wide = pl.BlockSpec((blk, width), lambda j: (j, 0))
    frow = pl.BlockSpec((1, nh, blk), lambda j: (j, 0, 0))
    tile_f32, tile_b16 = pltpu.VMEM((blk, blk), F32), pltpu.VMEM((blk, blk), BF16)
    return pl.pallas_call(
        body, name="fox_attn_bwd_dkv",
        grid=(nq,),
        in_specs=[vm, wide, wide, vm, frow, vm, vm, vm],
        out_specs=[wide, wide, frow],
        out_shape=[jax.ShapeDtypeStruct((t, width), F32), jax.ShapeDtypeStruct((t, width), F32),
                   jax.ShapeDtypeStruct((nq, nh, blk), F32)],
        scratch_shapes=[tile_f32, tile_f32, tile_b16, tile_b16, pltpu.VMEM((1, blk), F32),
                        pltpu.VMEM((blk, HEAD), F32), pltpu.VMEM((blk, HEAD), F32)],
        compiler_params=_cparams(dimension_semantics=("parallel",)),
    )(q, k, v, f_col, f_row, do, lse, delta)


def _shift_down(u, n):
    row = lax.broadcasted_iota(jnp.int32, u.shape, 0)
    return jnp.where(row < n, 0.0, pltpu.roll(u, n, 0))


def _shift_up(u, n):
    t = u.shape[0]
    row = lax.broadcasted_iota(jnp.int32, u.shape, 0)
    return jnp.where(row >= t - n, 0.0, pltpu.roll(u, t - n, 0))


def _convglu_specs(t):
    return [pl.BlockSpec((2, t, LANES), lambda j: (0, 0, j)),
            pl.BlockSpec((2, CONV_TAPS, LANES), lambda j: (0, 0, j)),
            pl.BlockSpec((2, 1, LANES), lambda j: (0, 0, j))]


def _convglu_fwd(u, cw, cb):
    _, t, fp = u.shape

    def body(u_ref, w_ref, b_ref, a_ref):
        c = []
        for hf in range(2):
            uv, w = u_ref[hf], w_ref[hf]
            c.append(w[0:1] * _shift_down(uv, 2) + w[1:2] * _shift_down(uv, 1) + w[2:3] * uv + b_ref[hf])
        a_ref[...] = (_silu(c[0]) * c[1]).astype(a_ref.dtype)

    return pl.pallas_call(
        body, name="convglu_fwd",
        grid=(fp // LANES,),
        in_specs=_convglu_specs(t),
        out_specs=pl.BlockSpec((t, LANES), lambda j: (0, j)),
        out_shape=jax.ShapeDtypeStruct((t, fp), BF16),
        compiler_params=_cparams(dimension_semantics=("parallel",)),
    )(u, cw, cb)


def _convglu_bwd(u, cw, cb, da):
    _, t, fp = u.shape

    def body(u_ref, w_ref, b_ref, da_ref, du_ref, dw_ref, db_ref):
        us, c = [], []
        for hf in range(2):
            uv, w = u_ref[hf], w_ref[hf]
            u1, u2 = _shift_down(uv, 1), _shift_down(uv, 2)
            us.append((uv, u1, u2))
            c.append(w[0:1] * u2 + w[1:2] * u1 + w[2:3] * uv + b_ref[hf])
        gc, vc = c
        sg = _sigmoid(gc)
        dav = da_ref[...].astype(F32)
        dcs = [dav * vc * (sg * (1.0 + gc * (1.0 - sg))), dav * (gc * sg)]
        for hf in range(2):
            dc, w = dcs[hf], w_ref[hf]
            uv, u1, u2 = us[hf]
            du = w[2:3] * dc + w[1:2] * _shift_up(dc, 1) + w[0:1] * _shift_up(dc, 2)
            du_ref[hf] = du.astype(du_ref.dtype)
            dw_ref[hf, 0:1, :] = jnp.sum(dc * u2, axis=0, keepdims=True)
            dw_ref[hf, 1:2, :] = jnp.sum(dc * u1, axis=0, keepdims=True)
            dw_ref[hf, 2:3, :] = jnp.sum(dc * uv, axis=0, keepdims=True)
            db_ref[hf] = jnp.sum(dc, axis=0, keepdims=True)

    specs = _convglu_specs(t)
    return pl.pallas_call(
        body, name="convglu_bwd",
        grid=(fp // LANES,),
        in_specs=specs + [pl.BlockSpec((t, LANES), lambda j: (0, j))],
        out_specs=specs,
        out_shape=[jax.ShapeDtypeStruct((2, t, fp), BF16), jax.ShapeDtypeStruct((2, CONV_TAPS, fp), F32),
                   jax.ShapeDtypeStruct((2, 1, fp), F32)],
        compiler_params=_cparams(dimension_semantics=("parallel",)),
    )(u, cw, cb, da)


def _local_step(x, target, mods, lb, small, get_w, put_g, *, tb=512, attn_blk=512):
    t, d = x.shape
    nh = d // HEAD
    nb = NDEV
    wts = {}
    vec = lambda *names: [mods[n] for n in names]

    def ffn_fwd(h2, l):
        u = _mm_wblk(h2, wts[f"up{l}"], F32, f"ffn{l}_up", gb=nb // 2, split=2, tm=512)
        a = _convglu_fwd(u, small[f"conv_w{l}"], small[f"conv_b{l}"])
        f = _mm(a, wts[f"down{l}"], "nn", F32, f"ffn{l}_down", tk=4096)
        return u, a, f

    def ffn_bwd(df, h2, u, a, l):
        da = _mm(df, wts[f"down{l}"], "nt", BF16, f"ffn{l}_down_dx", tn=1536)
        dwd = _mm(a, df, "tn", BF16, f"ffn{l}_down_dw", tm=1536, tk=1024)
        du, dcw, dcb = _convglu_bwd(u, small[f"conv_w{l}"], small[f"conv_b{l}"], da)
        dh2 = _mm_wblk_dx(du, wts[f"up{l}"], F32, f"ffn{l}_up_dx", k=d, gb=nb // 2, split=2, tm=512)
        dwu = _mm_wblk_dw(h2, du, f"ffn{l}_up_dw", nb=nb, gb=nb // 4, split=2, tk=2048)
        return dh2, dwu, dwd, dcw, dcb

    (h_a,) = _row_fwd(_f_mod, [(x, d, 0)], vec("sh1_0", "sc1_0"), [BF16], tb=tb, name="l0_mod1")
    wts.update(get_w("l0a", h_a))
    proj_a = _mm_wblk(h_a, wts["a_in"], F32, "a_in", gb=nb // 2)
    ypre, states = _hgrn2_fwd(proj_a, lb, small["a_norm_g"], tb)
    wts.update(get_w("l0b", ypre))
    y_a = _mm(ypre, wts["a_out"], "nn", F32, "a_out")
    x1, h2_0 = _row_fwd(_f_res_mod, [(x, d, 0), (y_a, d, 0)], vec("g1_0", "sh2_0", "sc2_0"), [F32, BF16],
                        tb=tb, name="l0_res_mod2")
    u0, a0, f0 = ffn_fwd(h2_0, 0)
    x2, h_kv, h_q = _row_fwd(_f_res_mod2, [(x1, d, 0), (f0, d, 0)],
                             vec("g2_0", "kv_sh", "kv_sc", "sh1_1", "sc1_1"), [F32, BF16, BF16],
                             tb=tb, name="l0_res_kvmod_qmod")
    wts.update(get_w("l1", h_kv))
    proj_kv = _mm(h_kv, wts["kv"], "nn", F32, "kv_proj")
    proj_f = _mm(h_kv, wts["kv_f"], "nn", F32, "kv_fproj")
    (k_n,) = _row_fwd(_f_knorm, [(proj_kv, HEAD, 0)], [small["k_norm_g"]], [BF16], nsub=nh, tb=tb, name="k_norm")
    v_b = proj_kv[:, d:].astype(BF16)
    f_logit_t = proj_f[:, :nh].T
    f_bias = small["kv_b_f"].reshape(nh, 1)
    f_t = _fgate_fwd(f_logit_t, f_bias)
    f_col = f_t.T
    f_row = f_t.reshape(nh, t // attn_blk, attn_blk).transpose(1, 0, 2)
    proj_q = _mm_wblk(h_q, wts["b_q"], F32, "b_q", gb=nb)
    (q_n,) = _row_fwd(_f_qnorm, [(proj_q, HEAD, 0)], [small["q_norm_g"]], [BF16], nsub=nh, tb=tb, name="q_norm")
    o_att, lse = _attn_fwd(q_n, k_n, v_b, f_col, f_row, attn_blk)
    (z,) = _row_fwd(_f_outgate, [(o_att, HEAD, 0), (proj_q, HEAD, 1)], [], [BF16], nsub=nh, tb=tb, name="out_gate")
    y_b = _mm(z, wts["b_out"], "nn", F32, "b_out")
    x3, h2_1 = _row_fwd(_f_res_mod, [(x2, d, 0), (y_b, d, 0)], vec("g1_1", "sh2_1", "sc2_1"), [F32, BF16],
                        tb=tb, name="l1_res_mod2")
    u1, a1, f1 = ffn_fwd(h2_1, 1)
    loss, dx4, df1, dg2_1 = _loss_call(x3, f1, mods["g2_1"], target, tb)

    g = {}
    dmods = {"g2_1": dg2_1}
    dh2, g["up1"], g["down1"], g["conv_w1"], g["conv_b1"] = ffn_bwd(df1, h2_1, u1, a1, 1)
    (dx2, dy_b), (dmods["g1_1"], dmods["sh2_1"], dmods["sc2_1"]) = _row_bwd(
        _f_res_mod, [(x2, d, 0), (y_b, d, 0)], vec("g1_1", "sh2_1", "sc2_1"),
        [(dx4, d, 0), (dh2, d, 0)], [F32, BF16], tb=tb, name="l1_res_mod2_bwd")
    dz = _mm(dy_b, wts["b_out"], "nt", F32, "b_out_dx")
    g["b_out"] = _mm(z, dy_b, "tn", BF16, "b_out_dw", tk=1024)
    (do_att, dog), _ = _row_bwd(_f_outgate, [(o_att, HEAD, 0), (proj_q, HEAD, 1)], [], [(dz, HEAD, 0)],
                                [BF16, BF16], nsub=nh, tb=tb, name="out_gate_bwd")
    dq_n, dfc_q, delta = _attn_bwd_dq(q_n, k_n, v_b, f_col, f_row, o_att, do_att, lse, attn_blk)
    dk_n, dv, dfr_k = _attn_bwd_dkv(q_n, k_n, v_b, f_col, f_row, do_att, lse, delta, attn_blk)
    (dpq,), (g["q_norm_g"],) = _row_bwd(_f_qnorm, [(proj_q, HEAD, 0)], [small["q_norm_g"]],
                                        [(dq_n, HEAD, 0)], [BF16], nsub=nh, tb=tb, name="q_norm_bwd")
    dproj_q = jnp.concatenate([dpq, dog], axis=1)
    dh_q = _mm_wblk_dx(dproj_q, wts["b_q"], F32, "b_q_dx", k=d, gb=nb)
    g["b_q"] = _mm_wblk_dw(h_q, dproj_q, "b_q_dw", nb=nb, gb=nb)
    (dpk,), (g["k_norm_g"],) = _row_bwd(_f_knorm, [(proj_kv, HEAD, 0)], [small["k_norm_g"]],
                                        [(dk_n, HEAD, 0)], [BF16], nsub=nh, tb=tb, name="k_norm_bwd")
    dproj_kv = jnp.concatenate([dpk, dv.astype(BF16)], axis=1)
    df_t = dfc_q.T + dfr_k.transpose(1, 0, 2).reshape(nh, t)
    dflogit_t, g["kv_b_f"] = _fgate_bwd(f_logit_t, f_bias, df_t)
    dproj_f = jnp.pad(dflogit_t.T, ((0, 0), (0, LANES - nh))).astype(BF16)
    dh_kv = _mm(dproj_kv, wts["kv"], "nt", F32, "kv_proj_dx") + _mm(dproj_f, wts["kv_f"], "nt", F32, "kv_fproj_dx")
    g["kv"] = _mm(h_kv, dproj_kv, "tn", BF16, "kv_proj_dw", tk=1024)
    g["kv_f"] = _mm(h_kv, dproj_f, "tn", F32, "kv_fproj_dw", tk=1024)
    sent = put_g("l1", {n: g.pop(n) for n in ("b_out", "b_q", "kv", "kv_f", "up1", "down1")})
    (dx1, df0), (dmods["g2_0"], dmods["kv_sh"], dmods["kv_sc"], dmods["sh1_1"], dmods["sc1_1"]) = _row_bwd(
        _f_res_mod2, [(x1, d, 0), (f0, d, 0)], [mods["g2_0"] + sent] + vec("kv_sh", "kv_sc", "sh1_1", "sc1_1"),
        [(dx2, d, 0), (dh_kv, d, 0), (dh_q, d, 0)], [F32, BF16], tb=tb, name="l0_res_kvmod_qmod_bwd")
    dh2, g["up0"], g["down0"], g["conv_w0"], g["conv_b0"] = ffn_bwd(df0, h2_0, u0, a0, 0)
    (dx0, dy_a), (dmods["g1_0"], dmods["sh2_0"], dmods["sc2_0"]) = _row_bwd(
        _f_res_mod, [(x, d, 0), (y_a, d, 0)], vec("g1_0", "sh2_0", "sc2_0"),
        [(dx1, d, 0), (dh2, d, 0)], [F32, BF16], tb=tb, name="l0_res_mod2_bwd")
    dypre = _mm(dy_a, wts["a_out"], "nt", BF16, "a_out_dx")
    g["a_out"] = _mm(ypre, dy_a, "tn", BF16, "a_out_dw", tk=1024)
    sent = put_g("l0b", {n: g.pop(n) for n in ("a_out", "up0", "down0")})
    dpa_q, dpa_f, dpa_i, dpa_g, dlb, g["a_norm_g"] = _hgrn2_bwd(proj_a, lb + sent, small["a_norm_g"], states,
                                                               dypre, tb)
    dproj_a = jnp.concatenate([dpa_q, dpa_f, dpa_i, dpa_g], axis=1)
    dh_a = _mm_wblk_dx(dproj_a, wts["a_in"], F32, "a_in_dx", k=d, gb=nb, tm=512)
    put_g("l0a", {"a_in": _mm_wblk_dw(h_a, dproj_a, "a_in_dw", nb=nb, gb=nb // 4, tk=2048)})
    (grad_x,), (dmods["sh1_0"], dmods["sc1_0"]) = _row_bwd(
        _f_mod, [(x, d, 0)], vec("sh1_0", "sc1_0"), [(dh_a, d, 0)], [F32], tb=tb, name="l0_mod1_bwd",
        add_to=(0, dx0))
    return loss, grad_x, dmods, dlb, g


def _position():
    return lax.axis_index("x"), lax.axis_index("y"), lax.axis_index("c")


def _hbm_specs(n):
    return [pl.BlockSpec(memory_space=pl.ANY)] * n


def _all_gather(arrs, name):
    n = len(arrs)

    def body(*refs):
        x_refs, out_refs = refs[:n], refs[n:2 * n]
        send_sems, recv_sems, local_sems = refs[2 * n:]
        x, y, cc = _position()
        me, sibling = (x, y, cc), (x, y, 1 - cc)
        chips = [(1 - x, y), (x, 1 - y), (1 - x, 1 - y)]

        def copy(a, k, block, to, src=None):
            slot = out_refs[a].at[4 * block[0] + 2 * block[1] + block[2]]
            return pltpu.make_async_remote_copy(
                src_ref=slot if src is None else src, dst_ref=slot,
                send_sem=send_sems.at[7 * a + k], recv_sem=recv_sems.at[7 * a + k],
                device_id=to, device_id_type=_MESH)

        local = [pltpu.make_async_copy(x_refs[a], out_refs[a].at[4 * x + 2 * y + cc], local_sems.at[a])
                 for a in range(n)]
        for cp in local:
            cp.start()
        first = []
        for a in range(n):
            first.append(copy(a, 0, me, sibling, src=x_refs[a]))
            first += [copy(a, 1 + j, me, (*chip, cc), src=x_refs[a]) for j, chip in enumerate(chips)]
        for cp in first:
            cp.start()
        passed = []
        for j, chip in enumerate(chips):
            for a in range(n):
                copy(a, 1 + j, (*chip, cc), me).wait_recv()
                fwd = copy(a, 4 + j, (*chip, cc), sibling)
                fwd.start()
                passed.append(fwd)
        for a in range(n):
            copy(a, 0, sibling, me).wait_recv()
        for j, chip in enumerate(chips):
            for a in range(n):
                copy(a, 4 + j, (*chip, 1 - cc), me).wait_recv()
        for cp in first + passed:
            cp.wait_send()
        for cp in local:
            cp.wait()

    return pl.pallas_call(
        body, name=name,
        out_shape=[jax.ShapeDtypeStruct((NDEV, *a.shape), a.dtype) for a in arrs],
        in_specs=_hbm_specs(n), out_specs=_hbm_specs(n),
        scratch_shapes=[pltpu.SemaphoreType.DMA((7 * n,)), pltpu.SemaphoreType.DMA((7 * n,)),
                        pltpu.SemaphoreType.DMA((n,))],
    )(*arrs)


_XCHG_EFFECT = pltpu.SideEffectType.DATAFLOW_SIDE_EFFECTING
ALL_PEERS = (1, 2, 3, 4, 5, 6, 7)
SAME_CORE = (2, 4, 6)


def _xchg_copies(src_refs, land_refs, send_sems, recv_sems, local_sems, scatter, rels):
    x, y, cc = _position()
    me = 4 * x + 2 * y + cc
    remote, local = [], []
    for a, (src, land) in enumerate(zip(src_refs, land_refs)):
        local.append(pltpu.make_async_copy(src.at[me] if scatter else src, land.at[me], local_sems.at[a]))
        for idx, rel in enumerate(rels):
            px = 1 - x if rel & 4 else x
            py = 1 - y if rel & 2 else y
            pc = 1 - cc if rel & 1 else cc
            k = len(rels) * a + idx
            remote.append(pltpu.make_async_remote_copy(
                src_ref=src.at[4 * px + 2 * py + pc] if scatter else src, dst_ref=land.at[me],
                send_sem=send_sems.at[k], recv_sem=recv_sems.at[k], device_id=(px, py, pc), device_id_type=_MESH))
    return remote, local


def _xchg_start(srcs, scatter, rels, name):
    n = len(srcs)
    lands = [lax.empty(s.shape if scatter else (NDEV, *s.shape), s.dtype) for s in srcs]

    def body(*refs):
        remote, local = _xchg_copies(refs[:n], refs[n:2 * n], *refs[2 * n:2 * n + 3], scatter, rels)
        for cp in local + remote:
            cp.start()
        token = refs[-1]
        token[...] = jnp.zeros_like(token)

    hbm = pl.BlockSpec(memory_space=pltpu.HBM)
    sem = pl.BlockSpec(memory_space=pltpu.SEMAPHORE)
    out = pl.pallas_call(
        body, name=name,
        out_shape=(pltpu.SemaphoreType.DMA((len(rels) * n,)), pltpu.SemaphoreType.DMA((len(rels) * n,)),
                   pltpu.SemaphoreType.DMA((n,)),
                   *[pltpu.HBM(a.shape, a.dtype) for a in srcs + lands], jax.ShapeDtypeStruct((8, LANES), F32)),
        in_specs=[hbm] * (2 * n),
        out_specs=(sem, sem, sem, *[hbm] * (2 * n), pl.BlockSpec(memory_space=pltpu.VMEM)),
        input_output_aliases={i: 3 + i for i in range(2 * n)},
        compiler_params=pltpu.CompilerParams(has_side_effects=_XCHG_EFFECT),
    )(*[pltpu.with_memory_space_constraint(a, pltpu.HBM) for a in srcs + lands])
    return out[:-1], out[-1][0, 0]


def _xchg_wait(handles, after, scatter, rels, name):
    n = (len(handles) - 3) // 2

    def body(*refs):
        remote, local = _xchg_copies(refs[:n], refs[n:2 * n], *refs[2 * n:2 * n + 3], scatter, rels)
        for cp in remote:
            cp.wait_send()
            cp.wait_recv()
        for cp in local:
            cp.wait()

    hbm = pl.BlockSpec(memory_space=pltpu.HBM)
    sem = pl.BlockSpec(memory_space=pltpu.SEMAPHORE)
    thru = list(handles[3:])
    out = pl.pallas_call(
        body, name=name,
        out_shape=tuple(pltpu.HBM(a.shape, a.dtype) for a in thru),
        in_specs=[hbm] * (2 * n) + [sem, sem, sem, pl.BlockSpec(memory_space=pl.ANY)],
        out_specs=tuple([hbm] * (2 * n)),
        input_output_aliases={i: i for i in range(2 * n)},
        compiler_params=pltpu.CompilerParams(has_side_effects=_XCHG_EFFECT),
    )(*thru, *handles[:3], after)
    return list(out[n:])


def _sibling_forward(lands, name):
    n = len(lands)

    def body(*refs):
        land_refs = refs[n:2 * n]
        send_sems, recv_sems = refs[2 * n:]
        x, y, cc = _position()

        def copy(a, q, core):
            slot = land_refs[a].at[2 * q + core]
            return pltpu.make_async_remote_copy(
                src_ref=slot, dst_ref=slot, send_sem=send_sems.at[NCHIP * a + q], recv_sem=recv_sems.at[NCHIP * a + q],
                device_id=(x, y, 1 - cc), device_id_type=_MESH)

        sends = [copy(a, q, cc) for a in range(n) for q in range(NCHIP)]
        for cp in sends:
            cp.start()
        for a in range(n):
            for q in range(NCHIP):
                copy(a, q, 1 - cc).wait_recv()
        for cp in sends:
            cp.wait_send()

    return pl.pallas_call(
        body, name=name,
        out_shape=[jax.ShapeDtypeStruct(a.shape, a.dtype) for a in lands],
        in_specs=_hbm_specs(n), out_specs=_hbm_specs(n),
        input_output_aliases={i: i for i in range(n)},
        scratch_shapes=[pltpu.SemaphoreType.DMA((NCHIP * n,)), pltpu.SemaphoreType.DMA((NCHIP * n,))],
    )(*lands)


def _rs_sibling_exchange(gs):
    n = len(gs)

    def body(*refs):
        g_refs, recv_refs = refs[:n], refs[n:2 * n]
        send_sems, recv_sems = refs[2 * n:]
        x, y, cc = _position()
        copies = [pltpu.make_async_remote_copy(
            src_ref=g_refs[a].at[q, 1 - cc], dst_ref=recv_refs[a].at[q], send_sem=send_sems.at[NCHIP * a + q],
            recv_sem=recv_sems.at[NCHIP * a + q], device_id=(x, y, 1 - cc), device_id_type=_MESH)
            for a in range(n) for q in range(NCHIP)]
        for cp in copies:
            cp.start()
        for cp in copies:
            cp.wait()

    return pl.pallas_call(
        body, name="rs_sibling_exchange",
        out_shape=[jax.ShapeDtypeStruct((NCHIP, *g.shape[2:]), g.dtype) for g in gs],
        in_specs=_hbm_specs(n), out_specs=_hbm_specs(n),
        scratch_shapes=[pltpu.SemaphoreType.DMA((NCHIP * n,)), pltpu.SemaphoreType.DMA((NCHIP * n,))],
    )(*gs)


def _rs_chip_exchange(parts):
    n = len(parts)

    def body(*refs):
        p_refs, recv_refs = refs[:n], refs[n:2 * n]
        send_sems, recv_sems, local_sems = refs[2 * n:]
        x, y, cc = _position()
        myq = 2 * x + y
        chips = [(1 - x, y), (x, 1 - y), (1 - x, 1 - y)]

        def copy(a, k, px, py, src_q, dst_q):
            return pltpu.make_async_remote_copy(
                src_ref=p_refs[a].at[src_q], dst_ref=recv_refs[a].at[dst_q], send_sem=send_sems.at[3 * a + k],
                recv_sem=recv_sems.at[3 * a + k], device_id=(px, py, cc), device_id_type=_MESH)

        local = [pltpu.make_async_copy(p_refs[a].at[myq], recv_refs[a].at[myq], local_sems.at[a]) for a in range(n)]
        for cp in local:
            cp.start()
        sends = [copy(a, k, px, py, 2 * px + py, myq) for a in range(n) for k, (px, py) in enumerate(chips)]
        for cp in sends:
            cp.start()
        for a in range(n):
            for k, (px, py) in enumerate(chips):
                copy(a, k, px, py, myq, 2 * px + py).wait_recv()
        for cp in sends:
            cp.wait_send()
        for cp in local:
            cp.wait()

    return pl.pallas_call(
        body, name="rs_chip_exchange",
        out_shape=[jax.ShapeDtypeStruct(p.shape, p.dtype) for p in parts],
        in_specs=_hbm_specs(n), out_specs=_hbm_specs(n),
        scratch_shapes=[pltpu.SemaphoreType.DMA((3 * n,)), pltpu.SemaphoreType.DMA((3 * n,)),
                        pltpu.SemaphoreType.DMA((n,))],
    )(*parts)


def _pair_sum(own, got, name):
    n, r, c = own.shape

    def body(a_ref, b_ref, o_ref):
        o_ref[...] = (a_ref[...].astype(F32) + b_ref[...].astype(F32)).astype(o_ref.dtype)

    spec = pl.BlockSpec((1, r, c), lambda q: (q, 0, 0))
    return pl.pallas_call(body, name=name, grid=(n,), in_specs=[spec, spec], out_specs=spec,
                          out_shape=jax.ShapeDtypeStruct((n, r, c), own.dtype),
                          compiler_params=_cparams(dimension_semantics=("parallel",)))(own, got)


def _slab_sum(slabs, name, tr=None):
    n, r, c = slabs.shape
    tr = r if tr is None else tr

    def body(s_ref, o_ref):
        acc = s_ref[0].astype(F32)
        for q in range(1, n):
            acc = acc + s_ref[q].astype(F32)
        o_ref[...] = acc

    return pl.pallas_call(body, name=name, grid=(r // tr,),
                          in_specs=[pl.BlockSpec((n, tr, c), lambda i: (0, i, 0))],
                          out_specs=pl.BlockSpec((tr, c), lambda i: (i, 0)),
                          out_shape=jax.ShapeDtypeStruct((r, c), F32),
                          compiler_params=_cparams(dimension_semantics=("parallel",)))(slabs)


def _ada_fwd(c_all, ada_w, kv_ada_w, logits):
    rows, d = c_all.shape
    n0, nkv = ada_w.shape[2], kv_ada_w.shape[1]

    def body(c_ref, w_ref, kw_ref, lg_ref, part_ref, cact_ref, lb_ref):
        ca = _silu(c_ref[...])
        cact_ref[...] = ca
        part_ref[:, 0:n0] = _bdot_raw(ca, w_ref[0], _NN)
        part_ref[:, n0:2 * n0] = _bdot_raw(ca, w_ref[1], _NN)
        part_ref[:, 2 * n0:2 * n0 + nkv] = _bdot_raw(ca, kw_ref[...], _NN)
        lb_ref[...] = _sigmoid(lg_ref[0:1, :] - lg_ref[1:2, :])

    vm = pl.BlockSpec(memory_space=pltpu.VMEM)
    return pl.pallas_call(
        body, name="ada_fwd", in_specs=[vm, vm, vm, vm], out_specs=[vm, vm, vm],
        out_shape=[jax.ShapeDtypeStruct((rows, 2 * n0 + nkv), F32), jax.ShapeDtypeStruct((rows, d), F32),
                   jax.ShapeDtypeStruct((1, d), F32)],
        compiler_params=_cparams(),
    )(c_all, ada_w, kv_ada_w, logits)


def _ada_bwd(c_act, dm0, dm1, dkv, lb, dlb):
    rows, d = c_act.shape

    def body(c_ref, d0_ref, d1_ref, dk_ref, lb_ref, dlb_ref, dw_ref, dkw_ref, dlg_ref):
        ca = c_ref[...]
        dw_ref[0] = _bdot_raw(ca, d0_ref[...], _TN)
        dw_ref[1] = _bdot_raw(ca, d1_ref[...], _TN)
        dkw_ref[...] = _bdot_raw(ca, dk_ref[...], _TN)
        lbv = lb_ref[...]
        dl0 = dlb_ref[...] * lbv * (1.0 - lbv)
        dlg_ref[0:1, :] = dl0
        dlg_ref[1:2, :] = -dl0

    vm = pl.BlockSpec(memory_space=pltpu.VMEM)
    return pl.pallas_call(
        body, name="ada_bwd", in_specs=[vm] * 6, out_specs=[vm, vm, vm],
        out_shape=[jax.ShapeDtypeStruct((2, d, dm0.shape[1]), F32), jax.ShapeDtypeStruct((d, dkv.shape[1]), F32),
                   jax.ShapeDtypeStruct((2, d), F32)],
        compiler_params=_cparams(),
    )(c_act, dm0, dm1, dkv, lb, dlb)


def _adamw(w, g, m, v, name, tr=512):
    r, c = w.shape
    tr = _divisor_tile(r, tr, unit=8)
    c1 = 1.0 - ADAM_B1 ** ADAM_STEP
    c2 = 1.0 - ADAM_B2 ** ADAM_STEP

    def body(w_ref, g_ref, m_ref, v_ref, d_ref, mo_ref, vo_ref):
        gv = g_ref[...]
        mn = ADAM_B1 * m_ref[...] + (1.0 - ADAM_B1) * gv
        vn = ADAM_B2 * v_ref[...] + (1.0 - ADAM_B2) * (gv * gv)
        d_ref[...] = -ADAM_LR * ((mn / c1) / (jnp.sqrt(vn / c2) + ADAM_EPS) + ADAM_WD * w_ref[...])
        mo_ref[...] = mn
        vo_ref[...] = vn

    spec = pl.BlockSpec((tr, c), lambda i: (i, 0))
    out = jax.ShapeDtypeStruct((r, c), F32)
    return pl.pallas_call(body, name=name, grid=(r // tr,), in_specs=[spec] * 4, out_specs=[spec] * 3,
                          out_shape=[out, out, out],
                          compiler_params=_cparams(dimension_semantics=("parallel",)))(w, g, m, v)


def _pad_rows(a, rows):
    return jnp.pad(a, ((0, rows - a.shape[0]), (0, 0)))


def _pack_small(parts, lanes=LANES, row_unit=8):
    flat = jnp.concatenate([p.reshape(-1).astype(F32) for p in parts])
    rows = _round_up(-(-flat.shape[0] // lanes), row_unit)
    return jnp.pad(flat, (0, rows * lanes - flat.shape[0])).reshape(rows, lanes)


def _unpack_small(flat, shapes):
    out, off = [], 0
    for s in shapes:
        n = 1
        for k in s:
            n *= k
        out.append(flat[off:off + n].reshape(s))
        off += n
    return out


def _pad_shard_cols(a, n_loc, n_pad):
    lead = a.shape[:-1]
    a = a.reshape(*lead, NDEV, n_loc)
    a = jnp.pad(a, [(0, 0)] * (len(lead) + 1) + [(0, n_pad - n_loc)])
    return a.reshape(*lead, NDEV * n_pad)


def _unpad_shard_cols(a, n_loc, n_pad):
    lead = a.shape[:-1]
    return a.reshape(*lead, NDEV, n_pad)[..., :n_loc].reshape(*lead, NDEV * n_loc)


def kernel(x, c, ada_w, ada_b, a_w_in, a_lb_logits, a_norm_g, a_w_out, kv_ada_w, kv_ada_b, kv_w, kv_b_f, k_norm_g, b_w_q, q_norm_g, b_w_out, ffn_w_up, ffn_conv_w, ffn_conv_b, ffn_w_down, loss_target, m_ada_w, m_ada_b, m_a_w_in, m_a_lb_logits, m_a_norm_g, m_a_w_out, m_kv_ada_w, m_kv_ada_b, m_kv_w, m_kv_b_f, m_k_norm_g, m_b_w_q, m_q_norm_g, m_b_w_out, m_ffn_w_up, m_ffn_conv_w, m_ffn_conv_b, m_ffn_w_down, v_ada_w, v_ada_b, v_a_w_in, v_a_lb_logits, v_a_norm_g, v_a_w_out, v_kv_ada_w, v_kv_ada_b, v_kv_w, v_kv_b_f, v_k_norm_g, v_b_w_q, v_q_norm_g, v_b_w_out, v_ffn_w_up, v_ffn_conv_w, v_ffn_conv_b, v_ffn_w_down):
    t, d = x.shape[1], x.shape[2]
    nh = d // HEAD
    ncw = ffn_w_up.shape[2]
    ncp = _round_up(ncw, LANES)
    two_f = ncw * NDEV
    ff = two_f // 2
    fp = ncp * NDEV // 2
    rd = ffn_w_down.shape[1]
    me = 4 * lax.axis_index("x") + 2 * lax.axis_index("y") + lax.axis_index("c")
    weights = dict(ada_w=ada_w, ada_b=ada_b, a_w_in=a_w_in, a_lb_logits=a_lb_logits, a_norm_g=a_norm_g,
                   a_w_out=a_w_out, kv_ada_w=kv_ada_w, kv_ada_b=kv_ada_b, kv_w=kv_w, kv_b_f=kv_b_f,
                   k_norm_g=k_norm_g, b_w_q=b_w_q, q_norm_g=q_norm_g, b_w_out=b_w_out, ffn_w_up=ffn_w_up,
                   ffn_conv_w=ffn_conv_w, ffn_conv_b=ffn_conv_b, ffn_w_down=ffn_w_down)
    m_in = dict(ada_w=m_ada_w, ada_b=m_ada_b, a_w_in=m_a_w_in, a_lb_logits=m_a_lb_logits, a_norm_g=m_a_norm_g,
                a_w_out=m_a_w_out, kv_ada_w=m_kv_ada_w, kv_ada_b=m_kv_ada_b, kv_w=m_kv_w, kv_b_f=m_kv_b_f,
                k_norm_g=m_k_norm_g, b_w_q=m_b_w_q, q_norm_g=m_q_norm_g, b_w_out=m_b_w_out, ffn_w_up=m_ffn_w_up,
                ffn_conv_w=m_ffn_conv_w, ffn_conv_b=m_ffn_conv_b, ffn_w_down=m_ffn_w_down)
    v_in = dict(ada_w=v_ada_w, ada_b=v_ada_b, a_w_in=v_a_w_in, a_lb_logits=v_a_lb_logits, a_norm_g=v_a_norm_g,
                a_w_out=v_a_w_out, kv_ada_w=v_kv_ada_w, kv_ada_b=v_kv_ada_b, kv_w=v_kv_w, kv_b_f=v_kv_b_f,
                k_norm_g=v_k_norm_g, b_w_q=v_b_w_q, q_norm_g=v_q_norm_g, b_w_out=v_b_w_out, ffn_w_up=v_ffn_w_up,
                ffn_conv_w=v_ffn_conv_w, ffn_conv_b=v_ffn_conv_b, ffn_w_down=v_ffn_w_down)
    order = list(weights)

    up_loc = jnp.pad(ffn_w_up, ((0, 0), (0, 0), (0, ncp - ncw))).astype(BF16)
    down_loc = ffn_w_down.astype(BF16)
    gather_names = {"l0b": ["a_out", "up0", "down0"], "l1": ["kv", "b_q", "b_out", "up1", "down1"]}
    shards = {"a_out": a_w_out[0].astype(BF16), "up0": up_loc[0], "down0": down_loc[0], "kv": kv_w.astype(BF16),
              "b_q": b_w_q[0].astype(BF16), "b_out": b_w_out[0].astype(BF16), "up1": up_loc[1],
              "down1": down_loc[1]}
    pre = _pack_small([c, a_lb_logits, ffn_conv_w])
    a_in_all, pre_all = _all_gather([a_w_in[0].astype(BF16), pre], "gather_a_w_in_and_small_inputs")
    pre_all = pre_all.reshape(NDEV, -1)
    c_all = pre_all[:, :d]
    logits = pre_all[:, d:d + 2 * HEAD].reshape(NDEV, 2, HEAD).transpose(1, 0, 2).reshape(2, d)
    conv_w_full = pre_all[:, d + 2 * HEAD:d + 2 * HEAD + 2 * CONV_TAPS * ncw]
    conv_w_full = conv_w_full.reshape(NDEV, 2, CONV_TAPS, ncw).transpose(1, 2, 0, 3).reshape(2, CONV_TAPS, two_f)

    part, c_act, lb = _ada_fwd(_pad_rows(c_all, 2 * NDEV), ada_w, kv_ada_w, logits)
    (part_all,) = _all_gather([part[:NDEV]], "gather_adaln")
    mine = lax.dynamic_index_in_dim(part_all, me, axis=1, keepdims=False)
    n0, nkv = ada_w.shape[2], kv_ada_w.shape[1]
    mod_names = ["sh1", "sc1", "g1", "sh2", "sc2", "g2"]
    mods = {}
    for l in range(2):
        row = mine[:, l * n0:(l + 1) * n0].reshape(-1) + ada_b[l]
        for k, nm in enumerate(mod_names):
            mods[f"{nm}_{l}"] = row[k * d:(k + 1) * d].reshape(1, d)
    kvrow = mine[:, 2 * n0:2 * n0 + nkv].reshape(-1) + kv_ada_b
    mods["kv_sh"], mods["kv_sc"] = kvrow[:d].reshape(1, d), kvrow[d:].reshape(1, d)

    in_flight, dep = {}, part_all
    for grp in ("l0b", "l1"):
        srcs = [shards[n] for n in gather_names[grp]]
        srcs[0], dep = lax.optimization_barrier((srcs[0], dep))
        in_flight[grp], dep = _xchg_start(srcs, False, SAME_CORE, f"gather_{grp}_start")
    zero = dep
    mods["sh1_0"] = mods["sh1_0"] + zero

    small = {"a_norm_g": a_norm_g, "k_norm_g": k_norm_g.reshape(1, HEAD), "q_norm_g": q_norm_g, "kv_b_f": kv_b_f}
    for l in range(2):
        small[f"conv_w{l}"] = _pad_shard_cols(conv_w_full[l], ncw, ncp).reshape(CONV_TAPS, 2, fp).transpose(1, 0, 2)
        small[f"conv_b{l}"] = _pad_shard_cols(ffn_conv_b[l], ncw, ncp).reshape(2, 1, fp)

    def get_w(grp, after):
        if grp == "l0a":
            return {"a_in": a_in_all}
        arrived = _xchg_wait(in_flight[grp], after, False, SAME_CORE, f"gather_{grp}_wait")
        got = dict(zip(gather_names[grp], _sibling_forward(arrived, f"gather_{grp}_to_sibling")))
        out = {}
        for n, a in got.items():
            if n in ("a_out", "b_out"):
                out[n] = a.reshape(d, d)
            elif n in ("down0", "down1"):
                dn = a.reshape(NCHIP, ff // NCHIP, d)
                out[n] = jnp.pad(dn, ((0, 0), (0, ncp - ncw), (0, 0))).reshape(fp, d)
            elif n == "kv":
                kv_full = a.transpose(1, 0, 2).reshape(d, NDEV * kv_w.shape[1])
                out["kv"] = kv_full[:, :2 * d]
                out["kv_f"] = jnp.pad(kv_full[:, 2 * d:], ((0, 0), (0, LANES - nh)))
            else:
                out[n] = a
        return out

    scatter_flight, g_last = {}, {}

    def put_g(grp, gr):
        if grp == "l0a":
            g_last.update(gr)
            return zero
        if grp == "l1":
            g_kvw = jnp.concatenate([gr["kv"], gr["kv_f"][:, :nh].astype(BF16)], axis=1)
            arrs = {"kv_w": g_kvw.reshape(d, NDEV, kv_w.shape[1]).transpose(1, 0, 2), "b_w_q": gr["b_q"],
                    "b_w_out": gr["b_out"].reshape(NDEV, d // NDEV, d), "up1": gr["up1"],
                    "down1": gr["down1"].reshape(NCHIP, ncp, d)[:, :ncw].reshape(NDEV, rd, d)}
        else:
            arrs = {"a_w_out": gr["a_out"].reshape(NDEV, d // NDEV, d), "up0": gr["up0"],
                    "down0": gr["down0"].reshape(NCHIP, ncp, d)[:, :ncw].reshape(NDEV, rd, d)}
        handles, sent = _xchg_start(list(arrs.values()), True, ALL_PEERS, f"scatter_{grp}_start")
        scatter_flight[grp] = (list(arrs), handles)
        return sent

    loss_v, grad_x, dmods, dlb, g = _local_step(x[0], loss_target[0], mods, lb, small, get_w, put_g)
    loss = lax.psum(loss_v[0, 0], ("x", "y", "c"))

    g_sum = {}
    for grp in ("l1", "l0b"):
        names, handles = scatter_flight[grp]
        for nm, a in zip(names, _xchg_wait(handles, grad_x, True, ALL_PEERS, f"scatter_{grp}_wait")):
            g_sum[nm] = _slab_sum(a, f"rs_slab_sum_{nm}")
    a_in_parts = g_last["a_in"].reshape(NCHIP, 2, *g_last["a_in"].shape[1:])
    (from_sibling,) = _rs_sibling_exchange([a_in_parts])
    own = lax.dynamic_index_in_dim(a_in_parts, lax.axis_index("c"), axis=1, keepdims=False)
    (from_chips,) = _rs_chip_exchange([_pair_sum(own, from_sibling, "rs_pair_sum_a_w_in")])
    g_sum["a_w_in"] = _slab_sum(from_chips, "rs_slab_sum_a_w_in")

    def conv_w_grad(a):
        return _unpad_shard_cols(a.transpose(1, 0, 2).reshape(CONV_TAPS, 2 * fp), ncw, ncp)

    def conv_b_grad(a):
        return _unpad_shard_cols(a.reshape(2 * fp), ncw, ncp)

    dmod_vec = [dmods[f"{nm}_{l}"] for l in range(2) for nm in mod_names] + [dmods["kv_sh"], dmods["kv_sc"]]
    post = _pack_small(dmod_vec + [dlb, g["a_norm_g"], g["k_norm_g"], g["q_norm_g"],
                                   jnp.pad(g["kv_b_f"].reshape(-1), (0, LANES - nh)),
                                   conv_w_grad(g["conv_w0"]), conv_w_grad(g["conv_w1"]),
                                   conv_b_grad(g["conv_b0"]), conv_b_grad(g["conv_b1"])])
    (post_all,) = _all_gather([post], "gather_small_grads")
    tot = _slab_sum(post_all, "small_grad_sum").reshape(-1)
    nmod = 14 * d
    (t_mod, t_lb, t_ang, t_kng, t_qng, t_bf, t_cw, t_cb) = _unpack_small(
        tot, [(nmod,), (1, d), (1, HEAD), (HEAD,), (1, HEAD), (LANES,), (2, CONV_TAPS, two_f), (2, two_f)])
    dm_all = post_all.reshape(NDEV, -1)[:, :nmod]
    dm0 = lax.dynamic_slice_in_dim(dm_all[:, :6 * d], me * n0, n0, axis=1)
    dm1 = lax.dynamic_slice_in_dim(dm_all[:, 6 * d:12 * d], me * n0, n0, axis=1)
    dkv = lax.dynamic_slice_in_dim(dm_all[:, 12 * d:], me * nkv, nkv, axis=1)
    g_ada_w, g_kv_ada_w, g_logits = _ada_bwd(c_act, _pad_rows(dm0, 2 * NDEV), _pad_rows(dm1, 2 * NDEV),
                                              _pad_rows(dkv, 2 * NDEV), lb, t_lb)

    grads = {
        "ada_w": g_ada_w,
        "ada_b": t_mod[:12 * d].reshape(2, 6 * d),
        "a_w_in": g_sum["a_w_in"].reshape(a_w_in.shape),
        "a_lb_logits": lax.dynamic_slice_in_dim(g_logits, me * HEAD, HEAD, axis=1),
        "a_norm_g": t_ang,
        "a_w_out": g_sum["a_w_out"].reshape(a_w_out.shape),
        "kv_ada_w": g_kv_ada_w,
        "kv_ada_b": t_mod[12 * d:],
        "kv_w": g_sum["kv_w"],
        "kv_b_f": t_bf[:nh],
        "k_norm_g": t_kng,
        "b_w_q": g_sum["b_w_q"].reshape(b_w_q.shape),
        "q_norm_g": t_qng,
        "b_w_out": g_sum["b_w_out"].reshape(b_w_out.shape),
        "ffn_w_up": jnp.stack([g_sum["up0"][:, :ncw], g_sum["up1"][:, :ncw]]),
        "ffn_conv_w": lax.dynamic_slice_in_dim(t_cw, me * ncw, ncw, axis=2),
        "ffn_conv_b": t_cb,
        "ffn_w_down": jnp.stack([g_sum["down0"], g_sum["down1"]]),
    }

    big_adam = ["ada_w", "a_w_in", "a_w_out", "kv_ada_w", "kv_w", "b_w_q", "b_w_out", "ffn_w_up", "ffn_w_down"]
    small_adam = [n for n in order if n not in big_adam]
    delta, new_m, new_v = {}, {}, {}
    for n in big_adam:
        shp = weights[n].shape
        two_d = lambda a: a.reshape(-1, shp[-1])
        dl, mn, vn = _adamw(two_d(weights[n]), two_d(grads[n]), two_d(m_in[n]), two_d(v_in[n]), f"adamw_{n}")
        delta[n], new_m[n], new_v[n] = dl.reshape(shp), mn.reshape(shp), vn.reshape(shp)
    packs = [_pack_small([src[n] for n in small_adam]) for src in (weights, grads, m_in, v_in)]
    outs = _adamw(*packs, "adamw_small", tr=packs[0].shape[0])
    shapes = [weights[n].shape for n in small_adam]
    for dst, o in zip((delta, new_m, new_v), outs):
        for n, a in zip(small_adam, _unpack_small(o.reshape(-1), shapes)):
            dst[n] = a

    return (loss, grad_x.reshape(x.shape), *[grads[n] for n in order], *[delta[n] for n in order],
            *[new_m[n] for n in order], *[new_v[n] for n in order])
```

```python
import functools

import jax
import jax.numpy as jnp
from jax import lax
from jax.experimental import pallas as pl
from jax.experimental.pallas import tpu as pltpu

F32 = jnp.float32
BF16 = jnp.bfloat16

NDEV = 8
NCHIP = 4
HEAD = 128
A_CHUNK = 64
CONV_TAPS = 3
EPS = 1e-6
NEG_INF = -1e30
LANES = 128
VMEM_LIMIT = 48 * 1024 * 1024

ADAM_LR = 0.001
ADAM_B1 = 0.9
ADAM_B2 = 0.999
ADAM_EPS = 1e-08
ADAM_WD = 0.01
ADAM_STEP = 10

_NN = (((1,), (0,)), ((), ()))
_NT = (((1,), (1,)), ((), ()))
_TN = (((0,), (0,)), ((), ()))
_MESH = pl.DeviceIdType.MESH


def _cparams(**kw):
    return pltpu.CompilerParams(vmem_limit_bytes=VMEM_LIMIT, **kw)


def _divisor_tile(n, pref, unit=LANES):
    if n <= pref:
        return n
    best = None
    for t in range(unit, pref + 1, unit):
        if n % t == 0:
            best = t
    assert best is not None, (n, pref)
    return best


def _round_up(n, unit):
    return -(-n // unit) * unit


def _bdot_raw(a, b, dims):
    return lax.dot_general(a.astype(BF16), b.astype(BF16), dims, preferred_element_type=F32)


@jax.custom_vjp
def _dot_nn(a, b):
    return _bdot_raw(a, b, _NN)


@jax.custom_vjp
def _dot_nt(a, b):
    return _bdot_raw(a, b, _NT)


@jax.custom_vjp
def _dot_tn(a, b):
    return _bdot_raw(a, b, _TN)


_dot_nn.defvjp(lambda a, b: (_bdot_raw(a, b, _NN), (a, b)),
               lambda r, g: (_dot_nt(g, r[1]), _dot_tn(r[0], g)))
_dot_nt.defvjp(lambda a, b: (_bdot_raw(a, b, _NT), (a, b)),
               lambda r, g: (_dot_nn(g, r[1]), _dot_tn(g, r[0])))
_dot_tn.defvjp(lambda a, b: (_bdot_raw(a, b, _TN), (a, b)),
               lambda r, g: (_dot_nt(r[1], g), _dot_nn(r[0], g)))


def _f32dot(a, b):
    return lax.dot_general(a, b, _NN, precision=lax.Precision.HIGHEST, preferred_element_type=F32)


def _sigmoid(x):
    return jax.nn.sigmoid(x)


def _silu(x):
    return x * jax.nn.sigmoid(x)


def _rms(x):
    return x * lax.rsqrt(jnp.mean(x * x, axis=-1, keepdims=True) + EPS)


def _modulate(x, sh, sc):
    return _rms(x) * (1.0 + sc) + sh


def _mm_call(a, b, dims, a_spec, b_spec, o_spec, o_shape, grid, acc_tile, name):
    nk = grid[2]

    def body(a_ref, b_ref, o_ref, *acc):
        p = lax.dot_general(a_ref[...].astype(BF16), b_ref[...].astype(BF16), dims,
                            preferred_element_type=F32)
        if nk == 1:
            o_ref[...] = p.astype(o_ref.dtype)
        else:
            kk = pl.program_id(2)

            @pl.when(kk == 0)
            def _():
                acc[0][...] = p

            @pl.when(kk > 0)
            def _():
                acc[0][...] += p

            @pl.when(kk == nk - 1)
            def _():
                o_ref[...] = acc[0][...].astype(o_ref.dtype)

    return pl.pallas_call(
        body, name=name, grid=grid, in_specs=[a_spec, b_spec], out_specs=o_spec, out_shape=o_shape,
        scratch_shapes=[pltpu.VMEM(acc_tile, F32)] if nk > 1 else [],
        compiler_params=_cparams(dimension_semantics=("parallel", "parallel", "arbitrary")),
    )(a, b)


def _mm(a, b, mode, out_dtype, name, tm=1024, tn=1024, tk=2048):
    if mode == "nn":
        (m, k), (k2, n) = a.shape, b.shape
    elif mode == "nt":
        (m, k), (n, k2) = a.shape, b.shape
    else:
        (k, m), (k2, n) = a.shape, b.shape
    assert k == k2, (a.shape, b.shape, mode)
    tm, tn, tk = _divisor_tile(m, tm), _divisor_tile(n, tn), _divisor_tile(k, tk)
    if mode == "tn":
        a_spec = pl.BlockSpec((tk, tm), lambda i, j, kk: (kk, i))
    else:
        a_spec = pl.BlockSpec((tm, tk), lambda i, j, kk: (i, kk))
    if mode == "nt":
        b_spec = pl.BlockSpec((tn, tk), lambda i, j, kk: (j, kk))
    else:
        b_spec = pl.BlockSpec((tk, tn), lambda i, j, kk: (kk, j))
    return _mm_call(a, b, {"nn": _NN, "nt": _NT, "tn": _TN}[mode], a_spec, b_spec,
                    pl.BlockSpec((tm, tn), lambda i, j, kk: (i, j)), jax.ShapeDtypeStruct((m, n), out_dtype),
                    (m // tm, n // tn, k // tk), (tm, tn), name)


def _wblk_act_spec(rows, gb, nl, split, nb, row_axis, blk_axis):
    if split == 1:
        return pl.BlockSpec((rows, gb * nl), lambda *g: (g[row_axis], g[blk_axis]))
    groups = nb // split // gb
    return pl.BlockSpec((None, rows, gb * nl),
                        lambda *g: (g[blk_axis] // groups, g[row_axis], g[blk_axis] % groups))


def _mm_wblk(a, wb, out_dtype, name, *, gb, row_off=0, split=1, tm=1024):
    m, k = a.shape
    nb, _, nl = wb.shape
    assert (nb // split) % gb == 0
    tm = _divisor_tile(m, tm)

    def body(a_ref, b_ref, o_ref):
        av = a_ref[...].astype(BF16)
        for s in range(gb):
            o_ref[:, s * nl:(s + 1) * nl] = lax.dot_general(
                av, b_ref[s].astype(BF16), _NN, preferred_element_type=F32).astype(o_ref.dtype)

    o_shape = (m, nb * nl) if split == 1 else (split, m, nb // split * nl)
    return pl.pallas_call(
        body, name=name, grid=(nb // gb, m // tm),
        in_specs=[pl.BlockSpec((tm, k), lambda j, i: (i, 0)),
                  pl.BlockSpec((gb, k, nl), lambda j, i: (j, row_off, 0))],
        out_specs=_wblk_act_spec(tm, gb, nl, split, nb, 1, 0),
        out_shape=jax.ShapeDtypeStruct(o_shape, out_dtype),
        compiler_params=_cparams(dimension_semantics=("parallel", "parallel")),
    )(a, wb)


def _mm_wblk_dx(dy, wb, out_dtype, name, *, k, gb, row_off=0, split=1, tm=1024):
    nb, _, nl = wb.shape
    assert (nb // split) % gb == 0
    m = dy.shape[-2]
    tm = _divisor_tile(m, tm)
    nk = nb // gb

    def body(a_ref, b_ref, o_ref, *acc):
        p = None
        for s in range(gb):
            q = lax.dot_general(a_ref[:, s * nl:(s + 1) * nl].astype(BF16), b_ref[s].astype(BF16), _NT,
                                preferred_element_type=F32)
            p = q if p is None else p + q
        if nk == 1:
            o_ref[...] = p.astype(o_ref.dtype)
        else:
            kk = pl.program_id(1)

            @pl.when(kk == 0)
            def _():
                acc[0][...] = p

            @pl.when(kk > 0)
            def _():
                acc[0][...] += p

            @pl.when(kk == nk - 1)
            def _():
                o_ref[...] = acc[0][...].astype(o_ref.dtype)

    return pl.pallas_call(
        body, name=name, grid=(m // tm, nk),
        in_specs=[_wblk_act_spec(tm, gb, nl, split, nb, 0, 1),
                  pl.BlockSpec((gb, k, nl), lambda i, kk: (kk, row_off, 0))],
        out_specs=pl.BlockSpec((tm, k), lambda i, kk: (i, 0)),
        out_shape=jax.ShapeDtypeStruct((m, k), out_dtype),
        scratch_shapes=[pltpu.VMEM((tm, k), F32)] if nk > 1 else [],
        compiler_params=_cparams(dimension_semantics=("parallel", "arbitrary")),
    )(dy, wb)


def _mm_wblk_dw(x, dy, name, *, nb, gb, split=1, tk=1024):
    t, k = x.shape
    assert (nb // split) % gb == 0
    nl = dy.shape[-1] * split // nb
    tk = _divisor_tile(t, tk)
    nk = t // tk

    def body(a_ref, b_ref, o_ref, acc):
        kk = pl.program_id(1)
        av = a_ref[...].astype(BF16)
        for s in range(gb):
            p = lax.dot_general(av, b_ref[:, s * nl:(s + 1) * nl].astype(BF16), _TN, preferred_element_type=F32)

            @pl.when(kk == 0)
            def _():
                acc[s] = p

            @pl.when(kk > 0)
            def _():
                acc[s] += p

        @pl.when(kk == nk - 1)
        def _():
            o_ref[...] = acc[...].astype(o_ref.dtype)

    return pl.pallas_call(
        body, name=name, grid=(nb // gb, nk),
        in_specs=[pl.BlockSpec((tk, k), lambda j, kk: (kk, 0)), _wblk_act_spec(tk, gb, nl, split, nb, 1, 0)],
        out_specs=pl.BlockSpec((gb, k, nl), lambda j, kk: (j, 0, 0)),
        out_shape=jax.ShapeDtypeStruct((nb, k, nl), BF16),
        scratch_shapes=[pltpu.VMEM((gb, k, nl), F32)],
        compiler_params=_cparams(dimension_semantics=("parallel", "arbitrary")),
    )(x, dy)


def _row_specs(rows, tb, nsub):
    return [pl.BlockSpec((tb, nsub * cw), functools.partial(lambda i, off: (i, off), off=off))
            for (_, cw, off) in rows]


def _vec_specs(params):
    return [pl.BlockSpec(p.shape, lambda i: (0, 0)) for p in params]


def _row_fwd(f, rows, params, out_dtypes, *, nsub=1, tb, name):
    t = rows[0][0].shape[0]
    tb = min(tb, t)
    n_r, n_p = len(rows), len(params)
    blk = [jax.ShapeDtypeStruct((tb, cw), F32) for (_, cw, _) in rows]
    blk += [jax.ShapeDtypeStruct(p.shape, F32) for p in params]
    out_avals = jax.eval_shape(f, *blk)

    def body(*refs):
        pv = [r[...] for r in refs[n_r:n_r + n_p]]
        for s in range(nsub):
            vals = [r[:, s * cw:(s + 1) * cw].astype(F32) for r, (_, cw, _) in zip(refs[:n_r], rows)]
            outs = f(*vals, *pv)
            for o_ref, o in zip(refs[n_r + n_p:], outs):
                w = o.shape[1]
                o_ref[:, s * w:(s + 1) * w] = o.astype(o_ref.dtype)

    return pl.pallas_call(
        body, name=name,
        grid=(t // tb,),
        in_specs=_row_specs(rows, tb, nsub) + _vec_specs(params),
        out_specs=[pl.BlockSpec((tb, nsub * av.shape[1]), lambda i: (i, 0)) for av in out_avals],
        out_shape=[jax.ShapeDtypeStruct((t, nsub * av.shape[1]), dt) for av, dt in zip(out_avals, out_dtypes)],
        compiler_params=_cparams(dimension_semantics=("parallel",)),
    )(*[r[0] for r in rows], *params)


def _row_bwd(f, rows, params, cots, row_grad_dtypes, *, nsub=1, tb, name, add_to=None, cot_add=None):
    t = rows[0][0].shape[0]
    tb = min(tb, t)
    n_r, n_p, n_c = len(rows), len(params), len(cots)
    want = [j for j in range(n_r) if row_grad_dtypes[j] is not None]
    extra = [] if add_to is None else [(add_to[1], rows[add_to[0]][1], 0)]
    extra += [] if cot_add is None else [(cot_add[1], cots[cot_add[0]][1], 0)]

    def body(*refs):
        i = pl.program_id(0)
        r_in, p_in = refs[:n_r], refs[n_r:n_r + n_p]
        c_in = refs[n_r + n_p:n_r + n_p + n_c]
        e_in = refs[n_r + n_p + n_c:n_r + n_p + n_c + len(extra)]
        outs = refs[n_r + n_p + n_c + len(extra):]
        pv = [r[...] for r in p_in]
        psum = [None] * n_p
        for s in range(nsub):
            vals = [r[:, s * cw:(s + 1) * cw].astype(F32) for r, (_, cw, _) in zip(r_in, rows)]
            cvals = [r[:, s * cw:(s + 1) * cw].astype(F32) for r, (_, cw, _) in zip(c_in, cots)]
            if cot_add is not None:
                cw = cots[cot_add[0]][1]
                cvals[cot_add[0]] = cvals[cot_add[0]] + e_in[-1][:, s * cw:(s + 1) * cw]
            _, vjp_fn = jax.vjp(f, *vals, *pv)
            grads = vjp_fn(tuple(cvals))
            for o_ref, jr in zip(outs[:len(want)], want):
                cw = rows[jr][1]
                gr = grads[jr]
                if add_to is not None and jr == add_to[0]:
                    gr = gr + e_in[0][:, s * cw:(s + 1) * cw]
                o_ref[:, s * cw:(s + 1) * cw] = gr.astype(o_ref.dtype)
            for jp in range(n_p):
                psum[jp] = grads[n_r + jp] if psum[jp] is None else psum[jp] + grads[n_r + jp]
        for o_ref, g in zip(outs[len(want):], psum):
            @pl.when(i == 0)
            def _():
                o_ref[...] = g

            @pl.when(i > 0)
            def _():
                o_ref[...] += g

    out_specs = [pl.BlockSpec((tb, nsub * rows[jr][1]), lambda i: (i, 0)) for jr in want]
    out_shape = [jax.ShapeDtypeStruct((t, nsub * rows[jr][1]), row_grad_dtypes[jr]) for jr in want]
    out_specs += _vec_specs(params)
    out_shape += [jax.ShapeDtypeStruct(p.shape, F32) for p in params]
    res = pl.pallas_call(
        body, name=name,
        grid=(t // tb,),
        in_specs=_row_specs(rows, tb, nsub) + _vec_specs(params) + _row_specs(cots, tb, nsub)
        + _row_specs(extra, tb, nsub),
        out_specs=out_specs, out_shape=out_shape,
        compiler_params=_cparams(dimension_semantics=("arbitrary",)),
    )(*[r[0] for r in rows], *params, *[c[0] for c in cots], *[e[0] for e in extra])
    return res[:len(want)], res[len(want):]


def _f_mod(x, sh, sc):
    return (_modulate(x, sh, sc),)


def _f_res_mod(x, y, g, sh, sc):
    x1 = x + g * y
    return x1, _modulate(x1, sh, sc)


def _f_res_mod2(x, y, g, sh_a, sc_a, sh_b, sc_b):
    x1 = x + g * y
    return x1, _modulate(x1, sh_a, sc_a), _modulate(x1, sh_b, sc_b)


def _f_qnorm(p, g):
    return (_rms(p) * g * (HEAD ** -0.5),)


def _f_knorm(p, g):
    return (_rms(p) * g,)


def _f_outgate(o, og):
    return (o * _sigmoid(og),)


def _loss_call(x3, f, g2, target, tb):
    t, d = x3.shape
    tb = min(tb, t)

    def body(x_ref, f_ref, g_ref, t_ref, loss_ref, dx_ref, df_ref, dg_ref):
        i = pl.program_id(0)
        fv = f_ref[...]
        g = g_ref[...]
        e = x_ref[...] + g * fv - t_ref[...]
        dx = e * (1.0 / d)
        part = 0.5 * jnp.sum(jnp.sum(e * dx, axis=1, keepdims=True), axis=0, keepdims=True)
        dx_ref[...] = dx
        df_ref[...] = (g * dx).astype(df_ref.dtype)
        dg = jnp.sum(dx * fv, axis=0, keepdims=True)

        @pl.when(i == 0)
        def _():
            loss_ref[...] = jnp.broadcast_to(part, loss_ref.shape)
            dg_ref[...] = dg

        @pl.when(i > 0)
        def _():
            loss_ref[...] += jnp.broadcast_to(part, loss_ref.shape)
            dg_ref[...] += dg

    row = pl.BlockSpec((tb, d), lambda i: (i, 0))
    vec = pl.BlockSpec((1, d), lambda i: (0, 0))
    return pl.pallas_call(
        body, name="loss_head",
        grid=(t // tb,),
        in_specs=[row, row, vec, row],
        out_specs=[pl.BlockSpec((1, LANES), lambda i: (0, 0)), row, row, vec],
        out_shape=[jax.ShapeDtypeStruct((1, LANES), F32), jax.ShapeDtypeStruct((t, d), F32),
                   jax.ShapeDtypeStruct((t, d), BF16), jax.ShapeDtypeStruct((1, d), F32)],
        compiler_params=_cparams(dimension_semantics=("arbitrary",)),
    )(x3, f, g2, target)


def _hg_consts(tb):
    c = A_CHUNK
    r = lax.broadcasted_iota(jnp.int32, (c, c), 0)
    s = lax.broadcasted_iota(jnp.int32, (c, c), 1)
    br = lax.broadcasted_iota(jnp.int32, (tb, tb), 0)
    bs = lax.broadcasted_iota(jnp.int32, (tb, tb), 1)
    shift = c.bit_length() - 1
    same_chunk = jnp.right_shift(br, shift) == jnp.right_shift(bs, shift)
    return (s <= r).astype(F32), (r <= s).astype(F32), jnp.logical_and(same_chunk, bs <= br)


def _chunk_apply(mat, x):
    c = mat.shape[0]
    return jnp.concatenate([_f32dot(mat, x[i * c:(i + 1) * c]) for i in range(x.shape[0] // c)], axis=0)


@jax.custom_vjp
def _chunk_cumsum(x, tri, tri_t):
    return _chunk_apply(tri, x)


_chunk_cumsum.defvjp(lambda x, tri, tri_t: (_chunk_apply(tri, x), (tri, tri_t)),
                     lambda r, g: (_chunk_apply(r[1], g), jnp.zeros_like(r[0]), jnp.zeros_like(r[1])))


def _per_chunk(a, b, dims):
    return jnp.stack([_bdot_raw(a[i], b[i], dims) for i in range(a.shape[0])])


@jax.custom_vjp
def _chunk_tn(a, b):
    return _per_chunk(a, b, _TN)


@jax.custom_vjp
def _chunk_nt(a, b):
    return _per_chunk(a, b, _NT)


@jax.custom_vjp
def _chunk_nn(a, b):
    return _per_chunk(a, b, _NN)


_chunk_tn.defvjp(lambda a, b: (_per_chunk(a, b, _TN), (a, b)),
                 lambda r, g: (_chunk_nt(r[1], g), _chunk_nn(r[0], g)))
_chunk_nt.defvjp(lambda a, b: (_per_chunk(a, b, _NT), (a, b)),
                 lambda r, g: (_chunk_nn(g, r[1]), _chunk_tn(g, r[0])))
_chunk_nn.defvjp(lambda a, b: (_per_chunk(a, b, _NN), (a, b)),
                 lambda r, g: (_chunk_nt(g, r[1]), _chunk_tn(r[0], g)))


def _scan_states(decay, m, st):
    sts = []
    for i in range(m.shape[0]):
        sts.append(st)
        st = st * decay[i] + m[i]
    return jnp.stack(sts), st


@jax.custom_vjp
def _state_scan(decay, m, st):
    return _scan_states(decay, m, st)


def _state_scan_fwd(decay, m, st):
    sts, st_out = _scan_states(decay, m, st)
    return (sts, st_out), (decay, sts)


def _state_scan_bwd(res, cts):
    decay, sts = res
    d_sts, g = cts
    d_decay, d_m = [], []
    for i in range(sts.shape[0] - 1, -1, -1):
        d_m.append(g)
        d_decay.append(jnp.sum(g * sts[i], axis=0, keepdims=True))
        g = g * decay[i] + d_sts[i]
    return jnp.stack(d_decay[::-1]), jnp.stack(d_m[::-1]), g


_state_scan.defvjp(_state_scan_fwd, _state_scan_bwd)


def _hg_block(qp, fp, ip, gp, lb, ng, st, tri, tri_t, bd_causal):
    tb = qp.shape[0]
    c = A_CHUNK
    n = tb // c
    q = _silu(qp)
    fg = lb + (1.0 - lb) * _sigmoid(fp)
    logf = jnp.log(fg)
    k = 1.0 - fg
    b3 = _chunk_cumsum(logf, tri, tri_t).reshape(n, c, HEAD)
    pos = lax.broadcasted_iota(jnp.int32, (1, c, 1), 1)
    b_mid = lax.stop_gradient(jnp.sum(jnp.where(pos == c // 2, b3, 0.0), axis=1, keepdims=True))
    b_last = jnp.sum(jnp.where(pos == c - 1, b3, 0.0), axis=1, keepdims=True)
    q3, k3, v3 = q.reshape(n, c, HEAD), k.reshape(n, c, HEAD), ip.reshape(n, c, HEAD)
    scores = _dot_nt((q3 * jnp.exp(b3 - b_mid)).reshape(tb, HEAD), (k3 * jnp.exp(b_mid - b3)).reshape(tb, HEAD))
    o_intra = _dot_nn(jnp.where(bd_causal, scores, 0.0), ip)
    states, st_new = _state_scan(jnp.exp(b_last), _chunk_tn(v3, k3 * jnp.exp(b_last - b3)), st)
    o = o_intra + _chunk_nt(q3 * jnp.exp(b3), states).reshape(tb, HEAD)
    y = _rms(o) * ng * _silu(gp)
    return y, st_new


def _hg_specs(tb, nh, rev_nb=None):
    def row(off):
        if rev_nb is None:
            return pl.BlockSpec((tb, HEAD), functools.partial(lambda h, i, off: (i, off + h), off=off))
        return pl.BlockSpec((tb, HEAD), functools.partial(lambda h, i, off: (rev_nb - 1 - i, off + h), off=off))
    return [row(0), row(nh), row(2 * nh), row(3 * nh),
            pl.BlockSpec((1, HEAD), lambda h, i: (0, h)), pl.BlockSpec((1, HEAD), lambda h, i: (0, 0))]


def _hgrn2_fwd(proj, lb, ng, tb):
    t = proj.shape[0]
    nh = proj.shape[1] // (4 * HEAD)
    tb = min(tb, t)
    nb = t // tb

    def body(q_ref, f_ref, i_ref, g_ref, lb_ref, ng_ref, y_ref, s_ref, st_ref):
        i = pl.program_id(1)

        @pl.when(i == 0)
        def _():
            st_ref[...] = jnp.zeros_like(st_ref)

        st = st_ref[...]
        s_ref[0, 0] = st
        y, st_new = _hg_block(q_ref[...], f_ref[...], i_ref[...], g_ref[...], lb_ref[...], ng_ref[...], st,
                              *_hg_consts(tb))
        y_ref[...] = y.astype(y_ref.dtype)
        st_ref[...] = st_new

    return pl.pallas_call(
        body, name="hgrn2_fwd",
        grid=(nh, nb),
        in_specs=_hg_specs(tb, nh),
        out_specs=[pl.BlockSpec((tb, HEAD), lambda h, i: (i, h)),
                   pl.BlockSpec((1, 1, HEAD, HEAD), lambda h, i: (h, i, 0, 0))],
        out_shape=[jax.ShapeDtypeStruct((t, nh * HEAD), BF16),
                   jax.ShapeDtypeStruct((nh, nb, HEAD, HEAD), F32)],
        scratch_shapes=[pltpu.VMEM((HEAD, HEAD), F32)],
        compiler_params=_cparams(dimension_semantics=("parallel", "arbitrary")),
    )(proj, proj, proj, proj, lb, ng)


def _hgrn2_bwd(proj, lb, ng, states, dy, tb):
    t = proj.shape[0]
    nh = proj.shape[1] // (4 * HEAD)
    tb = min(tb, t)
    nb = t // tb

    def body(q_ref, f_ref, i_ref, g_ref, lb_ref, ng_ref, s_ref, dy_ref,
             dq_ref, df_ref, di_ref, dg_ref, dlb_ref, dng_ref, dst_ref):
        h, i = pl.program_id(0), pl.program_id(1)
        consts = _hg_consts(tb)

        @pl.when(i == 0)
        def _():
            dst_ref[...] = jnp.zeros_like(dst_ref)
            dlb_ref[...] = jnp.zeros_like(dlb_ref)

        @pl.when(jnp.logical_and(i == 0, h == 0))
        def _():
            dng_ref[...] = jnp.zeros_like(dng_ref)

        def fn(qp, fp, ip, gp, lbx, ngx, stx):
            return _hg_block(qp, fp, ip, gp, lbx, ngx, stx, *consts)

        _, vjp_fn = jax.vjp(fn, q_ref[...], f_ref[...], i_ref[...], g_ref[...], lb_ref[...], ng_ref[...],
                            s_ref[0, 0])
        gq, gf, gi, gg, glb, gng, dst = vjp_fn((dy_ref[...].astype(F32), dst_ref[...]))
        dq_ref[...] = gq.astype(dq_ref.dtype)
        df_ref[...] = gf.astype(df_ref.dtype)
        di_ref[...] = gi.astype(di_ref.dtype)
        dg_ref[...] = gg.astype(dg_ref.dtype)
        dst_ref[...] = dst
        dlb_ref[...] += glb
        dng_ref[...] += gng

    rev = lambda h, i: (nb - 1 - i, h)
    slab = jax.ShapeDtypeStruct((t, nh * HEAD), BF16)
    return pl.pallas_call(
        body, name="hgrn2_bwd",
        grid=(nh, nb),
        in_specs=_hg_specs(tb, nh, rev_nb=nb) + [
            pl.BlockSpec((1, 1, HEAD, HEAD), lambda h, i: (h, nb - 1 - i, 0, 0)),
            pl.BlockSpec((tb, HEAD), rev)],
        out_specs=[pl.BlockSpec((tb, HEAD), rev)] * 4 + [
            pl.BlockSpec((1, HEAD), lambda h, i: (0, h)), pl.BlockSpec((1, HEAD), lambda h, i: (0, 0))],
        out_shape=[slab, slab, slab, slab,
                   jax.ShapeDtypeStruct((1, nh * HEAD), F32), jax.ShapeDtypeStruct((1, HEAD), F32)],
        scratch_shapes=[pltpu.VMEM((HEAD, HEAD), F32)],
        compiler_params=_cparams(dimension_semantics=("arbitrary", "arbitrary")),
    )(proj, proj, proj, proj, lb, ng, states, dy)


def _fgate_consts(cb):
    r = lax.broadcasted_iota(jnp.int32, (cb, cb), 0)
    s = lax.broadcasted_iota(jnp.int32, (cb, cb), 1)
    return (r <= s).astype(F32), (r >= s).astype(F32)


def _fgate_fwd(xt, bias, cb=512):
    nh, t = xt.shape
    cb = min(cb, t)

    def body(x_ref, b_ref, o_ref):
        upper, _ = _fgate_consts(cb)
        carry = jnp.zeros((nh, 1), F32)
        for blk in range(t // cb):
            z = x_ref[:, blk * cb:(blk + 1) * cb] + b_ref[...]
            logf = jnp.minimum(z, 0.0) - jnp.log(1.0 + jnp.exp(-jnp.abs(z)))
            cs = _f32dot(logf, upper) + carry
            o_ref[:, blk * cb:(blk + 1) * cb] = cs
            carry = cs[:, cb - 1:cb]

    vm = pl.BlockSpec(memory_space=pltpu.VMEM)
    return pl.pallas_call(
        body, name="fgate_fwd", in_specs=[vm, vm], out_specs=vm,
        out_shape=jax.ShapeDtypeStruct((nh, t), F32), compiler_params=_cparams(),
    )(xt, bias)


def _fgate_bwd(xt, bias, dft, cb=512):
    nh, t = xt.shape
    cb = min(cb, t)
    nblk = t // cb

    def body(x_ref, b_ref, d_ref, dx_ref, db_ref):
        _, lower = _fgate_consts(cb)
        carry = jnp.zeros((nh, 1), F32)
        db = jnp.zeros((nh, 1), F32)
        for blk in range(nblk - 1, -1, -1):
            sl = slice(blk * cb, (blk + 1) * cb)
            dlogf = _f32dot(d_ref[:, sl], lower) + carry
            carry = dlogf[:, 0:1]
            z = x_ref[:, sl] + b_ref[...]
            dz = dlogf * (1.0 - _sigmoid(z))
            dx_ref[:, sl] = dz
            db = db + jnp.sum(dz, axis=1, keepdims=True)
        db_ref[...] = db

    vm = pl.BlockSpec(memory_space=pltpu.VMEM)
    return pl.pallas_call(
        body, name="fgate_bwd", in_specs=[vm, vm, vm], out_specs=[vm, vm],
        out_shape=[jax.ShapeDtypeStruct((nh, t), F32), jax.ShapeDtypeStruct((nh, 1), F32)],
        compiler_params=_cparams(),
    )(xt, bias, dft)


def _attn_fwd(q, k, v, f_col, f_row, blk):
    t, width = q.shape
    nh = width // HEAD
    nq = t // blk

    def body(q_ref, k_ref, v_ref, fc_ref, fr_ref, o_ref, lse_ref):
        i = pl.program_id(0)
        tri = (lax.broadcasted_iota(jnp.int32, (blk, blk), 1) <= lax.broadcasted_iota(jnp.int32, (blk, blk), 0))
        for h in range(nh):
            cs = slice(h * HEAD, (h + 1) * HEAD)
            qh = q_ref[:, cs]
            fq = fc_ref[:, h:h + 1]

            def tile(j, carry, masked):
                m, l, acc = carry
                rs = pl.ds(pl.multiple_of(j * blk, blk), blk)
                s = _bdot_raw(qh, k_ref[rs, cs], _NT) + (fq - fr_ref[j, h:h + 1, :])
                if masked:
                    s = jnp.where(tri, s, NEG_INF)
                m_new = jnp.maximum(m, jnp.max(s, axis=1, keepdims=True))
                p = jnp.exp(s - m_new)
                alpha = jnp.exp(m - m_new)
                l_new = alpha * l + jnp.sum(p, axis=1, keepdims=True)
                acc_new = alpha * acc + _bdot_raw(p, v_ref[rs, cs], _NN)
                return m_new, l_new, acc_new

            init = (jnp.full((blk, 1), NEG_INF, F32), jnp.zeros((blk, 1), F32), jnp.zeros((blk, HEAD), F32))
            carry = lax.fori_loop(0, i, lambda j, c: tile(j, c, False), init)
            m, l, acc = tile(i, carry, True)
            o_ref[:, cs] = acc / l
            lse_ref[:, h:h + 1] = m + jnp.log(l)

    vm = pl.BlockSpec(memory_space=pltpu.VMEM)
    return pl.pallas_call(
        body, name="fox_attn_fwd",
        grid=(nq,),
        in_specs=[pl.BlockSpec((blk, width), lambda i: (i, 0)), vm, vm,
                  pl.BlockSpec((blk, nh), lambda i: (i, 0)), vm],
        out_specs=[pl.BlockSpec((blk, width), lambda i: (i, 0)), pl.BlockSpec((blk, nh), lambda i: (i, 0))],
        out_shape=[jax.ShapeDtypeStruct((t, width), F32), jax.ShapeDtypeStruct((t, nh), F32)],
        compiler_params=_cparams(dimension_semantics=("parallel",)),
    )(q, k, v, f_col, f_row)


def _attn_bwd_dq(q, k, v, f_col, f_row, o, do, lse, blk):
    t, width = q.shape
    nh = width // HEAD
    nq = t // blk

    def body(q_ref, k_ref, v_ref, fc_ref, fr_ref, o_ref, do_ref, lse_ref, dq_ref, dfc_ref, dl_ref):
        i = pl.program_id(0)
        tri = (lax.broadcasted_iota(jnp.int32, (blk, blk), 1) <= lax.broadcasted_iota(jnp.int32, (blk, blk), 0))
        for h in range(nh):
            cs = slice(h * HEAD, (h + 1) * HEAD)
            qh = q_ref[:, cs]
            doh = do_ref[:, cs]
            bias = fc_ref[:, h:h + 1] - lse_ref[:, h:h + 1]
            delta = jnp.sum(doh.astype(F32) * o_ref[:, cs], axis=1, keepdims=True)

            def tile(j, carry, masked):
                dq, dfq = carry
                rs = pl.ds(pl.multiple_of(j * blk, blk), blk)
                kj = k_ref[rs, cs]
                p = jnp.exp(_bdot_raw(qh, kj, _NT) + (bias - fr_ref[j, h:h + 1, :]))
                if masked:
                    p = jnp.where(tri, p, 0.0)
                ds = p * (_bdot_raw(doh, v_ref[rs, cs], _NT) - delta)
                return dq + _bdot_raw(ds, kj, _NN), dfq + jnp.sum(ds, axis=1, keepdims=True)

            carry = lax.fori_loop(0, i, lambda j, c: tile(j, c, False),
                                  (jnp.zeros((blk, HEAD), F32), jnp.zeros((blk, 1), F32)))
            dq, dfq = tile(i, carry, True)
            dq_ref[:, cs] = dq
            dfc_ref[:, h:h + 1] = dfq
            dl_ref[:, h:h + 1] = delta

    vm = pl.BlockSpec(memory_space=pltpu.VMEM)
    wide = pl.BlockSpec((blk, width), lambda i: (i, 0))
    thin = pl.BlockSpec((blk, nh), lambda i: (i, 0))
    return pl.pallas_call(
        body, name="fox_attn_bwd_dq",
        grid=(nq,),
        in_specs=[wide, vm, vm, thin, vm, wide, wide, thin],
        out_specs=[wide, thin, thin],
        out_shape=[jax.ShapeDtypeStruct((t, width), F32), jax.ShapeDtypeStruct((t, nh), F32),
                   jax.ShapeDtypeStruct((t, nh), F32)],
        compiler_params=_cparams(dimension_semantics=("parallel",)),
    )(q, k, v, f_col, f_row, o, do, lse)


def _attn_bwd_dkv(q, k, v, f_col, f_row, do, lse, delta, blk):
    t, width = q.shape
    nh = width // HEAD
    nq = t // blk

    def body(q_ref, k_ref, v_ref, fc_ref, fr_ref, do_ref, lse_ref, dl_ref, dk_ref, dv_ref, dfr_ref,
             s_ref, dp_ref, p_ref, ds_ref, dfs_ref, dk_acc, dv_acc):
        j = pl.program_id(0)
        for h in range(nh):
            cs = slice(h * HEAD, (h + 1) * HEAD)
            fs = fr_ref[0, h:h + 1, :]
            dfs_ref[...] = jnp.zeros_like(dfs_ref)
            dk_acc[...] = jnp.zeros_like(dk_acc)
            dv_acc[...] = jnp.zeros_like(dv_acc)

            def tile(i, masked):
                base = pl.multiple_of(i * blk, blk)
                rs = pl.ds(base, blk)
                s_ref[...] = _bdot_raw(q_ref[rs, cs], k_ref[:, cs], _NT)
                dp_ref[...] = _bdot_raw(do_ref[rs, cs], v_ref[:, cs], _NT)

                def rows(rr, first_row):
                    gr = pl.ds(pl.multiple_of(base + first_row, ATTN_ROWS), ATTN_ROWS)
                    bias = fc_ref[gr, h:h + 1] - lse_ref[gr, h:h + 1]
                    p = jnp.exp(s_ref[rr, :] + (bias - fs))
                    if masked:
                        p = jnp.where(_chunk_causal(first_row, blk), p, 0.0)
                    ds = p * (dp_ref[rr, :] - dl_ref[gr, h:h + 1])
                    dfs_ref[...] -= jnp.sum(ds, axis=0, keepdims=True)
                    p_ref[rr, :] = p.astype(p_ref.dtype)
                    ds_ref[rr, :] = ds.astype(ds_ref.dtype)

                _row_chunks(blk, rows)
                dv_acc[...] += _bdot_raw(p_ref[...], do_ref[rs, cs], _TN)
                dk_acc[...] += _bdot_raw(ds_ref[...], q_ref[rs, cs], _TN)

            def off_diagonal(i, carry):
                tile(i, False)
                return carry

            tile(j, True)
            lax.fori_loop(j + 1, nq, off_diagonal, 0)
            dk_ref[:, cs] = dk_acc[...]
            dv_ref[:, cs] = dv_acc[...]
            dfr_ref[0, h:h + 1, :] = dfs_ref[...]

    vm = pl.BlockSpec(memory_space=pltpu.VMEM)
    wide = pl.BlockSpec((blk, width), lambda j: (j, 0))
    frow = pl.BlockSpec((1, nh, blk), lambda j: (j, 0, 0))
    tile_f32, tile_b16 = pltpu.VMEM((blk, blk), F32), pltpu.VMEM((blk, blk), BF16)
    return pl.pallas_call(
        body, name="fox_attn_bwd_dkv",
        grid=(nq,),
        in_specs=[vm, wide, wide, vm, frow, vm, vm, vm],
        out_specs=[wide, wide, frow],
        out_shape=[jax.ShapeDtypeStruct((t, width), F32), jax.ShapeDtypeStruct((t, width), F32),
                   jax.ShapeDtypeStruct((nq, nh, blk), F32)],
        scratch_shapes=[tile_f32, tile_f32, tile_b16, tile_b16, pltpu.VMEM((1, blk), F32),
                        pltpu.VMEM((blk, HEAD), F32), pltpu.VMEM((blk, HEAD), F32)],
        compiler_params=_cparams(dimension_semantics=("parallel",)),
    )(q, k, v, f_col, f_row, do, lse, delta)


def _attn_delta(do, o, tb):
    t, width = o.shape
    nh = width // HEAD
    tb = min(tb, t)

    def body(do_ref, o_ref, dl_ref):
        for h in range(nh):
            cs = slice(h * HEAD, (h + 1) * HEAD)
            dl_ref[:, h:h + 1] = jnp.sum(do_ref[:, cs].astype(F32) * o_ref[:, cs], axis=1, keepdims=True)

    wide = pl.BlockSpec((tb, width), lambda i: (i, 0))
    return pl.pallas_call(body, name="fox_attn_delta", grid=(t // tb,), in_specs=[wide, wide],
                          out_specs=pl.BlockSpec((tb, nh), lambda i: (i, 0)),
                          out_shape=jax.ShapeDtypeStruct((t, nh), F32),
                          compiler_params=_cparams(dimension_semantics=("parallel",)))(do, o)


ATTN_BWD_GROUPS = 2
ATTN_BWD_VMEM = 56 * 1024 * 1024


def _attn_bwd(q, k, v, f_col, f_row, do, lse, delta, blk):
    t, width = q.shape
    nh = width // HEAD
    nq = t // blk
    hpg = nh // ATTN_BWD_GROUPS
    gw = hpg * HEAD

    def body(q_ref, do_ref, k_ref, v_ref, fc_ref, fr_ref, lse_ref, dl_ref,
             dq_ref, dk_ref, dv_ref, dfc_ref, dfr_ref):
        g, j = pl.program_id(0), pl.program_id(1)
        tri = (lax.broadcasted_iota(jnp.int32, (blk, blk), 1) <= lax.broadcasted_iota(jnp.int32, (blk, blk), 0))

        @pl.when(j == 0)
        def _():
            dq_ref[...] = jnp.zeros_like(dq_ref)
            dfc_ref[...] = jnp.zeros_like(dfc_ref)

        for h in range(hpg):
            cs = slice(h * HEAD, (h + 1) * HEAD)
            kj = k_ref[:, cs]
            vj = v_ref[:, cs]
            fs = fr_ref[0, 0, h:h + 1, :]

            def tile(i, carry, masked):
                dk, dv, dfs = carry
                rs = pl.ds(pl.multiple_of(i * blk, blk), blk)
                qi = q_ref[rs, cs]
                doi = do_ref[rs, cs]
                bias = fc_ref[0, rs, h:h + 1] - lse_ref[0, rs, h:h + 1]
                p = jnp.exp(_bdot_raw(qi, kj, _NT) + (bias - fs))
                if masked:
                    p = jnp.where(tri, p, 0.0)
                ds = p * (_bdot_raw(doi, vj, _NT) - dl_ref[0, rs, h:h + 1])
                dsb = ds.astype(BF16)
                dq_ref[rs, cs] += _bdot_raw(dsb, kj, _NN)
                dfc_ref[0, rs, h:h + 1] += jnp.sum(ds, axis=1, keepdims=True)
                return (dk + _bdot_raw(dsb, qi, _TN), dv + _bdot_raw(p, doi, _TN),
                        dfs - jnp.sum(ds, axis=0, keepdims=True))

            init = (jnp.zeros((blk, HEAD), F32), jnp.zeros((blk, HEAD), F32), jnp.zeros((1, blk), F32))
            carry = tile(j, init, True)
            dk, dv, dfs = lax.fori_loop(j + 1, nq, lambda i, c: tile(i, c, False), carry)
            dk_ref[:, cs] = dk
            dv_ref[:, cs] = dv
            dfr_ref[0, 0, h:h + 1, :] = dfs

    by_group = lambda a: a.reshape(t, ATTN_BWD_GROUPS, hpg).transpose(1, 0, 2)
    fr_g = f_row.reshape(nq, ATTN_BWD_GROUPS, hpg, blk).transpose(1, 0, 2, 3)
    resident = pl.BlockSpec((t, gw), lambda g, j: (0, g), pipeline_mode=pl.Buffered(1))
    stat = pl.BlockSpec((1, t, hpg), lambda g, j: (g, 0, 0), pipeline_mode=pl.Buffered(1))
    kv_blk = pl.BlockSpec((blk, gw), lambda g, j: (j, g))
    frow = pl.BlockSpec((1, 1, hpg, blk), lambda g, j: (g, j, 0, 0))
    dq, dk, dv, dfc, dfr = pl.pallas_call(
        body, name="fox_attn_bwd",
        grid=(ATTN_BWD_GROUPS, nq),
        in_specs=[resident, resident, kv_blk, kv_blk, stat, frow, stat, stat],
        out_specs=[pl.BlockSpec((t, gw), lambda g, j: (0, g)), kv_blk, kv_blk,
                   pl.BlockSpec((1, t, hpg), lambda g, j: (g, 0, 0)), frow],
        out_shape=[jax.ShapeDtypeStruct((t, width), F32), jax.ShapeDtypeStruct((t, width), F32),
                   jax.ShapeDtypeStruct((t, width), F32), jax.ShapeDtypeStruct((ATTN_BWD_GROUPS, t, hpg), F32),
                   jax.ShapeDtypeStruct((ATTN_BWD_GROUPS, nq, hpg, blk), F32)],
        compiler_params=pltpu.CompilerParams(vmem_limit_bytes=ATTN_BWD_VMEM,
                                             dimension_semantics=("parallel", "arbitrary")),
    )(q, do, k, v, by_group(f_col), fr_g, by_group(lse), by_group(delta))
    return (dq, dk, dv, dfc.transpose(1, 0, 2).reshape(t, nh),
            dfr.transpose(1, 0, 2, 3).reshape(nq, nh, blk))


def _shift_down(u, n):
    row = lax.broadcasted_iota(jnp.int32, u.shape, 0)
    return jnp.where(row < n, 0.0, pltpu.roll(u, n, 0))


def _shift_up(u, n):
    t = u.shape[0]
    row = lax.broadcasted_iota(jnp.int32, u.shape, 0)
    return jnp.where(row >= t - n, 0.0, pltpu.roll(u, t - n, 0))


def _convglu_specs(t):
    return [pl.BlockSpec((2, t, LANES), lambda j: (0, 0, j)),
            pl.BlockSpec((2, CONV_TAPS, LANES), lambda j: (0, 0, j)),
            pl.BlockSpec((2, 1, LANES), lambda j: (0, 0, j))]


def _convglu_fwd(u, cw, cb):
    _, t, fp = u.shape

    def body(u_ref, w_ref, b_ref, a_ref):
        c = []
        for hf in range(2):
            uv, w = u_ref[hf], w_ref[hf]
            c.append(w[0:1] * _shift_down(uv, 2) + w[1:2] * _shift_down(uv, 1) + w[2:3] * uv + b_ref[hf])
        a_ref[...] = (_silu(c[0]) * c[1]).astype(a_ref.dtype)

    return pl.pallas_call(
        body, name="convglu_fwd",
        grid=(fp // LANES,),
        in_specs=_convglu_specs(t),
        out_specs=pl.BlockSpec((t, LANES), lambda j: (0, j)),
        out_shape=jax.ShapeDtypeStruct((t, fp), BF16),
        compiler_params=_cparams(dimension_semantics=("parallel",)),
    )(u, cw, cb)


def _convglu_bwd(u, cw, cb, da):
    _, t, fp = u.shape

    def body(u_ref, w_ref, b_ref, da_ref, du_ref, dw_ref, db_ref):
        us, c = [], []
        for hf in range(2):
            uv, w = u_ref[hf], w_ref[hf]
            u1, u2 = _shift_down(uv, 1), _shift_down(uv, 2)
            us.append((uv, u1, u2))
            c.append(w[0:1] * u2 + w[1:2] * u1 + w[2:3] * uv + b_ref[hf])
        gc, vc = c
        sg = _sigmoid(gc)
        dav = da_ref[...].astype(F32)
        dcs = [dav * vc * (sg * (1.0 + gc * (1.0 - sg))), dav * (gc * sg)]
        for hf in range(2):
            dc, w = dcs[hf], w_ref[hf]
            uv, u1, u2 = us[hf]
            du = w[2:3] * dc + w[1:2] * _shift_up(dc, 1) + w[0:1] * _shift_up(dc, 2)
            du_ref[hf] = du.astype(du_ref.dtype)
            dw_ref[hf, 0:1, :] = jnp.sum(dc * u2, axis=0, keepdims=True)
            dw_ref[hf, 1:2, :] = jnp.sum(dc * u1, axis=0, keepdims=True)
            dw_ref[hf, 2:3, :] = jnp.sum(dc * uv, axis=0, keepdims=True)
            db_ref[hf] = jnp.sum(dc, axis=0, keepdims=True)

    specs = _convglu_specs(t)
    return pl.pallas_call(
        body, name="convglu_bwd",
        grid=(fp // LANES,),
        in_specs=specs + [pl.BlockSpec((t, LANES), lambda j: (0, j))],
        out_specs=specs,
        out_shape=[jax.ShapeDtypeStruct((2, t, fp), BF16), jax.ShapeDtypeStruct((2, CONV_TAPS, fp), F32),
                   jax.ShapeDtypeStruct((2, 1, fp), F32)],
        compiler_params=_cparams(dimension_semantics=("parallel",)),
    )(u, cw, cb, da)


def _local_step(x, target, mods, lb, small, get_w, put_g, *, tb=512, attn_blk=512):
    t, d = x.shape
    nh = d // HEAD
    nb = NDEV
    wts = {}
    vec = lambda *names: [mods[n] for n in names]

    def ffn_fwd(h2, l):
        u = _mm_wblk(h2, wts[f"up{l}"], F32, f"ffn{l}_up", gb=nb // 2, split=2, tm=512)
        a = _convglu_fwd(u, small[f"conv_w{l}"], small[f"conv_b{l}"])
        f = _mm(a, wts[f"down{l}"], "nn", F32, f"ffn{l}_down", tk=4096)
        return u, a, f

    def ffn_bwd(df, h2, u, a, l):
        da = _mm(df, wts[f"down{l}"], "nt", BF16, f"ffn{l}_down_dx", tn=1536)
        dwd = _mm(a, df, "tn", BF16, f"ffn{l}_down_dw", tm=1536, tk=1024)
        du, dcw, dcb = _convglu_bwd(u, small[f"conv_w{l}"], small[f"conv_b{l}"], da)
        dh2 = _mm_wblk_dx(du, wts[f"up{l}"], F32, f"ffn{l}_up_dx", k=d, gb=nb // 2, split=2, tm=1024)
        dwu = _mm_wblk_dw(h2, du, f"ffn{l}_up_dw", nb=nb, gb=nb // 4, split=2, tk=2048)
        return dh2, dwu, dwd, dcw, dcb

    (h_a,) = _row_fwd(_f_mod, [(x, d, 0)], vec("sh1_0", "sc1_0"), [BF16], tb=tb, name="l0_mod1")
    wts.update(get_w("l0a", h_a))
    proj_a = _mm_wblk(h_a, wts["a_in"], F32, "a_in", gb=nb // 2)
    ypre, states = _hgrn2_fwd(proj_a, lb, small["a_norm_g"], tb)
    wts.update(get_w("l0b", ypre))
    y_a = _mm(ypre, wts["a_out"], "nn", F32, "a_out")
    x1, h2_0 = _row_fwd(_f_res_mod, [(x, d, 0), (y_a, d, 0)], vec("g1_0", "sh2_0", "sc2_0"), [F32, BF16],
                        tb=tb, name="l0_res_mod2")
    u0, a0, f0 = ffn_fwd(h2_0, 0)
    x2, h_kv, h_q = _row_fwd(_f_res_mod2, [(x1, d, 0), (f0, d, 0)],
                             vec("g2_0", "kv_sh", "kv_sc", "sh1_1", "sc1_1"), [F32, BF16, BF16],
                             tb=tb, name="l0_res_kvmod_qmod")
    wts.update(get_w("l1", h_kv))
    proj_kv = _mm(h_kv, wts["kv"], "nn", F32, "kv_proj")
    proj_f = _mm(h_kv, wts["kv_f"], "nn", F32, "kv_fproj")
    (k_n,) = _row_fwd(_f_knorm, [(proj_kv, HEAD, 0)], [small["k_norm_g"]], [BF16], nsub=nh, tb=tb, name="k_norm")
    v_b = proj_kv[:, d:].astype(BF16)
    f_logit_t = proj_f[:, :nh].T
    f_bias = small["kv_b_f"].reshape(nh, 1)
    f_t = _fgate_fwd(f_logit_t, f_bias)
    f_col = f_t.T
    f_row = f_t.reshape(nh, t // attn_blk, attn_blk).transpose(1, 0, 2)
    proj_q = _mm_wblk(h_q, wts["b_q"], F32, "b_q", gb=nb)
    (q_n,) = _row_fwd(_f_qnorm, [(proj_q, HEAD, 0)], [small["q_norm_g"]], [BF16], nsub=nh, tb=tb, name="q_norm")
    o_att, lse = _attn_fwd(q_n, k_n, v_b, f_col, f_row, attn_blk)
    (z,) = _row_fwd(_f_outgate, [(o_att, HEAD, 0), (proj_q, HEAD, 1)], [], [BF16], nsub=nh, tb=tb, name="out_gate")
    y_b = _mm(z, wts["b_out"], "nn", F32, "b_out")
    x3, h2_1 = _row_fwd(_f_res_mod, [(x2, d, 0), (y_b, d, 0)], vec("g1_1", "sh2_1", "sc2_1"), [F32, BF16],
                        tb=tb, name="l1_res_mod2")
    u1, a1, f1 = ffn_fwd(h2_1, 1)
    loss, dx4, df1, dg2_1 = _loss_call(x3, f1, mods["g2_1"], target, tb)

    g = {}
    dmods = {"g2_1": dg2_1}
    dh2, g["up1"], g["down1"], g["conv_w1"], g["conv_b1"] = ffn_bwd(df1, h2_1, u1, a1, 1)
    (dx2, dy_b), (dmods["g1_1"], dmods["sh2_1"], dmods["sc2_1"]) = _row_bwd(
        _f_res_mod, [(x2, d, 0), (y_b, d, 0)], vec("g1_1", "sh2_1", "sc2_1"),
        [(dx4, d, 0), (dh2, d, 0)], [F32, BF16], tb=tb, name="l1_res_mod2_bwd")
    dz = _mm(dy_b, wts["b_out"], "nt", F32, "b_out_dx")
    g["b_out"] = _mm(z, dy_b, "tn", BF16, "b_out_dw", tk=1024)
    (do_att, dog), _ = _row_bwd(_f_outgate, [(o_att, HEAD, 0), (proj_q, HEAD, 1)], [], [(dz, HEAD, 0)],
                                [BF16, BF16], nsub=nh, tb=tb, name="out_gate_bwd")
    delta = _attn_delta(do_att, o_att, tb)
    dq_n, dk_n, dv, dfc_q, dfr_k = _attn_bwd(q_n, k_n, v_b, f_col, f_row, do_att, lse, delta, attn_blk)
    (dpq,), (g["q_norm_g"],) = _row_bwd(_f_qnorm, [(proj_q, HEAD, 0)], [small["q_norm_g"]],
                                        [(dq_n, HEAD, 0)], [BF16], nsub=nh, tb=tb, name="q_norm_bwd")
    dproj_q = jnp.concatenate([dpq, dog], axis=1)
    dh_q = _mm_wblk_dx(dproj_q, wts["b_q"], F32, "b_q_dx", k=d, gb=nb)
    g["b_q"] = _mm_wblk_dw(h_q, dproj_q, "b_q_dw", nb=nb, gb=nb)
    (dpk,), (g["k_norm_g"],) = _row_bwd(_f_knorm, [(proj_kv, HEAD, 0)], [small["k_norm_g"]],
                                        [(dk_n, HEAD, 0)], [BF16], nsub=nh, tb=tb, name="k_norm_bwd")
    dproj_kv = jnp.concatenate([dpk, dv.astype(BF16)], axis=1)
    df_t = dfc_q.T + dfr_k.transpose(1, 0, 2).reshape(nh, t)
    dflogit_t, g["kv_b_f"] = _fgate_bwd(f_logit_t, f_bias, df_t)
    dproj_f = jnp.pad(dflogit_t.T, ((0, 0), (0, LANES - nh))).astype(BF16)
    dh_kv = _mm(dproj_kv, wts["kv"], "nt", F32, "kv_proj_dx")
    dh_kv_f = _mm(dproj_f, wts["kv_f"], "nt", F32, "kv_fproj_dx")
    g["kv"] = _mm(h_kv, dproj_kv, "tn", BF16, "kv_proj_dw", tk=1024)
    g["kv_f"] = _mm(h_kv, dproj_f, "tn", F32, "kv_fproj_dw", tk=1024)
    sent = put_g("l1", {n: g.pop(n) for n in ("b_out", "b_q", "kv", "kv_f", "up1", "down1")})
    (dx1, df0), (dmods["g2_0"], dmods["kv_sh"], dmods["kv_sc"], dmods["sh1_1"], dmods["sc1_1"]) = _row_bwd(
        _f_res_mod2, [(x1, d, 0), (f0, d, 0)], [mods["g2_0"] + sent] + vec("kv_sh", "kv_sc", "sh1_1", "sc1_1"),
        [(dx2, d, 0), (dh_kv, d, 0), (dh_q, d, 0)], [F32, BF16], tb=tb, name="l0_res_kvmod_qmod_bwd",
        cot_add=(1, dh_kv_f))
    dh2, g["up0"], g["down0"], g["conv_w0"], g["conv_b0"] = ffn_bwd(df0, h2_0, u0, a0, 0)
    (dx0, dy_a), (dmods["g1_0"], dmods["sh2_0"], dmods["sc2_0"]) = _row_bwd(
        _f_res_mod, [(x, d, 0), (y_a, d, 0)], vec("g1_0", "sh2_0", "sc2_0"),
        [(dx1, d, 0), (dh2, d, 0)], [F32, BF16], tb=tb, name="l0_res_mod2_bwd")
    dypre = _mm(dy_a, wts["a_out"], "nt", BF16, "a_out_dx")
    g["a_out"] = _mm(ypre, dy_a, "tn", BF16, "a_out_dw", tk=1024)
    sent = put_g("l0b", {n: g.pop(n) for n in ("a_out", "up0", "down0")})
    dpa_q, dpa_f, dpa_i, dpa_g, dlb, g["a_norm_g"] = _hgrn2_bwd(proj_a, lb + sent, small["a_norm_g"], states,
                                                               dypre, tb)
    dproj_a = jnp.concatenate([dpa_q, dpa_f, dpa_i, dpa_g], axis=1)
    dh_a = _mm_wblk_dx(dproj_a, wts["a_in"], F32, "a_in_dx", k=d, gb=nb, tm=512)
    put_g("l0a", {"a_in": _mm_wblk_dw(h_a, dproj_a, "a_in_dw", nb=nb, gb=nb // 4, tk=2048)})
    (grad_x,), (dmods["sh1_0"], dmods["sc1_0"]) = _row_bwd(
        _f_mod, [(x, d, 0)], vec("sh1_0", "sc1_0"), [(dh_a, d, 0)], [F32], tb=tb, name="l0_mod1_bwd",
        add_to=(0, dx0))
    return loss, grad_x, dmods, dlb, g


def _position():
    return lax.axis_index("x"), lax.axis_index("y"), lax.axis_index("c")


def _hbm_specs(n):
    return [pl.BlockSpec(memory_space=pl.ANY)] * n


def _all_gather(arrs, name):
    n = len(arrs)

    def body(*refs):
        x_refs, out_refs = refs[:n], refs[n:2 * n]
        send_sems, recv_sems, local_sems = refs[2 * n:]
        x, y, cc = _position()
        me, sibling = (x, y, cc), (x, y, 1 - cc)
        chips = [(1 - x, y), (x, 1 - y), (1 - x, 1 - y)]

        def copy(a, k, block, to, src=None):
            slot = out_refs[a].at[4 * block[0] + 2 * block[1] + block[2]]
            return pltpu.make_async_remote_copy(
                src_ref=slot if src is None else src, dst_ref=slot,
                send_sem=send_sems.at[7 * a + k], recv_sem=recv_sems.at[7 * a + k],
                device_id=to, device_id_type=_MESH)

        local = [pltpu.make_async_copy(x_refs[a], out_refs[a].at[4 * x + 2 * y + cc], local_sems.at[a])
                 for a in range(n)]
        for cp in local:
            cp.start()
        first = []
        for a in range(n):
            first.append(copy(a, 0, me, sibling, src=x_refs[a]))
            first += [copy(a, 1 + j, me, (*chip, cc), src=x_refs[a]) for j, chip in enumerate(chips)]
        for cp in first:
            cp.start()
        passed = []
        for j, chip in enumerate(chips):
            for a in range(n):
                copy(a, 1 + j, (*chip, cc), me).wait_recv()
                fwd = copy(a, 4 + j, (*chip, cc), sibling)
                fwd.start()
                passed.append(fwd)
        for a in range(n):
            copy(a, 0, sibling, me).wait_recv()
        for j, chip in enumerate(chips):
            for a in range(n):
                copy(a, 4 + j, (*chip, 1 - cc), me).wait_recv()
        for cp in first + passed:
            cp.wait_send()
        for cp in local:
            cp.wait()

    return pl.pallas_call(
        body, name=name,
        out_shape=[jax.ShapeDtypeStruct((NDEV, *a.shape), a.dtype) for a in arrs],
        in_specs=_hbm_specs(n), out_specs=_hbm_specs(n),
        scratch_shapes=[pltpu.SemaphoreType.DMA((7 * n,)), pltpu.SemaphoreType.DMA((7 * n,)),
                        pltpu.SemaphoreType.DMA((n,))],
    )(*arrs)


_XCHG_EFFECT = pltpu.SideEffectType.DATAFLOW_SIDE_EFFECTING
ALL_PEERS = (1, 2, 3, 4, 5, 6, 7)
SAME_CORE = (2, 4, 6)


def _xchg_copies(src_refs, land_refs, send_sems, recv_sems, local_sems, scatter, rels):
    x, y, cc = _position()
    me = 4 * x + 2 * y + cc
    remote, local = [], []
    for a, (src, land) in enumerate(zip(src_refs, land_refs)):
        local.append(pltpu.make_async_copy(src.at[me] if scatter else src, land.at[me], local_sems.at[a]))
        for idx, rel in enumerate(rels):
            px = 1 - x if rel & 4 else x
            py = 1 - y if rel & 2 else y
            pc = 1 - cc if rel & 1 else cc
            k = len(rels) * a + idx
            remote.append(pltpu.make_async_remote_copy(
                src_ref=src.at[4 * px + 2 * py + pc] if scatter else src, dst_ref=land.at[me],
                send_sem=send_sems.at[k], recv_sem=recv_sems.at[k], device_id=(px, py, pc), device_id_type=_MESH))
    return remote, local


def _xchg_start(srcs, scatter, rels, name):
    n = len(srcs)
    lands = [lax.empty(s.shape if scatter else (NDEV, *s.shape), s.dtype) for s in srcs]

    def body(*refs):
        remote, local = _xchg_copies(refs[:n], refs[n:2 * n], *refs[2 * n:2 * n + 3], scatter, rels)
        for cp in local + remote:
            cp.start()
        token = refs[-1]
        token[...] = jnp.zeros_like(token)

    hbm = pl.BlockSpec(memory_space=pltpu.HBM)
    sem = pl.BlockSpec(memory_space=pltpu.SEMAPHORE)
    out = pl.pallas_call(
        body, name=name,
        out_shape=(pltpu.SemaphoreType.DMA((len(rels) * n,)), pltpu.SemaphoreType.DMA((len(rels) * n,)),
                   pltpu.SemaphoreType.DMA((n,)),
                   *[pltpu.HBM(a.shape, a.dtype) for a in srcs + lands], jax.ShapeDtypeStruct((8, LANES), F32)),
        in_specs=[hbm] * (2 * n),
        out_specs=(sem, sem, sem, *[hbm] * (2 * n), pl.BlockSpec(memory_space=pltpu.VMEM)),
        input_output_aliases={i: 3 + i for i in range(2 * n)},
        compiler_params=pltpu.CompilerParams(has_side_effects=_XCHG_EFFECT),
    )(*[pltpu.with_memory_space_constraint(a, pltpu.HBM) for a in srcs + lands])
    return out[:-1], out[-1][0, 0]


def _xchg_wait(handles, after, scatter, rels, name):
    n = (len(handles) - 3) // 2

    def body(*refs):
        remote, local = _xchg_copies(refs[:n], refs[n:2 * n], *refs[2 * n:2 * n + 3], scatter, rels)
        for cp in remote:
            cp.wait_send()
            cp.wait_recv()
        for cp in local:
            cp.wait()

    hbm = pl.BlockSpec(memory_space=pltpu.HBM)
    sem = pl.BlockSpec(memory_space=pltpu.SEMAPHORE)
    thru = list(handles[3:])
    out = pl.pallas_call(
        body, name=name,
        out_shape=tuple(pltpu.HBM(a.shape, a.dtype) for a in thru),
        in_specs=[hbm] * (2 * n) + [sem, sem, sem, pl.BlockSpec(memory_space=pl.ANY)],
        out_specs=tuple([hbm] * (2 * n)),
        input_output_aliases={i: i for i in range(2 * n)},
        compiler_params=pltpu.CompilerParams(has_side_effects=_XCHG_EFFECT),
    )(*thru, *handles[:3], after)
    return list(out[n:])


def _sibling_forward(lands, name):
    n = len(lands)

    def body(*refs):
        land_refs = refs[n:2 * n]
        send_sems, recv_sems = refs[2 * n:]
        x, y, cc = _position()

        def copy(a, q, core):
            slot = land_refs[a].at[2 * q + core]
            return pltpu.make_async_remote_copy(
                src_ref=slot, dst_ref=slot, send_sem=send_sems.at[NCHIP * a + q], recv_sem=recv_sems.at[NCHIP * a + q],
                device_id=(x, y, 1 - cc), device_id_type=_MESH)

        sends = [copy(a, q, cc) for a in range(n) for q in range(NCHIP)]
        for cp in sends:
            cp.start()
        for a in range(n):
            for q in range(NCHIP):
                copy(a, q, 1 - cc).wait_recv()
        for cp in sends:
            cp.wait_send()

    return pl.pallas_call(
        body, name=name,
        out_shape=[jax.ShapeDtypeStruct(a.shape, a.dtype) for a in lands],
        in_specs=_hbm_specs(n), out_specs=_hbm_specs(n),
        input_output_aliases={i: i for i in range(n)},
        scratch_shapes=[pltpu.SemaphoreType.DMA((NCHIP * n,)), pltpu.SemaphoreType.DMA((NCHIP * n,))],
    )(*lands)


def _rs_sibling_exchange(gs):
    n = len(gs)

    def body(*refs):
        g_refs, recv_refs = refs[:n], refs[n:2 * n]
        send_sems, recv_sems = refs[2 * n:]
        x, y, cc = _position()
        copies = [pltpu.make_async_remote_copy(
            src_ref=g_refs[a].at[q, 1 - cc], dst_ref=recv_refs[a].at[q], send_sem=send_sems.at[NCHIP * a + q],
            recv_sem=recv_sems.at[NCHIP * a + q], device_id=(x, y, 1 - cc), device_id_type=_MESH)
            for a in range(n) for q in range(NCHIP)]
        for cp in copies:
            cp.start()
        for cp in copies:
            cp.wait()

    return pl.pallas_call(
        body, name="rs_sibling_exchange",
        out_shape=[jax.ShapeDtypeStruct((NCHIP, *g.shape[2:]), g.dtype) for g in gs],
        in_specs=_hbm_specs(n), out_specs=_hbm_specs(n),
        scratch_shapes=[pltpu.SemaphoreType.DMA((NCHIP * n,)), pltpu.SemaphoreType.DMA((NCHIP * n,))],
    )(*gs)


def _rs_chip_exchange(parts):
    n = len(parts)

    def body(*refs):
        p_refs, recv_refs = refs[:n], refs[n:2 * n]
        send_sems, recv_sems, local_sems = refs[2 * n:]
        x, y, cc = _position()
        myq = 2 * x + y
        chips = [(1 - x, y), (x, 1 - y), (1 - x, 1 - y)]

        def copy(a, k, px, py, src_q, dst_q):
            return pltpu.make_async_remote_copy(
                src_ref=p_refs[a].at[src_q], dst_ref=recv_refs[a].at[dst_q], send_sem=send_sems.at[3 * a + k],
                recv_sem=recv_sems.at[3 * a + k], device_id=(px, py, cc), device_id_type=_MESH)

        local = [pltpu.make_async_copy(p_refs[a].at[myq], recv_refs[a].at[myq], local_sems.at[a]) for a in range(n)]
        for cp in local:
            cp.start()
        sends = [copy(a, k, px, py, 2 * px + py, myq) for a in range(n) for k, (px, py) in enumerate(chips)]
        for cp in sends:
            cp.start()
        for a in range(n):
            for k, (px, py) in enumerate(chips):
                copy(a, k, px, py, myq, 2 * px + py).wait_recv()
        for cp in sends:
            cp.wait_send()
        for cp in local:
            cp.wait()

    return pl.pallas_call(
        body, name="rs_chip_exchange",
        out_shape=[jax.ShapeDtypeStruct(p.shape, p.dtype) for p in parts],
        in_specs=_hbm_specs(n), out_specs=_hbm_specs(n),
        scratch_shapes=[pltpu.SemaphoreType.DMA((3 * n,)), pltpu.SemaphoreType.DMA((3 * n,)),
                        pltpu.SemaphoreType.DMA((n,))],
    )(*parts)


def _pair_sum(own, got, name):
    n, r, c = own.shape

    def body(a_ref, b_ref, o_ref):
        o_ref[...] = (a_ref[...].astype(F32) + b_ref[...].astype(F32)).astype(o_ref.dtype)

    spec = pl.BlockSpec((1, r, c), lambda q: (q, 0, 0))
    return pl.pallas_call(body, name=name, grid=(n,), in_specs=[spec, spec], out_specs=spec,
                          out_shape=jax.ShapeDtypeStruct((n, r, c), own.dtype),
                          compiler_params=_cparams(dimension_semantics=("parallel",)))(own, got)


def _slab_sum(slabs, name, tr=None):
    n, r, c = slabs.shape
    tr = r if tr is None else tr

    def body(s_ref, o_ref):
        acc = s_ref[0].astype(F32)
        for q in range(1, n):
            acc = acc + s_ref[q].astype(F32)
        o_ref[...] = acc

    return pl.pallas_call(body, name=name, grid=(r // tr,),
                          in_specs=[pl.BlockSpec((n, tr, c), lambda i: (0, i, 0))],
                          out_specs=pl.BlockSpec((tr, c), lambda i: (i, 0)),
                          out_shape=jax.ShapeDtypeStruct((r, c), F32),
                          compiler_params=_cparams(dimension_semantics=("parallel",)))(slabs)


def _ada_fwd(c_all, ada_w, kv_ada_w, logits):
    rows, d = c_all.shape
    n0, nkv = ada_w.shape[2], kv_ada_w.shape[1]

    def body(c_ref, w_ref, kw_ref, lg_ref, part_ref, cact_ref, lb_ref):
        ca = _silu(c_ref[...])
        cact_ref[...] = ca
        part_ref[:, 0:n0] = _bdot_raw(ca, w_ref[0], _NN)
        part_ref[:, n0:2 * n0] = _bdot_raw(ca, w_ref[1], _NN)
        part_ref[:, 2 * n0:2 * n0 + nkv] = _bdot_raw(ca, kw_ref[...], _NN)
        lb_ref[...] = _sigmoid(lg_ref[0:1, :] - lg_ref[1:2, :])

    vm = pl.BlockSpec(memory_space=pltpu.VMEM)
    return pl.pallas_call(
        body, name="ada_fwd", in_specs=[vm, vm, vm, vm], out_specs=[vm, vm, vm],
        out_shape=[jax.ShapeDtypeStruct((rows, 2 * n0 + nkv), F32), jax.ShapeDtypeStruct((rows, d), F32),
                   jax.ShapeDtypeStruct((1, d), F32)],
        compiler_params=_cparams(),
    )(c_all, ada_w, kv_ada_w, logits)


def _ada_bwd(c_act, dm0, dm1, dkv, lb, dlb):
    rows, d = c_act.shape

    def body(c_ref, d0_ref, d1_ref, dk_ref, lb_ref, dlb_ref, dw_ref, dkw_ref, dlg_ref):
        ca = c_ref[...]
        dw_ref[0] = _bdot_raw(ca, d0_ref[...], _TN)
        dw_ref[1] = _bdot_raw(ca, d1_ref[...], _TN)
        dkw_ref[...] = _bdot_raw(ca, dk_ref[...], _TN)
        lbv = lb_ref[...]
        dl0 = dlb_ref[...] * lbv * (1.0 - lbv)
        dlg_ref[0:1, :] = dl0
        dlg_ref[1:2, :] = -dl0

    vm = pl.BlockSpec(memory_space=pltpu.VMEM)
    return pl.pallas_call(
        body, name="ada_bwd", in_specs=[vm] * 6, out_specs=[vm, vm, vm],
        out_shape=[jax.ShapeDtypeStruct((2, d, dm0.shape[1]), F32), jax.ShapeDtypeStruct((d, dkv.shape[1]), F32),
                   jax.ShapeDtypeStruct((2, d), F32)],
        compiler_params=_cparams(),
    )(c_act, dm0, dm1, dkv, lb, dlb)


def _adamw(w, g, m, v, name, tr=512):
    r, c = w.shape
    tr = _divisor_tile(r, tr, unit=8)
    c1 = 1.0 - ADAM_B1 ** ADAM_STEP
    c2 = 1.0 - ADAM_B2 ** ADAM_STEP

    def body(w_ref, g_ref, m_ref, v_ref, d_ref, mo_ref, vo_ref):
        gv = g_ref[...]
        mn = ADAM_B1 * m_ref[...] + (1.0 - ADAM_B1) * gv
        vn = ADAM_B2 * v_ref[...] + (1.0 - ADAM_B2) * (gv * gv)
        d_ref[...] = -ADAM_LR * ((mn / c1) / (jnp.sqrt(vn / c2) + ADAM_EPS) + ADAM_WD * w_ref[...])
        mo_ref[...] = mn
        vo_ref[...] = vn

    spec = pl.BlockSpec((tr, c), lambda i: (i, 0))
    out = jax.ShapeDtypeStruct((r, c), F32)
    return pl.pallas_call(body, name=name, grid=(r // tr,), in_specs=[spec] * 4, out_specs=[spec] * 3,
                          out_shape=[out, out, out],
                          compiler_params=_cparams(dimension_semantics=("parallel",)))(w, g, m, v)


def _pad_rows(a, rows):
    return jnp.pad(a, ((0, rows - a.shape[0]), (0, 0)))


def _pack_small(parts, lanes=LANES, row_unit=8):
    flat = jnp.concatenate([p.reshape(-1).astype(F32) for p in parts])
    rows = _round_up(-(-flat.shape[0] // lanes), row_unit)
    return jnp.pad(flat, (0, rows * lanes - flat.shape[0])).reshape(rows, lanes)


def _unpack_small(flat, shapes):
    out, off = [], 0
    for s in shapes:
        n = 1
        for k in s:
            n *= k
        out.append(flat[off:off + n].reshape(s))
        off += n
    return out


def _pad_shard_cols(a, n_loc, n_pad):
    lead = a.shape[:-1]
    a = a.reshape(*lead, NDEV, n_loc)
    a = jnp.pad(a, [(0, 0)] * (len(lead) + 1) + [(0, n_pad - n_loc)])
    return a.reshape(*lead, NDEV * n_pad)


def _unpad_shard_cols(a, n_loc, n_pad):
    lead = a.shape[:-1]
    return a.reshape(*lead, NDEV, n_pad)[..., :n_loc].reshape(*lead, NDEV * n_loc)


def kernel(x, c, ada_w, ada_b, a_w_in, a_lb_logits, a_norm_g, a_w_out, kv_ada_w, kv_ada_b, kv_w, kv_b_f, k_norm_g, b_w_q, q_norm_g, b_w_out, ffn_w_up, ffn_conv_w, ffn_conv_b, ffn_w_down, loss_target, m_ada_w, m_ada_b, m_a_w_in, m_a_lb_logits, m_a_norm_g, m_a_w_out, m_kv_ada_w, m_kv_ada_b, m_kv_w, m_kv_b_f, m_k_norm_g, m_b_w_q, m_q_norm_g, m_b_w_out, m_ffn_w_up, m_ffn_conv_w, m_ffn_conv_b, m_ffn_w_down, v_ada_w, v_ada_b, v_a_w_in, v_a_lb_logits, v_a_norm_g, v_a_w_out, v_kv_ada_w, v_kv_ada_b, v_kv_w, v_kv_b_f, v_k_norm_g, v_b_w_q, v_q_norm_g, v_b_w_out, v_ffn_w_up, v_ffn_conv_w, v_ffn_conv_b, v_ffn_w_down):
    t, d = x.shape[1], x.shape[2]
    nh = d // HEAD
    ncw = ffn_w_up.shape[2]
    ncp = _round_up(ncw, LANES)
    two_f = ncw * NDEV
    ff = two_f // 2
    fp = ncp * NDEV // 2
    rd = ffn_w_down.shape[1]
    me = 4 * lax.axis_index("x") + 2 * lax.axis_index("y") + lax.axis_index("c")
    weights = dict(ada_w=ada_w, ada_b=ada_b, a_w_in=a_w_in, a_lb_logits=a_lb_logits, a_norm_g=a_norm_g,
                   a_w_out=a_w_out, kv_ada_w=kv_ada_w, kv_ada_b=kv_ada_b, kv_w=kv_w, kv_b_f=kv_b_f,
                   k_norm_g=k_norm_g, b_w_q=b_w_q, q_norm_g=q_norm_g, b_w_out=b_w_out, ffn_w_up=ffn_w_up,
                   ffn_conv_w=ffn_conv_w, ffn_conv_b=ffn_conv_b, ffn_w_down=ffn_w_down)
    m_in = dict(ada_w=m_ada_w, ada_b=m_ada_b, a_w_in=m_a_w_in, a_lb_logits=m_a_lb_logits, a_norm_g=m_a_norm_g,
                a_w_out=m_a_w_out, kv_ada_w=m_kv_ada_w, kv_ada_b=m_kv_ada_b, kv_w=m_kv_w, kv_b_f=m_kv_b_f,
                k_norm_g=m_k_norm_g, b_w_q=m_b_w_q, q_norm_g=m_q_norm_g, b_w_out=m_b_w_out, ffn_w_up=m_ffn_w_up,
                ffn_conv_w=m_ffn_conv_w, ffn_conv_b=m_ffn_conv_b, ffn_w_down=m_ffn_w_down)
    v_in = dict(ada_w=v_ada_w, ada_b=v_ada_b, a_w_in=v_a_w_in, a_lb_logits=v_a_lb_logits, a_norm_g=v_a_norm_g,
                a_w_out=v_a_w_out, kv_ada_w=v_kv_ada_w, kv_ada_b=v_kv_ada_b, kv_w=v_kv_w, kv_b_f=v_kv_b_f,
                k_norm_g=v_k_norm_g, b_w_q=v_b_w_q, q_norm_g=v_q_norm_g, b_w_out=v_b_w_out, ffn_w_up=v_ffn_w_up,
                ffn_conv_w=v_ffn_conv_w, ffn_conv_b=v_ffn_conv_b, ffn_w_down=v_ffn_w_down)
    order = list(weights)

    up_loc = jnp.pad(ffn_w_up, ((0, 0), (0, 0), (0, ncp - ncw))).astype(BF16)
    down_loc = ffn_w_down.astype(BF16)
    gather_names = {"l0b": ["a_out", "up0", "down0"], "l1": ["kv", "b_q", "b_out", "up1", "down1"]}
    shards = {"a_out": a_w_out[0].astype(BF16), "up0": up_loc[0], "down0": down_loc[0], "kv": kv_w.astype(BF16),
              "b_q": b_w_q[0].astype(BF16), "b_out": b_w_out[0].astype(BF16), "up1": up_loc[1],
              "down1": down_loc[1]}
    pre = _pack_small([c, a_lb_logits, ffn_conv_w])
    a_in_all, pre_all = _all_gather([a_w_in[0].astype(BF16), pre], "gather_a_w_in_and_small_inputs")
    pre_all = pre_all.reshape(NDEV, -1)
    c_all = pre_all[:, :d]
    logits = pre_all[:, d:d + 2 * HEAD].reshape(NDEV, 2, HEAD).transpose(1, 0, 2).reshape(2, d)
    conv_w_full = pre_all[:, d + 2 * HEAD:d + 2 * HEAD + 2 * CONV_TAPS * ncw]
    conv_w_full = conv_w_full.reshape(NDEV, 2, CONV_TAPS, ncw).transpose(1, 2, 0, 3).reshape(2, CONV_TAPS, two_f)

    part, c_act, lb = _ada_fwd(_pad_rows(c_all, 2 * NDEV), ada_w, kv_ada_w, logits)
    (part_all,) = _all_gather([part[:NDEV]], "gather_adaln")
    mine = lax.dynamic_index_in_dim(part_all, me, axis=1, keepdims=False)
    n0, nkv = ada_w.shape[2], kv_ada_w.shape[1]
    mod_names = ["sh1", "sc1", "g1", "sh2", "sc2", "g2"]
    mods = {}
    for l in range(2):
        row = mine[:, l * n0:(l + 1) * n0].reshape(-1) + ada_b[l]
        for k, nm in enumerate(mod_names):
            mods[f"{nm}_{l}"] = row[k * d:(k + 1) * d].reshape(1, d)
    kvrow = mine[:, 2 * n0:2 * n0 + nkv].reshape(-1) + kv_ada_b
    mods["kv_sh"], mods["kv_sc"] = kvrow[:d].reshape(1, d), kvrow[d:].reshape(1, d)

    in_flight = {}

    def start_gather(grp, dep):
        srcs = [shards[n] for n in gather_names[grp]]
        srcs[0], _ = lax.optimization_barrier((srcs[0], dep))
        in_flight[grp], started = _xchg_start(srcs, False, SAME_CORE, f"gather_{grp}_start")
        return started

    zero = start_gather("l0b", part_all)
    mods["sh1_0"] = mods["sh1_0"] + zero

    small = {"a_norm_g": a_norm_g, "k_norm_g": k_norm_g.reshape(1, HEAD), "q_norm_g": q_norm_g, "kv_b_f": kv_b_f}
    for l in range(2):
        small[f"conv_w{l}"] = _pad_shard_cols(conv_w_full[l], ncw, ncp).reshape(CONV_TAPS, 2, fp).transpose(1, 0, 2)
        small[f"conv_b{l}"] = _pad_shard_cols(ffn_conv_b[l], ncw, ncp).reshape(2, 1, fp)

    def get_w(grp, after):
        if grp == "l0a":
            return {"a_in": a_in_all}
        arrived = _xchg_wait(in_flight[grp], after, False, SAME_CORE, f"gather_{grp}_wait")
        full = list(_sibling_forward(arrived, f"gather_{grp}_to_sibling"))
        if grp == "l0b":
            started = start_gather("l1", full[0])
            full[0], _ = lax.optimization_barrier((full[0], started))
        got = dict(zip(gather_names[grp], full))
        out = {}
        for n, a in got.items():
            if n in ("a_out", "b_out"):
                out[n] = a.reshape(d, d)
            elif n in ("down0", "down1"):
                dn = a.reshape(NCHIP, ff // NCHIP, d)
                out[n] = jnp.pad(dn, ((0, 0), (0, ncp - ncw), (0, 0))).reshape(fp, d)
            elif n == "kv":
                kv_full = a.transpose(1, 0, 2).reshape(d, NDEV * kv_w.shape[1])
                out["kv"] = kv_full[:, :2 * d]
                out["kv_f"] = jnp.pad(kv_full[:, 2 * d:], ((0, 0), (0, LANES - nh)))
            else:
                out[n] = a
        return out

    scatter_flight, g_last = {}, {}

    def put_g(grp, gr):
        if grp == "l0a":
            g_last.update(gr)
            return zero
        if grp == "l1":
            g_kvw = jnp.concatenate([gr["kv"], gr["kv_f"][:, :nh].astype(BF16)], axis=1)
            arrs = {"kv_w": g_kvw.reshape(d, NDEV, kv_w.shape[1]).transpose(1, 0, 2), "b_w_q": gr["b_q"],
                    "b_w_out": gr["b_out"].reshape(NDEV, d // NDEV, d), "up1": gr["up1"],
                    "down1": gr["down1"].reshape(NCHIP, ncp, d)[:, :ncw].reshape(NDEV, rd, d)}
        else:
            arrs = {"a_w_out": gr["a_out"].reshape(NDEV, d // NDEV, d), "up0": gr["up0"],
                    "down0": gr["down0"].reshape(NCHIP, ncp, d)[:, :ncw].reshape(NDEV, rd, d)}
        handles, sent = _xchg_start(list(arrs.values()), True, ALL_PEERS, f"scatter_{grp}_start")
        scatter_flight[grp] = (list(arrs), handles)
        return sent

    loss_v, grad_x, dmods, dlb, g = _local_step(x[0], loss_target[0], mods, lb, small, get_w, put_g)
    loss = lax.psum(loss_v[0, 0], ("x", "y", "c"))

    g_sum = {}
    for grp in ("l1", "l0b"):
        names, handles = scatter_flight[grp]
        for nm, a in zip(names, _xchg_wait(handles, grad_x, True, ALL_PEERS, f"scatter_{grp}_wait")):
            g_sum[nm] = _slab_sum(a, f"rs_slab_sum_{nm}")
    a_in_parts = g_last["a_in"].reshape(NCHIP, 2, *g_last["a_in"].shape[1:])
    (from_sibling,) = _rs_sibling_exchange([a_in_parts])
    own = lax.dynamic_index_in_dim(a_in_parts, lax.axis_index("c"), axis=1, keepdims=False)
    (from_chips,) = _rs_chip_exchange([_pair_sum(own, from_sibling, "rs_pair_sum_a_w_in")])
    g_sum["a_w_in"] = _slab_sum(from_chips, "rs_slab_sum_a_w_in")

    def conv_w_grad(a):
        return _unpad_shard_cols(a.transpose(1, 0, 2).reshape(CONV_TAPS, 2 * fp), ncw, ncp)

    def conv_b_grad(a):
        return _unpad_shard_cols(a.reshape(2 * fp), ncw, ncp)

    dmod_vec = [dmods[f"{nm}_{l}"] for l in range(2) for nm in mod_names] + [dmods["kv_sh"], dmods["kv_sc"]]
    post = _pack_small(dmod_vec + [dlb, g["a_norm_g"], g["k_norm_g"], g["q_norm_g"],
                                   jnp.pad(g["kv_b_f"].reshape(-1), (0, LANES - nh)),
                                   conv_w_grad(g["conv_w0"]), conv_w_grad(g["conv_w1"]),
                                   conv_b_grad(g["conv_b0"]), conv_b_grad(g["conv_b1"])])
    (post_all,) = _all_gather([post], "gather_small_grads")
    tot = _slab_sum(post_all, "small_grad_sum").reshape(-1)
    nmod = 14 * d
    (t_mod, t_lb, t_ang, t_kng, t_qng, t_bf, t_cw, t_cb) = _unpack_small(
        tot, [(nmod,), (1, d), (1, HEAD), (HEAD,), (1, HEAD), (LANES,), (2, CONV_TAPS, two_f), (2, two_f)])
    dm_all = post_all.reshape(NDEV, -1)[:, :nmod]
    dm0 = lax.dynamic_slice_in_dim(dm_all[:, :6 * d], me * n0, n0, axis=1)
    dm1 = lax.dynamic_slice_in_dim(dm_all[:, 6 * d:12 * d], me * n0, n0, axis=1)
    dkv = lax.dynamic_slice_in_dim(dm_all[:, 12 * d:], me * nkv, nkv, axis=1)
    g_ada_w, g_kv_ada_w, g_logits = _ada_bwd(c_act, _pad_rows(dm0, 2 * NDEV), _pad_rows(dm1, 2 * NDEV),
                                              _pad_rows(dkv, 2 * NDEV), lb, t_lb)

    grads = {
        "ada_w": g_ada_w,
        "ada_b": t_mod[:12 * d].reshape(2, 6 * d),
        "a_w_in": g_sum["a_w_in"].reshape(a_w_in.shape),
        "a_lb_logits": lax.dynamic_slice_in_dim(g_logits, me * HEAD, HEAD, axis=1),
        "a_norm_g": t_ang,
        "a_w_out": g_sum["a_w_out"].reshape(a_w_out.shape),
        "kv_ada_w": g_kv_ada_w,
        "kv_ada_b": t_mod[12 * d:],
        "kv_w": g_sum["kv_w"],
        "kv_b_f": t_bf[:nh],
        "k_norm_g": t_kng,
        "b_w_q": g_sum["b_w_q"].reshape(b_w_q.shape),
        "q_norm_g": t_qng,
        "b_w_out": g_sum["b_w_out"].reshape(b_w_out.shape),
        "ffn_w_up": jnp.stack([g_sum["up0"][:, :ncw], g_sum["up1"][:, :ncw]]),
        "ffn_conv_w": lax.dynamic_slice_in_dim(t_cw, me * ncw, ncw, axis=2),
        "ffn_conv_b": t_cb,
        "ffn_w_down": jnp.stack([g_sum["down0"], g_sum["down1"]]),
    }

    big_adam = ["ada_w", "a_w_in", "a_w_out", "kv_ada_w", "kv_w", "b_w_q", "b_w_out", "ffn_w_up", "ffn_w_down"]
    small_adam = [n for n in order if n not in big_adam]
    delta, new_m, new_v = {}, {}, {}
    for n in big_adam:
        shp = weights[n].shape
        two_d = lambda a: a.reshape(-1, shp[-1])
        dl, mn, vn = _adamw(two_d(weights[n]), two_d(grads[n]), two_d(m_in[n]), two_d(v_in[n]), f"adamw_{n}")
        delta[n], new_m[n], new_v[n] = dl.reshape(shp), mn.reshape(shp), vn.reshape(shp)
    packs = [_pack_small([src[n] for n in small_adam]) for src in (weights, grads, m_in, v_in)]
    outs = _adamw(*packs, "adamw_small", tr=packs[0].shape[0])
    shapes = [weights[n].shape for n in small_adam]
    for dst, o in zip((delta, new_m, new_v), outs):
        for n, a in zip(small_adam, _unpack_small(o.reshape(-1), shapes)):
            dst[n] = a

    return (loss, grad_x.reshape(x.shape), *[grads[n] for n in order], *[delta[n] for n in order],
            *[new_m[n] for n in order], *[new_v[n] for n in order])
```

```python
import functools

import jax
import jax.numpy as jnp
from jax import lax
from jax.experimental import pallas as pl
from jax.experimental.pallas import tpu as pltpu

F32 = jnp.float32
BF16 = jnp.bfloat16

NDEV = 8
NCHIP = 4
HEAD = 128
A_CHUNK = 64
CONV_TAPS = 3
EPS = 1e-6
NEG_INF = -1e30
LANES = 128
VMEM_LIMIT = 48 * 1024 * 1024

ADAM_LR = 0.001
ADAM_B1 = 0.9
ADAM_B2 = 0.999
ADAM_EPS = 1e-08
ADAM_WD = 0.01
ADAM_STEP = 10

_NN = (((1,), (0,)), ((), ()))
_NT = (((1,), (1,)), ((), ()))
_TN = (((0,), (0,)), ((), ()))
_MESH = pl.DeviceIdType.MESH


def _cparams(**kw):
    return pltpu.CompilerParams(vmem_limit_bytes=VMEM_LIMIT, **kw)


def _divisor_tile(n, pref, unit=LANES):
    if n <= pref:
        return n
    best = None
    for t in range(unit, pref + 1, unit):
        if n % t == 0:
            best = t
    assert best is not None, (n, pref)
    return best


def _round_up(n, unit):
    return -(-n // unit) * unit


def _bdot_raw(a, b, dims):
    return lax.dot_general(a.astype(BF16), b.astype(BF16), dims, preferred_element_type=F32)


@jax.custom_vjp
def _dot_nn(a, b):
    return _bdot_raw(a, b, _NN)


@jax.custom_vjp
def _dot_nt(a, b):
    return _bdot_raw(a, b, _NT)


@jax.custom_vjp
def _dot_tn(a, b):
    return _bdot_raw(a, b, _TN)


_dot_nn.defvjp(lambda a, b: (_bdot_raw(a, b, _NN), (a, b)),
               lambda r, g: (_dot_nt(g, r[1]), _dot_tn(r[0], g)))
_dot_nt.defvjp(lambda a, b: (_bdot_raw(a, b, _NT), (a, b)),
               lambda r, g: (_dot_nn(g, r[1]), _dot_tn(g, r[0])))
_dot_tn.defvjp(lambda a, b: (_bdot_raw(a, b, _TN), (a, b)),
               lambda r, g: (_dot_nt(r[1], g), _dot_nn(r[0], g)))


def _f32dot(a, b):
    return lax.dot_general(a, b, _NN, precision=lax.Precision.HIGHEST, preferred_element_type=F32)


def _sigmoid(x):
    return jax.nn.sigmoid(x)


def _silu(x):
    return x * jax.nn.sigmoid(x)


def _rms(x):
    return x * lax.rsqrt(jnp.mean(x * x, axis=-1, keepdims=True) + EPS)


def _modulate(x, sh, sc):
    return _rms(x) * (1.0 + sc) + sh


def _mm_call(a, b, dims, a_spec, b_spec, o_spec, o_shape, grid, acc_tile, name):
    nk = grid[2]

    def body(a_ref, b_ref, o_ref, *acc):
        p = lax.dot_general(a_ref[...].astype(BF16), b_ref[...].astype(BF16), dims,
                            preferred_element_type=F32)
        if nk == 1:
            o_ref[...] = p.astype(o_ref.dtype)
        else:
            kk = pl.program_id(2)

            @pl.when(kk == 0)
            def _():
                acc[0][...] = p

            @pl.when(kk > 0)
            def _():
                acc[0][...] += p

            @pl.when(kk == nk - 1)
            def _():
                o_ref[...] = acc[0][...].astype(o_ref.dtype)

    return pl.pallas_call(
        body, name=name, grid=grid, in_specs=[a_spec, b_spec], out_specs=o_spec, out_shape=o_shape,
        scratch_shapes=[pltpu.VMEM(acc_tile, F32)] if nk > 1 else [],
        compiler_params=_cparams(dimension_semantics=("parallel", "parallel", "arbitrary")),
    )(a, b)


def _mm(a, b, mode, out_dtype, name, tm=1024, tn=1024, tk=2048):
    if mode == "nn":
        (m, k), (k2, n) = a.shape, b.shape
    elif mode == "nt":
        (m, k), (n, k2) = a.shape, b.shape
    else:
        (k, m), (k2, n) = a.shape, b.shape
    assert k == k2, (a.shape, b.shape, mode)
    tm, tn, tk = _divisor_tile(m, tm), _divisor_tile(n, tn), _divisor_tile(k, tk)
    if mode == "tn":
        a_spec = pl.BlockSpec((tk, tm), lambda i, j, kk: (kk, i))
    else:
        a_spec = pl.BlockSpec((tm, tk), lambda i, j, kk: (i, kk))
    if mode == "nt":
        b_spec = pl.BlockSpec((tn, tk), lambda i, j, kk: (j, kk))
    else:
        b_spec = pl.BlockSpec((tk, tn), lambda i, j, kk: (kk, j))
    return _mm_call(a, b, {"nn": _NN, "nt": _NT, "tn": _TN}[mode], a_spec, b_spec,
                    pl.BlockSpec((tm, tn), lambda i, j, kk: (i, j)), jax.ShapeDtypeStruct((m, n), out_dtype),
                    (m // tm, n // tn, k // tk), (tm, tn), name)


def _wblk_act_spec(rows, gb, nl, split, nb, row_axis, blk_axis):
    if split == 1:
        return pl.BlockSpec((rows, gb * nl), lambda *g: (g[row_axis], g[blk_axis]))
    groups = nb // split // gb
    return pl.BlockSpec((None, rows, gb * nl),
                        lambda *g: (g[blk_axis] // groups, g[row_axis], g[blk_axis] % groups))


def _mm_wblk(a, wb, out_dtype, name, *, gb, row_off=0, split=1, tm=1024):
    m, k = a.shape
    nb, _, nl = wb.shape
    assert (nb // split) % gb == 0
    tm = _divisor_tile(m, tm)

    def body(a_ref, b_ref, o_ref):
        av = a_ref[...].astype(BF16)
        for s in range(gb):
            o_ref[:, s * nl:(s + 1) * nl] = lax.dot_general(
                av, b_ref[s].astype(BF16), _NN, preferred_element_type=F32).astype(o_ref.dtype)

    o_shape = (m, nb * nl) if split == 1 else (split, m, nb // split * nl)
    return pl.pallas_call(
        body, name=name, grid=(nb // gb, m // tm),
        in_specs=[pl.BlockSpec((tm, k), lambda j, i: (i, 0)),
                  pl.BlockSpec((gb, k, nl), lambda j, i: (j, row_off, 0))],
        out_specs=_wblk_act_spec(tm, gb, nl, split, nb, 1, 0),
        out_shape=jax.ShapeDtypeStruct(o_shape, out_dtype),
        compiler_params=_cparams(dimension_semantics=("parallel", "parallel")),
    )(a, wb)


def _mm_wblk_dx(dy, wb, out_dtype, name, *, k, gb, row_off=0, split=1, tm=1024):
    nb, _, nl = wb.shape
    assert (nb // split) % gb == 0
    m = dy.shape[-2]
    tm = _divisor_tile(m, tm)
    nk = nb // gb

    def body(a_ref, b_ref, o_ref, *acc):
        p = None
        for s in range(gb):
            q = lax.dot_general(a_ref[:, s * nl:(s + 1) * nl].astype(BF16), b_ref[s].astype(BF16), _NT,
                                preferred_element_type=F32)
            p = q if p is None else p + q
        if nk == 1:
            o_ref[...] = p.astype(o_ref.dtype)
        else:
            kk = pl.program_id(1)

            @pl.when(kk == 0)
            def _():
                acc[0][...] = p

            @pl.when(kk > 0)
            def _():
                acc[0][...] += p

            @pl.when(kk == nk - 1)
            def _():
                o_ref[...] = acc[0][...].astype(o_ref.dtype)

    return pl.pallas_call(
        body, name=name, grid=(m // tm, nk),
        in_specs=[_wblk_act_spec(tm, gb, nl, split, nb, 0, 1),
                  pl.BlockSpec((gb, k, nl), lambda i, kk: (kk, row_off, 0))],
        out_specs=pl.BlockSpec((tm, k), lambda i, kk: (i, 0)),
        out_shape=jax.ShapeDtypeStruct((m, k), out_dtype),
        scratch_shapes=[pltpu.VMEM((tm, k), F32)] if nk > 1 else [],
        compiler_params=_cparams(dimension_semantics=("parallel", "arbitrary")),
    )(dy, wb)


def _mm_wblk_dw(x, dy, name, *, nb, gb, split=1, tk=1024):
    t, k = x.shape
    assert (nb // split) % gb == 0
    nl = dy.shape[-1] * split // nb
    tk = _divisor_tile(t, tk)
    nk = t // tk

    def body(a_ref, b_ref, o_ref, *acc):
        kk = pl.program_id(1)
        av = a_ref[...].astype(BF16)
        for s in range(gb):
            p = lax.dot_general(av, b_ref[:, s * nl:(s + 1) * nl].astype(BF16), _TN, preferred_element_type=F32)
            if nk == 1:
                o_ref[s] = p.astype(o_ref.dtype)
                continue

            @pl.when(kk == 0)
            def _():
                acc[0][s] = p

            @pl.when(kk > 0)
            def _():
                acc[0][s] += p

        if nk > 1:
            @pl.when(kk == nk - 1)
            def _():
                o_ref[...] = acc[0][...].astype(o_ref.dtype)

    return pl.pallas_call(
        body, name=name, grid=(nb // gb, nk),
        in_specs=[pl.BlockSpec((tk, k), lambda j, kk: (kk, 0)), _wblk_act_spec(tk, gb, nl, split, nb, 1, 0)],
        out_specs=pl.BlockSpec((gb, k, nl), lambda j, kk: (j, 0, 0)),
        out_shape=jax.ShapeDtypeStruct((nb, k, nl), BF16),
        scratch_shapes=[pltpu.VMEM((gb, k, nl), F32)] if nk > 1 else [],
        compiler_params=_cparams(dimension_semantics=("parallel", "arbitrary")),
    )(x, dy)


def _row_specs(rows, tb, nsub):
    return [pl.BlockSpec((tb, nsub * cw), functools.partial(lambda i, off: (i, off), off=off))
            for (_, cw, off) in rows]


def _vec_specs(params):
    return [pl.BlockSpec(p.shape, lambda i: (0, 0)) for p in params]


def _row_fwd(f, rows, params, out_dtypes, *, nsub=1, tb, name):
    t = rows[0][0].shape[0]
    tb = min(tb, t)
    n_r, n_p = len(rows), len(params)
    blk = [jax.ShapeDtypeStruct((tb, cw), F32) for (_, cw, _) in rows]
    blk += [jax.ShapeDtypeStruct(p.shape, F32) for p in params]
    out_avals = jax.eval_shape(f, *blk)

    def body(*refs):
        pv = [r[...] for r in refs[n_r:n_r + n_p]]
        for s in range(nsub):
            vals = [r[:, s * cw:(s + 1) * cw].astype(F32) for r, (_, cw, _) in zip(refs[:n_r], rows)]
            outs = f(*vals, *pv)
            for o_ref, o in zip(refs[n_r + n_p:], outs):
                w = o.shape[1]
                o_ref[:, s * w:(s + 1) * w] = o.astype(o_ref.dtype)

    return pl.pallas_call(
        body, name=name,
        grid=(t // tb,),
        in_specs=_row_specs(rows, tb, nsub) + _vec_specs(params),
        out_specs=[pl.BlockSpec((tb, nsub * av.shape[1]), lambda i: (i, 0)) for av in out_avals],
        out_shape=[jax.ShapeDtypeStruct((t, nsub * av.shape[1]), dt) for av, dt in zip(out_avals, out_dtypes)],
        compiler_params=_cparams(dimension_semantics=("parallel",)),
    )(*[r[0] for r in rows], *params)


def _row_bwd(f, rows, params, cots, row_grad_dtypes, *, nsub=1, tb, name, add_to=None, cot_add=None):
    t = rows[0][0].shape[0]
    tb = min(tb, t)
    n_r, n_p, n_c = len(rows), len(params), len(cots)
    want = [j for j in range(n_r) if row_grad_dtypes[j] is not None]
    extra = [] if add_to is None else [(add_to[1], rows[add_to[0]][1], 0)]
    extra += [] if cot_add is None else [(cot_add[1], cots[cot_add[0]][1], 0)]

    def body(*refs):
        i = pl.program_id(0)
        r_in, p_in = refs[:n_r], refs[n_r:n_r + n_p]
        c_in = refs[n_r + n_p:n_r + n_p + n_c]
        e_in = refs[n_r + n_p + n_c:n_r + n_p + n_c + len(extra)]
        outs = refs[n_r + n_p + n_c + len(extra):]
        pv = [r[...] for r in p_in]
        psum = [None] * n_p
        for s in range(nsub):
            vals = [r[:, s * cw:(s + 1) * cw].astype(F32) for r, (_, cw, _) in zip(r_in, rows)]
            cvals = [r[:, s * cw:(s + 1) * cw].astype(F32) for r, (_, cw, _) in zip(c_in, cots)]
            if cot_add is not None:
                cw = cots[cot_add[0]][1]
                cvals[cot_add[0]] = cvals[cot_add[0]] + e_in[-1][:, s * cw:(s + 1) * cw]
            _, vjp_fn = jax.vjp(f, *vals, *pv)
            grads = vjp_fn(tuple(cvals))
            for o_ref, jr in zip(outs[:len(want)], want):
                cw = rows[jr][1]
                gr = grads[jr]
                if add_to is not None and jr == add_to[0]:
                    gr = gr + e_in[0][:, s * cw:(s + 1) * cw]
                o_ref[:, s * cw:(s + 1) * cw] = gr.astype(o_ref.dtype)
            for jp in range(n_p):
                psum[jp] = grads[n_r + jp] if psum[jp] is None else psum[jp] + grads[n_r + jp]
        for o_ref, g in zip(outs[len(want):], psum):
            @pl.when(i == 0)
            def _():
                o_ref[...] = g

            @pl.when(i > 0)
            def _():
                o_ref[...] += g

    out_specs = [pl.BlockSpec((tb, nsub * rows[jr][1]), lambda i: (i, 0)) for jr in want]
    out_shape = [jax.ShapeDtypeStruct((t, nsub * rows[jr][1]), row_grad_dtypes[jr]) for jr in want]
    out_specs += _vec_specs(params)
    out_shape += [jax.ShapeDtypeStruct(p.shape, F32) for p in params]
    res = pl.pallas_call(
        body, name=name,
        grid=(t // tb,),
        in_specs=_row_specs(rows, tb, nsub) + _vec_specs(params) + _row_specs(cots, tb, nsub)
        + _row_specs(extra, tb, nsub),
        out_specs=out_specs, out_shape=out_shape,
        compiler_params=_cparams(dimension_semantics=("arbitrary",)),
    )(*[r[0] for r in rows], *params, *[c[0] for c in cots], *[e[0] for e in extra])
    return res[:len(want)], res[len(want):]


def _f_mod(x, sh, sc):
    return (_modulate(x, sh, sc),)


def _f_res_mod(x, y, g, sh, sc):
    x1 = x + g * y
    return x1, _modulate(x1, sh, sc)


def _f_res_mod2(x, y, g, sh_a, sc_a, sh_b, sc_b):
    x1 = x + g * y
    return x1, _modulate(x1, sh_a, sc_a), _modulate(x1, sh_b, sc_b)


def _f_qnorm(p, g):
    return (_rms(p) * g * (HEAD ** -0.5),)


def _f_knorm(p, g):
    return (_rms(p) * g,)


def _f_outgate(o, og):
    return (o * _sigmoid(og),)


def _loss_call(x3, f, g2, target, tb):
    t, d = x3.shape
    tb = min(tb, t)

    def body(x_ref, f_ref, g_ref, t_ref, loss_ref, dx_ref, df_ref, dg_ref):
        i = pl.program_id(0)
        fv = f_ref[...]
        g = g_ref[...]
        e = x_ref[...] + g * fv - t_ref[...]
        dx = e * (1.0 / d)
        part = 0.5 * jnp.sum(jnp.sum(e * dx, axis=1, keepdims=True), axis=0, keepdims=True)
        dx_ref[...] = dx
        df_ref[...] = (g * dx).astype(df_ref.dtype)
        dg = jnp.sum(dx * fv, axis=0, keepdims=True)

        @pl.when(i == 0)
        def _():
            loss_ref[...] = jnp.broadcast_to(part, loss_ref.shape)
            dg_ref[...] = dg

        @pl.when(i > 0)
        def _():
            loss_ref[...] += jnp.broadcast_to(part, loss_ref.shape)
            dg_ref[...] += dg

    row = pl.BlockSpec((tb, d), lambda i: (i, 0))
    vec = pl.BlockSpec((1, d), lambda i: (0, 0))
    return pl.pallas_call(
        body, name="loss_head",
        grid=(t // tb,),
        in_specs=[row, row, vec, row],
        out_specs=[pl.BlockSpec((1, LANES), lambda i: (0, 0)), row, row, vec],
        out_shape=[jax.ShapeDtypeStruct((1, LANES), F32), jax.ShapeDtypeStruct((t, d), F32),
                   jax.ShapeDtypeStruct((t, d), BF16), jax.ShapeDtypeStruct((1, d), F32)],
        compiler_params=_cparams(dimension_semantics=("arbitrary",)),
    )(x3, f, g2, target)


def _hg_consts(tb):
    c = A_CHUNK
    r = lax.broadcasted_iota(jnp.int32, (c, c), 0)
    s = lax.broadcasted_iota(jnp.int32, (c, c), 1)
    br = lax.broadcasted_iota(jnp.int32, (tb, tb), 0)
    bs = lax.broadcasted_iota(jnp.int32, (tb, tb), 1)
    shift = c.bit_length() - 1
    same_chunk = jnp.right_shift(br, shift) == jnp.right_shift(bs, shift)
    return (s <= r).astype(F32), (r <= s).astype(F32), jnp.logical_and(same_chunk, bs <= br)


def _chunk_apply(mat, x):
    c = mat.shape[0]
    return jnp.concatenate([_f32dot(mat, x[i * c:(i + 1) * c]) for i in range(x.shape[0] // c)], axis=0)


@jax.custom_vjp
def _chunk_cumsum(x, tri, tri_t):
    return _chunk_apply(tri, x)


_chunk_cumsum.defvjp(lambda x, tri, tri_t: (_chunk_apply(tri, x), (tri, tri_t)),
                     lambda r, g: (_chunk_apply(r[1], g), jnp.zeros_like(r[0]), jnp.zeros_like(r[1])))


def _per_chunk(a, b, dims):
    return jnp.stack([_bdot_raw(a[i], b[i], dims) for i in range(a.shape[0])])


@jax.custom_vjp
def _chunk_tn(a, b):
    return _per_chunk(a, b, _TN)


@jax.custom_vjp
def _chunk_nt(a, b):
    return _per_chunk(a, b, _NT)


@jax.custom_vjp
def _chunk_nn(a, b):
    return _per_chunk(a, b, _NN)


_chunk_tn.defvjp(lambda a, b: (_per_chunk(a, b, _TN), (a, b)),
                 lambda r, g: (_chunk_nt(r[1], g), _chunk_nn(r[0], g)))
_chunk_nt.defvjp(lambda a, b: (_per_chunk(a, b, _NT), (a, b)),
                 lambda r, g: (_chunk_nn(g, r[1]), _chunk_tn(g, r[0])))
_chunk_nn.defvjp(lambda a, b: (_per_chunk(a, b, _NN), (a, b)),
                 lambda r, g: (_chunk_nt(g, r[1]), _chunk_tn(r[0], g)))


def _scan_states(decay, m, st):
    sts = []
    for i in range(m.shape[0]):
        sts.append(st)
        st = st * decay[i] + m[i]
    return jnp.stack(sts), st


@jax.custom_vjp
def _state_scan(decay, m, st):
    return _scan_states(decay, m, st)


def _state_scan_fwd(decay, m, st):
    sts, st_out = _scan_states(decay, m, st)
    return (sts, st_out), (decay, sts)


def _state_scan_bwd(res, cts):
    decay, sts = res
    d_sts, g = cts
    d_decay, d_m = [], []
    for i in range(sts.shape[0] - 1, -1, -1):
        d_m.append(g)
        d_decay.append(jnp.sum(g * sts[i], axis=0, keepdims=True))
        g = g * decay[i] + d_sts[i]
    return jnp.stack(d_decay[::-1]), jnp.stack(d_m[::-1]), g


_state_scan.defvjp(_state_scan_fwd, _state_scan_bwd)


def _hg_block(qp, fp, ip, gp, lb, ng, st, tri, tri_t, bd_causal):
    tb = qp.shape[0]
    c = A_CHUNK
    n = tb // c
    q = _silu(qp)
    fg = lb + (1.0 - lb) * _sigmoid(fp)
    logf = jnp.log(fg)
    k = 1.0 - fg
    b3 = _chunk_cumsum(logf, tri, tri_t).reshape(n, c, HEAD)
    pos = lax.broadcasted_iota(jnp.int32, (1, c, 1), 1)
    b_mid = lax.stop_gradient(jnp.sum(jnp.where(pos == c // 2, b3, 0.0), axis=1, keepdims=True))
    b_last = jnp.sum(jnp.where(pos == c - 1, b3, 0.0), axis=1, keepdims=True)
    q3, k3, v3 = q.reshape(n, c, HEAD), k.reshape(n, c, HEAD), ip.reshape(n, c, HEAD)
    scores = _dot_nt((q3 * jnp.exp(b3 - b_mid)).reshape(tb, HEAD), (k3 * jnp.exp(b_mid - b3)).reshape(tb, HEAD))
    o_intra = _dot_nn(jnp.where(bd_causal, scores, 0.0), ip)
    states, st_new = _state_scan(jnp.exp(b_last), _chunk_tn(v3, k3 * jnp.exp(b_last - b3)), st)
    o = o_intra + _chunk_nt(q3 * jnp.exp(b3), states).reshape(tb, HEAD)
    y = _rms(o) * ng * _silu(gp)
    return y, st_new


def _hg_specs(tb, nh, rev_nb=None):
    def row(off):
        if rev_nb is None:
            return pl.BlockSpec((tb, HEAD), functools.partial(lambda h, i, off: (i, off + h), off=off))
        return pl.BlockSpec((tb, HEAD), functools.partial(lambda h, i, off: (rev_nb - 1 - i, off + h), off=off))
    return [row(0), row(nh), row(2 * nh), row(3 * nh),
            pl.BlockSpec((1, HEAD), lambda h, i: (0, h)), pl.BlockSpec((1, HEAD), lambda h, i: (0, 0))]


def _hgrn2_fwd(proj, lb, ng, tb):
    t = proj.shape[0]
    nh = proj.shape[1] // (4 * HEAD)
    tb = min(tb, t)
    nb = t // tb

    def body(q_ref, f_ref, i_ref, g_ref, lb_ref, ng_ref, y_ref, s_ref, st_ref):
        i = pl.program_id(1)

        @pl.when(i == 0)
        def _():
            st_ref[...] = jnp.zeros_like(st_ref)

        st = st_ref[...]
        s_ref[0, 0] = st
        y, st_new = _hg_block(q_ref[...], f_ref[...], i_ref[...], g_ref[...], lb_ref[...], ng_ref[...], st,
                              *_hg_consts(tb))
        y_ref[...] = y.astype(y_ref.dtype)
        st_ref[...] = st_new

    return pl.pallas_call(
        body, name="hgrn2_fwd",
        grid=(nh, nb),
        in_specs=_hg_specs(tb, nh),
        out_specs=[pl.BlockSpec((tb, HEAD), lambda h, i: (i, h)),
                   pl.BlockSpec((1, 1, HEAD, HEAD), lambda h, i: (h, i, 0, 0))],
        out_shape=[jax.ShapeDtypeStruct((t, nh * HEAD), BF16),
                   jax.ShapeDtypeStruct((nh, nb, HEAD, HEAD), F32)],
        scratch_shapes=[pltpu.VMEM((HEAD, HEAD), F32)],
        compiler_params=_cparams(dimension_semantics=("parallel", "arbitrary")),
    )(proj, proj, proj, proj, lb, ng)


def _hgrn2_bwd(proj, lb, ng, states, dy, tb):
    t = proj.shape[0]
    nh = proj.shape[1] // (4 * HEAD)
    tb = min(tb, t)
    nb = t // tb

    def body(q_ref, f_ref, i_ref, g_ref, lb_ref, ng_ref, s_ref, dy_ref,
             dq_ref, df_ref, di_ref, dg_ref, dlb_ref, dng_ref, dst_ref):
        h, i = pl.program_id(0), pl.program_id(1)
        consts = _hg_consts(tb)

        @pl.when(i == 0)
        def _():
            dst_ref[...] = jnp.zeros_like(dst_ref)
            dlb_ref[...] = jnp.zeros_like(dlb_ref)

        @pl.when(jnp.logical_and(i == 0, h == 0))
        def _():
            dng_ref[...] = jnp.zeros_like(dng_ref)

        def fn(qp, fp, ip, gp, lbx, ngx, stx):
            return _hg_block(qp, fp, ip, gp, lbx, ngx, stx, *consts)

        _, vjp_fn = jax.vjp(fn, q_ref[...], f_ref[...], i_ref[...], g_ref[...], lb_ref[...], ng_ref[...],
                            s_ref[0, 0])
        gq, gf, gi, gg, glb, gng, dst = vjp_fn((dy_ref[...].astype(F32), dst_ref[...]))
        dq_ref[...] = gq.astype(dq_ref.dtype)
        df_ref[...] = gf.astype(df_ref.dtype)
        di_ref[...] = gi.astype(di_ref.dtype)
        dg_ref[...] = gg.astype(dg_ref.dtype)
        dst_ref[...] = dst
        dlb_ref[...] += glb
        dng_ref[...] += gng

    rev = lambda h, i: (nb - 1 - i, h)
    slab = jax.ShapeDtypeStruct((t, nh * HEAD), BF16)
    return pl.pallas_call(
        body, name="hgrn2_bwd",
        grid=(nh, nb),
        in_specs=_hg_specs(tb, nh, rev_nb=nb) + [
            pl.BlockSpec((1, 1, HEAD, HEAD), lambda h, i: (h, nb - 1 - i, 0, 0)),
            pl.BlockSpec((tb, HEAD), rev)],
        out_specs=[pl.BlockSpec((tb, HEAD), rev)] * 4 + [
            pl.BlockSpec((1, HEAD), lambda h, i: (0, h)), pl.BlockSpec((1, HEAD), lambda h, i: (0, 0))],
        out_shape=[slab, slab, slab, slab,
                   jax.ShapeDtypeStruct((1, nh * HEAD), F32), jax.ShapeDtypeStruct((1, HEAD), F32)],
        scratch_shapes=[pltpu.VMEM((HEAD, HEAD), F32)],
        compiler_params=_cparams(dimension_semantics=("arbitrary", "arbitrary")),
    )(proj, proj, proj, proj, lb, ng, states, dy)


def _fgate_consts(cb):
    r = lax.broadcasted_iota(jnp.int32, (cb, cb), 0)
    s = lax.broadcasted_iota(jnp.int32, (cb, cb), 1)
    return (r <= s).astype(F32), (r >= s).astype(F32)


def _fgate_fwd(xt, bias, cb=512):
    nh, t = xt.shape
    cb = min(cb, t)

    def body(x_ref, b_ref, o_ref):
        upper, _ = _fgate_consts(cb)
        carry = jnp.zeros((nh, 1), F32)
        for blk in range(t // cb):
            z = x_ref[:, blk * cb:(blk + 1) * cb] + b_ref[...]
            logf = jnp.minimum(z, 0.0) - jnp.log(1.0 + jnp.exp(-jnp.abs(z)))
            cs = _f32dot(logf, upper) + carry
            o_ref[:, blk * cb:(blk + 1) * cb] = cs
            carry = cs[:, cb - 1:cb]

    vm = pl.BlockSpec(memory_space=pltpu.VMEM)
    return pl.pallas_call(
        body, name="fgate_fwd", in_specs=[vm, vm], out_specs=vm,
        out_shape=jax.ShapeDtypeStruct((nh, t), F32), compiler_params=_cparams(),
    )(xt, bias)


def _fgate_bwd(xt, bias, dft, cb=512):
    nh, t = xt.shape
    cb = min(cb, t)
    nblk = t // cb

    def body(x_ref, b_ref, d_ref, dx_ref, db_ref):
        _, lower = _fgate_consts(cb)
        carry = jnp.zeros((nh, 1), F32)
        db = jnp.zeros((nh, 1), F32)
        for blk in range(nblk - 1, -1, -1):
            sl = slice(blk * cb, (blk + 1) * cb)
            dlogf = _f32dot(d_ref[:, sl], lower) + carry
            carry = dlogf[:, 0:1]
            z = x_ref[:, sl] + b_ref[...]
            dz = dlogf * (1.0 - _sigmoid(z))
            dx_ref[:, sl] = dz
            db = db + jnp.sum(dz, axis=1, keepdims=True)
        db_ref[...] = db

    vm = pl.BlockSpec(memory_space=pltpu.VMEM)
    return pl.pallas_call(
        body, name="fgate_bwd", in_specs=[vm, vm, vm], out_specs=[vm, vm],
        out_shape=[jax.ShapeDtypeStruct((nh, t), F32), jax.ShapeDtypeStruct((nh, 1), F32)],
        compiler_params=_cparams(),
    )(xt, bias, dft)


def _attn_fwd(q, k, v, f_col, f_row, blk):
    t, width = q.shape
    nh = width // HEAD
    nq = t // blk

    def body(q_ref, k_ref, v_ref, fc_ref, fr_ref, o_ref, lse_ref):
        i = pl.program_id(0)
        tri = (lax.broadcasted_iota(jnp.int32, (blk, blk), 1) <= lax.broadcasted_iota(jnp.int32, (blk, blk), 0))
        for h in range(nh):
            cs = slice(h * HEAD, (h + 1) * HEAD)
            qh = q_ref[:, cs]
            fq = fc_ref[:, h:h + 1]

            def tile(j, carry, masked):
                m, l, acc = carry
                rs = pl.ds(pl.multiple_of(j * blk, blk), blk)
                s = _bdot_raw(qh, k_ref[rs, cs], _NT) + (fq - fr_ref[j, h:h + 1, :])
                if masked:
                    s = jnp.where(tri, s, NEG_INF)
                m_new = jnp.maximum(m, jnp.max(s, axis=1, keepdims=True))
                p = jnp.exp(s - m_new)
                alpha = jnp.exp(m - m_new)
                l_new = alpha * l + jnp.sum(p, axis=1, keepdims=True)
                acc_new = alpha * acc + _bdot_raw(p, v_ref[rs, cs], _NN)
                return m_new, l_new, acc_new

            init = (jnp.full((blk, 1), NEG_INF, F32), jnp.zeros((blk, 1), F32), jnp.zeros((blk, HEAD), F32))
            carry = lax.fori_loop(0, i, lambda j, c: tile(j, c, False), init)
            m, l, acc = tile(i, carry, True)
            o_ref[:, cs] = acc / l
            lse_ref[:, h:h + 1] = m + jnp.log(l)

    vm = pl.BlockSpec(memory_space=pltpu.VMEM)
    return pl.pallas_call(
        body, name="fox_attn_fwd",
        grid=(nq,),
        in_specs=[pl.BlockSpec((blk, width), lambda i: (i, 0)), vm, vm,
                  pl.BlockSpec((blk, nh), lambda i: (i, 0)), vm],
        out_specs=[pl.BlockSpec((blk, width), lambda i: (i, 0)), pl.BlockSpec((blk, nh), lambda i: (i, 0))],
        out_shape=[jax.ShapeDtypeStruct((t, width), F32), jax.ShapeDtypeStruct((t, nh), F32)],
        compiler_params=_cparams(dimension_semantics=("parallel",)),
    )(q, k, v, f_col, f_row)


def _attn_bwd_dq(q, k, v, f_col, f_row, o, do, lse, blk):
    t, width = q.shape
    nh = width // HEAD
    nq = t // blk

    def body(q_ref, k_ref, v_ref, fc_ref, fr_ref, o_ref, do_ref, lse_ref, dq_ref, dfc_ref, dl_ref):
        i = pl.program_id(0)
        tri = (lax.broadcasted_iota(jnp.int32, (blk, blk), 1) <= lax.broadcasted_iota(jnp.int32, (blk, blk), 0))
        for h in range(nh):
            cs = slice(h * HEAD, (h + 1) * HEAD)
            qh = q_ref[:, cs]
            doh = do_ref[:, cs]
            bias = fc_ref[:, h:h + 1] - lse_ref[:, h:h + 1]
            delta = jnp.sum(doh.astype(F32) * o_ref[:, cs], axis=1, keepdims=True)

            def tile(j, carry, masked):
                dq, dfq = carry
                rs = pl.ds(pl.multiple_of(j * blk, blk), blk)
                kj = k_ref[rs, cs]
                p = jnp.exp(_bdot_raw(qh, kj, _NT) + (bias - fr_ref[j, h:h + 1, :]))
                if masked:
                    p = jnp.where(tri, p, 0.0)
                ds = p * (_bdot_raw(doh, v_ref[rs, cs], _NT) - delta)
                return dq + _bdot_raw(ds, kj, _NN), dfq + jnp.sum(ds, axis=1, keepdims=True)

            carry = lax.fori_loop(0, i, lambda j, c: tile(j, c, False),
                                  (jnp.zeros((blk, HEAD), F32), jnp.zeros((blk, 1), F32)))
            dq, dfq = tile(i, carry, True)
            dq_ref[:, cs] = dq
            dfc_ref[:, h:h + 1] = dfq
            dl_ref[:, h:h + 1] = delta

    vm = pl.BlockSpec(memory_space=pltpu.VMEM)
    wide = pl.BlockSpec((blk, width), lambda i: (i, 0))
    thin = pl.BlockSpec((blk, nh), lambda i: (i, 0))
    return pl.pallas_call(
        body, name="fox_attn_bwd_dq",
        grid=(nq,),
        in_specs=[wide, vm, vm, thin, vm, wide, wide, thin],
        out_specs=[wide, thin, thin],
        out_shape=[jax.ShapeDtypeStruct((t, width), F32), jax.ShapeDtypeStruct((t, nh), F32),
                   jax.ShapeDtypeStruct((t, nh), F32)],
        compiler_params=_cparams(dimension_semantics=("parallel",)),
    )(q, k, v, f_col, f_row, o, do, lse)


def _attn_bwd_dkv(q, k, v, f_col, f_row, do, lse, delta, blk):
    t, width = q.shape
    nh = width // HEAD
    nq = t // blk

    def body(q_ref, k_ref, v_ref, fc_ref, fr_ref, do_ref, lse_ref, dl_ref, dk_ref, dv_ref, dfr_ref,
             s_ref, dp_ref, p_ref, ds_ref, dfs_ref, dk_acc, dv_acc):
        j = pl.program_id(0)
        for h in range(nh):
            cs = slice(h * HEAD, (h + 1) * HEAD)
            fs = fr_ref[0, h:h + 1, :]
            dfs_ref[...] = jnp.zeros_like(dfs_ref)
            dk_acc[...] = jnp.zeros_like(dk_acc)
            dv_acc[...] = jnp.zeros_like(dv_acc)

            def tile(i, masked):
                base = pl.multiple_of(i * blk, blk)
                rs = pl.ds(base, blk)
                s_ref[...] = _bdot_raw(q_ref[rs, cs], k_ref[:, cs], _NT)
                dp_ref[...] = _bdot_raw(do_ref[rs, cs], v_ref[:, cs], _NT)

                def rows(rr, first_row):
                    gr = pl.ds(pl.multiple_of(base + first_row, ATTN_ROWS), ATTN_ROWS)
                    bias = fc_ref[gr, h:h + 1] - lse_ref[gr, h:h + 1]
                    p = jnp.exp(s_ref[rr, :] + (bias - fs))
                    if masked:
                        p = jnp.where(_chunk_causal(first_row, blk), p, 0.0)
                    ds = p * (dp_ref[rr, :] - dl_ref[gr, h:h + 1])
                    dfs_ref[...] -= jnp.sum(ds, axis=0, keepdims=True)
                    p_ref[rr, :] = p.astype(p_ref.dtype)
                    ds_ref[rr, :] = ds.astype(ds_ref.dtype)

                _row_chunks(blk, rows)
                dv_acc[...] += _bdot_raw(p_ref[...], do_ref[rs, cs], _TN)
                dk_acc[...] += _bdot_raw(ds_ref[...], q_ref[rs, cs], _TN)

            def off_diagonal(i, carry):
                tile(i, False)
                return carry

            tile(j, True)
            lax.fori_loop(j + 1, nq, off_diagonal, 0)
            dk_ref[:, cs] = dk_acc[...]
            dv_ref[:, cs] = dv_acc[...]
            dfr_ref[0, h:h + 1, :] = dfs_ref[...]

    vm = pl.BlockSpec(memory_space=pltpu.VMEM)
    wide = pl.BlockSpec((blk, width), lambda j: (j, 0))
    frow = pl.BlockSpec((1, nh, blk), lambda j: (j, 0, 0))
    tile_f32, tile_b16 = pltpu.VMEM((blk, blk), F32), pltpu.VMEM((blk, blk), BF16)
    return pl.pallas_call(
        body, name="fox_attn_bwd_dkv",
        grid=(nq,),
        in_specs=[vm, wide, wide, vm, frow, vm, vm, vm],
        out_specs=[wide, wide, frow],
        out_shape=[jax.ShapeDtypeStruct((t, width), F32), jax.ShapeDtypeStruct((t, width), F32),
                   jax.ShapeDtypeStruct((nq, nh, blk), F32)],
        scratch_shapes=[tile_f32, tile_f32, tile_b16, tile_b16, pltpu.VMEM((1, blk), F32),
                        pltpu.VMEM((blk, HEAD), F32), pltpu.VMEM((blk, HEAD), F32)],
        compiler_params=_cparams(dimension_semantics=("parallel",)),
    )(q, k, v, f_col, f_row, do, lse, delta)


def _attn_delta(do, o, tb):
    t, width = o.shape
    nh = width // HEAD
    tb = min(tb, t)

    def body(do_ref, o_ref, dl_ref):
        for h in range(nh):
            cs = slice(h * HEAD, (h + 1) * HEAD)
            dl_ref[:, h:h + 1] = jnp.sum(do_ref[:, cs].astype(F32) * o_ref[:, cs], axis=1, keepdims=True)

    wide = pl.BlockSpec((tb, width), lambda i: (i, 0))
    return pl.pallas_call(body, name="fox_attn_delta", grid=(t // tb,), in_specs=[wide, wide],
                          out_specs=pl.BlockSpec((tb, nh), lambda i: (i, 0)),
                          out_shape=jax.ShapeDtypeStruct((t, nh), F32),
                          compiler_params=_cparams(dimension_semantics=("parallel",)))(do, o)


ATTN_BWD_GROUPS = 2
ATTN_BWD_VMEM = 56 * 1024 * 1024


def _attn_bwd(q, k, v, f_col, f_row, do, lse, delta, blk):
    t, width = q.shape
    nh = width // HEAD
    nq = t // blk
    hpg = nh // ATTN_BWD_GROUPS
    gw = hpg * HEAD

    def body(q_ref, do_ref, k_ref, v_ref, fc_ref, fr_ref, lse_ref, dl_ref,
             dq_ref, dk_ref, dv_ref, dfc_ref, dfr_ref):
        g, j = pl.program_id(0), pl.program_id(1)
        tri = (lax.broadcasted_iota(jnp.int32, (blk, blk), 1) <= lax.broadcasted_iota(jnp.int32, (blk, blk), 0))

        @pl.when(j == 0)
        def _():
            dq_ref[...] = jnp.zeros_like(dq_ref)
            dfc_ref[...] = jnp.zeros_like(dfc_ref)

        for h in range(hpg):
            cs = slice(h * HEAD, (h + 1) * HEAD)
            kj = k_ref[:, cs]
            vj = v_ref[:, cs]
            fs = fr_ref[0, 0, h:h + 1, :]

            def tile(i, carry, masked):
                dk, dv, dfs = carry
                rs = pl.ds(pl.multiple_of(i * blk, blk), blk)
                qi = q_ref[rs, cs]
                doi = do_ref[rs, cs]
                bias = fc_ref[0, rs, h:h + 1] - lse_ref[0, rs, h:h + 1]
                p = jnp.exp(_bdot_raw(qi, kj, _NT) + (bias - fs))
                if masked:
                    p = jnp.where(tri, p, 0.0)
                ds = p * (_bdot_raw(doi, vj, _NT) - dl_ref[0, rs, h:h + 1])
                dsb = ds.astype(BF16)
                dq_ref[rs, cs] += _bdot_raw(dsb, kj, _NN)
                dfc_ref[0, rs, h:h + 1] += jnp.sum(ds, axis=1, keepdims=True)
                return (dk + _bdot_raw(dsb, qi, _TN), dv + _bdot_raw(p, doi, _TN),
                        dfs - jnp.sum(ds, axis=0, keepdims=True))

            init = (jnp.zeros((blk, HEAD), F32), jnp.zeros((blk, HEAD), F32), jnp.zeros((1, blk), F32))
            carry = tile(j, init, True)
            dk, dv, dfs = lax.fori_loop(j + 1, nq, lambda i, c: tile(i, c, False), carry)
            dk_ref[:, cs] = dk
            dv_ref[:, cs] = dv
            dfr_ref[0, 0, h:h + 1, :] = dfs

    by_group = lambda a: a.reshape(t, ATTN_BWD_GROUPS, hpg).transpose(1, 0, 2)
    fr_g = f_row.reshape(nq, ATTN_BWD_GROUPS, hpg, blk).transpose(1, 0, 2, 3)
    resident = pl.BlockSpec((t, gw), lambda g, j: (0, g), pipeline_mode=pl.Buffered(1))
    stat = pl.BlockSpec((1, t, hpg), lambda g, j: (g, 0, 0), pipeline_mode=pl.Buffered(1))
    kv_blk = pl.BlockSpec((blk, gw), lambda g, j: (j, g))
    frow = pl.BlockSpec((1, 1, hpg, blk), lambda g, j: (g, j, 0, 0))
    dq, dk, dv, dfc, dfr = pl.pallas_call(
        body, name="fox_attn_bwd",
        grid=(ATTN_BWD_GROUPS, nq),
        in_specs=[resident, resident, kv_blk, kv_blk, stat, frow, stat, stat],
        out_specs=[pl.BlockSpec((t, gw), lambda g, j: (0, g)), kv_blk, kv_blk,
                   pl.BlockSpec((1, t, hpg), lambda g, j: (g, 0, 0)), frow],
        out_shape=[jax.ShapeDtypeStruct((t, width), F32), jax.ShapeDtypeStruct((t, width), F32),
                   jax.ShapeDtypeStruct((t, width), F32), jax.ShapeDtypeStruct((ATTN_BWD_GROUPS, t, hpg), F32),
                   jax.ShapeDtypeStruct((ATTN_BWD_GROUPS, nq, hpg, blk), F32)],
        compiler_params=pltpu.CompilerParams(vmem_limit_bytes=ATTN_BWD_VMEM,
                                             dimension_semantics=("parallel", "arbitrary")),
    )(q, do, k, v, by_group(f_col), fr_g, by_group(lse), by_group(delta))
    return (dq, dk, dv, dfc.transpose(1, 0, 2).reshape(t, nh),
            dfr.transpose(1, 0, 2, 3).reshape(nq, nh, blk))


SUBLANES = 8


def _shift_down(u, n):
    r = pltpu.roll(u, n, 0)
    row = lax.broadcasted_iota(jnp.int32, (SUBLANES, u.shape[1]), 0)
    return jnp.concatenate([jnp.where(row < n, 0.0, r[:SUBLANES]), r[SUBLANES:]], axis=0)


def _shift_up(u, n):
    t = u.shape[0]
    r = pltpu.roll(u, t - n, 0)
    row = lax.broadcasted_iota(jnp.int32, (SUBLANES, u.shape[1]), 0)
    return jnp.concatenate([r[:t - SUBLANES], jnp.where(row >= SUBLANES - n, 0.0, r[t - SUBLANES:])], axis=0)


def _convglu_specs(t):
    return [pl.BlockSpec((2, t, LANES), lambda j: (0, 0, j)),
            pl.BlockSpec((2, CONV_TAPS, LANES), lambda j: (0, 0, j)),
            pl.BlockSpec((2, 1, LANES), lambda j: (0, 0, j))]


def _convglu_fwd(u, cw, cb):
    _, t, fp = u.shape

    def body(u_ref, w_ref, b_ref, a_ref):
        c = []
        for hf in range(2):
            uv, w = u_ref[hf], w_ref[hf]
            c.append(w[0:1] * _shift_down(uv, 2) + w[1:2] * _shift_down(uv, 1) + w[2:3] * uv + b_ref[hf])
        a_ref[...] = (_silu(c[0]) * c[1]).astype(a_ref.dtype)

    return pl.pallas_call(
        body, name="convglu_fwd",
        grid=(fp // LANES,),
        in_specs=_convglu_specs(t),
        out_specs=pl.BlockSpec((t, LANES), lambda j: (0, j)),
        out_shape=jax.ShapeDtypeStruct((t, fp), BF16),
        compiler_params=_cparams(dimension_semantics=("parallel",)),
    )(u, cw, cb)


def _convglu_bwd(u, cw, cb, da):
    _, t, fp = u.shape

    def body(u_ref, w_ref, b_ref, da_ref, du_ref, dw_ref, db_ref):
        us, c = [], []
        for hf in range(2):
            uv, w = u_ref[hf], w_ref[hf]
            u1, u2 = _shift_down(uv, 1), _shift_down(uv, 2)
            us.append((uv, u1, u2))
            c.append(w[0:1] * u2 + w[1:2] * u1 + w[2:3] * uv + b_ref[hf])
        gc, vc = c
        sg = _sigmoid(gc)
        dav = da_ref[...].astype(F32)
        dcs = [dav * vc * (sg * (1.0 + gc * (1.0 - sg))), dav * (gc * sg)]
        for hf in range(2):
            dc, w = dcs[hf], w_ref[hf]
            uv, u1, u2 = us[hf]
            du = w[2:3] * dc + w[1:2] * _shift_up(dc, 1) + w[0:1] * _shift_up(dc, 2)
            du_ref[hf] = du.astype(du_ref.dtype)
            dw_ref[hf, 0:1, :] = jnp.sum(dc * u2, axis=0, keepdims=True)
            dw_ref[hf, 1:2, :] = jnp.sum(dc * u1, axis=0, keepdims=True)
            dw_ref[hf, 2:3, :] = jnp.sum(dc * uv, axis=0, keepdims=True)
            db_ref[hf] = jnp.sum(dc, axis=0, keepdims=True)

    specs = _convglu_specs(t)
    return pl.pallas_call(
        body, name="convglu_bwd",
        grid=(fp // LANES,),
        in_specs=specs + [pl.BlockSpec((t, LANES), lambda j: (0, j))],
        out_specs=specs,
        out_shape=[jax.ShapeDtypeStruct((2, t, fp), BF16), jax.ShapeDtypeStruct((2, CONV_TAPS, fp), F32),
                   jax.ShapeDtypeStruct((2, 1, fp), F32)],
        compiler_params=_cparams(dimension_semantics=("parallel",)),
    )(u, cw, cb, da)


def _local_step(x, target, mods, lb, small, get_w, put_g, *, tb=512, attn_blk=512):
    t, d = x.shape
    nh = d // HEAD
    nb = NDEV
    wts = {}
    vec = lambda *names: [mods[n] for n in names]

    def ffn_fwd(h2, l):
        u = _mm_wblk(h2, wts[f"up{l}"], F32, f"ffn{l}_up", gb=nb // 2, split=2, tm=512)
        a = _convglu_fwd(u, small[f"conv_w{l}"], small[f"conv_b{l}"])
        f = _mm(a, wts[f"down{l}"], "nn", F32, f"ffn{l}_down", tk=4096)
        return u, a, f

    def ffn_bwd(df, h2, u, a, l):
        da = _mm(df, wts[f"down{l}"], "nt", BF16, f"ffn{l}_down_dx", tn=1536)
        dwd = _mm(a, df, "tn", BF16, f"ffn{l}_down_dw", tm=1536, tk=1024)
        du, dcw, dcb = _convglu_bwd(u, small[f"conv_w{l}"], small[f"conv_b{l}"], da)
        dh2 = _mm_wblk_dx(du, wts[f"up{l}"], F32, f"ffn{l}_up_dx", k=d, gb=nb // 2, split=2, tm=1024)
        dwu = _mm_wblk_dw(h2, du, f"ffn{l}_up_dw", nb=nb, gb=1, split=2, tk=t)
        return dh2, dwu, dwd, dcw, dcb

    (h_a,) = _row_fwd(_f_mod, [(x, d, 0)], vec("sh1_0", "sc1_0"), [BF16], tb=tb, name="l0_mod1")
    wts.update(get_w("l0a", h_a))
    proj_a = _mm_wblk(h_a, wts["a_in"], F32, "a_in", gb=nb // 2)
    ypre, states = _hgrn2_fwd(proj_a, lb, small["a_norm_g"], tb)
    wts.update(get_w("l0b", ypre))
    y_a = _mm(ypre, wts["a_out"], "nn", F32, "a_out")
    x1, h2_0 = _row_fwd(_f_res_mod, [(x, d, 0), (y_a, d, 0)], vec("g1_0", "sh2_0", "sc2_0"), [F32, BF16],
                        tb=tb, name="l0_res_mod2")
    u0, a0, f0 = ffn_fwd(h2_0, 0)
    x2, h_kv, h_q = _row_fwd(_f_res_mod2, [(x1, d, 0), (f0, d, 0)],
                             vec("g2_0", "kv_sh", "kv_sc", "sh1_1", "sc1_1"), [F32, BF16, BF16],
                             tb=tb, name="l0_res_kvmod_qmod")
    wts.update(get_w("l1", h_kv))
    proj_kv = _mm(h_kv, wts["kv"], "nn", F32, "kv_proj")
    proj_f = _mm(h_kv, wts["kv_f"], "nn", F32, "kv_fproj")
    (k_n,) = _row_fwd(_f_knorm, [(proj_kv, HEAD, 0)], [small["k_norm_g"]], [BF16], nsub=nh, tb=tb, name="k_norm")
    v_b = proj_kv[:, d:].astype(BF16)
    f_logit_t = proj_f[:, :nh].T
    f_bias = small["kv_b_f"].reshape(nh, 1)
    f_t = _fgate_fwd(f_logit_t, f_bias)
    f_col = f_t.T
    f_row = f_t.reshape(nh, t // attn_blk, attn_blk).transpose(1, 0, 2)
    proj_q = _mm_wblk(h_q, wts["b_q"], F32, "b_q", gb=nb)
    (q_n,) = _row_fwd(_f_qnorm, [(proj_q, HEAD, 0)], [small["q_norm_g"]], [BF16], nsub=nh, tb=tb, name="q_norm")
    o_att, lse = _attn_fwd(q_n, k_n, v_b, f_col, f_row, attn_blk)
    (z,) = _row_fwd(_f_outgate, [(o_att, HEAD, 0), (proj_q, HEAD, 1)], [], [BF16], nsub=nh, tb=tb, name="out_gate")
    y_b = _mm(z, wts["b_out"], "nn", F32, "b_out")
    x3, h2_1 = _row_fwd(_f_res_mod, [(x2, d, 0), (y_b, d, 0)], vec("g1_1", "sh2_1", "sc2_1"), [F32, BF16],
                        tb=tb, name="l1_res_mod2")
    u1, a1, f1 = ffn_fwd(h2_1, 1)
    loss, dx4, df1, dg2_1 = _loss_call(x3, f1, mods["g2_1"], target, tb)

    g = {}
    dmods = {"g2_1": dg2_1}
    dh2, g["up1"], g["down1"], g["conv_w1"], g["conv_b1"] = ffn_bwd(df1, h2_1, u1, a1, 1)
    (dx2, dy_b), (dmods["g1_1"], dmods["sh2_1"], dmods["sc2_1"]) = _row_bwd(
        _f_res_mod, [(x2, d, 0), (y_b, d, 0)], vec("g1_1", "sh2_1", "sc2_1"),
        [(dx4, d, 0), (dh2, d, 0)], [F32, BF16], tb=tb, name="l1_res_mod2_bwd")
    dz = _mm(dy_b, wts["b_out"], "nt", F32, "b_out_dx")
    g["b_out"] = _mm(z, dy_b, "tn", BF16, "b_out_dw", tk=1024)
    (do_att, dog), _ = _row_bwd(_f_outgate, [(o_att, HEAD, 0), (proj_q, HEAD, 1)], [], [(dz, HEAD, 0)],
                                [BF16, BF16], nsub=nh, tb=tb, name="out_gate_bwd")
    delta = _attn_delta(do_att, o_att, tb)
    dq_n, dk_n, dv, dfc_q, dfr_k = _attn_bwd(q_n, k_n, v_b, f_col, f_row, do_att, lse, delta, attn_blk)
    (dpq,), (g["q_norm_g"],) = _row_bwd(_f_qnorm, [(proj_q, HEAD, 0)], [small["q_norm_g"]],
                                        [(dq_n, HEAD, 0)], [BF16], nsub=nh, tb=tb, name="q_norm_bwd")
    dproj_q = jnp.concatenate([dpq, dog], axis=1)
    dh_q = _mm_wblk_dx(dproj_q, wts["b_q"], F32, "b_q_dx", k=d, gb=nb)
    g["b_q"] = _mm_wblk_dw(h_q, dproj_q, "b_q_dw", nb=nb, gb=nb // 4, tk=t)
    (dpk,), (g["k_norm_g"],) = _row_bwd(_f_knorm, [(proj_kv, HEAD, 0)], [small["k_norm_g"]],
                                        [(dk_n, HEAD, 0)], [BF16], nsub=nh, tb=tb, name="k_norm_bwd")
    dproj_kv = jnp.concatenate([dpk, dv.astype(BF16)], axis=1)
    df_t = dfc_q.T + dfr_k.transpose(1, 0, 2).reshape(nh, t)
    dflogit_t, g["kv_b_f"] = _fgate_bwd(f_logit_t, f_bias, df_t)
    dproj_f = jnp.pad(dflogit_t.T, ((0, 0), (0, LANES - nh))).astype(BF16)
    dh_kv = _mm(dproj_kv, wts["kv"], "nt", F32, "kv_proj_dx")
    dh_kv_f = _mm(dproj_f, wts["kv_f"], "nt", F32, "kv_fproj_dx")
    g["kv"] = _mm(h_kv, dproj_kv, "tn", BF16, "kv_proj_dw", tk=1024)
    g["kv_f"] = _mm(h_kv, dproj_f, "tn", F32, "kv_fproj_dw", tk=1024)
    sent = put_g("l1", {n: g.pop(n) for n in ("b_out", "b_q", "kv", "kv_f", "up1", "down1")})
    (dx1, df0), (dmods["g2_0"], dmods["kv_sh"], dmods["kv_sc"], dmods["sh1_1"], dmods["sc1_1"]) = _row_bwd(
        _f_res_mod2, [(x1, d, 0), (f0, d, 0)], [mods["g2_0"] + sent] + vec("kv_sh", "kv_sc", "sh1_1", "sc1_1"),
        [(dx2, d, 0), (dh_kv, d, 0), (dh_q, d, 0)], [F32, BF16], tb=tb, name="l0_res_kvmod_qmod_bwd",
        cot_add=(1, dh_kv_f))
    dh2, g["up0"], g["down0"], g["conv_w0"], g["conv_b0"] = ffn_bwd(df0, h2_0, u0, a0, 0)
    (dx0, dy_a), (dmods["g1_0"], dmods["sh2_0"], dmods["sc2_0"]) = _row_bwd(
        _f_res_mod, [(x, d, 0), (y_a, d, 0)], vec("g1_0", "sh2_0", "sc2_0"),
        [(dx1, d, 0), (dh2, d, 0)], [F32, BF16], tb=tb, name="l0_res_mod2_bwd")
    dypre = _mm(dy_a, wts["a_out"], "nt", BF16, "a_out_dx")
    g["a_out"] = _mm(ypre, dy_a, "tn", BF16, "a_out_dw", tk=1024)
    sent = put_g("l0b", {n: g.pop(n) for n in ("a_out", "up0", "down0")})
    dpa_q, dpa_f, dpa_i, dpa_g, dlb, g["a_norm_g"] = _hgrn2_bwd(proj_a, lb + sent, small["a_norm_g"], states,
                                                               dypre, tb)
    dproj_a = jnp.concatenate([dpa_q, dpa_f, dpa_i, dpa_g], axis=1)
    dh_a = _mm_wblk_dx(dproj_a, wts["a_in"], F32, "a_in_dx", k=d, gb=nb, tm=512)
    put_g("l0a", {"a_in": _mm_wblk_dw(h_a, dproj_a, "a_in_dw", nb=nb, gb=1, tk=t)})
    (grad_x,), (dmods["sh1_0"], dmods["sc1_0"]) = _row_bwd(
        _f_mod, [(x, d, 0)], vec("sh1_0", "sc1_0"), [(dh_a, d, 0)], [F32], tb=tb, name="l0_mod1_bwd",
        add_to=(0, dx0))
    return loss, grad_x, dmods, dlb, g


def _position():
    return lax.axis_index("x"), lax.axis_index("y"), lax.axis_index("c")


def _hbm_specs(n):
    return [pl.BlockSpec(memory_space=pl.ANY)] * n


def _all_gather(arrs, name):
    n = len(arrs)

    def body(*refs):
        x_refs, out_refs = refs[:n], refs[n:2 * n]
        send_sems, recv_sems, local_sems = refs[2 * n:]
        x, y, cc = _position()
        me, sibling = (x, y, cc), (x, y, 1 - cc)
        chips = [(1 - x, y), (x, 1 - y), (1 - x, 1 - y)]

        def copy(a, k, block, to, src=None):
            slot = out_refs[a].at[4 * block[0] + 2 * block[1] + block[2]]
            return pltpu.make_async_remote_copy(
                src_ref=slot if src is None else src, dst_ref=slot,
                send_sem=send_sems.at[7 * a + k], recv_sem=recv_sems.at[7 * a + k],
                device_id=to, device_id_type=_MESH)

        local = [pltpu.make_async_copy(x_refs[a], out_refs[a].at[4 * x + 2 * y + cc], local_sems.at[a])
                 for a in range(n)]
        for cp in local:
            cp.start()
        first = []
        for a in range(n):
            first.append(copy(a, 0, me, sibling, src=x_refs[a]))
            first += [copy(a, 1 + j, me, (*chip, cc), src=x_refs[a]) for j, chip in enumerate(chips)]
        for cp in first:
            cp.start()
        passed = []
        for j, chip in enumerate(chips):
            for a in range(n):
                copy(a, 1 + j, (*chip, cc), me).wait_recv()
                fwd = copy(a, 4 + j, (*chip, cc), sibling)
                fwd.start()
                passed.append(fwd)
        for a in range(n):
            copy(a, 0, sibling, me).wait_recv()
        for j, chip in enumerate(chips):
            for a in range(n):
                copy(a, 4 + j, (*chip, 1 - cc), me).wait_recv()
        for cp in first + passed:
            cp.wait_send()
        for cp in local:
            cp.wait()

    return pl.pallas_call(
        body, name=name,
        out_shape=[jax.ShapeDtypeStruct((NDEV, *a.shape), a.dtype) for a in arrs],
        in_specs=_hbm_specs(n), out_specs=_hbm_specs(n),
        scratch_shapes=[pltpu.SemaphoreType.DMA((7 * n,)), pltpu.SemaphoreType.DMA((7 * n,)),
                        pltpu.SemaphoreType.DMA((n,))],
    )(*arrs)


_XCHG_EFFECT = pltpu.SideEffectType.DATAFLOW_SIDE_EFFECTING
ALL_PEERS = (1, 2, 3, 4, 5, 6, 7)
SAME_CORE = (2, 4, 6)


def _xchg_copies(src_refs, land_refs, send_sems, recv_sems, local_sems, scatter, rels):
    x, y, cc = _position()
    me = 4 * x + 2 * y + cc
    remote, local = [], []
    for a, (src, land) in enumerate(zip(src_refs, land_refs)):
        local.append(pltpu.make_async_copy(src.at[me] if scatter else src, land.at[me], local_sems.at[a]))
        for idx, rel in enumerate(rels):
            px = 1 - x if rel & 4 else x
            py = 1 - y if rel & 2 else y
            pc = 1 - cc if rel & 1 else cc
            k = len(rels) * a + idx
            remote.append(pltpu.make_async_remote_copy(
                src_ref=src.at[4 * px + 2 * py + pc] if scatter else src, dst_ref=land.at[me],
                send_sem=send_sems.at[k], recv_sem=recv_sems.at[k], device_id=(px, py, pc), device_id_type=_MESH))
    return remote, local


def _xchg_start(srcs, scatter, rels, after, name):
    n = len(srcs)
    lands = [lax.empty(s.shape if scatter else (NDEV, *s.shape), s.dtype) for s in srcs]

    def body(*refs):
        remote, local = _xchg_copies(refs[:n], refs[n:2 * n], *refs[2 * n + 1:2 * n + 4], scatter, rels)
        for cp in local + remote:
            cp.start()
        token = refs[-1]
        token[...] = jnp.zeros_like(token)

    hbm = pl.BlockSpec(memory_space=pltpu.HBM)
    sem = pl.BlockSpec(memory_space=pltpu.SEMAPHORE)
    out = pl.pallas_call(
        body, name=name,
        out_shape=(pltpu.SemaphoreType.DMA((len(rels) * n,)), pltpu.SemaphoreType.DMA((len(rels) * n,)),
                   pltpu.SemaphoreType.DMA((n,)),
                   *[pltpu.HBM(a.shape, a.dtype) for a in srcs + lands], jax.ShapeDtypeStruct((8, LANES), F32)),
        in_specs=[hbm] * (2 * n) + [pl.BlockSpec(memory_space=pl.ANY)],
        out_specs=(sem, sem, sem, *[hbm] * (2 * n), pl.BlockSpec(memory_space=pltpu.VMEM)),
        input_output_aliases={i: 3 + i for i in range(2 * n)},
        compiler_params=pltpu.CompilerParams(has_side_effects=_XCHG_EFFECT),
    )(*[pltpu.with_memory_space_constraint(a, pltpu.HBM) for a in srcs + lands], after)
    return out[:-1], out[-1][0, 0]


def _xchg_wait(handles, after, scatter, rels, name):
    n = (len(handles) - 3) // 2

    def body(*refs):
        remote, local = _xchg_copies(refs[:n], refs[n:2 * n], *refs[2 * n:2 * n + 3], scatter, rels)
        for cp in remote:
            cp.wait_send()
            cp.wait_recv()
        for cp in local:
            cp.wait()

    hbm = pl.BlockSpec(memory_space=pltpu.HBM)
    sem = pl.BlockSpec(memory_space=pltpu.SEMAPHORE)
    thru = list(handles[3:])
    out = pl.pallas_call(
        body, name=name,
        out_shape=tuple(pltpu.HBM(a.shape, a.dtype) for a in thru),
        in_specs=[hbm] * (2 * n) + [sem, sem, sem, pl.BlockSpec(memory_space=pl.ANY)],
        out_specs=tuple([hbm] * (2 * n)),
        input_output_aliases={i: i for i in range(2 * n)},
        compiler_params=pltpu.CompilerParams(has_side_effects=_XCHG_EFFECT),
    )(*thru, *handles[:3], after)
    return list(out[n:])


def _sibling_forward(lands, name):
    n = len(lands)

    def body(*refs):
        land_refs = refs[n:2 * n]
        send_sems, recv_sems = refs[2 * n:]
        x, y, cc = _position()

        def copy(a, q, core):
            slot = land_refs[a].at[2 * q + core]
            return pltpu.make_async_remote_copy(
                src_ref=slot, dst_ref=slot, send_sem=send_sems.at[NCHIP * a + q], recv_sem=recv_sems.at[NCHIP * a + q],
                device_id=(x, y, 1 - cc), device_id_type=_MESH)

        sends = [copy(a, q, cc) for a in range(n) for q in range(NCHIP)]
        for cp in sends:
            cp.start()
        for a in range(n):
            for q in range(NCHIP):
                copy(a, q, 1 - cc).wait_recv()
        for cp in sends:
            cp.wait_send()

    return pl.pallas_call(
        body, name=name,
        out_shape=[jax.ShapeDtypeStruct(a.shape, a.dtype) for a in lands],
        in_specs=_hbm_specs(n), out_specs=_hbm_specs(n),
        input_output_aliases={i: i for i in range(n)},
        scratch_shapes=[pltpu.SemaphoreType.DMA((NCHIP * n,)), pltpu.SemaphoreType.DMA((NCHIP * n,))],
    )(*lands)


def _rs_sibling_exchange(gs):
    n = len(gs)

    def body(*refs):
        g_refs, recv_refs = refs[:n], refs[n:2 * n]
        send_sems, recv_sems = refs[2 * n:]
        x, y, cc = _position()
        copies = [pltpu.make_async_remote_copy(
            src_ref=g_refs[a].at[q, 1 - cc], dst_ref=recv_refs[a].at[q], send_sem=send_sems.at[NCHIP * a + q],
            recv_sem=recv_sems.at[NCHIP * a + q], device_id=(x, y, 1 - cc), device_id_type=_MESH)
            for a in range(n) for q in range(NCHIP)]
        for cp in copies:
            cp.start()
        for cp in copies:
            cp.wait()

    return pl.pallas_call(
        body, name="rs_sibling_exchange",
        out_shape=[jax.ShapeDtypeStruct((NCHIP, *g.shape[2:]), g.dtype) for g in gs],
        in_specs=_hbm_specs(n), out_specs=_hbm_specs(n),
        scratch_shapes=[pltpu.SemaphoreType.DMA((NCHIP * n,)), pltpu.SemaphoreType.DMA((NCHIP * n,))],
    )(*gs)


def _rs_chip_exchange(parts):
    n = len(parts)

    def body(*refs):
        p_refs, recv_refs = refs[:n], refs[n:2 * n]
        send_sems, recv_sems, local_sems = refs[2 * n:]
        x, y, cc = _position()
        myq = 2 * x + y
        chips = [(1 - x, y), (x, 1 - y), (1 - x, 1 - y)]

        def copy(a, k, px, py, src_q, dst_q):
            return pltpu.make_async_remote_copy(
                src_ref=p_refs[a].at[src_q], dst_ref=recv_refs[a].at[dst_q], send_sem=send_sems.at[3 * a + k],
                recv_sem=recv_sems.at[3 * a + k], device_id=(px, py, cc), device_id_type=_MESH)

        local = [pltpu.make_async_copy(p_refs[a].at[myq], recv_refs[a].at[myq], local_sems.at[a]) for a in range(n)]
        for cp in local:
            cp.start()
        sends = [copy(a, k, px, py, 2 * px + py, myq) for a in range(n) for k, (px, py) in enumerate(chips)]
        for cp in sends:
            cp.start()
        for a in range(n):
            for k, (px, py) in enumerate(chips):
                copy(a, k, px, py, myq, 2 * px + py).wait_recv()
        for cp in sends:
            cp.wait_send()
        for cp in local:
            cp.wait()

    return pl.pallas_call(
        body, name="rs_chip_exchange",
        out_shape=[jax.ShapeDtypeStruct(p.shape, p.dtype) for p in parts],
        in_specs=_hbm_specs(n), out_specs=_hbm_specs(n),
        scratch_shapes=[pltpu.SemaphoreType.DMA((3 * n,)), pltpu.SemaphoreType.DMA((3 * n,)),
                        pltpu.SemaphoreType.DMA((n,))],
    )(*parts)


def _pair_sum(own, got, name):
    n, r, c = own.shape

    def body(a_ref, b_ref, o_ref):
        o_ref[...] = (a_ref[...].astype(F32) + b_ref[...].astype(F32)).astype(o_ref.dtype)

    spec = pl.BlockSpec((1, r, c), lambda q: (q, 0, 0))
    return pl.pallas_call(body, name=name, grid=(n,), in_specs=[spec, spec], out_specs=spec,
                          out_shape=jax.ShapeDtypeStruct((n, r, c), own.dtype),
                          compiler_params=_cparams(dimension_semantics=("parallel",)))(own, got)


def _slab_sum(slabs, name, tr=None):
    n, r, c = slabs.shape
    tr = r if tr is None else tr

    def body(s_ref, o_ref):
        acc = s_ref[0].astype(F32)
        for q in range(1, n):
            acc = acc + s_ref[q].astype(F32)
        o_ref[...] = acc

    return pl.pallas_call(body, name=name, grid=(r // tr,),
                          in_specs=[pl.BlockSpec((n, tr, c), lambda i: (0, i, 0))],
                          out_specs=pl.BlockSpec((tr, c), lambda i: (i, 0)),
                          out_shape=jax.ShapeDtypeStruct((r, c), F32),
                          compiler_params=_cparams(dimension_semantics=("parallel",)))(slabs)


def _ada_fwd(c_all, ada_w, kv_ada_w, logits):
    rows, d = c_all.shape
    n0, nkv = ada_w.shape[2], kv_ada_w.shape[1]

    def body(c_ref, w_ref, kw_ref, lg_ref, part_ref, cact_ref, lb_ref):
        ca = _silu(c_ref[...])
        cact_ref[...] = ca
        part_ref[:, 0:n0] = _bdot_raw(ca, w_ref[0], _NN)
        part_ref[:, n0:2 * n0] = _bdot_raw(ca, w_ref[1], _NN)
        part_ref[:, 2 * n0:2 * n0 + nkv] = _bdot_raw(ca, kw_ref[...], _NN)
        lb_ref[...] = _sigmoid(lg_ref[0:1, :] - lg_ref[1:2, :])

    vm = pl.BlockSpec(memory_space=pltpu.VMEM)
    return pl.pallas_call(
        body, name="ada_fwd", in_specs=[vm, vm, vm, vm], out_specs=[vm, vm, vm],
        out_shape=[jax.ShapeDtypeStruct((rows, 2 * n0 + nkv), F32), jax.ShapeDtypeStruct((rows, d), F32),
                   jax.ShapeDtypeStruct((1, d), F32)],
        compiler_params=_cparams(),
    )(c_all, ada_w, kv_ada_w, logits)


def _ada_bwd(c_act, dm0, dm1, dkv, lb, dlb):
    rows, d = c_act.shape

    def body(c_ref, d0_ref, d1_ref, dk_ref, lb_ref, dlb_ref, dw_ref, dkw_ref, dlg_ref):
        ca = c_ref[...]
        dw_ref[0] = _bdot_raw(ca, d0_ref[...], _TN)
        dw_ref[1] = _bdot_raw(ca, d1_ref[...], _TN)
        dkw_ref[...] = _bdot_raw(ca, dk_ref[...], _TN)
        lbv = lb_ref[...]
        dl0 = dlb_ref[...] * lbv * (1.0 - lbv)
        dlg_ref[0:1, :] = dl0
        dlg_ref[1:2, :] = -dl0

    vm = pl.BlockSpec(memory_space=pltpu.VMEM)
    return pl.pallas_call(
        body, name="ada_bwd", in_specs=[vm] * 6, out_specs=[vm, vm, vm],
        out_shape=[jax.ShapeDtypeStruct((2, d, dm0.shape[1]), F32), jax.ShapeDtypeStruct((d, dkv.shape[1]), F32),
                   jax.ShapeDtypeStruct((2, d), F32)],
        compiler_params=_cparams(),
    )(c_act, dm0, dm1, dkv, lb, dlb)


def _adamw(w, g, m, v, name, tr=512, after=None):
    r, c = w.shape
    tr = _divisor_tile(r, tr, unit=8)
    c1 = 1.0 - ADAM_B1 ** ADAM_STEP
    c2 = 1.0 - ADAM_B2 ** ADAM_STEP
    deps = [] if after is None else [after]

    def body(w_ref, g_ref, m_ref, v_ref, *rest):
        d_ref, mo_ref, vo_ref = rest[len(deps):]
        gv = g_ref[...]
        mn = ADAM_B1 * m_ref[...] + (1.0 - ADAM_B1) * gv
        vn = ADAM_B2 * v_ref[...] + (1.0 - ADAM_B2) * (gv * gv)
        d_ref[...] = -ADAM_LR * ((mn / c1) / (jnp.sqrt(vn / c2) + ADAM_EPS) + ADAM_WD * w_ref[...])
        mo_ref[...] = mn
        vo_ref[...] = vn

    spec = pl.BlockSpec((tr, c), lambda i: (i, 0))
    out = jax.ShapeDtypeStruct((r, c), F32)
    return pl.pallas_call(body, name=name, grid=(r // tr,),
                          in_specs=[spec] * 4 + [pl.BlockSpec(a.shape, lambda i: (0, 0)) for a in deps],
                          out_specs=[spec] * 3, out_shape=[out, out, out],
                          compiler_params=_cparams(dimension_semantics=("parallel",)))(w, g, m, v, *deps)


def _pad_rows(a, rows):
    return jnp.pad(a, ((0, rows - a.shape[0]), (0, 0)))


def _pack_small(parts, lanes=LANES, row_unit=8):
    flat = jnp.concatenate([p.reshape(-1).astype(F32) for p in parts])
    rows = _round_up(-(-flat.shape[0] // lanes), row_unit)
    return jnp.pad(flat, (0, rows * lanes - flat.shape[0])).reshape(rows, lanes)


def _unpack_small(flat, shapes):
    out, off = [], 0
    for s in shapes:
        n = 1
        for k in s:
            n *= k
        out.append(flat[off:off + n].reshape(s))
        off += n
    return out


def _pad_shard_cols(a, n_loc, n_pad):
    lead = a.shape[:-1]
    a = a.reshape(*lead, NDEV, n_loc)
    a = jnp.pad(a, [(0, 0)] * (len(lead) + 1) + [(0, n_pad - n_loc)])
    return a.reshape(*lead, NDEV * n_pad)


def _unpad_shard_cols(a, n_loc, n_pad):
    lead = a.shape[:-1]
    return a.reshape(*lead, NDEV, n_pad)[..., :n_loc].reshape(*lead, NDEV * n_loc)


def kernel(x, c, ada_w, ada_b, a_w_in, a_lb_logits, a_norm_g, a_w_out, kv_ada_w, kv_ada_b, kv_w, kv_b_f, k_norm_g, b_w_q, q_norm_g, b_w_out, ffn_w_up, ffn_conv_w, ffn_conv_b, ffn_w_down, loss_target, m_ada_w, m_ada_b, m_a_w_in, m_a_lb_logits, m_a_norm_g, m_a_w_out, m_kv_ada_w, m_kv_ada_b, m_kv_w, m_kv_b_f, m_k_norm_g, m_b_w_q, m_q_norm_g, m_b_w_out, m_ffn_w_up, m_ffn_conv_w, m_ffn_conv_b, m_ffn_w_down, v_ada_w, v_ada_b, v_a_w_in, v_a_lb_logits, v_a_norm_g, v_a_w_out, v_kv_ada_w, v_kv_ada_b, v_kv_w, v_kv_b_f, v_k_norm_g, v_b_w_q, v_q_norm_g, v_b_w_out, v_ffn_w_up, v_ffn_conv_w, v_ffn_conv_b, v_ffn_w_down):
    t, d = x.shape[1], x.shape[2]
    nh = d // HEAD
    ncw = ffn_w_up.shape[2]
    ncp = _round_up(ncw, LANES)
    two_f = ncw * NDEV
    ff = two_f // 2
    fp = ncp * NDEV // 2
    rd = ffn_w_down.shape[1]
    me = 4 * lax.axis_index("x") + 2 * lax.axis_index("y") + lax.axis_index("c")
    weights = dict(ada_w=ada_w, ada_b=ada_b, a_w_in=a_w_in, a_lb_logits=a_lb_logits, a_norm_g=a_norm_g,
                   a_w_out=a_w_out, kv_ada_w=kv_ada_w, kv_ada_b=kv_ada_b, kv_w=kv_w, kv_b_f=kv_b_f,
                   k_norm_g=k_norm_g, b_w_q=b_w_q, q_norm_g=q_norm_g, b_w_out=b_w_out, ffn_w_up=ffn_w_up,
                   ffn_conv_w=ffn_conv_w, ffn_conv_b=ffn_conv_b, ffn_w_down=ffn_w_down)
    m_in = dict(ada_w=m_ada_w, ada_b=m_ada_b, a_w_in=m_a_w_in, a_lb_logits=m_a_lb_logits, a_norm_g=m_a_norm_g,
                a_w_out=m_a_w_out, kv_ada_w=m_kv_ada_w, kv_ada_b=m_kv_ada_b, kv_w=m_kv_w, kv_b_f=m_kv_b_f,
                k_norm_g=m_k_norm_g, b_w_q=m_b_w_q, q_norm_g=m_q_norm_g, b_w_out=m_b_w_out, ffn_w_up=m_ffn_w_up,
                ffn_conv_w=m_ffn_conv_w, ffn_conv_b=m_ffn_conv_b, ffn_w_down=m_ffn_w_down)
    v_in = dict(ada_w=v_ada_w, ada_b=v_ada_b, a_w_in=v_a_w_in, a_lb_logits=v_a_lb_logits, a_norm_g=v_a_norm_g,
                a_w_out=v_a_w_out, kv_ada_w=v_kv_ada_w, kv_ada_b=v_kv_ada_b, kv_w=v_kv_w, kv_b_f=v_kv_b_f,
                k_norm_g=v_k_norm_g, b_w_q=v_b_w_q, q_norm_g=v_q_norm_g, b_w_out=v_b_w_out, ffn_w_up=v_ffn_w_up,
                ffn_conv_w=v_ffn_conv_w, ffn_conv_b=v_ffn_conv_b, ffn_w_down=v_ffn_w_down)
    order = list(weights)

    up_loc = jnp.pad(ffn_w_up, ((0, 0), (0, 0), (0, ncp - ncw))).astype(BF16)
    down_loc = ffn_w_down.astype(BF16)
    gather_names = {"l0b": ["a_out", "up0", "down0"], "l1": ["kv", "b_q", "b_out", "up1", "down1"]}
    shards = {"a_out": a_w_out[0].astype(BF16), "up0": up_loc[0], "down0": down_loc[0], "kv": kv_w.astype(BF16),
              "b_q": b_w_q[0].astype(BF16), "b_out": b_w_out[0].astype(BF16), "up1": up_loc[1],
              "down1": down_loc[1]}
    pre = _pack_small([c, a_lb_logits, ffn_conv_w])
    a_in_all, pre_all = _all_gather([a_w_in[0].astype(BF16), pre], "gather_a_w_in_and_small_inputs")
    pre_all = pre_all.reshape(NDEV, -1)
    c_all = pre_all[:, :d]
    logits = pre_all[:, d:d + 2 * HEAD].reshape(NDEV, 2, HEAD).transpose(1, 0, 2).reshape(2, d)
    conv_w_full = pre_all[:, d + 2 * HEAD:d + 2 * HEAD + 2 * CONV_TAPS * ncw]
    conv_w_full = conv_w_full.reshape(NDEV, 2, CONV_TAPS, ncw).transpose(1, 2, 0, 3).reshape(2, CONV_TAPS, two_f)

    part, c_act, lb = _ada_fwd(_pad_rows(c_all, 2 * NDEV), ada_w, kv_ada_w, logits)
    (part_all,) = _all_gather([part[:NDEV]], "gather_adaln")
    mine = lax.dynamic_index_in_dim(part_all, me, axis=1, keepdims=False)
    n0, nkv = ada_w.shape[2], kv_ada_w.shape[1]
    mod_names = ["sh1", "sc1", "g1", "sh2", "sc2", "g2"]
    mods = {}
    for l in range(2):
        row = mine[:, l * n0:(l + 1) * n0].reshape(-1) + ada_b[l]
        for k, nm in enumerate(mod_names):
            mods[f"{nm}_{l}"] = row[k * d:(k + 1) * d].reshape(1, d)
    kvrow = mine[:, 2 * n0:2 * n0 + nkv].reshape(-1) + kv_ada_b
    mods["kv_sh"], mods["kv_sc"] = kvrow[:d].reshape(1, d), kvrow[d:].reshape(1, d)

    in_flight = {}

    def start_gather(grp, dep):
        srcs = [shards[n] for n in gather_names[grp]]
        in_flight[grp], started = _xchg_start(srcs, False, SAME_CORE, dep, f"gather_{grp}_start")
        return started

    zero = start_gather("l0b", part_all)
    mods["sh1_0"] = mods["sh1_0"] + zero

    small = {"a_norm_g": a_norm_g, "k_norm_g": k_norm_g.reshape(1, HEAD), "q_norm_g": q_norm_g, "kv_b_f": kv_b_f}
    for l in range(2):
        small[f"conv_w{l}"] = _pad_shard_cols(conv_w_full[l], ncw, ncp).reshape(CONV_TAPS, 2, fp).transpose(1, 0, 2)
        small[f"conv_b{l}"] = _pad_shard_cols(ffn_conv_b[l], ncw, ncp).reshape(2, 1, fp)

    def get_w(grp, after):
        if grp == "l0a":
            return {"a_in": a_in_all}
        arrived = _xchg_wait(in_flight[grp], after, False, SAME_CORE, f"gather_{grp}_wait")
        full = list(_sibling_forward(arrived, f"gather_{grp}_to_sibling"))
        if grp == "l0b":
            started = start_gather("l1", full[0])
            full[0] = full[0] + started.astype(full[0].dtype)
        got = dict(zip(gather_names[grp], full))
        out = {}
        for n, a in got.items():
            if n in ("a_out", "b_out"):
                out[n] = a.reshape(d, d)
            elif n in ("down0", "down1"):
                dn = a.reshape(NCHIP, ff // NCHIP, d)
                out[n] = jnp.pad(dn, ((0, 0), (0, ncp - ncw), (0, 0))).reshape(fp, d)
            elif n == "kv":
                kv_full = a.transpose(1, 0, 2).reshape(d, NDEV * kv_w.shape[1])
                out["kv"] = kv_full[:, :2 * d]
                out["kv_f"] = jnp.pad(kv_full[:, 2 * d:], ((0, 0), (0, LANES - nh)))
            else:
                out[n] = a
        return out

    scatter_flight, g_last = {}, {}

    def put_g(grp, gr):
        if grp == "l0a":
            g_last.update(gr)
            return zero
        if grp == "l1":
            g_kvw = jnp.concatenate([gr["kv"], gr["kv_f"][:, :nh].astype(BF16)], axis=1)
            arrs = {"kv_w": g_kvw.reshape(d, NDEV, kv_w.shape[1]).transpose(1, 0, 2), "b_w_q": gr["b_q"],
                    "b_w_out": gr["b_out"].reshape(NDEV, d // NDEV, d), "up1": gr["up1"],
                    "down1": gr["down1"].reshape(NCHIP, ncp, d)[:, :ncw].reshape(NDEV, rd, d)}
        else:
            arrs = {"a_w_out": gr["a_out"].reshape(NDEV, d // NDEV, d), "up0": gr["up0"],
                    "down0": gr["down0"].reshape(NCHIP, ncp, d)[:, :ncw].reshape(NDEV, rd, d)}
        srcs = list(arrs.values())
        handles, sent = _xchg_start(srcs, True, ALL_PEERS, srcs[0], f"scatter_{grp}_start")
        scatter_flight[grp] = (list(arrs), handles)
        return sent

    loss_v, grad_x, dmods, dlb, g = _local_step(x[0], loss_target[0], mods, lb, small, get_w, put_g)
    loss = lax.psum(loss_v[0, 0], ("x", "y", "c"))

    g_sum = {}
    for grp in ("l1", "l0b"):
        names, handles = scatter_flight[grp]
        for nm, a in zip(names, _xchg_wait(handles, grad_x, True, ALL_PEERS, f"scatter_{grp}_wait")):
            g_sum[nm] = _slab_sum(a, f"rs_slab_sum_{nm}")

    def conv_w_grad(a):
        return _unpad_shard_cols(a.transpose(1, 0, 2).reshape(CONV_TAPS, 2 * fp), ncw, ncp)

    def conv_b_grad(a):
        return _unpad_shard_cols(a.reshape(2 * fp), ncw, ncp)

    dmod_vec = [dmods[f"{nm}_{l}"] for l in range(2) for nm in mod_names] + [dmods["kv_sh"], dmods["kv_sc"]]
    post = _pack_small(dmod_vec + [dlb, g["a_norm_g"], g["k_norm_g"], g["q_norm_g"],
                                   jnp.pad(g["kv_b_f"].reshape(-1), (0, LANES - nh)),
                                   conv_w_grad(g["conv_w0"]), conv_w_grad(g["conv_w1"]),
                                   conv_b_grad(g["conv_b0"]), conv_b_grad(g["conv_b1"])])
    (post_all,) = _all_gather([post], "gather_small_grads")
    a_in_flight, a_in_sent = _xchg_start([g_last["a_in"]], True, ALL_PEERS, post_all, "scatter_l0a_start")
    a_in_sent = a_in_sent.reshape(1, 1)
    tot = _slab_sum(post_all, "small_grad_sum").reshape(-1)
    nmod = 14 * d
    (t_mod, t_lb, t_ang, t_kng, t_qng, t_bf, t_cw, t_cb) = _unpack_small(
        tot, [(nmod,), (1, d), (1, HEAD), (HEAD,), (1, HEAD), (LANES,), (2, CONV_TAPS, two_f), (2, two_f)])
    dm_all = post_all.reshape(NDEV, -1)[:, :nmod]
    dm0 = lax.dynamic_slice_in_dim(dm_all[:, :6 * d], me * n0, n0, axis=1)
    dm1 = lax.dynamic_slice_in_dim(dm_all[:, 6 * d:12 * d], me * n0, n0, axis=1)
    dkv = lax.dynamic_slice_in_dim(dm_all[:, 12 * d:], me * nkv, nkv, axis=1)
    g_ada_w, g_kv_ada_w, g_logits = _ada_bwd(c_act, _pad_rows(dm0, 2 * NDEV), _pad_rows(dm1, 2 * NDEV),
                                              _pad_rows(dkv, 2 * NDEV), lb, t_lb)

    grads = {
        "ada_w": g_ada_w,
        "ada_b": t_mod[:12 * d].reshape(2, 6 * d),
        "a_lb_logits": lax.dynamic_slice_in_dim(g_logits, me * HEAD, HEAD, axis=1),
        "a_norm_g": t_ang,
        "a_w_out": g_sum["a_w_out"].reshape(a_w_out.shape),
        "kv_ada_w": g_kv_ada_w,
        "kv_ada_b": t_mod[12 * d:],
        "kv_w": g_sum["kv_w"],
        "kv_b_f": t_bf[:nh],
        "k_norm_g": t_kng,
        "b_w_q": g_sum["b_w_q"].reshape(b_w_q.shape),
        "q_norm_g": t_qng,
        "b_w_out": g_sum["b_w_out"].reshape(b_w_out.shape),
        "ffn_w_up": jnp.stack([g_sum["up0"][:, :ncw], g_sum["up1"][:, :ncw]]),
        "ffn_conv_w": lax.dynamic_slice_in_dim(t_cw, me * ncw, ncw, axis=2),
        "ffn_conv_b": t_cb,
        "ffn_w_down": jnp.stack([g_sum["down0"], g_sum["down1"]]),
    }

    big_adam = ["ada_w", "a_w_out", "kv_ada_w", "kv_w", "b_w_q", "b_w_out", "ffn_w_up", "ffn_w_down", "a_w_in"]
    small_adam = [n for n in order if n not in big_adam]
    delta, new_m, new_v = {}, {}, {}
    packs = [_pack_small([src[n] for n in small_adam]) for src in (weights, grads, m_in, v_in)]
    outs = _adamw(*packs, "adamw_small", tr=packs[0].shape[0])
    shapes = [weights[n].shape for n in small_adam]
    for dst, o in zip((delta, new_m, new_v), outs):
        for n, a in zip(small_adam, _unpack_small(o.reshape(-1), shapes)):
            dst[n] = a
    for n in big_adam:
        if n == "a_w_in":
            (landed,) = _xchg_wait(a_in_flight, new_v["ffn_w_down"], True, ALL_PEERS, "scatter_l0a_wait")
            grads[n] = _slab_sum(landed, "rs_slab_sum_a_w_in").reshape(a_w_in.shape)
        shp = weights[n].shape
        two_d = lambda a: a.reshape(-1, shp[-1])
        dl, mn, vn = _adamw(two_d(weights[n]), two_d(grads[n]), two_d(m_in[n]), two_d(v_in[n]), f"adamw_{n}",
                            after=a_in_sent)
        delta[n], new_m[n], new_v[n] = dl.reshape(shp), mn.reshape(shp), vn.reshape(shp)

    return (loss, grad_x.reshape(x.shape), *[grads[n] for n in order], *[delta[n] for n in order],
            *[new_m[n] for n in order], *[new_v[n] for n in order])
```

```python
import functools

import jax
import jax.numpy as jnp
from jax import lax
from jax.experimental import pallas as pl
from jax.experimental.pallas import tpu as pltpu

F32 = jnp.float32
BF16 = jnp.bfloat16

NDEV = 8
NCHIP = 4
HEAD = 128
A_CHUNK = 64
CONV_TAPS = 3
EPS = 1e-6
NEG_INF = -1e30
LANES = 128
VMEM_LIMIT = 48 * 1024 * 1024

ADAM_LR = 0.001
ADAM_B1 = 0.9
ADAM_B2 = 0.999
ADAM_EPS = 1e-08
ADAM_WD = 0.01
ADAM_STEP = 10

_NN = (((1,), (0,)), ((), ()))
_NT = (((1,), (1,)), ((), ()))
_TN = (((0,), (0,)), ((), ()))
_MESH = pl.DeviceIdType.MESH


def _cparams(**kw):
    return pltpu.CompilerParams(vmem_limit_bytes=VMEM_LIMIT, **kw)


def _divisor_tile(n, pref, unit=LANES):
    if n <= pref:
        return n
    best = None
    for t in range(unit, pref + 1, unit):
        if n % t == 0:
            best = t
    assert best is not None, (n, pref)
    return best


def _round_up(n, unit):
    return -(-n // unit) * unit


def _bdot_raw(a, b, dims):
    return lax.dot_general(a.astype(BF16), b.astype(BF16), dims, preferred_element_type=F32)


@jax.custom_vjp
def _dot_nn(a, b):
    return _bdot_raw(a, b, _NN)


@jax.custom_vjp
def _dot_nt(a, b):
    return _bdot_raw(a, b, _NT)


@jax.custom_vjp
def _dot_tn(a, b):
    return _bdot_raw(a, b, _TN)


_dot_nn.defvjp(lambda a, b: (_bdot_raw(a, b, _NN), (a, b)),
               lambda r, g: (_dot_nt(g, r[1]), _dot_tn(r[0], g)))
_dot_nt.defvjp(lambda a, b: (_bdot_raw(a, b, _NT), (a, b)),
               lambda r, g: (_dot_nn(g, r[1]), _dot_tn(g, r[0])))
_dot_tn.defvjp(lambda a, b: (_bdot_raw(a, b, _TN), (a, b)),
               lambda r, g: (_dot_nt(r[1], g), _dot_nn(r[0], g)))


def _f32dot(a, b):
    return lax.dot_general(a, b, _NN, precision=lax.Precision.HIGHEST, preferred_element_type=F32)


def _sigmoid(x):
    return jax.nn.sigmoid(x)


def _silu(x):
    return x * jax.nn.sigmoid(x)


def _rms(x):
    return x * lax.rsqrt(jnp.mean(x * x, axis=-1, keepdims=True) + EPS)


def _modulate(x, sh, sc):
    return _rms(x) * (1.0 + sc) + sh


def _mm_call(a, b, dims, a_spec, b_spec, o_spec, o_shape, grid, acc_tile, name):
    nk = grid[2]

    def body(a_ref, b_ref, o_ref, *acc):
        p = lax.dot_general(a_ref[...].astype(BF16), b_ref[...].astype(BF16), dims,
                            preferred_element_type=F32)
        if nk == 1:
            o_ref[...] = p.astype(o_ref.dtype)
        else:
            kk = pl.program_id(2)

            @pl.when(kk == 0)
            def _():
                acc[0][...] = p

            @pl.when(kk > 0)
            def _():
                acc[0][...] += p

            @pl.when(kk == nk - 1)
            def _():
                o_ref[...] = acc[0][...].astype(o_ref.dtype)

    return pl.pallas_call(
        body, name=name, grid=grid, in_specs=[a_spec, b_spec], out_specs=o_spec, out_shape=o_shape,
        scratch_shapes=[pltpu.VMEM(acc_tile, F32)] if nk > 1 else [],
        compiler_params=_cparams(dimension_semantics=("parallel", "parallel", "arbitrary")),
    )(a, b)


def _mm(a, b, mode, out_dtype, name, tm=1024, tn=1024, tk=2048):
    if mode == "nn":
        (m, k), (k2, n) = a.shape, b.shape
    elif mode == "nt":
        (m, k), (n, k2) = a.shape, b.shape
    else:
        (k, m), (k2, n) = a.shape, b.shape
    assert k == k2, (a.shape, b.shape, mode)
    tm, tn, tk = _divisor_tile(m, tm), _divisor_tile(n, tn), _divisor_tile(k, tk)
    if mode == "tn":
        a_spec = pl.BlockSpec((tk, tm), lambda i, j, kk: (kk, i))
    else:
        a_spec = pl.BlockSpec((tm, tk), lambda i, j, kk: (i, kk))
    if mode == "nt":
        b_spec = pl.BlockSpec((tn, tk), lambda i, j, kk: (j, kk))
    else:
        b_spec = pl.BlockSpec((tk, tn), lambda i, j, kk: (kk, j))
    return _mm_call(a, b, {"nn": _NN, "nt": _NT, "tn": _TN}[mode], a_spec, b_spec,
                    pl.BlockSpec((tm, tn), lambda i, j, kk: (i, j)), jax.ShapeDtypeStruct((m, n), out_dtype),
                    (m // tm, n // tn, k // tk), (tm, tn), name)


def _wblk_act_spec(rows, gb, nl, split, nb, row_axis, blk_axis):
    if split == 1:
        return pl.BlockSpec((rows, gb * nl), lambda *g: (g[row_axis], g[blk_axis]))
    groups = nb // split // gb
    return pl.BlockSpec((None, rows, gb * nl),
                        lambda *g: (g[blk_axis] // groups, g[row_axis], g[blk_axis] % groups))


def _mm_wblk(a, wb, out_dtype, name, *, gb, row_off=0, split=1, tm=1024):
    m, k = a.shape
    nb, _, nl = wb.shape
    assert (nb // split) % gb == 0
    tm = _divisor_tile(m, tm)

    def body(a_ref, b_ref, o_ref):
        av = a_ref[...].astype(BF16)
        for s in range(gb):
            o_ref[:, s * nl:(s + 1) * nl] = lax.dot_general(
                av, b_ref[s].astype(BF16), _NN, preferred_element_type=F32).astype(o_ref.dtype)

    o_shape = (m, nb * nl) if split == 1 else (split, m, nb // split * nl)
    return pl.pallas_call(
        body, name=name, grid=(nb // gb, m // tm),
        in_specs=[pl.BlockSpec((tm, k), lambda j, i: (i, 0)),
                  pl.BlockSpec((gb, k, nl), lambda j, i: (j, row_off, 0))],
        out_specs=_wblk_act_spec(tm, gb, nl, split, nb, 1, 0),
        out_shape=jax.ShapeDtypeStruct(o_shape, out_dtype),
        compiler_params=_cparams(dimension_semantics=("parallel", "parallel")),
    )(a, wb)


def _mm_wblk_dx(dy, wb, out_dtype, name, *, k, gb, row_off=0, split=1, tm=1024):
    nb, _, nl = wb.shape
    assert (nb // split) % gb == 0
    m = dy.shape[-2]
    tm = _divisor_tile(m, tm)
    nk = nb // gb

    def body(a_ref, b_ref, o_ref, *acc):
        p = None
        for s in range(gb):
            q = lax.dot_general(a_ref[:, s * nl:(s + 1) * nl].astype(BF16), b_ref[s].astype(BF16), _NT,
                                preferred_element_type=F32)
            p = q if p is None else p + q
        if nk == 1:
            o_ref[...] = p.astype(o_ref.dtype)
        else:
            kk = pl.program_id(1)

            @pl.when(kk == 0)
            def _():
                acc[0][...] = p

            @pl.when(kk > 0)
            def _():
                acc[0][...] += p

            @pl.when(kk == nk - 1)
            def _():
                o_ref[...] = acc[0][...].astype(o_ref.dtype)

    return pl.pallas_call(
        body, name=name, grid=(m // tm, nk),
        in_specs=[_wblk_act_spec(tm, gb, nl, split, nb, 0, 1),
                  pl.BlockSpec((gb, k, nl), lambda i, kk: (kk, row_off, 0))],
        out_specs=pl.BlockSpec((tm, k), lambda i, kk: (i, 0)),
        out_shape=jax.ShapeDtypeStruct((m, k), out_dtype),
        scratch_shapes=[pltpu.VMEM((tm, k), F32)] if nk > 1 else [],
        compiler_params=_cparams(dimension_semantics=("parallel", "arbitrary")),
    )(dy, wb)


def _mm_wblk_dw(x, dy, name, *, nb, gb, split=1, tk=1024):
    t, k = x.shape
    assert (nb // split) % gb == 0
    nl = dy.shape[-1] * split // nb
    tk = _divisor_tile(t, tk)
    nk = t // tk

    def body(a_ref, b_ref, o_ref, *acc):
        kk = pl.program_id(1)
        av = a_ref[...].astype(BF16)
        for s in range(gb):
            p = lax.dot_general(av, b_ref[:, s * nl:(s + 1) * nl].astype(BF16), _TN, preferred_element_type=F32)
            if nk == 1:
                o_ref[s] = p.astype(o_ref.dtype)
                continue

            @pl.when(kk == 0)
            def _():
                acc[0][s] = p

            @pl.when(kk > 0)
            def _():
                acc[0][s] += p

        if nk > 1:
            @pl.when(kk == nk - 1)
            def _():
                o_ref[...] = acc[0][...].astype(o_ref.dtype)

    return pl.pallas_call(
        body, name=name, grid=(nb // gb, nk),
        in_specs=[pl.BlockSpec((tk, k), lambda j, kk: (kk, 0)), _wblk_act_spec(tk, gb, nl, split, nb, 1, 0)],
        out_specs=pl.BlockSpec((gb, k, nl), lambda j, kk: (j, 0, 0)),
        out_shape=jax.ShapeDtypeStruct((nb, k, nl), BF16),
        scratch_shapes=[pltpu.VMEM((gb, k, nl), F32)] if nk > 1 else [],
        compiler_params=_cparams(dimension_semantics=("parallel", "arbitrary")),
    )(x, dy)


def _row_specs(rows, tb, nsub):
    return [pl.BlockSpec((tb, nsub * cw), functools.partial(lambda i, off: (i, off), off=off))
            for (_, cw, off) in rows]


def _vec_specs(params):
    return [pl.BlockSpec(p.shape, lambda i: (0, 0)) for p in params]


def _row_fwd(f, rows, params, out_dtypes, *, nsub=1, tb, name):
    t = rows[0][0].shape[0]
    tb = min(tb, t)
    n_r, n_p = len(rows), len(params)
    blk = [jax.ShapeDtypeStruct((tb, cw), F32) for (_, cw, _) in rows]
    blk += [jax.ShapeDtypeStruct(p.shape, F32) for p in params]
    out_avals = jax.eval_shape(f, *blk)

    def body(*refs):
        pv = [r[...] for r in refs[n_r:n_r + n_p]]
        for s in range(nsub):
            vals = [r[:, s * cw:(s + 1) * cw].astype(F32) for r, (_, cw, _) in zip(refs[:n_r], rows)]
            outs = f(*vals, *pv)
            for o_ref, o in zip(refs[n_r + n_p:], outs):
                w = o.shape[1]
                o_ref[:, s * w:(s + 1) * w] = o.astype(o_ref.dtype)

    return pl.pallas_call(
        body, name=name,
        grid=(t // tb,),
        in_specs=_row_specs(rows, tb, nsub) + _vec_specs(params),
        out_specs=[pl.BlockSpec((tb, nsub * av.shape[1]), lambda i: (i, 0)) for av in out_avals],
        out_shape=[jax.ShapeDtypeStruct((t, nsub * av.shape[1]), dt) for av, dt in zip(out_avals, out_dtypes)],
        compiler_params=_cparams(dimension_semantics=("parallel",)),
    )(*[r[0] for r in rows], *params)


def _row_bwd(f, rows, params, cots, row_grad_dtypes, *, nsub=1, tb, name, add_to=None, cot_add=None):
    t = rows[0][0].shape[0]
    tb = min(tb, t)
    n_r, n_p, n_c = len(rows), len(params), len(cots)
    want = [j for j in range(n_r) if row_grad_dtypes[j] is not None]
    extra = [] if add_to is None else [(add_to[1], rows[add_to[0]][1], 0)]
    extra += [] if cot_add is None else [(cot_add[1], cots[cot_add[0]][1], 0)]

    def body(*refs):
        i = pl.program_id(0)
        r_in, p_in = refs[:n_r], refs[n_r:n_r + n_p]
        c_in = refs[n_r + n_p:n_r + n_p + n_c]
        e_in = refs[n_r + n_p + n_c:n_r + n_p + n_c + len(extra)]
        outs = refs[n_r + n_p + n_c + len(extra):]
        pv = [r[...] for r in p_in]
        psum = [None] * n_p
        for s in range(nsub):
            vals = [r[:, s * cw:(s + 1) * cw].astype(F32) for r, (_, cw, _) in zip(r_in, rows)]
            cvals = [r[:, s * cw:(s + 1) * cw].astype(F32) for r, (_, cw, _) in zip(c_in, cots)]
            if cot_add is not None:
                cw = cots[cot_add[0]][1]
                cvals[cot_add[0]] = cvals[cot_add[0]] + e_in[-1][:, s * cw:(s + 1) * cw]
            _, vjp_fn = jax.vjp(f, *vals, *pv)
            grads = vjp_fn(tuple(cvals))
            for o_ref, jr in zip(outs[:len(want)], want):
                cw = rows[jr][1]
                gr = grads[jr]
                if add_to is not None and jr == add_to[0]:
                    gr = gr + e_in[0][:, s * cw:(s + 1) * cw]
                o_ref[:, s * cw:(s + 1) * cw] = gr.astype(o_ref.dtype)
            for jp in range(n_p):
                psum[jp] = grads[n_r + jp] if psum[jp] is None else psum[jp] + grads[n_r + jp]
        for o_ref, g in zip(outs[len(want):], psum):
            @pl.when(i == 0)
            def _():
                o_ref[...] = g

            @pl.when(i > 0)
            def _():
                o_ref[...] += g

    out_specs = [pl.BlockSpec((tb, nsub * rows[jr][1]), lambda i: (i, 0)) for jr in want]
    out_shape = [jax.ShapeDtypeStruct((t, nsub * rows[jr][1]), row_grad_dtypes[jr]) for jr in want]
    out_specs += _vec_specs(params)
    out_shape += [jax.ShapeDtypeStruct(p.shape, F32) for p in params]
    res = pl.pallas_call(
        body, name=name,
        grid=(t // tb,),
        in_specs=_row_specs(rows, tb, nsub) + _vec_specs(params) + _row_specs(cots, tb, nsub)
        + _row_specs(extra, tb, nsub),
        out_specs=out_specs, out_shape=out_shape,
        compiler_params=_cparams(dimension_semantics=("arbitrary",)),
    )(*[r[0] for r in rows], *params, *[c[0] for c in cots], *[e[0] for e in extra])
    return res[:len(want)], res[len(want):]


def _f_mod(x, sh, sc):
    return (_modulate(x, sh, sc),)


def _f_res_mod(x, y, g, sh, sc):
    x1 = x + g * y
    return x1, _modulate(x1, sh, sc)


def _f_res_mod2(x, y, g, sh_a, sc_a, sh_b, sc_b):
    x1 = x + g * y
    return x1, _modulate(x1, sh_a, sc_a), _modulate(x1, sh_b, sc_b)


def _f_qnorm(p, g):
    return (_rms(p) * g * (HEAD ** -0.5),)


def _f_knorm(p, g):
    return (_rms(p) * g,)


def _f_qnorm_aug(p, g):
    lane = lax.broadcasted_iota(jnp.int32, p.shape, 1)
    return (jnp.concatenate([_rms(p) * g * (HEAD ** -0.5), jnp.where(lane < 3, 1.0, 0.0)], axis=1),)


def _f_knorm_aug(p, c0, c1, c2, g):
    lane = lax.broadcasted_iota(jnp.int32, p.shape, 1)
    aug = jnp.where(lane == 0, c0, jnp.where(lane == 1, c1, jnp.where(lane == 2, c2, 0.0)))
    return (jnp.concatenate([_rms(p) * g, aug], axis=1),)


def _split3(a):
    round_bf16 = lambda v: lax.reduce_precision(v, exponent_bits=8, mantissa_bits=7)
    hi = round_bf16(a)
    mid = round_bf16(a - hi)
    lo = round_bf16(a - hi - mid)
    return hi.astype(BF16), mid.astype(BF16), lo.astype(BF16)


def _f_outgate(o, og):
    return (o * _sigmoid(og),)


def _loss_call(x3, f, g2, target, tb):
    t, d = x3.shape
    tb = min(tb, t)

    def body(x_ref, f_ref, g_ref, t_ref, loss_ref, dx_ref, df_ref, dg_ref):
        i = pl.program_id(0)
        fv = f_ref[...]
        g = g_ref[...]
        e = x_ref[...] + g * fv - t_ref[...]
        dx = e * (1.0 / d)
        part = 0.5 * jnp.sum(jnp.sum(e * dx, axis=1, keepdims=True), axis=0, keepdims=True)
        dx_ref[...] = dx
        df_ref[...] = (g * dx).astype(df_ref.dtype)
        dg = jnp.sum(dx * fv, axis=0, keepdims=True)

        @pl.when(i == 0)
        def _():
            loss_ref[...] = jnp.broadcast_to(part, loss_ref.shape)
            dg_ref[...] = dg

        @pl.when(i > 0)
        def _():
            loss_ref[...] += jnp.broadcast_to(part, loss_ref.shape)
            dg_ref[...] += dg

    row = pl.BlockSpec((tb, d), lambda i: (i, 0))
    vec = pl.BlockSpec((1, d), lambda i: (0, 0))
    return pl.pallas_call(
        body, name="loss_head",
        grid=(t // tb,),
        in_specs=[row, row, vec, row],
        out_specs=[pl.BlockSpec((1, LANES), lambda i: (0, 0)), row, row, vec],
        out_shape=[jax.ShapeDtypeStruct((1, LANES), F32), jax.ShapeDtypeStruct((t, d), F32),
                   jax.ShapeDtypeStruct((t, d), BF16), jax.ShapeDtypeStruct((1, d), F32)],
        compiler_params=_cparams(dimension_semantics=("arbitrary",)),
    )(x3, f, g2, target)


def _hg_consts(tb):
    c = A_CHUNK
    r = lax.broadcasted_iota(jnp.int32, (c, c), 0)
    s = lax.broadcasted_iota(jnp.int32, (c, c), 1)
    br = lax.broadcasted_iota(jnp.int32, (tb, tb), 0)
    bs = lax.broadcasted_iota(jnp.int32, (tb, tb), 1)
    shift = c.bit_length() - 1
    same_chunk = jnp.right_shift(br, shift) == jnp.right_shift(bs, shift)
    return (s <= r).astype(F32), (r <= s).astype(F32), jnp.logical_and(same_chunk, bs <= br)


def _chunk_apply(mat, x):
    c = mat.shape[0]
    return jnp.concatenate([_f32dot(mat, x[i * c:(i + 1) * c]) for i in range(x.shape[0] // c)], axis=0)


@jax.custom_vjp
def _chunk_cumsum(x, tri, tri_t):
    return _chunk_apply(tri, x)


_chunk_cumsum.defvjp(lambda x, tri, tri_t: (_chunk_apply(tri, x), (tri, tri_t)),
                     lambda r, g: (_chunk_apply(r[1], g), jnp.zeros_like(r[0]), jnp.zeros_like(r[1])))


def _per_chunk(a, b, dims):
    return jnp.stack([_bdot_raw(a[i], b[i], dims) for i in range(a.shape[0])])


@jax.custom_vjp
def _chunk_tn(a, b):
    return _per_chunk(a, b, _TN)


@jax.custom_vjp
def _chunk_nt(a, b):
    return _per_chunk(a, b, _NT)


@jax.custom_vjp
def _chunk_nn(a, b):
    return _per_chunk(a, b, _NN)


_chunk_tn.defvjp(lambda a, b: (_per_chunk(a, b, _TN), (a, b)),
                 lambda r, g: (_chunk_nt(r[1], g), _chunk_nn(r[0], g)))
_chunk_nt.defvjp(lambda a, b: (_per_chunk(a, b, _NT), (a, b)),
                 lambda r, g: (_chunk_nn(g, r[1]), _chunk_tn(g, r[0])))
_chunk_nn.defvjp(lambda a, b: (_per_chunk(a, b, _NN), (a, b)),
                 lambda r, g: (_chunk_nt(g, r[1]), _chunk_tn(r[0], g)))


def _scan_states(decay, m, st):
    sts = []
    for i in range(m.shape[0]):
        sts.append(st)
        st = st * decay[i] + m[i]
    return jnp.stack(sts), st


@jax.custom_vjp
def _state_scan(decay, m, st):
    return _scan_states(decay, m, st)


def _state_scan_fwd(decay, m, st):
    sts, st_out = _scan_states(decay, m, st)
    return (sts, st_out), (decay, sts)


def _state_scan_bwd(res, cts):
    decay, sts = res
    d_sts, g = cts
    d_decay, d_m = [], []
    for i in range(sts.shape[0] - 1, -1, -1):
        d_m.append(g)
        d_decay.append(jnp.sum(g * sts[i], axis=0, keepdims=True))
        g = g * decay[i] + d_sts[i]
    return jnp.stack(d_decay[::-1]), jnp.stack(d_m[::-1]), g


_state_scan.defvjp(_state_scan_fwd, _state_scan_bwd)


def _hg_block(qp, fp, ip, gp, lb, ng, st, tri, tri_t, bd_causal):
    tb = qp.shape[0]
    c = A_CHUNK
    n = tb // c
    q = _silu(qp)
    fg = lb + (1.0 - lb) * _sigmoid(fp)
    logf = jnp.log(fg)
    k = 1.0 - fg
    b3 = _chunk_cumsum(logf, tri, tri_t).reshape(n, c, HEAD)
    pos = lax.broadcasted_iota(jnp.int32, (1, c, 1), 1)
    b_mid = lax.stop_gradient(jnp.sum(jnp.where(pos == c // 2, b3, 0.0), axis=1, keepdims=True))
    b_last = jnp.sum(jnp.where(pos == c - 1, b3, 0.0), axis=1, keepdims=True)
    q3, k3, v3 = q.reshape(n, c, HEAD), k.reshape(n, c, HEAD), ip.reshape(n, c, HEAD)
    scores = _dot_nt((q3 * jnp.exp(b3 - b_mid)).reshape(tb, HEAD), (k3 * jnp.exp(b_mid - b3)).reshape(tb, HEAD))
    o_intra = _dot_nn(jnp.where(bd_causal, scores, 0.0), ip)
    states, st_new = _state_scan(jnp.exp(b_last), _chunk_tn(v3, k3 * jnp.exp(b_last - b3)), st)
    o = o_intra + _chunk_nt(q3 * jnp.exp(b3), states).reshape(tb, HEAD)
    y = _rms(o) * ng * _silu(gp)
    return y, st_new


def _hg_specs(tb, nh, rev_nb=None):
    def row(off):
        if rev_nb is None:
            return pl.BlockSpec((tb, HEAD), functools.partial(lambda h, i, off: (i, off + h), off=off))
        return pl.BlockSpec((tb, HEAD), functools.partial(lambda h, i, off: (rev_nb - 1 - i, off + h), off=off))
    return [row(0), row(nh), row(2 * nh), row(3 * nh),
            pl.BlockSpec((1, HEAD), lambda h, i: (0, h)), pl.BlockSpec((1, HEAD), lambda h, i: (0, 0))]


def _hgrn2_fwd(proj, lb, ng, tb):
    t = proj.shape[0]
    nh = proj.shape[1] // (4 * HEAD)
    tb = min(tb, t)
    nb = t // tb

    def body(q_ref, f_ref, i_ref, g_ref, lb_ref, ng_ref, y_ref, s_ref, st_ref):
        i = pl.program_id(1)

        @pl.when(i == 0)
        def _():
            st_ref[...] = jnp.zeros_like(st_ref)

        st = st_ref[...]
        s_ref[0, 0] = st
        y, st_new = _hg_block(q_ref[...], f_ref[...], i_ref[...], g_ref[...], lb_ref[...], ng_ref[...], st,
                              *_hg_consts(tb))
        y_ref[...] = y.astype(y_ref.dtype)
        st_ref[...] = st_new

    return pl.pallas_call(
        body, name="hgrn2_fwd",
        grid=(nh, nb),
        in_specs=_hg_specs(tb, nh),
        out_specs=[pl.BlockSpec((tb, HEAD), lambda h, i: (i, h)),
                   pl.BlockSpec((1, 1, HEAD, HEAD), lambda h, i: (h, i, 0, 0))],
        out_shape=[jax.ShapeDtypeStruct((t, nh * HEAD), BF16),
                   jax.ShapeDtypeStruct((nh, nb, HEAD, HEAD), F32)],
        scratch_shapes=[pltpu.VMEM((HEAD, HEAD), F32)],
        compiler_params=_cparams(dimension_semantics=("parallel", "arbitrary")),
    )(proj, proj, proj, proj, lb, ng)


def _hgrn2_bwd(proj, lb, ng, states, dy, tb):
    t = proj.shape[0]
    nh = proj.shape[1] // (4 * HEAD)
    tb = min(tb, t)
    nb = t // tb

    def body(q_ref, f_ref, i_ref, g_ref, lb_ref, ng_ref, s_ref, dy_ref,
             dq_ref, df_ref, di_ref, dg_ref, dlb_ref, dng_ref, dst_ref):
        h, i = pl.program_id(0), pl.program_id(1)
        consts = _hg_consts(tb)

        @pl.when(i == 0)
        def _():
            dst_ref[...] = jnp.zeros_like(dst_ref)
            dlb_ref[...] = jnp.zeros_like(dlb_ref)

        @pl.when(jnp.logical_and(i == 0, h == 0))
        def _():
            dng_ref[...] = jnp.zeros_like(dng_ref)

        def fn(qp, fp, ip, gp, lbx, ngx, stx):
            return _hg_block(qp, fp, ip, gp, lbx, ngx, stx, *consts)

        _, vjp_fn = jax.vjp(fn, q_ref[...], f_ref[...], i_ref[...], g_ref[...], lb_ref[...], ng_ref[...],
                            s_ref[0, 0])
        gq, gf, gi, gg, glb, gng, dst = vjp_fn((dy_ref[...].astype(F32), dst_ref[...]))
        dq_ref[...] = gq.astype(dq_ref.dtype)
        df_ref[...] = gf.astype(df_ref.dtype)
        di_ref[...] = gi.astype(di_ref.dtype)
        dg_ref[...] = gg.astype(dg_ref.dtype)
        dst_ref[...] = dst
        dlb_ref[...] += glb
        dng_ref[...] += gng

    rev = lambda h, i: (nb - 1 - i, h)
    slab = jax.ShapeDtypeStruct((t, nh * HEAD), BF16)
    return pl.pallas_call(
        body, name="hgrn2_bwd",
        grid=(nh, nb),
        in_specs=_hg_specs(tb, nh, rev_nb=nb) + [
            pl.BlockSpec((1, 1, HEAD, HEAD), lambda h, i: (h, nb - 1 - i, 0, 0)),
            pl.BlockSpec((tb, HEAD), rev)],
        out_specs=[pl.BlockSpec((tb, HEAD), rev)] * 4 + [
            pl.BlockSpec((1, HEAD), lambda h, i: (0, h)), pl.BlockSpec((1, HEAD), lambda h, i: (0, 0))],
        out_shape=[slab, slab, slab, slab,
                   jax.ShapeDtypeStruct((1, nh * HEAD), F32), jax.ShapeDtypeStruct((1, HEAD), F32)],
        scratch_shapes=[pltpu.VMEM((HEAD, HEAD), F32)],
        compiler_params=_cparams(dimension_semantics=("arbitrary", "arbitrary")),
    )(proj, proj, proj, proj, lb, ng, states, dy)


def _fgate_consts(cb):
    r = lax.broadcasted_iota(jnp.int32, (cb, cb), 0)
    s = lax.broadcasted_iota(jnp.int32, (cb, cb), 1)
    return (r <= s).astype(F32), (r >= s).astype(F32)


def _fgate_fwd(xt, bias, cb=512):
    nh, t = xt.shape
    cb = min(cb, t)

    def body(x_ref, b_ref, o_ref):
        upper, _ = _fgate_consts(cb)
        carry = jnp.zeros((nh, 1), F32)
        for blk in range(t // cb):
            z = x_ref[:, blk * cb:(blk + 1) * cb] + b_ref[...]
            logf = jnp.minimum(z, 0.0) - jnp.log(1.0 + jnp.exp(-jnp.abs(z)))
            cs = _f32dot(logf, upper) + carry
            o_ref[:, blk * cb:(blk + 1) * cb] = cs
            carry = cs[:, cb - 1:cb]

    vm = pl.BlockSpec(memory_space=pltpu.VMEM)
    return pl.pallas_call(
        body, name="fgate_fwd", in_specs=[vm, vm], out_specs=vm,
        out_shape=jax.ShapeDtypeStruct((nh, t), F32), compiler_params=_cparams(),
    )(xt, bias)


def _fgate_bwd(xt, bias, dft, cb=512):
    nh, t = xt.shape
    cb = min(cb, t)
    nblk = t // cb

    def body(x_ref, b_ref, d_ref, dx_ref, db_ref):
        _, lower = _fgate_consts(cb)
        carry = jnp.zeros((nh, 1), F32)
        db = jnp.zeros((nh, 1), F32)
        for blk in range(nblk - 1, -1, -1):
            sl = slice(blk * cb, (blk + 1) * cb)
            dlogf = _f32dot(d_ref[:, sl], lower) + carry
            carry = dlogf[:, 0:1]
            z = x_ref[:, sl] + b_ref[...]
            dz = dlogf * (1.0 - _sigmoid(z))
            dx_ref[:, sl] = dz
            db = db + jnp.sum(dz, axis=1, keepdims=True)
        db_ref[...] = db

    vm = pl.BlockSpec(memory_space=pltpu.VMEM)
    return pl.pallas_call(
        body, name="fgate_bwd", in_specs=[vm, vm, vm], out_specs=[vm, vm],
        out_shape=[jax.ShapeDtypeStruct((nh, t), F32), jax.ShapeDtypeStruct((nh, 1), F32)],
        compiler_params=_cparams(),
    )(xt, bias, dft)


def _attn_fwd(q, k, v, f_col, blk):
    t, width = v.shape
    nh = width // HEAD
    nq = t // blk

    def body(q_ref, k_ref, v_ref, fc_ref, o_ref, lse_ref):
        i = pl.program_id(0)
        tri = (lax.broadcasted_iota(jnp.int32, (blk, blk), 1) <= lax.broadcasted_iota(jnp.int32, (blk, blk), 0))
        for h in range(nh):
            cs = slice(h * HEAD, (h + 1) * HEAD)
            cs2 = slice(2 * h * HEAD, 2 * (h + 1) * HEAD)
            qh = q_ref[:, cs2]

            def tile(j, carry, masked):
                m, l, acc = carry
                rs = pl.ds(pl.multiple_of(j * blk, blk), blk)
                s = _bdot_raw(qh, k_ref[rs, cs2], _NT)
                if masked:
                    s = jnp.where(tri, s, NEG_INF)
                m_new = jnp.maximum(m, jnp.max(s, axis=1, keepdims=True))
                p = jnp.exp(s - m_new)
                alpha = jnp.exp(m - m_new)
                l_new = alpha * l + jnp.sum(p, axis=1, keepdims=True)
                acc_new = alpha * acc + _bdot_raw(p, v_ref[rs, cs], _NN)
                return m_new, l_new, acc_new

            init = (jnp.full((blk, 1), NEG_INF, F32), jnp.zeros((blk, 1), F32), jnp.zeros((blk, HEAD), F32))
            carry = lax.fori_loop(0, i, lambda j, c: tile(j, c, False), init)
            m, l, acc = tile(i, carry, True)
            o_ref[:, cs] = acc / l
            lse_ref[:, h:h + 1] = m + jnp.log(l) + fc_ref[:, h:h + 1]

    vm = pl.BlockSpec(memory_space=pltpu.VMEM)
    return pl.pallas_call(
        body, name="fox_attn_fwd",
        grid=(nq,),
        in_specs=[pl.BlockSpec((blk, 2 * width), lambda i: (i, 0)), vm, vm,
                  pl.BlockSpec((blk, nh), lambda i: (i, 0))],
        out_specs=[pl.BlockSpec((blk, width), lambda i: (i, 0)), pl.BlockSpec((blk, nh), lambda i: (i, 0))],
        out_shape=[jax.ShapeDtypeStruct((t, width), F32), jax.ShapeDtypeStruct((t, nh), F32)],
        compiler_params=_cparams(dimension_semantics=("parallel",)),
    )(q, k, v, f_col)


def _attn_delta(do, o, tb):
    t, width = o.shape
    nh = width // HEAD
    tb = min(tb, t)

    def body(do_ref, o_ref, dl_ref):
        for h in range(nh):
            cs = slice(h * HEAD, (h + 1) * HEAD)
            dl_ref[:, h:h + 1] = jnp.sum(do_ref[:, cs].astype(F32) * o_ref[:, cs], axis=1, keepdims=True)

    wide = pl.BlockSpec((tb, width), lambda i: (i, 0))
    return pl.pallas_call(body, name="fox_attn_delta", grid=(t // tb,), in_specs=[wide, wide],
                          out_specs=pl.BlockSpec((tb, nh), lambda i: (i, 0)),
                          out_shape=jax.ShapeDtypeStruct((t, nh), F32),
                          compiler_params=_cparams(dimension_semantics=("parallel",)))(do, o)


ATTN_BWD_GROUPS = 4


def _attn_bwd(q, k, v, f_col, do, lse, delta, blk):
    t, width = v.shape
    nh = width // HEAD
    nq = t // blk
    hpg = nh // ATTN_BWD_GROUPS
    gw = hpg * HEAD

    def body(q_ref, do_ref, k_ref, v_ref, fc_ref, lse_ref, dl_ref,
             dq_ref, dk_ref, dv_ref, dfc_ref, dfr_ref):
        g, j = pl.program_id(0), pl.program_id(1)
        tri = (lax.broadcasted_iota(jnp.int32, (blk, blk), 1) <= lax.broadcasted_iota(jnp.int32, (blk, blk), 0))

        @pl.when(j == 0)
        def _():
            dq_ref[...] = jnp.zeros_like(dq_ref)
            dfc_ref[...] = jnp.zeros_like(dfc_ref)

        for h in range(hpg):
            cs = slice(h * HEAD, (h + 1) * HEAD)
            cs2 = slice(2 * h * HEAD, 2 * (h + 1) * HEAD)
            csq = slice(2 * h * HEAD, (2 * h + 1) * HEAD)
            kj2 = k_ref[:, cs2]
            kj = k_ref[:, csq]
            vj = v_ref[:, cs]

            def tile(i, carry, masked):
                dk, dv, dfs = carry
                rs = pl.ds(pl.multiple_of(i * blk, blk), blk)
                qi = q_ref[rs, csq]
                doi = do_ref[rs, cs]
                bias = fc_ref[0, rs, h:h + 1] - lse_ref[0, rs, h:h + 1]
                p = jnp.exp(_bdot_raw(q_ref[rs, cs2], kj2, _NT) + bias)
                if masked:
                    p = jnp.where(tri, p, 0.0)
                ds = p * (_bdot_raw(doi, vj, _NT) - dl_ref[0, rs, h:h + 1])
                dsb = ds.astype(BF16)
                dq_ref[rs, cs] += _bdot_raw(dsb, kj, _NN)
                dfc_ref[0, rs, h:h + 1] += jnp.sum(ds, axis=1, keepdims=True)
                return (dk + _bdot_raw(dsb, qi, _TN), dv + _bdot_raw(p, doi, _TN),
                        dfs - jnp.sum(ds, axis=0, keepdims=True))

            init = (jnp.zeros((blk, HEAD), F32), jnp.zeros((blk, HEAD), F32), jnp.zeros((1, blk), F32))
            carry = tile(j, init, True)
            dk, dv, dfs = lax.fori_loop(j + 1, nq, lambda i, c: tile(i, c, False), carry)
            dk_ref[:, cs] = dk
            dv_ref[:, cs] = dv
            dfr_ref[0, 0, h:h + 1, :] = dfs

    by_group = lambda a: a.reshape(t, ATTN_BWD_GROUPS, hpg).transpose(1, 0, 2)
    once = pl.Buffered(1)
    stat = pl.BlockSpec((1, t, hpg), lambda g, j: (g, 0, 0), pipeline_mode=once)
    kv_blk = pl.BlockSpec((blk, gw), lambda g, j: (j, g))
    frow = pl.BlockSpec((1, 1, hpg, blk), lambda g, j: (g, j, 0, 0))
    dq, dk, dv, dfc, dfr = pl.pallas_call(
        body, name="fox_attn_bwd",
        grid=(ATTN_BWD_GROUPS, nq),
        in_specs=[pl.BlockSpec((t, 2 * gw), lambda g, j: (0, g), pipeline_mode=once),
                  pl.BlockSpec((t, gw), lambda g, j: (0, g), pipeline_mode=once),
                  pl.BlockSpec((blk, 2 * gw), lambda g, j: (j, g)), kv_blk, stat, stat, stat],
        out_specs=[pl.BlockSpec((t, gw), lambda g, j: (0, g)), kv_blk, kv_blk,
                   pl.BlockSpec((1, t, hpg), lambda g, j: (g, 0, 0)), frow],
        out_shape=[jax.ShapeDtypeStruct((t, width), F32), jax.ShapeDtypeStruct((t, width), F32),
                   jax.ShapeDtypeStruct((t, width), F32), jax.ShapeDtypeStruct((ATTN_BWD_GROUPS, t, hpg), F32),
                   jax.ShapeDtypeStruct((ATTN_BWD_GROUPS, nq, hpg, blk), F32)],
        compiler_params=_cparams(dimension_semantics=("parallel", "arbitrary")),
    )(q, do, k, v, by_group(f_col), by_group(lse), by_group(delta))
    return (dq, dk, dv, dfc.transpose(1, 0, 2).reshape(t, nh),
            dfr.transpose(1, 0, 2, 3).reshape(nq, nh, blk))


SUBLANES = 8


def _shift_down(u, n):
    r = pltpu.roll(u, n, 0)
    row = lax.broadcasted_iota(jnp.int32, (SUBLANES, u.shape[1]), 0)
    return jnp.concatenate([jnp.where(row < n, 0.0, r[:SUBLANES]), r[SUBLANES:]], axis=0)


def _shift_up(u, n):
    t = u.shape[0]
    r = pltpu.roll(u, t - n, 0)
    row = lax.broadcasted_iota(jnp.int32, (SUBLANES, u.shape[1]), 0)
    return jnp.concatenate([r[:t - SUBLANES], jnp.where(row >= SUBLANES - n, 0.0, r[t - SUBLANES:])], axis=0)


def _convglu_specs(t):
    return [pl.BlockSpec((2, t, LANES), lambda j: (0, 0, j)),
            pl.BlockSpec((2, CONV_TAPS, LANES), lambda j: (0, 0, j)),
            pl.BlockSpec((2, 1, LANES), lambda j: (0, 0, j))]


def _convglu_fwd(u, cw, cb):
    _, t, fp = u.shape

    def body(u_ref, w_ref, b_ref, a_ref):
        c = []
        for hf in range(2):
            uv, w = u_ref[hf], w_ref[hf]
            c.append(w[0:1] * _shift_down(uv, 2) + w[1:2] * _shift_down(uv, 1) + w[2:3] * uv + b_ref[hf])
        a_ref[...] = (_silu(c[0]) * c[1]).astype(a_ref.dtype)

    return pl.pallas_call(
        body, name="convglu_fwd",
        grid=(fp // LANES,),
        in_specs=_convglu_specs(t),
        out_specs=pl.BlockSpec((t, LANES), lambda j: (0, j)),
        out_shape=jax.ShapeDtypeStruct((t, fp), BF16),
        compiler_params=_cparams(dimension_semantics=("parallel",)),
    )(u, cw, cb)


def _convglu_bwd(u, cw, cb, da):
    _, t, fp = u.shape

    def body(u_ref, w_ref, b_ref, da_ref, du_ref, dw_ref, db_ref):
        us, c = [], []
        for hf in range(2):
            uv, w = u_ref[hf], w_ref[hf]
            u1, u2 = _shift_down(uv, 1), _shift_down(uv, 2)
            us.append((uv, u1, u2))
            c.append(w[0:1] * u2 + w[1:2] * u1 + w[2:3] * uv + b_ref[hf])
        gc, vc = c
        sg = _sigmoid(gc)
        dav = da_ref[...].astype(F32)
        dcs = [dav * vc * (sg * (1.0 + gc * (1.0 - sg))), dav * (gc * sg)]
        for hf in range(2):
            dc, w = dcs[hf], w_ref[hf]
            uv, u1, u2 = us[hf]
            du = w[2:3] * dc + w[1:2] * _shift_up(dc, 1) + w[0:1] * _shift_up(dc, 2)
            du_ref[hf] = du.astype(du_ref.dtype)
            dw_ref[hf, 0:1, :] = jnp.sum(dc * u2, axis=0, keepdims=True)
            dw_ref[hf, 1:2, :] = jnp.sum(dc * u1, axis=0, keepdims=True)
            dw_ref[hf, 2:3, :] = jnp.sum(dc * uv, axis=0, keepdims=True)
            db_ref[hf] = jnp.sum(dc, axis=0, keepdims=True)

    specs = _convglu_specs(t)
    return pl.pallas_call(
        body, name="convglu_bwd",
        grid=(fp // LANES,),
        in_specs=specs + [pl.BlockSpec((t, LANES), lambda j: (0, j))],
        out_specs=specs,
        out_shape=[jax.ShapeDtypeStruct((2, t, fp), BF16), jax.ShapeDtypeStruct((2, CONV_TAPS, fp), F32),
                   jax.ShapeDtypeStruct((2, 1, fp), F32)],
        compiler_params=_cparams(dimension_semantics=("parallel",)),
    )(u, cw, cb, da)


def _local_step(x, target, mods, lb, small, get_w, put_g, *, tb=512, attn_blk=512):
    t, d = x.shape
    nh = d // HEAD
    nb = NDEV
    wts = {}
    vec = lambda *names: [mods[n] for n in names]

    def ffn_fwd(h2, l):
        u = _mm_wblk(h2, wts[f"up{l}"], F32, f"ffn{l}_up", gb=nb // 2, split=2, tm=512)
        a = _convglu_fwd(u, small[f"conv_w{l}"], small[f"conv_b{l}"])
        f = _mm(a, wts[f"down{l}"], "nn", F32, f"ffn{l}_down", tk=4096)
        return u, a, f

    def ffn_bwd(df, h2, u, a, l):
        da = _mm(df, wts[f"down{l}"], "nt", BF16, f"ffn{l}_down_dx", tn=1536)
        dwd = _mm(a, df, "tn", BF16, f"ffn{l}_down_dw", tm=1536, tk=1024)
        du, dcw, dcb = _convglu_bwd(u, small[f"conv_w{l}"], small[f"conv_b{l}"], da)
        dh2 = _mm_wblk_dx(du, wts[f"up{l}"], F32, f"ffn{l}_up_dx", k=d, gb=nb // 2, split=2, tm=1024)
        dwu = _mm_wblk_dw(h2, du, f"ffn{l}_up_dw", nb=nb, gb=1, split=2, tk=t)
        return dh2, dwu, dwd, dcw, dcb

    (h_a,) = _row_fwd(_f_mod, [(x, d, 0)], vec("sh1_0", "sc1_0"), [BF16], tb=tb, name="l0_mod1")
    wts.update(get_w("l0a", h_a))
    proj_a = _mm_wblk(h_a, wts["a_in"], F32, "a_in", gb=nb // 2)
    ypre, states = _hgrn2_fwd(proj_a, lb, small["a_norm_g"], tb)
    wts.update(get_w("l0b", ypre))
    y_a = _mm(ypre, wts["a_out"], "nn", F32, "a_out")
    x1, h2_0 = _row_fwd(_f_res_mod, [(x, d, 0), (y_a, d, 0)], vec("g1_0", "sh2_0", "sc2_0"), [F32, BF16],
                        tb=tb, name="l0_res_mod2")
    u0, a0, f0 = ffn_fwd(h2_0, 0)
    x2, h_kv, h_q = _row_fwd(_f_res_mod2, [(x1, d, 0), (f0, d, 0)],
                             vec("g2_0", "kv_sh", "kv_sc", "sh1_1", "sc1_1"), [F32, BF16, BF16],
                             tb=tb, name="l0_res_kvmod_qmod")
    wts.update(get_w("l1", h_kv))
    proj_kv = _mm(h_kv, wts["kv"], "nn", F32, "kv_proj")
    proj_f = _mm(h_kv, wts["kv_f"], "nn", F32, "kv_fproj")
    v_b = proj_kv[:, d:].astype(BF16)
    f_logit_t = proj_f[:, :nh].T
    f_bias = small["kv_b_f"].reshape(nh, 1)
    f_col = _fgate_fwd(f_logit_t, f_bias).T
    (k_n,) = _row_fwd(_f_knorm_aug, [(proj_kv, HEAD, 0)] + [(piece, 1, 0) for piece in _split3(-f_col)],
                      [small["k_norm_g"]], [BF16], nsub=nh, tb=tb, name="k_norm")
    proj_q = _mm_wblk(h_q, wts["b_q"], F32, "b_q", gb=nb)
    (q_n,) = _row_fwd(_f_qnorm_aug, [(proj_q, HEAD, 0)], [small["q_norm_g"]], [BF16], nsub=nh, tb=tb,
                      name="q_norm")
    o_att, lse = _attn_fwd(q_n, k_n, v_b, f_col, attn_blk)
    (z,) = _row_fwd(_f_outgate, [(o_att, HEAD, 0), (proj_q, HEAD, 1)], [], [BF16], nsub=nh, tb=tb, name="out_gate")
    y_b = _mm(z, wts["b_out"], "nn", F32, "b_out")
    x3, h2_1 = _row_fwd(_f_res_mod, [(x2, d, 0), (y_b, d, 0)], vec("g1_1", "sh2_1", "sc2_1"), [F32, BF16],
                        tb=tb, name="l1_res_mod2")
    u1, a1, f1 = ffn_fwd(h2_1, 1)
    loss, dx4, df1, dg2_1 = _loss_call(x3, f1, mods["g2_1"], target, tb)

    g = {}
    dmods = {"g2_1": dg2_1}
    dh2, g["up1"], g["down1"], g["conv_w1"], g["conv_b1"] = ffn_bwd(df1, h2_1, u1, a1, 1)
    (dx2, dy_b), (dmods["g1_1"], dmods["sh2_1"], dmods["sc2_1"]) = _row_bwd(
        _f_res_mod, [(x2, d, 0), (y_b, d, 0)], vec("g1_1", "sh2_1", "sc2_1"),
        [(dx4, d, 0), (dh2, d, 0)], [F32, BF16], tb=tb, name="l1_res_mod2_bwd")
    dz = _mm(dy_b, wts["b_out"], "nt", F32, "b_out_dx")
    g["b_out"] = _mm(z, dy_b, "tn", BF16, "b_out_dw", tk=1024)
    (do_att, dog), _ = _row_bwd(_f_outgate, [(o_att, HEAD, 0), (proj_q, HEAD, 1)], [], [(dz, HEAD, 0)],
                                [BF16, BF16], nsub=nh, tb=tb, name="out_gate_bwd")
    delta = _attn_delta(do_att, o_att, tb)
    dq_n, dk_n, dv, dfc_q, dfr_k = _attn_bwd(q_n, k_n, v_b, f_col, do_att, lse, delta, attn_blk)
    (dpq,), (g["q_norm_g"],) = _row_bwd(_f_qnorm, [(proj_q, HEAD, 0)], [small["q_norm_g"]],
                                        [(dq_n, HEAD, 0)], [BF16], nsub=nh, tb=tb, name="q_norm_bwd")
    dproj_q = jnp.concatenate([dpq, dog], axis=1)
    dh_q = _mm_wblk_dx(dproj_q, wts["b_q"], F32, "b_q_dx", k=d, gb=nb)
    g["b_q"] = _mm_wblk_dw(h_q, dproj_q, "b_q_dw", nb=nb, gb=nb // 4, tk=t)
    (dpk,), (g["k_norm_g"],) = _row_bwd(_f_knorm, [(proj_kv, HEAD, 0)], [small["k_norm_g"]],
                                        [(dk_n, HEAD, 0)], [BF16], nsub=nh, tb=tb, name="k_norm_bwd")
    dproj_kv = jnp.concatenate([dpk, dv.astype(BF16)], axis=1)
    df_t = dfc_q.T + dfr_k.transpose(1, 0, 2).reshape(nh, t)
    dflogit_t, g["kv_b_f"] = _fgate_bwd(f_logit_t, f_bias, df_t)
    dproj_f = jnp.pad(dflogit_t.T, ((0, 0), (0, LANES - nh))).astype(BF16)
    dh_kv = _mm(dproj_kv, wts["kv"], "nt", F32, "kv_proj_dx")
    dh_kv_f = _mm(dproj_f, wts["kv_f"], "nt", F32, "kv_fproj_dx")
    g["kv"] = _mm(h_kv, dproj_kv, "tn", BF16, "kv_proj_dw", tk=1024)
    g["kv_f"] = _mm(h_kv, dproj_f, "tn", F32, "kv_fproj_dw", tk=1024)
    sent = put_g("l1", {n: g.pop(n) for n in ("b_out", "b_q", "kv", "kv_f", "up1", "down1")})
    (dx1, df0), (dmods["g2_0"], dmods["kv_sh"], dmods["kv_sc"], dmods["sh1_1"], dmods["sc1_1"]) = _row_bwd(
        _f_res_mod2, [(x1, d, 0), (f0, d, 0)], [mods["g2_0"] + sent] + vec("kv_sh", "kv_sc", "sh1_1", "sc1_1"),
        [(dx2, d, 0), (dh_kv, d, 0), (dh_q, d, 0)], [F32, BF16], tb=tb, name="l0_res_kvmod_qmod_bwd",
        cot_add=(1, dh_kv_f))
    dh2, g["up0"], g["down0"], g["conv_w0"], g["conv_b0"] = ffn_bwd(df0, h2_0, u0, a0, 0)
    (dx0, dy_a), (dmods["g1_0"], dmods["sh2_0"], dmods["sc2_0"]) = _row_bwd(
        _f_res_mod, [(x, d, 0), (y_a, d, 0)], vec("g1_0", "sh2_0", "sc2_0"),
        [(dx1, d, 0), (dh2, d, 0)], [F32, BF16], tb=tb, name="l0_res_mod2_bwd")
    dypre = _mm(dy_a, wts["a_out"], "nt", BF16, "a_out_dx")
    g["a_out"] = _mm(ypre, dy_a, "tn", BF16, "a_out_dw", tk=1024)
    sent = put_g("l0b", {n: g.pop(n) for n in ("a_out", "up0", "down0")})
    dpa_q, dpa_f, dpa_i, dpa_g, dlb, g["a_norm_g"] = _hgrn2_bwd(proj_a, lb + sent, small["a_norm_g"], states,
                                                               dypre, tb)
    dproj_a = jnp.concatenate([dpa_q, dpa_f, dpa_i, dpa_g], axis=1)
    dh_a = _mm_wblk_dx(dproj_a, wts["a_in"], F32, "a_in_dx", k=d, gb=nb, tm=512)
    put_g("l0a", {"a_in": _mm_wblk_dw(h_a, dproj_a, "a_in_dw", nb=nb, gb=1, tk=t)})
    (grad_x,), (dmods["sh1_0"], dmods["sc1_0"]) = _row_bwd(
        _f_mod, [(x, d, 0)], vec("sh1_0", "sc1_0"), [(dh_a, d, 0)], [F32], tb=tb, name="l0_mod1_bwd",
        add_to=(0, dx0))
    return loss, grad_x, dmods, dlb, g


def _position():
    return lax.axis_index("x"), lax.axis_index("y"), lax.axis_index("c")


def _hbm_specs(n):
    return [pl.BlockSpec(memory_space=pl.ANY)] * n


def _all_gather(arrs, name):
    n = len(arrs)

    def body(*refs):
        x_refs, out_refs = refs[:n], refs[n:2 * n]
        send_sems, recv_sems, local_sems = refs[2 * n:]
        x, y, cc = _position()
        me, sibling = (x, y, cc), (x, y, 1 - cc)
        chips = [(1 - x, y), (x, 1 - y), (1 - x, 1 - y)]

        def copy(a, k, block, to, src=None):
            slot = out_refs[a].at[4 * block[0] + 2 * block[1] + block[2]]
            return pltpu.make_async_remote_copy(
                src_ref=slot if src is None else src, dst_ref=slot,
                send_sem=send_sems.at[7 * a + k], recv_sem=recv_sems.at[7 * a + k],
                device_id=to, device_id_type=_MESH)

        local = [pltpu.make_async_copy(x_refs[a], out_refs[a].at[4 * x + 2 * y + cc], local_sems.at[a])
                 for a in range(n)]
        for cp in local:
            cp.start()
        first = []
        for a in range(n):
            first.append(copy(a, 0, me, sibling, src=x_refs[a]))
            first += [copy(a, 1 + j, me, (*chip, cc), src=x_refs[a]) for j, chip in enumerate(chips)]
        for cp in first:
            cp.start()
        passed = []
        for j, chip in enumerate(chips):
            for a in range(n):
                copy(a, 1 + j, (*chip, cc), me).wait_recv()
                fwd = copy(a, 4 + j, (*chip, cc), sibling)
                fwd.start()
                passed.append(fwd)
        for a in range(n):
            copy(a, 0, sibling, me).wait_recv()
        for j, chip in enumerate(chips):
            for a in range(n):
                copy(a, 4 + j, (*chip, 1 - cc), me).wait_recv()
        for cp in first + passed:
            cp.wait_send()
        for cp in local:
            cp.wait()

    return pl.pallas_call(
        body, name=name,
        out_shape=[jax.ShapeDtypeStruct((NDEV, *a.shape), a.dtype) for a in arrs],
        in_specs=_hbm_specs(n), out_specs=_hbm_specs(n),
        scratch_shapes=[pltpu.SemaphoreType.DMA((7 * n,)), pltpu.SemaphoreType.DMA((7 * n,)),
                        pltpu.SemaphoreType.DMA((n,))],
    )(*arrs)


_XCHG_EFFECT = pltpu.SideEffectType.DATAFLOW_SIDE_EFFECTING
ALL_PEERS = (1, 2, 3, 4, 5, 6, 7)
SAME_CORE = (2, 4, 6)


def _xchg_copies(src_refs, land_refs, send_sems, recv_sems, local_sems, scatter, rels):
    x, y, cc = _position()
    me = 4 * x + 2 * y + cc
    remote, local = [], []
    for a, (src, land) in enumerate(zip(src_refs, land_refs)):
        local.append(pltpu.make_async_copy(src.at[me] if scatter else src, land.at[me], local_sems.at[a]))
        for idx, rel in enumerate(rels):
            px = 1 - x if rel & 4 else x
            py = 1 - y if rel & 2 else y
            pc = 1 - cc if rel & 1 else cc
            k = len(rels) * a + idx
            remote.append(pltpu.make_async_remote_copy(
                src_ref=src.at[4 * px + 2 * py + pc] if scatter else src, dst_ref=land.at[me],
                send_sem=send_sems.at[k], recv_sem=recv_sems.at[k], device_id=(px, py, pc), device_id_type=_MESH))
    return remote, local


def _xchg_start(srcs, scatter, rels, after, name):
    n = len(srcs)
    lands = [lax.empty(s.shape if scatter else (NDEV, *s.shape), s.dtype) for s in srcs]

    def body(*refs):
        remote, local = _xchg_copies(refs[:n], refs[n:2 * n], *refs[2 * n + 1:2 * n + 4], scatter, rels)
        for cp in local + remote:
            cp.start()
        token = refs[-1]
        token[...] = jnp.zeros_like(token)

    hbm = pl.BlockSpec(memory_space=pltpu.HBM)
    sem = pl.BlockSpec(memory_space=pltpu.SEMAPHORE)
    out = pl.pallas_call(
        body, name=name,
        out_shape=(pltpu.SemaphoreType.DMA((len(rels) * n,)), pltpu.SemaphoreType.DMA((len(rels) * n,)),
                   pltpu.SemaphoreType.DMA((n,)),
                   *[pltpu.HBM(a.shape, a.dtype) for a in srcs + lands], jax.ShapeDtypeStruct((8, LANES), F32)),
        in_specs=[hbm] * (2 * n) + [pl.BlockSpec(memory_space=pl.ANY)],
        out_specs=(sem, sem, sem, *[hbm] * (2 * n), pl.BlockSpec(memory_space=pltpu.VMEM)),
        input_output_aliases={i: 3 + i for i in range(2 * n)},
        compiler_params=pltpu.CompilerParams(has_side_effects=_XCHG_EFFECT),
    )(*[pltpu.with_memory_space_constraint(a, pltpu.HBM) for a in srcs + lands], after)
    return out[:-1], out[-1][0, 0]


def _xchg_wait(handles, after, scatter, rels, name):
    n = (len(handles) - 3) // 2

    def body(*refs):
        remote, local = _xchg_copies(refs[:n], refs[n:2 * n], *refs[2 * n:2 * n + 3], scatter, rels)
        for cp in remote:
            cp.wait_send()
            cp.wait_recv()
        for cp in local:
            cp.wait()

    hbm = pl.BlockSpec(memory_space=pltpu.HBM)
    sem = pl.BlockSpec(memory_space=pltpu.SEMAPHORE)
    thru = list(handles[3:])
    out = pl.pallas_call(
        body, name=name,
        out_shape=tuple(pltpu.HBM(a.shape, a.dtype) for a in thru),
        in_specs=[hbm] * (2 * n) + [sem, sem, sem, pl.BlockSpec(memory_space=pl.ANY)],
        out_specs=tuple([hbm] * (2 * n)),
        input_output_aliases={i: i for i in range(2 * n)},
        compiler_params=pltpu.CompilerParams(has_side_effects=_XCHG_EFFECT),
    )(*thru, *handles[:3], after)
    return list(out[n:])


def _sibling_forward(lands, name):
    n = len(lands)

    def body(*refs):
        land_refs = refs[n:2 * n]
        send_sems, recv_sems = refs[2 * n:]
        x, y, cc = _position()

        def copy(a, q, core):
            slot = land_refs[a].at[2 * q + core]
            return pltpu.make_async_remote_copy(
                src_ref=slot, dst_ref=slot, send_sem=send_sems.at[NCHIP * a + q], recv_sem=recv_sems.at[NCHIP * a + q],
                device_id=(x, y, 1 - cc), device_id_type=_MESH)

        sends = [copy(a, q, cc) for a in range(n) for q in range(NCHIP)]
        for cp in sends:
            cp.start()
        for a in range(n):
            for q in range(NCHIP):
                copy(a, q, 1 - cc).wait_recv()
        for cp in sends:
            cp.wait_send()

    return pl.pallas_call(
        body, name=name,
        out_shape=[jax.ShapeDtypeStruct(a.shape, a.dtype) for a in lands],
        in_specs=_hbm_specs(n), out_specs=_hbm_specs(n),
        input_output_aliases={i: i for i in range(n)},
        scratch_shapes=[pltpu.SemaphoreType.DMA((NCHIP * n,)), pltpu.SemaphoreType.DMA((NCHIP * n,))],
    )(*lands)


def _slab_sum(slabs, name, tr=None):
    n, r, c = slabs.shape
    tr = r if tr is None else tr

    def body(s_ref, o_ref):
        acc = s_ref[0].astype(F32)
        for q in range(1, n):
            acc = acc + s_ref[q].astype(F32)
        o_ref[...] = acc

    return pl.pallas_call(body, name=name, grid=(r // tr,),
                          in_specs=[pl.BlockSpec((n, tr, c), lambda i: (0, i, 0))],
                          out_specs=pl.BlockSpec((tr, c), lambda i: (i, 0)),
                          out_shape=jax.ShapeDtypeStruct((r, c), F32),
                          compiler_params=_cparams(dimension_semantics=("parallel",)))(slabs)


def _ada_fwd(c_all, ada_w, kv_ada_w, logits):
    rows, d = c_all.shape
    n0, nkv = ada_w.shape[2], kv_ada_w.shape[1]

    def body(c_ref, w_ref, kw_ref, lg_ref, part_ref, cact_ref, lb_ref):
        ca = _silu(c_ref[...])
        cact_ref[...] = ca
        part_ref[:, 0:n0] = _bdot_raw(ca, w_ref[0], _NN)
        part_ref[:, n0:2 * n0] = _bdot_raw(ca, w_ref[1], _NN)
        part_ref[:, 2 * n0:2 * n0 + nkv] = _bdot_raw(ca, kw_ref[...], _NN)
        lb_ref[...] = _sigmoid(lg_ref[0:1, :] - lg_ref[1:2, :])

    vm = pl.BlockSpec(memory_space=pltpu.VMEM)
    return pl.pallas_call(
        body, name="ada_fwd", in_specs=[vm, vm, vm, vm], out_specs=[vm, vm, vm],
        out_shape=[jax.ShapeDtypeStruct((rows, 2 * n0 + nkv), F32), jax.ShapeDtypeStruct((rows, d), F32),
                   jax.ShapeDtypeStruct((1, d), F32)],
        compiler_params=_cparams(),
    )(c_all, ada_w, kv_ada_w, logits)


def _ada_bwd(c_act, dm0, dm1, dkv, lb, dlb):
    rows, d = c_act.shape

    def body(c_ref, d0_ref, d1_ref, dk_ref, lb_ref, dlb_ref, dw_ref, dkw_ref, dlg_ref):
        ca = c_ref[...]
        dw_ref[0] = _bdot_raw(ca, d0_ref[...], _TN)
        dw_ref[1] = _bdot_raw(ca, d1_ref[...], _TN)
        dkw_ref[...] = _bdot_raw(ca, dk_ref[...], _TN)
        lbv = lb_ref[...]
        dl0 = dlb_ref[...] * lbv * (1.0 - lbv)
        dlg_ref[0:1, :] = dl0
        dlg_ref[1:2, :] = -dl0

    vm = pl.BlockSpec(memory_space=pltpu.VMEM)
    return pl.pallas_call(
        body, name="ada_bwd", in_specs=[vm] * 6, out_specs=[vm, vm, vm],
        out_shape=[jax.ShapeDtypeStruct((2, d, dm0.shape[1]), F32), jax.ShapeDtypeStruct((d, dkv.shape[1]), F32),
                   jax.ShapeDtypeStruct((2, d), F32)],
        compiler_params=_cparams(),
    )(c_act, dm0, dm1, dkv, lb, dlb)


def _adamw(w, g, m, v, name, tr=512, after=None):
    r, c = w.shape
    tr = _divisor_tile(r, tr, unit=8)
    c1 = 1.0 - ADAM_B1 ** ADAM_STEP
    c2 = 1.0 - ADAM_B2 ** ADAM_STEP
    deps = [] if after is None else [after]

    def body(w_ref, g_ref, m_ref, v_ref, *rest):
        d_ref, mo_ref, vo_ref = rest[len(deps):]
        gv = g_ref[...]
        mn = ADAM_B1 * m_ref[...] + (1.0 - ADAM_B1) * gv
        vn = ADAM_B2 * v_ref[...] + (1.0 - ADAM_B2) * (gv * gv)
        d_ref[...] = -ADAM_LR * ((mn / c1) / (jnp.sqrt(vn / c2) + ADAM_EPS) + ADAM_WD * w_ref[...])
        mo_ref[...] = mn
        vo_ref[...] = vn

    spec = pl.BlockSpec((tr, c), lambda i: (i, 0))
    out = jax.ShapeDtypeStruct((r, c), F32)
    return pl.pallas_call(body, name=name, grid=(r // tr,),
                          in_specs=[spec] * 4 + [pl.BlockSpec(a.shape, lambda i: (0, 0)) for a in deps],
                          out_specs=[spec] * 3, out_shape=[out, out, out],
                          compiler_params=_cparams(dimension_semantics=("parallel",)))(w, g, m, v, *deps)


def _pad_rows(a, rows):
    return jnp.pad(a, ((0, rows - a.shape[0]), (0, 0)))


def _pack_small(parts, lanes=LANES, row_unit=8):
    flat = jnp.concatenate([p.reshape(-1).astype(F32) for p in parts])
    rows = _round_up(-(-flat.shape[0] // lanes), row_unit)
    return jnp.pad(flat, (0, rows * lanes - flat.shape[0])).reshape(rows, lanes)


def _unpack_small(flat, shapes):
    out, off = [], 0
    for s in shapes:
        n = 1
        for k in s:
            n *= k
        out.append(flat[off:off + n].reshape(s))
        off += n
    return out


def _pad_shard_cols(a, n_loc, n_pad):
    lead = a.shape[:-1]
    a = a.reshape(*lead, NDEV, n_loc)
    a = jnp.pad(a, [(0, 0)] * (len(lead) + 1) + [(0, n_pad - n_loc)])
    return a.reshape(*lead, NDEV * n_pad)


def _unpad_shard_cols(a, n_loc, n_pad):
    lead = a.shape[:-1]
    return a.reshape(*lead, NDEV, n_pad)[..., :n_loc].reshape(*lead, NDEV * n_loc)


def kernel(x, c, ada_w, ada_b, a_w_in, a_lb_logits, a_norm_g, a_w_out, kv_ada_w, kv_ada_b, kv_w, kv_b_f, k_norm_g, b_w_q, q_norm_g, b_w_out, ffn_w_up, ffn_conv_w, ffn_conv_b, ffn_w_down, loss_target, m_ada_w, m_ada_b, m_a_w_in, m_a_lb_logits, m_a_norm_g, m_a_w_out, m_kv_ada_w, m_kv_ada_b, m_kv_w, m_kv_b_f, m_k_norm_g, m_b_w_q, m_q_norm_g, m_b_w_out, m_ffn_w_up, m_ffn_conv_w, m_ffn_conv_b, m_ffn_w_down, v_ada_w, v_ada_b, v_a_w_in, v_a_lb_logits, v_a_norm_g, v_a_w_out, v_kv_ada_w, v_kv_ada_b, v_kv_w, v_kv_b_f, v_k_norm_g, v_b_w_q, v_q_norm_g, v_b_w_out, v_ffn_w_up, v_ffn_conv_w, v_ffn_conv_b, v_ffn_w_down):
    t, d = x.shape[1], x.shape[2]
    nh = d // HEAD
    ncw = ffn_w_up.shape[2]
    ncp = _round_up(ncw, LANES)
    two_f = ncw * NDEV
    ff = two_f // 2
    fp = ncp * NDEV // 2
    rd = ffn_w_down.shape[1]
    me = 4 * lax.axis_index("x") + 2 * lax.axis_index("y") + lax.axis_index("c")
    weights = dict(ada_w=ada_w, ada_b=ada_b, a_w_in=a_w_in, a_lb_logits=a_lb_logits, a_norm_g=a_norm_g,
                   a_w_out=a_w_out, kv_ada_w=kv_ada_w, kv_ada_b=kv_ada_b, kv_w=kv_w, kv_b_f=kv_b_f,
                   k_norm_g=k_norm_g, b_w_q=b_w_q, q_norm_g=q_norm_g, b_w_out=b_w_out, ffn_w_up=ffn_w_up,
                   ffn_conv_w=ffn_conv_w, ffn_conv_b=ffn_conv_b, ffn_w_down=ffn_w_down)
    m_in = dict(ada_w=m_ada_w, ada_b=m_ada_b, a_w_in=m_a_w_in, a_lb_logits=m_a_lb_logits, a_norm_g=m_a_norm_g,
                a_w_out=m_a_w_out, kv_ada_w=m_kv_ada_w, kv_ada_b=m_kv_ada_b, kv_w=m_kv_w, kv_b_f=m_kv_b_f,
                k_norm_g=m_k_norm_g, b_w_q=m_b_w_q, q_norm_g=m_q_norm_g, b_w_out=m_b_w_out, ffn_w_up=m_ffn_w_up,
                ffn_conv_w=m_ffn_conv_w, ffn_conv_b=m_ffn_conv_b, ffn_w_down=m_ffn_w_down)
    v_in = dict(ada_w=v_ada_w, ada_b=v_ada_b, a_w_in=v_a_w_in, a_lb_logits=v_a_lb_logits, a_norm_g=v_a_norm_g,
                a_w_out=v_a_w_out, kv_ada_w=v_kv_ada_w, kv_ada_b=v_kv_ada_b, kv_w=v_kv_w, kv_b_f=v_kv_b_f,
                k_norm_g=v_k_norm_g, b_w_q=v_b_w_q, q_norm_g=v_q_norm_g, b_w_out=v_b_w_out, ffn_w_up=v_ffn_w_up,
                ffn_conv_w=v_ffn_conv_w, ffn_conv_b=v_ffn_conv_b, ffn_w_down=v_ffn_w_down)
    order = list(weights)

    up_loc = jnp.pad(ffn_w_up, ((0, 0), (0, 0), (0, ncp - ncw))).astype(BF16)
    down_loc = ffn_w_down.astype(BF16)
    gather_names = {"l0b": ["a_out", "up0", "down0"], "l1": ["kv", "b_q", "b_out", "up1", "down1"]}
    shards = {"a_out": a_w_out[0].astype(BF16), "up0": up_loc[0], "down0": down_loc[0], "kv": kv_w.astype(BF16),
              "b_q": b_w_q[0].astype(BF16), "b_out": b_w_out[0].astype(BF16), "up1": up_loc[1],
              "down1": down_loc[1]}
    pre = _pack_small([c, a_lb_logits, ffn_conv_w])
    a_in_all, pre_all = _all_gather([a_w_in[0].astype(BF16), pre], "gather_a_w_in_and_small_inputs")
    pre_all = pre_all.reshape(NDEV, -1)
    c_all = pre_all[:, :d]
    logits = pre_all[:, d:d + 2 * HEAD].reshape(NDEV, 2, HEAD).transpose(1, 0, 2).reshape(2, d)
    conv_w_full = pre_all[:, d + 2 * HEAD:d + 2 * HEAD + 2 * CONV_TAPS * ncw]
    conv_w_full = conv_w_full.reshape(NDEV, 2, CONV_TAPS, ncw).transpose(1, 2, 0, 3).reshape(2, CONV_TAPS, two_f)

    part, c_act, lb = _ada_fwd(_pad_rows(c_all, 2 * NDEV), ada_w, kv_ada_w, logits)
    (part_all,) = _all_gather([part[:NDEV]], "gather_adaln")
    mine = lax.dynamic_index_in_dim(part_all, me, axis=1, keepdims=False)
    n0, nkv = ada_w.shape[2], kv_ada_w.shape[1]
    mod_names = ["sh1", "sc1", "g1", "sh2", "sc2", "g2"]
    mods = {}
    for l in range(2):
        row = mine[:, l * n0:(l + 1) * n0].reshape(-1) + ada_b[l]
        for k, nm in enumerate(mod_names):
            mods[f"{nm}_{l}"] = row[k * d:(k + 1) * d].reshape(1, d)
    kvrow = mine[:, 2 * n0:2 * n0 + nkv].reshape(-1) + kv_ada_b
    mods["kv_sh"], mods["kv_sc"] = kvrow[:d].reshape(1, d), kvrow[d:].reshape(1, d)

    in_flight = {}

    def start_gather(grp, dep):
        srcs = [shards[n] for n in gather_names[grp]]
        in_flight[grp], started = _xchg_start(srcs, False, SAME_CORE, dep, f"gather_{grp}_start")
        return started

    zero = start_gather("l0b", part_all)
    mods["sh1_0"] = mods["sh1_0"] + zero

    small = {"a_norm_g": a_norm_g, "k_norm_g": k_norm_g.reshape(1, HEAD), "q_norm_g": q_norm_g, "kv_b_f": kv_b_f}
    for l in range(2):
        small[f"conv_w{l}"] = _pad_shard_cols(conv_w_full[l], ncw, ncp).reshape(CONV_TAPS, 2, fp).transpose(1, 0, 2)
        small[f"conv_b{l}"] = _pad_shard_cols(ffn_conv_b[l], ncw, ncp).reshape(2, 1, fp)

    def get_w(grp, after):
        if grp == "l0a":
            return {"a_in": a_in_all}
        arrived = _xchg_wait(in_flight[grp], after, False, SAME_CORE, f"gather_{grp}_wait")
        full = list(_sibling_forward(arrived, f"gather_{grp}_to_sibling"))
        if grp == "l0b":
            started = start_gather("l1", full[0])
            full[0] = full[0] + started.astype(full[0].dtype)
        got = dict(zip(gather_names[grp], full))
        out = {}
        for n, a in got.items():
            if n in ("a_out", "b_out"):
                out[n] = a.reshape(d, d)
            elif n in ("down0", "down1"):
                dn = a.reshape(NCHIP, ff // NCHIP, d)
                out[n] = jnp.pad(dn, ((0, 0), (0, ncp - ncw), (0, 0))).reshape(fp, d)
            elif n == "kv":
                kv_full = a.transpose(1, 0, 2).reshape(d, NDEV * kv_w.shape[1])
                out["kv"] = kv_full[:, :2 * d]
                out["kv_f"] = jnp.pad(kv_full[:, 2 * d:], ((0, 0), (0, LANES - nh)))
            else:
                out[n] = a
        return out

    scatter_flight, g_last = {}, {}

    def put_g(grp, gr):
        if grp == "l0a":
            g_last.update(gr)
            return zero
        if grp == "l1":
            g_kvw = jnp.concatenate([gr["kv"], gr["kv_f"][:, :nh].astype(BF16)], axis=1)
            arrs = {"kv_w": g_kvw.reshape(d, NDEV, kv_w.shape[1]).transpose(1, 0, 2), "b_w_q": gr["b_q"],
                    "b_w_out": gr["b_out"].reshape(NDEV, d // NDEV, d), "up1": gr["up1"],
                    "down1": gr["down1"].reshape(NCHIP, ncp, d)[:, :ncw].reshape(NDEV, rd, d)}
        else:
            arrs = {"a_w_out": gr["a_out"].reshape(NDEV, d // NDEV, d), "up0": gr["up0"],
                    "down0": gr["down0"].reshape(NCHIP, ncp, d)[:, :ncw].reshape(NDEV, rd, d)}
        srcs = list(arrs.values())
        handles, sent = _xchg_start(srcs, True, ALL_PEERS, srcs[0], f"scatter_{grp}_start")
        scatter_flight[grp] = (list(arrs), handles)
        return sent

    loss_v, grad_x, dmods, dlb, g = _local_step(x[0], loss_target[0], mods, lb, small, get_w, put_g)
    loss = lax.psum(loss_v[0, 0], ("x", "y", "c"))

    g_sum = {}
    for grp in ("l1", "l0b"):
        names, handles = scatter_flight[grp]
        for nm, a in zip(names, _xchg_wait(handles, grad_x, True, ALL_PEERS, f"scatter_{grp}_wait")):
            g_sum[nm] = _slab_sum(a, f"rs_slab_sum_{nm}")

    def conv_w_grad(a):
        return _unpad_shard_cols(a.transpose(1, 0, 2).reshape(CONV_TAPS, 2 * fp), ncw, ncp)

    def conv_b_grad(a):
        return _unpad_shard_cols(a.reshape(2 * fp), ncw, ncp)

    dmod_vec = [dmods[f"{nm}_{l}"] for l in range(2) for nm in mod_names] + [dmods["kv_sh"], dmods["kv_sc"]]
    post = _pack_small(dmod_vec + [dlb, g["a_norm_g"], g["k_norm_g"], g["q_norm_g"],
                                   jnp.pad(g["kv_b_f"].reshape(-1), (0, LANES - nh)),
                                   conv_w_grad(g["conv_w0"]), conv_w_grad(g["conv_w1"]),
                                   conv_b_grad(g["conv_b0"]), conv_b_grad(g["conv_b1"])])
    (post_all,) = _all_gather([post], "gather_small_grads")
    a_in_flight, a_in_sent = _xchg_start([g_last["a_in"]], True, ALL_PEERS, post_all, "scatter_l0a_start")
    a_in_sent = a_in_sent.reshape(1, 1)
    tot = _slab_sum(post_all, "small_grad_sum").reshape(-1)
    nmod = 14 * d
    (t_mod, t_lb, t_ang, t_kng, t_qng, t_bf, t_cw, t_cb) = _unpack_small(
        tot, [(nmod,), (1, d), (1, HEAD), (HEAD,), (1, HEAD), (LANES,), (2, CONV_TAPS, two_f), (2, two_f)])
    dm_all = post_all.reshape(NDEV, -1)[:, :nmod]
    dm0 = lax.dynamic_slice_in_dim(dm_all[:, :6 * d], me * n0, n0, axis=1)
    dm1 = lax.dynamic_slice_in_dim(dm_all[:, 6 * d:12 * d], me * n0, n0, axis=1)
    dkv = lax.dynamic_slice_in_dim(dm_all[:, 12 * d:], me * nkv, nkv, axis=1)
    g_ada_w, g_kv_ada_w, g_logits = _ada_bwd(c_act, _pad_rows(dm0, 2 * NDEV), _pad_rows(dm1, 2 * NDEV),
                                              _pad_rows(dkv, 2 * NDEV), lb, t_lb)

    grads = {
        "ada_w": g_ada_w,
        "ada_b": t_mod[:12 * d].reshape(2, 6 * d),
        "a_lb_logits": lax.dynamic_slice_in_dim(g_logits, me * HEAD, HEAD, axis=1),
        "a_norm_g": t_ang,
        "a_w_out": g_sum["a_w_out"].reshape(a_w_out.shape),
        "kv_ada_w": g_kv_ada_w,
        "kv_ada_b": t_mod[12 * d:],
        "kv_w": g_sum["kv_w"],
        "kv_b_f": t_bf[:nh],
        "k_norm_g": t_kng,
        "b_w_q": g_sum["b_w_q"].reshape(b_w_q.shape),
        "q_norm_g": t_qng,
        "b_w_out": g_sum["b_w_out"].reshape(b_w_out.shape),
        "ffn_w_up": jnp.stack([g_sum["up0"][:, :ncw], g_sum["up1"][:, :ncw]]),
        "ffn_conv_w": lax.dynamic_slice_in_dim(t_cw, me * ncw, ncw, axis=2),
        "ffn_conv_b": t_cb,
        "ffn_w_down": jnp.stack([g_sum["down0"], g_sum["down1"]]),
    }

    big_adam = ["ada_w", "a_w_out", "kv_ada_w", "kv_w", "b_w_q", "b_w_out", "ffn_w_up", "ffn_w_down", "a_w_in"]
    small_adam = [n for n in order if n not in big_adam]
    delta, new_m, new_v = {}, {}, {}
    packs = [_pack_small([src[n] for n in small_adam]) for src in (weights, grads, m_in, v_in)]
    outs = _adamw(*packs, "adamw_small", tr=packs[0].shape[0])
    shapes = [weights[n].shape for n in small_adam]
    for dst, o in zip((delta, new_m, new_v), outs):
        for n, a in zip(small_adam, _unpack_small(o.reshape(-1), shapes)):
            dst[n] = a
    for n in big_adam:
        if n == "a_w_in":
            (landed,) = _xchg_wait(a_in_flight, new_v["ffn_w_down"], True, ALL_PEERS, "scatter_l0a_wait")
            grads[n] = _slab_sum(landed, "rs_slab_sum_a_w_in").reshape(a_w_in.shape)
        shp = weights[n].shape
        two_d = lambda a: a.reshape(-1, shp[-1])
        dl, mn, vn = _adamw(two_d(weights[n]), two_d(grads[n]), two_d(m_in[n]), two_d(v_in[n]), f"adamw_{n}",
                            after=a_in_sent)
        delta[n], new_m[n], new_v[n] = dl.reshape(shp), mn.reshape(shp), vn.reshape(shp)

    return (loss, grad_x.reshape(x.shape), *[grads[n] for n in order], *[delta[n] for n in order],
            *[new_m[n] for n in order], *[new_v[n] for n in order])
```

```python
import functools

import jax
import jax.numpy as jnp
from jax import lax
from jax.experimental import pallas as pl
from jax.experimental.pallas import tpu as pltpu

F32 = jnp.float32
BF16 = jnp.bfloat16

NDEV = 8
NCHIP = 4
HEAD = 128
A_CHUNK = 64
CONV_TAPS = 3
EPS = 1e-6
NEG_INF = -1e30
LANES = 128
VMEM_LIMIT = 48 * 1024 * 1024

ADAM_LR = 0.001
ADAM_B1 = 0.9
ADAM_B2 = 0.999
ADAM_EPS = 1e-08
ADAM_WD = 0.01
ADAM_STEP = 10

_NN = (((1,), (0,)), ((), ()))
_NT = (((1,), (1,)), ((), ()))
_TN = (((0,), (0,)), ((), ()))
_MESH = pl.DeviceIdType.MESH


def _cparams(**kw):
    return pltpu.CompilerParams(vmem_limit_bytes=VMEM_LIMIT, **kw)


def _divisor_tile(n, pref, unit=LANES):
    if n <= pref:
        return n
    best = None
    for t in range(unit, pref + 1, unit):
        if n % t == 0:
            best = t
    assert best is not None, (n, pref)
    return best


def _round_up(n, unit):
    return -(-n // unit) * unit


def _bdot_raw(a, b, dims):
    return lax.dot_general(a.astype(BF16), b.astype(BF16), dims, preferred_element_type=F32)


@jax.custom_vjp
def _dot_nn(a, b):
    return _bdot_raw(a, b, _NN)


@jax.custom_vjp
def _dot_nt(a, b):
    return _bdot_raw(a, b, _NT)


@jax.custom_vjp
def _dot_tn(a, b):
    return _bdot_raw(a, b, _TN)


_dot_nn.defvjp(lambda a, b: (_bdot_raw(a, b, _NN), (a, b)),
               lambda r, g: (_dot_nt(g, r[1]), _dot_tn(r[0], g)))
_dot_nt.defvjp(lambda a, b: (_bdot_raw(a, b, _NT), (a, b)),
               lambda r, g: (_dot_nn(g, r[1]), _dot_tn(g, r[0])))
_dot_tn.defvjp(lambda a, b: (_bdot_raw(a, b, _TN), (a, b)),
               lambda r, g: (_dot_nt(r[1], g), _dot_nn(r[0], g)))


def _f32dot(a, b):
    return lax.dot_general(a, b, _NN, precision=lax.Precision.HIGHEST, preferred_element_type=F32)


def _sigmoid(x):
    return jax.nn.sigmoid(x)


def _silu(x):
    return x * jax.nn.sigmoid(x)


def _rms(x):
    return x * lax.rsqrt(jnp.mean(x * x, axis=-1, keepdims=True) + EPS)


def _modulate(x, sh, sc):
    return _rms(x) * (1.0 + sc) + sh


def _mm_call(a, b, dims, a_spec, b_spec, o_spec, o_shape, grid, acc_tile, name):
    nk = grid[2]

    def body(a_ref, b_ref, o_ref, *acc):
        p = lax.dot_general(a_ref[...].astype(BF16), b_ref[...].astype(BF16), dims,
                            preferred_element_type=F32)
        if nk == 1:
            o_ref[...] = p.astype(o_ref.dtype)
        else:
            kk = pl.program_id(2)

            @pl.when(kk == 0)
            def _():
                acc[0][...] = p

            @pl.when(kk > 0)
            def _():
                acc[0][...] += p

            @pl.when(kk == nk - 1)
            def _():
                o_ref[...] = acc[0][...].astype(o_ref.dtype)

    return pl.pallas_call(
        body, name=name, grid=grid, in_specs=[a_spec, b_spec], out_specs=o_spec, out_shape=o_shape,
        scratch_shapes=[pltpu.VMEM(acc_tile, F32)] if nk > 1 else [],
        compiler_params=_cparams(dimension_semantics=("parallel", "parallel", "arbitrary")),
    )(a, b)


def _mm(a, b, mode, out_dtype, name, tm=1024, tn=1024, tk=2048):
    if mode == "nn":
        (m, k), (k2, n) = a.shape, b.shape
    elif mode == "nt":
        (m, k), (n, k2) = a.shape, b.shape
    else:
        (k, m), (k2, n) = a.shape, b.shape
    assert k == k2, (a.shape, b.shape, mode)
    tm, tn, tk = _divisor_tile(m, tm), _divisor_tile(n, tn), _divisor_tile(k, tk)
    if mode == "tn":
        a_spec = pl.BlockSpec((tk, tm), lambda i, j, kk: (kk, i))
    else:
        a_spec = pl.BlockSpec((tm, tk), lambda i, j, kk: (i, kk))
    if mode == "nt":
        b_spec = pl.BlockSpec((tn, tk), lambda i, j, kk: (j, kk))
    else:
        b_spec = pl.BlockSpec((tk, tn), lambda i, j, kk: (kk, j))
    return _mm_call(a, b, {"nn": _NN, "nt": _NT, "tn": _TN}[mode], a_spec, b_spec,
                    pl.BlockSpec((tm, tn), lambda i, j, kk: (i, j)), jax.ShapeDtypeStruct((m, n), out_dtype),
                    (m // tm, n // tn, k // tk), (tm, tn), name)


def _wblk_act_spec(rows, gb, nl, split, nb, row_axis, blk_axis):
    if split == 1:
        return pl.BlockSpec((rows, gb * nl), lambda *g: (g[row_axis], g[blk_axis]))
    groups = nb // split // gb
    return pl.BlockSpec((None, rows, gb * nl),
                        lambda *g: (g[blk_axis] // groups, g[row_axis], g[blk_axis] % groups))


def _mm_wblk(a, wb, out_dtype, name, *, gb, row_off=0, split=1, tm=1024):
    m, k = a.shape
    nb, _, nl = wb.shape
    assert (nb // split) % gb == 0
    tm = _divisor_tile(m, tm)

    def body(a_ref, b_ref, o_ref):
        av = a_ref[...].astype(BF16)
        for s in range(gb):
            o_ref[:, s * nl:(s + 1) * nl] = lax.dot_general(
                av, b_ref[s].astype(BF16), _NN, preferred_element_type=F32).astype(o_ref.dtype)

    o_shape = (m, nb * nl) if split == 1 else (split, m, nb // split * nl)
    return pl.pallas_call(
        body, name=name, grid=(nb // gb, m // tm),
        in_specs=[pl.BlockSpec((tm, k), lambda j, i: (i, 0)),
                  pl.BlockSpec((gb, k, nl), lambda j, i: (j, row_off, 0))],
        out_specs=_wblk_act_spec(tm, gb, nl, split, nb, 1, 0),
        out_shape=jax.ShapeDtypeStruct(o_shape, out_dtype),
        compiler_params=_cparams(dimension_semantics=("parallel", "parallel")),
    )(a, wb)


def _mm_wblk_dx(dy, wb, out_dtype, name, *, k, gb, row_off=0, split=1, tm=1024):
    nb, _, nl = wb.shape
    m = dy.shape[-2]
    tm = _divisor_tile(m, tm)
    nk = nb // gb
    per = nb // split
    whole = split > 1 and gb == nb
    assert whole or per % gb == 0

    def body(a_ref, b_ref, o_ref, *acc):
        p = None
        for s in range(gb):
            a_blk = a_ref[s // per, :, (s % per) * nl:(s % per + 1) * nl] if whole else a_ref[:, s * nl:(s + 1) * nl]
            q = lax.dot_general(a_blk.astype(BF16), b_ref[s].astype(BF16), _NT, preferred_element_type=F32)
            p = q if p is None else p + q
        if nk == 1:
            o_ref[...] = p.astype(o_ref.dtype)
        else:
            kk = pl.program_id(1)

            @pl.when(kk == 0)
            def _():
                acc[0][...] = p

            @pl.when(kk > 0)
            def _():
                acc[0][...] += p

            @pl.when(kk == nk - 1)
            def _():
                o_ref[...] = acc[0][...].astype(o_ref.dtype)

    return pl.pallas_call(
        body, name=name, grid=(m // tm, nk),
        in_specs=[pl.BlockSpec((split, tm, per * nl), lambda i, kk: (0, i, 0)) if whole
                  else _wblk_act_spec(tm, gb, nl, split, nb, 0, 1),
                  pl.BlockSpec((gb, k, nl), lambda i, kk: (kk, row_off, 0))],
        out_specs=pl.BlockSpec((tm, k), lambda i, kk: (i, 0)),
        out_shape=jax.ShapeDtypeStruct((m, k), out_dtype),
        scratch_shapes=[pltpu.VMEM((tm, k), F32)] if nk > 1 else [],
        compiler_params=_cparams(dimension_semantics=("parallel", "arbitrary")),
    )(dy, wb)


def _mm_wblk_dw(x, dy, name, *, nb, gb, split=1, tk=1024):
    t, k = x.shape
    assert (nb // split) % gb == 0
    nl = dy.shape[-1] * split // nb
    tk = _divisor_tile(t, tk)
    nk = t // tk

    def body(a_ref, b_ref, o_ref, *acc):
        kk = pl.program_id(1)
        av = a_ref[...].astype(BF16)
        for s in range(gb):
            p = lax.dot_general(av, b_ref[:, s * nl:(s + 1) * nl].astype(BF16), _TN, preferred_element_type=F32)
            if nk == 1:
                o_ref[s] = p.astype(o_ref.dtype)
                continue

            @pl.when(kk == 0)
            def _():
                acc[0][s] = p

            @pl.when(kk > 0)
            def _():
                acc[0][s] += p

        if nk > 1:
            @pl.when(kk == nk - 1)
            def _():
                o_ref[...] = acc[0][...].astype(o_ref.dtype)

    return pl.pallas_call(
        body, name=name, grid=(nb // gb, nk),
        in_specs=[pl.BlockSpec((tk, k), lambda j, kk: (kk, 0)), _wblk_act_spec(tk, gb, nl, split, nb, 1, 0)],
        out_specs=pl.BlockSpec((gb, k, nl), lambda j, kk: (j, 0, 0)),
        out_shape=jax.ShapeDtypeStruct((nb, k, nl), BF16),
        scratch_shapes=[pltpu.VMEM((gb, k, nl), F32)] if nk > 1 else [],
        compiler_params=_cparams(dimension_semantics=("parallel", "arbitrary")),
    )(x, dy)


def _row_specs(rows, tb, nsub):
    return [pl.BlockSpec((tb, nsub * cw), functools.partial(lambda i, off: (i, off), off=off))
            for (_, cw, off) in rows]


def _vec_specs(params):
    return [pl.BlockSpec(p.shape, lambda i: (0, 0)) for p in params]


def _row_fwd(f, rows, params, out_dtypes, *, nsub=1, tb, name):
    t = rows[0][0].shape[0]
    tb = min(tb, t)
    n_r, n_p = len(rows), len(params)
    blk = [jax.ShapeDtypeStruct((tb, cw), F32) for (_, cw, _) in rows]
    blk += [jax.ShapeDtypeStruct(p.shape, F32) for p in params]
    out_avals = jax.eval_shape(f, *blk)

    def body(*refs):
        pv = [r[...] for r in refs[n_r:n_r + n_p]]
        for s in range(nsub):
            vals = [r[:, s * cw:(s + 1) * cw].astype(F32) for r, (_, cw, _) in zip(refs[:n_r], rows)]
            outs = f(*vals, *pv)
            for o_ref, o in zip(refs[n_r + n_p:], outs):
                w = o.shape[1]
                o_ref[:, s * w:(s + 1) * w] = o.astype(o_ref.dtype)

    return pl.pallas_call(
        body, name=name,
        grid=(t // tb,),
        in_specs=_row_specs(rows, tb, nsub) + _vec_specs(params),
        out_specs=[pl.BlockSpec((tb, nsub * av.shape[1]), lambda i: (i, 0)) for av in out_avals],
        out_shape=[jax.ShapeDtypeStruct((t, nsub * av.shape[1]), dt) for av, dt in zip(out_avals, out_dtypes)],
        compiler_params=_cparams(dimension_semantics=("parallel",)),
    )(*[r[0] for r in rows], *params)


def _row_bwd(f, rows, params, cots, row_grad_dtypes, *, nsub=1, tb, name, add_to=None, cot_add=None):
    t = rows[0][0].shape[0]
    tb = min(tb, t)
    n_r, n_p, n_c = len(rows), len(params), len(cots)
    want = [j for j in range(n_r) if row_grad_dtypes[j] is not None]
    extra = [] if add_to is None else [(add_to[1], rows[add_to[0]][1], 0)]
    extra += [] if cot_add is None else [(cot_add[1], cots[cot_add[0]][1], 0)]

    def body(*refs):
        i = pl.program_id(0)
        r_in, p_in = refs[:n_r], refs[n_r:n_r + n_p]
        c_in = refs[n_r + n_p:n_r + n_p + n_c]
        e_in = refs[n_r + n_p + n_c:n_r + n_p + n_c + len(extra)]
        outs = refs[n_r + n_p + n_c + len(extra):]
        pv = [r[...] for r in p_in]
        psum = [None] * n_p
        for s in range(nsub):
            vals = [r[:, s * cw:(s + 1) * cw].astype(F32) for r, (_, cw, _) in zip(r_in, rows)]
            cvals = [r[:, s * cw:(s + 1) * cw].astype(F32) for r, (_, cw, _) in zip(c_in, cots)]
            if cot_add is not None:
                cw = cots[cot_add[0]][1]
                cvals[cot_add[0]] = cvals[cot_add[0]] + e_in[-1][:, s * cw:(s + 1) * cw]
            _, vjp_fn = jax.vjp(f, *vals, *pv)
            grads = vjp_fn(tuple(cvals))
            for o_ref, jr in zip(outs[:len(want)], want):
                cw = rows[jr][1]
                gr = grads[jr]
                if add_to is not None and jr == add_to[0]:
                    gr = gr + e_in[0][:, s * cw:(s + 1) * cw]
                o_ref[:, s * cw:(s + 1) * cw] = gr.astype(o_ref.dtype)
            for jp in range(n_p):
                psum[jp] = grads[n_r + jp] if psum[jp] is None else psum[jp] + grads[n_r + jp]
        for o_ref, g in zip(outs[len(want):], psum):
            @pl.when(i == 0)
            def _():
                o_ref[...] = g

            @pl.when(i > 0)
            def _():
                o_ref[...] += g

    out_specs = [pl.BlockSpec((tb, nsub * rows[jr][1]), lambda i: (i, 0)) for jr in want]
    out_shape = [jax.ShapeDtypeStruct((t, nsub * rows[jr][1]), row_grad_dtypes[jr]) for jr in want]
    out_specs += _vec_specs(params)
    out_shape += [jax.ShapeDtypeStruct(p.shape, F32) for p in params]
    res = pl.pallas_call(
        body, name=name,
        grid=(t // tb,),
        in_specs=_row_specs(rows, tb, nsub) + _vec_specs(params) + _row_specs(cots, tb, nsub)
        + _row_specs(extra, tb, nsub),
        out_specs=out_specs, out_shape=out_shape,
        compiler_params=_cparams(dimension_semantics=("arbitrary",)),
    )(*[r[0] for r in rows], *params, *[c[0] for c in cots], *[e[0] for e in extra])
    return res[:len(want)], res[len(want):]


def _f_mod(x, sh, sc):
    return (_modulate(x, sh, sc),)


def _f_res_mod(x, y, g, sh, sc):
    x1 = x + g * y
    return x1, _modulate(x1, sh, sc)


def _f_res_mod2(x, y, g, sh_a, sc_a, sh_b, sc_b):
    x1 = x + g * y
    return x1, _modulate(x1, sh_a, sc_a), _modulate(x1, sh_b, sc_b)


def _f_qnorm(p, g):
    return (_rms(p) * g * (HEAD ** -0.5),)


def _f_knorm(p, g):
    return (_rms(p) * g,)


def _f_qnorm_aug(p, g):
    lane = lax.broadcasted_iota(jnp.int32, p.shape, 1)
    return (jnp.concatenate([_rms(p) * g * (HEAD ** -0.5), jnp.where(lane < 3, 1.0, 0.0)], axis=1),)


def _f_knorm_aug(p, c0, c1, c2, g):
    lane = lax.broadcasted_iota(jnp.int32, p.shape, 1)
    aug = jnp.where(lane == 0, c0, jnp.where(lane == 1, c1, jnp.where(lane == 2, c2, 0.0)))
    return (jnp.concatenate([_rms(p) * g, aug], axis=1),)


def _split3(a):
    round_bf16 = lambda v: lax.reduce_precision(v, exponent_bits=8, mantissa_bits=7)
    hi = round_bf16(a)
    mid = round_bf16(a - hi)
    lo = round_bf16(a - hi - mid)
    return hi.astype(BF16), mid.astype(BF16), lo.astype(BF16)


def _f_outgate(o, og):
    return (o * _sigmoid(og),)


def _loss_call(x3, f, g2, target, tb):
    t, d = x3.shape
    tb = min(tb, t)

    def body(x_ref, f_ref, g_ref, t_ref, loss_ref, dx_ref, df_ref, dg_ref):
        i = pl.program_id(0)
        fv = f_ref[...]
        g = g_ref[...]
        e = x_ref[...] + g * fv - t_ref[...]
        dx = e * (1.0 / d)
        part = 0.5 * jnp.sum(jnp.sum(e * dx, axis=1, keepdims=True), axis=0, keepdims=True)
        dx_ref[...] = dx
        df_ref[...] = (g * dx).astype(df_ref.dtype)
        dg = jnp.sum(dx * fv, axis=0, keepdims=True)

        @pl.when(i == 0)
        def _():
            loss_ref[...] = jnp.broadcast_to(part, loss_ref.shape)
            dg_ref[...] = dg

        @pl.when(i > 0)
        def _():
            loss_ref[...] += jnp.broadcast_to(part, loss_ref.shape)
            dg_ref[...] += dg

    row = pl.BlockSpec((tb, d), lambda i: (i, 0))
    vec = pl.BlockSpec((1, d), lambda i: (0, 0))
    return pl.pallas_call(
        body, name="loss_head",
        grid=(t // tb,),
        in_specs=[row, row, vec, row],
        out_specs=[pl.BlockSpec((1, LANES), lambda i: (0, 0)), row, row, vec],
        out_shape=[jax.ShapeDtypeStruct((1, LANES), F32), jax.ShapeDtypeStruct((t, d), F32),
                   jax.ShapeDtypeStruct((t, d), BF16), jax.ShapeDtypeStruct((1, d), F32)],
        compiler_params=_cparams(dimension_semantics=("arbitrary",)),
    )(x3, f, g2, target)


def _hg_consts(tb):
    c = A_CHUNK
    r = lax.broadcasted_iota(jnp.int32, (c, c), 0)
    s = lax.broadcasted_iota(jnp.int32, (c, c), 1)
    br = lax.broadcasted_iota(jnp.int32, (tb, tb), 0)
    bs = lax.broadcasted_iota(jnp.int32, (tb, tb), 1)
    shift = c.bit_length() - 1
    same_chunk = jnp.right_shift(br, shift) == jnp.right_shift(bs, shift)
    return (s <= r).astype(F32), (r <= s).astype(F32), jnp.logical_and(same_chunk, bs <= br)


def _chunk_apply(mat, x):
    c = mat.shape[0]
    return jnp.concatenate([_f32dot(mat, x[i * c:(i + 1) * c]) for i in range(x.shape[0] // c)], axis=0)


@jax.custom_vjp
def _chunk_cumsum(x, tri, tri_t):
    return _chunk_apply(tri, x)


_chunk_cumsum.defvjp(lambda x, tri, tri_t: (_chunk_apply(tri, x), (tri, tri_t)),
                     lambda r, g: (_chunk_apply(r[1], g), jnp.zeros_like(r[0]), jnp.zeros_like(r[1])))


def _per_chunk(a, b, dims):
    return jnp.stack([_bdot_raw(a[i], b[i], dims) for i in range(a.shape[0])])


@jax.custom_vjp
def _chunk_tn(a, b):
    return _per_chunk(a, b, _TN)


@jax.custom_vjp
def _chunk_nt(a, b):
    return _per_chunk(a, b, _NT)


@jax.custom_vjp
def _chunk_nn(a, b):
    return _per_chunk(a, b, _NN)


_chunk_tn.defvjp(lambda a, b: (_per_chunk(a, b, _TN), (a, b)),
                 lambda r, g: (_chunk_nt(r[1], g), _chunk_nn(r[0], g)))
_chunk_nt.defvjp(lambda a, b: (_per_chunk(a, b, _NT), (a, b)),
                 lambda r, g: (_chunk_nn(g, r[1]), _chunk_tn(g, r[0])))
_chunk_nn.defvjp(lambda a, b: (_per_chunk(a, b, _NN), (a, b)),
                 lambda r, g: (_chunk_nt(g, r[1]), _chunk_tn(r[0], g)))


def _scan_states(decay, m, st):
    sts = []
    for i in range(m.shape[0]):
        sts.append(st)
        st = st * decay[i] + m[i]
    return jnp.stack(sts), st


@jax.custom_vjp
def _state_scan(decay, m, st):
    return _scan_states(decay, m, st)


def _state_scan_fwd(decay, m, st):
    sts, st_out = _scan_states(decay, m, st)
    return (sts, st_out), (decay, sts)


def _state_scan_bwd(res, cts):
    decay, sts = res
    d_sts, g = cts
    d_decay, d_m = [], []
    for i in range(sts.shape[0] - 1, -1, -1):
        d_m.append(g)
        d_decay.append(jnp.sum(g * sts[i], axis=0, keepdims=True))
        g = g * decay[i] + d_sts[i]
    return jnp.stack(d_decay[::-1]), jnp.stack(d_m[::-1]), g


_state_scan.defvjp(_state_scan_fwd, _state_scan_bwd)


def _hg_block(qp, fp, ip, gp, lb, ng, st, tri, tri_t, bd_causal):
    tb = qp.shape[0]
    c = A_CHUNK
    n = tb // c
    q = _silu(qp)
    fg = lb + (1.0 - lb) * _sigmoid(fp)
    logf = jnp.log(fg)
    k = 1.0 - fg
    b3 = _chunk_cumsum(logf, tri, tri_t).reshape(n, c, HEAD)
    pos = lax.broadcasted_iota(jnp.int32, (1, c, 1), 1)
    b_mid = lax.stop_gradient(jnp.sum(jnp.where(pos == c // 2, b3, 0.0), axis=1, keepdims=True))
    b_last = jnp.sum(jnp.where(pos == c - 1, b3, 0.0), axis=1, keepdims=True)
    q3, k3, v3 = q.reshape(n, c, HEAD), k.reshape(n, c, HEAD), ip.reshape(n, c, HEAD)
    scores = _dot_nt((q3 * jnp.exp(b3 - b_mid)).reshape(tb, HEAD), (k3 * jnp.exp(b_mid - b3)).reshape(tb, HEAD))
    o_intra = _dot_nn(jnp.where(bd_causal, scores, 0.0), ip)
    states, st_new = _state_scan(jnp.exp(b_last), _chunk_tn(v3, k3 * jnp.exp(b_last - b3)), st)
    o = o_intra + _chunk_nt(q3 * jnp.exp(b3), states).reshape(tb, HEAD)
    y = _rms(o) * ng * _silu(gp)
    return y, st_new


HG_HEADS = 2


def _hg_specs(tb, nh, rev_nb=None):
    wide = HG_HEADS * HEAD
    per = nh // HG_HEADS

    def row(part):
        if rev_nb is None:
            return pl.BlockSpec((tb, wide), functools.partial(lambda h, i, off: (i, off + h), off=part * per))
        return pl.BlockSpec((tb, wide),
                            functools.partial(lambda h, i, off: (rev_nb - 1 - i, off + h), off=part * per))
    return [row(0), row(1), row(2), row(3),
            pl.BlockSpec((1, wide), lambda h, i: (0, h)), pl.BlockSpec((1, HEAD), lambda h, i: (0, 0))]


def _hgrn2_fwd(proj, lb, ng, tb):
    t = proj.shape[0]
    nh = proj.shape[1] // (4 * HEAD)
    tb = min(tb, t)
    nb = t // tb
    wide = HG_HEADS * HEAD

    def body(q_ref, f_ref, i_ref, g_ref, lb_ref, ng_ref, y_ref, s_ref, st_ref):
        i = pl.program_id(1)

        @pl.when(i == 0)
        def _():
            st_ref[...] = jnp.zeros_like(st_ref)

        consts = _hg_consts(tb)
        for p in range(HG_HEADS):
            cs = slice(p * HEAD, (p + 1) * HEAD)
            st = st_ref[p]
            s_ref[p, 0] = st
            y, st_new = _hg_block(q_ref[:, cs], f_ref[:, cs], i_ref[:, cs], g_ref[:, cs], lb_ref[:, cs],
                                  ng_ref[...], st, *consts)
            y_ref[:, cs] = y.astype(y_ref.dtype)
            st_ref[p] = st_new

    return pl.pallas_call(
        body, name="hgrn2_fwd",
        grid=(nh // HG_HEADS, nb),
        in_specs=_hg_specs(tb, nh),
        out_specs=[pl.BlockSpec((tb, wide), lambda h, i: (i, h)),
                   pl.BlockSpec((HG_HEADS, 1, HEAD, HEAD), lambda h, i: (h, i, 0, 0))],
        out_shape=[jax.ShapeDtypeStruct((t, nh * HEAD), BF16),
                   jax.ShapeDtypeStruct((nh, nb, HEAD, HEAD), F32)],
        scratch_shapes=[pltpu.VMEM((HG_HEADS, HEAD, HEAD), F32)],
        compiler_params=_cparams(dimension_semantics=("parallel", "arbitrary")),
    )(proj, proj, proj, proj, lb, ng)


def _hgrn2_bwd(proj, lb, ng, states, dy, tb):
    t = proj.shape[0]
    nh = proj.shape[1] // (4 * HEAD)
    tb = min(tb, t)
    nb = t // tb
    wide = HG_HEADS * HEAD

    def body(q_ref, f_ref, i_ref, g_ref, lb_ref, ng_ref, s_ref, dy_ref, dp_ref, dlb_ref, dng_ref, dst_ref):
        h, i = pl.program_id(0), pl.program_id(1)
        consts = _hg_consts(tb)

        @pl.when(i == 0)
        def _():
            dst_ref[...] = jnp.zeros_like(dst_ref)
            dlb_ref[...] = jnp.zeros_like(dlb_ref)

        @pl.when(jnp.logical_and(i == 0, h == 0))
        def _():
            dng_ref[...] = jnp.zeros_like(dng_ref)

        def fn(qp, fp, ip, gp, lbx, ngx, stx):
            return _hg_block(qp, fp, ip, gp, lbx, ngx, stx, *consts)

        for p in range(HG_HEADS):
            cs = slice(p * HEAD, (p + 1) * HEAD)
            _, vjp_fn = jax.vjp(fn, q_ref[:, cs], f_ref[:, cs], i_ref[:, cs], g_ref[:, cs], lb_ref[:, cs],
                                ng_ref[...], s_ref[p, 0])
            *gparts, glb, gng, dst = vjp_fn((dy_ref[:, cs].astype(F32), dst_ref[p]))
            for part, gpart in enumerate(gparts):
                dp_ref[part, :, cs] = gpart.astype(dp_ref.dtype)
            dst_ref[p] = dst
            dlb_ref[:, cs] += glb
            dng_ref[...] += gng

    rev = lambda h, i: (nb - 1 - i, h)
    return pl.pallas_call(
        body, name="hgrn2_bwd",
        grid=(nh // HG_HEADS, nb),
        in_specs=_hg_specs(tb, nh, rev_nb=nb) + [
            pl.BlockSpec((HG_HEADS, 1, HEAD, HEAD), lambda h, i: (h, nb - 1 - i, 0, 0)),
            pl.BlockSpec((tb, wide), rev)],
        out_specs=[pl.BlockSpec((4, tb, wide), lambda h, i: (0, nb - 1 - i, h)),
                   pl.BlockSpec((1, wide), lambda h, i: (0, h)), pl.BlockSpec((1, HEAD), lambda h, i: (0, 0))],
        out_shape=[jax.ShapeDtypeStruct((4, t, nh * HEAD), BF16),
                   jax.ShapeDtypeStruct((1, nh * HEAD), F32), jax.ShapeDtypeStruct((1, HEAD), F32)],
        scratch_shapes=[pltpu.VMEM((HG_HEADS, HEAD, HEAD), F32)],
        compiler_params=_cparams(dimension_semantics=("arbitrary", "arbitrary")),
    )(proj, proj, proj, proj, lb, ng, states, dy)


def _fgate_consts(cb):
    r = lax.broadcasted_iota(jnp.int32, (cb, cb), 0)
    s = lax.broadcasted_iota(jnp.int32, (cb, cb), 1)
    return (r <= s).astype(F32), (r >= s).astype(F32)


def _fgate_fwd(xt, bias, cb=512):
    nh, t = xt.shape
    cb = min(cb, t)

    def body(x_ref, b_ref, o_ref):
        upper, _ = _fgate_consts(cb)
        carry = jnp.zeros((nh, 1), F32)
        for blk in range(t // cb):
            z = x_ref[:, blk * cb:(blk + 1) * cb] + b_ref[...]
            logf = jnp.minimum(z, 0.0) - jnp.log(1.0 + jnp.exp(-jnp.abs(z)))
            cs = _f32dot(logf, upper) + carry
            o_ref[:, blk * cb:(blk + 1) * cb] = cs
            carry = cs[:, cb - 1:cb]

    vm = pl.BlockSpec(memory_space=pltpu.VMEM)
    return pl.pallas_call(
        body, name="fgate_fwd", in_specs=[vm, vm], out_specs=vm,
        out_shape=jax.ShapeDtypeStruct((nh, t), F32), compiler_params=_cparams(),
    )(xt, bias)


def _fgate_bwd(xt, bias, dft, cb=512):
    nh, t = xt.shape
    cb = min(cb, t)
    nblk = t // cb

    def body(x_ref, b_ref, d_ref, dx_ref, db_ref):
        _, lower = _fgate_consts(cb)
        carry = jnp.zeros((nh, 1), F32)
        db = jnp.zeros((nh, 1), F32)
        for blk in range(nblk - 1, -1, -1):
            sl = slice(blk * cb, (blk + 1) * cb)
            dlogf = _f32dot(d_ref[:, sl], lower) + carry
            carry = dlogf[:, 0:1]
            z = x_ref[:, sl] + b_ref[...]
            dz = dlogf * (1.0 - _sigmoid(z))
            dx_ref[:, sl] = dz
            db = db + jnp.sum(dz, axis=1, keepdims=True)
        db_ref[...] = db

    vm = pl.BlockSpec(memory_space=pltpu.VMEM)
    return pl.pallas_call(
        body, name="fgate_bwd", in_specs=[vm, vm, vm], out_specs=[vm, vm],
        out_shape=[jax.ShapeDtypeStruct((nh, t), F32), jax.ShapeDtypeStruct((nh, 1), F32)],
        compiler_params=_cparams(),
    )(xt, bias, dft)


def _attn_fwd(q, k, v, f_col, blk):
    t, width = v.shape
    nh = width // HEAD
    nq = t // blk

    def body(q_ref, k_ref, v_ref, fc_ref, o_ref, lse_ref):
        i = pl.program_id(0)
        tri = (lax.broadcasted_iota(jnp.int32, (blk, blk), 1) <= lax.broadcasted_iota(jnp.int32, (blk, blk), 0))
        for h in range(nh):
            cs = slice(h * HEAD, (h + 1) * HEAD)
            cs2 = slice(2 * h * HEAD, 2 * (h + 1) * HEAD)
            qh = q_ref[:, cs2]

            def tile(j, carry, masked):
                m, l, acc = carry
                rs = pl.ds(pl.multiple_of(j * blk, blk), blk)
                s = _bdot_raw(qh, k_ref[rs, cs2], _NT)
                if masked:
                    s = jnp.where(tri, s, NEG_INF)
                m_new = jnp.maximum(m, jnp.max(s, axis=1, keepdims=True))
                p = jnp.exp(s - m_new)
                alpha = jnp.exp(m - m_new)
                l_new = alpha * l + jnp.sum(p, axis=1, keepdims=True)
                acc_new = alpha * acc + _bdot_raw(p, v_ref[rs, cs], _NN)
                return m_new, l_new, acc_new

            init = (jnp.full((blk, 1), NEG_INF, F32), jnp.zeros((blk, 1), F32), jnp.zeros((blk, HEAD), F32))
            carry = lax.fori_loop(0, i, lambda j, c: tile(j, c, False), init)
            m, l, acc = tile(i, carry, True)
            o_ref[:, cs] = acc / l
            lse_ref[:, h:h + 1] = m + jnp.log(l) + fc_ref[:, h:h + 1]

    vm = pl.BlockSpec(memory_space=pltpu.VMEM)
    return pl.pallas_call(
        body, name="fox_attn_fwd",
        grid=(nq,),
        in_specs=[pl.BlockSpec((blk, 2 * width), lambda i: (i, 0)), vm, vm,
                  pl.BlockSpec((blk, nh), lambda i: (i, 0))],
        out_specs=[pl.BlockSpec((blk, width), lambda i: (i, 0)), pl.BlockSpec((blk, nh), lambda i: (i, 0))],
        out_shape=[jax.ShapeDtypeStruct((t, width), F32), jax.ShapeDtypeStruct((t, nh), F32)],
        compiler_params=_cparams(dimension_semantics=("parallel",)),
    )(q, k, v, f_col)


def _attn_delta(do, o, tb):
    t, width = o.shape
    nh = width // HEAD
    tb = min(tb, t)

    def body(do_ref, o_ref, dl_ref):
        for h in range(nh):
            cs = slice(h * HEAD, (h + 1) * HEAD)
            dl_ref[:, h:h + 1] = jnp.sum(do_ref[:, cs].astype(F32) * o_ref[:, cs], axis=1, keepdims=True)

    wide = pl.BlockSpec((tb, width), lambda i: (i, 0))
    return pl.pallas_call(body, name="fox_attn_delta", grid=(t // tb,), in_specs=[wide, wide],
                          out_specs=pl.BlockSpec((tb, nh), lambda i: (i, 0)),
                          out_shape=jax.ShapeDtypeStruct((t, nh), F32),
                          compiler_params=_cparams(dimension_semantics=("parallel",)))(do, o)


ATTN_BWD_GROUPS = 4


def _attn_bwd(q, k, v, f_col, do, lse, delta, blk):
    t, width = v.shape
    nh = width // HEAD
    nq = t // blk
    hpg = nh // ATTN_BWD_GROUPS
    gw = hpg * HEAD

    def body(q_ref, do_ref, k_ref, v_ref, fc_ref, lse_ref, dl_ref,
             dq_ref, dk_ref, dv_ref, dfc_ref, dfr_ref):
        g, j = pl.program_id(0), pl.program_id(1)
        tri = (lax.broadcasted_iota(jnp.int32, (blk, blk), 1) <= lax.broadcasted_iota(jnp.int32, (blk, blk), 0))

        @pl.when(j == 0)
        def _():
            dq_ref[...] = jnp.zeros_like(dq_ref)
            dfc_ref[...] = jnp.zeros_like(dfc_ref)

        for h in range(hpg):
            cs = slice(h * HEAD, (h + 1) * HEAD)
            cs2 = slice(2 * h * HEAD, 2 * (h + 1) * HEAD)
            csq = slice(2 * h * HEAD, (2 * h + 1) * HEAD)
            kj2 = k_ref[:, cs2]
            kj = k_ref[:, csq]
            vj = v_ref[:, cs]

            def tile(i, carry, masked):
                dk, dv, dfs = carry
                rs = pl.ds(pl.multiple_of(i * blk, blk), blk)
                qi = q_ref[rs, csq]
                doi = do_ref[rs, cs]
                bias = fc_ref[0, rs, h:h + 1] - lse_ref[0, rs, h:h + 1]
                p = jnp.exp(_bdot_raw(q_ref[rs, cs2], kj2, _NT) + bias)
                if masked:
                    p = jnp.where(tri, p, 0.0)
                ds = p * (_bdot_raw(doi, vj, _NT) - dl_ref[0, rs, h:h + 1])
                dsb = ds.astype(BF16)
                dq_ref[rs, cs] += _bdot_raw(dsb, kj, _NN)
                dfc_ref[0, rs, h:h + 1] += jnp.sum(ds, axis=1, keepdims=True)
                return (dk + _bdot_raw(dsb, qi, _TN), dv + _bdot_raw(p, doi, _TN),
                        dfs - jnp.sum(ds, axis=0, keepdims=True))

            init = (jnp.zeros((blk, HEAD), F32), jnp.zeros((blk, HEAD), F32), jnp.zeros((1, blk), F32))
            carry = tile(j, init, True)
            dk, dv, dfs = lax.fori_loop(j + 1, nq, lambda i, c: tile(i, c, False), carry)
            dk_ref[:, cs] = dk
            dv_ref[:, cs] = dv.astype(dv_ref.dtype)
            dfr_ref[0, 0, h:h + 1, :] = dfs

    by_group = lambda a: a.reshape(t, ATTN_BWD_GROUPS, hpg).transpose(1, 0, 2)
    once = pl.Buffered(1)
    stat = pl.BlockSpec((1, t, hpg), lambda g, j: (g, 0, 0), pipeline_mode=once)
    kv_blk = pl.BlockSpec((blk, gw), lambda g, j: (j, g))
    frow = pl.BlockSpec((1, 1, hpg, blk), lambda g, j: (g, j, 0, 0))
    dq, dk, dv, dfc, dfr = pl.pallas_call(
        body, name="fox_attn_bwd",
        grid=(ATTN_BWD_GROUPS, nq),
        in_specs=[pl.BlockSpec((t, 2 * gw), lambda g, j: (0, g), pipeline_mode=once),
                  pl.BlockSpec((t, gw), lambda g, j: (0, g), pipeline_mode=once),
                  pl.BlockSpec((blk, 2 * gw), lambda g, j: (j, g)), kv_blk, stat, stat, stat],
        out_specs=[pl.BlockSpec((t, gw), lambda g, j: (0, g)), kv_blk, kv_blk,
                   pl.BlockSpec((1, t, hpg), lambda g, j: (g, 0, 0)), frow],
        out_shape=[jax.ShapeDtypeStruct((t, width), F32), jax.ShapeDtypeStruct((t, width), F32),
                   jax.ShapeDtypeStruct((t, width), BF16), jax.ShapeDtypeStruct((ATTN_BWD_GROUPS, t, hpg), F32),
                   jax.ShapeDtypeStruct((ATTN_BWD_GROUPS, nq, hpg, blk), F32)],
        compiler_params=_cparams(dimension_semantics=("parallel", "arbitrary")),
    )(q, do, k, v, by_group(f_col), by_group(lse), by_group(delta))
    return (dq, dk, dv, dfc.transpose(1, 0, 2).reshape(t, nh),
            dfr.transpose(1, 0, 2, 3).reshape(nq, nh, blk))


SUBLANES = 8


def _shift_down(u, n):
    r = pltpu.roll(u, n, 0)
    row = lax.broadcasted_iota(jnp.int32, (SUBLANES, u.shape[1]), 0)
    return jnp.concatenate([jnp.where(row < n, 0.0, r[:SUBLANES]), r[SUBLANES:]], axis=0)


def _shift_up(u, n):
    t = u.shape[0]
    r = pltpu.roll(u, t - n, 0)
    row = lax.broadcasted_iota(jnp.int32, (SUBLANES, u.shape[1]), 0)
    return jnp.concatenate([r[:t - SUBLANES], jnp.where(row >= SUBLANES - n, 0.0, r[t - SUBLANES:])], axis=0)


def _convglu_specs(t):
    return [pl.BlockSpec((2, t, LANES), lambda j: (0, 0, j)),
            pl.BlockSpec((2, CONV_TAPS, LANES), lambda j: (0, 0, j)),
            pl.BlockSpec((2, 1, LANES), lambda j: (0, 0, j))]


def _convglu_fwd(u, cw, cb):
    _, t, fp = u.shape

    def body(u_ref, w_ref, b_ref, a_ref):
        c = []
        for hf in range(2):
            uv, w = u_ref[hf], w_ref[hf]
            c.append(w[0:1] * _shift_down(uv, 2) + w[1:2] * _shift_down(uv, 1) + w[2:3] * uv + b_ref[hf])
        a_ref[...] = (_silu(c[0]) * c[1]).astype(a_ref.dtype)

    return pl.pallas_call(
        body, name="convglu_fwd",
        grid=(fp // LANES,),
        in_specs=_convglu_specs(t),
        out_specs=pl.BlockSpec((t, LANES), lambda j: (0, j)),
        out_shape=jax.ShapeDtypeStruct((t, fp), BF16),
        compiler_params=_cparams(dimension_semantics=("parallel",)),
    )(u, cw, cb)


def _convglu_bwd(u, cw, cb, da):
    _, t, fp = u.shape

    def body(u_ref, w_ref, b_ref, da_ref, du_ref, dw_ref, db_ref):
        us, c = [], []
        for hf in range(2):
            uv, w = u_ref[hf], w_ref[hf]
            u1, u2 = _shift_down(uv, 1), _shift_down(uv, 2)
            us.append((uv, u1, u2))
            c.append(w[0:1] * u2 + w[1:2] * u1 + w[2:3] * uv + b_ref[hf])
        gc, vc = c
        sg = _sigmoid(gc)
        dav = da_ref[...].astype(F32)
        dcs = [dav * vc * (sg * (1.0 + gc * (1.0 - sg))), dav * (gc * sg)]
        for hf in range(2):
            dc, w = dcs[hf], w_ref[hf]
            uv, u1, u2 = us[hf]
            du = w[2:3] * dc + w[1:2] * _shift_up(dc, 1) + w[0:1] * _shift_up(dc, 2)
            du_ref[hf] = du.astype(du_ref.dtype)
            dw_ref[hf, 0:1, :] = jnp.sum(dc * u2, axis=0, keepdims=True)
            dw_ref[hf, 1:2, :] = jnp.sum(dc * u1, axis=0, keepdims=True)
            dw_ref[hf, 2:3, :] = jnp.sum(dc * uv, axis=0, keepdims=True)
            db_ref[hf] = jnp.sum(dc, axis=0, keepdims=True)

    specs = _convglu_specs(t)
    return pl.pallas_call(
        body, name="convglu_bwd",
        grid=(fp // LANES,),
        in_specs=specs + [pl.BlockSpec((t, LANES), lambda j: (0, j))],
        out_specs=specs,
        out_shape=[jax.ShapeDtypeStruct((2, t, fp), BF16), jax.ShapeDtypeStruct((2, CONV_TAPS, fp), F32),
                   jax.ShapeDtypeStruct((2, 1, fp), F32)],
        compiler_params=_cparams(dimension_semantics=("parallel",)),
    )(u, cw, cb, da)


def _local_step(x, target, mods, lb, small, get_w, put_g, *, tb=512, attn_blk=512):
    t, d = x.shape
    nh = d // HEAD
    nb = NDEV
    wts = {}
    vec = lambda *names: [mods[n] for n in names]

    def ffn_fwd(h2, l):
        u = _mm_wblk(h2, wts[f"up{l}"], F32, f"ffn{l}_up", gb=nb // 2, split=2, tm=512)
        a = _convglu_fwd(u, small[f"conv_w{l}"], small[f"conv_b{l}"])
        f = _mm(a, wts[f"down{l}"], "nn", F32, f"ffn{l}_down", tk=4096)
        return u, a, f

    def ffn_bwd(df, h2, u, a, l):
        da = _mm(df, wts[f"down{l}"], "nt", BF16, f"ffn{l}_down_dx", tn=1536)
        dwd = _mm(a, df, "tn", BF16, f"ffn{l}_down_dw", tm=1536, tk=1024)
        du, dcw, dcb = _convglu_bwd(u, small[f"conv_w{l}"], small[f"conv_b{l}"], da)
        dh2 = _mm_wblk_dx(du, wts[f"up{l}"], F32, f"ffn{l}_up_dx", k=d, gb=nb // 2, split=2, tm=1024)
        dwu = _mm_wblk_dw(h2, du, f"ffn{l}_up_dw", nb=nb, gb=1, split=2, tk=t)
        return dh2, dwu, dwd, dcw, dcb

    (h_a,) = _row_fwd(_f_mod, [(x, d, 0)], vec("sh1_0", "sc1_0"), [BF16], tb=tb, name="l0_mod1")
    wts.update(get_w("l0a", h_a))
    proj_a = _mm_wblk(h_a, wts["a_in"], F32, "a_in", gb=nb // 2)
    ypre, states = _hgrn2_fwd(proj_a, lb, small["a_norm_g"], tb)
    wts.update(get_w("l0b", ypre))
    y_a = _mm(ypre, wts["a_out"], "nn", F32, "a_out")
    x1, h2_0 = _row_fwd(_f_res_mod, [(x, d, 0), (y_a, d, 0)], vec("g1_0", "sh2_0", "sc2_0"), [F32, BF16],
                        tb=tb, name="l0_res_mod2")
    u0, a0, f0 = ffn_fwd(h2_0, 0)
    x2, h_kv, h_q = _row_fwd(_f_res_mod2, [(x1, d, 0), (f0, d, 0)],
                             vec("g2_0", "kv_sh", "kv_sc", "sh1_1", "sc1_1"), [F32, BF16, BF16],
                             tb=tb, name="l0_res_kvmod_qmod")
    wts.update(get_w("l1", h_kv))
    proj_kv = _mm(h_kv, wts["kv"], "nn", F32, "kv_proj")
    proj_f = _mm(h_kv, wts["kv_f"], "nn", F32, "kv_fproj")
    v_b = proj_kv[:, d:].astype(BF16)
    f_logit_t = proj_f[:, :nh].T
    f_bias = small["kv_b_f"].reshape(nh, 1)
    f_col = _fgate_fwd(f_logit_t, f_bias).T
    (k_n,) = _row_fwd(_f_knorm_aug, [(proj_kv, HEAD, 0)] + [(piece, 1, 0) for piece in _split3(-f_col)],
                      [small["k_norm_g"]], [BF16], nsub=nh, tb=tb, name="k_norm")
    proj_q = _mm_wblk(h_q, wts["b_q"], F32, "b_q", gb=nb)
    (q_n,) = _row_fwd(_f_qnorm_aug, [(proj_q, HEAD, 0)], [small["q_norm_g"]], [BF16], nsub=nh, tb=tb,
                      name="q_norm")
    o_att, lse = _attn_fwd(q_n, k_n, v_b, f_col, attn_blk)
    (z,) = _row_fwd(_f_outgate, [(o_att, HEAD, 0), (proj_q, HEAD, 1)], [], [BF16], nsub=nh, tb=tb, name="out_gate")
    y_b = _mm(z, wts["b_out"], "nn", F32, "b_out")
    x3, h2_1 = _row_fwd(_f_res_mod, [(x2, d, 0), (y_b, d, 0)], vec("g1_1", "sh2_1", "sc2_1"), [F32, BF16],
                        tb=tb, name="l1_res_mod2")
    u1, a1, f1 = ffn_fwd(h2_1, 1)
    loss, dx4, df1, dg2_1 = _loss_call(x3, f1, mods["g2_1"], target, tb)

    g = {}
    dmods = {"g2_1": dg2_1}
    dh2, g["up1"], g["down1"], g["conv_w1"], g["conv_b1"] = ffn_bwd(df1, h2_1, u1, a1, 1)
    (dx2, dy_b), (dmods["g1_1"], dmods["sh2_1"], dmods["sc2_1"]) = _row_bwd(
        _f_res_mod, [(x2, d, 0), (y_b, d, 0)], vec("g1_1", "sh2_1", "sc2_1"),
        [(dx4, d, 0), (dh2, d, 0)], [F32, BF16], tb=tb, name="l1_res_mod2_bwd")
    dz = _mm(dy_b, wts["b_out"], "nt", F32, "b_out_dx")
    g["b_out"] = _mm(z, dy_b, "tn", BF16, "b_out_dw", tk=1024)
    (do_att, dog), _ = _row_bwd(_f_outgate, [(o_att, HEAD, 0), (proj_q, HEAD, 1)], [], [(dz, HEAD, 0)],
                                [BF16, BF16], nsub=nh, tb=tb, name="out_gate_bwd")
    delta = _attn_delta(do_att, o_att, tb)
    dq_n, dk_n, dv, dfc_q, dfr_k = _attn_bwd(q_n, k_n, v_b, f_col, do_att, lse, delta, attn_blk)
    (dpq,), (g["q_norm_g"],) = _row_bwd(_f_qnorm, [(proj_q, HEAD, 0)], [small["q_norm_g"]],
                                        [(dq_n, HEAD, 0)], [BF16], nsub=nh, tb=tb, name="q_norm_bwd")
    dproj_q = jnp.concatenate([dpq, dog], axis=1)
    dh_q = _mm_wblk_dx(dproj_q, wts["b_q"], F32, "b_q_dx", k=d, gb=nb)
    g["b_q"] = _mm_wblk_dw(h_q, dproj_q, "b_q_dw", nb=nb, gb=nb // 4, tk=t)
    (dpk,), (g["k_norm_g"],) = _row_bwd(_f_knorm, [(proj_kv, HEAD, 0)], [small["k_norm_g"]],
                                        [(dk_n, HEAD, 0)], [BF16], nsub=nh, tb=tb, name="k_norm_bwd")
    dproj_kv = jnp.concatenate([dpk, dv], axis=1)
    df_t = dfc_q.T + dfr_k.transpose(1, 0, 2).reshape(nh, t)
    dflogit_t, g["kv_b_f"] = _fgate_bwd(f_logit_t, f_bias, df_t)
    dproj_f = jnp.pad(dflogit_t.T, ((0, 0), (0, LANES - nh))).astype(BF16)
    dh_kv = _mm(dproj_kv, wts["kv"], "nt", F32, "kv_proj_dx")
    dh_kv_f = _mm(dproj_f, wts["kv_f"], "nt", F32, "kv_fproj_dx")
    g["kv"] = _mm(h_kv, dproj_kv, "tn", BF16, "kv_proj_dw", tk=1024)
    g["kv_f"] = _mm(h_kv, dproj_f, "tn", F32, "kv_fproj_dw", tk=1024)
    sent = put_g("l1", {n: g.pop(n) for n in ("b_out", "b_q", "kv", "kv_f", "up1", "down1")})
    (dx1, df0), (dmods["g2_0"], dmods["kv_sh"], dmods["kv_sc"], dmods["sh1_1"], dmods["sc1_1"]) = _row_bwd(
        _f_res_mod2, [(x1, d, 0), (f0, d, 0)], [mods["g2_0"] + sent] + vec("kv_sh", "kv_sc", "sh1_1", "sc1_1"),
        [(dx2, d, 0), (dh_kv, d, 0), (dh_q, d, 0)], [F32, BF16], tb=tb, name="l0_res_kvmod_qmod_bwd",
        cot_add=(1, dh_kv_f))
    dh2, g["up0"], g["down0"], g["conv_w0"], g["conv_b0"] = ffn_bwd(df0, h2_0, u0, a0, 0)
    (dx0, dy_a), (dmods["g1_0"], dmods["sh2_0"], dmods["sc2_0"]) = _row_bwd(
        _f_res_mod, [(x, d, 0), (y_a, d, 0)], vec("g1_0", "sh2_0", "sc2_0"),
        [(dx1, d, 0), (dh2, d, 0)], [F32, BF16], tb=tb, name="l0_res_mod2_bwd")
    dypre = _mm(dy_a, wts["a_out"], "nt", BF16, "a_out_dx")
    g["a_out"] = _mm(ypre, dy_a, "tn", BF16, "a_out_dw", tk=1024)
    sent = put_g("l0b", {n: g.pop(n) for n in ("a_out", "up0", "down0")})
    dproj_a, dlb, g["a_norm_g"] = _hgrn2_bwd(proj_a, lb + sent, small["a_norm_g"], states, dypre, tb)
    dh_a = _mm_wblk_dx(dproj_a, wts["a_in"], F32, "a_in_dx", k=d, gb=nb, split=4, tm=512)
    put_g("l0a", {"a_in": _mm_wblk_dw(h_a, dproj_a, "a_in_dw", nb=nb, gb=1, split=4, tk=t)})
    (grad_x,), (dmods["sh1_0"], dmods["sc1_0"]) = _row_bwd(
        _f_mod, [(x, d, 0)], vec("sh1_0", "sc1_0"), [(dh_a, d, 0)], [F32], tb=tb, name="l0_mod1_bwd",
        add_to=(0, dx0))
    return loss, grad_x, dmods, dlb, g


def _position():
    return lax.axis_index("x"), lax.axis_index("y"), lax.axis_index("c")


def _hbm_specs(n):
    return [pl.BlockSpec(memory_space=pl.ANY)] * n


def _all_gather(arrs, name):
    n = len(arrs)

    def body(*refs):
        x_refs, out_refs = refs[:n], refs[n:2 * n]
        send_sems, recv_sems, local_sems = refs[2 * n:]
        x, y, cc = _position()
        me, sibling = (x, y, cc), (x, y, 1 - cc)
        chips = [(1 - x, y), (x, 1 - y), (1 - x, 1 - y)]

        def copy(a, k, block, to, src=None):
            slot = out_refs[a].at[4 * block[0] + 2 * block[1] + block[2]]
            return pltpu.make_async_remote_copy(
                src_ref=slot if src is None else src, dst_ref=slot,
                send_sem=send_sems.at[7 * a + k], recv_sem=recv_sems.at[7 * a + k],
                device_id=to, device_id_type=_MESH)

        local = [pltpu.make_async_copy(x_refs[a], out_refs[a].at[4 * x + 2 * y + cc], local_sems.at[a])
                 for a in range(n)]
        for cp in local:
            cp.start()
        first = []
        for a in range(n):
            first.append(copy(a, 0, me, sibling, src=x_refs[a]))
            first += [copy(a, 1 + j, me, (*chip, cc), src=x_refs[a]) for j, chip in enumerate(chips)]
        for cp in first:
            cp.start()
        passed = []
        for j, chip in enumerate(chips):
            for a in range(n):
                copy(a, 1 + j, (*chip, cc), me).wait_recv()
                fwd = copy(a, 4 + j, (*chip, cc), sibling)
                fwd.start()
                passed.append(fwd)
        for a in range(n):
            copy(a, 0, sibling, me).wait_recv()
        for j, chip in enumerate(chips):
            for a in range(n):
                copy(a, 4 + j, (*chip, 1 - cc), me).wait_recv()
        for cp in first + passed:
            cp.wait_send()
        for cp in local:
            cp.wait()

    return pl.pallas_call(
        body, name=name,
        out_shape=[jax.ShapeDtypeStruct((NDEV, *a.shape), a.dtype) for a in arrs],
        in_specs=_hbm_specs(n), out_specs=_hbm_specs(n),
        scratch_shapes=[pltpu.SemaphoreType.DMA((7 * n,)), pltpu.SemaphoreType.DMA((7 * n,)),
                        pltpu.SemaphoreType.DMA((n,))],
    )(*arrs)


_XCHG_EFFECT = pltpu.SideEffectType.DATAFLOW_SIDE_EFFECTING
ALL_PEERS = (1, 2, 3, 4, 5, 6, 7)
SAME_CORE = (2, 4, 6)


def _xchg_copies(src_refs, land_refs, send_sems, recv_sems, local_sems, scatter, rels):
    x, y, cc = _position()
    me = 4 * x + 2 * y + cc
    remote, local = [], []
    for a, (src, land) in enumerate(zip(src_refs, land_refs)):
        local.append(pltpu.make_async_copy(src.at[me] if scatter else src, land.at[me], local_sems.at[a]))
        for idx, rel in enumerate(rels):
            px = 1 - x if rel & 4 else x
            py = 1 - y if rel & 2 else y
            pc = 1 - cc if rel & 1 else cc
            k = len(rels) * a + idx
            remote.append(pltpu.make_async_remote_copy(
                src_ref=src.at[4 * px + 2 * py + pc] if scatter else src, dst_ref=land.at[me],
                send_sem=send_sems.at[k], recv_sem=recv_sems.at[k], device_id=(px, py, pc), device_id_type=_MESH))
    return remote, local


def _xchg_start(srcs, scatter, rels, after, name):
    n = len(srcs)
    lands = [lax.empty(s.shape if scatter else (NDEV, *s.shape), s.dtype) for s in srcs]

    def body(*refs):
        remote, local = _xchg_copies(refs[:n], refs[n:2 * n], *refs[2 * n + 1:2 * n + 4], scatter, rels)
        for cp in local + remote:
            cp.start()
        token = refs[-1]
        token[...] = jnp.zeros_like(token)

    hbm = pl.BlockSpec(memory_space=pltpu.HBM)
    sem = pl.BlockSpec(memory_space=pltpu.SEMAPHORE)
    out = pl.pallas_call(
        body, name=name,
        out_shape=(pltpu.SemaphoreType.DMA((len(rels) * n,)), pltpu.SemaphoreType.DMA((len(rels) * n,)),
                   pltpu.SemaphoreType.DMA((n,)),
                   *[pltpu.HBM(a.shape, a.dtype) for a in srcs + lands], jax.ShapeDtypeStruct((8, LANES), F32)),
        in_specs=[hbm] * (2 * n) + [pl.BlockSpec(memory_space=pl.ANY)],
        out_specs=(sem, sem, sem, *[hbm] * (2 * n), pl.BlockSpec(memory_space=pltpu.VMEM)),
        input_output_aliases={i: 3 + i for i in range(2 * n)},
        compiler_params=pltpu.CompilerParams(has_side_effects=_XCHG_EFFECT),
    )(*[pltpu.with_memory_space_constraint(a, pltpu.HBM) for a in srcs + lands], after)
    return out[:-1], out[-1][0, 0]


def _xchg_wait(handles, after, scatter, rels, name):
    n = (len(handles) - 3) // 2

    def body(*refs):
        remote, local = _xchg_copies(refs[:n], refs[n:2 * n], *refs[2 * n:2 * n + 3], scatter, rels)
        for cp in remote:
            cp.wait_send()
            cp.wait_recv()
        for cp in local:
            cp.wait()

    hbm = pl.BlockSpec(memory_space=pltpu.HBM)
    sem = pl.BlockSpec(memory_space=pltpu.SEMAPHORE)
    thru = list(handles[3:])
    out = pl.pallas_call(
        body, name=name,
        out_shape=tuple(pltpu.HBM(a.shape, a.dtype) for a in thru),
        in_specs=[hbm] * (2 * n) + [sem, sem, sem, pl.BlockSpec(memory_space=pl.ANY)],
        out_specs=tuple([hbm] * (2 * n)),
        input_output_aliases={i: i for i in range(2 * n)},
        compiler_params=pltpu.CompilerParams(has_side_effects=_XCHG_EFFECT),
    )(*thru, *handles[:3], after)
    return list(out[n:])


def _sibling_forward(lands, name):
    n = len(lands)

    def body(*refs):
        land_refs = refs[n:2 * n]
        send_sems, recv_sems = refs[2 * n:]
        x, y, cc = _position()

        def copy(a, q, core):
            slot = land_refs[a].at[2 * q + core]
            return pltpu.make_async_remote_copy(
                src_ref=slot, dst_ref=slot, send_sem=send_sems.at[NCHIP * a + q], recv_sem=recv_sems.at[NCHIP * a + q],
                device_id=(x, y, 1 - cc), device_id_type=_MESH)

        sends = [copy(a, q, cc) for a in range(n) for q in range(NCHIP)]
        for cp in sends:
            cp.start()
        for a in range(n):
            for q in range(NCHIP):
                copy(a, q, 1 - cc).wait_recv()
        for cp in sends:
            cp.wait_send()

    return pl.pallas_call(
        body, name=name,
        out_shape=[jax.ShapeDtypeStruct(a.shape, a.dtype) for a in lands],
        in_specs=_hbm_specs(n), out_specs=_hbm_specs(n),
        input_output_aliases={i: i for i in range(n)},
        scratch_shapes=[pltpu.SemaphoreType.DMA((NCHIP * n,)), pltpu.SemaphoreType.DMA((NCHIP * n,))],
    )(*lands)


def _slab_sum(slabs, name, tr=None):
    n, r, c = slabs.shape
    tr = r if tr is None else tr

    def body(s_ref, o_ref):
        acc = s_ref[0].astype(F32)
        for q in range(1, n):
            acc = acc + s_ref[q].astype(F32)
        o_ref[...] = acc

    return pl.pallas_call(body, name=name, grid=(r // tr,),
                          in_specs=[pl.BlockSpec((n, tr, c), lambda i: (0, i, 0))],
                          out_specs=pl.BlockSpec((tr, c), lambda i: (i, 0)),
                          out_shape=jax.ShapeDtypeStruct((r, c), F32),
                          compiler_params=_cparams(dimension_semantics=("parallel",)))(slabs)


def _ada_fwd(c_all, ada_w, kv_ada_w, logits):
    rows, d = c_all.shape
    n0, nkv = ada_w.shape[2], kv_ada_w.shape[1]

    def body(c_ref, w_ref, kw_ref, lg_ref, part_ref, cact_ref, lb_ref):
        ca = _silu(c_ref[...])
        cact_ref[...] = ca
        part_ref[:, 0:n0] = _bdot_raw(ca, w_ref[0], _NN)
        part_ref[:, n0:2 * n0] = _bdot_raw(ca, w_ref[1], _NN)
        part_ref[:, 2 * n0:2 * n0 + nkv] = _bdot_raw(ca, kw_ref[...], _NN)
        lb_ref[...] = _sigmoid(lg_ref[0:1, :] - lg_ref[1:2, :])

    vm = pl.BlockSpec(memory_space=pltpu.VMEM)
    return pl.pallas_call(
        body, name="ada_fwd", in_specs=[vm, vm, vm, vm], out_specs=[vm, vm, vm],
        out_shape=[jax.ShapeDtypeStruct((rows, 2 * n0 + nkv), F32), jax.ShapeDtypeStruct((rows, d), F32),
                   jax.ShapeDtypeStruct((1, d), F32)],
        compiler_params=_cparams(),
    )(c_all, ada_w, kv_ada_w, logits)


def _ada_bwd(c_act, dm0, dm1, dkv, lb, dlb):
    rows, d = c_act.shape

    def body(c_ref, d0_ref, d1_ref, dk_ref, lb_ref, dlb_ref, dw_ref, dkw_ref, dlg_ref):
        ca = c_ref[...]
        dw_ref[0] = _bdot_raw(ca, d0_ref[...], _TN)
        dw_ref[1] = _bdot_raw(ca, d1_ref[...], _TN)
        dkw_ref[...] = _bdot_raw(ca, dk_ref[...], _TN)
        lbv = lb_ref[...]
        dl0 = dlb_ref[...] * lbv * (1.0 - lbv)
        dlg_ref[0:1, :] = dl0
        dlg_ref[1:2, :] = -dl0

    vm = pl.BlockSpec(memory_space=pltpu.VMEM)
    return pl.pallas_call(
        body, name="ada_bwd", in_specs=[vm] * 6, out_specs=[vm, vm, vm],
        out_shape=[jax.ShapeDtypeStruct((2, d, dm0.shape[1]), F32), jax.ShapeDtypeStruct((d, dkv.shape[1]), F32),
                   jax.ShapeDtypeStruct((2, d), F32)],
        compiler_params=_cparams(),
    )(c_act, dm0, dm1, dkv, lb, dlb)


def _adamw(w, g, m, v, name, tr=512, after=None):
    r, c = w.shape
    tr = _divisor_tile(r, tr, unit=8)
    c1 = 1.0 - ADAM_B1 ** ADAM_STEP
    c2 = 1.0 - ADAM_B2 ** ADAM_STEP
    deps = [] if after is None else [after]

    def body(w_ref, g_ref, m_ref, v_ref, *rest):
        d_ref, mo_ref, vo_ref = rest[len(deps):]
        gv = g_ref[...]
        mn = ADAM_B1 * m_ref[...] + (1.0 - ADAM_B1) * gv
        vn = ADAM_B2 * v_ref[...] + (1.0 - ADAM_B2) * (gv * gv)
        d_ref[...] = -ADAM_LR * ((mn / c1) / (jnp.sqrt(vn / c2) + ADAM_EPS) + ADAM_WD * w_ref[...])
        mo_ref[...] = mn
        vo_ref[...] = vn

    spec = pl.BlockSpec((tr, c), lambda i: (i, 0))
    out = jax.ShapeDtypeStruct((r, c), F32)
    return pl.pallas_call(body, name=name, grid=(r // tr,),
                          in_specs=[spec] * 4 + [pl.BlockSpec(a.shape, lambda i: (0, 0)) for a in deps],
                          out_specs=[spec] * 3, out_shape=[out, out, out],
                          compiler_params=_cparams(dimension_semantics=("parallel",)))(w, g, m, v, *deps)


def _pad_rows(a, rows):
    return jnp.pad(a, ((0, rows - a.shape[0]), (0, 0)))


def _pack_small(parts, lanes=LANES, row_unit=8):
    flat = jnp.concatenate([p.reshape(-1).astype(F32) for p in parts])
    rows = _round_up(-(-flat.shape[0] // lanes), row_unit)
    return jnp.pad(flat, (0, rows * lanes - flat.shape[0])).reshape(rows, lanes)


def _unpack_small(flat, shapes):
    out, off = [], 0
    for s in shapes:
        n = 1
        for k in s:
            n *= k
        out.append(flat[off:off + n].reshape(s))
        off += n
    return out


def _pad_shard_cols(a, n_loc, n_pad):
    lead = a.shape[:-1]
    a = a.reshape(*lead, NDEV, n_loc)
    a = jnp.pad(a, [(0, 0)] * (len(lead) + 1) + [(0, n_pad - n_loc)])
    return a.reshape(*lead, NDEV * n_pad)


def _unpad_shard_cols(a, n_loc, n_pad):
    lead = a.shape[:-1]
    return a.reshape(*lead, NDEV, n_pad)[..., :n_loc].reshape(*lead, NDEV * n_loc)


def kernel(x, c, ada_w, ada_b, a_w_in, a_lb_logits, a_norm_g, a_w_out, kv_ada_w, kv_ada_b, kv_w, kv_b_f, k_norm_g, b_w_q, q_norm_g, b_w_out, ffn_w_up, ffn_conv_w, ffn_conv_b, ffn_w_down, loss_target, m_ada_w, m_ada_b, m_a_w_in, m_a_lb_logits, m_a_norm_g, m_a_w_out, m_kv_ada_w, m_kv_ada_b, m_kv_w, m_kv_b_f, m_k_norm_g, m_b_w_q, m_q_norm_g, m_b_w_out, m_ffn_w_up, m_ffn_conv_w, m_ffn_conv_b, m_ffn_w_down, v_ada_w, v_ada_b, v_a_w_in, v_a_lb_logits, v_a_norm_g, v_a_w_out, v_kv_ada_w, v_kv_ada_b, v_kv_w, v_kv_b_f, v_k_norm_g, v_b_w_q, v_q_norm_g, v_b_w_out, v_ffn_w_up, v_ffn_conv_w, v_ffn_conv_b, v_ffn_w_down):
    t, d = x.shape[1], x.shape[2]
    nh = d // HEAD
    ncw = ffn_w_up.shape[2]
    ncp = _round_up(ncw, LANES)
    two_f = ncw * NDEV
    ff = two_f // 2
    fp = ncp * NDEV // 2
    rd = ffn_w_down.shape[1]
    me = 4 * lax.axis_index("x") + 2 * lax.axis_index("y") + lax.axis_index("c")
    weights = dict(ada_w=ada_w, ada_b=ada_b, a_w_in=a_w_in, a_lb_logits=a_lb_logits, a_norm_g=a_norm_g,
                   a_w_out=a_w_out, kv_ada_w=kv_ada_w, kv_ada_b=kv_ada_b, kv_w=kv_w, kv_b_f=kv_b_f,
                   k_norm_g=k_norm_g, b_w_q=b_w_q, q_norm_g=q_norm_g, b_w_out=b_w_out, ffn_w_up=ffn_w_up,
                   ffn_conv_w=ffn_conv_w, ffn_conv_b=ffn_conv_b, ffn_w_down=ffn_w_down)
    m_in = dict(ada_w=m_ada_w, ada_b=m_ada_b, a_w_in=m_a_w_in, a_lb_logits=m_a_lb_logits, a_norm_g=m_a_norm_g,
                a_w_out=m_a_w_out, kv_ada_w=m_kv_ada_w, kv_ada_b=m_kv_ada_b, kv_w=m_kv_w, kv_b_f=m_kv_b_f,
                k_norm_g=m_k_norm_g, b_w_q=m_b_w_q, q_norm_g=m_q_norm_g, b_w_out=m_b_w_out, ffn_w_up=m_ffn_w_up,
                ffn_conv_w=m_ffn_conv_w, ffn_conv_b=m_ffn_conv_b, ffn_w_down=m_ffn_w_down)
    v_in = dict(ada_w=v_ada_w, ada_b=v_ada_b, a_w_in=v_a_w_in, a_lb_logits=v_a_lb_logits, a_norm_g=v_a_norm_g,
                a_w_out=v_a_w_out, kv_ada_w=v_kv_ada_w, kv_ada_b=v_kv_ada_b, kv_w=v_kv_w, kv_b_f=v_kv_b_f,
                k_norm_g=v_k_norm_g, b_w_q=v_b_w_q, q_norm_g=v_q_norm_g, b_w_out=v_b_w_out, ffn_w_up=v_ffn_w_up,
                ffn_conv_w=v_ffn_conv_w, ffn_conv_b=v_ffn_conv_b, ffn_w_down=v_ffn_w_down)
    order = list(weights)

    up_loc = jnp.pad(ffn_w_up, ((0, 0), (0, 0), (0, ncp - ncw))).astype(BF16)
    down_loc = ffn_w_down.astype(BF16)
    gather_names = {"l0b": ["a_out", "up0", "down0"], "l1": ["kv", "b_q", "b_out", "up1", "down1"]}
    shards = {"a_out": a_w_out[0].astype(BF16), "up0": up_loc[0], "down0": down_loc[0], "kv": kv_w.astype(BF16),
              "b_q": b_w_q[0].astype(BF16), "b_out": b_w_out[0].astype(BF16), "up1": up_loc[1],
              "down1": down_loc[1]}
    pre = _pack_small([c, a_lb_logits, ffn_conv_w])
    a_in_all, pre_all = _all_gather([a_w_in[0].astype(BF16), pre], "gather_a_w_in_and_small_inputs")
    pre_all = pre_all.reshape(NDEV, -1)
    c_all = pre_all[:, :d]
    logits = pre_all[:, d:d + 2 * HEAD].reshape(NDEV, 2, HEAD).transpose(1, 0, 2).reshape(2, d)
    conv_w_full = pre_all[:, d + 2 * HEAD:d + 2 * HEAD + 2 * CONV_TAPS * ncw]
    conv_w_full = conv_w_full.reshape(NDEV, 2, CONV_TAPS, ncw).transpose(1, 2, 0, 3).reshape(2, CONV_TAPS, two_f)

    part, c_act, lb = _ada_fwd(_pad_rows(c_all, 2 * NDEV), ada_w, kv_ada_w, logits)
    (part_all,) = _all_gather([part[:NDEV]], "gather_adaln")
    mine = lax.dynamic_index_in_dim(part_all, me, axis=1, keepdims=False)
    n0, nkv = ada_w.shape[2], kv_ada_w.shape[1]
    mod_names = ["sh1", "sc1", "g1", "sh2", "sc2", "g2"]
    mods = {}
    for l in range(2):
        row = mine[:, l * n0:(l + 1) * n0].reshape(-1) + ada_b[l]
        for k, nm in enumerate(mod_names):
            mods[f"{nm}_{l}"] = row[k * d:(k + 1) * d].reshape(1, d)
    kvrow = mine[:, 2 * n0:2 * n0 + nkv].reshape(-1) + kv_ada_b
    mods["kv_sh"], mods["kv_sc"] = kvrow[:d].reshape(1, d), kvrow[d:].reshape(1, d)

    in_flight = {}

    def start_gather(grp, dep):
        srcs = [shards[n] for n in gather_names[grp]]
        in_flight[grp], started = _xchg_start(srcs, False, SAME_CORE, dep, f"gather_{grp}_start")
        return started

    zero = start_gather("l0b", part_all)
    mods["sh1_0"] = mods["sh1_0"] + zero

    small = {"a_norm_g": a_norm_g, "k_norm_g": k_norm_g.reshape(1, HEAD), "q_norm_g": q_norm_g, "kv_b_f": kv_b_f}
    for l in range(2):
        small[f"conv_w{l}"] = _pad_shard_cols(conv_w_full[l], ncw, ncp).reshape(CONV_TAPS, 2, fp).transpose(1, 0, 2)
        small[f"conv_b{l}"] = _pad_shard_cols(ffn_conv_b[l], ncw, ncp).reshape(2, 1, fp)

    def get_w(grp, after):
        if grp == "l0a":
            return {"a_in": a_in_all}
        arrived = _xchg_wait(in_flight[grp], after, False, SAME_CORE, f"gather_{grp}_wait")
        full = list(_sibling_forward(arrived, f"gather_{grp}_to_sibling"))
        if grp == "l0b":
            started = start_gather("l1", full[0])
            full[0] = full[0] + started.astype(full[0].dtype)
        got = dict(zip(gather_names[grp], full))
        out = {}
        for n, a in got.items():
            if n in ("a_out", "b_out"):
                out[n] = a.reshape(d, d)
            elif n in ("down0", "down1"):
                dn = a.reshape(NCHIP, ff // NCHIP, d)
                out[n] = jnp.pad(dn, ((0, 0), (0, ncp - ncw), (0, 0))).reshape(fp, d)
            elif n == "kv":
                kv_full = a.transpose(1, 0, 2).reshape(d, NDEV * kv_w.shape[1])
                out["kv"] = kv_full[:, :2 * d]
                out["kv_f"] = jnp.pad(kv_full[:, 2 * d:], ((0, 0), (0, LANES - nh)))
            else:
                out[n] = a
        return out

    scatter_flight, g_last = {}, {}

    def put_g(grp, gr):
        if grp == "l0a":
            g_last.update(gr)
            return zero
        if grp == "l1":
            g_kvw = jnp.concatenate([gr["kv"], gr["kv_f"][:, :nh].astype(BF16)], axis=1)
            arrs = {"kv_w": g_kvw.reshape(d, NDEV, kv_w.shape[1]).transpose(1, 0, 2), "b_w_q": gr["b_q"],
                    "b_w_out": gr["b_out"].reshape(NDEV, d // NDEV, d), "up1": gr["up1"],
                    "down1": gr["down1"].reshape(NCHIP, ncp, d)[:, :ncw].reshape(NDEV, rd, d)}
        else:
            arrs = {"a_w_out": gr["a_out"].reshape(NDEV, d // NDEV, d), "up0": gr["up0"],
                    "down0": gr["down0"].reshape(NCHIP, ncp, d)[:, :ncw].reshape(NDEV, rd, d)}
        srcs = list(arrs.values())
        handles, sent = _xchg_start(srcs, True, ALL_PEERS, srcs[0], f"scatter_{grp}_start")
        scatter_flight[grp] = (list(arrs), handles)
        return sent

    loss_v, grad_x, dmods, dlb, g = _local_step(x[0], loss_target[0], mods, lb, small, get_w, put_g)
    loss = lax.psum(loss_v[0, 0], ("x", "y", "c"))

    g_sum = {}
    for grp in ("l1", "l0b"):
        names, handles = scatter_flight[grp]
        for nm, a in zip(names, _xchg_wait(handles, grad_x, True, ALL_PEERS, f"scatter_{grp}_wait")):
            g_sum[nm] = _slab_sum(a, f"rs_slab_sum_{nm}")

    def conv_w_grad(a):
        return _unpad_shard_cols(a.transpose(1, 0, 2).reshape(CONV_TAPS, 2 * fp), ncw, ncp)

    def conv_b_grad(a):
        return _unpad_shard_cols(a.reshape(2 * fp), ncw, ncp)

    dmod_vec = [dmods[f"{nm}_{l}"] for l in range(2) for nm in mod_names] + [dmods["kv_sh"], dmods["kv_sc"]]
    post = _pack_small(dmod_vec + [dlb, g["a_norm_g"], g["k_norm_g"], g["q_norm_g"],
                                   jnp.pad(g["kv_b_f"].reshape(-1), (0, LANES - nh)),
                                   conv_w_grad(g["conv_w0"]), conv_w_grad(g["conv_w1"]),
                                   conv_b_grad(g["conv_b0"]), conv_b_grad(g["conv_b1"])])
    (post_all,) = _all_gather([post], "gather_small_grads")
    a_in_flight, a_in_sent = _xchg_start([g_last["a_in"]], True, ALL_PEERS, post_all, "scatter_l0a_start")
    a_in_sent = a_in_sent.reshape(1, 1)
    tot = _slab_sum(post_all, "small_grad_sum").reshape(-1)
    nmod = 14 * d
    (t_mod, t_lb, t_ang, t_kng, t_qng, t_bf, t_cw, t_cb) = _unpack_small(
        tot, [(nmod,), (1, d), (1, HEAD), (HEAD,), (1, HEAD), (LANES,), (2, CONV_TAPS, two_f), (2, two_f)])
    dm_all = post_all.reshape(NDEV, -1)[:, :nmod]
    dm0 = lax.dynamic_slice_in_dim(dm_all[:, :6 * d], me * n0, n0, axis=1)
    dm1 = lax.dynamic_slice_in_dim(dm_all[:, 6 * d:12 * d], me * n0, n0, axis=1)
    dkv = lax.dynamic_slice_in_dim(dm_all[:, 12 * d:], me * nkv, nkv, axis=1)
    g_ada_w, g_kv_ada_w, g_logits = _ada_bwd(c_act, _pad_rows(dm0, 2 * NDEV), _pad_rows(dm1, 2 * NDEV),
                                              _pad_rows(dkv, 2 * NDEV), lb, t_lb)

    grads = {
        "ada_w": g_ada_w,
        "ada_b": t_mod[:12 * d].reshape(2, 6 * d),
        "a_lb_logits": lax.dynamic_slice_in_dim(g_logits, me * HEAD, HEAD, axis=1),
        "a_norm_g": t_ang,
        "a_w_out": g_sum["a_w_out"].reshape(a_w_out.shape),
        "kv_ada_w": g_kv_ada_w,
        "kv_ada_b": t_mod[12 * d:],
        "kv_w": g_sum["kv_w"],
        "kv_b_f": t_bf[:nh],
        "k_norm_g": t_kng,
        "b_w_q": g_sum["b_w_q"].reshape(b_w_q.shape),
        "q_norm_g": t_qng,
        "b_w_out": g_sum["b_w_out"].reshape(b_w_out.shape),
        "ffn_w_up": jnp.stack([g_sum["up0"][:, :ncw], g_sum["up1"][:, :ncw]]),
        "ffn_conv_w": lax.dynamic_slice_in_dim(t_cw, me * ncw, ncw, axis=2),
        "ffn_conv_b": t_cb,
        "ffn_w_down": jnp.stack([g_sum["down0"], g_sum["down1"]]),
    }

    big_adam = ["ada_w", "a_w_out", "kv_ada_w", "kv_w", "b_w_q", "b_w_out", "ffn_w_up", "ffn_w_down", "a_w_in"]
    small_adam = [n for n in order if n not in big_adam]
    delta, new_m, new_v = {}, {}, {}
    packs = [_pack_small([src[n] for n in small_adam]) for src in (weights, grads, m_in, v_in)]
    outs = _adamw(*packs, "adamw_small", tr=packs[0].shape[0])
    shapes = [weights[n].shape for n in small_adam]
    for dst, o in zip((delta, new_m, new_v), outs):
        for n, a in zip(small_adam, _unpack_small(o.reshape(-1), shapes)):
            dst[n] = a
    for n in big_adam:
        if n == "a_w_in":
            (landed,) = _xchg_wait(a_in_flight, new_v["ffn_w_down"], True, ALL_PEERS, "scatter_l0a_wait")
            grads[n] = _slab_sum(landed, "rs_slab_sum_a_w_in").reshape(a_w_in.shape)
        shp = weights[n].shape
        two_d = lambda a: a.reshape(-1, shp[-1])
        dl, mn, vn = _adamw(two_d(weights[n]), two_d(grads[n]), two_d(m_in[n]), two_d(v_in[n]), f"adamw_{n}",
                            after=a_in_sent)
        delta[n], new_m[n], new_v[n] = dl.reshape(shp), mn.reshape(shp), vn.reshape(shp)

    return (loss, grad_x.reshape(x.shape), *[grads[n] for n in order], *[delta[n] for n in order],
            *[new_m[n] for n in order], *[new_v[n] for n in order])
```

```python
import functools

import jax
import jax.numpy as jnp
from jax import lax
from jax.experimental import pallas as pl
from jax.experimental.pallas import tpu as pltpu

F32 = jnp.float32
BF16 = jnp.bfloat16

NDEV = 8
NCHIP = 4
HEAD = 128
A_CHUNK = 64
CONV_TAPS = 3
EPS = 1e-6
NEG_INF = -1e30
LANES = 128
VMEM_LIMIT = 48 * 1024 * 1024

ADAM_LR = 0.001
ADAM_B1 = 0.9
ADAM_B2 = 0.999
ADAM_EPS = 1e-08
ADAM_WD = 0.01
ADAM_STEP = 10

_NN = (((1,), (0,)), ((), ()))
_NT = (((1,), (1,)), ((), ()))
_TN = (((0,), (0,)), ((), ()))
_MESH = pl.DeviceIdType.MESH


def _cparams(**kw):
    return pltpu.CompilerParams(vmem_limit_bytes=VMEM_LIMIT, **kw)


def _divisor_tile(n, pref, unit=LANES):
    if n <= pref:
        return n
    best = None
    for t in range(unit, pref + 1, unit):
        if n % t == 0:
            best = t
    assert best is not None, (n, pref)
    return best


def _round_up(n, unit):
    return -(-n // unit) * unit


def _bdot_raw(a, b, dims):
    return lax.dot_general(a.astype(BF16), b.astype(BF16), dims, preferred_element_type=F32)


@jax.custom_vjp
def _dot_nn(a, b):
    return _bdot_raw(a, b, _NN)


@jax.custom_vjp
def _dot_nt(a, b):
    return _bdot_raw(a, b, _NT)


@jax.custom_vjp
def _dot_tn(a, b):
    return _bdot_raw(a, b, _TN)


_dot_nn.defvjp(lambda a, b: (_bdot_raw(a, b, _NN), (a, b)),
               lambda r, g: (_dot_nt(g, r[1]), _dot_tn(r[0], g)))
_dot_nt.defvjp(lambda a, b: (_bdot_raw(a, b, _NT), (a, b)),
               lambda r, g: (_dot_nn(g, r[1]), _dot_tn(g, r[0])))
_dot_tn.defvjp(lambda a, b: (_bdot_raw(a, b, _TN), (a, b)),
               lambda r, g: (_dot_nt(r[1], g), _dot_nn(r[0], g)))


def _f32dot(a, b):
    return lax.dot_general(a, b, _NN, precision=lax.Precision.HIGHEST, preferred_element_type=F32)


def _sigmoid(x):
    return jax.nn.sigmoid(x)


def _silu(x):
    return x * jax.nn.sigmoid(x)


def _rms(x):
    return x * lax.rsqrt(jnp.mean(x * x, axis=-1, keepdims=True) + EPS)


def _modulate(x, sh, sc):
    return _rms(x) * (1.0 + sc) + sh


def _mm_call(a, b, dims, a_spec, b_spec, o_spec, o_shape, grid, acc_tile, name):
    nk = grid[2]

    def body(a_ref, b_ref, o_ref, *acc):
        p = lax.dot_general(a_ref[...].astype(BF16), b_ref[...].astype(BF16), dims,
                            preferred_element_type=F32)
        if nk == 1:
            o_ref[...] = p.astype(o_ref.dtype)
        else:
            kk = pl.program_id(2)

            @pl.when(kk == 0)
            def _():
                acc[0][...] = p

            @pl.when(kk > 0)
            def _():
                acc[0][...] += p

            @pl.when(kk == nk - 1)
            def _():
                o_ref[...] = acc[0][...].astype(o_ref.dtype)

    return pl.pallas_call(
        body, name=name, grid=grid, in_specs=[a_spec, b_spec], out_specs=o_spec, out_shape=o_shape,
        scratch_shapes=[pltpu.VMEM(acc_tile, F32)] if nk > 1 else [],
        compiler_params=_cparams(dimension_semantics=("parallel", "parallel", "arbitrary")),
    )(a, b)


def _mm(a, b, mode, out_dtype, name, tm=1024, tn=1024, tk=2048):
    if mode == "nn":
        (m, k), (k2, n) = a.shape, b.shape
    elif mode == "nt":
        (m, k), (n, k2) = a.shape, b.shape
    else:
        (k, m), (k2, n) = a.shape, b.shape
    assert k == k2, (a.shape, b.shape, mode)
    tm, tn, tk = _divisor_tile(m, tm), _divisor_tile(n, tn), _divisor_tile(k, tk)
    if mode == "tn":
        a_spec = pl.BlockSpec((tk, tm), lambda i, j, kk: (kk, i))
    else:
        a_spec = pl.BlockSpec((tm, tk), lambda i, j, kk: (i, kk))
    if mode == "nt":
        b_spec = pl.BlockSpec((tn, tk), lambda i, j, kk: (j, kk))
    else:
        b_spec = pl.BlockSpec((tk, tn), lambda i, j, kk: (kk, j))
    return _mm_call(a, b, {"nn": _NN, "nt": _NT, "tn": _TN}[mode], a_spec, b_spec,
                    pl.BlockSpec((tm, tn), lambda i, j, kk: (i, j)), jax.ShapeDtypeStruct((m, n), out_dtype),
                    (m // tm, n // tn, k // tk), (tm, tn), name)


def _wblk_act_spec(rows, gb, nl, split, nb, row_axis, blk_axis):
    if split == 1:
        return pl.BlockSpec((rows, gb * nl), lambda *g: (g[row_axis], g[blk_axis]))
    groups = nb // split // gb
    return pl.BlockSpec((None, rows, gb * nl),
                        lambda *g: (g[blk_axis] // groups, g[row_axis], g[blk_axis] % groups))


def _mm_wblk(a, wb, out_dtype, name, *, gb, row_off=0, split=1, tm=1024):
    m, k = a.shape
    nb, _, nl = wb.shape
    assert (nb // split) % gb == 0
    tm = _divisor_tile(m, tm)

    def body(a_ref, b_ref, o_ref):
        av = a_ref[...].astype(BF16)
        for s in range(gb):
            o_ref[:, s * nl:(s + 1) * nl] = lax.dot_general(
                av, b_ref[s].astype(BF16), _NN, preferred_element_type=F32).astype(o_ref.dtype)

    o_shape = (m, nb * nl) if split == 1 else (split, m, nb // split * nl)
    return pl.pallas_call(
        body, name=name, grid=(nb // gb, m // tm),
        in_specs=[pl.BlockSpec((tm, k), lambda j, i: (i, 0)),
                  pl.BlockSpec((gb, k, nl), lambda j, i: (j, row_off, 0))],
        out_specs=_wblk_act_spec(tm, gb, nl, split, nb, 1, 0),
        out_shape=jax.ShapeDtypeStruct(o_shape, out_dtype),
        compiler_params=_cparams(dimension_semantics=("parallel", "parallel")),
    )(a, wb)


def _mm_wblk_dx(dy, wb, out_dtype, name, *, k, gb, row_off=0, split=1, tm=1024):
    nb, _, nl = wb.shape
    m = dy.shape[-2]
    tm = _divisor_tile(m, tm)
    nk = nb // gb
    per = nb // split
    whole = split > 1 and gb == nb
    assert whole or per % gb == 0

    def body(a_ref, b_ref, o_ref, *acc):
        p = None
        for s in range(gb):
            a_blk = a_ref[s // per, :, (s % per) * nl:(s % per + 1) * nl] if whole else a_ref[:, s * nl:(s + 1) * nl]
            q = lax.dot_general(a_blk.astype(BF16), b_ref[s].astype(BF16), _NT, preferred_element_type=F32)
            p = q if p is None else p + q
        if nk == 1:
            o_ref[...] = p.astype(o_ref.dtype)
        else:
            kk = pl.program_id(1)

            @pl.when(kk == 0)
            def _():
                acc[0][...] = p

            @pl.when(kk > 0)
            def _():
                acc[0][...] += p

            @pl.when(kk == nk - 1)
            def _():
                o_ref[...] = acc[0][...].astype(o_ref.dtype)

    return pl.pallas_call(
        body, name=name, grid=(m // tm, nk),
        in_specs=[pl.BlockSpec((split, tm, per * nl), lambda i, kk: (0, i, 0)) if whole
                  else _wblk_act_spec(tm, gb, nl, split, nb, 0, 1),
                  pl.BlockSpec((gb, k, nl), lambda i, kk: (kk, row_off, 0))],
        out_specs=pl.BlockSpec((tm, k), lambda i, kk: (i, 0)),
        out_shape=jax.ShapeDtypeStruct((m, k), out_dtype),
        scratch_shapes=[pltpu.VMEM((tm, k), F32)] if nk > 1 else [],
        compiler_params=_cparams(dimension_semantics=("parallel", "arbitrary")),
    )(dy, wb)


def _mm_wblk_dw(x, dy, name, *, nb, gb, split=1, tk=1024):
    t, k = x.shape
    assert (nb // split) % gb == 0
    nl = dy.shape[-1] * split // nb
    tk = _divisor_tile(t, tk)
    nk = t // tk

    def body(a_ref, b_ref, o_ref, *acc):
        kk = pl.program_id(1)
        av = a_ref[...].astype(BF16)
        for s in range(gb):
            p = lax.dot_general(av, b_ref[:, s * nl:(s + 1) * nl].astype(BF16), _TN, preferred_element_type=F32)
            if nk == 1:
                o_ref[s] = p.astype(o_ref.dtype)
                continue

            @pl.when(kk == 0)
            def _():
                acc[0][s] = p

            @pl.when(kk > 0)
            def _():
                acc[0][s] += p

        if nk > 1:
            @pl.when(kk == nk - 1)
            def _():
                o_ref[...] = acc[0][...].astype(o_ref.dtype)

    return pl.pallas_call(
        body, name=name, grid=(nb // gb, nk),
        in_specs=[pl.BlockSpec((tk, k), lambda j, kk: (kk, 0)), _wblk_act_spec(tk, gb, nl, split, nb, 1, 0)],
        out_specs=pl.BlockSpec((gb, k, nl), lambda j, kk: (j, 0, 0)),
        out_shape=jax.ShapeDtypeStruct((nb, k, nl), BF16),
        scratch_shapes=[pltpu.VMEM((gb, k, nl), F32)] if nk > 1 else [],
        compiler_params=_cparams(dimension_semantics=("parallel", "arbitrary")),
    )(x, dy)


def _row_specs(rows, tb, nsub):
    return [pl.BlockSpec((tb, nsub * cw), functools.partial(lambda i, off: (i, off), off=off))
            for (_, cw, off) in rows]


def _vec_specs(params):
    return [pl.BlockSpec(p.shape, lambda i: (0, 0)) for p in params]


def _row_fwd(f, rows, params, out_dtypes, *, nsub=1, tb, name):
    t = rows[0][0].shape[0]
    tb = min(tb, t)
    n_r, n_p = len(rows), len(params)
    blk = [jax.ShapeDtypeStruct((tb, cw), F32) for (_, cw, _) in rows]
    blk += [jax.ShapeDtypeStruct(p.shape, F32) for p in params]
    out_avals = jax.eval_shape(f, *blk)

    def body(*refs):
        pv = [r[...] for r in refs[n_r:n_r + n_p]]
        for s in range(nsub):
            vals = [r[:, s * cw:(s + 1) * cw].astype(F32) for r, (_, cw, _) in zip(refs[:n_r], rows)]
            outs = f(*vals, *pv)
            for o_ref, o in zip(refs[n_r + n_p:], outs):
                w = o.shape[1]
                o_ref[:, s * w:(s + 1) * w] = o.astype(o_ref.dtype)

    return pl.pallas_call(
        body, name=name,
        grid=(t // tb,),
        in_specs=_row_specs(rows, tb, nsub) + _vec_specs(params),
        out_specs=[pl.BlockSpec((tb, nsub * av.shape[1]), lambda i: (i, 0)) for av in out_avals],
        out_shape=[jax.ShapeDtypeStruct((t, nsub * av.shape[1]), dt) for av, dt in zip(out_avals, out_dtypes)],
        compiler_params=_cparams(dimension_semantics=("parallel",)),
    )(*[r[0] for r in rows], *params)


def _row_bwd(f, rows, params, cots, row_grad_dtypes, *, nsub=1, tb, name, add_to=None, cot_add=None):
    t = rows[0][0].shape[0]
    tb = min(tb, t)
    n_r, n_p, n_c = len(rows), len(params), len(cots)
    want = [j for j in range(n_r) if row_grad_dtypes[j] is not None]
    extra = [] if add_to is None else [(add_to[1], rows[add_to[0]][1], 0)]
    extra += [] if cot_add is None else [(cot_add[1], cots[cot_add[0]][1], 0)]

    def body(*refs):
        i = pl.program_id(0)
        r_in, p_in = refs[:n_r], refs[n_r:n_r + n_p]
        c_in = refs[n_r + n_p:n_r + n_p + n_c]
        e_in = refs[n_r + n_p + n_c:n_r + n_p + n_c + len(extra)]
        outs = refs[n_r + n_p + n_c + len(extra):]
        pv = [r[...] for r in p_in]
        psum = [None] * n_p
        for s in range(nsub):
            vals = [r[:, s * cw:(s + 1) * cw].astype(F32) for r, (_, cw, _) in zip(r_in, rows)]
            cvals = [r[:, s * cw:(s + 1) * cw].astype(F32) for r, (_, cw, _) in zip(c_in, cots)]
            if cot_add is not None:
                cw = cots[cot_add[0]][1]
                cvals[cot_add[0]] = cvals[cot_add[0]] + e_in[-1][:, s * cw:(s + 1) * cw]
            _, vjp_fn = jax.vjp(f, *vals, *pv)
            grads = vjp_fn(tuple(cvals))
            for o_ref, jr in zip(outs[:len(want)], want):
                cw = rows[jr][1]
                gr = grads[jr]
                if add_to is not None and jr == add_to[0]:
                    gr = gr + e_in[0][:, s * cw:(s + 1) * cw]
                o_ref[:, s * cw:(s + 1) * cw] = gr.astype(o_ref.dtype)
            for jp in range(n_p):
                psum[jp] = grads[n_r + jp] if psum[jp] is None else psum[jp] + grads[n_r + jp]
        for o_ref, g in zip(outs[len(want):], psum):
            @pl.when(i == 0)
            def _():
                o_ref[...] = g

            @pl.when(i > 0)
            def _():
                o_ref[...] += g

    out_specs = [pl.BlockSpec((tb, nsub * rows[jr][1]), lambda i: (i, 0)) for jr in want]
    out_shape = [jax.ShapeDtypeStruct((t, nsub * rows[jr][1]), row_grad_dtypes[jr]) for jr in want]
    out_specs += _vec_specs(params)
    out_shape += [jax.ShapeDtypeStruct(p.shape, F32) for p in params]
    res = pl.pallas_call(
        body, name=name,
        grid=(t // tb,),
        in_specs=_row_specs(rows, tb, nsub) + _vec_specs(params) + _row_specs(cots, tb, nsub)
        + _row_specs(extra, tb, nsub),
        out_specs=out_specs, out_shape=out_shape,
        compiler_params=_cparams(dimension_semantics=("arbitrary",)),
    )(*[r[0] for r in rows], *params, *[c[0] for c in cots], *[e[0] for e in extra])
    return res[:len(want)], res[len(want):]


def _f_mod(x, sh, sc):
    return (_modulate(x, sh, sc),)


def _f_res_mod(x, y, g, sh, sc):
    x1 = x + g * y
    return x1, _modulate(x1, sh, sc)


def _f_res_mod2(x, y, g, sh_a, sc_a, sh_b, sc_b):
    x1 = x + g * y
    return x1, _modulate(x1, sh_a, sc_a), _modulate(x1, sh_b, sc_b)


def _f_qnorm(p, g):
    return (_rms(p) * g * (HEAD ** -0.5),)


def _f_knorm(p, g):
    return (_rms(p) * g,)


def _f_qnorm_aug(p, g):
    lane = lax.broadcasted_iota(jnp.int32, p.shape, 1)
    return (jnp.concatenate([_rms(p) * g * (HEAD ** -0.5), jnp.where(lane < 3, 1.0, 0.0)], axis=1),)


def _f_knorm_aug(p, c0, c1, c2, g):
    lane = lax.broadcasted_iota(jnp.int32, p.shape, 1)
    aug = jnp.where(lane == 0, c0, jnp.where(lane == 1, c1, jnp.where(lane == 2, c2, 0.0)))
    return (jnp.concatenate([_rms(p) * g, aug], axis=1),)


def _split3(a):
    round_bf16 = lambda v: lax.reduce_precision(v, exponent_bits=8, mantissa_bits=7)
    hi = round_bf16(a)
    mid = round_bf16(a - hi)
    lo = round_bf16(a - hi - mid)
    return hi.astype(BF16), mid.astype(BF16), lo.astype(BF16)


def _f_outgate(o, og):
    return (o * _sigmoid(og),)


def _loss_call(x3, f, g2, target, tb):
    t, d = x3.shape
    tb = min(tb, t)

    def body(x_ref, f_ref, g_ref, t_ref, loss_ref, dx_ref, df_ref, dg_ref):
        i = pl.program_id(0)
        fv = f_ref[...]
        g = g_ref[...]
        e = x_ref[...] + g * fv - t_ref[...]
        dx = e * (1.0 / d)
        part = 0.5 * jnp.sum(jnp.sum(e * dx, axis=1, keepdims=True), axis=0, keepdims=True)
        dx_ref[...] = dx
        df_ref[...] = (g * dx).astype(df_ref.dtype)
        dg = jnp.sum(dx * fv, axis=0, keepdims=True)

        @pl.when(i == 0)
        def _():
            loss_ref[...] = jnp.broadcast_to(part, loss_ref.shape)
            dg_ref[...] = dg

        @pl.when(i > 0)
        def _():
            loss_ref[...] += jnp.broadcast_to(part, loss_ref.shape)
            dg_ref[...] += dg

    row = pl.BlockSpec((tb, d), lambda i: (i, 0))
    vec = pl.BlockSpec((1, d), lambda i: (0, 0))
    return pl.pallas_call(
        body, name="loss_head",
        grid=(t // tb,),
        in_specs=[row, row, vec, row],
        out_specs=[pl.BlockSpec((1, LANES), lambda i: (0, 0)), row, row, vec],
        out_shape=[jax.ShapeDtypeStruct((1, LANES), F32), jax.ShapeDtypeStruct((t, d), F32),
                   jax.ShapeDtypeStruct((t, d), BF16), jax.ShapeDtypeStruct((1, d), F32)],
        compiler_params=_cparams(dimension_semantics=("arbitrary",)),
    )(x3, f, g2, target)


def _hg_consts(tb):
    c = A_CHUNK
    r = lax.broadcasted_iota(jnp.int32, (c, c), 0)
    s = lax.broadcasted_iota(jnp.int32, (c, c), 1)
    br = lax.broadcasted_iota(jnp.int32, (tb, tb), 0)
    bs = lax.broadcasted_iota(jnp.int32, (tb, tb), 1)
    shift = c.bit_length() - 1
    same_chunk = jnp.right_shift(br, shift) == jnp.right_shift(bs, shift)
    return (s <= r).astype(F32), (r <= s).astype(F32), jnp.logical_and(same_chunk, bs <= br)


def _chunk_apply(mat, x):
    c = mat.shape[0]
    return jnp.concatenate([_f32dot(mat, x[i * c:(i + 1) * c]) for i in range(x.shape[0] // c)], axis=0)


@jax.custom_vjp
def _chunk_cumsum(x, tri, tri_t):
    return _chunk_apply(tri, x)


_chunk_cumsum.defvjp(lambda x, tri, tri_t: (_chunk_apply(tri, x), (tri, tri_t)),
                     lambda r, g: (_chunk_apply(r[1], g), jnp.zeros_like(r[0]), jnp.zeros_like(r[1])))


def _per_chunk(a, b, dims):
    return jnp.stack([_bdot_raw(a[i], b[i], dims) for i in range(a.shape[0])])


@jax.custom_vjp
def _chunk_tn(a, b):
    return _per_chunk(a, b, _TN)


@jax.custom_vjp
def _chunk_nt(a, b):
    return _per_chunk(a, b, _NT)


@jax.custom_vjp
def _chunk_nn(a, b):
    return _per_chunk(a, b, _NN)


_chunk_tn.defvjp(lambda a, b: (_per_chunk(a, b, _TN), (a, b)),
                 lambda r, g: (_chunk_nt(r[1], g), _chunk_nn(r[0], g)))
_chunk_nt.defvjp(lambda a, b: (_per_chunk(a, b, _NT), (a, b)),
                 lambda r, g: (_chunk_nn(g, r[1]), _chunk_tn(g, r[0])))
_chunk_nn.defvjp(lambda a, b: (_per_chunk(a, b, _NN), (a, b)),
                 lambda r, g: (_chunk_nt(g, r[1]), _chunk_tn(r[0], g)))


def _scan_states(decay, m, st):
    sts = []
    for i in range(m.shape[0]):
        sts.append(st)
        st = st * decay[i] + m[i]
    return jnp.stack(sts), st


@jax.custom_vjp
def _state_scan(decay, m, st):
    return _scan_states(decay, m, st)


def _state_scan_fwd(decay, m, st):
    sts, st_out = _scan_states(decay, m, st)
    return (sts, st_out), (decay, sts)


def _state_scan_bwd(res, cts):
    decay, sts = res
    d_sts, g = cts
    d_decay, d_m = [], []
    for i in range(sts.shape[0] - 1, -1, -1):
        d_m.append(g)
        d_decay.append(jnp.sum(g * sts[i], axis=0, keepdims=True))
        g = g * decay[i] + d_sts[i]
    return jnp.stack(d_decay[::-1]), jnp.stack(d_m[::-1]), g


_state_scan.defvjp(_state_scan_fwd, _state_scan_bwd)


def _hg_block(qp, fp, ip, gp, lb, ng, st, tri, tri_t, bd_causal):
    tb = qp.shape[0]
    c = A_CHUNK
    n = tb // c
    q = _silu(qp)
    fg = lb + (1.0 - lb) * _sigmoid(fp)
    logf = jnp.log(fg)
    k = 1.0 - fg
    b3 = _chunk_cumsum(logf, tri, tri_t).reshape(n, c, HEAD)
    pos = lax.broadcasted_iota(jnp.int32, (1, c, 1), 1)
    b_mid = lax.stop_gradient(jnp.sum(jnp.where(pos == c // 2, b3, 0.0), axis=1, keepdims=True))
    b_last = jnp.sum(jnp.where(pos == c - 1, b3, 0.0), axis=1, keepdims=True)
    q3, k3, v3 = q.reshape(n, c, HEAD), k.reshape(n, c, HEAD), ip.reshape(n, c, HEAD)
    scores = _dot_nt((q3 * jnp.exp(b3 - b_mid)).reshape(tb, HEAD), (k3 * jnp.exp(b_mid - b3)).reshape(tb, HEAD))
    o_intra = _dot_nn(jnp.where(bd_causal, scores, 0.0), ip)
    states, st_new = _state_scan(jnp.exp(b_last), _chunk_tn(v3, k3 * jnp.exp(b_last - b3)), st)
    o = o_intra + _chunk_nt(q3 * jnp.exp(b3), states).reshape(tb, HEAD)
    y = _rms(o) * ng * _silu(gp)
    return y, st_new


HG_HEADS = 2


def _hg_specs(tb, nh, rev_nb=None):
    wide = HG_HEADS * HEAD
    per = nh // HG_HEADS

    def row(part):
        if rev_nb is None:
            return pl.BlockSpec((tb, wide), functools.partial(lambda h, i, off: (i, off + h), off=part * per))
        return pl.BlockSpec((tb, wide),
                            functools.partial(lambda h, i, off: (rev_nb - 1 - i, off + h), off=part * per))
    return [row(0), row(1), row(2), row(3),
            pl.BlockSpec((1, wide), lambda h, i: (0, h)), pl.BlockSpec((1, HEAD), lambda h, i: (0, 0))]


def _hgrn2_fwd(proj, lb, ng, tb):
    t = proj.shape[0]
    nh = proj.shape[1] // (4 * HEAD)
    tb = min(tb, t)
    nb = t // tb
    wide = HG_HEADS * HEAD

    def body(q_ref, f_ref, i_ref, g_ref, lb_ref, ng_ref, y_ref, s_ref, st_ref):
        i = pl.program_id(1)

        @pl.when(i == 0)
        def _():
            st_ref[...] = jnp.zeros_like(st_ref)

        consts = _hg_consts(tb)
        for p in range(HG_HEADS):
            cs = slice(p * HEAD, (p + 1) * HEAD)
            st = st_ref[p]
            s_ref[p, 0] = st
            y, st_new = _hg_block(q_ref[:, cs], f_ref[:, cs], i_ref[:, cs], g_ref[:, cs], lb_ref[:, cs],
                                  ng_ref[...], st, *consts)
            y_ref[:, cs] = y.astype(y_ref.dtype)
            st_ref[p] = st_new

    return pl.pallas_call(
        body, name="hgrn2_fwd",
        grid=(nh // HG_HEADS, nb),
        in_specs=_hg_specs(tb, nh),
        out_specs=[pl.BlockSpec((tb, wide), lambda h, i: (i, h)),
                   pl.BlockSpec((HG_HEADS, 1, HEAD, HEAD), lambda h, i: (h, i, 0, 0))],
        out_shape=[jax.ShapeDtypeStruct((t, nh * HEAD), BF16),
                   jax.ShapeDtypeStruct((nh, nb, HEAD, HEAD), F32)],
        scratch_shapes=[pltpu.VMEM((HG_HEADS, HEAD, HEAD), F32)],
        compiler_params=_cparams(dimension_semantics=("parallel", "arbitrary")),
    )(proj, proj, proj, proj, lb, ng)


def _hgrn2_bwd(proj, lb, ng, states, dy, tb):
    t = proj.shape[0]
    nh = proj.shape[1] // (4 * HEAD)
    tb = min(tb, t)
    nb = t // tb
    wide = HG_HEADS * HEAD

    def body(q_ref, f_ref, i_ref, g_ref, lb_ref, ng_ref, s_ref, dy_ref, dp_ref, dlb_ref, dng_ref, dst_ref):
        h, i = pl.program_id(0), pl.program_id(1)
        consts = _hg_consts(tb)

        @pl.when(i == 0)
        def _():
            dst_ref[...] = jnp.zeros_like(dst_ref)
            dlb_ref[...] = jnp.zeros_like(dlb_ref)

        @pl.when(jnp.logical_and(i == 0, h == 0))
        def _():
            dng_ref[...] = jnp.zeros_like(dng_ref)

        def fn(qp, fp, ip, gp, lbx, ngx, stx):
            return _hg_block(qp, fp, ip, gp, lbx, ngx, stx, *consts)

        for p in range(HG_HEADS):
            cs = slice(p * HEAD, (p + 1) * HEAD)
            _, vjp_fn = jax.vjp(fn, q_ref[:, cs], f_ref[:, cs], i_ref[:, cs], g_ref[:, cs], lb_ref[:, cs],
                                ng_ref[...], s_ref[p, 0])
            *gparts, glb, gng, dst = vjp_fn((dy_ref[:, cs].astype(F32), dst_ref[p]))
            for part, gpart in enumerate(gparts):
                dp_ref[part, :, cs] = gpart.astype(dp_ref.dtype)
            dst_ref[p] = dst
            dlb_ref[:, cs] += glb
            dng_ref[...] += gng

    rev = lambda h, i: (nb - 1 - i, h)
    return pl.pallas_call(
        body, name="hgrn2_bwd",
        grid=(nh // HG_HEADS, nb),
        in_specs=_hg_specs(tb, nh, rev_nb=nb) + [
            pl.BlockSpec((HG_HEADS, 1, HEAD, HEAD), lambda h, i: (h, nb - 1 - i, 0, 0)),
            pl.BlockSpec((tb, wide), rev)],
        out_specs=[pl.BlockSpec((4, tb, wide), lambda h, i: (0, nb - 1 - i, h)),
                   pl.BlockSpec((1, wide), lambda h, i: (0, h)), pl.BlockSpec((1, HEAD), lambda h, i: (0, 0))],
        out_shape=[jax.ShapeDtypeStruct((4, t, nh * HEAD), BF16),
                   jax.ShapeDtypeStruct((1, nh * HEAD), F32), jax.ShapeDtypeStruct((1, HEAD), F32)],
        scratch_shapes=[pltpu.VMEM((HG_HEADS, HEAD, HEAD), F32)],
        compiler_params=_cparams(dimension_semantics=("arbitrary", "arbitrary")),
    )(proj, proj, proj, proj, lb, ng, states, dy)


def _fgate_consts(cb):
    r = lax.broadcasted_iota(jnp.int32, (cb, cb), 0)
    s = lax.broadcasted_iota(jnp.int32, (cb, cb), 1)
    return (r <= s).astype(F32), (r >= s).astype(F32)


def _fgate_fwd(xt, bias, cb=512):
    nh, t = xt.shape
    cb = min(cb, t)

    def body(x_ref, b_ref, o_ref):
        upper, _ = _fgate_consts(cb)
        carry = jnp.zeros((nh, 1), F32)
        for blk in range(t // cb):
            z = x_ref[:, blk * cb:(blk + 1) * cb] + b_ref[...]
            logf = jnp.minimum(z, 0.0) - jnp.log(1.0 + jnp.exp(-jnp.abs(z)))
            cs = _f32dot(logf, upper) + carry
            o_ref[:, blk * cb:(blk + 1) * cb] = cs
            carry = cs[:, cb - 1:cb]

    vm = pl.BlockSpec(memory_space=pltpu.VMEM)
    return pl.pallas_call(
        body, name="fgate_fwd", in_specs=[vm, vm], out_specs=vm,
        out_shape=jax.ShapeDtypeStruct((nh, t), F32), compiler_params=_cparams(),
    )(xt, bias)


def _fgate_bwd(xt, bias, dft, cb=512):
    nh, t = xt.shape
    cb = min(cb, t)
    nblk = t // cb

    def body(x_ref, b_ref, d_ref, dx_ref, db_ref):
        _, lower = _fgate_consts(cb)
        carry = jnp.zeros((nh, 1), F32)
        db = jnp.zeros((nh, 1), F32)
        for blk in range(nblk - 1, -1, -1):
            sl = slice(blk * cb, (blk + 1) * cb)
            dlogf = _f32dot(d_ref[:, sl], lower) + carry
            carry = dlogf[:, 0:1]
            z = x_ref[:, sl] + b_ref[...]
            dz = dlogf * (1.0 - _sigmoid(z))
            dx_ref[:, sl] = dz
            db = db + jnp.sum(dz, axis=1, keepdims=True)
        db_ref[...] = db

    vm = pl.BlockSpec(memory_space=pltpu.VMEM)
    return pl.pallas_call(
        body, name="fgate_bwd", in_specs=[vm, vm, vm], out_specs=[vm, vm],
        out_shape=[jax.ShapeDtypeStruct((nh, t), F32), jax.ShapeDtypeStruct((nh, 1), F32)],
        compiler_params=_cparams(),
    )(xt, bias, dft)


def _attn_fwd(q, k, v, f_col, blk):
    t, width = v.shape
    nh = width // HEAD
    nq = t // blk

    def body(q_ref, k_ref, v_ref, fc_ref, o_ref, lse_ref):
        i = pl.program_id(0)
        tri = (lax.broadcasted_iota(jnp.int32, (blk, blk), 1) <= lax.broadcasted_iota(jnp.int32, (blk, blk), 0))
        for h in range(nh):
            cs = slice(h * HEAD, (h + 1) * HEAD)
            cs2 = slice(2 * h * HEAD, 2 * (h + 1) * HEAD)
            qh = q_ref[:, cs2]

            def tile(j, carry, masked):
                m, l, acc = carry
                rs = pl.ds(pl.multiple_of(j * blk, blk), blk)
                s = _bdot_raw(qh, k_ref[rs, cs2], _NT)
                if masked:
                    s = jnp.where(tri, s, NEG_INF)
                m_new = jnp.maximum(m, jnp.max(s, axis=1, keepdims=True))
                p = jnp.exp(s - m_new)
                alpha = jnp.exp(m - m_new)
                l_new = alpha * l + jnp.sum(p, axis=1, keepdims=True)
                acc_new = alpha * acc + _bdot_raw(p, v_ref[rs, cs], _NN)
                return m_new, l_new, acc_new

            init = (jnp.full((blk, 1), NEG_INF, F32), jnp.zeros((blk, 1), F32), jnp.zeros((blk, HEAD), F32))
            carry = lax.fori_loop(0, i, lambda j, c: tile(j, c, False), init)
            m, l, acc = tile(i, carry, True)
            o_ref[:, cs] = acc / l
            lse_ref[:, h:h + 1] = m + jnp.log(l) + fc_ref[:, h:h + 1]

    vm = pl.BlockSpec(memory_space=pltpu.VMEM)
    return pl.pallas_call(
        body, name="fox_attn_fwd",
        grid=(nq,),
        in_specs=[pl.BlockSpec((blk, 2 * width), lambda i: (i, 0)), vm, vm,
                  pl.BlockSpec((blk, nh), lambda i: (i, 0))],
        out_specs=[pl.BlockSpec((blk, width), lambda i: (i, 0)), pl.BlockSpec((blk, nh), lambda i: (i, 0))],
        out_shape=[jax.ShapeDtypeStruct((t, width), F32), jax.ShapeDtypeStruct((t, nh), F32)],
        compiler_params=_cparams(dimension_semantics=("parallel",)),
    )(q, k, v, f_col)


def _attn_delta(do, o, tb):
    t, width = o.shape
    nh = width // HEAD
    tb = min(tb, t)

    def body(do_ref, o_ref, dl_ref):
        for h in range(nh):
            cs = slice(h * HEAD, (h + 1) * HEAD)
            dl_ref[:, h:h + 1] = jnp.sum(do_ref[:, cs].astype(F32) * o_ref[:, cs], axis=1, keepdims=True)

    wide = pl.BlockSpec((tb, width), lambda i: (i, 0))
    return pl.pallas_call(body, name="fox_attn_delta", grid=(t // tb,), in_specs=[wide, wide],
                          out_specs=pl.BlockSpec((tb, nh), lambda i: (i, 0)),
                          out_shape=jax.ShapeDtypeStruct((t, nh), F32),
                          compiler_params=_cparams(dimension_semantics=("parallel",)))(do, o)


ATTN_BWD_GROUPS = 4


def _attn_bwd(q, k, v, f_col, do, lse, delta, blk):
    t, width = v.shape
    nh = width // HEAD
    nq = t // blk
    hpg = nh // ATTN_BWD_GROUPS
    gw = hpg * HEAD

    def body(q_ref, do_ref, k_ref, v_ref, fc_ref, lse_ref, dl_ref,
             dq_ref, dk_ref, dv_ref, dfc_ref, dfr_ref):
        g, j = pl.program_id(0), pl.program_id(1)
        tri = (lax.broadcasted_iota(jnp.int32, (blk, blk), 1) <= lax.broadcasted_iota(jnp.int32, (blk, blk), 0))

        @pl.when(j == 0)
        def _():
            dq_ref[...] = jnp.zeros_like(dq_ref)
            dfc_ref[...] = jnp.zeros_like(dfc_ref)

        for h in range(hpg):
            cs = slice(h * HEAD, (h + 1) * HEAD)
            cs2 = slice(2 * h * HEAD, 2 * (h + 1) * HEAD)
            csq = slice(2 * h * HEAD, (2 * h + 1) * HEAD)
            kj2 = k_ref[:, cs2]
            kj = k_ref[:, csq]
            vj = v_ref[:, cs]

            def tile(i, carry, masked):
                dk, dv, dfs = carry
                rs = pl.ds(pl.multiple_of(i * blk, blk), blk)
                qi = q_ref[rs, csq]
                doi = do_ref[rs, cs]
                bias = fc_ref[0, rs, h:h + 1] - lse_ref[0, rs, h:h + 1]
                p = jnp.exp(_bdot_raw(q_ref[rs, cs2], kj2, _NT) + bias)
                if masked:
                    p = jnp.where(tri, p, 0.0)
                ds = p * (_bdot_raw(doi, vj, _NT) - dl_ref[0, rs, h:h + 1])
                dsb = ds.astype(BF16)
                dq_ref[rs, cs] += _bdot_raw(dsb, kj, _NN)
                dfc_ref[0, rs, h:h + 1] += jnp.sum(ds, axis=1, keepdims=True)
                return (dk + _bdot_raw(dsb, qi, _TN), dv + _bdot_raw(p, doi, _TN),
                        dfs - jnp.sum(ds, axis=0, keepdims=True))

            init = (jnp.zeros((blk, HEAD), F32), jnp.zeros((blk, HEAD), F32), jnp.zeros((1, blk), F32))
            carry = tile(j, init, True)
            dk, dv, dfs = lax.fori_loop(j + 1, nq, lambda i, c: tile(i, c, False), carry)
            dk_ref[:, cs] = dk
            dv_ref[:, cs] = dv.astype(dv_ref.dtype)
            dfr_ref[0, 0, h:h + 1, :] = dfs

    by_group = lambda a: a.reshape(t, ATTN_BWD_GROUPS, hpg).transpose(1, 0, 2)
    once = pl.Buffered(1)
    stat = pl.BlockSpec((1, t, hpg), lambda g, j: (g, 0, 0), pipeline_mode=once)
    kv_blk = pl.BlockSpec((blk, gw), lambda g, j: (j, g))
    frow = pl.BlockSpec((1, 1, hpg, blk), lambda g, j: (g, j, 0, 0))
    dq, dk, dv, dfc, dfr = pl.pallas_call(
        body, name="fox_attn_bwd",
        grid=(ATTN_BWD_GROUPS, nq),
        in_specs=[pl.BlockSpec((t, 2 * gw), lambda g, j: (0, g), pipeline_mode=once),
                  pl.BlockSpec((t, gw), lambda g, j: (0, g), pipeline_mode=once),
                  pl.BlockSpec((blk, 2 * gw), lambda g, j: (j, g)), kv_blk, stat, stat, stat],
        out_specs=[pl.BlockSpec((t, gw), lambda g, j: (0, g)), kv_blk, kv_blk,
                   pl.BlockSpec((1, t, hpg), lambda g, j: (g, 0, 0)), frow],
        out_shape=[jax.ShapeDtypeStruct((t, width), F32), jax.ShapeDtypeStruct((t, width), F32),
                   jax.ShapeDtypeStruct((t, width), BF16), jax.ShapeDtypeStruct((ATTN_BWD_GROUPS, t, hpg), F32),
                   jax.ShapeDtypeStruct((ATTN_BWD_GROUPS, nq, hpg, blk), F32)],
        compiler_params=_cparams(dimension_semantics=("parallel", "arbitrary")),
    )(q, do, k, v, by_group(f_col), by_group(lse), by_group(delta))
    return (dq, dk, dv, dfc.transpose(1, 0, 2).reshape(t, nh),
            dfr.transpose(1, 0, 2, 3).reshape(nq, nh, blk))


SUBLANES = 8


def _shift_down(u, n):
    r = pltpu.roll(u, n, 0)
    row = lax.broadcasted_iota(jnp.int32, (SUBLANES, u.shape[1]), 0)
    return jnp.concatenate([jnp.where(row < n, 0.0, r[:SUBLANES]), r[SUBLANES:]], axis=0)


def _shift_up(u, n):
    t = u.shape[0]
    r = pltpu.roll(u, t - n, 0)
    row = lax.broadcasted_iota(jnp.int32, (SUBLANES, u.shape[1]), 0)
    return jnp.concatenate([r[:t - SUBLANES], jnp.where(row >= SUBLANES - n, 0.0, r[t - SUBLANES:])], axis=0)


def _convglu_specs(t):
    return [pl.BlockSpec((2, t, LANES), lambda j: (0, 0, j)),
            pl.BlockSpec((2, CONV_TAPS, LANES), lambda j: (0, 0, j)),
            pl.BlockSpec((2, 1, LANES), lambda j: (0, 0, j))]


def _convglu_fwd(u, cw, cb):
    _, t, fp = u.shape

    def body(u_ref, w_ref, b_ref, a_ref):
        c = []
        for hf in range(2):
            uv, w = u_ref[hf], w_ref[hf]
            c.append(w[0:1] * _shift_down(uv, 2) + w[1:2] * _shift_down(uv, 1) + w[2:3] * uv + b_ref[hf])
        a_ref[...] = (_silu(c[0]) * c[1]).astype(a_ref.dtype)

    return pl.pallas_call(
        body, name="convglu_fwd",
        grid=(fp // LANES,),
        in_specs=_convglu_specs(t),
        out_specs=pl.BlockSpec((t, LANES), lambda j: (0, j)),
        out_shape=jax.ShapeDtypeStruct((t, fp), BF16),
        compiler_params=_cparams(dimension_semantics=("parallel",)),
    )(u, cw, cb)


def _convglu_bwd(u, cw, cb, da):
    _, t, fp = u.shape

    def body(u_ref, w_ref, b_ref, da_ref, du_ref, dw_ref, db_ref):
        us, c = [], []
        for hf in range(2):
            uv, w = u_ref[hf], w_ref[hf]
            u1, u2 = _shift_down(uv, 1), _shift_down(uv, 2)
            us.append((uv, u1, u2))
            c.append(w[0:1] * u2 + w[1:2] * u1 + w[2:3] * uv + b_ref[hf])
        gc, vc = c
        sg = _sigmoid(gc)
        dav = da_ref[...].astype(F32)
        dcs = [dav * vc * (sg * (1.0 + gc * (1.0 - sg))), dav * (gc * sg)]
        for hf in range(2):
            dc, w = dcs[hf], w_ref[hf]
            uv, u1, u2 = us[hf]
            du = w[2:3] * dc + w[1:2] * _shift_up(dc, 1) + w[0:1] * _shift_up(dc, 2)
            du_ref[hf] = du.astype(du_ref.dtype)
            dw_ref[hf, 0:1, :] = jnp.sum(dc * u2, axis=0, keepdims=True)
            dw_ref[hf, 1:2, :] = jnp.sum(dc * u1, axis=0, keepdims=True)
            dw_ref[hf, 2:3, :] = jnp.sum(dc * uv, axis=0, keepdims=True)
            db_ref[hf] = jnp.sum(dc, axis=0, keepdims=True)

    specs = _convglu_specs(t)
    return pl.pallas_call(
        body, name="convglu_bwd",
        grid=(fp // LANES,),
        in_specs=specs + [pl.BlockSpec((t, LANES), lambda j: (0, j))],
        out_specs=specs,
        out_shape=[jax.ShapeDtypeStruct((2, t, fp), BF16), jax.ShapeDtypeStruct((2, CONV_TAPS, fp), F32),
                   jax.ShapeDtypeStruct((2, 1, fp), F32)],
        compiler_params=_cparams(dimension_semantics=("parallel",)),
    )(u, cw, cb, da)


def _local_step(x, target, mods, lb, small, get_w, put_g, *, tb=512, attn_blk=512):
    t, d = x.shape
    nh = d // HEAD
    nb = NDEV
    wts = {}
    vec = lambda *names: [mods[n] for n in names]

    def ffn_fwd(h2, l):
        u = _mm_wblk(h2, wts[f"up{l}"], F32, f"ffn{l}_up", gb=nb // 2, split=2, tm=512)
        a = _convglu_fwd(u, small[f"conv_w{l}"], small[f"conv_b{l}"])
        f = _mm(a, wts[f"down{l}"], "nn", F32, f"ffn{l}_down", tk=4096)
        return u, a, f

    def ffn_bwd(df, h2, u, a, l):
        da = _mm(df, wts[f"down{l}"], "nt", BF16, f"ffn{l}_down_dx", tn=1536)
        dwd = _mm(a, df, "tn", BF16, f"ffn{l}_down_dw", tm=1536, tk=1024)
        du, dcw, dcb = _convglu_bwd(u, small[f"conv_w{l}"], small[f"conv_b{l}"], da)
        dh2 = _mm_wblk_dx(du, wts[f"up{l}"], F32, f"ffn{l}_up_dx", k=d, gb=nb // 2, split=2, tm=1024)
        dwu = _mm_wblk_dw(h2, du, f"ffn{l}_up_dw", nb=nb, gb=1, split=2, tk=t)
        return dh2, dwu, dwd, dcw, dcb

    (h_a,) = _row_fwd(_f_mod, [(x, d, 0)], vec("sh1_0", "sc1_0"), [BF16], tb=tb, name="l0_mod1")
    wts.update(get_w("l0a", h_a))
    proj_a = _mm_wblk(h_a, wts["a_in"], F32, "a_in", gb=nb // 2)
    ypre, states = _hgrn2_fwd(proj_a, lb, small["a_norm_g"], tb)
    wts.update(get_w("l0b", ypre))
    y_a = _mm(ypre, wts["a_out"], "nn", F32, "a_out")
    x1, h2_0 = _row_fwd(_f_res_mod, [(x, d, 0), (y_a, d, 0)], vec("g1_0", "sh2_0", "sc2_0"), [F32, BF16],
                        tb=tb, name="l0_res_mod2")
    u0, a0, f0 = ffn_fwd(h2_0, 0)
    x2, h_kv, h_q = _row_fwd(_f_res_mod2, [(x1, d, 0), (f0, d, 0)],
                             vec("g2_0", "kv_sh", "kv_sc", "sh1_1", "sc1_1"), [F32, BF16, BF16],
                             tb=tb, name="l0_res_kvmod_qmod")
    wts.update(get_w("l1", h_kv))
    proj_kv = _mm(h_kv, wts["kv"], "nt", F32, "kv_proj")
    proj_f = _mm(h_kv, wts["kv_f"], "nt", F32, "kv_fproj")
    v_b = proj_kv[:, d:].astype(BF16)
    f_logit_t = proj_f[:, :nh].T
    f_bias = small["kv_b_f"].reshape(nh, 1)
    f_col = _fgate_fwd(f_logit_t, f_bias).T
    (k_n,) = _row_fwd(_f_knorm_aug, [(proj_kv, HEAD, 0)] + [(piece, 1, 0) for piece in _split3(-f_col)],
                      [small["k_norm_g"]], [BF16], nsub=nh, tb=tb, name="k_norm")
    proj_q = _mm_wblk(h_q, wts["b_q"], F32, "b_q", gb=nb)
    (q_n,) = _row_fwd(_f_qnorm_aug, [(proj_q, HEAD, 0)], [small["q_norm_g"]], [BF16], nsub=nh, tb=tb,
                      name="q_norm")
    o_att, lse = _attn_fwd(q_n, k_n, v_b, f_col, attn_blk)
    (z,) = _row_fwd(_f_outgate, [(o_att, HEAD, 0), (proj_q, HEAD, 1)], [], [BF16], nsub=nh, tb=tb, name="out_gate")
    y_b = _mm(z, wts["b_out"], "nn", F32, "b_out")
    x3, h2_1 = _row_fwd(_f_res_mod, [(x2, d, 0), (y_b, d, 0)], vec("g1_1", "sh2_1", "sc2_1"), [F32, BF16],
                        tb=tb, name="l1_res_mod2")
    u1, a1, f1 = ffn_fwd(h2_1, 1)
    loss, dx4, df1, dg2_1 = _loss_call(x3, f1, mods["g2_1"], target, tb)

    g = {}
    dmods = {"g2_1": dg2_1}
    dh2, g["up1"], g["down1"], g["conv_w1"], g["conv_b1"] = ffn_bwd(df1, h2_1, u1, a1, 1)
    (dx2, dy_b), (dmods["g1_1"], dmods["sh2_1"], dmods["sc2_1"]) = _row_bwd(
        _f_res_mod, [(x2, d, 0), (y_b, d, 0)], vec("g1_1", "sh2_1", "sc2_1"),
        [(dx4, d, 0), (dh2, d, 0)], [F32, BF16], tb=tb, name="l1_res_mod2_bwd")
    dz = _mm(dy_b, wts["b_out"], "nt", F32, "b_out_dx")
    g["b_out"] = _mm(z, dy_b, "tn", BF16, "b_out_dw", tk=1024)
    (do_att, dog), _ = _row_bwd(_f_outgate, [(o_att, HEAD, 0), (proj_q, HEAD, 1)], [], [(dz, HEAD, 0)],
                                [BF16, BF16], nsub=nh, tb=tb, name="out_gate_bwd")
    delta = _attn_delta(do_att, o_att, tb)
    dq_n, dk_n, dv, dfc_q, dfr_k = _attn_bwd(q_n, k_n, v_b, f_col, do_att, lse, delta, attn_blk)
    (dpq,), (g["q_norm_g"],) = _row_bwd(_f_qnorm, [(proj_q, HEAD, 0)], [small["q_norm_g"]],
                                        [(dq_n, HEAD, 0)], [BF16], nsub=nh, tb=tb, name="q_norm_bwd")
    dproj_q = jnp.concatenate([dpq, dog], axis=1)
    dh_q = _mm_wblk_dx(dproj_q, wts["b_q"], F32, "b_q_dx", k=d, gb=nb)
    g["b_q"] = _mm_wblk_dw(h_q, dproj_q, "b_q_dw", nb=nb, gb=nb // 4, tk=t)
    (dpk,), (g["k_norm_g"],) = _row_bwd(_f_knorm, [(proj_kv, HEAD, 0)], [small["k_norm_g"]],
                                        [(dk_n, HEAD, 0)], [BF16], nsub=nh, tb=tb, name="k_norm_bwd")
    dproj_kv = jnp.concatenate([dpk, dv], axis=1)
    df_t = dfc_q.T + dfr_k.transpose(1, 0, 2).reshape(nh, t)
    dflogit_t, g["kv_b_f"] = _fgate_bwd(f_logit_t, f_bias, df_t)
    dproj_f = jnp.pad(dflogit_t.T, ((0, 0), (0, LANES - nh))).astype(BF16)
    dh_kv = _mm(dproj_kv, wts["kv"], "nn", F32, "kv_proj_dx")
    dh_kv_f = _mm(dproj_f, wts["kv_f"], "nn", F32, "kv_fproj_dx")
    g["kv"] = _mm(dproj_kv, h_kv, "tn", BF16, "kv_proj_dw", tk=1024)
    g["kv_f"] = _mm(dproj_f, h_kv, "tn", F32, "kv_fproj_dw", tk=1024)
    sent = put_g("l1", {n: g.pop(n) for n in ("b_out", "b_q", "kv", "kv_f", "up1", "down1")})
    (dx1, df0), (dmods["g2_0"], dmods["kv_sh"], dmods["kv_sc"], dmods["sh1_1"], dmods["sc1_1"]) = _row_bwd(
        _f_res_mod2, [(x1, d, 0), (f0, d, 0)], [mods["g2_0"] + sent] + vec("kv_sh", "kv_sc", "sh1_1", "sc1_1"),
        [(dx2, d, 0), (dh_kv, d, 0), (dh_q, d, 0)], [F32, BF16], tb=tb, name="l0_res_kvmod_qmod_bwd",
        cot_add=(1, dh_kv_f))
    dh2, g["up0"], g["down0"], g["conv_w0"], g["conv_b0"] = ffn_bwd(df0, h2_0, u0, a0, 0)
    (dx0, dy_a), (dmods["g1_0"], dmods["sh2_0"], dmods["sc2_0"]) = _row_bwd(
        _f_res_mod, [(x, d, 0), (y_a, d, 0)], vec("g1_0", "sh2_0", "sc2_0"),
        [(dx1, d, 0), (dh2, d, 0)], [F32, BF16], tb=tb, name="l0_res_mod2_bwd")
    dypre = _mm(dy_a, wts["a_out"], "nt", BF16, "a_out_dx")
    g["a_out"] = _mm(ypre, dy_a, "tn", BF16, "a_out_dw", tk=1024)
    sent = put_g("l0b", {n: g.pop(n) for n in ("a_out", "up0", "down0")})
    dproj_a, dlb, g["a_norm_g"] = _hgrn2_bwd(proj_a, lb + sent, small["a_norm_g"], states, dypre, tb)
    dh_a = _mm_wblk_dx(dproj_a, wts["a_in"], F32, "a_in_dx", k=d, gb=nb, split=4, tm=512)
    put_g("l0a", {"a_in": _mm_wblk_dw(h_a, dproj_a, "a_in_dw", nb=nb, gb=1, split=4, tk=t)})
    (grad_x,), (dmods["sh1_0"], dmods["sc1_0"]) = _row_bwd(
        _f_mod, [(x, d, 0)], vec("sh1_0", "sc1_0"), [(dh_a, d, 0)], [F32], tb=tb, name="l0_mod1_bwd",
        add_to=(0, dx0))
    return loss, grad_x, dmods, dlb, g


def _position():
    return lax.axis_index("x"), lax.axis_index("y"), lax.axis_index("c")


def _hbm_specs(n):
    return [pl.BlockSpec(memory_space=pl.ANY)] * n


def _all_gather(arrs, name):
    n = len(arrs)

    def body(*refs):
        x_refs, out_refs = refs[:n], refs[n:2 * n]
        send_sems, recv_sems, local_sems = refs[2 * n:]
        x, y, cc = _position()
        me, sibling = (x, y, cc), (x, y, 1 - cc)
        chips = [(1 - x, y), (x, 1 - y), (1 - x, 1 - y)]

        def copy(a, k, block, to, src=None):
            slot = out_refs[a].at[4 * block[0] + 2 * block[1] + block[2]]
            return pltpu.make_async_remote_copy(
                src_ref=slot if src is None else src, dst_ref=slot,
                send_sem=send_sems.at[7 * a + k], recv_sem=recv_sems.at[7 * a + k],
                device_id=to, device_id_type=_MESH)

        local = [pltpu.make_async_copy(x_refs[a], out_refs[a].at[4 * x + 2 * y + cc], local_sems.at[a])
                 for a in range(n)]
        for cp in local:
            cp.start()
        first = []
        for a in range(n):
            first.append(copy(a, 0, me, sibling, src=x_refs[a]))
            first += [copy(a, 1 + j, me, (*chip, cc), src=x_refs[a]) for j, chip in enumerate(chips)]
        for cp in first:
            cp.start()
        passed = []
        for j, chip in enumerate(chips):
            for a in range(n):
                copy(a, 1 + j, (*chip, cc), me).wait_recv()
                fwd = copy(a, 4 + j, (*chip, cc), sibling)
                fwd.start()
                passed.append(fwd)
        for a in range(n):
            copy(a, 0, sibling, me).wait_recv()
        for j, chip in enumerate(chips):
            for a in range(n):
                copy(a, 4 + j, (*chip, 1 - cc), me).wait_recv()
        for cp in first + passed:
            cp.wait_send()
        for cp in local:
            cp.wait()

    return pl.pallas_call(
        body, name=name,
        out_shape=[jax.ShapeDtypeStruct((NDEV, *a.shape), a.dtype) for a in arrs],
        in_specs=_hbm_specs(n), out_specs=_hbm_specs(n),
        scratch_shapes=[pltpu.SemaphoreType.DMA((7 * n,)), pltpu.SemaphoreType.DMA((7 * n,)),
                        pltpu.SemaphoreType.DMA((n,))],
    )(*arrs)


_XCHG_EFFECT = pltpu.SideEffectType.DATAFLOW_SIDE_EFFECTING
ALL_PEERS = (1, 2, 3, 4, 5, 6, 7)
SAME_CORE = (2, 4, 6)


def _xchg_copies(src_refs, land_refs, send_sems, recv_sems, local_sems, scatter, rels):
    x, y, cc = _position()
    me = 4 * x + 2 * y + cc
    remote, local = [], []
    for a, (src, land) in enumerate(zip(src_refs, land_refs)):
        local.append(pltpu.make_async_copy(src.at[me] if scatter else src, land.at[me], local_sems.at[a]))
        for idx, rel in enumerate(rels):
            px = 1 - x if rel & 4 else x
            py = 1 - y if rel & 2 else y
            pc = 1 - cc if rel & 1 else cc
            k = len(rels) * a + idx
            remote.append(pltpu.make_async_remote_copy(
                src_ref=src.at[4 * px + 2 * py + pc] if scatter else src, dst_ref=land.at[me],
                send_sem=send_sems.at[k], recv_sem=recv_sems.at[k], device_id=(px, py, pc), device_id_type=_MESH))
    return remote, local


def _xchg_start(srcs, scatter, rels, after, name):
    n = len(srcs)
    lands = [lax.empty(s.shape if scatter else (NDEV, *s.shape), s.dtype) for s in srcs]

    def body(*refs):
        remote, local = _xchg_copies(refs[:n], refs[n:2 * n], *refs[2 * n + 1:2 * n + 4], scatter, rels)
        for cp in local + remote:
            cp.start()
        token = refs[-1]
        token[...] = jnp.zeros_like(token)

    hbm = pl.BlockSpec(memory_space=pltpu.HBM)
    sem = pl.BlockSpec(memory_space=pltpu.SEMAPHORE)
    out = pl.pallas_call(
        body, name=name,
        out_shape=(pltpu.SemaphoreType.DMA((len(rels) * n,)), pltpu.SemaphoreType.DMA((len(rels) * n,)),
                   pltpu.SemaphoreType.DMA((n,)),
                   *[pltpu.HBM(a.shape, a.dtype) for a in srcs + lands], jax.ShapeDtypeStruct((8, LANES), F32)),
        in_specs=[hbm] * (2 * n) + [pl.BlockSpec(memory_space=pl.ANY)],
        out_specs=(sem, sem, sem, *[hbm] * (2 * n), pl.BlockSpec(memory_space=pltpu.VMEM)),
        input_output_aliases={i: 3 + i for i in range(2 * n)},
        compiler_params=pltpu.CompilerParams(has_side_effects=_XCHG_EFFECT),
    )(*[pltpu.with_memory_space_constraint(a, pltpu.HBM) for a in srcs + lands], after)
    return out[:-1], out[-1][0, 0]


def _xchg_wait(handles, after, scatter, rels, name):
    n = (len(handles) - 3) // 2

    def body(*refs):
        remote, local = _xchg_copies(refs[:n], refs[n:2 * n], *refs[2 * n:2 * n + 3], scatter, rels)
        for cp in remote:
            cp.wait_send()
            cp.wait_recv()
        for cp in local:
            cp.wait()

    hbm = pl.BlockSpec(memory_space=pltpu.HBM)
    sem = pl.BlockSpec(memory_space=pltpu.SEMAPHORE)
    thru = list(handles[3:])
    out = pl.pallas_call(
        body, name=name,
        out_shape=tuple(pltpu.HBM(a.shape, a.dtype) for a in thru),
        in_specs=[hbm] * (2 * n) + [sem, sem, sem, pl.BlockSpec(memory_space=pl.ANY)],
        out_specs=tuple([hbm] * (2 * n)),
        input_output_aliases={i: i for i in range(2 * n)},
        compiler_params=pltpu.CompilerParams(has_side_effects=_XCHG_EFFECT),
    )(*thru, *handles[:3], after)
    return list(out[n:])


def _sibling_forward(lands, name):
    n = len(lands)

    def body(*refs):
        land_refs = refs[n:2 * n]
        send_sems, recv_sems = refs[2 * n:]
        x, y, cc = _position()

        def copy(a, q, core):
            slot = land_refs[a].at[2 * q + core]
            return pltpu.make_async_remote_copy(
                src_ref=slot, dst_ref=slot, send_sem=send_sems.at[NCHIP * a + q], recv_sem=recv_sems.at[NCHIP * a + q],
                device_id=(x, y, 1 - cc), device_id_type=_MESH)

        sends = [copy(a, q, cc) for a in range(n) for q in range(NCHIP)]
        for cp in sends:
            cp.start()
        for a in range(n):
            for q in range(NCHIP):
                copy(a, q, 1 - cc).wait_recv()
        for cp in sends:
            cp.wait_send()

    return pl.pallas_call(
        body, name=name,
        out_shape=[jax.ShapeDtypeStruct(a.shape, a.dtype) for a in lands],
        in_specs=_hbm_specs(n), out_specs=_hbm_specs(n),
        input_output_aliases={i: i for i in range(n)},
        scratch_shapes=[pltpu.SemaphoreType.DMA((NCHIP * n,)), pltpu.SemaphoreType.DMA((NCHIP * n,))],
    )(*lands)


def _slab_sum(slabs, name, tr=None):
    n, r, c = slabs.shape
    tr = r if tr is None else tr

    def body(s_ref, o_ref):
        acc = s_ref[0].astype(F32)
        for q in range(1, n):
            acc = acc + s_ref[q].astype(F32)
        o_ref[...] = acc

    return pl.pallas_call(body, name=name, grid=(r // tr,),
                          in_specs=[pl.BlockSpec((n, tr, c), lambda i: (0, i, 0))],
                          out_specs=pl.BlockSpec((tr, c), lambda i: (i, 0)),
                          out_shape=jax.ShapeDtypeStruct((r, c), F32),
                          compiler_params=_cparams(dimension_semantics=("parallel",)))(slabs)


def _ada_fwd(c_all, ada_w, kv_ada_w, logits):
    rows, d = c_all.shape
    n0, nkv = ada_w.shape[2], kv_ada_w.shape[1]

    def body(c_ref, w_ref, kw_ref, lg_ref, part_ref, cact_ref, lb_ref):
        ca = _silu(c_ref[...])
        cact_ref[...] = ca
        part_ref[:, 0:n0] = _bdot_raw(ca, w_ref[0], _NN)
        part_ref[:, n0:2 * n0] = _bdot_raw(ca, w_ref[1], _NN)
        part_ref[:, 2 * n0:2 * n0 + nkv] = _bdot_raw(ca, kw_ref[...], _NN)
        lb_ref[...] = _sigmoid(lg_ref[0:1, :] - lg_ref[1:2, :])

    vm = pl.BlockSpec(memory_space=pltpu.VMEM)
    return pl.pallas_call(
        body, name="ada_fwd", in_specs=[vm, vm, vm, vm], out_specs=[vm, vm, vm],
        out_shape=[jax.ShapeDtypeStruct((rows, 2 * n0 + nkv), F32), jax.ShapeDtypeStruct((rows, d), F32),
                   jax.ShapeDtypeStruct((1, d), F32)],
        compiler_params=_cparams(),
    )(c_all, ada_w, kv_ada_w, logits)


def _ada_bwd(c_act, dm0, dm1, dkv, lb, dlb):
    rows, d = c_act.shape

    def body(c_ref, d0_ref, d1_ref, dk_ref, lb_ref, dlb_ref, dw_ref, dkw_ref, dlg_ref):
        ca = c_ref[...]
        dw_ref[0] = _bdot_raw(ca, d0_ref[...], _TN)
        dw_ref[1] = _bdot_raw(ca, d1_ref[...], _TN)
        dkw_ref[...] = _bdot_raw(ca, dk_ref[...], _TN)
        lbv = lb_ref[...]
        dl0 = dlb_ref[...] * lbv * (1.0 - lbv)
        dlg_ref[0:1, :] = dl0
        dlg_ref[1:2, :] = -dl0

    vm = pl.BlockSpec(memory_space=pltpu.VMEM)
    return pl.pallas_call(
        body, name="ada_bwd", in_specs=[vm] * 6, out_specs=[vm, vm, vm],
        out_shape=[jax.ShapeDtypeStruct((2, d, dm0.shape[1]), F32), jax.ShapeDtypeStruct((d, dkv.shape[1]), F32),
                   jax.ShapeDtypeStruct((2, d), F32)],
        compiler_params=_cparams(),
    )(c_act, dm0, dm1, dkv, lb, dlb)


def _adamw(w, g, m, v, name, tr=512, after=None):
    r, c = w.shape
    tr = _divisor_tile(r, tr, unit=8)
    c1 = 1.0 - ADAM_B1 ** ADAM_STEP
    c2 = 1.0 - ADAM_B2 ** ADAM_STEP
    deps = [] if after is None else [after]

    def body(w_ref, g_ref, m_ref, v_ref, *rest):
        d_ref, mo_ref, vo_ref = rest[len(deps):]
        gv = g_ref[...]
        mn = ADAM_B1 * m_ref[...] + (1.0 - ADAM_B1) * gv
        vn = ADAM_B2 * v_ref[...] + (1.0 - ADAM_B2) * (gv * gv)
        d_ref[...] = -ADAM_LR * ((mn / c1) / (jnp.sqrt(vn / c2) + ADAM_EPS) + ADAM_WD * w_ref[...])
        mo_ref[...] = mn
        vo_ref[...] = vn

    spec = pl.BlockSpec((tr, c), lambda i: (i, 0))
    out = jax.ShapeDtypeStruct((r, c), F32)
    return pl.pallas_call(body, name=name, grid=(r // tr,),
                          in_specs=[spec] * 4 + [pl.BlockSpec(a.shape, lambda i: (0, 0)) for a in deps],
                          out_specs=[spec] * 3, out_shape=[out, out, out],
                          compiler_params=_cparams(dimension_semantics=("parallel",)))(w, g, m, v, *deps)


def _pad_rows(a, rows):
    return jnp.pad(a, ((0, rows - a.shape[0]), (0, 0)))


def _pack_small(parts, lanes=LANES, row_unit=8):
    flat = jnp.concatenate([p.reshape(-1).astype(F32) for p in parts])
    rows = _round_up(-(-flat.shape[0] // lanes), row_unit)
    return jnp.pad(flat, (0, rows * lanes - flat.shape[0])).reshape(rows, lanes)


def _unpack_small(flat, shapes):
    out, off = [], 0
    for s in shapes:
        n = 1
        for k in s:
            n *= k
        out.append(flat[off:off + n].reshape(s))
        off += n
    return out


def _pad_shard_cols(a, n_loc, n_pad):
    lead = a.shape[:-1]
    a = a.reshape(*lead, NDEV, n_loc)
    a = jnp.pad(a, [(0, 0)] * (len(lead) + 1) + [(0, n_pad - n_loc)])
    return a.reshape(*lead, NDEV * n_pad)


def _unpad_shard_cols(a, n_loc, n_pad):
    lead = a.shape[:-1]
    return a.reshape(*lead, NDEV, n_pad)[..., :n_loc].reshape(*lead, NDEV * n_loc)


def kernel(x, c, ada_w, ada_b, a_w_in, a_lb_logits, a_norm_g, a_w_out, kv_ada_w, kv_ada_b, kv_w, kv_b_f, k_norm_g, b_w_q, q_norm_g, b_w_out, ffn_w_up, ffn_conv_w, ffn_conv_b, ffn_w_down, loss_target, m_ada_w, m_ada_b, m_a_w_in, m_a_lb_logits, m_a_norm_g, m_a_w_out, m_kv_ada_w, m_kv_ada_b, m_kv_w, m_kv_b_f, m_k_norm_g, m_b_w_q, m_q_norm_g, m_b_w_out, m_ffn_w_up, m_ffn_conv_w, m_ffn_conv_b, m_ffn_w_down, v_ada_w, v_ada_b, v_a_w_in, v_a_lb_logits, v_a_norm_g, v_a_w_out, v_kv_ada_w, v_kv_ada_b, v_kv_w, v_kv_b_f, v_k_norm_g, v_b_w_q, v_q_norm_g, v_b_w_out, v_ffn_w_up, v_ffn_conv_w, v_ffn_conv_b, v_ffn_w_down):
    t, d = x.shape[1], x.shape[2]
    nh = d // HEAD
    ncw = ffn_w_up.shape[2]
    ncp = _round_up(ncw, LANES)
    two_f = ncw * NDEV
    ff = two_f // 2
    fp = ncp * NDEV // 2
    rd = ffn_w_down.shape[1]
    me = 4 * lax.axis_index("x") + 2 * lax.axis_index("y") + lax.axis_index("c")
    weights = dict(ada_w=ada_w, ada_b=ada_b, a_w_in=a_w_in, a_lb_logits=a_lb_logits, a_norm_g=a_norm_g,
                   a_w_out=a_w_out, kv_ada_w=kv_ada_w, kv_ada_b=kv_ada_b, kv_w=kv_w, kv_b_f=kv_b_f,
                   k_norm_g=k_norm_g, b_w_q=b_w_q, q_norm_g=q_norm_g, b_w_out=b_w_out, ffn_w_up=ffn_w_up,
                   ffn_conv_w=ffn_conv_w, ffn_conv_b=ffn_conv_b, ffn_w_down=ffn_w_down)
    m_in = dict(ada_w=m_ada_w, ada_b=m_ada_b, a_w_in=m_a_w_in, a_lb_logits=m_a_lb_logits, a_norm_g=m_a_norm_g,
                a_w_out=m_a_w_out, kv_ada_w=m_kv_ada_w, kv_ada_b=m_kv_ada_b, kv_w=m_kv_w, kv_b_f=m_kv_b_f,
                k_norm_g=m_k_norm_g, b_w_q=m_b_w_q, q_norm_g=m_q_norm_g, b_w_out=m_b_w_out, ffn_w_up=m_ffn_w_up,
                ffn_conv_w=m_ffn_conv_w, ffn_conv_b=m_ffn_conv_b, ffn_w_down=m_ffn_w_down)
    v_in = dict(ada_w=v_ada_w, ada_b=v_ada_b, a_w_in=v_a_w_in, a_lb_logits=v_a_lb_logits, a_norm_g=v_a_norm_g,
                a_w_out=v_a_w_out, kv_ada_w=v_kv_ada_w, kv_ada_b=v_kv_ada_b, kv_w=v_kv_w, kv_b_f=v_kv_b_f,
                k_norm_g=v_k_norm_g, b_w_q=v_b_w_q, q_norm_g=v_q_norm_g, b_w_out=v_b_w_out, ffn_w_up=v_ffn_w_up,
                ffn_conv_w=v_ffn_conv_w, ffn_conv_b=v_ffn_conv_b, ffn_w_down=v_ffn_w_down)
    order = list(weights)

    up_loc = jnp.pad(ffn_w_up, ((0, 0), (0, 0), (0, ncp - ncw))).astype(BF16)
    down_loc = ffn_w_down.astype(BF16)
    gather_names = {"l0b": ["a_out", "up0", "down0"], "l1": ["kv", "b_q", "b_out", "up1", "down1"]}
    shards = {"a_out": a_w_out[0].astype(BF16), "up0": up_loc[0], "down0": down_loc[0], "kv": kv_w.T.astype(BF16),
              "b_q": b_w_q[0].astype(BF16), "b_out": b_w_out[0].astype(BF16), "up1": up_loc[1],
              "down1": down_loc[1]}
    pre = _pack_small([c, a_lb_logits, ffn_conv_w])
    a_in_all, pre_all = _all_gather([a_w_in[0].astype(BF16), pre], "gather_a_w_in_and_small_inputs")
    pre_all = pre_all.reshape(NDEV, -1)
    c_all = pre_all[:, :d]
    logits = pre_all[:, d:d + 2 * HEAD].reshape(NDEV, 2, HEAD).transpose(1, 0, 2).reshape(2, d)
    conv_w_full = pre_all[:, d + 2 * HEAD:d + 2 * HEAD + 2 * CONV_TAPS * ncw]
    conv_w_full = conv_w_full.reshape(NDEV, 2, CONV_TAPS, ncw).transpose(1, 2, 0, 3).reshape(2, CONV_TAPS, two_f)

    part, c_act, lb = _ada_fwd(_pad_rows(c_all, 2 * NDEV), ada_w, kv_ada_w, logits)
    (part_all,) = _all_gather([part[:NDEV]], "gather_adaln")
    mine = lax.dynamic_index_in_dim(part_all, me, axis=1, keepdims=False)
    n0, nkv = ada_w.shape[2], kv_ada_w.shape[1]
    mod_names = ["sh1", "sc1", "g1", "sh2", "sc2", "g2"]
    mods = {}
    for l in range(2):
        row = mine[:, l * n0:(l + 1) * n0].reshape(-1) + ada_b[l]
        for k, nm in enumerate(mod_names):
            mods[f"{nm}_{l}"] = row[k * d:(k + 1) * d].reshape(1, d)
    kvrow = mine[:, 2 * n0:2 * n0 + nkv].reshape(-1) + kv_ada_b
    mods["kv_sh"], mods["kv_sc"] = kvrow[:d].reshape(1, d), kvrow[d:].reshape(1, d)

    in_flight = {}

    def start_gather(grp, dep):
        srcs = [shards[n] for n in gather_names[grp]]
        in_flight[grp], started = _xchg_start(srcs, False, SAME_CORE, dep, f"gather_{grp}_start")
        return started

    zero = start_gather("l0b", part_all)
    mods["sh1_0"] = mods["sh1_0"] + zero

    small = {"a_norm_g": a_norm_g, "k_norm_g": k_norm_g.reshape(1, HEAD), "q_norm_g": q_norm_g, "kv_b_f": kv_b_f}
    for l in range(2):
        small[f"conv_w{l}"] = _pad_shard_cols(conv_w_full[l], ncw, ncp).reshape(CONV_TAPS, 2, fp).transpose(1, 0, 2)
        small[f"conv_b{l}"] = _pad_shard_cols(ffn_conv_b[l], ncw, ncp).reshape(2, 1, fp)

    def get_w(grp, after):
        if grp == "l0a":
            return {"a_in": a_in_all}
        arrived = _xchg_wait(in_flight[grp], after, False, SAME_CORE, f"gather_{grp}_wait")
        full = list(_sibling_forward(arrived, f"gather_{grp}_to_sibling"))
        if grp == "l0b":
            started = start_gather("l1", full[0])
            full[0] = full[0] + started.astype(full[0].dtype)
        got = dict(zip(gather_names[grp], full))
        out = {}
        for n, a in got.items():
            if n in ("a_out", "b_out"):
                out[n] = a.reshape(d, d)
            elif n in ("down0", "down1"):
                dn = a.reshape(NCHIP, ff // NCHIP, d)
                out[n] = jnp.pad(dn, ((0, 0), (0, ncp - ncw), (0, 0))).reshape(fp, d)
            elif n == "kv":
                kv_t = a.reshape(NDEV * kv_w.shape[1], d)
                out["kv"] = kv_t[:2 * d]
                out["kv_f"] = jnp.pad(kv_t[2 * d:], ((0, LANES - nh), (0, 0)))
            else:
                out[n] = a
        return out

    scatter_flight, g_last = {}, {}

    def put_g(grp, gr):
        if grp == "l0a":
            g_last.update(gr)
            return zero
        if grp == "l1":
            g_kvw = jnp.concatenate([gr["kv"], gr["kv_f"][:nh].astype(BF16)], axis=0)
            arrs = {"kv_w": g_kvw.reshape(NDEV, kv_w.shape[1], d), "b_w_q": gr["b_q"],
                    "b_w_out": gr["b_out"].reshape(NDEV, d // NDEV, d), "up1": gr["up1"],
                    "down1": gr["down1"].reshape(NCHIP, ncp, d)[:, :ncw].reshape(NDEV, rd, d)}
        else:
            arrs = {"a_w_out": gr["a_out"].reshape(NDEV, d // NDEV, d), "up0": gr["up0"],
                    "down0": gr["down0"].reshape(NCHIP, ncp, d)[:, :ncw].reshape(NDEV, rd, d)}
        srcs = list(arrs.values())
        handles, sent = _xchg_start(srcs, True, ALL_PEERS, srcs[0], f"scatter_{grp}_start")
        scatter_flight[grp] = (list(arrs), handles)
        return sent

    loss_v, grad_x, dmods, dlb, g = _local_step(x[0], loss_target[0], mods, lb, small, get_w, put_g)

    g_sum = {}
    for grp in ("l1", "l0b"):
        names, handles = scatter_flight[grp]
        for nm, a in zip(names, _xchg_wait(handles, grad_x, True, ALL_PEERS, f"scatter_{grp}_wait")):
            g_sum[nm] = _slab_sum(a, f"rs_slab_sum_{nm}")

    def conv_w_grad(a):
        return _unpad_shard_cols(a.transpose(1, 0, 2).reshape(CONV_TAPS, 2 * fp), ncw, ncp)

    def conv_b_grad(a):
        return _unpad_shard_cols(a.reshape(2 * fp), ncw, ncp)

    dmod_vec = [dmods[f"{nm}_{l}"] for l in range(2) for nm in mod_names] + [dmods["kv_sh"], dmods["kv_sc"]]
    post = _pack_small(dmod_vec + [dlb, g["a_norm_g"], g["k_norm_g"], g["q_norm_g"],
                                   jnp.pad(g["kv_b_f"].reshape(-1), (0, LANES - nh)),
                                   conv_w_grad(g["conv_w0"]), conv_w_grad(g["conv_w1"]),
                                   conv_b_grad(g["conv_b0"]), conv_b_grad(g["conv_b1"]), loss_v])
    (post_all,) = _all_gather([post], "gather_small_grads")
    a_in_flight, a_in_sent = _xchg_start([g_last["a_in"]], True, ALL_PEERS, post_all, "scatter_l0a_start")
    a_in_sent = a_in_sent.reshape(1, 1)
    tot = _slab_sum(post_all, "small_grad_sum").reshape(-1)
    nmod = 14 * d
    (t_mod, t_lb, t_ang, t_kng, t_qng, t_bf, t_cw, t_cb, t_loss) = _unpack_small(
        tot, [(nmod,), (1, d), (1, HEAD), (HEAD,), (1, HEAD), (LANES,), (2, CONV_TAPS, two_f), (2, two_f),
              (LANES,)])
    loss = t_loss[0]
    dm_all = post_all.reshape(NDEV, -1)[:, :nmod]
    dm0 = lax.dynamic_slice_in_dim(dm_all[:, :6 * d], me * n0, n0, axis=1)
    dm1 = lax.dynamic_slice_in_dim(dm_all[:, 6 * d:12 * d], me * n0, n0, axis=1)
    dkv = lax.dynamic_slice_in_dim(dm_all[:, 12 * d:], me * nkv, nkv, axis=1)
    g_ada_w, g_kv_ada_w, g_logits = _ada_bwd(c_act, _pad_rows(dm0, 2 * NDEV), _pad_rows(dm1, 2 * NDEV),
                                              _pad_rows(dkv, 2 * NDEV), lb, t_lb)

    grads = {
        "ada_w": g_ada_w,
        "ada_b": t_mod[:12 * d].reshape(2, 6 * d),
        "a_lb_logits": lax.dynamic_slice_in_dim(g_logits, me * HEAD, HEAD, axis=1),
        "a_norm_g": t_ang,
        "a_w_out": g_sum["a_w_out"].reshape(a_w_out.shape),
        "kv_ada_w": g_kv_ada_w,
        "kv_ada_b": t_mod[12 * d:],
        "kv_w": g_sum["kv_w"].T,
        "kv_b_f": t_bf[:nh],
        "k_norm_g": t_kng,
        "b_w_q": g_sum["b_w_q"].reshape(b_w_q.shape),
        "q_norm_g": t_qng,
        "b_w_out": g_sum["b_w_out"].reshape(b_w_out.shape),
        "ffn_w_up": jnp.stack([g_sum["up0"][:, :ncw], g_sum["up1"][:, :ncw]]),
        "ffn_conv_w": lax.dynamic_slice_in_dim(t_cw, me * ncw, ncw, axis=2),
        "ffn_conv_b": t_cb,
        "ffn_w_down": jnp.stack([g_sum["down0"], g_sum["down1"]]),
    }

    big_adam = ["ada_w", "a_w_out", "kv_ada_w", "kv_w", "b_w_q", "b_w_out", "ffn_w_up", "ffn_w_down", "a_w_in"]
    small_adam = [n for n in order if n not in big_adam]
    delta, new_m, new_v = {}, {}, {}
    packs = [_pack_small([src[n] for n in small_adam]) for src in (weights, grads, m_in, v_in)]
    outs = _adamw(*packs, "adamw_small", tr=packs[0].shape[0])
    shapes = [weights[n].shape for n in small_adam]
    for dst, o in zip((delta, new_m, new_v), outs):
        for n, a in zip(small_adam, _unpack_small(o.reshape(-1), shapes)):
            dst[n] = a
    for n in big_adam:
        if n == "a_w_in":
            (landed,) = _xchg_wait(a_in_flight, new_v["ffn_w_down"], True, ALL_PEERS, "scatter_l0a_wait")
            grads[n] = _slab_sum(landed, "rs_slab_sum_a_w_in").reshape(a_w_in.shape)
        shp = weights[n].shape
        two_d = lambda a: a.reshape(-1, shp[-1])
        dl, mn, vn = _adamw(two_d(weights[n]), two_d(grads[n]), two_d(m_in[n]), two_d(v_in[n]), f"adamw_{n}",
                            after=a_in_sent)
        delta[n], new_m[n], new_v[n] = dl.reshape(shp), mn.reshape(shp), vn.reshape(shp)

    return (loss, grad_x.reshape(x.shape), *[grads[n] for n in order], *[delta[n] for n in order],
            *[new_m[n] for n in order], *[new_v[n] for n in order])
```

```python
import functools

import jax
import jax.numpy as jnp
from jax import lax
from jax.experimental import pallas as pl
from jax.experimental.pallas import tpu as pltpu

F32 = jnp.float32
BF16 = jnp.bfloat16

NDEV = 8
NCHIP = 4
HEAD = 128
A_CHUNK = 64
CONV_TAPS = 3
EPS = 1e-6
NEG_INF = -1e30
LANES = 128
VMEM_LIMIT = 48 * 1024 * 1024

ADAM_LR = 0.001
ADAM_B1 = 0.9
ADAM_B2 = 0.999
ADAM_EPS = 1e-08
ADAM_WD = 0.01
ADAM_STEP = 10

_NN = (((1,), (0,)), ((), ()))
_NT = (((1,), (1,)), ((), ()))
_TN = (((0,), (0,)), ((), ()))
_MESH = pl.DeviceIdType.MESH


def _cparams(**kw):
    return pltpu.CompilerParams(vmem_limit_bytes=VMEM_LIMIT, **kw)


def _divisor_tile(n, pref, unit=LANES):
    if n <= pref:
        return n
    best = None
    for t in range(unit, pref + 1, unit):
        if n % t == 0:
            best = t
    assert best is not None, (n, pref)
    return best


def _round_up(n, unit):
    return -(-n // unit) * unit


def _bdot_raw(a, b, dims):
    return lax.dot_general(a.astype(BF16), b.astype(BF16), dims, preferred_element_type=F32)


@jax.custom_vjp
def _dot_nn(a, b):
    return _bdot_raw(a, b, _NN)


@jax.custom_vjp
def _dot_nt(a, b):
    return _bdot_raw(a, b, _NT)


@jax.custom_vjp
def _dot_tn(a, b):
    return _bdot_raw(a, b, _TN)


_dot_nn.defvjp(lambda a, b: (_bdot_raw(a, b, _NN), (a, b)),
               lambda r, g: (_dot_nt(g, r[1]), _dot_tn(r[0], g)))
_dot_nt.defvjp(lambda a, b: (_bdot_raw(a, b, _NT), (a, b)),
               lambda r, g: (_dot_nn(g, r[1]), _dot_tn(g, r[0])))
_dot_tn.defvjp(lambda a, b: (_bdot_raw(a, b, _TN), (a, b)),
               lambda r, g: (_dot_nt(r[1], g), _dot_nn(r[0], g)))


def _f32dot(a, b):
    return lax.dot_general(a, b, _NN, precision=lax.Precision.HIGHEST, preferred_element_type=F32)


def _sigmoid(x):
    return jax.nn.sigmoid(x)


def _silu(x):
    return x * jax.nn.sigmoid(x)


def _rms(x):
    return x * lax.rsqrt(jnp.mean(x * x, axis=-1, keepdims=True) + EPS)


def _modulate(x, sh, sc):
    return _rms(x) * (1.0 + sc) + sh


def _mm_call(a, b, dims, a_spec, b_spec, o_spec, o_shape, grid, acc_tile, name):
    nk = grid[2]

    def body(a_ref, b_ref, o_ref, *acc):
        p = lax.dot_general(a_ref[...].astype(BF16), b_ref[...].astype(BF16), dims,
                            preferred_element_type=F32)
        if nk == 1:
            o_ref[...] = p.astype(o_ref.dtype)
        else:
            kk = pl.program_id(2)

            @pl.when(kk == 0)
            def _():
                acc[0][...] = p

            @pl.when(kk > 0)
            def _():
                acc[0][...] += p

            @pl.when(kk == nk - 1)
            def _():
                o_ref[...] = acc[0][...].astype(o_ref.dtype)

    return pl.pallas_call(
        body, name=name, grid=grid, in_specs=[a_spec, b_spec], out_specs=o_spec, out_shape=o_shape,
        scratch_shapes=[pltpu.VMEM(acc_tile, F32)] if nk > 1 else [],
        compiler_params=_cparams(dimension_semantics=("parallel", "parallel", "arbitrary")),
    )(a, b)


def _mm(a, b, mode, out_dtype, name, tm=1024, tn=1024, tk=2048):
    if mode == "nn":
        (m, k), (k2, n) = a.shape, b.shape
    elif mode == "nt":
        (m, k), (n, k2) = a.shape, b.shape
    else:
        (k, m), (k2, n) = a.shape, b.shape
    assert k == k2, (a.shape, b.shape, mode)
    tm, tn, tk = _divisor_tile(m, tm), _divisor_tile(n, tn), _divisor_tile(k, tk)
    if mode == "tn":
        a_spec = pl.BlockSpec((tk, tm), lambda i, j, kk: (kk, i))
    else:
        a_spec = pl.BlockSpec((tm, tk), lambda i, j, kk: (i, kk))
    if mode == "nt":
        b_spec = pl.BlockSpec((tn, tk), lambda i, j, kk: (j, kk))
    else:
        b_spec = pl.BlockSpec((tk, tn), lambda i, j, kk: (kk, j))
    return _mm_call(a, b, {"nn": _NN, "nt": _NT, "tn": _TN}[mode], a_spec, b_spec,
                    pl.BlockSpec((tm, tn), lambda i, j, kk: (i, j)), jax.ShapeDtypeStruct((m, n), out_dtype),
                    (m // tm, n // tn, k // tk), (tm, tn), name)


def _wblk_act_spec(rows, gb, nl, split, nb, row_axis, blk_axis):
    if split == 1:
        return pl.BlockSpec((rows, gb * nl), lambda *g: (g[row_axis], g[blk_axis]))
    groups = nb // split // gb
    return pl.BlockSpec((None, rows, gb * nl),
                        lambda *g: (g[blk_axis] // groups, g[row_axis], g[blk_axis] % groups))


def _mm_wblk(a, wb, out_dtype, name, *, gb, row_off=0, split=1, tm=1024):
    m, k = a.shape
    nb, _, nl = wb.shape
    assert (nb // split) % gb == 0
    tm = _divisor_tile(m, tm)

    def body(a_ref, b_ref, o_ref):
        av = a_ref[...].astype(BF16)
        for s in range(gb):
            o_ref[:, s * nl:(s + 1) * nl] = lax.dot_general(
                av, b_ref[s].astype(BF16), _NN, preferred_element_type=F32).astype(o_ref.dtype)

    o_shape = (m, nb * nl) if split == 1 else (split, m, nb // split * nl)
    return pl.pallas_call(
        body, name=name, grid=(nb // gb, m // tm),
        in_specs=[pl.BlockSpec((tm, k), lambda j, i: (i, 0)),
                  pl.BlockSpec((gb, k, nl), lambda j, i: (j, row_off, 0))],
        out_specs=_wblk_act_spec(tm, gb, nl, split, nb, 1, 0),
        out_shape=jax.ShapeDtypeStruct(o_shape, out_dtype),
        compiler_params=_cparams(dimension_semantics=("parallel", "parallel")),
    )(a, wb)


def _mm_wblk_dx(dy, wb, out_dtype, name, *, k, gb, row_off=0, split=1, tm=1024):
    nb, _, nl = wb.shape
    m = dy.shape[-2]
    tm = _divisor_tile(m, tm)
    nk = nb // gb
    per = nb // split
    whole = split > 1 and gb == nb
    assert whole or per % gb == 0

    def body(a_ref, b_ref, o_ref, *acc):
        p = None
        for s in range(gb):
            a_blk = a_ref[s // per, :, (s % per) * nl:(s % per + 1) * nl] if whole else a_ref[:, s * nl:(s + 1) * nl]
            q = lax.dot_general(a_blk.astype(BF16), b_ref[s].astype(BF16), _NT, preferred_element_type=F32)
            p = q if p is None else p + q
        if nk == 1:
            o_ref[...] = p.astype(o_ref.dtype)
        else:
            kk = pl.program_id(1)

            @pl.when(kk == 0)
            def _():
                acc[0][...] = p

            @pl.when(kk > 0)
            def _():
                acc[0][...] += p

            @pl.when(kk == nk - 1)
            def _():
                o_ref[...] = acc[0][...].astype(o_ref.dtype)

    return pl.pallas_call(
        body, name=name, grid=(m // tm, nk),
        in_specs=[pl.BlockSpec((split, tm, per * nl), lambda i, kk: (0, i, 0)) if whole
                  else _wblk_act_spec(tm, gb, nl, split, nb, 0, 1),
                  pl.BlockSpec((gb, k, nl), lambda i, kk: (kk, row_off, 0))],
        out_specs=pl.BlockSpec((tm, k), lambda i, kk: (i, 0)),
        out_shape=jax.ShapeDtypeStruct((m, k), out_dtype),
        scratch_shapes=[pltpu.VMEM((tm, k), F32)] if nk > 1 else [],
        compiler_params=_cparams(dimension_semantics=("parallel", "arbitrary")),
    )(dy, wb)


def _mm_wblk_dw(x, dy, name, *, nb, gb, split=1, tk=1024):
    t, k = x.shape
    assert (nb // split) % gb == 0
    nl = dy.shape[-1] * split // nb
    tk = _divisor_tile(t, tk)
    nk = t // tk

    def body(a_ref, b_ref, o_ref, *acc):
        kk = pl.program_id(1)
        av = a_ref[...].astype(BF16)
        for s in range(gb):
            p = lax.dot_general(av, b_ref[:, s * nl:(s + 1) * nl].astype(BF16), _TN, preferred_element_type=F32)
            if nk == 1:
                o_ref[s] = p.astype(o_ref.dtype)
                continue

            @pl.when(kk == 0)
            def _():
                acc[0][s] = p

            @pl.when(kk > 0)
            def _():
                acc[0][s] += p

        if nk > 1:
            @pl.when(kk == nk - 1)
            def _():
                o_ref[...] = acc[0][...].astype(o_ref.dtype)

    return pl.pallas_call(
        body, name=name, grid=(nb // gb, nk),
        in_specs=[pl.BlockSpec((tk, k), lambda j, kk: (kk, 0)), _wblk_act_spec(tk, gb, nl, split, nb, 1, 0)],
        out_specs=pl.BlockSpec((gb, k, nl), lambda j, kk: (j, 0, 0)),
        out_shape=jax.ShapeDtypeStruct((nb, k, nl), BF16),
        scratch_shapes=[pltpu.VMEM((gb, k, nl), F32)] if nk > 1 else [],
        compiler_params=_cparams(dimension_semantics=("parallel", "arbitrary")),
    )(x, dy)


def _row_specs(rows, tb, nsub):
    return [pl.BlockSpec((tb, nsub * cw), functools.partial(lambda i, off: (i, off), off=off))
            for (_, cw, off) in rows]


def _vec_specs(params):
    return [pl.BlockSpec(p.shape, lambda i: (0, 0)) for p in params]


def _row_fwd(f, rows, params, out_dtypes, *, nsub=1, tb, name):
    t = rows[0][0].shape[0]
    tb = min(tb, t)
    n_r, n_p = len(rows), len(params)
    blk = [jax.ShapeDtypeStruct((tb, cw), F32) for (_, cw, _) in rows]
    blk += [jax.ShapeDtypeStruct(p.shape, F32) for p in params]
    out_avals = jax.eval_shape(f, *blk)

    def body(*refs):
        pv = [r[...] for r in refs[n_r:n_r + n_p]]
        for s in range(nsub):
            vals = [r[:, s * cw:(s + 1) * cw].astype(F32) for r, (_, cw, _) in zip(refs[:n_r], rows)]
            outs = f(*vals, *pv)
            for o_ref, o in zip(refs[n_r + n_p:], outs):
                w = o.shape[1]
                o_ref[:, s * w:(s + 1) * w] = o.astype(o_ref.dtype)

    return pl.pallas_call(
        body, name=name,
        grid=(t // tb,),
        in_specs=_row_specs(rows, tb, nsub) + _vec_specs(params),
        out_specs=[pl.BlockSpec((tb, nsub * av.shape[1]), lambda i: (i, 0)) for av in out_avals],
        out_shape=[jax.ShapeDtypeStruct((t, nsub * av.shape[1]), dt) for av, dt in zip(out_avals, out_dtypes)],
        compiler_params=_cparams(dimension_semantics=("parallel",)),
    )(*[r[0] for r in rows], *params)


def _row_bwd(f, rows, params, cots, row_grad_dtypes, *, nsub=1, tb, name, add_to=None, cot_add=None):
    t = rows[0][0].shape[0]
    tb = min(tb, t)
    n_r, n_p, n_c = len(rows), len(params), len(cots)
    want = [j for j in range(n_r) if row_grad_dtypes[j] is not None]
    extra = [] if add_to is None else [(add_to[1], rows[add_to[0]][1], 0)]
    extra += [] if cot_add is None else [(cot_add[1], cots[cot_add[0]][1], 0)]

    def body(*refs):
        i = pl.program_id(0)
        r_in, p_in = refs[:n_r], refs[n_r:n_r + n_p]
        c_in = refs[n_r + n_p:n_r + n_p + n_c]
        e_in = refs[n_r + n_p + n_c:n_r + n_p + n_c + len(extra)]
        outs = refs[n_r + n_p + n_c + len(extra):]
        pv = [r[...] for r in p_in]
        psum = [None] * n_p
        for s in range(nsub):
            vals = [r[:, s * cw:(s + 1) * cw].astype(F32) for r, (_, cw, _) in zip(r_in, rows)]
            cvals = [r[:, s * cw:(s + 1) * cw].astype(F32) for r, (_, cw, _) in zip(c_in, cots)]
            if cot_add is not None:
                cw = cots[cot_add[0]][1]
                cvals[cot_add[0]] = cvals[cot_add[0]] + e_in[-1][:, s * cw:(s + 1) * cw]
            _, vjp_fn = jax.vjp(f, *vals, *pv)
            grads = vjp_fn(tuple(cvals))
            for o_ref, jr in zip(outs[:len(want)], want):
                cw = rows[jr][1]
                gr = grads[jr]
                if add_to is not None and jr == add_to[0]:
                    gr = gr + e_in[0][:, s * cw:(s + 1) * cw]
                o_ref[:, s * cw:(s + 1) * cw] = gr.astype(o_ref.dtype)
            for jp in range(n_p):
                psum[jp] = grads[n_r + jp] if psum[jp] is None else psum[jp] + grads[n_r + jp]
        for o_ref, g in zip(outs[len(want):], psum):
            @pl.when(i == 0)
            def _():
                o_ref[...] = g

            @pl.when(i > 0)
            def _():
                o_ref[...] += g

    out_specs = [pl.BlockSpec((tb, nsub * rows[jr][1]), lambda i: (i, 0)) for jr in want]
    out_shape = [jax.ShapeDtypeStruct((t, nsub * rows[jr][1]), row_grad_dtypes[jr]) for jr in want]
    out_specs += _vec_specs(params)
    out_shape += [jax.ShapeDtypeStruct(p.shape, F32) for p in params]
    res = pl.pallas_call(
        body, name=name,
        grid=(t // tb,),
        in_specs=_row_specs(rows, tb, nsub) + _vec_specs(params) + _row_specs(cots, tb, nsub)
        + _row_specs(extra, tb, nsub),
        out_specs=out_specs, out_shape=out_shape,
        compiler_params=_cparams(dimension_semantics=("arbitrary",)),
    )(*[r[0] for r in rows], *params, *[c[0] for c in cots], *[e[0] for e in extra])
    return res[:len(want)], res[len(want):]


def _f_mod(x, sh, sc):
    return (_modulate(x, sh, sc),)


def _f_res_mod(x, y, g, sh, sc):
    x1 = x + g * y
    return x1, _modulate(x1, sh, sc)


def _f_res_mod2(x, y, g, sh_a, sc_a, sh_b, sc_b):
    x1 = x + g * y
    return x1, _modulate(x1, sh_a, sc_a), _modulate(x1, sh_b, sc_b)


def _f_qnorm(p, g):
    return (_rms(p) * g * (HEAD ** -0.5),)


def _f_knorm(p, g):
    return (_rms(p) * g,)


def _f_qnorm_aug(p, g):
    lane = lax.broadcasted_iota(jnp.int32, p.shape, 1)
    return (jnp.concatenate([_rms(p) * g * (HEAD ** -0.5), jnp.where(lane < 3, 1.0, 0.0)], axis=1),)


def _f_knorm_aug(p, c0, c1, c2, g):
    lane = lax.broadcasted_iota(jnp.int32, p.shape, 1)
    aug = jnp.where(lane == 0, c0, jnp.where(lane == 1, c1, jnp.where(lane == 2, c2, 0.0)))
    return (jnp.concatenate([_rms(p) * g, aug], axis=1),)


def _split3(a):
    round_bf16 = lambda v: lax.reduce_precision(v, exponent_bits=8, mantissa_bits=7)
    hi = round_bf16(a)
    mid = round_bf16(a - hi)
    lo = round_bf16(a - hi - mid)
    return hi.astype(BF16), mid.astype(BF16), lo.astype(BF16)


def _f_outgate(o, og):
    return (o * _sigmoid(og),)


def _loss_call(x3, f, g2, target, tb):
    t, d = x3.shape
    tb = min(tb, t)

    def body(x_ref, f_ref, g_ref, t_ref, loss_ref, dx_ref, df_ref, dg_ref):
        i = pl.program_id(0)
        fv = f_ref[...]
        g = g_ref[...]
        e = x_ref[...] + g * fv - t_ref[...]
        dx = e * (1.0 / d)
        part = 0.5 * jnp.sum(jnp.sum(e * dx, axis=1, keepdims=True), axis=0, keepdims=True)
        dx_ref[...] = dx
        df_ref[...] = (g * dx).astype(df_ref.dtype)
        dg = jnp.sum(dx * fv, axis=0, keepdims=True)

        @pl.when(i == 0)
        def _():
            loss_ref[...] = jnp.broadcast_to(part, loss_ref.shape)
            dg_ref[...] = dg

        @pl.when(i > 0)
        def _():
            loss_ref[...] += jnp.broadcast_to(part, loss_ref.shape)
            dg_ref[...] += dg

    row = pl.BlockSpec((tb, d), lambda i: (i, 0))
    vec = pl.BlockSpec((1, d), lambda i: (0, 0))
    return pl.pallas_call(
        body, name="loss_head",
        grid=(t // tb,),
        in_specs=[row, row, vec, row],
        out_specs=[pl.BlockSpec((1, LANES), lambda i: (0, 0)), row, row, vec],
        out_shape=[jax.ShapeDtypeStruct((1, LANES), F32), jax.ShapeDtypeStruct((t, d), F32),
                   jax.ShapeDtypeStruct((t, d), BF16), jax.ShapeDtypeStruct((1, d), F32)],
        compiler_params=_cparams(dimension_semantics=("arbitrary",)),
    )(x3, f, g2, target)


def _hg_mask(tb):
    br = lax.broadcasted_iota(jnp.int32, (tb, tb), 0)
    bs = lax.broadcasted_iota(jnp.int32, (tb, tb), 1)
    return jnp.logical_and(br // A_CHUNK == bs // A_CHUNK, bs <= br).astype(F32)


def _hg_consts(mask):
    c = A_CHUNK
    r = lax.broadcasted_iota(jnp.int32, (c, c), 0)
    s = lax.broadcasted_iota(jnp.int32, (c, c), 1)
    return (s <= r).astype(F32), (r <= s).astype(F32), mask > 0.5


def _chunk_apply(mat, x):
    c = mat.shape[0]
    return jnp.concatenate([_f32dot(mat, x[i * c:(i + 1) * c]) for i in range(x.shape[0] // c)], axis=0)


@jax.custom_vjp
def _chunk_cumsum(x, tri, tri_t):
    return _chunk_apply(tri, x)


_chunk_cumsum.defvjp(lambda x, tri, tri_t: (_chunk_apply(tri, x), (tri, tri_t)),
                     lambda r, g: (_chunk_apply(r[1], g), jnp.zeros_like(r[0]), jnp.zeros_like(r[1])))


def _per_chunk(a, b, dims):
    return jnp.stack([_bdot_raw(a[i], b[i], dims) for i in range(a.shape[0])])


@jax.custom_vjp
def _chunk_tn(a, b):
    return _per_chunk(a, b, _TN)


@jax.custom_vjp
def _chunk_nt(a, b):
    return _per_chunk(a, b, _NT)


@jax.custom_vjp
def _chunk_nn(a, b):
    return _per_chunk(a, b, _NN)


_chunk_tn.defvjp(lambda a, b: (_per_chunk(a, b, _TN), (a, b)),
                 lambda r, g: (_chunk_nt(r[1], g), _chunk_nn(r[0], g)))
_chunk_nt.defvjp(lambda a, b: (_per_chunk(a, b, _NT), (a, b)),
                 lambda r, g: (_chunk_nn(g, r[1]), _chunk_tn(g, r[0])))
_chunk_nn.defvjp(lambda a, b: (_per_chunk(a, b, _NN), (a, b)),
                 lambda r, g: (_chunk_nt(g, r[1]), _chunk_tn(r[0], g)))


def _scan_states(decay, m, st):
    sts = []
    for i in range(m.shape[0]):
        sts.append(st)
        st = st * decay[i] + m[i]
    return jnp.stack(sts), st


@jax.custom_vjp
def _state_scan(decay, m, st):
    return _scan_states(decay, m, st)


def _state_scan_fwd(decay, m, st):
    sts, st_out = _scan_states(decay, m, st)
    return (sts, st_out), (decay, sts)


def _state_scan_bwd(res, cts):
    decay, sts = res
    d_sts, g = cts
    d_decay, d_m = [], []
    for i in range(sts.shape[0] - 1, -1, -1):
        d_m.append(g)
        d_decay.append(jnp.sum(g * sts[i], axis=0, keepdims=True))
        g = g * decay[i] + d_sts[i]
    return jnp.stack(d_decay[::-1]), jnp.stack(d_m[::-1]), g


_state_scan.defvjp(_state_scan_fwd, _state_scan_bwd)


def _hg_block(qp, fp, ip, gp, lb, ng, st, tri, tri_t, bd_causal):
    tb = qp.shape[0]
    c = A_CHUNK
    n = tb // c
    q = _silu(qp)
    fg = lb + (1.0 - lb) * _sigmoid(fp)
    logf = jnp.log(fg)
    k = 1.0 - fg
    b3 = _chunk_cumsum(logf, tri, tri_t).reshape(n, c, HEAD)
    pos = lax.broadcasted_iota(jnp.int32, (1, c, 1), 1)
    b_mid = lax.stop_gradient(jnp.sum(jnp.where(pos == c // 2, b3, 0.0), axis=1, keepdims=True))
    b_last = jnp.sum(jnp.where(pos == c - 1, b3, 0.0), axis=1, keepdims=True)
    q3, k3, v3 = q.reshape(n, c, HEAD), k.reshape(n, c, HEAD), ip.reshape(n, c, HEAD)
    scores = _dot_nt((q3 * jnp.exp(b3 - b_mid)).reshape(tb, HEAD), (k3 * jnp.exp(b_mid - b3)).reshape(tb, HEAD))
    o_intra = _dot_nn(jnp.where(bd_causal, scores, 0.0), ip)
    states, st_new = _state_scan(jnp.exp(b_last), _chunk_tn(v3, k3 * jnp.exp(b_last - b3)), st)
    o = o_intra + _chunk_nt(q3 * jnp.exp(b3), states).reshape(tb, HEAD)
    y = _rms(o) * ng * _silu(gp)
    return y, st_new


HG_HEADS = 2


def _hg_specs(tb, nh, rev_nb=None):
    wide = HG_HEADS * HEAD
    per = nh // HG_HEADS

    def row(part):
        if rev_nb is None:
            return pl.BlockSpec((tb, wide), functools.partial(lambda h, i, off: (i, off + h), off=part * per))
        return pl.BlockSpec((tb, wide),
                            functools.partial(lambda h, i, off: (rev_nb - 1 - i, off + h), off=part * per))
    return [row(0), row(1), row(2), row(3),
            pl.BlockSpec((1, wide), lambda h, i: (0, h)), pl.BlockSpec((1, HEAD), lambda h, i: (0, 0)),
            pl.BlockSpec((tb, tb), lambda h, i: (0, 0))]


def _hgrn2_fwd(proj, lb, ng, tb):
    t = proj.shape[0]
    nh = proj.shape[1] // (4 * HEAD)
    tb = min(tb, t)
    nb = t // tb
    wide = HG_HEADS * HEAD

    def body(q_ref, f_ref, i_ref, g_ref, lb_ref, ng_ref, mask_ref, y_ref, s_ref, st_ref):
        i = pl.program_id(1)

        @pl.when(i == 0)
        def _():
            st_ref[...] = jnp.zeros_like(st_ref)

        consts = _hg_consts(mask_ref[...])
        for p in range(HG_HEADS):
            cs = slice(p * HEAD, (p + 1) * HEAD)
            st = st_ref[p]
            s_ref[p, 0] = st
            y, st_new = _hg_block(q_ref[:, cs], f_ref[:, cs], i_ref[:, cs], g_ref[:, cs], lb_ref[:, cs],
                                  ng_ref[...], st, *consts)
            y_ref[:, cs] = y.astype(y_ref.dtype)
            st_ref[p] = st_new

    return pl.pallas_call(
        body, name="hgrn2_fwd",
        grid=(nh // HG_HEADS, nb),
        in_specs=_hg_specs(tb, nh),
        out_specs=[pl.BlockSpec((tb, wide), lambda h, i: (i, h)),
                   pl.BlockSpec((HG_HEADS, 1, HEAD, HEAD), lambda h, i: (h, i, 0, 0))],
        out_shape=[jax.ShapeDtypeStruct((t, nh * HEAD), BF16),
                   jax.ShapeDtypeStruct((nh, nb, HEAD, HEAD), F32)],
        scratch_shapes=[pltpu.VMEM((HG_HEADS, HEAD, HEAD), F32)],
        compiler_params=_cparams(dimension_semantics=("parallel", "arbitrary")),
    )(proj, proj, proj, proj, lb, ng, _hg_mask(tb))


def _hgrn2_bwd(proj, lb, ng, states, dy, tb):
    t = proj.shape[0]
    nh = proj.shape[1] // (4 * HEAD)
    tb = min(tb, t)
    nb = t // tb
    wide = HG_HEADS * HEAD

    def body(q_ref, f_ref, i_ref, g_ref, lb_ref, ng_ref, mask_ref, s_ref, dy_ref,
             dp_ref, dlb_ref, dng_ref, dst_ref):
        h, i = pl.program_id(0), pl.program_id(1)
        consts = _hg_consts(mask_ref[...])

        @pl.when(i == 0)
        def _():
            dst_ref[...] = jnp.zeros_like(dst_ref)
            dlb_ref[...] = jnp.zeros_like(dlb_ref)

        @pl.when(jnp.logical_and(i == 0, h == 0))
        def _():
            dng_ref[...] = jnp.zeros_like(dng_ref)

        def fn(qp, fp, ip, gp, lbx, ngx, stx):
            return _hg_block(qp, fp, ip, gp, lbx, ngx, stx, *consts)

        for p in range(HG_HEADS):
            cs = slice(p * HEAD, (p + 1) * HEAD)
            _, vjp_fn = jax.vjp(fn, q_ref[:, cs], f_ref[:, cs], i_ref[:, cs], g_ref[:, cs], lb_ref[:, cs],
                                ng_ref[...], s_ref[p, 0])
            *gparts, glb, gng, dst = vjp_fn((dy_ref[:, cs].astype(F32), dst_ref[p]))
            for part, gpart in enumerate(gparts):
                dp_ref[part, :, cs] = gpart.astype(dp_ref.dtype)
            dst_ref[p] = dst
            dlb_ref[:, cs] += glb
            dng_ref[...] += gng

    rev = lambda h, i: (nb - 1 - i, h)
    return pl.pallas_call(
        body, name="hgrn2_bwd",
        grid=(nh // HG_HEADS, nb),
        in_specs=_hg_specs(tb, nh, rev_nb=nb) + [
            pl.BlockSpec((HG_HEADS, 1, HEAD, HEAD), lambda h, i: (h, nb - 1 - i, 0, 0)),
            pl.BlockSpec((tb, wide), rev)],
        out_specs=[pl.BlockSpec((4, tb, wide), lambda h, i: (0, nb - 1 - i, h)),
                   pl.BlockSpec((1, wide), lambda h, i: (0, h)), pl.BlockSpec((1, HEAD), lambda h, i: (0, 0))],
        out_shape=[jax.ShapeDtypeStruct((4, t, nh * HEAD), BF16),
                   jax.ShapeDtypeStruct((1, nh * HEAD), F32), jax.ShapeDtypeStruct((1, HEAD), F32)],
        scratch_shapes=[pltpu.VMEM((HG_HEADS, HEAD, HEAD), F32)],
        compiler_params=_cparams(dimension_semantics=("arbitrary", "arbitrary")),
    )(proj, proj, proj, proj, lb, ng, _hg_mask(tb), states, dy)


def _fgate_consts(cb):
    r = lax.broadcasted_iota(jnp.int32, (cb, cb), 0)
    s = lax.broadcasted_iota(jnp.int32, (cb, cb), 1)
    return (r <= s).astype(F32), (r >= s).astype(F32)


def _fgate_fwd(xt, bias, cb=512):
    nh, t = xt.shape
    cb = min(cb, t)

    def body(x_ref, b_ref, o_ref):
        upper, _ = _fgate_consts(cb)
        carry = jnp.zeros((nh, 1), F32)
        for blk in range(t // cb):
            z = x_ref[:, blk * cb:(blk + 1) * cb] + b_ref[...]
            logf = jnp.minimum(z, 0.0) - jnp.log(1.0 + jnp.exp(-jnp.abs(z)))
            cs = _f32dot(logf, upper) + carry
            o_ref[:, blk * cb:(blk + 1) * cb] = cs
            carry = cs[:, cb - 1:cb]

    vm = pl.BlockSpec(memory_space=pltpu.VMEM)
    return pl.pallas_call(
        body, name="fgate_fwd", in_specs=[vm, vm], out_specs=vm,
        out_shape=jax.ShapeDtypeStruct((nh, t), F32), compiler_params=_cparams(),
    )(xt, bias)


def _fgate_bwd(xt, bias, dft, cb=512):
    nh, t = xt.shape
    cb = min(cb, t)
    nblk = t // cb

    def body(x_ref, b_ref, d_ref, dx_ref, db_ref):
        _, lower = _fgate_consts(cb)
        carry = jnp.zeros((nh, 1), F32)
        db = jnp.zeros((nh, 1), F32)
        for blk in range(nblk - 1, -1, -1):
            sl = slice(blk * cb, (blk + 1) * cb)
            dlogf = _f32dot(d_ref[:, sl], lower) + carry
            carry = dlogf[:, 0:1]
            z = x_ref[:, sl] + b_ref[...]
            dz = dlogf * (1.0 - _sigmoid(z))
            dx_ref[:, sl] = dz
            db = db + jnp.sum(dz, axis=1, keepdims=True)
        db_ref[...] = db

    vm = pl.BlockSpec(memory_space=pltpu.VMEM)
    return pl.pallas_call(
        body, name="fgate_bwd", in_specs=[vm, vm, vm], out_specs=[vm, vm],
        out_shape=[jax.ShapeDtypeStruct((nh, t), F32), jax.ShapeDtypeStruct((nh, 1), F32)],
        compiler_params=_cparams(),
    )(xt, bias, dft)


def _attn_fwd(q, k, v, f_col, blk):
    t, width = v.shape
    nh = width // HEAD
    nq = t // blk

    def body(q_ref, k_ref, v_ref, fc_ref, o_ref, lse_ref):
        i = pl.program_id(0)
        tri = (lax.broadcasted_iota(jnp.int32, (blk, blk), 1) <= lax.broadcasted_iota(jnp.int32, (blk, blk), 0))
        for h in range(nh):
            cs = slice(h * HEAD, (h + 1) * HEAD)
            cs2 = slice(2 * h * HEAD, 2 * (h + 1) * HEAD)
            qh = q_ref[:, cs2]

            def tile(j, carry, masked):
                m, l, acc = carry
                rs = pl.ds(pl.multiple_of(j * blk, blk), blk)
                s = _bdot_raw(qh, k_ref[rs, cs2], _NT)
                if masked:
                    s = jnp.where(tri, s, NEG_INF)
                m_new = jnp.maximum(m, jnp.max(s, axis=1, keepdims=True))
                p = jnp.exp(s - m_new)
                alpha = jnp.exp(m - m_new)
                l_new = alpha * l + jnp.sum(p, axis=1, keepdims=True)
                acc_new = alpha * acc + _bdot_raw(p, v_ref[rs, cs], _NN)
                return m_new, l_new, acc_new

            init = (jnp.full((blk, 1), NEG_INF, F32), jnp.zeros((blk, 1), F32), jnp.zeros((blk, HEAD), F32))
            carry = lax.fori_loop(0, i, lambda j, c: tile(j, c, False), init)
            m, l, acc = tile(i, carry, True)
            o_ref[:, cs] = acc / l
            lse_ref[:, h:h + 1] = m + jnp.log(l) + fc_ref[:, h:h + 1]

    vm = pl.BlockSpec(memory_space=pltpu.VMEM)
    return pl.pallas_call(
        body, name="fox_attn_fwd",
        grid=(nq,),
        in_specs=[pl.BlockSpec((blk, 2 * width), lambda i: (i, 0)), vm, vm,
                  pl.BlockSpec((blk, nh), lambda i: (i, 0))],
        out_specs=[pl.BlockSpec((blk, width), lambda i: (i, 0)), pl.BlockSpec((blk, nh), lambda i: (i, 0))],
        out_shape=[jax.ShapeDtypeStruct((t, width), F32), jax.ShapeDtypeStruct((t, nh), F32)],
        compiler_params=_cparams(dimension_semantics=("parallel",)),
    )(q, k, v, f_col)


def _attn_delta(do, o, tb):
    t, width = o.shape
    nh = width // HEAD
    tb = min(tb, t)

    def body(do_ref, o_ref, dl_ref):
        for h in range(nh):
            cs = slice(h * HEAD, (h + 1) * HEAD)
            dl_ref[:, h:h + 1] = jnp.sum(do_ref[:, cs].astype(F32) * o_ref[:, cs], axis=1, keepdims=True)

    wide = pl.BlockSpec((tb, width), lambda i: (i, 0))
    return pl.pallas_call(body, name="fox_attn_delta", grid=(t // tb,), in_specs=[wide, wide],
                          out_specs=pl.BlockSpec((tb, nh), lambda i: (i, 0)),
                          out_shape=jax.ShapeDtypeStruct((t, nh), F32),
                          compiler_params=_cparams(dimension_semantics=("parallel",)))(do, o)


ATTN_BWD_GROUPS = 4


def _attn_bwd(q, k, v, f_col, do, lse, delta, blk):
    t, width = v.shape
    nh = width // HEAD
    nq = t // blk
    hpg = nh // ATTN_BWD_GROUPS
    gw = hpg * HEAD

    def body(q_ref, do_ref, k_ref, v_ref, fc_ref, lse_ref, dl_ref,
             dq_ref, dk_ref, dv_ref, dfc_ref, dfr_ref):
        g, j = pl.program_id(0), pl.program_id(1)
        tri = (lax.broadcasted_iota(jnp.int32, (blk, blk), 1) <= lax.broadcasted_iota(jnp.int32, (blk, blk), 0))

        @pl.when(j == 0)
        def _():
            dq_ref[...] = jnp.zeros_like(dq_ref)
            dfc_ref[...] = jnp.zeros_like(dfc_ref)

        for h in range(hpg):
            cs = slice(h * HEAD, (h + 1) * HEAD)
            cs2 = slice(2 * h * HEAD, 2 * (h + 1) * HEAD)
            csq = slice(2 * h * HEAD, (2 * h + 1) * HEAD)
            kj2 = k_ref[:, cs2]
            kj = k_ref[:, csq]
            vj = v_ref[:, cs]

            def tile(i, carry, masked):
                dk, dv, dfs = carry
                rs = pl.ds(pl.multiple_of(i * blk, blk), blk)
                qi = q_ref[rs, csq]
                doi = do_ref[rs, cs]
                bias = fc_ref[0, rs, h:h + 1] - lse_ref[0, rs, h:h + 1]
                p = jnp.exp(_bdot_raw(q_ref[rs, cs2], kj2, _NT) + bias)
                if masked:
                    p = jnp.where(tri, p, 0.0)
                ds = p * (_bdot_raw(doi, vj, _NT) - dl_ref[0, rs, h:h + 1])
                dsb = ds.astype(BF16)
                dq_ref[rs, cs] += _bdot_raw(dsb, kj, _NN)
                dfc_ref[0, rs, h:h + 1] += jnp.sum(ds, axis=1, keepdims=True)
                return (dk + _bdot_raw(dsb, qi, _TN), dv + _bdot_raw(p, doi, _TN),
                        dfs - jnp.sum(ds, axis=0, keepdims=True))

            init = (jnp.zeros((blk, HEAD), F32), jnp.zeros((blk, HEAD), F32), jnp.zeros((1, blk), F32))
            carry = tile(j, init, True)
            dk, dv, dfs = lax.fori_loop(j + 1, nq, lambda i, c: tile(i, c, False), carry)
            dk_ref[:, cs] = dk
            dv_ref[:, cs] = dv.astype(dv_ref.dtype)
            dfr_ref[0, 0, h:h + 1, :] = dfs

    by_group = lambda a: a.reshape(t, ATTN_BWD_GROUPS, hpg).transpose(1, 0, 2)
    once = pl.Buffered(1)
    stat = pl.BlockSpec((1, t, hpg), lambda g, j: (g, 0, 0), pipeline_mode=once)
    kv_blk = pl.BlockSpec((blk, gw), lambda g, j: (j, g))
    frow = pl.BlockSpec((1, 1, hpg, blk), lambda g, j: (g, j, 0, 0))
    dq, dk, dv, dfc, dfr = pl.pallas_call(
        body, name="fox_attn_bwd",
        grid=(ATTN_BWD_GROUPS, nq),
        in_specs=[pl.BlockSpec((t, 2 * gw), lambda g, j: (0, g), pipeline_mode=once),
                  pl.BlockSpec((t, gw), lambda g, j: (0, g), pipeline_mode=once),
                  pl.BlockSpec((blk, 2 * gw), lambda g, j: (j, g)), kv_blk, stat, stat, stat],
        out_specs=[pl.BlockSpec((t, gw), lambda g, j: (0, g)), kv_blk, kv_blk,
                   pl.BlockSpec((1, t, hpg), lambda g, j: (g, 0, 0)), frow],
        out_shape=[jax.ShapeDtypeStruct((t, width), F32), jax.ShapeDtypeStruct((t, width), F32),
                   jax.ShapeDtypeStruct((t, width), BF16), jax.ShapeDtypeStruct((ATTN_BWD_GROUPS, t, hpg), F32),
                   jax.ShapeDtypeStruct((ATTN_BWD_GROUPS, nq, hpg, blk), F32)],
        compiler_params=_cparams(dimension_semantics=("parallel", "arbitrary")),
    )(q, do, k, v, by_group(f_col), by_group(lse), by_group(delta))
    return (dq, dk, dv, dfc.transpose(1, 0, 2).reshape(t, nh),
            dfr.transpose(1, 0, 2, 3).reshape(nq, nh, blk))


SUBLANES = 8


def _shift_down(u, n):
    r = pltpu.roll(u, n, 0)
    row = lax.broadcasted_iota(jnp.int32, (SUBLANES, u.shape[1]), 0)
    return jnp.concatenate([jnp.where(row < n, 0.0, r[:SUBLANES]), r[SUBLANES:]], axis=0)


def _shift_up(u, n):
    t = u.shape[0]
    r = pltpu.roll(u, t - n, 0)
    row = lax.broadcasted_iota(jnp.int32, (SUBLANES, u.shape[1]), 0)
    return jnp.concatenate([r[:t - SUBLANES], jnp.where(row >= SUBLANES - n, 0.0, r[t - SUBLANES:])], axis=0)


def _convglu_specs(t):
    return [pl.BlockSpec((2, t, LANES), lambda j: (0, 0, j)),
            pl.BlockSpec((2, CONV_TAPS, LANES), lambda j: (0, 0, j)),
            pl.BlockSpec((2, 1, LANES), lambda j: (0, 0, j))]


def _convglu_fwd(u, cw, cb):
    _, t, fp = u.shape

    def body(u_ref, w_ref, b_ref, a_ref):
        c = []
        for hf in range(2):
            uv, w = u_ref[hf], w_ref[hf]
            c.append(w[0:1] * _shift_down(uv, 2) + w[1:2] * _shift_down(uv, 1) + w[2:3] * uv + b_ref[hf])
        a_ref[...] = (_silu(c[0]) * c[1]).astype(a_ref.dtype)

    return pl.pallas_call(
        body, name="convglu_fwd",
        grid=(fp // LANES,),
        in_specs=_convglu_specs(t),
        out_specs=pl.BlockSpec((t, LANES), lambda j: (0, j)),
        out_shape=jax.ShapeDtypeStruct((t, fp), BF16),
        compiler_params=_cparams(dimension_semantics=("parallel",)),
    )(u, cw, cb)


def _convglu_bwd(u, cw, cb, da):
    _, t, fp = u.shape

    def body(u_ref, w_ref, b_ref, da_ref, du_ref, dw_ref, db_ref):
        us, c = [], []
        for hf in range(2):
            uv, w = u_ref[hf], w_ref[hf]
            u1, u2 = _shift_down(uv, 1), _shift_down(uv, 2)
            us.append((uv, u1, u2))
            c.append(w[0:1] * u2 + w[1:2] * u1 + w[2:3] * uv + b_ref[hf])
        gc, vc = c
        sg = _sigmoid(gc)
        dav = da_ref[...].astype(F32)
        dcs = [dav * vc * (sg * (1.0 + gc * (1.0 - sg))), dav * (gc * sg)]
        for hf in range(2):
            dc, w = dcs[hf], w_ref[hf]
            uv, u1, u2 = us[hf]
            du = w[2:3] * dc + w[1:2] * _shift_up(dc, 1) + w[0:1] * _shift_up(dc, 2)
            du_ref[hf] = du.astype(du_ref.dtype)
            dw_ref[hf, 0:1, :] = jnp.sum(dc * u2, axis=0, keepdims=True)
            dw_ref[hf, 1:2, :] = jnp.sum(dc * u1, axis=0, keepdims=True)
            dw_ref[hf, 2:3, :] = jnp.sum(dc * uv, axis=0, keepdims=True)
            db_ref[hf] = jnp.sum(dc, axis=0, keepdims=True)

    specs = _convglu_specs(t)
    return pl.pallas_call(
        body, name="convglu_bwd",
        grid=(fp // LANES,),
        in_specs=specs + [pl.BlockSpec((t, LANES), lambda j: (0, j))],
        out_specs=specs,
        out_shape=[jax.ShapeDtypeStruct((2, t, fp), BF16), jax.ShapeDtypeStruct((2, CONV_TAPS, fp), F32),
                   jax.ShapeDtypeStruct((2, 1, fp), F32)],
        compiler_params=_cparams(dimension_semantics=("parallel",)),
    )(u, cw, cb, da)


def _local_step(x, target, mods, lb, small, get_w, put_g, *, tb=512, attn_blk=512):
    t, d = x.shape
    nh = d // HEAD
    nb = NDEV
    wts = {}
    vec = lambda *names: [mods[n] for n in names]

    def ffn_fwd(h2, l):
        u = _mm_wblk(h2, wts[f"up{l}"], F32, f"ffn{l}_up", gb=nb // 2, split=2, tm=512)
        a = _convglu_fwd(u, small[f"conv_w{l}"], small[f"conv_b{l}"])
        f = _mm(a, wts[f"down{l}"], "nn", F32, f"ffn{l}_down", tk=4096)
        return u, a, f

    def ffn_bwd(df, h2, u, a, l):
        da = _mm(df, wts[f"down{l}"], "nt", BF16, f"ffn{l}_down_dx", tn=1536)
        dwd = _mm(a, df, "tn", BF16, f"ffn{l}_down_dw", tm=1536, tk=1024)
        du, dcw, dcb = _convglu_bwd(u, small[f"conv_w{l}"], small[f"conv_b{l}"], da)
        dh2 = _mm_wblk_dx(du, wts[f"up{l}"], BF16, f"ffn{l}_up_dx", k=d, gb=nb // 2, split=2, tm=1024)
        dwu = _mm_wblk_dw(h2, du, f"ffn{l}_up_dw", nb=nb, gb=1, split=2, tk=t)
        return dh2, dwu, dwd, dcw, dcb

    (h_a,) = _row_fwd(_f_mod, [(x, d, 0)], vec("sh1_0", "sc1_0"), [BF16], tb=tb, name="l0_mod1")
    wts.update(get_w("l0a", h_a))
    proj_a = _mm_wblk(h_a, wts["a_in"], F32, "a_in", gb=nb // 2)
    ypre, states = _hgrn2_fwd(proj_a, lb, small["a_norm_g"], tb)
    wts.update(get_w("l0b", ypre))
    y_a = _mm(ypre, wts["a_out"], "nn", F32, "a_out")
    x1, h2_0 = _row_fwd(_f_res_mod, [(x, d, 0), (y_a, d, 0)], vec("g1_0", "sh2_0", "sc2_0"), [F32, BF16],
                        tb=tb, name="l0_res_mod2")
    u0, a0, f0 = ffn_fwd(h2_0, 0)
    x2, h_kv, h_q = _row_fwd(_f_res_mod2, [(x1, d, 0), (f0, d, 0)],
                             vec("g2_0", "kv_sh", "kv_sc", "sh1_1", "sc1_1"), [F32, BF16, BF16],
                             tb=tb, name="l0_res_kvmod_qmod")
    wts.update(get_w("l1", h_kv))
    proj_kv = _mm(h_kv, wts["kv"], "nt", F32, "kv_proj")
    proj_f = _mm(h_kv, wts["kv_f"], "nt", F32, "kv_fproj")
    v_b = proj_kv[:, d:].astype(BF16)
    f_logit_t = proj_f[:, :nh].T
    f_bias = small["kv_b_f"].reshape(nh, 1)
    f_col = _fgate_fwd(f_logit_t, f_bias).T
    (k_n,) = _row_fwd(_f_knorm_aug, [(proj_kv, HEAD, 0)] + [(piece, 1, 0) for piece in _split3(-f_col)],
                      [small["k_norm_g"]], [BF16], nsub=nh, tb=tb, name="k_norm")
    proj_q = _mm_wblk(h_q, wts["b_q"], F32, "b_q", gb=nb)
    (q_n,) = _row_fwd(_f_qnorm_aug, [(proj_q, HEAD, 0)], [small["q_norm_g"]], [BF16], nsub=nh, tb=tb,
                      name="q_norm")
    o_att, lse = _attn_fwd(q_n, k_n, v_b, f_col, attn_blk)
    (z,) = _row_fwd(_f_outgate, [(o_att, HEAD, 0), (proj_q, HEAD, 1)], [], [BF16], nsub=nh, tb=tb, name="out_gate")
    y_b = _mm(z, wts["b_out"], "nn", F32, "b_out")
    x3, h2_1 = _row_fwd(_f_res_mod, [(x2, d, 0), (y_b, d, 0)], vec("g1_1", "sh2_1", "sc2_1"), [F32, BF16],
                        tb=tb, name="l1_res_mod2")
    u1, a1, f1 = ffn_fwd(h2_1, 1)
    loss, dx4, df1, dg2_1 = _loss_call(x3, f1, mods["g2_1"], target, tb)

    g = {}
    dmods = {"g2_1": dg2_1}
    dh2, g["up1"], g["down1"], g["conv_w1"], g["conv_b1"] = ffn_bwd(df1, h2_1, u1, a1, 1)
    (dx2, dy_b), (dmods["g1_1"], dmods["sh2_1"], dmods["sc2_1"]) = _row_bwd(
        _f_res_mod, [(x2, d, 0), (y_b, d, 0)], vec("g1_1", "sh2_1", "sc2_1"),
        [(dx4, d, 0), (dh2, d, 0)], [F32, BF16], tb=tb, name="l1_res_mod2_bwd")
    dz = _mm(dy_b, wts["b_out"], "nt", BF16, "b_out_dx")
    g["b_out"] = _mm(z, dy_b, "tn", BF16, "b_out_dw", tk=1024)
    (do_att, dog), _ = _row_bwd(_f_outgate, [(o_att, HEAD, 0), (proj_q, HEAD, 1)], [], [(dz, HEAD, 0)],
                                [BF16, BF16], nsub=nh, tb=tb, name="out_gate_bwd")
    delta = _attn_delta(do_att, o_att, tb)
    dq_n, dk_n, dv, dfc_q, dfr_k = _attn_bwd(q_n, k_n, v_b, f_col, do_att, lse, delta, attn_blk)
    (dpq,), (g["q_norm_g"],) = _row_bwd(_f_qnorm, [(proj_q, HEAD, 0)], [small["q_norm_g"]],
                                        [(dq_n, HEAD, 0)], [BF16], nsub=nh, tb=tb, name="q_norm_bwd")
    dproj_q = jnp.concatenate([dpq, dog], axis=1)
    dh_q = _mm_wblk_dx(dproj_q, wts["b_q"], BF16, "b_q_dx", k=d, gb=nb)
    g["b_q"] = _mm_wblk_dw(h_q, dproj_q, "b_q_dw", nb=nb, gb=nb // 4, tk=t)
    (dpk,), (g["k_norm_g"],) = _row_bwd(_f_knorm, [(proj_kv, HEAD, 0)], [small["k_norm_g"]],
                                        [(dk_n, HEAD, 0)], [BF16], nsub=nh, tb=tb, name="k_norm_bwd")
    dproj_kv = jnp.concatenate([dpk, dv], axis=1)
    df_t = dfc_q.T + dfr_k.transpose(1, 0, 2).reshape(nh, t)
    dflogit_t, g["kv_b_f"] = _fgate_bwd(f_logit_t, f_bias, df_t)
    dproj_f = jnp.pad(dflogit_t.T, ((0, 0), (0, LANES - nh))).astype(BF16)
    dh_kv = _mm(dproj_kv, wts["kv"], "nn", BF16, "kv_proj_dx")
    dh_kv_f = _mm(dproj_f, wts["kv_f"], "nn", BF16, "kv_fproj_dx")
    g["kv"] = _mm(dproj_kv, h_kv, "tn", BF16, "kv_proj_dw", tk=1024)
    g["kv_f"] = _mm(dproj_f, h_kv, "tn", F32, "kv_fproj_dw", tk=1024)
    sent = put_g("l1", {n: g.pop(n) for n in ("b_out", "b_q", "kv", "kv_f", "up1", "down1")})
    (dx1, df0), (dmods["g2_0"], dmods["kv_sh"], dmods["kv_sc"], dmods["sh1_1"], dmods["sc1_1"]) = _row_bwd(
        _f_res_mod2, [(x1, d, 0), (f0, d, 0)], [mods["g2_0"] + sent] + vec("kv_sh", "kv_sc", "sh1_1", "sc1_1"),
        [(dx2, d, 0), (dh_kv, d, 0), (dh_q, d, 0)], [F32, BF16], tb=tb, name="l0_res_kvmod_qmod_bwd",
        cot_add=(1, dh_kv_f))
    dh2, g["up0"], g["down0"], g["conv_w0"], g["conv_b0"] = ffn_bwd(df0, h2_0, u0, a0, 0)
    (dx0, dy_a), (dmods["g1_0"], dmods["sh2_0"], dmods["sc2_0"]) = _row_bwd(
        _f_res_mod, [(x, d, 0), (y_a, d, 0)], vec("g1_0", "sh2_0", "sc2_0"),
        [(dx1, d, 0), (dh2, d, 0)], [F32, BF16], tb=tb, name="l0_res_mod2_bwd")
    dypre = _mm(dy_a, wts["a_out"], "nt", BF16, "a_out_dx")
    g["a_out"] = _mm(ypre, dy_a, "tn", BF16, "a_out_dw", tk=1024)
    sent = put_g("l0b", {n: g.pop(n) for n in ("a_out", "up0", "down0")})
    dproj_a, dlb, g["a_norm_g"] = _hgrn2_bwd(proj_a, lb + sent, small["a_norm_g"], states, dypre, tb)
    dh_a = _mm_wblk_dx(dproj_a, wts["a_in"], BF16, "a_in_dx", k=d, gb=nb, split=4, tm=512)
    put_g("l0a", {"a_in": _mm_wblk_dw(h_a, dproj_a, "a_in_dw", nb=nb, gb=1, split=4, tk=t)})
    (grad_x,), (dmods["sh1_0"], dmods["sc1_0"]) = _row_bwd(
        _f_mod, [(x, d, 0)], vec("sh1_0", "sc1_0"), [(dh_a, d, 0)], [F32], tb=tb, name="l0_mod1_bwd",
        add_to=(0, dx0))
    return loss, grad_x, dmods, dlb, g


def _position():
    return lax.axis_index("x"), lax.axis_index("y"), lax.axis_index("c")


def _hbm_specs(n):
    return [pl.BlockSpec(memory_space=pl.ANY)] * n


def _all_gather(arrs, name):
    n = len(arrs)

    def body(*refs):
        x_refs, out_refs = refs[:n], refs[n:2 * n]
        send_sems, recv_sems, local_sems = refs[2 * n:]
        x, y, cc = _position()
        me, sibling = (x, y, cc), (x, y, 1 - cc)
        chips = [(1 - x, y), (x, 1 - y), (1 - x, 1 - y)]

        def copy(a, k, block, to, src=None):
            slot = out_refs[a].at[4 * block[0] + 2 * block[1] + block[2]]
            return pltpu.make_async_remote_copy(
                src_ref=slot if src is None else src, dst_ref=slot,
                send_sem=send_sems.at[7 * a + k], recv_sem=recv_sems.at[7 * a + k],
                device_id=to, device_id_type=_MESH)

        local = [pltpu.make_async_copy(x_refs[a], out_refs[a].at[4 * x + 2 * y + cc], local_sems.at[a])
                 for a in range(n)]
        for cp in local:
            cp.start()
        first = []
        for a in range(n):
            first.append(copy(a, 0, me, sibling, src=x_refs[a]))
            first += [copy(a, 1 + j, me, (*chip, cc), src=x_refs[a]) for j, chip in enumerate(chips)]
        for cp in first:
            cp.start()
        passed = []
        for j, chip in enumerate(chips):
            for a in range(n):
                copy(a, 1 + j, (*chip, cc), me).wait_recv()
                fwd = copy(a, 4 + j, (*chip, cc), sibling)
                fwd.start()
                passed.append(fwd)
        for a in range(n):
            copy(a, 0, sibling, me).wait_recv()
        for j, chip in enumerate(chips):
            for a in range(n):
                copy(a, 4 + j, (*chip, 1 - cc), me).wait_recv()
        for cp in first + passed:
            cp.wait_send()
        for cp in local:
            cp.wait()

    return pl.pallas_call(
        body, name=name,
        out_shape=[jax.ShapeDtypeStruct((NDEV, *a.shape), a.dtype) for a in arrs],
        in_specs=_hbm_specs(n), out_specs=_hbm_specs(n),
        scratch_shapes=[pltpu.SemaphoreType.DMA((7 * n,)), pltpu.SemaphoreType.DMA((7 * n,)),
                        pltpu.SemaphoreType.DMA((n,))],
    )(*arrs)


_XCHG_EFFECT = pltpu.SideEffectType.DATAFLOW_SIDE_EFFECTING
ALL_PEERS = (1, 2, 3, 4, 5, 6, 7)
SAME_CORE = (2, 4, 6)


def _xchg_copies(src_refs, land_refs, send_sems, recv_sems, local_sems, scatter, rels):
    x, y, cc = _position()
    me = 4 * x + 2 * y + cc
    remote, local = [], []
    for a, (src, land) in enumerate(zip(src_refs, land_refs)):
        local.append(pltpu.make_async_copy(src.at[me] if scatter else src, land.at[me], local_sems.at[a]))
        for idx, rel in enumerate(rels):
            px = 1 - x if rel & 4 else x
            py = 1 - y if rel & 2 else y
            pc = 1 - cc if rel & 1 else cc
            k = len(rels) * a + idx
            remote.append(pltpu.make_async_remote_copy(
                src_ref=src.at[4 * px + 2 * py + pc] if scatter else src, dst_ref=land.at[me],
                send_sem=send_sems.at[k], recv_sem=recv_sems.at[k], device_id=(px, py, pc), device_id_type=_MESH))
    return remote, local


def _xchg_start(srcs, scatter, rels, after, name):
    n = len(srcs)
    lands = [lax.empty(s.shape if scatter else (NDEV, *s.shape), s.dtype) for s in srcs]

    def body(*refs):
        remote, local = _xchg_copies(refs[:n], refs[n:2 * n], *refs[2 * n + 1:2 * n + 4], scatter, rels)
        for cp in local + remote:
            cp.start()
        token = refs[-1]
        token[...] = jnp.zeros_like(token)

    hbm = pl.BlockSpec(memory_space=pltpu.HBM)
    sem = pl.BlockSpec(memory_space=pltpu.SEMAPHORE)
    out = pl.pallas_call(
        body, name=name,
        out_shape=(pltpu.SemaphoreType.DMA((len(rels) * n,)), pltpu.SemaphoreType.DMA((len(rels) * n,)),
                   pltpu.SemaphoreType.DMA((n,)),
                   *[pltpu.HBM(a.shape, a.dtype) for a in srcs + lands], jax.ShapeDtypeStruct((8, LANES), F32)),
        in_specs=[hbm] * (2 * n) + [pl.BlockSpec(memory_space=pl.ANY)],
        out_specs=(sem, sem, sem, *[hbm] * (2 * n), pl.BlockSpec(memory_space=pltpu.VMEM)),
        input_output_aliases={i: 3 + i for i in range(2 * n)},
        compiler_params=pltpu.CompilerParams(has_side_effects=_XCHG_EFFECT),
    )(*[pltpu.with_memory_space_constraint(a, pltpu.HBM) for a in srcs + lands], after)
    return out[:-1], out[-1][0, 0]


def _xchg_wait(handles, after, scatter, rels, name):
    n = (len(handles) - 3) // 2

    def body(*refs):
        remote, local = _xchg_copies(refs[:n], refs[n:2 * n], *refs[2 * n:2 * n + 3], scatter, rels)
        for cp in remote:
            cp.wait_send()
            cp.wait_recv()
        for cp in local:
            cp.wait()

    hbm = pl.BlockSpec(memory_space=pltpu.HBM)
    sem = pl.BlockSpec(memory_space=pltpu.SEMAPHORE)
    thru = list(handles[3:])
    out = pl.pallas_call(
        body, name=name,
        out_shape=tuple(pltpu.HBM(a.shape, a.dtype) for a in thru),
        in_specs=[hbm] * (2 * n) + [sem, sem, sem, pl.BlockSpec(memory_space=pl.ANY)],
        out_specs=tuple([hbm] * (2 * n)),
        input_output_aliases={i: i for i in range(2 * n)},
        compiler_params=pltpu.CompilerParams(has_side_effects=_XCHG_EFFECT),
    )(*thru, *handles[:3], after)
    return list(out[n:])


def _sibling_forward(lands, name):
    n = len(lands)

    def body(*refs):
        land_refs = refs[n:2 * n]
        send_sems, recv_sems = refs[2 * n:]
        x, y, cc = _position()

        def copy(a, q, core):
            slot = land_refs[a].at[2 * q + core]
            return pltpu.make_async_remote_copy(
                src_ref=slot, dst_ref=slot, send_sem=send_sems.at[NCHIP * a + q], recv_sem=recv_sems.at[NCHIP * a + q],
                device_id=(x, y, 1 - cc), device_id_type=_MESH)

        sends = [copy(a, q, cc) for a in range(n) for q in range(NCHIP)]
        for cp in sends:
            cp.start()
        for a in range(n):
            for q in range(NCHIP):
                copy(a, q, 1 - cc).wait_recv()
        for cp in sends:
            cp.wait_send()

    return pl.pallas_call(
        body, name=name,
        out_shape=[jax.ShapeDtypeStruct(a.shape, a.dtype) for a in lands],
        in_specs=_hbm_specs(n), out_specs=_hbm_specs(n),
        input_output_aliases={i: i for i in range(n)},
        scratch_shapes=[pltpu.SemaphoreType.DMA((NCHIP * n,)), pltpu.SemaphoreType.DMA((NCHIP * n,))],
    )(*lands)


def _slab_sum(slabs, name, tr=None):
    n, r, c = slabs.shape
    tr = r if tr is None else tr

    def body(s_ref, o_ref):
        acc = s_ref[0].astype(F32)
        for q in range(1, n):
            acc = acc + s_ref[q].astype(F32)
        o_ref[...] = acc

    return pl.pallas_call(body, name=name, grid=(r // tr,),
                          in_specs=[pl.BlockSpec((n, tr, c), lambda i: (0, i, 0))],
                          out_specs=pl.BlockSpec((tr, c), lambda i: (i, 0)),
                          out_shape=jax.ShapeDtypeStruct((r, c), F32),
                          compiler_params=_cparams(dimension_semantics=("parallel",)))(slabs)


def _ada_fwd(c_all, ada_w, kv_ada_w, logits):
    rows, d = c_all.shape
    n0, nkv = ada_w.shape[2], kv_ada_w.shape[1]

    def body(c_ref, w_ref, kw_ref, lg_ref, part_ref, cact_ref, lb_ref):
        ca = _silu(c_ref[...])
        cact_ref[...] = ca
        part_ref[:, 0:n0] = _bdot_raw(ca, w_ref[0], _NN)
        part_ref[:, n0:2 * n0] = _bdot_raw(ca, w_ref[1], _NN)
        part_ref[:, 2 * n0:2 * n0 + nkv] = _bdot_raw(ca, kw_ref[...], _NN)
        lb_ref[...] = _sigmoid(lg_ref[0:1, :] - lg_ref[1:2, :])

    vm = pl.BlockSpec(memory_space=pltpu.VMEM)
    return pl.pallas_call(
        body, name="ada_fwd", in_specs=[vm, vm, vm, vm], out_specs=[vm, vm, vm],
        out_shape=[jax.ShapeDtypeStruct((rows, 2 * n0 + nkv), F32), jax.ShapeDtypeStruct((rows, d), F32),
                   jax.ShapeDtypeStruct((1, d), F32)],
        compiler_params=_cparams(),
    )(c_all, ada_w, kv_ada_w, logits)


def _ada_bwd(c_act, dm0, dm1, dkv, lb, dlb):
    rows, d = c_act.shape

    def body(c_ref, d0_ref, d1_ref, dk_ref, lb_ref, dlb_ref, dw_ref, dkw_ref, dlg_ref):
        ca = c_ref[...]
        dw_ref[0] = _bdot_raw(ca, d0_ref[...], _TN)
        dw_ref[1] = _bdot_raw(ca, d1_ref[...], _TN)
        dkw_ref[...] = _bdot_raw(ca, dk_ref[...], _TN)
        lbv = lb_ref[...]
        dl0 = dlb_ref[...] * lbv * (1.0 - lbv)
        dlg_ref[0:1, :] = dl0
        dlg_ref[1:2, :] = -dl0

    vm = pl.BlockSpec(memory_space=pltpu.VMEM)
    return pl.pallas_call(
        body, name="ada_bwd", in_specs=[vm] * 6, out_specs=[vm, vm, vm],
        out_shape=[jax.ShapeDtypeStruct((2, d, dm0.shape[1]), F32), jax.ShapeDtypeStruct((d, dkv.shape[1]), F32),
                   jax.ShapeDtypeStruct((2, d), F32)],
        compiler_params=_cparams(),
    )(c_act, dm0, dm1, dkv, lb, dlb)


def _adamw(w, g, m, v, name, tr=512, after=None):
    r, c = w.shape
    tr = _divisor_tile(r, tr, unit=8)
    c1 = 1.0 - ADAM_B1 ** ADAM_STEP
    c2 = 1.0 - ADAM_B2 ** ADAM_STEP
    deps = [] if after is None else [after]

    def body(w_ref, g_ref, m_ref, v_ref, *rest):
        d_ref, mo_ref, vo_ref = rest[len(deps):]
        gv = g_ref[...]
        mn = ADAM_B1 * m_ref[...] + (1.0 - ADAM_B1) * gv
        vn = ADAM_B2 * v_ref[...] + (1.0 - ADAM_B2) * (gv * gv)
        d_ref[...] = -ADAM_LR * ((mn / c1) / (jnp.sqrt(vn / c2) + ADAM_EPS) + ADAM_WD * w_ref[...])
        mo_ref[...] = mn
        vo_ref[...] = vn

    spec = pl.BlockSpec((tr, c), lambda i: (i, 0))
    out = jax.ShapeDtypeStruct((r, c), F32)
    return pl.pallas_call(body, name=name, grid=(r // tr,),
                          in_specs=[spec] * 4 + [pl.BlockSpec(a.shape, lambda i: (0, 0)) for a in deps],
                          out_specs=[spec] * 3, out_shape=[out, out, out],
                          compiler_params=_cparams(dimension_semantics=("parallel",)))(w, g, m, v, *deps)


def _pad_rows(a, rows):
    return jnp.pad(a, ((0, rows - a.shape[0]), (0, 0)))


def _pack_small(parts, lanes=LANES, row_unit=8):
    flat = jnp.concatenate([p.reshape(-1).astype(F32) for p in parts])
    rows = _round_up(-(-flat.shape[0] // lanes), row_unit)
    return jnp.pad(flat, (0, rows * lanes - flat.shape[0])).reshape(rows, lanes)


def _unpack_small(flat, shapes):
    out, off = [], 0
    for s in shapes:
        n = 1
        for k in s:
            n *= k
        out.append(flat[off:off + n].reshape(s))
        off += n
    return out


def _pad_shard_cols(a, n_loc, n_pad):
    lead = a.shape[:-1]
    a = a.reshape(*lead, NDEV, n_loc)
    a = jnp.pad(a, [(0, 0)] * (len(lead) + 1) + [(0, n_pad - n_loc)])
    return a.reshape(*lead, NDEV * n_pad)


def _unpad_shard_cols(a, n_loc, n_pad):
    lead = a.shape[:-1]
    return a.reshape(*lead, NDEV, n_pad)[..., :n_loc].reshape(*lead, NDEV * n_loc)


def kernel(x, c, ada_w, ada_b, a_w_in, a_lb_logits, a_norm_g, a_w_out, kv_ada_w, kv_ada_b, kv_w, kv_b_f, k_norm_g, b_w_q, q_norm_g, b_w_out, ffn_w_up, ffn_conv_w, ffn_conv_b, ffn_w_down, loss_target, m_ada_w, m_ada_b, m_a_w_in, m_a_lb_logits, m_a_norm_g, m_a_w_out, m_kv_ada_w, m_kv_ada_b, m_kv_w, m_kv_b_f, m_k_norm_g, m_b_w_q, m_q_norm_g, m_b_w_out, m_ffn_w_up, m_ffn_conv_w, m_ffn_conv_b, m_ffn_w_down, v_ada_w, v_ada_b, v_a_w_in, v_a_lb_logits, v_a_norm_g, v_a_w_out, v_kv_ada_w, v_kv_ada_b, v_kv_w, v_kv_b_f, v_k_norm_g, v_b_w_q, v_q_norm_g, v_b_w_out, v_ffn_w_up, v_ffn_conv_w, v_ffn_conv_b, v_ffn_w_down):
    t, d = x.shape[1], x.shape[2]
    nh = d // HEAD
    ncw = ffn_w_up.shape[2]
    ncp = _round_up(ncw, LANES)
    two_f = ncw * NDEV
    ff = two_f // 2
    fp = ncp * NDEV // 2
    rd = ffn_w_down.shape[1]
    me = 4 * lax.axis_index("x") + 2 * lax.axis_index("y") + lax.axis_index("c")
    weights = dict(ada_w=ada_w, ada_b=ada_b, a_w_in=a_w_in, a_lb_logits=a_lb_logits, a_norm_g=a_norm_g,
                   a_w_out=a_w_out, kv_ada_w=kv_ada_w, kv_ada_b=kv_ada_b, kv_w=kv_w, kv_b_f=kv_b_f,
                   k_norm_g=k_norm_g, b_w_q=b_w_q, q_norm_g=q_norm_g, b_w_out=b_w_out, ffn_w_up=ffn_w_up,
                   ffn_conv_w=ffn_conv_w, ffn_conv_b=ffn_conv_b, ffn_w_down=ffn_w_down)
    m_in = dict(ada_w=m_ada_w, ada_b=m_ada_b, a_w_in=m_a_w_in, a_lb_logits=m_a_lb_logits, a_norm_g=m_a_norm_g,
                a_w_out=m_a_w_out, kv_ada_w=m_kv_ada_w, kv_ada_b=m_kv_ada_b, kv_w=m_kv_w, kv_b_f=m_kv_b_f,
                k_norm_g=m_k_norm_g, b_w_q=m_b_w_q, q_norm_g=m_q_norm_g, b_w_out=m_b_w_out, ffn_w_up=m_ffn_w_up,
                ffn_conv_w=m_ffn_conv_w, ffn_conv_b=m_ffn_conv_b, ffn_w_down=m_ffn_w_down)
    v_in = dict(ada_w=v_ada_w, ada_b=v_ada_b, a_w_in=v_a_w_in, a_lb_logits=v_a_lb_logits, a_norm_g=v_a_norm_g,
                a_w_out=v_a_w_out, kv_ada_w=v_kv_ada_w, kv_ada_b=v_kv_ada_b, kv_w=v_kv_w, kv_b_f=v_kv_b_f,
                k_norm_g=v_k_norm_g, b_w_q=v_b_w_q, q_norm_g=v_q_norm_g, b_w_out=v_b_w_out, ffn_w_up=v_ffn_w_up,
                ffn_conv_w=v_ffn_conv_w, ffn_conv_b=v_ffn_conv_b, ffn_w_down=v_ffn_w_down)
    order = list(weights)

    up_loc = jnp.pad(ffn_w_up, ((0, 0), (0, 0), (0, ncp - ncw))).astype(BF16)
    down_loc = ffn_w_down.astype(BF16)
    gather_names = {"l0b": ["a_out", "up0", "down0"], "l1": ["kv", "b_q", "b_out", "up1", "down1"]}
    shards = {"a_out": a_w_out[0].astype(BF16), "up0": up_loc[0], "down0": down_loc[0], "kv": kv_w.T.astype(BF16),
              "b_q": b_w_q[0].astype(BF16), "b_out": b_w_out[0].astype(BF16), "up1": up_loc[1],
              "down1": down_loc[1]}
    pre = _pack_small([c, a_lb_logits, ffn_conv_w])
    a_in_all, pre_all = _all_gather([a_w_in[0].astype(BF16), pre], "gather_a_w_in_and_small_inputs")
    pre_all = pre_all.reshape(NDEV, -1)
    c_all = pre_all[:, :d]
    logits = pre_all[:, d:d + 2 * HEAD].reshape(NDEV, 2, HEAD).transpose(1, 0, 2).reshape(2, d)
    conv_w_full = pre_all[:, d + 2 * HEAD:d + 2 * HEAD + 2 * CONV_TAPS * ncw]
    conv_w_full = conv_w_full.reshape(NDEV, 2, CONV_TAPS, ncw).transpose(1, 2, 0, 3).reshape(2, CONV_TAPS, two_f)

    part, c_act, lb = _ada_fwd(_pad_rows(c_all, 2 * NDEV), ada_w, kv_ada_w, logits)
    (part_all,) = _all_gather([part[:NDEV]], "gather_adaln")
    mine = lax.dynamic_index_in_dim(part_all, me, axis=1, keepdims=False)
    n0, nkv = ada_w.shape[2], kv_ada_w.shape[1]
    mod_names = ["sh1", "sc1", "g1", "sh2", "sc2", "g2"]
    mods = {}
    for l in range(2):
        row = mine[:, l * n0:(l + 1) * n0].reshape(-1) + ada_b[l]
        for k, nm in enumerate(mod_names):
            mods[f"{nm}_{l}"] = row[k * d:(k + 1) * d].reshape(1, d)
    kvrow = mine[:, 2 * n0:2 * n0 + nkv].reshape(-1) + kv_ada_b
    mods["kv_sh"], mods["kv_sc"] = kvrow[:d].reshape(1, d), kvrow[d:].reshape(1, d)

    in_flight = {}

    def start_gather(grp, dep):
        srcs = [shards[n] for n in gather_names[grp]]
        in_flight[grp], started = _xchg_start(srcs, False, SAME_CORE, dep, f"gather_{grp}_start")
        return started

    zero = start_gather("l0b", part_all)
    mods["sh1_0"] = mods["sh1_0"] + zero

    small = {"a_norm_g": a_norm_g, "k_norm_g": k_norm_g.reshape(1, HEAD), "q_norm_g": q_norm_g, "kv_b_f": kv_b_f}
    for l in range(2):
        small[f"conv_w{l}"] = _pad_shard_cols(conv_w_full[l], ncw, ncp).reshape(CONV_TAPS, 2, fp).transpose(1, 0, 2)
        small[f"conv_b{l}"] = _pad_shard_cols(ffn_conv_b[l], ncw, ncp).reshape(2, 1, fp)

    def get_w(grp, after):
        if grp == "l0a":
            return {"a_in": a_in_all}
        arrived = _xchg_wait(in_flight[grp], after, False, SAME_CORE, f"gather_{grp}_wait")
        full = list(_sibling_forward(arrived, f"gather_{grp}_to_sibling"))
        if grp == "l0b":
            started = start_gather("l1", full[0])
            full[0] = full[0] + started.astype(full[0].dtype)
        got = dict(zip(gather_names[grp], full))
        out = {}
        for n, a in got.items():
            if n in ("a_out", "b_out"):
                out[n] = a.reshape(d, d)
            elif n in ("down0", "down1"):
                dn = a.reshape(NCHIP, ff // NCHIP, d)
                out[n] = jnp.pad(dn, ((0, 0), (0, ncp - ncw), (0, 0))).reshape(fp, d)
            elif n == "kv":
                kv_t = a.reshape(NDEV * kv_w.shape[1], d)
                out["kv"] = kv_t[:2 * d]
                out["kv_f"] = jnp.pad(kv_t[2 * d:], ((0, LANES - nh), (0, 0)))
            else:
                out[n] = a
        return out

    scatter_flight, g_last = {}, {}

    def put_g(grp, gr):
        if grp == "l0a":
            g_last.update(gr)
            return zero
        if grp == "l1":
            g_kvw = jnp.concatenate([gr["kv"], gr["kv_f"][:nh].astype(BF16)], axis=0)
            arrs = {"kv_w": g_kvw.reshape(NDEV, kv_w.shape[1], d), "b_w_q": gr["b_q"],
                    "b_w_out": gr["b_out"].reshape(NDEV, d // NDEV, d), "up1": gr["up1"],
                    "down1": gr["down1"].reshape(NCHIP, ncp, d)[:, :ncw].reshape(NDEV, rd, d)}
        else:
            arrs = {"a_w_out": gr["a_out"].reshape(NDEV, d // NDEV, d), "up0": gr["up0"],
                    "down0": gr["down0"].reshape(NCHIP, ncp, d)[:, :ncw].reshape(NDEV, rd, d)}
        srcs = list(arrs.values())
        handles, sent = _xchg_start(srcs, True, ALL_PEERS, srcs[0], f"scatter_{grp}_start")
        scatter_flight[grp] = (list(arrs), handles)
        return sent

    loss_v, grad_x, dmods, dlb, g = _local_step(x[0], loss_target[0], mods, lb, small, get_w, put_g)

    g_sum = {}
    for grp in ("l1", "l0b"):
        names, handles = scatter_flight[grp]
        for nm, a in zip(names, _xchg_wait(handles, grad_x, True, ALL_PEERS, f"scatter_{grp}_wait")):
            g_sum[nm] = _slab_sum(a, f"rs_slab_sum_{nm}")

    def conv_w_grad(a):
        return _unpad_shard_cols(a.transpose(1, 0, 2).reshape(CONV_TAPS, 2 * fp), ncw, ncp)

    def conv_b_grad(a):
        return _unpad_shard_cols(a.reshape(2 * fp), ncw, ncp)

    dmod_vec = [dmods[f"{nm}_{l}"] for l in range(2) for nm in mod_names] + [dmods["kv_sh"], dmods["kv_sc"]]
    post = _pack_small(dmod_vec + [dlb, g["a_norm_g"], g["k_norm_g"], g["q_norm_g"],
                                   jnp.pad(g["kv_b_f"].reshape(-1), (0, LANES - nh)),
                                   conv_w_grad(g["conv_w0"]), conv_w_grad(g["conv_w1"]),
                                   conv_b_grad(g["conv_b0"]), conv_b_grad(g["conv_b1"]), loss_v])
    (post_all,) = _all_gather([post], "gather_small_grads")
    a_in_flight, a_in_sent = _xchg_start([g_last["a_in"]], True, ALL_PEERS, post_all, "scatter_l0a_start")
    a_in_sent = a_in_sent.reshape(1, 1)
    tot = _slab_sum(post_all, "small_grad_sum").reshape(-1)
    nmod = 14 * d
    (t_mod, t_lb, t_ang, t_kng, t_qng, t_bf, t_cw, t_cb, t_loss) = _unpack_small(
        tot, [(nmod,), (1, d), (1, HEAD), (HEAD,), (1, HEAD), (LANES,), (2, CONV_TAPS, two_f), (2, two_f),
              (LANES,)])
    loss = t_loss[0]
    dm_all = post_all.reshape(NDEV, -1)[:, :nmod]
    dm0 = lax.dynamic_slice_in_dim(dm_all[:, :6 * d], me * n0, n0, axis=1)
    dm1 = lax.dynamic_slice_in_dim(dm_all[:, 6 * d:12 * d], me * n0, n0, axis=1)
    dkv = lax.dynamic_slice_in_dim(dm_all[:, 12 * d:], me * nkv, nkv, axis=1)
    g_ada_w, g_kv_ada_w, g_logits = _ada_bwd(c_act, _pad_rows(dm0, 2 * NDEV), _pad_rows(dm1, 2 * NDEV),
                                              _pad_rows(dkv, 2 * NDEV), lb, t_lb)

    grads = {
        "ada_w": g_ada_w,
        "ada_b": t_mod[:12 * d].reshape(2, 6 * d),
        "a_lb_logits": lax.dynamic_slice_in_dim(g_logits, me * HEAD, HEAD, axis=1),
        "a_norm_g": t_ang,
        "a_w_out": g_sum["a_w_out"].reshape(a_w_out.shape),
        "kv_ada_w": g_kv_ada_w,
        "kv_ada_b": t_mod[12 * d:],
        "kv_w": g_sum["kv_w"].T,
        "kv_b_f": t_bf[:nh],
        "k_norm_g": t_kng,
        "b_w_q": g_sum["b_w_q"].reshape(b_w_q.shape),
        "q_norm_g": t_qng,
        "b_w_out": g_sum["b_w_out"].reshape(b_w_out.shape),
        "ffn_w_up": jnp.stack([g_sum["up0"][:, :ncw], g_sum["up1"][:, :ncw]]),
        "ffn_conv_w": lax.dynamic_slice_in_dim(t_cw, me * ncw, ncw, axis=2),
        "ffn_conv_b": t_cb,
        "ffn_w_down": jnp.stack([g_sum["down0"], g_sum["down1"]]),
    }

    big_adam = ["ada_w", "a_w_out", "kv_ada_w", "kv_w", "b_w_q", "b_w_out", "ffn_w_up", "ffn_w_down", "a_w_in"]
    small_adam = [n for n in order if n not in big_adam]
    delta, new_m, new_v = {}, {}, {}
    packs = [_pack_small([src[n] for n in small_adam]) for src in (weights, grads, m_in, v_in)]
    outs = _adamw(*packs, "adamw_small", tr=packs[0].shape[0])
    shapes = [weights[n].shape for n in small_adam]
    for dst, o in zip((delta, new_m, new_v), outs):
        for n, a in zip(small_adam, _unpack_small(o.reshape(-1), shapes)):
            dst[n] = a
    for n in big_adam:
        if n == "a_w_in":
            (landed,) = _xchg_wait(a_in_flight, new_v["ffn_w_down"], True, ALL_PEERS, "scatter_l0a_wait")
            grads[n] = _slab_sum(landed, "rs_slab_sum_a_w_in").reshape(a_w_in.shape)
        shp = weights[n].shape
        two_d = lambda a: a.reshape(-1, shp[-1])
        dl, mn, vn = _adamw(two_d(weights[n]), two_d(grads[n]), two_d(m_in[n]), two_d(v_in[n]), f"adamw_{n}",
                            after=a_in_sent)
        delta[n], new_m[n], new_v[n] = dl.reshape(shp), mn.reshape(shp), vn.reshape(shp)

    return (loss, grad_x.reshape(x.shape), *[grads[n] for n in order], *[delta[n] for n in order],
            *[new_m[n] for n in order], *[new_v[n] for n in order])
```

```python
import functools

import jax
import jax.numpy as jnp
from jax import lax
from jax.experimental import pallas as pl
from jax.experimental.pallas import tpu as pltpu

F32 = jnp.float32
BF16 = jnp.bfloat16

NDEV = 8
NCHIP = 4
HEAD = 128
A_CHUNK = 64
CONV_TAPS = 3
EPS = 1e-6
NEG_INF = -1e30
LANES = 128
VMEM_LIMIT = 48 * 1024 * 1024

ADAM_LR = 0.001
ADAM_B1 = 0.9
ADAM_B2 = 0.999
ADAM_EPS = 1e-08
ADAM_WD = 0.01
ADAM_STEP = 10

_NN = (((1,), (0,)), ((), ()))
_NT = (((1,), (1,)), ((), ()))
_TN = (((0,), (0,)), ((), ()))
_MESH = pl.DeviceIdType.MESH


def _cparams(**kw):
    return pltpu.CompilerParams(vmem_limit_bytes=VMEM_LIMIT, **kw)


def _divisor_tile(n, pref, unit=LANES):
    if n <= pref:
        return n
    best = None
    for t in range(unit, pref + 1, unit):
        if n % t == 0:
            best = t
    assert best is not None, (n, pref)
    return best


def _round_up(n, unit):
    return -(-n // unit) * unit


def _bdot_raw(a, b, dims):
    return lax.dot_general(a.astype(BF16), b.astype(BF16), dims, preferred_element_type=F32)


@jax.custom_vjp
def _dot_nn(a, b):
    return _bdot_raw(a, b, _NN)


@jax.custom_vjp
def _dot_nt(a, b):
    return _bdot_raw(a, b, _NT)


@jax.custom_vjp
def _dot_tn(a, b):
    return _bdot_raw(a, b, _TN)


_dot_nn.defvjp(lambda a, b: (_bdot_raw(a, b, _NN), (a, b)),
               lambda r, g: (_dot_nt(g, r[1]), _dot_tn(r[0], g)))
_dot_nt.defvjp(lambda a, b: (_bdot_raw(a, b, _NT), (a, b)),
               lambda r, g: (_dot_nn(g, r[1]), _dot_tn(g, r[0])))
_dot_tn.defvjp(lambda a, b: (_bdot_raw(a, b, _TN), (a, b)),
               lambda r, g: (_dot_nt(r[1], g), _dot_nn(r[0], g)))


def _f32dot(a, b):
    return lax.dot_general(a, b, _NN, precision=lax.Precision.HIGHEST, preferred_element_type=F32)


def _sigmoid(x):
    return jax.nn.sigmoid(x)


def _silu(x):
    return x * jax.nn.sigmoid(x)


def _rms(x):
    return x * lax.rsqrt(jnp.mean(x * x, axis=-1, keepdims=True) + EPS)


def _modulate(x, sh, sc):
    return _rms(x) * (1.0 + sc) + sh


def _mm_call(a, b, dims, a_spec, b_spec, o_spec, o_shape, grid, acc_tile, name):
    nk = grid[2]

    def body(a_ref, b_ref, o_ref, *acc):
        p = lax.dot_general(a_ref[...].astype(BF16), b_ref[...].astype(BF16), dims,
                            preferred_element_type=F32)
        if nk == 1:
            o_ref[...] = p.astype(o_ref.dtype)
        else:
            kk = pl.program_id(2)

            @pl.when(kk == 0)
            def _():
                acc[0][...] = p

            @pl.when(kk > 0)
            def _():
                acc[0][...] += p

            @pl.when(kk == nk - 1)
            def _():
                o_ref[...] = acc[0][...].astype(o_ref.dtype)

    return pl.pallas_call(
        body, name=name, grid=grid, in_specs=[a_spec, b_spec], out_specs=o_spec, out_shape=o_shape,
        scratch_shapes=[pltpu.VMEM(acc_tile, F32)] if nk > 1 else [],
        compiler_params=_cparams(dimension_semantics=("parallel", "parallel", "arbitrary")),
    )(a, b)


def _mm(a, b, mode, out_dtype, name, tm=1024, tn=1024, tk=2048):
    if mode == "nn":
        (m, k), (k2, n) = a.shape, b.shape
    elif mode == "nt":
        (m, k), (n, k2) = a.shape, b.shape
    else:
        (k, m), (k2, n) = a.shape, b.shape
    assert k == k2, (a.shape, b.shape, mode)
    tm, tn, tk = _divisor_tile(m, tm), _divisor_tile(n, tn), _divisor_tile(k, tk)
    if mode == "tn":
        a_spec = pl.BlockSpec((tk, tm), lambda i, j, kk: (kk, i))
    else:
        a_spec = pl.BlockSpec((tm, tk), lambda i, j, kk: (i, kk))
    if mode == "nt":
        b_spec = pl.BlockSpec((tn, tk), lambda i, j, kk: (j, kk))
    else:
        b_spec = pl.BlockSpec((tk, tn), lambda i, j, kk: (kk, j))
    return _mm_call(a, b, {"nn": _NN, "nt": _NT, "tn": _TN}[mode], a_spec, b_spec,
                    pl.BlockSpec((tm, tn), lambda i, j, kk: (i, j)), jax.ShapeDtypeStruct((m, n), out_dtype),
                    (m // tm, n // tn, k // tk), (tm, tn), name)


def _wblk_act_spec(rows, gb, nl, split, nb, row_axis, blk_axis):
    if split == 1:
        return pl.BlockSpec((rows, gb * nl), lambda *g: (g[row_axis], g[blk_axis]))
    groups = nb // split // gb
    return pl.BlockSpec((None, rows, gb * nl),
                        lambda *g: (g[blk_axis] // groups, g[row_axis], g[blk_axis] % groups))


def _mm_wblk(a, wb, out_dtype, name, *, gb, row_off=0, split=1, tm=1024):
    m, k = a.shape
    nb, _, nl = wb.shape
    assert (nb // split) % gb == 0
    tm = _divisor_tile(m, tm)

    def body(a_ref, b_ref, o_ref):
        av = a_ref[...].astype(BF16)
        for s in range(gb):
            o_ref[:, s * nl:(s + 1) * nl] = lax.dot_general(
                av, b_ref[s].astype(BF16), _NN, preferred_element_type=F32).astype(o_ref.dtype)

    o_shape = (m, nb * nl) if split == 1 else (split, m, nb // split * nl)
    return pl.pallas_call(
        body, name=name, grid=(nb // gb, m // tm),
        in_specs=[pl.BlockSpec((tm, k), lambda j, i: (i, 0)),
                  pl.BlockSpec((gb, k, nl), lambda j, i: (j, row_off, 0))],
        out_specs=_wblk_act_spec(tm, gb, nl, split, nb, 1, 0),
        out_shape=jax.ShapeDtypeStruct(o_shape, out_dtype),
        compiler_params=_cparams(dimension_semantics=("parallel", "parallel")),
    )(a, wb)


def _mm_wblk_dx(dy, wb, out_dtype, name, *, k, gb, row_off=0, split=1, tm=1024):
    nb, _, nl = wb.shape
    m = dy.shape[-2]
    tm = _divisor_tile(m, tm)
    nk = nb // gb
    per = nb // split
    whole = split > 1 and gb == nb
    assert whole or per % gb == 0

    def body(a_ref, b_ref, o_ref, *acc):
        p = None
        for s in range(gb):
            a_blk = a_ref[s // per, :, (s % per) * nl:(s % per + 1) * nl] if whole else a_ref[:, s * nl:(s + 1) * nl]
            q = lax.dot_general(a_blk.astype(BF16), b_ref[s].astype(BF16), _NT, preferred_element_type=F32)
            p = q if p is None else p + q
        if nk == 1:
            o_ref[...] = p.astype(o_ref.dtype)
        else:
            kk = pl.program_id(1)

            @pl.when(kk == 0)
            def _():
                acc[0][...] = p

            @pl.when(kk > 0)
            def _():
                acc[0][...] += p

            @pl.when(kk == nk - 1)
            def _():
                o_ref[...] = acc[0][...].astype(o_ref.dtype)

    return pl.pallas_call(
        body, name=name, grid=(m // tm, nk),
        in_specs=[pl.BlockSpec((split, tm, per * nl), lambda i, kk: (0, i, 0)) if whole
                  else _wblk_act_spec(tm, gb, nl, split, nb, 0, 1),
                  pl.BlockSpec((gb, k, nl), lambda i, kk: (kk, row_off, 0))],
        out_specs=pl.BlockSpec((tm, k), lambda i, kk: (i, 0)),
        out_shape=jax.ShapeDtypeStruct((m, k), out_dtype),
        scratch_shapes=[pltpu.VMEM((tm, k), F32)] if nk > 1 else [],
        compiler_params=_cparams(dimension_semantics=("parallel", "arbitrary")),
    )(dy, wb)


def _mm_wblk_dw(x, dy, name, *, nb, gb, split=1, tk=1024):
    t, k = x.shape
    assert (nb // split) % gb == 0
    nl = dy.shape[-1] * split // nb
    tk = _divisor_tile(t, tk)
    nk = t // tk

    def body(a_ref, b_ref, o_ref, *acc):
        kk = pl.program_id(1)
        av = a_ref[...].astype(BF16)
        for s in range(gb):
            p = lax.dot_general(av, b_ref[:, s * nl:(s + 1) * nl].astype(BF16), _TN, preferred_element_type=F32)
            if nk == 1:
                o_ref[s] = p.astype(o_ref.dtype)
                continue

            @pl.when(kk == 0)
            def _():
                acc[0][s] = p

            @pl.when(kk > 0)
            def _():
                acc[0][s] += p

        if nk > 1:
            @pl.when(kk == nk - 1)
            def _():
                o_ref[...] = acc[0][...].astype(o_ref.dtype)

    return pl.pallas_call(
        body, name=name, grid=(nb // gb, nk),
        in_specs=[pl.BlockSpec((tk, k), lambda j, kk: (kk, 0)), _wblk_act_spec(tk, gb, nl, split, nb, 1, 0)],
        out_specs=pl.BlockSpec((gb, k, nl), lambda j, kk: (j, 0, 0)),
        out_shape=jax.ShapeDtypeStruct((nb, k, nl), BF16),
        scratch_shapes=[pltpu.VMEM((gb, k, nl), F32)] if nk > 1 else [],
        compiler_params=_cparams(dimension_semantics=("parallel", "arbitrary")),
    )(x, dy)


def _row_specs(rows, tb, nsub):
    return [pl.BlockSpec((tb, nsub * cw), functools.partial(lambda i, off: (i, off), off=off))
            for (_, cw, off) in rows]


def _vec_specs(params):
    return [pl.BlockSpec(p.shape, lambda i: (0, 0)) for p in params]


def _row_fwd(f, rows, params, out_dtypes, *, nsub=1, tb, name):
    t = rows[0][0].shape[0]
    tb = min(tb, t)
    n_r, n_p = len(rows), len(params)
    blk = [jax.ShapeDtypeStruct((tb, cw), F32) for (_, cw, _) in rows]
    blk += [jax.ShapeDtypeStruct(p.shape, F32) for p in params]
    out_avals = jax.eval_shape(f, *blk)

    def body(*refs):
        pv = [r[...] for r in refs[n_r:n_r + n_p]]
        for s in range(nsub):
            vals = [r[:, s * cw:(s + 1) * cw].astype(F32) for r, (_, cw, _) in zip(refs[:n_r], rows)]
            outs = f(*vals, *pv)
            for o_ref, o in zip(refs[n_r + n_p:], outs):
                w = o.shape[1]
                o_ref[:, s * w:(s + 1) * w] = o.astype(o_ref.dtype)

    return pl.pallas_call(
        body, name=name,
        grid=(t // tb,),
        in_specs=_row_specs(rows, tb, nsub) + _vec_specs(params),
        out_specs=[pl.BlockSpec((tb, nsub * av.shape[1]), lambda i: (i, 0)) for av in out_avals],
        out_shape=[jax.ShapeDtypeStruct((t, nsub * av.shape[1]), dt) for av, dt in zip(out_avals, out_dtypes)],
        compiler_params=_cparams(dimension_semantics=("parallel",)),
    )(*[r[0] for r in rows], *params)


def _row_bwd(f, rows, params, cots, row_grad_dtypes, *, nsub=1, tb, name, add_to=None, cot_add=None):
    t = rows[0][0].shape[0]
    tb = min(tb, t)
    n_r, n_p, n_c = len(rows), len(params), len(cots)
    want = [j for j in range(n_r) if row_grad_dtypes[j] is not None]
    extra = [] if add_to is None else [(add_to[1], rows[add_to[0]][1], 0)]
    extra += [] if cot_add is None else [(cot_add[1], cots[cot_add[0]][1], 0)]

    def body(*refs):
        i = pl.program_id(0)
        r_in, p_in = refs[:n_r], refs[n_r:n_r + n_p]
        c_in = refs[n_r + n_p:n_r + n_p + n_c]
        e_in = refs[n_r + n_p + n_c:n_r + n_p + n_c + len(extra)]
        outs = refs[n_r + n_p + n_c + len(extra):]
        pv = [r[...] for r in p_in]
        psum = [None] * n_p
        for s in range(nsub):
            vals = [r[:, s * cw:(s + 1) * cw].astype(F32) for r, (_, cw, _) in zip(r_in, rows)]
            cvals = [r[:, s * cw:(s + 1) * cw].astype(F32) for r, (_, cw, _) in zip(c_in, cots)]
            if cot_add is not None:
                cw = cots[cot_add[0]][1]
                cvals[cot_add[0]] = cvals[cot_add[0]] + e_in[-1][:, s * cw:(s + 1) * cw]
            _, vjp_fn = jax.vjp(f, *vals, *pv)
            grads = vjp_fn(tuple(cvals))
            for o_ref, jr in zip(outs[:len(want)], want):
                cw = rows[jr][1]
                gr = grads[jr]
                if add_to is not None and jr == add_to[0]:
                    gr = gr + e_in[0][:, s * cw:(s + 1) * cw]
                o_ref[:, s * cw:(s + 1) * cw] = gr.astype(o_ref.dtype)
            for jp in range(n_p):
                psum[jp] = grads[n_r + jp] if psum[jp] is None else psum[jp] + grads[n_r + jp]
        for o_ref, g in zip(outs[len(want):], psum):
            @pl.when(i == 0)
            def _():
                o_ref[...] = g

            @pl.when(i > 0)
            def _():
                o_ref[...] += g

    out_specs = [pl.BlockSpec((tb, nsub * rows[jr][1]), lambda i: (i, 0)) for jr in want]
    out_shape = [jax.ShapeDtypeStruct((t, nsub * rows[jr][1]), row_grad_dtypes[jr]) for jr in want]
    out_specs += _vec_specs(params)
    out_shape += [jax.ShapeDtypeStruct(p.shape, F32) for p in params]
    res = pl.pallas_call(
        body, name=name,
        grid=(t // tb,),
        in_specs=_row_specs(rows, tb, nsub) + _vec_specs(params) + _row_specs(cots, tb, nsub)
        + _row_specs(extra, tb, nsub),
        out_specs=out_specs, out_shape=out_shape,
        compiler_params=_cparams(dimension_semantics=("arbitrary",)),
    )(*[r[0] for r in rows], *params, *[c[0] for c in cots], *[e[0] for e in extra])
    return res[:len(want)], res[len(want):]


def _f_mod(x, sh, sc):
    return (_modulate(x, sh, sc),)


def _f_res_mod(x, y, g, sh, sc):
    x1 = x + g * y
    return x1, _modulate(x1, sh, sc)


def _f_res_mod2(x, y, g, sh_a, sc_a, sh_b, sc_b):
    x1 = x + g * y
    return x1, _modulate(x1, sh_a, sc_a), _modulate(x1, sh_b, sc_b)


def _f_qnorm(p, g):
    return (_rms(p) * g * (HEAD ** -0.5),)


def _f_knorm(p, g):
    return (_rms(p) * g,)


def _f_qnorm_aug(p, g):
    lane = lax.broadcasted_iota(jnp.int32, p.shape, 1)
    return (jnp.concatenate([_rms(p) * g * (HEAD ** -0.5), jnp.where(lane < 3, 1.0, 0.0)], axis=1),)


def _f_knorm_aug(p, c0, c1, c2, g):
    lane = lax.broadcasted_iota(jnp.int32, p.shape, 1)
    aug = jnp.where(lane == 0, c0, jnp.where(lane == 1, c1, jnp.where(lane == 2, c2, 0.0)))
    return (jnp.concatenate([_rms(p) * g, aug], axis=1),)


def _split3(a):
    round_bf16 = lambda v: lax.reduce_precision(v, exponent_bits=8, mantissa_bits=7)
    hi = round_bf16(a)
    mid = round_bf16(a - hi)
    lo = round_bf16(a - hi - mid)
    return hi.astype(BF16), mid.astype(BF16), lo.astype(BF16)


def _f_outgate(o, og):
    return (o * _sigmoid(og),)


def _loss_call(x3, f, g2, target, tb):
    t, d = x3.shape
    tb = min(tb, t)

    def body(x_ref, f_ref, g_ref, t_ref, loss_ref, dx_ref, df_ref, dg_ref):
        i = pl.program_id(0)
        fv = f_ref[...]
        g = g_ref[...]
        e = x_ref[...] + g * fv - t_ref[...]
        dx = e * (1.0 / d)
        part = 0.5 * jnp.sum(jnp.sum(e * dx, axis=1, keepdims=True), axis=0, keepdims=True)
        dx_ref[...] = dx
        df_ref[...] = (g * dx).astype(df_ref.dtype)
        dg = jnp.sum(dx * fv, axis=0, keepdims=True)

        @pl.when(i == 0)
        def _():
            loss_ref[...] = jnp.broadcast_to(part, loss_ref.shape)
            dg_ref[...] = dg

        @pl.when(i > 0)
        def _():
            loss_ref[...] += jnp.broadcast_to(part, loss_ref.shape)
            dg_ref[...] += dg

    row = pl.BlockSpec((tb, d), lambda i: (i, 0))
    vec = pl.BlockSpec((1, d), lambda i: (0, 0))
    return pl.pallas_call(
        body, name="loss_head",
        grid=(t // tb,),
        in_specs=[row, row, vec, row],
        out_specs=[pl.BlockSpec((1, LANES), lambda i: (0, 0)), row, row, vec],
        out_shape=[jax.ShapeDtypeStruct((1, LANES), F32), jax.ShapeDtypeStruct((t, d), F32),
                   jax.ShapeDtypeStruct((t, d), BF16), jax.ShapeDtypeStruct((1, d), F32)],
        compiler_params=_cparams(dimension_semantics=("arbitrary",)),
    )(x3, f, g2, target)


def _hg_mask(tb):
    br = lax.broadcasted_iota(jnp.int32, (tb, tb), 0)
    bs = lax.broadcasted_iota(jnp.int32, (tb, tb), 1)
    return jnp.logical_and(br // A_CHUNK == bs // A_CHUNK, bs <= br).astype(F32)


def _hg_consts(mask):
    c = A_CHUNK
    r = lax.broadcasted_iota(jnp.int32, (c, c), 0)
    s = lax.broadcasted_iota(jnp.int32, (c, c), 1)
    return (s <= r).astype(F32), (r <= s).astype(F32), mask > 0.5


def _chunk_apply(mat, x):
    c = mat.shape[0]
    return jnp.concatenate([_f32dot(mat, x[i * c:(i + 1) * c]) for i in range(x.shape[0] // c)], axis=0)


@jax.custom_vjp
def _chunk_cumsum(x, tri, tri_t):
    return _chunk_apply(tri, x)


_chunk_cumsum.defvjp(lambda x, tri, tri_t: (_chunk_apply(tri, x), (tri, tri_t)),
                     lambda r, g: (_chunk_apply(r[1], g), jnp.zeros_like(r[0]), jnp.zeros_like(r[1])))


def _per_chunk(a, b, dims):
    return jnp.stack([_bdot_raw(a[i], b[i], dims) for i in range(a.shape[0])])


@jax.custom_vjp
def _chunk_tn(a, b):
    return _per_chunk(a, b, _TN)


@jax.custom_vjp
def _chunk_nt(a, b):
    return _per_chunk(a, b, _NT)


@jax.custom_vjp
def _chunk_nn(a, b):
    return _per_chunk(a, b, _NN)


_chunk_tn.defvjp(lambda a, b: (_per_chunk(a, b, _TN), (a, b)),
                 lambda r, g: (_chunk_nt(r[1], g), _chunk_nn(r[0], g)))
_chunk_nt.defvjp(lambda a, b: (_per_chunk(a, b, _NT), (a, b)),
                 lambda r, g: (_chunk_nn(g, r[1]), _chunk_tn(g, r[0])))
_chunk_nn.defvjp(lambda a, b: (_per_chunk(a, b, _NN), (a, b)),
                 lambda r, g: (_chunk_nt(g, r[1]), _chunk_tn(r[0], g)))


def _scan_states(decay, m, st):
    sts = []
    for i in range(m.shape[0]):
        sts.append(st)
        st = st * decay[i] + m[i]
    return jnp.stack(sts), st


@jax.custom_vjp
def _state_scan(decay, m, st):
    return _scan_states(decay, m, st)


def _state_scan_fwd(decay, m, st):
    sts, st_out = _scan_states(decay, m, st)
    return (sts, st_out), (decay, sts)


def _state_scan_bwd(res, cts):
    decay, sts = res
    d_sts, g = cts
    d_decay, d_m = [], []
    for i in range(sts.shape[0] - 1, -1, -1):
        d_m.append(g)
        d_decay.append(jnp.sum(g * sts[i], axis=0, keepdims=True))
        g = g * decay[i] + d_sts[i]
    return jnp.stack(d_decay[::-1]), jnp.stack(d_m[::-1]), g


_state_scan.defvjp(_state_scan_fwd, _state_scan_bwd)


def _hg_block(qp, fp, ip, gp, lb, ng, st, tri, tri_t, bd_causal):
    tb = qp.shape[0]
    c = A_CHUNK
    n = tb // c
    q = _silu(qp)
    fg = lb + (1.0 - lb) * _sigmoid(fp)
    logf = jnp.log(fg)
    k = 1.0 - fg
    b3 = _chunk_cumsum(logf, tri, tri_t).reshape(n, c, HEAD)
    pos = lax.broadcasted_iota(jnp.int32, (1, c, 1), 1)
    b_mid = lax.stop_gradient(jnp.sum(jnp.where(pos == c // 2, b3, 0.0), axis=1, keepdims=True))
    b_last = jnp.sum(jnp.where(pos == c - 1, b3, 0.0), axis=1, keepdims=True)
    q3, k3, v3 = q.reshape(n, c, HEAD), k.reshape(n, c, HEAD), ip.reshape(n, c, HEAD)
    scores = _dot_nt((q3 * jnp.exp(b3 - b_mid)).reshape(tb, HEAD), (k3 * jnp.exp(b_mid - b3)).reshape(tb, HEAD))
    o_intra = _dot_nn(jnp.where(bd_causal, scores, 0.0), ip)
    states, st_new = _state_scan(jnp.exp(b_last), _chunk_tn(v3, k3 * jnp.exp(b_last - b3)), st)
    o = o_intra + _chunk_nt(q3 * jnp.exp(b3), states).reshape(tb, HEAD)
    y = _rms(o) * ng * _silu(gp)
    return y, st_new


HG_HEADS = 2


def _hg_specs(tb, nh, rev_nb=None):
    wide = HG_HEADS * HEAD
    per = nh // HG_HEADS

    def row(part):
        if rev_nb is None:
            return pl.BlockSpec((tb, wide), functools.partial(lambda h, i, off: (i, off + h), off=part * per))
        return pl.BlockSpec((tb, wide),
                            functools.partial(lambda h, i, off: (rev_nb - 1 - i, off + h), off=part * per))
    return [row(0), row(1), row(2), row(3),
            pl.BlockSpec((1, wide), lambda h, i: (0, h)), pl.BlockSpec((1, HEAD), lambda h, i: (0, 0)),
            pl.BlockSpec((tb, tb), lambda h, i: (0, 0))]


def _hgrn2_fwd(proj, lb, ng, tb):
    t = proj.shape[0]
    nh = proj.shape[1] // (4 * HEAD)
    tb = min(tb, t)
    nb = t // tb
    wide = HG_HEADS * HEAD

    def body(q_ref, f_ref, i_ref, g_ref, lb_ref, ng_ref, mask_ref, y_ref, s_ref, st_ref):
        i = pl.program_id(1)

        @pl.when(i == 0)
        def _():
            st_ref[...] = jnp.zeros_like(st_ref)

        consts = _hg_consts(mask_ref[...])
        for p in range(HG_HEADS):
            cs = slice(p * HEAD, (p + 1) * HEAD)
            st = st_ref[p]
            s_ref[p, 0] = st
            y, st_new = _hg_block(q_ref[:, cs], f_ref[:, cs], i_ref[:, cs], g_ref[:, cs], lb_ref[:, cs],
                                  ng_ref[...], st, *consts)
            y_ref[:, cs] = y.astype(y_ref.dtype)
            st_ref[p] = st_new

    return pl.pallas_call(
        body, name="hgrn2_fwd",
        grid=(nh // HG_HEADS, nb),
        in_specs=_hg_specs(tb, nh),
        out_specs=[pl.BlockSpec((tb, wide), lambda h, i: (i, h)),
                   pl.BlockSpec((HG_HEADS, 1, HEAD, HEAD), lambda h, i: (h, i, 0, 0))],
        out_shape=[jax.ShapeDtypeStruct((t, nh * HEAD), BF16),
                   jax.ShapeDtypeStruct((nh, nb, HEAD, HEAD), F32)],
        scratch_shapes=[pltpu.VMEM((HG_HEADS, HEAD, HEAD), F32)],
        compiler_params=_cparams(dimension_semantics=("parallel", "arbitrary")),
    )(proj, proj, proj, proj, lb, ng, _hg_mask(tb))


def _hgrn2_bwd(proj, lb, ng, states, dy, tb):
    t = proj.shape[0]
    nh = proj.shape[1] // (4 * HEAD)
    tb = min(tb, t)
    nb = t // tb
    wide = HG_HEADS * HEAD

    def body(q_ref, f_ref, i_ref, g_ref, lb_ref, ng_ref, mask_ref, s_ref, dy_ref,
             dp_ref, dlb_ref, dng_ref, dst_ref):
        h, i = pl.program_id(0), pl.program_id(1)
        consts = _hg_consts(mask_ref[...])

        @pl.when(i == 0)
        def _():
            dst_ref[...] = jnp.zeros_like(dst_ref)
            dlb_ref[...] = jnp.zeros_like(dlb_ref)

        @pl.when(jnp.logical_and(i == 0, h == 0))
        def _():
            dng_ref[...] = jnp.zeros_like(dng_ref)

        def fn(qp, fp, ip, gp, lbx, ngx, stx):
            return _hg_block(qp, fp, ip, gp, lbx, ngx, stx, *consts)

        for p in range(HG_HEADS):
            cs = slice(p * HEAD, (p + 1) * HEAD)
            _, vjp_fn = jax.vjp(fn, q_ref[:, cs], f_ref[:, cs], i_ref[:, cs], g_ref[:, cs], lb_ref[:, cs],
                                ng_ref[...], s_ref[p, 0])
            *gparts, glb, gng, dst = vjp_fn((dy_ref[:, cs].astype(F32), dst_ref[p]))
            for part, gpart in enumerate(gparts):
                dp_ref[part, :, cs] = gpart.astype(dp_ref.dtype)
            dst_ref[p] = dst
            dlb_ref[:, cs] += glb
            dng_ref[...] += gng

    rev = lambda h, i: (nb - 1 - i, h)
    return pl.pallas_call(
        body, name="hgrn2_bwd",
        grid=(nh // HG_HEADS, nb),
        in_specs=_hg_specs(tb, nh, rev_nb=nb) + [
            pl.BlockSpec((HG_HEADS, 1, HEAD, HEAD), lambda h, i: (h, nb - 1 - i, 0, 0)),
            pl.BlockSpec((tb, wide), rev)],
        out_specs=[pl.BlockSpec((4, tb, wide), lambda h, i: (0, nb - 1 - i, h)),
                   pl.BlockSpec((1, wide), lambda h, i: (0, h)), pl.BlockSpec((1, HEAD), lambda h, i: (0, 0))],
        out_shape=[jax.ShapeDtypeStruct((4, t, nh * HEAD), BF16),
                   jax.ShapeDtypeStruct((1, nh * HEAD), F32), jax.ShapeDtypeStruct((1, HEAD), F32)],
        scratch_shapes=[pltpu.VMEM((HG_HEADS, HEAD, HEAD), F32)],
        compiler_params=_cparams(dimension_semantics=("arbitrary", "arbitrary")),
    )(proj, proj, proj, proj, lb, ng, _hg_mask(tb), states, dy)


def _fgate_consts(cb):
    r = lax.broadcasted_iota(jnp.int32, (cb, cb), 0)
    s = lax.broadcasted_iota(jnp.int32, (cb, cb), 1)
    return (r <= s).astype(F32), (r >= s).astype(F32)


def _fgate_fwd(xt, bias, cb=512):
    nh, t = xt.shape
    cb = min(cb, t)

    def body(x_ref, b_ref, o_ref):
        upper, _ = _fgate_consts(cb)
        carry = jnp.zeros((nh, 1), F32)
        for blk in range(t // cb):
            z = x_ref[:, blk * cb:(blk + 1) * cb] + b_ref[...]
            logf = jnp.minimum(z, 0.0) - jnp.log(1.0 + jnp.exp(-jnp.abs(z)))
            cs = _f32dot(logf, upper) + carry
            o_ref[:, blk * cb:(blk + 1) * cb] = cs
            carry = cs[:, cb - 1:cb]

    vm = pl.BlockSpec(memory_space=pltpu.VMEM)
    return pl.pallas_call(
        body, name="fgate_fwd", in_specs=[vm, vm], out_specs=vm,
        out_shape=jax.ShapeDtypeStruct((nh, t), F32), compiler_params=_cparams(),
    )(xt, bias)


def _fgate_bwd(xt, bias, dft, cb=512):
    nh, t = xt.shape
    cb = min(cb, t)
    nblk = t // cb

    def body(x_ref, b_ref, d_ref, dx_ref, db_ref):
        _, lower = _fgate_consts(cb)
        carry = jnp.zeros((nh, 1), F32)
        db = jnp.zeros((nh, 1), F32)
        for blk in range(nblk - 1, -1, -1):
            sl = slice(blk * cb, (blk + 1) * cb)
            dlogf = _f32dot(d_ref[:, sl], lower) + carry
            carry = dlogf[:, 0:1]
            z = x_ref[:, sl] + b_ref[...]
            dz = dlogf * (1.0 - _sigmoid(z))
            dx_ref[:, sl] = dz
            db = db + jnp.sum(dz, axis=1, keepdims=True)
        db_ref[...] = db

    vm = pl.BlockSpec(memory_space=pltpu.VMEM)
    return pl.pallas_call(
        body, name="fgate_bwd", in_specs=[vm, vm, vm], out_specs=[vm, vm],
        out_shape=[jax.ShapeDtypeStruct((nh, t), F32), jax.ShapeDtypeStruct((nh, 1), F32)],
        compiler_params=_cparams(),
    )(xt, bias, dft)


def _attn_fwd(q, k, v, f_col, blk):
    t, width = v.shape
    nh = width // HEAD
    nq = t // blk

    def body(q_ref, k_ref, v_ref, fc_ref, o_ref, lse_ref):
        i = pl.program_id(0)
        tri = (lax.broadcasted_iota(jnp.int32, (blk, blk), 1) <= lax.broadcasted_iota(jnp.int32, (blk, blk), 0))
        for h in range(nh):
            cs = slice(h * HEAD, (h + 1) * HEAD)
            cs2 = slice(2 * h * HEAD, 2 * (h + 1) * HEAD)
            qh = q_ref[:, cs2]

            def tile(j, carry, masked):
                m, l, acc = carry
                rs = pl.ds(pl.multiple_of(j * blk, blk), blk)
                s = _bdot_raw(qh, k_ref[rs, cs2], _NT)
                if masked:
                    s = jnp.where(tri, s, NEG_INF)
                m_new = jnp.maximum(m, jnp.max(s, axis=1, keepdims=True))
                p = jnp.exp(s - m_new)
                alpha = jnp.exp(m - m_new)
                l_new = alpha * l + jnp.sum(p, axis=1, keepdims=True)
                acc_new = alpha * acc + _bdot_raw(p, v_ref[rs, cs], _NN)
                return m_new, l_new, acc_new

            init = (jnp.full((blk, 1), NEG_INF, F32), jnp.zeros((blk, 1), F32), jnp.zeros((blk, HEAD), F32))
            carry = lax.fori_loop(0, i, lambda j, c: tile(j, c, False), init)
            m, l, acc = tile(i, carry, True)
            o_ref[:, cs] = acc / l
            lse_ref[:, h:h + 1] = m + jnp.log(l) + fc_ref[:, h:h + 1]

    vm = pl.BlockSpec(memory_space=pltpu.VMEM)
    return pl.pallas_call(
        body, name="fox_attn_fwd",
        grid=(nq,),
        in_specs=[pl.BlockSpec((blk, 2 * width), lambda i: (i, 0)), vm, vm,
                  pl.BlockSpec((blk, nh), lambda i: (i, 0))],
        out_specs=[pl.BlockSpec((blk, width), lambda i: (i, 0)), pl.BlockSpec((blk, nh), lambda i: (i, 0))],
        out_shape=[jax.ShapeDtypeStruct((t, width), F32), jax.ShapeDtypeStruct((t, nh), F32)],
        compiler_params=_cparams(dimension_semantics=("parallel",)),
    )(q, k, v, f_col)


def _attn_delta(do, o, tb):
    t, width = o.shape
    nh = width // HEAD
    tb = min(tb, t)

    def body(do_ref, o_ref, dl_ref):
        for h in range(nh):
            cs = slice(h * HEAD, (h + 1) * HEAD)
            dl_ref[:, h:h + 1] = jnp.sum(do_ref[:, cs].astype(F32) * o_ref[:, cs], axis=1, keepdims=True)

    wide = pl.BlockSpec((tb, width), lambda i: (i, 0))
    return pl.pallas_call(body, name="fox_attn_delta", grid=(t // tb,), in_specs=[wide, wide],
                          out_specs=pl.BlockSpec((tb, nh), lambda i: (i, 0)),
                          out_shape=jax.ShapeDtypeStruct((t, nh), F32),
                          compiler_params=_cparams(dimension_semantics=("parallel",)))(do, o)


ATTN_BWD_GROUPS = 4


def _attn_bwd(q, k, v, f_col, do, lse, delta, blk):
    t, width = v.shape
    nh = width // HEAD
    nq = t // blk
    hpg = nh // ATTN_BWD_GROUPS
    gw = hpg * HEAD

    def body(q_ref, do_ref, k_ref, v_ref, fc_ref, lse_ref, dl_ref,
             dq_ref, dk_ref, dv_ref, dfc_ref, dfr_ref):
        g, j = pl.program_id(0), pl.program_id(1)
        tri = (lax.broadcasted_iota(jnp.int32, (blk, blk), 1) <= lax.broadcasted_iota(jnp.int32, (blk, blk), 0))

        @pl.when(j == 0)
        def _():
            dq_ref[...] = jnp.zeros_like(dq_ref)
            dfc_ref[...] = jnp.zeros_like(dfc_ref)

        for h in range(hpg):
            cs = slice(h * HEAD, (h + 1) * HEAD)
            cs2 = slice(2 * h * HEAD, 2 * (h + 1) * HEAD)
            csq = slice(2 * h * HEAD, (2 * h + 1) * HEAD)
            kj2 = k_ref[:, cs2]
            kj = k_ref[:, csq]
            vj = v_ref[:, cs]

            def tile(i, carry, masked):
                dk, dv, dfs = carry
                rs = pl.ds(pl.multiple_of(i * blk, blk), blk)
                qi = q_ref[rs, csq]
                doi = do_ref[rs, cs]
                bias = fc_ref[0, rs, h:h + 1] - lse_ref[0, rs, h:h + 1]
                p = jnp.exp(_bdot_raw(q_ref[rs, cs2], kj2, _NT) + bias)
                if masked:
                    p = jnp.where(tri, p, 0.0)
                ds = p * (_bdot_raw(doi, vj, _NT) - dl_ref[0, rs, h:h + 1])
                dsb = ds.astype(BF16)
                dq_ref[rs, cs] += _bdot_raw(dsb, kj, _NN)
                dfc_ref[0, rs, h:h + 1] += jnp.sum(ds, axis=1, keepdims=True)
                return (dk + _bdot_raw(dsb, qi, _TN), dv + _bdot_raw(p, doi, _TN),
                        dfs - jnp.sum(ds, axis=0, keepdims=True))

            init = (jnp.zeros((blk, HEAD), F32), jnp.zeros((blk, HEAD), F32), jnp.zeros((1, blk), F32))
            carry = tile(j, init, True)
            dk, dv, dfs = lax.fori_loop(j + 1, nq, lambda i, c: tile(i, c, False), carry)
            dk_ref[:, cs] = dk
            dv_ref[:, cs] = dv.astype(dv_ref.dtype)
            dfr_ref[0, 0, h:h + 1, :] = dfs

    by_group = lambda a: a.reshape(t, ATTN_BWD_GROUPS, hpg).transpose(1, 0, 2)
    once = pl.Buffered(1)
    stat = pl.BlockSpec((1, t, hpg), lambda g, j: (g, 0, 0), pipeline_mode=once)
    kv_blk = pl.BlockSpec((blk, gw), lambda g, j: (j, g))
    frow = pl.BlockSpec((1, 1, hpg, blk), lambda g, j: (g, j, 0, 0))
    dq, dk, dv, dfc, dfr = pl.pallas_call(
        body, name="fox_attn_bwd",
        grid=(ATTN_BWD_GROUPS, nq),
        in_specs=[pl.BlockSpec((t, 2 * gw), lambda g, j: (0, g), pipeline_mode=once),
                  pl.BlockSpec((t, gw), lambda g, j: (0, g), pipeline_mode=once),
                  pl.BlockSpec((blk, 2 * gw), lambda g, j: (j, g)), kv_blk, stat, stat, stat],
        out_specs=[pl.BlockSpec((t, gw), lambda g, j: (0, g)), kv_blk, kv_blk,
                   pl.BlockSpec((1, t, hpg), lambda g, j: (g, 0, 0)), frow],
        out_shape=[jax.ShapeDtypeStruct((t, width), F32), jax.ShapeDtypeStruct((t, width), F32),
                   jax.ShapeDtypeStruct((t, width), BF16), jax.ShapeDtypeStruct((ATTN_BWD_GROUPS, t, hpg), F32),
                   jax.ShapeDtypeStruct((ATTN_BWD_GROUPS, nq, hpg, blk), F32)],
        compiler_params=_cparams(dimension_semantics=("parallel", "arbitrary")),
    )(q, do, k, v, by_group(f_col), by_group(lse), by_group(delta))
    return (dq, dk, dv, dfc.transpose(1, 0, 2).reshape(t, nh),
            dfr.transpose(1, 0, 2, 3).reshape(nq, nh, blk))


SUBLANES = 8


def _shift_down(u, n):
    r = pltpu.roll(u, n, 0)
    row = lax.broadcasted_iota(jnp.int32, (SUBLANES, u.shape[1]), 0)
    return jnp.concatenate([jnp.where(row < n, 0.0, r[:SUBLANES]), r[SUBLANES:]], axis=0)


def _shift_up(u, n):
    t = u.shape[0]
    r = pltpu.roll(u, t - n, 0)
    row = lax.broadcasted_iota(jnp.int32, (SUBLANES, u.shape[1]), 0)
    return jnp.concatenate([r[:t - SUBLANES], jnp.where(row >= SUBLANES - n, 0.0, r[t - SUBLANES:])], axis=0)


def _convglu_specs(t):
    return [pl.BlockSpec((2, t, LANES), lambda j: (0, 0, j)),
            pl.BlockSpec((2, CONV_TAPS, LANES), lambda j: (0, 0, j)),
            pl.BlockSpec((2, 1, LANES), lambda j: (0, 0, j))]


def _convglu_fwd(u, cw, cb):
    _, t, fp = u.shape

    def body(u_ref, w_ref, b_ref, a_ref, c_ref):
        c = []
        for hf in range(2):
            uv, w = u_ref[hf], w_ref[hf]
            c.append(w[0:1] * _shift_down(uv, 2) + w[1:2] * _shift_down(uv, 1) + w[2:3] * uv + b_ref[hf])
            c_ref[hf] = c[hf].astype(c_ref.dtype)
        a_ref[...] = (_silu(c[0]) * c[1]).astype(a_ref.dtype)

    return pl.pallas_call(
        body, name="convglu_fwd",
        grid=(fp // LANES,),
        in_specs=_convglu_specs(t),
        out_specs=[pl.BlockSpec((t, LANES), lambda j: (0, j)), pl.BlockSpec((2, t, LANES), lambda j: (0, 0, j))],
        out_shape=[jax.ShapeDtypeStruct((t, fp), BF16), jax.ShapeDtypeStruct((2, t, fp), BF16)],
        compiler_params=_cparams(dimension_semantics=("parallel",)),
    )(u, cw, cb)


def _convglu_bwd(u, c, cw, da):
    _, t, fp = u.shape

    def body(u_ref, c_ref, w_ref, da_ref, du_ref, dw_ref, db_ref):
        gc, vc = c_ref[0].astype(F32), c_ref[1].astype(F32)
        sg = _sigmoid(gc)
        dav = da_ref[...].astype(F32)
        dcs = [dav * vc * (sg * (1.0 + gc * (1.0 - sg))), dav * (gc * sg)]
        for hf in range(2):
            dc, w, uv = dcs[hf], w_ref[hf], u_ref[hf]
            dc1, dc2 = _shift_up(dc, 1), _shift_up(dc, 2)
            du_ref[hf] = (w[2:3] * dc + w[1:2] * dc1 + w[0:1] * dc2).astype(du_ref.dtype)
            dw_ref[hf, 0:1, :] = jnp.sum(dc2 * uv, axis=0, keepdims=True)
            dw_ref[hf, 1:2, :] = jnp.sum(dc1 * uv, axis=0, keepdims=True)
            dw_ref[hf, 2:3, :] = jnp.sum(dc * uv, axis=0, keepdims=True)
            db_ref[hf] = jnp.sum(dc, axis=0, keepdims=True)

    pair, taps, bias = _convglu_specs(t)
    return pl.pallas_call(
        body, name="convglu_bwd",
        grid=(fp // LANES,),
        in_specs=[pair, pair, taps, pl.BlockSpec((t, LANES), lambda j: (0, j))],
        out_specs=[pair, taps, bias],
        out_shape=[jax.ShapeDtypeStruct((2, t, fp), BF16), jax.ShapeDtypeStruct((2, CONV_TAPS, fp), F32),
                   jax.ShapeDtypeStruct((2, 1, fp), F32)],
        compiler_params=_cparams(dimension_semantics=("parallel",)),
    )(u, c, cw, da)


def _local_step(x, target, mods, lb, small, get_w, put_g, *, tb=512, attn_blk=512):
    t, d = x.shape
    nh = d // HEAD
    nb = NDEV
    wts = {}
    vec = lambda *names: [mods[n] for n in names]

    def ffn_fwd(h2, l):
        u = _mm_wblk(h2, wts[f"up{l}"], F32, f"ffn{l}_up", gb=nb // 2, split=2, tm=512)
        a, c = _convglu_fwd(u, small[f"conv_w{l}"], small[f"conv_b{l}"])
        f = _mm(a, wts[f"down{l}"], "nn", F32, f"ffn{l}_down", tk=4096)
        return (u, c), a, f

    def ffn_bwd(df, h2, uc, a, l):
        u, c = uc
        da = _mm(df, wts[f"down{l}"], "nt", BF16, f"ffn{l}_down_dx", tn=1536)
        dwd = _mm(a, df, "tn", BF16, f"ffn{l}_down_dw", tm=1536, tk=1024)
        du, dcw, dcb = _convglu_bwd(u, c, small[f"conv_w{l}"], da)
        dh2 = _mm_wblk_dx(du, wts[f"up{l}"], BF16, f"ffn{l}_up_dx", k=d, gb=nb // 2, split=2, tm=1024)
        dwu = _mm_wblk_dw(h2, du, f"ffn{l}_up_dw", nb=nb, gb=1, split=2, tk=t)
        return dh2, dwu, dwd, dcw, dcb

    (h_a,) = _row_fwd(_f_mod, [(x, d, 0)], vec("sh1_0", "sc1_0"), [BF16], tb=tb, name="l0_mod1")
    wts.update(get_w("l0a", h_a))
    proj_a = _mm_wblk(h_a, wts["a_in"], F32, "a_in", gb=nb // 2)
    ypre, states = _hgrn2_fwd(proj_a, lb, small["a_norm_g"], tb)
    wts.update(get_w("l0b", ypre))
    y_a = _mm(ypre, wts["a_out"], "nn", F32, "a_out")
    x1, h2_0 = _row_fwd(_f_res_mod, [(x, d, 0), (y_a, d, 0)], vec("g1_0", "sh2_0", "sc2_0"), [F32, BF16],
                        tb=tb, name="l0_res_mod2")
    u0, a0, f0 = ffn_fwd(h2_0, 0)
    x2, h_kv, h_q = _row_fwd(_f_res_mod2, [(x1, d, 0), (f0, d, 0)],
                             vec("g2_0", "kv_sh", "kv_sc", "sh1_1", "sc1_1"), [F32, BF16, BF16],
                             tb=tb, name="l0_res_kvmod_qmod")
    wts.update(get_w("l1", h_kv))
    proj_kv = _mm(h_kv, wts["kv"], "nt", F32, "kv_proj")
    proj_f = _mm(h_kv, wts["kv_f"], "nt", F32, "kv_fproj")
    v_b = proj_kv[:, d:].astype(BF16)
    f_logit_t = proj_f[:, :nh].T
    f_bias = small["kv_b_f"].reshape(nh, 1)
    f_col = _fgate_fwd(f_logit_t, f_bias).T
    (k_n,) = _row_fwd(_f_knorm_aug, [(proj_kv, HEAD, 0)] + [(piece, 1, 0) for piece in _split3(-f_col)],
                      [small["k_norm_g"]], [BF16], nsub=nh, tb=tb, name="k_norm")
    proj_q = _mm_wblk(h_q, wts["b_q"], F32, "b_q", gb=nb)
    (q_n,) = _row_fwd(_f_qnorm_aug, [(proj_q, HEAD, 0)], [small["q_norm_g"]], [BF16], nsub=nh, tb=tb,
                      name="q_norm")
    o_att, lse = _attn_fwd(q_n, k_n, v_b, f_col, attn_blk)
    (z,) = _row_fwd(_f_outgate, [(o_att, HEAD, 0), (proj_q, HEAD, 1)], [], [BF16], nsub=nh, tb=tb, name="out_gate")
    y_b = _mm(z, wts["b_out"], "nn", F32, "b_out")
    x3, h2_1 = _row_fwd(_f_res_mod, [(x2, d, 0), (y_b, d, 0)], vec("g1_1", "sh2_1", "sc2_1"), [F32, BF16],
                        tb=tb, name="l1_res_mod2")
    u1, a1, f1 = ffn_fwd(h2_1, 1)
    loss, dx4, df1, dg2_1 = _loss_call(x3, f1, mods["g2_1"], target, tb)

    g = {}
    dmods = {"g2_1": dg2_1}
    dh2, g["up1"], g["down1"], g["conv_w1"], g["conv_b1"] = ffn_bwd(df1, h2_1, u1, a1, 1)
    (dx2, dy_b), (dmods["g1_1"], dmods["sh2_1"], dmods["sc2_1"]) = _row_bwd(
        _f_res_mod, [(x2, d, 0), (y_b, d, 0)], vec("g1_1", "sh2_1", "sc2_1"),
        [(dx4, d, 0), (dh2, d, 0)], [F32, BF16], tb=tb, name="l1_res_mod2_bwd")
    dz = _mm(dy_b, wts["b_out"], "nt", BF16, "b_out_dx")
    g["b_out"] = _mm(z, dy_b, "tn", BF16, "b_out_dw", tk=1024)
    (do_att, dog), _ = _row_bwd(_f_outgate, [(o_att, HEAD, 0), (proj_q, HEAD, 1)], [], [(dz, HEAD, 0)],
                                [BF16, BF16], nsub=nh, tb=tb, name="out_gate_bwd")
    delta = _attn_delta(do_att, o_att, tb)
    dq_n, dk_n, dv, dfc_q, dfr_k = _attn_bwd(q_n, k_n, v_b, f_col, do_att, lse, delta, attn_blk)
    (dpq,), (g["q_norm_g"],) = _row_bwd(_f_qnorm, [(proj_q, HEAD, 0)], [small["q_norm_g"]],
                                        [(dq_n, HEAD, 0)], [BF16], nsub=nh, tb=tb, name="q_norm_bwd")
    dproj_q = jnp.concatenate([dpq, dog], axis=1)
    dh_q = _mm_wblk_dx(dproj_q, wts["b_q"], BF16, "b_q_dx", k=d, gb=nb)
    g["b_q"] = _mm_wblk_dw(h_q, dproj_q, "b_q_dw", nb=nb, gb=nb // 4, tk=t)
    (dpk,), (g["k_norm_g"],) = _row_bwd(_f_knorm, [(proj_kv, HEAD, 0)], [small["k_norm_g"]],
                                        [(dk_n, HEAD, 0)], [BF16], nsub=nh, tb=tb, name="k_norm_bwd")
    dproj_kv = jnp.concatenate([dpk, dv], axis=1)
    df_t = dfc_q.T + dfr_k.transpose(1, 0, 2).reshape(nh, t)
    dflogit_t, g["kv_b_f"] = _fgate_bwd(f_logit_t, f_bias, df_t)
    dproj_f = jnp.pad(dflogit_t.T, ((0, 0), (0, LANES - nh))).astype(BF16)
    dh_kv = _mm(dproj_kv, wts["kv"], "nn", BF16, "kv_proj_dx")
    dh_kv_f = _mm(dproj_f, wts["kv_f"], "nn", BF16, "kv_fproj_dx")
    g["kv"] = _mm(dproj_kv, h_kv, "tn", BF16, "kv_proj_dw", tk=1024)
    g["kv_f"] = _mm(dproj_f, h_kv, "tn", F32, "kv_fproj_dw", tk=1024)
    sent = put_g("l1", {n: g.pop(n) for n in ("b_out", "b_q", "kv", "kv_f", "up1", "down1")})
    (dx1, df0), (dmods["g2_0"], dmods["kv_sh"], dmods["kv_sc"], dmods["sh1_1"], dmods["sc1_1"]) = _row_bwd(
        _f_res_mod2, [(x1, d, 0), (f0, d, 0)], [mods["g2_0"] + sent] + vec("kv_sh", "kv_sc", "sh1_1", "sc1_1"),
        [(dx2, d, 0), (dh_kv, d, 0), (dh_q, d, 0)], [F32, BF16], tb=tb, name="l0_res_kvmod_qmod_bwd",
        cot_add=(1, dh_kv_f))
    dh2, g["up0"], g["down0"], g["conv_w0"], g["conv_b0"] = ffn_bwd(df0, h2_0, u0, a0, 0)
    (dx0, dy_a), (dmods["g1_0"], dmods["sh2_0"], dmods["sc2_0"]) = _row_bwd(
        _f_res_mod, [(x, d, 0), (y_a, d, 0)], vec("g1_0", "sh2_0", "sc2_0"),
        [(dx1, d, 0), (dh2, d, 0)], [F32, BF16], tb=tb, name="l0_res_mod2_bwd")
    dypre = _mm(dy_a, wts["a_out"], "nt", BF16, "a_out_dx")
    g["a_out"] = _mm(ypre, dy_a, "tn", BF16, "a_out_dw", tk=1024)
    sent = put_g("l0b", {n: g.pop(n) for n in ("a_out", "up0", "down0")})
    dproj_a, dlb, g["a_norm_g"] = _hgrn2_bwd(proj_a, lb + sent, small["a_norm_g"], states, dypre, tb)
    dh_a = _mm_wblk_dx(dproj_a, wts["a_in"], BF16, "a_in_dx", k=d, gb=nb, split=4, tm=512)
    put_g("l0a", {"a_in": _mm_wblk_dw(h_a, dproj_a, "a_in_dw", nb=nb, gb=1, split=4, tk=t)})
    (grad_x,), (dmods["sh1_0"], dmods["sc1_0"]) = _row_bwd(
        _f_mod, [(x, d, 0)], vec("sh1_0", "sc1_0"), [(dh_a, d, 0)], [F32], tb=tb, name="l0_mod1_bwd",
        add_to=(0, dx0))
    return loss, grad_x, dmods, dlb, g


def _position():
    return lax.axis_index("x"), lax.axis_index("y"), lax.axis_index("c")


def _hbm_specs(n):
    return [pl.BlockSpec(memory_space=pl.ANY)] * n


def _all_gather(arrs, name):
    n = len(arrs)

    def body(*refs):
        x_refs, out_refs = refs[:n], refs[n:2 * n]
        send_sems, recv_sems, local_sems = refs[2 * n:]
        x, y, cc = _position()
        me, sibling = (x, y, cc), (x, y, 1 - cc)
        chips = [(1 - x, y), (x, 1 - y), (1 - x, 1 - y)]

        def copy(a, k, block, to, src=None):
            slot = out_refs[a].at[4 * block[0] + 2 * block[1] + block[2]]
            return pltpu.make_async_remote_copy(
                src_ref=slot if src is None else src, dst_ref=slot,
                send_sem=send_sems.at[7 * a + k], recv_sem=recv_sems.at[7 * a + k],
                device_id=to, device_id_type=_MESH)

        local = [pltpu.make_async_copy(x_refs[a], out_refs[a].at[4 * x + 2 * y + cc], local_sems.at[a])
                 for a in range(n)]
        for cp in local:
            cp.start()
        first = []
        for a in range(n):
            first.append(copy(a, 0, me, sibling, src=x_refs[a]))
            first += [copy(a, 1 + j, me, (*chip, cc), src=x_refs[a]) for j, chip in enumerate(chips)]
        for cp in first:
            cp.start()
        passed = []
        for j, chip in enumerate(chips):
            for a in range(n):
                copy(a, 1 + j, (*chip, cc), me).wait_recv()
                fwd = copy(a, 4 + j, (*chip, cc), sibling)
                fwd.start()
                passed.append(fwd)
        for a in range(n):
            copy(a, 0, sibling, me).wait_recv()
        for j, chip in enumerate(chips):
            for a in range(n):
                copy(a, 4 + j, (*chip, 1 - cc), me).wait_recv()
        for cp in first + passed:
            cp.wait_send()
        for cp in local:
            cp.wait()

    return pl.pallas_call(
        body, name=name,
        out_shape=[jax.ShapeDtypeStruct((NDEV, *a.shape), a.dtype) for a in arrs],
        in_specs=_hbm_specs(n), out_specs=_hbm_specs(n),
        scratch_shapes=[pltpu.SemaphoreType.DMA((7 * n,)), pltpu.SemaphoreType.DMA((7 * n,)),
                        pltpu.SemaphoreType.DMA((n,))],
    )(*arrs)


_XCHG_EFFECT = pltpu.SideEffectType.DATAFLOW_SIDE_EFFECTING
ALL_PEERS = (1, 2, 3, 4, 5, 6, 7)
SAME_CORE = (2, 4, 6)


def _xchg_copies(src_refs, land_refs, send_sems, recv_sems, local_sems, scatter, rels):
    x, y, cc = _position()
    me = 4 * x + 2 * y + cc
    remote, local = [], []
    for a, (src, land) in enumerate(zip(src_refs, land_refs)):
        local.append(pltpu.make_async_copy(src.at[me] if scatter else src, land.at[me], local_sems.at[a]))
        for idx, rel in enumerate(rels):
            px = 1 - x if rel & 4 else x
            py = 1 - y if rel & 2 else y
            pc = 1 - cc if rel & 1 else cc
            k = len(rels) * a + idx
            remote.append(pltpu.make_async_remote_copy(
                src_ref=src.at[4 * px + 2 * py + pc] if scatter else src, dst_ref=land.at[me],
                send_sem=send_sems.at[k], recv_sem=recv_sems.at[k], device_id=(px, py, pc), device_id_type=_MESH))
    return remote, local


def _xchg_start(srcs, scatter, rels, after, name):
    n = len(srcs)
    lands = [lax.empty(s.shape if scatter else (NDEV, *s.shape), s.dtype) for s in srcs]

    def body(*refs):
        remote, local = _xchg_copies(refs[:n], refs[n:2 * n], *refs[2 * n + 1:2 * n + 4], scatter, rels)
        for cp in local + remote:
            cp.start()
        token = refs[-1]
        token[...] = jnp.zeros_like(token)

    hbm = pl.BlockSpec(memory_space=pltpu.HBM)
    sem = pl.BlockSpec(memory_space=pltpu.SEMAPHORE)
    out = pl.pallas_call(
        body, name=name,
        out_shape=(pltpu.SemaphoreType.DMA((len(rels) * n,)), pltpu.SemaphoreType.DMA((len(rels) * n,)),
                   pltpu.SemaphoreType.DMA((n,)),
                   *[pltpu.HBM(a.shape, a.dtype) for a in srcs + lands], jax.ShapeDtypeStruct((8, LANES), F32)),
        in_specs=[hbm] * (2 * n) + [pl.BlockSpec(memory_space=pl.ANY)],
        out_specs=(sem, sem, sem, *[hbm] * (2 * n), pl.BlockSpec(memory_space=pltpu.VMEM)),
        input_output_aliases={i: 3 + i for i in range(2 * n)},
        compiler_params=pltpu.CompilerParams(has_side_effects=_XCHG_EFFECT),
    )(*[pltpu.with_memory_space_constraint(a, pltpu.HBM) for a in srcs + lands], after)
    return out[:-1], out[-1][0, 0]


def _xchg_wait(handles, after, scatter, rels, name):
    n = (len(handles) - 3) // 2

    def body(*refs):
        remote, local = _xchg_copies(refs[:n], refs[n:2 * n], *refs[2 * n:2 * n + 3], scatter, rels)
        for cp in remote:
            cp.wait_send()
            cp.wait_recv()
        for cp in local:
            cp.wait()

    hbm = pl.BlockSpec(memory_space=pltpu.HBM)
    sem = pl.BlockSpec(memory_space=pltpu.SEMAPHORE)
    thru = list(handles[3:])
    out = pl.pallas_call(
        body, name=name,
        out_shape=tuple(pltpu.HBM(a.shape, a.dtype) for a in thru),
        in_specs=[hbm] * (2 * n) + [sem, sem, sem, pl.BlockSpec(memory_space=pl.ANY)],
        out_specs=tuple([hbm] * (2 * n)),
        input_output_aliases={i: i for i in range(2 * n)},
        compiler_params=pltpu.CompilerParams(has_side_effects=_XCHG_EFFECT),
    )(*thru, *handles[:3], after)
    return list(out[n:])


def _sibling_forward(lands, name):
    n = len(lands)

    def body(*refs):
        land_refs = refs[n:2 * n]
        send_sems, recv_sems = refs[2 * n:]
        x, y, cc = _position()

        def copy(a, q, core):
            slot = land_refs[a].at[2 * q + core]
            return pltpu.make_async_remote_copy(
                src_ref=slot, dst_ref=slot, send_sem=send_sems.at[NCHIP * a + q], recv_sem=recv_sems.at[NCHIP * a + q],
                device_id=(x, y, 1 - cc), device_id_type=_MESH)

        sends = [copy(a, q, cc) for a in range(n) for q in range(NCHIP)]
        for cp in sends:
            cp.start()
        for a in range(n):
            for q in range(NCHIP):
                copy(a, q, 1 - cc).wait_recv()
        for cp in sends:
            cp.wait_send()

    return pl.pallas_call(
        body, name=name,
        out_shape=[jax.ShapeDtypeStruct(a.shape, a.dtype) for a in lands],
        in_specs=_hbm_specs(n), out_specs=_hbm_specs(n),
        input_output_aliases={i: i for i in range(n)},
        scratch_shapes=[pltpu.SemaphoreType.DMA((NCHIP * n,)), pltpu.SemaphoreType.DMA((NCHIP * n,))],
    )(*lands)


def _slab_sum(slabs, name, tr=None):
    n, r, c = slabs.shape
    tr = r if tr is None else tr

    def body(s_ref, o_ref):
        acc = s_ref[0].astype(F32)
        for q in range(1, n):
            acc = acc + s_ref[q].astype(F32)
        o_ref[...] = acc

    return pl.pallas_call(body, name=name, grid=(r // tr,),
                          in_specs=[pl.BlockSpec((n, tr, c), lambda i: (0, i, 0))],
                          out_specs=pl.BlockSpec((tr, c), lambda i: (i, 0)),
                          out_shape=jax.ShapeDtypeStruct((r, c), F32),
                          compiler_params=_cparams(dimension_semantics=("parallel",)))(slabs)


def _ada_fwd(c_all, ada_w, kv_ada_w, logits):
    rows, d = c_all.shape
    n0, nkv = ada_w.shape[2], kv_ada_w.shape[1]

    def body(c_ref, w_ref, kw_ref, lg_ref, part_ref, cact_ref, lb_ref):
        ca = _silu(c_ref[...])
        cact_ref[...] = ca
        part_ref[:, 0:n0] = _bdot_raw(ca, w_ref[0], _NN)
        part_ref[:, n0:2 * n0] = _bdot_raw(ca, w_ref[1], _NN)
        part_ref[:, 2 * n0:2 * n0 + nkv] = _bdot_raw(ca, kw_ref[...], _NN)
        lb_ref[...] = _sigmoid(lg_ref[0:1, :] - lg_ref[1:2, :])

    vm = pl.BlockSpec(memory_space=pltpu.VMEM)
    return pl.pallas_call(
        body, name="ada_fwd", in_specs=[vm, vm, vm, vm], out_specs=[vm, vm, vm],
        out_shape=[jax.ShapeDtypeStruct((rows, 2 * n0 + nkv), F32), jax.ShapeDtypeStruct((rows, d), F32),
                   jax.ShapeDtypeStruct((1, d), F32)],
        compiler_params=_cparams(),
    )(c_all, ada_w, kv_ada_w, logits)


def _ada_bwd(c_act, dm0, dm1, dkv, lb, dlb):
    rows, d = c_act.shape

    def body(c_ref, d0_ref, d1_ref, dk_ref, lb_ref, dlb_ref, dw_ref, dkw_ref, dlg_ref):
        ca = c_ref[...]
        dw_ref[0] = _bdot_raw(ca, d0_ref[...], _TN)
        dw_ref[1] = _bdot_raw(ca, d1_ref[...], _TN)
        dkw_ref[...] = _bdot_raw(ca, dk_ref[...], _TN)
        lbv = lb_ref[...]
        dl0 = dlb_ref[...] * lbv * (1.0 - lbv)
        dlg_ref[0:1, :] = dl0
        dlg_ref[1:2, :] = -dl0

    vm = pl.BlockSpec(memory_space=pltpu.VMEM)
    return pl.pallas_call(
        body, name="ada_bwd", in_specs=[vm] * 6, out_specs=[vm, vm, vm],
        out_shape=[jax.ShapeDtypeStruct((2, d, dm0.shape[1]), F32), jax.ShapeDtypeStruct((d, dkv.shape[1]), F32),
                   jax.ShapeDtypeStruct((2, d), F32)],
        compiler_params=_cparams(),
    )(c_act, dm0, dm1, dkv, lb, dlb)


def _adamw(w, g, m, v, name, tr=512, after=None):
    r, c = w.shape
    tr = _divisor_tile(r, tr, unit=8)
    c1 = 1.0 - ADAM_B1 ** ADAM_STEP
    c2 = 1.0 - ADAM_B2 ** ADAM_STEP
    deps = [] if after is None else [after]

    def body(w_ref, g_ref, m_ref, v_ref, *rest):
        d_ref, mo_ref, vo_ref = rest[len(deps):]
        gv = g_ref[...]
        mn = ADAM_B1 * m_ref[...] + (1.0 - ADAM_B1) * gv
        vn = ADAM_B2 * v_ref[...] + (1.0 - ADAM_B2) * (gv * gv)
        d_ref[...] = -ADAM_LR * ((mn / c1) / (jnp.sqrt(vn / c2) + ADAM_EPS) + ADAM_WD * w_ref[...])
        mo_ref[...] = mn
        vo_ref[...] = vn

    spec = pl.BlockSpec((tr, c), lambda i: (i, 0))
    out = jax.ShapeDtypeStruct((r, c), F32)
    return pl.pallas_call(body, name=name, grid=(r // tr,),
                          in_specs=[spec] * 4 + [pl.BlockSpec(a.shape, lambda i: (0, 0)) for a in deps],
                          out_specs=[spec] * 3, out_shape=[out, out, out],
                          compiler_params=_cparams(dimension_semantics=("parallel",)))(w, g, m, v, *deps)


def _pad_rows(a, rows):
    return jnp.pad(a, ((0, rows - a.shape[0]), (0, 0)))


def _pack_small(parts, lanes=LANES, row_unit=8):
    flat = jnp.concatenate([p.reshape(-1).astype(F32) for p in parts])
    rows = _round_up(-(-flat.shape[0] // lanes), row_unit)
    return jnp.pad(flat, (0, rows * lanes - flat.shape[0])).reshape(rows, lanes)


def _unpack_small(flat, shapes):
    out, off = [], 0
    for s in shapes:
        n = 1
        for k in s:
            n *= k
        out.append(flat[off:off + n].reshape(s))
        off += n
    return out


def _pad_shard_cols(a, n_loc, n_pad):
    lead = a.shape[:-1]
    a = a.reshape(*lead, NDEV, n_loc)
    a = jnp.pad(a, [(0, 0)] * (len(lead) + 1) + [(0, n_pad - n_loc)])
    return a.reshape(*lead, NDEV * n_pad)


def _unpad_shard_cols(a, n_loc, n_pad):
    lead = a.shape[:-1]
    return a.reshape(*lead, NDEV, n_pad)[..., :n_loc].reshape(*lead, NDEV * n_loc)


def kernel(x, c, ada_w, ada_b, a_w_in, a_lb_logits, a_norm_g, a_w_out, kv_ada_w, kv_ada_b, kv_w, kv_b_f, k_norm_g, b_w_q, q_norm_g, b_w_out, ffn_w_up, ffn_conv_w, ffn_conv_b, ffn_w_down, loss_target, m_ada_w, m_ada_b, m_a_w_in, m_a_lb_logits, m_a_norm_g, m_a_w_out, m_kv_ada_w, m_kv_ada_b, m_kv_w, m_kv_b_f, m_k_norm_g, m_b_w_q, m_q_norm_g, m_b_w_out, m_ffn_w_up, m_ffn_conv_w, m_ffn_conv_b, m_ffn_w_down, v_ada_w, v_ada_b, v_a_w_in, v_a_lb_logits, v_a_norm_g, v_a_w_out, v_kv_ada_w, v_kv_ada_b, v_kv_w, v_kv_b_f, v_k_norm_g, v_b_w_q, v_q_norm_g, v_b_w_out, v_ffn_w_up, v_ffn_conv_w, v_ffn_conv_b, v_ffn_w_down):
    t, d = x.shape[1], x.shape[2]
    nh = d // HEAD
    ncw = ffn_w_up.shape[2]
    ncp = _round_up(ncw, LANES)
    two_f = ncw * NDEV
    ff = two_f // 2
    fp = ncp * NDEV // 2
    rd = ffn_w_down.shape[1]
    me = 4 * lax.axis_index("x") + 2 * lax.axis_index("y") + lax.axis_index("c")
    weights = dict(ada_w=ada_w, ada_b=ada_b, a_w_in=a_w_in, a_lb_logits=a_lb_logits, a_norm_g=a_norm_g,
                   a_w_out=a_w_out, kv_ada_w=kv_ada_w, kv_ada_b=kv_ada_b, kv_w=kv_w, kv_b_f=kv_b_f,
                   k_norm_g=k_norm_g, b_w_q=b_w_q, q_norm_g=q_norm_g, b_w_out=b_w_out, ffn_w_up=ffn_w_up,
                   ffn_conv_w=ffn_conv_w, ffn_conv_b=ffn_conv_b, ffn_w_down=ffn_w_down)
    m_in = dict(ada_w=m_ada_w, ada_b=m_ada_b, a_w_in=m_a_w_in, a_lb_logits=m_a_lb_logits, a_norm_g=m_a_norm_g,
                a_w_out=m_a_w_out, kv_ada_w=m_kv_ada_w, kv_ada_b=m_kv_ada_b, kv_w=m_kv_w, kv_b_f=m_kv_b_f,
                k_norm_g=m_k_norm_g, b_w_q=m_b_w_q, q_norm_g=m_q_norm_g, b_w_out=m_b_w_out, ffn_w_up=m_ffn_w_up,
                ffn_conv_w=m_ffn_conv_w, ffn_conv_b=m_ffn_conv_b, ffn_w_down=m_ffn_w_down)
    v_in = dict(ada_w=v_ada_w, ada_b=v_ada_b, a_w_in=v_a_w_in, a_lb_logits=v_a_lb_logits, a_norm_g=v_a_norm_g,
                a_w_out=v_a_w_out, kv_ada_w=v_kv_ada_w, kv_ada_b=v_kv_ada_b, kv_w=v_kv_w, kv_b_f=v_kv_b_f,
                k_norm_g=v_k_norm_g, b_w_q=v_b_w_q, q_norm_g=v_q_norm_g, b_w_out=v_b_w_out, ffn_w_up=v_ffn_w_up,
                ffn_conv_w=v_ffn_conv_w, ffn_conv_b=v_ffn_conv_b, ffn_w_down=v_ffn_w_down)
    order = list(weights)

    up_loc = jnp.pad(ffn_w_up, ((0, 0), (0, 0), (0, ncp - ncw))).astype(BF16)
    down_loc = ffn_w_down.astype(BF16)
    gather_names = {"l0b": ["a_out", "up0", "down0"], "l1": ["kv", "b_q", "b_out", "up1", "down1"]}
    shards = {"a_out": a_w_out[0].astype(BF16), "up0": up_loc[0], "down0": down_loc[0], "kv": kv_w.T.astype(BF16),
              "b_q": b_w_q[0].astype(BF16), "b_out": b_w_out[0].astype(BF16), "up1": up_loc[1],
              "down1": down_loc[1]}
    pre = _pack_small([c, a_lb_logits, ffn_conv_w])
    a_in_all, pre_all = _all_gather([a_w_in[0].astype(BF16), pre], "gather_a_w_in_and_small_inputs")
    pre_all = pre_all.reshape(NDEV, -1)
    c_all = pre_all[:, :d]
    logits = pre_all[:, d:d + 2 * HEAD].reshape(NDEV, 2, HEAD).transpose(1, 0, 2).reshape(2, d)
    conv_w_full = pre_all[:, d + 2 * HEAD:d + 2 * HEAD + 2 * CONV_TAPS * ncw]
    conv_w_full = conv_w_full.reshape(NDEV, 2, CONV_TAPS, ncw).transpose(1, 2, 0, 3).reshape(2, CONV_TAPS, two_f)

    part, c_act, lb = _ada_fwd(_pad_rows(c_all, 2 * NDEV), ada_w, kv_ada_w, logits)
    (part_all,) = _all_gather([part[:NDEV]], "gather_adaln")
    mine = lax.dynamic_index_in_dim(part_all, me, axis=1, keepdims=False)
    n0, nkv = ada_w.shape[2], kv_ada_w.shape[1]
    mod_names = ["sh1", "sc1", "g1", "sh2", "sc2", "g2"]
    mods = {}
    for l in range(2):
        row = mine[:, l * n0:(l + 1) * n0].reshape(-1) + ada_b[l]
        for k, nm in enumerate(mod_names):
            mods[f"{nm}_{l}"] = row[k * d:(k + 1) * d].reshape(1, d)
    kvrow = mine[:, 2 * n0:2 * n0 + nkv].reshape(-1) + kv_ada_b
    mods["kv_sh"], mods["kv_sc"] = kvrow[:d].reshape(1, d), kvrow[d:].reshape(1, d)

    in_flight = {}

    def start_gather(grp, dep):
        srcs = [shards[n] for n in gather_names[grp]]
        in_flight[grp], started = _xchg_start(srcs, False, SAME_CORE, dep, f"gather_{grp}_start")
        return started

    zero = start_gather("l0b", part_all)
    mods["sh1_0"] = mods["sh1_0"] + zero

    small = {"a_norm_g": a_norm_g, "k_norm_g": k_norm_g.reshape(1, HEAD), "q_norm_g": q_norm_g, "kv_b_f": kv_b_f}
    for l in range(2):
        small[f"conv_w{l}"] = _pad_shard_cols(conv_w_full[l], ncw, ncp).reshape(CONV_TAPS, 2, fp).transpose(1, 0, 2)
        small[f"conv_b{l}"] = _pad_shard_cols(ffn_conv_b[l], ncw, ncp).reshape(2, 1, fp)

    def get_w(grp, after):
        if grp == "l0a":
            return {"a_in": a_in_all}
        arrived = _xchg_wait(in_flight[grp], after, False, SAME_CORE, f"gather_{grp}_wait")
        full = list(_sibling_forward(arrived, f"gather_{grp}_to_sibling"))
        if grp == "l0b":
            started = start_gather("l1", full[0])
            full[0] = full[0] + started.astype(full[0].dtype)
        got = dict(zip(gather_names[grp], full))
        out = {}
        for n, a in got.items():
            if n in ("a_out", "b_out"):
                out[n] = a.reshape(d, d)
            elif n in ("down0", "down1"):
                dn = a.reshape(NCHIP, ff // NCHIP, d)
                out[n] = jnp.pad(dn, ((0, 0), (0, ncp - ncw), (0, 0))).reshape(fp, d)
            elif n == "kv":
                kv_t = a.reshape(NDEV * kv_w.shape[1], d)
                out["kv"] = kv_t[:2 * d]
                out["kv_f"] = jnp.pad(kv_t[2 * d:], ((0, LANES - nh), (0, 0)))
            else:
                out[n] = a
        return out

    scatter_flight, g_last = {}, {}

    def put_g(grp, gr):
        if grp == "l0a":
            g_last.update(gr)
            return zero
        if grp == "l1":
            g_kvw = jnp.concatenate([gr["kv"], gr["kv_f"][:nh].astype(BF16)], axis=0)
            arrs = {"kv_w": g_kvw.reshape(NDEV, kv_w.shape[1], d), "b_w_q": gr["b_q"],
                    "b_w_out": gr["b_out"].reshape(NDEV, d // NDEV, d), "up1": gr["up1"],
                    "down1": gr["down1"].reshape(NCHIP, ncp, d)[:, :ncw].reshape(NDEV, rd, d)}
        else:
            arrs = {"a_w_out": gr["a_out"].reshape(NDEV, d // NDEV, d), "up0": gr["up0"],
                    "down0": gr["down0"].reshape(NCHIP, ncp, d)[:, :ncw].reshape(NDEV, rd, d)}
        srcs = list(arrs.values())
        handles, sent = _xchg_start(srcs, True, ALL_PEERS, srcs[0], f"scatter_{grp}_start")
        scatter_flight[grp] = (list(arrs), handles)
        return sent

    loss_v, grad_x, dmods, dlb, g = _local_step(x[0], loss_target[0], mods, lb, small, get_w, put_g)

    g_sum = {}
    for grp in ("l1", "l0b"):
        names, handles = scatter_flight[grp]
        for nm, a in zip(names, _xchg_wait(handles, grad_x, True, ALL_PEERS, f"scatter_{grp}_wait")):
            g_sum[nm] = _slab_sum(a, f"rs_slab_sum_{nm}")

    def conv_w_grad(a):
        return _unpad_shard_cols(a.transpose(1, 0, 2).reshape(CONV_TAPS, 2 * fp), ncw, ncp)

    def conv_b_grad(a):
        return _unpad_shard_cols(a.reshape(2 * fp), ncw, ncp)

    dmod_vec = [dmods[f"{nm}_{l}"] for l in range(2) for nm in mod_names] + [dmods["kv_sh"], dmods["kv_sc"]]
    post = _pack_small(dmod_vec + [dlb, g["a_norm_g"], g["k_norm_g"], g["q_norm_g"],
                                   jnp.pad(g["kv_b_f"].reshape(-1), (0, LANES - nh)),
                                   conv_w_grad(g["conv_w0"]), conv_w_grad(g["conv_w1"]),
                                   conv_b_grad(g["conv_b0"]), conv_b_grad(g["conv_b1"]), loss_v])
    (post_all,) = _all_gather([post], "gather_small_grads")
    a_in_flight, a_in_sent = _xchg_start([g_last["a_in"]], True, ALL_PEERS, post_all, "scatter_l0a_start")
    a_in_sent = a_in_sent.reshape(1, 1)
    tot = _slab_sum(post_all, "small_grad_sum").reshape(-1)
    nmod = 14 * d
    (t_mod, t_lb, t_ang, t_kng, t_qng, t_bf, t_cw, t_cb, t_loss) = _unpack_small(
        tot, [(nmod,), (1, d), (1, HEAD), (HEAD,), (1, HEAD), (LANES,), (2, CONV_TAPS, two_f), (2, two_f),
              (LANES,)])
    loss = t_loss[0]
    dm_all = post_all.reshape(NDEV, -1)[:, :nmod]
    dm0 = lax.dynamic_slice_in_dim(dm_all[:, :6 * d], me * n0, n0, axis=1)
    dm1 = lax.dynamic_slice_in_dim(dm_all[:, 6 * d:12 * d], me * n0, n0, axis=1)
    dkv = lax.dynamic_slice_in_dim(dm_all[:, 12 * d:], me * nkv, nkv, axis=1)
    g_ada_w, g_kv_ada_w, g_logits = _ada_bwd(c_act, _pad_rows(dm0, 2 * NDEV), _pad_rows(dm1, 2 * NDEV),
                                              _pad_rows(dkv, 2 * NDEV), lb, t_lb)

    grads = {
        "ada_w": g_ada_w,
        "ada_b": t_mod[:12 * d].reshape(2, 6 * d),
        "a_lb_logits": lax.dynamic_slice_in_dim(g_logits, me * HEAD, HEAD, axis=1),
        "a_norm_g": t_ang,
        "a_w_out": g_sum["a_w_out"].reshape(a_w_out.shape),
        "kv_ada_w": g_kv_ada_w,
        "kv_ada_b": t_mod[12 * d:],
        "kv_w": g_sum["kv_w"].T,
        "kv_b_f": t_bf[:nh],
        "k_norm_g": t_kng,
        "b_w_q": g_sum["b_w_q"].reshape(b_w_q.shape),
        "q_norm_g": t_qng,
        "b_w_out": g_sum["b_w_out"].reshape(b_w_out.shape),
        "ffn_w_up": jnp.stack([g_sum["up0"][:, :ncw], g_sum["up1"][:, :ncw]]),
        "ffn_conv_w": lax.dynamic_slice_in_dim(t_cw, me * ncw, ncw, axis=2),
        "ffn_conv_b": t_cb,
        "ffn_w_down": jnp.stack([g_sum["down0"], g_sum["down1"]]),
    }

    big_adam = ["ada_w", "a_w_out", "kv_ada_w", "kv_w", "b_w_q", "b_w_out", "ffn_w_up", "ffn_w_down", "a_w_in"]
    small_adam = [n for n in order if n not in big_adam]
    delta, new_m, new_v = {}, {}, {}
    packs = [_pack_small([src[n] for n in small_adam]) for src in (weights, grads, m_in, v_in)]
    outs = _adamw(*packs, "adamw_small", tr=packs[0].shape[0])
    shapes = [weights[n].shape for n in small_adam]
    for dst, o in zip((delta, new_m, new_v), outs):
        for n, a in zip(small_adam, _unpack_small(o.reshape(-1), shapes)):
            dst[n] = a
    for n in big_adam:
        if n == "a_w_in":
            (landed,) = _xchg_wait(a_in_flight, new_v["ffn_w_down"], True, ALL_PEERS, "scatter_l0a_wait")
            grads[n] = _slab_sum(landed, "rs_slab_sum_a_w_in").reshape(a_w_in.shape)
        shp = weights[n].shape
        two_d = lambda a: a.reshape(-1, shp[-1])
        dl, mn, vn = _adamw(two_d(weights[n]), two_d(grads[n]), two_d(m_in[n]), two_d(v_in[n]), f"adamw_{n}",
                            after=a_in_sent)
        delta[n], new_m[n], new_v[n] = dl.reshape(shp), mn.reshape(shp), vn.reshape(shp)

    return (loss, grad_x.reshape(x.shape), *[grads[n] for n in order], *[delta[n] for n in order],
            *[new_m[n] for n in order], *[new_v[n] for n in order])
```

```python
import functools

import jax
import jax.numpy as jnp
from jax import lax
from jax.experimental import pallas as pl
from jax.experimental.pallas import tpu as pltpu

F32 = jnp.float32
BF16 = jnp.bfloat16

NDEV = 8
NCHIP = 4
HEAD = 128
A_CHUNK = 64
CONV_TAPS = 3
EPS = 1e-6
NEG_INF = -1e30
LANES = 128
VMEM_LIMIT = 48 * 1024 * 1024

ADAM_LR = 0.001
ADAM_B1 = 0.9
ADAM_B2 = 0.999
ADAM_EPS = 1e-08
ADAM_WD = 0.01
ADAM_STEP = 10

_NN = (((1,), (0,)), ((), ()))
_NT = (((1,), (1,)), ((), ()))
_TN = (((0,), (0,)), ((), ()))
_MESH = pl.DeviceIdType.MESH


def _cparams(**kw):
    return pltpu.CompilerParams(vmem_limit_bytes=VMEM_LIMIT, **kw)


def _divisor_tile(n, pref, unit=LANES):
    if n <= pref:
        return n
    best = None
    for t in range(unit, pref + 1, unit):
        if n % t == 0:
            best = t
    assert best is not None, (n, pref)
    return best


def _round_up(n, unit):
    return -(-n // unit) * unit


def _bdot_raw(a, b, dims):
    return lax.dot_general(a.astype(BF16), b.astype(BF16), dims, preferred_element_type=F32)


@jax.custom_vjp
def _dot_nn(a, b):
    return _bdot_raw(a, b, _NN)


@jax.custom_vjp
def _dot_nt(a, b):
    return _bdot_raw(a, b, _NT)


@jax.custom_vjp
def _dot_tn(a, b):
    return _bdot_raw(a, b, _TN)


_dot_nn.defvjp(lambda a, b: (_bdot_raw(a, b, _NN), (a, b)),
               lambda r, g: (_dot_nt(g, r[1]), _dot_tn(r[0], g)))
_dot_nt.defvjp(lambda a, b: (_bdot_raw(a, b, _NT), (a, b)),
               lambda r, g: (_dot_nn(g, r[1]), _dot_tn(g, r[0])))
_dot_tn.defvjp(lambda a, b: (_bdot_raw(a, b, _TN), (a, b)),
               lambda r, g: (_dot_nt(r[1], g), _dot_nn(r[0], g)))


def _f32dot(a, b):
    return lax.dot_general(a, b, _NN, precision=lax.Precision.HIGHEST, preferred_element_type=F32)


def _sigmoid(x):
    return jax.nn.sigmoid(x)


def _silu(x):
    return x * jax.nn.sigmoid(x)


def _rms(x):
    return x * lax.rsqrt(jnp.mean(x * x, axis=-1, keepdims=True) + EPS)


def _modulate(x, sh, sc):
    return _rms(x) * (1.0 + sc) + sh


def _mm_call(a, b, dims, a_spec, b_spec, o_spec, o_shape, grid, acc_tile, name):
    nk = grid[2]

    def body(a_ref, b_ref, o_ref, *acc):
        p = lax.dot_general(a_ref[...].astype(BF16), b_ref[...].astype(BF16), dims,
                            preferred_element_type=F32)
        if nk == 1:
            o_ref[...] = p.astype(o_ref.dtype)
        else:
            kk = pl.program_id(2)

            @pl.when(kk == 0)
            def _():
                acc[0][...] = p

            @pl.when(kk > 0)
            def _():
                acc[0][...] += p

            @pl.when(kk == nk - 1)
            def _():
                o_ref[...] = acc[0][...].astype(o_ref.dtype)

    return pl.pallas_call(
        body, name=name, grid=grid, in_specs=[a_spec, b_spec], out_specs=o_spec, out_shape=o_shape,
        scratch_shapes=[pltpu.VMEM(acc_tile, F32)] if nk > 1 else [],
        compiler_params=_cparams(dimension_semantics=("parallel", "parallel", "arbitrary")),
    )(a, b)


def _mm(a, b, mode, out_dtype, name, tm=1024, tn=1024, tk=2048):
    if mode == "nn":
        (m, k), (k2, n) = a.shape, b.shape
    elif mode == "nt":
        (m, k), (n, k2) = a.shape, b.shape
    else:
        (k, m), (k2, n) = a.shape, b.shape
    assert k == k2, (a.shape, b.shape, mode)
    tm, tn, tk = _divisor_tile(m, tm), _divisor_tile(n, tn), _divisor_tile(k, tk)
    if mode == "tn":
        a_spec = pl.BlockSpec((tk, tm), lambda i, j, kk: (kk, i))
    else:
        a_spec = pl.BlockSpec((tm, tk), lambda i, j, kk: (i, kk))
    if mode == "nt":
        b_spec = pl.BlockSpec((tn, tk), lambda i, j, kk: (j, kk))
    else:
        b_spec = pl.BlockSpec((tk, tn), lambda i, j, kk: (kk, j))
    return _mm_call(a, b, {"nn": _NN, "nt": _NT, "tn": _TN}[mode], a_spec, b_spec,
                    pl.BlockSpec((tm, tn), lambda i, j, kk: (i, j)), jax.ShapeDtypeStruct((m, n), out_dtype),
                    (m // tm, n // tn, k // tk), (tm, tn), name)


def _wblk_act_spec(rows, gb, nl, split, nb, row_axis, blk_axis):
    if split == 1:
        return pl.BlockSpec((rows, gb * nl), lambda *g: (g[row_axis], g[blk_axis]))
    groups = nb // split // gb
    return pl.BlockSpec((None, rows, gb * nl),
                        lambda *g: (g[blk_axis] // groups, g[row_axis], g[blk_axis] % groups))


def _mm_wblk(a, wb, out_dtype, name, *, gb, row_off=0, split=1, tm=1024):
    m, k = a.shape
    nb, _, nl = wb.shape
    assert (nb // split) % gb == 0
    tm = _divisor_tile(m, tm)

    def body(a_ref, b_ref, o_ref):
        av = a_ref[...].astype(BF16)
        for s in range(gb):
            o_ref[:, s * nl:(s + 1) * nl] = lax.dot_general(
                av, b_ref[s].astype(BF16), _NN, preferred_element_type=F32).astype(o_ref.dtype)

    o_shape = (m, nb * nl) if split == 1 else (split, m, nb // split * nl)
    return pl.pallas_call(
        body, name=name, grid=(nb // gb, m // tm),
        in_specs=[pl.BlockSpec((tm, k), lambda j, i: (i, 0)),
                  pl.BlockSpec((gb, k, nl), lambda j, i: (j, row_off, 0))],
        out_specs=_wblk_act_spec(tm, gb, nl, split, nb, 1, 0),
        out_shape=jax.ShapeDtypeStruct(o_shape, out_dtype),
        compiler_params=_cparams(dimension_semantics=("parallel", "parallel")),
    )(a, wb)


def _mm_wblk_dx(dy, wb, out_dtype, name, *, k, gb, row_off=0, split=1, tm=1024):
    nb, _, nl = wb.shape
    m = dy.shape[-2]
    tm = _divisor_tile(m, tm)
    nk = nb // gb
    per = nb // split
    whole = split > 1 and gb == nb
    assert whole or per % gb == 0

    def body(a_ref, b_ref, o_ref, *acc):
        p = None
        for s in range(gb):
            a_blk = a_ref[s // per, :, (s % per) * nl:(s % per + 1) * nl] if whole else a_ref[:, s * nl:(s + 1) * nl]
            q = lax.dot_general(a_blk.astype(BF16), b_ref[s].astype(BF16), _NT, preferred_element_type=F32)
            p = q if p is None else p + q
        if nk == 1:
            o_ref[...] = p.astype(o_ref.dtype)
        else:
            kk = pl.program_id(1)

            @pl.when(kk == 0)
            def _():
                acc[0][...] = p

            @pl.when(kk > 0)
            def _():
                acc[0][...] += p

            @pl.when(kk == nk - 1)
            def _():
                o_ref[...] = acc[0][...].astype(o_ref.dtype)

    return pl.pallas_call(
        body, name=name, grid=(m // tm, nk),
        in_specs=[pl.BlockSpec((split, tm, per * nl), lambda i, kk: (0, i, 0)) if whole
                  else _wblk_act_spec(tm, gb, nl, split, nb, 0, 1),
                  pl.BlockSpec((gb, k, nl), lambda i, kk: (kk, row_off, 0))],
        out_specs=pl.BlockSpec((tm, k), lambda i, kk: (i, 0)),
        out_shape=jax.ShapeDtypeStruct((m, k), out_dtype),
        scratch_shapes=[pltpu.VMEM((tm, k), F32)] if nk > 1 else [],
        compiler_params=_cparams(dimension_semantics=("parallel", "arbitrary")),
    )(dy, wb)


def _mm_wblk_dw(x, dy, name, *, nb, gb, split=1, tk=1024):
    t, k = x.shape
    assert (nb // split) % gb == 0
    nl = dy.shape[-1] * split // nb
    tk = _divisor_tile(t, tk)
    nk = t // tk

    def body(a_ref, b_ref, o_ref, *acc):
        kk = pl.program_id(1)
        av = a_ref[...].astype(BF16)
        for s in range(gb):
            p = lax.dot_general(av, b_ref[:, s * nl:(s + 1) * nl].astype(BF16), _TN, preferred_element_type=F32)
            if nk == 1:
                o_ref[s] = p.astype(o_ref.dtype)
                continue

            @pl.when(kk == 0)
            def _():
                acc[0][s] = p

            @pl.when(kk > 0)
            def _():
                acc[0][s] += p

        if nk > 1:
            @pl.when(kk == nk - 1)
            def _():
                o_ref[...] = acc[0][...].astype(o_ref.dtype)

    return pl.pallas_call(
        body, name=name, grid=(nb // gb, nk),
        in_specs=[pl.BlockSpec((tk, k), lambda j, kk: (kk, 0)), _wblk_act_spec(tk, gb, nl, split, nb, 1, 0)],
        out_specs=pl.BlockSpec((gb, k, nl), lambda j, kk: (j, 0, 0)),
        out_shape=jax.ShapeDtypeStruct((nb, k, nl), BF16),
        scratch_shapes=[pltpu.VMEM((gb, k, nl), F32)] if nk > 1 else [],
        compiler_params=_cparams(dimension_semantics=("parallel", "arbitrary")),
    )(x, dy)


def _row_specs(rows, tb, nsub):
    return [pl.BlockSpec((tb, nsub * cw), functools.partial(lambda i, off: (i, off), off=off))
            for (_, cw, off) in rows]


def _vec_specs(params):
    return [pl.BlockSpec(p.shape, lambda i: (0, 0)) for p in params]


def _row_fwd(f, rows, params, out_dtypes, *, nsub=1, tb, name):
    t = rows[0][0].shape[0]
    tb = min(tb, t)
    n_r, n_p = len(rows), len(params)
    blk = [jax.ShapeDtypeStruct((tb, cw), F32) for (_, cw, _) in rows]
    blk += [jax.ShapeDtypeStruct(p.shape, F32) for p in params]
    out_avals = jax.eval_shape(f, *blk)

    def body(*refs):
        pv = [r[...] for r in refs[n_r:n_r + n_p]]
        for s in range(nsub):
            vals = [r[:, s * cw:(s + 1) * cw].astype(F32) for r, (_, cw, _) in zip(refs[:n_r], rows)]
            outs = f(*vals, *pv)
            for o_ref, o in zip(refs[n_r + n_p:], outs):
                w = o.shape[1]
                o_ref[:, s * w:(s + 1) * w] = o.astype(o_ref.dtype)

    return pl.pallas_call(
        body, name=name,
        grid=(t // tb,),
        in_specs=_row_specs(rows, tb, nsub) + _vec_specs(params),
        out_specs=[pl.BlockSpec((tb, nsub * av.shape[1]), lambda i: (i, 0)) for av in out_avals],
        out_shape=[jax.ShapeDtypeStruct((t, nsub * av.shape[1]), dt) for av, dt in zip(out_avals, out_dtypes)],
        compiler_params=_cparams(dimension_semantics=("parallel",)),
    )(*[r[0] for r in rows], *params)


def _row_bwd(f, rows, params, cots, row_grad_dtypes, *, nsub=1, tb, name, add_to=None, cot_add=None):
    t = rows[0][0].shape[0]
    tb = min(tb, t)
    n_r, n_p, n_c = len(rows), len(params), len(cots)
    want = [j for j in range(n_r) if row_grad_dtypes[j] is not None]
    extra = [] if add_to is None else [(add_to[1], rows[add_to[0]][1], 0)]
    extra += [] if cot_add is None else [(cot_add[1], cots[cot_add[0]][1], 0)]

    def body(*refs):
        i = pl.program_id(0)
        r_in, p_in = refs[:n_r], refs[n_r:n_r + n_p]
        c_in = refs[n_r + n_p:n_r + n_p + n_c]
        e_in = refs[n_r + n_p + n_c:n_r + n_p + n_c + len(extra)]
        outs = refs[n_r + n_p + n_c + len(extra):]
        pv = [r[...] for r in p_in]
        psum = [None] * n_p
        for s in range(nsub):
            vals = [r[:, s * cw:(s + 1) * cw].astype(F32) for r, (_, cw, _) in zip(r_in, rows)]
            cvals = [r[:, s * cw:(s + 1) * cw].astype(F32) for r, (_, cw, _) in zip(c_in, cots)]
            if cot_add is not None:
                cw = cots[cot_add[0]][1]
                cvals[cot_add[0]] = cvals[cot_add[0]] + e_in[-1][:, s * cw:(s + 1) * cw]
            _, vjp_fn = jax.vjp(f, *vals, *pv)
            grads = vjp_fn(tuple(cvals))
            for o_ref, jr in zip(outs[:len(want)], want):
                cw = rows[jr][1]
                gr = grads[jr]
                if add_to is not None and jr == add_to[0]:
                    gr = gr + e_in[0][:, s * cw:(s + 1) * cw]
                o_ref[:, s * cw:(s + 1) * cw] = gr.astype(o_ref.dtype)
            for jp in range(n_p):
                psum[jp] = grads[n_r + jp] if psum[jp] is None else psum[jp] + grads[n_r + jp]
        for o_ref, g in zip(outs[len(want):], psum):
            @pl.when(i == 0)
            def _():
                o_ref[...] = g

            @pl.when(i > 0)
            def _():
                o_ref[...] += g

    out_specs = [pl.BlockSpec((tb, nsub * rows[jr][1]), lambda i: (i, 0)) for jr in want]
    out_shape = [jax.ShapeDtypeStruct((t, nsub * rows[jr][1]), row_grad_dtypes[jr]) for jr in want]
    out_specs += _vec_specs(params)
    out_shape += [jax.ShapeDtypeStruct(p.shape, F32) for p in params]
    res = pl.pallas_call(
        body, name=name,
        grid=(t // tb,),
        in_specs=_row_specs(rows, tb, nsub) + _vec_specs(params) + _row_specs(cots, tb, nsub)
        + _row_specs(extra, tb, nsub),
        out_specs=out_specs, out_shape=out_shape,
        compiler_params=_cparams(dimension_semantics=("arbitrary",)),
    )(*[r[0] for r in rows], *params, *[c[0] for c in cots], *[e[0] for e in extra])
    return res[:len(want)], res[len(want):]


def _f_mod(x, sh, sc):
    return (_modulate(x, sh, sc),)


def _f_res_mod(x, y, g, sh, sc):
    x1 = x + g * y
    return x1, _modulate(x1, sh, sc)


def _f_res_mod2(x, y, g, sh_a, sc_a, sh_b, sc_b):
    x1 = x + g * y
    return x1, _modulate(x1, sh_a, sc_a), _modulate(x1, sh_b, sc_b)


def _f_qnorm(p, g):
    return (_rms(p) * g * (HEAD ** -0.5),)


def _f_knorm(p, g):
    return (_rms(p) * g,)


def _f_qnorm_aug(p, g):
    lane = lax.broadcasted_iota(jnp.int32, p.shape, 1)
    return (jnp.concatenate([_rms(p) * g * (HEAD ** -0.5), jnp.where(lane < 3, 1.0, 0.0)], axis=1),)


def _f_knorm_aug(p, c0, c1, c2, g):
    lane = lax.broadcasted_iota(jnp.int32, p.shape, 1)
    aug = jnp.where(lane == 0, c0, jnp.where(lane == 1, c1, jnp.where(lane == 2, c2, 0.0)))
    return (jnp.concatenate([_rms(p) * g, aug], axis=1),)


def _split3(a):
    round_bf16 = lambda v: lax.reduce_precision(v, exponent_bits=8, mantissa_bits=7)
    hi = round_bf16(a)
    mid = round_bf16(a - hi)
    lo = round_bf16(a - hi - mid)
    return hi.astype(BF16), mid.astype(BF16), lo.astype(BF16)


def _f_outgate(o, og):
    return (o * _sigmoid(og),)


def _loss_call(x3, f, g2, target, tb):
    t, d = x3.shape
    tb = min(tb, t)

    def body(x_ref, f_ref, g_ref, t_ref, loss_ref, dx_ref, df_ref, dg_ref):
        i = pl.program_id(0)
        fv = f_ref[...]
        g = g_ref[...]
        e = x_ref[...] + g * fv - t_ref[...]
        dx = e * (1.0 / d)
        part = 0.5 * jnp.sum(jnp.sum(e * dx, axis=1, keepdims=True), axis=0, keepdims=True)
        dx_ref[...] = dx
        df_ref[...] = (g * dx).astype(df_ref.dtype)
        dg = jnp.sum(dx * fv, axis=0, keepdims=True)

        @pl.when(i == 0)
        def _():
            loss_ref[...] = jnp.broadcast_to(part, loss_ref.shape)
            dg_ref[...] = dg

        @pl.when(i > 0)
        def _():
            loss_ref[...] += jnp.broadcast_to(part, loss_ref.shape)
            dg_ref[...] += dg

    row = pl.BlockSpec((tb, d), lambda i: (i, 0))
    vec = pl.BlockSpec((1, d), lambda i: (0, 0))
    return pl.pallas_call(
        body, name="loss_head",
        grid=(t // tb,),
        in_specs=[row, row, vec, row],
        out_specs=[pl.BlockSpec((1, LANES), lambda i: (0, 0)), row, row, vec],
        out_shape=[jax.ShapeDtypeStruct((1, LANES), F32), jax.ShapeDtypeStruct((t, d), F32),
                   jax.ShapeDtypeStruct((t, d), BF16), jax.ShapeDtypeStruct((1, d), F32)],
        compiler_params=_cparams(dimension_semantics=("arbitrary",)),
    )(x3, f, g2, target)


def _hg_mask(tb):
    br = lax.broadcasted_iota(jnp.int32, (tb, tb), 0)
    bs = lax.broadcasted_iota(jnp.int32, (tb, tb), 1)
    return jnp.logical_and(br // A_CHUNK == bs // A_CHUNK, bs <= br).astype(F32)


def _hg_consts(mask):
    c = A_CHUNK
    r = lax.broadcasted_iota(jnp.int32, (c, c), 0)
    s = lax.broadcasted_iota(jnp.int32, (c, c), 1)
    return (s <= r).astype(F32), (r <= s).astype(F32), mask > 0.5


def _chunk_apply(mat, x):
    c = mat.shape[0]
    return jnp.concatenate([_f32dot(mat, x[i * c:(i + 1) * c]) for i in range(x.shape[0] // c)], axis=0)


@jax.custom_vjp
def _chunk_cumsum(x, tri, tri_t):
    return _chunk_apply(tri, x)


_chunk_cumsum.defvjp(lambda x, tri, tri_t: (_chunk_apply(tri, x), (tri, tri_t)),
                     lambda r, g: (_chunk_apply(r[1], g), jnp.zeros_like(r[0]), jnp.zeros_like(r[1])))


def _per_chunk(a, b, dims):
    return jnp.stack([_bdot_raw(a[i], b[i], dims) for i in range(a.shape[0])])


@jax.custom_vjp
def _chunk_tn(a, b):
    return _per_chunk(a, b, _TN)


@jax.custom_vjp
def _chunk_nt(a, b):
    return _per_chunk(a, b, _NT)


@jax.custom_vjp
def _chunk_nn(a, b):
    return _per_chunk(a, b, _NN)


_chunk_tn.defvjp(lambda a, b: (_per_chunk(a, b, _TN), (a, b)),
                 lambda r, g: (_chunk_nt(r[1], g), _chunk_nn(r[0], g)))
_chunk_nt.defvjp(lambda a, b: (_per_chunk(a, b, _NT), (a, b)),
                 lambda r, g: (_chunk_nn(g, r[1]), _chunk_tn(g, r[0])))
_chunk_nn.defvjp(lambda a, b: (_per_chunk(a, b, _NN), (a, b)),
                 lambda r, g: (_chunk_nt(g, r[1]), _chunk_tn(r[0], g)))


def _scan_states(decay, m, st):
    sts = []
    for i in range(m.shape[0]):
        sts.append(st)
        st = st * decay[i] + m[i]
    return jnp.stack(sts), st


@jax.custom_vjp
def _state_scan(decay, m, st):
    return _scan_states(decay, m, st)


def _state_scan_fwd(decay, m, st):
    sts, st_out = _scan_states(decay, m, st)
    return (sts, st_out), (decay, sts)


def _state_scan_bwd(res, cts):
    decay, sts = res
    d_sts, g = cts
    d_decay, d_m = [], []
    for i in range(sts.shape[0] - 1, -1, -1):
        d_m.append(g)
        d_decay.append(jnp.sum(g * sts[i], axis=0, keepdims=True))
        g = g * decay[i] + d_sts[i]
    return jnp.stack(d_decay[::-1]), jnp.stack(d_m[::-1]), g


_state_scan.defvjp(_state_scan_fwd, _state_scan_bwd)


def _hg_block(qp, fp, ip, gp, lb, ng, st, tri, tri_t, bd_causal):
    tb = qp.shape[0]
    c = A_CHUNK
    n = tb // c
    q = _silu(qp)
    fg = lb + (1.0 - lb) * _sigmoid(fp)
    logf = jnp.log(fg)
    k = 1.0 - fg
    b3 = _chunk_cumsum(logf, tri, tri_t).reshape(n, c, HEAD)
    pos = lax.broadcasted_iota(jnp.int32, (1, c, 1), 1)
    b_mid = lax.stop_gradient(jnp.sum(jnp.where(pos == c // 2, b3, 0.0), axis=1, keepdims=True))
    b_last = jnp.sum(jnp.where(pos == c - 1, b3, 0.0), axis=1, keepdims=True)
    q3, k3, v3 = q.reshape(n, c, HEAD), k.reshape(n, c, HEAD), ip.reshape(n, c, HEAD)
    scores = _dot_nt((q3 * jnp.exp(b3 - b_mid)).reshape(tb, HEAD), (k3 * jnp.exp(b_mid - b3)).reshape(tb, HEAD))
    o_intra = _dot_nn(jnp.where(bd_causal, scores, 0.0), ip)
    states, st_new = _state_scan(jnp.exp(b_last), _chunk_tn(v3, k3 * jnp.exp(b_last - b3)), st)
    o = o_intra + _chunk_nt(q3 * jnp.exp(b3), states).reshape(tb, HEAD)
    y = _rms(o) * ng * _silu(gp)
    return y, st_new


HG_HEADS = 2


def _hg_specs(tb, nh, rev_nb=None):
    wide = HG_HEADS * HEAD
    per = nh // HG_HEADS

    def row(part):
        if rev_nb is None:
            return pl.BlockSpec((tb, wide), functools.partial(lambda h, i, off: (i, off + h), off=part * per))
        return pl.BlockSpec((tb, wide),
                            functools.partial(lambda h, i, off: (rev_nb - 1 - i, off + h), off=part * per))
    return [row(0), row(1), row(2), row(3),
            pl.BlockSpec((1, wide), lambda h, i: (0, h)), pl.BlockSpec((1, HEAD), lambda h, i: (0, 0)),
            pl.BlockSpec((tb, tb), lambda h, i: (0, 0))]


def _hgrn2_fwd(proj, lb, ng, tb):
    t = proj.shape[0]
    nh = proj.shape[1] // (4 * HEAD)
    tb = min(tb, t)
    nb = t // tb
    wide = HG_HEADS * HEAD

    def body(q_ref, f_ref, i_ref, g_ref, lb_ref, ng_ref, mask_ref, y_ref, s_ref, st_ref):
        i = pl.program_id(1)

        @pl.when(i == 0)
        def _():
            st_ref[...] = jnp.zeros_like(st_ref)

        consts = _hg_consts(mask_ref[...])
        for p in range(HG_HEADS):
            cs = slice(p * HEAD, (p + 1) * HEAD)
            st = st_ref[p]
            s_ref[p, 0] = st
            y, st_new = _hg_block(q_ref[:, cs], f_ref[:, cs], i_ref[:, cs], g_ref[:, cs], lb_ref[:, cs],
                                  ng_ref[...], st, *consts)
            y_ref[:, cs] = y.astype(y_ref.dtype)
            st_ref[p] = st_new

    return pl.pallas_call(
        body, name="hgrn2_fwd",
        grid=(nh // HG_HEADS, nb),
        in_specs=_hg_specs(tb, nh),
        out_specs=[pl.BlockSpec((tb, wide), lambda h, i: (i, h)),
                   pl.BlockSpec((HG_HEADS, 1, HEAD, HEAD), lambda h, i: (h, i, 0, 0))],
        out_shape=[jax.ShapeDtypeStruct((t, nh * HEAD), BF16),
                   jax.ShapeDtypeStruct((nh, nb, HEAD, HEAD), F32)],
        scratch_shapes=[pltpu.VMEM((HG_HEADS, HEAD, HEAD), F32)],
        compiler_params=_cparams(dimension_semantics=("parallel", "arbitrary")),
    )(proj, proj, proj, proj, lb, ng, _hg_mask(tb))


def _hgrn2_bwd(proj, lb, ng, states, dy, tb):
    t = proj.shape[0]
    nh = proj.shape[1] // (4 * HEAD)
    tb = min(tb, t)
    nb = t // tb
    wide = HG_HEADS * HEAD

    def body(q_ref, f_ref, i_ref, g_ref, lb_ref, ng_ref, mask_ref, s_ref, dy_ref,
             dp_ref, dlb_ref, dng_ref, dst_ref):
        h, i = pl.program_id(0), pl.program_id(1)
        consts = _hg_consts(mask_ref[...])

        @pl.when(i == 0)
        def _():
            dst_ref[...] = jnp.zeros_like(dst_ref)
            dlb_ref[...] = jnp.zeros_like(dlb_ref)

        @pl.when(jnp.logical_and(i == 0, h == 0))
        def _():
            dng_ref[...] = jnp.zeros_like(dng_ref)

        def fn(qp, fp, ip, gp, lbx, ngx, stx):
            return _hg_block(qp, fp, ip, gp, lbx, ngx, stx, *consts)

        for p in range(HG_HEADS):
            cs = slice(p * HEAD, (p + 1) * HEAD)
            _, vjp_fn = jax.vjp(fn, q_ref[:, cs], f_ref[:, cs], i_ref[:, cs], g_ref[:, cs], lb_ref[:, cs],
                                ng_ref[...], s_ref[p, 0])
            *gparts, glb, gng, dst = vjp_fn((dy_ref[:, cs].astype(F32), dst_ref[p]))
            for part, gpart in enumerate(gparts):
                dp_ref[part, :, cs] = gpart.astype(dp_ref.dtype)
            dst_ref[p] = dst
            dlb_ref[:, cs] += glb
            dng_ref[...] += gng

    rev = lambda h, i: (nb - 1 - i, h)
    return pl.pallas_call(
        body, name="hgrn2_bwd",
        grid=(nh // HG_HEADS, nb),
        in_specs=_hg_specs(tb, nh, rev_nb=nb) + [
            pl.BlockSpec((HG_HEADS, 1, HEAD, HEAD), lambda h, i: (h, nb - 1 - i, 0, 0)),
            pl.BlockSpec((tb, wide), rev)],
        out_specs=[pl.BlockSpec((4, tb, wide), lambda h, i: (0, nb - 1 - i, h)),
                   pl.BlockSpec((1, wide), lambda h, i: (0, h)), pl.BlockSpec((1, HEAD), lambda h, i: (0, 0))],
        out_shape=[jax.ShapeDtypeStruct((4, t, nh * HEAD), BF16),
                   jax.ShapeDtypeStruct((1, nh * HEAD), F32), jax.ShapeDtypeStruct((1, HEAD), F32)],
        scratch_shapes=[pltpu.VMEM((HG_HEADS, HEAD, HEAD), F32)],
        compiler_params=_cparams(dimension_semantics=("arbitrary", "arbitrary")),
    )(proj, proj, proj, proj, lb, ng, _hg_mask(tb), states, dy)


def _fgate_consts(cb):
    r = lax.broadcasted_iota(jnp.int32, (cb, cb), 0)
    s = lax.broadcasted_iota(jnp.int32, (cb, cb), 1)
    return (r <= s).astype(F32), (r >= s).astype(F32)


def _fgate_fwd(xt, bias, cb=512):
    nh, t = xt.shape
    cb = min(cb, t)

    def body(x_ref, b_ref, o_ref):
        upper, _ = _fgate_consts(cb)
        carry = jnp.zeros((nh, 1), F32)
        for blk in range(t // cb):
            z = x_ref[:, blk * cb:(blk + 1) * cb] + b_ref[...]
            logf = jnp.minimum(z, 0.0) - jnp.log(1.0 + jnp.exp(-jnp.abs(z)))
            cs = _f32dot(logf, upper) + carry
            o_ref[:, blk * cb:(blk + 1) * cb] = cs
            carry = cs[:, cb - 1:cb]

    vm = pl.BlockSpec(memory_space=pltpu.VMEM)
    return pl.pallas_call(
        body, name="fgate_fwd", in_specs=[vm, vm], out_specs=vm,
        out_shape=jax.ShapeDtypeStruct((nh, t), F32), compiler_params=_cparams(),
    )(xt, bias)


def _fgate_bwd(xt, bias, dft, cb=512):
    nh, t = xt.shape
    cb = min(cb, t)
    nblk = t // cb

    def body(x_ref, b_ref, d_ref, dx_ref, db_ref):
        _, lower = _fgate_consts(cb)
        carry = jnp.zeros((nh, 1), F32)
        db = jnp.zeros((nh, 1), F32)
        for blk in range(nblk - 1, -1, -1):
            sl = slice(blk * cb, (blk + 1) * cb)
            dlogf = _f32dot(d_ref[:, sl], lower) + carry
            carry = dlogf[:, 0:1]
            z = x_ref[:, sl] + b_ref[...]
            dz = dlogf * (1.0 - _sigmoid(z))
            dx_ref[:, sl] = dz
            db = db + jnp.sum(dz, axis=1, keepdims=True)
        db_ref[...] = db

    vm = pl.BlockSpec(memory_space=pltpu.VMEM)
    return pl.pallas_call(
        body, name="fgate_bwd", in_specs=[vm, vm, vm], out_specs=[vm, vm],
        out_shape=[jax.ShapeDtypeStruct((nh, t), F32), jax.ShapeDtypeStruct((nh, 1), F32)],
        compiler_params=_cparams(),
    )(xt, bias, dft)


def _attn_fwd(q, k, v, f_col, blk):
    t, width = v.shape
    nh = width // HEAD
    nq = t // blk

    def body(q_ref, k_ref, v_ref, fc_ref, o_ref, lse_ref):
        i = pl.program_id(0)
        tri = (lax.broadcasted_iota(jnp.int32, (blk, blk), 1) <= lax.broadcasted_iota(jnp.int32, (blk, blk), 0))
        for h in range(nh):
            cs = slice(h * HEAD, (h + 1) * HEAD)
            cs2 = slice(2 * h * HEAD, 2 * (h + 1) * HEAD)
            qh = q_ref[:, cs2]

            def tile(j, carry, masked):
                m, l, acc = carry
                rs = pl.ds(pl.multiple_of(j * blk, blk), blk)
                s = _bdot_raw(qh, k_ref[rs, cs2], _NT)
                if masked:
                    s = jnp.where(tri, s, NEG_INF)
                m_new = jnp.maximum(m, jnp.max(s, axis=1, keepdims=True))
                p = jnp.exp(s - m_new)
                alpha = jnp.exp(m - m_new)
                l_new = alpha * l + jnp.sum(p, axis=1, keepdims=True)
                acc_new = alpha * acc + _bdot_raw(p, v_ref[rs, cs], _NN)
                return m_new, l_new, acc_new

            init = (jnp.full((blk, 1), NEG_INF, F32), jnp.zeros((blk, 1), F32), jnp.zeros((blk, HEAD), F32))
            carry = lax.fori_loop(0, i, lambda j, c: tile(j, c, False), init)
            m, l, acc = tile(i, carry, True)
            o_ref[:, cs] = acc / l
            lse_ref[:, h:h + 1] = m + jnp.log(l) + fc_ref[:, h:h + 1]

    vm = pl.BlockSpec(memory_space=pltpu.VMEM)
    return pl.pallas_call(
        body, name="fox_attn_fwd",
        grid=(nq,),
        in_specs=[pl.BlockSpec((blk, 2 * width), lambda i: (i, 0)), vm, vm,
                  pl.BlockSpec((blk, nh), lambda i: (i, 0))],
        out_specs=[pl.BlockSpec((blk, width), lambda i: (i, 0)), pl.BlockSpec((blk, nh), lambda i: (i, 0))],
        out_shape=[jax.ShapeDtypeStruct((t, width), F32), jax.ShapeDtypeStruct((t, nh), F32)],
        compiler_params=_cparams(dimension_semantics=("parallel",)),
    )(q, k, v, f_col)


def _attn_delta(do, o, tb):
    t, width = o.shape
    nh = width // HEAD
    tb = min(tb, t)

    def body(do_ref, o_ref, dl_ref):
        for h in range(nh):
            cs = slice(h * HEAD, (h + 1) * HEAD)
            dl_ref[:, h:h + 1] = jnp.sum(do_ref[:, cs].astype(F32) * o_ref[:, cs], axis=1, keepdims=True)

    wide = pl.BlockSpec((tb, width), lambda i: (i, 0))
    return pl.pallas_call(body, name="fox_attn_delta", grid=(t // tb,), in_specs=[wide, wide],
                          out_specs=pl.BlockSpec((tb, nh), lambda i: (i, 0)),
                          out_shape=jax.ShapeDtypeStruct((t, nh), F32),
                          compiler_params=_cparams(dimension_semantics=("parallel",)))(do, o)


ATTN_BWD_GROUPS = 4


def _attn_bwd(q, k, v, f_col, do, lse, delta, blk):
    t, width = v.shape
    nh = width // HEAD
    nq = t // blk
    hpg = nh // ATTN_BWD_GROUPS
    gw = hpg * HEAD

    def body(q_ref, do_ref, k_ref, v_ref, fc_ref, lse_ref, dl_ref,
             dq_ref, dk_ref, dv_ref, dfc_ref, dfr_ref):
        g, j = pl.program_id(0), pl.program_id(1)
        tri = (lax.broadcasted_iota(jnp.int32, (blk, blk), 1) <= lax.broadcasted_iota(jnp.int32, (blk, blk), 0))

        @pl.when(j == 0)
        def _():
            dq_ref[...] = jnp.zeros_like(dq_ref)
            dfc_ref[...] = jnp.zeros_like(dfc_ref)

        for h in range(hpg):
            cs = slice(h * HEAD, (h + 1) * HEAD)
            cs2 = slice(2 * h * HEAD, 2 * (h + 1) * HEAD)
            csq = slice(2 * h * HEAD, (2 * h + 1) * HEAD)
            kj2 = k_ref[:, cs2]
            kj = k_ref[:, csq]
            vj = v_ref[:, cs]

            def tile(i, carry, masked):
                dk, dv, dfs = carry
                rs = pl.ds(pl.multiple_of(i * blk, blk), blk)
                qi = q_ref[rs, csq]
                doi = do_ref[rs, cs]
                bias = fc_ref[0, rs, h:h + 1] - lse_ref[0, rs, h:h + 1]
                p = jnp.exp(_bdot_raw(q_ref[rs, cs2], kj2, _NT) + bias)
                if masked:
                    p = jnp.where(tri, p, 0.0)
                ds = p * (_bdot_raw(doi, vj, _NT) - dl_ref[0, rs, h:h + 1])
                dsb = ds.astype(BF16)
                dq_ref[rs, cs] += _bdot_raw(dsb, kj, _NN)
                dfc_ref[0, rs, h:h + 1] += jnp.sum(ds, axis=1, keepdims=True)
                return (dk + _bdot_raw(dsb, qi, _TN), dv + _bdot_raw(p, doi, _TN),
                        dfs - jnp.sum(ds, axis=0, keepdims=True))

            init = (jnp.zeros((blk, HEAD), F32), jnp.zeros((blk, HEAD), F32), jnp.zeros((1, blk), F32))
            carry = tile(j, init, True)
            dk, dv, dfs = lax.fori_loop(j + 1, nq, lambda i, c: tile(i, c, False), carry)
            dk_ref[:, cs] = dk
            dv_ref[:, cs] = dv.astype(dv_ref.dtype)
            dfr_ref[0, 0, h:h + 1, :] = dfs

    by_group = lambda a: a.reshape(t, ATTN_BWD_GROUPS, hpg).transpose(1, 0, 2)
    once = pl.Buffered(1)
    stat = pl.BlockSpec((1, t, hpg), lambda g, j: (g, 0, 0), pipeline_mode=once)
    kv_blk = pl.BlockSpec((blk, gw), lambda g, j: (j, g))
    frow = pl.BlockSpec((1, 1, hpg, blk), lambda g, j: (g, j, 0, 0))
    dq, dk, dv, dfc, dfr = pl.pallas_call(
        body, name="fox_attn_bwd",
        grid=(ATTN_BWD_GROUPS, nq),
        in_specs=[pl.BlockSpec((t, 2 * gw), lambda g, j: (0, g), pipeline_mode=once),
                  pl.BlockSpec((t, gw), lambda g, j: (0, g), pipeline_mode=once),
                  pl.BlockSpec((blk, 2 * gw), lambda g, j: (j, g)), kv_blk, stat, stat, stat],
        out_specs=[pl.BlockSpec((t, gw), lambda g, j: (0, g)), kv_blk, kv_blk,
                   pl.BlockSpec((1, t, hpg), lambda g, j: (g, 0, 0)), frow],
        out_shape=[jax.ShapeDtypeStruct((t, width), F32), jax.ShapeDtypeStruct((t, width), F32),
                   jax.ShapeDtypeStruct((t, width), BF16), jax.ShapeDtypeStruct((ATTN_BWD_GROUPS, t, hpg), F32),
                   jax.ShapeDtypeStruct((ATTN_BWD_GROUPS, nq, hpg, blk), F32)],
        compiler_params=_cparams(dimension_semantics=("parallel", "arbitrary")),
    )(q, do, k, v, by_group(f_col), by_group(lse), by_group(delta))
    return (dq, dk, dv, dfc.transpose(1, 0, 2).reshape(t, nh),
            dfr.transpose(1, 0, 2, 3).reshape(nq, nh, blk))


SUBLANES = 8


def _shift_down(u, n):
    r = pltpu.roll(u, n, 0)
    row = lax.broadcasted_iota(jnp.int32, (SUBLANES, u.shape[1]), 0)
    return jnp.concatenate([jnp.where(row < n, 0.0, r[:SUBLANES]), r[SUBLANES:]], axis=0)


def _shift_up(u, n):
    t = u.shape[0]
    r = pltpu.roll(u, t - n, 0)
    row = lax.broadcasted_iota(jnp.int32, (SUBLANES, u.shape[1]), 0)
    return jnp.concatenate([r[:t - SUBLANES], jnp.where(row >= SUBLANES - n, 0.0, r[t - SUBLANES:])], axis=0)


def _convglu_specs(t):
    return [pl.BlockSpec((2, t, LANES), lambda j: (0, 0, j)),
            pl.BlockSpec((2, CONV_TAPS, LANES), lambda j: (0, 0, j)),
            pl.BlockSpec((2, 1, LANES), lambda j: (0, 0, j))]


def _convglu_fwd(u, cw, cb):
    _, t, fp = u.shape

    def body(u_ref, w_ref, b_ref, a_ref, c_ref):
        c = []
        for hf in range(2):
            uv, w = u_ref[hf].astype(F32), w_ref[hf]
            c.append(w[0:1] * _shift_down(uv, 2) + w[1:2] * _shift_down(uv, 1) + w[2:3] * uv + b_ref[hf])
            c_ref[hf] = c[hf].astype(c_ref.dtype)
        a_ref[...] = (_silu(c[0]) * c[1]).astype(a_ref.dtype)

    return pl.pallas_call(
        body, name="convglu_fwd",
        grid=(fp // LANES,),
        in_specs=_convglu_specs(t),
        out_specs=[pl.BlockSpec((t, LANES), lambda j: (0, j)), pl.BlockSpec((2, t, LANES), lambda j: (0, 0, j))],
        out_shape=[jax.ShapeDtypeStruct((t, fp), BF16), jax.ShapeDtypeStruct((2, t, fp), BF16)],
        compiler_params=_cparams(dimension_semantics=("parallel",)),
    )(u, cw, cb)


def _convglu_bwd(u, c, cw, da):
    _, t, fp = u.shape

    def body(u_ref, c_ref, w_ref, da_ref, du_ref, dw_ref, db_ref):
        gc, vc = c_ref[0].astype(F32), c_ref[1].astype(F32)
        sg = _sigmoid(gc)
        dav = da_ref[...].astype(F32)
        dcs = [dav * vc * (sg * (1.0 + gc * (1.0 - sg))), dav * (gc * sg)]
        for hf in range(2):
            dc, w, uv = dcs[hf], w_ref[hf], u_ref[hf].astype(F32)
            dc1, dc2 = _shift_up(dc, 1), _shift_up(dc, 2)
            du_ref[hf] = (w[2:3] * dc + w[1:2] * dc1 + w[0:1] * dc2).astype(du_ref.dtype)
            dw_ref[hf, 0:1, :] = jnp.sum(dc2 * uv, axis=0, keepdims=True)
            dw_ref[hf, 1:2, :] = jnp.sum(dc1 * uv, axis=0, keepdims=True)
            dw_ref[hf, 2:3, :] = jnp.sum(dc * uv, axis=0, keepdims=True)
            db_ref[hf] = jnp.sum(dc, axis=0, keepdims=True)

    pair, taps, bias = _convglu_specs(t)
    return pl.pallas_call(
        body, name="convglu_bwd",
        grid=(fp // LANES,),
        in_specs=[pair, pair, taps, pl.BlockSpec((t, LANES), lambda j: (0, j))],
        out_specs=[pair, taps, bias],
        out_shape=[jax.ShapeDtypeStruct((2, t, fp), BF16), jax.ShapeDtypeStruct((2, CONV_TAPS, fp), F32),
                   jax.ShapeDtypeStruct((2, 1, fp), F32)],
        compiler_params=_cparams(dimension_semantics=("parallel",)),
    )(u, c, cw, da)


def _local_step(x, target, mods, lb, small, get_w, put_g, *, tb=512, attn_blk=512):
    t, d = x.shape
    nh = d // HEAD
    nb = NDEV
    wts = {}
    vec = lambda *names: [mods[n] for n in names]

    def ffn_fwd(h2, l):
        u = _mm_wblk(h2, wts[f"up{l}"], BF16, f"ffn{l}_up", gb=nb // 2, split=2, tm=512)
        a, c = _convglu_fwd(u, small[f"conv_w{l}"], small[f"conv_b{l}"])
        f = _mm(a, wts[f"down{l}"], "nn", F32, f"ffn{l}_down", tk=4096)
        return (u, c), a, f

    def ffn_bwd(df, h2, uc, a, l):
        u, c = uc
        da = _mm(df, wts[f"down{l}"], "nt", BF16, f"ffn{l}_down_dx", tn=1536)
        dwd = _mm(a, df, "tn", BF16, f"ffn{l}_down_dw", tm=1536, tk=1024)
        du, dcw, dcb = _convglu_bwd(u, c, small[f"conv_w{l}"], da)
        dh2 = _mm_wblk_dx(du, wts[f"up{l}"], BF16, f"ffn{l}_up_dx", k=d, gb=nb // 2, split=2, tm=1024)
        dwu = _mm_wblk_dw(h2, du, f"ffn{l}_up_dw", nb=nb, gb=1, split=2, tk=t)
        return dh2, dwu, dwd, dcw, dcb

    (h_a,) = _row_fwd(_f_mod, [(x, d, 0)], vec("sh1_0", "sc1_0"), [BF16], tb=tb, name="l0_mod1")
    wts.update(get_w("l0a", h_a))
    proj_a = _mm_wblk(h_a, wts["a_in"], F32, "a_in", gb=nb // 2)
    ypre, states = _hgrn2_fwd(proj_a, lb, small["a_norm_g"], tb)
    wts.update(get_w("l0b", ypre))
    y_a = _mm(ypre, wts["a_out"], "nn", F32, "a_out")
    x1, h2_0 = _row_fwd(_f_res_mod, [(x, d, 0), (y_a, d, 0)], vec("g1_0", "sh2_0", "sc2_0"), [F32, BF16],
                        tb=tb, name="l0_res_mod2")
    u0, a0, f0 = ffn_fwd(h2_0, 0)
    x2, h_kv, h_q = _row_fwd(_f_res_mod2, [(x1, d, 0), (f0, d, 0)],
                             vec("g2_0", "kv_sh", "kv_sc", "sh1_1", "sc1_1"), [F32, BF16, BF16],
                             tb=tb, name="l0_res_kvmod_qmod")
    wts.update(get_w("l1", h_kv))
    proj_kv = _mm(h_kv, wts["kv"], "nt", F32, "kv_proj")
    proj_f = _mm(h_kv, wts["kv_f"], "nt", F32, "kv_fproj")
    v_b = proj_kv[:, d:].astype(BF16)
    f_logit_t = proj_f[:, :nh].T
    f_bias = small["kv_b_f"].reshape(nh, 1)
    f_col = _fgate_fwd(f_logit_t, f_bias).T
    (k_n,) = _row_fwd(_f_knorm_aug, [(proj_kv, HEAD, 0)] + [(piece, 1, 0) for piece in _split3(-f_col)],
                      [small["k_norm_g"]], [BF16], nsub=nh, tb=tb, name="k_norm")
    proj_q = _mm_wblk(h_q, wts["b_q"], F32, "b_q", gb=nb)
    (q_n,) = _row_fwd(_f_qnorm_aug, [(proj_q, HEAD, 0)], [small["q_norm_g"]], [BF16], nsub=nh, tb=tb,
                      name="q_norm")
    o_att, lse = _attn_fwd(q_n, k_n, v_b, f_col, attn_blk)
    (z,) = _row_fwd(_f_outgate, [(o_att, HEAD, 0), (proj_q, HEAD, 1)], [], [BF16], nsub=nh, tb=tb, name="out_gate")
    y_b = _mm(z, wts["b_out"], "nn", F32, "b_out")
    x3, h2_1 = _row_fwd(_f_res_mod, [(x2, d, 0), (y_b, d, 0)], vec("g1_1", "sh2_1", "sc2_1"), [F32, BF16],
                        tb=tb, name="l1_res_mod2")
    u1, a1, f1 = ffn_fwd(h2_1, 1)
    loss, dx4, df1, dg2_1 = _loss_call(x3, f1, mods["g2_1"], target, tb)

    g = {}
    dmods = {"g2_1": dg2_1}
    dh2, g["up1"], g["down1"], g["conv_w1"], g["conv_b1"] = ffn_bwd(df1, h2_1, u1, a1, 1)
    (dx2, dy_b), (dmods["g1_1"], dmods["sh2_1"], dmods["sc2_1"]) = _row_bwd(
        _f_res_mod, [(x2, d, 0), (y_b, d, 0)], vec("g1_1", "sh2_1", "sc2_1"),
        [(dx4, d, 0), (dh2, d, 0)], [F32, BF16], tb=tb, name="l1_res_mod2_bwd")
    dz = _mm(dy_b, wts["b_out"], "nt", BF16, "b_out_dx")
    g["b_out"] = _mm(z, dy_b, "tn", BF16, "b_out_dw", tk=1024)
    (do_att, dog), _ = _row_bwd(_f_outgate, [(o_att, HEAD, 0), (proj_q, HEAD, 1)], [], [(dz, HEAD, 0)],
                                [BF16, BF16], nsub=nh, tb=tb, name="out_gate_bwd")
    delta = _attn_delta(do_att, o_att, tb)
    dq_n, dk_n, dv, dfc_q, dfr_k = _attn_bwd(q_n, k_n, v_b, f_col, do_att, lse, delta, attn_blk)
    (dpq,), (g["q_norm_g"],) = _row_bwd(_f_qnorm, [(proj_q, HEAD, 0)], [small["q_norm_g"]],
                                        [(dq_n, HEAD, 0)], [BF16], nsub=nh, tb=tb, name="q_norm_bwd")
    dproj_q = jnp.concatenate([dpq, dog], axis=1)
    dh_q = _mm_wblk_dx(dproj_q, wts["b_q"], BF16, "b_q_dx", k=d, gb=nb)
    g["b_q"] = _mm_wblk_dw(h_q, dproj_q, "b_q_dw", nb=nb, gb=nb // 4, tk=t)
    (dpk,), (g["k_norm_g"],) = _row_bwd(_f_knorm, [(proj_kv, HEAD, 0)], [small["k_norm_g"]],
                                        [(dk_n, HEAD, 0)], [BF16], nsub=nh, tb=tb, name="k_norm_bwd")
    dproj_kv = jnp.concatenate([dpk, dv], axis=1)
    df_t = dfc_q.T + dfr_k.transpose(1, 0, 2).reshape(nh, t)
    dflogit_t, g["kv_b_f"] = _fgate_bwd(f_logit_t, f_bias, df_t)
    dproj_f = jnp.pad(dflogit_t.T, ((0, 0), (0, LANES - nh))).astype(BF16)
    dh_kv = _mm(dproj_kv, wts["kv"], "nn", BF16, "kv_proj_dx")
    dh_kv_f = _mm(dproj_f, wts["kv_f"], "nn", BF16, "kv_fproj_dx")
    g["kv"] = _mm(dproj_kv, h_kv, "tn", BF16, "kv_proj_dw", tk=1024)
    g["kv_f"] = _mm(dproj_f, h_kv, "tn", F32, "kv_fproj_dw", tk=1024)
    sent = put_g("l1", {n: g.pop(n) for n in ("b_out", "b_q", "kv", "kv_f", "up1", "down1")})
    (dx1, df0), (dmods["g2_0"], dmods["kv_sh"], dmods["kv_sc"], dmods["sh1_1"], dmods["sc1_1"]) = _row_bwd(
        _f_res_mod2, [(x1, d, 0), (f0, d, 0)], [mods["g2_0"] + sent] + vec("kv_sh", "kv_sc", "sh1_1", "sc1_1"),
        [(dx2, d, 0), (dh_kv, d, 0), (dh_q, d, 0)], [F32, BF16], tb=tb, name="l0_res_kvmod_qmod_bwd",
        cot_add=(1, dh_kv_f))
    dh2, g["up0"], g["down0"], g["conv_w0"], g["conv_b0"] = ffn_bwd(df0, h2_0, u0, a0, 0)
    (dx0, dy_a), (dmods["g1_0"], dmods["sh2_0"], dmods["sc2_0"]) = _row_bwd(
        _f_res_mod, [(x, d, 0), (y_a, d, 0)], vec("g1_0", "sh2_0", "sc2_0"),
        [(dx1, d, 0), (dh2, d, 0)], [F32, BF16], tb=tb, name="l0_res_mod2_bwd")
    dypre = _mm(dy_a, wts["a_out"], "nt", BF16, "a_out_dx")
    g["a_out"] = _mm(ypre, dy_a, "tn", BF16, "a_out_dw", tk=1024)
    sent = put_g("l0b", {n: g.pop(n) for n in ("a_out", "up0", "down0")})
    dproj_a, dlb, g["a_norm_g"] = _hgrn2_bwd(proj_a, lb + sent, small["a_norm_g"], states, dypre, tb)
    dh_a = _mm_wblk_dx(dproj_a, wts["a_in"], BF16, "a_in_dx", k=d, gb=nb, split=4, tm=512)
    put_g("l0a", {"a_in": _mm_wblk_dw(h_a, dproj_a, "a_in_dw", nb=nb, gb=1, split=4, tk=t)})
    (grad_x,), (dmods["sh1_0"], dmods["sc1_0"]) = _row_bwd(
        _f_mod, [(x, d, 0)], vec("sh1_0", "sc1_0"), [(dh_a, d, 0)], [F32], tb=tb, name="l0_mod1_bwd",
        add_to=(0, dx0))
    return loss, grad_x, dmods, dlb, g


def _position():
    return lax.axis_index("x"), lax.axis_index("y"), lax.axis_index("c")


def _hbm_specs(n):
    return [pl.BlockSpec(memory_space=pl.ANY)] * n


def _all_gather(arrs, name):
    n = len(arrs)

    def body(*refs):
        x_refs, out_refs = refs[:n], refs[n:2 * n]
        send_sems, recv_sems, local_sems = refs[2 * n:]
        x, y, cc = _position()
        me, sibling = (x, y, cc), (x, y, 1 - cc)
        chips = [(1 - x, y), (x, 1 - y), (1 - x, 1 - y)]

        def copy(a, k, block, to, src=None):
            slot = out_refs[a].at[4 * block[0] + 2 * block[1] + block[2]]
            return pltpu.make_async_remote_copy(
                src_ref=slot if src is None else src, dst_ref=slot,
                send_sem=send_sems.at[7 * a + k], recv_sem=recv_sems.at[7 * a + k],
                device_id=to, device_id_type=_MESH)

        local = [pltpu.make_async_copy(x_refs[a], out_refs[a].at[4 * x + 2 * y + cc], local_sems.at[a])
                 for a in range(n)]
        for cp in local:
            cp.start()
        first = []
        for a in range(n):
            first.append(copy(a, 0, me, sibling, src=x_refs[a]))
            first += [copy(a, 1 + j, me, (*chip, cc), src=x_refs[a]) for j, chip in enumerate(chips)]
        for cp in first:
            cp.start()
        passed = []
        for j, chip in enumerate(chips):
            for a in range(n):
                copy(a, 1 + j, (*chip, cc), me).wait_recv()
                fwd = copy(a, 4 + j, (*chip, cc), sibling)
                fwd.start()
                passed.append(fwd)
        for a in range(n):
            copy(a, 0, sibling, me).wait_recv()
        for j, chip in enumerate(chips):
            for a in range(n):
                copy(a, 4 + j, (*chip, 1 - cc), me).wait_recv()
        for cp in first + passed:
            cp.wait_send()
        for cp in local:
            cp.wait()

    return pl.pallas_call(
        body, name=name,
        out_shape=[jax.ShapeDtypeStruct((NDEV, *a.shape), a.dtype) for a in arrs],
        in_specs=_hbm_specs(n), out_specs=_hbm_specs(n),
        scratch_shapes=[pltpu.SemaphoreType.DMA((7 * n,)), pltpu.SemaphoreType.DMA((7 * n,)),
                        pltpu.SemaphoreType.DMA((n,))],
    )(*arrs)


_XCHG_EFFECT = pltpu.SideEffectType.DATAFLOW_SIDE_EFFECTING
ALL_PEERS = (1, 2, 3, 4, 5, 6, 7)
SAME_CORE = (2, 4, 6)


def _xchg_copies(src_refs, land_refs, send_sems, recv_sems, local_sems, scatter, rels):
    x, y, cc = _position()
    me = 4 * x + 2 * y + cc
    remote, local = [], []
    for a, (src, land) in enumerate(zip(src_refs, land_refs)):
        local.append(pltpu.make_async_copy(src.at[me] if scatter else src, land.at[me], local_sems.at[a]))
        for idx, rel in enumerate(rels):
            px = 1 - x if rel & 4 else x
            py = 1 - y if rel & 2 else y
            pc = 1 - cc if rel & 1 else cc
            k = len(rels) * a + idx
            remote.append(pltpu.make_async_remote_copy(
                src_ref=src.at[4 * px + 2 * py + pc] if scatter else src, dst_ref=land.at[me],
                send_sem=send_sems.at[k], recv_sem=recv_sems.at[k], device_id=(px, py, pc), device_id_type=_MESH))
    return remote, local


def _xchg_start(srcs, scatter, rels, after, name):
    n = len(srcs)
    lands = [lax.empty(s.shape if scatter else (NDEV, *s.shape), s.dtype) for s in srcs]

    def body(*refs):
        remote, local = _xchg_copies(refs[:n], refs[n:2 * n], *refs[2 * n + 1:2 * n + 4], scatter, rels)
        for cp in local + remote:
            cp.start()
        token = refs[-1]
        token[...] = jnp.zeros_like(token)

    hbm = pl.BlockSpec(memory_space=pltpu.HBM)
    sem = pl.BlockSpec(memory_space=pltpu.SEMAPHORE)
    out = pl.pallas_call(
        body, name=name,
        out_shape=(pltpu.SemaphoreType.DMA((len(rels) * n,)), pltpu.SemaphoreType.DMA((len(rels) * n,)),
                   pltpu.SemaphoreType.DMA((n,)),
                   *[pltpu.HBM(a.shape, a.dtype) for a in srcs + lands], jax.ShapeDtypeStruct((8, LANES), F32)),
        in_specs=[hbm] * (2 * n) + [pl.BlockSpec(memory_space=pl.ANY)],
        out_specs=(sem, sem, sem, *[hbm] * (2 * n), pl.BlockSpec(memory_space=pltpu.VMEM)),
        input_output_aliases={i: 3 + i for i in range(2 * n)},
        compiler_params=pltpu.CompilerParams(has_side_effects=_XCHG_EFFECT),
    )(*[pltpu.with_memory_space_constraint(a, pltpu.HBM) for a in srcs + lands], after)
    return out[:-1], out[-1][0, 0]


def _xchg_wait(handles, after, scatter, rels, name):
    n = (len(handles) - 3) // 2

    def body(*refs):
        remote, local = _xchg_copies(refs[:n], refs[n:2 * n], *refs[2 * n:2 * n + 3], scatter, rels)
        for cp in remote:
            cp.wait_send()
            cp.wait_recv()
        for cp in local:
            cp.wait()

    hbm = pl.BlockSpec(memory_space=pltpu.HBM)
    sem = pl.BlockSpec(memory_space=pltpu.SEMAPHORE)
    thru = list(handles[3:])
    out = pl.pallas_call(
        body, name=name,
        out_shape=tuple(pltpu.HBM(a.shape, a.dtype) for a in thru),
        in_specs=[hbm] * (2 * n) + [sem, sem, sem, pl.BlockSpec(memory_space=pl.ANY)],
        out_specs=tuple([hbm] * (2 * n)),
        input_output_aliases={i: i for i in range(2 * n)},
        compiler_params=pltpu.CompilerParams(has_side_effects=_XCHG_EFFECT),
    )(*thru, *handles[:3], after)
    return list(out[n:])


def _sibling_forward(lands, name):
    n = len(lands)

    def body(*refs):
        land_refs = refs[n:2 * n]
        send_sems, recv_sems = refs[2 * n:]
        x, y, cc = _position()

        def copy(a, q, core):
            slot = land_refs[a].at[2 * q + core]
            return pltpu.make_async_remote_copy(
                src_ref=slot, dst_ref=slot, send_sem=send_sems.at[NCHIP * a + q], recv_sem=recv_sems.at[NCHIP * a + q],
                device_id=(x, y, 1 - cc), device_id_type=_MESH)

        sends = [copy(a, q, cc) for a in range(n) for q in range(NCHIP)]
        for cp in sends:
            cp.start()
        for a in range(n):
            for q in range(NCHIP):
                copy(a, q, 1 - cc).wait_recv()
        for cp in sends:
            cp.wait_send()

    return pl.pallas_call(
        body, name=name,
        out_shape=[jax.ShapeDtypeStruct(a.shape, a.dtype) for a in lands],
        in_specs=_hbm_specs(n), out_specs=_hbm_specs(n),
        input_output_aliases={i: i for i in range(n)},
        scratch_shapes=[pltpu.SemaphoreType.DMA((NCHIP * n,)), pltpu.SemaphoreType.DMA((NCHIP * n,))],
    )(*lands)


def _slab_sum(slabs, name, tr=None):
    n, r, c = slabs.shape
    tr = r if tr is None else tr

    def body(s_ref, o_ref):
        acc = s_ref[0].astype(F32)
        for q in range(1, n):
            acc = acc + s_ref[q].astype(F32)
        o_ref[...] = acc

    return pl.pallas_call(body, name=name, grid=(r // tr,),
                          in_specs=[pl.BlockSpec((n, tr, c), lambda i: (0, i, 0))],
                          out_specs=pl.BlockSpec((tr, c), lambda i: (i, 0)),
                          out_shape=jax.ShapeDtypeStruct((r, c), F32),
                          compiler_params=_cparams(dimension_semantics=("parallel",)))(slabs)


def _ada_fwd(c_all, ada_w, kv_ada_w, logits):
    rows, d = c_all.shape
    n0, nkv = ada_w.shape[2], kv_ada_w.shape[1]

    def body(c_ref, w_ref, kw_ref, lg_ref, part_ref, cact_ref, lb_ref):
        ca = _silu(c_ref[...])
        cact_ref[...] = ca
        part_ref[:, 0:n0] = _bdot_raw(ca, w_ref[0], _NN)
        part_ref[:, n0:2 * n0] = _bdot_raw(ca, w_ref[1], _NN)
        part_ref[:, 2 * n0:2 * n0 + nkv] = _bdot_raw(ca, kw_ref[...], _NN)
        lb_ref[...] = _sigmoid(lg_ref[0:1, :] - lg_ref[1:2, :])

    vm = pl.BlockSpec(memory_space=pltpu.VMEM)
    return pl.pallas_call(
        body, name="ada_fwd", in_specs=[vm, vm, vm, vm], out_specs=[vm, vm, vm],
        out_shape=[jax.ShapeDtypeStruct((rows, 2 * n0 + nkv), F32), jax.ShapeDtypeStruct((rows, d), F32),
                   jax.ShapeDtypeStruct((1, d), F32)],
        compiler_params=_cparams(),
    )(c_all, ada_w, kv_ada_w, logits)


def _ada_bwd(c_act, dm0, dm1, dkv, lb, dlb):
    rows, d = c_act.shape

    def body(c_ref, d0_ref, d1_ref, dk_ref, lb_ref, dlb_ref, dw_ref, dkw_ref, dlg_ref):
        ca = c_ref[...]
        dw_ref[0] = _bdot_raw(ca, d0_ref[...], _TN)
        dw_ref[1] = _bdot_raw(ca, d1_ref[...], _TN)
        dkw_ref[...] = _bdot_raw(ca, dk_ref[...], _TN)
        lbv = lb_ref[...]
        dl0 = dlb_ref[...] * lbv * (1.0 - lbv)
        dlg_ref[0:1, :] = dl0
        dlg_ref[1:2, :] = -dl0

    vm = pl.BlockSpec(memory_space=pltpu.VMEM)
    return pl.pallas_call(
        body, name="ada_bwd", in_specs=[vm] * 6, out_specs=[vm, vm, vm],
        out_shape=[jax.ShapeDtypeStruct((2, d, dm0.shape[1]), F32), jax.ShapeDtypeStruct((d, dkv.shape[1]), F32),
                   jax.ShapeDtypeStruct((2, d), F32)],
        compiler_params=_cparams(),
    )(c_act, dm0, dm1, dkv, lb, dlb)


def _adamw(w, g, m, v, name, tr=512, after=None):
    r, c = w.shape
    tr = _divisor_tile(r, tr, unit=8)
    c1 = 1.0 - ADAM_B1 ** ADAM_STEP
    c2 = 1.0 - ADAM_B2 ** ADAM_STEP
    deps = [] if after is None else [after]

    def body(w_ref, g_ref, m_ref, v_ref, *rest):
        d_ref, mo_ref, vo_ref = rest[len(deps):]
        gv = g_ref[...]
        mn = ADAM_B1 * m_ref[...] + (1.0 - ADAM_B1) * gv
        vn = ADAM_B2 * v_ref[...] + (1.0 - ADAM_B2) * (gv * gv)
        d_ref[...] = -ADAM_LR * ((mn / c1) / (jnp.sqrt(vn / c2) + ADAM_EPS) + ADAM_WD * w_ref[...])
        mo_ref[...] = mn
        vo_ref[...] = vn

    spec = pl.BlockSpec((tr, c), lambda i: (i, 0))
    out = jax.ShapeDtypeStruct((r, c), F32)
    return pl.pallas_call(body, name=name, grid=(r // tr,),
                          in_specs=[spec] * 4 + [pl.BlockSpec(a.shape, lambda i: (0, 0)) for a in deps],
                          out_specs=[spec] * 3, out_shape=[out, out, out],
                          compiler_params=_cparams(dimension_semantics=("parallel",)))(w, g, m, v, *deps)


def _pad_rows(a, rows):
    return jnp.pad(a, ((0, rows - a.shape[0]), (0, 0)))


def _pack_small(parts, lanes=LANES, row_unit=8):
    flat = jnp.concatenate([p.reshape(-1).astype(F32) for p in parts])
    rows = _round_up(-(-flat.shape[0] // lanes), row_unit)
    return jnp.pad(flat, (0, rows * lanes - flat.shape[0])).reshape(rows, lanes)


def _unpack_small(flat, shapes):
    out, off = [], 0
    for s in shapes:
        n = 1
        for k in s:
            n *= k
        out.append(flat[off:off + n].reshape(s))
        off += n
    return out


def _pad_shard_cols(a, n_loc, n_pad):
    lead = a.shape[:-1]
    a = a.reshape(*lead, NDEV, n_loc)
    a = jnp.pad(a, [(0, 0)] * (len(lead) + 1) + [(0, n_pad - n_loc)])
    return a.reshape(*lead, NDEV * n_pad)


def _unpad_shard_cols(a, n_loc, n_pad):
    lead = a.shape[:-1]
    return a.reshape(*lead, NDEV, n_pad)[..., :n_loc].reshape(*lead, NDEV * n_loc)


def kernel(x, c, ada_w, ada_b, a_w_in, a_lb_logits, a_norm_g, a_w_out, kv_ada_w, kv_ada_b, kv_w, kv_b_f, k_norm_g, b_w_q, q_norm_g, b_w_out, ffn_w_up, ffn_conv_w, ffn_conv_b, ffn_w_down, loss_target, m_ada_w, m_ada_b, m_a_w_in, m_a_lb_logits, m_a_norm_g, m_a_w_out, m_kv_ada_w, m_kv_ada_b, m_kv_w, m_kv_b_f, m_k_norm_g, m_b_w_q, m_q_norm_g, m_b_w_out, m_ffn_w_up, m_ffn_conv_w, m_ffn_conv_b, m_ffn_w_down, v_ada_w, v_ada_b, v_a_w_in, v_a_lb_logits, v_a_norm_g, v_a_w_out, v_kv_ada_w, v_kv_ada_b, v_kv_w, v_kv_b_f, v_k_norm_g, v_b_w_q, v_q_norm_g, v_b_w_out, v_ffn_w_up, v_ffn_conv_w, v_ffn_conv_b, v_ffn_w_down):
    t, d = x.shape[1], x.shape[2]
    nh = d // HEAD
    ncw = ffn_w_up.shape[2]
    ncp = _round_up(ncw, LANES)
    two_f = ncw * NDEV
    ff = two_f // 2
    fp = ncp * NDEV // 2
    rd = ffn_w_down.shape[1]
    me = 4 * lax.axis_index("x") + 2 * lax.axis_index("y") + lax.axis_index("c")
    weights = dict(ada_w=ada_w, ada_b=ada_b, a_w_in=a_w_in, a_lb_logits=a_lb_logits, a_norm_g=a_norm_g,
                   a_w_out=a_w_out, kv_ada_w=kv_ada_w, kv_ada_b=kv_ada_b, kv_w=kv_w, kv_b_f=kv_b_f,
                   k_norm_g=k_norm_g, b_w_q=b_w_q, q_norm_g=q_norm_g, b_w_out=b_w_out, ffn_w_up=ffn_w_up,
                   ffn_conv_w=ffn_conv_w, ffn_conv_b=ffn_conv_b, ffn_w_down=ffn_w_down)
    m_in = dict(ada_w=m_ada_w, ada_b=m_ada_b, a_w_in=m_a_w_in, a_lb_logits=m_a_lb_logits, a_norm_g=m_a_norm_g,
                a_w_out=m_a_w_out, kv_ada_w=m_kv_ada_w, kv_ada_b=m_kv_ada_b, kv_w=m_kv_w, kv_b_f=m_kv_b_f,
                k_norm_g=m_k_norm_g, b_w_q=m_b_w_q, q_norm_g=m_q_norm_g, b_w_out=m_b_w_out, ffn_w_up=m_ffn_w_up,
                ffn_conv_w=m_ffn_conv_w, ffn_conv_b=m_ffn_conv_b, ffn_w_down=m_ffn_w_down)
    v_in = dict(ada_w=v_ada_w, ada_b=v_ada_b, a_w_in=v_a_w_in, a_lb_logits=v_a_lb_logits, a_norm_g=v_a_norm_g,
                a_w_out=v_a_w_out, kv_ada_w=v_kv_ada_w, kv_ada_b=v_kv_ada_b, kv_w=v_kv_w, kv_b_f=v_kv_b_f,
                k_norm_g=v_k_norm_g, b_w_q=v_b_w_q, q_norm_g=v_q_norm_g, b_w_out=v_b_w_out, ffn_w_up=v_ffn_w_up,
                ffn_conv_w=v_ffn_conv_w, ffn_conv_b=v_ffn_conv_b, ffn_w_down=v_ffn_w_down)
    order = list(weights)

    up_loc = jnp.pad(ffn_w_up, ((0, 0), (0, 0), (0, ncp - ncw))).astype(BF16)
    down_loc = ffn_w_down.astype(BF16)
    gather_names = {"l0b": ["a_out", "up0", "down0"], "l1": ["kv", "b_q", "b_out", "up1", "down1"]}
    shards = {"a_out": a_w_out[0].astype(BF16), "up0": up_loc[0], "down0": down_loc[0], "kv": kv_w.T.astype(BF16),
              "b_q": b_w_q[0].astype(BF16), "b_out": b_w_out[0].astype(BF16), "up1": up_loc[1],
              "down1": down_loc[1]}
    pre = _pack_small([c, a_lb_logits, ffn_conv_w])
    a_in_all, pre_all = _all_gather([a_w_in[0].astype(BF16), pre], "gather_a_w_in_and_small_inputs")
    pre_all = pre_all.reshape(NDEV, -1)
    c_all = pre_all[:, :d]
    logits = pre_all[:, d:d + 2 * HEAD].reshape(NDEV, 2, HEAD).transpose(1, 0, 2).reshape(2, d)
    conv_w_full = pre_all[:, d + 2 * HEAD:d + 2 * HEAD + 2 * CONV_TAPS * ncw]
    conv_w_full = conv_w_full.reshape(NDEV, 2, CONV_TAPS, ncw).transpose(1, 2, 0, 3).reshape(2, CONV_TAPS, two_f)

    part, c_act, lb = _ada_fwd(_pad_rows(c_all, 2 * NDEV), ada_w, kv_ada_w, logits)
    (part_all,) = _all_gather([part[:NDEV]], "gather_adaln")
    mine = lax.dynamic_index_in_dim(part_all, me, axis=1, keepdims=False)
    n0, nkv = ada_w.shape[2], kv_ada_w.shape[1]
    mod_names = ["sh1", "sc1", "g1", "sh2", "sc2", "g2"]
    mods = {}
    for l in range(2):
        row = mine[:, l * n0:(l + 1) * n0].reshape(-1) + ada_b[l]
        for k, nm in enumerate(mod_names):
            mods[f"{nm}_{l}"] = row[k * d:(k + 1) * d].reshape(1, d)
    kvrow = mine[:, 2 * n0:2 * n0 + nkv].reshape(-1) + kv_ada_b
    mods["kv_sh"], mods["kv_sc"] = kvrow[:d].reshape(1, d), kvrow[d:].reshape(1, d)

    in_flight = {}

    def start_gather(grp, dep):
        srcs = [shards[n] for n in gather_names[grp]]
        in_flight[grp], started = _xchg_start(srcs, False, SAME_CORE, dep, f"gather_{grp}_start")
        return started

    zero = start_gather("l0b", part_all)
    mods["sh1_0"] = mods["sh1_0"] + zero

    small = {"a_norm_g": a_norm_g, "k_norm_g": k_norm_g.reshape(1, HEAD), "q_norm_g": q_norm_g, "kv_b_f": kv_b_f}
    for l in range(2):
        small[f"conv_w{l}"] = _pad_shard_cols(conv_w_full[l], ncw, ncp).reshape(CONV_TAPS, 2, fp).transpose(1, 0, 2)
        small[f"conv_b{l}"] = _pad_shard_cols(ffn_conv_b[l], ncw, ncp).reshape(2, 1, fp)

    def get_w(grp, after):
        if grp == "l0a":
            return {"a_in": a_in_all}
        arrived = _xchg_wait(in_flight[grp], after, False, SAME_CORE, f"gather_{grp}_wait")
        full = list(_sibling_forward(arrived, f"gather_{grp}_to_sibling"))
        if grp == "l0b":
            started = start_gather("l1", full[0])
            full[0] = full[0] + started.astype(full[0].dtype)
        got = dict(zip(gather_names[grp], full))
        out = {}
        for n, a in got.items():
            if n in ("a_out", "b_out"):
                out[n] = a.reshape(d, d)
            elif n in ("down0", "down1"):
                dn = a.reshape(NCHIP, ff // NCHIP, d)
                out[n] = jnp.pad(dn, ((0, 0), (0, ncp - ncw), (0, 0))).reshape(fp, d)
            elif n == "kv":
                kv_t = a.reshape(NDEV * kv_w.shape[1], d)
                out["kv"] = kv_t[:2 * d]
                out["kv_f"] = jnp.pad(kv_t[2 * d:], ((0, LANES - nh), (0, 0)))
            else:
                out[n] = a
        return out

    scatter_flight, g_last = {}, {}

    def put_g(grp, gr):
        if grp == "l0a":
            g_last.update(gr)
            return zero
        if grp == "l1":
            g_kvw = jnp.concatenate([gr["kv"], gr["kv_f"][:nh].astype(BF16)], axis=0)
            arrs = {"kv_w": g_kvw.reshape(NDEV, kv_w.shape[1], d), "b_w_q": gr["b_q"],
                    "b_w_out": gr["b_out"].reshape(NDEV, d // NDEV, d), "up1": gr["up1"],
                    "down1": gr["down1"].reshape(NCHIP, ncp, d)[:, :ncw].reshape(NDEV, rd, d)}
        else:
            arrs = {"a_w_out": gr["a_out"].reshape(NDEV, d // NDEV, d), "up0": gr["up0"],
                    "down0": gr["down0"].reshape(NCHIP, ncp, d)[:, :ncw].reshape(NDEV, rd, d)}
        srcs = list(arrs.values())
        handles, sent = _xchg_start(srcs, True, ALL_PEERS, srcs[0], f"scatter_{grp}_start")
        scatter_flight[grp] = (list(arrs), handles)
        return sent

    loss_v, grad_x, dmods, dlb, g = _local_step(x[0], loss_target[0], mods, lb, small, get_w, put_g)

    g_sum = {}
    for grp in ("l1", "l0b"):
        names, handles = scatter_flight[grp]
        for nm, a in zip(names, _xchg_wait(handles, grad_x, True, ALL_PEERS, f"scatter_{grp}_wait")):
            g_sum[nm] = _slab_sum(a, f"rs_slab_sum_{nm}")

    def conv_w_grad(a):
        return _unpad_shard_cols(a.transpose(1, 0, 2).reshape(CONV_TAPS, 2 * fp), ncw, ncp)

    def conv_b_grad(a):
        return _unpad_shard_cols(a.reshape(2 * fp), ncw, ncp)

    dmod_vec = [dmods[f"{nm}_{l}"] for l in range(2) for nm in mod_names] + [dmods["kv_sh"], dmods["kv_sc"]]
    post = _pack_small(dmod_vec + [dlb, g["a_norm_g"], g["k_norm_g"], g["q_norm_g"],
                                   jnp.pad(g["kv_b_f"].reshape(-1), (0, LANES - nh)),
                                   conv_w_grad(g["conv_w0"]), conv_w_grad(g["conv_w1"]),
                                   conv_b_grad(g["conv_b0"]), conv_b_grad(g["conv_b1"]), loss_v])
    (post_all,) = _all_gather([post], "gather_small_grads")
    a_in_flight, a_in_sent = _xchg_start([g_last["a_in"]], True, ALL_PEERS, post_all, "scatter_l0a_start")
    a_in_sent = a_in_sent.reshape(1, 1)
    tot = _slab_sum(post_all, "small_grad_sum").reshape(-1)
    nmod = 14 * d
    (t_mod, t_lb, t_ang, t_kng, t_qng, t_bf, t_cw, t_cb, t_loss) = _unpack_small(
        tot, [(nmod,), (1, d), (1, HEAD), (HEAD,), (1, HEAD), (LANES,), (2, CONV_TAPS, two_f), (2, two_f),
              (LANES,)])
    loss = t_loss[0]
    dm_all = post_all.reshape(NDEV, -1)[:, :nmod]
    dm0 = lax.dynamic_slice_in_dim(dm_all[:, :6 * d], me * n0, n0, axis=1)
    dm1 = lax.dynamic_slice_in_dim(dm_all[:, 6 * d:12 * d], me * n0, n0, axis=1)
    dkv = lax.dynamic_slice_in_dim(dm_all[:, 12 * d:], me * nkv, nkv, axis=1)
    g_ada_w, g_kv_ada_w, g_logits = _ada_bwd(c_act, _pad_rows(dm0, 2 * NDEV), _pad_rows(dm1, 2 * NDEV),
                                              _pad_rows(dkv, 2 * NDEV), lb, t_lb)

    grads = {
        "ada_w": g_ada_w,
        "ada_b": t_mod[:12 * d].reshape(2, 6 * d),
        "a_lb_logits": lax.dynamic_slice_in_dim(g_logits, me * HEAD, HEAD, axis=1),
        "a_norm_g": t_ang,
        "a_w_out": g_sum["a_w_out"].reshape(a_w_out.shape),
        "kv_ada_w": g_kv_ada_w,
        "kv_ada_b": t_mod[12 * d:],
        "kv_w": g_sum["kv_w"].T,
        "kv_b_f": t_bf[:nh],
        "k_norm_g": t_kng,
        "b_w_q": g_sum["b_w_q"].reshape(b_w_q.shape),
        "q_norm_g": t_qng,
        "b_w_out": g_sum["b_w_out"].reshape(b_w_out.shape),
        "ffn_w_up": jnp.stack([g_sum["up0"][:, :ncw], g_sum["up1"][:, :ncw]]),
        "ffn_conv_w": lax.dynamic_slice_in_dim(t_cw, me * ncw, ncw, axis=2),
        "ffn_conv_b": t_cb,
        "ffn_w_down": jnp.stack([g_sum["down0"], g_sum["down1"]]),
    }

    big_adam = ["ada_w", "a_w_out", "kv_ada_w", "kv_w", "b_w_q", "b_w_out", "ffn_w_up", "ffn_w_down", "a_w_in"]
    small_adam = [n for n in order if n not in big_adam]
    delta, new_m, new_v = {}, {}, {}
    packs = [_pack_small([src[n] for n in small_adam]) for src in (weights, grads, m_in, v_in)]
    outs = _adamw(*packs, "adamw_small", tr=packs[0].shape[0])
    shapes = [weights[n].shape for n in small_adam]
    for dst, o in zip((delta, new_m, new_v), outs):
        for n, a in zip(small_adam, _unpack_small(o.reshape(-1), shapes)):
            dst[n] = a
    for n in big_adam:
        if n == "a_w_in":
            (landed,) = _xchg_wait(a_in_flight, new_v["ffn_w_down"], True, ALL_PEERS, "scatter_l0a_wait")
            grads[n] = _slab_sum(landed, "rs_slab_sum_a_w_in").reshape(a_w_in.shape)
        shp = weights[n].shape
        two_d = lambda a: a.reshape(-1, shp[-1])
        dl, mn, vn = _adamw(two_d(weights[n]), two_d(grads[n]), two_d(m_in[n]), two_d(v_in[n]), f"adamw_{n}",
                            after=a_in_sent)
        delta[n], new_m[n], new_v[n] = dl.reshape(shp), mn.reshape(shp), vn.reshape(shp)

    return (loss, grad_x.reshape(x.shape), *[grads[n] for n in order], *[delta[n] for n in order],
            *[new_m[n] for n in order], *[new_v[n] for n in order])
```

```python
import functools

import jax
import jax.numpy as jnp
from jax import lax
from jax.experimental import pallas as pl
from jax.experimental.pallas import tpu as pltpu

F32 = jnp.float32
BF16 = jnp.bfloat16

NDEV = 8
NCHIP = 4
HEAD = 128
A_CHUNK = 64
CONV_TAPS = 3
EPS = 1e-6
NEG_INF = -1e30
LANES = 128
VMEM_LIMIT = 48 * 1024 * 1024

ADAM_LR = 0.001
ADAM_B1 = 0.9
ADAM_B2 = 0.999
ADAM_EPS = 1e-08
ADAM_WD = 0.01
ADAM_STEP = 10

_NN = (((1,), (0,)), ((), ()))
_NT = (((1,), (1,)), ((), ()))
_TN = (((0,), (0,)), ((), ()))
_MESH = pl.DeviceIdType.MESH


def _cparams(**kw):
    return pltpu.CompilerParams(vmem_limit_bytes=VMEM_LIMIT, **kw)


def _divisor_tile(n, pref, unit=LANES):
    if n <= pref:
        return n
    best = None
    for t in range(unit, pref + 1, unit):
        if n % t == 0:
            best = t
    assert best is not None, (n, pref)
    return best


def _round_up(n, unit):
    return -(-n // unit) * unit


def _bdot_raw(a, b, dims):
    return lax.dot_general(a.astype(BF16), b.astype(BF16), dims, preferred_element_type=F32)


@jax.custom_vjp
def _dot_nn(a, b):
    return _bdot_raw(a, b, _NN)


@jax.custom_vjp
def _dot_nt(a, b):
    return _bdot_raw(a, b, _NT)


@jax.custom_vjp
def _dot_tn(a, b):
    return _bdot_raw(a, b, _TN)


_dot_nn.defvjp(lambda a, b: (_bdot_raw(a, b, _NN), (a, b)),
               lambda r, g: (_dot_nt(g, r[1]), _dot_tn(r[0], g)))
_dot_nt.defvjp(lambda a, b: (_bdot_raw(a, b, _NT), (a, b)),
               lambda r, g: (_dot_nn(g, r[1]), _dot_tn(g, r[0])))
_dot_tn.defvjp(lambda a, b: (_bdot_raw(a, b, _TN), (a, b)),
               lambda r, g: (_dot_nt(r[1], g), _dot_nn(r[0], g)))


def _f32dot(a, b):
    return lax.dot_general(a, b, _NN, precision=lax.Precision.HIGHEST, preferred_element_type=F32)


def _sigmoid(x):
    return jax.nn.sigmoid(x)


def _silu(x):
    return x * jax.nn.sigmoid(x)


def _rms(x):
    return x * lax.rsqrt(jnp.mean(x * x, axis=-1, keepdims=True) + EPS)


def _modulate(x, sh, sc):
    return _rms(x) * (1.0 + sc) + sh


def _mm_call(a, b, dims, a_spec, b_spec, o_spec, o_shape, grid, acc_tile, name):
    nk = grid[2]

    def body(a_ref, b_ref, o_ref, *acc):
        p = lax.dot_general(a_ref[...].astype(BF16), b_ref[...].astype(BF16), dims,
                            preferred_element_type=F32)
        if nk == 1:
            o_ref[...] = p.astype(o_ref.dtype)
        else:
            kk = pl.program_id(2)

            @pl.when(kk == 0)
            def _():
                acc[0][...] = p

            @pl.when(kk > 0)
            def _():
                acc[0][...] += p

            @pl.when(kk == nk - 1)
            def _():
                o_ref[...] = acc[0][...].astype(o_ref.dtype)

    return pl.pallas_call(
        body, name=name, grid=grid, in_specs=[a_spec, b_spec], out_specs=o_spec, out_shape=o_shape,
        scratch_shapes=[pltpu.VMEM(acc_tile, F32)] if nk > 1 else [],
        compiler_params=_cparams(dimension_semantics=("parallel", "parallel", "arbitrary")),
    )(a, b)


def _mm(a, b, mode, out_dtype, name, tm=1024, tn=1024, tk=2048):
    if mode == "nn":
        (m, k), (k2, n) = a.shape, b.shape
    elif mode == "nt":
        (m, k), (n, k2) = a.shape, b.shape
    else:
        (k, m), (k2, n) = a.shape, b.shape
    assert k == k2, (a.shape, b.shape, mode)
    tm, tn, tk = _divisor_tile(m, tm), _divisor_tile(n, tn), _divisor_tile(k, tk)
    if mode == "tn":
        a_spec = pl.BlockSpec((tk, tm), lambda i, j, kk: (kk, i))
    else:
        a_spec = pl.BlockSpec((tm, tk), lambda i, j, kk: (i, kk))
    if mode == "nt":
        b_spec = pl.BlockSpec((tn, tk), lambda i, j, kk: (j, kk))
    else:
        b_spec = pl.BlockSpec((tk, tn), lambda i, j, kk: (kk, j))
    return _mm_call(a, b, {"nn": _NN, "nt": _NT, "tn": _TN}[mode], a_spec, b_spec,
                    pl.BlockSpec((tm, tn), lambda i, j, kk: (i, j)), jax.ShapeDtypeStruct((m, n), out_dtype),
                    (m // tm, n // tn, k // tk), (tm, tn), name)


def _wblk_act_spec(rows, gb, nl, split, nb, row_axis, blk_axis):
    if split == 1:
        return pl.BlockSpec((rows, gb * nl), lambda *g: (g[row_axis], g[blk_axis]))
    groups = nb // split // gb
    return pl.BlockSpec((None, rows, gb * nl),
                        lambda *g: (g[blk_axis] // groups, g[row_axis], g[blk_axis] % groups))


def _mm_wblk(a, wb, out_dtype, name, *, gb, row_off=0, split=1, tm=1024):
    m, k = a.shape
    nb, _, nl = wb.shape
    assert (nb // split) % gb == 0
    tm = _divisor_tile(m, tm)

    def body(a_ref, b_ref, o_ref):
        av = a_ref[...].astype(BF16)
        for s in range(gb):
            o_ref[:, s * nl:(s + 1) * nl] = lax.dot_general(
                av, b_ref[s].astype(BF16), _NN, preferred_element_type=F32).astype(o_ref.dtype)

    o_shape = (m, nb * nl) if split == 1 else (split, m, nb // split * nl)
    return pl.pallas_call(
        body, name=name, grid=(nb // gb, m // tm),
        in_specs=[pl.BlockSpec((tm, k), lambda j, i: (i, 0)),
                  pl.BlockSpec((gb, k, nl), lambda j, i: (j, row_off, 0))],
        out_specs=_wblk_act_spec(tm, gb, nl, split, nb, 1, 0),
        out_shape=jax.ShapeDtypeStruct(o_shape, out_dtype),
        compiler_params=_cparams(dimension_semantics=("parallel", "parallel")),
    )(a, wb)


def _mm_wblk_dx(dy, wb, out_dtype, name, *, k, gb, row_off=0, split=1, tm=1024):
    nb, _, nl = wb.shape
    m = dy.shape[-2]
    tm = _divisor_tile(m, tm)
    nk = nb // gb
    per = nb // split
    whole = split > 1 and gb == nb
    assert whole or per % gb == 0

    def body(a_ref, b_ref, o_ref, *acc):
        p = None
        for s in range(gb):
            a_blk = a_ref[s // per, :, (s % per) * nl:(s % per + 1) * nl] if whole else a_ref[:, s * nl:(s + 1) * nl]
            q = lax.dot_general(a_blk.astype(BF16), b_ref[s].astype(BF16), _NT, preferred_element_type=F32)
            p = q if p is None else p + q
        if nk == 1:
            o_ref[...] = p.astype(o_ref.dtype)
        else:
            kk = pl.program_id(1)

            @pl.when(kk == 0)
            def _():
                acc[0][...] = p

            @pl.when(kk > 0)
            def _():
                acc[0][...] += p

            @pl.when(kk == nk - 1)
            def _():
                o_ref[...] = acc[0][...].astype(o_ref.dtype)

    return pl.pallas_call(
        body, name=name, grid=(m // tm, nk),
        in_specs=[pl.BlockSpec((split, tm, per * nl), lambda i, kk: (0, i, 0)) if whole
                  else _wblk_act_spec(tm, gb, nl, split, nb, 0, 1),
                  pl.BlockSpec((gb, k, nl), lambda i, kk: (kk, row_off, 0))],
        out_specs=pl.BlockSpec((tm, k), lambda i, kk: (i, 0)),
        out_shape=jax.ShapeDtypeStruct((m, k), out_dtype),
        scratch_shapes=[pltpu.VMEM((tm, k), F32)] if nk > 1 else [],
        compiler_params=_cparams(dimension_semantics=("parallel", "arbitrary")),
    )(dy, wb)


def _mm_wblk_dw(x, dy, name, *, nb, gb, split=1, tk=1024):
    t, k = x.shape
    assert (nb // split) % gb == 0
    nl = dy.shape[-1] * split // nb
    tk = _divisor_tile(t, tk)
    nk = t // tk

    def body(a_ref, b_ref, o_ref, *acc):
        kk = pl.program_id(1)
        av = a_ref[...].astype(BF16)
        for s in range(gb):
            p = lax.dot_general(av, b_ref[:, s * nl:(s + 1) * nl].astype(BF16), _TN, preferred_element_type=F32)
            if nk == 1:
                o_ref[s] = p.astype(o_ref.dtype)
                continue

            @pl.when(kk == 0)
            def _():
                acc[0][s] = p

            @pl.when(kk > 0)
            def _():
                acc[0][s] += p

        if nk > 1:
            @pl.when(kk == nk - 1)
            def _():
                o_ref[...] = acc[0][...].astype(o_ref.dtype)

    return pl.pallas_call(
        body, name=name, grid=(nb // gb, nk),
        in_specs=[pl.BlockSpec((tk, k), lambda j, kk: (kk, 0)), _wblk_act_spec(tk, gb, nl, split, nb, 1, 0)],
        out_specs=pl.BlockSpec((gb, k, nl), lambda j, kk: (j, 0, 0)),
        out_shape=jax.ShapeDtypeStruct((nb, k, nl), BF16),
        scratch_shapes=[pltpu.VMEM((gb, k, nl), F32)] if nk > 1 else [],
        compiler_params=_cparams(dimension_semantics=("parallel", "arbitrary")),
    )(x, dy)


def _row_specs(rows, tb, nsub):
    return [pl.BlockSpec((tb, nsub * cw), functools.partial(lambda i, off: (i, off), off=off))
            for (_, cw, off) in rows]


def _vec_specs(params):
    return [pl.BlockSpec(p.shape, lambda i: (0, 0)) for p in params]


def _row_fwd(f, rows, params, out_dtypes, *, nsub=1, tb, name):
    t = rows[0][0].shape[0]
    tb = min(tb, t)
    n_r, n_p = len(rows), len(params)
    blk = [jax.ShapeDtypeStruct((tb, cw), F32) for (_, cw, _) in rows]
    blk += [jax.ShapeDtypeStruct(p.shape, F32) for p in params]
    out_avals = jax.eval_shape(f, *blk)

    def body(*refs):
        pv = [r[...] for r in refs[n_r:n_r + n_p]]
        for s in range(nsub):
            vals = [r[:, s * cw:(s + 1) * cw].astype(F32) for r, (_, cw, _) in zip(refs[:n_r], rows)]
            outs = f(*vals, *pv)
            for o_ref, o in zip(refs[n_r + n_p:], outs):
                w = o.shape[1]
                o_ref[:, s * w:(s + 1) * w] = o.astype(o_ref.dtype)

    return pl.pallas_call(
        body, name=name,
        grid=(t // tb,),
        in_specs=_row_specs(rows, tb, nsub) + _vec_specs(params),
        out_specs=[pl.BlockSpec((tb, nsub * av.shape[1]), lambda i: (i, 0)) for av in out_avals],
        out_shape=[jax.ShapeDtypeStruct((t, nsub * av.shape[1]), dt) for av, dt in zip(out_avals, out_dtypes)],
        compiler_params=_cparams(dimension_semantics=("parallel",)),
    )(*[r[0] for r in rows], *params)


def _row_bwd(f, rows, params, cots, row_grad_dtypes, *, nsub=1, tb, name, add_to=None, cot_add=None):
    t = rows[0][0].shape[0]
    tb = min(tb, t)
    n_r, n_p, n_c = len(rows), len(params), len(cots)
    want = [j for j in range(n_r) if row_grad_dtypes[j] is not None]
    extra = [] if add_to is None else [(add_to[1], rows[add_to[0]][1], 0)]
    extra += [] if cot_add is None else [(cot_add[1], cots[cot_add[0]][1], 0)]

    def body(*refs):
        i = pl.program_id(0)
        r_in, p_in = refs[:n_r], refs[n_r:n_r + n_p]
        c_in = refs[n_r + n_p:n_r + n_p + n_c]
        e_in = refs[n_r + n_p + n_c:n_r + n_p + n_c + len(extra)]
        outs = refs[n_r + n_p + n_c + len(extra):]
        pv = [r[...] for r in p_in]
        psum = [None] * n_p
        for s in range(nsub):
            vals = [r[:, s * cw:(s + 1) * cw].astype(F32) for r, (_, cw, _) in zip(r_in, rows)]
            cvals = [r[:, s * cw:(s + 1) * cw].astype(F32) for r, (_, cw, _) in zip(c_in, cots)]
            if cot_add is not None:
                cw = cots[cot_add[0]][1]
                cvals[cot_add[0]] = cvals[cot_add[0]] + e_in[-1][:, s * cw:(s + 1) * cw]
            _, vjp_fn = jax.vjp(f, *vals, *pv)
            grads = vjp_fn(tuple(cvals))
            for o_ref, jr in zip(outs[:len(want)], want):
                cw = rows[jr][1]
                gr = grads[jr]
                if add_to is not None and jr == add_to[0]:
                    gr = gr + e_in[0][:, s * cw:(s + 1) * cw]
                o_ref[:, s * cw:(s + 1) * cw] = gr.astype(o_ref.dtype)
            for jp in range(n_p):
                psum[jp] = grads[n_r + jp] if psum[jp] is None else psum[jp] + grads[n_r + jp]
        for o_ref, g in zip(outs[len(want):], psum):
            @pl.when(i == 0)
            def _():
                o_ref[...] = g

            @pl.when(i > 0)
            def _():
                o_ref[...] += g

    out_specs = [pl.BlockSpec((tb, nsub * rows[jr][1]), lambda i: (i, 0)) for jr in want]
    out_shape = [jax.ShapeDtypeStruct((t, nsub * rows[jr][1]), row_grad_dtypes[jr]) for jr in want]
    out_specs += _vec_specs(params)
    out_shape += [jax.ShapeDtypeStruct(p.shape, F32) for p in params]
    res = pl.pallas_call(
        body, name=name,
        grid=(t // tb,),
        in_specs=_row_specs(rows, tb, nsub) + _vec_specs(params) + _row_specs(cots, tb, nsub)
        + _row_specs(extra, tb, nsub),
        out_specs=out_specs, out_shape=out_shape,
        compiler_params=_cparams(dimension_semantics=("arbitrary",)),
    )(*[r[0] for r in rows], *params, *[c[0] for c in cots], *[e[0] for e in extra])
    return res[:len(want)], res[len(want):]


def _f_mod(x, sh, sc):
    return (_modulate(x, sh, sc),)


def _f_res_mod(x, y, g, sh, sc):
    x1 = x + g * y
    return x1, _modulate(x1, sh, sc)


def _f_res_mod2(x, y, g, sh_a, sc_a, sh_b, sc_b):
    x1 = x + g * y
    return x1, _modulate(x1, sh_a, sc_a), _modulate(x1, sh_b, sc_b)


def _f_qnorm(p, g):
    return (_rms(p) * g * (HEAD ** -0.5),)


def _f_knorm(p, g):
    return (_rms(p) * g,)


def _f_qnorm_aug(p, g):
    lane = lax.broadcasted_iota(jnp.int32, p.shape, 1)
    return (jnp.concatenate([_rms(p) * g * (HEAD ** -0.5), jnp.where(lane < 3, 1.0, 0.0)], axis=1),)


def _f_knorm_aug(p, c0, c1, c2, g):
    lane = lax.broadcasted_iota(jnp.int32, p.shape, 1)
    aug = jnp.where(lane == 0, c0, jnp.where(lane == 1, c1, jnp.where(lane == 2, c2, 0.0)))
    return (jnp.concatenate([_rms(p) * g, aug], axis=1),)


def _split3(a):
    round_bf16 = lambda v: lax.reduce_precision(v, exponent_bits=8, mantissa_bits=7)
    hi = round_bf16(a)
    mid = round_bf16(a - hi)
    lo = round_bf16(a - hi - mid)
    return hi.astype(BF16), mid.astype(BF16), lo.astype(BF16)


def _f_outgate(o, og):
    return (o * _sigmoid(og),)


def _loss_call(x3, f, g2, target, tb):
    t, d = x3.shape
    tb = min(tb, t)

    def body(x_ref, f_ref, g_ref, t_ref, loss_ref, dx_ref, df_ref, dg_ref):
        i = pl.program_id(0)
        fv = f_ref[...]
        g = g_ref[...]
        e = x_ref[...] + g * fv - t_ref[...]
        dx = e * (1.0 / d)
        part = 0.5 * jnp.sum(jnp.sum(e * dx, axis=1, keepdims=True), axis=0, keepdims=True)
        dx_ref[...] = dx
        df_ref[...] = (g * dx).astype(df_ref.dtype)
        dg = jnp.sum(dx * fv, axis=0, keepdims=True)

        @pl.when(i == 0)
        def _():
            loss_ref[...] = jnp.broadcast_to(part, loss_ref.shape)
            dg_ref[...] = dg

        @pl.when(i > 0)
        def _():
            loss_ref[...] += jnp.broadcast_to(part, loss_ref.shape)
            dg_ref[...] += dg

    row = pl.BlockSpec((tb, d), lambda i: (i, 0))
    vec = pl.BlockSpec((1, d), lambda i: (0, 0))
    return pl.pallas_call(
        body, name="loss_head",
        grid=(t // tb,),
        in_specs=[row, row, vec, row],
        out_specs=[pl.BlockSpec((1, LANES), lambda i: (0, 0)), row, row, vec],
        out_shape=[jax.ShapeDtypeStruct((1, LANES), F32), jax.ShapeDtypeStruct((t, d), F32),
                   jax.ShapeDtypeStruct((t, d), BF16), jax.ShapeDtypeStruct((1, d), F32)],
        compiler_params=_cparams(dimension_semantics=("arbitrary",)),
    )(x3, f, g2, target)


def _hg_mask(tb):
    br = lax.broadcasted_iota(jnp.int32, (tb, tb), 0)
    bs = lax.broadcasted_iota(jnp.int32, (tb, tb), 1)
    return jnp.logical_and(br // A_CHUNK == bs // A_CHUNK, bs <= br).astype(F32)


def _hg_consts(mask):
    c = A_CHUNK
    r = lax.broadcasted_iota(jnp.int32, (c, c), 0)
    s = lax.broadcasted_iota(jnp.int32, (c, c), 1)
    return (s <= r).astype(F32), (r <= s).astype(F32), mask > 0.5


def _chunk_apply(mat, x):
    c = mat.shape[0]
    return jnp.concatenate([_f32dot(mat, x[i * c:(i + 1) * c]) for i in range(x.shape[0] // c)], axis=0)


@jax.custom_vjp
def _chunk_cumsum(x, tri, tri_t):
    return _chunk_apply(tri, x)


_chunk_cumsum.defvjp(lambda x, tri, tri_t: (_chunk_apply(tri, x), (tri, tri_t)),
                     lambda r, g: (_chunk_apply(r[1], g), jnp.zeros_like(r[0]), jnp.zeros_like(r[1])))


def _per_chunk(a, b, dims):
    return jnp.stack([_bdot_raw(a[i], b[i], dims) for i in range(a.shape[0])])


@jax.custom_vjp
def _chunk_tn(a, b):
    return _per_chunk(a, b, _TN)


@jax.custom_vjp
def _chunk_nt(a, b):
    return _per_chunk(a, b, _NT)


@jax.custom_vjp
def _chunk_nn(a, b):
    return _per_chunk(a, b, _NN)


_chunk_tn.defvjp(lambda a, b: (_per_chunk(a, b, _TN), (a, b)),
                 lambda r, g: (_chunk_nt(r[1], g), _chunk_nn(r[0], g)))
_chunk_nt.defvjp(lambda a, b: (_per_chunk(a, b, _NT), (a, b)),
                 lambda r, g: (_chunk_nn(g, r[1]), _chunk_tn(g, r[0])))
_chunk_nn.defvjp(lambda a, b: (_per_chunk(a, b, _NN), (a, b)),
                 lambda r, g: (_chunk_nt(g, r[1]), _chunk_tn(r[0], g)))


def _scan_states(decay, m, st):
    sts = []
    for i in range(m.shape[0]):
        sts.append(st)
        st = st * decay[i] + m[i]
    return jnp.stack(sts), st


@jax.custom_vjp
def _state_scan(decay, m, st):
    return _scan_states(decay, m, st)


def _state_scan_fwd(decay, m, st):
    sts, st_out = _scan_states(decay, m, st)
    return (sts, st_out), (decay, sts)


def _state_scan_bwd(res, cts):
    decay, sts = res
    d_sts, g = cts
    d_decay, d_m = [], []
    for i in range(sts.shape[0] - 1, -1, -1):
        d_m.append(g)
        d_decay.append(jnp.sum(g * sts[i], axis=0, keepdims=True))
        g = g * decay[i] + d_sts[i]
    return jnp.stack(d_decay[::-1]), jnp.stack(d_m[::-1]), g


_state_scan.defvjp(_state_scan_fwd, _state_scan_bwd)


def _hg_block(qp, fp, ip, gp, lb, ng, st, tri, tri_t, bd_causal):
    tb = qp.shape[0]
    c = A_CHUNK
    n = tb // c
    q = _silu(qp)
    fg = lb + (1.0 - lb) * _sigmoid(fp)
    logf = jnp.log(fg)
    k = 1.0 - fg
    b3 = _chunk_cumsum(logf, tri, tri_t).reshape(n, c, HEAD)
    pos = lax.broadcasted_iota(jnp.int32, (1, c, 1), 1)
    b_mid = lax.stop_gradient(jnp.sum(jnp.where(pos == c // 2, b3, 0.0), axis=1, keepdims=True))
    b_last = jnp.sum(jnp.where(pos == c - 1, b3, 0.0), axis=1, keepdims=True)
    q3, k3, v3 = q.reshape(n, c, HEAD), k.reshape(n, c, HEAD), ip.reshape(n, c, HEAD)
    scores = _dot_nt((q3 * jnp.exp(b3 - b_mid)).reshape(tb, HEAD), (k3 * jnp.exp(b_mid - b3)).reshape(tb, HEAD))
    o_intra = _dot_nn(jnp.where(bd_causal, scores, 0.0), ip)
    states, st_new = _state_scan(jnp.exp(b_last), _chunk_tn(v3, k3 * jnp.exp(b_last - b3)), st)
    o = o_intra + _chunk_nt(q3 * jnp.exp(b3), states).reshape(tb, HEAD)
    y = _rms(o) * ng * _silu(gp)
    return y, st_new


HG_HEADS = 2


def _hg_specs(tb, nh, rev_nb=None):
    wide = HG_HEADS * HEAD
    per = nh // HG_HEADS

    def row(part):
        if rev_nb is None:
            return pl.BlockSpec((tb, wide), functools.partial(lambda h, i, off: (i, off + h), off=part * per))
        return pl.BlockSpec((tb, wide),
                            functools.partial(lambda h, i, off: (rev_nb - 1 - i, off + h), off=part * per))
    return [row(0), row(1), row(2), row(3),
            pl.BlockSpec((1, wide), lambda h, i: (0, h)), pl.BlockSpec((1, HEAD), lambda h, i: (0, 0)),
            pl.BlockSpec((tb, tb), lambda h, i: (0, 0))]


def _hgrn2_fwd(proj, lb, ng, tb):
    t = proj.shape[0]
    nh = proj.shape[1] // (4 * HEAD)
    tb = min(tb, t)
    nb = t // tb
    wide = HG_HEADS * HEAD

    def body(q_ref, f_ref, i_ref, g_ref, lb_ref, ng_ref, mask_ref, y_ref, s_ref, st_ref):
        i = pl.program_id(1)

        @pl.when(i == 0)
        def _():
            st_ref[...] = jnp.zeros_like(st_ref)

        consts = _hg_consts(mask_ref[...])
        for p in range(HG_HEADS):
            cs = slice(p * HEAD, (p + 1) * HEAD)
            st = st_ref[p]
            s_ref[p, 0] = st
            y, st_new = _hg_block(q_ref[:, cs], f_ref[:, cs], i_ref[:, cs], g_ref[:, cs], lb_ref[:, cs],
                                  ng_ref[...], st, *consts)
            y_ref[:, cs] = y.astype(y_ref.dtype)
            st_ref[p] = st_new

    return pl.pallas_call(
        body, name="hgrn2_fwd",
        grid=(nh // HG_HEADS, nb),
        in_specs=_hg_specs(tb, nh),
        out_specs=[pl.BlockSpec((tb, wide), lambda h, i: (i, h)),
                   pl.BlockSpec((HG_HEADS, 1, HEAD, HEAD), lambda h, i: (h, i, 0, 0))],
        out_shape=[jax.ShapeDtypeStruct((t, nh * HEAD), BF16),
                   jax.ShapeDtypeStruct((nh, nb, HEAD, HEAD), F32)],
        scratch_shapes=[pltpu.VMEM((HG_HEADS, HEAD, HEAD), F32)],
        compiler_params=_cparams(dimension_semantics=("parallel", "arbitrary")),
    )(proj, proj, proj, proj, lb, ng, _hg_mask(tb))


def _hgrn2_bwd(proj, lb, ng, states, dy, tb):
    t = proj.shape[0]
    nh = proj.shape[1] // (4 * HEAD)
    tb = min(tb, t)
    nb = t // tb
    wide = HG_HEADS * HEAD

    def body(q_ref, f_ref, i_ref, g_ref, lb_ref, ng_ref, mask_ref, s_ref, dy_ref,
             dp_ref, dlb_ref, dng_ref, dst_ref):
        h, i = pl.program_id(0), pl.program_id(1)
        consts = _hg_consts(mask_ref[...])

        @pl.when(i == 0)
        def _():
            dst_ref[...] = jnp.zeros_like(dst_ref)
            dlb_ref[...] = jnp.zeros_like(dlb_ref)

        @pl.when(jnp.logical_and(i == 0, h == 0))
        def _():
            dng_ref[...] = jnp.zeros_like(dng_ref)

        def fn(qp, fp, ip, gp, lbx, ngx, stx):
            return _hg_block(qp, fp, ip, gp, lbx, ngx, stx, *consts)

        for p in range(HG_HEADS):
            cs = slice(p * HEAD, (p + 1) * HEAD)
            _, vjp_fn = jax.vjp(fn, q_ref[:, cs], f_ref[:, cs], i_ref[:, cs], g_ref[:, cs], lb_ref[:, cs],
                                ng_ref[...], s_ref[p, 0])
            *gparts, glb, gng, dst = vjp_fn((dy_ref[:, cs].astype(F32), dst_ref[p]))
            for part, gpart in enumerate(gparts):
                dp_ref[part, :, cs] = gpart.astype(dp_ref.dtype)
            dst_ref[p] = dst
            dlb_ref[:, cs] += glb
            dng_ref[...] += gng

    rev = lambda h, i: (nb - 1 - i, h)
    return pl.pallas_call(
        body, name="hgrn2_bwd",
        grid=(nh // HG_HEADS, nb),
        in_specs=_hg_specs(tb, nh, rev_nb=nb) + [
            pl.BlockSpec((HG_HEADS, 1, HEAD, HEAD), lambda h, i: (h, nb - 1 - i, 0, 0)),
            pl.BlockSpec((tb, wide), rev)],
        out_specs=[pl.BlockSpec((4, tb, wide), lambda h, i: (0, nb - 1 - i, h)),
                   pl.BlockSpec((1, wide), lambda h, i: (0, h)), pl.BlockSpec((1, HEAD), lambda h, i: (0, 0))],
        out_shape=[jax.ShapeDtypeStruct((4, t, nh * HEAD), BF16),
                   jax.ShapeDtypeStruct((1, nh * HEAD), F32), jax.ShapeDtypeStruct((1, HEAD), F32)],
        scratch_shapes=[pltpu.VMEM((HG_HEADS, HEAD, HEAD), F32)],
        compiler_params=_cparams(dimension_semantics=("arbitrary", "arbitrary")),
    )(proj, proj, proj, proj, lb, ng, _hg_mask(tb), states, dy)


def _fgate_consts(cb):
    r = lax.broadcasted_iota(jnp.int32, (cb, cb), 0)
    s = lax.broadcasted_iota(jnp.int32, (cb, cb), 1)
    return (r <= s).astype(F32), (r >= s).astype(F32)


def _fgate_fwd(xt, bias, cb=512):
    nh, t = xt.shape
    cb = min(cb, t)

    def body(x_ref, b_ref, o_ref):
        upper, _ = _fgate_consts(cb)
        carry = jnp.zeros((nh, 1), F32)
        for blk in range(t // cb):
            z = x_ref[:, blk * cb:(blk + 1) * cb] + b_ref[...]
            logf = jnp.minimum(z, 0.0) - jnp.log(1.0 + jnp.exp(-jnp.abs(z)))
            cs = _f32dot(logf, upper) + carry
            o_ref[:, blk * cb:(blk + 1) * cb] = cs
            carry = cs[:, cb - 1:cb]

    vm = pl.BlockSpec(memory_space=pltpu.VMEM)
    return pl.pallas_call(
        body, name="fgate_fwd", in_specs=[vm, vm], out_specs=vm,
        out_shape=jax.ShapeDtypeStruct((nh, t), F32), compiler_params=_cparams(),
    )(xt, bias)


def _fgate_bwd(xt, bias, dft, cb=512):
    nh, t = xt.shape
    cb = min(cb, t)
    nblk = t // cb

    def body(x_ref, b_ref, d_ref, dx_ref, db_ref):
        _, lower = _fgate_consts(cb)
        carry = jnp.zeros((nh, 1), F32)
        db = jnp.zeros((nh, 1), F32)
        for blk in range(nblk - 1, -1, -1):
            sl = slice(blk * cb, (blk + 1) * cb)
            dlogf = _f32dot(d_ref[:, sl], lower) + carry
            carry = dlogf[:, 0:1]
            z = x_ref[:, sl] + b_ref[...]
            dz = dlogf * (1.0 - _sigmoid(z))
            dx_ref[:, sl] = dz
            db = db + jnp.sum(dz, axis=1, keepdims=True)
        db_ref[...] = db

    vm = pl.BlockSpec(memory_space=pltpu.VMEM)
    return pl.pallas_call(
        body, name="fgate_bwd", in_specs=[vm, vm, vm], out_specs=[vm, vm],
        out_shape=[jax.ShapeDtypeStruct((nh, t), F32), jax.ShapeDtypeStruct((nh, 1), F32)],
        compiler_params=_cparams(),
    )(xt, bias, dft)


def _attn_fwd(q, k, v, f_col, blk):
    t, width = v.shape
    nh = width // HEAD
    nq = t // blk

    def body(q_ref, k_ref, v_ref, fc_ref, o_ref, lse_ref):
        i = pl.program_id(0)
        tri = (lax.broadcasted_iota(jnp.int32, (blk, blk), 1) <= lax.broadcasted_iota(jnp.int32, (blk, blk), 0))
        for h in range(nh):
            cs = slice(h * HEAD, (h + 1) * HEAD)
            cs2 = slice(2 * h * HEAD, 2 * (h + 1) * HEAD)
            qh = q_ref[:, cs2]

            def tile(j, carry, masked):
                m, l, acc = carry
                rs = pl.ds(pl.multiple_of(j * blk, blk), blk)
                s = _bdot_raw(qh, k_ref[rs, cs2], _NT)
                if masked:
                    s = jnp.where(tri, s, NEG_INF)
                m_new = jnp.maximum(m, jnp.max(s, axis=1, keepdims=True))
                p = jnp.exp(s - m_new)
                alpha = jnp.exp(m - m_new)
                l_new = alpha * l + jnp.sum(p, axis=1, keepdims=True)
                acc_new = alpha * acc + _bdot_raw(p, v_ref[rs, cs], _NN)
                return m_new, l_new, acc_new

            init = (jnp.full((blk, 1), NEG_INF, F32), jnp.zeros((blk, 1), F32), jnp.zeros((blk, HEAD), F32))
            carry = lax.fori_loop(0, i, lambda j, c: tile(j, c, False), init)
            m, l, acc = tile(i, carry, True)
            o_ref[:, cs] = acc / l
            lse_ref[:, h:h + 1] = m + jnp.log(l) + fc_ref[:, h:h + 1]

    vm = pl.BlockSpec(memory_space=pltpu.VMEM)
    return pl.pallas_call(
        body, name="fox_attn_fwd",
        grid=(nq,),
        in_specs=[pl.BlockSpec((blk, 2 * width), lambda i: (i, 0)), vm, vm,
                  pl.BlockSpec((blk, nh), lambda i: (i, 0))],
        out_specs=[pl.BlockSpec((blk, width), lambda i: (i, 0)), pl.BlockSpec((blk, nh), lambda i: (i, 0))],
        out_shape=[jax.ShapeDtypeStruct((t, width), F32), jax.ShapeDtypeStruct((t, nh), F32)],
        compiler_params=_cparams(dimension_semantics=("parallel",)),
    )(q, k, v, f_col)


def _attn_delta(do, o, tb):
    t, width = o.shape
    nh = width // HEAD
    tb = min(tb, t)

    def body(do_ref, o_ref, dl_ref):
        for h in range(nh):
            cs = slice(h * HEAD, (h + 1) * HEAD)
            dl_ref[:, h:h + 1] = jnp.sum(do_ref[:, cs].astype(F32) * o_ref[:, cs], axis=1, keepdims=True)

    wide = pl.BlockSpec((tb, width), lambda i: (i, 0))
    return pl.pallas_call(body, name="fox_attn_delta", grid=(t // tb,), in_specs=[wide, wide],
                          out_specs=pl.BlockSpec((tb, nh), lambda i: (i, 0)),
                          out_shape=jax.ShapeDtypeStruct((t, nh), F32),
                          compiler_params=_cparams(dimension_semantics=("parallel",)))(do, o)


ATTN_BWD_GROUPS = 4


def _attn_bwd(q, k, v, f_col, do, lse, delta, blk):
    t, width = v.shape
    nh = width // HEAD
    nq = t // blk
    hpg = nh // ATTN_BWD_GROUPS
    gw = hpg * HEAD

    def body(q_ref, do_ref, k_ref, v_ref, fc_ref, lse_ref, dl_ref,
             dq_ref, dk_ref, dv_ref, dfc_ref, dfr_ref):
        g, j = pl.program_id(0), pl.program_id(1)
        tri = (lax.broadcasted_iota(jnp.int32, (blk, blk), 1) <= lax.broadcasted_iota(jnp.int32, (blk, blk), 0))

        @pl.when(j == 0)
        def _():
            dq_ref[...] = jnp.zeros_like(dq_ref)
            dfc_ref[...] = jnp.zeros_like(dfc_ref)

        for h in range(hpg):
            cs = slice(h * HEAD, (h + 1) * HEAD)
            cs2 = slice(2 * h * HEAD, 2 * (h + 1) * HEAD)
            csq = slice(2 * h * HEAD, (2 * h + 1) * HEAD)
            kj2 = k_ref[:, cs2]
            kj = k_ref[:, csq]
            vj = v_ref[:, cs]

            def tile(i, carry, masked):
                dk, dv, dfs = carry
                rs = pl.ds(pl.multiple_of(i * blk, blk), blk)
                qi = q_ref[rs, csq]
                doi = do_ref[rs, cs]
                bias = fc_ref[0, rs, h:h + 1] - lse_ref[0, rs, h:h + 1]
                p = jnp.exp(_bdot_raw(q_ref[rs, cs2], kj2, _NT) + bias)
                if masked:
                    p = jnp.where(tri, p, 0.0)
                ds = p * (_bdot_raw(doi, vj, _NT) - dl_ref[0, rs, h:h + 1])
                dsb = ds.astype(BF16)
                dq_ref[rs, cs] += _bdot_raw(dsb, kj, _NN)
                dfc_ref[0, rs, h:h + 1] += jnp.sum(ds, axis=1, keepdims=True)
                return (dk + _bdot_raw(dsb, qi, _TN), dv + _bdot_raw(p, doi, _TN),
                        dfs - jnp.sum(ds, axis=0, keepdims=True))

            init = (jnp.zeros((blk, HEAD), F32), jnp.zeros((blk, HEAD), F32), jnp.zeros((1, blk), F32))
            carry = tile(j, init, True)
            dk, dv, dfs = lax.fori_loop(j + 1, nq, lambda i, c: tile(i, c, False), carry)
            dk_ref[:, cs] = dk
            dv_ref[:, cs] = dv.astype(dv_ref.dtype)
            dfr_ref[0, 0, h:h + 1, :] = dfs

    by_group = lambda a: a.reshape(t, ATTN_BWD_GROUPS, hpg).transpose(1, 0, 2)
    once = pl.Buffered(1)
    stat = pl.BlockSpec((1, t, hpg), lambda g, j: (g, 0, 0), pipeline_mode=once)
    kv_blk = pl.BlockSpec((blk, gw), lambda g, j: (j, g))
    frow = pl.BlockSpec((1, 1, hpg, blk), lambda g, j: (g, j, 0, 0))
    dq, dk, dv, dfc, dfr = pl.pallas_call(
        body, name="fox_attn_bwd",
        grid=(ATTN_BWD_GROUPS, nq),
        in_specs=[pl.BlockSpec((t, 2 * gw), lambda g, j: (0, g), pipeline_mode=once),
                  pl.BlockSpec((t, gw), lambda g, j: (0, g), pipeline_mode=once),
                  pl.BlockSpec((blk, 2 * gw), lambda g, j: (j, g)), kv_blk, stat, stat, stat],
        out_specs=[pl.BlockSpec((t, gw), lambda g, j: (0, g)), kv_blk, kv_blk,
                   pl.BlockSpec((1, t, hpg), lambda g, j: (g, 0, 0)), frow],
        out_shape=[jax.ShapeDtypeStruct((t, width), F32), jax.ShapeDtypeStruct((t, width), F32),
                   jax.ShapeDtypeStruct((t, width), BF16), jax.ShapeDtypeStruct((ATTN_BWD_GROUPS, t, hpg), F32),
                   jax.ShapeDtypeStruct((ATTN_BWD_GROUPS, nq, hpg, blk), F32)],
        compiler_params=_cparams(dimension_semantics=("parallel", "arbitrary")),
    )(q, do, k, v, by_group(f_col), by_group(lse), by_group(delta))
    return (dq, dk, dv, dfc.transpose(1, 0, 2).reshape(t, nh),
            dfr.transpose(1, 0, 2, 3).reshape(nq, nh, blk))


SUBLANES = 8


def _shift_down(u, n):
    r = pltpu.roll(u, n, 0)
    row = lax.broadcasted_iota(jnp.int32, (SUBLANES, u.shape[1]), 0)
    return jnp.concatenate([jnp.where(row < n, 0.0, r[:SUBLANES]), r[SUBLANES:]], axis=0)


def _shift_up(u, n):
    t = u.shape[0]
    r = pltpu.roll(u, t - n, 0)
    row = lax.broadcasted_iota(jnp.int32, (SUBLANES, u.shape[1]), 0)
    return jnp.concatenate([r[:t - SUBLANES], jnp.where(row >= SUBLANES - n, 0.0, r[t - SUBLANES:])], axis=0)


def _convglu_specs(t):
    return [pl.BlockSpec((2, t, LANES), lambda j: (0, 0, j)),
            pl.BlockSpec((2, CONV_TAPS, LANES), lambda j: (0, 0, j)),
            pl.BlockSpec((2, 1, LANES), lambda j: (0, 0, j))]


def _convglu_fwd(u, cw, cb):
    _, t, fp = u.shape

    def body(u_ref, w_ref, b_ref, a_ref, c_ref):
        c = []
        for hf in range(2):
            uv, w = u_ref[hf].astype(F32), w_ref[hf]
            c.append(w[0:1] * _shift_down(uv, 2) + w[1:2] * _shift_down(uv, 1) + w[2:3] * uv + b_ref[hf])
            c_ref[hf] = c[hf].astype(c_ref.dtype)
        a_ref[...] = (_silu(c[0]) * c[1]).astype(a_ref.dtype)

    return pl.pallas_call(
        body, name="convglu_fwd",
        grid=(fp // LANES,),
        in_specs=_convglu_specs(t),
        out_specs=[pl.BlockSpec((t, LANES), lambda j: (0, j)), pl.BlockSpec((2, t, LANES), lambda j: (0, 0, j))],
        out_shape=[jax.ShapeDtypeStruct((t, fp), BF16), jax.ShapeDtypeStruct((2, t, fp), BF16)],
        compiler_params=_cparams(dimension_semantics=("parallel",)),
    )(u, cw, cb)


def _convglu_bwd(u, c, cw, da):
    _, t, fp = u.shape

    def body(u_ref, c_ref, w_ref, da_ref, du_ref, dw_ref, db_ref):
        gc, vc = c_ref[0].astype(F32), c_ref[1].astype(F32)
        sg = _sigmoid(gc)
        dav = da_ref[...].astype(F32)
        dcs = [dav * vc * (sg * (1.0 + gc * (1.0 - sg))), dav * (gc * sg)]
        for hf in range(2):
            dc, w, uv = dcs[hf], w_ref[hf], u_ref[hf].astype(F32)
            dc1, dc2 = _shift_up(dc, 1), _shift_up(dc, 2)
            du_ref[hf] = (w[2:3] * dc + w[1:2] * dc1 + w[0:1] * dc2).astype(du_ref.dtype)
            dw_ref[hf, 0:1, :] = jnp.sum(dc2 * uv, axis=0, keepdims=True)
            dw_ref[hf, 1:2, :] = jnp.sum(dc1 * uv, axis=0, keepdims=True)
            dw_ref[hf, 2:3, :] = jnp.sum(dc * uv, axis=0, keepdims=True)
            db_ref[hf] = jnp.sum(dc, axis=0, keepdims=True)

    pair, taps, bias = _convglu_specs(t)
    return pl.pallas_call(
        body, name="convglu_bwd",
        grid=(fp // LANES,),
        in_specs=[pair, pair, taps, pl.BlockSpec((t, LANES), lambda j: (0, j))],
        out_specs=[pair, taps, bias],
        out_shape=[jax.ShapeDtypeStruct((2, t, fp), BF16), jax.ShapeDtypeStruct((2, CONV_TAPS, fp), F32),
                   jax.ShapeDtypeStruct((2, 1, fp), F32)],
        compiler_params=_cparams(dimension_semantics=("parallel",)),
    )(u, c, cw, da)


def _local_step(x, target, mods, lb, small, pre_w, get_w, put_g, *, tb=512, attn_blk=512):
    t, d = x.shape
    nh = d // HEAD
    nb = NDEV
    wts = {}
    vec = lambda *names: [mods[n] for n in names]

    def ffn_fwd(h2, l):
        u = _mm_wblk(h2, wts[f"up{l}"], BF16, f"ffn{l}_up", gb=nb // 2, split=2, tm=512)
        a, c = _convglu_fwd(u, small[f"conv_w{l}"], small[f"conv_b{l}"])
        f = _mm(a, wts[f"down{l}"], "nn", F32, f"ffn{l}_down", tk=4096)
        return (u, c), a, f

    def ffn_bwd(df, h2, uc, a, l):
        u, c = uc
        da = _mm(df, wts[f"down{l}"], "nt", BF16, f"ffn{l}_down_dx", tn=1536)
        dwd = _mm(a, df, "tn", BF16, f"ffn{l}_down_dw", tm=768, tk=t)
        du, dcw, dcb = _convglu_bwd(u, c, small[f"conv_w{l}"], da)
        dh2 = _mm_wblk_dx(du, wts[f"up{l}"], BF16, f"ffn{l}_up_dx", k=d, gb=nb // 2, split=2, tm=1024)
        dwu = _mm_wblk_dw(h2, du, f"ffn{l}_up_dw", nb=nb, gb=1, split=2, tk=t)
        return dh2, dwu, dwd, dcw, dcb

    (h_a,) = _row_fwd(_f_mod, [(x, d, 0)], vec("sh1_0", "sc1_0"), [BF16], tb=tb, name="l0_mod1")
    wts.update(get_w("l0a", h_a))
    proj_a = _mm_wblk(h_a, wts["a_in"], F32, "a_in", gb=nb // 2)
    lb = lb + pre_w("l0b", proj_a)
    ypre, states = _hgrn2_fwd(proj_a, lb, small["a_norm_g"], tb)
    wts.update(get_w("l0b", ypre))
    y_a = _mm(ypre, wts["a_out"], "nn", F32, "a_out")
    x1, h2_0 = _row_fwd(_f_res_mod, [(x, d, 0), (y_a, d, 0)], vec("g1_0", "sh2_0", "sc2_0"), [F32, BF16],
                        tb=tb, name="l0_res_mod2")
    u0, a0, f0 = ffn_fwd(h2_0, 0)
    x2, h_kv, h_q = _row_fwd(_f_res_mod2, [(x1, d, 0), (f0, d, 0)],
                             [mods["g2_0"] + pre_w("l1", f0)] + vec("kv_sh", "kv_sc", "sh1_1", "sc1_1"),
                             [F32, BF16, BF16], tb=tb, name="l0_res_kvmod_qmod")
    wts.update(get_w("l1", h_kv))
    proj_kv = _mm(h_kv, wts["kv"], "nt", F32, "kv_proj")
    proj_f = _mm(h_kv, wts["kv_f"], "nt", F32, "kv_fproj")
    v_b = proj_kv[:, d:].astype(BF16)
    f_logit_t = proj_f[:, :nh].T
    f_bias = small["kv_b_f"].reshape(nh, 1)
    f_col = _fgate_fwd(f_logit_t, f_bias).T
    (k_n,) = _row_fwd(_f_knorm_aug, [(proj_kv, HEAD, 0)] + [(piece, 1, 0) for piece in _split3(-f_col)],
                      [small["k_norm_g"]], [BF16], nsub=nh, tb=tb, name="k_norm")
    proj_q = _mm_wblk(h_q, wts["b_q"], F32, "b_q", gb=nb)
    (q_n,) = _row_fwd(_f_qnorm_aug, [(proj_q, HEAD, 0)], [small["q_norm_g"]], [BF16], nsub=nh, tb=tb,
                      name="q_norm")
    o_att, lse = _attn_fwd(q_n, k_n, v_b, f_col, attn_blk)
    (z,) = _row_fwd(_f_outgate, [(o_att, HEAD, 0), (proj_q, HEAD, 1)], [], [BF16], nsub=nh, tb=tb, name="out_gate")
    y_b = _mm(z, wts["b_out"], "nn", F32, "b_out")
    x3, h2_1 = _row_fwd(_f_res_mod, [(x2, d, 0), (y_b, d, 0)], vec("g1_1", "sh2_1", "sc2_1"), [F32, BF16],
                        tb=tb, name="l1_res_mod2")
    u1, a1, f1 = ffn_fwd(h2_1, 1)
    loss, dx4, df1, dg2_1 = _loss_call(x3, f1, mods["g2_1"], target, tb)

    g = {}
    dmods = {"g2_1": dg2_1}
    dh2, g["up1"], g["down1"], g["conv_w1"], g["conv_b1"] = ffn_bwd(df1, h2_1, u1, a1, 1)
    (dx2, dy_b), (dmods["g1_1"], dmods["sh2_1"], dmods["sc2_1"]) = _row_bwd(
        _f_res_mod, [(x2, d, 0), (y_b, d, 0)], vec("g1_1", "sh2_1", "sc2_1"),
        [(dx4, d, 0), (dh2, d, 0)], [F32, BF16], tb=tb, name="l1_res_mod2_bwd")
    dz = _mm(dy_b, wts["b_out"], "nt", BF16, "b_out_dx")
    g["b_out"] = _mm(z, dy_b, "tn", BF16, "b_out_dw", tk=t)
    (do_att, dog), _ = _row_bwd(_f_outgate, [(o_att, HEAD, 0), (proj_q, HEAD, 1)], [], [(dz, HEAD, 0)],
                                [BF16, BF16], nsub=nh, tb=tb, name="out_gate_bwd")
    delta = _attn_delta(do_att, o_att, tb)
    dq_n, dk_n, dv, dfc_q, dfr_k = _attn_bwd(q_n, k_n, v_b, f_col, do_att, lse, delta, attn_blk)
    (dpq,), (g["q_norm_g"],) = _row_bwd(_f_qnorm, [(proj_q, HEAD, 0)], [small["q_norm_g"]],
                                        [(dq_n, HEAD, 0)], [BF16], nsub=nh, tb=tb, name="q_norm_bwd")
    dproj_q = jnp.concatenate([dpq, dog], axis=1)
    dh_q = _mm_wblk_dx(dproj_q, wts["b_q"], BF16, "b_q_dx", k=d, gb=nb)
    g["b_q"] = _mm_wblk_dw(h_q, dproj_q, "b_q_dw", nb=nb, gb=nb // 4, tk=t)
    (dpk,), (g["k_norm_g"],) = _row_bwd(_f_knorm, [(proj_kv, HEAD, 0)], [small["k_norm_g"]],
                                        [(dk_n, HEAD, 0)], [BF16], nsub=nh, tb=tb, name="k_norm_bwd")
    dproj_kv = jnp.concatenate([dpk, dv], axis=1)
    df_t = dfc_q.T + dfr_k.transpose(1, 0, 2).reshape(nh, t)
    dflogit_t, g["kv_b_f"] = _fgate_bwd(f_logit_t, f_bias, df_t)
    dproj_f = jnp.pad(dflogit_t.T, ((0, 0), (0, LANES - nh))).astype(BF16)
    dh_kv = _mm(dproj_kv, wts["kv"], "nn", BF16, "kv_proj_dx")
    dh_kv_f = _mm(dproj_f, wts["kv_f"], "nn", BF16, "kv_fproj_dx")
    g["kv"] = _mm(dproj_kv, h_kv, "tn", BF16, "kv_proj_dw", tk=t)
    g["kv_f"] = _mm(dproj_f, h_kv, "tn", F32, "kv_fproj_dw", tk=1024)
    sent = put_g("l1", {n: g.pop(n) for n in ("b_out", "b_q", "kv", "kv_f", "up1", "down1")})
    (dx1, df0), (dmods["g2_0"], dmods["kv_sh"], dmods["kv_sc"], dmods["sh1_1"], dmods["sc1_1"]) = _row_bwd(
        _f_res_mod2, [(x1, d, 0), (f0, d, 0)], [mods["g2_0"] + sent] + vec("kv_sh", "kv_sc", "sh1_1", "sc1_1"),
        [(dx2, d, 0), (dh_kv, d, 0), (dh_q, d, 0)], [F32, BF16], tb=tb, name="l0_res_kvmod_qmod_bwd",
        cot_add=(1, dh_kv_f))
    dh2, g["up0"], g["down0"], g["conv_w0"], g["conv_b0"] = ffn_bwd(df0, h2_0, u0, a0, 0)
    (dx0, dy_a), (dmods["g1_0"], dmods["sh2_0"], dmods["sc2_0"]) = _row_bwd(
        _f_res_mod, [(x, d, 0), (y_a, d, 0)], vec("g1_0", "sh2_0", "sc2_0"),
        [(dx1, d, 0), (dh2, d, 0)], [F32, BF16], tb=tb, name="l0_res_mod2_bwd")
    dypre = _mm(dy_a, wts["a_out"], "nt", BF16, "a_out_dx")
    g["a_out"] = _mm(ypre, dy_a, "tn", BF16, "a_out_dw", tk=t)
    sent = put_g("l0b", {n: g.pop(n) for n in ("a_out", "up0", "down0")})
    dproj_a, dlb, g["a_norm_g"] = _hgrn2_bwd(proj_a, lb + sent, small["a_norm_g"], states, dypre, tb)
    dh_a = _mm_wblk_dx(dproj_a, wts["a_in"], BF16, "a_in_dx", k=d, gb=nb, split=4, tm=512)
    put_g("l0a", {"a_in": _mm_wblk_dw(h_a, dproj_a, "a_in_dw", nb=nb, gb=1, split=4, tk=t)})
    (grad_x,), (dmods["sh1_0"], dmods["sc1_0"]) = _row_bwd(
        _f_mod, [(x, d, 0)], vec("sh1_0", "sc1_0"), [(dh_a, d, 0)], [F32], tb=tb, name="l0_mod1_bwd",
        add_to=(0, dx0))
    return loss, grad_x, dmods, dlb, g


def _position():
    return lax.axis_index("x"), lax.axis_index("y"), lax.axis_index("c")


def _hbm_specs(n):
    return [pl.BlockSpec(memory_space=pl.ANY)] * n


def _all_gather(arrs, name):
    n = len(arrs)

    def body(*refs):
        x_refs, out_refs = refs[:n], refs[n:2 * n]
        send_sems, recv_sems, local_sems = refs[2 * n:]
        x, y, cc = _position()
        me, sibling = (x, y, cc), (x, y, 1 - cc)
        chips = [(1 - x, y), (x, 1 - y), (1 - x, 1 - y)]

        def copy(a, k, block, to, src=None):
            slot = out_refs[a].at[4 * block[0] + 2 * block[1] + block[2]]
            return pltpu.make_async_remote_copy(
                src_ref=slot if src is None else src, dst_ref=slot,
                send_sem=send_sems.at[7 * a + k], recv_sem=recv_sems.at[7 * a + k],
                device_id=to, device_id_type=_MESH)

        local = [pltpu.make_async_copy(x_refs[a], out_refs[a].at[4 * x + 2 * y + cc], local_sems.at[a])
                 for a in range(n)]
        for cp in local:
            cp.start()
        first = []
        for a in range(n):
            first.append(copy(a, 0, me, sibling, src=x_refs[a]))
            first += [copy(a, 1 + j, me, (*chip, cc), src=x_refs[a]) for j, chip in enumerate(chips)]
        for cp in first:
            cp.start()
        passed = []
        for j, chip in enumerate(chips):
            for a in range(n):
                copy(a, 1 + j, (*chip, cc), me).wait_recv()
                fwd = copy(a, 4 + j, (*chip, cc), sibling)
                fwd.start()
                passed.append(fwd)
        for a in range(n):
            copy(a, 0, sibling, me).wait_recv()
        for j, chip in enumerate(chips):
            for a in range(n):
                copy(a, 4 + j, (*chip, 1 - cc), me).wait_recv()
        for cp in first + passed:
            cp.wait_send()
        for cp in local:
            cp.wait()

    return pl.pallas_call(
        body, name=name,
        out_shape=[jax.ShapeDtypeStruct((NDEV, *a.shape), a.dtype) for a in arrs],
        in_specs=_hbm_specs(n), out_specs=_hbm_specs(n),
        scratch_shapes=[pltpu.SemaphoreType.DMA((7 * n,)), pltpu.SemaphoreType.DMA((7 * n,)),
                        pltpu.SemaphoreType.DMA((n,))],
    )(*arrs)


_XCHG_EFFECT = pltpu.SideEffectType.DATAFLOW_SIDE_EFFECTING
ALL_PEERS = (1, 2, 3, 4, 5, 6, 7)
SAME_CORE = (2, 4, 6)


def _xchg_copies(src_refs, land_refs, send_sems, recv_sems, local_sems, scatter, rels):
    x, y, cc = _position()
    me = 4 * x + 2 * y + cc
    remote, local = [], []
    for a, (src, land) in enumerate(zip(src_refs, land_refs)):
        local.append(pltpu.make_async_copy(src.at[me] if scatter else src, land.at[me], local_sems.at[a]))
        for idx, rel in enumerate(rels):
            px = 1 - x if rel & 4 else x
            py = 1 - y if rel & 2 else y
            pc = 1 - cc if rel & 1 else cc
            k = len(rels) * a + idx
            remote.append(pltpu.make_async_remote_copy(
                src_ref=src.at[4 * px + 2 * py + pc] if scatter else src, dst_ref=land.at[me],
                send_sem=send_sems.at[k], recv_sem=recv_sems.at[k], device_id=(px, py, pc), device_id_type=_MESH))
    return remote, local


def _xchg_start(srcs, scatter, rels, after, name):
    n = len(srcs)
    lands = [lax.empty(s.shape if scatter else (NDEV, *s.shape), s.dtype) for s in srcs]

    def body(*refs):
        remote, local = _xchg_copies(refs[:n], refs[n:2 * n], *refs[2 * n + 1:2 * n + 4], scatter, rels)
        for cp in local + remote:
            cp.start()
        token = refs[-1]
        token[...] = jnp.zeros_like(token)

    hbm = pl.BlockSpec(memory_space=pltpu.HBM)
    sem = pl.BlockSpec(memory_space=pltpu.SEMAPHORE)
    out = pl.pallas_call(
        body, name=name,
        out_shape=(pltpu.SemaphoreType.DMA((len(rels) * n,)), pltpu.SemaphoreType.DMA((len(rels) * n,)),
                   pltpu.SemaphoreType.DMA((n,)),
                   *[pltpu.HBM(a.shape, a.dtype) for a in srcs + lands], jax.ShapeDtypeStruct((8, LANES), F32)),
        in_specs=[hbm] * (2 * n) + [pl.BlockSpec(memory_space=pl.ANY)],
        out_specs=(sem, sem, sem, *[hbm] * (2 * n), pl.BlockSpec(memory_space=pltpu.VMEM)),
        input_output_aliases={i: 3 + i for i in range(2 * n)},
        compiler_params=pltpu.CompilerParams(has_side_effects=_XCHG_EFFECT),
    )(*[pltpu.with_memory_space_constraint(a, pltpu.HBM) for a in srcs + lands], after)
    return out[:-1], out[-1][0, 0]


def _xchg_wait(handles, after, scatter, rels, name):
    n = (len(handles) - 3) // 2

    def body(*refs):
        remote, local = _xchg_copies(refs[:n], refs[n:2 * n], *refs[2 * n:2 * n + 3], scatter, rels)
        for cp in remote:
            cp.wait_send()
            cp.wait_recv()
        for cp in local:
            cp.wait()

    hbm = pl.BlockSpec(memory_space=pltpu.HBM)
    sem = pl.BlockSpec(memory_space=pltpu.SEMAPHORE)
    thru = list(handles[3:])
    out = pl.pallas_call(
        body, name=name,
        out_shape=tuple(pltpu.HBM(a.shape, a.dtype) for a in thru),
        in_specs=[hbm] * (2 * n) + [sem, sem, sem, pl.BlockSpec(memory_space=pl.ANY)],
        out_specs=tuple([hbm] * (2 * n)),
        input_output_aliases={i: i for i in range(2 * n)},
        compiler_params=pltpu.CompilerParams(has_side_effects=_XCHG_EFFECT),
    )(*thru, *handles[:3], after)
    return list(out[n:])


def _sibling_copies(land_refs, send_sems, recv_sems):
    x, y, cc = _position()

    def copy(a, q, core):
        slot = land_refs[a].at[2 * q + core]
        return pltpu.make_async_remote_copy(
            src_ref=slot, dst_ref=slot, send_sem=send_sems.at[NCHIP * a + q], recv_sem=recv_sems.at[NCHIP * a + q],
            device_id=(x, y, 1 - cc), device_id_type=_MESH)

    pairs = [(a, q) for a in range(len(land_refs)) for q in range(NCHIP)]
    return [copy(a, q, cc) for a, q in pairs], [copy(a, q, 1 - cc) for a, q in pairs]


def _sibling_forward_start(lands, name):
    n = len(lands)

    def body(*refs):
        sends, _ = _sibling_copies(refs[:n], refs[n], refs[n + 1])
        for cp in sends:
            cp.start()
        refs[-1][...] = jnp.zeros_like(refs[-1])

    hbm = pl.BlockSpec(memory_space=pltpu.HBM)
    sem = pl.BlockSpec(memory_space=pltpu.SEMAPHORE)
    out = pl.pallas_call(
        body, name=name,
        out_shape=(pltpu.SemaphoreType.DMA((NCHIP * n,)), pltpu.SemaphoreType.DMA((NCHIP * n,)),
                   *[pltpu.HBM(a.shape, a.dtype) for a in lands], jax.ShapeDtypeStruct((8, LANES), F32)),
        in_specs=[hbm] * n,
        out_specs=(sem, sem, *[hbm] * n, pl.BlockSpec(memory_space=pltpu.VMEM)),
        input_output_aliases={i: 2 + i for i in range(n)},
        compiler_params=pltpu.CompilerParams(has_side_effects=_XCHG_EFFECT),
    )(*lands)
    return out[:-1], out[-1][0, 0]


def _sibling_forward_wait(handles, after, name):
    n = len(handles) - 2

    def body(*refs):
        sends, arrivals = _sibling_copies(refs[:n], refs[n], refs[n + 1])
        for cp in sends:
            cp.wait_send()
        for cp in arrivals:
            cp.wait_recv()

    hbm = pl.BlockSpec(memory_space=pltpu.HBM)
    sem = pl.BlockSpec(memory_space=pltpu.SEMAPHORE)
    lands = list(handles[2:])
    return list(pl.pallas_call(
        body, name=name,
        out_shape=tuple(pltpu.HBM(a.shape, a.dtype) for a in lands),
        in_specs=[hbm] * n + [sem, sem, pl.BlockSpec(memory_space=pl.ANY)],
        out_specs=tuple([hbm] * n),
        input_output_aliases={i: i for i in range(n)},
        compiler_params=pltpu.CompilerParams(has_side_effects=_XCHG_EFFECT),
    )(*lands, *handles[:2], after))


def _slab_sum(slabs, name, tr=None):
    n, r, c = slabs.shape
    tr = r if tr is None else tr

    def body(s_ref, o_ref):
        acc = s_ref[0].astype(F32)
        for q in range(1, n):
            acc = acc + s_ref[q].astype(F32)
        o_ref[...] = acc

    return pl.pallas_call(body, name=name, grid=(r // tr,),
                          in_specs=[pl.BlockSpec((n, tr, c), lambda i: (0, i, 0))],
                          out_specs=pl.BlockSpec((tr, c), lambda i: (i, 0)),
                          out_shape=jax.ShapeDtypeStruct((r, c), F32),
                          compiler_params=_cparams(dimension_semantics=("parallel",)))(slabs)


def _ada_fwd(c_all, ada_w, kv_ada_w, logits):
    rows, d = c_all.shape
    n0, nkv = ada_w.shape[2], kv_ada_w.shape[1]

    def body(c_ref, w_ref, kw_ref, lg_ref, part_ref, cact_ref, lb_ref):
        ca = _silu(c_ref[...])
        cact_ref[...] = ca
        part_ref[:, 0:n0] = _bdot_raw(ca, w_ref[0], _NN)
        part_ref[:, n0:2 * n0] = _bdot_raw(ca, w_ref[1], _NN)
        part_ref[:, 2 * n0:2 * n0 + nkv] = _bdot_raw(ca, kw_ref[...], _NN)
        lb_ref[...] = _sigmoid(lg_ref[0:1, :] - lg_ref[1:2, :])

    vm = pl.BlockSpec(memory_space=pltpu.VMEM)
    return pl.pallas_call(
        body, name="ada_fwd", in_specs=[vm, vm, vm, vm], out_specs=[vm, vm, vm],
        out_shape=[jax.ShapeDtypeStruct((rows, 2 * n0 + nkv), F32), jax.ShapeDtypeStruct((rows, d), F32),
                   jax.ShapeDtypeStruct((1, d), F32)],
        compiler_params=_cparams(),
    )(c_all, ada_w, kv_ada_w, logits)


def _ada_bwd(c_act, dm0, dm1, dkv, lb, dlb):
    rows, d = c_act.shape

    def body(c_ref, d0_ref, d1_ref, dk_ref, lb_ref, dlb_ref, dw_ref, dkw_ref, dlg_ref):
        ca = c_ref[...]
        dw_ref[0] = _bdot_raw(ca, d0_ref[...], _TN)
        dw_ref[1] = _bdot_raw(ca, d1_ref[...], _TN)
        dkw_ref[...] = _bdot_raw(ca, dk_ref[...], _TN)
        lbv = lb_ref[...]
        dl0 = dlb_ref[...] * lbv * (1.0 - lbv)
        dlg_ref[0:1, :] = dl0
        dlg_ref[1:2, :] = -dl0

    vm = pl.BlockSpec(memory_space=pltpu.VMEM)
    return pl.pallas_call(
        body, name="ada_bwd", in_specs=[vm] * 6, out_specs=[vm, vm, vm],
        out_shape=[jax.ShapeDtypeStruct((2, d, dm0.shape[1]), F32), jax.ShapeDtypeStruct((d, dkv.shape[1]), F32),
                   jax.ShapeDtypeStruct((2, d), F32)],
        compiler_params=_cparams(),
    )(c_act, dm0, dm1, dkv, lb, dlb)


def _adamw(w, g, m, v, name, tr=512, after=None):
    r, c = w.shape
    tr = _divisor_tile(r, tr, unit=8)
    c1 = 1.0 - ADAM_B1 ** ADAM_STEP
    c2 = 1.0 - ADAM_B2 ** ADAM_STEP
    deps = [] if after is None else [after]

    def body(w_ref, g_ref, m_ref, v_ref, *rest):
        d_ref, mo_ref, vo_ref = rest[len(deps):]
        gv = g_ref[...]
        mn = ADAM_B1 * m_ref[...] + (1.0 - ADAM_B1) * gv
        vn = ADAM_B2 * v_ref[...] + (1.0 - ADAM_B2) * (gv * gv)
        d_ref[...] = -ADAM_LR * ((mn / c1) / (jnp.sqrt(vn / c2) + ADAM_EPS) + ADAM_WD * w_ref[...])
        mo_ref[...] = mn
        vo_ref[...] = vn

    spec = pl.BlockSpec((tr, c), lambda i: (i, 0))
    out = jax.ShapeDtypeStruct((r, c), F32)
    return pl.pallas_call(body, name=name, grid=(r // tr,),
                          in_specs=[spec] * 4 + [pl.BlockSpec(a.shape, lambda i: (0, 0)) for a in deps],
                          out_specs=[spec] * 3, out_shape=[out, out, out],
                          compiler_params=_cparams(dimension_semantics=("parallel",)))(w, g, m, v, *deps)


def _pad_rows(a, rows):
    return jnp.pad(a, ((0, rows - a.shape[0]), (0, 0)))


def _pack_small(parts, lanes=LANES, row_unit=8):
    flat = jnp.concatenate([p.reshape(-1).astype(F32) for p in parts])
    rows = _round_up(-(-flat.shape[0] // lanes), row_unit)
    return jnp.pad(flat, (0, rows * lanes - flat.shape[0])).reshape(rows, lanes)


def _unpack_small(flat, shapes):
    out, off = [], 0
    for s in shapes:
        n = 1
        for k in s:
            n *= k
        out.append(flat[off:off + n].reshape(s))
        off += n
    return out


def _pad_shard_cols(a, n_loc, n_pad):
    lead = a.shape[:-1]
    a = a.reshape(*lead, NDEV, n_loc)
    a = jnp.pad(a, [(0, 0)] * (len(lead) + 1) + [(0, n_pad - n_loc)])
    return a.reshape(*lead, NDEV * n_pad)


def _unpad_shard_cols(a, n_loc, n_pad):
    lead = a.shape[:-1]
    return a.reshape(*lead, NDEV, n_pad)[..., :n_loc].reshape(*lead, NDEV * n_loc)


def kernel(x, c, ada_w, ada_b, a_w_in, a_lb_logits, a_norm_g, a_w_out, kv_ada_w, kv_ada_b, kv_w, kv_b_f, k_norm_g, b_w_q, q_norm_g, b_w_out, ffn_w_up, ffn_conv_w, ffn_conv_b, ffn_w_down, loss_target, m_ada_w, m_ada_b, m_a_w_in, m_a_lb_logits, m_a_norm_g, m_a_w_out, m_kv_ada_w, m_kv_ada_b, m_kv_w, m_kv_b_f, m_k_norm_g, m_b_w_q, m_q_norm_g, m_b_w_out, m_ffn_w_up, m_ffn_conv_w, m_ffn_conv_b, m_ffn_w_down, v_ada_w, v_ada_b, v_a_w_in, v_a_lb_logits, v_a_norm_g, v_a_w_out, v_kv_ada_w, v_kv_ada_b, v_kv_w, v_kv_b_f, v_k_norm_g, v_b_w_q, v_q_norm_g, v_b_w_out, v_ffn_w_up, v_ffn_conv_w, v_ffn_conv_b, v_ffn_w_down):
    t, d = x.shape[1], x.shape[2]
    nh = d // HEAD
    ncw = ffn_w_up.shape[2]
    ncp = _round_up(ncw, LANES)
    two_f = ncw * NDEV
    ff = two_f // 2
    fp = ncp * NDEV // 2
    rd = ffn_w_down.shape[1]
    me = 4 * lax.axis_index("x") + 2 * lax.axis_index("y") + lax.axis_index("c")
    weights = dict(ada_w=ada_w, ada_b=ada_b, a_w_in=a_w_in, a_lb_logits=a_lb_logits, a_norm_g=a_norm_g,
                   a_w_out=a_w_out, kv_ada_w=kv_ada_w, kv_ada_b=kv_ada_b, kv_w=kv_w, kv_b_f=kv_b_f,
                   k_norm_g=k_norm_g, b_w_q=b_w_q, q_norm_g=q_norm_g, b_w_out=b_w_out, ffn_w_up=ffn_w_up,
                   ffn_conv_w=ffn_conv_w, ffn_conv_b=ffn_conv_b, ffn_w_down=ffn_w_down)
    m_in = dict(ada_w=m_ada_w, ada_b=m_ada_b, a_w_in=m_a_w_in, a_lb_logits=m_a_lb_logits, a_norm_g=m_a_norm_g,
                a_w_out=m_a_w_out, kv_ada_w=m_kv_ada_w, kv_ada_b=m_kv_ada_b, kv_w=m_kv_w, kv_b_f=m_kv_b_f,
                k_norm_g=m_k_norm_g, b_w_q=m_b_w_q, q_norm_g=m_q_norm_g, b_w_out=m_b_w_out, ffn_w_up=m_ffn_w_up,
                ffn_conv_w=m_ffn_conv_w, ffn_conv_b=m_ffn_conv_b, ffn_w_down=m_ffn_w_down)
    v_in = dict(ada_w=v_ada_w, ada_b=v_ada_b, a_w_in=v_a_w_in, a_lb_logits=v_a_lb_logits, a_norm_g=v_a_norm_g,
                a_w_out=v_a_w_out, kv_ada_w=v_kv_ada_w, kv_ada_b=v_kv_ada_b, kv_w=v_kv_w, kv_b_f=v_kv_b_f,
                k_norm_g=v_k_norm_g, b_w_q=v_b_w_q, q_norm_g=v_q_norm_g, b_w_out=v_b_w_out, ffn_w_up=v_ffn_w_up,
                ffn_conv_w=v_ffn_conv_w, ffn_conv_b=v_ffn_conv_b, ffn_w_down=v_ffn_w_down)
    order = list(weights)

    up_loc = jnp.pad(ffn_w_up, ((0, 0), (0, 0), (0, ncp - ncw))).astype(BF16)
    down_loc = ffn_w_down.astype(BF16)
    gather_names = {"l0b": ["a_out", "up0", "down0"], "l1": ["kv", "b_q", "b_out", "up1", "down1"]}
    shards = {"a_out": a_w_out[0].astype(BF16), "up0": up_loc[0], "down0": down_loc[0], "kv": kv_w.T.astype(BF16),
              "b_q": b_w_q[0].astype(BF16), "b_out": b_w_out[0].astype(BF16), "up1": up_loc[1],
              "down1": down_loc[1]}
    pre = _pack_small([c, a_lb_logits, ffn_conv_w])
    a_in_all, pre_all = _all_gather([a_w_in[0].astype(BF16), pre], "gather_a_w_in_and_small_inputs")
    pre_all = pre_all.reshape(NDEV, -1)
    c_all = pre_all[:, :d]
    logits = pre_all[:, d:d + 2 * HEAD].reshape(NDEV, 2, HEAD).transpose(1, 0, 2).reshape(2, d)
    conv_w_full = pre_all[:, d + 2 * HEAD:d + 2 * HEAD + 2 * CONV_TAPS * ncw]
    conv_w_full = conv_w_full.reshape(NDEV, 2, CONV_TAPS, ncw).transpose(1, 2, 0, 3).reshape(2, CONV_TAPS, two_f)

    part, c_act, lb = _ada_fwd(_pad_rows(c_all, 2 * NDEV), ada_w, kv_ada_w, logits)
    (part_all,) = _all_gather([part[:NDEV]], "gather_adaln")
    mine = lax.dynamic_index_in_dim(part_all, me, axis=1, keepdims=False)
    n0, nkv = ada_w.shape[2], kv_ada_w.shape[1]
    mod_names = ["sh1", "sc1", "g1", "sh2", "sc2", "g2"]
    mods = {}
    for l in range(2):
        row = mine[:, l * n0:(l + 1) * n0].reshape(-1) + ada_b[l]
        for k, nm in enumerate(mod_names):
            mods[f"{nm}_{l}"] = row[k * d:(k + 1) * d].reshape(1, d)
    kvrow = mine[:, 2 * n0:2 * n0 + nkv].reshape(-1) + kv_ada_b
    mods["kv_sh"], mods["kv_sc"] = kvrow[:d].reshape(1, d), kvrow[d:].reshape(1, d)

    in_flight = {}

    def start_gather(grp, dep):
        srcs = [shards[n] for n in gather_names[grp]]
        in_flight[grp], started = _xchg_start(srcs, False, SAME_CORE, dep, f"gather_{grp}_start")
        return started

    zero = start_gather("l0b", part_all)
    mods["sh1_0"] = mods["sh1_0"] + zero

    small = {"a_norm_g": a_norm_g, "k_norm_g": k_norm_g.reshape(1, HEAD), "q_norm_g": q_norm_g, "kv_b_f": kv_b_f}
    for l in range(2):
        small[f"conv_w{l}"] = _pad_shard_cols(conv_w_full[l], ncw, ncp).reshape(CONV_TAPS, 2, fp).transpose(1, 0, 2)
        small[f"conv_b{l}"] = _pad_shard_cols(ffn_conv_b[l], ncw, ncp).reshape(2, 1, fp)

    forwarding = {}

    def pre_w(grp, after):
        arrived = _xchg_wait(in_flight[grp], after, False, SAME_CORE, f"gather_{grp}_wait")
        forwarding[grp], started = _sibling_forward_start(arrived, f"gather_{grp}_to_sibling_start")
        return started

    def get_w(grp, after):
        if grp == "l0a":
            return {"a_in": a_in_all}
        full = _sibling_forward_wait(forwarding[grp], after, f"gather_{grp}_to_sibling_wait")
        if grp == "l0b":
            started = start_gather("l1", full[0])
            full[0] = full[0] + started.astype(full[0].dtype)
        got = dict(zip(gather_names[grp], full))
        out = {}
        for n, a in got.items():
            if n in ("a_out", "b_out"):
                out[n] = a.reshape(d, d)
            elif n in ("down0", "down1"):
                dn = a.reshape(NCHIP, ff // NCHIP, d)
                out[n] = jnp.pad(dn, ((0, 0), (0, ncp - ncw), (0, 0))).reshape(fp, d)
            elif n == "kv":
                kv_t = a.reshape(NDEV * kv_w.shape[1], d)
                out["kv"] = kv_t[:2 * d]
                out["kv_f"] = jnp.pad(kv_t[2 * d:], ((0, LANES - nh), (0, 0)))
            else:
                out[n] = a
        return out

    scatter_flight, g_last = {}, {}

    def put_g(grp, gr):
        if grp == "l0a":
            g_last.update(gr)
            return zero
        if grp == "l1":
            g_kvw = jnp.concatenate([gr["kv"], gr["kv_f"][:nh].astype(BF16)], axis=0)
            arrs = {"kv_w": g_kvw.reshape(NDEV, kv_w.shape[1], d), "b_w_q": gr["b_q"],
                    "b_w_out": gr["b_out"].reshape(NDEV, d // NDEV, d), "up1": gr["up1"],
                    "down1": gr["down1"].reshape(NCHIP, ncp, d)[:, :ncw].reshape(NDEV, rd, d)}
        else:
            arrs = {"a_w_out": gr["a_out"].reshape(NDEV, d // NDEV, d), "up0": gr["up0"],
                    "down0": gr["down0"].reshape(NCHIP, ncp, d)[:, :ncw].reshape(NDEV, rd, d)}
        srcs = list(arrs.values())
        handles, sent = _xchg_start(srcs, True, ALL_PEERS, srcs[0], f"scatter_{grp}_start")
        scatter_flight[grp] = (list(arrs), handles)
        return sent

    loss_v, grad_x, dmods, dlb, g = _local_step(x[0], loss_target[0], mods, lb, small, pre_w, get_w, put_g)

    g_sum = {}
    for grp in ("l1", "l0b"):
        names, handles = scatter_flight[grp]
        for nm, a in zip(names, _xchg_wait(handles, grad_x, True, ALL_PEERS, f"scatter_{grp}_wait")):
            g_sum[nm] = _slab_sum(a, f"rs_slab_sum_{nm}")

    def conv_w_grad(a):
        return _unpad_shard_cols(a.transpose(1, 0, 2).reshape(CONV_TAPS, 2 * fp), ncw, ncp)

    def conv_b_grad(a):
        return _unpad_shard_cols(a.reshape(2 * fp), ncw, ncp)

    dmod_vec = [dmods[f"{nm}_{l}"] for l in range(2) for nm in mod_names] + [dmods["kv_sh"], dmods["kv_sc"]]
    post = _pack_small(dmod_vec + [dlb, g["a_norm_g"], g["k_norm_g"], g["q_norm_g"],
                                   jnp.pad(g["kv_b_f"].reshape(-1), (0, LANES - nh)),
                                   conv_w_grad(g["conv_w0"]), conv_w_grad(g["conv_w1"]),
                                   conv_b_grad(g["conv_b0"]), conv_b_grad(g["conv_b1"]), loss_v])
    (post_all,) = _all_gather([post], "gather_small_grads")
    a_in_flight, a_in_sent = _xchg_start([g_last["a_in"]], True, ALL_PEERS, post_all, "scatter_l0a_start")
    a_in_sent = a_in_sent.reshape(1, 1)
    tot = _slab_sum(post_all, "small_grad_sum").reshape(-1)
    nmod = 14 * d
    (t_mod, t_lb, t_ang, t_kng, t_qng, t_bf, t_cw, t_cb, t_loss) = _unpack_small(
        tot, [(nmod,), (1, d), (1, HEAD), (HEAD,), (1, HEAD), (LANES,), (2, CONV_TAPS, two_f), (2, two_f),
              (LANES,)])
    loss = t_loss[0]
    dm_all = post_all.reshape(NDEV, -1)[:, :nmod]
    dm0 = lax.dynamic_slice_in_dim(dm_all[:, :6 * d], me * n0, n0, axis=1)
    dm1 = lax.dynamic_slice_in_dim(dm_all[:, 6 * d:12 * d], me * n0, n0, axis=1)
    dkv = lax.dynamic_slice_in_dim(dm_all[:, 12 * d:], me * nkv, nkv, axis=1)
    g_ada_w, g_kv_ada_w, g_logits = _ada_bwd(c_act, _pad_rows(dm0, 2 * NDEV), _pad_rows(dm1, 2 * NDEV),
                                              _pad_rows(dkv, 2 * NDEV), lb, t_lb)

    grads = {
        "ada_w": g_ada_w,
        "ada_b": t_mod[:12 * d].reshape(2, 6 * d),
        "a_lb_logits": lax.dynamic_slice_in_dim(g_logits, me * HEAD, HEAD, axis=1),
        "a_norm_g": t_ang,
        "a_w_out": g_sum["a_w_out"].reshape(a_w_out.shape),
        "kv_ada_w": g_kv_ada_w,
        "kv_ada_b": t_mod[12 * d:],
        "kv_w": g_sum["kv_w"].T,
        "kv_b_f": t_bf[:nh],
        "k_norm_g": t_kng,
        "b_w_q": g_sum["b_w_q"].reshape(b_w_q.shape),
        "q_norm_g": t_qng,
        "b_w_out": g_sum["b_w_out"].reshape(b_w_out.shape),
        "ffn_w_up": jnp.stack([g_sum["up0"][:, :ncw], g_sum["up1"][:, :ncw]]),
        "ffn_conv_w": lax.dynamic_slice_in_dim(t_cw, me * ncw, ncw, axis=2),
        "ffn_conv_b": t_cb,
        "ffn_w_down": jnp.stack([g_sum["down0"], g_sum["down1"]]),
    }

    big_adam = ["ada_w", "a_w_out", "kv_ada_w", "kv_w", "b_w_q", "b_w_out", "ffn_w_up", "ffn_w_down", "a_w_in"]
    small_adam = [n for n in order if n not in big_adam]
    delta, new_m, new_v = {}, {}, {}
    packs = [_pack_small([src[n] for n in small_adam]) for src in (weights, grads, m_in, v_in)]
    outs = _adamw(*packs, "adamw_small", tr=packs[0].shape[0])
    shapes = [weights[n].shape for n in small_adam]
    for dst, o in zip((delta, new_m, new_v), outs):
        for n, a in zip(small_adam, _unpack_small(o.reshape(-1), shapes)):
            dst[n] = a
    for n in big_adam:
        if n == "a_w_in":
            (landed,) = _xchg_wait(a_in_flight, new_v["ffn_w_down"], True, ALL_PEERS, "scatter_l0a_wait")
            grads[n] = _slab_sum(landed, "rs_slab_sum_a_w_in").reshape(a_w_in.shape)
        shp = weights[n].shape
        two_d = lambda a: a.reshape(-1, shp[-1])
        dl, mn, vn = _adamw(two_d(weights[n]), two_d(grads[n]), two_d(m_in[n]), two_d(v_in[n]), f"adamw_{n}",
                            after=a_in_sent)
        delta[n], new_m[n], new_v[n] = dl.reshape(shp), mn.reshape(shp), vn.reshape(shp)

    return (loss, grad_x.reshape(x.shape), *[grads[n] for n in order], *[delta[n] for n in order],
            *[new_m[n] for n in order], *[new_v[n] for n in order])
```

```python
import functools

import jax
import jax.numpy as jnp
from jax import lax
from jax.experimental import pallas as pl
from jax.experimental.pallas import tpu as pltpu

F32 = jnp.float32
BF16 = jnp.bfloat16

NDEV = 8
NCHIP = 4
HEAD = 128
A_CHUNK = 64
CONV_TAPS = 3
EPS = 1e-6
NEG_INF = -1e30
LANES = 128
VMEM_LIMIT = 48 * 1024 * 1024

ADAM_LR = 0.001
ADAM_B1 = 0.9
ADAM_B2 = 0.999
ADAM_EPS = 1e-08
ADAM_WD = 0.01
ADAM_STEP = 10

_NN = (((1,), (0,)), ((), ()))
_NT = (((1,), (1,)), ((), ()))
_TN = (((0,), (0,)), ((), ()))
_MESH = pl.DeviceIdType.MESH


def _cparams(**kw):
    return pltpu.CompilerParams(vmem_limit_bytes=VMEM_LIMIT, **kw)


def _divisor_tile(n, pref, unit=LANES):
    if n <= pref:
        return n
    best = None
    for t in range(unit, pref + 1, unit):
        if n % t == 0:
            best = t
    assert best is not None, (n, pref)
    return best


def _round_up(n, unit):
    return -(-n // unit) * unit


def _bdot_raw(a, b, dims):
    return lax.dot_general(a.astype(BF16), b.astype(BF16), dims, preferred_element_type=F32)


@jax.custom_vjp
def _dot_nn(a, b):
    return _bdot_raw(a, b, _NN)


@jax.custom_vjp
def _dot_nt(a, b):
    return _bdot_raw(a, b, _NT)


@jax.custom_vjp
def _dot_tn(a, b):
    return _bdot_raw(a, b, _TN)


_dot_nn.defvjp(lambda a, b: (_bdot_raw(a, b, _NN), (a, b)),
               lambda r, g: (_dot_nt(g, r[1]), _dot_tn(r[0], g)))
_dot_nt.defvjp(lambda a, b: (_bdot_raw(a, b, _NT), (a, b)),
               lambda r, g: (_dot_nn(g, r[1]), _dot_tn(g, r[0])))
_dot_tn.defvjp(lambda a, b: (_bdot_raw(a, b, _TN), (a, b)),
               lambda r, g: (_dot_nt(r[1], g), _dot_nn(r[0], g)))


def _f32dot(a, b):
    return lax.dot_general(a, b, _NN, precision=lax.Precision.HIGHEST, preferred_element_type=F32)


def _sigmoid(x):
    return jax.nn.sigmoid(x)


def _silu(x):
    return x * jax.nn.sigmoid(x)


def _rms(x):
    return x * lax.rsqrt(jnp.mean(x * x, axis=-1, keepdims=True) + EPS)


def _modulate(x, sh, sc):
    return _rms(x) * (1.0 + sc) + sh


def _mm_call(a, b, dims, a_spec, b_spec, o_spec, o_shape, grid, acc_tile, name):
    nk = grid[2]

    def body(a_ref, b_ref, o_ref, *acc):
        p = lax.dot_general(a_ref[...].astype(BF16), b_ref[...].astype(BF16), dims,
                            preferred_element_type=F32)
        if nk == 1:
            o_ref[...] = p.astype(o_ref.dtype)
        else:
            kk = pl.program_id(2)

            @pl.when(kk == 0)
            def _():
                acc[0][...] = p

            @pl.when(kk > 0)
            def _():
                acc[0][...] += p

            @pl.when(kk == nk - 1)
            def _():
                o_ref[...] = acc[0][...].astype(o_ref.dtype)

    return pl.pallas_call(
        body, name=name, grid=grid, in_specs=[a_spec, b_spec], out_specs=o_spec, out_shape=o_shape,
        scratch_shapes=[pltpu.VMEM(acc_tile, F32)] if nk > 1 else [],
        compiler_params=_cparams(dimension_semantics=("parallel", "parallel", "arbitrary")),
    )(a, b)


def _mm(a, b, mode, out_dtype, name, tm=1024, tn=1024, tk=2048):
    if mode == "nn":
        (m, k), (k2, n) = a.shape, b.shape
    elif mode == "nt":
        (m, k), (n, k2) = a.shape, b.shape
    else:
        (k, m), (k2, n) = a.shape, b.shape
    assert k == k2, (a.shape, b.shape, mode)
    tm, tn, tk = _divisor_tile(m, tm), _divisor_tile(n, tn), _divisor_tile(k, tk)
    if mode == "tn":
        a_spec = pl.BlockSpec((tk, tm), lambda i, j, kk: (kk, i))
    else:
        a_spec = pl.BlockSpec((tm, tk), lambda i, j, kk: (i, kk))
    if mode == "nt":
        b_spec = pl.BlockSpec((tn, tk), lambda i, j, kk: (j, kk))
    else:
        b_spec = pl.BlockSpec((tk, tn), lambda i, j, kk: (kk, j))
    return _mm_call(a, b, {"nn": _NN, "nt": _NT, "tn": _TN}[mode], a_spec, b_spec,
                    pl.BlockSpec((tm, tn), lambda i, j, kk: (i, j)), jax.ShapeDtypeStruct((m, n), out_dtype),
                    (m // tm, n // tn, k // tk), (tm, tn), name)


def _wblk_act_spec(rows, gb, nl, split, nb, row_axis, blk_axis):
    if split == 1:
        return pl.BlockSpec((rows, gb * nl), lambda *g: (g[row_axis], g[blk_axis]))
    groups = nb // split // gb
    return pl.BlockSpec((None, rows, gb * nl),
                        lambda *g: (g[blk_axis] // groups, g[row_axis], g[blk_axis] % groups))


def _mm_wblk(a, wb, out_dtype, name, *, gb, row_off=0, split=1, tm=1024):
    m, k = a.shape
    nb, _, nl = wb.shape
    assert (nb // split) % gb == 0
    tm = _divisor_tile(m, tm)

    def body(a_ref, b_ref, o_ref):
        av = a_ref[...].astype(BF16)
        for s in range(gb):
            o_ref[:, s * nl:(s + 1) * nl] = lax.dot_general(
                av, b_ref[s].astype(BF16), _NN, preferred_element_type=F32).astype(o_ref.dtype)

    o_shape = (m, nb * nl) if split == 1 else (split, m, nb // split * nl)
    return pl.pallas_call(
        body, name=name, grid=(nb // gb, m // tm),
        in_specs=[pl.BlockSpec((tm, k), lambda j, i: (i, 0)),
                  pl.BlockSpec((gb, k, nl), lambda j, i: (j, row_off, 0))],
        out_specs=_wblk_act_spec(tm, gb, nl, split, nb, 1, 0),
        out_shape=jax.ShapeDtypeStruct(o_shape, out_dtype),
        compiler_params=_cparams(dimension_semantics=("parallel", "parallel")),
    )(a, wb)


def _mm_wblk_dx(dy, wb, out_dtype, name, *, k, gb, row_off=0, split=1, tm=1024):
    nb, _, nl = wb.shape
    m = dy.shape[-2]
    tm = _divisor_tile(m, tm)
    nk = nb // gb
    per = nb // split
    whole = split > 1 and gb == nb
    assert whole or per % gb == 0

    def body(a_ref, b_ref, o_ref, *acc):
        p = None
        for s in range(gb):
            a_blk = a_ref[s // per, :, (s % per) * nl:(s % per + 1) * nl] if whole else a_ref[:, s * nl:(s + 1) * nl]
            q = lax.dot_general(a_blk.astype(BF16), b_ref[s].astype(BF16), _NT, preferred_element_type=F32)
            p = q if p is None else p + q
        if nk == 1:
            o_ref[...] = p.astype(o_ref.dtype)
        else:
            kk = pl.program_id(1)

            @pl.when(kk == 0)
            def _():
                acc[0][...] = p

            @pl.when(kk > 0)
            def _():
                acc[0][...] += p

            @pl.when(kk == nk - 1)
            def _():
                o_ref[...] = acc[0][...].astype(o_ref.dtype)

    return pl.pallas_call(
        body, name=name, grid=(m // tm, nk),
        in_specs=[pl.BlockSpec((split, tm, per * nl), lambda i, kk: (0, i, 0)) if whole
                  else _wblk_act_spec(tm, gb, nl, split, nb, 0, 1),
                  pl.BlockSpec((gb, k, nl), lambda i, kk: (kk, row_off, 0))],
        out_specs=pl.BlockSpec((tm, k), lambda i, kk: (i, 0)),
        out_shape=jax.ShapeDtypeStruct((m, k), out_dtype),
        scratch_shapes=[pltpu.VMEM((tm, k), F32)] if nk > 1 else [],
        compiler_params=_cparams(dimension_semantics=("parallel", "arbitrary")),
    )(dy, wb)


def _mm_wblk_dw(x, dy, name, *, nb, gb, split=1, tk=1024):
    t, k = x.shape
    assert (nb // split) % gb == 0
    nl = dy.shape[-1] * split // nb
    tk = _divisor_tile(t, tk)
    nk = t // tk

    def body(a_ref, b_ref, o_ref, *acc):
        kk = pl.program_id(1)
        av = a_ref[...].astype(BF16)
        for s in range(gb):
            p = lax.dot_general(av, b_ref[:, s * nl:(s + 1) * nl].astype(BF16), _TN, preferred_element_type=F32)
            if nk == 1:
                o_ref[s] = p.astype(o_ref.dtype)
                continue

            @pl.when(kk == 0)
            def _():
                acc[0][s] = p

            @pl.when(kk > 0)
            def _():
                acc[0][s] += p

        if nk > 1:
            @pl.when(kk == nk - 1)
            def _():
                o_ref[...] = acc[0][...].astype(o_ref.dtype)

    return pl.pallas_call(
        body, name=name, grid=(nb // gb, nk),
        in_specs=[pl.BlockSpec((tk, k), lambda j, kk: (kk, 0)), _wblk_act_spec(tk, gb, nl, split, nb, 1, 0)],
        out_specs=pl.BlockSpec((gb, k, nl), lambda j, kk: (j, 0, 0)),
        out_shape=jax.ShapeDtypeStruct((nb, k, nl), BF16),
        scratch_shapes=[pltpu.VMEM((gb, k, nl), F32)] if nk > 1 else [],
        compiler_params=_cparams(dimension_semantics=("parallel", "arbitrary")),
    )(x, dy)


def _row_specs(rows, tb, nsub):
    return [pl.BlockSpec((tb, nsub * cw), functools.partial(lambda i, off: (i, off), off=off))
            for (_, cw, off) in rows]


def _vec_specs(params):
    return [pl.BlockSpec(p.shape, lambda i: (0, 0)) for p in params]


def _row_fwd(f, rows, params, out_dtypes, *, nsub=1, tb, name):
    t = rows[0][0].shape[0]
    tb = min(tb, t)
    n_r, n_p = len(rows), len(params)
    blk = [jax.ShapeDtypeStruct((tb, cw), F32) for (_, cw, _) in rows]
    blk += [jax.ShapeDtypeStruct(p.shape, F32) for p in params]
    out_avals = jax.eval_shape(f, *blk)

    def body(*refs):
        pv = [r[...] for r in refs[n_r:n_r + n_p]]
        for s in range(nsub):
            vals = [r[:, s * cw:(s + 1) * cw].astype(F32) for r, (_, cw, _) in zip(refs[:n_r], rows)]
            outs = f(*vals, *pv)
            for o_ref, o in zip(refs[n_r + n_p:], outs):
                w = o.shape[1]
                o_ref[:, s * w:(s + 1) * w] = o.astype(o_ref.dtype)

    return pl.pallas_call(
        body, name=name,
        grid=(t // tb,),
        in_specs=_row_specs(rows, tb, nsub) + _vec_specs(params),
        out_specs=[pl.BlockSpec((tb, nsub * av.shape[1]), lambda i: (i, 0)) for av in out_avals],
        out_shape=[jax.ShapeDtypeStruct((t, nsub * av.shape[1]), dt) for av, dt in zip(out_avals, out_dtypes)],
        compiler_params=_cparams(dimension_semantics=("parallel",)),
    )(*[r[0] for r in rows], *params)


def _row_bwd(f, rows, params, cots, row_grad_dtypes, *, nsub=1, tb, name, add_to=None, cot_add=None):
    t = rows[0][0].shape[0]
    tb = min(tb, t)
    n_r, n_p, n_c = len(rows), len(params), len(cots)
    want = [j for j in range(n_r) if row_grad_dtypes[j] is not None]
    cot_add = cot_add or []
    extra = [] if add_to is None else [(add_to[1], rows[add_to[0]][1], 0)]
    n_add_to = len(extra)
    extra += [(arr, cots[ci][1], 0) for ci, arr in cot_add]

    def body(*refs):
        i = pl.program_id(0)
        r_in, p_in = refs[:n_r], refs[n_r:n_r + n_p]
        c_in = refs[n_r + n_p:n_r + n_p + n_c]
        e_in = refs[n_r + n_p + n_c:n_r + n_p + n_c + len(extra)]
        outs = refs[n_r + n_p + n_c + len(extra):]
        pv = [r[...] for r in p_in]
        psum = [None] * n_p
        for s in range(nsub):
            vals = [r[:, s * cw:(s + 1) * cw].astype(F32) for r, (_, cw, _) in zip(r_in, rows)]
            cvals = [r[:, s * cw:(s + 1) * cw].astype(F32) for r, (_, cw, _) in zip(c_in, cots)]
            for (ci, _), e_ref in zip(cot_add, e_in[n_add_to:]):
                cw = cots[ci][1]
                cvals[ci] = cvals[ci] + e_ref[:, s * cw:(s + 1) * cw].astype(F32)
            _, vjp_fn = jax.vjp(f, *vals, *pv)
            grads = vjp_fn(tuple(cvals))
            for o_ref, jr in zip(outs[:len(want)], want):
                cw = rows[jr][1]
                gr = grads[jr]
                if add_to is not None and jr == add_to[0]:
                    gr = gr + e_in[0][:, s * cw:(s + 1) * cw]
                o_ref[:, s * cw:(s + 1) * cw] = gr.astype(o_ref.dtype)
            for jp in range(n_p):
                psum[jp] = grads[n_r + jp] if psum[jp] is None else psum[jp] + grads[n_r + jp]
        for o_ref, g in zip(outs[len(want):], psum):
            @pl.when(i == 0)
            def _():
                o_ref[...] = g

            @pl.when(i > 0)
            def _():
                o_ref[...] += g

    out_specs = [pl.BlockSpec((tb, nsub * rows[jr][1]), lambda i: (i, 0)) for jr in want]
    out_shape = [jax.ShapeDtypeStruct((t, nsub * rows[jr][1]), row_grad_dtypes[jr]) for jr in want]
    out_specs += _vec_specs(params)
    out_shape += [jax.ShapeDtypeStruct(p.shape, F32) for p in params]
    res = pl.pallas_call(
        body, name=name,
        grid=(t // tb,),
        in_specs=_row_specs(rows, tb, nsub) + _vec_specs(params) + _row_specs(cots, tb, nsub)
        + _row_specs(extra, tb, nsub),
        out_specs=out_specs, out_shape=out_shape,
        compiler_params=_cparams(dimension_semantics=("arbitrary",)),
    )(*[r[0] for r in rows], *params, *[c[0] for c in cots], *[e[0] for e in extra])
    return res[:len(want)], res[len(want):]


def _f_mod(x, sh, sc):
    return (_modulate(x, sh, sc),)


def _f_res_mod(x, y, g, sh, sc):
    x1 = x + g * y
    return x1, _modulate(x1, sh, sc)


def _f_res_mod2(x, y, g, sh_a, sc_a, sh_b, sc_b):
    x1 = x + g * y
    return x1, _modulate(x1, sh_a, sc_a), _modulate(x1, sh_b, sc_b)


def _f_qnorm(p, g):
    return (_rms(p) * g * (HEAD ** -0.5),)


def _f_knorm(p, g):
    return (_rms(p) * g,)


def _f_qnorm_aug(p, g):
    lane = lax.broadcasted_iota(jnp.int32, p.shape, 1)
    return (jnp.concatenate([_rms(p) * g * (HEAD ** -0.5), jnp.where(lane < 3, 1.0, 0.0)], axis=1),)


def _f_knorm_aug(p, c0, c1, c2, g):
    lane = lax.broadcasted_iota(jnp.int32, p.shape, 1)
    aug = jnp.where(lane == 0, c0, jnp.where(lane == 1, c1, jnp.where(lane == 2, c2, 0.0)))
    return (jnp.concatenate([_rms(p) * g, aug], axis=1),)


def _split3(a):
    round_bf16 = lambda v: lax.reduce_precision(v, exponent_bits=8, mantissa_bits=7)
    hi = round_bf16(a)
    mid = round_bf16(a - hi)
    lo = round_bf16(a - hi - mid)
    return hi.astype(BF16), mid.astype(BF16), lo.astype(BF16)


def _f_outgate(o, og):
    return (o * _sigmoid(og),)


def _loss_call(x3, f, g2, target, tb):
    t, d = x3.shape
    tb = min(tb, t)

    def body(x_ref, f_ref, g_ref, t_ref, loss_ref, dx_ref, df_ref, dg_ref):
        i = pl.program_id(0)
        fv = f_ref[...]
        g = g_ref[...]
        e = x_ref[...] + g * fv - t_ref[...]
        dx = e * (1.0 / d)
        part = 0.5 * jnp.sum(jnp.sum(e * dx, axis=1, keepdims=True), axis=0, keepdims=True)
        dx_ref[...] = dx
        df_ref[...] = (g * dx).astype(df_ref.dtype)
        dg = jnp.sum(dx * fv, axis=0, keepdims=True)

        @pl.when(i == 0)
        def _():
            loss_ref[...] = jnp.broadcast_to(part, loss_ref.shape)
            dg_ref[...] = dg

        @pl.when(i > 0)
        def _():
            loss_ref[...] += jnp.broadcast_to(part, loss_ref.shape)
            dg_ref[...] += dg

    row = pl.BlockSpec((tb, d), lambda i: (i, 0))
    vec = pl.BlockSpec((1, d), lambda i: (0, 0))
    return pl.pallas_call(
        body, name="loss_head",
        grid=(t // tb,),
        in_specs=[row, row, vec, row],
        out_specs=[pl.BlockSpec((1, LANES), lambda i: (0, 0)), row, row, vec],
        out_shape=[jax.ShapeDtypeStruct((1, LANES), F32), jax.ShapeDtypeStruct((t, d), F32),
                   jax.ShapeDtypeStruct((t, d), BF16), jax.ShapeDtypeStruct((1, d), F32)],
        compiler_params=_cparams(dimension_semantics=("arbitrary",)),
    )(x3, f, g2, target)


def _hg_mask(tb):
    br = lax.broadcasted_iota(jnp.int32, (tb, tb), 0)
    bs = lax.broadcasted_iota(jnp.int32, (tb, tb), 1)
    return jnp.logical_and(br // A_CHUNK == bs // A_CHUNK, bs <= br).astype(F32)


def _hg_consts(mask):
    c = A_CHUNK
    r = lax.broadcasted_iota(jnp.int32, (c, c), 0)
    s = lax.broadcasted_iota(jnp.int32, (c, c), 1)
    return (s <= r).astype(F32), (r <= s).astype(F32), mask > 0.5


def _chunk_apply(mat, x):
    c = mat.shape[0]
    return jnp.concatenate([_f32dot(mat, x[i * c:(i + 1) * c]) for i in range(x.shape[0] // c)], axis=0)


@jax.custom_vjp
def _chunk_cumsum(x, tri, tri_t):
    return _chunk_apply(tri, x)


_chunk_cumsum.defvjp(lambda x, tri, tri_t: (_chunk_apply(tri, x), (tri, tri_t)),
                     lambda r, g: (_chunk_apply(r[1], g), jnp.zeros_like(r[0]), jnp.zeros_like(r[1])))


def _per_chunk(a, b, dims):
    return jnp.stack([_bdot_raw(a[i], b[i], dims) for i in range(a.shape[0])])


@jax.custom_vjp
def _chunk_tn(a, b):
    return _per_chunk(a, b, _TN)


@jax.custom_vjp
def _chunk_nt(a, b):
    return _per_chunk(a, b, _NT)


@jax.custom_vjp
def _chunk_nn(a, b):
    return _per_chunk(a, b, _NN)


_chunk_tn.defvjp(lambda a, b: (_per_chunk(a, b, _TN), (a, b)),
                 lambda r, g: (_chunk_nt(r[1], g), _chunk_nn(r[0], g)))
_chunk_nt.defvjp(lambda a, b: (_per_chunk(a, b, _NT), (a, b)),
                 lambda r, g: (_chunk_nn(g, r[1]), _chunk_tn(g, r[0])))
_chunk_nn.defvjp(lambda a, b: (_per_chunk(a, b, _NN), (a, b)),
                 lambda r, g: (_chunk_nt(g, r[1]), _chunk_tn(r[0], g)))


def _scan_states(decay, m, st):
    sts = []
    for i in range(m.shape[0]):
        sts.append(st)
        st = st * decay[i] + m[i]
    return jnp.stack(sts), st


@jax.custom_vjp
def _state_scan(decay, m, st):
    return _scan_states(decay, m, st)


def _state_scan_fwd(decay, m, st):
    sts, st_out = _scan_states(decay, m, st)
    return (sts, st_out), (decay, sts)


def _state_scan_bwd(res, cts):
    decay, sts = res
    d_sts, g = cts
    d_decay, d_m = [], []
    for i in range(sts.shape[0] - 1, -1, -1):
        d_m.append(g)
        d_decay.append(jnp.sum(g * sts[i], axis=0, keepdims=True))
        g = g * decay[i] + d_sts[i]
    return jnp.stack(d_decay[::-1]), jnp.stack(d_m[::-1]), g


_state_scan.defvjp(_state_scan_fwd, _state_scan_bwd)


def _hg_block(qp, fp, ip, gp, lb, ng, st, tri, tri_t, bd_causal):
    tb = qp.shape[0]
    c = A_CHUNK
    n = tb // c
    q = _silu(qp)
    fg = lb + (1.0 - lb) * _sigmoid(fp)
    logf = jnp.log(fg)
    k = 1.0 - fg
    b3 = _chunk_cumsum(logf, tri, tri_t).reshape(n, c, HEAD)
    pos = lax.broadcasted_iota(jnp.int32, (1, c, 1), 1)
    b_mid = lax.stop_gradient(jnp.sum(jnp.where(pos == c // 2, b3, 0.0), axis=1, keepdims=True))
    b_last = jnp.sum(jnp.where(pos == c - 1, b3, 0.0), axis=1, keepdims=True)
    q3, k3, v3 = q.reshape(n, c, HEAD), k.reshape(n, c, HEAD), ip.reshape(n, c, HEAD)
    scores = _dot_nt((q3 * jnp.exp(b3 - b_mid)).reshape(tb, HEAD), (k3 * jnp.exp(b_mid - b3)).reshape(tb, HEAD))
    o_intra = _dot_nn(jnp.where(bd_causal, scores, 0.0), ip)
    states, st_new = _state_scan(jnp.exp(b_last), _chunk_tn(v3, k3 * jnp.exp(b_last - b3)), st)
    o = o_intra + _chunk_nt(q3 * jnp.exp(b3), states).reshape(tb, HEAD)
    y = _rms(o) * ng * _silu(gp)
    return y, st_new


HG_HEADS = 2


def _hg_specs(tb, nh, rev_nb=None):
    wide = HG_HEADS * HEAD
    per = nh // HG_HEADS

    def row(part):
        if rev_nb is None:
            return pl.BlockSpec((tb, wide), functools.partial(lambda h, i, off: (i, off + h), off=part * per))
        return pl.BlockSpec((tb, wide),
                            functools.partial(lambda h, i, off: (rev_nb - 1 - i, off + h), off=part * per))
    return [row(0), row(1), row(2), row(3),
            pl.BlockSpec((1, wide), lambda h, i: (0, h)), pl.BlockSpec((1, HEAD), lambda h, i: (0, 0)),
            pl.BlockSpec((tb, tb), lambda h, i: (0, 0))]


def _hgrn2_fwd(proj, lb, ng, tb):
    t = proj.shape[0]
    nh = proj.shape[1] // (4 * HEAD)
    tb = min(tb, t)
    nb = t // tb
    wide = HG_HEADS * HEAD

    def body(q_ref, f_ref, i_ref, g_ref, lb_ref, ng_ref, mask_ref, y_ref, s_ref, st_ref):
        i = pl.program_id(1)

        @pl.when(i == 0)
        def _():
            st_ref[...] = jnp.zeros_like(st_ref)

        consts = _hg_consts(mask_ref[...])
        for p in range(HG_HEADS):
            cs = slice(p * HEAD, (p + 1) * HEAD)
            st = st_ref[p]
            s_ref[p, 0] = st
            y, st_new = _hg_block(q_ref[:, cs], f_ref[:, cs], i_ref[:, cs], g_ref[:, cs], lb_ref[:, cs],
                                  ng_ref[...], st, *consts)
            y_ref[:, cs] = y.astype(y_ref.dtype)
            st_ref[p] = st_new

    return pl.pallas_call(
        body, name="hgrn2_fwd",
        grid=(nh // HG_HEADS, nb),
        in_specs=_hg_specs(tb, nh),
        out_specs=[pl.BlockSpec((tb, wide), lambda h, i: (i, h)),
                   pl.BlockSpec((HG_HEADS, 1, HEAD, HEAD), lambda h, i: (h, i, 0, 0))],
        out_shape=[jax.ShapeDtypeStruct((t, nh * HEAD), BF16),
                   jax.ShapeDtypeStruct((nh, nb, HEAD, HEAD), F32)],
        scratch_shapes=[pltpu.VMEM((HG_HEADS, HEAD, HEAD), F32)],
        compiler_params=_cparams(dimension_semantics=("parallel", "arbitrary")),
    )(proj, proj, proj, proj, lb, ng, _hg_mask(tb))


def _hgrn2_bwd(proj, lb, ng, states, dy, tb):
    t = proj.shape[0]
    nh = proj.shape[1] // (4 * HEAD)
    tb = min(tb, t)
    nb = t // tb
    wide = HG_HEADS * HEAD

    def body(q_ref, f_ref, i_ref, g_ref, lb_ref, ng_ref, mask_ref, s_ref, dy_ref,
             dp_ref, dlb_ref, dng_ref, dst_ref):
        h, i = pl.program_id(0), pl.program_id(1)
        consts = _hg_consts(mask_ref[...])

        @pl.when(i == 0)
        def _():
            dst_ref[...] = jnp.zeros_like(dst_ref)
            dlb_ref[...] = jnp.zeros_like(dlb_ref)

        @pl.when(jnp.logical_and(i == 0, h == 0))
        def _():
            dng_ref[...] = jnp.zeros_like(dng_ref)

        def fn(qp, fp, ip, gp, lbx, ngx, stx):
            return _hg_block(qp, fp, ip, gp, lbx, ngx, stx, *consts)

        for p in range(HG_HEADS):
            cs = slice(p * HEAD, (p + 1) * HEAD)
            _, vjp_fn = jax.vjp(fn, q_ref[:, cs], f_ref[:, cs], i_ref[:, cs], g_ref[:, cs], lb_ref[:, cs],
                                ng_ref[...], s_ref[p, 0])
            *gparts, glb, gng, dst = vjp_fn((dy_ref[:, cs].astype(F32), dst_ref[p]))
            for part, gpart in enumerate(gparts):
                dp_ref[part, :, cs] = gpart.astype(dp_ref.dtype)
            dst_ref[p] = dst
            dlb_ref[:, cs] += glb
            dng_ref[...] += gng

    rev = lambda h, i: (nb - 1 - i, h)
    return pl.pallas_call(
        body, name="hgrn2_bwd",
        grid=(nh // HG_HEADS, nb),
        in_specs=_hg_specs(tb, nh, rev_nb=nb) + [
            pl.BlockSpec((HG_HEADS, 1, HEAD, HEAD), lambda h, i: (h, nb - 1 - i, 0, 0)),
            pl.BlockSpec((tb, wide), rev)],
        out_specs=[pl.BlockSpec((4, tb, wide), lambda h, i: (0, nb - 1 - i, h)),
                   pl.BlockSpec((1, wide), lambda h, i: (0, h)), pl.BlockSpec((1, HEAD), lambda h, i: (0, 0))],
        out_shape=[jax.ShapeDtypeStruct((4, t, nh * HEAD), BF16),
                   jax.ShapeDtypeStruct((1, nh * HEAD), F32), jax.ShapeDtypeStruct((1, HEAD), F32)],
        scratch_shapes=[pltpu.VMEM((HG_HEADS, HEAD, HEAD), F32)],
        compiler_params=_cparams(dimension_semantics=("arbitrary", "arbitrary")),
    )(proj, proj, proj, proj, lb, ng, _hg_mask(tb), states, dy)


def _fgate_consts(cb):
    r = lax.broadcasted_iota(jnp.int32, (cb, cb), 0)
    s = lax.broadcasted_iota(jnp.int32, (cb, cb), 1)
    return (r <= s).astype(F32), (r >= s).astype(F32)


def _fgate_fwd(xt, bias, cb=512):
    nh, t = xt.shape
    cb = min(cb, t)

    def body(x_ref, b_ref, o_ref):
        upper, _ = _fgate_consts(cb)
        carry = jnp.zeros((nh, 1), F32)
        for blk in range(t // cb):
            z = x_ref[:, blk * cb:(blk + 1) * cb] + b_ref[...]
            logf = jnp.minimum(z, 0.0) - jnp.log(1.0 + jnp.exp(-jnp.abs(z)))
            cs = _f32dot(logf, upper) + carry
            o_ref[:, blk * cb:(blk + 1) * cb] = cs
            carry = cs[:, cb - 1:cb]

    vm = pl.BlockSpec(memory_space=pltpu.VMEM)
    return pl.pallas_call(
        body, name="fgate_fwd", in_specs=[vm, vm], out_specs=vm,
        out_shape=jax.ShapeDtypeStruct((nh, t), F32), compiler_params=_cparams(),
    )(xt, bias)


def _fgate_bwd(xt, bias, dft, cb=512):
    nh, t = xt.shape
    cb = min(cb, t)
    nblk = t // cb

    def body(x_ref, b_ref, d_ref, dx_ref, db_ref):
        _, lower = _fgate_consts(cb)
        carry = jnp.zeros((nh, 1), F32)
        db = jnp.zeros((nh, 1), F32)
        for blk in range(nblk - 1, -1, -1):
            sl = slice(blk * cb, (blk + 1) * cb)
            dlogf = _f32dot(d_ref[:, sl], lower) + carry
            carry = dlogf[:, 0:1]
            z = x_ref[:, sl] + b_ref[...]
            dz = dlogf * (1.0 - _sigmoid(z))
            dx_ref[:, sl] = dz
            db = db + jnp.sum(dz, axis=1, keepdims=True)
        db_ref[...] = db

    vm = pl.BlockSpec(memory_space=pltpu.VMEM)
    return pl.pallas_call(
        body, name="fgate_bwd", in_specs=[vm, vm, vm], out_specs=[vm, vm],
        out_shape=[jax.ShapeDtypeStruct((nh, t), F32), jax.ShapeDtypeStruct((nh, 1), F32)],
        compiler_params=_cparams(),
    )(xt, bias, dft)


def _attn_fwd(q, k, v, f_col, blk):
    t, width = v.shape
    nh = width // HEAD
    nq = t // blk

    def body(q_ref, k_ref, v_ref, fc_ref, o_ref, lse_ref):
        i = pl.program_id(0)
        tri = (lax.broadcasted_iota(jnp.int32, (blk, blk), 1) <= lax.broadcasted_iota(jnp.int32, (blk, blk), 0))
        for h in range(nh):
            cs = slice(h * HEAD, (h + 1) * HEAD)
            cs2 = slice(2 * h * HEAD, 2 * (h + 1) * HEAD)
            qh = q_ref[:, cs2]

            def tile(j, carry, masked):
                m, l, acc = carry
                rs = pl.ds(pl.multiple_of(j * blk, blk), blk)
                s = _bdot_raw(qh, k_ref[rs, cs2], _NT)
                if masked:
                    s = jnp.where(tri, s, NEG_INF)
                m_new = jnp.maximum(m, jnp.max(s, axis=1, keepdims=True))
                p = jnp.exp(s - m_new)
                alpha = jnp.exp(m - m_new)
                l_new = alpha * l + jnp.sum(p, axis=1, keepdims=True)
                acc_new = alpha * acc + _bdot_raw(p, v_ref[rs, cs], _NN)
                return m_new, l_new, acc_new

            init = (jnp.full((blk, 1), NEG_INF, F32), jnp.zeros((blk, 1), F32), jnp.zeros((blk, HEAD), F32))
            carry = lax.fori_loop(0, i, lambda j, c: tile(j, c, False), init)
            m, l, acc = tile(i, carry, True)
            o_ref[:, cs] = acc / l
            lse_ref[:, h:h + 1] = m + jnp.log(l) + fc_ref[:, h:h + 1]

    vm = pl.BlockSpec(memory_space=pltpu.VMEM)
    return pl.pallas_call(
        body, name="fox_attn_fwd",
        grid=(nq,),
        in_specs=[pl.BlockSpec((blk, 2 * width), lambda i: (i, 0)), vm, vm,
                  pl.BlockSpec((blk, nh), lambda i: (i, 0))],
        out_specs=[pl.BlockSpec((blk, width), lambda i: (i, 0)), pl.BlockSpec((blk, nh), lambda i: (i, 0))],
        out_shape=[jax.ShapeDtypeStruct((t, width), F32), jax.ShapeDtypeStruct((t, nh), F32)],
        compiler_params=_cparams(dimension_semantics=("parallel",)),
    )(q, k, v, f_col)


def _attn_delta(do, o, tb):
    t, width = o.shape
    nh = width // HEAD
    tb = min(tb, t)

    def body(do_ref, o_ref, dl_ref):
        for h in range(nh):
            cs = slice(h * HEAD, (h + 1) * HEAD)
            dl_ref[:, h:h + 1] = jnp.sum(do_ref[:, cs].astype(F32) * o_ref[:, cs], axis=1, keepdims=True)

    wide = pl.BlockSpec((tb, width), lambda i: (i, 0))
    return pl.pallas_call(body, name="fox_attn_delta", grid=(t // tb,), in_specs=[wide, wide],
                          out_specs=pl.BlockSpec((tb, nh), lambda i: (i, 0)),
                          out_shape=jax.ShapeDtypeStruct((t, nh), F32),
                          compiler_params=_cparams(dimension_semantics=("parallel",)))(do, o)


ATTN_BWD_GROUPS = 4


def _attn_bwd(q, k, v, f_col, do, lse, delta, blk):
    t, width = v.shape
    nh = width // HEAD
    nq = t // blk
    hpg = nh // ATTN_BWD_GROUPS
    gw = hpg * HEAD

    def body(q_ref, do_ref, k_ref, v_ref, fc_ref, lse_ref, dl_ref,
             dq_ref, dk_ref, dv_ref, dfc_ref, dfr_ref):
        g, j = pl.program_id(0), pl.program_id(1)
        tri = (lax.broadcasted_iota(jnp.int32, (blk, blk), 1) <= lax.broadcasted_iota(jnp.int32, (blk, blk), 0))

        @pl.when(j == 0)
        def _():
            dq_ref[...] = jnp.zeros_like(dq_ref)
            dfc_ref[...] = jnp.zeros_like(dfc_ref)

        for h in range(hpg):
            cs = slice(h * HEAD, (h + 1) * HEAD)
            cs2 = slice(2 * h * HEAD, 2 * (h + 1) * HEAD)
            csq = slice(2 * h * HEAD, (2 * h + 1) * HEAD)
            kj2 = k_ref[:, cs2]
            kj = k_ref[:, csq]
            vj = v_ref[:, cs]

            def tile(i, carry, masked):
                dk, dv, dfs = carry
                rs = pl.ds(pl.multiple_of(i * blk, blk), blk)
                qi = q_ref[rs, csq]
                doi = do_ref[rs, cs]
                bias = fc_ref[0, rs, h:h + 1] - lse_ref[0, rs, h:h + 1]
                p = jnp.exp(_bdot_raw(q_ref[rs, cs2], kj2, _NT) + bias)
                if masked:
                    p = jnp.where(tri, p, 0.0)
                ds = p * (_bdot_raw(doi, vj, _NT) - dl_ref[0, rs, h:h + 1])
                dsb = ds.astype(BF16)
                dq_ref[rs, cs] += _bdot_raw(dsb, kj, _NN)
                dfc_ref[0, rs, h:h + 1] += jnp.sum(ds, axis=1, keepdims=True)
                return (dk + _bdot_raw(dsb, qi, _TN), dv + _bdot_raw(p, doi, _TN),
                        dfs - jnp.sum(ds, axis=0, keepdims=True))

            init = (jnp.zeros((blk, HEAD), F32), jnp.zeros((blk, HEAD), F32), jnp.zeros((1, blk), F32))
            carry = tile(j, init, True)
            dk, dv, dfs = lax.fori_loop(j + 1, nq, lambda i, c: tile(i, c, False), carry)
            dk_ref[:, cs] = dk
            dv_ref[:, cs] = dv.astype(dv_ref.dtype)
            dfr_ref[0, 0, h:h + 1, :] = dfs

    by_group = lambda a: a.reshape(t, ATTN_BWD_GROUPS, hpg).transpose(1, 0, 2)
    once = pl.Buffered(1)
    stat = pl.BlockSpec((1, t, hpg), lambda g, j: (g, 0, 0), pipeline_mode=once)
    kv_blk = pl.BlockSpec((blk, gw), lambda g, j: (j, g))
    frow = pl.BlockSpec((1, 1, hpg, blk), lambda g, j: (g, j, 0, 0))
    dq, dk, dv, dfc, dfr = pl.pallas_call(
        body, name="fox_attn_bwd",
        grid=(ATTN_BWD_GROUPS, nq),
        in_specs=[pl.BlockSpec((t, 2 * gw), lambda g, j: (0, g), pipeline_mode=once),
                  pl.BlockSpec((t, gw), lambda g, j: (0, g), pipeline_mode=once),
                  pl.BlockSpec((blk, 2 * gw), lambda g, j: (j, g)), kv_blk, stat, stat, stat],
        out_specs=[pl.BlockSpec((t, gw), lambda g, j: (0, g)), kv_blk, kv_blk,
                   pl.BlockSpec((1, t, hpg), lambda g, j: (g, 0, 0)), frow],
        out_shape=[jax.ShapeDtypeStruct((t, width), F32), jax.ShapeDtypeStruct((t, width), F32),
                   jax.ShapeDtypeStruct((t, width), BF16), jax.ShapeDtypeStruct((ATTN_BWD_GROUPS, t, hpg), F32),
                   jax.ShapeDtypeStruct((ATTN_BWD_GROUPS, nq, hpg, blk), F32)],
        compiler_params=_cparams(dimension_semantics=("parallel", "arbitrary")),
    )(q, do, k, v, by_group(f_col), by_group(lse), by_group(delta))
    return (dq, dk, dv, dfc.transpose(1, 0, 2).reshape(t, nh),
            dfr.transpose(1, 0, 2, 3).reshape(nq, nh, blk))


SUBLANES = 8


def _shift_down(u, n):
    r = pltpu.roll(u, n, 0)
    row = lax.broadcasted_iota(jnp.int32, (SUBLANES, u.shape[1]), 0)
    return jnp.concatenate([jnp.where(row < n, 0.0, r[:SUBLANES]), r[SUBLANES:]], axis=0)


def _shift_up(u, n):
    t = u.shape[0]
    r = pltpu.roll(u, t - n, 0)
    row = lax.broadcasted_iota(jnp.int32, (SUBLANES, u.shape[1]), 0)
    return jnp.concatenate([r[:t - SUBLANES], jnp.where(row >= SUBLANES - n, 0.0, r[t - SUBLANES:])], axis=0)


def _convglu_specs(t):
    return [pl.BlockSpec((2, t, LANES), lambda j: (0, 0, j)),
            pl.BlockSpec((2, CONV_TAPS, LANES), lambda j: (0, 0, j)),
            pl.BlockSpec((2, 1, LANES), lambda j: (0, 0, j))]


def _convglu_fwd(u, cw, cb):
    _, t, fp = u.shape

    def body(u_ref, w_ref, b_ref, a_ref, c_ref):
        c = []
        for hf in range(2):
            uv, w = u_ref[hf].astype(F32), w_ref[hf]
            c.append(w[0:1] * _shift_down(uv, 2) + w[1:2] * _shift_down(uv, 1) + w[2:3] * uv + b_ref[hf])
            c_ref[hf] = c[hf].astype(c_ref.dtype)
        a_ref[...] = (_silu(c[0]) * c[1]).astype(a_ref.dtype)

    return pl.pallas_call(
        body, name="convglu_fwd",
        grid=(fp // LANES,),
        in_specs=_convglu_specs(t),
        out_specs=[pl.BlockSpec((t, LANES), lambda j: (0, j)), pl.BlockSpec((2, t, LANES), lambda j: (0, 0, j))],
        out_shape=[jax.ShapeDtypeStruct((t, fp), BF16), jax.ShapeDtypeStruct((2, t, fp), BF16)],
        compiler_params=_cparams(dimension_semantics=("parallel",)),
    )(u, cw, cb)


def _convglu_bwd(u, c, cw, da):
    _, t, fp = u.shape

    def body(u_ref, c_ref, w_ref, da_ref, du_ref, dw_ref, db_ref):
        gc, vc = c_ref[0].astype(F32), c_ref[1].astype(F32)
        sg = _sigmoid(gc)
        dav = da_ref[...].astype(F32)
        dcs = [dav * vc * (sg * (1.0 + gc * (1.0 - sg))), dav * (gc * sg)]
        for hf in range(2):
            dc, w, uv = dcs[hf], w_ref[hf], u_ref[hf].astype(F32)
            dc1, dc2 = _shift_up(dc, 1), _shift_up(dc, 2)
            du_ref[hf] = (w[2:3] * dc + w[1:2] * dc1 + w[0:1] * dc2).astype(du_ref.dtype)
            dw_ref[hf, 0:1, :] = jnp.sum(dc2 * uv, axis=0, keepdims=True)
            dw_ref[hf, 1:2, :] = jnp.sum(dc1 * uv, axis=0, keepdims=True)
            dw_ref[hf, 2:3, :] = jnp.sum(dc * uv, axis=0, keepdims=True)
            db_ref[hf] = jnp.sum(dc, axis=0, keepdims=True)

    pair, taps, bias = _convglu_specs(t)
    return pl.pallas_call(
        body, name="convglu_bwd",
        grid=(fp // LANES,),
        in_specs=[pair, pair, taps, pl.BlockSpec((t, LANES), lambda j: (0, j))],
        out_specs=[pair, taps, bias],
        out_shape=[jax.ShapeDtypeStruct((2, t, fp), BF16), jax.ShapeDtypeStruct((2, CONV_TAPS, fp), F32),
                   jax.ShapeDtypeStruct((2, 1, fp), F32)],
        compiler_params=_cparams(dimension_semantics=("parallel",)),
    )(u, c, cw, da)


def _local_step(x, target, mods, lb, small, pre_w, get_w, put_g, *, tb=512, attn_blk=512):
    t, d = x.shape
    nh = d // HEAD
    nb = NDEV
    wts = {}
    vec = lambda *names: [mods[n] for n in names]

    def ffn_fwd(h2, l):
        u = _mm_wblk(h2, wts[f"up{l}"], BF16, f"ffn{l}_up", gb=nb // 2, split=2, tm=512)
        a, c = _convglu_fwd(u, small[f"conv_w{l}"], small[f"conv_b{l}"])
        f = _mm(a, wts[f"down{l}"], "nn", F32, f"ffn{l}_down", tk=4096)
        return (u, c), a, f

    def ffn_bwd(df, h2, uc, a, l):
        u, c = uc
        da = _mm(df, wts[f"down{l}"], "nt", BF16, f"ffn{l}_down_dx", tn=1536)
        dwd = _mm(a, df, "tn", BF16, f"ffn{l}_down_dw", tm=768, tk=t)
        du, dcw, dcb = _convglu_bwd(u, c, small[f"conv_w{l}"], da)
        dh2 = _mm_wblk_dx(du, wts[f"up{l}"], BF16, f"ffn{l}_up_dx", k=d, gb=nb // 2, split=2, tm=1024)
        dwu = _mm_wblk_dw(h2, du, f"ffn{l}_up_dw", nb=nb, gb=1, split=2, tk=t)
        return dh2, dwu, dwd, dcw, dcb

    (h_a,) = _row_fwd(_f_mod, [(x, d, 0)], vec("sh1_0", "sc1_0"), [BF16], tb=tb, name="l0_mod1")
    wts.update(get_w("l0a", h_a))
    proj_a = _mm_wblk(h_a, wts["a_in"], F32, "a_in", gb=nb // 2)
    ypre, states = _hgrn2_fwd(proj_a, lb, small["a_norm_g"], tb)
    pre_w("l0b", ypre)
    wts.update(get_w("l0b", ypre))
    y_a = _mm(ypre, wts["a_out"], "nn", F32, "a_out")
    x1, h2_0 = _row_fwd(_f_res_mod, [(x, d, 0), (y_a, d, 0)], vec("g1_0", "sh2_0", "sc2_0"), [F32, BF16],
                        tb=tb, name="l0_res_mod2")
    u0, a0, f0 = ffn_fwd(h2_0, 0)
    x2, h_kv, h_q = _row_fwd(_f_res_mod2, [(x1, d, 0), (f0, d, 0)],
                             [mods["g2_0"] + pre_w("l1", f0)] + vec("kv_sh", "kv_sc", "sh1_1", "sc1_1"),
                             [F32, BF16, BF16], tb=tb, name="l0_res_kvmod_qmod")
    wts.update(get_w("l1", h_kv))
    proj_k = _mm(h_kv, wts["kv_k"], "nt", F32, "k_proj")
    v_b = _mm(h_kv, wts["kv_v"], "nt", BF16, "v_proj")
    proj_f = _mm(h_kv, wts["kv_f"], "nt", F32, "kv_fproj")
    f_logit_t = proj_f[:, :nh].T
    f_bias = small["kv_b_f"].reshape(nh, 1)
    f_col = _fgate_fwd(f_logit_t, f_bias).T
    (k_n,) = _row_fwd(_f_knorm_aug, [(proj_k, HEAD, 0)] + [(piece, 1, 0) for piece in _split3(-f_col)],
                      [small["k_norm_g"]], [BF16], nsub=nh, tb=tb, name="k_norm")
    proj_q = _mm_wblk(h_q, wts["b_q"], F32, "b_q", gb=nb)
    (q_n,) = _row_fwd(_f_qnorm_aug, [(proj_q, HEAD, 0)], [small["q_norm_g"]], [BF16], nsub=nh, tb=tb,
                      name="q_norm")
    o_att, lse = _attn_fwd(q_n, k_n, v_b, f_col, attn_blk)
    (z,) = _row_fwd(_f_outgate, [(o_att, HEAD, 0), (proj_q, HEAD, 1)], [], [BF16], nsub=nh, tb=tb, name="out_gate")
    y_b = _mm(z, wts["b_out"], "nn", F32, "b_out")
    x3, h2_1 = _row_fwd(_f_res_mod, [(x2, d, 0), (y_b, d, 0)], vec("g1_1", "sh2_1", "sc2_1"), [F32, BF16],
                        tb=tb, name="l1_res_mod2")
    u1, a1, f1 = ffn_fwd(h2_1, 1)
    loss, dx4, df1, dg2_1 = _loss_call(x3, f1, mods["g2_1"], target, tb)

    g = {}
    dmods = {"g2_1": dg2_1}
    dh2, g["up1"], g["down1"], g["conv_w1"], g["conv_b1"] = ffn_bwd(df1, h2_1, u1, a1, 1)
    (dx2, dy_b), (dmods["g1_1"], dmods["sh2_1"], dmods["sc2_1"]) = _row_bwd(
        _f_res_mod, [(x2, d, 0), (y_b, d, 0)], vec("g1_1", "sh2_1", "sc2_1"),
        [(dx4, d, 0), (dh2, d, 0)], [F32, BF16], tb=tb, name="l1_res_mod2_bwd")
    dz = _mm(dy_b, wts["b_out"], "nt", BF16, "b_out_dx")
    g["b_out"] = _mm(z, dy_b, "tn", BF16, "b_out_dw", tk=t)
    (do_att, dog), _ = _row_bwd(_f_outgate, [(o_att, HEAD, 0), (proj_q, HEAD, 1)], [], [(dz, HEAD, 0)],
                                [BF16, BF16], nsub=nh, tb=tb, name="out_gate_bwd")
    delta = _attn_delta(do_att, o_att, tb)
    dq_n, dk_n, dv, dfc_q, dfr_k = _attn_bwd(q_n, k_n, v_b, f_col, do_att, lse, delta, attn_blk)
    (dpq,), (g["q_norm_g"],) = _row_bwd(_f_qnorm, [(proj_q, HEAD, 0)], [small["q_norm_g"]],
                                        [(dq_n, HEAD, 0)], [BF16], nsub=nh, tb=tb, name="q_norm_bwd")
    dproj_q = jnp.concatenate([dpq, dog], axis=1)
    dh_q = _mm_wblk_dx(dproj_q, wts["b_q"], BF16, "b_q_dx", k=d, gb=nb)
    g["b_q"] = _mm_wblk_dw(h_q, dproj_q, "b_q_dw", nb=nb, gb=nb // 4, tk=t)
    (dpk,), (g["k_norm_g"],) = _row_bwd(_f_knorm, [(proj_k, HEAD, 0)], [small["k_norm_g"]],
                                        [(dk_n, HEAD, 0)], [BF16], nsub=nh, tb=tb, name="k_norm_bwd")
    df_t = dfc_q.T + dfr_k.transpose(1, 0, 2).reshape(nh, t)
    dflogit_t, g["kv_b_f"] = _fgate_bwd(f_logit_t, f_bias, df_t)
    dproj_f = jnp.pad(dflogit_t.T, ((0, 0), (0, LANES - nh))).astype(BF16)
    dh_kv = _mm(dpk, wts["kv_k"], "nn", BF16, "k_proj_dx")
    dh_kv_v = _mm(dv, wts["kv_v"], "nn", BF16, "v_proj_dx")
    dh_kv_f = _mm(dproj_f, wts["kv_f"], "nn", BF16, "kv_fproj_dx")
    g["kv_k"] = _mm(dpk, h_kv, "tn", BF16, "k_proj_dw", tk=t)
    g["kv_v"] = _mm(dv, h_kv, "tn", BF16, "v_proj_dw", tk=t)
    g["kv_f"] = _mm(dproj_f, h_kv, "tn", F32, "kv_fproj_dw", tk=1024)
    sent = put_g("l1", {n: g.pop(n) for n in ("b_out", "b_q", "kv_k", "kv_v", "kv_f", "up1", "down1")})
    (dx1, df0), (dmods["g2_0"], dmods["kv_sh"], dmods["kv_sc"], dmods["sh1_1"], dmods["sc1_1"]) = _row_bwd(
        _f_res_mod2, [(x1, d, 0), (f0, d, 0)], [mods["g2_0"] + sent] + vec("kv_sh", "kv_sc", "sh1_1", "sc1_1"),
        [(dx2, d, 0), (dh_kv, d, 0), (dh_q, d, 0)], [F32, BF16], tb=tb, name="l0_res_kvmod_qmod_bwd",
        cot_add=[(1, dh_kv_v), (1, dh_kv_f)])
    dh2, g["up0"], g["down0"], g["conv_w0"], g["conv_b0"] = ffn_bwd(df0, h2_0, u0, a0, 0)
    (dx0, dy_a), (dmods["g1_0"], dmods["sh2_0"], dmods["sc2_0"]) = _row_bwd(
        _f_res_mod, [(x, d, 0), (y_a, d, 0)], vec("g1_0", "sh2_0", "sc2_0"),
        [(dx1, d, 0), (dh2, d, 0)], [F32, BF16], tb=tb, name="l0_res_mod2_bwd")
    dypre = _mm(dy_a, wts["a_out"], "nt", BF16, "a_out_dx")
    g["a_out"] = _mm(ypre, dy_a, "tn", BF16, "a_out_dw", tk=t)
    sent = put_g("l0b", {n: g.pop(n) for n in ("a_out", "up0", "down0")})
    dproj_a, dlb, g["a_norm_g"] = _hgrn2_bwd(proj_a, lb + sent, small["a_norm_g"], states, dypre, tb)
    dh_a = _mm_wblk_dx(dproj_a, wts["a_in"], BF16, "a_in_dx", k=d, gb=nb, split=4, tm=512)
    put_g("l0a", {"a_in": _mm_wblk_dw(h_a, dproj_a, "a_in_dw", nb=nb, gb=1, split=4, tk=t)})
    (grad_x,), (dmods["sh1_0"], dmods["sc1_0"]) = _row_bwd(
        _f_mod, [(x, d, 0)], vec("sh1_0", "sc1_0"), [(dh_a, d, 0)], [F32], tb=tb, name="l0_mod1_bwd",
        add_to=(0, dx0))
    return loss, grad_x, dmods, dlb, g


def _position():
    return lax.axis_index("x"), lax.axis_index("y"), lax.axis_index("c")


def _hbm_specs(n):
    return [pl.BlockSpec(memory_space=pl.ANY)] * n


def _all_gather(arrs, name):
    n = len(arrs)

    def body(*refs):
        x_refs, out_refs = refs[:n], refs[n:2 * n]
        send_sems, recv_sems, local_sems = refs[2 * n:]
        x, y, cc = _position()
        me, sibling = (x, y, cc), (x, y, 1 - cc)
        chips = [(1 - x, y), (x, 1 - y), (1 - x, 1 - y)]

        def copy(a, k, block, to, src=None):
            slot = out_refs[a].at[4 * block[0] + 2 * block[1] + block[2]]
            return pltpu.make_async_remote_copy(
                src_ref=slot if src is None else src, dst_ref=slot,
                send_sem=send_sems.at[7 * a + k], recv_sem=recv_sems.at[7 * a + k],
                device_id=to, device_id_type=_MESH)

        local = [pltpu.make_async_copy(x_refs[a], out_refs[a].at[4 * x + 2 * y + cc], local_sems.at[a])
                 for a in range(n)]
        for cp in local:
            cp.start()
        first = []
        for a in range(n):
            first.append(copy(a, 0, me, sibling, src=x_refs[a]))
            first += [copy(a, 1 + j, me, (*chip, cc), src=x_refs[a]) for j, chip in enumerate(chips)]
        for cp in first:
            cp.start()
        passed = []
        for j, chip in enumerate(chips):
            for a in range(n):
                copy(a, 1 + j, (*chip, cc), me).wait_recv()
                fwd = copy(a, 4 + j, (*chip, cc), sibling)
                fwd.start()
                passed.append(fwd)
        for a in range(n):
            copy(a, 0, sibling, me).wait_recv()
        for j, chip in enumerate(chips):
            for a in range(n):
                copy(a, 4 + j, (*chip, 1 - cc), me).wait_recv()
        for cp in first + passed:
            cp.wait_send()
        for cp in local:
            cp.wait()

    return pl.pallas_call(
        body, name=name,
        out_shape=[jax.ShapeDtypeStruct((NDEV, *a.shape), a.dtype) for a in arrs],
        in_specs=_hbm_specs(n), out_specs=_hbm_specs(n),
        scratch_shapes=[pltpu.SemaphoreType.DMA((7 * n,)), pltpu.SemaphoreType.DMA((7 * n,)),
                        pltpu.SemaphoreType.DMA((n,))],
    )(*arrs)


_XCHG_EFFECT = pltpu.SideEffectType.DATAFLOW_SIDE_EFFECTING
ALL_PEERS = (1, 2, 3, 4, 5, 6, 7)
SAME_CORE = (2, 4, 6)


def _xchg_copies(src_refs, land_refs, send_sems, recv_sems, local_sems, scatter, rels):
    x, y, cc = _position()
    me = 4 * x + 2 * y + cc
    remote, local = [], []
    for a, (src, land) in enumerate(zip(src_refs, land_refs)):
        local.append(pltpu.make_async_copy(src.at[me] if scatter else src, land.at[me], local_sems.at[a]))
        for idx, rel in enumerate(rels):
            px = 1 - x if rel & 4 else x
            py = 1 - y if rel & 2 else y
            pc = 1 - cc if rel & 1 else cc
            k = len(rels) * a + idx
            remote.append(pltpu.make_async_remote_copy(
                src_ref=src.at[4 * px + 2 * py + pc] if scatter else src, dst_ref=land.at[me],
                send_sem=send_sems.at[k], recv_sem=recv_sems.at[k], device_id=(px, py, pc), device_id_type=_MESH))
    return remote, local


def _xchg_start(srcs, scatter, rels, after, name):
    n = len(srcs)
    lands = [lax.empty(s.shape if scatter else (NDEV, *s.shape), s.dtype) for s in srcs]

    def body(*refs):
        remote, local = _xchg_copies(refs[:n], refs[n:2 * n], *refs[2 * n + 1:2 * n + 4], scatter, rels)
        for cp in local + remote:
            cp.start()
        token = refs[-1]
        token[...] = jnp.zeros_like(token)

    hbm = pl.BlockSpec(memory_space=pltpu.HBM)
    sem = pl.BlockSpec(memory_space=pltpu.SEMAPHORE)
    out = pl.pallas_call(
        body, name=name,
        out_shape=(pltpu.SemaphoreType.DMA((len(rels) * n,)), pltpu.SemaphoreType.DMA((len(rels) * n,)),
                   pltpu.SemaphoreType.DMA((n,)),
                   *[pltpu.HBM(a.shape, a.dtype) for a in srcs + lands], jax.ShapeDtypeStruct((8, LANES), F32)),
        in_specs=[hbm] * (2 * n) + [pl.BlockSpec(memory_space=pl.ANY)],
        out_specs=(sem, sem, sem, *[hbm] * (2 * n), pl.BlockSpec(memory_space=pltpu.VMEM)),
        input_output_aliases={i: 3 + i for i in range(2 * n)},
        compiler_params=pltpu.CompilerParams(has_side_effects=_XCHG_EFFECT),
    )(*[pltpu.with_memory_space_constraint(a, pltpu.HBM) for a in srcs + lands], after)
    return out[:-1], out[-1][0, 0]


def _xchg_wait(handles, after, scatter, rels, name):
    n = (len(handles) - 3) // 2

    def body(*refs):
        remote, local = _xchg_copies(refs[:n], refs[n:2 * n], *refs[2 * n:2 * n + 3], scatter, rels)
        for cp in remote:
            cp.wait_send()
            cp.wait_recv()
        for cp in local:
            cp.wait()

    hbm = pl.BlockSpec(memory_space=pltpu.HBM)
    sem = pl.BlockSpec(memory_space=pltpu.SEMAPHORE)
    thru = list(handles[3:])
    out = pl.pallas_call(
        body, name=name,
        out_shape=tuple(pltpu.HBM(a.shape, a.dtype) for a in thru),
        in_specs=[hbm] * (2 * n) + [sem, sem, sem, pl.BlockSpec(memory_space=pl.ANY)],
        out_specs=tuple([hbm] * (2 * n)),
        input_output_aliases={i: i for i in range(2 * n)},
        compiler_params=pltpu.CompilerParams(has_side_effects=_XCHG_EFFECT),
    )(*thru, *handles[:3], after)
    return list(out[n:])


def _sibling_copies(land_refs, send_sems, recv_sems):
    x, y, cc = _position()

    def copy(a, q, core):
        slot = land_refs[a].at[2 * q + core]
        return pltpu.make_async_remote_copy(
            src_ref=slot, dst_ref=slot, send_sem=send_sems.at[NCHIP * a + q], recv_sem=recv_sems.at[NCHIP * a + q],
            device_id=(x, y, 1 - cc), device_id_type=_MESH)

    pairs = [(a, q) for a in range(len(land_refs)) for q in range(NCHIP)]
    return [copy(a, q, cc) for a, q in pairs], [copy(a, q, 1 - cc) for a, q in pairs]


def _sibling_forward_start(lands, name):
    n = len(lands)

    def body(*refs):
        sends, _ = _sibling_copies(refs[:n], refs[n], refs[n + 1])
        for cp in sends:
            cp.start()
        refs[-1][...] = jnp.zeros_like(refs[-1])

    hbm = pl.BlockSpec(memory_space=pltpu.HBM)
    sem = pl.BlockSpec(memory_space=pltpu.SEMAPHORE)
    out = pl.pallas_call(
        body, name=name,
        out_shape=(pltpu.SemaphoreType.DMA((NCHIP * n,)), pltpu.SemaphoreType.DMA((NCHIP * n,)),
                   *[pltpu.HBM(a.shape, a.dtype) for a in lands], jax.ShapeDtypeStruct((8, LANES), F32)),
        in_specs=[hbm] * n,
        out_specs=(sem, sem, *[hbm] * n, pl.BlockSpec(memory_space=pltpu.VMEM)),
        input_output_aliases={i: 2 + i for i in range(n)},
        compiler_params=pltpu.CompilerParams(has_side_effects=_XCHG_EFFECT),
    )(*lands)
    return out[:-1], out[-1][0, 0]


def _sibling_forward_wait(handles, after, name):
    n = len(handles) - 2

    def body(*refs):
        sends, arrivals = _sibling_copies(refs[:n], refs[n], refs[n + 1])
        for cp in sends:
            cp.wait_send()
        for cp in arrivals:
            cp.wait_recv()

    hbm = pl.BlockSpec(memory_space=pltpu.HBM)
    sem = pl.BlockSpec(memory_space=pltpu.SEMAPHORE)
    lands = list(handles[2:])
    return list(pl.pallas_call(
        body, name=name,
        out_shape=tuple(pltpu.HBM(a.shape, a.dtype) for a in lands),
        in_specs=[hbm] * n + [sem, sem, pl.BlockSpec(memory_space=pl.ANY)],
        out_specs=tuple([hbm] * n),
        input_output_aliases={i: i for i in range(n)},
        compiler_params=pltpu.CompilerParams(has_side_effects=_XCHG_EFFECT),
    )(*lands, *handles[:2], after))


def _slab_sum(slabs, name, tr=None):
    n, r, c = slabs.shape
    tr = r if tr is None else tr

    def body(s_ref, o_ref):
        acc = s_ref[0].astype(F32)
        for q in range(1, n):
            acc = acc + s_ref[q].astype(F32)
        o_ref[...] = acc

    return pl.pallas_call(body, name=name, grid=(r // tr,),
                          in_specs=[pl.BlockSpec((n, tr, c), lambda i: (0, i, 0))],
                          out_specs=pl.BlockSpec((tr, c), lambda i: (i, 0)),
                          out_shape=jax.ShapeDtypeStruct((r, c), F32),
                          compiler_params=_cparams(dimension_semantics=("parallel",)))(slabs)


def _ada_fwd(c_all, ada_w, kv_ada_w, logits):
    rows, d = c_all.shape
    n0, nkv = ada_w.shape[2], kv_ada_w.shape[1]

    def body(c_ref, w_ref, kw_ref, lg_ref, part_ref, cact_ref, lb_ref):
        ca = _silu(c_ref[...])
        cact_ref[...] = ca
        part_ref[:, 0:n0] = _bdot_raw(ca, w_ref[0], _NN)
        part_ref[:, n0:2 * n0] = _bdot_raw(ca, w_ref[1], _NN)
        part_ref[:, 2 * n0:2 * n0 + nkv] = _bdot_raw(ca, kw_ref[...], _NN)
        lb_ref[...] = _sigmoid(lg_ref[0:1, :] - lg_ref[1:2, :])

    vm = pl.BlockSpec(memory_space=pltpu.VMEM)
    return pl.pallas_call(
        body, name="ada_fwd", in_specs=[vm, vm, vm, vm], out_specs=[vm, vm, vm],
        out_shape=[jax.ShapeDtypeStruct((rows, 2 * n0 + nkv), F32), jax.ShapeDtypeStruct((rows, d), F32),
                   jax.ShapeDtypeStruct((1, d), F32)],
        compiler_params=_cparams(),
    )(c_all, ada_w, kv_ada_w, logits)


def _ada_bwd(c_act, dm0, dm1, dkv, lb, dlb):
    rows, d = c_act.shape

    def body(c_ref, d0_ref, d1_ref, dk_ref, lb_ref, dlb_ref, dw_ref, dkw_ref, dlg_ref):
        ca = c_ref[...]
        dw_ref[0] = _bdot_raw(ca, d0_ref[...], _TN)
        dw_ref[1] = _bdot_raw(ca, d1_ref[...], _TN)
        dkw_ref[...] = _bdot_raw(ca, dk_ref[...], _TN)
        lbv = lb_ref[...]
        dl0 = dlb_ref[...] * lbv * (1.0 - lbv)
        dlg_ref[0:1, :] = dl0
        dlg_ref[1:2, :] = -dl0

    vm = pl.BlockSpec(memory_space=pltpu.VMEM)
    return pl.pallas_call(
        body, name="ada_bwd", in_specs=[vm] * 6, out_specs=[vm, vm, vm],
        out_shape=[jax.ShapeDtypeStruct((2, d, dm0.shape[1]), F32), jax.ShapeDtypeStruct((d, dkv.shape[1]), F32),
                   jax.ShapeDtypeStruct((2, d), F32)],
        compiler_params=_cparams(),
    )(c_act, dm0, dm1, dkv, lb, dlb)


def _adamw(w, g, m, v, name, tr=512, after=None):
    r, c = w.shape
    tr = _divisor_tile(r, tr, unit=8)
    c1 = 1.0 - ADAM_B1 ** ADAM_STEP
    c2 = 1.0 - ADAM_B2 ** ADAM_STEP
    deps = [] if after is None else [after]

    def body(w_ref, g_ref, m_ref, v_ref, *rest):
        d_ref, mo_ref, vo_ref = rest[len(deps):]
        gv = g_ref[...]
        mn = ADAM_B1 * m_ref[...] + (1.0 - ADAM_B1) * gv
        vn = ADAM_B2 * v_ref[...] + (1.0 - ADAM_B2) * (gv * gv)
        d_ref[...] = -ADAM_LR * ((mn / c1) / (jnp.sqrt(vn / c2) + ADAM_EPS) + ADAM_WD * w_ref[...])
        mo_ref[...] = mn
        vo_ref[...] = vn

    spec = pl.BlockSpec((tr, c), lambda i: (i, 0))
    out = jax.ShapeDtypeStruct((r, c), F32)
    return pl.pallas_call(body, name=name, grid=(r // tr,),
                          in_specs=[spec] * 4 + [pl.BlockSpec(a.shape, lambda i: (0, 0)) for a in deps],
                          out_specs=[spec] * 3, out_shape=[out, out, out],
                          compiler_params=_cparams(dimension_semantics=("parallel",)))(w, g, m, v, *deps)


def _pad_rows(a, rows):
    return jnp.pad(a, ((0, rows - a.shape[0]), (0, 0)))


def _pack_small(parts, lanes=LANES, row_unit=8):
    flat = jnp.concatenate([p.reshape(-1).astype(F32) for p in parts])
    rows = _round_up(-(-flat.shape[0] // lanes), row_unit)
    return jnp.pad(flat, (0, rows * lanes - flat.shape[0])).reshape(rows, lanes)


def _unpack_small(flat, shapes):
    out, off = [], 0
    for s in shapes:
        n = 1
        for k in s:
            n *= k
        out.append(flat[off:off + n].reshape(s))
        off += n
    return out


def _pad_shard_cols(a, n_loc, n_pad):
    lead = a.shape[:-1]
    a = a.reshape(*lead, NDEV, n_loc)
    a = jnp.pad(a, [(0, 0)] * (len(lead) + 1) + [(0, n_pad - n_loc)])
    return a.reshape(*lead, NDEV * n_pad)


def _unpad_shard_cols(a, n_loc, n_pad):
    lead = a.shape[:-1]
    return a.reshape(*lead, NDEV, n_pad)[..., :n_loc].reshape(*lead, NDEV * n_loc)


def kernel(x, c, ada_w, ada_b, a_w_in, a_lb_logits, a_norm_g, a_w_out, kv_ada_w, kv_ada_b, kv_w, kv_b_f, k_norm_g, b_w_q, q_norm_g, b_w_out, ffn_w_up, ffn_conv_w, ffn_conv_b, ffn_w_down, loss_target, m_ada_w, m_ada_b, m_a_w_in, m_a_lb_logits, m_a_norm_g, m_a_w_out, m_kv_ada_w, m_kv_ada_b, m_kv_w, m_kv_b_f, m_k_norm_g, m_b_w_q, m_q_norm_g, m_b_w_out, m_ffn_w_up, m_ffn_conv_w, m_ffn_conv_b, m_ffn_w_down, v_ada_w, v_ada_b, v_a_w_in, v_a_lb_logits, v_a_norm_g, v_a_w_out, v_kv_ada_w, v_kv_ada_b, v_kv_w, v_kv_b_f, v_k_norm_g, v_b_w_q, v_q_norm_g, v_b_w_out, v_ffn_w_up, v_ffn_conv_w, v_ffn_conv_b, v_ffn_w_down):
    t, d = x.shape[1], x.shape[2]
    nh = d // HEAD
    ncw = ffn_w_up.shape[2]
    ncp = _round_up(ncw, LANES)
    two_f = ncw * NDEV
    ff = two_f // 2
    fp = ncp * NDEV // 2
    rd = ffn_w_down.shape[1]
    me = 4 * lax.axis_index("x") + 2 * lax.axis_index("y") + lax.axis_index("c")
    weights = dict(ada_w=ada_w, ada_b=ada_b, a_w_in=a_w_in, a_lb_logits=a_lb_logits, a_norm_g=a_norm_g,
                   a_w_out=a_w_out, kv_ada_w=kv_ada_w, kv_ada_b=kv_ada_b, kv_w=kv_w, kv_b_f=kv_b_f,
                   k_norm_g=k_norm_g, b_w_q=b_w_q, q_norm_g=q_norm_g, b_w_out=b_w_out, ffn_w_up=ffn_w_up,
                   ffn_conv_w=ffn_conv_w, ffn_conv_b=ffn_conv_b, ffn_w_down=ffn_w_down)
    m_in = dict(ada_w=m_ada_w, ada_b=m_ada_b, a_w_in=m_a_w_in, a_lb_logits=m_a_lb_logits, a_norm_g=m_a_norm_g,
                a_w_out=m_a_w_out, kv_ada_w=m_kv_ada_w, kv_ada_b=m_kv_ada_b, kv_w=m_kv_w, kv_b_f=m_kv_b_f,
                k_norm_g=m_k_norm_g, b_w_q=m_b_w_q, q_norm_g=m_q_norm_g, b_w_out=m_b_w_out, ffn_w_up=m_ffn_w_up,
                ffn_conv_w=m_ffn_conv_w, ffn_conv_b=m_ffn_conv_b, ffn_w_down=m_ffn_w_down)
    v_in = dict(ada_w=v_ada_w, ada_b=v_ada_b, a_w_in=v_a_w_in, a_lb_logits=v_a_lb_logits, a_norm_g=v_a_norm_g,
                a_w_out=v_a_w_out, kv_ada_w=v_kv_ada_w, kv_ada_b=v_kv_ada_b, kv_w=v_kv_w, kv_b_f=v_kv_b_f,
                k_norm_g=v_k_norm_g, b_w_q=v_b_w_q, q_norm_g=v_q_norm_g, b_w_out=v_b_w_out, ffn_w_up=v_ffn_w_up,
                ffn_conv_w=v_ffn_conv_w, ffn_conv_b=v_ffn_conv_b, ffn_w_down=v_ffn_w_down)
    order = list(weights)

    up_loc = jnp.pad(ffn_w_up, ((0, 0), (0, 0), (0, ncp - ncw))).astype(BF16)
    down_loc = ffn_w_down.astype(BF16)
    gather_names = {"l0b": ["a_out", "up0", "down0"], "l1": ["kv", "b_q", "b_out", "up1", "down1"]}
    shards = {"a_out": a_w_out[0].astype(BF16), "up0": up_loc[0], "down0": down_loc[0], "kv": kv_w.T.astype(BF16),
              "b_q": b_w_q[0].astype(BF16), "b_out": b_w_out[0].astype(BF16), "up1": up_loc[1],
              "down1": down_loc[1]}
    pre = _pack_small([c, a_lb_logits, ffn_conv_w])
    a_in_all, pre_all = _all_gather([a_w_in[0].astype(BF16), pre], "gather_a_w_in_and_small_inputs")
    pre_all = pre_all.reshape(NDEV, -1)
    c_all = pre_all[:, :d]
    logits = pre_all[:, d:d + 2 * HEAD].reshape(NDEV, 2, HEAD).transpose(1, 0, 2).reshape(2, d)
    conv_w_full = pre_all[:, d + 2 * HEAD:d + 2 * HEAD + 2 * CONV_TAPS * ncw]
    conv_w_full = conv_w_full.reshape(NDEV, 2, CONV_TAPS, ncw).transpose(1, 2, 0, 3).reshape(2, CONV_TAPS, two_f)

    part, c_act, lb = _ada_fwd(_pad_rows(c_all, 2 * NDEV), ada_w, kv_ada_w, logits)
    (part_all,) = _all_gather([part[:NDEV]], "gather_adaln")
    mine = lax.dynamic_index_in_dim(part_all, me, axis=1, keepdims=False)
    n0, nkv = ada_w.shape[2], kv_ada_w.shape[1]
    mod_names = ["sh1", "sc1", "g1", "sh2", "sc2", "g2"]
    mods = {}
    for l in range(2):
        row = mine[:, l * n0:(l + 1) * n0].reshape(-1) + ada_b[l]
        for k, nm in enumerate(mod_names):
            mods[f"{nm}_{l}"] = row[k * d:(k + 1) * d].reshape(1, d)
    kvrow = mine[:, 2 * n0:2 * n0 + nkv].reshape(-1) + kv_ada_b
    mods["kv_sh"], mods["kv_sc"] = kvrow[:d].reshape(1, d), kvrow[d:].reshape(1, d)

    in_flight = {}

    def start_gather(grp, dep):
        srcs = [shards[n] for n in gather_names[grp]]
        in_flight[grp], started = _xchg_start(srcs, False, SAME_CORE, dep, f"gather_{grp}_start")
        return started

    zero = start_gather("l0b", part_all)
    mods["sh1_0"] = mods["sh1_0"] + zero

    small = {"a_norm_g": a_norm_g, "k_norm_g": k_norm_g.reshape(1, HEAD), "q_norm_g": q_norm_g, "kv_b_f": kv_b_f}
    for l in range(2):
        small[f"conv_w{l}"] = _pad_shard_cols(conv_w_full[l], ncw, ncp).reshape(CONV_TAPS, 2, fp).transpose(1, 0, 2)
        small[f"conv_b{l}"] = _pad_shard_cols(ffn_conv_b[l], ncw, ncp).reshape(2, 1, fp)

    forwarding = {}

    def pre_w(grp, after):
        arrived = _xchg_wait(in_flight[grp], after, False, SAME_CORE, f"gather_{grp}_wait")
        forwarding[grp], started = _sibling_forward_start(arrived, f"gather_{grp}_to_sibling_start")
        return started

    def get_w(grp, after):
        if grp == "l0a":
            return {"a_in": a_in_all}
        full = _sibling_forward_wait(forwarding[grp], after, f"gather_{grp}_to_sibling_wait")
        if grp == "l0b":
            started = start_gather("l1", full[0])
            full[0] = full[0] + started.astype(full[0].dtype)
        got = dict(zip(gather_names[grp], full))
        out = {}
        for n, a in got.items():
            if n in ("a_out", "b_out"):
                out[n] = a.reshape(d, d)
            elif n in ("down0", "down1"):
                dn = a.reshape(NCHIP, ff // NCHIP, d)
                out[n] = jnp.pad(dn, ((0, 0), (0, ncp - ncw), (0, 0))).reshape(fp, d)
            elif n == "kv":
                kv_t = a.reshape(NDEV * kv_w.shape[1], d)
                out["kv_k"], out["kv_v"] = kv_t[:d], kv_t[d:2 * d]
                out["kv_f"] = jnp.pad(kv_t[2 * d:], ((0, LANES - nh), (0, 0)))
            else:
                out[n] = a
        return out

    scatter_flight, g_last = {}, {}

    def put_g(grp, gr):
        if grp == "l0a":
            g_last.update(gr)
            return zero
        if grp == "l1":
            g_kvw = jnp.concatenate([gr["kv_k"], gr["kv_v"], gr["kv_f"][:nh].astype(BF16)], axis=0)
            arrs = {"kv_w": g_kvw.reshape(NDEV, kv_w.shape[1], d), "b_w_q": gr["b_q"],
                    "b_w_out": gr["b_out"].reshape(NDEV, d // NDEV, d), "up1": gr["up1"],
                    "down1": gr["down1"].reshape(NCHIP, ncp, d)[:, :ncw].reshape(NDEV, rd, d)}
        else:
            arrs = {"a_w_out": gr["a_out"].reshape(NDEV, d // NDEV, d), "up0": gr["up0"],
                    "down0": gr["down0"].reshape(NCHIP, ncp, d)[:, :ncw].reshape(NDEV, rd, d)}
        srcs = list(arrs.values())
        handles, sent = _xchg_start(srcs, True, ALL_PEERS, srcs[0], f"scatter_{grp}_start")
        scatter_flight[grp] = (list(arrs), handles)
        return sent

    loss_v, grad_x, dmods, dlb, g = _local_step(x[0], loss_target[0], mods, lb, small, pre_w, get_w, put_g)

    g_sum = {}
    for grp in ("l1", "l0b"):
        names, handles = scatter_flight[grp]
        for nm, a in zip(names, _xchg_wait(handles, grad_x, True, ALL_PEERS, f"scatter_{grp}_wait")):
            g_sum[nm] = _slab_sum(a, f"rs_slab_sum_{nm}")

    def conv_w_grad(a):
        return _unpad_shard_cols(a.transpose(1, 0, 2).reshape(CONV_TAPS, 2 * fp), ncw, ncp)

    def conv_b_grad(a):
        return _unpad_shard_cols(a.reshape(2 * fp), ncw, ncp)

    dmod_vec = [dmods[f"{nm}_{l}"] for l in range(2) for nm in mod_names] + [dmods["kv_sh"], dmods["kv_sc"]]
    post = _pack_small(dmod_vec + [dlb, g["a_norm_g"], g["k_norm_g"], g["q_norm_g"],
                                   jnp.pad(g["kv_b_f"].reshape(-1), (0, LANES - nh)),
                                   conv_w_grad(g["conv_w0"]), conv_w_grad(g["conv_w1"]),
                                   conv_b_grad(g["conv_b0"]), conv_b_grad(g["conv_b1"]), loss_v])
    (post_all,) = _all_gather([post], "gather_small_grads")
    a_in_flight, a_in_sent = _xchg_start([g_last["a_in"]], True, ALL_PEERS, post_all, "scatter_l0a_start")
    a_in_sent = a_in_sent.reshape(1, 1)
    tot = _slab_sum(post_all, "small_grad_sum").reshape(-1)
    nmod = 14 * d
    (t_mod, t_lb, t_ang, t_kng, t_qng, t_bf, t_cw, t_cb, t_loss) = _unpack_small(
        tot, [(nmod,), (1, d), (1, HEAD), (HEAD,), (1, HEAD), (LANES,), (2, CONV_TAPS, two_f), (2, two_f),
              (LANES,)])
    loss = t_loss[0]
    dm_all = post_all.reshape(NDEV, -1)[:, :nmod]
    dm0 = lax.dynamic_slice_in_dim(dm_all[:, :6 * d], me * n0, n0, axis=1)
    dm1 = lax.dynamic_slice_in_dim(dm_all[:, 6 * d:12 * d], me * n0, n0, axis=1)
    dkv = lax.dynamic_slice_in_dim(dm_all[:, 12 * d:], me * nkv, nkv, axis=1)
    g_ada_w, g_kv_ada_w, g_logits = _ada_bwd(c_act, _pad_rows(dm0, 2 * NDEV), _pad_rows(dm1, 2 * NDEV),
                                              _pad_rows(dkv, 2 * NDEV), lb, t_lb)

    grads = {
        "ada_w": g_ada_w,
        "ada_b": t_mod[:12 * d].reshape(2, 6 * d),
        "a_lb_logits": lax.dynamic_slice_in_dim(g_logits, me * HEAD, HEAD, axis=1),
        "a_norm_g": t_ang,
        "a_w_out": g_sum["a_w_out"].reshape(a_w_out.shape),
        "kv_ada_w": g_kv_ada_w,
        "kv_ada_b": t_mod[12 * d:],
        "kv_w": g_sum["kv_w"].T,
        "kv_b_f": t_bf[:nh],
        "k_norm_g": t_kng,
        "b_w_q": g_sum["b_w_q"].reshape(b_w_q.shape),
        "q_norm_g": t_qng,
        "b_w_out": g_sum["b_w_out"].reshape(b_w_out.shape),
        "ffn_w_up": jnp.stack([g_sum["up0"][:, :ncw], g_sum["up1"][:, :ncw]]),
        "ffn_conv_w": lax.dynamic_slice_in_dim(t_cw, me * ncw, ncw, axis=2),
        "ffn_conv_b": t_cb,
        "ffn_w_down": jnp.stack([g_sum["down0"], g_sum["down1"]]),
    }

    big_adam = ["ada_w", "a_w_out", "kv_ada_w", "kv_w", "b_w_q", "b_w_out", "ffn_w_up", "ffn_w_down", "a_w_in"]
    small_adam = [n for n in order if n not in big_adam]
    delta, new_m, new_v = {}, {}, {}
    packs = [_pack_small([src[n] for n in small_adam]) for src in (weights, grads, m_in, v_in)]
    outs = _adamw(*packs, "adamw_small", tr=packs[0].shape[0])
    shapes = [weights[n].shape for n in small_adam]
    for dst, o in zip((delta, new_m, new_v), outs):
        for n, a in zip(small_adam, _unpack_small(o.reshape(-1), shapes)):
            dst[n] = a
    for n in big_adam:
        if n == "a_w_in":
            (landed,) = _xchg_wait(a_in_flight, new_v["ffn_w_down"], True, ALL_PEERS, "scatter_l0a_wait")
            grads[n] = _slab_sum(landed, "rs_slab_sum_a_w_in").reshape(a_w_in.shape)
        shp = weights[n].shape
        two_d = lambda a: a.reshape(-1, shp[-1])
        dl, mn, vn = _adamw(two_d(weights[n]), two_d(grads[n]), two_d(m_in[n]), two_d(v_in[n]), f"adamw_{n}",
                            after=a_in_sent)
        delta[n], new_m[n], new_v[n] = dl.reshape(shp), mn.reshape(shp), vn.reshape(shp)

    return (loss, grad_x.reshape(x.shape), *[grads[n] for n in order], *[delta[n] for n in order],
            *[new_m[n] for n in order], *[new_v[n] for n in order])
```

```python
import functools

import jax
import jax.numpy as jnp
from jax import lax
from jax.experimental import pallas as pl
from jax.experimental.pallas import tpu as pltpu

F32 = jnp.float32
BF16 = jnp.bfloat16

NDEV = 8
NCHIP = 4
HEAD = 128
A_CHUNK = 64
CONV_TAPS = 3
EPS = 1e-6
NEG_INF = -1e30
LANES = 128
VMEM_LIMIT = 48 * 1024 * 1024

ADAM_LR = 0.001
ADAM_B1 = 0.9
ADAM_B2 = 0.999
ADAM_EPS = 1e-08
ADAM_WD = 0.01
ADAM_STEP = 10

_NN = (((1,), (0,)), ((), ()))
_NT = (((1,), (1,)), ((), ()))
_TN = (((0,), (0,)), ((), ()))
_MESH = pl.DeviceIdType.MESH


def _cparams(**kw):
    return pltpu.CompilerParams(vmem_limit_bytes=VMEM_LIMIT, **kw)


def _divisor_tile(n, pref, unit=LANES):
    if n <= pref:
        return n
    best = None
    for t in range(unit, pref + 1, unit):
        if n % t == 0:
            best = t
    assert best is not None, (n, pref)
    return best


def _round_up(n, unit):
    return -(-n // unit) * unit


def _bdot_raw(a, b, dims):
    return lax.dot_general(a.astype(BF16), b.astype(BF16), dims, preferred_element_type=F32)


@jax.custom_vjp
def _dot_nn(a, b):
    return _bdot_raw(a, b, _NN)


@jax.custom_vjp
def _dot_nt(a, b):
    return _bdot_raw(a, b, _NT)


@jax.custom_vjp
def _dot_tn(a, b):
    return _bdot_raw(a, b, _TN)


_dot_nn.defvjp(lambda a, b: (_bdot_raw(a, b, _NN), (a, b)),
               lambda r, g: (_dot_nt(g, r[1]), _dot_tn(r[0], g)))
_dot_nt.defvjp(lambda a, b: (_bdot_raw(a, b, _NT), (a, b)),
               lambda r, g: (_dot_nn(g, r[1]), _dot_tn(g, r[0])))
_dot_tn.defvjp(lambda a, b: (_bdot_raw(a, b, _TN), (a, b)),
               lambda r, g: (_dot_nt(r[1], g), _dot_nn(r[0], g)))


def _f32dot(a, b):
    return lax.dot_general(a, b, _NN, precision=lax.Precision.HIGHEST, preferred_element_type=F32)


def _sigmoid(x):
    return jax.nn.sigmoid(x)


def _silu(x):
    return x * jax.nn.sigmoid(x)


def _rms(x):
    return x * lax.rsqrt(jnp.mean(x * x, axis=-1, keepdims=True) + EPS)


def _modulate(x, sh, sc):
    return _rms(x) * (1.0 + sc) + sh


def _mm_call(a, b, dims, a_spec, b_spec, o_spec, o_shape, grid, acc_tile, name):
    nk = grid[2]

    def body(a_ref, b_ref, o_ref, *acc):
        p = lax.dot_general(a_ref[...].astype(BF16), b_ref[...].astype(BF16), dims,
                            preferred_element_type=F32)
        if nk == 1:
            o_ref[...] = p.astype(o_ref.dtype)
        else:
            kk = pl.program_id(2)

            @pl.when(kk == 0)
            def _():
                acc[0][...] = p

            @pl.when(kk > 0)
            def _():
                acc[0][...] += p

            @pl.when(kk == nk - 1)
            def _():
                o_ref[...] = acc[0][...].astype(o_ref.dtype)

    return pl.pallas_call(
        body, name=name, grid=grid, in_specs=[a_spec, b_spec], out_specs=o_spec, out_shape=o_shape,
        scratch_shapes=[pltpu.VMEM(acc_tile, F32)] if nk > 1 else [],
        compiler_params=_cparams(dimension_semantics=("parallel", "parallel", "arbitrary")),
    )(a, b)


def _mm(a, b, mode, out_dtype, name, tm=1024, tn=1024, tk=2048):
    if mode == "nn":
        (m, k), (k2, n) = a.shape, b.shape
    elif mode == "nt":
        (m, k), (n, k2) = a.shape, b.shape
    else:
        (k, m), (k2, n) = a.shape, b.shape
    assert k == k2, (a.shape, b.shape, mode)
    tm, tn, tk = _divisor_tile(m, tm), _divisor_tile(n, tn), _divisor_tile(k, tk)
    if mode == "tn":
        a_spec = pl.BlockSpec((tk, tm), lambda i, j, kk: (kk, i))
    else:
        a_spec = pl.BlockSpec((tm, tk), lambda i, j, kk: (i, kk))
    if mode == "nt":
        b_spec = pl.BlockSpec((tn, tk), lambda i, j, kk: (j, kk))
    else:
        b_spec = pl.BlockSpec((tk, tn), lambda i, j, kk: (kk, j))
    return _mm_call(a, b, {"nn": _NN, "nt": _NT, "tn": _TN}[mode], a_spec, b_spec,
                    pl.BlockSpec((tm, tn), lambda i, j, kk: (i, j)), jax.ShapeDtypeStruct((m, n), out_dtype),
                    (m // tm, n // tn, k // tk), (tm, tn), name)


def _wblk_act_spec(rows, gb, nl, split, nb, row_axis, blk_axis):
    if split == 1:
        return pl.BlockSpec((rows, gb * nl), lambda *g: (g[row_axis], g[blk_axis]))
    groups = nb // split // gb
    return pl.BlockSpec((None, rows, gb * nl),
                        lambda *g: (g[blk_axis] // groups, g[row_axis], g[blk_axis] % groups))


def _mm_wblk(a, wb, out_dtype, name, *, gb, row_off=0, split=1, tm=1024):
    m, k = a.shape
    nb, _, nl = wb.shape
    assert (nb // split) % gb == 0
    tm = _divisor_tile(m, tm)

    def body(a_ref, b_ref, o_ref):
        av = a_ref[...].astype(BF16)
        for s in range(gb):
            o_ref[:, s * nl:(s + 1) * nl] = lax.dot_general(
                av, b_ref[s].astype(BF16), _NN, preferred_element_type=F32).astype(o_ref.dtype)

    o_shape = (m, nb * nl) if split == 1 else (split, m, nb // split * nl)
    return pl.pallas_call(
        body, name=name, grid=(nb // gb, m // tm),
        in_specs=[pl.BlockSpec((tm, k), lambda j, i: (i, 0)),
                  pl.BlockSpec((gb, k, nl), lambda j, i: (j, row_off, 0))],
        out_specs=_wblk_act_spec(tm, gb, nl, split, nb, 1, 0),
        out_shape=jax.ShapeDtypeStruct(o_shape, out_dtype),
        compiler_params=_cparams(dimension_semantics=("parallel", "parallel")),
    )(a, wb)


def _mm_wblk_dx(dy, wb, out_dtype, name, *, k, gb, row_off=0, split=1, tm=1024):
    nb, _, nl = wb.shape
    m = dy.shape[-2]
    tm = _divisor_tile(m, tm)
    nk = nb // gb
    per = nb // split
    whole = split > 1 and gb == nb
    assert whole or per % gb == 0

    def body(a_ref, b_ref, o_ref, *acc):
        p = None
        for s in range(gb):
            a_blk = a_ref[s // per, :, (s % per) * nl:(s % per + 1) * nl] if whole else a_ref[:, s * nl:(s + 1) * nl]
            q = lax.dot_general(a_blk.astype(BF16), b_ref[s].astype(BF16), _NT, preferred_element_type=F32)
            p = q if p is None else p + q
        if nk == 1:
            o_ref[...] = p.astype(o_ref.dtype)
        else:
            kk = pl.program_id(1)

            @pl.when(kk == 0)
            def _():
                acc[0][...] = p

            @pl.when(kk > 0)
            def _():
                acc[0][...] += p

            @pl.when(kk == nk - 1)
            def _():
                o_ref[...] = acc[0][...].astype(o_ref.dtype)

    return pl.pallas_call(
        body, name=name, grid=(m // tm, nk),
        in_specs=[pl.BlockSpec((split, tm, per * nl), lambda i, kk: (0, i, 0)) if whole
                  else _wblk_act_spec(tm, gb, nl, split, nb, 0, 1),
                  pl.BlockSpec((gb, k, nl), lambda i, kk: (kk, row_off, 0))],
        out_specs=pl.BlockSpec((tm, k), lambda i, kk: (i, 0)),
        out_shape=jax.ShapeDtypeStruct((m, k), out_dtype),
        scratch_shapes=[pltpu.VMEM((tm, k), F32)] if nk > 1 else [],
        compiler_params=_cparams(dimension_semantics=("parallel", "arbitrary")),
    )(dy, wb)


def _mm_wblk_dw(x, dy, name, *, nb, gb, split=1, tk=1024):
    t, k = x.shape
    assert (nb // split) % gb == 0
    nl = dy.shape[-1] * split // nb
    tk = _divisor_tile(t, tk)
    nk = t // tk

    def body(a_ref, b_ref, o_ref, *acc):
        kk = pl.program_id(1)
        av = a_ref[...].astype(BF16)
        for s in range(gb):
            p = lax.dot_general(av, b_ref[:, s * nl:(s + 1) * nl].astype(BF16), _TN, preferred_element_type=F32)
            if nk == 1:
                o_ref[s] = p.astype(o_ref.dtype)
                continue

            @pl.when(kk == 0)
            def _():
                acc[0][s] = p

            @pl.when(kk > 0)
            def _():
                acc[0][s] += p

        if nk > 1:
            @pl.when(kk == nk - 1)
            def _():
                o_ref[...] = acc[0][...].astype(o_ref.dtype)

    return pl.pallas_call(
        body, name=name, grid=(nb // gb, nk),
        in_specs=[pl.BlockSpec((tk, k), lambda j, kk: (kk, 0)), _wblk_act_spec(tk, gb, nl, split, nb, 1, 0)],
        out_specs=pl.BlockSpec((gb, k, nl), lambda j, kk: (j, 0, 0)),
        out_shape=jax.ShapeDtypeStruct((nb, k, nl), BF16),
        scratch_shapes=[pltpu.VMEM((gb, k, nl), F32)] if nk > 1 else [],
        compiler_params=_cparams(dimension_semantics=("parallel", "arbitrary")),
    )(x, dy)


def _row_specs(rows, tb, nsub):
    return [pl.BlockSpec((tb, nsub * cw), functools.partial(lambda i, off: (i, off), off=off))
            for (_, cw, off) in rows]


def _vec_specs(params):
    return [pl.BlockSpec(p.shape, lambda i: (0, 0)) for p in params]


def _row_fwd(f, rows, params, out_dtypes, *, nsub=1, tb, name):
    t = rows[0][0].shape[0]
    tb = min(tb, t)
    n_r, n_p = len(rows), len(params)
    blk = [jax.ShapeDtypeStruct((tb, cw), F32) for (_, cw, _) in rows]
    blk += [jax.ShapeDtypeStruct(p.shape, F32) for p in params]
    out_avals = jax.eval_shape(f, *blk)

    def body(*refs):
        pv = [r[...] for r in refs[n_r:n_r + n_p]]
        for s in range(nsub):
            vals = [r[:, s * cw:(s + 1) * cw].astype(F32) for r, (_, cw, _) in zip(refs[:n_r], rows)]
            outs = f(*vals, *pv)
            for o_ref, o in zip(refs[n_r + n_p:], outs):
                w = o.shape[1]
                o_ref[:, s * w:(s + 1) * w] = o.astype(o_ref.dtype)

    return pl.pallas_call(
        body, name=name,
        grid=(t // tb,),
        in_specs=_row_specs(rows, tb, nsub) + _vec_specs(params),
        out_specs=[pl.BlockSpec((tb, nsub * av.shape[1]), lambda i: (i, 0)) for av in out_avals],
        out_shape=[jax.ShapeDtypeStruct((t, nsub * av.shape[1]), dt) for av, dt in zip(out_avals, out_dtypes)],
        compiler_params=_cparams(dimension_semantics=("parallel",)),
    )(*[r[0] for r in rows], *params)


def _row_bwd(f, rows, params, cots, row_grad_dtypes, *, nsub=1, tb, name, add_to=None, cot_add=None):
    t = rows[0][0].shape[0]
    tb = min(tb, t)
    n_r, n_p, n_c = len(rows), len(params), len(cots)
    want = [j for j in range(n_r) if row_grad_dtypes[j] is not None]
    cot_add = cot_add or []
    extra = [] if add_to is None else [(add_to[1], rows[add_to[0]][1], 0)]
    n_add_to = len(extra)
    extra += [(arr, cots[ci][1], 0) for ci, arr in cot_add]

    def body(*refs):
        i = pl.program_id(0)
        r_in, p_in = refs[:n_r], refs[n_r:n_r + n_p]
        c_in = refs[n_r + n_p:n_r + n_p + n_c]
        e_in = refs[n_r + n_p + n_c:n_r + n_p + n_c + len(extra)]
        outs = refs[n_r + n_p + n_c + len(extra):]
        pv = [r[...] for r in p_in]
        psum = [None] * n_p
        for s in range(nsub):
            vals = [r[:, s * cw:(s + 1) * cw].astype(F32) for r, (_, cw, _) in zip(r_in, rows)]
            cvals = [r[:, s * cw:(s + 1) * cw].astype(F32) for r, (_, cw, _) in zip(c_in, cots)]
            for (ci, _), e_ref in zip(cot_add, e_in[n_add_to:]):
                cw = cots[ci][1]
                cvals[ci] = cvals[ci] + e_ref[:, s * cw:(s + 1) * cw].astype(F32)
            _, vjp_fn = jax.vjp(f, *vals, *pv)
            grads = vjp_fn(tuple(cvals))
            for o_ref, jr in zip(outs[:len(want)], want):
                cw = rows[jr][1]
                gr = grads[jr]
                if add_to is not None and jr == add_to[0]:
                    gr = gr + e_in[0][:, s * cw:(s + 1) * cw]
                o_ref[:, s * cw:(s + 1) * cw] = gr.astype(o_ref.dtype)
            for jp in range(n_p):
                psum[jp] = grads[n_r + jp] if psum[jp] is None else psum[jp] + grads[n_r + jp]
        for o_ref, g in zip(outs[len(want):], psum):
            @pl.when(i == 0)
            def _():
                o_ref[...] = g

            @pl.when(i > 0)
            def _():
                o_ref[...] += g

    out_specs = [pl.BlockSpec((tb, nsub * rows[jr][1]), lambda i: (i, 0)) for jr in want]
    out_shape = [jax.ShapeDtypeStruct((t, nsub * rows[jr][1]), row_grad_dtypes[jr]) for jr in want]
    out_specs += _vec_specs(params)
    out_shape += [jax.ShapeDtypeStruct(p.shape, F32) for p in params]
    res = pl.pallas_call(
        body, name=name,
        grid=(t // tb,),
        in_specs=_row_specs(rows, tb, nsub) + _vec_specs(params) + _row_specs(cots, tb, nsub)
        + _row_specs(extra, tb, nsub),
        out_specs=out_specs, out_shape=out_shape,
        compiler_params=_cparams(dimension_semantics=("arbitrary",)),
    )(*[r[0] for r in rows], *params, *[c[0] for c in cots], *[e[0] for e in extra])
    return res[:len(want)], res[len(want):]


def _f_mod(x, sh, sc):
    return (_modulate(x, sh, sc),)


def _f_res_mod(x, y, g, sh, sc):
    x1 = x + g * y
    return x1, _modulate(x1, sh, sc)


def _f_res_mod2(x, y, g, sh_a, sc_a, sh_b, sc_b):
    x1 = x + g * y
    return x1, _modulate(x1, sh_a, sc_a), _modulate(x1, sh_b, sc_b)


def _f_qnorm(p, g):
    return (_rms(p) * g * (HEAD ** -0.5),)


def _f_knorm(p, g):
    return (_rms(p) * g,)


def _f_qnorm_aug(p, g):
    lane = lax.broadcasted_iota(jnp.int32, p.shape, 1)
    return (jnp.concatenate([_rms(p) * g * (HEAD ** -0.5), jnp.where(lane < 3, 1.0, 0.0)], axis=1),)


def _f_knorm_aug(p, c0, c1, c2, g):
    lane = lax.broadcasted_iota(jnp.int32, p.shape, 1)
    aug = jnp.where(lane == 0, c0, jnp.where(lane == 1, c1, jnp.where(lane == 2, c2, 0.0)))
    return (jnp.concatenate([_rms(p) * g, aug], axis=1),)


def _split3(a):
    round_bf16 = lambda v: lax.reduce_precision(v, exponent_bits=8, mantissa_bits=7)
    hi = round_bf16(a)
    mid = round_bf16(a - hi)
    lo = round_bf16(a - hi - mid)
    return hi.astype(BF16), mid.astype(BF16), lo.astype(BF16)


def _f_outgate(o, og):
    return (o * _sigmoid(og),)


def _loss_call(x3, f, g2, target, tb):
    t, d = x3.shape
    tb = min(tb, t)

    def body(x_ref, f_ref, g_ref, t_ref, loss_ref, dx_ref, df_ref, dg_ref):
        i = pl.program_id(0)
        fv = f_ref[...]
        g = g_ref[...]
        e = x_ref[...] + g * fv - t_ref[...]
        dx = e * (1.0 / d)
        part = 0.5 * jnp.sum(jnp.sum(e * dx, axis=1, keepdims=True), axis=0, keepdims=True)
        dx_ref[...] = dx
        df_ref[...] = (g * dx).astype(df_ref.dtype)
        dg = jnp.sum(dx * fv, axis=0, keepdims=True)

        @pl.when(i == 0)
        def _():
            loss_ref[...] = jnp.broadcast_to(part, loss_ref.shape)
            dg_ref[...] = dg

        @pl.when(i > 0)
        def _():
            loss_ref[...] += jnp.broadcast_to(part, loss_ref.shape)
            dg_ref[...] += dg

    row = pl.BlockSpec((tb, d), lambda i: (i, 0))
    vec = pl.BlockSpec((1, d), lambda i: (0, 0))
    return pl.pallas_call(
        body, name="loss_head",
        grid=(t // tb,),
        in_specs=[row, row, vec, row],
        out_specs=[pl.BlockSpec((1, LANES), lambda i: (0, 0)), row, row, vec],
        out_shape=[jax.ShapeDtypeStruct((1, LANES), F32), jax.ShapeDtypeStruct((t, d), F32),
                   jax.ShapeDtypeStruct((t, d), BF16), jax.ShapeDtypeStruct((1, d), F32)],
        compiler_params=_cparams(dimension_semantics=("arbitrary",)),
    )(x3, f, g2, target)


def _hg_mask(tb):
    br = lax.broadcasted_iota(jnp.int32, (tb, tb), 0)
    bs = lax.broadcasted_iota(jnp.int32, (tb, tb), 1)
    return jnp.logical_and(br // A_CHUNK == bs // A_CHUNK, bs <= br).astype(F32)


def _hg_consts(mask):
    c = A_CHUNK
    r = lax.broadcasted_iota(jnp.int32, (c, c), 0)
    s = lax.broadcasted_iota(jnp.int32, (c, c), 1)
    return (s <= r).astype(F32), (r <= s).astype(F32), mask > 0.5


def _chunk_apply(mat, x):
    c = mat.shape[0]
    return jnp.concatenate([_f32dot(mat, x[i * c:(i + 1) * c]) for i in range(x.shape[0] // c)], axis=0)


@jax.custom_vjp
def _chunk_cumsum(x, tri, tri_t):
    return _chunk_apply(tri, x)


_chunk_cumsum.defvjp(lambda x, tri, tri_t: (_chunk_apply(tri, x), (tri, tri_t)),
                     lambda r, g: (_chunk_apply(r[1], g), jnp.zeros_like(r[0]), jnp.zeros_like(r[1])))


def _per_chunk(a, b, dims):
    return jnp.stack([_bdot_raw(a[i], b[i], dims) for i in range(a.shape[0])])


@jax.custom_vjp
def _chunk_tn(a, b):
    return _per_chunk(a, b, _TN)


@jax.custom_vjp
def _chunk_nt(a, b):
    return _per_chunk(a, b, _NT)


@jax.custom_vjp
def _chunk_nn(a, b):
    return _per_chunk(a, b, _NN)


_chunk_tn.defvjp(lambda a, b: (_per_chunk(a, b, _TN), (a, b)),
                 lambda r, g: (_chunk_nt(r[1], g), _chunk_nn(r[0], g)))
_chunk_nt.defvjp(lambda a, b: (_per_chunk(a, b, _NT), (a, b)),
                 lambda r, g: (_chunk_nn(g, r[1]), _chunk_tn(g, r[0])))
_chunk_nn.defvjp(lambda a, b: (_per_chunk(a, b, _NN), (a, b)),
                 lambda r, g: (_chunk_nt(g, r[1]), _chunk_tn(r[0], g)))


def _scan_states(decay, m, st):
    sts = []
    for i in range(m.shape[0]):
        sts.append(st)
        st = st * decay[i] + m[i]
    return jnp.stack(sts), st


@jax.custom_vjp
def _state_scan(decay, m, st):
    return _scan_states(decay, m, st)


def _state_scan_fwd(decay, m, st):
    sts, st_out = _scan_states(decay, m, st)
    return (sts, st_out), (decay, sts)


def _state_scan_bwd(res, cts):
    decay, sts = res
    d_sts, g = cts
    d_decay, d_m = [], []
    for i in range(sts.shape[0] - 1, -1, -1):
        d_m.append(g)
        d_decay.append(jnp.sum(g * sts[i], axis=0, keepdims=True))
        g = g * decay[i] + d_sts[i]
    return jnp.stack(d_decay[::-1]), jnp.stack(d_m[::-1]), g


_state_scan.defvjp(_state_scan_fwd, _state_scan_bwd)


def _hg_block(qp, fp, ip, gp, lb, ng, st, tri, tri_t, bd_causal):
    tb = qp.shape[0]
    c = A_CHUNK
    n = tb // c
    q = _silu(qp)
    fg = lb + (1.0 - lb) * _sigmoid(fp)
    logf = jnp.log(fg)
    k = 1.0 - fg
    b3 = _chunk_cumsum(logf, tri, tri_t).reshape(n, c, HEAD)
    pos = lax.broadcasted_iota(jnp.int32, (1, c, 1), 1)
    b_mid = lax.stop_gradient(jnp.sum(jnp.where(pos == c // 2, b3, 0.0), axis=1, keepdims=True))
    b_last = jnp.sum(jnp.where(pos == c - 1, b3, 0.0), axis=1, keepdims=True)
    q3, k3, v3 = q.reshape(n, c, HEAD), k.reshape(n, c, HEAD), ip.reshape(n, c, HEAD)
    scores = _dot_nt((q3 * jnp.exp(b3 - b_mid)).reshape(tb, HEAD), (k3 * jnp.exp(b_mid - b3)).reshape(tb, HEAD))
    o_intra = _dot_nn(jnp.where(bd_causal, scores, 0.0), ip)
    states, st_new = _state_scan(jnp.exp(b_last), _chunk_tn(v3, k3 * jnp.exp(b_last - b3)), st)
    o = o_intra + _chunk_nt(q3 * jnp.exp(b3), states).reshape(tb, HEAD)
    y = _rms(o) * ng * _silu(gp)
    return y, st_new


HG_HEADS = 2


def _hg_specs(tb, nh, rev_nb=None):
    wide = HG_HEADS * HEAD
    per = nh // HG_HEADS

    def row(part):
        if rev_nb is None:
            return pl.BlockSpec((tb, wide), functools.partial(lambda h, i, off: (i, off + h), off=part * per))
        return pl.BlockSpec((tb, wide),
                            functools.partial(lambda h, i, off: (rev_nb - 1 - i, off + h), off=part * per))
    return [row(0), row(1), row(2), row(3),
            pl.BlockSpec((1, wide), lambda h, i: (0, h)), pl.BlockSpec((1, HEAD), lambda h, i: (0, 0)),
            pl.BlockSpec((tb, tb), lambda h, i: (0, 0))]


def _hgrn2_fwd(proj, lb, ng, tb):
    t = proj.shape[0]
    nh = proj.shape[1] // (4 * HEAD)
    tb = min(tb, t)
    nb = t // tb
    wide = HG_HEADS * HEAD

    def body(q_ref, f_ref, i_ref, g_ref, lb_ref, ng_ref, mask_ref, y_ref, s_ref, st_ref):
        i = pl.program_id(1)

        @pl.when(i == 0)
        def _():
            st_ref[...] = jnp.zeros_like(st_ref)

        consts = _hg_consts(mask_ref[...])
        for p in range(HG_HEADS):
            cs = slice(p * HEAD, (p + 1) * HEAD)
            st = st_ref[p]
            s_ref[p, 0] = st
            y, st_new = _hg_block(q_ref[:, cs], f_ref[:, cs], i_ref[:, cs], g_ref[:, cs], lb_ref[:, cs],
                                  ng_ref[...], st, *consts)
            y_ref[:, cs] = y.astype(y_ref.dtype)
            st_ref[p] = st_new

    return pl.pallas_call(
        body, name="hgrn2_fwd",
        grid=(nh // HG_HEADS, nb),
        in_specs=_hg_specs(tb, nh),
        out_specs=[pl.BlockSpec((tb, wide), lambda h, i: (i, h)),
                   pl.BlockSpec((HG_HEADS, 1, HEAD, HEAD), lambda h, i: (h, i, 0, 0))],
        out_shape=[jax.ShapeDtypeStruct((t, nh * HEAD), BF16),
                   jax.ShapeDtypeStruct((nh, nb, HEAD, HEAD), F32)],
        scratch_shapes=[pltpu.VMEM((HG_HEADS, HEAD, HEAD), F32)],
        compiler_params=_cparams(dimension_semantics=("parallel", "arbitrary")),
    )(proj, proj, proj, proj, lb, ng, _hg_mask(tb))


def _hgrn2_bwd(proj, lb, ng, states, dy, tb):
    t = proj.shape[0]
    nh = proj.shape[1] // (4 * HEAD)
    tb = min(tb, t)
    nb = t // tb
    wide = HG_HEADS * HEAD

    def body(q_ref, f_ref, i_ref, g_ref, lb_ref, ng_ref, mask_ref, s_ref, dy_ref,
             dp_ref, dlb_ref, dng_ref, dst_ref):
        h, i = pl.program_id(0), pl.program_id(1)
        consts = _hg_consts(mask_ref[...])

        @pl.when(i == 0)
        def _():
            dst_ref[...] = jnp.zeros_like(dst_ref)
            dlb_ref[...] = jnp.zeros_like(dlb_ref)

        @pl.when(jnp.logical_and(i == 0, h == 0))
        def _():
            dng_ref[...] = jnp.zeros_like(dng_ref)

        def fn(qp, fp, ip, gp, lbx, ngx, stx):
            return _hg_block(qp, fp, ip, gp, lbx, ngx, stx, *consts)

        for p in range(HG_HEADS):
            cs = slice(p * HEAD, (p + 1) * HEAD)
            _, vjp_fn = jax.vjp(fn, q_ref[:, cs], f_ref[:, cs], i_ref[:, cs], g_ref[:, cs], lb_ref[:, cs],
                                ng_ref[...], s_ref[p, 0])
            *gparts, glb, gng, dst = vjp_fn((dy_ref[:, cs].astype(F32), dst_ref[p]))
            for part, gpart in enumerate(gparts):
                dp_ref[part, :, cs] = gpart.astype(dp_ref.dtype)
            dst_ref[p] = dst
            dlb_ref[:, cs] += glb
            dng_ref[...] += gng

    rev = lambda h, i: (nb - 1 - i, h)
    return pl.pallas_call(
        body, name="hgrn2_bwd",
        grid=(nh // HG_HEADS, nb),
        in_specs=_hg_specs(tb, nh, rev_nb=nb) + [
            pl.BlockSpec((HG_HEADS, 1, HEAD, HEAD), lambda h, i: (h, nb - 1 - i, 0, 0)),
            pl.BlockSpec((tb, wide), rev)],
        out_specs=[pl.BlockSpec((4, tb, wide), lambda h, i: (0, nb - 1 - i, h)),
                   pl.BlockSpec((1, wide), lambda h, i: (0, h)), pl.BlockSpec((1, HEAD), lambda h, i: (0, 0))],
        out_shape=[jax.ShapeDtypeStruct((4, t, nh * HEAD), BF16),
                   jax.ShapeDtypeStruct((1, nh * HEAD), F32), jax.ShapeDtypeStruct((1, HEAD), F32)],
        scratch_shapes=[pltpu.VMEM((HG_HEADS, HEAD, HEAD), F32)],
        compiler_params=_cparams(dimension_semantics=("arbitrary", "arbitrary")),
    )(proj, proj, proj, proj, lb, ng, _hg_mask(tb), states, dy)


def _fgate_consts(cb):
    r = lax.broadcasted_iota(jnp.int32, (cb, cb), 0)
    s = lax.broadcasted_iota(jnp.int32, (cb, cb), 1)
    return (r <= s).astype(F32), (r >= s).astype(F32)


def _fgate_fwd(xt, bias, cb=512):
    nh, t = xt.shape
    cb = min(cb, t)

    def body(x_ref, b_ref, o_ref):
        upper, _ = _fgate_consts(cb)
        carry = jnp.zeros((nh, 1), F32)
        for blk in range(t // cb):
            z = x_ref[:, blk * cb:(blk + 1) * cb] + b_ref[...]
            logf = jnp.minimum(z, 0.0) - jnp.log(1.0 + jnp.exp(-jnp.abs(z)))
            cs = _f32dot(logf, upper) + carry
            o_ref[:, blk * cb:(blk + 1) * cb] = cs
            carry = cs[:, cb - 1:cb]

    vm = pl.BlockSpec(memory_space=pltpu.VMEM)
    return pl.pallas_call(
        body, name="fgate_fwd", in_specs=[vm, vm], out_specs=vm,
        out_shape=jax.ShapeDtypeStruct((nh, t), F32), compiler_params=_cparams(),
    )(xt, bias)


def _fgate_bwd(xt, bias, dft, cb=512):
    nh, t = xt.shape
    cb = min(cb, t)
    nblk = t // cb

    def body(x_ref, b_ref, d_ref, dx_ref, db_ref):
        _, lower = _fgate_consts(cb)
        carry = jnp.zeros((nh, 1), F32)
        db = jnp.zeros((nh, 1), F32)
        for blk in range(nblk - 1, -1, -1):
            sl = slice(blk * cb, (blk + 1) * cb)
            dlogf = _f32dot(d_ref[:, sl], lower) + carry
            carry = dlogf[:, 0:1]
            z = x_ref[:, sl] + b_ref[...]
            dz = dlogf * (1.0 - _sigmoid(z))
            dx_ref[:, sl] = dz
            db = db + jnp.sum(dz, axis=1, keepdims=True)
        db_ref[...] = db

    vm = pl.BlockSpec(memory_space=pltpu.VMEM)
    return pl.pallas_call(
        body, name="fgate_bwd", in_specs=[vm, vm, vm], out_specs=[vm, vm],
        out_shape=[jax.ShapeDtypeStruct((nh, t), F32), jax.ShapeDtypeStruct((nh, 1), F32)],
        compiler_params=_cparams(),
    )(xt, bias, dft)


ATTN_GROUPS = 4


def _attn_fwd(q, k, v, f_grp, blk):
    t, width = v.shape
    nh = width // HEAD
    nq = t // blk
    hpg = nh // ATTN_GROUPS

    def body(q_ref, k_ref, v_ref, fc_ref, o_ref, lse_ref):
        i = pl.program_id(0)
        tri = (lax.broadcasted_iota(jnp.int32, (blk, blk), 1) <= lax.broadcasted_iota(jnp.int32, (blk, blk), 0))
        for h in range(nh):
            cs = slice(h * HEAD, (h + 1) * HEAD)
            cs2 = slice(2 * h * HEAD, 2 * (h + 1) * HEAD)
            qh = q_ref[:, cs2]

            def tile(j, carry, masked):
                m, l, acc = carry
                rs = pl.ds(pl.multiple_of(j * blk, blk), blk)
                s = _bdot_raw(qh, k_ref[rs, cs2], _NT)
                if masked:
                    s = jnp.where(tri, s, NEG_INF)
                m_new = jnp.maximum(m, jnp.max(s, axis=1, keepdims=True))
                p = jnp.exp(s - m_new)
                alpha = jnp.exp(m - m_new)
                l_new = alpha * l + jnp.sum(p, axis=1, keepdims=True)
                acc_new = alpha * acc + _bdot_raw(p, v_ref[rs, cs], _NN)
                return m_new, l_new, acc_new

            init = (jnp.full((blk, 1), NEG_INF, F32), jnp.zeros((blk, 1), F32), jnp.zeros((blk, HEAD), F32))
            carry = lax.fori_loop(0, i, lambda j, c: tile(j, c, False), init)
            m, l, acc = tile(i, carry, True)
            o_ref[:, cs] = acc / l
            g, hh = divmod(h, hpg)
            lse_ref[g, :, hh:hh + 1] = m + jnp.log(l) + fc_ref[g, :, hh:hh + 1]

    vm = pl.BlockSpec(memory_space=pltpu.VMEM)
    stat = pl.BlockSpec((ATTN_GROUPS, blk, hpg), lambda i: (0, i, 0))
    return pl.pallas_call(
        body, name="fox_attn_fwd",
        grid=(nq,),
        in_specs=[pl.BlockSpec((blk, 2 * width), lambda i: (i, 0)), vm, vm, stat],
        out_specs=[pl.BlockSpec((blk, width), lambda i: (i, 0)), stat],
        out_shape=[jax.ShapeDtypeStruct((t, width), F32), jax.ShapeDtypeStruct((ATTN_GROUPS, t, hpg), F32)],
        compiler_params=_cparams(dimension_semantics=("parallel",)),
    )(q, k, v, f_grp)


def _attn_delta(do, o, tb):
    t, width = o.shape
    nh = width // HEAD
    hpg = nh // ATTN_GROUPS
    tb = min(tb, t)

    def body(do_ref, o_ref, dl_ref):
        for h in range(nh):
            cs = slice(h * HEAD, (h + 1) * HEAD)
            g, hh = divmod(h, hpg)
            dl_ref[g, :, hh:hh + 1] = jnp.sum(do_ref[:, cs].astype(F32) * o_ref[:, cs], axis=1, keepdims=True)

    wide = pl.BlockSpec((tb, width), lambda i: (i, 0))
    return pl.pallas_call(body, name="fox_attn_delta", grid=(t // tb,), in_specs=[wide, wide],
                          out_specs=pl.BlockSpec((ATTN_GROUPS, tb, hpg), lambda i: (0, i, 0)),
                          out_shape=jax.ShapeDtypeStruct((ATTN_GROUPS, t, hpg), F32),
                          compiler_params=_cparams(dimension_semantics=("parallel",)))(do, o)


def _attn_bwd(q, k, v, f_grp, do, lse, delta, blk):
    t, width = v.shape
    nh = width // HEAD
    nq = t // blk
    hpg = nh // ATTN_GROUPS
    gw = hpg * HEAD

    def body(q_ref, do_ref, k_ref, v_ref, fc_ref, lse_ref, dl_ref,
             dq_ref, dk_ref, dv_ref, dfc_ref, dfr_ref):
        g, j = pl.program_id(0), pl.program_id(1)
        tri = (lax.broadcasted_iota(jnp.int32, (blk, blk), 1) <= lax.broadcasted_iota(jnp.int32, (blk, blk), 0))

        @pl.when(j == 0)
        def _():
            dq_ref[...] = jnp.zeros_like(dq_ref)
            dfc_ref[...] = jnp.zeros_like(dfc_ref)

        for h in range(hpg):
            cs = slice(h * HEAD, (h + 1) * HEAD)
            cs2 = slice(2 * h * HEAD, 2 * (h + 1) * HEAD)
            csq = slice(2 * h * HEAD, (2 * h + 1) * HEAD)
            kj2 = k_ref[:, cs2]
            kj = k_ref[:, csq]
            vj = v_ref[:, cs]

            def tile(i, carry, masked):
                dk, dv, dfs = carry
                rs = pl.ds(pl.multiple_of(i * blk, blk), blk)
                qi = q_ref[rs, csq]
                doi = do_ref[rs, cs]
                bias = fc_ref[0, rs, h:h + 1] - lse_ref[0, rs, h:h + 1]
                p = jnp.exp(_bdot_raw(q_ref[rs, cs2], kj2, _NT) + bias)
                if masked:
                    p = jnp.where(tri, p, 0.0)
                ds = p * (_bdot_raw(doi, vj, _NT) - dl_ref[0, rs, h:h + 1])
                dsb = ds.astype(BF16)
                dq_ref[rs, cs] += _bdot_raw(dsb, kj, _NN)
                dfc_ref[0, rs, h:h + 1] += jnp.sum(ds, axis=1, keepdims=True)
                return (dk + _bdot_raw(dsb, qi, _TN), dv + _bdot_raw(p, doi, _TN),
                        dfs - jnp.sum(ds, axis=0, keepdims=True))

            init = (jnp.zeros((blk, HEAD), F32), jnp.zeros((blk, HEAD), F32), jnp.zeros((1, blk), F32))
            carry = tile(j, init, True)
            dk, dv, dfs = lax.fori_loop(j + 1, nq, lambda i, c: tile(i, c, False), carry)
            dk_ref[:, cs] = dk
            dv_ref[:, cs] = dv.astype(dv_ref.dtype)
            dfr_ref[0, 0, h:h + 1, :] = dfs

    once = pl.Buffered(1)
    stat = pl.BlockSpec((1, t, hpg), lambda g, j: (g, 0, 0), pipeline_mode=once)
    kv_blk = pl.BlockSpec((blk, gw), lambda g, j: (j, g))
    frow = pl.BlockSpec((1, 1, hpg, blk), lambda g, j: (g, j, 0, 0))
    dq, dk, dv, dfc, dfr = pl.pallas_call(
        body, name="fox_attn_bwd",
        grid=(ATTN_GROUPS, nq),
        in_specs=[pl.BlockSpec((t, 2 * gw), lambda g, j: (0, g), pipeline_mode=once),
                  pl.BlockSpec((t, gw), lambda g, j: (0, g), pipeline_mode=once),
                  pl.BlockSpec((blk, 2 * gw), lambda g, j: (j, g)), kv_blk, stat, stat, stat],
        out_specs=[pl.BlockSpec((t, gw), lambda g, j: (0, g)), kv_blk, kv_blk,
                   pl.BlockSpec((1, t, hpg), lambda g, j: (g, 0, 0)), frow],
        out_shape=[jax.ShapeDtypeStruct((t, width), F32), jax.ShapeDtypeStruct((t, width), F32),
                   jax.ShapeDtypeStruct((t, width), BF16), jax.ShapeDtypeStruct((ATTN_GROUPS, t, hpg), F32),
                   jax.ShapeDtypeStruct((ATTN_GROUPS, nq, hpg, blk), F32)],
        compiler_params=_cparams(dimension_semantics=("parallel", "arbitrary")),
    )(q, do, k, v, f_grp, lse, delta)
    return dq, dk, dv, dfc, dfr


SUBLANES = 8


def _shift_down(u, n):
    r = pltpu.roll(u, n, 0)
    row = lax.broadcasted_iota(jnp.int32, (SUBLANES, u.shape[1]), 0)
    return jnp.concatenate([jnp.where(row < n, 0.0, r[:SUBLANES]), r[SUBLANES:]], axis=0)


def _shift_up(u, n):
    t = u.shape[0]
    r = pltpu.roll(u, t - n, 0)
    row = lax.broadcasted_iota(jnp.int32, (SUBLANES, u.shape[1]), 0)
    return jnp.concatenate([r[:t - SUBLANES], jnp.where(row >= SUBLANES - n, 0.0, r[t - SUBLANES:])], axis=0)


def _convglu_specs(t):
    return [pl.BlockSpec((2, t, LANES), lambda j: (0, 0, j)),
            pl.BlockSpec((2, CONV_TAPS, LANES), lambda j: (0, 0, j)),
            pl.BlockSpec((2, 1, LANES), lambda j: (0, 0, j))]


def _convglu_fwd(u, cw, cb):
    _, t, fp = u.shape

    def body(u_ref, w_ref, b_ref, a_ref, c_ref):
        c = []
        for hf in range(2):
            uv, w = u_ref[hf].astype(F32), w_ref[hf]
            c.append(w[0:1] * _shift_down(uv, 2) + w[1:2] * _shift_down(uv, 1) + w[2:3] * uv + b_ref[hf])
            c_ref[hf] = c[hf].astype(c_ref.dtype)
        a_ref[...] = (_silu(c[0]) * c[1]).astype(a_ref.dtype)

    return pl.pallas_call(
        body, name="convglu_fwd",
        grid=(fp // LANES,),
        in_specs=_convglu_specs(t),
        out_specs=[pl.BlockSpec((t, LANES), lambda j: (0, j)), pl.BlockSpec((2, t, LANES), lambda j: (0, 0, j))],
        out_shape=[jax.ShapeDtypeStruct((t, fp), BF16), jax.ShapeDtypeStruct((2, t, fp), BF16)],
        compiler_params=_cparams(dimension_semantics=("parallel",)),
    )(u, cw, cb)


def _convglu_bwd(u, c, cw, da):
    _, t, fp = u.shape

    def body(u_ref, c_ref, w_ref, da_ref, du_ref, dw_ref, db_ref):
        gc, vc = c_ref[0].astype(F32), c_ref[1].astype(F32)
        sg = _sigmoid(gc)
        dav = da_ref[...].astype(F32)
        dcs = [dav * vc * (sg * (1.0 + gc * (1.0 - sg))), dav * (gc * sg)]
        for hf in range(2):
            dc, w, uv = dcs[hf], w_ref[hf], u_ref[hf].astype(F32)
            dc1, dc2 = _shift_up(dc, 1), _shift_up(dc, 2)
            du_ref[hf] = (w[2:3] * dc + w[1:2] * dc1 + w[0:1] * dc2).astype(du_ref.dtype)
            dw_ref[hf, 0:1, :] = jnp.sum(dc2 * uv, axis=0, keepdims=True)
            dw_ref[hf, 1:2, :] = jnp.sum(dc1 * uv, axis=0, keepdims=True)
            dw_ref[hf, 2:3, :] = jnp.sum(dc * uv, axis=0, keepdims=True)
            db_ref[hf] = jnp.sum(dc, axis=0, keepdims=True)

    pair, taps, bias = _convglu_specs(t)
    return pl.pallas_call(
        body, name="convglu_bwd",
        grid=(fp // LANES,),
        in_specs=[pair, pair, taps, pl.BlockSpec((t, LANES), lambda j: (0, j))],
        out_specs=[pair, taps, bias],
        out_shape=[jax.ShapeDtypeStruct((2, t, fp), BF16), jax.ShapeDtypeStruct((2, CONV_TAPS, fp), F32),
                   jax.ShapeDtypeStruct((2, 1, fp), F32)],
        compiler_params=_cparams(dimension_semantics=("parallel",)),
    )(u, c, cw, da)


def _local_step(x, target, mods, lb, small, pre_w, get_w, put_g, *, tb=512, attn_blk=512):
    t, d = x.shape
    nh = d // HEAD
    nb = NDEV
    wts = {}
    vec = lambda *names: [mods[n] for n in names]

    def ffn_fwd(h2, l):
        u = _mm_wblk(h2, wts[f"up{l}"], BF16, f"ffn{l}_up", gb=nb // 2, split=2, tm=512)
        a, c = _convglu_fwd(u, small[f"conv_w{l}"], small[f"conv_b{l}"])
        f = _mm(a, wts[f"down{l}"], "nn", F32, f"ffn{l}_down", tk=4096)
        return (u, c), a, f

    def ffn_bwd(df, h2, uc, a, l):
        u, c = uc
        da = _mm(df, wts[f"down{l}"], "nt", BF16, f"ffn{l}_down_dx", tn=1536)
        dwd = _mm(a, df, "tn", BF16, f"ffn{l}_down_dw", tm=768, tk=t)
        du, dcw, dcb = _convglu_bwd(u, c, small[f"conv_w{l}"], da)
        dh2 = _mm_wblk_dx(du, wts[f"up{l}"], BF16, f"ffn{l}_up_dx", k=d, gb=nb // 2, split=2, tm=1024)
        dwu = _mm_wblk_dw(h2, du, f"ffn{l}_up_dw", nb=nb, gb=1, split=2, tk=t)
        return dh2, dwu, dwd, dcw, dcb

    (h_a,) = _row_fwd(_f_mod, [(x, d, 0)], vec("sh1_0", "sc1_0"), [BF16], tb=tb, name="l0_mod1")
    wts.update(get_w("l0a", h_a))
    proj_a = _mm_wblk(h_a, wts["a_in"], F32, "a_in", gb=nb // 2)
    ypre, states = _hgrn2_fwd(proj_a, lb, small["a_norm_g"], tb)
    pre_w("l0b", ypre)
    wts.update(get_w("l0b", ypre))
    y_a = _mm(ypre, wts["a_out"], "nn", F32, "a_out")
    x1, h2_0 = _row_fwd(_f_res_mod, [(x, d, 0), (y_a, d, 0)], vec("g1_0", "sh2_0", "sc2_0"), [F32, BF16],
                        tb=tb, name="l0_res_mod2")
    u0, a0, f0 = ffn_fwd(h2_0, 0)
    x2, h_kv, h_q = _row_fwd(_f_res_mod2, [(x1, d, 0), (f0, d, 0)],
                             [mods["g2_0"] + pre_w("l1", f0)] + vec("kv_sh", "kv_sc", "sh1_1", "sc1_1"),
                             [F32, BF16, BF16], tb=tb, name="l0_res_kvmod_qmod")
    wts.update(get_w("l1", h_kv))
    proj_k = _mm(h_kv, wts["kv_k"], "nt", F32, "k_proj")
    v_b = _mm(h_kv, wts["kv_v"], "nt", BF16, "v_proj")
    proj_f = _mm(h_kv, wts["kv_f"], "nt", F32, "kv_fproj")
    f_logit_t = proj_f[:, :nh].T
    f_bias = small["kv_b_f"].reshape(nh, 1)
    f_t = _fgate_fwd(f_logit_t, f_bias)
    f_grp = f_t.reshape(ATTN_GROUPS, nh // ATTN_GROUPS, t).transpose(0, 2, 1)
    (k_n,) = _row_fwd(_f_knorm_aug, [(proj_k, HEAD, 0)] + [(piece, 1, 0) for piece in _split3(-f_t.T)],
                      [small["k_norm_g"]], [BF16], nsub=nh, tb=tb, name="k_norm")
    proj_q = _mm_wblk(h_q, wts["b_q"], F32, "b_q", gb=nb)
    (q_n,) = _row_fwd(_f_qnorm_aug, [(proj_q, HEAD, 0)], [small["q_norm_g"]], [BF16], nsub=nh, tb=tb,
                      name="q_norm")
    o_att, lse = _attn_fwd(q_n, k_n, v_b, f_grp, attn_blk)
    (z,) = _row_fwd(_f_outgate, [(o_att, HEAD, 0), (proj_q, HEAD, 1)], [], [BF16], nsub=nh, tb=tb, name="out_gate")
    y_b = _mm(z, wts["b_out"], "nn", F32, "b_out")
    x3, h2_1 = _row_fwd(_f_res_mod, [(x2, d, 0), (y_b, d, 0)], vec("g1_1", "sh2_1", "sc2_1"), [F32, BF16],
                        tb=tb, name="l1_res_mod2")
    u1, a1, f1 = ffn_fwd(h2_1, 1)
    loss, dx4, df1, dg2_1 = _loss_call(x3, f1, mods["g2_1"], target, tb)

    g = {}
    dmods = {"g2_1": dg2_1}
    dh2, g["up1"], g["down1"], g["conv_w1"], g["conv_b1"] = ffn_bwd(df1, h2_1, u1, a1, 1)
    (dx2, dy_b), (dmods["g1_1"], dmods["sh2_1"], dmods["sc2_1"]) = _row_bwd(
        _f_res_mod, [(x2, d, 0), (y_b, d, 0)], vec("g1_1", "sh2_1", "sc2_1"),
        [(dx4, d, 0), (dh2, d, 0)], [F32, BF16], tb=tb, name="l1_res_mod2_bwd")
    dz = _mm(dy_b, wts["b_out"], "nt", BF16, "b_out_dx")
    g["b_out"] = _mm(z, dy_b, "tn", BF16, "b_out_dw", tk=t)
    (do_att, dog), _ = _row_bwd(_f_outgate, [(o_att, HEAD, 0), (proj_q, HEAD, 1)], [], [(dz, HEAD, 0)],
                                [BF16, BF16], nsub=nh, tb=tb, name="out_gate_bwd")
    delta = _attn_delta(do_att, o_att, tb)
    dq_n, dk_n, dv, dfc_q, dfr_k = _attn_bwd(q_n, k_n, v_b, f_grp, do_att, lse, delta, attn_blk)
    (dpq,), (g["q_norm_g"],) = _row_bwd(_f_qnorm, [(proj_q, HEAD, 0)], [small["q_norm_g"]],
                                        [(dq_n, HEAD, 0)], [BF16], nsub=nh, tb=tb, name="q_norm_bwd")
    dproj_q = jnp.concatenate([dpq, dog], axis=1)
    dh_q = _mm_wblk_dx(dproj_q, wts["b_q"], BF16, "b_q_dx", k=d, gb=nb)
    g["b_q"] = _mm_wblk_dw(h_q, dproj_q, "b_q_dw", nb=nb, gb=nb // 4, tk=t)
    (dpk,), (g["k_norm_g"],) = _row_bwd(_f_knorm, [(proj_k, HEAD, 0)], [small["k_norm_g"]],
                                        [(dk_n, HEAD, 0)], [BF16], nsub=nh, tb=tb, name="k_norm_bwd")
    df_t = dfc_q.transpose(0, 2, 1).reshape(nh, t) + dfr_k.transpose(0, 2, 1, 3).reshape(nh, t)
    dflogit_t, g["kv_b_f"] = _fgate_bwd(f_logit_t, f_bias, df_t)
    dproj_f = jnp.pad(dflogit_t.T, ((0, 0), (0, LANES - nh))).astype(BF16)
    dh_kv = _mm(dpk, wts["kv_k"], "nn", BF16, "k_proj_dx")
    dh_kv_v = _mm(dv, wts["kv_v"], "nn", BF16, "v_proj_dx")
    dh_kv_f = _mm(dproj_f, wts["kv_f"], "nn", BF16, "kv_fproj_dx")
    g["kv_k"] = _mm(dpk, h_kv, "tn", BF16, "k_proj_dw", tk=t)
    g["kv_v"] = _mm(dv, h_kv, "tn", BF16, "v_proj_dw", tk=t)
    g["kv_f"] = _mm(dproj_f, h_kv, "tn", F32, "kv_fproj_dw", tk=1024)
    sent = put_g("l1", {n: g.pop(n) for n in ("b_out", "b_q", "kv_k", "kv_v", "kv_f", "up1", "down1")})
    (dx1, df0), (dmods["g2_0"], dmods["kv_sh"], dmods["kv_sc"], dmods["sh1_1"], dmods["sc1_1"]) = _row_bwd(
        _f_res_mod2, [(x1, d, 0), (f0, d, 0)], [mods["g2_0"] + sent] + vec("kv_sh", "kv_sc", "sh1_1", "sc1_1"),
        [(dx2, d, 0), (dh_kv, d, 0), (dh_q, d, 0)], [F32, BF16], tb=tb, name="l0_res_kvmod_qmod_bwd",
        cot_add=[(1, dh_kv_v), (1, dh_kv_f)])
    dh2, g["up0"], g["down0"], g["conv_w0"], g["conv_b0"] = ffn_bwd(df0, h2_0, u0, a0, 0)
    (dx0, dy_a), (dmods["g1_0"], dmods["sh2_0"], dmods["sc2_0"]) = _row_bwd(
        _f_res_mod, [(x, d, 0), (y_a, d, 0)], vec("g1_0", "sh2_0", "sc2_0"),
        [(dx1, d, 0), (dh2, d, 0)], [F32, BF16], tb=tb, name="l0_res_mod2_bwd")
    dypre = _mm(dy_a, wts["a_out"], "nt", BF16, "a_out_dx")
    g["a_out"] = _mm(ypre, dy_a, "tn", BF16, "a_out_dw", tk=t)
    sent = put_g("l0b", {n: g.pop(n) for n in ("a_out", "up0", "down0")})
    dproj_a, dlb, g["a_norm_g"] = _hgrn2_bwd(proj_a, lb + sent, small["a_norm_g"], states, dypre, tb)
    dh_a = _mm_wblk_dx(dproj_a, wts["a_in"], BF16, "a_in_dx", k=d, gb=nb, split=4, tm=512)
    put_g("l0a", {"a_in": _mm_wblk_dw(h_a, dproj_a, "a_in_dw", nb=nb, gb=1, split=4, tk=t)})
    (grad_x,), (dmods["sh1_0"], dmods["sc1_0"]) = _row_bwd(
        _f_mod, [(x, d, 0)], vec("sh1_0", "sc1_0"), [(dh_a, d, 0)], [F32], tb=tb, name="l0_mod1_bwd",
        add_to=(0, dx0))
    return loss, grad_x, dmods, dlb, g


def _position():
    return lax.axis_index("x"), lax.axis_index("y"), lax.axis_index("c")


def _hbm_specs(n):
    return [pl.BlockSpec(memory_space=pl.ANY)] * n


def _all_gather(arrs, name):
    n = len(arrs)

    def body(*refs):
        x_refs, out_refs = refs[:n], refs[n:2 * n]
        send_sems, recv_sems, local_sems = refs[2 * n:]
        x, y, cc = _position()
        me, sibling = (x, y, cc), (x, y, 1 - cc)
        chips = [(1 - x, y), (x, 1 - y), (1 - x, 1 - y)]

        def copy(a, k, block, to, src=None):
            slot = out_refs[a].at[4 * block[0] + 2 * block[1] + block[2]]
            return pltpu.make_async_remote_copy(
                src_ref=slot if src is None else src, dst_ref=slot,
                send_sem=send_sems.at[7 * a + k], recv_sem=recv_sems.at[7 * a + k],
                device_id=to, device_id_type=_MESH)

        local = [pltpu.make_async_copy(x_refs[a], out_refs[a].at[4 * x + 2 * y + cc], local_sems.at[a])
                 for a in range(n)]
        for cp in local:
            cp.start()
        first = []
        for a in range(n):
            first.append(copy(a, 0, me, sibling, src=x_refs[a]))
            first += [copy(a, 1 + j, me, (*chip, cc), src=x_refs[a]) for j, chip in enumerate(chips)]
        for cp in first:
            cp.start()
        passed = []
        for j, chip in enumerate(chips):
            for a in range(n):
                copy(a, 1 + j, (*chip, cc), me).wait_recv()
                fwd = copy(a, 4 + j, (*chip, cc), sibling)
                fwd.start()
                passed.append(fwd)
        for a in range(n):
            copy(a, 0, sibling, me).wait_recv()
        for j, chip in enumerate(chips):
            for a in range(n):
                copy(a, 4 + j, (*chip, 1 - cc), me).wait_recv()
        for cp in first + passed:
            cp.wait_send()
        for cp in local:
            cp.wait()

    return pl.pallas_call(
        body, name=name,
        out_shape=[jax.ShapeDtypeStruct((NDEV, *a.shape), a.dtype) for a in arrs],
        in_specs=_hbm_specs(n), out_specs=_hbm_specs(n),
        scratch_shapes=[pltpu.SemaphoreType.DMA((7 * n,)), pltpu.SemaphoreType.DMA((7 * n,)),
                        pltpu.SemaphoreType.DMA((n,))],
    )(*arrs)


_XCHG_EFFECT = pltpu.SideEffectType.DATAFLOW_SIDE_EFFECTING
ALL_PEERS = (1, 2, 3, 4, 5, 6, 7)
SAME_CORE = (2, 4, 6)


def _xchg_copies(src_refs, land_refs, send_sems, recv_sems, local_sems, scatter, rels):
    x, y, cc = _position()
    me = 4 * x + 2 * y + cc
    remote, local = [], []
    for a, (src, land) in enumerate(zip(src_refs, land_refs)):
        local.append(pltpu.make_async_copy(src.at[me] if scatter else src, land.at[me], local_sems.at[a]))
        for idx, rel in enumerate(rels):
            px = 1 - x if rel & 4 else x
            py = 1 - y if rel & 2 else y
            pc = 1 - cc if rel & 1 else cc
            k = len(rels) * a + idx
            remote.append(pltpu.make_async_remote_copy(
                src_ref=src.at[4 * px + 2 * py + pc] if scatter else src, dst_ref=land.at[me],
                send_sem=send_sems.at[k], recv_sem=recv_sems.at[k], device_id=(px, py, pc), device_id_type=_MESH))
    return remote, local


def _xchg_start(srcs, scatter, rels, after, name):
    n = len(srcs)
    lands = [lax.empty(s.shape if scatter else (NDEV, *s.shape), s.dtype) for s in srcs]

    def body(*refs):
        remote, local = _xchg_copies(refs[:n], refs[n:2 * n], *refs[2 * n + 1:2 * n + 4], scatter, rels)
        for cp in local + remote:
            cp.start()
        token = refs[-1]
        token[...] = jnp.zeros_like(token)

    hbm = pl.BlockSpec(memory_space=pltpu.HBM)
    sem = pl.BlockSpec(memory_space=pltpu.SEMAPHORE)
    out = pl.pallas_call(
        body, name=name,
        out_shape=(pltpu.SemaphoreType.DMA((len(rels) * n,)), pltpu.SemaphoreType.DMA((len(rels) * n,)),
                   pltpu.SemaphoreType.DMA((n,)),
                   *[pltpu.HBM(a.shape, a.dtype) for a in srcs + lands], jax.ShapeDtypeStruct((8, LANES), F32)),
        in_specs=[hbm] * (2 * n) + [pl.BlockSpec(memory_space=pl.ANY)],
        out_specs=(sem, sem, sem, *[hbm] * (2 * n), pl.BlockSpec(memory_space=pltpu.VMEM)),
        input_output_aliases={i: 3 + i for i in range(2 * n)},
        compiler_params=pltpu.CompilerParams(has_side_effects=_XCHG_EFFECT),
    )(*[pltpu.with_memory_space_constraint(a, pltpu.HBM) for a in srcs + lands], after)
    return out[:-1], out[-1][0, 0]


def _xchg_wait(handles, after, scatter, rels, name):
    n = (len(handles) - 3) // 2

    def body(*refs):
        remote, local = _xchg_copies(refs[:n], refs[n:2 * n], *refs[2 * n:2 * n + 3], scatter, rels)
        for cp in remote:
            cp.wait_send()
            cp.wait_recv()
        for cp in local:
            cp.wait()

    hbm = pl.BlockSpec(memory_space=pltpu.HBM)
    sem = pl.BlockSpec(memory_space=pltpu.SEMAPHORE)
    thru = list(handles[3:])
    out = pl.pallas_call(
        body, name=name,
        out_shape=tuple(pltpu.HBM(a.shape, a.dtype) for a in thru),
        in_specs=[hbm] * (2 * n) + [sem, sem, sem, pl.BlockSpec(memory_space=pl.ANY)],
        out_specs=tuple([hbm] * (2 * n)),
        input_output_aliases={i: i for i in range(2 * n)},
        compiler_params=pltpu.CompilerParams(has_side_effects=_XCHG_EFFECT),
    )(*thru, *handles[:3], after)
    return list(out[n:])


def _sibling_copies(land_refs, send_sems, recv_sems):
    x, y, cc = _position()

    def copy(a, q, core):
        slot = land_refs[a].at[2 * q + core]
        return pltpu.make_async_remote_copy(
            src_ref=slot, dst_ref=slot, send_sem=send_sems.at[NCHIP * a + q], recv_sem=recv_sems.at[NCHIP * a + q],
            device_id=(x, y, 1 - cc), device_id_type=_MESH)

    pairs = [(a, q) for a in range(len(land_refs)) for q in range(NCHIP)]
    return [copy(a, q, cc) for a, q in pairs], [copy(a, q, 1 - cc) for a, q in pairs]


def _sibling_forward_start(lands, name):
    n = len(lands)

    def body(*refs):
        sends, _ = _sibling_copies(refs[:n], refs[n], refs[n + 1])
        for cp in sends:
            cp.start()
        refs[-1][...] = jnp.zeros_like(refs[-1])

    hbm = pl.BlockSpec(memory_space=pltpu.HBM)
    sem = pl.BlockSpec(memory_space=pltpu.SEMAPHORE)
    out = pl.pallas_call(
        body, name=name,
        out_shape=(pltpu.SemaphoreType.DMA((NCHIP * n,)), pltpu.SemaphoreType.DMA((NCHIP * n,)),
                   *[pltpu.HBM(a.shape, a.dtype) for a in lands], jax.ShapeDtypeStruct((8, LANES), F32)),
        in_specs=[hbm] * n,
        out_specs=(sem, sem, *[hbm] * n, pl.BlockSpec(memory_space=pltpu.VMEM)),
        input_output_aliases={i: 2 + i for i in range(n)},
        compiler_params=pltpu.CompilerParams(has_side_effects=_XCHG_EFFECT),
    )(*lands)
    return out[:-1], out[-1][0, 0]


def _sibling_forward_wait(handles, after, name):
    n = len(handles) - 2

    def body(*refs):
        sends, arrivals = _sibling_copies(refs[:n], refs[n], refs[n + 1])
        for cp in sends:
            cp.wait_send()
        for cp in arrivals:
            cp.wait_recv()

    hbm = pl.BlockSpec(memory_space=pltpu.HBM)
    sem = pl.BlockSpec(memory_space=pltpu.SEMAPHORE)
    lands = list(handles[2:])
    return list(pl.pallas_call(
        body, name=name,
        out_shape=tuple(pltpu.HBM(a.shape, a.dtype) for a in lands),
        in_specs=[hbm] * n + [sem, sem, pl.BlockSpec(memory_space=pl.ANY)],
        out_specs=tuple([hbm] * n),
        input_output_aliases={i: i for i in range(n)},
        compiler_params=pltpu.CompilerParams(has_side_effects=_XCHG_EFFECT),
    )(*lands, *handles[:2], after))


def _slab_sum(slabs, name, tr=None):
    n, r, c = slabs.shape
    tr = r if tr is None else tr

    def body(s_ref, o_ref):
        acc = s_ref[0].astype(F32)
        for q in range(1, n):
            acc = acc + s_ref[q].astype(F32)
        o_ref[...] = acc

    return pl.pallas_call(body, name=name, grid=(r // tr,),
                          in_specs=[pl.BlockSpec((n, tr, c), lambda i: (0, i, 0))],
                          out_specs=pl.BlockSpec((tr, c), lambda i: (i, 0)),
                          out_shape=jax.ShapeDtypeStruct((r, c), F32),
                          compiler_params=_cparams(dimension_semantics=("parallel",)))(slabs)


def _ada_fwd(c_all, ada_w, kv_ada_w, logits):
    rows, d = c_all.shape
    n0, nkv = ada_w.shape[2], kv_ada_w.shape[1]

    def body(c_ref, w_ref, kw_ref, lg_ref, part_ref, cact_ref, lb_ref):
        ca = _silu(c_ref[...])
        cact_ref[...] = ca
        part_ref[:, 0:n0] = _bdot_raw(ca, w_ref[0], _NN)
        part_ref[:, n0:2 * n0] = _bdot_raw(ca, w_ref[1], _NN)
        part_ref[:, 2 * n0:2 * n0 + nkv] = _bdot_raw(ca, kw_ref[...], _NN)
        lb_ref[...] = _sigmoid(lg_ref[0:1, :] - lg_ref[1:2, :])

    vm = pl.BlockSpec(memory_space=pltpu.VMEM)
    return pl.pallas_call(
        body, name="ada_fwd", in_specs=[vm, vm, vm, vm], out_specs=[vm, vm, vm],
        out_shape=[jax.ShapeDtypeStruct((rows, 2 * n0 + nkv), F32), jax.ShapeDtypeStruct((rows, d), F32),
                   jax.ShapeDtypeStruct((1, d), F32)],
        compiler_params=_cparams(),
    )(c_all, ada_w, kv_ada_w, logits)


def _ada_bwd(c_act, dm0, dm1, dkv, lb, dlb):
    rows, d = c_act.shape

    def body(c_ref, d0_ref, d1_ref, dk_ref, lb_ref, dlb_ref, dw_ref, dkw_ref, dlg_ref):
        ca = c_ref[...]
        dw_ref[0] = _bdot_raw(ca, d0_ref[...], _TN)
        dw_ref[1] = _bdot_raw(ca, d1_ref[...], _TN)
        dkw_ref[...] = _bdot_raw(ca, dk_ref[...], _TN)
        lbv = lb_ref[...]
        dl0 = dlb_ref[...] * lbv * (1.0 - lbv)
        dlg_ref[0:1, :] = dl0
        dlg_ref[1:2, :] = -dl0

    vm = pl.BlockSpec(memory_space=pltpu.VMEM)
    return pl.pallas_call(
        body, name="ada_bwd", in_specs=[vm] * 6, out_specs=[vm, vm, vm],
        out_shape=[jax.ShapeDtypeStruct((2, d, dm0.shape[1]), F32), jax.ShapeDtypeStruct((d, dkv.shape[1]), F32),
                   jax.ShapeDtypeStruct((2, d), F32)],
        compiler_params=_cparams(),
    )(c_act, dm0, dm1, dkv, lb, dlb)


def _adamw(w, g, m, v, name, tr=512, after=None):
    shape = w.shape
    lead = 1 if w.ndim == 2 else shape[0]
    r, c = shape[-2:]
    w, g, m, v = [a.reshape(lead, r, c) for a in (w, g, m, v)]
    tr = _divisor_tile(r, tr, unit=8)
    c1 = 1.0 - ADAM_B1 ** ADAM_STEP
    c2 = 1.0 - ADAM_B2 ** ADAM_STEP
    deps = [] if after is None else [after]

    def body(w_ref, g_ref, m_ref, v_ref, *rest):
        d_ref, mo_ref, vo_ref = rest[len(deps):]
        gv = g_ref[...]
        mn = ADAM_B1 * m_ref[...] + (1.0 - ADAM_B1) * gv
        vn = ADAM_B2 * v_ref[...] + (1.0 - ADAM_B2) * (gv * gv)
        d_ref[...] = -ADAM_LR * ((mn / c1) / (jnp.sqrt(vn / c2) + ADAM_EPS) + ADAM_WD * w_ref[...])
        mo_ref[...] = mn
        vo_ref[...] = vn

    spec = pl.BlockSpec((1, tr, c), lambda l, i: (l, i, 0))
    out = jax.ShapeDtypeStruct((lead, r, c), F32)
    res = pl.pallas_call(body, name=name, grid=(lead, r // tr),
                         in_specs=[spec] * 4 + [pl.BlockSpec(a.shape, lambda l, i: (0, 0)) for a in deps],
                         out_specs=[spec] * 3, out_shape=[out, out, out],
                         compiler_params=_cparams(dimension_semantics=("parallel", "parallel")))(w, g, m, v, *deps)
    return [a.reshape(shape) for a in res]


def _pad_rows(a, rows):
    return jnp.pad(a, ((0, rows - a.shape[0]), (0, 0)))


def _pack_small(parts, lanes=LANES, row_unit=8):
    flat = jnp.concatenate([p.reshape(-1).astype(F32) for p in parts])
    rows = _round_up(-(-flat.shape[0] // lanes), row_unit)
    return jnp.pad(flat, (0, rows * lanes - flat.shape[0])).reshape(rows, lanes)


def _unpack_small(flat, shapes):
    out, off = [], 0
    for s in shapes:
        n = 1
        for k in s:
            n *= k
        out.append(flat[off:off + n].reshape(s))
        off += n
    return out


def _pad_shard_cols(a, n_loc, n_pad):
    lead = a.shape[:-1]
    a = a.reshape(*lead, NDEV, n_loc)
    a = jnp.pad(a, [(0, 0)] * (len(lead) + 1) + [(0, n_pad - n_loc)])
    return a.reshape(*lead, NDEV * n_pad)


def _unpad_shard_cols(a, n_loc, n_pad):
    lead = a.shape[:-1]
    return a.reshape(*lead, NDEV, n_pad)[..., :n_loc].reshape(*lead, NDEV * n_loc)


def kernel(x, c, ada_w, ada_b, a_w_in, a_lb_logits, a_norm_g, a_w_out, kv_ada_w, kv_ada_b, kv_w, kv_b_f, k_norm_g, b_w_q, q_norm_g, b_w_out, ffn_w_up, ffn_conv_w, ffn_conv_b, ffn_w_down, loss_target, m_ada_w, m_ada_b, m_a_w_in, m_a_lb_logits, m_a_norm_g, m_a_w_out, m_kv_ada_w, m_kv_ada_b, m_kv_w, m_kv_b_f, m_k_norm_g, m_b_w_q, m_q_norm_g, m_b_w_out, m_ffn_w_up, m_ffn_conv_w, m_ffn_conv_b, m_ffn_w_down, v_ada_w, v_ada_b, v_a_w_in, v_a_lb_logits, v_a_norm_g, v_a_w_out, v_kv_ada_w, v_kv_ada_b, v_kv_w, v_kv_b_f, v_k_norm_g, v_b_w_q, v_q_norm_g, v_b_w_out, v_ffn_w_up, v_ffn_conv_w, v_ffn_conv_b, v_ffn_w_down):
    t, d = x.shape[1], x.shape[2]
    nh = d // HEAD
    ncw = ffn_w_up.shape[2]
    ncp = _round_up(ncw, LANES)
    two_f = ncw * NDEV
    ff = two_f // 2
    fp = ncp * NDEV // 2
    rd = ffn_w_down.shape[1]
    me = 4 * lax.axis_index("x") + 2 * lax.axis_index("y") + lax.axis_index("c")
    weights = dict(ada_w=ada_w, ada_b=ada_b, a_w_in=a_w_in, a_lb_logits=a_lb_logits, a_norm_g=a_norm_g,
                   a_w_out=a_w_out, kv_ada_w=kv_ada_w, kv_ada_b=kv_ada_b, kv_w=kv_w, kv_b_f=kv_b_f,
                   k_norm_g=k_norm_g, b_w_q=b_w_q, q_norm_g=q_norm_g, b_w_out=b_w_out, ffn_w_up=ffn_w_up,
                   ffn_conv_w=ffn_conv_w, ffn_conv_b=ffn_conv_b, ffn_w_down=ffn_w_down)
    m_in = dict(ada_w=m_ada_w, ada_b=m_ada_b, a_w_in=m_a_w_in, a_lb_logits=m_a_lb_logits, a_norm_g=m_a_norm_g,
                a_w_out=m_a_w_out, kv_ada_w=m_kv_ada_w, kv_ada_b=m_kv_ada_b, kv_w=m_kv_w, kv_b_f=m_kv_b_f,
                k_norm_g=m_k_norm_g, b_w_q=m_b_w_q, q_norm_g=m_q_norm_g, b_w_out=m_b_w_out, ffn_w_up=m_ffn_w_up,
                ffn_conv_w=m_ffn_conv_w, ffn_conv_b=m_ffn_conv_b, ffn_w_down=m_ffn_w_down)
    v_in = dict(ada_w=v_ada_w, ada_b=v_ada_b, a_w_in=v_a_w_in, a_lb_logits=v_a_lb_logits, a_norm_g=v_a_norm_g,
                a_w_out=v_a_w_out, kv_ada_w=v_kv_ada_w, kv_ada_b=v_kv_ada_b, kv_w=v_kv_w, kv_b_f=v_kv_b_f,
                k_norm_g=v_k_norm_g, b_w_q=v_b_w_q, q_norm_g=v_q_norm_g, b_w_out=v_b_w_out, ffn_w_up=v_ffn_w_up,
                ffn_conv_w=v_ffn_conv_w, ffn_conv_b=v_ffn_conv_b, ffn_w_down=v_ffn_w_down)
    order = list(weights)

    up_loc = [jnp.pad(ffn_w_up[l].astype(BF16), ((0, 0), (0, ncp - ncw))) for l in range(2)]
    down_loc = ffn_w_down.astype(BF16)
    gather_names = {"l0b": ["a_out", "up0", "down0"], "l1": ["kv", "b_q", "b_out", "up1", "down1"]}
    shards = {"a_out": a_w_out[0].astype(BF16), "up0": up_loc[0], "down0": down_loc[0], "kv": kv_w.T.astype(BF16),
              "b_q": b_w_q[0].astype(BF16), "b_out": b_w_out[0].astype(BF16), "up1": up_loc[1],
              "down1": down_loc[1]}
    pre = _pack_small([c, a_lb_logits, ffn_conv_w])
    a_in_all, pre_all = _all_gather([a_w_in[0].astype(BF16), pre], "gather_a_w_in_and_small_inputs")
    pre_all = pre_all.reshape(NDEV, -1)
    c_all = pre_all[:, :d]
    logits = pre_all[:, d:d + 2 * HEAD].reshape(NDEV, 2, HEAD).transpose(1, 0, 2).reshape(2, d)
    conv_w_full = pre_all[:, d + 2 * HEAD:d + 2 * HEAD + 2 * CONV_TAPS * ncw]
    conv_w_full = conv_w_full.reshape(NDEV, 2, CONV_TAPS, ncw).transpose(1, 2, 0, 3).reshape(2, CONV_TAPS, two_f)

    part, c_act, lb = _ada_fwd(_pad_rows(c_all, 2 * NDEV), ada_w, kv_ada_w, logits)
    (part_all,) = _all_gather([part[:NDEV]], "gather_adaln")
    mine = lax.dynamic_index_in_dim(part_all, me, axis=1, keepdims=False)
    n0, nkv = ada_w.shape[2], kv_ada_w.shape[1]
    mod_names = ["sh1", "sc1", "g1", "sh2", "sc2", "g2"]
    mods = {}
    for l in range(2):
        row = mine[:, l * n0:(l + 1) * n0].reshape(-1) + ada_b[l]
        for k, nm in enumerate(mod_names):
            mods[f"{nm}_{l}"] = row[k * d:(k + 1) * d].reshape(1, d)
    kvrow = mine[:, 2 * n0:2 * n0 + nkv].reshape(-1) + kv_ada_b
    mods["kv_sh"], mods["kv_sc"] = kvrow[:d].reshape(1, d), kvrow[d:].reshape(1, d)

    in_flight = {}

    def start_gather(grp, dep):
        srcs = [shards[n] for n in gather_names[grp]]
        in_flight[grp], started = _xchg_start(srcs, False, SAME_CORE, dep, f"gather_{grp}_start")
        return started

    zero = start_gather("l0b", part_all)
    mods["sh1_0"] = mods["sh1_0"] + zero

    small = {"a_norm_g": a_norm_g, "k_norm_g": k_norm_g.reshape(1, HEAD), "q_norm_g": q_norm_g, "kv_b_f": kv_b_f}
    for l in range(2):
        small[f"conv_w{l}"] = _pad_shard_cols(conv_w_full[l], ncw, ncp).reshape(CONV_TAPS, 2, fp).transpose(1, 0, 2)
        small[f"conv_b{l}"] = _pad_shard_cols(ffn_conv_b[l], ncw, ncp).reshape(2, 1, fp)

    forwarding = {}

    def pre_w(grp, after):
        arrived = _xchg_wait(in_flight[grp], after, False, SAME_CORE, f"gather_{grp}_wait")
        forwarding[grp], started = _sibling_forward_start(arrived, f"gather_{grp}_to_sibling_start")
        return started

    def get_w(grp, after):
        if grp == "l0a":
            return {"a_in": a_in_all}
        full = _sibling_forward_wait(forwarding[grp], after, f"gather_{grp}_to_sibling_wait")
        if grp == "l0b":
            started = start_gather("l1", full[0])
            full[0] = full[0] + started.astype(full[0].dtype)
        got = dict(zip(gather_names[grp], full))
        out = {}
        for n, a in got.items():
            if n in ("a_out", "b_out"):
                out[n] = a.reshape(d, d)
            elif n in ("down0", "down1"):
                dn = a.reshape(NCHIP, ff // NCHIP, d)
                out[n] = jnp.pad(dn, ((0, 0), (0, ncp - ncw), (0, 0))).reshape(fp, d)
            elif n == "kv":
                kv_t = a.reshape(NDEV * kv_w.shape[1], d)
                out["kv_k"], out["kv_v"] = kv_t[:d], kv_t[d:2 * d]
                out["kv_f"] = jnp.pad(kv_t[2 * d:], ((0, LANES - nh), (0, 0)))
            else:
                out[n] = a
        return out

    scatter_flight, g_last = {}, {}

    def put_g(grp, gr):
        if grp == "l0a":
            g_last.update(gr)
            return zero
        if grp == "l1":
            g_kvw = jnp.concatenate([gr["kv_k"], gr["kv_v"], gr["kv_f"][:nh].astype(BF16)], axis=0)
            arrs = {"kv_w": g_kvw.reshape(NDEV, kv_w.shape[1], d), "b_w_q": gr["b_q"],
                    "b_w_out": gr["b_out"].reshape(NDEV, d // NDEV, d), "up1": gr["up1"],
                    "down1": gr["down1"].reshape(NCHIP, ncp, d)[:, :ncw].reshape(NDEV, rd, d)}
        else:
            arrs = {"a_w_out": gr["a_out"].reshape(NDEV, d // NDEV, d), "up0": gr["up0"],
                    "down0": gr["down0"].reshape(NCHIP, ncp, d)[:, :ncw].reshape(NDEV, rd, d)}
        srcs = list(arrs.values())
        handles, sent = _xchg_start(srcs, True, ALL_PEERS, srcs[0], f"scatter_{grp}_start")
        scatter_flight[grp] = (list(arrs), handles)
        return sent

    loss_v, grad_x, dmods, dlb, g = _local_step(x[0], loss_target[0], mods, lb, small, pre_w, get_w, put_g)

    g_sum = {}
    for grp in ("l1", "l0b"):
        names, handles = scatter_flight[grp]
        for nm, a in zip(names, _xchg_wait(handles, grad_x, True, ALL_PEERS, f"scatter_{grp}_wait")):
            g_sum[nm] = _slab_sum(a, f"rs_slab_sum_{nm}")

    def conv_w_grad(a):
        return _unpad_shard_cols(a.transpose(1, 0, 2).reshape(CONV_TAPS, 2 * fp), ncw, ncp)

    def conv_b_grad(a):
        return _unpad_shard_cols(a.reshape(2 * fp), ncw, ncp)

    dmod_vec = [dmods[f"{nm}_{l}"] for l in range(2) for nm in mod_names] + [dmods["kv_sh"], dmods["kv_sc"]]
    post = _pack_small(dmod_vec + [dlb, g["a_norm_g"], g["k_norm_g"], g["q_norm_g"],
                                   jnp.pad(g["kv_b_f"].reshape(-1), (0, LANES - nh)),
                                   conv_w_grad(g["conv_w0"]), conv_w_grad(g["conv_w1"]),
                                   conv_b_grad(g["conv_b0"]), conv_b_grad(g["conv_b1"]), loss_v])
    (post_all,) = _all_gather([post], "gather_small_grads")
    a_in_flight, a_in_sent = _xchg_start([g_last["a_in"]], True, ALL_PEERS, post_all, "scatter_l0a_start")
    a_in_sent = a_in_sent.reshape(1, 1)
    tot = _slab_sum(post_all, "small_grad_sum").reshape(-1)
    nmod = 14 * d
    (t_mod, t_lb, t_ang, t_kng, t_qng, t_bf, t_cw, t_cb, t_loss) = _unpack_small(
        tot, [(nmod,), (1, d), (1, HEAD), (HEAD,), (1, HEAD), (LANES,), (2, CONV_TAPS, two_f), (2, two_f),
              (LANES,)])
    loss = t_loss[0]
    dm_all = post_all.reshape(NDEV, -1)[:, :nmod]
    dm0 = lax.dynamic_slice_in_dim(dm_all[:, :6 * d], me * n0, n0, axis=1)
    dm1 = lax.dynamic_slice_in_dim(dm_all[:, 6 * d:12 * d], me * n0, n0, axis=1)
    dkv = lax.dynamic_slice_in_dim(dm_all[:, 12 * d:], me * nkv, nkv, axis=1)
    g_ada_w, g_kv_ada_w, g_logits = _ada_bwd(c_act, _pad_rows(dm0, 2 * NDEV), _pad_rows(dm1, 2 * NDEV),
                                              _pad_rows(dkv, 2 * NDEV), lb, t_lb)

    grads = {
        "ada_w": g_ada_w,
        "ada_b": t_mod[:12 * d].reshape(2, 6 * d),
        "a_lb_logits": lax.dynamic_slice_in_dim(g_logits, me * HEAD, HEAD, axis=1),
        "a_norm_g": t_ang,
        "a_w_out": g_sum["a_w_out"].reshape(a_w_out.shape),
        "kv_ada_w": g_kv_ada_w,
        "kv_ada_b": t_mod[12 * d:],
        "kv_w": g_sum["kv_w"].T,
        "kv_b_f": t_bf[:nh],
        "k_norm_g": t_kng,
        "b_w_q": g_sum["b_w_q"].reshape(b_w_q.shape),
        "q_norm_g": t_qng,
        "b_w_out": g_sum["b_w_out"].reshape(b_w_out.shape),
        "ffn_w_up": jnp.stack([g_sum["up0"][:, :ncw], g_sum["up1"][:, :ncw]]),
        "ffn_conv_w": lax.dynamic_slice_in_dim(t_cw, me * ncw, ncw, axis=2),
        "ffn_conv_b": t_cb,
        "ffn_w_down": jnp.stack([g_sum["down0"], g_sum["down1"]]),
    }

    big_adam = ["ada_w", "a_w_out", "kv_ada_w", "kv_w", "b_w_q", "b_w_out", "ffn_w_up", "ffn_w_down", "a_w_in"]
    small_adam = [n for n in order if n not in big_adam]
    delta, new_m, new_v = {}, {}, {}
    packs = [_pack_small([src[n] for n in small_adam]) for src in (weights, grads, m_in, v_in)]
    outs = _adamw(*packs, "adamw_small", tr=packs[0].shape[0])
    shapes = [weights[n].shape for n in small_adam]
    for dst, o in zip((delta, new_m, new_v), outs):
        for n, a in zip(small_adam, _unpack_small(o.reshape(-1), shapes)):
            dst[n] = a
    for n in big_adam:
        if n == "a_w_in":
            (landed,) = _xchg_wait(a_in_flight, new_v["ffn_w_down"], True, ALL_PEERS, "scatter_l0a_wait")
            grads[n] = _slab_sum(landed, "rs_slab_sum_a_w_in").reshape(a_w_in.shape)
        delta[n], new_m[n], new_v[n] = _adamw(weights[n], grads[n], m_in[n], v_in[n], f"adamw_{n}",
                                              after=a_in_sent)

    return (loss, grad_x.reshape(x.shape), *[grads[n] for n in order], *[delta[n] for n in order],
            *[new_m[n] for n in order], *[new_v[n] for n in order])
```

```python
import functools

import jax
import jax.numpy as jnp
from jax import lax
from jax.experimental import pallas as pl
from jax.experimental.pallas import tpu as pltpu

F32 = jnp.float32
BF16 = jnp.bfloat16

NDEV = 8
NCHIP = 4
HEAD = 128
A_CHUNK = 64
CONV_TAPS = 3
EPS = 1e-6
NEG_INF = -1e30
LANES = 128
VMEM_LIMIT = 48 * 1024 * 1024

ADAM_LR = 0.001
ADAM_B1 = 0.9
ADAM_B2 = 0.999
ADAM_EPS = 1e-08
ADAM_WD = 0.01
ADAM_STEP = 10

_NN = (((1,), (0,)), ((), ()))
_NT = (((1,), (1,)), ((), ()))
_TN = (((0,), (0,)), ((), ()))
_MESH = pl.DeviceIdType.MESH


def _cparams(**kw):
    return pltpu.CompilerParams(vmem_limit_bytes=VMEM_LIMIT, **kw)


def _divisor_tile(n, pref, unit=LANES):
    if n <= pref:
        return n
    best = None
    for t in range(unit, pref + 1, unit):
        if n % t == 0:
            best = t
    assert best is not None, (n, pref)
    return best


def _round_up(n, unit):
    return -(-n // unit) * unit


def _bdot_raw(a, b, dims):
    return lax.dot_general(a.astype(BF16), b.astype(BF16), dims, preferred_element_type=F32)


@jax.custom_vjp
def _dot_nn(a, b):
    return _bdot_raw(a, b, _NN)


@jax.custom_vjp
def _dot_nt(a, b):
    return _bdot_raw(a, b, _NT)


@jax.custom_vjp
def _dot_tn(a, b):
    return _bdot_raw(a, b, _TN)


_dot_nn.defvjp(lambda a, b: (_bdot_raw(a, b, _NN), (a, b)),
               lambda r, g: (_dot_nt(g, r[1]), _dot_tn(r[0], g)))
_dot_nt.defvjp(lambda a, b: (_bdot_raw(a, b, _NT), (a, b)),
               lambda r, g: (_dot_nn(g, r[1]), _dot_tn(g, r[0])))
_dot_tn.defvjp(lambda a, b: (_bdot_raw(a, b, _TN), (a, b)),
               lambda r, g: (_dot_nt(r[1], g), _dot_nn(r[0], g)))


def _f32dot(a, b):
    return lax.dot_general(a, b, _NN, precision=lax.Precision.HIGHEST, preferred_element_type=F32)


def _sigmoid(x):
    return jax.nn.sigmoid(x)


def _silu(x):
    return x * jax.nn.sigmoid(x)


def _rms(x):
    return x * lax.rsqrt(jnp.mean(x * x, axis=-1, keepdims=True) + EPS)


def _modulate(x, sh, sc):
    return _rms(x) * (1.0 + sc) + sh


def _mm_call(a, b, dims, a_spec, b_spec, o_spec, o_shape, grid, acc_tile, name):
    nk = grid[2]

    def body(a_ref, b_ref, o_ref, *acc):
        p = lax.dot_general(a_ref[...].astype(BF16), b_ref[...].astype(BF16), dims,
                            preferred_element_type=F32)
        if nk == 1:
            o_ref[...] = p.astype(o_ref.dtype)
        else:
            kk = pl.program_id(2)

            @pl.when(kk == 0)
            def _():
                acc[0][...] = p

            @pl.when(kk > 0)
            def _():
                acc[0][...] += p

            @pl.when(kk == nk - 1)
            def _():
                o_ref[...] = acc[0][...].astype(o_ref.dtype)

    return pl.pallas_call(
        body, name=name, grid=grid, in_specs=[a_spec, b_spec], out_specs=o_spec, out_shape=o_shape,
        scratch_shapes=[pltpu.VMEM(acc_tile, F32)] if nk > 1 else [],
        compiler_params=_cparams(dimension_semantics=("parallel", "parallel", "arbitrary")),
    )(a, b)


def _mm(a, b, mode, out_dtype, name, tm=1024, tn=1024, tk=2048):
    if mode == "nn":
        (m, k), (k2, n) = a.shape, b.shape
    elif mode == "nt":
        (m, k), (n, k2) = a.shape, b.shape
    else:
        (k, m), (k2, n) = a.shape, b.shape
    assert k == k2, (a.shape, b.shape, mode)
    tm, tn, tk = _divisor_tile(m, tm), _divisor_tile(n, tn), _divisor_tile(k, tk)
    if mode == "tn":
        a_spec = pl.BlockSpec((tk, tm), lambda i, j, kk: (kk, i))
    else:
        a_spec = pl.BlockSpec((tm, tk), lambda i, j, kk: (i, kk))
    if mode == "nt":
        b_spec = pl.BlockSpec((tn, tk), lambda i, j, kk: (j, kk))
    else:
        b_spec = pl.BlockSpec((tk, tn), lambda i, j, kk: (kk, j))
    return _mm_call(a, b, {"nn": _NN, "nt": _NT, "tn": _TN}[mode], a_spec, b_spec,
                    pl.BlockSpec((tm, tn), lambda i, j, kk: (i, j)), jax.ShapeDtypeStruct((m, n), out_dtype),
                    (m // tm, n // tn, k // tk), (tm, tn), name)


def _wblk_act_spec(rows, gb, nl, split, nb, row_axis, blk_axis):
    if split == 1:
        return pl.BlockSpec((rows, gb * nl), lambda *g: (g[row_axis], g[blk_axis]))
    groups = nb // split // gb
    return pl.BlockSpec((None, rows, gb * nl),
                        lambda *g: (g[blk_axis] // groups, g[row_axis], g[blk_axis] % groups))


def _mm_wblk(a, wb, out_dtype, name, *, gb, row_off=0, split=1, tm=1024):
    m, k = a.shape
    nb, _, nl = wb.shape
    assert (nb // split) % gb == 0
    tm = _divisor_tile(m, tm)

    def body(a_ref, b_ref, o_ref):
        av = a_ref[...].astype(BF16)
        for s in range(gb):
            o_ref[:, s * nl:(s + 1) * nl] = lax.dot_general(
                av, b_ref[s].astype(BF16), _NN, preferred_element_type=F32).astype(o_ref.dtype)

    o_shape = (m, nb * nl) if split == 1 else (split, m, nb // split * nl)
    return pl.pallas_call(
        body, name=name, grid=(nb // gb, m // tm),
        in_specs=[pl.BlockSpec((tm, k), lambda j, i: (i, 0)),
                  pl.BlockSpec((gb, k, nl), lambda j, i: (j, row_off, 0))],
        out_specs=_wblk_act_spec(tm, gb, nl, split, nb, 1, 0),
        out_shape=jax.ShapeDtypeStruct(o_shape, out_dtype),
        compiler_params=_cparams(dimension_semantics=("parallel", "parallel")),
    )(a, wb)


def _mm_wblk_dx(dy, wb, out_dtype, name, *, k, gb, row_off=0, split=1, tm=1024):
    nb, _, nl = wb.shape
    m = dy.shape[-2]
    tm = _divisor_tile(m, tm)
    nk = nb // gb
    per = nb // split
    whole = split > 1 and gb == nb
    assert whole or per % gb == 0

    def body(a_ref, b_ref, o_ref, *acc):
        p = None
        for s in range(gb):
            a_blk = a_ref[s // per, :, (s % per) * nl:(s % per + 1) * nl] if whole else a_ref[:, s * nl:(s + 1) * nl]
            q = lax.dot_general(a_blk.astype(BF16), b_ref[s].astype(BF16), _NT, preferred_element_type=F32)
            p = q if p is None else p + q
        if nk == 1:
            o_ref[...] = p.astype(o_ref.dtype)
        else:
            kk = pl.program_id(1)

            @pl.when(kk == 0)
            def _():
                acc[0][...] = p

            @pl.when(kk > 0)
            def _():
                acc[0][...] += p

            @pl.when(kk == nk - 1)
            def _():
                o_ref[...] = acc[0][...].astype(o_ref.dtype)

    return pl.pallas_call(
        body, name=name, grid=(m // tm, nk),
        in_specs=[pl.BlockSpec((split, tm, per * nl), lambda i, kk: (0, i, 0)) if whole
                  else _wblk_act_spec(tm, gb, nl, split, nb, 0, 1),
                  pl.BlockSpec((gb, k, nl), lambda i, kk: (kk, row_off, 0))],
        out_specs=pl.BlockSpec((tm, k), lambda i, kk: (i, 0)),
        out_shape=jax.ShapeDtypeStruct((m, k), out_dtype),
        scratch_shapes=[pltpu.VMEM((tm, k), F32)] if nk > 1 else [],
        compiler_params=_cparams(dimension_semantics=("parallel", "arbitrary")),
    )(dy, wb)


def _mm_wblk_dw(x, dy, name, *, nb, gb, split=1, tk=1024):
    t, k = x.shape
    assert (nb // split) % gb == 0
    nl = dy.shape[-1] * split // nb
    tk = _divisor_tile(t, tk)
    nk = t // tk

    def body(a_ref, b_ref, o_ref, *acc):
        kk = pl.program_id(1)
        av = a_ref[...].astype(BF16)
        for s in range(gb):
            p = lax.dot_general(av, b_ref[:, s * nl:(s + 1) * nl].astype(BF16), _TN, preferred_element_type=F32)
            if nk == 1:
                o_ref[s] = p.astype(o_ref.dtype)
                continue

            @pl.when(kk == 0)
            def _():
                acc[0][s] = p

            @pl.when(kk > 0)
            def _():
                acc[0][s] += p

        if nk > 1:
            @pl.when(kk == nk - 1)
            def _():
                o_ref[...] = acc[0][...].astype(o_ref.dtype)

    return pl.pallas_call(
        body, name=name, grid=(nb // gb, nk),
        in_specs=[pl.BlockSpec((tk, k), lambda j, kk: (kk, 0)), _wblk_act_spec(tk, gb, nl, split, nb, 1, 0)],
        out_specs=pl.BlockSpec((gb, k, nl), lambda j, kk: (j, 0, 0)),
        out_shape=jax.ShapeDtypeStruct((nb, k, nl), BF16),
        scratch_shapes=[pltpu.VMEM((gb, k, nl), F32)] if nk > 1 else [],
        compiler_params=_cparams(dimension_semantics=("parallel", "arbitrary")),
    )(x, dy)


def _row_specs(rows, tb, nsub):
    return [pl.BlockSpec((tb, nsub * cw), functools.partial(lambda i, off: (i, off), off=off))
            for (_, cw, off) in rows]


def _vec_specs(params):
    return [pl.BlockSpec(p.shape, lambda i: (0, 0)) for p in params]


def _row_fwd(f, rows, params, out_dtypes, *, nsub=1, tb, name):
    t = rows[0][0].shape[0]
    tb = min(tb, t)
    n_r, n_p = len(rows), len(params)
    blk = [jax.ShapeDtypeStruct((tb, cw), F32) for (_, cw, _) in rows]
    blk += [jax.ShapeDtypeStruct(p.shape, F32) for p in params]
    out_avals = jax.eval_shape(f, *blk)

    def body(*refs):
        pv = [r[...] for r in refs[n_r:n_r + n_p]]
        for s in range(nsub):
            vals = [r[:, s * cw:(s + 1) * cw].astype(F32) for r, (_, cw, _) in zip(refs[:n_r], rows)]
            outs = f(*vals, *pv)
            for o_ref, o in zip(refs[n_r + n_p:], outs):
                w = o.shape[1]
                o_ref[:, s * w:(s + 1) * w] = o.astype(o_ref.dtype)

    return pl.pallas_call(
        body, name=name,
        grid=(t // tb,),
        in_specs=_row_specs(rows, tb, nsub) + _vec_specs(params),
        out_specs=[pl.BlockSpec((tb, nsub * av.shape[1]), lambda i: (i, 0)) for av in out_avals],
        out_shape=[jax.ShapeDtypeStruct((t, nsub * av.shape[1]), dt) for av, dt in zip(out_avals, out_dtypes)],
        compiler_params=_cparams(dimension_semantics=("parallel",)),
    )(*[r[0] for r in rows], *params)


def _row_bwd(f, rows, params, cots, row_grad_dtypes, *, nsub=1, tb, name, add_to=None, cot_add=None):
    t = rows[0][0].shape[0]
    tb = min(tb, t)
    n_r, n_p, n_c = len(rows), len(params), len(cots)
    want = [j for j in range(n_r) if row_grad_dtypes[j] is not None]
    cot_add = cot_add or []
    extra = [] if add_to is None else [(add_to[1], rows[add_to[0]][1], 0)]
    n_add_to = len(extra)
    extra += [(arr, cots[ci][1], 0) for ci, arr in cot_add]

    def body(*refs):
        i = pl.program_id(0)
        r_in, p_in = refs[:n_r], refs[n_r:n_r + n_p]
        c_in = refs[n_r + n_p:n_r + n_p + n_c]
        e_in = refs[n_r + n_p + n_c:n_r + n_p + n_c + len(extra)]
        outs = refs[n_r + n_p + n_c + len(extra):]
        pv = [r[...] for r in p_in]
        psum = [None] * n_p
        for s in range(nsub):
            vals = [r[:, s * cw:(s + 1) * cw].astype(F32) for r, (_, cw, _) in zip(r_in, rows)]
            cvals = [r[:, s * cw:(s + 1) * cw].astype(F32) for r, (_, cw, _) in zip(c_in, cots)]
            for (ci, _), e_ref in zip(cot_add, e_in[n_add_to:]):
                cw = cots[ci][1]
                cvals[ci] = cvals[ci] + e_ref[:, s * cw:(s + 1) * cw].astype(F32)
            _, vjp_fn = jax.vjp(f, *vals, *pv)
            grads = vjp_fn(tuple(cvals))
            for o_ref, jr in zip(outs[:len(want)], want):
                cw = rows[jr][1]
                gr = grads[jr]
                if add_to is not None and jr == add_to[0]:
                    gr = gr + e_in[0][:, s * cw:(s + 1) * cw]
                o_ref[:, s * cw:(s + 1) * cw] = gr.astype(o_ref.dtype)
            for jp in range(n_p):
                psum[jp] = grads[n_r + jp] if psum[jp] is None else psum[jp] + grads[n_r + jp]
        for o_ref, g in zip(outs[len(want):], psum):
            @pl.when(i == 0)
            def _():
                o_ref[...] = g

            @pl.when(i > 0)
            def _():
                o_ref[...] += g

    out_specs = [pl.BlockSpec((tb, nsub * rows[jr][1]), lambda i: (i, 0)) for jr in want]
    out_shape = [jax.ShapeDtypeStruct((t, nsub * rows[jr][1]), row_grad_dtypes[jr]) for jr in want]
    out_specs += _vec_specs(params)
    out_shape += [jax.ShapeDtypeStruct(p.shape, F32) for p in params]
    res = pl.pallas_call(
        body, name=name,
        grid=(t // tb,),
        in_specs=_row_specs(rows, tb, nsub) + _vec_specs(params) + _row_specs(cots, tb, nsub)
        + _row_specs(extra, tb, nsub),
        out_specs=out_specs, out_shape=out_shape,
        compiler_params=_cparams(dimension_semantics=("arbitrary",)),
    )(*[r[0] for r in rows], *params, *[c[0] for c in cots], *[e[0] for e in extra])
    return res[:len(want)], res[len(want):]


def _f_mod(x, sh, sc):
    return (_modulate(x, sh, sc),)


def _f_res_mod(x, y, g, sh, sc):
    x1 = x + g * y
    return x1, _modulate(x1, sh, sc)


def _f_res_mod2(x, y, g, sh_a, sc_a, sh_b, sc_b):
    x1 = x + g * y
    return x1, _modulate(x1, sh_a, sc_a), _modulate(x1, sh_b, sc_b)


def _f_qnorm(p, g):
    return (_rms(p) * g * (HEAD ** -0.5),)


def _f_knorm(p, g):
    return (_rms(p) * g,)


def _f_qnorm_aug(p, g):
    lane = lax.broadcasted_iota(jnp.int32, p.shape, 1)
    return (jnp.concatenate([_rms(p) * g * (HEAD ** -0.5), jnp.where(lane < 3, 1.0, 0.0)], axis=1),)


def _f_knorm_aug(p, c0, c1, c2, g):
    lane = lax.broadcasted_iota(jnp.int32, p.shape, 1)
    aug = jnp.where(lane == 0, c0, jnp.where(lane == 1, c1, jnp.where(lane == 2, c2, 0.0)))
    return (jnp.concatenate([_rms(p) * g, aug], axis=1),)


def _split3(a):
    round_bf16 = lambda v: lax.reduce_precision(v, exponent_bits=8, mantissa_bits=7)
    hi = round_bf16(a)
    mid = round_bf16(a - hi)
    lo = round_bf16(a - hi - mid)
    return hi.astype(BF16), mid.astype(BF16), lo.astype(BF16)


def _f_outgate(o, og):
    return (o * _sigmoid(og),)


def _loss_call(x3, f, g2, target, tb):
    t, d = x3.shape
    tb = min(tb, t)

    def body(x_ref, f_ref, g_ref, t_ref, loss_ref, dx_ref, df_ref, dg_ref):
        i = pl.program_id(0)
        fv = f_ref[...]
        g = g_ref[...]
        e = x_ref[...] + g * fv - t_ref[...]
        dx = e * (1.0 / d)
        part = 0.5 * jnp.sum(jnp.sum(e * dx, axis=1, keepdims=True), axis=0, keepdims=True)
        dx_ref[...] = dx
        df_ref[...] = (g * dx).astype(df_ref.dtype)
        dg = jnp.sum(dx * fv, axis=0, keepdims=True)

        @pl.when(i == 0)
        def _():
            loss_ref[...] = jnp.broadcast_to(part, loss_ref.shape)
            dg_ref[...] = dg

        @pl.when(i > 0)
        def _():
            loss_ref[...] += jnp.broadcast_to(part, loss_ref.shape)
            dg_ref[...] += dg

    row = pl.BlockSpec((tb, d), lambda i: (i, 0))
    vec = pl.BlockSpec((1, d), lambda i: (0, 0))
    return pl.pallas_call(
        body, name="loss_head",
        grid=(t // tb,),
        in_specs=[row, row, vec, row],
        out_specs=[pl.BlockSpec((1, LANES), lambda i: (0, 0)), row, row, vec],
        out_shape=[jax.ShapeDtypeStruct((1, LANES), F32), jax.ShapeDtypeStruct((t, d), F32),
                   jax.ShapeDtypeStruct((t, d), BF16), jax.ShapeDtypeStruct((1, d), F32)],
        compiler_params=_cparams(dimension_semantics=("arbitrary",)),
    )(x3, f, g2, target)


def _hg_mask(tb):
    br = lax.broadcasted_iota(jnp.int32, (tb, tb), 0)
    bs = lax.broadcasted_iota(jnp.int32, (tb, tb), 1)
    return jnp.logical_and(br // A_CHUNK == bs // A_CHUNK, bs <= br).astype(F32)


def _hg_consts(mask):
    c = A_CHUNK
    r = lax.broadcasted_iota(jnp.int32, (c, c), 0)
    s = lax.broadcasted_iota(jnp.int32, (c, c), 1)
    return (s <= r).astype(F32), (r <= s).astype(F32), mask > 0.5


def _chunk_apply(mat, x):
    c = mat.shape[0]
    return jnp.concatenate([_f32dot(mat, x[i * c:(i + 1) * c]) for i in range(x.shape[0] // c)], axis=0)


@jax.custom_vjp
def _chunk_cumsum(x, tri, tri_t):
    return _chunk_apply(tri, x)


_chunk_cumsum.defvjp(lambda x, tri, tri_t: (_chunk_apply(tri, x), (tri, tri_t)),
                     lambda r, g: (_chunk_apply(r[1], g), jnp.zeros_like(r[0]), jnp.zeros_like(r[1])))


def _per_chunk(a, b, dims):
    return jnp.stack([_bdot_raw(a[i], b[i], dims) for i in range(a.shape[0])])


@jax.custom_vjp
def _chunk_tn(a, b):
    return _per_chunk(a, b, _TN)


@jax.custom_vjp
def _chunk_nt(a, b):
    return _per_chunk(a, b, _NT)


@jax.custom_vjp
def _chunk_nn(a, b):
    return _per_chunk(a, b, _NN)


_chunk_tn.defvjp(lambda a, b: (_per_chunk(a, b, _TN), (a, b)),
                 lambda r, g: (_chunk_nt(r[1], g), _chunk_nn(r[0], g)))
_chunk_nt.defvjp(lambda a, b: (_per_chunk(a, b, _NT), (a, b)),
                 lambda r, g: (_chunk_nn(g, r[1]), _chunk_tn(g, r[0])))
_chunk_nn.defvjp(lambda a, b: (_per_chunk(a, b, _NN), (a, b)),
                 lambda r, g: (_chunk_nt(g, r[1]), _chunk_tn(r[0], g)))


def _scan_states(decay, m, st):
    sts = []
    for i in range(m.shape[0]):
        sts.append(st)
        st = st * decay[i] + m[i]
    return jnp.stack(sts), st


@jax.custom_vjp
def _state_scan(decay, m, st):
    return _scan_states(decay, m, st)


def _state_scan_fwd(decay, m, st):
    sts, st_out = _scan_states(decay, m, st)
    return (sts, st_out), (decay, sts)


def _state_scan_bwd(res, cts):
    decay, sts = res
    d_sts, g = cts
    d_decay, d_m = [], []
    for i in range(sts.shape[0] - 1, -1, -1):
        d_m.append(g)
        d_decay.append(jnp.sum(g * sts[i], axis=0, keepdims=True))
        g = g * decay[i] + d_sts[i]
    return jnp.stack(d_decay[::-1]), jnp.stack(d_m[::-1]), g


_state_scan.defvjp(_state_scan_fwd, _state_scan_bwd)


def _hg_block(qp, fp, ip, gp, lb, ng, st, tri, tri_t, bd_causal):
    tb = qp.shape[0]
    c = A_CHUNK
    n = tb // c
    q = _silu(qp)
    fg = lb + (1.0 - lb) * _sigmoid(fp)
    logf = jnp.log(fg)
    k = 1.0 - fg
    b3 = _chunk_cumsum(logf, tri, tri_t).reshape(n, c, HEAD)
    pos = lax.broadcasted_iota(jnp.int32, (1, c, 1), 1)
    b_mid = lax.stop_gradient(jnp.sum(jnp.where(pos == c // 2, b3, 0.0), axis=1, keepdims=True))
    b_last = jnp.sum(jnp.where(pos == c - 1, b3, 0.0), axis=1, keepdims=True)
    q3, k3, v3 = q.reshape(n, c, HEAD), k.reshape(n, c, HEAD), ip.reshape(n, c, HEAD)
    scores = _dot_nt((q3 * jnp.exp(b3 - b_mid)).reshape(tb, HEAD), (k3 * jnp.exp(b_mid - b3)).reshape(tb, HEAD))
    o_intra = _dot_nn(jnp.where(bd_causal, scores, 0.0), ip)
    states, st_new = _state_scan(jnp.exp(b_last), _chunk_tn(v3, k3 * jnp.exp(b_last - b3)), st)
    o = o_intra + _chunk_nt(q3 * jnp.exp(b3), states).reshape(tb, HEAD)
    y = _rms(o) * ng * _silu(gp)
    return y, st_new


HG_HEADS = 2


def _hg_specs(tb, nh, rev_nb=None):
    wide = HG_HEADS * HEAD
    per = nh // HG_HEADS

    def row(part):
        if rev_nb is None:
            return pl.BlockSpec((tb, wide), functools.partial(lambda h, i, off: (i, off + h), off=part * per))
        return pl.BlockSpec((tb, wide),
                            functools.partial(lambda h, i, off: (rev_nb - 1 - i, off + h), off=part * per))
    return [row(0), row(1), row(2), row(3),
            pl.BlockSpec((1, wide), lambda h, i: (0, h)), pl.BlockSpec((1, HEAD), lambda h, i: (0, 0)),
            pl.BlockSpec((tb, tb), lambda h, i: (0, 0))]


def _hgrn2_fwd(proj, lb, ng, tb):
    t = proj.shape[0]
    nh = proj.shape[1] // (4 * HEAD)
    tb = min(tb, t)
    nb = t // tb
    wide = HG_HEADS * HEAD

    def body(q_ref, f_ref, i_ref, g_ref, lb_ref, ng_ref, mask_ref, y_ref, s_ref, st_ref):
        i = pl.program_id(1)

        @pl.when(i == 0)
        def _():
            st_ref[...] = jnp.zeros_like(st_ref)

        consts = _hg_consts(mask_ref[...])
        for p in range(HG_HEADS):
            cs = slice(p * HEAD, (p + 1) * HEAD)
            st = st_ref[p]
            s_ref[p, 0] = st
            y, st_new = _hg_block(q_ref[:, cs], f_ref[:, cs], i_ref[:, cs], g_ref[:, cs], lb_ref[:, cs],
                                  ng_ref[...], st, *consts)
            y_ref[:, cs] = y.astype(y_ref.dtype)
            st_ref[p] = st_new

    return pl.pallas_call(
        body, name="hgrn2_fwd",
        grid=(nh // HG_HEADS, nb),
        in_specs=_hg_specs(tb, nh),
        out_specs=[pl.BlockSpec((tb, wide), lambda h, i: (i, h)),
                   pl.BlockSpec((HG_HEADS, 1, HEAD, HEAD), lambda h, i: (h, i, 0, 0))],
        out_shape=[jax.ShapeDtypeStruct((t, nh * HEAD), BF16),
                   jax.ShapeDtypeStruct((nh, nb, HEAD, HEAD), F32)],
        scratch_shapes=[pltpu.VMEM((HG_HEADS, HEAD, HEAD), F32)],
        compiler_params=_cparams(dimension_semantics=("parallel", "arbitrary")),
    )(proj, proj, proj, proj, lb, ng, _hg_mask(tb))


def _hgrn2_bwd(proj, lb, ng, states, dy, tb):
    t = proj.shape[0]
    nh = proj.shape[1] // (4 * HEAD)
    tb = min(tb, t)
    nb = t // tb
    wide = HG_HEADS * HEAD

    def body(q_ref, f_ref, i_ref, g_ref, lb_ref, ng_ref, mask_ref, s_ref, dy_ref,
             dp_ref, dlb_ref, dng_ref, dst_ref):
        h, i = pl.program_id(0), pl.program_id(1)
        consts = _hg_consts(mask_ref[...])

        @pl.when(i == 0)
        def _():
            dst_ref[...] = jnp.zeros_like(dst_ref)
            dlb_ref[...] = jnp.zeros_like(dlb_ref)

        @pl.when(jnp.logical_and(i == 0, h == 0))
        def _():
            dng_ref[...] = jnp.zeros_like(dng_ref)

        def fn(qp, fp, ip, gp, lbx, ngx, stx):
            return _hg_block(qp, fp, ip, gp, lbx, ngx, stx, *consts)

        for p in range(HG_HEADS):
            cs = slice(p * HEAD, (p + 1) * HEAD)
            _, vjp_fn = jax.vjp(fn, q_ref[:, cs], f_ref[:, cs], i_ref[:, cs], g_ref[:, cs], lb_ref[:, cs],
                                ng_ref[...], s_ref[p, 0])
            *gparts, glb, gng, dst = vjp_fn((dy_ref[:, cs].astype(F32), dst_ref[p]))
            for part, gpart in enumerate(gparts):
                dp_ref[part, :, cs] = gpart.astype(dp_ref.dtype)
            dst_ref[p] = dst
            dlb_ref[:, cs] += glb
            dng_ref[...] += gng

    rev = lambda h, i: (nb - 1 - i, h)
    return pl.pallas_call(
        body, name="hgrn2_bwd",
        grid=(nh // HG_HEADS, nb),
        in_specs=_hg_specs(tb, nh, rev_nb=nb) + [
            pl.BlockSpec((HG_HEADS, 1, HEAD, HEAD), lambda h, i: (h, nb - 1 - i, 0, 0)),
            pl.BlockSpec((tb, wide), rev)],
        out_specs=[pl.BlockSpec((4, tb, wide), lambda h, i: (0, nb - 1 - i, h)),
                   pl.BlockSpec((1, wide), lambda h, i: (0, h)), pl.BlockSpec((1, HEAD), lambda h, i: (0, 0))],
        out_shape=[jax.ShapeDtypeStruct((4, t, nh * HEAD), BF16),
                   jax.ShapeDtypeStruct((1, nh * HEAD), F32), jax.ShapeDtypeStruct((1, HEAD), F32)],
        scratch_shapes=[pltpu.VMEM((HG_HEADS, HEAD, HEAD), F32)],
        compiler_params=_cparams(dimension_semantics=("arbitrary", "arbitrary")),
    )(proj, proj, proj, proj, lb, ng, _hg_mask(tb), states, dy)


def _fgate_consts(cb):
    r = lax.broadcasted_iota(jnp.int32, (cb, cb), 0)
    s = lax.broadcasted_iota(jnp.int32, (cb, cb), 1)
    return (r <= s).astype(F32), (r >= s).astype(F32)


def _fgate_fwd(xt, bias, cb=512):
    nh, t = xt.shape
    cb = min(cb, t)

    def body(x_ref, b_ref, o_ref):
        upper, _ = _fgate_consts(cb)
        carry = jnp.zeros((nh, 1), F32)
        for blk in range(t // cb):
            z = x_ref[:, blk * cb:(blk + 1) * cb] + b_ref[...]
            logf = jnp.minimum(z, 0.0) - jnp.log(1.0 + jnp.exp(-jnp.abs(z)))
            cs = _f32dot(logf, upper) + carry
            o_ref[:, blk * cb:(blk + 1) * cb] = cs
            carry = cs[:, cb - 1:cb]

    vm = pl.BlockSpec(memory_space=pltpu.VMEM)
    return pl.pallas_call(
        body, name="fgate_fwd", in_specs=[vm, vm], out_specs=vm,
        out_shape=jax.ShapeDtypeStruct((nh, t), F32), compiler_params=_cparams(),
    )(xt, bias)


def _fgate_bwd(xt, bias, dft, cb=512):
    nh, t = xt.shape
    cb = min(cb, t)
    nblk = t // cb

    def body(x_ref, b_ref, d_ref, dx_ref, db_ref):
        _, lower = _fgate_consts(cb)
        carry = jnp.zeros((nh, 1), F32)
        db = jnp.zeros((nh, 1), F32)
        for blk in range(nblk - 1, -1, -1):
            sl = slice(blk * cb, (blk + 1) * cb)
            dlogf = _f32dot(d_ref[:, sl], lower) + carry
            carry = dlogf[:, 0:1]
            z = x_ref[:, sl] + b_ref[...]
            dz = dlogf * (1.0 - _sigmoid(z))
            dx_ref[:, sl] = dz
            db = db + jnp.sum(dz, axis=1, keepdims=True)
        db_ref[...] = db

    vm = pl.BlockSpec(memory_space=pltpu.VMEM)
    return pl.pallas_call(
        body, name="fgate_bwd", in_specs=[vm, vm, vm], out_specs=[vm, vm],
        out_shape=[jax.ShapeDtypeStruct((nh, t), F32), jax.ShapeDtypeStruct((nh, 1), F32)],
        compiler_params=_cparams(),
    )(xt, bias, dft)


ATTN_GROUPS = 4


def _attn_fwd(q, k, v, f_grp, blk):
    t, width = v.shape
    nh = width // HEAD
    nq = t // blk
    hpg = nh // ATTN_GROUPS

    def body(q_ref, k_ref, v_ref, fc_ref, o_ref, lse_ref):
        i = pl.program_id(0)
        tri = (lax.broadcasted_iota(jnp.int32, (blk, blk), 1) <= lax.broadcasted_iota(jnp.int32, (blk, blk), 0))
        for h in range(nh):
            cs = slice(h * HEAD, (h + 1) * HEAD)
            cs2 = slice(2 * h * HEAD, 2 * (h + 1) * HEAD)
            qh = q_ref[:, cs2]

            def tile(j, carry, masked):
                m, l, acc = carry
                rs = pl.ds(pl.multiple_of(j * blk, blk), blk)
                s = _bdot_raw(qh, k_ref[rs, cs2], _NT)
                if masked:
                    s = jnp.where(tri, s, NEG_INF)
                m_new = jnp.maximum(m, jnp.max(s, axis=1, keepdims=True))
                p = jnp.exp(s - m_new)
                alpha = jnp.exp(m - m_new)
                l_new = alpha * l + jnp.sum(p, axis=1, keepdims=True)
                acc_new = alpha * acc + _bdot_raw(p, v_ref[rs, cs], _NN)
                return m_new, l_new, acc_new

            init = (jnp.full((blk, 1), NEG_INF, F32), jnp.zeros((blk, 1), F32), jnp.zeros((blk, HEAD), F32))
            carry = lax.fori_loop(0, i, lambda j, c: tile(j, c, False), init)
            m, l, acc = tile(i, carry, True)
            o_ref[:, cs] = acc / l
            g, hh = divmod(h, hpg)
            lse_ref[g, :, hh:hh + 1] = m + jnp.log(l) + fc_ref[g, :, hh:hh + 1]

    vm = pl.BlockSpec(memory_space=pltpu.VMEM)
    stat = pl.BlockSpec((ATTN_GROUPS, blk, hpg), lambda i: (0, i, 0))
    return pl.pallas_call(
        body, name="fox_attn_fwd",
        grid=(nq,),
        in_specs=[pl.BlockSpec((blk, 2 * width), lambda i: (i, 0)), vm, vm, stat],
        out_specs=[pl.BlockSpec((blk, width), lambda i: (i, 0)), stat],
        out_shape=[jax.ShapeDtypeStruct((t, width), F32), jax.ShapeDtypeStruct((ATTN_GROUPS, t, hpg), F32)],
        compiler_params=_cparams(dimension_semantics=("parallel",)),
    )(q, k, v, f_grp)


def _attn_delta(do, o, tb):
    t, width = o.shape
    nh = width // HEAD
    hpg = nh // ATTN_GROUPS
    tb = min(tb, t)

    def body(do_ref, o_ref, dl_ref):
        for h in range(nh):
            cs = slice(h * HEAD, (h + 1) * HEAD)
            g, hh = divmod(h, hpg)
            dl_ref[g, :, hh:hh + 1] = jnp.sum(do_ref[:, cs].astype(F32) * o_ref[:, cs], axis=1, keepdims=True)

    wide = pl.BlockSpec((tb, width), lambda i: (i, 0))
    return pl.pallas_call(body, name="fox_attn_delta", grid=(t // tb,), in_specs=[wide, wide],
                          out_specs=pl.BlockSpec((ATTN_GROUPS, tb, hpg), lambda i: (0, i, 0)),
                          out_shape=jax.ShapeDtypeStruct((ATTN_GROUPS, t, hpg), F32),
                          compiler_params=_cparams(dimension_semantics=("parallel",)))(do, o)


def _attn_bwd(q, k, v, f_grp, do, lse, delta, blk):
    t, width = v.shape
    nh = width // HEAD
    nq = t // blk
    hpg = nh // ATTN_GROUPS
    gw = hpg * HEAD

    def body(q_ref, do_ref, k_ref, v_ref, fc_ref, lse_ref, dl_ref,
             dq_ref, dk_ref, dv_ref, dfc_ref, dfr_ref):
        g, j = pl.program_id(0), pl.program_id(1)
        tri = (lax.broadcasted_iota(jnp.int32, (blk, blk), 1) <= lax.broadcasted_iota(jnp.int32, (blk, blk), 0))

        @pl.when(j == 0)
        def _():
            dq_ref[...] = jnp.zeros_like(dq_ref)
            dfc_ref[...] = jnp.zeros_like(dfc_ref)

        for h in range(hpg):
            cs = slice(h * HEAD, (h + 1) * HEAD)
            cs2 = slice(2 * h * HEAD, 2 * (h + 1) * HEAD)
            csq = slice(2 * h * HEAD, (2 * h + 1) * HEAD)
            kj2 = k_ref[:, cs2]
            kj = k_ref[:, csq]
            vj = v_ref[:, cs]

            def tile(i, carry, masked):
                dk, dv, dfs = carry
                rs = pl.ds(pl.multiple_of(i * blk, blk), blk)
                qi = q_ref[rs, csq]
                doi = do_ref[rs, cs]
                bias = fc_ref[0, rs, h:h + 1] - lse_ref[0, rs, h:h + 1]
                p = jnp.exp(_bdot_raw(q_ref[rs, cs2], kj2, _NT) + bias)
                if masked:
                    p = jnp.where(tri, p, 0.0)
                ds = p * (_bdot_raw(doi, vj, _NT) - dl_ref[0, rs, h:h + 1])
                dsb = ds.astype(BF16)
                dq_ref[rs, cs] += _bdot_raw(dsb, kj, _NN)
                dfc_ref[0, rs, h:h + 1] += jnp.sum(ds, axis=1, keepdims=True)
                return (dk + _bdot_raw(dsb, qi, _TN), dv + _bdot_raw(p, doi, _TN),
                        dfs - jnp.sum(ds, axis=0, keepdims=True))

            init = (jnp.zeros((blk, HEAD), F32), jnp.zeros((blk, HEAD), F32), jnp.zeros((1, blk), F32))
            carry = tile(j, init, True)
            dk, dv, dfs = lax.fori_loop(j + 1, nq, lambda i, c: tile(i, c, False), carry)
            dk_ref[:, cs] = dk
            dv_ref[:, cs] = dv.astype(dv_ref.dtype)
            dfr_ref[0, 0, h:h + 1, :] = dfs

    once = pl.Buffered(1)
    stat = pl.BlockSpec((1, t, hpg), lambda g, j: (g, 0, 0), pipeline_mode=once)
    kv_blk = pl.BlockSpec((blk, gw), lambda g, j: (j, g))
    frow = pl.BlockSpec((1, 1, hpg, blk), lambda g, j: (g, j, 0, 0))
    dq, dk, dv, dfc, dfr = pl.pallas_call(
        body, name="fox_attn_bwd",
        grid=(ATTN_GROUPS, nq),
        in_specs=[pl.BlockSpec((t, 2 * gw), lambda g, j: (0, g), pipeline_mode=once),
                  pl.BlockSpec((t, gw), lambda g, j: (0, g), pipeline_mode=once),
                  pl.BlockSpec((blk, 2 * gw), lambda g, j: (j, g)), kv_blk, stat, stat, stat],
        out_specs=[pl.BlockSpec((t, gw), lambda g, j: (0, g)), kv_blk, kv_blk,
                   pl.BlockSpec((1, t, hpg), lambda g, j: (g, 0, 0)), frow],
        out_shape=[jax.ShapeDtypeStruct((t, width), F32), jax.ShapeDtypeStruct((t, width), F32),
                   jax.ShapeDtypeStruct((t, width), BF16), jax.ShapeDtypeStruct((ATTN_GROUPS, t, hpg), F32),
                   jax.ShapeDtypeStruct((ATTN_GROUPS, nq, hpg, blk), F32)],
        compiler_params=_cparams(dimension_semantics=("parallel", "arbitrary")),
    )(q, do, k, v, f_grp, lse, delta)
    return dq, dk, dv, dfc, dfr


SUBLANES = 8


def _shift_down(u, n):
    r = pltpu.roll(u, n, 0)
    row = lax.broadcasted_iota(jnp.int32, (SUBLANES, u.shape[1]), 0)
    return jnp.concatenate([jnp.where(row < n, 0.0, r[:SUBLANES]), r[SUBLANES:]], axis=0)


def _shift_up(u, n):
    t = u.shape[0]
    r = pltpu.roll(u, t - n, 0)
    row = lax.broadcasted_iota(jnp.int32, (SUBLANES, u.shape[1]), 0)
    return jnp.concatenate([r[:t - SUBLANES], jnp.where(row >= SUBLANES - n, 0.0, r[t - SUBLANES:])], axis=0)


def _convglu_specs(t):
    return [pl.BlockSpec((2, t, LANES), lambda j: (0, 0, j)),
            pl.BlockSpec((2, CONV_TAPS, LANES), lambda j: (0, 0, j)),
            pl.BlockSpec((2, 1, LANES), lambda j: (0, 0, j))]


def _convglu_fwd(u, cw, cb):
    _, t, fp = u.shape

    def body(u_ref, w_ref, b_ref, a_ref, c_ref):
        c = []
        for hf in range(2):
            uv, w = u_ref[hf].astype(F32), w_ref[hf]
            c.append(w[0:1] * _shift_down(uv, 2) + w[1:2] * _shift_down(uv, 1) + w[2:3] * uv + b_ref[hf])
            c_ref[hf] = c[hf].astype(c_ref.dtype)
        a_ref[...] = (_silu(c[0]) * c[1]).astype(a_ref.dtype)

    return pl.pallas_call(
        body, name="convglu_fwd",
        grid=(fp // LANES,),
        in_specs=_convglu_specs(t),
        out_specs=[pl.BlockSpec((t, LANES), lambda j: (0, j)), pl.BlockSpec((2, t, LANES), lambda j: (0, 0, j))],
        out_shape=[jax.ShapeDtypeStruct((t, fp), BF16), jax.ShapeDtypeStruct((2, t, fp), BF16)],
        compiler_params=_cparams(dimension_semantics=("parallel",)),
    )(u, cw, cb)


def _convglu_bwd(u, c, cw, da):
    _, t, fp = u.shape

    def body(u_ref, c_ref, w_ref, da_ref, du_ref, dw_ref, db_ref):
        gc, vc = c_ref[0].astype(F32), c_ref[1].astype(F32)
        sg = _sigmoid(gc)
        dav = da_ref[...].astype(F32)
        dcs = [dav * vc * (sg * (1.0 + gc * (1.0 - sg))), dav * (gc * sg)]
        for hf in range(2):
            dc, w, uv = dcs[hf], w_ref[hf], u_ref[hf].astype(F32)
            dc1, dc2 = _shift_up(dc, 1), _shift_up(dc, 2)
            du_ref[hf] = (w[2:3] * dc + w[1:2] * dc1 + w[0:1] * dc2).astype(du_ref.dtype)
            dw_ref[hf, 0:1, :] = jnp.sum(dc2 * uv, axis=0, keepdims=True)
            dw_ref[hf, 1:2, :] = jnp.sum(dc1 * uv, axis=0, keepdims=True)
            dw_ref[hf, 2:3, :] = jnp.sum(dc * uv, axis=0, keepdims=True)
            db_ref[hf] = jnp.sum(dc, axis=0, keepdims=True)

    pair, taps, bias = _convglu_specs(t)
    return pl.pallas_call(
        body, name="convglu_bwd",
        grid=(fp // LANES,),
        in_specs=[pair, pair, taps, pl.BlockSpec((t, LANES), lambda j: (0, j))],
        out_specs=[pair, taps, bias],
        out_shape=[jax.ShapeDtypeStruct((2, t, fp), BF16), jax.ShapeDtypeStruct((2, CONV_TAPS, fp), F32),
                   jax.ShapeDtypeStruct((2, 1, fp), F32)],
        compiler_params=_cparams(dimension_semantics=("parallel",)),
    )(u, c, cw, da)


def _local_step(x, target, mods, lb, small, pre_w, get_w, put_g, *, tb=512, attn_blk=512):
    t, d = x.shape
    nh = d // HEAD
    nb = NDEV
    wts = {}
    vec = lambda *names: [mods[n] for n in names]

    def ffn_fwd(h2, l):
        u = _mm_wblk(h2, wts[f"up{l}"], BF16, f"ffn{l}_up", gb=nb // 2, split=2, tm=512)
        a, c = _convglu_fwd(u, small[f"conv_w{l}"], small[f"conv_b{l}"])
        f = _mm(a, wts[f"down{l}"], "nn", F32, f"ffn{l}_down", tk=4096)
        return (u, c), a, f

    def ffn_bwd(df, h2, uc, a, l):
        u, c = uc
        da = _mm(df, wts[f"down{l}"], "nt", BF16, f"ffn{l}_down_dx", tn=1536)
        dwd = _mm(a, df, "tn", BF16, f"ffn{l}_down_dw", tm=768, tk=t)
        du, dcw, dcb = _convglu_bwd(u, c, small[f"conv_w{l}"], da)
        dh2 = _mm_wblk_dx(du, wts[f"up{l}"], BF16, f"ffn{l}_up_dx", k=d, gb=nb // 2, split=2, tm=1024)
        dwu = _mm_wblk_dw(h2, du, f"ffn{l}_up_dw", nb=nb, gb=1, split=2, tk=t)
        return dh2, dwu, dwd, dcw, dcb

    (h_a,) = _row_fwd(_f_mod, [(x, d, 0)], vec("sh1_0", "sc1_0"), [BF16], tb=tb, name="l0_mod1")
    wts.update(get_w("l0a", h_a))
    proj_a = _mm_wblk(h_a, wts["a_in"], F32, "a_in", gb=nb // 2)
    ypre, states = _hgrn2_fwd(proj_a, lb, small["a_norm_g"], tb)
    pre_w("l0b", ypre)
    wts.update(get_w("l0b", ypre))
    y_a = _mm(ypre, wts["a_out"], "nn", F32, "a_out")
    x1, h2_0 = _row_fwd(_f_res_mod, [(x, d, 0), (y_a, d, 0)], vec("g1_0", "sh2_0", "sc2_0"), [F32, BF16],
                        tb=tb, name="l0_res_mod2")
    u0, a0, f0 = ffn_fwd(h2_0, 0)
    x2, h_kv, h_q = _row_fwd(_f_res_mod2, [(x1, d, 0), (f0, d, 0)],
                             [mods["g2_0"] + pre_w("l1", f0)] + vec("kv_sh", "kv_sc", "sh1_1", "sc1_1"),
                             [F32, BF16, BF16], tb=tb, name="l0_res_kvmod_qmod")
    wts.update(get_w("l1", h_kv))
    proj_k = _mm(h_kv, wts["kv_k"], "nt", F32, "k_proj")
    v_b = _mm(h_kv, wts["kv_v"], "nt", BF16, "v_proj")
    proj_f = _mm(h_kv, wts["kv_f"], "nt", F32, "kv_fproj")
    f_logit_t = proj_f[:, :nh].T
    f_bias = small["kv_b_f"].reshape(nh, 1)
    f_t = _fgate_fwd(f_logit_t, f_bias)
    f_grp = f_t.reshape(ATTN_GROUPS, nh // ATTN_GROUPS, t).transpose(0, 2, 1)
    (k_n,) = _row_fwd(_f_knorm_aug, [(proj_k, HEAD, 0)] + [(piece, 1, 0) for piece in _split3(-f_t.T)],
                      [small["k_norm_g"]], [BF16], nsub=nh, tb=tb, name="k_norm")
    proj_q = _mm_wblk(h_q, wts["b_q"], F32, "b_q", gb=nb)
    (q_n,) = _row_fwd(_f_qnorm_aug, [(proj_q, HEAD, 0)], [small["q_norm_g"]], [BF16], nsub=nh, tb=tb,
                      name="q_norm")
    o_att, lse = _attn_fwd(q_n, k_n, v_b, f_grp, attn_blk)
    (z,) = _row_fwd(_f_outgate, [(o_att, HEAD, 0), (proj_q, HEAD, 1)], [], [BF16], nsub=nh, tb=tb, name="out_gate")
    y_b = _mm(z, wts["b_out"], "nn", F32, "b_out")
    x3, h2_1 = _row_fwd(_f_res_mod, [(x2, d, 0), (y_b, d, 0)], vec("g1_1", "sh2_1", "sc2_1"), [F32, BF16],
                        tb=tb, name="l1_res_mod2")
    u1, a1, f1 = ffn_fwd(h2_1, 1)
    loss, dx4, df1, dg2_1 = _loss_call(x3, f1, mods["g2_1"], target, tb)

    g = {}
    dmods = {"g2_1": dg2_1}
    dh2, g["up1"], g["down1"], g["conv_w1"], g["conv_b1"] = ffn_bwd(df1, h2_1, u1, a1, 1)
    (dx2, dy_b), (dmods["g1_1"], dmods["sh2_1"], dmods["sc2_1"]) = _row_bwd(
        _f_res_mod, [(x2, d, 0), (y_b, d, 0)], vec("g1_1", "sh2_1", "sc2_1"),
        [(dx4, d, 0), (dh2, d, 0)], [F32, BF16], tb=tb, name="l1_res_mod2_bwd")
    dz = _mm(dy_b, wts["b_out"], "nt", BF16, "b_out_dx")
    g["b_out"] = _mm(z, dy_b, "tn", BF16, "b_out_dw", tk=t)
    (do_att, dog), _ = _row_bwd(_f_outgate, [(o_att, HEAD, 0), (proj_q, HEAD, 1)], [], [(dz, HEAD, 0)],
                                [BF16, BF16], nsub=nh, tb=tb, name="out_gate_bwd")
    delta = _attn_delta(do_att, o_att, tb)
    dq_n, dk_n, dv, dfc_q, dfr_k = _attn_bwd(q_n, k_n, v_b, f_grp, do_att, lse, delta, attn_blk)
    (dpq,), (g["q_norm_g"],) = _row_bwd(_f_qnorm, [(proj_q, HEAD, 0)], [small["q_norm_g"]],
                                        [(dq_n, HEAD, 0)], [BF16], nsub=nh, tb=tb, name="q_norm_bwd")
    dproj_q = jnp.concatenate([dpq, dog], axis=1)
    dh_q = _mm_wblk_dx(dproj_q, wts["b_q"], BF16, "b_q_dx", k=d, gb=nb)
    g["b_q"] = _mm_wblk_dw(h_q, dproj_q, "b_q_dw", nb=nb, gb=nb // 4, tk=t)
    (dpk,), (g["k_norm_g"],) = _row_bwd(_f_knorm, [(proj_k, HEAD, 0)], [small["k_norm_g"]],
                                        [(dk_n, HEAD, 0)], [BF16], nsub=nh, tb=tb, name="k_norm_bwd")
    df_t = dfc_q.transpose(0, 2, 1).reshape(nh, t) + dfr_k.transpose(0, 2, 1, 3).reshape(nh, t)
    dflogit_t, g["kv_b_f"] = _fgate_bwd(f_logit_t, f_bias, df_t)
    dproj_f = jnp.pad(dflogit_t.T, ((0, 0), (0, LANES - nh))).astype(BF16)
    dh_kv = _mm(dpk, wts["kv_k"], "nn", BF16, "k_proj_dx")
    dh_kv_v = _mm(dv, wts["kv_v"], "nn", BF16, "v_proj_dx")
    dh_kv_f = _mm(dproj_f, wts["kv_f"], "nn", BF16, "kv_fproj_dx")
    g["kv_k"] = _mm(dpk, h_kv, "tn", BF16, "k_proj_dw", tk=t)
    g["kv_v"] = _mm(dv, h_kv, "tn", BF16, "v_proj_dw", tk=t)
    g["kv_f"] = _mm(dproj_f, h_kv, "tn", F32, "kv_fproj_dw", tk=1024)
    sent = put_g("l1", {n: g.pop(n) for n in ("b_out", "b_q", "kv_k", "kv_v", "kv_f", "up1", "down1")})
    (dx1, df0), (dmods["g2_0"], dmods["kv_sh"], dmods["kv_sc"], dmods["sh1_1"], dmods["sc1_1"]) = _row_bwd(
        _f_res_mod2, [(x1, d, 0), (f0, d, 0)], [mods["g2_0"] + sent] + vec("kv_sh", "kv_sc", "sh1_1", "sc1_1"),
        [(dx2, d, 0), (dh_kv, d, 0), (dh_q, d, 0)], [F32, BF16], tb=tb, name="l0_res_kvmod_qmod_bwd",
        cot_add=[(1, dh_kv_v), (1, dh_kv_f)])
    dh2, g["up0"], g["down0"], g["conv_w0"], g["conv_b0"] = ffn_bwd(df0, h2_0, u0, a0, 0)
    (dx0, dy_a), (dmods["g1_0"], dmods["sh2_0"], dmods["sc2_0"]) = _row_bwd(
        _f_res_mod, [(x, d, 0), (y_a, d, 0)], vec("g1_0", "sh2_0", "sc2_0"),
        [(dx1, d, 0), (dh2, d, 0)], [F32, BF16], tb=tb, name="l0_res_mod2_bwd")
    dypre = _mm(dy_a, wts["a_out"], "nt", BF16, "a_out_dx")
    g["a_out"] = _mm(ypre, dy_a, "tn", BF16, "a_out_dw", tk=t)
    sent = put_g("l0b", {n: g.pop(n) for n in ("a_out", "up0", "down0")})
    dproj_a, dlb, g["a_norm_g"] = _hgrn2_bwd(proj_a, lb + sent, small["a_norm_g"], states, dypre, tb)
    dh_a = _mm_wblk_dx(dproj_a, wts["a_in"], BF16, "a_in_dx", k=d, gb=nb, split=4, tm=512)
    put_g("l0a", {"a_in": _mm_wblk_dw(h_a, dproj_a, "a_in_dw", nb=nb, gb=1, split=4, tk=t)})
    (grad_x,), (dmods["sh1_0"], dmods["sc1_0"]) = _row_bwd(
        _f_mod, [(x, d, 0)], vec("sh1_0", "sc1_0"), [(dh_a, d, 0)], [F32], tb=tb, name="l0_mod1_bwd",
        add_to=(0, dx0))
    return loss, grad_x, dmods, dlb, g


def _position():
    return lax.axis_index("x"), lax.axis_index("y"), lax.axis_index("c")


def _hbm_specs(n):
    return [pl.BlockSpec(memory_space=pl.ANY)] * n


def _all_gather(arrs, name):
    n = len(arrs)

    def body(*refs):
        x_refs, out_refs = refs[:n], refs[n:2 * n]
        send_sems, recv_sems, local_sems = refs[2 * n:]
        x, y, cc = _position()
        me, sibling = (x, y, cc), (x, y, 1 - cc)
        chips = [(1 - x, y), (x, 1 - y), (1 - x, 1 - y)]

        def copy(a, k, block, to, src=None):
            slot = out_refs[a].at[4 * block[0] + 2 * block[1] + block[2]]
            return pltpu.make_async_remote_copy(
                src_ref=slot if src is None else src, dst_ref=slot,
                send_sem=send_sems.at[7 * a + k], recv_sem=recv_sems.at[7 * a + k],
                device_id=to, device_id_type=_MESH)

        local = [pltpu.make_async_copy(x_refs[a], out_refs[a].at[4 * x + 2 * y + cc], local_sems.at[a])
                 for a in range(n)]
        for cp in local:
            cp.start()
        first = []
        for a in range(n):
            first.append(copy(a, 0, me, sibling, src=x_refs[a]))
            first += [copy(a, 1 + j, me, (*chip, cc), src=x_refs[a]) for j, chip in enumerate(chips)]
        for cp in first:
            cp.start()
        passed = []
        for j, chip in enumerate(chips):
            for a in range(n):
                copy(a, 1 + j, (*chip, cc), me).wait_recv()
                fwd = copy(a, 4 + j, (*chip, cc), sibling)
                fwd.start()
                passed.append(fwd)
        for a in range(n):
            copy(a, 0, sibling, me).wait_recv()
        for j, chip in enumerate(chips):
            for a in range(n):
                copy(a, 4 + j, (*chip, 1 - cc), me).wait_recv()
        for cp in first + passed:
            cp.wait_send()
        for cp in local:
            cp.wait()

    return pl.pallas_call(
        body, name=name,
        out_shape=[jax.ShapeDtypeStruct((NDEV, *a.shape), a.dtype) for a in arrs],
        in_specs=_hbm_specs(n), out_specs=_hbm_specs(n),
        scratch_shapes=[pltpu.SemaphoreType.DMA((7 * n,)), pltpu.SemaphoreType.DMA((7 * n,)),
                        pltpu.SemaphoreType.DMA((n,))],
    )(*arrs)


_XCHG_EFFECT = pltpu.SideEffectType.DATAFLOW_SIDE_EFFECTING
ALL_PEERS = (1, 2, 3, 4, 5, 6, 7)
SAME_CORE = (2, 4, 6)


def _xchg_copies(src_refs, land_refs, send_sems, recv_sems, local_sems, scatter, rels):
    x, y, cc = _position()
    me = 4 * x + 2 * y + cc
    remote, local = [], []
    for a, (src, land) in enumerate(zip(src_refs, land_refs)):
        local.append(pltpu.make_async_copy(src.at[me] if scatter else src, land.at[me], local_sems.at[a]))
        for idx, rel in enumerate(rels):
            px = 1 - x if rel & 4 else x
            py = 1 - y if rel & 2 else y
            pc = 1 - cc if rel & 1 else cc
            k = len(rels) * a + idx
            remote.append(pltpu.make_async_remote_copy(
                src_ref=src.at[4 * px + 2 * py + pc] if scatter else src, dst_ref=land.at[me],
                send_sem=send_sems.at[k], recv_sem=recv_sems.at[k], device_id=(px, py, pc), device_id_type=_MESH))
    return remote, local


def _xchg_start(srcs, scatter, rels, after, name):
    n = len(srcs)
    lands = [lax.empty(s.shape if scatter else (NDEV, *s.shape), s.dtype) for s in srcs]

    def body(*refs):
        remote, local = _xchg_copies(refs[:n], refs[n:2 * n], *refs[2 * n + 1:2 * n + 4], scatter, rels)
        for cp in local + remote:
            cp.start()
        token = refs[-1]
        token[...] = jnp.zeros_like(token)

    hbm = pl.BlockSpec(memory_space=pltpu.HBM)
    sem = pl.BlockSpec(memory_space=pltpu.SEMAPHORE)
    out = pl.pallas_call(
        body, name=name,
        out_shape=(pltpu.SemaphoreType.DMA((len(rels) * n,)), pltpu.SemaphoreType.DMA((len(rels) * n,)),
                   pltpu.SemaphoreType.DMA((n,)),
                   *[pltpu.HBM(a.shape, a.dtype) for a in srcs + lands], jax.ShapeDtypeStruct((8, LANES), F32)),
        in_specs=[hbm] * (2 * n) + [pl.BlockSpec(memory_space=pl.ANY)],
        out_specs=(sem, sem, sem, *[hbm] * (2 * n), pl.BlockSpec(memory_space=pltpu.VMEM)),
        input_output_aliases={i: 3 + i for i in range(2 * n)},
        compiler_params=pltpu.CompilerParams(has_side_effects=_XCHG_EFFECT),
    )(*[pltpu.with_memory_space_constraint(a, pltpu.HBM) for a in srcs + lands], after)
    return out[:-1], out[-1][0, 0]


def _xchg_wait(handles, after, scatter, rels, name):
    n = (len(handles) - 3) // 2

    def body(*refs):
        remote, local = _xchg_copies(refs[:n], refs[n:2 * n], *refs[2 * n:2 * n + 3], scatter, rels)
        for cp in remote:
            cp.wait_send()
            cp.wait_recv()
        for cp in local:
            cp.wait()

    hbm = pl.BlockSpec(memory_space=pltpu.HBM)
    sem = pl.BlockSpec(memory_space=pltpu.SEMAPHORE)
    thru = list(handles[3:])
    out = pl.pallas_call(
        body, name=name,
        out_shape=tuple(pltpu.HBM(a.shape, a.dtype) for a in thru),
        in_specs=[hbm] * (2 * n) + [sem, sem, sem, pl.BlockSpec(memory_space=pl.ANY)],
        out_specs=tuple([hbm] * (2 * n)),
        input_output_aliases={i: i for i in range(2 * n)},
        compiler_params=pltpu.CompilerParams(has_side_effects=_XCHG_EFFECT),
    )(*thru, *handles[:3], after)
    return list(out[n:])


def _sibling_copies(land_refs, send_sems, recv_sems):
    x, y, cc = _position()

    def copy(a, q, core):
        slot = land_refs[a].at[2 * q + core]
        return pltpu.make_async_remote_copy(
            src_ref=slot, dst_ref=slot, send_sem=send_sems.at[NCHIP * a + q], recv_sem=recv_sems.at[NCHIP * a + q],
            device_id=(x, y, 1 - cc), device_id_type=_MESH)

    pairs = [(a, q) for a in range(len(land_refs)) for q in range(NCHIP)]
    return [copy(a, q, cc) for a, q in pairs], [copy(a, q, 1 - cc) for a, q in pairs]


def _sibling_forward_start(lands, name):
    n = len(lands)

    def body(*refs):
        sends, _ = _sibling_copies(refs[:n], refs[n], refs[n + 1])
        for cp in sends:
            cp.start()
        refs[-1][...] = jnp.zeros_like(refs[-1])

    hbm = pl.BlockSpec(memory_space=pltpu.HBM)
    sem = pl.BlockSpec(memory_space=pltpu.SEMAPHORE)
    out = pl.pallas_call(
        body, name=name,
        out_shape=(pltpu.SemaphoreType.DMA((NCHIP * n,)), pltpu.SemaphoreType.DMA((NCHIP * n,)),
                   *[pltpu.HBM(a.shape, a.dtype) for a in lands], jax.ShapeDtypeStruct((8, LANES), F32)),
        in_specs=[hbm] * n,
        out_specs=(sem, sem, *[hbm] * n, pl.BlockSpec(memory_space=pltpu.VMEM)),
        input_output_aliases={i: 2 + i for i in range(n)},
        compiler_params=pltpu.CompilerParams(has_side_effects=_XCHG_EFFECT),
    )(*lands)
    return out[:-1], out[-1][0, 0]


def _sibling_forward_wait(handles, after, name):
    n = len(handles) - 2

    def body(*refs):
        sends, arrivals = _sibling_copies(refs[:n], refs[n], refs[n + 1])
        for cp in sends:
            cp.wait_send()
        for cp in arrivals:
            cp.wait_recv()

    hbm = pl.BlockSpec(memory_space=pltpu.HBM)
    sem = pl.BlockSpec(memory_space=pltpu.SEMAPHORE)
    lands = list(handles[2:])
    return list(pl.pallas_call(
        body, name=name,
        out_shape=tuple(pltpu.HBM(a.shape, a.dtype) for a in lands),
        in_specs=[hbm] * n + [sem, sem, pl.BlockSpec(memory_space=pl.ANY)],
        out_specs=tuple([hbm] * n),
        input_output_aliases={i: i for i in range(n)},
        compiler_params=pltpu.CompilerParams(has_side_effects=_XCHG_EFFECT),
    )(*lands, *handles[:2], after))


def _slab_sum(slabs, name, tr=None):
    n, r, c = slabs.shape
    tr = r if tr is None else tr

    def body(s_ref, o_ref):
        acc = s_ref[0].astype(F32)
        for q in range(1, n):
            acc = acc + s_ref[q].astype(F32)
        o_ref[...] = acc

    return pl.pallas_call(body, name=name, grid=(r // tr,),
                          in_specs=[pl.BlockSpec((n, tr, c), lambda i: (0, i, 0))],
                          out_specs=pl.BlockSpec((tr, c), lambda i: (i, 0)),
                          out_shape=jax.ShapeDtypeStruct((r, c), F32),
                          compiler_params=_cparams(dimension_semantics=("parallel",)))(slabs)


def _ada_fwd(c_all, ada_w, kv_ada_w, logits):
    rows, d = c_all.shape
    n0, nkv = ada_w.shape[2], kv_ada_w.shape[1]

    def body(c_ref, w_ref, kw_ref, lg_ref, part_ref, cact_ref, lb_ref):
        ca = _silu(c_ref[...])
        cact_ref[...] = ca
        part_ref[:, 0:n0] = _bdot_raw(ca, w_ref[0], _NN)
        part_ref[:, n0:2 * n0] = _bdot_raw(ca, w_ref[1], _NN)
        part_ref[:, 2 * n0:2 * n0 + nkv] = _bdot_raw(ca, kw_ref[...], _NN)
        lb_ref[...] = _sigmoid(lg_ref[0:1, :] - lg_ref[1:2, :])

    vm = pl.BlockSpec(memory_space=pltpu.VMEM)
    return pl.pallas_call(
        body, name="ada_fwd", in_specs=[vm, vm, vm, vm], out_specs=[vm, vm, vm],
        out_shape=[jax.ShapeDtypeStruct((rows, 2 * n0 + nkv), F32), jax.ShapeDtypeStruct((rows, d), F32),
                   jax.ShapeDtypeStruct((1, d), F32)],
        compiler_params=_cparams(),
    )(c_all, ada_w, kv_ada_w, logits)


def _ada_bwd(c_act, dm0, dm1, dkv, lb, dlb):
    rows, d = c_act.shape

    def body(c_ref, d0_ref, d1_ref, dk_ref, lb_ref, dlb_ref, dw_ref, dkw_ref, dlg_ref):
        ca = c_ref[...]
        dw_ref[0] = _bdot_raw(ca, d0_ref[...], _TN)
        dw_ref[1] = _bdot_raw(ca, d1_ref[...], _TN)
        dkw_ref[...] = _bdot_raw(ca, dk_ref[...], _TN)
        lbv = lb_ref[...]
        dl0 = dlb_ref[...] * lbv * (1.0 - lbv)
        dlg_ref[0:1, :] = dl0
        dlg_ref[1:2, :] = -dl0

    vm = pl.BlockSpec(memory_space=pltpu.VMEM)
    return pl.pallas_call(
        body, name="ada_bwd", in_specs=[vm] * 6, out_specs=[vm, vm, vm],
        out_shape=[jax.ShapeDtypeStruct((2, d, dm0.shape[1]), F32), jax.ShapeDtypeStruct((d, dkv.shape[1]), F32),
                   jax.ShapeDtypeStruct((2, d), F32)],
        compiler_params=_cparams(),
    )(c_act, dm0, dm1, dkv, lb, dlb)


def _adamw(w, g, m, v, name, tr=512, after=None):
    r, c = w.shape
    tr = _divisor_tile(r, tr, unit=8)
    c1 = 1.0 - ADAM_B1 ** ADAM_STEP
    c2 = 1.0 - ADAM_B2 ** ADAM_STEP
    deps = [] if after is None else [after]

    def body(w_ref, g_ref, m_ref, v_ref, *rest):
        d_ref, mo_ref, vo_ref = rest[len(deps):]
        gv = g_ref[...]
        mn = ADAM_B1 * m_ref[...] + (1.0 - ADAM_B1) * gv
        vn = ADAM_B2 * v_ref[...] + (1.0 - ADAM_B2) * (gv * gv)
        d_ref[...] = -ADAM_LR * ((mn / c1) / (jnp.sqrt(vn / c2) + ADAM_EPS) + ADAM_WD * w_ref[...])
        mo_ref[...] = mn
        vo_ref[...] = vn

    spec = pl.BlockSpec((tr, c), lambda i: (i, 0))
    out = jax.ShapeDtypeStruct((r, c), F32)
    return pl.pallas_call(body, name=name, grid=(r // tr,),
                          in_specs=[spec] * 4 + [pl.BlockSpec(a.shape, lambda i: (0, 0)) for a in deps],
                          out_specs=[spec] * 3, out_shape=[out, out, out],
                          compiler_params=_cparams(dimension_semantics=("parallel",)))(w, g, m, v, *deps)


def _pad_rows(a, rows):
    return jnp.pad(a, ((0, rows - a.shape[0]), (0, 0)))


def _pack_small(parts, lanes=LANES, row_unit=8):
    flat = jnp.concatenate([p.reshape(-1).astype(F32) for p in parts])
    rows = _round_up(-(-flat.shape[0] // lanes), row_unit)
    return jnp.pad(flat, (0, rows * lanes - flat.shape[0])).reshape(rows, lanes)


def _unpack_small(flat, shapes):
    out, off = [], 0
    for s in shapes:
        n = 1
        for k in s:
            n *= k
        out.append(flat[off:off + n].reshape(s))
        off += n
    return out


def _pad_shard_cols(a, n_loc, n_pad):
    lead = a.shape[:-1]
    a = a.reshape(*lead, NDEV, n_loc)
    a = jnp.pad(a, [(0, 0)] * (len(lead) + 1) + [(0, n_pad - n_loc)])
    return a.reshape(*lead, NDEV * n_pad)


def _unpad_shard_cols(a, n_loc, n_pad):
    lead = a.shape[:-1]
    return a.reshape(*lead, NDEV, n_pad)[..., :n_loc].reshape(*lead, NDEV * n_loc)


def kernel(x, c, ada_w, ada_b, a_w_in, a_lb_logits, a_norm_g, a_w_out, kv_ada_w, kv_ada_b, kv_w, kv_b_f, k_norm_g, b_w_q, q_norm_g, b_w_out, ffn_w_up, ffn_conv_w, ffn_conv_b, ffn_w_down, loss_target, m_ada_w, m_ada_b, m_a_w_in, m_a_lb_logits, m_a_norm_g, m_a_w_out, m_kv_ada_w, m_kv_ada_b, m_kv_w, m_kv_b_f, m_k_norm_g, m_b_w_q, m_q_norm_g, m_b_w_out, m_ffn_w_up, m_ffn_conv_w, m_ffn_conv_b, m_ffn_w_down, v_ada_w, v_ada_b, v_a_w_in, v_a_lb_logits, v_a_norm_g, v_a_w_out, v_kv_ada_w, v_kv_ada_b, v_kv_w, v_kv_b_f, v_k_norm_g, v_b_w_q, v_q_norm_g, v_b_w_out, v_ffn_w_up, v_ffn_conv_w, v_ffn_conv_b, v_ffn_w_down):
    t, d = x.shape[1], x.shape[2]
    nh = d // HEAD
    ncw = ffn_w_up.shape[2]
    ncp = _round_up(ncw, LANES)
    two_f = ncw * NDEV
    ff = two_f // 2
    fp = ncp * NDEV // 2
    rd = ffn_w_down.shape[1]
    me = 4 * lax.axis_index("x") + 2 * lax.axis_index("y") + lax.axis_index("c")
    weights = dict(ada_w=ada_w, ada_b=ada_b, a_w_in=a_w_in, a_lb_logits=a_lb_logits, a_norm_g=a_norm_g,
                   a_w_out=a_w_out, kv_ada_w=kv_ada_w, kv_ada_b=kv_ada_b, kv_w=kv_w, kv_b_f=kv_b_f,
                   k_norm_g=k_norm_g, b_w_q=b_w_q, q_norm_g=q_norm_g, b_w_out=b_w_out, ffn_w_up=ffn_w_up,
                   ffn_conv_w=ffn_conv_w, ffn_conv_b=ffn_conv_b, ffn_w_down=ffn_w_down)
    m_in = dict(ada_w=m_ada_w, ada_b=m_ada_b, a_w_in=m_a_w_in, a_lb_logits=m_a_lb_logits, a_norm_g=m_a_norm_g,
                a_w_out=m_a_w_out, kv_ada_w=m_kv_ada_w, kv_ada_b=m_kv_ada_b, kv_w=m_kv_w, kv_b_f=m_kv_b_f,
                k_norm_g=m_k_norm_g, b_w_q=m_b_w_q, q_norm_g=m_q_norm_g, b_w_out=m_b_w_out, ffn_w_up=m_ffn_w_up,
                ffn_conv_w=m_ffn_conv_w, ffn_conv_b=m_ffn_conv_b, ffn_w_down=m_ffn_w_down)
    v_in = dict(ada_w=v_ada_w, ada_b=v_ada_b, a_w_in=v_a_w_in, a_lb_logits=v_a_lb_logits, a_norm_g=v_a_norm_g,
                a_w_out=v_a_w_out, kv_ada_w=v_kv_ada_w, kv_ada_b=v_kv_ada_b, kv_w=v_kv_w, kv_b_f=v_kv_b_f,
                k_norm_g=v_k_norm_g, b_w_q=v_b_w_q, q_norm_g=v_q_norm_g, b_w_out=v_b_w_out, ffn_w_up=v_ffn_w_up,
                ffn_conv_w=v_ffn_conv_w, ffn_conv_b=v_ffn_conv_b, ffn_w_down=v_ffn_w_down)
    order = list(weights)

    up_loc = jnp.pad(ffn_w_up, ((0, 0), (0, 0), (0, ncp - ncw))).astype(BF16)
    down_loc = ffn_w_down.astype(BF16)
    gather_names = {"l0b": ["a_out", "up0", "down0"], "l1": ["kv", "b_q", "b_out", "up1", "down1"]}
    shards = {"a_out": a_w_out[0].astype(BF16), "up0": up_loc[0], "down0": down_loc[0], "kv": kv_w.T.astype(BF16),
              "b_q": b_w_q[0].astype(BF16), "b_out": b_w_out[0].astype(BF16), "up1": up_loc[1],
              "down1": down_loc[1]}
    pre = _pack_small([c, a_lb_logits, ffn_conv_w])
    a_in_all, pre_all = _all_gather([a_w_in[0].astype(BF16), pre], "gather_a_w_in_and_small_inputs")
    pre_all = pre_all.reshape(NDEV, -1)
    c_all = pre_all[:, :d]
    logits = pre_all[:, d:d + 2 * HEAD].reshape(NDEV, 2, HEAD).transpose(1, 0, 2).reshape(2, d)
    conv_w_full = pre_all[:, d + 2 * HEAD:d + 2 * HEAD + 2 * CONV_TAPS * ncw]
    conv_w_full = conv_w_full.reshape(NDEV, 2, CONV_TAPS, ncw).transpose(1, 2, 0, 3).reshape(2, CONV_TAPS, two_f)

    part, c_act, lb = _ada_fwd(_pad_rows(c_all, 2 * NDEV), ada_w, kv_ada_w, logits)
    (part_all,) = _all_gather([part[:NDEV]], "gather_adaln")
    mine = lax.dynamic_index_in_dim(part_all, me, axis=1, keepdims=False)
    n0, nkv = ada_w.shape[2], kv_ada_w.shape[1]
    mod_names = ["sh1", "sc1", "g1", "sh2", "sc2", "g2"]
    mods = {}
    for l in range(2):
        row = mine[:, l * n0:(l + 1) * n0].reshape(-1) + ada_b[l]
        for k, nm in enumerate(mod_names):
            mods[f"{nm}_{l}"] = row[k * d:(k + 1) * d].reshape(1, d)
    kvrow = mine[:, 2 * n0:2 * n0 + nkv].reshape(-1) + kv_ada_b
    mods["kv_sh"], mods["kv_sc"] = kvrow[:d].reshape(1, d), kvrow[d:].reshape(1, d)

    in_flight = {}

    def start_gather(grp, dep):
        srcs = [shards[n] for n in gather_names[grp]]
        in_flight[grp], started = _xchg_start(srcs, False, SAME_CORE, dep, f"gather_{grp}_start")
        return started

    zero = start_gather("l0b", part_all)
    mods["sh1_0"] = mods["sh1_0"] + zero

    small = {"a_norm_g": a_norm_g, "k_norm_g": k_norm_g.reshape(1, HEAD), "q_norm_g": q_norm_g, "kv_b_f": kv_b_f}
    for l in range(2):
        small[f"conv_w{l}"] = _pad_shard_cols(conv_w_full[l], ncw, ncp).reshape(CONV_TAPS, 2, fp).transpose(1, 0, 2)
        small[f"conv_b{l}"] = _pad_shard_cols(ffn_conv_b[l], ncw, ncp).reshape(2, 1, fp)

    forwarding = {}

    def pre_w(grp, after):
        arrived = _xchg_wait(in_flight[grp], after, False, SAME_CORE, f"gather_{grp}_wait")
        forwarding[grp], started = _sibling_forward_start(arrived, f"gather_{grp}_to_sibling_start")
        return started

    def get_w(grp, after):
        if grp == "l0a":
            return {"a_in": a_in_all}
        full = _sibling_forward_wait(forwarding[grp], after, f"gather_{grp}_to_sibling_wait")
        if grp == "l0b":
            started = start_gather("l1", full[0])
            full[0] = full[0] + started.astype(full[0].dtype)
        got = dict(zip(gather_names[grp], full))
        out = {}
        for n, a in got.items():
            if n in ("a_out", "b_out"):
                out[n] = a.reshape(d, d)
            elif n in ("down0", "down1"):
                dn = a.reshape(NCHIP, ff // NCHIP, d)
                out[n] = jnp.pad(dn, ((0, 0), (0, ncp - ncw), (0, 0))).reshape(fp, d)
            elif n == "kv":
                kv_t = a.reshape(NDEV * kv_w.shape[1], d)
                out["kv_k"], out["kv_v"] = kv_t[:d], kv_t[d:2 * d]
                out["kv_f"] = jnp.pad(kv_t[2 * d:], ((0, LANES - nh), (0, 0)))
            else:
                out[n] = a
        return out

    scatter_flight, g_last = {}, {}

    def put_g(grp, gr):
        if grp == "l0a":
            g_last.update(gr)
            return zero
        if grp == "l1":
            g_kvw = jnp.concatenate([gr["kv_k"], gr["kv_v"], gr["kv_f"][:nh].astype(BF16)], axis=0)
            arrs = {"kv_w": g_kvw.reshape(NDEV, kv_w.shape[1], d), "b_w_q": gr["b_q"],
                    "b_w_out": gr["b_out"].reshape(NDEV, d // NDEV, d), "up1": gr["up1"],
                    "down1": gr["down1"].reshape(NCHIP, ncp, d)[:, :ncw].reshape(NDEV, rd, d)}
        else:
            arrs = {"a_w_out": gr["a_out"].reshape(NDEV, d // NDEV, d), "up0": gr["up0"],
                    "down0": gr["down0"].reshape(NCHIP, ncp, d)[:, :ncw].reshape(NDEV, rd, d)}
        srcs = list(arrs.values())
        handles, sent = _xchg_start(srcs, True, ALL_PEERS, srcs[0], f"scatter_{grp}_start")
        scatter_flight[grp] = (list(arrs), handles)
        return sent

    loss_v, grad_x, dmods, dlb, g = _local_step(x[0], loss_target[0], mods, lb, small, pre_w, get_w, put_g)

    g_sum = {}
    for grp in ("l1", "l0b"):
        names, handles = scatter_flight[grp]
        for nm, a in zip(names, _xchg_wait(handles, grad_x, True, ALL_PEERS, f"scatter_{grp}_wait")):
            g_sum[nm] = _slab_sum(a, f"rs_slab_sum_{nm}")

    def conv_w_grad(a):
        return _unpad_shard_cols(a.transpose(1, 0, 2).reshape(CONV_TAPS, 2 * fp), ncw, ncp)

    def conv_b_grad(a):
        return _unpad_shard_cols(a.reshape(2 * fp), ncw, ncp)

    dmod_vec = [dmods[f"{nm}_{l}"] for l in range(2) for nm in mod_names] + [dmods["kv_sh"], dmods["kv_sc"]]
    post = _pack_small(dmod_vec + [dlb, g["a_norm_g"], g["k_norm_g"], g["q_norm_g"],
                                   jnp.pad(g["kv_b_f"].reshape(-1), (0, LANES - nh)),
                                   conv_w_grad(g["conv_w0"]), conv_w_grad(g["conv_w1"]),
                                   conv_b_grad(g["conv_b0"]), conv_b_grad(g["conv_b1"]), loss_v])
    (post_all,) = _all_gather([post], "gather_small_grads")
    a_in_flight, a_in_sent = _xchg_start([g_last["a_in"]], True, ALL_PEERS, post_all, "scatter_l0a_start")
    a_in_sent = a_in_sent.reshape(1, 1)
    tot = _slab_sum(post_all, "small_grad_sum").reshape(-1)
    nmod = 14 * d
    (t_mod, t_lb, t_ang, t_kng, t_qng, t_bf, t_cw, t_cb, t_loss) = _unpack_small(
        tot, [(nmod,), (1, d), (1, HEAD), (HEAD,), (1, HEAD), (LANES,), (2, CONV_TAPS, two_f), (2, two_f),
              (LANES,)])
    loss = t_loss[0]
    dm_all = post_all.reshape(NDEV, -1)[:, :nmod]
    dm0 = lax.dynamic_slice_in_dim(dm_all[:, :6 * d], me * n0, n0, axis=1)
    dm1 = lax.dynamic_slice_in_dim(dm_all[:, 6 * d:12 * d], me * n0, n0, axis=1)
    dkv = lax.dynamic_slice_in_dim(dm_all[:, 12 * d:], me * nkv, nkv, axis=1)
    g_ada_w, g_kv_ada_w, g_logits = _ada_bwd(c_act, _pad_rows(dm0, 2 * NDEV), _pad_rows(dm1, 2 * NDEV),
                                              _pad_rows(dkv, 2 * NDEV), lb, t_lb)

    grads = {
        "ada_w": g_ada_w,
        "ada_b": t_mod[:12 * d].reshape(2, 6 * d),
        "a_lb_logits": lax.dynamic_slice_in_dim(g_logits, me * HEAD, HEAD, axis=1),
        "a_norm_g": t_ang,
        "a_w_out": g_sum["a_w_out"].reshape(a_w_out.shape),
        "kv_ada_w": g_kv_ada_w,
        "kv_ada_b": t_mod[12 * d:],
        "kv_w": g_sum["kv_w"].T,
        "kv_b_f": t_bf[:nh],
        "k_norm_g": t_kng,
        "b_w_q": g_sum["b_w_q"].reshape(b_w_q.shape),
        "q_norm_g": t_qng,
        "b_w_out": g_sum["b_w_out"].reshape(b_w_out.shape),
        "ffn_w_up": jnp.stack([g_sum["up0"][:, :ncw], g_sum["up1"][:, :ncw]]),
        "ffn_conv_w": lax.dynamic_slice_in_dim(t_cw, me * ncw, ncw, axis=2),
        "ffn_conv_b": t_cb,
        "ffn_w_down": jnp.stack([g_sum["down0"], g_sum["down1"]]),
    }

    big_adam = ["ada_w", "a_w_out", "kv_ada_w", "kv_w", "b_w_q", "b_w_out", "ffn_w_up", "ffn_w_down", "a_w_in"]
    small_adam = [n for n in order if n not in big_adam]
    delta, new_m, new_v = {}, {}, {}
    packs = [_pack_small([src[n] for n in small_adam]) for src in (weights, grads, m_in, v_in)]
    outs = _adamw(*packs, "adamw_small", tr=packs[0].shape[0])
    shapes = [weights[n].shape for n in small_adam]
    for dst, o in zip((delta, new_m, new_v), outs):
        for n, a in zip(small_adam, _unpack_small(o.reshape(-1), shapes)):
            dst[n] = a
    for n in big_adam:
        if n == "a_w_in":
            (landed,) = _xchg_wait(a_in_flight, new_v["ffn_w_down"], True, ALL_PEERS, "scatter_l0a_wait")
            grads[n] = _slab_sum(landed, "rs_slab_sum_a_w_in").reshape(a_w_in.shape)
        shp = weights[n].shape
        two_d = lambda a: a.reshape(-1, shp[-1])
        dl, mn, vn = _adamw(two_d(weights[n]), two_d(grads[n]), two_d(m_in[n]), two_d(v_in[n]), f"adamw_{n}",
                            after=a_in_sent)
        delta[n], new_m[n], new_v[n] = dl.reshape(shp), mn.reshape(shp), vn.reshape(shp)

    return (loss, grad_x.reshape(x.shape), *[grads[n] for n in order], *[delta[n] for n in order],
            *[new_m[n] for n in order], *[new_v[n] for n in order])
```

```python
import functools

import jax
import jax.numpy as jnp
from jax import lax
from jax.experimental import pallas as pl
from jax.experimental.pallas import tpu as pltpu

F32 = jnp.float32
BF16 = jnp.bfloat16

NDEV = 8
NCHIP = 4
HEAD = 128
A_CHUNK = 64
CONV_TAPS = 3
EPS = 1e-6
NEG_INF = -1e30
LANES = 128
VMEM_LIMIT = 48 * 1024 * 1024

ADAM_LR = 0.001
ADAM_B1 = 0.9
ADAM_B2 = 0.999
ADAM_EPS = 1e-08
ADAM_WD = 0.01
ADAM_STEP = 10

_NN = (((1,), (0,)), ((), ()))
_NT = (((1,), (1,)), ((), ()))
_TN = (((0,), (0,)), ((), ()))
_MESH = pl.DeviceIdType.MESH


def _cparams(**kw):
    return pltpu.CompilerParams(vmem_limit_bytes=VMEM_LIMIT, **kw)


def _divisor_tile(n, pref, unit=LANES):
    if n <= pref:
        return n
    best = None
    for t in range(unit, pref + 1, unit):
        if n % t == 0:
            best = t
    assert best is not None, (n, pref)
    return best


def _round_up(n, unit):
    return -(-n // unit) * unit


def _bdot_raw(a, b, dims):
    return lax.dot_general(a.astype(BF16), b.astype(BF16), dims, preferred_element_type=F32)


@jax.custom_vjp
def _dot_nn(a, b):
    return _bdot_raw(a, b, _NN)


@jax.custom_vjp
def _dot_nt(a, b):
    return _bdot_raw(a, b, _NT)


@jax.custom_vjp
def _dot_tn(a, b):
    return _bdot_raw(a, b, _TN)


_dot_nn.defvjp(lambda a, b: (_bdot_raw(a, b, _NN), (a, b)),
               lambda r, g: (_dot_nt(g, r[1]), _dot_tn(r[0], g)))
_dot_nt.defvjp(lambda a, b: (_bdot_raw(a, b, _NT), (a, b)),
               lambda r, g: (_dot_nn(g, r[1]), _dot_tn(g, r[0])))
_dot_tn.defvjp(lambda a, b: (_bdot_raw(a, b, _TN), (a, b)),
               lambda r, g: (_dot_nt(r[1], g), _dot_nn(r[0], g)))


def _f32dot(a, b):
    return lax.dot_general(a, b, _NN, precision=lax.Precision.HIGHEST, preferred_element_type=F32)


def _sigmoid(x):
    return jax.nn.sigmoid(x)


def _silu(x):
    return x * jax.nn.sigmoid(x)


def _rms(x):
    return x * lax.rsqrt(jnp.mean(x * x, axis=-1, keepdims=True) + EPS)


def _modulate(x, sh, sc):
    return _rms(x) * (1.0 + sc) + sh


def _mm_call(a, b, dims, a_spec, b_spec, o_spec, o_shape, grid, acc_tile, name):
    nk = grid[2]

    def body(a_ref, b_ref, o_ref, *acc):
        p = lax.dot_general(a_ref[...].astype(BF16), b_ref[...].astype(BF16), dims,
                            preferred_element_type=F32)
        if nk == 1:
            o_ref[...] = p.astype(o_ref.dtype)
        else:
            kk = pl.program_id(2)

            @pl.when(kk == 0)
            def _():
                acc[0][...] = p

            @pl.when(kk > 0)
            def _():
                acc[0][...] += p

            @pl.when(kk == nk - 1)
            def _():
                o_ref[...] = acc[0][...].astype(o_ref.dtype)

    return pl.pallas_call(
        body, name=name, grid=grid, in_specs=[a_spec, b_spec], out_specs=o_spec, out_shape=o_shape,
        scratch_shapes=[pltpu.VMEM(acc_tile, F32)] if nk > 1 else [],
        compiler_params=_cparams(dimension_semantics=("parallel", "parallel", "arbitrary")),
    )(a, b)


def _mm(a, b, mode, out_dtype, name, tm=1024, tn=1024, tk=2048):
    if mode == "nn":
        (m, k), (k2, n) = a.shape, b.shape
    elif mode == "nt":
        (m, k), (n, k2) = a.shape, b.shape
    else:
        (k, m), (k2, n) = a.shape, b.shape
    assert k == k2, (a.shape, b.shape, mode)
    tm, tn, tk = _divisor_tile(m, tm), _divisor_tile(n, tn), _divisor_tile(k, tk)
    if mode == "tn":
        a_spec = pl.BlockSpec((tk, tm), lambda i, j, kk: (kk, i))
    else:
        a_spec = pl.BlockSpec((tm, tk), lambda i, j, kk: (i, kk))
    if mode == "nt":
        b_spec = pl.BlockSpec((tn, tk), lambda i, j, kk: (j, kk))
    else:
        b_spec = pl.BlockSpec((tk, tn), lambda i, j, kk: (kk, j))
    return _mm_call(a, b, {"nn": _NN, "nt": _NT, "tn": _TN}[mode], a_spec, b_spec,
                    pl.BlockSpec((tm, tn), lambda i, j, kk: (i, j)), jax.ShapeDtypeStruct((m, n), out_dtype),
                    (m // tm, n // tn, k // tk), (tm, tn), name)


def _wblk_act_spec(rows, gb, nl, split, nb, row_axis, blk_axis):
    if split == 1:
        return pl.BlockSpec((rows, gb * nl), lambda *g: (g[row_axis], g[blk_axis]))
    groups = nb // split // gb
    return pl.BlockSpec((None, rows, gb * nl),
                        lambda *g: (g[blk_axis] // groups, g[row_axis], g[blk_axis] % groups))


def _mm_wblk(a, wb, out_dtype, name, *, gb, row_off=0, split=1, tm=1024):
    m, k = a.shape
    nb, _, nl = wb.shape
    assert (nb // split) % gb == 0
    tm = _divisor_tile(m, tm)

    def body(a_ref, b_ref, o_ref):
        av = a_ref[...].astype(BF16)
        for s in range(gb):
            o_ref[:, s * nl:(s + 1) * nl] = lax.dot_general(
                av, b_ref[s].astype(BF16), _NN, preferred_element_type=F32).astype(o_ref.dtype)

    o_shape = (m, nb * nl) if split == 1 else (split, m, nb // split * nl)
    return pl.pallas_call(
        body, name=name, grid=(nb // gb, m // tm),
        in_specs=[pl.BlockSpec((tm, k), lambda j, i: (i, 0)),
                  pl.BlockSpec((gb, k, nl), lambda j, i: (j, row_off, 0))],
        out_specs=_wblk_act_spec(tm, gb, nl, split, nb, 1, 0),
        out_shape=jax.ShapeDtypeStruct(o_shape, out_dtype),
        compiler_params=_cparams(dimension_semantics=("parallel", "parallel")),
    )(a, wb)


def _mm_wblk_dx(dy, wb, out_dtype, name, *, k, gb, row_off=0, split=1, tm=1024):
    nb, _, nl = wb.shape
    m = dy.shape[-2]
    tm = _divisor_tile(m, tm)
    nk = nb // gb
    per = nb // split
    whole = split > 1 and gb == nb
    assert whole or per % gb == 0

    def body(a_ref, b_ref, o_ref, *acc):
        p = None
        for s in range(gb):
            a_blk = a_ref[s // per, :, (s % per) * nl:(s % per + 1) * nl] if whole else a_ref[:, s * nl:(s + 1) * nl]
            q = lax.dot_general(a_blk.astype(BF16), b_ref[s].astype(BF16), _NT, preferred_element_type=F32)
            p = q if p is None else p + q
        if nk == 1:
            o_ref[...] = p.astype(o_ref.dtype)
        else:
            kk = pl.program_id(1)

            @pl.when(kk == 0)
            def _():
                acc[0][...] = p

            @pl.when(kk > 0)
            def _():
                acc[0][...] += p

            @pl.when(kk == nk - 1)
            def _():
                o_ref[...] = acc[0][...].astype(o_ref.dtype)

    return pl.pallas_call(
        body, name=name, grid=(m // tm, nk),
        in_specs=[pl.BlockSpec((split, tm, per * nl), lambda i, kk: (0, i, 0)) if whole
                  else _wblk_act_spec(tm, gb, nl, split, nb, 0, 1),
                  pl.BlockSpec((gb, k, nl), lambda i, kk: (kk, row_off, 0))],
        out_specs=pl.BlockSpec((tm, k), lambda i, kk: (i, 0)),
        out_shape=jax.ShapeDtypeStruct((m, k), out_dtype),
        scratch_shapes=[pltpu.VMEM((tm, k), F32)] if nk > 1 else [],
        compiler_params=_cparams(dimension_semantics=("parallel", "arbitrary")),
    )(dy, wb)


def _mm_wblk_dw(x, dy, name, *, nb, gb, split=1, tk=1024):
    t, k = x.shape
    assert (nb // split) % gb == 0
    nl = dy.shape[-1] * split // nb
    tk = _divisor_tile(t, tk)
    nk = t // tk

    def body(a_ref, b_ref, o_ref, *acc):
        kk = pl.program_id(1)
        av = a_ref[...].astype(BF16)
        for s in range(gb):
            p = lax.dot_general(av, b_ref[:, s * nl:(s + 1) * nl].astype(BF16), _TN, preferred_element_type=F32)
            if nk == 1:
                o_ref[s] = p.astype(o_ref.dtype)
                continue

            @pl.when(kk == 0)
            def _():
                acc[0][s] = p

            @pl.when(kk > 0)
            def _():
                acc[0][s] += p

        if nk > 1:
            @pl.when(kk == nk - 1)
            def _():
                o_ref[...] = acc[0][...].astype(o_ref.dtype)

    return pl.pallas_call(
        body, name=name, grid=(nb // gb, nk),
        in_specs=[pl.BlockSpec((tk, k), lambda j, kk: (kk, 0)), _wblk_act_spec(tk, gb, nl, split, nb, 1, 0)],
        out_specs=pl.BlockSpec((gb, k, nl), lambda j, kk: (j, 0, 0)),
        out_shape=jax.ShapeDtypeStruct((nb, k, nl), BF16),
        scratch_shapes=[pltpu.VMEM((gb, k, nl), F32)] if nk > 1 else [],
        compiler_params=_cparams(dimension_semantics=("parallel", "arbitrary")),
    )(x, dy)


def _row_specs(rows, tb, nsub):
    return [pl.BlockSpec((tb, nsub * cw), functools.partial(lambda i, off: (i, off), off=off))
            for (_, cw, off) in rows]


def _vec_specs(params):
    return [pl.BlockSpec(p.shape, lambda i: (0, 0)) for p in params]


def _row_fwd(f, rows, params, out_dtypes, *, nsub=1, tb, name):
    t = rows[0][0].shape[0]
    tb = min(tb, t)
    n_r, n_p = len(rows), len(params)
    blk = [jax.ShapeDtypeStruct((tb, cw), F32) for (_, cw, _) in rows]
    blk += [jax.ShapeDtypeStruct(p.shape, F32) for p in params]
    out_avals = jax.eval_shape(f, *blk)

    def body(*refs):
        pv = [r[...] for r in refs[n_r:n_r + n_p]]
        for s in range(nsub):
            vals = [r[:, s * cw:(s + 1) * cw].astype(F32) for r, (_, cw, _) in zip(refs[:n_r], rows)]
            outs = f(*vals, *pv)
            for o_ref, o in zip(refs[n_r + n_p:], outs):
                w = o.shape[1]
                o_ref[:, s * w:(s + 1) * w] = o.astype(o_ref.dtype)

    return pl.pallas_call(
        body, name=name,
        grid=(t // tb,),
        in_specs=_row_specs(rows, tb, nsub) + _vec_specs(params),
        out_specs=[pl.BlockSpec((tb, nsub * av.shape[1]), lambda i: (i, 0)) for av in out_avals],
        out_shape=[jax.ShapeDtypeStruct((t, nsub * av.shape[1]), dt) for av, dt in zip(out_avals, out_dtypes)],
        compiler_params=_cparams(dimension_semantics=("parallel",)),
    )(*[r[0] for r in rows], *params)


def _row_bwd(f, rows, params, cots, row_grad_dtypes, *, nsub=1, tb, name, add_to=None, cot_add=None):
    t = rows[0][0].shape[0]
    tb = min(tb, t)
    n_r, n_p, n_c = len(rows), len(params), len(cots)
    want = [j for j in range(n_r) if row_grad_dtypes[j] is not None]
    cot_add = cot_add or []
    extra = [] if add_to is None else [(add_to[1], rows[add_to[0]][1], 0)]
    n_add_to = len(extra)
    extra += [(arr, cots[ci][1], 0) for ci, arr in cot_add]

    def body(*refs):
        i = pl.program_id(0)
        r_in, p_in = refs[:n_r], refs[n_r:n_r + n_p]
        c_in = refs[n_r + n_p:n_r + n_p + n_c]
        e_in = refs[n_r + n_p + n_c:n_r + n_p + n_c + len(extra)]
        outs = refs[n_r + n_p + n_c + len(extra):]
        pv = [r[...] for r in p_in]
        psum = [None] * n_p
        for s in range(nsub):
            vals = [r[:, s * cw:(s + 1) * cw].astype(F32) for r, (_, cw, _) in zip(r_in, rows)]
            cvals = [r[:, s * cw:(s + 1) * cw].astype(F32) for r, (_, cw, _) in zip(c_in, cots)]
            for (ci, _), e_ref in zip(cot_add, e_in[n_add_to:]):
                cw = cots[ci][1]
                cvals[ci] = cvals[ci] + e_ref[:, s * cw:(s + 1) * cw].astype(F32)
            _, vjp_fn = jax.vjp(f, *vals, *pv)
            grads = vjp_fn(tuple(cvals))
            for o_ref, jr in zip(outs[:len(want)], want):
                cw = rows[jr][1]
                gr = grads[jr]
                if add_to is not None and jr == add_to[0]:
                    gr = gr + e_in[0][:, s * cw:(s + 1) * cw]
                o_ref[:, s * cw:(s + 1) * cw] = gr.astype(o_ref.dtype)
            for jp in range(n_p):
                psum[jp] = grads[n_r + jp] if psum[jp] is None else psum[jp] + grads[n_r + jp]
        for o_ref, g in zip(outs[len(want):], psum):
            @pl.when(i == 0)
            def _():
                o_ref[...] = g

            @pl.when(i > 0)
            def _():
                o_ref[...] += g

    out_specs = [pl.BlockSpec((tb, nsub * rows[jr][1]), lambda i: (i, 0)) for jr in want]
    out_shape = [jax.ShapeDtypeStruct((t, nsub * rows[jr][1]), row_grad_dtypes[jr]) for jr in want]
    out_specs += _vec_specs(params)
    out_shape += [jax.ShapeDtypeStruct(p.shape, F32) for p in params]
    res = pl.pallas_call(
        body, name=name,
        grid=(t // tb,),
        in_specs=_row_specs(rows, tb, nsub) + _vec_specs(params) + _row_specs(cots, tb, nsub)
        + _row_specs(extra, tb, nsub),
        out_specs=out_specs, out_shape=out_shape,
        compiler_params=_cparams(dimension_semantics=("arbitrary",)),
    )(*[r[0] for r in rows], *params, *[c[0] for c in cots], *[e[0] for e in extra])
    return res[:len(want)], res[len(want):]


def _f_mod(x, sh, sc):
    return (_modulate(x, sh, sc),)


def _f_res_mod(x, y, g, sh, sc):
    x1 = x + g * y
    return x1, _modulate(x1, sh, sc)


def _f_res_mod2(x, y, g, sh_a, sc_a, sh_b, sc_b):
    x1 = x + g * y
    return x1, _modulate(x1, sh_a, sc_a), _modulate(x1, sh_b, sc_b)


def _f_qnorm(p, g):
    return (_rms(p) * g * (HEAD ** -0.5),)


def _f_knorm(p, g):
    return (_rms(p) * g,)


def _f_qnorm_aug(p, g):
    lane = lax.broadcasted_iota(jnp.int32, p.shape, 1)
    return (jnp.concatenate([_rms(p) * g * (HEAD ** -0.5), jnp.where(lane < 3, 1.0, 0.0)], axis=1),)


def _f_knorm_aug(p, c0, c1, c2, g):
    lane = lax.broadcasted_iota(jnp.int32, p.shape, 1)
    aug = jnp.where(lane == 0, c0, jnp.where(lane == 1, c1, jnp.where(lane == 2, c2, 0.0)))
    return (jnp.concatenate([_rms(p) * g, aug], axis=1),)


def _split3(a):
    round_bf16 = lambda v: lax.reduce_precision(v, exponent_bits=8, mantissa_bits=7)
    hi = round_bf16(a)
    mid = round_bf16(a - hi)
    lo = round_bf16(a - hi - mid)
    return hi.astype(BF16), mid.astype(BF16), lo.astype(BF16)


def _f_outgate(o, og):
    return (o * _sigmoid(og),)


def _loss_call(x3, f, g2, target, tb):
    t, d = x3.shape
    tb = min(tb, t)

    def body(x_ref, f_ref, g_ref, t_ref, loss_ref, dx_ref, df_ref, dg_ref):
        i = pl.program_id(0)
        fv = f_ref[...]
        g = g_ref[...]
        e = x_ref[...] + g * fv - t_ref[...]
        dx = e * (1.0 / d)
        part = 0.5 * jnp.sum(jnp.sum(e * dx, axis=1, keepdims=True), axis=0, keepdims=True)
        dx_ref[...] = dx
        df_ref[...] = (g * dx).astype(df_ref.dtype)
        dg = jnp.sum(dx * fv, axis=0, keepdims=True)

        @pl.when(i == 0)
        def _():
            loss_ref[...] = jnp.broadcast_to(part, loss_ref.shape)
            dg_ref[...] = dg

        @pl.when(i > 0)
        def _():
            loss_ref[...] += jnp.broadcast_to(part, loss_ref.shape)
            dg_ref[...] += dg

    row = pl.BlockSpec((tb, d), lambda i: (i, 0))
    vec = pl.BlockSpec((1, d), lambda i: (0, 0))
    return pl.pallas_call(
        body, name="loss_head",
        grid=(t // tb,),
        in_specs=[row, row, vec, row],
        out_specs=[pl.BlockSpec((1, LANES), lambda i: (0, 0)), row, row, vec],
        out_shape=[jax.ShapeDtypeStruct((1, LANES), F32), jax.ShapeDtypeStruct((t, d), F32),
                   jax.ShapeDtypeStruct((t, d), BF16), jax.ShapeDtypeStruct((1, d), F32)],
        compiler_params=_cparams(dimension_semantics=("arbitrary",)),
    )(x3, f, g2, target)


def _hg_mask(tb):
    br = lax.broadcasted_iota(jnp.int32, (tb, tb), 0)
    bs = lax.broadcasted_iota(jnp.int32, (tb, tb), 1)
    return jnp.logical_and(br // A_CHUNK == bs // A_CHUNK, bs <= br).astype(F32)


def _hg_consts(mask):
    c = A_CHUNK
    r = lax.broadcasted_iota(jnp.int32, (c, c), 0)
    s = lax.broadcasted_iota(jnp.int32, (c, c), 1)
    return (s <= r).astype(F32), (r <= s).astype(F32), mask > 0.5


def _chunk_apply(mat, x):
    c = mat.shape[0]
    return jnp.concatenate([_f32dot(mat, x[i * c:(i + 1) * c]) for i in range(x.shape[0] // c)], axis=0)


@jax.custom_vjp
def _chunk_cumsum(x, tri, tri_t):
    return _chunk_apply(tri, x)


_chunk_cumsum.defvjp(lambda x, tri, tri_t: (_chunk_apply(tri, x), (tri, tri_t)),
                     lambda r, g: (_chunk_apply(r[1], g), jnp.zeros_like(r[0]), jnp.zeros_like(r[1])))


def _per_chunk(a, b, dims):
    return jnp.stack([_bdot_raw(a[i], b[i], dims) for i in range(a.shape[0])])


@jax.custom_vjp
def _chunk_tn(a, b):
    return _per_chunk(a, b, _TN)


@jax.custom_vjp
def _chunk_nt(a, b):
    return _per_chunk(a, b, _NT)


@jax.custom_vjp
def _chunk_nn(a, b):
    return _per_chunk(a, b, _NN)


_chunk_tn.defvjp(lambda a, b: (_per_chunk(a, b, _TN), (a, b)),
                 lambda r, g: (_chunk_nt(r[1], g), _chunk_nn(r[0], g)))
_chunk_nt.defvjp(lambda a, b: (_per_chunk(a, b, _NT), (a, b)),
                 lambda r, g: (_chunk_nn(g, r[1]), _chunk_tn(g, r[0])))
_chunk_nn.defvjp(lambda a, b: (_per_chunk(a, b, _NN), (a, b)),
                 lambda r, g: (_chunk_nt(g, r[1]), _chunk_tn(r[0], g)))


def _scan_states(decay, m, st):
    sts = []
    for i in range(m.shape[0]):
        sts.append(st)
        st = st * decay[i] + m[i]
    return jnp.stack(sts), st


@jax.custom_vjp
def _state_scan(decay, m, st):
    return _scan_states(decay, m, st)


def _state_scan_fwd(decay, m, st):
    sts, st_out = _scan_states(decay, m, st)
    return (sts, st_out), (decay, sts)


def _state_scan_bwd(res, cts):
    decay, sts = res
    d_sts, g = cts
    d_decay, d_m = [], []
    for i in range(sts.shape[0] - 1, -1, -1):
        d_m.append(g)
        d_decay.append(jnp.sum(g * sts[i], axis=0, keepdims=True))
        g = g * decay[i] + d_sts[i]
    return jnp.stack(d_decay[::-1]), jnp.stack(d_m[::-1]), g


_state_scan.defvjp(_state_scan_fwd, _state_scan_bwd)


def _hg_block(qp, fp, ip, gp, lb, ng, st, tri, tri_t, bd_causal):
    tb = qp.shape[0]
    c = A_CHUNK
    n = tb // c
    q = _silu(qp)
    fg = lb + (1.0 - lb) * _sigmoid(fp)
    logf = jnp.log(fg)
    k = 1.0 - fg
    b3 = _chunk_cumsum(logf, tri, tri_t).reshape(n, c, HEAD)
    pos = lax.broadcasted_iota(jnp.int32, (1, c, 1), 1)
    b_mid = lax.stop_gradient(jnp.sum(jnp.where(pos == c // 2, b3, 0.0), axis=1, keepdims=True))
    b_last = jnp.sum(jnp.where(pos == c - 1, b3, 0.0), axis=1, keepdims=True)
    q3, k3, v3 = q.reshape(n, c, HEAD), k.reshape(n, c, HEAD), ip.reshape(n, c, HEAD)
    scores = _dot_nt((q3 * jnp.exp(b3 - b_mid)).reshape(tb, HEAD), (k3 * jnp.exp(b_mid - b3)).reshape(tb, HEAD))
    o_intra = _dot_nn(jnp.where(bd_causal, scores, 0.0), ip)
    states, st_new = _state_scan(jnp.exp(b_last), _chunk_tn(v3, k3 * jnp.exp(b_last - b3)), st)
    o = o_intra + _chunk_nt(q3 * jnp.exp(b3), states).reshape(tb, HEAD)
    y = _rms(o) * ng * _silu(gp)
    return y, st_new


HG_HEADS = 2


def _hg_specs(tb, nh, rev_nb=None):
    wide = HG_HEADS * HEAD
    per = nh // HG_HEADS

    def row(part):
        if rev_nb is None:
            return pl.BlockSpec((tb, wide), functools.partial(lambda h, i, off: (i, off + h), off=part * per))
        return pl.BlockSpec((tb, wide),
                            functools.partial(lambda h, i, off: (rev_nb - 1 - i, off + h), off=part * per))
    return [row(0), row(1), row(2), row(3),
            pl.BlockSpec((1, wide), lambda h, i: (0, h)), pl.BlockSpec((1, HEAD), lambda h, i: (0, 0)),
            pl.BlockSpec((tb, tb), lambda h, i: (0, 0))]


def _hgrn2_fwd(proj, lb, ng, tb):
    t = proj.shape[0]
    nh = proj.shape[1] // (4 * HEAD)
    tb = min(tb, t)
    nb = t // tb
    wide = HG_HEADS * HEAD

    def body(q_ref, f_ref, i_ref, g_ref, lb_ref, ng_ref, mask_ref, y_ref, s_ref, st_ref):
        i = pl.program_id(1)

        @pl.when(i == 0)
        def _():
            st_ref[...] = jnp.zeros_like(st_ref)

        consts = _hg_consts(mask_ref[...])
        for p in range(HG_HEADS):
            cs = slice(p * HEAD, (p + 1) * HEAD)
            st = st_ref[p]
            s_ref[p, 0] = st
            y, st_new = _hg_block(q_ref[:, cs], f_ref[:, cs], i_ref[:, cs], g_ref[:, cs], lb_ref[:, cs],
                                  ng_ref[...], st, *consts)
            y_ref[:, cs] = y.astype(y_ref.dtype)
            st_ref[p] = st_new

    return pl.pallas_call(
        body, name="hgrn2_fwd",
        grid=(nh // HG_HEADS, nb),
        in_specs=_hg_specs(tb, nh),
        out_specs=[pl.BlockSpec((tb, wide), lambda h, i: (i, h)),
                   pl.BlockSpec((HG_HEADS, 1, HEAD, HEAD), lambda h, i: (h, i, 0, 0))],
        out_shape=[jax.ShapeDtypeStruct((t, nh * HEAD), BF16),
                   jax.ShapeDtypeStruct((nh, nb, HEAD, HEAD), F32)],
        scratch_shapes=[pltpu.VMEM((HG_HEADS, HEAD, HEAD), F32)],
        compiler_params=_cparams(dimension_semantics=("parallel", "arbitrary")),
    )(proj, proj, proj, proj, lb, ng, _hg_mask(tb))


def _hgrn2_bwd(proj, lb, ng, states, dy, tb):
    t = proj.shape[0]
    nh = proj.shape[1] // (4 * HEAD)
    tb = min(tb, t)
    nb = t // tb
    wide = HG_HEADS * HEAD

    def body(q_ref, f_ref, i_ref, g_ref, lb_ref, ng_ref, mask_ref, s_ref, dy_ref,
             dp_ref, dlb_ref, dng_ref, dst_ref):
        h, i = pl.program_id(0), pl.program_id(1)
        consts = _hg_consts(mask_ref[...])

        @pl.when(i == 0)
        def _():
            dst_ref[...] = jnp.zeros_like(dst_ref)
            dlb_ref[...] = jnp.zeros_like(dlb_ref)

        @pl.when(jnp.logical_and(i == 0, h == 0))
        def _():
            dng_ref[...] = jnp.zeros_like(dng_ref)

        def fn(qp, fp, ip, gp, lbx, ngx, stx):
            return _hg_block(qp, fp, ip, gp, lbx, ngx, stx, *consts)

        for p in range(HG_HEADS):
            cs = slice(p * HEAD, (p + 1) * HEAD)
            _, vjp_fn = jax.vjp(fn, q_ref[:, cs], f_ref[:, cs], i_ref[:, cs], g_ref[:, cs], lb_ref[:, cs],
                                ng_ref[...], s_ref[p, 0])
            *gparts, glb, gng, dst = vjp_fn((dy_ref[:, cs].astype(F32), dst_ref[p]))
            for part, gpart in enumerate(gparts):
                dp_ref[part, :, cs] = gpart.astype(dp_ref.dtype)
            dst_ref[p] = dst
            dlb_ref[:, cs] += glb
            dng_ref[...] += gng

    rev = lambda h, i: (nb - 1 - i, h)
    return pl.pallas_call(
        body, name="hgrn2_bwd",
        grid=(nh // HG_HEADS, nb),
        in_specs=_hg_specs(tb, nh, rev_nb=nb) + [
            pl.BlockSpec((HG_HEADS, 1, HEAD, HEAD), lambda h, i: (h, nb - 1 - i, 0, 0)),
            pl.BlockSpec((tb, wide), rev)],
        out_specs=[pl.BlockSpec((4, tb, wide), lambda h, i: (0, nb - 1 - i, h)),
                   pl.BlockSpec((1, wide), lambda h, i: (0, h)), pl.BlockSpec((1, HEAD), lambda h, i: (0, 0))],
        out_shape=[jax.ShapeDtypeStruct((4, t, nh * HEAD), BF16),
                   jax.ShapeDtypeStruct((1, nh * HEAD), F32), jax.ShapeDtypeStruct((1, HEAD), F32)],
        scratch_shapes=[pltpu.VMEM((HG_HEADS, HEAD, HEAD), F32)],
        compiler_params=_cparams(dimension_semantics=("arbitrary", "arbitrary")),
    )(proj, proj, proj, proj, lb, ng, _hg_mask(tb), states, dy)


def _fgate_consts(cb):
    r = lax.broadcasted_iota(jnp.int32, (cb, cb), 0)
    s = lax.broadcasted_iota(jnp.int32, (cb, cb), 1)
    return (r <= s).astype(F32), (r >= s).astype(F32)


def _fgate_fwd(xt, bias, cb=512):
    nh, t = xt.shape
    cb = min(cb, t)

    def body(x_ref, b_ref, o_ref):
        upper, _ = _fgate_consts(cb)
        carry = jnp.zeros((nh, 1), F32)
        for blk in range(t // cb):
            z = x_ref[:, blk * cb:(blk + 1) * cb] + b_ref[...]
            logf = jnp.minimum(z, 0.0) - jnp.log(1.0 + jnp.exp(-jnp.abs(z)))
            cs = _f32dot(logf, upper) + carry
            o_ref[:, blk * cb:(blk + 1) * cb] = cs
            carry = cs[:, cb - 1:cb]

    vm = pl.BlockSpec(memory_space=pltpu.VMEM)
    return pl.pallas_call(
        body, name="fgate_fwd", in_specs=[vm, vm], out_specs=vm,
        out_shape=jax.ShapeDtypeStruct((nh, t), F32), compiler_params=_cparams(),
    )(xt, bias)


def _fgate_bwd(xt, bias, dft, cb=512):
    nh, t = xt.shape
    cb = min(cb, t)
    nblk = t // cb

    def body(x_ref, b_ref, d_ref, dx_ref, db_ref):
        _, lower = _fgate_consts(cb)
        carry = jnp.zeros((nh, 1), F32)
        db = jnp.zeros((nh, 1), F32)
        for blk in range(nblk - 1, -1, -1):
            sl = slice(blk * cb, (blk + 1) * cb)
            dlogf = _f32dot(d_ref[:, sl], lower) + carry
            carry = dlogf[:, 0:1]
            z = x_ref[:, sl] + b_ref[...]
            dz = dlogf * (1.0 - _sigmoid(z))
            dx_ref[:, sl] = dz
            db = db + jnp.sum(dz, axis=1, keepdims=True)
        db_ref[...] = db

    vm = pl.BlockSpec(memory_space=pltpu.VMEM)
    return pl.pallas_call(
        body, name="fgate_bwd", in_specs=[vm, vm, vm], out_specs=[vm, vm],
        out_shape=[jax.ShapeDtypeStruct((nh, t), F32), jax.ShapeDtypeStruct((nh, 1), F32)],
        compiler_params=_cparams(),
    )(xt, bias, dft)


ATTN_GROUPS = 4
ATTN_FWD_HEADS = 2


def _attn_fwd(q, k, v, f_grp, blk):
    t, width = v.shape
    nh = width // HEAD
    nq = t // blk
    hpg = nh // ATTN_GROUPS

    def body(q_ref, k_ref, v_ref, fc_ref, o_ref, lse_ref):
        i = pl.program_id(0)
        tri = (lax.broadcasted_iota(jnp.int32, (blk, blk), 1) <= lax.broadcasted_iota(jnp.int32, (blk, blk), 0))
        for h0 in range(0, nh, ATTN_FWD_HEADS):
            heads = range(h0, min(h0 + ATTN_FWD_HEADS, nh))

            def tile(j, carries, masked):
                rs = pl.ds(pl.multiple_of(j * blk, blk), blk)
                out = []
                for h, (m, l, acc) in zip(heads, carries):
                    cs = slice(h * HEAD, (h + 1) * HEAD)
                    cs2 = slice(2 * h * HEAD, 2 * (h + 1) * HEAD)
                    s = _bdot_raw(q_ref[:, cs2], k_ref[rs, cs2], _NT)
                    if masked:
                        s = jnp.where(tri, s, NEG_INF)
                    m_new = jnp.maximum(m, jnp.max(s, axis=1, keepdims=True))
                    p = jnp.exp(s - m_new)
                    alpha = jnp.exp(m - m_new)
                    l_new = alpha * l + jnp.sum(p, axis=1, keepdims=True)
                    out.append((m_new, l_new, alpha * acc + _bdot_raw(p, v_ref[rs, cs], _NN)))
                return tuple(out)

            init = tuple((jnp.full((blk, 1), NEG_INF, F32), jnp.zeros((blk, 1), F32), jnp.zeros((blk, HEAD), F32))
                         for _ in heads)
            carries = lax.fori_loop(0, i, lambda j, c: tile(j, c, False), init)
            for h, (m, l, acc) in zip(heads, tile(i, carries, True)):
                o_ref[:, h * HEAD:(h + 1) * HEAD] = acc / l
                g, hh = divmod(h, hpg)
                lse_ref[g, :, hh:hh + 1] = m + jnp.log(l) + fc_ref[g, :, hh:hh + 1]

    vm = pl.BlockSpec(memory_space=pltpu.VMEM)
    stat = pl.BlockSpec((ATTN_GROUPS, blk, hpg), lambda i: (0, i, 0))
    return pl.pallas_call(
        body, name="fox_attn_fwd",
        grid=(nq,),
        in_specs=[pl.BlockSpec((blk, 2 * width), lambda i: (i, 0)), vm, vm, stat],
        out_specs=[pl.BlockSpec((blk, width), lambda i: (i, 0)), stat],
        out_shape=[jax.ShapeDtypeStruct((t, width), F32), jax.ShapeDtypeStruct((ATTN_GROUPS, t, hpg), F32)],
        compiler_params=_cparams(dimension_semantics=("parallel",)),
    )(q, k, v, f_grp)


def _attn_delta(do, o, tb):
    t, width = o.shape
    nh = width // HEAD
    hpg = nh // ATTN_GROUPS
    tb = min(tb, t)

    def body(do_ref, o_ref, dl_ref):
        for h in range(nh):
            cs = slice(h * HEAD, (h + 1) * HEAD)
            g, hh = divmod(h, hpg)
            dl_ref[g, :, hh:hh + 1] = jnp.sum(do_ref[:, cs].astype(F32) * o_ref[:, cs], axis=1, keepdims=True)

    wide = pl.BlockSpec((tb, width), lambda i: (i, 0))
    return pl.pallas_call(body, name="fox_attn_delta", grid=(t // tb,), in_specs=[wide, wide],
                          out_specs=pl.BlockSpec((ATTN_GROUPS, tb, hpg), lambda i: (0, i, 0)),
                          out_shape=jax.ShapeDtypeStruct((ATTN_GROUPS, t, hpg), F32),
                          compiler_params=_cparams(dimension_semantics=("parallel",)))(do, o)


def _attn_bwd(q, k, v, f_grp, do, lse, delta, blk):
    t, width = v.shape
    nh = width // HEAD
    nq = t // blk
    hpg = nh // ATTN_GROUPS
    gw = hpg * HEAD

    def body(q_ref, do_ref, k_ref, v_ref, fc_ref, lse_ref, dl_ref,
             dq_ref, dk_ref, dv_ref, dfc_ref, dfr_ref):
        g, j = pl.program_id(0), pl.program_id(1)
        tri = (lax.broadcasted_iota(jnp.int32, (blk, blk), 1) <= lax.broadcasted_iota(jnp.int32, (blk, blk), 0))

        @pl.when(j == 0)
        def _():
            dq_ref[...] = jnp.zeros_like(dq_ref)
            dfc_ref[...] = jnp.zeros_like(dfc_ref)

        for h in range(hpg):
            cs = slice(h * HEAD, (h + 1) * HEAD)
            cs2 = slice(2 * h * HEAD, 2 * (h + 1) * HEAD)
            csq = slice(2 * h * HEAD, (2 * h + 1) * HEAD)
            kj2 = k_ref[:, cs2]
            kj = k_ref[:, csq]
            vj = v_ref[:, cs]

            def tile(i, carry, masked):
                dk, dv, dfs = carry
                rs = pl.ds(pl.multiple_of(i * blk, blk), blk)
                qi = q_ref[rs, csq]
                doi = do_ref[rs, cs]
                bias = fc_ref[0, rs, h:h + 1] - lse_ref[0, rs, h:h + 1]
                p = jnp.exp(_bdot_raw(q_ref[rs, cs2], kj2, _NT) + bias)
                if masked:
                    p = jnp.where(tri, p, 0.0)
                ds = p * (_bdot_raw(doi, vj, _NT) - dl_ref[0, rs, h:h + 1])
                dsb = ds.astype(BF16)
                dq_ref[rs, cs] += _bdot_raw(dsb, kj, _NN)
                dfc_ref[0, rs, h:h + 1] += jnp.sum(ds, axis=1, keepdims=True)
                return (dk + _bdot_raw(dsb, qi, _TN), dv + _bdot_raw(p, doi, _TN),
                        dfs - jnp.sum(ds, axis=0, keepdims=True))

            init = (jnp.zeros((blk, HEAD), F32), jnp.zeros((blk, HEAD), F32), jnp.zeros((1, blk), F32))
            carry = tile(j, init, True)
            dk, dv, dfs = lax.fori_loop(j + 1, nq, lambda i, c: tile(i, c, False), carry)
            dk_ref[:, cs] = dk
            dv_ref[:, cs] = dv.astype(dv_ref.dtype)
            dfr_ref[0, 0, h:h + 1, :] = dfs

    once = pl.Buffered(1)
    stat = pl.BlockSpec((1, t, hpg), lambda g, j: (g, 0, 0), pipeline_mode=once)
    kv_blk = pl.BlockSpec((blk, gw), lambda g, j: (j, g))
    frow = pl.BlockSpec((1, 1, hpg, blk), lambda g, j: (g, j, 0, 0))
    dq, dk, dv, dfc, dfr = pl.pallas_call(
        body, name="fox_attn_bwd",
        grid=(ATTN_GROUPS, nq),
        in_specs=[pl.BlockSpec((t, 2 * gw), lambda g, j: (0, g), pipeline_mode=once),
                  pl.BlockSpec((t, gw), lambda g, j: (0, g), pipeline_mode=once),
                  pl.BlockSpec((blk, 2 * gw), lambda g, j: (j, g)), kv_blk, stat, stat, stat],
        out_specs=[pl.BlockSpec((t, gw), lambda g, j: (0, g)), kv_blk, kv_blk,
                   pl.BlockSpec((1, t, hpg), lambda g, j: (g, 0, 0)), frow],
        out_shape=[jax.ShapeDtypeStruct((t, width), F32), jax.ShapeDtypeStruct((t, width), F32),
                   jax.ShapeDtypeStruct((t, width), BF16), jax.ShapeDtypeStruct((ATTN_GROUPS, t, hpg), F32),
                   jax.ShapeDtypeStruct((ATTN_GROUPS, nq, hpg, blk), F32)],
        compiler_params=_cparams(dimension_semantics=("parallel", "arbitrary")),
    )(q, do, k, v, f_grp, lse, delta)
    return dq, dk, dv, dfc, dfr


SUBLANES = 8


def _shift_down(u, n):
    r = pltpu.roll(u, n, 0)
    row = lax.broadcasted_iota(jnp.int32, (SUBLANES, u.shape[1]), 0)
    return jnp.concatenate([jnp.where(row < n, 0.0, r[:SUBLANES]), r[SUBLANES:]], axis=0)


def _shift_up(u, n):
    t = u.shape[0]
    r = pltpu.roll(u, t - n, 0)
    row = lax.broadcasted_iota(jnp.int32, (SUBLANES, u.shape[1]), 0)
    return jnp.concatenate([r[:t - SUBLANES], jnp.where(row >= SUBLANES - n, 0.0, r[t - SUBLANES:])], axis=0)


def _convglu_specs(t):
    return [pl.BlockSpec((2, t, LANES), lambda j: (0, 0, j)),
            pl.BlockSpec((2, CONV_TAPS, LANES), lambda j: (0, 0, j)),
            pl.BlockSpec((2, 1, LANES), lambda j: (0, 0, j))]


def _convglu_fwd(u, cw, cb):
    _, t, fp = u.shape

    def body(u_ref, w_ref, b_ref, a_ref, c_ref):
        c = []
        for hf in range(2):
            uv, w = u_ref[hf].astype(F32), w_ref[hf]
            c.append(w[0:1] * _shift_down(uv, 2) + w[1:2] * _shift_down(uv, 1) + w[2:3] * uv + b_ref[hf])
            c_ref[hf] = c[hf].astype(c_ref.dtype)
        a_ref[...] = (_silu(c[0]) * c[1]).astype(a_ref.dtype)

    return pl.pallas_call(
        body, name="convglu_fwd",
        grid=(fp // LANES,),
        in_specs=_convglu_specs(t),
        out_specs=[pl.BlockSpec((t, LANES), lambda j: (0, j)), pl.BlockSpec((2, t, LANES), lambda j: (0, 0, j))],
        out_shape=[jax.ShapeDtypeStruct((t, fp), BF16), jax.ShapeDtypeStruct((2, t, fp), BF16)],
        compiler_params=_cparams(dimension_semantics=("parallel",)),
    )(u, cw, cb)


def _convglu_bwd(u, c, cw, da):
    _, t, fp = u.shape

    def body(u_ref, c_ref, w_ref, da_ref, du_ref, dw_ref, db_ref):
        gc, vc = c_ref[0].astype(F32), c_ref[1].astype(F32)
        sg = _sigmoid(gc)
        dav = da_ref[...].astype(F32)
        dcs = [dav * vc * (sg * (1.0 + gc * (1.0 - sg))), dav * (gc * sg)]
        for hf in range(2):
            dc, w, uv = dcs[hf], w_ref[hf], u_ref[hf].astype(F32)
            dc1, dc2 = _shift_up(dc, 1), _shift_up(dc, 2)
            du_ref[hf] = (w[2:3] * dc + w[1:2] * dc1 + w[0:1] * dc2).astype(du_ref.dtype)
            dw_ref[hf, 0:1, :] = jnp.sum(dc2 * uv, axis=0, keepdims=True)
            dw_ref[hf, 1:2, :] = jnp.sum(dc1 * uv, axis=0, keepdims=True)
            dw_ref[hf, 2:3, :] = jnp.sum(dc * uv, axis=0, keepdims=True)
            db_ref[hf] = jnp.sum(dc, axis=0, keepdims=True)

    pair, taps, bias = _convglu_specs(t)
    return pl.pallas_call(
        body, name="convglu_bwd",
        grid=(fp // LANES,),
        in_specs=[pair, pair, taps, pl.BlockSpec((t, LANES), lambda j: (0, j))],
        out_specs=[pair, taps, bias],
        out_shape=[jax.ShapeDtypeStruct((2, t, fp), BF16), jax.ShapeDtypeStruct((2, CONV_TAPS, fp), F32),
                   jax.ShapeDtypeStruct((2, 1, fp), F32)],
        compiler_params=_cparams(dimension_semantics=("parallel",)),
    )(u, c, cw, da)


def _local_step(x, target, mods, lb, small, pre_w, get_w, put_g, *, tb=512, attn_blk=512):
    t, d = x.shape
    nh = d // HEAD
    nb = NDEV
    wts = {}
    vec = lambda *names: [mods[n] for n in names]

    def ffn_fwd(h2, l):
        u = _mm_wblk(h2, wts[f"up{l}"], BF16, f"ffn{l}_up", gb=nb // 2, split=2, tm=512)
        a, c = _convglu_fwd(u, small[f"conv_w{l}"], small[f"conv_b{l}"])
        f = _mm(a, wts[f"down{l}"], "nn", F32, f"ffn{l}_down", tk=4096)
        return (u, c), a, f

    def ffn_bwd(df, h2, uc, a, l):
        u, c = uc
        da = _mm(df, wts[f"down{l}"], "nt", BF16, f"ffn{l}_down_dx", tn=1536)
        dwd = _mm(a, df, "tn", BF16, f"ffn{l}_down_dw", tm=768, tk=t)
        du, dcw, dcb = _convglu_bwd(u, c, small[f"conv_w{l}"], da)
        dh2 = _mm_wblk_dx(du, wts[f"up{l}"], BF16, f"ffn{l}_up_dx", k=d, gb=nb // 2, split=2, tm=1024)
        dwu = _mm_wblk_dw(h2, du, f"ffn{l}_up_dw", nb=nb, gb=1, split=2, tk=t)
        return dh2, dwu, dwd, dcw, dcb

    (h_a,) = _row_fwd(_f_mod, [(x, d, 0)], vec("sh1_0", "sc1_0"), [BF16], tb=tb, name="l0_mod1")
    wts.update(get_w("l0a", h_a))
    proj_a = _mm_wblk(h_a, wts["a_in"], F32, "a_in", gb=nb // 2)
    ypre, states = _hgrn2_fwd(proj_a, lb, small["a_norm_g"], tb)
    pre_w("l0b", ypre)
    wts.update(get_w("l0b", ypre))
    y_a = _mm(ypre, wts["a_out"], "nn", F32, "a_out")
    x1, h2_0 = _row_fwd(_f_res_mod, [(x, d, 0), (y_a, d, 0)], vec("g1_0", "sh2_0", "sc2_0"), [F32, BF16],
                        tb=tb, name="l0_res_mod2")
    u0, a0, f0 = ffn_fwd(h2_0, 0)
    x2, h_kv, h_q = _row_fwd(_f_res_mod2, [(x1, d, 0), (f0, d, 0)],
                             [mods["g2_0"] + pre_w("l1", f0)] + vec("kv_sh", "kv_sc", "sh1_1", "sc1_1"),
                             [F32, BF16, BF16], tb=tb, name="l0_res_kvmod_qmod")
    wts.update(get_w("l1", h_kv))
    proj_k = _mm(h_kv, wts["kv_k"], "nt", F32, "k_proj")
    v_b = _mm(h_kv, wts["kv_v"], "nt", BF16, "v_proj")
    proj_f = _mm(h_kv, wts["kv_f"], "nt", F32, "kv_fproj")
    f_logit_t = proj_f[:, :nh].T
    f_bias = small["kv_b_f"].reshape(nh, 1)
    f_t = _fgate_fwd(f_logit_t, f_bias)
    f_grp = f_t.reshape(ATTN_GROUPS, nh // ATTN_GROUPS, t).transpose(0, 2, 1)
    (k_n,) = _row_fwd(_f_knorm_aug, [(proj_k, HEAD, 0)] + [(piece, 1, 0) for piece in _split3(-f_t.T)],
                      [small["k_norm_g"]], [BF16], nsub=nh, tb=tb, name="k_norm")
    proj_q = _mm_wblk(h_q, wts["b_q"], F32, "b_q", gb=nb)
    (q_n,) = _row_fwd(_f_qnorm_aug, [(proj_q, HEAD, 0)], [small["q_norm_g"]], [BF16], nsub=nh, tb=tb,
                      name="q_norm")
    o_att, lse = _attn_fwd(q_n, k_n, v_b, f_grp, attn_blk)
    (z,) = _row_fwd(_f_outgate, [(o_att, HEAD, 0), (proj_q, HEAD, 1)], [], [BF16], nsub=nh, tb=tb, name="out_gate")
    y_b = _mm(z, wts["b_out"], "nn", F32, "b_out")
    x3, h2_1 = _row_fwd(_f_res_mod, [(x2, d, 0), (y_b, d, 0)], vec("g1_1", "sh2_1", "sc2_1"), [F32, BF16],
                        tb=tb, name="l1_res_mod2")
    u1, a1, f1 = ffn_fwd(h2_1, 1)
    loss, dx4, df1, dg2_1 = _loss_call(x3, f1, mods["g2_1"], target, tb)

    g = {}
    dmods = {"g2_1": dg2_1}
    dh2, g["up1"], g["down1"], g["conv_w1"], g["conv_b1"] = ffn_bwd(df1, h2_1, u1, a1, 1)
    (dx2, dy_b), (dmods["g1_1"], dmods["sh2_1"], dmods["sc2_1"]) = _row_bwd(
        _f_res_mod, [(x2, d, 0), (y_b, d, 0)], vec("g1_1", "sh2_1", "sc2_1"),
        [(dx4, d, 0), (dh2, d, 0)], [F32, BF16], tb=tb, name="l1_res_mod2_bwd")
    dz = _mm(dy_b, wts["b_out"], "nt", BF16, "b_out_dx")
    g["b_out"] = _mm(z, dy_b, "tn", BF16, "b_out_dw", tk=t)
    (do_att, dog), _ = _row_bwd(_f_outgate, [(o_att, HEAD, 0), (proj_q, HEAD, 1)], [], [(dz, HEAD, 0)],
                                [BF16, BF16], nsub=nh, tb=tb, name="out_gate_bwd")
    delta = _attn_delta(do_att, o_att, tb)
    dq_n, dk_n, dv, dfc_q, dfr_k = _attn_bwd(q_n, k_n, v_b, f_grp, do_att, lse, delta, attn_blk)
    (dpq,), (g["q_norm_g"],) = _row_bwd(_f_qnorm, [(proj_q, HEAD, 0)], [small["q_norm_g"]],
                                        [(dq_n, HEAD, 0)], [BF16], nsub=nh, tb=tb, name="q_norm_bwd")
    dproj_q = jnp.concatenate([dpq, dog], axis=1)
    dh_q = _mm_wblk_dx(dproj_q, wts["b_q"], BF16, "b_q_dx", k=d, gb=nb)
    g["b_q"] = _mm_wblk_dw(h_q, dproj_q, "b_q_dw", nb=nb, gb=nb // 4, tk=t)
    (dpk,), (g["k_norm_g"],) = _row_bwd(_f_knorm, [(proj_k, HEAD, 0)], [small["k_norm_g"]],
                                        [(dk_n, HEAD, 0)], [BF16], nsub=nh, tb=tb, name="k_norm_bwd")
    df_t = dfc_q.transpose(0, 2, 1).reshape(nh, t) + dfr_k.transpose(0, 2, 1, 3).reshape(nh, t)
    dflogit_t, g["kv_b_f"] = _fgate_bwd(f_logit_t, f_bias, df_t)
    dproj_f = jnp.pad(dflogit_t.T, ((0, 0), (0, LANES - nh))).astype(BF16)
    dh_kv = _mm(dpk, wts["kv_k"], "nn", BF16, "k_proj_dx")
    dh_kv_v = _mm(dv, wts["kv_v"], "nn", BF16, "v_proj_dx")
    dh_kv_f = _mm(dproj_f, wts["kv_f"], "nn", BF16, "kv_fproj_dx")
    g["kv_k"] = _mm(dpk, h_kv, "tn", BF16, "k_proj_dw", tk=t)
    g["kv_v"] = _mm(dv, h_kv, "tn", BF16, "v_proj_dw", tk=t)
    g["kv_f"] = _mm(dproj_f, h_kv, "tn", F32, "kv_fproj_dw", tk=1024)
    sent = put_g("l1", {n: g.pop(n) for n in ("b_out", "b_q", "kv_k", "kv_v", "kv_f", "up1", "down1")})
    (dx1, df0), (dmods["g2_0"], dmods["kv_sh"], dmods["kv_sc"], dmods["sh1_1"], dmods["sc1_1"]) = _row_bwd(
        _f_res_mod2, [(x1, d, 0), (f0, d, 0)], [mods["g2_0"] + sent] + vec("kv_sh", "kv_sc", "sh1_1", "sc1_1"),
        [(dx2, d, 0), (dh_kv, d, 0), (dh_q, d, 0)], [F32, BF16], tb=tb, name="l0_res_kvmod_qmod_bwd",
        cot_add=[(1, dh_kv_v), (1, dh_kv_f)])
    dh2, g["up0"], g["down0"], g["conv_w0"], g["conv_b0"] = ffn_bwd(df0, h2_0, u0, a0, 0)
    (dx0, dy_a), (dmods["g1_0"], dmods["sh2_0"], dmods["sc2_0"]) = _row_bwd(
        _f_res_mod, [(x, d, 0), (y_a, d, 0)], vec("g1_0", "sh2_0", "sc2_0"),
        [(dx1, d, 0), (dh2, d, 0)], [F32, BF16], tb=tb, name="l0_res_mod2_bwd")
    dypre = _mm(dy_a, wts["a_out"], "nt", BF16, "a_out_dx")
    g["a_out"] = _mm(ypre, dy_a, "tn", BF16, "a_out_dw", tk=t)
    sent = put_g("l0b", {n: g.pop(n) for n in ("a_out", "up0", "down0")})
    dproj_a, dlb, g["a_norm_g"] = _hgrn2_bwd(proj_a, lb + sent, small["a_norm_g"], states, dypre, tb)
    dh_a = _mm_wblk_dx(dproj_a, wts["a_in"], BF16, "a_in_dx", k=d, gb=nb, split=4, tm=512)
    put_g("l0a", {"a_in": _mm_wblk_dw(h_a, dproj_a, "a_in_dw", nb=nb, gb=1, split=4, tk=t)})
    (grad_x,), (dmods["sh1_0"], dmods["sc1_0"]) = _row_bwd(
        _f_mod, [(x, d, 0)], vec("sh1_0", "sc1_0"), [(dh_a, d, 0)], [F32], tb=tb, name="l0_mod1_bwd",
        add_to=(0, dx0))
    return loss, grad_x, dmods, dlb, g


def _position():
    return lax.axis_index("x"), lax.axis_index("y"), lax.axis_index("c")


def _hbm_specs(n):
    return [pl.BlockSpec(memory_space=pl.ANY)] * n


def _all_gather(arrs, name):
    n = len(arrs)

    def body(*refs):
        x_refs, out_refs = refs[:n], refs[n:2 * n]
        send_sems, recv_sems, local_sems = refs[2 * n:]
        x, y, cc = _position()
        me, sibling = (x, y, cc), (x, y, 1 - cc)
        chips = [(1 - x, y), (x, 1 - y), (1 - x, 1 - y)]

        def copy(a, k, block, to, src=None):
            slot = out_refs[a].at[4 * block[0] + 2 * block[1] + block[2]]
            return pltpu.make_async_remote_copy(
                src_ref=slot if src is None else src, dst_ref=slot,
                send_sem=send_sems.at[7 * a + k], recv_sem=recv_sems.at[7 * a + k],
                device_id=to, device_id_type=_MESH)

        local = [pltpu.make_async_copy(x_refs[a], out_refs[a].at[4 * x + 2 * y + cc], local_sems.at[a])
                 for a in range(n)]
        for cp in local:
            cp.start()
        first = []
        for a in range(n):
            first.append(copy(a, 0, me, sibling, src=x_refs[a]))
            first += [copy(a, 1 + j, me, (*chip, cc), src=x_refs[a]) for j, chip in enumerate(chips)]
        for cp in first:
            cp.start()
        passed = []
        for j, chip in enumerate(chips):
            for a in range(n):
                copy(a, 1 + j, (*chip, cc), me).wait_recv()
                fwd = copy(a, 4 + j, (*chip, cc), sibling)
                fwd.start()
                passed.append(fwd)
        for a in range(n):
            copy(a, 0, sibling, me).wait_recv()
        for j, chip in enumerate(chips):
            for a in range(n):
                copy(a, 4 + j, (*chip, 1 - cc), me).wait_recv()
        for cp in first + passed:
            cp.wait_send()
        for cp in local:
            cp.wait()

    return pl.pallas_call(
        body, name=name,
        out_shape=[jax.ShapeDtypeStruct((NDEV, *a.shape), a.dtype) for a in arrs],
        in_specs=_hbm_specs(n), out_specs=_hbm_specs(n),
        scratch_shapes=[pltpu.SemaphoreType.DMA((7 * n,)), pltpu.SemaphoreType.DMA((7 * n,)),
                        pltpu.SemaphoreType.DMA((n,))],
    )(*arrs)


_XCHG_EFFECT = pltpu.SideEffectType.DATAFLOW_SIDE_EFFECTING
ALL_PEERS = (1, 2, 3, 4, 5, 6, 7)
SAME_CORE = (2, 4, 6)


def _xchg_copies(src_refs, land_refs, send_sems, recv_sems, local_sems, scatter, rels):
    x, y, cc = _position()
    me = 4 * x + 2 * y + cc
    remote, local = [], []
    for a, (src, land) in enumerate(zip(src_refs, land_refs)):
        local.append(pltpu.make_async_copy(src.at[me] if scatter else src, land.at[me], local_sems.at[a]))
        for idx, rel in enumerate(rels):
            px = 1 - x if rel & 4 else x
            py = 1 - y if rel & 2 else y
            pc = 1 - cc if rel & 1 else cc
            k = len(rels) * a + idx
            remote.append(pltpu.make_async_remote_copy(
                src_ref=src.at[4 * px + 2 * py + pc] if scatter else src, dst_ref=land.at[me],
                send_sem=send_sems.at[k], recv_sem=recv_sems.at[k], device_id=(px, py, pc), device_id_type=_MESH))
    return remote, local


def _xchg_start(srcs, scatter, rels, after, name):
    n = len(srcs)
    lands = [lax.empty(s.shape if scatter else (NDEV, *s.shape), s.dtype) for s in srcs]

    def body(*refs):
        remote, local = _xchg_copies(refs[:n], refs[n:2 * n], *refs[2 * n + 1:2 * n + 4], scatter, rels)
        for cp in local + remote:
            cp.start()
        token = refs[-1]
        token[...] = jnp.zeros_like(token)

    hbm = pl.BlockSpec(memory_space=pltpu.HBM)
    sem = pl.BlockSpec(memory_space=pltpu.SEMAPHORE)
    out = pl.pallas_call(
        body, name=name,
        out_shape=(pltpu.SemaphoreType.DMA((len(rels) * n,)), pltpu.SemaphoreType.DMA((len(rels) * n,)),
                   pltpu.SemaphoreType.DMA((n,)),
                   *[pltpu.HBM(a.shape, a.dtype) for a in srcs + lands], jax.ShapeDtypeStruct((8, LANES), F32)),
        in_specs=[hbm] * (2 * n) + [pl.BlockSpec(memory_space=pl.ANY)],
        out_specs=(sem, sem, sem, *[hbm] * (2 * n), pl.BlockSpec(memory_space=pltpu.VMEM)),
        input_output_aliases={i: 3 + i for i in range(2 * n)},
        compiler_params=pltpu.CompilerParams(has_side_effects=_XCHG_EFFECT),
    )(*[pltpu.with_memory_space_constraint(a, pltpu.HBM) for a in srcs + lands], after)
    return out[:-1], out[-1][0, 0]


def _xchg_wait(handles, after, scatter, rels, name):
    n = (len(handles) - 3) // 2

    def body(*refs):
        remote, local = _xchg_copies(refs[:n], refs[n:2 * n], *refs[2 * n:2 * n + 3], scatter, rels)
        for cp in remote:
            cp.wait_send()
            cp.wait_recv()
        for cp in local:
            cp.wait()

    hbm = pl.BlockSpec(memory_space=pltpu.HBM)
    sem = pl.BlockSpec(memory_space=pltpu.SEMAPHORE)
    thru = list(handles[3:])
    out = pl.pallas_call(
        body, name=name,
        out_shape=tuple(pltpu.HBM(a.shape, a.dtype) for a in thru),
        in_specs=[hbm] * (2 * n) + [sem, sem, sem, pl.BlockSpec(memory_space=pl.ANY)],
        out_specs=tuple([hbm] * (2 * n)),
        input_output_aliases={i: i for i in range(2 * n)},
        compiler_params=pltpu.CompilerParams(has_side_effects=_XCHG_EFFECT),
    )(*thru, *handles[:3], after)
    return list(out[n:])


def _sibling_copies(land_refs, send_sems, recv_sems):
    x, y, cc = _position()

    def copy(a, q, core):
        slot = land_refs[a].at[2 * q + core]
        return pltpu.make_async_remote_copy(
            src_ref=slot, dst_ref=slot, send_sem=send_sems.at[NCHIP * a + q], recv_sem=recv_sems.at[NCHIP * a + q],
            device_id=(x, y, 1 - cc), device_id_type=_MESH)

    pairs = [(a, q) for a in range(len(land_refs)) for q in range(NCHIP)]
    return [copy(a, q, cc) for a, q in pairs], [copy(a, q, 1 - cc) for a, q in pairs]


def _sibling_forward_start(lands, name):
    n = len(lands)

    def body(*refs):
        sends, _ = _sibling_copies(refs[:n], refs[n], refs[n + 1])
        for cp in sends:
            cp.start()
        refs[-1][...] = jnp.zeros_like(refs[-1])

    hbm = pl.BlockSpec(memory_space=pltpu.HBM)
    sem = pl.BlockSpec(memory_space=pltpu.SEMAPHORE)
    out = pl.pallas_call(
        body, name=name,
        out_shape=(pltpu.SemaphoreType.DMA((NCHIP * n,)), pltpu.SemaphoreType.DMA((NCHIP * n,)),
                   *[pltpu.HBM(a.shape, a.dtype) for a in lands], jax.ShapeDtypeStruct((8, LANES), F32)),
        in_specs=[hbm] * n,
        out_specs=(sem, sem, *[hbm] * n, pl.BlockSpec(memory_space=pltpu.VMEM)),
        input_output_aliases={i: 2 + i for i in range(n)},
        compiler_params=pltpu.CompilerParams(has_side_effects=_XCHG_EFFECT),
    )(*lands)
    return out[:-1], out[-1][0, 0]


def _sibling_forward_wait(handles, after, name):
    n = len(handles) - 2

    def body(*refs):
        sends, arrivals = _sibling_copies(refs[:n], refs[n], refs[n + 1])
        for cp in sends:
            cp.wait_send()
        for cp in arrivals:
            cp.wait_recv()

    hbm = pl.BlockSpec(memory_space=pltpu.HBM)
    sem = pl.BlockSpec(memory_space=pltpu.SEMAPHORE)
    lands = list(handles[2:])
    return list(pl.pallas_call(
        body, name=name,
        out_shape=tuple(pltpu.HBM(a.shape, a.dtype) for a in lands),
        in_specs=[hbm] * n + [sem, sem, pl.BlockSpec(memory_space=pl.ANY)],
        out_specs=tuple([hbm] * n),
        input_output_aliases={i: i for i in range(n)},
        compiler_params=pltpu.CompilerParams(has_side_effects=_XCHG_EFFECT),
    )(*lands, *handles[:2], after))


def _slab_sum(slabs, name, tr=None):
    n, r, c = slabs.shape
    tr = r if tr is None else tr

    def body(s_ref, o_ref):
        acc = s_ref[0].astype(F32)
        for q in range(1, n):
            acc = acc + s_ref[q].astype(F32)
        o_ref[...] = acc

    return pl.pallas_call(body, name=name, grid=(r // tr,),
                          in_specs=[pl.BlockSpec((n, tr, c), lambda i: (0, i, 0))],
                          out_specs=pl.BlockSpec((tr, c), lambda i: (i, 0)),
                          out_shape=jax.ShapeDtypeStruct((r, c), F32),
                          compiler_params=_cparams(dimension_semantics=("parallel",)))(slabs)


def _ada_fwd(c_all, ada_w, kv_ada_w, logits):
    rows, d = c_all.shape
    n0, nkv = ada_w.shape[2], kv_ada_w.shape[1]

    def body(c_ref, w_ref, kw_ref, lg_ref, part_ref, cact_ref, lb_ref):
        ca = _silu(c_ref[...])
        cact_ref[...] = ca
        part_ref[:, 0:n0] = _bdot_raw(ca, w_ref[0], _NN)
        part_ref[:, n0:2 * n0] = _bdot_raw(ca, w_ref[1], _NN)
        part_ref[:, 2 * n0:2 * n0 + nkv] = _bdot_raw(ca, kw_ref[...], _NN)
        lb_ref[...] = _sigmoid(lg_ref[0:1, :] - lg_ref[1:2, :])

    vm = pl.BlockSpec(memory_space=pltpu.VMEM)
    return pl.pallas_call(
        body, name="ada_fwd", in_specs=[vm, vm, vm, vm], out_specs=[vm, vm, vm],
        out_shape=[jax.ShapeDtypeStruct((rows, 2 * n0 + nkv), F32), jax.ShapeDtypeStruct((rows, d), F32),
                   jax.ShapeDtypeStruct((1, d), F32)],
        compiler_params=_cparams(),
    )(c_all, ada_w, kv_ada_w, logits)


def _ada_bwd(c_act, dm0, dm1, dkv, lb, dlb):
    rows, d = c_act.shape

    def body(c_ref, d0_ref, d1_ref, dk_ref, lb_ref, dlb_ref, dw_ref, dkw_ref, dlg_ref):
        ca = c_ref[...]
        dw_ref[0] = _bdot_raw(ca, d0_ref[...], _TN)
        dw_ref[1] = _bdot_raw(ca, d1_ref[...], _TN)
        dkw_ref[...] = _bdot_raw(ca, dk_ref[...], _TN)
        lbv = lb_ref[...]
        dl0 = dlb_ref[...] * lbv * (1.0 - lbv)
        dlg_ref[0:1, :] = dl0
        dlg_ref[1:2, :] = -dl0

    vm = pl.BlockSpec(memory_space=pltpu.VMEM)
    return pl.pallas_call(
        body, name="ada_bwd", in_specs=[vm] * 6, out_specs=[vm, vm, vm],
        out_shape=[jax.ShapeDtypeStruct((2, d, dm0.shape[1]), F32), jax.ShapeDtypeStruct((d, dkv.shape[1]), F32),
                   jax.ShapeDtypeStruct((2, d), F32)],
        compiler_params=_cparams(),
    )(c_act, dm0, dm1, dkv, lb, dlb)


def _adamw(w, g, m, v, name, tr=512, after=None):
    r, c = w.shape
    tr = _divisor_tile(r, tr, unit=8)
    c1 = 1.0 - ADAM_B1 ** ADAM_STEP
    c2 = 1.0 - ADAM_B2 ** ADAM_STEP
    deps = [] if after is None else [after]

    def body(w_ref, g_ref, m_ref, v_ref, *rest):
        d_ref, mo_ref, vo_ref = rest[len(deps):]
        gv = g_ref[...]
        mn = ADAM_B1 * m_ref[...] + (1.0 - ADAM_B1) * gv
        vn = ADAM_B2 * v_ref[...] + (1.0 - ADAM_B2) * (gv * gv)
        d_ref[...] = -ADAM_LR * ((mn / c1) / (jnp.sqrt(vn / c2) + ADAM_EPS) + ADAM_WD * w_ref[...])
        mo_ref[...] = mn
        vo_ref[...] = vn

    spec = pl.BlockSpec((tr, c), lambda i: (i, 0))
    out = jax.ShapeDtypeStruct((r, c), F32)
    return pl.pallas_call(body, name=name, grid=(r // tr,),
                          in_specs=[spec] * 4 + [pl.BlockSpec(a.shape, lambda i: (0, 0)) for a in deps],
                          out_specs=[spec] * 3, out_shape=[out, out, out],
                          compiler_params=_cparams(dimension_semantics=("parallel",)))(w, g, m, v, *deps)


def _pad_rows(a, rows):
    return jnp.pad(a, ((0, rows - a.shape[0]), (0, 0)))


def _pack_small(parts, lanes=LANES, row_unit=8):
    flat = jnp.concatenate([p.reshape(-1).astype(F32) for p in parts])
    rows = _round_up(-(-flat.shape[0] // lanes), row_unit)
    return jnp.pad(flat, (0, rows * lanes - flat.shape[0])).reshape(rows, lanes)


def _unpack_small(flat, shapes):
    out, off = [], 0
    for s in shapes:
        n = 1
        for k in s:
            n *= k
        out.append(flat[off:off + n].reshape(s))
        off += n
    return out


def _pad_shard_cols(a, n_loc, n_pad):
    lead = a.shape[:-1]
    a = a.reshape(*lead, NDEV, n_loc)
    a = jnp.pad(a, [(0, 0)] * (len(lead) + 1) + [(0, n_pad - n_loc)])
    return a.reshape(*lead, NDEV * n_pad)


def _unpad_shard_cols(a, n_loc, n_pad):
    lead = a.shape[:-1]
    return a.reshape(*lead, NDEV, n_pad)[..., :n_loc].reshape(*lead, NDEV * n_loc)


def kernel(x, c, ada_w, ada_b, a_w_in, a_lb_logits, a_norm_g, a_w_out, kv_ada_w, kv_ada_b, kv_w, kv_b_f, k_norm_g, b_w_q, q_norm_g, b_w_out, ffn_w_up, ffn_conv_w, ffn_conv_b, ffn_w_down, loss_target, m_ada_w, m_ada_b, m_a_w_in, m_a_lb_logits, m_a_norm_g, m_a_w_out, m_kv_ada_w, m_kv_ada_b, m_kv_w, m_kv_b_f, m_k_norm_g, m_b_w_q, m_q_norm_g, m_b_w_out, m_ffn_w_up, m_ffn_conv_w, m_ffn_conv_b, m_ffn_w_down, v_ada_w, v_ada_b, v_a_w_in, v_a_lb_logits, v_a_norm_g, v_a_w_out, v_kv_ada_w, v_kv_ada_b, v_kv_w, v_kv_b_f, v_k_norm_g, v_b_w_q, v_q_norm_g, v_b_w_out, v_ffn_w_up, v_ffn_conv_w, v_ffn_conv_b, v_ffn_w_down):
    t, d = x.shape[1], x.shape[2]
    nh = d // HEAD
    ncw = ffn_w_up.shape[2]
    ncp = _round_up(ncw, LANES)
    two_f = ncw * NDEV
    ff = two_f // 2
    fp = ncp * NDEV // 2
    rd = ffn_w_down.shape[1]
    me = 4 * lax.axis_index("x") + 2 * lax.axis_index("y") + lax.axis_index("c")
    weights = dict(ada_w=ada_w, ada_b=ada_b, a_w_in=a_w_in, a_lb_logits=a_lb_logits, a_norm_g=a_norm_g,
                   a_w_out=a_w_out, kv_ada_w=kv_ada_w, kv_ada_b=kv_ada_b, kv_w=kv_w, kv_b_f=kv_b_f,
                   k_norm_g=k_norm_g, b_w_q=b_w_q, q_norm_g=q_norm_g, b_w_out=b_w_out, ffn_w_up=ffn_w_up,
                   ffn_conv_w=ffn_conv_w, ffn_conv_b=ffn_conv_b, ffn_w_down=ffn_w_down)
    m_in = dict(ada_w=m_ada_w, ada_b=m_ada_b, a_w_in=m_a_w_in, a_lb_logits=m_a_lb_logits, a_norm_g=m_a_norm_g,
                a_w_out=m_a_w_out, kv_ada_w=m_kv_ada_w, kv_ada_b=m_kv_ada_b, kv_w=m_kv_w, kv_b_f=m_kv_b_f,
                k_norm_g=m_k_norm_g, b_w_q=m_b_w_q, q_norm_g=m_q_norm_g, b_w_out=m_b_w_out, ffn_w_up=m_ffn_w_up,
                ffn_conv_w=m_ffn_conv_w, ffn_conv_b=m_ffn_conv_b, ffn_w_down=m_ffn_w_down)
    v_in = dict(ada_w=v_ada_w, ada_b=v_ada_b, a_w_in=v_a_w_in, a_lb_logits=v_a_lb_logits, a_norm_g=v_a_norm_g,
                a_w_out=v_a_w_out, kv_ada_w=v_kv_ada_w, kv_ada_b=v_kv_ada_b, kv_w=v_kv_w, kv_b_f=v_kv_b_f,
                k_norm_g=v_k_norm_g, b_w_q=v_b_w_q, q_norm_g=v_q_norm_g, b_w_out=v_b_w_out, ffn_w_up=v_ffn_w_up,
                ffn_conv_w=v_ffn_conv_w, ffn_conv_b=v_ffn_conv_b, ffn_w_down=v_ffn_w_down)
    order = list(weights)

    up_loc = jnp.pad(ffn_w_up, ((0, 0), (0, 0), (0, ncp - ncw))).astype(BF16)
    down_loc = ffn_w_down.astype(BF16)
    gather_names = {"l0b": ["a_out", "up0", "down0"], "l1": ["kv", "b_q", "b_out", "up1", "down1"]}
    shards = {"a_out": a_w_out[0].astype(BF16), "up0": up_loc[0], "down0": down_loc[0], "kv": kv_w.T.astype(BF16),
              "b_q": b_w_q[0].astype(BF16), "b_out": b_w_out[0].astype(BF16), "up1": up_loc[1],
              "down1": down_loc[1]}
    pre = _pack_small([c, a_lb_logits, ffn_conv_w])
    a_in_all, pre_all = _all_gather([a_w_in[0].astype(BF16), pre], "gather_a_w_in_and_small_inputs")
    pre_all = pre_all.reshape(NDEV, -1)
    c_all = pre_all[:, :d]
    logits = pre_all[:, d:d + 2 * HEAD].reshape(NDEV, 2, HEAD).transpose(1, 0, 2).reshape(2, d)
    conv_w_full = pre_all[:, d + 2 * HEAD:d + 2 * HEAD + 2 * CONV_TAPS * ncw]
    conv_w_full = conv_w_full.reshape(NDEV, 2, CONV_TAPS, ncw).transpose(1, 2, 0, 3).reshape(2, CONV_TAPS, two_f)

    part, c_act, lb = _ada_fwd(_pad_rows(c_all, 2 * NDEV), ada_w, kv_ada_w, logits)
    (part_all,) = _all_gather([part[:NDEV]], "gather_adaln")
    mine = lax.dynamic_index_in_dim(part_all, me, axis=1, keepdims=False)
    n0, nkv = ada_w.shape[2], kv_ada_w.shape[1]
    mod_names = ["sh1", "sc1", "g1", "sh2", "sc2", "g2"]
    mods = {}
    for l in range(2):
        row = mine[:, l * n0:(l + 1) * n0].reshape(-1) + ada_b[l]
        for k, nm in enumerate(mod_names):
            mods[f"{nm}_{l}"] = row[k * d:(k + 1) * d].reshape(1, d)
    kvrow = mine[:, 2 * n0:2 * n0 + nkv].reshape(-1) + kv_ada_b
    mods["kv_sh"], mods["kv_sc"] = kvrow[:d].reshape(1, d), kvrow[d:].reshape(1, d)

    in_flight = {}

    def start_gather(grp, dep):
        srcs = [shards[n] for n in gather_names[grp]]
        in_flight[grp], started = _xchg_start(srcs, False, SAME_CORE, dep, f"gather_{grp}_start")
        return started

    zero = start_gather("l0b", part_all)
    mods["sh1_0"] = mods["sh1_0"] + zero

    small = {"a_norm_g": a_norm_g, "k_norm_g": k_norm_g.reshape(1, HEAD), "q_norm_g": q_norm_g, "kv_b_f": kv_b_f}
    for l in range(2):
        small[f"conv_w{l}"] = _pad_shard_cols(conv_w_full[l], ncw, ncp).reshape(CONV_TAPS, 2, fp).transpose(1, 0, 2)
        small[f"conv_b{l}"] = _pad_shard_cols(ffn_conv_b[l], ncw, ncp).reshape(2, 1, fp)

    forwarding = {}

    def pre_w(grp, after):
        arrived = _xchg_wait(in_flight[grp], after, False, SAME_CORE, f"gather_{grp}_wait")
        forwarding[grp], started = _sibling_forward_start(arrived, f"gather_{grp}_to_sibling_start")
        return started

    def get_w(grp, after):
        if grp == "l0a":
            return {"a_in": a_in_all}
        full = _sibling_forward_wait(forwarding[grp], after, f"gather_{grp}_to_sibling_wait")
        if grp == "l0b":
            started = start_gather("l1", full[0])
            full[0] = full[0] + started.astype(full[0].dtype)
        got = dict(zip(gather_names[grp], full))
        out = {}
        for n, a in got.items():
            if n in ("a_out", "b_out"):
                out[n] = a.reshape(d, d)
            elif n in ("down0", "down1"):
                dn = a.reshape(NCHIP, ff // NCHIP, d)
                out[n] = jnp.pad(dn, ((0, 0), (0, ncp - ncw), (0, 0))).reshape(fp, d)
            elif n == "kv":
                kv_t = a.reshape(NDEV * kv_w.shape[1], d)
                out["kv_k"], out["kv_v"] = kv_t[:d], kv_t[d:2 * d]
                out["kv_f"] = jnp.pad(kv_t[2 * d:], ((0, LANES - nh), (0, 0)))
            else:
                out[n] = a
        return out

    scatter_flight, g_last = {}, {}

    def put_g(grp, gr):
        if grp == "l0a":
            g_last.update(gr)
            return zero
        if grp == "l1":
            g_kvw = jnp.concatenate([gr["kv_k"], gr["kv_v"], gr["kv_f"][:nh].astype(BF16)], axis=0)
            arrs = {"kv_w": g_kvw.reshape(NDEV, kv_w.shape[1], d), "b_w_q": gr["b_q"],
                    "b_w_out": gr["b_out"].reshape(NDEV, d // NDEV, d), "up1": gr["up1"],
                    "down1": gr["down1"].reshape(NCHIP, ncp, d)[:, :ncw].reshape(NDEV, rd, d)}
        else:
            arrs = {"a_w_out": gr["a_out"].reshape(NDEV, d // NDEV, d), "up0": gr["up0"],
                    "down0": gr["down0"].reshape(NCHIP, ncp, d)[:, :ncw].reshape(NDEV, rd, d)}
        srcs = list(arrs.values())
        handles, sent = _xchg_start(srcs, True, ALL_PEERS, srcs[0], f"scatter_{grp}_start")
        scatter_flight[grp] = (list(arrs), handles)
        return sent

    loss_v, grad_x, dmods, dlb, g = _local_step(x[0], loss_target[0], mods, lb, small, pre_w, get_w, put_g)

    g_sum = {}
    for grp in ("l1", "l0b"):
        names, handles = scatter_flight[grp]
        for nm, a in zip(names, _xchg_wait(handles, grad_x, True, ALL_PEERS, f"scatter_{grp}_wait")):
            g_sum[nm] = _slab_sum(a, f"rs_slab_sum_{nm}")

    def conv_w_grad(a):
        return _unpad_shard_cols(a.transpose(1, 0, 2).reshape(CONV_TAPS, 2 * fp), ncw, ncp)

    def conv_b_grad(a):
        return _unpad_shard_cols(a.reshape(2 * fp), ncw, ncp)

    dmod_vec = [dmods[f"{nm}_{l}"] for l in range(2) for nm in mod_names] + [dmods["kv_sh"], dmods["kv_sc"]]
    post = _pack_small(dmod_vec + [dlb, g["a_norm_g"], g["k_norm_g"], g["q_norm_g"],
                                   jnp.pad(g["kv_b_f"].reshape(-1), (0, LANES - nh)),
                                   conv_w_grad(g["conv_w0"]), conv_w_grad(g["conv_w1"]),
                                   conv_b_grad(g["conv_b0"]), conv_b_grad(g["conv_b1"]), loss_v])
    (post_all,) = _all_gather([post], "gather_small_grads")
    a_in_flight, a_in_sent = _xchg_start([g_last["a_in"]], True, ALL_PEERS, post_all, "scatter_l0a_start")
    a_in_sent = a_in_sent.reshape(1, 1)
    tot = _slab_sum(post_all, "small_grad_sum").reshape(-1)
    nmod = 14 * d
    (t_mod, t_lb, t_ang, t_kng, t_qng, t_bf, t_cw, t_cb, t_loss) = _unpack_small(
        tot, [(nmod,), (1, d), (1, HEAD), (HEAD,), (1, HEAD), (LANES,), (2, CONV_TAPS, two_f), (2, two_f),
              (LANES,)])
    loss = t_loss[0]
    dm_all = post_all.reshape(NDEV, -1)[:, :nmod]
    dm0 = lax.dynamic_slice_in_dim(dm_all[:, :6 * d], me * n0, n0, axis=1)
    dm1 = lax.dynamic_slice_in_dim(dm_all[:, 6 * d:12 * d], me * n0, n0, axis=1)
    dkv = lax.dynamic_slice_in_dim(dm_all[:, 12 * d:], me * nkv, nkv, axis=1)
    g_ada_w, g_kv_ada_w, g_logits = _ada_bwd(c_act, _pad_rows(dm0, 2 * NDEV), _pad_rows(dm1, 2 * NDEV),
                                              _pad_rows(dkv, 2 * NDEV), lb, t_lb)

    grads = {
        "ada_w": g_ada_w,
        "ada_b": t_mod[:12 * d].reshape(2, 6 * d),
        "a_lb_logits": lax.dynamic_slice_in_dim(g_logits, me * HEAD, HEAD, axis=1),
        "a_norm_g": t_ang,
        "a_w_out": g_sum["a_w_out"].reshape(a_w_out.shape),
        "kv_ada_w": g_kv_ada_w,
        "kv_ada_b": t_mod[12 * d:],
        "kv_w": g_sum["kv_w"].T,
        "kv_b_f": t_bf[:nh],
        "k_norm_g": t_kng,
        "b_w_q": g_sum["b_w_q"].reshape(b_w_q.shape),
        "q_norm_g": t_qng,
        "b_w_out": g_sum["b_w_out"].reshape(b_w_out.shape),
        "ffn_w_up": jnp.stack([g_sum["up0"][:, :ncw], g_sum["up1"][:, :ncw]]),
        "ffn_conv_w": lax.dynamic_slice_in_dim(t_cw, me * ncw, ncw, axis=2),
        "ffn_conv_b": t_cb,
        "ffn_w_down": jnp.stack([g_sum["down0"], g_sum["down1"]]),
    }

    big_adam = ["ada_w", "a_w_out", "kv_ada_w", "kv_w", "b_w_q", "b_w_out", "ffn_w_up", "ffn_w_down", "a_w_in"]
    small_adam = [n for n in order if n not in big_adam]
    delta, new_m, new_v = {}, {}, {}
    packs = [_pack_small([src[n] for n in small_adam]) for src in (weights, grads, m_in, v_in)]
    outs = _adamw(*packs, "adamw_small", tr=packs[0].shape[0])
    shapes = [weights[n].shape for n in small_adam]
    for dst, o in zip((delta, new_m, new_v), outs):
        for n, a in zip(small_adam, _unpack_small(o.reshape(-1), shapes)):
            dst[n] = a
    for n in big_adam:
        if n == "a_w_in":
            (landed,) = _xchg_wait(a_in_flight, new_v["ffn_w_down"], True, ALL_PEERS, "scatter_l0a_wait")
            grads[n] = _slab_sum(landed, "rs_slab_sum_a_w_in").reshape(a_w_in.shape)
        shp = weights[n].shape
        two_d = lambda a: a.reshape(-1, shp[-1])
        dl, mn, vn = _adamw(two_d(weights[n]), two_d(grads[n]), two_d(m_in[n]), two_d(v_in[n]), f"adamw_{n}",
                            after=a_in_sent)
        delta[n], new_m[n], new_v[n] = dl.reshape(shp), mn.reshape(shp), vn.reshape(shp)

    return (loss, grad_x.reshape(x.shape), *[grads[n] for n in order], *[delta[n] for n in order],
            *[new_m[n] for n in order], *[new_v[n] for n in order])
```

```python
import functools

import jax
import jax.numpy as jnp
from jax import lax
from jax.experimental import pallas as pl
from jax.experimental.pallas import tpu as pltpu

F32 = jnp.float32
BF16 = jnp.bfloat16

NDEV = 8
NCHIP = 4
HEAD = 128
A_CHUNK = 64
CONV_TAPS = 3
EPS = 1e-6
NEG_INF = -1e30
LANES = 128
VMEM_LIMIT = 48 * 1024 * 1024

ADAM_LR = 0.001
ADAM_B1 = 0.9
ADAM_B2 = 0.999
ADAM_EPS = 1e-08
ADAM_WD = 0.01
ADAM_STEP = 10

_NN = (((1,), (0,)), ((), ()))
_NT = (((1,), (1,)), ((), ()))
_TN = (((0,), (0,)), ((), ()))
_MESH = pl.DeviceIdType.MESH


def _cparams(**kw):
    return pltpu.CompilerParams(vmem_limit_bytes=VMEM_LIMIT, **kw)


def _divisor_tile(n, pref, unit=LANES):
    if n <= pref:
        return n
    best = None
    for t in range(unit, pref + 1, unit):
        if n % t == 0:
            best = t
    assert best is not None, (n, pref)
    return best


def _round_up(n, unit):
    return -(-n // unit) * unit


def _bdot_raw(a, b, dims):
    return lax.dot_general(a.astype(BF16), b.astype(BF16), dims, preferred_element_type=F32)


@jax.custom_vjp
def _dot_nn(a, b):
    return _bdot_raw(a, b, _NN)


@jax.custom_vjp
def _dot_nt(a, b):
    return _bdot_raw(a, b, _NT)


@jax.custom_vjp
def _dot_tn(a, b):
    return _bdot_raw(a, b, _TN)


_dot_nn.defvjp(lambda a, b: (_bdot_raw(a, b, _NN), (a, b)),
               lambda r, g: (_dot_nt(g, r[1]), _dot_tn(r[0], g)))
_dot_nt.defvjp(lambda a, b: (_bdot_raw(a, b, _NT), (a, b)),
               lambda r, g: (_dot_nn(g, r[1]), _dot_tn(g, r[0])))
_dot_tn.defvjp(lambda a, b: (_bdot_raw(a, b, _TN), (a, b)),
               lambda r, g: (_dot_nt(r[1], g), _dot_nn(r[0], g)))


def _f32dot(a, b):
    return lax.dot_general(a, b, _NN, precision=lax.Precision.HIGHEST, preferred_element_type=F32)


def _sigmoid(x):
    return jax.nn.sigmoid(x)


def _silu(x):
    return x * jax.nn.sigmoid(x)


def _rms(x):
    return x * lax.rsqrt(jnp.mean(x * x, axis=-1, keepdims=True) + EPS)


def _modulate(x, sh, sc):
    return _rms(x) * (1.0 + sc) + sh


def _mm_call(a, b, dims, a_spec, b_spec, o_spec, o_shape, grid, acc_tile, name):
    nk = grid[2]

    def body(a_ref, b_ref, o_ref, *acc):
        p = lax.dot_general(a_ref[...].astype(BF16), b_ref[...].astype(BF16), dims,
                            preferred_element_type=F32)
        if nk == 1:
            o_ref[...] = p.astype(o_ref.dtype)
        else:
            kk = pl.program_id(2)

            @pl.when(kk == 0)
            def _():
                acc[0][...] = p

            @pl.when(kk > 0)
            def _():
                acc[0][...] += p

            @pl.when(kk == nk - 1)
            def _():
                o_ref[...] = acc[0][...].astype(o_ref.dtype)

    return pl.pallas_call(
        body, name=name, grid=grid, in_specs=[a_spec, b_spec], out_specs=o_spec, out_shape=o_shape,
        scratch_shapes=[pltpu.VMEM(acc_tile, F32)] if nk > 1 else [],
        compiler_params=_cparams(dimension_semantics=("parallel", "parallel", "arbitrary")),
    )(a, b)


def _mm(a, b, mode, out_dtype, name, tm=1024, tn=1024, tk=2048):
    if mode == "nn":
        (m, k), (k2, n) = a.shape, b.shape
    elif mode == "nt":
        (m, k), (n, k2) = a.shape, b.shape
    else:
        (k, m), (k2, n) = a.shape, b.shape
    assert k == k2, (a.shape, b.shape, mode)
    tm, tn, tk = _divisor_tile(m, tm), _divisor_tile(n, tn), _divisor_tile(k, tk)
    if mode == "tn":
        a_spec = pl.BlockSpec((tk, tm), lambda i, j, kk: (kk, i))
    else:
        a_spec = pl.BlockSpec((tm, tk), lambda i, j, kk: (i, kk))
    if mode == "nt":
        b_spec = pl.BlockSpec((tn, tk), lambda i, j, kk: (j, kk))
    else:
        b_spec = pl.BlockSpec((tk, tn), lambda i, j, kk: (kk, j))
    return _mm_call(a, b, {"nn": _NN, "nt": _NT, "tn": _TN}[mode], a_spec, b_spec,
                    pl.BlockSpec((tm, tn), lambda i, j, kk: (i, j)), jax.ShapeDtypeStruct((m, n), out_dtype),
                    (m // tm, n // tn, k // tk), (tm, tn), name)


def _wblk_act_spec(rows, gb, nl, split, nb, row_axis, blk_axis):
    if split == 1:
        return pl.BlockSpec((rows, gb * nl), lambda *g: (g[row_axis], g[blk_axis]))
    groups = nb // split // gb
    return pl.BlockSpec((None, rows, gb * nl),
                        lambda *g: (g[blk_axis] // groups, g[row_axis], g[blk_axis] % groups))


def _mm_wblk(a, wb, out_dtype, name, *, gb, row_off=0, split=1, tm=1024):
    m, k = a.shape
    nb, _, nl = wb.shape
    assert (nb // split) % gb == 0
    tm = _divisor_tile(m, tm)

    def body(a_ref, b_ref, o_ref):
        av = a_ref[...].astype(BF16)
        for s in range(gb):
            o_ref[:, s * nl:(s + 1) * nl] = lax.dot_general(
                av, b_ref[s].astype(BF16), _NN, preferred_element_type=F32).astype(o_ref.dtype)

    o_shape = (m, nb * nl) if split == 1 else (split, m, nb // split * nl)
    return pl.pallas_call(
        body, name=name, grid=(nb // gb, m // tm),
        in_specs=[pl.BlockSpec((tm, k), lambda j, i: (i, 0)),
                  pl.BlockSpec((gb, k, nl), lambda j, i: (j, row_off, 0))],
        out_specs=_wblk_act_spec(tm, gb, nl, split, nb, 1, 0),
        out_shape=jax.ShapeDtypeStruct(o_shape, out_dtype),
        compiler_params=_cparams(dimension_semantics=("parallel", "parallel")),
    )(a, wb)


def _mm_wblk_dx(dy, wb, out_dtype, name, *, k, gb, row_off=0, split=1, tm=1024):
    nb, _, nl = wb.shape
    m = dy.shape[-2]
    tm = _divisor_tile(m, tm)
    nk = nb // gb
    per = nb // split
    whole = split > 1 and gb == nb
    assert whole or per % gb == 0

    def body(a_ref, b_ref, o_ref, *acc):
        p = None
        for s in range(gb):
            a_blk = a_ref[s // per, :, (s % per) * nl:(s % per + 1) * nl] if whole else a_ref[:, s * nl:(s + 1) * nl]
            q = lax.dot_general(a_blk.astype(BF16), b_ref[s].astype(BF16), _NT, preferred_element_type=F32)
            p = q if p is None else p + q
        if nk == 1:
            o_ref[...] = p.astype(o_ref.dtype)
        else:
            kk = pl.program_id(1)

            @pl.when(kk == 0)
            def _():
                acc[0][...] = p

            @pl.when(kk > 0)
            def _():
                acc[0][...] += p

            @pl.when(kk == nk - 1)
            def _():
                o_ref[...] = acc[0][...].astype(o_ref.dtype)

    return pl.pallas_call(
        body, name=name, grid=(m // tm, nk),
        in_specs=[pl.BlockSpec((split, tm, per * nl), lambda i, kk: (0, i, 0)) if whole
                  else _wblk_act_spec(tm, gb, nl, split, nb, 0, 1),
                  pl.BlockSpec((gb, k, nl), lambda i, kk: (kk, row_off, 0))],
        out_specs=pl.BlockSpec((tm, k), lambda i, kk: (i, 0)),
        out_shape=jax.ShapeDtypeStruct((m, k), out_dtype),
        scratch_shapes=[pltpu.VMEM((tm, k), F32)] if nk > 1 else [],
        compiler_params=_cparams(dimension_semantics=("parallel", "arbitrary")),
    )(dy, wb)


def _mm_wblk_dw(x, dy, name, *, nb, gb, split=1, tk=1024):
    t, k = x.shape
    assert (nb // split) % gb == 0
    nl = dy.shape[-1] * split // nb
    tk = _divisor_tile(t, tk)
    nk = t // tk

    def body(a_ref, b_ref, o_ref, *acc):
        kk = pl.program_id(1)
        av = a_ref[...].astype(BF16)
        for s in range(gb):
            p = lax.dot_general(av, b_ref[:, s * nl:(s + 1) * nl].astype(BF16), _TN, preferred_element_type=F32)
            if nk == 1:
                o_ref[s] = p.astype(o_ref.dtype)
                continue

            @pl.when(kk == 0)
            def _():
                acc[0][s] = p

            @pl.when(kk > 0)
            def _():
                acc[0][s] += p

        if nk > 1:
            @pl.when(kk == nk - 1)
            def _():
                o_ref[...] = acc[0][...].astype(o_ref.dtype)

    return pl.pallas_call(
        body, name=name, grid=(nb // gb, nk),
        in_specs=[pl.BlockSpec((tk, k), lambda j, kk: (kk, 0)), _wblk_act_spec(tk, gb, nl, split, nb, 1, 0)],
        out_specs=pl.BlockSpec((gb, k, nl), lambda j, kk: (j, 0, 0)),
        out_shape=jax.ShapeDtypeStruct((nb, k, nl), BF16),
        scratch_shapes=[pltpu.VMEM((gb, k, nl), F32)] if nk > 1 else [],
        compiler_params=_cparams(dimension_semantics=("parallel", "arbitrary")),
    )(x, dy)


def _row_specs(rows, tb, nsub):
    return [pl.BlockSpec((tb, nsub * cw), functools.partial(lambda i, off: (i, off), off=off))
            for (_, cw, off) in rows]


def _vec_specs(params):
    return [pl.BlockSpec(p.shape, lambda i: (0, 0)) for p in params]


def _row_fwd(f, rows, params, out_dtypes, *, nsub=1, tb, name):
    t = rows[0][0].shape[0]
    tb = min(tb, t)
    n_r, n_p = len(rows), len(params)
    blk = [jax.ShapeDtypeStruct((tb, cw), F32) for (_, cw, _) in rows]
    blk += [jax.ShapeDtypeStruct(p.shape, F32) for p in params]
    out_avals = jax.eval_shape(f, *blk)

    def body(*refs):
        pv = [r[...] for r in refs[n_r:n_r + n_p]]
        for s in range(nsub):
            vals = [r[:, s * cw:(s + 1) * cw].astype(F32) for r, (_, cw, _) in zip(refs[:n_r], rows)]
            outs = f(*vals, *pv)
            for o_ref, o in zip(refs[n_r + n_p:], outs):
                w = o.shape[1]
                o_ref[:, s * w:(s + 1) * w] = o.astype(o_ref.dtype)

    return pl.pallas_call(
        body, name=name,
        grid=(t // tb,),
        in_specs=_row_specs(rows, tb, nsub) + _vec_specs(params),
        out_specs=[pl.BlockSpec((tb, nsub * av.shape[1]), lambda i: (i, 0)) for av in out_avals],
        out_shape=[jax.ShapeDtypeStruct((t, nsub * av.shape[1]), dt) for av, dt in zip(out_avals, out_dtypes)],
        compiler_params=_cparams(dimension_semantics=("parallel",)),
    )(*[r[0] for r in rows], *params)


def _row_bwd(f, rows, params, cots, row_grad_dtypes, *, nsub=1, tb, name, add_to=None, cot_add=None):
    t = rows[0][0].shape[0]
    tb = min(tb, t)
    n_r, n_p, n_c = len(rows), len(params), len(cots)
    want = [j for j in range(n_r) if row_grad_dtypes[j] is not None]
    cot_add = cot_add or []
    extra = [] if add_to is None else [(add_to[1], rows[add_to[0]][1], 0)]
    n_add_to = len(extra)
    extra += [(arr, cots[ci][1], 0) for ci, arr in cot_add]

    def body(*refs):
        i = pl.program_id(0)
        r_in, p_in = refs[:n_r], refs[n_r:n_r + n_p]
        c_in = refs[n_r + n_p:n_r + n_p + n_c]
        e_in = refs[n_r + n_p + n_c:n_r + n_p + n_c + len(extra)]
        outs = refs[n_r + n_p + n_c + len(extra):]
        pv = [r[...] for r in p_in]
        psum = [None] * n_p
        for s in range(nsub):
            vals = [r[:, s * cw:(s + 1) * cw].astype(F32) for r, (_, cw, _) in zip(r_in, rows)]
            cvals = [r[:, s * cw:(s + 1) * cw].astype(F32) for r, (_, cw, _) in zip(c_in, cots)]
            for (ci, _), e_ref in zip(cot_add, e_in[n_add_to:]):
                cw = cots[ci][1]
                cvals[ci] = cvals[ci] + e_ref[:, s * cw:(s + 1) * cw].astype(F32)
            _, vjp_fn = jax.vjp(f, *vals, *pv)
            grads = vjp_fn(tuple(cvals))
            for o_ref, jr in zip(outs[:len(want)], want):
                cw = rows[jr][1]
                gr = grads[jr]
                if add_to is not None and jr == add_to[0]:
                    gr = gr + e_in[0][:, s * cw:(s + 1) * cw]
                o_ref[:, s * cw:(s + 1) * cw] = gr.astype(o_ref.dtype)
            for jp in range(n_p):
                psum[jp] = grads[n_r + jp] if psum[jp] is None else psum[jp] + grads[n_r + jp]
        for o_ref, g in zip(outs[len(want):], psum):
            @pl.when(i == 0)
            def _():
                o_ref[...] = g

            @pl.when(i > 0)
            def _():
                o_ref[...] += g

    out_specs = [pl.BlockSpec((tb, nsub * rows[jr][1]), lambda i: (i, 0)) for jr in want]
    out_shape = [jax.ShapeDtypeStruct((t, nsub * rows[jr][1]), row_grad_dtypes[jr]) for jr in want]
    out_specs += _vec_specs(params)
    out_shape += [jax.ShapeDtypeStruct(p.shape, F32) for p in params]
    res = pl.pallas_call(
        body, name=name,
        grid=(t // tb,),
        in_specs=_row_specs(rows, tb, nsub) + _vec_specs(params) + _row_specs(cots, tb, nsub)
        + _row_specs(extra, tb, nsub),
        out_specs=out_specs, out_shape=out_shape,
        compiler_params=_cparams(dimension_semantics=("arbitrary",)),
    )(*[r[0] for r in rows], *params, *[c[0] for c in cots], *[e[0] for e in extra])
    return res[:len(want)], res[len(want):]


def _f_mod(x, sh, sc):
    return (_modulate(x, sh, sc),)


def _f_res_mod(x, y, g, sh, sc):
    x1 = x + g * y
    return x1, _modulate(x1, sh, sc)


def _f_res_mod2(x, y, g, sh_a, sc_a, sh_b, sc_b):
    x1 = x + g * y
    return x1, _modulate(x1, sh_a, sc_a), _modulate(x1, sh_b, sc_b)


def _f_qnorm(p, g):
    return (_rms(p) * g * (HEAD ** -0.5),)


def _f_knorm(p, g):
    return (_rms(p) * g,)


def _f_qnorm_aug(p, g):
    lane = lax.broadcasted_iota(jnp.int32, p.shape, 1)
    return (jnp.concatenate([_rms(p) * g * (HEAD ** -0.5), jnp.where(lane < 3, 1.0, 0.0)], axis=1),)


def _f_knorm_aug(p, c0, c1, c2, g):
    lane = lax.broadcasted_iota(jnp.int32, p.shape, 1)
    aug = jnp.where(lane == 0, c0, jnp.where(lane == 1, c1, jnp.where(lane == 2, c2, 0.0)))
    return (jnp.concatenate([_rms(p) * g, aug], axis=1),)


def _split3(a):
    round_bf16 = lambda v: lax.reduce_precision(v, exponent_bits=8, mantissa_bits=7)
    hi = round_bf16(a)
    mid = round_bf16(a - hi)
    lo = round_bf16(a - hi - mid)
    return hi.astype(BF16), mid.astype(BF16), lo.astype(BF16)


def _f_outgate(o, og):
    return (o * _sigmoid(og),)


def _loss_call(x3, f, g2, target, tb):
    t, d = x3.shape
    tb = min(tb, t)

    def body(x_ref, f_ref, g_ref, t_ref, loss_ref, dx_ref, df_ref, dg_ref):
        i = pl.program_id(0)
        fv = f_ref[...]
        g = g_ref[...]
        e = x_ref[...] + g * fv - t_ref[...]
        dx = e * (1.0 / d)
        part = 0.5 * jnp.sum(jnp.sum(e * dx, axis=1, keepdims=True), axis=0, keepdims=True)
        dx_ref[...] = dx
        df_ref[...] = (g * dx).astype(df_ref.dtype)
        dg = jnp.sum(dx * fv, axis=0, keepdims=True)

        @pl.when(i == 0)
        def _():
            loss_ref[...] = jnp.broadcast_to(part, loss_ref.shape)
            dg_ref[...] = dg

        @pl.when(i > 0)
        def _():
            loss_ref[...] += jnp.broadcast_to(part, loss_ref.shape)
            dg_ref[...] += dg

    row = pl.BlockSpec((tb, d), lambda i: (i, 0))
    vec = pl.BlockSpec((1, d), lambda i: (0, 0))
    return pl.pallas_call(
        body, name="loss_head",
        grid=(t // tb,),
        in_specs=[row, row, vec, row],
        out_specs=[pl.BlockSpec((1, LANES), lambda i: (0, 0)), row, row, vec],
        out_shape=[jax.ShapeDtypeStruct((1, LANES), F32), jax.ShapeDtypeStruct((t, d), F32),
                   jax.ShapeDtypeStruct((t, d), BF16), jax.ShapeDtypeStruct((1, d), F32)],
        compiler_params=_cparams(dimension_semantics=("arbitrary",)),
    )(x3, f, g2, target)


def _hg_mask(tb):
    br = lax.broadcasted_iota(jnp.int32, (tb, tb), 0)
    bs = lax.broadcasted_iota(jnp.int32, (tb, tb), 1)
    return jnp.logical_and(br // A_CHUNK == bs // A_CHUNK, bs <= br).astype(F32)


def _hg_consts(mask):
    c = A_CHUNK
    r = lax.broadcasted_iota(jnp.int32, (c, c), 0)
    s = lax.broadcasted_iota(jnp.int32, (c, c), 1)
    return (s <= r).astype(F32), (r <= s).astype(F32), mask > 0.5


def _chunk_apply(mat, x):
    c = mat.shape[0]
    return jnp.concatenate([_f32dot(mat, x[i * c:(i + 1) * c]) for i in range(x.shape[0] // c)], axis=0)


@jax.custom_vjp
def _chunk_cumsum(x, tri, tri_t):
    return _chunk_apply(tri, x)


_chunk_cumsum.defvjp(lambda x, tri, tri_t: (_chunk_apply(tri, x), (tri, tri_t)),
                     lambda r, g: (_chunk_apply(r[1], g), jnp.zeros_like(r[0]), jnp.zeros_like(r[1])))


def _per_chunk(a, b, dims):
    return jnp.stack([_bdot_raw(a[i], b[i], dims) for i in range(a.shape[0])])


@jax.custom_vjp
def _chunk_tn(a, b):
    return _per_chunk(a, b, _TN)


@jax.custom_vjp
def _chunk_nt(a, b):
    return _per_chunk(a, b, _NT)


@jax.custom_vjp
def _chunk_nn(a, b):
    return _per_chunk(a, b, _NN)


_chunk_tn.defvjp(lambda a, b: (_per_chunk(a, b, _TN), (a, b)),
                 lambda r, g: (_chunk_nt(r[1], g), _chunk_nn(r[0], g)))
_chunk_nt.defvjp(lambda a, b: (_per_chunk(a, b, _NT), (a, b)),
                 lambda r, g: (_chunk_nn(g, r[1]), _chunk_tn(g, r[0])))
_chunk_nn.defvjp(lambda a, b: (_per_chunk(a, b, _NN), (a, b)),
                 lambda r, g: (_chunk_nt(g, r[1]), _chunk_tn(r[0], g)))


def _scan_states(decay, m, st):
    sts = []
    for i in range(m.shape[0]):
        sts.append(st)
        st = st * decay[i] + m[i]
    return jnp.stack(sts), st


@jax.custom_vjp
def _state_scan(decay, m, st):
    return _scan_states(decay, m, st)


def _state_scan_fwd(decay, m, st):
    sts, st_out = _scan_states(decay, m, st)
    return (sts, st_out), (decay, sts)


def _state_scan_bwd(res, cts):
    decay, sts = res
    d_sts, g = cts
    d_decay, d_m = [], []
    for i in range(sts.shape[0] - 1, -1, -1):
        d_m.append(g)
        d_decay.append(jnp.sum(g * sts[i], axis=0, keepdims=True))
        g = g * decay[i] + d_sts[i]
    return jnp.stack(d_decay[::-1]), jnp.stack(d_m[::-1]), g


_state_scan.defvjp(_state_scan_fwd, _state_scan_bwd)


def _hg_block(qp, fp, ip, gp, lb, ng, st, tri, tri_t, bd_causal):
    tb = qp.shape[0]
    c = A_CHUNK
    n = tb // c
    q = _silu(qp)
    fg = lb + (1.0 - lb) * _sigmoid(fp)
    logf = jnp.log(fg)
    k = 1.0 - fg
    b3 = _chunk_cumsum(logf, tri, tri_t).reshape(n, c, HEAD)
    pos = lax.broadcasted_iota(jnp.int32, (1, c, 1), 1)
    b_mid = lax.stop_gradient(jnp.sum(jnp.where(pos == c // 2, b3, 0.0), axis=1, keepdims=True))
    b_last = jnp.sum(jnp.where(pos == c - 1, b3, 0.0), axis=1, keepdims=True)
    q3, k3, v3 = q.reshape(n, c, HEAD), k.reshape(n, c, HEAD), ip.reshape(n, c, HEAD)
    scores = _dot_nt((q3 * jnp.exp(b3 - b_mid)).reshape(tb, HEAD), (k3 * jnp.exp(b_mid - b3)).reshape(tb, HEAD))
    o_intra = _dot_nn(jnp.where(bd_causal, scores, 0.0), ip)
    states, st_new = _state_scan(jnp.exp(b_last), _chunk_tn(v3, k3 * jnp.exp(b_last - b3)), st)
    o = o_intra + _chunk_nt(q3 * jnp.exp(b3), states).reshape(tb, HEAD)
    y = _rms(o) * ng * _silu(gp)
    return y, st_new


HG_HEADS = 2


def _hg_specs(tb, nh, rev_nb=None):
    wide = HG_HEADS * HEAD
    per = nh // HG_HEADS

    def row(part):
        if rev_nb is None:
            return pl.BlockSpec((tb, wide), functools.partial(lambda h, i, off: (i, off + h), off=part * per))
        return pl.BlockSpec((tb, wide),
                            functools.partial(lambda h, i, off: (rev_nb - 1 - i, off + h), off=part * per))
    return [row(0), row(1), row(2), row(3),
            pl.BlockSpec((1, wide), lambda h, i: (0, h)), pl.BlockSpec((1, HEAD), lambda h, i: (0, 0)),
            pl.BlockSpec((tb, tb), lambda h, i: (0, 0))]


def _hgrn2_fwd(proj, lb, ng, tb):
    t = proj.shape[0]
    nh = proj.shape[1] // (4 * HEAD)
    tb = min(tb, t)
    nb = t // tb
    wide = HG_HEADS * HEAD

    def body(q_ref, f_ref, i_ref, g_ref, lb_ref, ng_ref, mask_ref, y_ref, s_ref, st_ref):
        i = pl.program_id(1)

        @pl.when(i == 0)
        def _():
            st_ref[...] = jnp.zeros_like(st_ref)

        consts = _hg_consts(mask_ref[...])
        for p in range(HG_HEADS):
            cs = slice(p * HEAD, (p + 1) * HEAD)
            st = st_ref[p]
            s_ref[p, 0] = st
            y, st_new = _hg_block(q_ref[:, cs], f_ref[:, cs], i_ref[:, cs], g_ref[:, cs], lb_ref[:, cs],
                                  ng_ref[...], st, *consts)
            y_ref[:, cs] = y.astype(y_ref.dtype)
            st_ref[p] = st_new

    return pl.pallas_call(
        body, name="hgrn2_fwd",
        grid=(nh // HG_HEADS, nb),
        in_specs=_hg_specs(tb, nh),
        out_specs=[pl.BlockSpec((tb, wide), lambda h, i: (i, h)),
                   pl.BlockSpec((HG_HEADS, 1, HEAD, HEAD), lambda h, i: (h, i, 0, 0))],
        out_shape=[jax.ShapeDtypeStruct((t, nh * HEAD), BF16),
                   jax.ShapeDtypeStruct((nh, nb, HEAD, HEAD), F32)],
        scratch_shapes=[pltpu.VMEM((HG_HEADS, HEAD, HEAD), F32)],
        compiler_params=_cparams(dimension_semantics=("parallel", "arbitrary")),
    )(proj, proj, proj, proj, lb, ng, _hg_mask(tb))


def _hgrn2_bwd(proj, lb, ng, states, dy, tb):
    t = proj.shape[0]
    nh = proj.shape[1] // (4 * HEAD)
    tb = min(tb, t)
    nb = t // tb
    wide = HG_HEADS * HEAD

    def body(q_ref, f_ref, i_ref, g_ref, lb_ref, ng_ref, mask_ref, s_ref, dy_ref,
             dp_ref, dlb_ref, dng_ref, dst_ref):
        h, i = pl.program_id(0), pl.program_id(1)
        consts = _hg_consts(mask_ref[...])

        @pl.when(i == 0)
        def _():
            dst_ref[...] = jnp.zeros_like(dst_ref)
            dlb_ref[...] = jnp.zeros_like(dlb_ref)

        @pl.when(jnp.logical_and(i == 0, h == 0))
        def _():
            dng_ref[...] = jnp.zeros_like(dng_ref)

        def fn(qp, fp, ip, gp, lbx, ngx, stx):
            return _hg_block(qp, fp, ip, gp, lbx, ngx, stx, *consts)

        for p in range(HG_HEADS):
            cs = slice(p * HEAD, (p + 1) * HEAD)
            _, vjp_fn = jax.vjp(fn, q_ref[:, cs], f_ref[:, cs], i_ref[:, cs], g_ref[:, cs], lb_ref[:, cs],
                                ng_ref[...], s_ref[p, 0])
            *gparts, glb, gng, dst = vjp_fn((dy_ref[:, cs].astype(F32), dst_ref[p]))
            for part, gpart in enumerate(gparts):
                dp_ref[part, :, cs] = gpart.astype(dp_ref.dtype)
            dst_ref[p] = dst
            dlb_ref[:, cs] += glb
            dng_ref[...] += gng

    rev = lambda h, i: (nb - 1 - i, h)
    return pl.pallas_call(
        body, name="hgrn2_bwd",
        grid=(nh // HG_HEADS, nb),
        in_specs=_hg_specs(tb, nh, rev_nb=nb) + [
            pl.BlockSpec((HG_HEADS, 1, HEAD, HEAD), lambda h, i: (h, nb - 1 - i, 0, 0)),
            pl.BlockSpec((tb, wide), rev)],
        out_specs=[pl.BlockSpec((4, tb, wide), lambda h, i: (0, nb - 1 - i, h)),
                   pl.BlockSpec((1, wide), lambda h, i: (0, h)), pl.BlockSpec((1, HEAD), lambda h, i: (0, 0))],
        out_shape=[jax.ShapeDtypeStruct((4, t, nh * HEAD), BF16),
                   jax.ShapeDtypeStruct((1, nh * HEAD), F32), jax.ShapeDtypeStruct((1, HEAD), F32)],
        scratch_shapes=[pltpu.VMEM((HG_HEADS, HEAD, HEAD), F32)],
        compiler_params=_cparams(dimension_semantics=("arbitrary", "arbitrary")),
    )(proj, proj, proj, proj, lb, ng, _hg_mask(tb), states, dy)


def _fgate_consts(cb):
    r = lax.broadcasted_iota(jnp.int32, (cb, cb), 0)
    s = lax.broadcasted_iota(jnp.int32, (cb, cb), 1)
    return (r <= s).astype(F32), (r >= s).astype(F32)


def _fgate_fwd(xt, bias, cb=512):
    nh, t = xt.shape
    cb = min(cb, t)

    def body(x_ref, b_ref, o_ref):
        upper, _ = _fgate_consts(cb)
        carry = jnp.zeros((nh, 1), F32)
        for blk in range(t // cb):
            z = x_ref[:, blk * cb:(blk + 1) * cb] + b_ref[...]
            logf = jnp.minimum(z, 0.0) - jnp.log(1.0 + jnp.exp(-jnp.abs(z)))
            cs = _f32dot(logf, upper) + carry
            o_ref[:, blk * cb:(blk + 1) * cb] = cs
            carry = cs[:, cb - 1:cb]

    vm = pl.BlockSpec(memory_space=pltpu.VMEM)
    return pl.pallas_call(
        body, name="fgate_fwd", in_specs=[vm, vm], out_specs=vm,
        out_shape=jax.ShapeDtypeStruct((nh, t), F32), compiler_params=_cparams(),
    )(xt, bias)


def _fgate_bwd(xt, bias, dft, cb=512):
    nh, t = xt.shape
    cb = min(cb, t)
    nblk = t // cb

    def body(x_ref, b_ref, d_ref, dx_ref, db_ref):
        _, lower = _fgate_consts(cb)
        carry = jnp.zeros((nh, 1), F32)
        db = jnp.zeros((nh, 1), F32)
        for blk in range(nblk - 1, -1, -1):
            sl = slice(blk * cb, (blk + 1) * cb)
            dlogf = _f32dot(d_ref[:, sl], lower) + carry
            carry = dlogf[:, 0:1]
            z = x_ref[:, sl] + b_ref[...]
            dz = dlogf * (1.0 - _sigmoid(z))
            dx_ref[:, sl] = dz
            db = db + jnp.sum(dz, axis=1, keepdims=True)
        db_ref[...] = db

    vm = pl.BlockSpec(memory_space=pltpu.VMEM)
    return pl.pallas_call(
        body, name="fgate_bwd", in_specs=[vm, vm, vm], out_specs=[vm, vm],
        out_shape=[jax.ShapeDtypeStruct((nh, t), F32), jax.ShapeDtypeStruct((nh, 1), F32)],
        compiler_params=_cparams(),
    )(xt, bias, dft)


ATTN_GROUPS = 4
ATTN_FWD_HEADS = 2


def _attn_fwd(q, k, v, f_grp, blk):
    t, width = v.shape
    nh = width // HEAD
    nq = t // blk
    hpg = nh // ATTN_GROUPS

    def body(q_ref, k_ref, v_ref, fc_ref, o_ref, lse_ref):
        i = pl.program_id(0)
        tri = (lax.broadcasted_iota(jnp.int32, (blk, blk), 1) <= lax.broadcasted_iota(jnp.int32, (blk, blk), 0))
        for h0 in range(0, nh, ATTN_FWD_HEADS):
            heads = range(h0, min(h0 + ATTN_FWD_HEADS, nh))

            def tile(j, carries, masked):
                rs = pl.ds(pl.multiple_of(j * blk, blk), blk)
                out = []
                for h, (m, l, acc) in zip(heads, carries):
                    cs = slice(h * HEAD, (h + 1) * HEAD)
                    cs2 = slice(2 * h * HEAD, 2 * (h + 1) * HEAD)
                    s = _bdot_raw(q_ref[:, cs2], k_ref[rs, cs2], _NT)
                    if masked:
                        s = jnp.where(tri, s, NEG_INF)
                    m_new = jnp.maximum(m, jnp.max(s, axis=1, keepdims=True))
                    p = jnp.exp(s - m_new)
                    alpha = jnp.exp(m - m_new)
                    l_new = alpha * l + jnp.sum(p, axis=1, keepdims=True)
                    out.append((m_new, l_new, alpha * acc + _bdot_raw(p, v_ref[rs, cs], _NN)))
                return tuple(out)

            init = tuple((jnp.full((blk, 1), NEG_INF, F32), jnp.zeros((blk, 1), F32), jnp.zeros((blk, HEAD), F32))
                         for _ in heads)
            carries = lax.fori_loop(0, i, lambda j, c: tile(j, c, False), init)
            for h, (m, l, acc) in zip(heads, tile(i, carries, True)):
                o_ref[:, h * HEAD:(h + 1) * HEAD] = acc / l
                g, hh = divmod(h, hpg)
                lse_ref[g, :, hh:hh + 1] = m + jnp.log(l) + fc_ref[g, :, hh:hh + 1]

    vm = pl.BlockSpec(memory_space=pltpu.VMEM)
    stat = pl.BlockSpec((ATTN_GROUPS, blk, hpg), lambda i: (0, i, 0))
    return pl.pallas_call(
        body, name="fox_attn_fwd",
        grid=(nq,),
        in_specs=[pl.BlockSpec((blk, 2 * width), lambda i: (i, 0)), vm, vm, stat],
        out_specs=[pl.BlockSpec((blk, width), lambda i: (i, 0)), stat],
        out_shape=[jax.ShapeDtypeStruct((t, width), F32), jax.ShapeDtypeStruct((ATTN_GROUPS, t, hpg), F32)],
        compiler_params=_cparams(dimension_semantics=("parallel",)),
    )(q, k, v, f_grp)


def _attn_delta(do, o, tb):
    t, width = o.shape
    nh = width // HEAD
    hpg = nh // ATTN_GROUPS
    tb = min(tb, t)

    def body(do_ref, o_ref, dl_ref):
        for h in range(nh):
            cs = slice(h * HEAD, (h + 1) * HEAD)
            g, hh = divmod(h, hpg)
            dl_ref[g, :, hh:hh + 1] = jnp.sum(do_ref[:, cs].astype(F32) * o_ref[:, cs], axis=1, keepdims=True)

    wide = pl.BlockSpec((tb, width), lambda i: (i, 0))
    return pl.pallas_call(body, name="fox_attn_delta", grid=(t // tb,), in_specs=[wide, wide],
                          out_specs=pl.BlockSpec((ATTN_GROUPS, tb, hpg), lambda i: (0, i, 0)),
                          out_shape=jax.ShapeDtypeStruct((ATTN_GROUPS, t, hpg), F32),
                          compiler_params=_cparams(dimension_semantics=("parallel",)))(do, o)


def _attn_bwd(q, k, v, f_grp, do, lse, delta, blk):
    t, width = v.shape
    nh = width // HEAD
    nq = t // blk
    hpg = nh // ATTN_GROUPS
    gw = hpg * HEAD

    def body(q_ref, do_ref, k_ref, v_ref, fc_ref, lse_ref, dl_ref,
             dq_ref, dk_ref, dv_ref, dfc_ref, dfr_ref):
        g, j = pl.program_id(0), pl.program_id(1)
        tri = (lax.broadcasted_iota(jnp.int32, (blk, blk), 1) <= lax.broadcasted_iota(jnp.int32, (blk, blk), 0))

        @pl.when(j == 0)
        def _():
            dq_ref[...] = jnp.zeros_like(dq_ref)
            dfc_ref[...] = jnp.zeros_like(dfc_ref)

        def tile(i, carries, masked):
            rs = pl.ds(pl.multiple_of(i * blk, blk), blk)
            out = []
            for h, (dk, dv, dfs) in enumerate(carries):
                cs = slice(h * HEAD, (h + 1) * HEAD)
                cs2 = slice(2 * h * HEAD, 2 * (h + 1) * HEAD)
                csq = slice(2 * h * HEAD, (2 * h + 1) * HEAD)
                qi = q_ref[rs, csq]
                doi = do_ref[rs, cs]
                bias = fc_ref[0, rs, h:h + 1] - lse_ref[0, rs, h:h + 1]
                p = jnp.exp(_bdot_raw(q_ref[rs, cs2], k_ref[:, cs2], _NT) + bias)
                if masked:
                    p = jnp.where(tri, p, 0.0)
                ds = p * (_bdot_raw(doi, v_ref[:, cs], _NT) - dl_ref[0, rs, h:h + 1])
                dsb = ds.astype(BF16)
                dq_ref[rs, cs] += _bdot_raw(dsb, k_ref[:, csq], _NN)
                dfc_ref[0, rs, h:h + 1] += jnp.sum(ds, axis=1, keepdims=True)
                out.append((dk + _bdot_raw(dsb, qi, _TN), dv + _bdot_raw(p, doi, _TN),
                            dfs - jnp.sum(ds, axis=0, keepdims=True)))
            return tuple(out)

        init = tuple((jnp.zeros((blk, HEAD), F32), jnp.zeros((blk, HEAD), F32), jnp.zeros((1, blk), F32))
                     for _ in range(hpg))
        carries = lax.fori_loop(j + 1, nq, lambda i, c: tile(i, c, False), tile(j, init, True))
        for h, (dk, dv, dfs) in enumerate(carries):
            cs = slice(h * HEAD, (h + 1) * HEAD)
            dk_ref[:, cs] = dk
            dv_ref[:, cs] = dv.astype(dv_ref.dtype)
            dfr_ref[0, 0, h:h + 1, :] = dfs

    once = pl.Buffered(1)
    stat = pl.BlockSpec((1, t, hpg), lambda g, j: (g, 0, 0), pipeline_mode=once)
    kv_blk = pl.BlockSpec((blk, gw), lambda g, j: (j, g))
    frow = pl.BlockSpec((1, 1, hpg, blk), lambda g, j: (g, j, 0, 0))
    dq, dk, dv, dfc, dfr = pl.pallas_call(
        body, name="fox_attn_bwd",
        grid=(ATTN_GROUPS, nq),
        in_specs=[pl.BlockSpec((t, 2 * gw), lambda g, j: (0, g), pipeline_mode=once),
                  pl.BlockSpec((t, gw), lambda g, j: (0, g), pipeline_mode=once),
                  pl.BlockSpec((blk, 2 * gw), lambda g, j: (j, g)), kv_blk, stat, stat, stat],
        out_specs=[pl.BlockSpec((t, gw), lambda g, j: (0, g)), kv_blk, kv_blk,
                   pl.BlockSpec((1, t, hpg), lambda g, j: (g, 0, 0)), frow],
        out_shape=[jax.ShapeDtypeStruct((t, width), F32), jax.ShapeDtypeStruct((t, width), F32),
                   jax.ShapeDtypeStruct((t, width), BF16), jax.ShapeDtypeStruct((ATTN_GROUPS, t, hpg), F32),
                   jax.ShapeDtypeStruct((ATTN_GROUPS, nq, hpg, blk), F32)],
        compiler_params=_cparams(dimension_semantics=("parallel", "arbitrary")),
    )(q, do, k, v, f_grp, lse, delta)
    return dq, dk, dv, dfc, dfr


SUBLANES = 8


def _shift_down(u, n):
    r = pltpu.roll(u, n, 0)
    row = lax.broadcasted_iota(jnp.int32, (SUBLANES, u.shape[1]), 0)
    return jnp.concatenate([jnp.where(row < n, 0.0, r[:SUBLANES]), r[SUBLANES:]], axis=0)


def _shift_up(u, n):
    t = u.shape[0]
    r = pltpu.roll(u, t - n, 0)
    row = lax.broadcasted_iota(jnp.int32, (SUBLANES, u.shape[1]), 0)
    return jnp.concatenate([r[:t - SUBLANES], jnp.where(row >= SUBLANES - n, 0.0, r[t - SUBLANES:])], axis=0)


def _convglu_specs(t):
    return [pl.BlockSpec((2, t, LANES), lambda j: (0, 0, j)),
            pl.BlockSpec((2, CONV_TAPS, LANES), lambda j: (0, 0, j)),
            pl.BlockSpec((2, 1, LANES), lambda j: (0, 0, j))]


def _convglu_fwd(u, cw, cb):
    _, t, fp = u.shape

    def body(u_ref, w_ref, b_ref, a_ref, c_ref):
        c = []
        for hf in range(2):
            uv, w = u_ref[hf].astype(F32), w_ref[hf]
            c.append(w[0:1] * _shift_down(uv, 2) + w[1:2] * _shift_down(uv, 1) + w[2:3] * uv + b_ref[hf])
            c_ref[hf] = c[hf].astype(c_ref.dtype)
        a_ref[...] = (_silu(c[0]) * c[1]).astype(a_ref.dtype)

    return pl.pallas_call(
        body, name="convglu_fwd",
        grid=(fp // LANES,),
        in_specs=_convglu_specs(t),
        out_specs=[pl.BlockSpec((t, LANES), lambda j: (0, j)), pl.BlockSpec((2, t, LANES), lambda j: (0, 0, j))],
        out_shape=[jax.ShapeDtypeStruct((t, fp), BF16), jax.ShapeDtypeStruct((2, t, fp), BF16)],
        compiler_params=_cparams(dimension_semantics=("parallel",)),
    )(u, cw, cb)


def _convglu_bwd(u, c, cw, da):
    _, t, fp = u.shape

    def body(u_ref, c_ref, w_ref, da_ref, du_ref, dw_ref, db_ref):
        gc, vc = c_ref[0].astype(F32), c_ref[1].astype(F32)
        sg = _sigmoid(gc)
        dav = da_ref[...].astype(F32)
        dcs = [dav * vc * (sg * (1.0 + gc * (1.0 - sg))), dav * (gc * sg)]
        for hf in range(2):
            dc, w, uv = dcs[hf], w_ref[hf], u_ref[hf].astype(F32)
            dc1, dc2 = _shift_up(dc, 1), _shift_up(dc, 2)
            du_ref[hf] = (w[2:3] * dc + w[1:2] * dc1 + w[0:1] * dc2).astype(du_ref.dtype)
            dw_ref[hf, 0:1, :] = jnp.sum(dc2 * uv, axis=0, keepdims=True)
            dw_ref[hf, 1:2, :] = jnp.sum(dc1 * uv, axis=0, keepdims=True)
            dw_ref[hf, 2:3, :] = jnp.sum(dc * uv, axis=0, keepdims=True)
            db_ref[hf] = jnp.sum(dc, axis=0, keepdims=True)

    pair, taps, bias = _convglu_specs(t)
    return pl.pallas_call(
        body, name="convglu_bwd",
        grid=(fp // LANES,),
        in_specs=[pair, pair, taps, pl.BlockSpec((t, LANES), lambda j: (0, j))],
        out_specs=[pair, taps, bias],
        out_shape=[jax.ShapeDtypeStruct((2, t, fp), BF16), jax.ShapeDtypeStruct((2, CONV_TAPS, fp), F32),
                   jax.ShapeDtypeStruct((2, 1, fp), F32)],
        compiler_params=_cparams(dimension_semantics=("parallel",)),
    )(u, c, cw, da)


def _local_step(x, target, mods, lb, small, pre_w, get_w, put_g, *, tb=512, attn_blk=512):
    t, d = x.shape
    nh = d // HEAD
    nb = NDEV
    wts = {}
    vec = lambda *names: [mods[n] for n in names]

    def ffn_fwd(h2, l):
        u = _mm_wblk(h2, wts[f"up{l}"], BF16, f"ffn{l}_up", gb=nb // 2, split=2, tm=512)
        a, c = _convglu_fwd(u, small[f"conv_w{l}"], small[f"conv_b{l}"])
        f = _mm(a, wts[f"down{l}"], "nn", F32, f"ffn{l}_down", tk=4096)
        return (u, c), a, f

    def ffn_bwd(df, h2, uc, a, l):
        u, c = uc
        da = _mm(df, wts[f"down{l}"], "nt", BF16, f"ffn{l}_down_dx", tn=1536)
        dwd = _mm(a, df, "tn", BF16, f"ffn{l}_down_dw", tm=768, tk=t)
        du, dcw, dcb = _convglu_bwd(u, c, small[f"conv_w{l}"], da)
        dh2 = _mm_wblk_dx(du, wts[f"up{l}"], BF16, f"ffn{l}_up_dx", k=d, gb=nb // 2, split=2, tm=1024)
        dwu = _mm_wblk_dw(h2, du, f"ffn{l}_up_dw", nb=nb, gb=1, split=2, tk=t)
        return dh2, dwu, dwd, dcw, dcb

    (h_a,) = _row_fwd(_f_mod, [(x, d, 0)], vec("sh1_0", "sc1_0"), [BF16], tb=tb, name="l0_mod1")
    wts.update(get_w("l0a", h_a))
    proj_a = _mm_wblk(h_a, wts["a_in"], F32, "a_in", gb=nb // 2)
    ypre, states = _hgrn2_fwd(proj_a, lb, small["a_norm_g"], tb)
    pre_w("l0b", ypre)
    wts.update(get_w("l0b", ypre))
    y_a = _mm(ypre, wts["a_out"], "nn", F32, "a_out")
    x1, h2_0 = _row_fwd(_f_res_mod, [(x, d, 0), (y_a, d, 0)], vec("g1_0", "sh2_0", "sc2_0"), [F32, BF16],
                        tb=tb, name="l0_res_mod2")
    u0, a0, f0 = ffn_fwd(h2_0, 0)
    x2, h_kv, h_q = _row_fwd(_f_res_mod2, [(x1, d, 0), (f0, d, 0)],
                             [mods["g2_0"] + pre_w("l1", f0)] + vec("kv_sh", "kv_sc", "sh1_1", "sc1_1"),
                             [F32, BF16, BF16], tb=tb, name="l0_res_kvmod_qmod")
    wts.update(get_w("l1", h_kv))
    proj_k = _mm(h_kv, wts["kv_k"], "nt", F32, "k_proj")
    v_b = _mm(h_kv, wts["kv_v"], "nt", BF16, "v_proj")
    proj_f = _mm(h_kv, wts["kv_f"], "nt", F32, "kv_fproj")
    f_logit_t = proj_f[:, :nh].T
    f_bias = small["kv_b_f"].reshape(nh, 1)
    f_t = _fgate_fwd(f_logit_t, f_bias)
    f_grp = f_t.reshape(ATTN_GROUPS, nh // ATTN_GROUPS, t).transpose(0, 2, 1)
    (k_n,) = _row_fwd(_f_knorm_aug, [(proj_k, HEAD, 0)] + [(piece, 1, 0) for piece in _split3(-f_t.T)],
                      [small["k_norm_g"]], [BF16], nsub=nh, tb=tb, name="k_norm")
    proj_q = _mm_wblk(h_q, wts["b_q"], F32, "b_q", gb=nb)
    (q_n,) = _row_fwd(_f_qnorm_aug, [(proj_q, HEAD, 0)], [small["q_norm_g"]], [BF16], nsub=nh, tb=tb,
                      name="q_norm")
    o_att, lse = _attn_fwd(q_n, k_n, v_b, f_grp, attn_blk)
    (z,) = _row_fwd(_f_outgate, [(o_att, HEAD, 0), (proj_q, HEAD, 1)], [], [BF16], nsub=nh, tb=tb, name="out_gate")
    y_b = _mm(z, wts["b_out"], "nn", F32, "b_out")
    x3, h2_1 = _row_fwd(_f_res_mod, [(x2, d, 0), (y_b, d, 0)], vec("g1_1", "sh2_1", "sc2_1"), [F32, BF16],
                        tb=tb, name="l1_res_mod2")
    u1, a1, f1 = ffn_fwd(h2_1, 1)
    loss, dx4, df1, dg2_1 = _loss_call(x3, f1, mods["g2_1"], target, tb)

    g = {}
    dmods = {"g2_1": dg2_1}
    dh2, g["up1"], g["down1"], g["conv_w1"], g["conv_b1"] = ffn_bwd(df1, h2_1, u1, a1, 1)
    (dx2, dy_b), (dmods["g1_1"], dmods["sh2_1"], dmods["sc2_1"]) = _row_bwd(
        _f_res_mod, [(x2, d, 0), (y_b, d, 0)], vec("g1_1", "sh2_1", "sc2_1"),
        [(dx4, d, 0), (dh2, d, 0)], [F32, BF16], tb=tb, name="l1_res_mod2_bwd")
    dz = _mm(dy_b, wts["b_out"], "nt", BF16, "b_out_dx")
    g["b_out"] = _mm(z, dy_b, "tn", BF16, "b_out_dw", tk=t)
    (do_att, dog), _ = _row_bwd(_f_outgate, [(o_att, HEAD, 0), (proj_q, HEAD, 1)], [], [(dz, HEAD, 0)],
                                [BF16, BF16], nsub=nh, tb=tb, name="out_gate_bwd")
    delta = _attn_delta(do_att, o_att, tb)
    dq_n, dk_n, dv, dfc_q, dfr_k = _attn_bwd(q_n, k_n, v_b, f_grp, do_att, lse, delta, attn_blk)
    (dpq,), (g["q_norm_g"],) = _row_bwd(_f_qnorm, [(proj_q, HEAD, 0)], [small["q_norm_g"]],
                                        [(dq_n, HEAD, 0)], [BF16], nsub=nh, tb=tb, name="q_norm_bwd")
    dproj_q = jnp.concatenate([dpq, dog], axis=1)
    dh_q = _mm_wblk_dx(dproj_q, wts["b_q"], BF16, "b_q_dx", k=d, gb=nb)
    g["b_q"] = _mm_wblk_dw(h_q, dproj_q, "b_q_dw", nb=nb, gb=nb // 4, tk=t)
    (dpk,), (g["k_norm_g"],) = _row_bwd(_f_knorm, [(proj_k, HEAD, 0)], [small["k_norm_g"]],
                                        [(dk_n, HEAD, 0)], [BF16], nsub=nh, tb=tb, name="k_norm_bwd")
    df_t = dfc_q.transpose(0, 2, 1).reshape(nh, t) + dfr_k.transpose(0, 2, 1, 3).reshape(nh, t)
    dflogit_t, g["kv_b_f"] = _fgate_bwd(f_logit_t, f_bias, df_t)
    dproj_f = jnp.pad(dflogit_t.T, ((0, 0), (0, LANES - nh))).astype(BF16)
    dh_kv = _mm(dpk, wts["kv_k"], "nn", BF16, "k_proj_dx")
    dh_kv_v = _mm(dv, wts["kv_v"], "nn", BF16, "v_proj_dx")
    dh_kv_f = _mm(dproj_f, wts["kv_f"], "nn", BF16, "kv_fproj_dx")
    g["kv_k"] = _mm(dpk, h_kv, "tn", BF16, "k_proj_dw", tk=t)
    g["kv_v"] = _mm(dv, h_kv, "tn", BF16, "v_proj_dw", tk=t)
    g["kv_f"] = _mm(dproj_f, h_kv, "tn", F32, "kv_fproj_dw", tk=1024)
    sent = put_g("l1", {n: g.pop(n) for n in ("b_out", "b_q", "kv_k", "kv_v", "kv_f", "up1", "down1")})
    (dx1, df0), (dmods["g2_0"], dmods["kv_sh"], dmods["kv_sc"], dmods["sh1_1"], dmods["sc1_1"]) = _row_bwd(
        _f_res_mod2, [(x1, d, 0), (f0, d, 0)], [mods["g2_0"] + sent] + vec("kv_sh", "kv_sc", "sh1_1", "sc1_1"),
        [(dx2, d, 0), (dh_kv, d, 0), (dh_q, d, 0)], [F32, BF16], tb=tb, name="l0_res_kvmod_qmod_bwd",
        cot_add=[(1, dh_kv_v), (1, dh_kv_f)])
    dh2, g["up0"], g["down0"], g["conv_w0"], g["conv_b0"] = ffn_bwd(df0, h2_0, u0, a0, 0)
    (dx0, dy_a), (dmods["g1_0"], dmods["sh2_0"], dmods["sc2_0"]) = _row_bwd(
        _f_res_mod, [(x, d, 0), (y_a, d, 0)], vec("g1_0", "sh2_0", "sc2_0"),
        [(dx1, d, 0), (dh2, d, 0)], [F32, BF16], tb=tb, name="l0_res_mod2_bwd")
    dypre = _mm(dy_a, wts["a_out"], "nt", BF16, "a_out_dx")
    g["a_out"] = _mm(ypre, dy_a, "tn", BF16, "a_out_dw", tk=t)
    sent = put_g("l0b", {n: g.pop(n) for n in ("a_out", "up0", "down0")})
    dproj_a, dlb, g["a_norm_g"] = _hgrn2_bwd(proj_a, lb + sent, small["a_norm_g"], states, dypre, tb)
    dh_a = _mm_wblk_dx(dproj_a, wts["a_in"], BF16, "a_in_dx", k=d, gb=nb, split=4, tm=512)
    put_g("l0a", {"a_in": _mm_wblk_dw(h_a, dproj_a, "a_in_dw", nb=nb, gb=1, split=4, tk=t)})
    (grad_x,), (dmods["sh1_0"], dmods["sc1_0"]) = _row_bwd(
        _f_mod, [(x, d, 0)], vec("sh1_0", "sc1_0"), [(dh_a, d, 0)], [F32], tb=tb, name="l0_mod1_bwd",
        add_to=(0, dx0))
    return loss, grad_x, dmods, dlb, g


def _position():
    return lax.axis_index("x"), lax.axis_index("y"), lax.axis_index("c")


def _hbm_specs(n):
    return [pl.BlockSpec(memory_space=pl.ANY)] * n


def _all_gather(arrs, name):
    n = len(arrs)

    def body(*refs):
        x_refs, out_refs = refs[:n], refs[n:2 * n]
        send_sems, recv_sems, local_sems = refs[2 * n:]
        x, y, cc = _position()
        me, sibling = (x, y, cc), (x, y, 1 - cc)
        chips = [(1 - x, y), (x, 1 - y), (1 - x, 1 - y)]

        def copy(a, k, block, to, src=None):
            slot = out_refs[a].at[4 * block[0] + 2 * block[1] + block[2]]
            return pltpu.make_async_remote_copy(
                src_ref=slot if src is None else src, dst_ref=slot,
                send_sem=send_sems.at[7 * a + k], recv_sem=recv_sems.at[7 * a + k],
                device_id=to, device_id_type=_MESH)

        local = [pltpu.make_async_copy(x_refs[a], out_refs[a].at[4 * x + 2 * y + cc], local_sems.at[a])
                 for a in range(n)]
        for cp in local:
            cp.start()
        first = []
        for a in range(n):
            first.append(copy(a, 0, me, sibling, src=x_refs[a]))
            first += [copy(a, 1 + j, me, (*chip, cc), src=x_refs[a]) for j, chip in enumerate(chips)]
        for cp in first:
            cp.start()
        passed = []
        for j, chip in enumerate(chips):
            for a in range(n):
                copy(a, 1 + j, (*chip, cc), me).wait_recv()
                fwd = copy(a, 4 + j, (*chip, cc), sibling)
                fwd.start()
                passed.append(fwd)
        for a in range(n):
            copy(a, 0, sibling, me).wait_recv()
        for j, chip in enumerate(chips):
            for a in range(n):
                copy(a, 4 + j, (*chip, 1 - cc), me).wait_recv()
        for cp in first + passed:
            cp.wait_send()
        for cp in local:
            cp.wait()

    return pl.pallas_call(
        body, name=name,
        out_shape=[jax.ShapeDtypeStruct((NDEV, *a.shape), a.dtype) for a in arrs],
        in_specs=_hbm_specs(n), out_specs=_hbm_specs(n),
        scratch_shapes=[pltpu.SemaphoreType.DMA((7 * n,)), pltpu.SemaphoreType.DMA((7 * n,)),
                        pltpu.SemaphoreType.DMA((n,))],
    )(*arrs)


_XCHG_EFFECT = pltpu.SideEffectType.DATAFLOW_SIDE_EFFECTING
ALL_PEERS = (1, 2, 3, 4, 5, 6, 7)
SAME_CORE = (2, 4, 6)


def _xchg_copies(src_refs, land_refs, send_sems, recv_sems, local_sems, scatter, rels):
    x, y, cc = _position()
    me = 4 * x + 2 * y + cc
    remote, local = [], []
    for a, (src, land) in enumerate(zip(src_refs, land_refs)):
        local.append(pltpu.make_async_copy(src.at[me] if scatter else src, land.at[me], local_sems.at[a]))
        for idx, rel in enumerate(rels):
            px = 1 - x if rel & 4 else x
            py = 1 - y if rel & 2 else y
            pc = 1 - cc if rel & 1 else cc
            k = len(rels) * a + idx
            remote.append(pltpu.make_async_remote_copy(
                src_ref=src.at[4 * px + 2 * py + pc] if scatter else src, dst_ref=land.at[me],
                send_sem=send_sems.at[k], recv_sem=recv_sems.at[k], device_id=(px, py, pc), device_id_type=_MESH))
    return remote, local


def _xchg_start(srcs, scatter, rels, after, name):
    n = len(srcs)
    lands = [lax.empty(s.shape if scatter else (NDEV, *s.shape), s.dtype) for s in srcs]

    def body(*refs):
        remote, local = _xchg_copies(refs[:n], refs[n:2 * n], *refs[2 * n + 1:2 * n + 4], scatter, rels)
        for cp in local + remote:
            cp.start()
        token = refs[-1]
        token[...] = jnp.zeros_like(token)

    hbm = pl.BlockSpec(memory_space=pltpu.HBM)
    sem = pl.BlockSpec(memory_space=pltpu.SEMAPHORE)
    out = pl.pallas_call(
        body, name=name,
        out_shape=(pltpu.SemaphoreType.DMA((len(rels) * n,)), pltpu.SemaphoreType.DMA((len(rels) * n,)),
                   pltpu.SemaphoreType.DMA((n,)),
                   *[pltpu.HBM(a.shape, a.dtype) for a in srcs + lands], jax.ShapeDtypeStruct((8, LANES), F32)),
        in_specs=[hbm] * (2 * n) + [pl.BlockSpec(memory_space=pl.ANY)],
        out_specs=(sem, sem, sem, *[hbm] * (2 * n), pl.BlockSpec(memory_space=pltpu.VMEM)),
        input_output_aliases={i: 3 + i for i in range(2 * n)},
        compiler_params=pltpu.CompilerParams(has_side_effects=_XCHG_EFFECT),
    )(*[pltpu.with_memory_space_constraint(a, pltpu.HBM) for a in srcs + lands], after)
    return out[:-1], out[-1][0, 0]


def _xchg_wait(handles, after, scatter, rels, name):
    n = (len(handles) - 3) // 2

    def body(*refs):
        remote, local = _xchg_copies(refs[:n], refs[n:2 * n], *refs[2 * n:2 * n + 3], scatter, rels)
        for cp in remote:
            cp.wait_send()
            cp.wait_recv()
        for cp in local:
            cp.wait()

    hbm = pl.BlockSpec(memory_space=pltpu.HBM)
    sem = pl.BlockSpec(memory_space=pltpu.SEMAPHORE)
    thru = list(handles[3:])
    out = pl.pallas_call(
        body, name=name,
        out_shape=tuple(pltpu.HBM(a.shape, a.dtype) for a in thru),
        in_specs=[hbm] * (2 * n) + [sem, sem, sem, pl.BlockSpec(memory_space=pl.ANY)],
        out_specs=tuple([hbm] * (2 * n)),
        input_output_aliases={i: i for i in range(2 * n)},
        compiler_params=pltpu.CompilerParams(has_side_effects=_XCHG_EFFECT),
    )(*thru, *handles[:3], after)
    return list(out[n:])


def _sibling_copies(land_refs, send_sems, recv_sems):
    x, y, cc = _position()

    def copy(a, q, core):
        slot = land_refs[a].at[2 * q + core]
        return pltpu.make_async_remote_copy(
            src_ref=slot, dst_ref=slot, send_sem=send_sems.at[NCHIP * a + q], recv_sem=recv_sems.at[NCHIP * a + q],
            device_id=(x, y, 1 - cc), device_id_type=_MESH)

    pairs = [(a, q) for a in range(len(land_refs)) for q in range(NCHIP)]
    return [copy(a, q, cc) for a, q in pairs], [copy(a, q, 1 - cc) for a, q in pairs]


def _sibling_forward_start(lands, name):
    n = len(lands)

    def body(*refs):
        sends, _ = _sibling_copies(refs[:n], refs[n], refs[n + 1])
        for cp in sends:
            cp.start()
        refs[-1][...] = jnp.zeros_like(refs[-1])

    hbm = pl.BlockSpec(memory_space=pltpu.HBM)
    sem = pl.BlockSpec(memory_space=pltpu.SEMAPHORE)
    out = pl.pallas_call(
        body, name=name,
        out_shape=(pltpu.SemaphoreType.DMA((NCHIP * n,)), pltpu.SemaphoreType.DMA((NCHIP * n,)),
                   *[pltpu.HBM(a.shape, a.dtype) for a in lands], jax.ShapeDtypeStruct((8, LANES), F32)),
        in_specs=[hbm] * n,
        out_specs=(sem, sem, *[hbm] * n, pl.BlockSpec(memory_space=pltpu.VMEM)),
        input_output_aliases={i: 2 + i for i in range(n)},
        compiler_params=pltpu.CompilerParams(has_side_effects=_XCHG_EFFECT),
    )(*lands)
    return out[:-1], out[-1][0, 0]


def _sibling_forward_wait(handles, after, name):
    n = len(handles) - 2

    def body(*refs):
        sends, arrivals = _sibling_copies(refs[:n], refs[n], refs[n + 1])
        for cp in sends:
            cp.wait_send()
        for cp in arrivals:
            cp.wait_recv()

    hbm = pl.BlockSpec(memory_space=pltpu.HBM)
    sem = pl.BlockSpec(memory_space=pltpu.SEMAPHORE)
    lands = list(handles[2:])
    return list(pl.pallas_call(
        body, name=name,
        out_shape=tuple(pltpu.HBM(a.shape, a.dtype) for a in lands),
        in_specs=[hbm] * n + [sem, sem, pl.BlockSpec(memory_space=pl.ANY)],
        out_specs=tuple([hbm] * n),
        input_output_aliases={i: i for i in range(n)},
        compiler_params=pltpu.CompilerParams(has_side_effects=_XCHG_EFFECT),
    )(*lands, *handles[:2], after))


def _slab_sum(slabs, name, tr=None):
    n, r, c = slabs.shape
    tr = r if tr is None else tr

    def body(s_ref, o_ref):
        acc = s_ref[0].astype(F32)
        for q in range(1, n):
            acc = acc + s_ref[q].astype(F32)
        o_ref[...] = acc

    return pl.pallas_call(body, name=name, grid=(r // tr,),
                          in_specs=[pl.BlockSpec((n, tr, c), lambda i: (0, i, 0))],
                          out_specs=pl.BlockSpec((tr, c), lambda i: (i, 0)),
                          out_shape=jax.ShapeDtypeStruct((r, c), F32),
                          compiler_params=_cparams(dimension_semantics=("parallel",)))(slabs)


def _ada_fwd(c_all, ada_w, kv_ada_w, logits):
    rows, d = c_all.shape
    n0, nkv = ada_w.shape[2], kv_ada_w.shape[1]

    def body(c_ref, w_ref, kw_ref, lg_ref, part_ref, cact_ref, lb_ref):
        ca = _silu(c_ref[...])
        cact_ref[...] = ca
        part_ref[:, 0:n0] = _bdot_raw(ca, w_ref[0], _NN)
        part_ref[:, n0:2 * n0] = _bdot_raw(ca, w_ref[1], _NN)
        part_ref[:, 2 * n0:2 * n0 + nkv] = _bdot_raw(ca, kw_ref[...], _NN)
        lb_ref[...] = _sigmoid(lg_ref[0:1, :] - lg_ref[1:2, :])

    vm = pl.BlockSpec(memory_space=pltpu.VMEM)
    return pl.pallas_call(
        body, name="ada_fwd", in_specs=[vm, vm, vm, vm], out_specs=[vm, vm, vm],
        out_shape=[jax.ShapeDtypeStruct((rows, 2 * n0 + nkv), F32), jax.ShapeDtypeStruct((rows, d), F32),
                   jax.ShapeDtypeStruct((1, d), F32)],
        compiler_params=_cparams(),
    )(c_all, ada_w, kv_ada_w, logits)


def _ada_bwd(c_act, dm0, dm1, dkv, lb, dlb):
    rows, d = c_act.shape

    def body(c_ref, d0_ref, d1_ref, dk_ref, lb_ref, dlb_ref, dw_ref, dkw_ref, dlg_ref):
        ca = c_ref[...]
        dw_ref[0] = _bdot_raw(ca, d0_ref[...], _TN)
        dw_ref[1] = _bdot_raw(ca, d1_ref[...], _TN)
        dkw_ref[...] = _bdot_raw(ca, dk_ref[...], _TN)
        lbv = lb_ref[...]
        dl0 = dlb_ref[...] * lbv * (1.0 - lbv)
        dlg_ref[0:1, :] = dl0
        dlg_ref[1:2, :] = -dl0

    vm = pl.BlockSpec(memory_space=pltpu.VMEM)
    return pl.pallas_call(
        body, name="ada_bwd", in_specs=[vm] * 6, out_specs=[vm, vm, vm],
        out_shape=[jax.ShapeDtypeStruct((2, d, dm0.shape[1]), F32), jax.ShapeDtypeStruct((d, dkv.shape[1]), F32),
                   jax.ShapeDtypeStruct((2, d), F32)],
        compiler_params=_cparams(),
    )(c_act, dm0, dm1, dkv, lb, dlb)


def _adamw(w, g, m, v, name, tr=512, after=None):
    r, c = w.shape
    tr = _divisor_tile(r, tr, unit=8)
    c1 = 1.0 - ADAM_B1 ** ADAM_STEP
    c2 = 1.0 - ADAM_B2 ** ADAM_STEP
    deps = [] if after is None else [after]

    def body(w_ref, g_ref, m_ref, v_ref, *rest):
        d_ref, mo_ref, vo_ref = rest[len(deps):]
        gv = g_ref[...]
        mn = ADAM_B1 * m_ref[...] + (1.0 - ADAM_B1) * gv
        vn = ADAM_B2 * v_ref[...] + (1.0 - ADAM_B2) * (gv * gv)
        d_ref[...] = -ADAM_LR * ((mn / c1) / (jnp.sqrt(vn / c2) + ADAM_EPS) + ADAM_WD * w_ref[...])
        mo_ref[...] = mn
        vo_ref[...] = vn

    spec = pl.BlockSpec((tr, c), lambda i: (i, 0))
    out = jax.ShapeDtypeStruct((r, c), F32)
    return pl.pallas_call(body, name=name, grid=(r // tr,),
                          in_specs=[spec] * 4 + [pl.BlockSpec(a.shape, lambda i: (0, 0)) for a in deps],
                          out_specs=[spec] * 3, out_shape=[out, out, out],
                          compiler_params=_cparams(dimension_semantics=("parallel",)))(w, g, m, v, *deps)


def _pad_rows(a, rows):
    return jnp.pad(a, ((0, rows - a.shape[0]), (0, 0)))


def _pack_small(parts, lanes=LANES, row_unit=8):
    flat = jnp.concatenate([p.reshape(-1).astype(F32) for p in parts])
    rows = _round_up(-(-flat.shape[0] // lanes), row_unit)
    return jnp.pad(flat, (0, rows * lanes - flat.shape[0])).reshape(rows, lanes)


def _unpack_small(flat, shapes):
    out, off = [], 0
    for s in shapes:
        n = 1
        for k in s:
            n *= k
        out.append(flat[off:off + n].reshape(s))
        off += n
    return out


def _pad_shard_cols(a, n_loc, n_pad):
    lead = a.shape[:-1]
    a = a.reshape(*lead, NDEV, n_loc)
    a = jnp.pad(a, [(0, 0)] * (len(lead) + 1) + [(0, n_pad - n_loc)])
    return a.reshape(*lead, NDEV * n_pad)


def _unpad_shard_cols(a, n_loc, n_pad):
    lead = a.shape[:-1]
    return a.reshape(*lead, NDEV, n_pad)[..., :n_loc].reshape(*lead, NDEV * n_loc)


def kernel(x, c, ada_w, ada_b, a_w_in, a_lb_logits, a_norm_g, a_w_out, kv_ada_w, kv_ada_b, kv_w, kv_b_f, k_norm_g, b_w_q, q_norm_g, b_w_out, ffn_w_up, ffn_conv_w, ffn_conv_b, ffn_w_down, loss_target, m_ada_w, m_ada_b, m_a_w_in, m_a_lb_logits, m_a_norm_g, m_a_w_out, m_kv_ada_w, m_kv_ada_b, m_kv_w, m_kv_b_f, m_k_norm_g, m_b_w_q, m_q_norm_g, m_b_w_out, m_ffn_w_up, m_ffn_conv_w, m_ffn_conv_b, m_ffn_w_down, v_ada_w, v_ada_b, v_a_w_in, v_a_lb_logits, v_a_norm_g, v_a_w_out, v_kv_ada_w, v_kv_ada_b, v_kv_w, v_kv_b_f, v_k_norm_g, v_b_w_q, v_q_norm_g, v_b_w_out, v_ffn_w_up, v_ffn_conv_w, v_ffn_conv_b, v_ffn_w_down):
    t, d = x.shape[1], x.shape[2]
    nh = d // HEAD
    ncw = ffn_w_up.shape[2]
    ncp = _round_up(ncw, LANES)
    two_f = ncw * NDEV
    ff = two_f // 2
    fp = ncp * NDEV // 2
    rd = ffn_w_down.shape[1]
    me = 4 * lax.axis_index("x") + 2 * lax.axis_index("y") + lax.axis_index("c")
    weights = dict(ada_w=ada_w, ada_b=ada_b, a_w_in=a_w_in, a_lb_logits=a_lb_logits, a_norm_g=a_norm_g,
                   a_w_out=a_w_out, kv_ada_w=kv_ada_w, kv_ada_b=kv_ada_b, kv_w=kv_w, kv_b_f=kv_b_f,
                   k_norm_g=k_norm_g, b_w_q=b_w_q, q_norm_g=q_norm_g, b_w_out=b_w_out, ffn_w_up=ffn_w_up,
                   ffn_conv_w=ffn_conv_w, ffn_conv_b=ffn_conv_b, ffn_w_down=ffn_w_down)
    m_in = dict(ada_w=m_ada_w, ada_b=m_ada_b, a_w_in=m_a_w_in, a_lb_logits=m_a_lb_logits, a_norm_g=m_a_norm_g,
                a_w_out=m_a_w_out, kv_ada_w=m_kv_ada_w, kv_ada_b=m_kv_ada_b, kv_w=m_kv_w, kv_b_f=m_kv_b_f,
                k_norm_g=m_k_norm_g, b_w_q=m_b_w_q, q_norm_g=m_q_norm_g, b_w_out=m_b_w_out, ffn_w_up=m_ffn_w_up,
                ffn_conv_w=m_ffn_conv_w, ffn_conv_b=m_ffn_conv_b, ffn_w_down=m_ffn_w_down)
    v_in = dict(ada_w=v_ada_w, ada_b=v_ada_b, a_w_in=v_a_w_in, a_lb_logits=v_a_lb_logits, a_norm_g=v_a_norm_g,
                a_w_out=v_a_w_out, kv_ada_w=v_kv_ada_w, kv_ada_b=v_kv_ada_b, kv_w=v_kv_w, kv_b_f=v_kv_b_f,
                k_norm_g=v_k_norm_g, b_w_q=v_b_w_q, q_norm_g=v_q_norm_g, b_w_out=v_b_w_out, ffn_w_up=v_ffn_w_up,
                ffn_conv_w=v_ffn_conv_w, ffn_conv_b=v_ffn_conv_b, ffn_w_down=v_ffn_w_down)
    order = list(weights)

    up_loc = jnp.pad(ffn_w_up, ((0, 0), (0, 0), (0, ncp - ncw))).astype(BF16)
    down_loc = ffn_w_down.astype(BF16)
    gather_names = {"l0b": ["a_out", "up0", "down0"], "l1": ["kv", "b_q", "b_out", "up1", "down1"]}
    shards = {"a_out": a_w_out[0].astype(BF16), "up0": up_loc[0], "down0": down_loc[0], "kv": kv_w.T.astype(BF16),
              "b_q": b_w_q[0].astype(BF16), "b_out": b_w_out[0].astype(BF16), "up1": up_loc[1],
              "down1": down_loc[1]}
    pre = _pack_small([c, a_lb_logits, ffn_conv_w])
    a_in_all, pre_all = _all_gather([a_w_in[0].astype(BF16), pre], "gather_a_w_in_and_small_inputs")
    pre_all = pre_all.reshape(NDEV, -1)
    c_all = pre_all[:, :d]
    logits = pre_all[:, d:d + 2 * HEAD].reshape(NDEV, 2, HEAD).transpose(1, 0, 2).reshape(2, d)
    conv_w_full = pre_all[:, d + 2 * HEAD:d + 2 * HEAD + 2 * CONV_TAPS * ncw]
    conv_w_full = conv_w_full.reshape(NDEV, 2, CONV_TAPS, ncw).transpose(1, 2, 0, 3).reshape(2, CONV_TAPS, two_f)

    part, c_act, lb = _ada_fwd(_pad_rows(c_all, 2 * NDEV), ada_w, kv_ada_w, logits)
    (part_all,) = _all_gather([part[:NDEV]], "gather_adaln")
    mine = lax.dynamic_index_in_dim(part_all, me, axis=1, keepdims=False)
    n0, nkv = ada_w.shape[2], kv_ada_w.shape[1]
    mod_names = ["sh1", "sc1", "g1", "sh2", "sc2", "g2"]
    mods = {}
    for l in range(2):
        row = mine[:, l * n0:(l + 1) * n0].reshape(-1) + ada_b[l]
        for k, nm in enumerate(mod_names):
            mods[f"{nm}_{l}"] = row[k * d:(k + 1) * d].reshape(1, d)
    kvrow = mine[:, 2 * n0:2 * n0 + nkv].reshape(-1) + kv_ada_b
    mods["kv_sh"], mods["kv_sc"] = kvrow[:d].reshape(1, d), kvrow[d:].reshape(1, d)

    in_flight = {}

    def start_gather(grp, dep):
        srcs = [shards[n] for n in gather_names[grp]]
        in_flight[grp], started = _xchg_start(srcs, False, SAME_CORE, dep, f"gather_{grp}_start")
        return started

    zero = start_gather("l0b", part_all)
    mods["sh1_0"] = mods["sh1_0"] + zero

    small = {"a_norm_g": a_norm_g, "k_norm_g": k_norm_g.reshape(1, HEAD), "q_norm_g": q_norm_g, "kv_b_f": kv_b_f}
    for l in range(2):
        small[f"conv_w{l}"] = _pad_shard_cols(conv_w_full[l], ncw, ncp).reshape(CONV_TAPS, 2, fp).transpose(1, 0, 2)
        small[f"conv_b{l}"] = _pad_shard_cols(ffn_conv_b[l], ncw, ncp).reshape(2, 1, fp)

    forwarding = {}

    def pre_w(grp, after):
        arrived = _xchg_wait(in_flight[grp], after, False, SAME_CORE, f"gather_{grp}_wait")
        forwarding[grp], started = _sibling_forward_start(arrived, f"gather_{grp}_to_sibling_start")
        return started

    def get_w(grp, after):
        if grp == "l0a":
            return {"a_in": a_in_all}
        full = _sibling_forward_wait(forwarding[grp], after, f"gather_{grp}_to_sibling_wait")
        if grp == "l0b":
            started = start_gather("l1", full[0])
            full[0] = full[0] + started.astype(full[0].dtype)
        got = dict(zip(gather_names[grp], full))
        out = {}
        for n, a in got.items():
            if n in ("a_out", "b_out"):
                out[n] = a.reshape(d, d)
            elif n in ("down0", "down1"):
                dn = a.reshape(NCHIP, ff // NCHIP, d)
                out[n] = jnp.pad(dn, ((0, 0), (0, ncp - ncw), (0, 0))).reshape(fp, d)
            elif n == "kv":
                kv_t = a.reshape(NDEV * kv_w.shape[1], d)
                out["kv_k"], out["kv_v"] = kv_t[:d], kv_t[d:2 * d]
                out["kv_f"] = jnp.pad(kv_t[2 * d:], ((0, LANES - nh), (0, 0)))
            else:
                out[n] = a
        return out

    scatter_flight, g_last = {}, {}

    def put_g(grp, gr):
        if grp == "l0a":
            g_last.update(gr)
            return zero
        if grp == "l1":
            g_kvw = jnp.concatenate([gr["kv_k"], gr["kv_v"], gr["kv_f"][:nh].astype(BF16)], axis=0)
            arrs = {"kv_w": g_kvw.reshape(NDEV, kv_w.shape[1], d), "b_w_q": gr["b_q"],
                    "b_w_out": gr["b_out"].reshape(NDEV, d // NDEV, d), "up1": gr["up1"],
                    "down1": gr["down1"].reshape(NCHIP, ncp, d)[:, :ncw].reshape(NDEV, rd, d)}
        else:
            arrs = {"a_w_out": gr["a_out"].reshape(NDEV, d // NDEV, d), "up0": gr["up0"],
                    "down0": gr["down0"].reshape(NCHIP, ncp, d)[:, :ncw].reshape(NDEV, rd, d)}
        srcs = list(arrs.values())
        handles, sent = _xchg_start(srcs, True, ALL_PEERS, srcs[0], f"scatter_{grp}_start")
        scatter_flight[grp] = (list(arrs), handles)
        return sent

    loss_v, grad_x, dmods, dlb, g = _local_step(x[0], loss_target[0], mods, lb, small, pre_w, get_w, put_g)

    g_sum = {}
    for grp in ("l1", "l0b"):
        names, handles = scatter_flight[grp]
        for nm, a in zip(names, _xchg_wait(handles, grad_x, True, ALL_PEERS, f"scatter_{grp}_wait")):
            g_sum[nm] = _slab_sum(a, f"rs_slab_sum_{nm}")

    def conv_w_grad(a):
        return _unpad_shard_cols(a.transpose(1, 0, 2).reshape(CONV_TAPS, 2 * fp), ncw, ncp)

    def conv_b_grad(a):
        return _unpad_shard_cols(a.reshape(2 * fp), ncw, ncp)

    dmod_vec = [dmods[f"{nm}_{l}"] for l in range(2) for nm in mod_names] + [dmods["kv_sh"], dmods["kv_sc"]]
    post = _pack_small(dmod_vec + [dlb, g["a_norm_g"], g["k_norm_g"], g["q_norm_g"],
                                   jnp.pad(g["kv_b_f"].reshape(-1), (0, LANES - nh)),
                                   conv_w_grad(g["conv_w0"]), conv_w_grad(g["conv_w1"]),
                                   conv_b_grad(g["conv_b0"]), conv_b_grad(g["conv_b1"]), loss_v])
    (post_all,) = _all_gather([post], "gather_small_grads")
    a_in_flight, a_in_sent = _xchg_start([g_last["a_in"]], True, ALL_PEERS, post_all, "scatter_l0a_start")
    a_in_sent = a_in_sent.reshape(1, 1)
    tot = _slab_sum(post_all, "small_grad_sum").reshape(-1)
    nmod = 14 * d
    (t_mod, t_lb, t_ang, t_kng, t_qng, t_bf, t_cw, t_cb, t_loss) = _unpack_small(
        tot, [(nmod,), (1, d), (1, HEAD), (HEAD,), (1, HEAD), (LANES,), (2, CONV_TAPS, two_f), (2, two_f),
              (LANES,)])
    loss = t_loss[0]
    dm_all = post_all.reshape(NDEV, -1)[:, :nmod]
    dm0 = lax.dynamic_slice_in_dim(dm_all[:, :6 * d], me * n0, n0, axis=1)
    dm1 = lax.dynamic_slice_in_dim(dm_all[:, 6 * d:12 * d], me * n0, n0, axis=1)
    dkv = lax.dynamic_slice_in_dim(dm_all[:, 12 * d:], me * nkv, nkv, axis=1)
    g_ada_w, g_kv_ada_w, g_logits = _ada_bwd(c_act, _pad_rows(dm0, 2 * NDEV), _pad_rows(dm1, 2 * NDEV),
                                              _pad_rows(dkv, 2 * NDEV), lb, t_lb)

    grads = {
        "ada_w": g_ada_w,
        "ada_b": t_mod[:12 * d].reshape(2, 6 * d),
        "a_lb_logits": lax.dynamic_slice_in_dim(g_logits, me * HEAD, HEAD, axis=1),
        "a_norm_g": t_ang,
        "a_w_out": g_sum["a_w_out"].reshape(a_w_out.shape),
        "kv_ada_w": g_kv_ada_w,
        "kv_ada_b": t_mod[12 * d:],
        "kv_w": g_sum["kv_w"].T,
        "kv_b_f": t_bf[:nh],
        "k_norm_g": t_kng,
        "b_w_q": g_sum["b_w_q"].reshape(b_w_q.shape),
        "q_norm_g": t_qng,
        "b_w_out": g_sum["b_w_out"].reshape(b_w_out.shape),
        "ffn_w_up": jnp.stack([g_sum["up0"][:, :ncw], g_sum["up1"][:, :ncw]]),
        "ffn_conv_w": lax.dynamic_slice_in_dim(t_cw, me * ncw, ncw, axis=2),
        "ffn_conv_b": t_cb,
        "ffn_w_down": jnp.stack([g_sum["down0"], g_sum["down1"]]),
    }

    big_adam = ["ada_w", "a_w_out", "kv_ada_w", "kv_w", "b_w_q", "b_w_out", "ffn_w_up", "ffn_w_down", "a_w_in"]
    small_adam = [n for n in order if n not in big_adam]
    delta, new_m, new_v = {}, {}, {}
    packs = [_pack_small([src[n] for n in small_adam]) for src in (weights, grads, m_in, v_in)]
    outs = _adamw(*packs, "adamw_small", tr=packs[0].shape[0])
    shapes = [weights[n].shape for n in small_adam]
    for dst, o in zip((delta, new_m, new_v), outs):
        for n, a in zip(small_adam, _unpack_small(o.reshape(-1), shapes)):
            dst[n] = a
    for n in big_adam:
        if n == "a_w_in":
            (landed,) = _xchg_wait(a_in_flight, new_v["ffn_w_down"], True, ALL_PEERS, "scatter_l0a_wait")
            grads[n] = _slab_sum(landed, "rs_slab_sum_a_w_in").reshape(a_w_in.shape)
        shp = weights[n].shape
        two_d = lambda a: a.reshape(-1, shp[-1])
        dl, mn, vn = _adamw(two_d(weights[n]), two_d(grads[n]), two_d(m_in[n]), two_d(v_in[n]), f"adamw_{n}",
                            after=a_in_sent)
        delta[n], new_m[n], new_v[n] = dl.reshape(shp), mn.reshape(shp), vn.reshape(shp)

    return (loss, grad_x.reshape(x.shape), *[grads[n] for n in order], *[delta[n] for n in order],
            *[new_m[n] for n in order], *[new_v[n] for n in order])
```

```python
import functools

import jax
import jax.numpy as jnp
from jax import lax
from jax.experimental import pallas as pl
from jax.experimental.pallas import tpu as pltpu

F32 = jnp.float32
BF16 = jnp.bfloat16

NDEV = 8
NCHIP = 4
HEAD = 128
A_CHUNK = 64
CONV_TAPS = 3
EPS = 1e-6
NEG_INF = -1e30
LANES = 128
VMEM_LIMIT = 48 * 1024 * 1024

ADAM_LR = 0.001
ADAM_B1 = 0.9
ADAM_B2 = 0.999
ADAM_EPS = 1e-08
ADAM_WD = 0.01
ADAM_STEP = 10

_NN = (((1,), (0,)), ((), ()))
_NT = (((1,), (1,)), ((), ()))
_TN = (((0,), (0,)), ((), ()))
_MESH = pl.DeviceIdType.MESH


def _cparams(**kw):
    return pltpu.CompilerParams(vmem_limit_bytes=VMEM_LIMIT, **kw)


def _divisor_tile(n, pref, unit=LANES):
    if n <= pref:
        return n
    best = None
    for t in range(unit, pref + 1, unit):
        if n % t == 0:
            best = t
    assert best is not None, (n, pref)
    return best


def _round_up(n, unit):
    return -(-n // unit) * unit


def _bdot_raw(a, b, dims):
    return lax.dot_general(a.astype(BF16), b.astype(BF16), dims, preferred_element_type=F32)


@jax.custom_vjp
def _dot_nn(a, b):
    return _bdot_raw(a, b, _NN)


@jax.custom_vjp
def _dot_nt(a, b):
    return _bdot_raw(a, b, _NT)


@jax.custom_vjp
def _dot_tn(a, b):
    return _bdot_raw(a, b, _TN)


_dot_nn.defvjp(lambda a, b: (_bdot_raw(a, b, _NN), (a, b)),
               lambda r, g: (_dot_nt(g, r[1]), _dot_tn(r[0], g)))
_dot_nt.defvjp(lambda a, b: (_bdot_raw(a, b, _NT), (a, b)),
               lambda r, g: (_dot_nn(g, r[1]), _dot_tn(g, r[0])))
_dot_tn.defvjp(lambda a, b: (_bdot_raw(a, b, _TN), (a, b)),
               lambda r, g: (_dot_nt(r[1], g), _dot_nn(r[0], g)))


def _f32dot(a, b):
    return lax.dot_general(a, b, _NN, precision=lax.Precision.HIGHEST, preferred_element_type=F32)


def _sigmoid(x):
    return jax.nn.sigmoid(x)


def _silu(x):
    return x * jax.nn.sigmoid(x)


def _rms(x):
    return x * lax.rsqrt(jnp.mean(x * x, axis=-1, keepdims=True) + EPS)


def _modulate(x, sh, sc):
    return _rms(x) * (1.0 + sc) + sh


def _mm_call(a, b, dims, a_spec, b_spec, o_spec, o_shape, grid, acc_tile, name):
    nk = grid[2]

    def body(a_ref, b_ref, o_ref, *acc):
        p = lax.dot_general(a_ref[...].astype(BF16), b_ref[...].astype(BF16), dims,
                            preferred_element_type=F32)
        if nk == 1:
            o_ref[...] = p.astype(o_ref.dtype)
        else:
            kk = pl.program_id(2)

            @pl.when(kk == 0)
            def _():
                acc[0][...] = p

            @pl.when(kk > 0)
            def _():
                acc[0][...] += p

            @pl.when(kk == nk - 1)
            def _():
                o_ref[...] = acc[0][...].astype(o_ref.dtype)

    return pl.pallas_call(
        body, name=name, grid=grid, in_specs=[a_spec, b_spec], out_specs=o_spec, out_shape=o_shape,
        scratch_shapes=[pltpu.VMEM(acc_tile, F32)] if nk > 1 else [],
        compiler_params=_cparams(dimension_semantics=("parallel", "parallel", "arbitrary")),
    )(a, b)


def _mm(a, b, mode, out_dtype, name, tm=1024, tn=1024, tk=2048):
    if mode == "nn":
        (m, k), (k2, n) = a.shape, b.shape
    elif mode == "nt":
        (m, k), (n, k2) = a.shape, b.shape
    else:
        (k, m), (k2, n) = a.shape, b.shape
    assert k == k2, (a.shape, b.shape, mode)
    tm, tn, tk = _divisor_tile(m, tm), _divisor_tile(n, tn), _divisor_tile(k, tk)
    if mode == "tn":
        a_spec = pl.BlockSpec((tk, tm), lambda i, j, kk: (kk, i))
    else:
        a_spec = pl.BlockSpec((tm, tk), lambda i, j, kk: (i, kk))
    if mode == "nt":
        b_spec = pl.BlockSpec((tn, tk), lambda i, j, kk: (j, kk))
    else:
        b_spec = pl.BlockSpec((tk, tn), lambda i, j, kk: (kk, j))
    return _mm_call(a, b, {"nn": _NN, "nt": _NT, "tn": _TN}[mode], a_spec, b_spec,
                    pl.BlockSpec((tm, tn), lambda i, j, kk: (i, j)), jax.ShapeDtypeStruct((m, n), out_dtype),
                    (m // tm, n // tn, k // tk), (tm, tn), name)


def _wblk_act_spec(rows, gb, nl, split, nb, row_axis, blk_axis):
    if split == 1:
        return pl.BlockSpec((rows, gb * nl), lambda *g: (g[row_axis], g[blk_axis]))
    groups = nb // split // gb
    return pl.BlockSpec((None, rows, gb * nl),
                        lambda *g: (g[blk_axis] // groups, g[row_axis], g[blk_axis] % groups))


def _mm_wblk(a, wb, out_dtype, name, *, gb, row_off=0, split=1, tm=1024):
    m, k = a.shape
    nb, _, nl = wb.shape
    assert (nb // split) % gb == 0
    tm = _divisor_tile(m, tm)

    def body(a_ref, b_ref, o_ref):
        av = a_ref[...].astype(BF16)
        for s in range(gb):
            o_ref[:, s * nl:(s + 1) * nl] = lax.dot_general(
                av, b_ref[s].astype(BF16), _NN, preferred_element_type=F32).astype(o_ref.dtype)

    o_shape = (m, nb * nl) if split == 1 else (split, m, nb // split * nl)
    return pl.pallas_call(
        body, name=name, grid=(nb // gb, m // tm),
        in_specs=[pl.BlockSpec((tm, k), lambda j, i: (i, 0)),
                  pl.BlockSpec((gb, k, nl), lambda j, i: (j, row_off, 0))],
        out_specs=_wblk_act_spec(tm, gb, nl, split, nb, 1, 0),
        out_shape=jax.ShapeDtypeStruct(o_shape, out_dtype),
        compiler_params=_cparams(dimension_semantics=("parallel", "parallel")),
    )(a, wb)


def _mm_wblk_dx(dy, wb, out_dtype, name, *, k, gb, row_off=0, split=1, tm=1024):
    nb, _, nl = wb.shape
    m = dy.shape[-2]
    tm = _divisor_tile(m, tm)
    nk = nb // gb
    per = nb // split
    whole = split > 1 and gb == nb
    assert whole or per % gb == 0

    def body(a_ref, b_ref, o_ref, *acc):
        p = None
        for s in range(gb):
            a_blk = a_ref[s // per, :, (s % per) * nl:(s % per + 1) * nl] if whole else a_ref[:, s * nl:(s + 1) * nl]
            q = lax.dot_general(a_blk.astype(BF16), b_ref[s].astype(BF16), _NT, preferred_element_type=F32)
            p = q if p is None else p + q
        if nk == 1:
            o_ref[...] = p.astype(o_ref.dtype)
        else:
            kk = pl.program_id(1)

            @pl.when(kk == 0)
            def _():
                acc[0][...] = p

            @pl.when(kk > 0)
            def _():
                acc[0][...] += p

            @pl.when(kk == nk - 1)
            def _():
                o_ref[...] = acc[0][...].astype(o_ref.dtype)

    return pl.pallas_call(
        body, name=name, grid=(m // tm, nk),
        in_specs=[pl.BlockSpec((split, tm, per * nl), lambda i, kk: (0, i, 0)) if whole
                  else _wblk_act_spec(tm, gb, nl, split, nb, 0, 1),
                  pl.BlockSpec((gb, k, nl), lambda i, kk: (kk, row_off, 0))],
        out_specs=pl.BlockSpec((tm, k), lambda i, kk: (i, 0)),
        out_shape=jax.ShapeDtypeStruct((m, k), out_dtype),
        scratch_shapes=[pltpu.VMEM((tm, k), F32)] if nk > 1 else [],
        compiler_params=_cparams(dimension_semantics=("parallel", "arbitrary")),
    )(dy, wb)


def _mm_wblk_dw(x, dy, name, *, nb, gb, split=1, tk=1024):
    t, k = x.shape
    assert (nb // split) % gb == 0
    nl = dy.shape[-1] * split // nb
    tk = _divisor_tile(t, tk)
    nk = t // tk

    def body(a_ref, b_ref, o_ref, *acc):
        kk = pl.program_id(1)
        av = a_ref[...].astype(BF16)
        for s in range(gb):
            p = lax.dot_general(av, b_ref[:, s * nl:(s + 1) * nl].astype(BF16), _TN, preferred_element_type=F32)
            if nk == 1:
                o_ref[s] = p.astype(o_ref.dtype)
                continue

            @pl.when(kk == 0)
            def _():
                acc[0][s] = p

            @pl.when(kk > 0)
            def _():
                acc[0][s] += p

        if nk > 1:
            @pl.when(kk == nk - 1)
            def _():
                o_ref[...] = acc[0][...].astype(o_ref.dtype)

    return pl.pallas_call(
        body, name=name, grid=(nb // gb, nk),
        in_specs=[pl.BlockSpec((tk, k), lambda j, kk: (kk, 0)), _wblk_act_spec(tk, gb, nl, split, nb, 1, 0)],
        out_specs=pl.BlockSpec((gb, k, nl), lambda j, kk: (j, 0, 0)),
        out_shape=jax.ShapeDtypeStruct((nb, k, nl), BF16),
        scratch_shapes=[pltpu.VMEM((gb, k, nl), F32)] if nk > 1 else [],
        compiler_params=_cparams(dimension_semantics=("parallel", "arbitrary")),
    )(x, dy)


def _row_specs(rows, tb, nsub):
    return [pl.BlockSpec((tb, nsub * cw), functools.partial(lambda i, off: (i, off), off=off))
            for (_, cw, off) in rows]


def _vec_specs(params):
    return [pl.BlockSpec(p.shape, lambda i: (0, 0)) for p in params]


def _row_fwd(f, rows, params, out_dtypes, *, nsub=1, tb, name):
    t = rows[0][0].shape[0]
    tb = min(tb, t)
    n_r, n_p = len(rows), len(params)
    blk = [jax.ShapeDtypeStruct((tb, cw), F32) for (_, cw, _) in rows]
    blk += [jax.ShapeDtypeStruct(p.shape, F32) for p in params]
    out_avals = jax.eval_shape(f, *blk)

    def body(*refs):
        pv = [r[...] for r in refs[n_r:n_r + n_p]]
        for s in range(nsub):
            vals = [r[:, s * cw:(s + 1) * cw].astype(F32) for r, (_, cw, _) in zip(refs[:n_r], rows)]
            outs = f(*vals, *pv)
            for o_ref, o in zip(refs[n_r + n_p:], outs):
                w = o.shape[1]
                o_ref[:, s * w:(s + 1) * w] = o.astype(o_ref.dtype)

    return pl.pallas_call(
        body, name=name,
        grid=(t // tb,),
        in_specs=_row_specs(rows, tb, nsub) + _vec_specs(params),
        out_specs=[pl.BlockSpec((tb, nsub * av.shape[1]), lambda i: (i, 0)) for av in out_avals],
        out_shape=[jax.ShapeDtypeStruct((t, nsub * av.shape[1]), dt) for av, dt in zip(out_avals, out_dtypes)],
        compiler_params=_cparams(dimension_semantics=("parallel",)),
    )(*[r[0] for r in rows], *params)


def _row_bwd(f, rows, params, cots, row_grad_dtypes, *, nsub=1, tb, name, add_to=None, cot_add=None):
    t = rows[0][0].shape[0]
    tb = min(tb, t)
    n_r, n_p, n_c = len(rows), len(params), len(cots)
    want = [j for j in range(n_r) if row_grad_dtypes[j] is not None]
    cot_add = cot_add or []
    extra = [] if add_to is None else [(add_to[1], rows[add_to[0]][1], 0)]
    n_add_to = len(extra)
    extra += [(arr, cots[ci][1], 0) for ci, arr in cot_add]

    def body(*refs):
        i = pl.program_id(0)
        r_in, p_in = refs[:n_r], refs[n_r:n_r + n_p]
        c_in = refs[n_r + n_p:n_r + n_p + n_c]
        e_in = refs[n_r + n_p + n_c:n_r + n_p + n_c + len(extra)]
        outs = refs[n_r + n_p + n_c + len(extra):]
        pv = [r[...] for r in p_in]
        psum = [None] * n_p
        for s in range(nsub):
            vals = [r[:, s * cw:(s + 1) * cw].astype(F32) for r, (_, cw, _) in zip(r_in, rows)]
            cvals = [r[:, s * cw:(s + 1) * cw].astype(F32) for r, (_, cw, _) in zip(c_in, cots)]
            for (ci, _), e_ref in zip(cot_add, e_in[n_add_to:]):
                cw = cots[ci][1]
                cvals[ci] = cvals[ci] + e_ref[:, s * cw:(s + 1) * cw].astype(F32)
            _, vjp_fn = jax.vjp(f, *vals, *pv)
            grads = vjp_fn(tuple(cvals))
            for o_ref, jr in zip(outs[:len(want)], want):
                cw = rows[jr][1]
                gr = grads[jr]
                if add_to is not None and jr == add_to[0]:
                    gr = gr + e_in[0][:, s * cw:(s + 1) * cw]
                o_ref[:, s * cw:(s + 1) * cw] = gr.astype(o_ref.dtype)
            for jp in range(n_p):
                psum[jp] = grads[n_r + jp] if psum[jp] is None else psum[jp] + grads[n_r + jp]
        for o_ref, g in zip(outs[len(want):], psum):
            @pl.when(i == 0)
            def _():
                o_ref[...] = g

            @pl.when(i > 0)
            def _():
                o_ref[...] += g

    out_specs = [pl.BlockSpec((tb, nsub * rows[jr][1]), lambda i: (i, 0)) for jr in want]
    out_shape = [jax.ShapeDtypeStruct((t, nsub * rows[jr][1]), row_grad_dtypes[jr]) for jr in want]
    out_specs += _vec_specs(params)
    out_shape += [jax.ShapeDtypeStruct(p.shape, F32) for p in params]
    res = pl.pallas_call(
        body, name=name,
        grid=(t // tb,),
        in_specs=_row_specs(rows, tb, nsub) + _vec_specs(params) + _row_specs(cots, tb, nsub)
        + _row_specs(extra, tb, nsub),
        out_specs=out_specs, out_shape=out_shape,
        compiler_params=_cparams(dimension_semantics=("arbitrary",)),
    )(*[r[0] for r in rows], *params, *[c[0] for c in cots], *[e[0] for e in extra])
    return res[:len(want)], res[len(want):]


def _f_mod(x, sh, sc):
    return (_modulate(x, sh, sc),)


def _f_res_mod(x, y, g, sh, sc):
    x1 = x + g * y
    return x1, _modulate(x1, sh, sc)


def _f_res_mod2(x, y, g, sh_a, sc_a, sh_b, sc_b):
    x1 = x + g * y
    return x1, _modulate(x1, sh_a, sc_a), _modulate(x1, sh_b, sc_b)


def _f_qnorm(p, g):
    return (_rms(p) * g * (HEAD ** -0.5),)


def _f_knorm(p, g):
    return (_rms(p) * g,)


def _f_qnorm_aug(p, g):
    lane = lax.broadcasted_iota(jnp.int32, p.shape, 1)
    return (jnp.concatenate([_rms(p) * g * (HEAD ** -0.5), jnp.where(lane < 3, 1.0, 0.0)], axis=1),)


def _f_knorm_aug(p, c0, c1, c2, g):
    lane = lax.broadcasted_iota(jnp.int32, p.shape, 1)
    aug = jnp.where(lane == 0, c0, jnp.where(lane == 1, c1, jnp.where(lane == 2, c2, 0.0)))
    return (jnp.concatenate([_rms(p) * g, aug], axis=1),)


def _split3(a):
    round_bf16 = lambda v: lax.reduce_precision(v, exponent_bits=8, mantissa_bits=7)
    hi = round_bf16(a)
    mid = round_bf16(a - hi)
    lo = round_bf16(a - hi - mid)
    return hi.astype(BF16), mid.astype(BF16), lo.astype(BF16)


def _f_outgate(o, og):
    return (o * _sigmoid(og),)


def _loss_call(x3, f, g2, target, tb):
    t, d = x3.shape
    tb = min(tb, t)

    def body(x_ref, f_ref, g_ref, t_ref, loss_ref, dx_ref, df_ref, dg_ref):
        i = pl.program_id(0)
        fv = f_ref[...]
        g = g_ref[...]
        e = x_ref[...] + g * fv - t_ref[...]
        dx = e * (1.0 / d)
        part = 0.5 * jnp.sum(jnp.sum(e * dx, axis=1, keepdims=True), axis=0, keepdims=True)
        dx_ref[...] = dx
        df_ref[...] = (g * dx).astype(df_ref.dtype)
        dg = jnp.sum(dx * fv, axis=0, keepdims=True)

        @pl.when(i == 0)
        def _():
            loss_ref[...] = jnp.broadcast_to(part, loss_ref.shape)
            dg_ref[...] = dg

        @pl.when(i > 0)
        def _():
            loss_ref[...] += jnp.broadcast_to(part, loss_ref.shape)
            dg_ref[...] += dg

    row = pl.BlockSpec((tb, d), lambda i: (i, 0))
    vec = pl.BlockSpec((1, d), lambda i: (0, 0))
    return pl.pallas_call(
        body, name="loss_head",
        grid=(t // tb,),
        in_specs=[row, row, vec, row],
        out_specs=[pl.BlockSpec((1, LANES), lambda i: (0, 0)), row, row, vec],
        out_shape=[jax.ShapeDtypeStruct((1, LANES), F32), jax.ShapeDtypeStruct((t, d), F32),
                   jax.ShapeDtypeStruct((t, d), BF16), jax.ShapeDtypeStruct((1, d), F32)],
        compiler_params=_cparams(dimension_semantics=("arbitrary",)),
    )(x3, f, g2, target)


def _hg_mask(tb):
    br = lax.broadcasted_iota(jnp.int32, (tb, tb), 0)
    bs = lax.broadcasted_iota(jnp.int32, (tb, tb), 1)
    return jnp.logical_and(br // A_CHUNK == bs // A_CHUNK, bs <= br).astype(F32)


def _hg_consts(mask):
    c = A_CHUNK
    r = lax.broadcasted_iota(jnp.int32, (c, c), 0)
    s = lax.broadcasted_iota(jnp.int32, (c, c), 1)
    return (s <= r).astype(F32), (r <= s).astype(F32), mask > 0.5


def _chunk_apply(mat, x):
    c = mat.shape[0]
    return jnp.concatenate([_f32dot(mat, x[i * c:(i + 1) * c]) for i in range(x.shape[0] // c)], axis=0)


@jax.custom_vjp
def _chunk_cumsum(x, tri, tri_t):
    return _chunk_apply(tri, x)


_chunk_cumsum.defvjp(lambda x, tri, tri_t: (_chunk_apply(tri, x), (tri, tri_t)),
                     lambda r, g: (_chunk_apply(r[1], g), jnp.zeros_like(r[0]), jnp.zeros_like(r[1])))


def _per_chunk(a, b, dims):
    return jnp.stack([_bdot_raw(a[i], b[i], dims) for i in range(a.shape[0])])


@jax.custom_vjp
def _chunk_tn(a, b):
    return _per_chunk(a, b, _TN)


@jax.custom_vjp
def _chunk_nt(a, b):
    return _per_chunk(a, b, _NT)


@jax.custom_vjp
def _chunk_nn(a, b):
    return _per_chunk(a, b, _NN)


_chunk_tn.defvjp(lambda a, b: (_per_chunk(a, b, _TN), (a, b)),
                 lambda r, g: (_chunk_nt(r[1], g), _chunk_nn(r[0], g)))
_chunk_nt.defvjp(lambda a, b: (_per_chunk(a, b, _NT), (a, b)),
                 lambda r, g: (_chunk_nn(g, r[1]), _chunk_tn(g, r[0])))
_chunk_nn.defvjp(lambda a, b: (_per_chunk(a, b, _NN), (a, b)),
                 lambda r, g: (_chunk_nt(g, r[1]), _chunk_tn(r[0], g)))


def _scan_states(decay, m, st):
    sts = []
    for i in range(m.shape[0]):
        sts.append(st)
        st = st * decay[i] + m[i]
    return jnp.stack(sts), st


@jax.custom_vjp
def _state_scan(decay, m, st):
    return _scan_states(decay, m, st)


def _state_scan_fwd(decay, m, st):
    sts, st_out = _scan_states(decay, m, st)
    return (sts, st_out), (decay, sts)


def _state_scan_bwd(res, cts):
    decay, sts = res
    d_sts, g = cts
    d_decay, d_m = [], []
    for i in range(sts.shape[0] - 1, -1, -1):
        d_m.append(g)
        d_decay.append(jnp.sum(g * sts[i], axis=0, keepdims=True))
        g = g * decay[i] + d_sts[i]
    return jnp.stack(d_decay[::-1]), jnp.stack(d_m[::-1]), g


_state_scan.defvjp(_state_scan_fwd, _state_scan_bwd)


def _hg_block(qp, fp, ip, gp, lb, ng, st, tri, tri_t, bd_causal):
    tb = qp.shape[0]
    c = A_CHUNK
    n = tb // c
    q = _silu(qp)
    fg = lb + (1.0 - lb) * _sigmoid(fp)
    logf = jnp.log(fg)
    k = 1.0 - fg
    b3 = _chunk_cumsum(logf, tri, tri_t).reshape(n, c, HEAD)
    pos = lax.broadcasted_iota(jnp.int32, (1, c, 1), 1)
    b_mid = lax.stop_gradient(jnp.sum(jnp.where(pos == c // 2, b3, 0.0), axis=1, keepdims=True))
    b_last = jnp.sum(jnp.where(pos == c - 1, b3, 0.0), axis=1, keepdims=True)
    q3, k3, v3 = q.reshape(n, c, HEAD), k.reshape(n, c, HEAD), ip.reshape(n, c, HEAD)
    scores = _dot_nt((q3 * jnp.exp(b3 - b_mid)).reshape(tb, HEAD), (k3 * jnp.exp(b_mid - b3)).reshape(tb, HEAD))
    o_intra = _dot_nn(jnp.where(bd_causal, scores, 0.0), ip)
    states, st_new = _state_scan(jnp.exp(b_last), _chunk_tn(v3, k3 * jnp.exp(b_last - b3)), st)
    o = o_intra + _chunk_nt(q3 * jnp.exp(b3), states).reshape(tb, HEAD)
    y = _rms(o) * ng * _silu(gp)
    return y, st_new


HG_HEADS = 2


def _hg_specs(tb, nh, rev_nb=None):
    wide = HG_HEADS * HEAD
    per = nh // HG_HEADS

    def row(part):
        if rev_nb is None:
            return pl.BlockSpec((tb, wide), functools.partial(lambda h, i, off: (i, off + h), off=part * per))
        return pl.BlockSpec((tb, wide),
                            functools.partial(lambda h, i, off: (rev_nb - 1 - i, off + h), off=part * per))
    return [row(0), row(1), row(2), row(3),
            pl.BlockSpec((1, wide), lambda h, i: (0, h)), pl.BlockSpec((1, HEAD), lambda h, i: (0, 0)),
            pl.BlockSpec((tb, tb), lambda h, i: (0, 0))]


def _hgrn2_fwd(proj, lb, ng, tb):
    t = proj.shape[0]
    nh = proj.shape[1] // (4 * HEAD)
    tb = min(tb, t)
    nb = t // tb
    wide = HG_HEADS * HEAD

    def body(q_ref, f_ref, i_ref, g_ref, lb_ref, ng_ref, mask_ref, y_ref, s_ref, st_ref):
        i = pl.program_id(1)

        @pl.when(i == 0)
        def _():
            st_ref[...] = jnp.zeros_like(st_ref)

        consts = _hg_consts(mask_ref[...])
        for p in range(HG_HEADS):
            cs = slice(p * HEAD, (p + 1) * HEAD)
            st = st_ref[p]
            s_ref[p, 0] = st
            y, st_new = _hg_block(q_ref[:, cs], f_ref[:, cs], i_ref[:, cs], g_ref[:, cs], lb_ref[:, cs],
                                  ng_ref[...], st, *consts)
            y_ref[:, cs] = y.astype(y_ref.dtype)
            st_ref[p] = st_new

    return pl.pallas_call(
        body, name="hgrn2_fwd",
        grid=(nh // HG_HEADS, nb),
        in_specs=_hg_specs(tb, nh),
        out_specs=[pl.BlockSpec((tb, wide), lambda h, i: (i, h)),
                   pl.BlockSpec((HG_HEADS, 1, HEAD, HEAD), lambda h, i: (h, i, 0, 0))],
        out_shape=[jax.ShapeDtypeStruct((t, nh * HEAD), BF16),
                   jax.ShapeDtypeStruct((nh, nb, HEAD, HEAD), F32)],
        scratch_shapes=[pltpu.VMEM((HG_HEADS, HEAD, HEAD), F32)],
        compiler_params=_cparams(dimension_semantics=("parallel", "arbitrary")),
    )(proj, proj, proj, proj, lb, ng, _hg_mask(tb))


def _hgrn2_bwd(proj, lb, ng, states, dy, tb):
    t = proj.shape[0]
    nh = proj.shape[1] // (4 * HEAD)
    tb = min(tb, t)
    nb = t // tb
    wide = HG_HEADS * HEAD

    def body(q_ref, f_ref, i_ref, g_ref, lb_ref, ng_ref, mask_ref, s_ref, dy_ref,
             dp_ref, dlb_ref, dng_ref, dst_ref):
        h, i = pl.program_id(0), pl.program_id(1)
        consts = _hg_consts(mask_ref[...])

        @pl.when(i == 0)
        def _():
            dst_ref[...] = jnp.zeros_like(dst_ref)
            dlb_ref[...] = jnp.zeros_like(dlb_ref)

        @pl.when(jnp.logical_and(i == 0, h == 0))
        def _():
            dng_ref[...] = jnp.zeros_like(dng_ref)

        def fn(qp, fp, ip, gp, lbx, ngx, stx):
            return _hg_block(qp, fp, ip, gp, lbx, ngx, stx, *consts)

        for p in range(HG_HEADS):
            cs = slice(p * HEAD, (p + 1) * HEAD)
            _, vjp_fn = jax.vjp(fn, q_ref[:, cs], f_ref[:, cs], i_ref[:, cs], g_ref[:, cs], lb_ref[:, cs],
                                ng_ref[...], s_ref[p, 0])
            *gparts, glb, gng, dst = vjp_fn((dy_ref[:, cs].astype(F32), dst_ref[p]))
            for part, gpart in enumerate(gparts):
                dp_ref[part, :, cs] = gpart.astype(dp_ref.dtype)
            dst_ref[p] = dst
            dlb_ref[:, cs] += glb
            dng_ref[...] += gng

    rev = lambda h, i: (nb - 1 - i, h)
    return pl.pallas_call(
        body, name="hgrn2_bwd",
        grid=(nh // HG_HEADS, nb),
        in_specs=_hg_specs(tb, nh, rev_nb=nb) + [
            pl.BlockSpec((HG_HEADS, 1, HEAD, HEAD), lambda h, i: (h, nb - 1 - i, 0, 0)),
            pl.BlockSpec((tb, wide), rev)],
        out_specs=[pl.BlockSpec((4, tb, wide), lambda h, i: (0, nb - 1 - i, h)),
                   pl.BlockSpec((1, wide), lambda h, i: (0, h)), pl.BlockSpec((1, HEAD), lambda h, i: (0, 0))],
        out_shape=[jax.ShapeDtypeStruct((4, t, nh * HEAD), BF16),
                   jax.ShapeDtypeStruct((1, nh * HEAD), F32), jax.ShapeDtypeStruct((1, HEAD), F32)],
        scratch_shapes=[pltpu.VMEM((HG_HEADS, HEAD, HEAD), F32)],
        compiler_params=_cparams(dimension_semantics=("arbitrary", "arbitrary")),
    )(proj, proj, proj, proj, lb, ng, _hg_mask(tb), states, dy)


def _fgate_consts(cb):
    r = lax.broadcasted_iota(jnp.int32, (cb, cb), 0)
    s = lax.broadcasted_iota(jnp.int32, (cb, cb), 1)
    return (r <= s).astype(F32), (r >= s).astype(F32)


def _fgate_fwd(xt, bias, cb=512):
    nh, t = xt.shape
    cb = min(cb, t)

    def body(x_ref, b_ref, o_ref):
        upper, _ = _fgate_consts(cb)
        carry = jnp.zeros((nh, 1), F32)
        for blk in range(t // cb):
            z = x_ref[:, blk * cb:(blk + 1) * cb] + b_ref[...]
            logf = jnp.minimum(z, 0.0) - jnp.log(1.0 + jnp.exp(-jnp.abs(z)))
            cs = _f32dot(logf, upper) + carry
            o_ref[:, blk * cb:(blk + 1) * cb] = cs
            carry = cs[:, cb - 1:cb]

    vm = pl.BlockSpec(memory_space=pltpu.VMEM)
    return pl.pallas_call(
        body, name="fgate_fwd", in_specs=[vm, vm], out_specs=vm,
        out_shape=jax.ShapeDtypeStruct((nh, t), F32), compiler_params=_cparams(),
    )(xt, bias)


def _fgate_bwd(xt, bias, dft, cb=512):
    nh, t = xt.shape
    cb = min(cb, t)
    nblk = t // cb

    def body(x_ref, b_ref, d_ref, dx_ref, db_ref):
        _, lower = _fgate_consts(cb)
        carry = jnp.zeros((nh, 1), F32)
        db = jnp.zeros((nh, 1), F32)
        for blk in range(nblk - 1, -1, -1):
            sl = slice(blk * cb, (blk + 1) * cb)
            dlogf = _f32dot(d_ref[:, sl], lower) + carry
            carry = dlogf[:, 0:1]
            z = x_ref[:, sl] + b_ref[...]
            dz = dlogf * (1.0 - _sigmoid(z))
            dx_ref[:, sl] = dz
            db = db + jnp.sum(dz, axis=1, keepdims=True)
        db_ref[...] = db

    vm = pl.BlockSpec(memory_space=pltpu.VMEM)
    return pl.pallas_call(
        body, name="fgate_bwd", in_specs=[vm, vm, vm], out_specs=[vm, vm],
        out_shape=[jax.ShapeDtypeStruct((nh, t), F32), jax.ShapeDtypeStruct((nh, 1), F32)],
        compiler_params=_cparams(),
    )(xt, bias, dft)


ATTN_GROUPS = 4
ATTN_FWD_HEADS = 2


def _attn_fwd(q, k, v, f_grp, blk):
    t, width = v.shape
    nh = width // HEAD
    nq = t // blk
    hpg = nh // ATTN_GROUPS

    def body(q_ref, k_ref, v_ref, fc_ref, o_ref, lse_ref):
        i = pl.program_id(0)
        tri = (lax.broadcasted_iota(jnp.int32, (blk, blk), 1) <= lax.broadcasted_iota(jnp.int32, (blk, blk), 0))
        for h0 in range(0, nh, ATTN_FWD_HEADS):
            heads = range(h0, min(h0 + ATTN_FWD_HEADS, nh))

            def tile(j, carries, masked):
                rs = pl.ds(pl.multiple_of(j * blk, blk), blk)
                out = []
                for h, (m, l, acc) in zip(heads, carries):
                    cs = slice(h * HEAD, (h + 1) * HEAD)
                    cs2 = slice(2 * h * HEAD, 2 * (h + 1) * HEAD)
                    s = _bdot_raw(q_ref[:, cs2], k_ref[rs, cs2], _NT)
                    if masked:
                        s = jnp.where(tri, s, NEG_INF)
                    m_new = jnp.maximum(m, jnp.max(s, axis=1, keepdims=True))
                    p = jnp.exp(s - m_new)
                    alpha = jnp.exp(m - m_new)
                    l_new = alpha * l + jnp.sum(p, axis=1, keepdims=True)
                    out.append((m_new, l_new, alpha * acc + _bdot_raw(p, v_ref[rs, cs], _NN)))
                return tuple(out)

            init = tuple((jnp.full((blk, 1), NEG_INF, F32), jnp.zeros((blk, 1), F32), jnp.zeros((blk, HEAD), F32))
                         for _ in heads)
            carries = lax.fori_loop(0, i, lambda j, c: tile(j, c, False), init)
            for h, (m, l, acc) in zip(heads, tile(i, carries, True)):
                o_ref[:, h * HEAD:(h + 1) * HEAD] = acc / l
                g, hh = divmod(h, hpg)
                lse_ref[g, :, hh:hh + 1] = m + jnp.log(l) + fc_ref[g, :, hh:hh + 1]

    vm = pl.BlockSpec(memory_space=pltpu.VMEM)
    stat = pl.BlockSpec((ATTN_GROUPS, blk, hpg), lambda i: (0, i, 0))
    return pl.pallas_call(
        body, name="fox_attn_fwd",
        grid=(nq,),
        in_specs=[pl.BlockSpec((blk, 2 * width), lambda i: (i, 0)), vm, vm, stat],
        out_specs=[pl.BlockSpec((blk, width), lambda i: (i, 0)), stat],
        out_shape=[jax.ShapeDtypeStruct((t, width), F32), jax.ShapeDtypeStruct((ATTN_GROUPS, t, hpg), F32)],
        compiler_params=_cparams(dimension_semantics=("parallel",)),
    )(q, k, v, f_grp)


def _attn_delta(do, o, tb):
    t, width = o.shape
    nh = width // HEAD
    hpg = nh // ATTN_GROUPS
    tb = min(tb, t)

    def body(do_ref, o_ref, dl_ref):
        for h in range(nh):
            cs = slice(h * HEAD, (h + 1) * HEAD)
            g, hh = divmod(h, hpg)
            dl_ref[g, :, hh:hh + 1] = jnp.sum(do_ref[:, cs].astype(F32) * o_ref[:, cs], axis=1, keepdims=True)

    wide = pl.BlockSpec((tb, width), lambda i: (i, 0))
    return pl.pallas_call(body, name="fox_attn_delta", grid=(t // tb,), in_specs=[wide, wide],
                          out_specs=pl.BlockSpec((ATTN_GROUPS, tb, hpg), lambda i: (0, i, 0)),
                          out_shape=jax.ShapeDtypeStruct((ATTN_GROUPS, t, hpg), F32),
                          compiler_params=_cparams(dimension_semantics=("parallel",)))(do, o)


def _attn_bwd(q, k, v, f_grp, do, lse, delta, blk):
    t, width = v.shape
    nh = width // HEAD
    nq = t // blk
    hpg = nh // ATTN_GROUPS
    gw = hpg * HEAD

    def body(q_ref, do_ref, k_ref, v_ref, fc_ref, lse_ref, dl_ref,
             dq_ref, dk_ref, dv_ref, dfc_ref, dfr_ref):
        g, j = pl.program_id(0), pl.program_id(1)
        tri = (lax.broadcasted_iota(jnp.int32, (blk, blk), 1) <= lax.broadcasted_iota(jnp.int32, (blk, blk), 0))

        @pl.when(j == 0)
        def _():
            dq_ref[...] = jnp.zeros_like(dq_ref)
            dfc_ref[...] = jnp.zeros_like(dfc_ref)

        def tile(i, carries, masked):
            rs = pl.ds(pl.multiple_of(i * blk, blk), blk)
            out = []
            for h, (dk, dv, dfs) in enumerate(carries):
                cs = slice(h * HEAD, (h + 1) * HEAD)
                cs2 = slice(2 * h * HEAD, 2 * (h + 1) * HEAD)
                csq = slice(2 * h * HEAD, (2 * h + 1) * HEAD)
                qi = q_ref[rs, csq]
                doi = do_ref[rs, cs]
                bias = fc_ref[0, rs, h:h + 1] - lse_ref[0, rs, h:h + 1]
                p = jnp.exp(_bdot_raw(q_ref[rs, cs2], k_ref[:, cs2], _NT) + bias)
                if masked:
                    p = jnp.where(tri, p, 0.0)
                ds = p * (_bdot_raw(doi, v_ref[:, cs], _NT) - dl_ref[0, rs, h:h + 1])
                dsb = ds.astype(BF16)
                dq_ref[rs, cs] += _bdot_raw(dsb, k_ref[:, csq], _NN)
                dfc_ref[0, rs, h:h + 1] += jnp.sum(ds, axis=1, keepdims=True)
                out.append((dk + _bdot_raw(dsb, qi, _TN), dv + _bdot_raw(p, doi, _TN),
                            dfs - jnp.sum(ds, axis=0, keepdims=True)))
            return tuple(out)

        init = tuple((jnp.zeros((blk, HEAD), F32), jnp.zeros((blk, HEAD), F32), jnp.zeros((1, blk), F32))
                     for _ in range(hpg))
        carries = lax.fori_loop(j + 1, nq, lambda i, c: tile(i, c, False), tile(j, init, True))
        for h, (dk, dv, dfs) in enumerate(carries):
            cs = slice(h * HEAD, (h + 1) * HEAD)
            dk_ref[:, cs] = dk
            dv_ref[:, cs] = dv.astype(dv_ref.dtype)
            dfr_ref[0, 0, h:h + 1, :] = dfs

    once = pl.Buffered(1)
    stat = pl.BlockSpec((1, t, hpg), lambda g, j: (g, 0, 0), pipeline_mode=once)
    kv_blk = pl.BlockSpec((blk, gw), lambda g, j: (j, g))
    frow = pl.BlockSpec((1, 1, hpg, blk), lambda g, j: (g, j, 0, 0))
    dq, dk, dv, dfc, dfr = pl.pallas_call(
        body, name="fox_attn_bwd",
        grid=(ATTN_GROUPS, nq),
        in_specs=[pl.BlockSpec((t, 2 * gw), lambda g, j: (0, g), pipeline_mode=once),
                  pl.BlockSpec((t, gw), lambda g, j: (0, g), pipeline_mode=once),
                  pl.BlockSpec((blk, 2 * gw), lambda g, j: (j, g)), kv_blk, stat, stat, stat],
        out_specs=[pl.BlockSpec((t, gw), lambda g, j: (0, g)), kv_blk, kv_blk,
                   pl.BlockSpec((1, t, hpg), lambda g, j: (g, 0, 0)), frow],
        out_shape=[jax.ShapeDtypeStruct((t, width), F32), jax.ShapeDtypeStruct((t, width), F32),
                   jax.ShapeDtypeStruct((t, width), BF16), jax.ShapeDtypeStruct((ATTN_GROUPS, t, hpg), F32),
                   jax.ShapeDtypeStruct((ATTN_GROUPS, nq, hpg, blk), F32)],
        compiler_params=_cparams(dimension_semantics=("parallel", "arbitrary")),
    )(q, do, k, v, f_grp, lse, delta)
    return dq, dk, dv, dfc, dfr


SUBLANES = 8


def _shift_down(u, n):
    r = pltpu.roll(u, n, 0)
    row = lax.broadcasted_iota(jnp.int32, (SUBLANES, u.shape[1]), 0)
    return jnp.concatenate([jnp.where(row < n, 0.0, r[:SUBLANES]), r[SUBLANES:]], axis=0)


def _shift_up(u, n):
    t = u.shape[0]
    r = pltpu.roll(u, t - n, 0)
    row = lax.broadcasted_iota(jnp.int32, (SUBLANES, u.shape[1]), 0)
    return jnp.concatenate([r[:t - SUBLANES], jnp.where(row >= SUBLANES - n, 0.0, r[t - SUBLANES:])], axis=0)


def _convglu_specs(t):
    return [pl.BlockSpec((2, t, LANES), lambda j: (0, 0, j)),
            pl.BlockSpec((2, CONV_TAPS, LANES), lambda j: (0, 0, j)),
            pl.BlockSpec((2, 1, LANES), lambda j: (0, 0, j))]


def _convglu_fwd(u, cw, cb):
    _, t, fp = u.shape

    def body(u_ref, w_ref, b_ref, a_ref, c_ref):
        c = []
        for hf in range(2):
            uv, w = u_ref[hf].astype(F32), w_ref[hf]
            c.append(w[0:1] * _shift_down(uv, 2) + w[1:2] * _shift_down(uv, 1) + w[2:3] * uv + b_ref[hf])
            c_ref[hf] = c[hf].astype(c_ref.dtype)
        a_ref[...] = (_silu(c[0]) * c[1]).astype(a_ref.dtype)

    return pl.pallas_call(
        body, name="convglu_fwd",
        grid=(fp // LANES,),
        in_specs=_convglu_specs(t),
        out_specs=[pl.BlockSpec((t, LANES), lambda j: (0, j)), pl.BlockSpec((2, t, LANES), lambda j: (0, 0, j))],
        out_shape=[jax.ShapeDtypeStruct((t, fp), BF16), jax.ShapeDtypeStruct((2, t, fp), BF16)],
        compiler_params=_cparams(dimension_semantics=("parallel",)),
    )(u, cw, cb)


def _convglu_bwd(u, c, cw, da):
    _, t, fp = u.shape

    def body(u_ref, c_ref, w_ref, da_ref, du_ref, dw_ref, db_ref):
        gc, vc = c_ref[0].astype(F32), c_ref[1].astype(F32)
        sg = _sigmoid(gc)
        dav = da_ref[...].astype(F32)
        dcs = [dav * vc * (sg * (1.0 + gc * (1.0 - sg))), dav * (gc * sg)]
        for hf in range(2):
            dc, w, uv = dcs[hf], w_ref[hf], u_ref[hf].astype(F32)
            dc1, dc2 = _shift_up(dc, 1), _shift_up(dc, 2)
            du_ref[hf] = (w[2:3] * dc + w[1:2] * dc1 + w[0:1] * dc2).astype(du_ref.dtype)
            dw_ref[hf, 0:1, :] = jnp.sum(dc2 * uv, axis=0, keepdims=True)
            dw_ref[hf, 1:2, :] = jnp.sum(dc1 * uv, axis=0, keepdims=True)
            dw_ref[hf, 2:3, :] = jnp.sum(dc * uv, axis=0, keepdims=True)
            db_ref[hf] = jnp.sum(dc, axis=0, keepdims=True)

    pair, taps, bias = _convglu_specs(t)
    return pl.pallas_call(
        body, name="convglu_bwd",
        grid=(fp // LANES,),
        in_specs=[pair, pair, taps, pl.BlockSpec((t, LANES), lambda j: (0, j))],
        out_specs=[pair, taps, bias],
        out_shape=[jax.ShapeDtypeStruct((2, t, fp), BF16), jax.ShapeDtypeStruct((2, CONV_TAPS, fp), F32),
                   jax.ShapeDtypeStruct((2, 1, fp), F32)],
        compiler_params=_cparams(dimension_semantics=("parallel",)),
    )(u, c, cw, da)


def _local_step(x, target, mods, lb, small, pre_w, get_w, put_g, *, tb=512, attn_blk=512):
    t, d = x.shape
    nh = d // HEAD
    nb = NDEV
    wts = {}
    vec = lambda *names: [mods[n] for n in names]

    def ffn_fwd(h2, l):
        u = _mm_wblk(h2, wts[f"up{l}"], BF16, f"ffn{l}_up", gb=nb // 2, split=2, tm=512)
        a, c = _convglu_fwd(u, small[f"conv_w{l}"], small[f"conv_b{l}"])
        f = _mm(a, wts[f"down{l}"], "nn", F32, f"ffn{l}_down", tk=4096)
        return (u, c), a, f

    def ffn_bwd(df, h2, uc, a, l):
        u, c = uc
        da = _mm(df, wts[f"down{l}"], "nt", BF16, f"ffn{l}_down_dx", tn=1536)
        dwd = _mm(a, df, "tn", BF16, f"ffn{l}_down_dw", tm=768, tk=t)
        du, dcw, dcb = _convglu_bwd(u, c, small[f"conv_w{l}"], da)
        dh2 = _mm_wblk_dx(du, wts[f"up{l}"], BF16, f"ffn{l}_up_dx", k=d, gb=nb // 2, split=2, tm=1024)
        dwu = _mm_wblk_dw(h2, du, f"ffn{l}_up_dw", nb=nb, gb=1, split=2, tk=t)
        return dh2, dwu, dwd, dcw, dcb

    (h_a,) = _row_fwd(_f_mod, [(x, d, 0)], vec("sh1_0", "sc1_0"), [BF16], tb=tb, name="l0_mod1")
    wts.update(get_w("l0a", h_a))
    proj_a = _mm_wblk(h_a, wts["a_in"], F32, "a_in", gb=nb // 2)
    ypre, states = _hgrn2_fwd(proj_a, lb, small["a_norm_g"], tb)
    pre_w("l0b", ypre)
    wts.update(get_w("l0b", ypre))
    y_a = _mm(ypre, wts["a_out"], "nn", F32, "a_out")
    x1, h2_0 = _row_fwd(_f_res_mod, [(x, d, 0), (y_a, d, 0)], vec("g1_0", "sh2_0", "sc2_0"), [F32, BF16],
                        tb=tb, name="l0_res_mod2")
    u0, a0, f0 = ffn_fwd(h2_0, 0)
    x2, h_kv, h_q = _row_fwd(_f_res_mod2, [(x1, d, 0), (f0, d, 0)],
                             [mods["g2_0"] + pre_w("l1", f0)] + vec("kv_sh", "kv_sc", "sh1_1", "sc1_1"),
                             [F32, BF16, BF16], tb=tb, name="l0_res_kvmod_qmod")
    wts.update(get_w("l1", h_kv))
    proj_k = _mm(h_kv, wts["kv_k"], "nt", F32, "k_proj")
    v_b = _mm(h_kv, wts["kv_v"], "nt", BF16, "v_proj")
    proj_f = _mm(h_kv, wts["kv_f"], "nt", F32, "kv_fproj")
    f_logit_t = proj_f[:, :nh].T
    f_bias = small["kv_b_f"].reshape(nh, 1)
    f_t = _fgate_fwd(f_logit_t, f_bias)
    f_grp = f_t.reshape(ATTN_GROUPS, nh // ATTN_GROUPS, t).transpose(0, 2, 1)
    (k_n,) = _row_fwd(_f_knorm_aug, [(proj_k, HEAD, 0)] + [(piece, 1, 0) for piece in _split3(-f_t.T)],
                      [small["k_norm_g"]], [BF16], nsub=nh, tb=tb, name="k_norm")
    proj_q = _mm_wblk(h_q, wts["b_q"], F32, "b_q", gb=nb)
    (q_n,) = _row_fwd(_f_qnorm_aug, [(proj_q, HEAD, 0)], [small["q_norm_g"]], [BF16], nsub=nh, tb=tb,
                      name="q_norm")
    o_att, lse = _attn_fwd(q_n, k_n, v_b, f_grp, attn_blk)
    (z,) = _row_fwd(_f_outgate, [(o_att, HEAD, 0), (proj_q, HEAD, 1)], [], [BF16], nsub=nh, tb=tb, name="out_gate")
    y_b = _mm(z, wts["b_out"], "nn", F32, "b_out")
    x3, h2_1 = _row_fwd(_f_res_mod, [(x2, d, 0), (y_b, d, 0)], vec("g1_1", "sh2_1", "sc2_1"), [F32, BF16],
                        tb=tb, name="l1_res_mod2")
    u1, a1, f1 = ffn_fwd(h2_1, 1)
    loss, dx4, df1, dg2_1 = _loss_call(x3, f1, mods["g2_1"], target, tb)

    g = {}
    dmods = {"g2_1": dg2_1}
    dh2, g["up1"], g["down1"], g["conv_w1"], g["conv_b1"] = ffn_bwd(df1, h2_1, u1, a1, 1)
    (dx2, dy_b), (dmods["g1_1"], dmods["sh2_1"], dmods["sc2_1"]) = _row_bwd(
        _f_res_mod, [(x2, d, 0), (y_b, d, 0)], vec("g1_1", "sh2_1", "sc2_1"),
        [(dx4, d, 0), (dh2, d, 0)], [F32, BF16], tb=tb, name="l1_res_mod2_bwd")
    dz = _mm(dy_b, wts["b_out"], "nt", BF16, "b_out_dx")
    g["b_out"] = _mm(z, dy_b, "tn", BF16, "b_out_dw", tk=t)
    (do_att, dog), _ = _row_bwd(_f_outgate, [(o_att, HEAD, 0), (proj_q, HEAD, 1)], [], [(dz, HEAD, 0)],
                                [BF16, BF16], nsub=nh, tb=tb, name="out_gate_bwd")
    delta = _attn_delta(do_att, o_att, tb)
    dq_n, dk_n, dv, dfc_q, dfr_k = _attn_bwd(q_n, k_n, v_b, f_grp, do_att, lse, delta, attn_blk)
    (dpq,), (g["q_norm_g"],) = _row_bwd(_f_qnorm, [(proj_q, HEAD, 0)], [small["q_norm_g"]],
                                        [(dq_n, HEAD, 0)], [BF16], nsub=nh, tb=tb, name="q_norm_bwd")
    dproj_q = jnp.concatenate([dpq, dog], axis=1)
    dh_q = _mm_wblk_dx(dproj_q, wts["b_q"], BF16, "b_q_dx", k=d, gb=nb)
    g["b_q"] = _mm_wblk_dw(h_q, dproj_q, "b_q_dw", nb=nb, gb=nb // 4, tk=t)
    (dpk,), (g["k_norm_g"],) = _row_bwd(_f_knorm, [(proj_k, HEAD, 0)], [small["k_norm_g"]],
                                        [(dk_n, HEAD, 0)], [BF16], nsub=nh, tb=tb, name="k_norm_bwd")
    df_t = dfc_q.transpose(0, 2, 1).reshape(nh, t) + dfr_k.transpose(0, 2, 1, 3).reshape(nh, t)
    dflogit_t, g["kv_b_f"] = _fgate_bwd(f_logit_t, f_bias, df_t)
    dproj_f = jnp.pad(dflogit_t.T, ((0, 0), (0, LANES - nh))).astype(BF16)
    dh_kv = _mm(dpk, wts["kv_k"], "nn", BF16, "k_proj_dx")
    dh_kv_v = _mm(dv, wts["kv_v"], "nn", BF16, "v_proj_dx")
    dh_kv_f = _mm(dproj_f, wts["kv_f"], "nn", BF16, "kv_fproj_dx")
    g["kv_k"] = _mm(dpk, h_kv, "tn", BF16, "k_proj_dw", tk=t)
    g["kv_v"] = _mm(dv, h_kv, "tn", BF16, "v_proj_dw", tk=t)
    g["kv_f"] = _mm(dproj_f, h_kv, "tn", F32, "kv_fproj_dw", tk=1024)
    sent = put_g("l1", {n: g.pop(n) for n in ("b_out", "b_q", "kv_k", "kv_v", "kv_f", "up1", "down1")})
    (dx1, df0), (dmods["g2_0"], dmods["kv_sh"], dmods["kv_sc"], dmods["sh1_1"], dmods["sc1_1"]) = _row_bwd(
        _f_res_mod2, [(x1, d, 0), (f0, d, 0)], [mods["g2_0"] + sent] + vec("kv_sh", "kv_sc", "sh1_1", "sc1_1"),
        [(dx2, d, 0), (dh_kv, d, 0), (dh_q, d, 0)], [F32, BF16], tb=tb, name="l0_res_kvmod_qmod_bwd",
        cot_add=[(1, dh_kv_v), (1, dh_kv_f)])
    dh2, g["up0"], g["down0"], g["conv_w0"], g["conv_b0"] = ffn_bwd(df0, h2_0, u0, a0, 0)
    (dx0, dy_a), (dmods["g1_0"], dmods["sh2_0"], dmods["sc2_0"]) = _row_bwd(
        _f_res_mod, [(x, d, 0), (y_a, d, 0)], vec("g1_0", "sh2_0", "sc2_0"),
        [(dx1, d, 0), (dh2, d, 0)], [F32, BF16], tb=tb, name="l0_res_mod2_bwd")
    dypre = _mm(dy_a, wts["a_out"], "nt", BF16, "a_out_dx")
    g["a_out"] = _mm(ypre, dy_a, "tn", BF16, "a_out_dw", tk=t)
    sent = put_g("l0b", {n: g.pop(n) for n in ("a_out", "up0", "down0")})
    dproj_a, dlb, g["a_norm_g"] = _hgrn2_bwd(proj_a, lb + sent, small["a_norm_g"], states, dypre, tb)
    dh_a = _mm_wblk_dx(dproj_a, wts["a_in"], BF16, "a_in_dx", k=d, gb=nb, split=4, tm=512)
    put_g("l0a", {"a_in": _mm_wblk_dw(h_a, dproj_a, "a_in_dw", nb=nb, gb=1, split=4, tk=t)})
    (grad_x,), (dmods["sh1_0"], dmods["sc1_0"]) = _row_bwd(
        _f_mod, [(x, d, 0)], vec("sh1_0", "sc1_0"), [(dh_a, d, 0)], [F32], tb=tb, name="l0_mod1_bwd",
        add_to=(0, dx0))
    return loss, grad_x, dmods, dlb, g


def _position():
    return lax.axis_index("x"), lax.axis_index("y"), lax.axis_index("c")


def _hbm_specs(n):
    return [pl.BlockSpec(memory_space=pl.ANY)] * n


def _all_gather(arrs, name):
    n = len(arrs)

    def body(*refs):
        x_refs, out_refs = refs[:n], refs[n:2 * n]
        send_sems, recv_sems, local_sems = refs[2 * n:]
        x, y, cc = _position()
        me, sibling = (x, y, cc), (x, y, 1 - cc)
        chips = [(1 - x, y), (x, 1 - y), (1 - x, 1 - y)]

        def copy(a, k, block, to, src=None):
            slot = out_refs[a].at[4 * block[0] + 2 * block[1] + block[2]]
            return pltpu.make_async_remote_copy(
                src_ref=slot if src is None else src, dst_ref=slot,
                send_sem=send_sems.at[7 * a + k], recv_sem=recv_sems.at[7 * a + k],
                device_id=to, device_id_type=_MESH)

        local = [pltpu.make_async_copy(x_refs[a], out_refs[a].at[4 * x + 2 * y + cc], local_sems.at[a])
                 for a in range(n)]
        for cp in local:
            cp.start()
        first = []
        for a in range(n):
            first.append(copy(a, 0, me, sibling, src=x_refs[a]))
            first += [copy(a, 1 + j, me, (*chip, cc), src=x_refs[a]) for j, chip in enumerate(chips)]
        for cp in first:
            cp.start()
        passed = []
        for j, chip in enumerate(chips):
            for a in range(n):
                copy(a, 1 + j, (*chip, cc), me).wait_recv()
                fwd = copy(a, 4 + j, (*chip, cc), sibling)
                fwd.start()
                passed.append(fwd)
        for a in range(n):
            copy(a, 0, sibling, me).wait_recv()
        for j, chip in enumerate(chips):
            for a in range(n):
                copy(a, 4 + j, (*chip, 1 - cc), me).wait_recv()
        for cp in first + passed:
            cp.wait_send()
        for cp in local:
            cp.wait()

    return pl.pallas_call(
        body, name=name,
        out_shape=[jax.ShapeDtypeStruct((NDEV, *a.shape), a.dtype) for a in arrs],
        in_specs=_hbm_specs(n), out_specs=_hbm_specs(n),
        scratch_shapes=[pltpu.SemaphoreType.DMA((7 * n,)), pltpu.SemaphoreType.DMA((7 * n,)),
                        pltpu.SemaphoreType.DMA((n,))],
    )(*arrs)


_XCHG_EFFECT = pltpu.SideEffectType.DATAFLOW_SIDE_EFFECTING
ALL_PEERS = (1, 2, 3, 4, 5, 6, 7)
SAME_CORE = (2, 4, 6)


def _xchg_copies(src_refs, land_refs, send_sems, recv_sems, local_sems, scatter, rels):
    x, y, cc = _position()
    me = 4 * x + 2 * y + cc
    remote, local = [], []
    for a, (src, land) in enumerate(zip(src_refs, land_refs)):
        local.append(pltpu.make_async_copy(src.at[me] if scatter else src, land.at[me], local_sems.at[a]))
        for idx, rel in enumerate(rels):
            px = 1 - x if rel & 4 else x
            py = 1 - y if rel & 2 else y
            pc = 1 - cc if rel & 1 else cc
            k = len(rels) * a + idx
            remote.append(pltpu.make_async_remote_copy(
                src_ref=src.at[4 * px + 2 * py + pc] if scatter else src, dst_ref=land.at[me],
                send_sem=send_sems.at[k], recv_sem=recv_sems.at[k], device_id=(px, py, pc), device_id_type=_MESH))
    return remote, local


def _xchg_start(srcs, scatter, rels, after, name):
    n = len(srcs)
    lands = [lax.empty(s.shape if scatter else (NDEV, *s.shape), s.dtype) for s in srcs]

    def body(*refs):
        remote, local = _xchg_copies(refs[:n], refs[n:2 * n], *refs[2 * n + 1:2 * n + 4], scatter, rels)
        for cp in local + remote:
            cp.start()
        token = refs[-1]
        token[...] = jnp.zeros_like(token)

    hbm = pl.BlockSpec(memory_space=pltpu.HBM)
    sem = pl.BlockSpec(memory_space=pltpu.SEMAPHORE)
    out = pl.pallas_call(
        body, name=name,
        out_shape=(pltpu.SemaphoreType.DMA((len(rels) * n,)), pltpu.SemaphoreType.DMA((len(rels) * n,)),
                   pltpu.SemaphoreType.DMA((n,)),
                   *[pltpu.HBM(a.shape, a.dtype) for a in srcs + lands], jax.ShapeDtypeStruct((8, LANES), F32)),
        in_specs=[hbm] * (2 * n) + [pl.BlockSpec(memory_space=pl.ANY)],
        out_specs=(sem, sem, sem, *[hbm] * (2 * n), pl.BlockSpec(memory_space=pltpu.VMEM)),
        input_output_aliases={i: 3 + i for i in range(2 * n)},
        compiler_params=pltpu.CompilerParams(has_side_effects=_XCHG_EFFECT),
    )(*[pltpu.with_memory_space_constraint(a, pltpu.HBM) for a in srcs + lands], after)
    return out[:-1], out[-1][0, 0]


def _xchg_wait(handles, after, scatter, rels, name):
    n = (len(handles) - 3) // 2

    def body(*refs):
        remote, local = _xchg_copies(refs[:n], refs[n:2 * n], *refs[2 * n:2 * n + 3], scatter, rels)
        for cp in remote:
            cp.wait_send()
            cp.wait_recv()
        for cp in local:
            cp.wait()

    hbm = pl.BlockSpec(memory_space=pltpu.HBM)
    sem = pl.BlockSpec(memory_space=pltpu.SEMAPHORE)
    thru = list(handles[3:])
    out = pl.pallas_call(
        body, name=name,
        out_shape=tuple(pltpu.HBM(a.shape, a.dtype) for a in thru),
        in_specs=[hbm] * (2 * n) + [sem, sem, sem, pl.BlockSpec(memory_space=pl.ANY)],
        out_specs=tuple([hbm] * (2 * n)),
        input_output_aliases={i: i for i in range(2 * n)},
        compiler_params=pltpu.CompilerParams(has_side_effects=_XCHG_EFFECT),
    )(*thru, *handles[:3], after)
    return list(out[n:])


def _sibling_copies(land_refs, send_sems, recv_sems):
    x, y, cc = _position()

    def copy(a, q, core):
        slot = land_refs[a].at[2 * q + core]
        return pltpu.make_async_remote_copy(
            src_ref=slot, dst_ref=slot, send_sem=send_sems.at[NCHIP * a + q], recv_sem=recv_sems.at[NCHIP * a + q],
            device_id=(x, y, 1 - cc), device_id_type=_MESH)

    pairs = [(a, q) for a in range(len(land_refs)) for q in range(NCHIP)]
    return [copy(a, q, cc) for a, q in pairs], [copy(a, q, 1 - cc) for a, q in pairs]


def _sibling_forward_start(lands, name):
    n = len(lands)

    def body(*refs):
        sends, _ = _sibling_copies(refs[:n], refs[n], refs[n + 1])
        for cp in sends:
            cp.start()
        refs[-1][...] = jnp.zeros_like(refs[-1])

    hbm = pl.BlockSpec(memory_space=pltpu.HBM)
    sem = pl.BlockSpec(memory_space=pltpu.SEMAPHORE)
    out = pl.pallas_call(
        body, name=name,
        out_shape=(pltpu.SemaphoreType.DMA((NCHIP * n,)), pltpu.SemaphoreType.DMA((NCHIP * n,)),
                   *[pltpu.HBM(a.shape, a.dtype) for a in lands], jax.ShapeDtypeStruct((8, LANES), F32)),
        in_specs=[hbm] * n,
        out_specs=(sem, sem, *[hbm] * n, pl.BlockSpec(memory_space=pltpu.VMEM)),
        input_output_aliases={i: 2 + i for i in range(n)},
        compiler_params=pltpu.CompilerParams(has_side_effects=_XCHG_EFFECT),
    )(*lands)
    return out[:-1], out[-1][0, 0]


def _sibling_forward_wait(handles, after, name):
    n = len(handles) - 2

    def body(*refs):
        sends, arrivals = _sibling_copies(refs[:n], refs[n], refs[n + 1])
        for cp in sends:
            cp.wait_send()
        for cp in arrivals:
            cp.wait_recv()

    hbm = pl.BlockSpec(memory_space=pltpu.HBM)
    sem = pl.BlockSpec(memory_space=pltpu.SEMAPHORE)
    lands = list(handles[2:])
    return list(pl.pallas_call(
        body, name=name,
        out_shape=tuple(pltpu.HBM(a.shape, a.dtype) for a in lands),
        in_specs=[hbm] * n + [sem, sem, pl.BlockSpec(memory_space=pl.ANY)],
        out_specs=tuple([hbm] * n),
        input_output_aliases={i: i for i in range(n)},
        compiler_params=pltpu.CompilerParams(has_side_effects=_XCHG_EFFECT),
    )(*lands, *handles[:2], after))


def _slab_sum(slabs, name, tr=None):
    n, r, c = slabs.shape
    tr = r if tr is None else tr

    def body(s_ref, o_ref):
        acc = s_ref[0].astype(F32)
        for q in range(1, n):
            acc = acc + s_ref[q].astype(F32)
        o_ref[...] = acc

    return pl.pallas_call(body, name=name, grid=(r // tr,),
                          in_specs=[pl.BlockSpec((n, tr, c), lambda i: (0, i, 0))],
                          out_specs=pl.BlockSpec((tr, c), lambda i: (i, 0)),
                          out_shape=jax.ShapeDtypeStruct((r, c), F32),
                          compiler_params=_cparams(dimension_semantics=("parallel",)))(slabs)


def _ada_fwd(c_all, ada_w, kv_ada_w, logits):
    rows, d = c_all.shape
    n0, nkv = ada_w.shape[2], kv_ada_w.shape[1]

    def body(c_ref, w_ref, kw_ref, lg_ref, part_ref, cact_ref, lb_ref):
        ca = _silu(c_ref[...])
        cact_ref[...] = ca
        part_ref[:, 0:n0] = _bdot_raw(ca, w_ref[0], _NN)
        part_ref[:, n0:2 * n0] = _bdot_raw(ca, w_ref[1], _NN)
        part_ref[:, 2 * n0:2 * n0 + nkv] = _bdot_raw(ca, kw_ref[...], _NN)
        lb_ref[...] = _sigmoid(lg_ref[0:1, :] - lg_ref[1:2, :])

    vm = pl.BlockSpec(memory_space=pltpu.VMEM)
    return pl.pallas_call(
        body, name="ada_fwd", in_specs=[vm, vm, vm, vm], out_specs=[vm, vm, vm],
        out_shape=[jax.ShapeDtypeStruct((rows, 2 * n0 + nkv), F32), jax.ShapeDtypeStruct((rows, d), F32),
                   jax.ShapeDtypeStruct((1, d), F32)],
        compiler_params=_cparams(),
    )(c_all, ada_w, kv_ada_w, logits)


def _ada_bwd(c_act, dm0, dm1, dkv, lb, dlb):
    rows, d = c_act.shape

    def body(c_ref, d0_ref, d1_ref, dk_ref, lb_ref, dlb_ref, dw_ref, dkw_ref, dlg_ref):
        ca = c_ref[...]
        dw_ref[0] = _bdot_raw(ca, d0_ref[...], _TN)
        dw_ref[1] = _bdot_raw(ca, d1_ref[...], _TN)
        dkw_ref[...] = _bdot_raw(ca, dk_ref[...], _TN)
        lbv = lb_ref[...]
        dl0 = dlb_ref[...] * lbv * (1.0 - lbv)
        dlg_ref[0:1, :] = dl0
        dlg_ref[1:2, :] = -dl0

    vm = pl.BlockSpec(memory_space=pltpu.VMEM)
    return pl.pallas_call(
        body, name="ada_bwd", in_specs=[vm] * 6, out_specs=[vm, vm, vm],
        out_shape=[jax.ShapeDtypeStruct((2, d, dm0.shape[1]), F32), jax.ShapeDtypeStruct((d, dkv.shape[1]), F32),
                   jax.ShapeDtypeStruct((2, d), F32)],
        compiler_params=_cparams(),
    )(c_act, dm0, dm1, dkv, lb, dlb)


def _adamw(w, g, m, v, name, tr=512, after=None):
    r, c = w.shape
    tr = _divisor_tile(r, tr, unit=8)
    c1 = 1.0 - ADAM_B1 ** ADAM_STEP
    c2 = 1.0 - ADAM_B2 ** ADAM_STEP
    deps = [] if after is None else [after]

    def body(w_ref, g_ref, m_ref, v_ref, *rest):
        d_ref, mo_ref, vo_ref = rest[len(deps):]
        gv = g_ref[...]
        mn = ADAM_B1 * m_ref[...] + (1.0 - ADAM_B1) * gv
        vn = ADAM_B2 * v_ref[...] + (1.0 - ADAM_B2) * (gv * gv)
        d_ref[...] = -ADAM_LR * ((mn / c1) / (jnp.sqrt(vn / c2) + ADAM_EPS) + ADAM_WD * w_ref[...])
        mo_ref[...] = mn
        vo_ref[...] = vn

    spec = pl.BlockSpec((tr, c), lambda i: (i, 0))
    out = jax.ShapeDtypeStruct((r, c), F32)
    return pl.pallas_call(body, name=name, grid=(r // tr,),
                          in_specs=[spec] * 4 + [pl.BlockSpec(a.shape, lambda i: (0, 0)) for a in deps],
                          out_specs=[spec] * 3, out_shape=[out, out, out],
                          compiler_params=_cparams(dimension_semantics=("parallel",)))(w, g, m, v, *deps)


def _pad_rows(a, rows):
    return jnp.pad(a, ((0, rows - a.shape[0]), (0, 0)))


def _pack_small(parts, lanes=LANES, row_unit=8):
    flat = jnp.concatenate([p.reshape(-1).astype(F32) for p in parts])
    rows = _round_up(-(-flat.shape[0] // lanes), row_unit)
    return jnp.pad(flat, (0, rows * lanes - flat.shape[0])).reshape(rows, lanes)


def _unpack_small(flat, shapes):
    out, off = [], 0
    for s in shapes:
        n = 1
        for k in s:
            n *= k
        out.append(flat[off:off + n].reshape(s))
        off += n
    return out


def _pad_shard_cols(a, n_loc, n_pad):
    lead = a.shape[:-1]
    a = a.reshape(*lead, NDEV, n_loc)
    a = jnp.pad(a, [(0, 0)] * (len(lead) + 1) + [(0, n_pad - n_loc)])
    return a.reshape(*lead, NDEV * n_pad)


def _unpad_shard_cols(a, n_loc, n_pad):
    lead = a.shape[:-1]
    return a.reshape(*lead, NDEV, n_pad)[..., :n_loc].reshape(*lead, NDEV * n_loc)


def kernel(x, c, ada_w, ada_b, a_w_in, a_lb_logits, a_norm_g, a_w_out, kv_ada_w, kv_ada_b, kv_w, kv_b_f, k_norm_g, b_w_q, q_norm_g, b_w_out, ffn_w_up, ffn_conv_w, ffn_conv_b, ffn_w_down, loss_target, m_ada_w, m_ada_b, m_a_w_in, m_a_lb_logits, m_a_norm_g, m_a_w_out, m_kv_ada_w, m_kv_ada_b, m_kv_w, m_kv_b_f, m_k_norm_g, m_b_w_q, m_q_norm_g, m_b_w_out, m_ffn_w_up, m_ffn_conv_w, m_ffn_conv_b, m_ffn_w_down, v_ada_w, v_ada_b, v_a_w_in, v_a_lb_logits, v_a_norm_g, v_a_w_out, v_kv_ada_w, v_kv_ada_b, v_kv_w, v_kv_b_f, v_k_norm_g, v_b_w_q, v_q_norm_g, v_b_w_out, v_ffn_w_up, v_ffn_conv_w, v_ffn_conv_b, v_ffn_w_down):
    t, d = x.shape[1], x.shape[2]
    nh = d // HEAD
    ncw = ffn_w_up.shape[2]
    ncp = _round_up(ncw, LANES)
    two_f = ncw * NDEV
    ff = two_f // 2
    fp = ncp * NDEV // 2
    rd = ffn_w_down.shape[1]
    me = 4 * lax.axis_index("x") + 2 * lax.axis_index("y") + lax.axis_index("c")
    weights = dict(ada_w=ada_w, ada_b=ada_b, a_w_in=a_w_in, a_lb_logits=a_lb_logits, a_norm_g=a_norm_g,
                   a_w_out=a_w_out, kv_ada_w=kv_ada_w, kv_ada_b=kv_ada_b, kv_w=kv_w, kv_b_f=kv_b_f,
                   k_norm_g=k_norm_g, b_w_q=b_w_q, q_norm_g=q_norm_g, b_w_out=b_w_out, ffn_w_up=ffn_w_up,
                   ffn_conv_w=ffn_conv_w, ffn_conv_b=ffn_conv_b, ffn_w_down=ffn_w_down)
    m_in = dict(ada_w=m_ada_w, ada_b=m_ada_b, a_w_in=m_a_w_in, a_lb_logits=m_a_lb_logits, a_norm_g=m_a_norm_g,
                a_w_out=m_a_w_out, kv_ada_w=m_kv_ada_w, kv_ada_b=m_kv_ada_b, kv_w=m_kv_w, kv_b_f=m_kv_b_f,
                k_norm_g=m_k_norm_g, b_w_q=m_b_w_q, q_norm_g=m_q_norm_g, b_w_out=m_b_w_out, ffn_w_up=m_ffn_w_up,
                ffn_conv_w=m_ffn_conv_w, ffn_conv_b=m_ffn_conv_b, ffn_w_down=m_ffn_w_down)
    v_in = dict(ada_w=v_ada_w, ada_b=v_ada_b, a_w_in=v_a_w_in, a_lb_logits=v_a_lb_logits, a_norm_g=v_a_norm_g,
                a_w_out=v_a_w_out, kv_ada_w=v_kv_ada_w, kv_ada_b=v_kv_ada_b, kv_w=v_kv_w, kv_b_f=v_kv_b_f,
                k_norm_g=v_k_norm_g, b_w_q=v_b_w_q, q_norm_g=v_q_norm_g, b_w_out=v_b_w_out, ffn_w_up=v_ffn_w_up,
                ffn_conv_w=v_ffn_conv_w, ffn_conv_b=v_ffn_conv_b, ffn_w_down=v_ffn_w_down)
    order = list(weights)

    up_loc = jnp.pad(ffn_w_up, ((0, 0), (0, 0), (0, ncp - ncw))).astype(BF16)
    down_loc = ffn_w_down.astype(BF16)
    gather_names = {"l0b": ["a_out", "up0", "down0"], "l1": ["kv", "b_q", "b_out", "up1", "down1"]}
    shards = {"a_out": a_w_out[0].astype(BF16), "up0": up_loc[0], "down0": down_loc[0], "kv": kv_w.T.astype(BF16),
              "b_q": b_w_q[0].astype(BF16), "b_out": b_w_out[0].astype(BF16), "up1": up_loc[1],
              "down1": down_loc[1]}
    pre = _pack_small([c, a_lb_logits, ffn_conv_w])
    in_flight = {}
    pre_flight, _ = _xchg_start([pre], False, ALL_PEERS, pre, "gather_small_inputs_start")
    in_flight["l0a"], _ = _xchg_start([a_w_in[0].astype(BF16)], False, SAME_CORE, pre_flight[-1], "gather_l0a_start")
    (pre_all,) = _xchg_wait(pre_flight, shards["down1"], False, ALL_PEERS, "gather_small_inputs_wait")
    pre_all = pre_all.reshape(NDEV, -1)
    c_all = pre_all[:, :d]
    logits = pre_all[:, d:d + 2 * HEAD].reshape(NDEV, 2, HEAD).transpose(1, 0, 2).reshape(2, d)
    conv_w_full = pre_all[:, d + 2 * HEAD:d + 2 * HEAD + 2 * CONV_TAPS * ncw]
    conv_w_full = conv_w_full.reshape(NDEV, 2, CONV_TAPS, ncw).transpose(1, 2, 0, 3).reshape(2, CONV_TAPS, two_f)

    part, c_act, lb = _ada_fwd(_pad_rows(c_all, 2 * NDEV), ada_w, kv_ada_w, logits)
    (part_all,) = _all_gather([part[:NDEV]], "gather_adaln")
    mine = lax.dynamic_index_in_dim(part_all, me, axis=1, keepdims=False)
    n0, nkv = ada_w.shape[2], kv_ada_w.shape[1]
    mod_names = ["sh1", "sc1", "g1", "sh2", "sc2", "g2"]
    mods = {}
    for l in range(2):
        row = mine[:, l * n0:(l + 1) * n0].reshape(-1) + ada_b[l]
        for k, nm in enumerate(mod_names):
            mods[f"{nm}_{l}"] = row[k * d:(k + 1) * d].reshape(1, d)
    kvrow = mine[:, 2 * n0:2 * n0 + nkv].reshape(-1) + kv_ada_b
    mods["kv_sh"], mods["kv_sc"] = kvrow[:d].reshape(1, d), kvrow[d:].reshape(1, d)

    def start_gather(grp, dep):
        srcs = [shards[n] for n in gather_names[grp]]
        in_flight[grp], started = _xchg_start(srcs, False, SAME_CORE, dep, f"gather_{grp}_start")
        return started

    zero = start_gather("l0b", part_all)
    mods["sh1_0"] = mods["sh1_0"] + zero

    small = {"a_norm_g": a_norm_g, "k_norm_g": k_norm_g.reshape(1, HEAD), "q_norm_g": q_norm_g, "kv_b_f": kv_b_f}
    for l in range(2):
        small[f"conv_w{l}"] = _pad_shard_cols(conv_w_full[l], ncw, ncp).reshape(CONV_TAPS, 2, fp).transpose(1, 0, 2)
        small[f"conv_b{l}"] = _pad_shard_cols(ffn_conv_b[l], ncw, ncp).reshape(2, 1, fp)

    forwarding = {}

    def pre_w(grp, after):
        arrived = _xchg_wait(in_flight[grp], after, False, SAME_CORE, f"gather_{grp}_wait")
        forwarding[grp], started = _sibling_forward_start(arrived, f"gather_{grp}_to_sibling_start")
        return started

    def get_w(grp, after):
        if grp == "l0a":
            arrived = _xchg_wait(in_flight["l0a"], after, False, SAME_CORE, "gather_l0a_wait")
            handles, _ = _sibling_forward_start(arrived, "gather_l0a_to_sibling_start")
            return {"a_in": _sibling_forward_wait(handles, after, "gather_l0a_to_sibling_wait")[0]}
        full = _sibling_forward_wait(forwarding[grp], after, f"gather_{grp}_to_sibling_wait")
        if grp == "l0b":
            started = start_gather("l1", full[0])
            full[0] = full[0] + started.astype(full[0].dtype)
        got = dict(zip(gather_names[grp], full))
        out = {}
        for n, a in got.items():
            if n in ("a_out", "b_out"):
                out[n] = a.reshape(d, d)
            elif n in ("down0", "down1"):
                dn = a.reshape(NCHIP, ff // NCHIP, d)
                out[n] = jnp.pad(dn, ((0, 0), (0, ncp - ncw), (0, 0))).reshape(fp, d)
            elif n == "kv":
                kv_t = a.reshape(NDEV * kv_w.shape[1], d)
                out["kv_k"], out["kv_v"] = kv_t[:d], kv_t[d:2 * d]
                out["kv_f"] = jnp.pad(kv_t[2 * d:], ((0, LANES - nh), (0, 0)))
            else:
                out[n] = a
        return out

    scatter_flight, g_last = {}, {}

    def put_g(grp, gr):
        if grp == "l0a":
            g_last.update(gr)
            return zero
        if grp == "l1":
            g_kvw = jnp.concatenate([gr["kv_k"], gr["kv_v"], gr["kv_f"][:nh].astype(BF16)], axis=0)
            arrs = {"kv_w": g_kvw.reshape(NDEV, kv_w.shape[1], d), "b_w_q": gr["b_q"],
                    "b_w_out": gr["b_out"].reshape(NDEV, d // NDEV, d), "up1": gr["up1"],
                    "down1": gr["down1"].reshape(NCHIP, ncp, d)[:, :ncw].reshape(NDEV, rd, d)}
        else:
            arrs = {"a_w_out": gr["a_out"].reshape(NDEV, d // NDEV, d), "up0": gr["up0"],
                    "down0": gr["down0"].reshape(NCHIP, ncp, d)[:, :ncw].reshape(NDEV, rd, d)}
        srcs = list(arrs.values())
        handles, sent = _xchg_start(srcs, True, ALL_PEERS, srcs[0], f"scatter_{grp}_start")
        scatter_flight[grp] = (list(arrs), handles)
        return sent

    loss_v, grad_x, dmods, dlb, g = _local_step(x[0], loss_target[0], mods, lb, small, pre_w, get_w, put_g)

    g_sum = {}
    for grp in ("l1", "l0b"):
        names, handles = scatter_flight[grp]
        for nm, a in zip(names, _xchg_wait(handles, grad_x, True, ALL_PEERS, f"scatter_{grp}_wait")):
            g_sum[nm] = _slab_sum(a, f"rs_slab_sum_{nm}")

    def conv_w_grad(a):
        return _unpad_shard_cols(a.transpose(1, 0, 2).reshape(CONV_TAPS, 2 * fp), ncw, ncp)

    def conv_b_grad(a):
        return _unpad_shard_cols(a.reshape(2 * fp), ncw, ncp)

    dmod_vec = [dmods[f"{nm}_{l}"] for l in range(2) for nm in mod_names] + [dmods["kv_sh"], dmods["kv_sc"]]
    post = _pack_small(dmod_vec + [dlb, g["a_norm_g"], g["k_norm_g"], g["q_norm_g"],
                                   jnp.pad(g["kv_b_f"].reshape(-1), (0, LANES - nh)),
                                   conv_w_grad(g["conv_w0"]), conv_w_grad(g["conv_w1"]),
                                   conv_b_grad(g["conv_b0"]), conv_b_grad(g["conv_b1"]), loss_v])
    post_flight, _ = _xchg_start([post], False, ALL_PEERS, post, "gather_small_grads_start")
    a_in_flight, a_in_sent = _xchg_start([g_last["a_in"]], True, ALL_PEERS, post_flight[-1], "scatter_l0a_start")
    a_in_sent = a_in_sent.reshape(1, 1)
    grads = {
        "a_w_out": g_sum["a_w_out"].reshape(a_w_out.shape),
        "kv_w": g_sum["kv_w"].T,
        "b_w_q": g_sum["b_w_q"].reshape(b_w_q.shape),
        "b_w_out": g_sum["b_w_out"].reshape(b_w_out.shape),
        "ffn_w_up": jnp.stack([g_sum["up0"][:, :ncw], g_sum["up1"][:, :ncw]]),
        "ffn_w_down": jnp.stack([g_sum["down0"], g_sum["down1"]]),
    }
    delta, new_m, new_v = {}, {}, {}

    def adamw_matrix(n):
        shp = weights[n].shape
        two_d = lambda a: a.reshape(-1, shp[-1])
        dl, mn, vn = _adamw(two_d(weights[n]), two_d(grads[n]), two_d(m_in[n]), two_d(v_in[n]), f"adamw_{n}",
                            after=a_in_sent)
        delta[n], new_m[n], new_v[n] = dl.reshape(shp), mn.reshape(shp), vn.reshape(shp)

    for n in grads:
        adamw_matrix(n)
    (post_all,) = _xchg_wait(post_flight, new_v["ffn_w_down"], False, ALL_PEERS, "gather_small_grads_wait")
    tot = _slab_sum(post_all, "small_grad_sum").reshape(-1)
    nmod = 14 * d
    (t_mod, t_lb, t_ang, t_kng, t_qng, t_bf, t_cw, t_cb, t_loss) = _unpack_small(
        tot, [(nmod,), (1, d), (1, HEAD), (HEAD,), (1, HEAD), (LANES,), (2, CONV_TAPS, two_f), (2, two_f),
              (LANES,)])
    loss = t_loss[0]
    dm_all = post_all.reshape(NDEV, -1)[:, :nmod]
    dm0 = lax.dynamic_slice_in_dim(dm_all[:, :6 * d], me * n0, n0, axis=1)
    dm1 = lax.dynamic_slice_in_dim(dm_all[:, 6 * d:12 * d], me * n0, n0, axis=1)
    dkv = lax.dynamic_slice_in_dim(dm_all[:, 12 * d:], me * nkv, nkv, axis=1)
    g_ada_w, g_kv_ada_w, g_logits = _ada_bwd(c_act, _pad_rows(dm0, 2 * NDEV), _pad_rows(dm1, 2 * NDEV),
                                              _pad_rows(dkv, 2 * NDEV), lb, t_lb)

    grads.update({
        "ada_w": g_ada_w,
        "ada_b": t_mod[:12 * d].reshape(2, 6 * d),
        "a_lb_logits": lax.dynamic_slice_in_dim(g_logits, me * HEAD, HEAD, axis=1),
        "a_norm_g": t_ang,
        "kv_ada_w": g_kv_ada_w,
        "kv_ada_b": t_mod[12 * d:],
        "kv_b_f": t_bf[:nh],
        "k_norm_g": t_kng,
        "q_norm_g": t_qng,
        "ffn_conv_w": lax.dynamic_slice_in_dim(t_cw, me * ncw, ncw, axis=2),
        "ffn_conv_b": t_cb,
    })

    small_adam = [n for n in order if n not in delta and n not in ("ada_w", "kv_ada_w", "a_w_in")]
    packs = [_pack_small([src[n] for n in small_adam]) for src in (weights, grads, m_in, v_in)]
    outs = _adamw(*packs, "adamw_small", tr=packs[0].shape[0])
    shapes = [weights[n].shape for n in small_adam]
    for dst, o in zip((delta, new_m, new_v), outs):
        for n, a in zip(small_adam, _unpack_small(o.reshape(-1), shapes)):
            dst[n] = a
    adamw_matrix("ada_w")
    adamw_matrix("kv_ada_w")
    (landed,) = _xchg_wait(a_in_flight, new_v["kv_ada_w"], True, ALL_PEERS, "scatter_l0a_wait")
    grads["a_w_in"] = _slab_sum(landed, "rs_slab_sum_a_w_in").reshape(a_w_in.shape)
    adamw_matrix("a_w_in")

    return (loss, grad_x.reshape(x.shape), *[grads[n] for n in order], *[delta[n] for n in order],
            *[new_m[n] for n in order], *[new_v[n] for n in order])
```

```python
import functools

import jax
import jax.numpy as jnp
from jax import lax
from jax.experimental import pallas as pl
from jax.experimental.pallas import tpu as pltpu

F32 = jnp.float32
BF16 = jnp.bfloat16

NDEV = 8
NCHIP = 4
HEAD = 128
A_CHUNK = 64
CONV_TAPS = 3
EPS = 1e-6
NEG_INF = -1e30
LANES = 128
VMEM_LIMIT = 48 * 1024 * 1024

ADAM_LR = 0.001
ADAM_B1 = 0.9
ADAM_B2 = 0.999
ADAM_EPS = 1e-08
ADAM_WD = 0.01
ADAM_STEP = 10

_NN = (((1,), (0,)), ((), ()))
_NT = (((1,), (1,)), ((), ()))
_TN = (((0,), (0,)), ((), ()))
_MESH = pl.DeviceIdType.MESH


def _cparams(**kw):
    return pltpu.CompilerParams(vmem_limit_bytes=VMEM_LIMIT, **kw)


def _divisor_tile(n, pref, unit=LANES):
    if n <= pref:
        return n
    best = None
    for t in range(unit, pref + 1, unit):
        if n % t == 0:
            best = t
    assert best is not None, (n, pref)
    return best


def _round_up(n, unit):
    return -(-n // unit) * unit


def _bdot_raw(a, b, dims):
    return lax.dot_general(a.astype(BF16), b.astype(BF16), dims, preferred_element_type=F32)


@jax.custom_vjp
def _dot_nn(a, b):
    return _bdot_raw(a, b, _NN)


@jax.custom_vjp
def _dot_nt(a, b):
    return _bdot_raw(a, b, _NT)


@jax.custom_vjp
def _dot_tn(a, b):
    return _bdot_raw(a, b, _TN)


_dot_nn.defvjp(lambda a, b: (_bdot_raw(a, b, _NN), (a, b)),
               lambda r, g: (_dot_nt(g, r[1]), _dot_tn(r[0], g)))
_dot_nt.defvjp(lambda a, b: (_bdot_raw(a, b, _NT), (a, b)),
               lambda r, g: (_dot_nn(g, r[1]), _dot_tn(g, r[0])))
_dot_tn.defvjp(lambda a, b: (_bdot_raw(a, b, _TN), (a, b)),
               lambda r, g: (_dot_nt(r[1], g), _dot_nn(r[0], g)))


def _f32dot(a, b):
    return lax.dot_general(a, b, _NN, precision=lax.Precision.HIGHEST, preferred_element_type=F32)


def _sigmoid(x):
    return jax.nn.sigmoid(x)


def _silu(x):
    return x * jax.nn.sigmoid(x)


def _rms(x):
    return x * lax.rsqrt(jnp.mean(x * x, axis=-1, keepdims=True) + EPS)


def _modulate(x, sh, sc):
    return _rms(x) * (1.0 + sc) + sh


def _mm_call(a, b, dims, a_spec, b_spec, o_spec, o_shape, grid, acc_tile, name):
    nk = grid[2]

    def body(a_ref, b_ref, o_ref, *acc):
        p = lax.dot_general(a_ref[...].astype(BF16), b_ref[...].astype(BF16), dims,
                            preferred_element_type=F32)
        if nk == 1:
            o_ref[...] = p.astype(o_ref.dtype)
        else:
            kk = pl.program_id(2)

            @pl.when(kk == 0)
            def _():
                acc[0][...] = p

            @pl.when(kk > 0)
            def _():
                acc[0][...] += p

            @pl.when(kk == nk - 1)
            def _():
                o_ref[...] = acc[0][...].astype(o_ref.dtype)

    return pl.pallas_call(
        body, name=name, grid=grid, in_specs=[a_spec, b_spec], out_specs=o_spec, out_shape=o_shape,
        scratch_shapes=[pltpu.VMEM(acc_tile, F32)] if nk > 1 else [],
        compiler_params=_cparams(dimension_semantics=("parallel", "parallel", "arbitrary")),
    )(a, b)


def _mm(a, b, mode, out_dtype, name, tm=1024, tn=1024, tk=2048):
    if mode == "nn":
        (m, k), (k2, n) = a.shape, b.shape
    elif mode == "nt":
        (m, k), (n, k2) = a.shape, b.shape
    else:
        (k, m), (k2, n) = a.shape, b.shape
    assert k == k2, (a.shape, b.shape, mode)
    tm, tn, tk = _divisor_tile(m, tm), _divisor_tile(n, tn), _divisor_tile(k, tk)
    if mode == "tn":
        a_spec = pl.BlockSpec((tk, tm), lambda i, j, kk: (kk, i))
    else:
        a_spec = pl.BlockSpec((tm, tk), lambda i, j, kk: (i, kk))
    if mode == "nt":
        b_spec = pl.BlockSpec((tn, tk), lambda i, j, kk: (j, kk))
    else:
        b_spec = pl.BlockSpec((tk, tn), lambda i, j, kk: (kk, j))
    return _mm_call(a, b, {"nn": _NN, "nt": _NT, "tn": _TN}[mode], a_spec, b_spec,
                    pl.BlockSpec((tm, tn), lambda i, j, kk: (i, j)), jax.ShapeDtypeStruct((m, n), out_dtype),
                    (m // tm, n // tn, k // tk), (tm, tn), name)


def _wblk_act_spec(rows, gb, nl, split, nb, row_axis, blk_axis):
    if split == 1:
        return pl.BlockSpec((rows, gb * nl), lambda *g: (g[row_axis], g[blk_axis]))
    groups = nb // split // gb
    return pl.BlockSpec((None, rows, gb * nl),
                        lambda *g: (g[blk_axis] // groups, g[row_axis], g[blk_axis] % groups))


def _mm_wblk(a, wb, out_dtype, name, *, gb, row_off=0, split=1, tm=1024):
    m, k = a.shape
    nb, _, nl = wb.shape
    assert (nb // split) % gb == 0
    tm = _divisor_tile(m, tm)

    def body(a_ref, b_ref, o_ref):
        av = a_ref[...].astype(BF16)
        for s in range(gb):
            o_ref[:, s * nl:(s + 1) * nl] = lax.dot_general(
                av, b_ref[s].astype(BF16), _NN, preferred_element_type=F32).astype(o_ref.dtype)

    o_shape = (m, nb * nl) if split == 1 else (split, m, nb // split * nl)
    return pl.pallas_call(
        body, name=name, grid=(nb // gb, m // tm),
        in_specs=[pl.BlockSpec((tm, k), lambda j, i: (i, 0)),
                  pl.BlockSpec((gb, k, nl), lambda j, i: (j, row_off, 0))],
        out_specs=_wblk_act_spec(tm, gb, nl, split, nb, 1, 0),
        out_shape=jax.ShapeDtypeStruct(o_shape, out_dtype),
        compiler_params=_cparams(dimension_semantics=("parallel", "parallel")),
    )(a, wb)


def _mm_wblk_dx(dy, wb, out_dtype, name, *, k, gb, row_off=0, split=1, tm=1024):
    nb, _, nl = wb.shape
    m = dy.shape[-2]
    tm = _divisor_tile(m, tm)
    nk = nb // gb
    per = nb // split
    whole = split > 1 and gb == nb
    assert whole or per % gb == 0

    def body(a_ref, b_ref, o_ref, *acc):
        p = None
        for s in range(gb):
            a_blk = a_ref[s // per, :, (s % per) * nl:(s % per + 1) * nl] if whole else a_ref[:, s * nl:(s + 1) * nl]
            q = lax.dot_general(a_blk.astype(BF16), b_ref[s].astype(BF16), _NT, preferred_element_type=F32)
            p = q if p is None else p + q
        if nk == 1:
            o_ref[...] = p.astype(o_ref.dtype)
        else:
            kk = pl.program_id(1)

            @pl.when(kk == 0)
            def _():
                acc[0][...] = p

            @pl.when(kk > 0)
            def _():
                acc[0][...] += p

            @pl.when(kk == nk - 1)
            def _():
                o_ref[...] = acc[0][...].astype(o_ref.dtype)

    return pl.pallas_call(
        body, name=name, grid=(m // tm, nk),
        in_specs=[pl.BlockSpec((split, tm, per * nl), lambda i, kk: (0, i, 0)) if whole
                  else _wblk_act_spec(tm, gb, nl, split, nb, 0, 1),
                  pl.BlockSpec((gb, k, nl), lambda i, kk: (kk, row_off, 0))],
        out_specs=pl.BlockSpec((tm, k), lambda i, kk: (i, 0)),
        out_shape=jax.ShapeDtypeStruct((m, k), out_dtype),
        scratch_shapes=[pltpu.VMEM((tm, k), F32)] if nk > 1 else [],
        compiler_params=_cparams(dimension_semantics=("parallel", "arbitrary")),
    )(dy, wb)


def _mm_wblk_dw(x, dy, name, *, nb, gb, split=1, tk=1024):
    t, k = x.shape
    assert (nb // split) % gb == 0
    nl = dy.shape[-1] * split // nb
    tk = _divisor_tile(t, tk)
    nk = t // tk

    def body(a_ref, b_ref, o_ref, *acc):
        kk = pl.program_id(1)
        av = a_ref[...].astype(BF16)
        for s in range(gb):
            p = lax.dot_general(av, b_ref[:, s * nl:(s + 1) * nl].astype(BF16), _TN, preferred_element_type=F32)
            if nk == 1:
                o_ref[s] = p.astype(o_ref.dtype)
                continue

            @pl.when(kk == 0)
            def _():
                acc[0][s] = p

            @pl.when(kk > 0)
            def _():
                acc[0][s] += p

        if nk > 1:
            @pl.when(kk == nk - 1)
            def _():
                o_ref[...] = acc[0][...].astype(o_ref.dtype)

    return pl.pallas_call(
        body, name=name, grid=(nb // gb, nk),
        in_specs=[pl.BlockSpec((tk, k), lambda j, kk: (kk, 0)), _wblk_act_spec(tk, gb, nl, split, nb, 1, 0)],
        out_specs=pl.BlockSpec((gb, k, nl), lambda j, kk: (j, 0, 0)),
        out_shape=jax.ShapeDtypeStruct((nb, k, nl), BF16),
        scratch_shapes=[pltpu.VMEM((gb, k, nl), F32)] if nk > 1 else [],
        compiler_params=_cparams(dimension_semantics=("parallel", "arbitrary")),
    )(x, dy)


def _row_specs(rows, tb, nsub):
    return [pl.BlockSpec((tb, nsub * cw), functools.partial(lambda i, off: (i, off), off=off))
            for (_, cw, off) in rows]


def _vec_specs(params):
    return [pl.BlockSpec(p.shape, lambda i: (0, 0)) for p in params]


def _row_fwd(f, rows, params, out_dtypes, *, nsub=1, tb, name):
    t = rows[0][0].shape[0]
    tb = min(tb, t)
    n_r, n_p = len(rows), len(params)
    blk = [jax.ShapeDtypeStruct((tb, cw), F32) for (_, cw, _) in rows]
    blk += [jax.ShapeDtypeStruct(p.shape, F32) for p in params]
    out_avals = jax.eval_shape(f, *blk)

    def body(*refs):
        pv = [r[...] for r in refs[n_r:n_r + n_p]]
        for s in range(nsub):
            vals = [r[:, s * cw:(s + 1) * cw].astype(F32) for r, (_, cw, _) in zip(refs[:n_r], rows)]
            outs = f(*vals, *pv)
            for o_ref, o in zip(refs[n_r + n_p:], outs):
                w = o.shape[1]
                o_ref[:, s * w:(s + 1) * w] = o.astype(o_ref.dtype)

    return pl.pallas_call(
        body, name=name,
        grid=(t // tb,),
        in_specs=_row_specs(rows, tb, nsub) + _vec_specs(params),
        out_specs=[pl.BlockSpec((tb, nsub * av.shape[1]), lambda i: (i, 0)) for av in out_avals],
        out_shape=[jax.ShapeDtypeStruct((t, nsub * av.shape[1]), dt) for av, dt in zip(out_avals, out_dtypes)],
        compiler_params=_cparams(dimension_semantics=("parallel",)),
    )(*[r[0] for r in rows], *params)


def _row_bwd(f, rows, params, cots, row_grad_dtypes, *, nsub=1, tb, name, add_to=None, cot_add=None):
    t = rows[0][0].shape[0]
    tb = min(tb, t)
    n_r, n_p, n_c = len(rows), len(params), len(cots)
    want = [j for j in range(n_r) if row_grad_dtypes[j] is not None]
    cot_add = cot_add or []
    extra = [] if add_to is None else [(add_to[1], rows[add_to[0]][1], 0)]
    n_add_to = len(extra)
    extra += [(arr, cots[ci][1], 0) for ci, arr in cot_add]

    def body(*refs):
        i = pl.program_id(0)
        r_in, p_in = refs[:n_r], refs[n_r:n_r + n_p]
        c_in = refs[n_r + n_p:n_r + n_p + n_c]
        e_in = refs[n_r + n_p + n_c:n_r + n_p + n_c + len(extra)]
        outs = refs[n_r + n_p + n_c + len(extra):]
        pv = [r[...] for r in p_in]
        psum = [None] * n_p
        for s in range(nsub):
            vals = [r[:, s * cw:(s + 1) * cw].astype(F32) for r, (_, cw, _) in zip(r_in, rows)]
            cvals = [r[:, s * cw:(s + 1) * cw].astype(F32) for r, (_, cw, _) in zip(c_in, cots)]
            for (ci, _), e_ref in zip(cot_add, e_in[n_add_to:]):
                cw = cots[ci][1]
                cvals[ci] = cvals[ci] + e_ref[:, s * cw:(s + 1) * cw].astype(F32)
            _, vjp_fn = jax.vjp(f, *vals, *pv)
            grads = vjp_fn(tuple(cvals))
            for o_ref, jr in zip(outs[:len(want)], want):
                cw = rows[jr][1]
                gr = grads[jr]
                if add_to is not None and jr == add_to[0]:
                    gr = gr + e_in[0][:, s * cw:(s + 1) * cw]
                o_ref[:, s * cw:(s + 1) * cw] = gr.astype(o_ref.dtype)
            for jp in range(n_p):
                psum[jp] = grads[n_r + jp] if psum[jp] is None else psum[jp] + grads[n_r + jp]
        for o_ref, g in zip(outs[len(want):], psum):
            @pl.when(i == 0)
            def _():
                o_ref[...] = g

            @pl.when(i > 0)
            def _():
                o_ref[...] += g

    out_specs = [pl.BlockSpec((tb, nsub * rows[jr][1]), lambda i: (i, 0)) for jr in want]
    out_shape = [jax.ShapeDtypeStruct((t, nsub * rows[jr][1]), row_grad_dtypes[jr]) for jr in want]
    out_specs += _vec_specs(params)
    out_shape += [jax.ShapeDtypeStruct(p.shape, F32) for p in params]
    res = pl.pallas_call(
        body, name=name,
        grid=(t // tb,),
        in_specs=_row_specs(rows, tb, nsub) + _vec_specs(params) + _row_specs(cots, tb, nsub)
        + _row_specs(extra, tb, nsub),
        out_specs=out_specs, out_shape=out_shape,
        compiler_params=_cparams(dimension_semantics=("arbitrary",)),
    )(*[r[0] for r in rows], *params, *[c[0] for c in cots], *[e[0] for e in extra])
    return res[:len(want)], res[len(want):]


def _f_mod(x, sh, sc):
    return (_modulate(x, sh, sc),)


def _f_res_mod(x, y, g, sh, sc):
    x1 = x + g * y
    return x1, _modulate(x1, sh, sc)


def _f_res_mod2(x, y, g, sh_a, sc_a, sh_b, sc_b):
    x1 = x + g * y
    return x1, _modulate(x1, sh_a, sc_a), _modulate(x1, sh_b, sc_b)


def _f_qnorm(p, g):
    return (_rms(p) * g * (HEAD ** -0.5),)


def _f_knorm(p, g):
    return (_rms(p) * g,)


def _f_qnorm_aug(p, g):
    lane = lax.broadcasted_iota(jnp.int32, p.shape, 1)
    return (jnp.concatenate([_rms(p) * g * (HEAD ** -0.5), jnp.where(lane < 3, 1.0, 0.0)], axis=1),)


def _f_knorm_aug(p, c0, c1, c2, g):
    lane = lax.broadcasted_iota(jnp.int32, p.shape, 1)
    aug = jnp.where(lane == 0, c0, jnp.where(lane == 1, c1, jnp.where(lane == 2, c2, 0.0)))
    return (jnp.concatenate([_rms(p) * g, aug], axis=1),)


def _split3(a):
    round_bf16 = lambda v: lax.reduce_precision(v, exponent_bits=8, mantissa_bits=7)
    hi = round_bf16(a)
    mid = round_bf16(a - hi)
    lo = round_bf16(a - hi - mid)
    return hi.astype(BF16), mid.astype(BF16), lo.astype(BF16)


def _f_outgate(o, og):
    return (o * _sigmoid(og),)


def _loss_call(x3, f, g2, target, tb):
    t, d = x3.shape
    tb = min(tb, t)

    def body(x_ref, f_ref, g_ref, t_ref, loss_ref, dx_ref, df_ref, dg_ref):
        i = pl.program_id(0)
        fv = f_ref[...]
        g = g_ref[...]
        e = x_ref[...] + g * fv - t_ref[...]
        dx = e * (1.0 / d)
        part = 0.5 * jnp.sum(jnp.sum(e * dx, axis=1, keepdims=True), axis=0, keepdims=True)
        dx_ref[...] = dx
        df_ref[...] = (g * dx).astype(df_ref.dtype)
        dg = jnp.sum(dx * fv, axis=0, keepdims=True)

        @pl.when(i == 0)
        def _():
            loss_ref[...] = jnp.broadcast_to(part, loss_ref.shape)
            dg_ref[...] = dg

        @pl.when(i > 0)
        def _():
            loss_ref[...] += jnp.broadcast_to(part, loss_ref.shape)
            dg_ref[...] += dg

    row = pl.BlockSpec((tb, d), lambda i: (i, 0))
    vec = pl.BlockSpec((1, d), lambda i: (0, 0))
    return pl.pallas_call(
        body, name="loss_head",
        grid=(t // tb,),
        in_specs=[row, row, vec, row],
        out_specs=[pl.BlockSpec((1, LANES), lambda i: (0, 0)), row, row, vec],
        out_shape=[jax.ShapeDtypeStruct((1, LANES), F32), jax.ShapeDtypeStruct((t, d), F32),
                   jax.ShapeDtypeStruct((t, d), BF16), jax.ShapeDtypeStruct((1, d), F32)],
        compiler_params=_cparams(dimension_semantics=("arbitrary",)),
    )(x3, f, g2, target)


def _hg_mask(tb):
    br = lax.broadcasted_iota(jnp.int32, (tb, tb), 0)
    bs = lax.broadcasted_iota(jnp.int32, (tb, tb), 1)
    return jnp.logical_and(br // A_CHUNK == bs // A_CHUNK, bs <= br).astype(F32)


def _hg_consts(mask):
    c = A_CHUNK
    r = lax.broadcasted_iota(jnp.int32, (c, c), 0)
    s = lax.broadcasted_iota(jnp.int32, (c, c), 1)
    return (s <= r).astype(F32), (r <= s).astype(F32), mask > 0.5


def _chunk_apply(mat, x):
    c = mat.shape[0]
    return jnp.concatenate([_f32dot(mat, x[i * c:(i + 1) * c]) for i in range(x.shape[0] // c)], axis=0)


@jax.custom_vjp
def _chunk_cumsum(x, tri, tri_t):
    return _chunk_apply(tri, x)


_chunk_cumsum.defvjp(lambda x, tri, tri_t: (_chunk_apply(tri, x), (tri, tri_t)),
                     lambda r, g: (_chunk_apply(r[1], g), jnp.zeros_like(r[0]), jnp.zeros_like(r[1])))


def _per_chunk(a, b, dims):
    return jnp.stack([_bdot_raw(a[i], b[i], dims) for i in range(a.shape[0])])


@jax.custom_vjp
def _chunk_tn(a, b):
    return _per_chunk(a, b, _TN)


@jax.custom_vjp
def _chunk_nt(a, b):
    return _per_chunk(a, b, _NT)


@jax.custom_vjp
def _chunk_nn(a, b):
    return _per_chunk(a, b, _NN)


_chunk_tn.defvjp(lambda a, b: (_per_chunk(a, b, _TN), (a, b)),
                 lambda r, g: (_chunk_nt(r[1], g), _chunk_nn(r[0], g)))
_chunk_nt.defvjp(lambda a, b: (_per_chunk(a, b, _NT), (a, b)),
                 lambda r, g: (_chunk_nn(g, r[1]), _chunk_tn(g, r[0])))
_chunk_nn.defvjp(lambda a, b: (_per_chunk(a, b, _NN), (a, b)),
                 lambda r, g: (_chunk_nt(g, r[1]), _chunk_tn(r[0], g)))


def _scan_states(decay, m, st):
    sts = []
    for i in range(m.shape[0]):
        sts.append(st)
        st = st * decay[i] + m[i]
    return jnp.stack(sts), st


@jax.custom_vjp
def _state_scan(decay, m, st):
    return _scan_states(decay, m, st)


def _state_scan_fwd(decay, m, st):
    sts, st_out = _scan_states(decay, m, st)
    return (sts, st_out), (decay, sts)


def _state_scan_bwd(res, cts):
    decay, sts = res
    d_sts, g = cts
    d_decay, d_m = [], []
    for i in range(sts.shape[0] - 1, -1, -1):
        d_m.append(g)
        d_decay.append(jnp.sum(g * sts[i], axis=0, keepdims=True))
        g = g * decay[i] + d_sts[i]
    return jnp.stack(d_decay[::-1]), jnp.stack(d_m[::-1]), g


_state_scan.defvjp(_state_scan_fwd, _state_scan_bwd)


def _hg_block(qp, fp, ip, gp, lb, ng, st, tri, tri_t, bd_causal):
    tb = qp.shape[0]
    c = A_CHUNK
    n = tb // c
    q = _silu(qp)
    fg = lb + (1.0 - lb) * _sigmoid(fp)
    logf = jnp.log(fg)
    k = 1.0 - fg
    b3 = _chunk_cumsum(logf, tri, tri_t).reshape(n, c, HEAD)
    pos = lax.broadcasted_iota(jnp.int32, (1, c, 1), 1)
    b_mid = lax.stop_gradient(jnp.sum(jnp.where(pos == c // 2, b3, 0.0), axis=1, keepdims=True))
    b_last = jnp.sum(jnp.where(pos == c - 1, b3, 0.0), axis=1, keepdims=True)
    q3, k3, v3 = q.reshape(n, c, HEAD), k.reshape(n, c, HEAD), ip.reshape(n, c, HEAD)
    scores = _dot_nt((q3 * jnp.exp(b3 - b_mid)).reshape(tb, HEAD), (k3 * jnp.exp(b_mid - b3)).reshape(tb, HEAD))
    o_intra = _dot_nn(jnp.where(bd_causal, scores, 0.0), ip)
    states, st_new = _state_scan(jnp.exp(b_last), _chunk_tn(v3, k3 * jnp.exp(b_last - b3)), st)
    o = o_intra + _chunk_nt(q3 * jnp.exp(b3), states).reshape(tb, HEAD)
    y = _rms(o) * ng * _silu(gp)
    return y, st_new


HG_HEADS = 2


def _hg_specs(tb, nh, rev_nb=None):
    wide = HG_HEADS * HEAD
    per = nh // HG_HEADS

    def row(part):
        if rev_nb is None:
            return pl.BlockSpec((tb, wide), functools.partial(lambda h, i, off: (i, off + h), off=part * per))
        return pl.BlockSpec((tb, wide),
                            functools.partial(lambda h, i, off: (rev_nb - 1 - i, off + h), off=part * per))
    return [row(0), row(1), row(2), row(3),
            pl.BlockSpec((1, wide), lambda h, i: (0, h)), pl.BlockSpec((1, HEAD), lambda h, i: (0, 0)),
            pl.BlockSpec((tb, tb), lambda h, i: (0, 0))]


def _hgrn2_fwd(proj, lb, ng, tb):
    t = proj.shape[0]
    nh = proj.shape[1] // (4 * HEAD)
    tb = min(tb, t)
    nb = t // tb
    wide = HG_HEADS * HEAD

    def body(q_ref, f_ref, i_ref, g_ref, lb_ref, ng_ref, mask_ref, y_ref, s_ref, st_ref):
        i = pl.program_id(1)

        @pl.when(i == 0)
        def _():
            st_ref[...] = jnp.zeros_like(st_ref)

        consts = _hg_consts(mask_ref[...])
        for p in range(HG_HEADS):
            cs = slice(p * HEAD, (p + 1) * HEAD)
            st = st_ref[p]
            s_ref[p, 0] = st
            y, st_new = _hg_block(q_ref[:, cs], f_ref[:, cs], i_ref[:, cs], g_ref[:, cs], lb_ref[:, cs],
                                  ng_ref[...], st, *consts)
            y_ref[:, cs] = y.astype(y_ref.dtype)
            st_ref[p] = st_new

    return pl.pallas_call(
        body, name="hgrn2_fwd",
        grid=(nh // HG_HEADS, nb),
        in_specs=_hg_specs(tb, nh),
        out_specs=[pl.BlockSpec((tb, wide), lambda h, i: (i, h)),
                   pl.BlockSpec((HG_HEADS, 1, HEAD, HEAD), lambda h, i: (h, i, 0, 0))],
        out_shape=[jax.ShapeDtypeStruct((t, nh * HEAD), BF16),
                   jax.ShapeDtypeStruct((nh, nb, HEAD, HEAD), F32)],
        scratch_shapes=[pltpu.VMEM((HG_HEADS, HEAD, HEAD), F32)],
        compiler_params=_cparams(dimension_semantics=("parallel", "arbitrary")),
    )(proj, proj, proj, proj, lb, ng, _hg_mask(tb))


def _hgrn2_bwd(proj, lb, ng, states, dy, tb):
    t = proj.shape[0]
    nh = proj.shape[1] // (4 * HEAD)
    tb = min(tb, t)
    nb = t // tb
    wide = HG_HEADS * HEAD

    def body(q_ref, f_ref, i_ref, g_ref, lb_ref, ng_ref, mask_ref, s_ref, dy_ref,
             dp_ref, dlb_ref, dng_ref, dst_ref):
        h, i = pl.program_id(0), pl.program_id(1)
        consts = _hg_consts(mask_ref[...])

        @pl.when(i == 0)
        def _():
            dst_ref[...] = jnp.zeros_like(dst_ref)
            dlb_ref[...] = jnp.zeros_like(dlb_ref)

        @pl.when(jnp.logical_and(i == 0, h == 0))
        def _():
            dng_ref[...] = jnp.zeros_like(dng_ref)

        def fn(qp, fp, ip, gp, lbx, ngx, stx):
            return _hg_block(qp, fp, ip, gp, lbx, ngx, stx, *consts)

        for p in range(HG_HEADS):
            cs = slice(p * HEAD, (p + 1) * HEAD)
            _, vjp_fn = jax.vjp(fn, q_ref[:, cs], f_ref[:, cs], i_ref[:, cs], g_ref[:, cs], lb_ref[:, cs],
                                ng_ref[...], s_ref[p, 0])
            *gparts, glb, gng, dst = vjp_fn((dy_ref[:, cs].astype(F32), dst_ref[p]))
            for part, gpart in enumerate(gparts):
                dp_ref[part, :, cs] = gpart.astype(dp_ref.dtype)
            dst_ref[p] = dst
            dlb_ref[:, cs] += glb
            dng_ref[...] += gng

    rev = lambda h, i: (nb - 1 - i, h)
    return pl.pallas_call(
        body, name="hgrn2_bwd",
        grid=(nh // HG_HEADS, nb),
        in_specs=_hg_specs(tb, nh, rev_nb=nb) + [
            pl.BlockSpec((HG_HEADS, 1, HEAD, HEAD), lambda h, i: (h, nb - 1 - i, 0, 0)),
            pl.BlockSpec((tb, wide), rev)],
        out_specs=[pl.BlockSpec((4, tb, wide), lambda h, i: (0, nb - 1 - i, h)),
                   pl.BlockSpec((1, wide), lambda h, i: (0, h)), pl.BlockSpec((1, HEAD), lambda h, i: (0, 0))],
        out_shape=[jax.ShapeDtypeStruct((4, t, nh * HEAD), BF16),
                   jax.ShapeDtypeStruct((1, nh * HEAD), F32), jax.ShapeDtypeStruct((1, HEAD), F32)],
        scratch_shapes=[pltpu.VMEM((HG_HEADS, HEAD, HEAD), F32)],
        compiler_params=_cparams(dimension_semantics=("arbitrary", "arbitrary")),
    )(proj, proj, proj, proj, lb, ng, _hg_mask(tb), states, dy)


def _fgate_consts(cb):
    r = lax.broadcasted_iota(jnp.int32, (cb, cb), 0)
    s = lax.broadcasted_iota(jnp.int32, (cb, cb), 1)
    return (r <= s).astype(F32), (r >= s).astype(F32)


def _fgate_fwd(xt, bias, cb=512):
    nh, t = xt.shape
    cb = min(cb, t)

    def body(x_ref, b_ref, o_ref):
        upper, _ = _fgate_consts(cb)
        carry = jnp.zeros((nh, 1), F32)
        for blk in range(t // cb):
            z = x_ref[:, blk * cb:(blk + 1) * cb] + b_ref[...]
            logf = jnp.minimum(z, 0.0) - jnp.log(1.0 + jnp.exp(-jnp.abs(z)))
            cs = _f32dot(logf, upper) + carry
            o_ref[:, blk * cb:(blk + 1) * cb] = cs
            carry = cs[:, cb - 1:cb]

    vm = pl.BlockSpec(memory_space=pltpu.VMEM)
    return pl.pallas_call(
        body, name="fgate_fwd", in_specs=[vm, vm], out_specs=vm,
        out_shape=jax.ShapeDtypeStruct((nh, t), F32), compiler_params=_cparams(),
    )(xt, bias)


def _fgate_bwd(xt, bias, dft, cb=512):
    nh, t = xt.shape
    cb = min(cb, t)
    nblk = t // cb

    def body(x_ref, b_ref, d_ref, dx_ref, db_ref):
        _, lower = _fgate_consts(cb)
        carry = jnp.zeros((nh, 1), F32)
        db = jnp.zeros((nh, 1), F32)
        for blk in range(nblk - 1, -1, -1):
            sl = slice(blk * cb, (blk + 1) * cb)
            dlogf = _f32dot(d_ref[:, sl], lower) + carry
            carry = dlogf[:, 0:1]
            z = x_ref[:, sl] + b_ref[...]
            dz = dlogf * (1.0 - _sigmoid(z))
            dx_ref[:, sl] = dz
            db = db + jnp.sum(dz, axis=1, keepdims=True)
        db_ref[...] = db

    vm = pl.BlockSpec(memory_space=pltpu.VMEM)
    return pl.pallas_call(
        body, name="fgate_bwd", in_specs=[vm, vm, vm], out_specs=[vm, vm],
        out_shape=[jax.ShapeDtypeStruct((nh, t), F32), jax.ShapeDtypeStruct((nh, 1), F32)],
        compiler_params=_cparams(),
    )(xt, bias, dft)


ATTN_GROUPS = 4
ATTN_FWD_HEADS = 2


def _attn_fwd(q, k, v, f_grp, blk):
    t, width = v.shape
    nh = width // HEAD
    nq = t // blk
    hpg = nh // ATTN_GROUPS

    def body(q_ref, k_ref, v_ref, fc_ref, o_ref, lse_ref):
        i = pl.program_id(0)
        tri = (lax.broadcasted_iota(jnp.int32, (blk, blk), 1) <= lax.broadcasted_iota(jnp.int32, (blk, blk), 0))
        for h0 in range(0, nh, ATTN_FWD_HEADS):
            heads = range(h0, min(h0 + ATTN_FWD_HEADS, nh))

            def tile(j, carries, masked):
                rs = pl.ds(pl.multiple_of(j * blk, blk), blk)
                out = []
                for h, (m, l, acc) in zip(heads, carries):
                    cs = slice(h * HEAD, (h + 1) * HEAD)
                    cs2 = slice(2 * h * HEAD, 2 * (h + 1) * HEAD)
                    s = _bdot_raw(q_ref[:, cs2], k_ref[rs, cs2], _NT)
                    if masked:
                        s = jnp.where(tri, s, NEG_INF)
                    m_new = jnp.maximum(m, jnp.max(s, axis=1, keepdims=True))
                    p = jnp.exp(s - m_new)
                    alpha = jnp.exp(m - m_new)
                    l_new = alpha * l + jnp.sum(p, axis=1, keepdims=True)
                    out.append((m_new, l_new, alpha * acc + _bdot_raw(p, v_ref[rs, cs], _NN)))
                return tuple(out)

            init = tuple((jnp.full((blk, 1), NEG_INF, F32), jnp.zeros((blk, 1), F32), jnp.zeros((blk, HEAD), F32))
                         for _ in heads)
            carries = lax.fori_loop(0, i, lambda j, c: tile(j, c, False), init)
            for h, (m, l, acc) in zip(heads, tile(i, carries, True)):
                o_ref[:, h * HEAD:(h + 1) * HEAD] = acc / l
                g, hh = divmod(h, hpg)
                lse_ref[g, :, hh:hh + 1] = m + jnp.log(l) + fc_ref[g, :, hh:hh + 1]

    vm = pl.BlockSpec(memory_space=pltpu.VMEM)
    stat = pl.BlockSpec((ATTN_GROUPS, blk, hpg), lambda i: (0, i, 0))
    return pl.pallas_call(
        body, name="fox_attn_fwd",
        grid=(nq,),
        in_specs=[pl.BlockSpec((blk, 2 * width), lambda i: (i, 0)), vm, vm, stat],
        out_specs=[pl.BlockSpec((blk, width), lambda i: (i, 0)), stat],
        out_shape=[jax.ShapeDtypeStruct((t, width), F32), jax.ShapeDtypeStruct((ATTN_GROUPS, t, hpg), F32)],
        compiler_params=_cparams(dimension_semantics=("parallel",)),
    )(q, k, v, f_grp)


def _attn_delta(do, o, tb):
    t, width = o.shape
    nh = width // HEAD
    hpg = nh // ATTN_GROUPS
    tb = min(tb, t)

    def body(do_ref, o_ref, dl_ref):
        for h in range(nh):
            cs = slice(h * HEAD, (h + 1) * HEAD)
            g, hh = divmod(h, hpg)
            dl_ref[g, :, hh:hh + 1] = jnp.sum(do_ref[:, cs].astype(F32) * o_ref[:, cs], axis=1, keepdims=True)

    wide = pl.BlockSpec((tb, width), lambda i: (i, 0))
    return pl.pallas_call(body, name="fox_attn_delta", grid=(t // tb,), in_specs=[wide, wide],
                          out_specs=pl.BlockSpec((ATTN_GROUPS, tb, hpg), lambda i: (0, i, 0)),
                          out_shape=jax.ShapeDtypeStruct((ATTN_GROUPS, t, hpg), F32),
                          compiler_params=_cparams(dimension_semantics=("parallel",)))(do, o)


def _attn_bwd(q, k, v, f_grp, do, lse, delta, blk):
    t, width = v.shape
    nh = width // HEAD
    nq = t // blk
    hpg = nh // ATTN_GROUPS
    gw = hpg * HEAD

    def body(q_ref, do_ref, k_ref, v_ref, fc_ref, lse_ref, dl_ref,
             dq_ref, dk_ref, dv_ref, dfc_ref, dfr_ref):
        g, j = pl.program_id(0), pl.program_id(1)
        tri = (lax.broadcasted_iota(jnp.int32, (blk, blk), 1) <= lax.broadcasted_iota(jnp.int32, (blk, blk), 0))

        @pl.when(j == 0)
        def _():
            dq_ref[...] = jnp.zeros_like(dq_ref)
            dfc_ref[...] = jnp.zeros_like(dfc_ref)

        def tile(i, carries, masked):
            rs = pl.ds(pl.multiple_of(i * blk, blk), blk)
            out = []
            for h, (dk, dv, dfs) in enumerate(carries):
                cs = slice(h * HEAD, (h + 1) * HEAD)
                cs2 = slice(2 * h * HEAD, 2 * (h + 1) * HEAD)
                csq = slice(2 * h * HEAD, (2 * h + 1) * HEAD)
                qi = q_ref[rs, csq]
                doi = do_ref[rs, cs]
                bias = fc_ref[0, rs, h:h + 1] - lse_ref[0, rs, h:h + 1]
                p = jnp.exp(_bdot_raw(q_ref[rs, cs2], k_ref[:, cs2], _NT) + bias)
                if masked:
                    p = jnp.where(tri, p, 0.0)
                ds = p * (_bdot_raw(doi, v_ref[:, cs], _NT) - dl_ref[0, rs, h:h + 1])
                dsb = ds.astype(BF16)
                dq_ref[rs, cs] += _bdot_raw(dsb, k_ref[:, csq], _NN)
                dfc_ref[0, rs, h:h + 1] += jnp.sum(ds, axis=1, keepdims=True)
                out.append((dk + _bdot_raw(dsb, qi, _TN), dv + _bdot_raw(p, doi, _TN),
                            dfs - jnp.sum(ds, axis=0, keepdims=True)))
            return tuple(out)

        init = tuple((jnp.zeros((blk, HEAD), F32), jnp.zeros((blk, HEAD), F32), jnp.zeros((1, blk), F32))
                     for _ in range(hpg))
        carries = lax.fori_loop(j + 1, nq, lambda i, c: tile(i, c, False), tile(j, init, True))
        for h, (dk, dv, dfs) in enumerate(carries):
            cs = slice(h * HEAD, (h + 1) * HEAD)
            dk_ref[:, cs] = dk
            dv_ref[:, cs] = dv.astype(dv_ref.dtype)
            dfr_ref[0, 0, h:h + 1, :] = dfs

    once = pl.Buffered(1)
    stat = pl.BlockSpec((1, t, hpg), lambda g, j: (g, 0, 0), pipeline_mode=once)
    kv_blk = pl.BlockSpec((blk, gw), lambda g, j: (j, g))
    frow = pl.BlockSpec((1, 1, hpg, blk), lambda g, j: (g, j, 0, 0))
    dq, dk, dv, dfc, dfr = pl.pallas_call(
        body, name="fox_attn_bwd",
        grid=(ATTN_GROUPS, nq),
        in_specs=[pl.BlockSpec((t, 2 * gw), lambda g, j: (0, g), pipeline_mode=once),
                  pl.BlockSpec((t, gw), lambda g, j: (0, g), pipeline_mode=once),
                  pl.BlockSpec((blk, 2 * gw), lambda g, j: (j, g)), kv_blk, stat, stat, stat],
        out_specs=[pl.BlockSpec((t, gw), lambda g, j: (0, g)), kv_blk, kv_blk,
                   pl.BlockSpec((1, t, hpg), lambda g, j: (g, 0, 0)), frow],
        out_shape=[jax.ShapeDtypeStruct((t, width), F32), jax.ShapeDtypeStruct((t, width), F32),
                   jax.ShapeDtypeStruct((t, width), BF16), jax.ShapeDtypeStruct((ATTN_GROUPS, t, hpg), F32),
                   jax.ShapeDtypeStruct((ATTN_GROUPS, nq, hpg, blk), F32)],
        compiler_params=_cparams(dimension_semantics=("parallel", "arbitrary")),
    )(q, do, k, v, f_grp, lse, delta)
    return dq, dk, dv, dfc, dfr


SUBLANES = 8


def _shift_down(u, n):
    r = pltpu.roll(u, n, 0)
    row = lax.broadcasted_iota(jnp.int32, (SUBLANES, u.shape[1]), 0)
    return jnp.concatenate([jnp.where(row < n, 0.0, r[:SUBLANES]), r[SUBLANES:]], axis=0)


def _shift_up(u, n):
    t = u.shape[0]
    r = pltpu.roll(u, t - n, 0)
    row = lax.broadcasted_iota(jnp.int32, (SUBLANES, u.shape[1]), 0)
    return jnp.concatenate([r[:t - SUBLANES], jnp.where(row >= SUBLANES - n, 0.0, r[t - SUBLANES:])], axis=0)


def _convglu_specs(t):
    return [pl.BlockSpec((2, t, LANES), lambda j: (0, 0, j)),
            pl.BlockSpec((2, CONV_TAPS, LANES), lambda j: (0, 0, j)),
            pl.BlockSpec((2, 1, LANES), lambda j: (0, 0, j))]


def _convglu_fwd(u, cw, cb):
    _, t, fp = u.shape

    def body(u_ref, w_ref, b_ref, a_ref, c_ref):
        c = []
        for hf in range(2):
            uv, w = u_ref[hf].astype(F32), w_ref[hf]
            c.append(w[0:1] * _shift_down(uv, 2) + w[1:2] * _shift_down(uv, 1) + w[2:3] * uv + b_ref[hf])
            c_ref[hf] = c[hf].astype(c_ref.dtype)
        a_ref[...] = (_silu(c[0]) * c[1]).astype(a_ref.dtype)

    return pl.pallas_call(
        body, name="convglu_fwd",
        grid=(fp // LANES,),
        in_specs=_convglu_specs(t),
        out_specs=[pl.BlockSpec((t, LANES), lambda j: (0, j)), pl.BlockSpec((2, t, LANES), lambda j: (0, 0, j))],
        out_shape=[jax.ShapeDtypeStruct((t, fp), BF16), jax.ShapeDtypeStruct((2, t, fp), BF16)],
        compiler_params=_cparams(dimension_semantics=("parallel",)),
    )(u, cw, cb)


def _convglu_bwd(u, c, cw, da):
    _, t, fp = u.shape

    def body(u_ref, c_ref, w_ref, da_ref, du_ref, dw_ref, db_ref):
        gc, vc = c_ref[0].astype(F32), c_ref[1].astype(F32)
        sg = _sigmoid(gc)
        dav = da_ref[...].astype(F32)
        dcs = [dav * vc * (sg * (1.0 + gc * (1.0 - sg))), dav * (gc * sg)]
        for hf in range(2):
            dc, w, uv = dcs[hf], w_ref[hf], u_ref[hf].astype(F32)
            dc1, dc2 = _shift_up(dc, 1), _shift_up(dc, 2)
            du_ref[hf] = (w[2:3] * dc + w[1:2] * dc1 + w[0:1] * dc2).astype(du_ref.dtype)
            dw_ref[hf, 0:1, :] = jnp.sum(dc2 * uv, axis=0, keepdims=True)
            dw_ref[hf, 1:2, :] = jnp.sum(dc1 * uv, axis=0, keepdims=True)
            dw_ref[hf, 2:3, :] = jnp.sum(dc * uv, axis=0, keepdims=True)
            db_ref[hf] = jnp.sum(dc, axis=0, keepdims=True)

    pair, taps, bias = _convglu_specs(t)
    return pl.pallas_call(
        body, name="convglu_bwd",
        grid=(fp // LANES,),
        in_specs=[pair, pair, taps, pl.BlockSpec((t, LANES), lambda j: (0, j))],
        out_specs=[pair, taps, bias],
        out_shape=[jax.ShapeDtypeStruct((2, t, fp), BF16), jax.ShapeDtypeStruct((2, CONV_TAPS, fp), F32),
                   jax.ShapeDtypeStruct((2, 1, fp), F32)],
        compiler_params=_cparams(dimension_semantics=("parallel",)),
    )(u, c, cw, da)


def _local_step(x, target, mods, lb, small, pre_w, get_w, put_g, *, tb=512, attn_blk=512):
    t, d = x.shape
    nh = d // HEAD
    nb = NDEV
    wts = {}
    vec = lambda *names: [mods[n] for n in names]

    def ffn_fwd(h2, l):
        u = _mm_wblk(h2, wts[f"up{l}"], BF16, f"ffn{l}_up", gb=nb // 2, split=2, tm=512)
        a, c = _convglu_fwd(u, small[f"conv_w{l}"], small[f"conv_b{l}"])
        f = _mm(a, wts[f"down{l}"], "nn", F32, f"ffn{l}_down", tk=4096)
        return (u, c), a, f

    def ffn_bwd(df, h2, uc, a, l):
        u, c = uc
        da = _mm(df, wts[f"down{l}"], "nt", BF16, f"ffn{l}_down_dx", tn=1536)
        dwd = _mm(a, df, "tn", BF16, f"ffn{l}_down_dw", tm=768, tk=t)
        du, dcw, dcb = _convglu_bwd(u, c, small[f"conv_w{l}"], da)
        dh2 = _mm_wblk_dx(du, wts[f"up{l}"], BF16, f"ffn{l}_up_dx", k=d, gb=nb // 2, split=2, tm=1024)
        dwu = _mm_wblk_dw(h2, du, f"ffn{l}_up_dw", nb=nb, gb=1, split=2, tk=t)
        return dh2, dwu, dwd, dcw, dcb

    (h_a,) = _row_fwd(_f_mod, [(x, d, 0)], vec("sh1_0", "sc1_0"), [BF16], tb=tb, name="l0_mod1")
    wts.update(get_w("l0a", h_a))
    proj_a = _mm_wblk(h_a, wts["a_in"], F32, "a_in", gb=nb // 2)
    ypre, states = _hgrn2_fwd(proj_a, lb, small["a_norm_g"], tb)
    pre_w("l0b", ypre)
    wts.update(get_w("l0b", ypre))
    y_a = _mm(ypre, wts["a_out"], "nn", F32, "a_out")
    x1, h2_0 = _row_fwd(_f_res_mod, [(x, d, 0), (y_a, d, 0)], vec("g1_0", "sh2_0", "sc2_0"), [F32, BF16],
                        tb=tb, name="l0_res_mod2")
    u0, a0, f0 = ffn_fwd(h2_0, 0)
    x2, h_kv, h_q = _row_fwd(_f_res_mod2, [(x1, d, 0), (f0, d, 0)],
                             [mods["g2_0"] + pre_w("l1", f0)] + vec("kv_sh", "kv_sc", "sh1_1", "sc1_1"),
                             [F32, BF16, BF16], tb=tb, name="l0_res_kvmod_qmod")
    wts.update(get_w("l1", h_kv))
    proj_k = _mm(h_kv, wts["kv_k"], "nt", F32, "k_proj")
    v_b = _mm(h_kv, wts["kv_v"], "nt", BF16, "v_proj")
    proj_f = _mm(h_kv, wts["kv_f"], "nt", F32, "kv_fproj")
    f_logit_t = proj_f[:, :nh].T
    f_bias = small["kv_b_f"].reshape(nh, 1)
    f_t = _fgate_fwd(f_logit_t, f_bias)
    f_grp = f_t.reshape(ATTN_GROUPS, nh // ATTN_GROUPS, t).transpose(0, 2, 1)
    (k_n,) = _row_fwd(_f_knorm_aug, [(proj_k, HEAD, 0)] + [(piece, 1, 0) for piece in _split3(-f_t.T)],
                      [small["k_norm_g"]], [BF16], nsub=nh, tb=tb, name="k_norm")
    proj_q = _mm_wblk(h_q, wts["b_q"], F32, "b_q", gb=nb)
    (q_n,) = _row_fwd(_f_qnorm_aug, [(proj_q, HEAD, 0)], [small["q_norm_g"]], [BF16], nsub=nh, tb=tb,
                      name="q_norm")
    o_att, lse = _attn_fwd(q_n, k_n, v_b, f_grp, attn_blk)
    (z,) = _row_fwd(_f_outgate, [(o_att, HEAD, 0), (proj_q, HEAD, 1)], [], [BF16], nsub=nh, tb=tb, name="out_gate")
    y_b = _mm(z, wts["b_out"], "nn", F32, "b_out")
    x3, h2_1 = _row_fwd(_f_res_mod, [(x2, d, 0), (y_b, d, 0)], vec("g1_1", "sh2_1", "sc2_1"), [F32, BF16],
                        tb=tb, name="l1_res_mod2")
    u1, a1, f1 = ffn_fwd(h2_1, 1)
    loss, dx4, df1, dg2_1 = _loss_call(x3, f1, mods["g2_1"], target, tb)

    g = {}
    dmods = {"g2_1": dg2_1}
    dh2, g["up1"], g["down1"], g["conv_w1"], g["conv_b1"] = ffn_bwd(df1, h2_1, u1, a1, 1)
    (dx2, dy_b), (dmods["g1_1"], dmods["sh2_1"], dmods["sc2_1"]) = _row_bwd(
        _f_res_mod, [(x2, d, 0), (y_b, d, 0)], vec("g1_1", "sh2_1", "sc2_1"),
        [(dx4, d, 0), (dh2, d, 0)], [F32, BF16], tb=tb, name="l1_res_mod2_bwd")
    dz = _mm(dy_b, wts["b_out"], "nt", BF16, "b_out_dx")
    g["b_out"] = _mm(z, dy_b, "tn", BF16, "b_out_dw", tk=t)
    (do_att, dog), _ = _row_bwd(_f_outgate, [(o_att, HEAD, 0), (proj_q, HEAD, 1)], [], [(dz, HEAD, 0)],
                                [BF16, BF16], nsub=nh, tb=tb, name="out_gate_bwd")
    delta = _attn_delta(do_att, o_att, tb)
    dq_n, dk_n, dv, dfc_q, dfr_k = _attn_bwd(q_n, k_n, v_b, f_grp, do_att, lse, delta, attn_blk)
    (dpq,), (g["q_norm_g"],) = _row_bwd(_f_qnorm, [(proj_q, HEAD, 0)], [small["q_norm_g"]],
                                        [(dq_n, HEAD, 0)], [BF16], nsub=nh, tb=tb, name="q_norm_bwd")
    dproj_q = jnp.concatenate([dpq, dog], axis=1)
    dh_q = _mm_wblk_dx(dproj_q, wts["b_q"], BF16, "b_q_dx", k=d, gb=nb)
    g["b_q"] = _mm_wblk_dw(h_q, dproj_q, "b_q_dw", nb=nb, gb=nb // 4, tk=t)
    (dpk,), (g["k_norm_g"],) = _row_bwd(_f_knorm, [(proj_k, HEAD, 0)], [small["k_norm_g"]],
                                        [(dk_n, HEAD, 0)], [BF16], nsub=nh, tb=tb, name="k_norm_bwd")
    df_t = dfc_q.transpose(0, 2, 1).reshape(nh, t) + dfr_k.transpose(0, 2, 1, 3).reshape(nh, t)
    dflogit_t, g["kv_b_f"] = _fgate_bwd(f_logit_t, f_bias, df_t)
    dproj_f = jnp.pad(dflogit_t.T, ((0, 0), (0, LANES - nh))).astype(BF16)
    dh_kv = _mm(dpk, wts["kv_k"], "nn", BF16, "k_proj_dx")
    dh_kv_v = _mm(dv, wts["kv_v"], "nn", BF16, "v_proj_dx")
    dh_kv_f = _mm(dproj_f, wts["kv_f"], "nn", BF16, "kv_fproj_dx")
    g["kv_k"] = _mm(dpk, h_kv, "tn", BF16, "k_proj_dw", tk=t)
    g["kv_v"] = _mm(dv, h_kv, "tn", BF16, "v_proj_dw", tk=t)
    g["kv_f"] = _mm(dproj_f, h_kv, "tn", F32, "kv_fproj_dw", tk=1024)
    sent = put_g("l1", {n: g.pop(n) for n in ("b_out", "b_q", "kv_k", "kv_v", "kv_f", "up1", "down1")})
    (dx1, df0), (dmods["g2_0"], dmods["kv_sh"], dmods["kv_sc"], dmods["sh1_1"], dmods["sc1_1"]) = _row_bwd(
        _f_res_mod2, [(x1, d, 0), (f0, d, 0)], [mods["g2_0"] + sent] + vec("kv_sh", "kv_sc", "sh1_1", "sc1_1"),
        [(dx2, d, 0), (dh_kv, d, 0), (dh_q, d, 0)], [F32, BF16], tb=tb, name="l0_res_kvmod_qmod_bwd",
        cot_add=[(1, dh_kv_v), (1, dh_kv_f)])
    dh2, g["up0"], g["down0"], g["conv_w0"], g["conv_b0"] = ffn_bwd(df0, h2_0, u0, a0, 0)
    (dx0, dy_a), (dmods["g1_0"], dmods["sh2_0"], dmods["sc2_0"]) = _row_bwd(
        _f_res_mod, [(x, d, 0), (y_a, d, 0)], vec("g1_0", "sh2_0", "sc2_0"),
        [(dx1, d, 0), (dh2, d, 0)], [F32, BF16], tb=tb, name="l0_res_mod2_bwd")
    dypre = _mm(dy_a, wts["a_out"], "nt", BF16, "a_out_dx")
    g["a_out"] = _mm(ypre, dy_a, "tn", BF16, "a_out_dw", tk=t)
    sent = put_g("l0b", {n: g.pop(n) for n in ("a_out", "up0", "down0")})
    dproj_a, dlb, g["a_norm_g"] = _hgrn2_bwd(proj_a, lb + sent, small["a_norm_g"], states, dypre, tb)
    dh_a = _mm_wblk_dx(dproj_a, wts["a_in"], BF16, "a_in_dx", k=d, gb=nb, split=4, tm=512)
    put_g("l0a", {"a_in": _mm_wblk_dw(h_a, dproj_a, "a_in_dw", nb=nb, gb=1, split=4, tk=t)})
    (grad_x,), (dmods["sh1_0"], dmods["sc1_0"]) = _row_bwd(
        _f_mod, [(x, d, 0)], vec("sh1_0", "sc1_0"), [(dh_a, d, 0)], [F32], tb=tb, name="l0_mod1_bwd",
        add_to=(0, dx0))
    return loss, grad_x, dmods, dlb, g


def _position():
    return lax.axis_index("x"), lax.axis_index("y"), lax.axis_index("c")


_XCHG_EFFECT = pltpu.SideEffectType.DATAFLOW_SIDE_EFFECTING
ALL_PEERS = (1, 2, 3, 4, 5, 6, 7)
SAME_CORE = (2, 4, 6)


def _xchg_copies(src_refs, land_refs, send_sems, recv_sems, local_sems, scatter, rels):
    x, y, cc = _position()
    me = 4 * x + 2 * y + cc
    remote, local = [], []
    for a, (src, land) in enumerate(zip(src_refs, land_refs)):
        local.append(pltpu.make_async_copy(src.at[me] if scatter else src, land.at[me], local_sems.at[a]))
        for idx, rel in enumerate(rels):
            px = 1 - x if rel & 4 else x
            py = 1 - y if rel & 2 else y
            pc = 1 - cc if rel & 1 else cc
            k = len(rels) * a + idx
            remote.append(pltpu.make_async_remote_copy(
                src_ref=src.at[4 * px + 2 * py + pc] if scatter else src, dst_ref=land.at[me],
                send_sem=send_sems.at[k], recv_sem=recv_sems.at[k], device_id=(px, py, pc), device_id_type=_MESH))
    return remote, local


def _xchg_start(srcs, scatter, rels, after, name):
    n = len(srcs)
    lands = [lax.empty(s.shape if scatter else (NDEV, *s.shape), s.dtype) for s in srcs]

    def body(*refs):
        remote, local = _xchg_copies(refs[:n], refs[n:2 * n], *refs[2 * n + 1:2 * n + 4], scatter, rels)
        for cp in local + remote:
            cp.start()
        token = refs[-1]
        token[...] = jnp.zeros_like(token)

    hbm = pl.BlockSpec(memory_space=pltpu.HBM)
    sem = pl.BlockSpec(memory_space=pltpu.SEMAPHORE)
    out = pl.pallas_call(
        body, name=name,
        out_shape=(pltpu.SemaphoreType.DMA((len(rels) * n,)), pltpu.SemaphoreType.DMA((len(rels) * n,)),
                   pltpu.SemaphoreType.DMA((n,)),
                   *[pltpu.HBM(a.shape, a.dtype) for a in srcs + lands], jax.ShapeDtypeStruct((8, LANES), F32)),
        in_specs=[hbm] * (2 * n) + [pl.BlockSpec(memory_space=pl.ANY)],
        out_specs=(sem, sem, sem, *[hbm] * (2 * n), pl.BlockSpec(memory_space=pltpu.VMEM)),
        input_output_aliases={i: 3 + i for i in range(2 * n)},
        compiler_params=pltpu.CompilerParams(has_side_effects=_XCHG_EFFECT),
    )(*[pltpu.with_memory_space_constraint(a, pltpu.HBM) for a in srcs + lands], after)
    return out[:-1], out[-1][0, 0]


def _xchg_wait(handles, after, scatter, rels, name):
    n = (len(handles) - 3) // 2

    def body(*refs):
        remote, local = _xchg_copies(refs[:n], refs[n:2 * n], *refs[2 * n:2 * n + 3], scatter, rels)
        for cp in remote:
            cp.wait_send()
            cp.wait_recv()
        for cp in local:
            cp.wait()

    hbm = pl.BlockSpec(memory_space=pltpu.HBM)
    sem = pl.BlockSpec(memory_space=pltpu.SEMAPHORE)
    thru = list(handles[3:])
    afters = list(after) if isinstance(after, (list, tuple)) else [after]
    out = pl.pallas_call(
        body, name=name,
        out_shape=tuple(pltpu.HBM(a.shape, a.dtype) for a in thru),
        in_specs=[hbm] * (2 * n) + [sem, sem, sem] + [pl.BlockSpec(memory_space=pl.ANY)] * len(afters),
        out_specs=tuple([hbm] * (2 * n)),
        input_output_aliases={i: i for i in range(2 * n)},
        compiler_params=pltpu.CompilerParams(has_side_effects=_XCHG_EFFECT),
    )(*thru, *handles[:3], *afters)
    return list(out[n:])


def _sibling_copies(land_refs, send_sems, recv_sems):
    x, y, cc = _position()

    def copy(a, q, core):
        slot = land_refs[a].at[2 * q + core]
        return pltpu.make_async_remote_copy(
            src_ref=slot, dst_ref=slot, send_sem=send_sems.at[NCHIP * a + q], recv_sem=recv_sems.at[NCHIP * a + q],
            device_id=(x, y, 1 - cc), device_id_type=_MESH)

    pairs = [(a, q) for a in range(len(land_refs)) for q in range(NCHIP)]
    return [copy(a, q, cc) for a, q in pairs], [copy(a, q, 1 - cc) for a, q in pairs]


def _sibling_forward_start(lands, name):
    n = len(lands)

    def body(*refs):
        sends, _ = _sibling_copies(refs[:n], refs[n], refs[n + 1])
        for cp in sends:
            cp.start()
        refs[-1][...] = jnp.zeros_like(refs[-1])

    hbm = pl.BlockSpec(memory_space=pltpu.HBM)
    sem = pl.BlockSpec(memory_space=pltpu.SEMAPHORE)
    out = pl.pallas_call(
        body, name=name,
        out_shape=(pltpu.SemaphoreType.DMA((NCHIP * n,)), pltpu.SemaphoreType.DMA((NCHIP * n,)),
                   *[pltpu.HBM(a.shape, a.dtype) for a in lands], jax.ShapeDtypeStruct((8, LANES), F32)),
        in_specs=[hbm] * n,
        out_specs=(sem, sem, *[hbm] * n, pl.BlockSpec(memory_space=pltpu.VMEM)),
        input_output_aliases={i: 2 + i for i in range(n)},
        compiler_params=pltpu.CompilerParams(has_side_effects=_XCHG_EFFECT),
    )(*lands)
    return out[:-1], out[-1][0, 0]


def _sibling_forward_wait(handles, after, name):
    n = len(handles) - 2

    def body(*refs):
        sends, arrivals = _sibling_copies(refs[:n], refs[n], refs[n + 1])
        for cp in sends:
            cp.wait_send()
        for cp in arrivals:
            cp.wait_recv()

    hbm = pl.BlockSpec(memory_space=pltpu.HBM)
    sem = pl.BlockSpec(memory_space=pltpu.SEMAPHORE)
    lands = list(handles[2:])
    return list(pl.pallas_call(
        body, name=name,
        out_shape=tuple(pltpu.HBM(a.shape, a.dtype) for a in lands),
        in_specs=[hbm] * n + [sem, sem, pl.BlockSpec(memory_space=pl.ANY)],
        out_specs=tuple([hbm] * n),
        input_output_aliases={i: i for i in range(n)},
        compiler_params=pltpu.CompilerParams(has_side_effects=_XCHG_EFFECT),
    )(*lands, *handles[:2], after))


def _slab_sum(slabs, name, tr=None):
    n, r, c = slabs.shape
    tr = r if tr is None else tr

    def body(s_ref, o_ref):
        acc = s_ref[0].astype(F32)
        for q in range(1, n):
            acc = acc + s_ref[q].astype(F32)
        o_ref[...] = acc

    return pl.pallas_call(body, name=name, grid=(r // tr,),
                          in_specs=[pl.BlockSpec((n, tr, c), lambda i: (0, i, 0))],
                          out_specs=pl.BlockSpec((tr, c), lambda i: (i, 0)),
                          out_shape=jax.ShapeDtypeStruct((r, c), F32),
                          compiler_params=_cparams(dimension_semantics=("parallel",)))(slabs)


def _ada_fwd(c_all, ada_w, kv_ada_w, logits):
    rows, d = c_all.shape
    n0, nkv = ada_w.shape[2], kv_ada_w.shape[1]

    def body(c_ref, w_ref, kw_ref, lg_ref, part_ref, cact_ref, lb_ref):
        ca = _silu(c_ref[...])
        cact_ref[...] = ca
        part_ref[:, 0:n0] = _bdot_raw(ca, w_ref[0], _NN)
        part_ref[:, n0:2 * n0] = _bdot_raw(ca, w_ref[1], _NN)
        part_ref[:, 2 * n0:2 * n0 + nkv] = _bdot_raw(ca, kw_ref[...], _NN)
        lb_ref[...] = _sigmoid(lg_ref[0:1, :] - lg_ref[1:2, :])

    vm = pl.BlockSpec(memory_space=pltpu.VMEM)
    return pl.pallas_call(
        body, name="ada_fwd", in_specs=[vm, vm, vm, vm], out_specs=[vm, vm, vm],
        out_shape=[jax.ShapeDtypeStruct((rows, 2 * n0 + nkv), F32), jax.ShapeDtypeStruct((rows, d), F32),
                   jax.ShapeDtypeStruct((1, d), F32)],
        compiler_params=_cparams(),
    )(c_all, ada_w, kv_ada_w, logits)


def _ada_bwd(c_act, dm0, dm1, dkv, lb, dlb):
    rows, d = c_act.shape

    def body(c_ref, d0_ref, d1_ref, dk_ref, lb_ref, dlb_ref, dw_ref, dkw_ref, dlg_ref):
        ca = c_ref[...]
        dw_ref[0] = _bdot_raw(ca, d0_ref[...], _TN)
        dw_ref[1] = _bdot_raw(ca, d1_ref[...], _TN)
        dkw_ref[...] = _bdot_raw(ca, dk_ref[...], _TN)
        lbv = lb_ref[...]
        dl0 = dlb_ref[...] * lbv * (1.0 - lbv)
        dlg_ref[0:1, :] = dl0
        dlg_ref[1:2, :] = -dl0

    vm = pl.BlockSpec(memory_space=pltpu.VMEM)
    return pl.pallas_call(
        body, name="ada_bwd", in_specs=[vm] * 6, out_specs=[vm, vm, vm],
        out_shape=[jax.ShapeDtypeStruct((2, d, dm0.shape[1]), F32), jax.ShapeDtypeStruct((d, dkv.shape[1]), F32),
                   jax.ShapeDtypeStruct((2, d), F32)],
        compiler_params=_cparams(),
    )(c_act, dm0, dm1, dkv, lb, dlb)


def _adamw(w, g, m, v, name, tr=512, after=None):
    r, c = w.shape
    tr = _divisor_tile(r, tr, unit=8)
    c1 = 1.0 - ADAM_B1 ** ADAM_STEP
    c2 = 1.0 - ADAM_B2 ** ADAM_STEP
    deps = [] if after is None else [after]

    def body(w_ref, g_ref, m_ref, v_ref, *rest):
        d_ref, mo_ref, vo_ref = rest[len(deps):]
        gv = g_ref[...]
        mn = ADAM_B1 * m_ref[...] + (1.0 - ADAM_B1) * gv
        vn = ADAM_B2 * v_ref[...] + (1.0 - ADAM_B2) * (gv * gv)
        d_ref[...] = -ADAM_LR * ((mn / c1) / (jnp.sqrt(vn / c2) + ADAM_EPS) + ADAM_WD * w_ref[...])
        mo_ref[...] = mn
        vo_ref[...] = vn

    spec = pl.BlockSpec((tr, c), lambda i: (i, 0))
    out = jax.ShapeDtypeStruct((r, c), F32)
    return pl.pallas_call(body, name=name, grid=(r // tr,),
                          in_specs=[spec] * 4 + [pl.BlockSpec(a.shape, lambda i: (0, 0)) for a in deps],
                          out_specs=[spec] * 3, out_shape=[out, out, out],
                          compiler_params=_cparams(dimension_semantics=("parallel",)))(w, g, m, v, *deps)


def _pad_rows(a, rows):
    return jnp.pad(a, ((0, rows - a.shape[0]), (0, 0)))


def _pack_small(parts, lanes=LANES, row_unit=8):
    flat = jnp.concatenate([p.reshape(-1).astype(F32) for p in parts])
    rows = _round_up(-(-flat.shape[0] // lanes), row_unit)
    return jnp.pad(flat, (0, rows * lanes - flat.shape[0])).reshape(rows, lanes)


def _unpack_small(flat, shapes):
    out, off = [], 0
    for s in shapes:
        n = 1
        for k in s:
            n *= k
        out.append(flat[off:off + n].reshape(s))
        off += n
    return out


def _pad_shard_cols(a, n_loc, n_pad):
    lead = a.shape[:-1]
    a = a.reshape(*lead, NDEV, n_loc)
    a = jnp.pad(a, [(0, 0)] * (len(lead) + 1) + [(0, n_pad - n_loc)])
    return a.reshape(*lead, NDEV * n_pad)


def _unpad_shard_cols(a, n_loc, n_pad):
    lead = a.shape[:-1]
    return a.reshape(*lead, NDEV, n_pad)[..., :n_loc].reshape(*lead, NDEV * n_loc)


def kernel(x, c, ada_w, ada_b, a_w_in, a_lb_logits, a_norm_g, a_w_out, kv_ada_w, kv_ada_b, kv_w, kv_b_f, k_norm_g, b_w_q, q_norm_g, b_w_out, ffn_w_up, ffn_conv_w, ffn_conv_b, ffn_w_down, loss_target, m_ada_w, m_ada_b, m_a_w_in, m_a_lb_logits, m_a_norm_g, m_a_w_out, m_kv_ada_w, m_kv_ada_b, m_kv_w, m_kv_b_f, m_k_norm_g, m_b_w_q, m_q_norm_g, m_b_w_out, m_ffn_w_up, m_ffn_conv_w, m_ffn_conv_b, m_ffn_w_down, v_ada_w, v_ada_b, v_a_w_in, v_a_lb_logits, v_a_norm_g, v_a_w_out, v_kv_ada_w, v_kv_ada_b, v_kv_w, v_kv_b_f, v_k_norm_g, v_b_w_q, v_q_norm_g, v_b_w_out, v_ffn_w_up, v_ffn_conv_w, v_ffn_conv_b, v_ffn_w_down):
    t, d = x.shape[1], x.shape[2]
    nh = d // HEAD
    ncw = ffn_w_up.shape[2]
    ncp = _round_up(ncw, LANES)
    two_f = ncw * NDEV
    ff = two_f // 2
    fp = ncp * NDEV // 2
    rd = ffn_w_down.shape[1]
    me = 4 * lax.axis_index("x") + 2 * lax.axis_index("y") + lax.axis_index("c")
    weights = dict(ada_w=ada_w, ada_b=ada_b, a_w_in=a_w_in, a_lb_logits=a_lb_logits, a_norm_g=a_norm_g,
                   a_w_out=a_w_out, kv_ada_w=kv_ada_w, kv_ada_b=kv_ada_b, kv_w=kv_w, kv_b_f=kv_b_f,
                   k_norm_g=k_norm_g, b_w_q=b_w_q, q_norm_g=q_norm_g, b_w_out=b_w_out, ffn_w_up=ffn_w_up,
                   ffn_conv_w=ffn_conv_w, ffn_conv_b=ffn_conv_b, ffn_w_down=ffn_w_down)
    m_in = dict(ada_w=m_ada_w, ada_b=m_ada_b, a_w_in=m_a_w_in, a_lb_logits=m_a_lb_logits, a_norm_g=m_a_norm_g,
                a_w_out=m_a_w_out, kv_ada_w=m_kv_ada_w, kv_ada_b=m_kv_ada_b, kv_w=m_kv_w, kv_b_f=m_kv_b_f,
                k_norm_g=m_k_norm_g, b_w_q=m_b_w_q, q_norm_g=m_q_norm_g, b_w_out=m_b_w_out, ffn_w_up=m_ffn_w_up,
                ffn_conv_w=m_ffn_conv_w, ffn_conv_b=m_ffn_conv_b, ffn_w_down=m_ffn_w_down)
    v_in = dict(ada_w=v_ada_w, ada_b=v_ada_b, a_w_in=v_a_w_in, a_lb_logits=v_a_lb_logits, a_norm_g=v_a_norm_g,
                a_w_out=v_a_w_out, kv_ada_w=v_kv_ada_w, kv_ada_b=v_kv_ada_b, kv_w=v_kv_w, kv_b_f=v_kv_b_f,
                k_norm_g=v_k_norm_g, b_w_q=v_b_w_q, q_norm_g=v_q_norm_g, b_w_out=v_b_w_out, ffn_w_up=v_ffn_w_up,
                ffn_conv_w=v_ffn_conv_w, ffn_conv_b=v_ffn_conv_b, ffn_w_down=v_ffn_w_down)
    order = list(weights)

    up_loc = jnp.pad(ffn_w_up, ((0, 0), (0, 0), (0, ncp - ncw))).astype(BF16)
    down_loc = ffn_w_down.astype(BF16)
    gather_names = {"l0b": ["a_out", "up0", "down0"], "l1": ["kv", "b_q", "b_out", "up1", "down1"]}
    shards = {"a_out": a_w_out[0].astype(BF16), "up0": up_loc[0], "down0": down_loc[0], "kv": kv_w.T.astype(BF16),
              "b_q": b_w_q[0].astype(BF16), "b_out": b_w_out[0].astype(BF16), "up1": up_loc[1],
              "down1": down_loc[1]}
    pre = _pack_small([c, a_lb_logits, ffn_conv_w])
    in_flight = {}
    pre_flight, _ = _xchg_start([pre], False, ALL_PEERS, pre, "gather_small_inputs_start")
    in_flight["l0a"], _ = _xchg_start([a_w_in[0].astype(BF16)], False, SAME_CORE, pre_flight[-1], "gather_l0a_start")
    (pre_all,) = _xchg_wait(pre_flight, shards["down1"], False, ALL_PEERS, "gather_small_inputs_wait")
    pre_all = pre_all.reshape(NDEV, -1)
    c_all = pre_all[:, :d]
    logits = pre_all[:, d:d + 2 * HEAD].reshape(NDEV, 2, HEAD).transpose(1, 0, 2).reshape(2, d)
    conv_w_full = pre_all[:, d + 2 * HEAD:d + 2 * HEAD + 2 * CONV_TAPS * ncw]
    conv_w_full = conv_w_full.reshape(NDEV, 2, CONV_TAPS, ncw).transpose(1, 2, 0, 3).reshape(2, CONV_TAPS, two_f)

    part, c_act, lb = _ada_fwd(_pad_rows(c_all, 2 * NDEV), ada_w, kv_ada_w, logits)
    part_flight, _ = _xchg_start([part[:NDEV]], False, ALL_PEERS, part, "gather_adaln_start")
    (part_all,) = _xchg_wait(part_flight, c_act, False, ALL_PEERS, "gather_adaln_wait")
    forwarding = {}
    arrived = _xchg_wait(in_flight["l0a"], part_all, False, SAME_CORE, "gather_l0a_wait")
    forwarding["l0a"], _ = _sibling_forward_start(arrived, "gather_l0a_to_sibling_start")
    mine = lax.dynamic_index_in_dim(part_all, me, axis=1, keepdims=False)
    n0, nkv = ada_w.shape[2], kv_ada_w.shape[1]
    mod_names = ["sh1", "sc1", "g1", "sh2", "sc2", "g2"]
    mods = {}
    for l in range(2):
        row = mine[:, l * n0:(l + 1) * n0].reshape(-1) + ada_b[l]
        for k, nm in enumerate(mod_names):
            mods[f"{nm}_{l}"] = row[k * d:(k + 1) * d].reshape(1, d)
    kvrow = mine[:, 2 * n0:2 * n0 + nkv].reshape(-1) + kv_ada_b
    mods["kv_sh"], mods["kv_sc"] = kvrow[:d].reshape(1, d), kvrow[d:].reshape(1, d)

    def start_gather(grp, dep):
        srcs = [shards[n] for n in gather_names[grp]]
        in_flight[grp], started = _xchg_start(srcs, False, SAME_CORE, dep, f"gather_{grp}_start")
        return started

    zero = start_gather("l0b", forwarding["l0a"][-1])
    mods["sh1_0"] = mods["sh1_0"] + zero

    small = {"a_norm_g": a_norm_g, "k_norm_g": k_norm_g.reshape(1, HEAD), "q_norm_g": q_norm_g, "kv_b_f": kv_b_f}
    for l in range(2):
        small[f"conv_w{l}"] = _pad_shard_cols(conv_w_full[l], ncw, ncp).reshape(CONV_TAPS, 2, fp).transpose(1, 0, 2)
        small[f"conv_b{l}"] = _pad_shard_cols(ffn_conv_b[l], ncw, ncp).reshape(2, 1, fp)

    def pre_w(grp, after):
        arrived = _xchg_wait(in_flight[grp], after, False, SAME_CORE, f"gather_{grp}_wait")
        forwarding[grp], started = _sibling_forward_start(arrived, f"gather_{grp}_to_sibling_start")
        return started

    def get_w(grp, after):
        if grp == "l0a":
            return {"a_in": _sibling_forward_wait(forwarding["l0a"], after, "gather_l0a_to_sibling_wait")[0]}
        full = _sibling_forward_wait(forwarding[grp], after, f"gather_{grp}_to_sibling_wait")
        if grp == "l0b":
            started = start_gather("l1", full[0])
            full[0] = full[0] + started.astype(full[0].dtype)
        got = dict(zip(gather_names[grp], full))
        out = {}
        for n, a in got.items():
            if n in ("a_out", "b_out"):
                out[n] = a.reshape(d, d)
            elif n in ("down0", "down1"):
                dn = a.reshape(NCHIP, ff // NCHIP, d)
                out[n] = jnp.pad(dn, ((0, 0), (0, ncp - ncw), (0, 0))).reshape(fp, d)
            elif n == "kv":
                kv_t = a.reshape(NDEV * kv_w.shape[1], d)
                out["kv_k"], out["kv_v"] = kv_t[:d], kv_t[d:2 * d]
                out["kv_f"] = jnp.pad(kv_t[2 * d:], ((0, LANES - nh), (0, 0)))
            else:
                out[n] = a
        return out

    scatter_flight, g_last = {}, {}

    def put_g(grp, gr):
        if grp == "l0a":
            g_last.update(gr)
            return zero
        if grp == "l1":
            g_kvw = jnp.concatenate([gr["kv_k"], gr["kv_v"], gr["kv_f"][:nh].astype(BF16)], axis=0)
            arrs = {"kv_w": g_kvw.reshape(NDEV, kv_w.shape[1], d), "b_w_q": gr["b_q"],
                    "b_w_out": gr["b_out"].reshape(NDEV, d // NDEV, d), "up1": gr["up1"],
                    "down1": gr["down1"].reshape(NCHIP, ncp, d)[:, :ncw].reshape(NDEV, rd, d)}
        else:
            arrs = {"a_w_out": gr["a_out"].reshape(NDEV, d // NDEV, d), "up0": gr["up0"],
                    "down0": gr["down0"].reshape(NCHIP, ncp, d)[:, :ncw].reshape(NDEV, rd, d)}
        srcs = list(arrs.values())
        handles, sent = _xchg_start(srcs, True, ALL_PEERS, srcs[0], f"scatter_{grp}_start")
        scatter_flight[grp] = (list(arrs), handles)
        return sent

    loss_v, grad_x, dmods, dlb, g = _local_step(x[0], loss_target[0], mods, lb, small, pre_w, get_w, put_g)

    g_sum = {}
    for grp in ("l1", "l0b"):
        names, handles = scatter_flight[grp]
        for nm, a in zip(names, _xchg_wait(handles, grad_x, True, ALL_PEERS, f"scatter_{grp}_wait")):
            g_sum[nm] = _slab_sum(a, f"rs_slab_sum_{nm}")

    def conv_w_grad(a):
        return _unpad_shard_cols(a.transpose(1, 0, 2).reshape(CONV_TAPS, 2 * fp), ncw, ncp)

    def conv_b_grad(a):
        return _unpad_shard_cols(a.reshape(2 * fp), ncw, ncp)

    dmod_vec = [dmods[f"{nm}_{l}"] for l in range(2) for nm in mod_names] + [dmods["kv_sh"], dmods["kv_sc"]]
    post = _pack_small(dmod_vec + [dlb, g["a_norm_g"], g["k_norm_g"], g["q_norm_g"],
                                   jnp.pad(g["kv_b_f"].reshape(-1), (0, LANES - nh)),
                                   conv_w_grad(g["conv_w0"]), conv_w_grad(g["conv_w1"]),
                                   conv_b_grad(g["conv_b0"]), conv_b_grad(g["conv_b1"]), loss_v])
    post_flight, _ = _xchg_start([post], False, ALL_PEERS, post, "gather_small_grads_start")
    a_in_flight, a_in_sent = _xchg_start([g_last["a_in"]], True, ALL_PEERS, post_flight[-1], "scatter_l0a_start")
    a_in_sent = a_in_sent.reshape(1, 1)
    grads = {
        "a_w_out": g_sum["a_w_out"].reshape(a_w_out.shape),
        "kv_w": g_sum["kv_w"].T,
        "b_w_q": g_sum["b_w_q"].reshape(b_w_q.shape),
        "b_w_out": g_sum["b_w_out"].reshape(b_w_out.shape),
        "ffn_w_up": jnp.stack([g_sum["up0"][:, :ncw], g_sum["up1"][:, :ncw]]),
        "ffn_w_down": jnp.stack([g_sum["down0"], g_sum["down1"]]),
    }
    delta, new_m, new_v = {}, {}, {}

    def adamw_matrix(n):
        shp = weights[n].shape
        two_d = lambda a: a.reshape(-1, shp[-1])
        dl, mn, vn = _adamw(two_d(weights[n]), two_d(grads[n]), two_d(m_in[n]), two_d(v_in[n]), f"adamw_{n}",
                            after=a_in_sent)
        delta[n], new_m[n], new_v[n] = dl.reshape(shp), mn.reshape(shp), vn.reshape(shp)

    for n in grads:
        adamw_matrix(n)
    (post_all,) = _xchg_wait(post_flight, [new_v[n] for n in grads], False, ALL_PEERS, "gather_small_grads_wait")
    tot = _slab_sum(post_all, "small_grad_sum").reshape(-1)
    nmod = 14 * d
    (t_mod, t_lb, t_ang, t_kng, t_qng, t_bf, t_cw, t_cb, t_loss) = _unpack_small(
        tot, [(nmod,), (1, d), (1, HEAD), (HEAD,), (1, HEAD), (LANES,), (2, CONV_TAPS, two_f), (2, two_f),
              (LANES,)])
    loss = t_loss[0]
    dm_all = post_all.reshape(NDEV, -1)[:, :nmod]
    dm0 = lax.dynamic_slice_in_dim(dm_all[:, :6 * d], me * n0, n0, axis=1)
    dm1 = lax.dynamic_slice_in_dim(dm_all[:, 6 * d:12 * d], me * n0, n0, axis=1)
    dkv = lax.dynamic_slice_in_dim(dm_all[:, 12 * d:], me * nkv, nkv, axis=1)
    g_ada_w, g_kv_ada_w, g_logits = _ada_bwd(c_act, _pad_rows(dm0, 2 * NDEV), _pad_rows(dm1, 2 * NDEV),
                                              _pad_rows(dkv, 2 * NDEV), lb, t_lb)

    grads.update({
        "ada_w": g_ada_w,
        "ada_b": t_mod[:12 * d].reshape(2, 6 * d),
        "a_lb_logits": lax.dynamic_slice_in_dim(g_logits, me * HEAD, HEAD, axis=1),
        "a_norm_g": t_ang,
        "kv_ada_w": g_kv_ada_w,
        "kv_ada_b": t_mod[12 * d:],
        "kv_b_f": t_bf[:nh],
        "k_norm_g": t_kng,
        "q_norm_g": t_qng,
        "ffn_conv_w": lax.dynamic_slice_in_dim(t_cw, me * ncw, ncw, axis=2),
        "ffn_conv_b": t_cb,
    })

    small_adam = [n for n in order if n not in delta and n not in ("ada_w", "kv_ada_w", "a_w_in")]
    packs = [_pack_small([src[n] for n in small_adam]) for src in (weights, grads, m_in, v_in)]
    outs = _adamw(*packs, "adamw_small", tr=packs[0].shape[0])
    shapes = [weights[n].shape for n in small_adam]
    for dst, o in zip((delta, new_m, new_v), outs):
        for n, a in zip(small_adam, _unpack_small(o.reshape(-1), shapes)):
            dst[n] = a
    adamw_matrix("ada_w")
    adamw_matrix("kv_ada_w")
    (landed,) = _xchg_wait(a_in_flight, new_v["kv_ada_w"], True, ALL_PEERS, "scatter_l0a_wait")
    grads["a_w_in"] = _slab_sum(landed, "rs_slab_sum_a_w_in").reshape(a_w_in.shape)
    adamw_matrix("a_w_in")

    return (loss, grad_x.reshape(x.shape), *[grads[n] for n in order], *[delta[n] for n in order],
            *[new_m[n] for n in order], *[new_v[n] for n in order])
```

```python
import functools

import jax
import jax.numpy as jnp
from jax import lax
from jax.experimental import pallas as pl
from jax.experimental.pallas import tpu as pltpu

F32 = jnp.float32
BF16 = jnp.bfloat16

NDEV = 8
NCHIP = 4
HEAD = 128
A_CHUNK = 64
CONV_TAPS = 3
EPS = 1e-6
NEG_INF = -1e30
LANES = 128
VMEM_LIMIT = 48 * 1024 * 1024

ADAM_LR = 0.001
ADAM_B1 = 0.9
ADAM_B2 = 0.999
ADAM_EPS = 1e-08
ADAM_WD = 0.01
ADAM_STEP = 10

_NN = (((1,), (0,)), ((), ()))
_NT = (((1,), (1,)), ((), ()))
_TN = (((0,), (0,)), ((), ()))
_MESH = pl.DeviceIdType.MESH


def _cparams(**kw):
    return pltpu.CompilerParams(vmem_limit_bytes=VMEM_LIMIT, **kw)


def _divisor_tile(n, pref, unit=LANES):
    if n <= pref:
        return n
    best = None
    for t in range(unit, pref + 1, unit):
        if n % t == 0:
            best = t
    assert best is not None, (n, pref)
    return best


def _round_up(n, unit):
    return -(-n // unit) * unit


def _bdot_raw(a, b, dims):
    return lax.dot_general(a.astype(BF16), b.astype(BF16), dims, preferred_element_type=F32)


@jax.custom_vjp
def _dot_nn(a, b):
    return _bdot_raw(a, b, _NN)


@jax.custom_vjp
def _dot_nt(a, b):
    return _bdot_raw(a, b, _NT)


@jax.custom_vjp
def _dot_tn(a, b):
    return _bdot_raw(a, b, _TN)


_dot_nn.defvjp(lambda a, b: (_bdot_raw(a, b, _NN), (a, b)),
               lambda r, g: (_dot_nt(g, r[1]), _dot_tn(r[0], g)))
_dot_nt.defvjp(lambda a, b: (_bdot_raw(a, b, _NT), (a, b)),
               lambda r, g: (_dot_nn(g, r[1]), _dot_tn(g, r[0])))
_dot_tn.defvjp(lambda a, b: (_bdot_raw(a, b, _TN), (a, b)),
               lambda r, g: (_dot_nt(r[1], g), _dot_nn(r[0], g)))


def _f32dot(a, b):
    return lax.dot_general(a, b, _NN, precision=lax.Precision.HIGHEST, preferred_element_type=F32)


def _sigmoid(x):
    return jax.nn.sigmoid(x)


def _silu(x):
    return x * jax.nn.sigmoid(x)


def _rms(x):
    return x * lax.rsqrt(jnp.mean(x * x, axis=-1, keepdims=True) + EPS)


def _modulate(x, sh, sc):
    return _rms(x) * (1.0 + sc) + sh


def _mm_call(a, b, dims, a_spec, b_spec, o_spec, o_shape, grid, acc_tile, name):
    nk = grid[2]

    def body(a_ref, b_ref, o_ref, *acc):
        p = lax.dot_general(a_ref[...].astype(BF16), b_ref[...].astype(BF16), dims,
                            preferred_element_type=F32)
        if nk == 1:
            o_ref[...] = p.astype(o_ref.dtype)
        else:
            kk = pl.program_id(2)

            @pl.when(kk == 0)
            def _():
                acc[0][...] = p

            @pl.when(kk > 0)
            def _():
                acc[0][...] += p

            @pl.when(kk == nk - 1)
            def _():
                o_ref[...] = acc[0][...].astype(o_ref.dtype)

    return pl.pallas_call(
        body, name=name, grid=grid, in_specs=[a_spec, b_spec], out_specs=o_spec, out_shape=o_shape,
        scratch_shapes=[pltpu.VMEM(acc_tile, F32)] if nk > 1 else [],
        compiler_params=_cparams(dimension_semantics=("parallel", "parallel", "arbitrary")),
    )(a, b)


def _mm(a, b, mode, out_dtype, name, tm=1024, tn=1024, tk=2048):
    if mode == "nn":
        (m, k), (k2, n) = a.shape, b.shape
    elif mode == "nt":
        (m, k), (n, k2) = a.shape, b.shape
    else:
        (k, m), (k2, n) = a.shape, b.shape
    assert k == k2, (a.shape, b.shape, mode)
    tm, tn, tk = _divisor_tile(m, tm), _divisor_tile(n, tn), _divisor_tile(k, tk)
    if mode == "tn":
        a_spec = pl.BlockSpec((tk, tm), lambda i, j, kk: (kk, i))
    else:
        a_spec = pl.BlockSpec((tm, tk), lambda i, j, kk: (i, kk))
    if mode == "nt":
        b_spec = pl.BlockSpec((tn, tk), lambda i, j, kk: (j, kk))
    else:
        b_spec = pl.BlockSpec((tk, tn), lambda i, j, kk: (kk, j))
    return _mm_call(a, b, {"nn": _NN, "nt": _NT, "tn": _TN}[mode], a_spec, b_spec,
                    pl.BlockSpec((tm, tn), lambda i, j, kk: (i, j)), jax.ShapeDtypeStruct((m, n), out_dtype),
                    (m // tm, n // tn, k // tk), (tm, tn), name)


def _wblk_act_spec(rows, gb, nl, split, nb, row_axis, blk_axis):
    if split == 1:
        return pl.BlockSpec((rows, gb * nl), lambda *g: (g[row_axis], g[blk_axis]))
    groups = nb // split // gb
    return pl.BlockSpec((None, rows, gb * nl),
                        lambda *g: (g[blk_axis] // groups, g[row_axis], g[blk_axis] % groups))


def _mm_wblk(a, wb, out_dtype, name, *, gb, row_off=0, split=1, tm=1024):
    m, k = a.shape
    nb, _, nl = wb.shape
    assert (nb // split) % gb == 0
    tm = _divisor_tile(m, tm)

    def body(a_ref, b_ref, o_ref):
        av = a_ref[...].astype(BF16)
        for s in range(gb):
            o_ref[:, s * nl:(s + 1) * nl] = lax.dot_general(
                av, b_ref[s].astype(BF16), _NN, preferred_element_type=F32).astype(o_ref.dtype)

    o_shape = (m, nb * nl) if split == 1 else (split, m, nb // split * nl)
    return pl.pallas_call(
        body, name=name, grid=(nb // gb, m // tm),
        in_specs=[pl.BlockSpec((tm, k), lambda j, i: (i, 0)),
                  pl.BlockSpec((gb, k, nl), lambda j, i: (j, row_off, 0))],
        out_specs=_wblk_act_spec(tm, gb, nl, split, nb, 1, 0),
        out_shape=jax.ShapeDtypeStruct(o_shape, out_dtype),
        compiler_params=_cparams(dimension_semantics=("parallel", "parallel")),
    )(a, wb)


def _mm_wblk_dx(dy, wb, out_dtype, name, *, k, gb, row_off=0, split=1, tm=1024):
    nb, _, nl = wb.shape
    m = dy.shape[-2]
    tm = _divisor_tile(m, tm)
    nk = nb // gb
    per = nb // split
    whole = split > 1 and gb == nb
    assert whole or per % gb == 0

    def body(a_ref, b_ref, o_ref, *acc):
        p = None
        for s in range(gb):
            a_blk = a_ref[s // per, :, (s % per) * nl:(s % per + 1) * nl] if whole else a_ref[:, s * nl:(s + 1) * nl]
            q = lax.dot_general(a_blk.astype(BF16), b_ref[s].astype(BF16), _NT, preferred_element_type=F32)
            p = q if p is None else p + q
        if nk == 1:
            o_ref[...] = p.astype(o_ref.dtype)
        else:
            kk = pl.program_id(1)

            @pl.when(kk == 0)
            def _():
                acc[0][...] = p

            @pl.when(kk > 0)
            def _():
                acc[0][...] += p

            @pl.when(kk == nk - 1)
            def _():
                o_ref[...] = acc[0][...].astype(o_ref.dtype)

    return pl.pallas_call(
        body, name=name, grid=(m // tm, nk),
        in_specs=[pl.BlockSpec((split, tm, per * nl), lambda i, kk: (0, i, 0)) if whole
                  else _wblk_act_spec(tm, gb, nl, split, nb, 0, 1),
                  pl.BlockSpec((gb, k, nl), lambda i, kk: (kk, row_off, 0))],
        out_specs=pl.BlockSpec((tm, k), lambda i, kk: (i, 0)),
        out_shape=jax.ShapeDtypeStruct((m, k), out_dtype),
        scratch_shapes=[pltpu.VMEM((tm, k), F32)] if nk > 1 else [],
        compiler_params=_cparams(dimension_semantics=("parallel", "arbitrary")),
    )(dy, wb)


def _mm_wblk_dw(x, dy, name, *, nb, gb, split=1, tk=1024):
    t, k = x.shape
    assert (nb // split) % gb == 0
    nl = dy.shape[-1] * split // nb
    tk = _divisor_tile(t, tk)
    nk = t // tk

    def body(a_ref, b_ref, o_ref, *acc):
        kk = pl.program_id(1)
        av = a_ref[...].astype(BF16)
        for s in range(gb):
            p = lax.dot_general(av, b_ref[:, s * nl:(s + 1) * nl].astype(BF16), _TN, preferred_element_type=F32)
            if nk == 1:
                o_ref[s] = p.astype(o_ref.dtype)
                continue

            @pl.when(kk == 0)
            def _():
                acc[0][s] = p

            @pl.when(kk > 0)
            def _():
                acc[0][s] += p

        if nk > 1:
            @pl.when(kk == nk - 1)
            def _():
                o_ref[...] = acc[0][...].astype(o_ref.dtype)

    return pl.pallas_call(
        body, name=name, grid=(nb // gb, nk),
        in_specs=[pl.BlockSpec((tk, k), lambda j, kk: (kk, 0)), _wblk_act_spec(tk, gb, nl, split, nb, 1, 0)],
        out_specs=pl.BlockSpec((gb, k, nl), lambda j, kk: (j, 0, 0)),
        out_shape=jax.ShapeDtypeStruct((nb, k, nl), BF16),
        scratch_shapes=[pltpu.VMEM((gb, k, nl), F32)] if nk > 1 else [],
        compiler_params=_cparams(dimension_semantics=("parallel", "arbitrary")),
    )(x, dy)


def _row_specs(rows, tb, nsub):
    return [pl.BlockSpec((tb, nsub * cw), functools.partial(lambda i, off: (i, off), off=off))
            for (_, cw, off) in rows]


def _vec_specs(params):
    return [pl.BlockSpec(p.shape, lambda i: (0, 0)) for p in params]


def _row_fwd(f, rows, params, out_dtypes, *, nsub=1, tb, name):
    t = rows[0][0].shape[0]
    tb = min(tb, t)
    n_r, n_p = len(rows), len(params)
    blk = [jax.ShapeDtypeStruct((tb, cw), F32) for (_, cw, _) in rows]
    blk += [jax.ShapeDtypeStruct(p.shape, F32) for p in params]
    out_avals = jax.eval_shape(f, *blk)

    def body(*refs):
        pv = [r[...] for r in refs[n_r:n_r + n_p]]
        for s in range(nsub):
            vals = [r[:, s * cw:(s + 1) * cw].astype(F32) for r, (_, cw, _) in zip(refs[:n_r], rows)]
            outs = f(*vals, *pv)
            for o_ref, o in zip(refs[n_r + n_p:], outs):
                w = o.shape[1]
                o_ref[:, s * w:(s + 1) * w] = o.astype(o_ref.dtype)

    return pl.pallas_call(
        body, name=name,
        grid=(t // tb,),
        in_specs=_row_specs(rows, tb, nsub) + _vec_specs(params),
        out_specs=[pl.BlockSpec((tb, nsub * av.shape[1]), lambda i: (i, 0)) for av in out_avals],
        out_shape=[jax.ShapeDtypeStruct((t, nsub * av.shape[1]), dt) for av, dt in zip(out_avals, out_dtypes)],
        compiler_params=_cparams(dimension_semantics=("parallel",)),
    )(*[r[0] for r in rows], *params)


def _row_bwd(f, rows, params, cots, row_grad_dtypes, *, nsub=1, tb, name, add_to=None, cot_add=None):
    t = rows[0][0].shape[0]
    tb = min(tb, t)
    n_r, n_p, n_c = len(rows), len(params), len(cots)
    want = [j for j in range(n_r) if row_grad_dtypes[j] is not None]
    cot_add = cot_add or []
    extra = [] if add_to is None else [(add_to[1], rows[add_to[0]][1], 0)]
    n_add_to = len(extra)
    extra += [(arr, cots[ci][1], 0) for ci, arr in cot_add]

    def body(*refs):
        i = pl.program_id(0)
        r_in, p_in = refs[:n_r], refs[n_r:n_r + n_p]
        c_in = refs[n_r + n_p:n_r + n_p + n_c]
        e_in = refs[n_r + n_p + n_c:n_r + n_p + n_c + len(extra)]
        outs = refs[n_r + n_p + n_c + len(extra):]
        pv = [r[...] for r in p_in]
        psum = [None] * n_p
        for s in range(nsub):
            vals = [r[:, s * cw:(s + 1) * cw].astype(F32) for r, (_, cw, _) in zip(r_in, rows)]
            cvals = [r[:, s * cw:(s + 1) * cw].astype(F32) for r, (_, cw, _) in zip(c_in, cots)]
            for (ci, _), e_ref in zip(cot_add, e_in[n_add_to:]):
                cw = cots[ci][1]
                cvals[ci] = cvals[ci] + e_ref[:, s * cw:(s + 1) * cw].astype(F32)
            _, vjp_fn = jax.vjp(f, *vals, *pv)
            grads = vjp_fn(tuple(cvals))
            for o_ref, jr in zip(outs[:len(want)], want):
                cw = rows[jr][1]
                gr = grads[jr]
                if add_to is not None and jr == add_to[0]:
                    gr = gr + e_in[0][:, s * cw:(s + 1) * cw]
                o_ref[:, s * cw:(s + 1) * cw] = gr.astype(o_ref.dtype)
            for jp in range(n_p):
                psum[jp] = grads[n_r + jp] if psum[jp] is None else psum[jp] + grads[n_r + jp]
        for o_ref, g in zip(outs[len(want):], psum):
            @pl.when(i == 0)
            def _():
                o_ref[...] = g

            @pl.when(i > 0)
            def _():
                o_ref[...] += g

    out_specs = [pl.BlockSpec((tb, nsub * rows[jr][1]), lambda i: (i, 0)) for jr in want]
    out_shape = [jax.ShapeDtypeStruct((t, nsub * rows[jr][1]), row_grad_dtypes[jr]) for jr in want]
    out_specs += _vec_specs(params)
    out_shape += [jax.ShapeDtypeStruct(p.shape, F32) for p in params]
    res = pl.pallas_call(
        body, name=name,
        grid=(t // tb,),
        in_specs=_row_specs(rows, tb, nsub) + _vec_specs(params) + _row_specs(cots, tb, nsub)
        + _row_specs(extra, tb, nsub),
        out_specs=out_specs, out_shape=out_shape,
        compiler_params=_cparams(dimension_semantics=("arbitrary",)),
    )(*[r[0] for r in rows], *params, *[c[0] for c in cots], *[e[0] for e in extra])
    return res[:len(want)], res[len(want):]


def _f_mod(x, sh, sc):
    return (_modulate(x, sh, sc),)


def _f_res_mod(x, y, g, sh, sc):
    x1 = x + g * y
    return x1, _modulate(x1, sh, sc)


def _f_res_mod2(x, y, g, sh_a, sc_a, sh_b, sc_b):
    x1 = x + g * y
    return x1, _modulate(x1, sh_a, sc_a), _modulate(x1, sh_b, sc_b)


def _f_qnorm(p, g):
    return (_rms(p) * g * (HEAD ** -0.5),)


def _f_knorm(p, g):
    return (_rms(p) * g,)


def _f_qnorm_aug(p, g):
    lane = lax.broadcasted_iota(jnp.int32, p.shape, 1)
    return (jnp.concatenate([_rms(p) * g * (HEAD ** -0.5), jnp.where(lane < 3, 1.0, 0.0)], axis=1),)


def _f_knorm_aug(p, c0, c1, c2, g):
    lane = lax.broadcasted_iota(jnp.int32, p.shape, 1)
    aug = jnp.where(lane == 0, c0, jnp.where(lane == 1, c1, jnp.where(lane == 2, c2, 0.0)))
    return (jnp.concatenate([_rms(p) * g, aug], axis=1),)


def _split3(a):
    round_bf16 = lambda v: lax.reduce_precision(v, exponent_bits=8, mantissa_bits=7)
    hi = round_bf16(a)
    mid = round_bf16(a - hi)
    lo = round_bf16(a - hi - mid)
    return hi.astype(BF16), mid.astype(BF16), lo.astype(BF16)


def _f_outgate(o, og):
    return (o * _sigmoid(og),)


def _loss_call(x3, f, g2, target, tb):
    t, d = x3.shape
    tb = min(tb, t)

    def body(x_ref, f_ref, g_ref, t_ref, loss_ref, dx_ref, df_ref, dg_ref):
        i = pl.program_id(0)
        fv = f_ref[...]
        g = g_ref[...]
        e = x_ref[...] + g * fv - t_ref[...]
        dx = e * (1.0 / d)
        part = 0.5 * jnp.sum(jnp.sum(e * dx, axis=1, keepdims=True), axis=0, keepdims=True)
        dx_ref[...] = dx
        df_ref[...] = (g * dx).astype(df_ref.dtype)
        dg = jnp.sum(dx * fv, axis=0, keepdims=True)

        @pl.when(i == 0)
        def _():
            loss_ref[...] = jnp.broadcast_to(part, loss_ref.shape)
            dg_ref[...] = dg

        @pl.when(i > 0)
        def _():
            loss_ref[...] += jnp.broadcast_to(part, loss_ref.shape)
            dg_ref[...] += dg

    row = pl.BlockSpec((tb, d), lambda i: (i, 0))
    vec = pl.BlockSpec((1, d), lambda i: (0, 0))
    return pl.pallas_call(
        body, name="loss_head",
        grid=(t // tb,),
        in_specs=[row, row, vec, row],
        out_specs=[pl.BlockSpec((1, LANES), lambda i: (0, 0)), row, row, vec],
        out_shape=[jax.ShapeDtypeStruct((1, LANES), F32), jax.ShapeDtypeStruct((t, d), F32),
                   jax.ShapeDtypeStruct((t, d), BF16), jax.ShapeDtypeStruct((1, d), F32)],
        compiler_params=_cparams(dimension_semantics=("arbitrary",)),
    )(x3, f, g2, target)


def _hg_mask(tb):
    br = lax.broadcasted_iota(jnp.int32, (tb, tb), 0)
    bs = lax.broadcasted_iota(jnp.int32, (tb, tb), 1)
    return jnp.logical_and(br // A_CHUNK == bs // A_CHUNK, bs <= br).astype(F32)


def _hg_consts(mask):
    c = A_CHUNK
    r = lax.broadcasted_iota(jnp.int32, (c, c), 0)
    s = lax.broadcasted_iota(jnp.int32, (c, c), 1)
    return (s <= r).astype(F32), (r <= s).astype(F32), mask > 0.5


def _chunk_apply(mat, x):
    c = mat.shape[0]
    return jnp.concatenate([_f32dot(mat, x[i * c:(i + 1) * c]) for i in range(x.shape[0] // c)], axis=0)


@jax.custom_vjp
def _chunk_cumsum(x, tri, tri_t):
    return _chunk_apply(tri, x)


_chunk_cumsum.defvjp(lambda x, tri, tri_t: (_chunk_apply(tri, x), (tri, tri_t)),
                     lambda r, g: (_chunk_apply(r[1], g), jnp.zeros_like(r[0]), jnp.zeros_like(r[1])))


def _per_chunk(a, b, dims):
    return jnp.stack([_bdot_raw(a[i], b[i], dims) for i in range(a.shape[0])])


@jax.custom_vjp
def _chunk_tn(a, b):
    return _per_chunk(a, b, _TN)


@jax.custom_vjp
def _chunk_nt(a, b):
    return _per_chunk(a, b, _NT)


@jax.custom_vjp
def _chunk_nn(a, b):
    return _per_chunk(a, b, _NN)


_chunk_tn.defvjp(lambda a, b: (_per_chunk(a, b, _TN), (a, b)),
                 lambda r, g: (_chunk_nt(r[1], g), _chunk_nn(r[0], g)))
_chunk_nt.defvjp(lambda a, b: (_per_chunk(a, b, _NT), (a, b)),
                 lambda r, g: (_chunk_nn(g, r[1]), _chunk_tn(g, r[0])))
_chunk_nn.defvjp(lambda a, b: (_per_chunk(a, b, _NN), (a, b)),
                 lambda r, g: (_chunk_nt(g, r[1]), _chunk_tn(r[0], g)))


def _scan_states(decay, m, st):
    sts = []
    for i in range(m.shape[0]):
        sts.append(st)
        st = st * decay[i] + m[i]
    return jnp.stack(sts), st


@jax.custom_vjp
def _state_scan(decay, m, st):
    return _scan_states(decay, m, st)


def _state_scan_fwd(decay, m, st):
    sts, st_out = _scan_states(decay, m, st)
    return (sts, st_out), (decay, sts)


def _state_scan_bwd(res, cts):
    decay, sts = res
    d_sts, g = cts
    d_decay, d_m = [], []
    for i in range(sts.shape[0] - 1, -1, -1):
        d_m.append(g)
        d_decay.append(jnp.sum(g * sts[i], axis=0, keepdims=True))
        g = g * decay[i] + d_sts[i]
    return jnp.stack(d_decay[::-1]), jnp.stack(d_m[::-1]), g


_state_scan.defvjp(_state_scan_fwd, _state_scan_bwd)


def _hg_block(qp, fp, ip, gp, lb, ng, st, tri, tri_t, bd_causal):
    tb = qp.shape[0]
    c = A_CHUNK
    n = tb // c
    q = _silu(qp)
    fg = lb + (1.0 - lb) * _sigmoid(fp)
    logf = jnp.log(fg)
    k = 1.0 - fg
    b3 = _chunk_cumsum(logf, tri, tri_t).reshape(n, c, HEAD)
    pos = lax.broadcasted_iota(jnp.int32, (1, c, 1), 1)
    b_mid = lax.stop_gradient(jnp.sum(jnp.where(pos == c // 2, b3, 0.0), axis=1, keepdims=True))
    b_last = jnp.sum(jnp.where(pos == c - 1, b3, 0.0), axis=1, keepdims=True)
    q3, k3, v3 = q.reshape(n, c, HEAD), k.reshape(n, c, HEAD), ip.reshape(n, c, HEAD)
    scores = _dot_nt((q3 * jnp.exp(b3 - b_mid)).reshape(tb, HEAD), (k3 * jnp.exp(b_mid - b3)).reshape(tb, HEAD))
    o_intra = _dot_nn(jnp.where(bd_causal, scores, 0.0), ip)
    states, st_new = _state_scan(jnp.exp(b_last), _chunk_tn(v3, k3 * jnp.exp(b_last - b3)), st)
    o = o_intra + _chunk_nt(q3 * jnp.exp(b3), states).reshape(tb, HEAD)
    y = _rms(o) * ng * _silu(gp)
    return y, st_new


HG_HEADS = 2


def _hg_specs(tb, nh, rev_nb=None):
    wide = HG_HEADS * HEAD
    per = nh // HG_HEADS

    def row(part):
        if rev_nb is None:
            return pl.BlockSpec((tb, wide), functools.partial(lambda h, i, off: (i, off + h), off=part * per))
        return pl.BlockSpec((tb, wide),
                            functools.partial(lambda h, i, off: (rev_nb - 1 - i, off + h), off=part * per))
    return [row(0), row(1), row(2), row(3),
            pl.BlockSpec((1, wide), lambda h, i: (0, h)), pl.BlockSpec((1, HEAD), lambda h, i: (0, 0)),
            pl.BlockSpec((tb, tb), lambda h, i: (0, 0))]


def _hgrn2_fwd(proj, lb, ng, tb):
    t = proj.shape[0]
    nh = proj.shape[1] // (4 * HEAD)
    tb = min(tb, t)
    nb = t // tb
    wide = HG_HEADS * HEAD

    def body(q_ref, f_ref, i_ref, g_ref, lb_ref, ng_ref, mask_ref, y_ref, s_ref, st_ref):
        i = pl.program_id(1)

        @pl.when(i == 0)
        def _():
            st_ref[...] = jnp.zeros_like(st_ref)

        consts = _hg_consts(mask_ref[...])
        for p in range(HG_HEADS):
            cs = slice(p * HEAD, (p + 1) * HEAD)
            st = st_ref[p]
            s_ref[p, 0] = st
            y, st_new = _hg_block(q_ref[:, cs], f_ref[:, cs], i_ref[:, cs], g_ref[:, cs], lb_ref[:, cs],
                                  ng_ref[...], st, *consts)
            y_ref[:, cs] = y.astype(y_ref.dtype)
            st_ref[p] = st_new

    return pl.pallas_call(
        body, name="hgrn2_fwd",
        grid=(nh // HG_HEADS, nb),
        in_specs=_hg_specs(tb, nh),
        out_specs=[pl.BlockSpec((tb, wide), lambda h, i: (i, h)),
                   pl.BlockSpec((HG_HEADS, 1, HEAD, HEAD), lambda h, i: (h, i, 0, 0))],
        out_shape=[jax.ShapeDtypeStruct((t, nh * HEAD), BF16),
                   jax.ShapeDtypeStruct((nh, nb, HEAD, HEAD), F32)],
        scratch_shapes=[pltpu.VMEM((HG_HEADS, HEAD, HEAD), F32)],
        compiler_params=_cparams(dimension_semantics=("parallel", "arbitrary")),
    )(proj, proj, proj, proj, lb, ng, _hg_mask(tb))


def _hgrn2_bwd(proj, lb, ng, states, dy, tb):
    t = proj.shape[0]
    nh = proj.shape[1] // (4 * HEAD)
    tb = min(tb, t)
    nb = t // tb
    wide = HG_HEADS * HEAD

    def body(q_ref, f_ref, i_ref, g_ref, lb_ref, ng_ref, mask_ref, s_ref, dy_ref,
             dp_ref, dlb_ref, dng_ref, dst_ref):
        h, i = pl.program_id(0), pl.program_id(1)
        consts = _hg_consts(mask_ref[...])

        @pl.when(i == 0)
        def _():
            dst_ref[...] = jnp.zeros_like(dst_ref)
            dlb_ref[...] = jnp.zeros_like(dlb_ref)

        @pl.when(jnp.logical_and(i == 0, h == 0))
        def _():
            dng_ref[...] = jnp.zeros_like(dng_ref)

        def fn(qp, fp, ip, gp, lbx, ngx, stx):
            return _hg_block(qp, fp, ip, gp, lbx, ngx, stx, *consts)

        for p in range(HG_HEADS):
            cs = slice(p * HEAD, (p + 1) * HEAD)
            _, vjp_fn = jax.vjp(fn, q_ref[:, cs], f_ref[:, cs], i_ref[:, cs], g_ref[:, cs], lb_ref[:, cs],
                                ng_ref[...], s_ref[p, 0])
            *gparts, glb, gng, dst = vjp_fn((dy_ref[:, cs].astype(F32), dst_ref[p]))
            for part, gpart in enumerate(gparts):
                dp_ref[part, :, cs] = gpart.astype(dp_ref.dtype)
            dst_ref[p] = dst
            dlb_ref[:, cs] += glb
            dng_ref[...] += gng

    rev = lambda h, i: (nb - 1 - i, h)
    return pl.pallas_call(
        body, name="hgrn2_bwd",
        grid=(nh // HG_HEADS, nb),
        in_specs=_hg_specs(tb, nh, rev_nb=nb) + [
            pl.BlockSpec((HG_HEADS, 1, HEAD, HEAD), lambda h, i: (h, nb - 1 - i, 0, 0)),
            pl.BlockSpec((tb, wide), rev)],
        out_specs=[pl.BlockSpec((4, tb, wide), lambda h, i: (0, nb - 1 - i, h)),
                   pl.BlockSpec((1, wide), lambda h, i: (0, h)), pl.BlockSpec((1, HEAD), lambda h, i: (0, 0))],
        out_shape=[jax.ShapeDtypeStruct((4, t, nh * HEAD), BF16),
                   jax.ShapeDtypeStruct((1, nh * HEAD), F32), jax.ShapeDtypeStruct((1, HEAD), F32)],
        scratch_shapes=[pltpu.VMEM((HG_HEADS, HEAD, HEAD), F32)],
        compiler_params=_cparams(dimension_semantics=("arbitrary", "arbitrary")),
    )(proj, proj, proj, proj, lb, ng, _hg_mask(tb), states, dy)


def _fgate_consts(cb):
    r = lax.broadcasted_iota(jnp.int32, (cb, cb), 0)
    s = lax.broadcasted_iota(jnp.int32, (cb, cb), 1)
    return (r <= s).astype(F32), (r >= s).astype(F32)


def _fgate_fwd(xt, bias, cb=512):
    nh, t = xt.shape
    cb = min(cb, t)

    def body(x_ref, b_ref, o_ref):
        upper, _ = _fgate_consts(cb)
        carry = jnp.zeros((nh, 1), F32)
        for blk in range(t // cb):
            z = x_ref[:, blk * cb:(blk + 1) * cb] + b_ref[...]
            logf = jnp.minimum(z, 0.0) - jnp.log(1.0 + jnp.exp(-jnp.abs(z)))
            cs = _f32dot(logf, upper) + carry
            o_ref[:, blk * cb:(blk + 1) * cb] = cs
            carry = cs[:, cb - 1:cb]

    vm = pl.BlockSpec(memory_space=pltpu.VMEM)
    return pl.pallas_call(
        body, name="fgate_fwd", in_specs=[vm, vm], out_specs=vm,
        out_shape=jax.ShapeDtypeStruct((nh, t), F32), compiler_params=_cparams(),
    )(xt, bias)


def _fgate_bwd(xt, bias, dft, cb=512):
    nh, t = xt.shape
    cb = min(cb, t)
    nblk = t // cb

    def body(x_ref, b_ref, d_ref, dx_ref, db_ref):
        _, lower = _fgate_consts(cb)
        carry = jnp.zeros((nh, 1), F32)
        db = jnp.zeros((nh, 1), F32)
        for blk in range(nblk - 1, -1, -1):
            sl = slice(blk * cb, (blk + 1) * cb)
            dlogf = _f32dot(d_ref[:, sl], lower) + carry
            carry = dlogf[:, 0:1]
            z = x_ref[:, sl] + b_ref[...]
            dz = dlogf * (1.0 - _sigmoid(z))
            dx_ref[:, sl] = dz
            db = db + jnp.sum(dz, axis=1, keepdims=True)
        db_ref[...] = db

    vm = pl.BlockSpec(memory_space=pltpu.VMEM)
    return pl.pallas_call(
        body, name="fgate_bwd", in_specs=[vm, vm, vm], out_specs=[vm, vm],
        out_shape=[jax.ShapeDtypeStruct((nh, t), F32), jax.ShapeDtypeStruct((nh, 1), F32)],
        compiler_params=_cparams(),
    )(xt, bias, dft)


ATTN_GROUPS = 4
ATTN_FWD_HEADS = 2


def _attn_fwd(q, k, v, f_grp, blk):
    t, width = v.shape
    nh = width // HEAD
    nq = t // blk
    hpg = nh // ATTN_GROUPS

    def body(q_ref, k_ref, v_ref, fc_ref, o_ref, lse_ref):
        i = pl.program_id(0)
        tri = (lax.broadcasted_iota(jnp.int32, (blk, blk), 1) <= lax.broadcasted_iota(jnp.int32, (blk, blk), 0))
        for h0 in range(0, nh, ATTN_FWD_HEADS):
            heads = range(h0, min(h0 + ATTN_FWD_HEADS, nh))

            def tile(j, carries, masked):
                rs = pl.ds(pl.multiple_of(j * blk, blk), blk)
                out = []
                for h, (m, l, acc) in zip(heads, carries):
                    cs = slice(h * HEAD, (h + 1) * HEAD)
                    cs2 = slice(2 * h * HEAD, 2 * (h + 1) * HEAD)
                    s = _bdot_raw(q_ref[:, cs2], k_ref[rs, cs2], _NT)
                    if masked:
                        s = jnp.where(tri, s, NEG_INF)
                    m_new = jnp.maximum(m, jnp.max(s, axis=1, keepdims=True))
                    p = jnp.exp(s - m_new)
                    alpha = jnp.exp(m - m_new)
                    l_new = alpha * l + jnp.sum(p, axis=1, keepdims=True)
                    out.append((m_new, l_new, alpha * acc + _bdot_raw(p, v_ref[rs, cs], _NN)))
                return tuple(out)

            init = tuple((jnp.full((blk, 1), NEG_INF, F32), jnp.zeros((blk, 1), F32), jnp.zeros((blk, HEAD), F32))
                         for _ in heads)
            carries = lax.fori_loop(0, i, lambda j, c: tile(j, c, False), init)
            for h, (m, l, acc) in zip(heads, tile(i, carries, True)):
                o_ref[:, h * HEAD:(h + 1) * HEAD] = acc / l
                g, hh = divmod(h, hpg)
                lse_ref[g, :, hh:hh + 1] = m + jnp.log(l) + fc_ref[g, :, hh:hh + 1]

    vm = pl.BlockSpec(memory_space=pltpu.VMEM)
    stat = pl.BlockSpec((ATTN_GROUPS, blk, hpg), lambda i: (0, i, 0))
    return pl.pallas_call(
        body, name="fox_attn_fwd",
        grid=(nq,),
        in_specs=[pl.BlockSpec((blk, 2 * width), lambda i: (i, 0)), vm, vm, stat],
        out_specs=[pl.BlockSpec((blk, width), lambda i: (i, 0)), stat],
        out_shape=[jax.ShapeDtypeStruct((t, width), F32), jax.ShapeDtypeStruct((ATTN_GROUPS, t, hpg), F32)],
        compiler_params=_cparams(dimension_semantics=("parallel",)),
    )(q, k, v, f_grp)


def _attn_delta(do, o, tb):
    t, width = o.shape
    nh = width // HEAD
    hpg = nh // ATTN_GROUPS
    tb = min(tb, t)

    def body(do_ref, o_ref, dl_ref):
        for h in range(nh):
            cs = slice(h * HEAD, (h + 1) * HEAD)
            g, hh = divmod(h, hpg)
            dl_ref[g, :, hh:hh + 1] = jnp.sum(do_ref[:, cs].astype(F32) * o_ref[:, cs], axis=1, keepdims=True)

    wide = pl.BlockSpec((tb, width), lambda i: (i, 0))
    return pl.pallas_call(body, name="fox_attn_delta", grid=(t // tb,), in_specs=[wide, wide],
                          out_specs=pl.BlockSpec((ATTN_GROUPS, tb, hpg), lambda i: (0, i, 0)),
                          out_shape=jax.ShapeDtypeStruct((ATTN_GROUPS, t, hpg), F32),
                          compiler_params=_cparams(dimension_semantics=("parallel",)))(do, o)


def _attn_bwd(q, k, v, f_grp, do, lse, delta, blk):
    t, width = v.shape
    nh = width // HEAD
    nq = t // blk
    hpg = nh // ATTN_GROUPS
    gw = hpg * HEAD

    def body(q_ref, do_ref, k_ref, v_ref, fc_ref, lse_ref, dl_ref,
             dq_ref, dk_ref, dv_ref, dfc_ref, dfr_ref):
        g, j = pl.program_id(0), pl.program_id(1)
        tri = (lax.broadcasted_iota(jnp.int32, (blk, blk), 1) <= lax.broadcasted_iota(jnp.int32, (blk, blk), 0))

        @pl.when(j == 0)
        def _():
            dq_ref[...] = jnp.zeros_like(dq_ref)
            dfc_ref[...] = jnp.zeros_like(dfc_ref)

        def tile(i, carries, masked):
            rs = pl.ds(pl.multiple_of(i * blk, blk), blk)
            out = []
            for h, (dk, dv, dfs) in enumerate(carries):
                cs = slice(h * HEAD, (h + 1) * HEAD)
                cs2 = slice(2 * h * HEAD, 2 * (h + 1) * HEAD)
                csq = slice(2 * h * HEAD, (2 * h + 1) * HEAD)
                qi = q_ref[rs, csq]
                doi = do_ref[rs, cs]
                bias = fc_ref[0, rs, h:h + 1] - lse_ref[0, rs, h:h + 1]
                p = jnp.exp(_bdot_raw(q_ref[rs, cs2], k_ref[:, cs2], _NT) + bias)
                if masked:
                    p = jnp.where(tri, p, 0.0)
                ds = p * (_bdot_raw(doi, v_ref[:, cs], _NT) - dl_ref[0, rs, h:h + 1])
                dsb = ds.astype(BF16)
                dq_ref[rs, cs] += _bdot_raw(dsb, k_ref[:, csq], _NN)
                dfc_ref[0, rs, h:h + 1] += jnp.sum(ds, axis=1, keepdims=True)
                out.append((dk + _bdot_raw(dsb, qi, _TN), dv + _bdot_raw(p, doi, _TN),
                            dfs - jnp.sum(ds, axis=0, keepdims=True)))
            return tuple(out)

        init = tuple((jnp.zeros((blk, HEAD), F32), jnp.zeros((blk, HEAD), F32), jnp.zeros((1, blk), F32))
                     for _ in range(hpg))
        carries = lax.fori_loop(j + 1, nq, lambda i, c: tile(i, c, False), tile(j, init, True))
        for h, (dk, dv, dfs) in enumerate(carries):
            cs = slice(h * HEAD, (h + 1) * HEAD)
            dk_ref[:, cs] = dk
            dv_ref[:, cs] = dv.astype(dv_ref.dtype)
            dfr_ref[0, 0, h:h + 1, :] = dfs

    once = pl.Buffered(1)
    stat = pl.BlockSpec((1, t, hpg), lambda g, j: (g, 0, 0), pipeline_mode=once)
    kv_blk = pl.BlockSpec((blk, gw), lambda g, j: (j, g))
    frow = pl.BlockSpec((1, 1, hpg, blk), lambda g, j: (g, j, 0, 0))
    dq, dk, dv, dfc, dfr = pl.pallas_call(
        body, name="fox_attn_bwd",
        grid=(ATTN_GROUPS, nq),
        in_specs=[pl.BlockSpec((t, 2 * gw), lambda g, j: (0, g), pipeline_mode=once),
                  pl.BlockSpec((t, gw), lambda g, j: (0, g), pipeline_mode=once),
                  pl.BlockSpec((blk, 2 * gw), lambda g, j: (j, g)), kv_blk, stat, stat, stat],
        out_specs=[pl.BlockSpec((t, gw), lambda g, j: (0, g)), kv_blk, kv_blk,
                   pl.BlockSpec((1, t, hpg), lambda g, j: (g, 0, 0)), frow],
        out_shape=[jax.ShapeDtypeStruct((t, width), F32), jax.ShapeDtypeStruct((t, width), F32),
                   jax.ShapeDtypeStruct((t, width), BF16), jax.ShapeDtypeStruct((ATTN_GROUPS, t, hpg), F32),
                   jax.ShapeDtypeStruct((ATTN_GROUPS, nq, hpg, blk), F32)],
        compiler_params=_cparams(dimension_semantics=("parallel", "arbitrary")),
    )(q, do, k, v, f_grp, lse, delta)
    return dq, dk, dv, dfc, dfr


SUBLANES = 8


def _shift_down(u, n):
    r = pltpu.roll(u, n, 0)
    row = lax.broadcasted_iota(jnp.int32, (SUBLANES, u.shape[1]), 0)
    return jnp.concatenate([jnp.where(row < n, 0.0, r[:SUBLANES]), r[SUBLANES:]], axis=0)


def _shift_up(u, n):
    t = u.shape[0]
    r = pltpu.roll(u, t - n, 0)
    row = lax.broadcasted_iota(jnp.int32, (SUBLANES, u.shape[1]), 0)
    return jnp.concatenate([r[:t - SUBLANES], jnp.where(row >= SUBLANES - n, 0.0, r[t - SUBLANES:])], axis=0)


def _convglu_specs(t):
    return [pl.BlockSpec((2, t, LANES), lambda j: (0, 0, j)),
            pl.BlockSpec((2, CONV_TAPS, LANES), lambda j: (0, 0, j)),
            pl.BlockSpec((2, 1, LANES), lambda j: (0, 0, j))]


def _convglu_fwd(u, cw, cb):
    _, t, fp = u.shape

    def body(u_ref, w_ref, b_ref, a_ref, c_ref):
        c = []
        for hf in range(2):
            uv, w = u_ref[hf].astype(F32), w_ref[hf]
            c.append(w[0:1] * _shift_down(uv, 2) + w[1:2] * _shift_down(uv, 1) + w[2:3] * uv + b_ref[hf])
            c_ref[hf] = c[hf].astype(c_ref.dtype)
        a_ref[...] = (_silu(c[0]) * c[1]).astype(a_ref.dtype)

    return pl.pallas_call(
        body, name="convglu_fwd",
        grid=(fp // LANES,),
        in_specs=_convglu_specs(t),
        out_specs=[pl.BlockSpec((t, LANES), lambda j: (0, j)), pl.BlockSpec((2, t, LANES), lambda j: (0, 0, j))],
        out_shape=[jax.ShapeDtypeStruct((t, fp), BF16), jax.ShapeDtypeStruct((2, t, fp), BF16)],
        compiler_params=_cparams(dimension_semantics=("parallel",)),
    )(u, cw, cb)


def _convglu_bwd(u, c, cw, da):
    _, t, fp = u.shape

    def body(u_ref, c_ref, w_ref, da_ref, du_ref, dw_ref, db_ref):
        gc, vc = c_ref[0].astype(F32), c_ref[1].astype(F32)
        sg = _sigmoid(gc)
        dav = da_ref[...].astype(F32)
        dcs = [dav * vc * (sg * (1.0 + gc * (1.0 - sg))), dav * (gc * sg)]
        for hf in range(2):
            dc, w, uv = dcs[hf], w_ref[hf], u_ref[hf].astype(F32)
            dc1, dc2 = _shift_up(dc, 1), _shift_up(dc, 2)
            du_ref[hf] = (w[2:3] * dc + w[1:2] * dc1 + w[0:1] * dc2).astype(du_ref.dtype)
            dw_ref[hf, 0:1, :] = jnp.sum(dc2 * uv, axis=0, keepdims=True)
            dw_ref[hf, 1:2, :] = jnp.sum(dc1 * uv, axis=0, keepdims=True)
            dw_ref[hf, 2:3, :] = jnp.sum(dc * uv, axis=0, keepdims=True)
            db_ref[hf] = jnp.sum(dc, axis=0, keepdims=True)

    pair, taps, bias = _convglu_specs(t)
    return pl.pallas_call(
        body, name="convglu_bwd",
        grid=(fp // LANES,),
        in_specs=[pair, pair, taps, pl.BlockSpec((t, LANES), lambda j: (0, j))],
        out_specs=[pair, taps, bias],
        out_shape=[jax.ShapeDtypeStruct((2, t, fp), BF16), jax.ShapeDtypeStruct((2, CONV_TAPS, fp), F32),
                   jax.ShapeDtypeStruct((2, 1, fp), F32)],
        compiler_params=_cparams(dimension_semantics=("parallel",)),
    )(u, c, cw, da)


def _local_step(x, target, mods, lb, small, pre_w, get_w, put_g, *, tb=512, attn_blk=512):
    t, d = x.shape
    nh = d // HEAD
    nb = NDEV
    wts = {}
    vec = lambda *names: [mods[n] for n in names]

    def ffn_fwd(h2, l):
        u = _mm_wblk(h2, wts[f"up{l}"], BF16, f"ffn{l}_up", gb=nb // 2, split=2, tm=512)
        a, c = _convglu_fwd(u, small[f"conv_w{l}"], small[f"conv_b{l}"])
        f = _mm(a, wts[f"down{l}"], "nn", F32, f"ffn{l}_down", tk=4096)
        return (u, c), a, f

    def ffn_bwd(df, h2, uc, a, l):
        u, c = uc
        da = _mm(df, wts[f"down{l}"], "nt", BF16, f"ffn{l}_down_dx", tn=1536)
        dwd = _mm(a, df, "tn", BF16, f"ffn{l}_down_dw", tm=768, tk=t)
        du, dcw, dcb = _convglu_bwd(u, c, small[f"conv_w{l}"], da)
        dh2 = _mm_wblk_dx(du, wts[f"up{l}"], BF16, f"ffn{l}_up_dx", k=d, gb=nb // 2, split=2, tm=1024)
        dwu = _mm_wblk_dw(h2, du, f"ffn{l}_up_dw", nb=nb, gb=1, split=2, tk=t)
        return dh2, dwu, dwd, dcw, dcb

    (h_a,) = _row_fwd(_f_mod, [(x, d, 0)], vec("sh1_0", "sc1_0"), [BF16], tb=tb, name="l0_mod1")
    wts.update(get_w("l0a", h_a))
    proj_a = _mm_wblk(h_a, wts["a_in"], F32, "a_in", gb=nb // 2)
    ypre, states = _hgrn2_fwd(proj_a, lb, small["a_norm_g"], tb)
    pre_w("l0b", ypre)
    wts.update(get_w("l0b", ypre))
    y_a = _mm(ypre, wts["a_out"], "nn", F32, "a_out")
    x1, h2_0 = _row_fwd(_f_res_mod, [(x, d, 0), (y_a, d, 0)], vec("g1_0", "sh2_0", "sc2_0"), [F32, BF16],
                        tb=tb, name="l0_res_mod2")
    u0, a0, f0 = ffn_fwd(h2_0, 0)
    x2, h_kv, h_q = _row_fwd(_f_res_mod2, [(x1, d, 0), (f0, d, 0)],
                             [mods["g2_0"] + pre_w("l1", f0)] + vec("kv_sh", "kv_sc", "sh1_1", "sc1_1"),
                             [F32, BF16, BF16], tb=tb, name="l0_res_kvmod_qmod")
    wts.update(get_w("l1", h_kv))
    proj_k = _mm(h_kv, wts["kv_k"], "nt", F32, "k_proj")
    v_b = _mm(h_kv, wts["kv_v"], "nt", BF16, "v_proj")
    proj_f = _mm(h_kv, wts["kv_f"], "nt", F32, "kv_fproj")
    f_logit_t = proj_f[:, :nh].T
    f_bias = small["kv_b_f"].reshape(nh, 1)
    f_t = _fgate_fwd(f_logit_t, f_bias)
    f_grp = f_t.reshape(ATTN_GROUPS, nh // ATTN_GROUPS, t).transpose(0, 2, 1)
    (k_n,) = _row_fwd(_f_knorm_aug, [(proj_k, HEAD, 0)] + [(piece, 1, 0) for piece in _split3(-f_t.T)],
                      [small["k_norm_g"]], [BF16], nsub=nh, tb=tb, name="k_norm")
    proj_q = _mm_wblk(h_q, wts["b_q"], F32, "b_q", gb=nb)
    (q_n,) = _row_fwd(_f_qnorm_aug, [(proj_q, HEAD, 0)], [small["q_norm_g"]], [BF16], nsub=nh, tb=tb,
                      name="q_norm")
    o_att, lse = _attn_fwd(q_n, k_n, v_b, f_grp, attn_blk)
    (z,) = _row_fwd(_f_outgate, [(o_att, HEAD, 0), (proj_q, HEAD, 1)], [], [BF16], nsub=nh, tb=tb, name="out_gate")
    y_b = _mm(z, wts["b_out"], "nn", F32, "b_out")
    x3, h2_1 = _row_fwd(_f_res_mod, [(x2, d, 0), (y_b, d, 0)], vec("g1_1", "sh2_1", "sc2_1"), [F32, BF16],
                        tb=tb, name="l1_res_mod2")
    u1, a1, f1 = ffn_fwd(h2_1, 1)
    loss, dx4, df1, dg2_1 = _loss_call(x3, f1, mods["g2_1"], target, tb)

    g = {}
    dmods = {"g2_1": dg2_1}
    dh2, g["up1"], g["down1"], g["conv_w1"], g["conv_b1"] = ffn_bwd(df1, h2_1, u1, a1, 1)
    (dx2, dy_b), (dmods["g1_1"], dmods["sh2_1"], dmods["sc2_1"]) = _row_bwd(
        _f_res_mod, [(x2, d, 0), (y_b, d, 0)], vec("g1_1", "sh2_1", "sc2_1"),
        [(dx4, d, 0), (dh2, d, 0)], [F32, BF16], tb=tb, name="l1_res_mod2_bwd")
    dz = _mm(dy_b, wts["b_out"], "nt", BF16, "b_out_dx")
    g["b_out"] = _mm(z, dy_b, "tn", BF16, "b_out_dw", tk=t)
    (do_att, dog), _ = _row_bwd(_f_outgate, [(o_att, HEAD, 0), (proj_q, HEAD, 1)], [], [(dz, HEAD, 0)],
                                [BF16, BF16], nsub=nh, tb=tb, name="out_gate_bwd")
    delta = _attn_delta(do_att, o_att, tb)
    dq_n, dk_n, dv, dfc_q, dfr_k = _attn_bwd(q_n, k_n, v_b, f_grp, do_att, lse, delta, attn_blk)
    (dpq,), (g["q_norm_g"],) = _row_bwd(_f_qnorm, [(proj_q, HEAD, 0)], [small["q_norm_g"]],
                                        [(dq_n, HEAD, 0)], [BF16], nsub=nh, tb=tb, name="q_norm_bwd")
    dproj_q = jnp.concatenate([dpq, dog], axis=1)
    dh_q = _mm_wblk_dx(dproj_q, wts["b_q"], BF16, "b_q_dx", k=d, gb=nb)
    g["b_q"] = _mm_wblk_dw(h_q, dproj_q, "b_q_dw", nb=nb, gb=nb // 4, tk=t)
    (dpk,), (g["k_norm_g"],) = _row_bwd(_f_knorm, [(proj_k, HEAD, 0)], [small["k_norm_g"]],
                                        [(dk_n, HEAD, 0)], [BF16], nsub=nh, tb=tb, name="k_norm_bwd")
    df_t = dfc_q.transpose(0, 2, 1).reshape(nh, t) + dfr_k.transpose(0, 2, 1, 3).reshape(nh, t)
    dflogit_t, g["kv_b_f"] = _fgate_bwd(f_logit_t, f_bias, df_t)
    dproj_f = jnp.pad(dflogit_t.T, ((0, 0), (0, LANES - nh))).astype(BF16)
    dh_kv = _mm(dpk, wts["kv_k"], "nn", BF16, "k_proj_dx")
    dh_kv_v = _mm(dv, wts["kv_v"], "nn", BF16, "v_proj_dx")
    dh_kv_f = _mm(dproj_f, wts["kv_f"], "nn", BF16, "kv_fproj_dx")
    g["kv_k"] = _mm(dpk, h_kv, "tn", BF16, "k_proj_dw", tk=t)
    g["kv_v"] = _mm(dv, h_kv, "tn", BF16, "v_proj_dw", tk=t)
    g["kv_f"] = _mm(dproj_f, h_kv, "tn", F32, "kv_fproj_dw", tk=1024)
    sent = put_g("l1", {n: g.pop(n) for n in ("b_out", "b_q", "kv_k", "kv_v", "kv_f", "up1", "down1")})
    (dx1, df0), (dmods["g2_0"], dmods["kv_sh"], dmods["kv_sc"], dmods["sh1_1"], dmods["sc1_1"]) = _row_bwd(
        _f_res_mod2, [(x1, d, 0), (f0, d, 0)], [mods["g2_0"] + sent] + vec("kv_sh", "kv_sc", "sh1_1", "sc1_1"),
        [(dx2, d, 0), (dh_kv, d, 0), (dh_q, d, 0)], [F32, BF16], tb=tb, name="l0_res_kvmod_qmod_bwd",
        cot_add=[(1, dh_kv_v), (1, dh_kv_f)])
    dh2, g["up0"], g["down0"], g["conv_w0"], g["conv_b0"] = ffn_bwd(df0, h2_0, u0, a0, 0)
    (dx0, dy_a), (dmods["g1_0"], dmods["sh2_0"], dmods["sc2_0"]) = _row_bwd(
        _f_res_mod, [(x, d, 0), (y_a, d, 0)], vec("g1_0", "sh2_0", "sc2_0"),
        [(dx1, d, 0), (dh2, d, 0)], [F32, BF16], tb=tb, name="l0_res_mod2_bwd")
    dypre = _mm(dy_a, wts["a_out"], "nt", BF16, "a_out_dx")
    g["a_out"] = _mm(ypre, dy_a, "tn", BF16, "a_out_dw", tk=t)
    sent = put_g("l0b", {n: g.pop(n) for n in ("a_out", "up0", "down0")})
    dproj_a, dlb, g["a_norm_g"] = _hgrn2_bwd(proj_a, lb + sent, small["a_norm_g"], states, dypre, tb)
    dh_a = _mm_wblk_dx(dproj_a, wts["a_in"], BF16, "a_in_dx", k=d, gb=nb, split=4, tm=512)
    put_g("l0a", {"a_in": _mm_wblk_dw(h_a, dproj_a, "a_in_dw", nb=nb, gb=1, split=4, tk=t)})
    (grad_x,), (dmods["sh1_0"], dmods["sc1_0"]) = _row_bwd(
        _f_mod, [(x, d, 0)], vec("sh1_0", "sc1_0"), [(dh_a, d, 0)], [F32], tb=tb, name="l0_mod1_bwd",
        add_to=(0, dx0))
    return loss, grad_x, dmods, dlb, g


def _position():
    return lax.axis_index("x"), lax.axis_index("y"), lax.axis_index("c")


_XCHG_EFFECT = pltpu.SideEffectType.DATAFLOW_SIDE_EFFECTING
ALL_PEERS = (1, 2, 3, 4, 5, 6, 7)
SAME_CORE = (2, 4, 6)


def _xchg_copies(src_refs, land_refs, send_sems, recv_sems, local_sems, scatter, rels):
    x, y, cc = _position()
    me = 4 * x + 2 * y + cc
    remote, local = [], []
    for a, (src, land) in enumerate(zip(src_refs, land_refs)):
        local.append(pltpu.make_async_copy(src.at[me] if scatter else src, land.at[me], local_sems.at[a]))
        for idx, rel in enumerate(rels):
            px = 1 - x if rel & 4 else x
            py = 1 - y if rel & 2 else y
            pc = 1 - cc if rel & 1 else cc
            k = len(rels) * a + idx
            remote.append(pltpu.make_async_remote_copy(
                src_ref=src.at[4 * px + 2 * py + pc] if scatter else src, dst_ref=land.at[me],
                send_sem=send_sems.at[k], recv_sem=recv_sems.at[k], device_id=(px, py, pc), device_id_type=_MESH))
    return remote, local


def _xchg_start(srcs, scatter, rels, after, name):
    n = len(srcs)
    lands = [lax.empty(s.shape if scatter else (NDEV, *s.shape), s.dtype) for s in srcs]

    def body(*refs):
        remote, local = _xchg_copies(refs[:n], refs[n:2 * n], *refs[2 * n + 1:2 * n + 4], scatter, rels)
        for cp in local + remote:
            cp.start()
        token = refs[-1]
        token[...] = jnp.zeros_like(token)

    hbm = pl.BlockSpec(memory_space=pltpu.HBM)
    sem = pl.BlockSpec(memory_space=pltpu.SEMAPHORE)
    out = pl.pallas_call(
        body, name=name,
        out_shape=(pltpu.SemaphoreType.DMA((len(rels) * n,)), pltpu.SemaphoreType.DMA((len(rels) * n,)),
                   pltpu.SemaphoreType.DMA((n,)),
                   *[pltpu.HBM(a.shape, a.dtype) for a in srcs + lands], jax.ShapeDtypeStruct((8, LANES), F32)),
        in_specs=[hbm] * (2 * n) + [pl.BlockSpec(memory_space=pl.ANY)],
        out_specs=(sem, sem, sem, *[hbm] * (2 * n), pl.BlockSpec(memory_space=pltpu.VMEM)),
        input_output_aliases={i: 3 + i for i in range(2 * n)},
        compiler_params=pltpu.CompilerParams(has_side_effects=_XCHG_EFFECT),
    )(*[pltpu.with_memory_space_constraint(a, pltpu.HBM) for a in srcs + lands], after)
    return out[:-1], out[-1][0, 0]


def _xchg_wait(handles, after, scatter, rels, name):
    n = (len(handles) - 3) // 2

    def body(*refs):
        remote, local = _xchg_copies(refs[:n], refs[n:2 * n], *refs[2 * n:2 * n + 3], scatter, rels)
        for cp in remote:
            cp.wait_send()
            cp.wait_recv()
        for cp in local:
            cp.wait()

    hbm = pl.BlockSpec(memory_space=pltpu.HBM)
    sem = pl.BlockSpec(memory_space=pltpu.SEMAPHORE)
    thru = list(handles[3:])
    afters = list(after) if isinstance(after, (list, tuple)) else [after]
    out = pl.pallas_call(
        body, name=name,
        out_shape=tuple(pltpu.HBM(a.shape, a.dtype) for a in thru),
        in_specs=[hbm] * (2 * n) + [sem, sem, sem] + [pl.BlockSpec(memory_space=pl.ANY)] * len(afters),
        out_specs=tuple([hbm] * (2 * n)),
        input_output_aliases={i: i for i in range(2 * n)},
        compiler_params=pltpu.CompilerParams(has_side_effects=_XCHG_EFFECT),
    )(*thru, *handles[:3], *afters)
    return list(out[n:])


def _sibling_copies(land_refs, send_sems, recv_sems):
    x, y, cc = _position()

    def copy(a, q, core):
        slot = land_refs[a].at[2 * q + core]
        return pltpu.make_async_remote_copy(
            src_ref=slot, dst_ref=slot, send_sem=send_sems.at[NCHIP * a + q], recv_sem=recv_sems.at[NCHIP * a + q],
            device_id=(x, y, 1 - cc), device_id_type=_MESH)

    pairs = [(a, q) for a in range(len(land_refs)) for q in range(NCHIP)]
    return [copy(a, q, cc) for a, q in pairs], [copy(a, q, 1 - cc) for a, q in pairs]


def _sibling_forward_start(lands, name):
    n = len(lands)

    def body(*refs):
        sends, _ = _sibling_copies(refs[:n], refs[n], refs[n + 1])
        for cp in sends:
            cp.start()
        refs[-1][...] = jnp.zeros_like(refs[-1])

    hbm = pl.BlockSpec(memory_space=pltpu.HBM)
    sem = pl.BlockSpec(memory_space=pltpu.SEMAPHORE)
    out = pl.pallas_call(
        body, name=name,
        out_shape=(pltpu.SemaphoreType.DMA((NCHIP * n,)), pltpu.SemaphoreType.DMA((NCHIP * n,)),
                   *[pltpu.HBM(a.shape, a.dtype) for a in lands], jax.ShapeDtypeStruct((8, LANES), F32)),
        in_specs=[hbm] * n,
        out_specs=(sem, sem, *[hbm] * n, pl.BlockSpec(memory_space=pltpu.VMEM)),
        input_output_aliases={i: 2 + i for i in range(n)},
        compiler_params=pltpu.CompilerParams(has_side_effects=_XCHG_EFFECT),
    )(*lands)
    return out[:-1], out[-1][0, 0]


def _sibling_forward_wait(handles, after, name):
    n = len(handles) - 2

    def body(*refs):
        sends, arrivals = _sibling_copies(refs[:n], refs[n], refs[n + 1])
        for cp in sends:
            cp.wait_send()
        for cp in arrivals:
            cp.wait_recv()

    hbm = pl.BlockSpec(memory_space=pltpu.HBM)
    sem = pl.BlockSpec(memory_space=pltpu.SEMAPHORE)
    lands = list(handles[2:])
    return list(pl.pallas_call(
        body, name=name,
        out_shape=tuple(pltpu.HBM(a.shape, a.dtype) for a in lands),
        in_specs=[hbm] * n + [sem, sem, pl.BlockSpec(memory_space=pl.ANY)],
        out_specs=tuple([hbm] * n),
        input_output_aliases={i: i for i in range(n)},
        compiler_params=pltpu.CompilerParams(has_side_effects=_XCHG_EFFECT),
    )(*lands, *handles[:2], after))


def _slab_sum(slabs, name, tr=None):
    n, r, c = slabs.shape
    tr = r if tr is None else tr

    def body(s_ref, o_ref):
        acc = s_ref[0].astype(F32)
        for q in range(1, n):
            acc = acc + s_ref[q].astype(F32)
        o_ref[...] = acc

    return pl.pallas_call(body, name=name, grid=(r // tr,),
                          in_specs=[pl.BlockSpec((n, tr, c), lambda i: (0, i, 0))],
                          out_specs=pl.BlockSpec((tr, c), lambda i: (i, 0)),
                          out_shape=jax.ShapeDtypeStruct((r, c), F32),
                          compiler_params=_cparams(dimension_semantics=("parallel",)))(slabs)


def _ada_fwd(c_all, ada_w, kv_ada_w, logits):
    rows, d = c_all.shape
    n0, nkv = ada_w.shape[2], kv_ada_w.shape[1]

    def body(c_ref, w_ref, kw_ref, lg_ref, part_ref, cact_ref, lb_ref):
        ca = _silu(c_ref[...])
        cact_ref[...] = ca
        part_ref[:, 0:n0] = _bdot_raw(ca, w_ref[0], _NN)
        part_ref[:, n0:2 * n0] = _bdot_raw(ca, w_ref[1], _NN)
        part_ref[:, 2 * n0:2 * n0 + nkv] = _bdot_raw(ca, kw_ref[...], _NN)
        lb_ref[...] = _sigmoid(lg_ref[0:1, :] - lg_ref[1:2, :])

    vm = pl.BlockSpec(memory_space=pltpu.VMEM)
    return pl.pallas_call(
        body, name="ada_fwd", in_specs=[vm, vm, vm, vm], out_specs=[vm, vm, vm],
        out_shape=[jax.ShapeDtypeStruct((rows, 2 * n0 + nkv), F32), jax.ShapeDtypeStruct((rows, d), F32),
                   jax.ShapeDtypeStruct((1, d), F32)],
        compiler_params=_cparams(),
    )(c_all, ada_w, kv_ada_w, logits)


def _ada_bwd(c_act, dm0, dm1, dkv, lb, dlb):
    rows, d = c_act.shape

    def body(c_ref, d0_ref, d1_ref, dk_ref, lb_ref, dlb_ref, dw_ref, dkw_ref, dlg_ref):
        ca = c_ref[...]
        dw_ref[0] = _bdot_raw(ca, d0_ref[...], _TN)
        dw_ref[1] = _bdot_raw(ca, d1_ref[...], _TN)
        dkw_ref[...] = _bdot_raw(ca, dk_ref[...], _TN)
        lbv = lb_ref[...]
        dl0 = dlb_ref[...] * lbv * (1.0 - lbv)
        dlg_ref[0:1, :] = dl0
        dlg_ref[1:2, :] = -dl0

    vm = pl.BlockSpec(memory_space=pltpu.VMEM)
    return pl.pallas_call(
        body, name="ada_bwd", in_specs=[vm] * 6, out_specs=[vm, vm, vm],
        out_shape=[jax.ShapeDtypeStruct((2, d, dm0.shape[1]), F32), jax.ShapeDtypeStruct((d, dkv.shape[1]), F32),
                   jax.ShapeDtypeStruct((2, d), F32)],
        compiler_params=_cparams(),
    )(c_act, dm0, dm1, dkv, lb, dlb)


def _adamw(w, g, m, v, name, tr=512, after=None):
    r, c = w.shape
    tr = _divisor_tile(r, tr, unit=8)
    c1 = 1.0 - ADAM_B1 ** ADAM_STEP
    c2 = 1.0 - ADAM_B2 ** ADAM_STEP
    deps = [] if after is None else [after]

    def body(w_ref, g_ref, m_ref, v_ref, *rest):
        d_ref, mo_ref, vo_ref = rest[len(deps):]
        gv = g_ref[...]
        mn = ADAM_B1 * m_ref[...] + (1.0 - ADAM_B1) * gv
        vn = ADAM_B2 * v_ref[...] + (1.0 - ADAM_B2) * (gv * gv)
        d_ref[...] = -ADAM_LR * ((mn / c1) / (jnp.sqrt(vn / c2) + ADAM_EPS) + ADAM_WD * w_ref[...])
        mo_ref[...] = mn
        vo_ref[...] = vn

    spec = pl.BlockSpec((tr, c), lambda i: (i, 0))
    out = jax.ShapeDtypeStruct((r, c), F32)
    return pl.pallas_call(body, name=name, grid=(r // tr,),
                          in_specs=[spec] * 4 + [pl.BlockSpec(a.shape, lambda i: (0, 0)) for a in deps],
                          out_specs=[spec] * 3, out_shape=[out, out, out],
                          compiler_params=_cparams(dimension_semantics=("parallel",)))(w, g, m, v, *deps)


def _pad_rows(a, rows):
    return jnp.pad(a, ((0, rows - a.shape[0]), (0, 0)))


def _pack_small(parts, lanes=LANES, row_unit=8):
    flat = jnp.concatenate([p.reshape(-1).astype(F32) for p in parts])
    rows = _round_up(-(-flat.shape[0] // lanes), row_unit)
    return jnp.pad(flat, (0, rows * lanes - flat.shape[0])).reshape(rows, lanes)


def _unpack_small(flat, shapes):
    out, off = [], 0
    for s in shapes:
        n = 1
        for k in s:
            n *= k
        out.append(flat[off:off + n].reshape(s))
        off += n
    return out


def _pad_shard_cols(a, n_loc, n_pad):
    lead = a.shape[:-1]
    a = a.reshape(*lead, NDEV, n_loc)
    a = jnp.pad(a, [(0, 0)] * (len(lead) + 1) + [(0, n_pad - n_loc)])
    return a.reshape(*lead, NDEV * n_pad)


def _unpad_shard_cols(a, n_loc, n_pad):
    lead = a.shape[:-1]
    return a.reshape(*lead, NDEV, n_pad)[..., :n_loc].reshape(*lead, NDEV * n_loc)


def kernel(x, c, ada_w, ada_b, a_w_in, a_lb_logits, a_norm_g, a_w_out, kv_ada_w, kv_ada_b, kv_w, kv_b_f, k_norm_g, b_w_q, q_norm_g, b_w_out, ffn_w_up, ffn_conv_w, ffn_conv_b, ffn_w_down, loss_target, m_ada_w, m_ada_b, m_a_w_in, m_a_lb_logits, m_a_norm_g, m_a_w_out, m_kv_ada_w, m_kv_ada_b, m_kv_w, m_kv_b_f, m_k_norm_g, m_b_w_q, m_q_norm_g, m_b_w_out, m_ffn_w_up, m_ffn_conv_w, m_ffn_conv_b, m_ffn_w_down, v_ada_w, v_ada_b, v_a_w_in, v_a_lb_logits, v_a_norm_g, v_a_w_out, v_kv_ada_w, v_kv_ada_b, v_kv_w, v_kv_b_f, v_k_norm_g, v_b_w_q, v_q_norm_g, v_b_w_out, v_ffn_w_up, v_ffn_conv_w, v_ffn_conv_b, v_ffn_w_down):
    t, d = x.shape[1], x.shape[2]
    nh = d // HEAD
    ncw = ffn_w_up.shape[2]
    ncp = _round_up(ncw, LANES)
    two_f = ncw * NDEV
    ff = two_f // 2
    fp = ncp * NDEV // 2
    rd = ffn_w_down.shape[1]
    me = 4 * lax.axis_index("x") + 2 * lax.axis_index("y") + lax.axis_index("c")
    weights = dict(ada_w=ada_w, ada_b=ada_b, a_w_in=a_w_in, a_lb_logits=a_lb_logits, a_norm_g=a_norm_g,
                   a_w_out=a_w_out, kv_ada_w=kv_ada_w, kv_ada_b=kv_ada_b, kv_w=kv_w, kv_b_f=kv_b_f,
                   k_norm_g=k_norm_g, b_w_q=b_w_q, q_norm_g=q_norm_g, b_w_out=b_w_out, ffn_w_up=ffn_w_up,
                   ffn_conv_w=ffn_conv_w, ffn_conv_b=ffn_conv_b, ffn_w_down=ffn_w_down)
    m_in = dict(ada_w=m_ada_w, ada_b=m_ada_b, a_w_in=m_a_w_in, a_lb_logits=m_a_lb_logits, a_norm_g=m_a_norm_g,
                a_w_out=m_a_w_out, kv_ada_w=m_kv_ada_w, kv_ada_b=m_kv_ada_b, kv_w=m_kv_w, kv_b_f=m_kv_b_f,
                k_norm_g=m_k_norm_g, b_w_q=m_b_w_q, q_norm_g=m_q_norm_g, b_w_out=m_b_w_out, ffn_w_up=m_ffn_w_up,
                ffn_conv_w=m_ffn_conv_w, ffn_conv_b=m_ffn_conv_b, ffn_w_down=m_ffn_w_down)
    v_in = dict(ada_w=v_ada_w, ada_b=v_ada_b, a_w_in=v_a_w_in, a_lb_logits=v_a_lb_logits, a_norm_g=v_a_norm_g,
                a_w_out=v_a_w_out, kv_ada_w=v_kv_ada_w, kv_ada_b=v_kv_ada_b, kv_w=v_kv_w, kv_b_f=v_kv_b_f,
                k_norm_g=v_k_norm_g, b_w_q=v_b_w_q, q_norm_g=v_q_norm_g, b_w_out=v_b_w_out, ffn_w_up=v_ffn_w_up,
                ffn_conv_w=v_ffn_conv_w, ffn_conv_b=v_ffn_conv_b, ffn_w_down=v_ffn_w_down)
    order = list(weights)

    up_loc = jnp.pad(ffn_w_up, ((0, 0), (0, 0), (0, ncp - ncw))).astype(BF16)
    down_loc = ffn_w_down.astype(BF16)
    gather_names = {"l0b": ["a_out", "up0", "down0"], "l1": ["kv", "b_q", "b_out", "up1", "down1"]}
    shards = {"a_out": a_w_out[0].astype(BF16), "up0": up_loc[0], "down0": down_loc[0], "kv": kv_w.T.astype(BF16),
              "b_q": b_w_q[0].astype(BF16), "b_out": b_w_out[0].astype(BF16), "up1": up_loc[1],
              "down1": down_loc[1]}
    pre = _pack_small([c, a_lb_logits, ffn_conv_w])
    in_flight = {}
    pre_flight, _ = _xchg_start([pre], False, ALL_PEERS, pre, "gather_small_inputs_start")
    (pre_all,) = _xchg_wait(pre_flight, pre, False, ALL_PEERS, "gather_small_inputs_wait")
    pre_all = pre_all.reshape(NDEV, -1)
    c_all = pre_all[:, :d]
    logits = pre_all[:, d:d + 2 * HEAD].reshape(NDEV, 2, HEAD).transpose(1, 0, 2).reshape(2, d)
    conv_w_full = pre_all[:, d + 2 * HEAD:d + 2 * HEAD + 2 * CONV_TAPS * ncw]
    conv_w_full = conv_w_full.reshape(NDEV, 2, CONV_TAPS, ncw).transpose(1, 2, 0, 3).reshape(2, CONV_TAPS, two_f)

    part, c_act, lb = _ada_fwd(_pad_rows(c_all, 2 * NDEV), ada_w, kv_ada_w, logits)
    part_flight, _ = _xchg_start([part[:NDEV]], False, ALL_PEERS, part, "gather_adaln_start")
    in_flight["l0a"], _ = _xchg_start([a_w_in[0].astype(BF16)], False, SAME_CORE, part_flight[-1], "gather_l0a_start")
    (part_all,) = _xchg_wait(part_flight, in_flight["l0a"][-1], False, ALL_PEERS, "gather_adaln_wait")
    forwarding = {}
    mine = lax.dynamic_index_in_dim(part_all, me, axis=1, keepdims=False)
    n0, nkv = ada_w.shape[2], kv_ada_w.shape[1]
    mod_names = ["sh1", "sc1", "g1", "sh2", "sc2", "g2"]
    mods = {}
    for l in range(2):
        row = mine[:, l * n0:(l + 1) * n0].reshape(-1) + ada_b[l]
        for k, nm in enumerate(mod_names):
            mods[f"{nm}_{l}"] = row[k * d:(k + 1) * d].reshape(1, d)
    kvrow = mine[:, 2 * n0:2 * n0 + nkv].reshape(-1) + kv_ada_b
    mods["kv_sh"], mods["kv_sc"] = kvrow[:d].reshape(1, d), kvrow[d:].reshape(1, d)

    def start_gather(grp, dep):
        srcs = [shards[n] for n in gather_names[grp]]
        in_flight[grp], started = _xchg_start(srcs, False, SAME_CORE, dep, f"gather_{grp}_start")
        return started

    zero = start_gather("l0b", part_all)
    mods["sh1_0"] = mods["sh1_0"] + zero

    small = {"a_norm_g": a_norm_g, "k_norm_g": k_norm_g.reshape(1, HEAD), "q_norm_g": q_norm_g, "kv_b_f": kv_b_f}
    for l in range(2):
        small[f"conv_w{l}"] = _pad_shard_cols(conv_w_full[l], ncw, ncp).reshape(CONV_TAPS, 2, fp).transpose(1, 0, 2)
        small[f"conv_b{l}"] = _pad_shard_cols(ffn_conv_b[l], ncw, ncp).reshape(2, 1, fp)

    def pre_w(grp, after):
        arrived = _xchg_wait(in_flight[grp], after, False, SAME_CORE, f"gather_{grp}_wait")
        forwarding[grp], started = _sibling_forward_start(arrived, f"gather_{grp}_to_sibling_start")
        return started

    def get_w(grp, after):
        if grp == "l0a":
            arrived = _xchg_wait(in_flight["l0a"], after, False, SAME_CORE, "gather_l0a_wait")
            handles, _ = _sibling_forward_start(arrived, "gather_l0a_to_sibling_start")
            return {"a_in": _sibling_forward_wait(handles, after, "gather_l0a_to_sibling_wait")[0]}
        full = _sibling_forward_wait(forwarding[grp], after, f"gather_{grp}_to_sibling_wait")
        if grp == "l0b":
            started = start_gather("l1", full[0])
            full[0] = full[0] + started.astype(full[0].dtype)
        got = dict(zip(gather_names[grp], full))
        out = {}
        for n, a in got.items():
            if n in ("a_out", "b_out"):
                out[n] = a.reshape(d, d)
            elif n in ("down0", "down1"):
                dn = a.reshape(NCHIP, ff // NCHIP, d)
                out[n] = jnp.pad(dn, ((0, 0), (0, ncp - ncw), (0, 0))).reshape(fp, d)
            elif n == "kv":
                kv_t = a.reshape(NDEV * kv_w.shape[1], d)
                out["kv_k"], out["kv_v"] = kv_t[:d], kv_t[d:2 * d]
                out["kv_f"] = jnp.pad(kv_t[2 * d:], ((0, LANES - nh), (0, 0)))
            else:
                out[n] = a
        return out

    scatter_flight, g_last = {}, {}

    def put_g(grp, gr):
        if grp == "l0a":
            g_last.update(gr)
            return zero
        if grp == "l1":
            g_kvw = jnp.concatenate([gr["kv_k"], gr["kv_v"], gr["kv_f"][:nh].astype(BF16)], axis=0)
            arrs = {"kv_w": g_kvw.reshape(NDEV, kv_w.shape[1], d), "b_w_q": gr["b_q"],
                    "b_w_out": gr["b_out"].reshape(NDEV, d // NDEV, d), "up1": gr["up1"],
                    "down1": gr["down1"].reshape(NCHIP, ncp, d)[:, :ncw].reshape(NDEV, rd, d)}
        else:
            arrs = {"a_w_out": gr["a_out"].reshape(NDEV, d // NDEV, d), "up0": gr["up0"],
                    "down0": gr["down0"].reshape(NCHIP, ncp, d)[:, :ncw].reshape(NDEV, rd, d)}
        srcs = list(arrs.values())
        handles, sent = _xchg_start(srcs, True, ALL_PEERS, srcs[0], f"scatter_{grp}_start")
        scatter_flight[grp] = (list(arrs), handles)
        return sent

    loss_v, grad_x, dmods, dlb, g = _local_step(x[0], loss_target[0], mods, lb, small, pre_w, get_w, put_g)

    g_sum = {}
    for grp in ("l1", "l0b"):
        names, handles = scatter_flight[grp]
        for nm, a in zip(names, _xchg_wait(handles, grad_x, True, ALL_PEERS, f"scatter_{grp}_wait")):
            g_sum[nm] = _slab_sum(a, f"rs_slab_sum_{nm}")

    def conv_w_grad(a):
        return _unpad_shard_cols(a.transpose(1, 0, 2).reshape(CONV_TAPS, 2 * fp), ncw, ncp)

    def conv_b_grad(a):
        return _unpad_shard_cols(a.reshape(2 * fp), ncw, ncp)

    dmod_vec = [dmods[f"{nm}_{l}"] for l in range(2) for nm in mod_names] + [dmods["kv_sh"], dmods["kv_sc"]]
    post = _pack_small(dmod_vec + [dlb, g["a_norm_g"], g["k_norm_g"], g["q_norm_g"],
                                   jnp.pad(g["kv_b_f"].reshape(-1), (0, LANES - nh)),
                                   conv_w_grad(g["conv_w0"]), conv_w_grad(g["conv_w1"]),
                                   conv_b_grad(g["conv_b0"]), conv_b_grad(g["conv_b1"]), loss_v])
    post_flight, _ = _xchg_start([post], False, ALL_PEERS, post, "gather_small_grads_start")
    a_in_flight, a_in_sent = _xchg_start([g_last["a_in"]], True, ALL_PEERS, post_flight[-1], "scatter_l0a_start")
    a_in_sent = a_in_sent.reshape(1, 1)
    grads = {
        "a_w_out": g_sum["a_w_out"].reshape(a_w_out.shape),
        "kv_w": g_sum["kv_w"].T,
        "b_w_q": g_sum["b_w_q"].reshape(b_w_q.shape),
        "b_w_out": g_sum["b_w_out"].reshape(b_w_out.shape),
        "ffn_w_up": jnp.stack([g_sum["up0"][:, :ncw], g_sum["up1"][:, :ncw]]),
        "ffn_w_down": jnp.stack([g_sum["down0"], g_sum["down1"]]),
    }
    delta, new_m, new_v = {}, {}, {}

    def adamw_matrix(n):
        shp = weights[n].shape
        two_d = lambda a: a.reshape(-1, shp[-1])
        dl, mn, vn = _adamw(two_d(weights[n]), two_d(grads[n]), two_d(m_in[n]), two_d(v_in[n]), f"adamw_{n}",
                            after=a_in_sent)
        delta[n], new_m[n], new_v[n] = dl.reshape(shp), mn.reshape(shp), vn.reshape(shp)

    for n in grads:
        adamw_matrix(n)
    (post_all,) = _xchg_wait(post_flight, [new_v[n] for n in grads], False, ALL_PEERS, "gather_small_grads_wait")
    tot = _slab_sum(post_all, "small_grad_sum").reshape(-1)
    nmod = 14 * d
    (t_mod, t_lb, t_ang, t_kng, t_qng, t_bf, t_cw, t_cb, t_loss) = _unpack_small(
        tot, [(nmod,), (1, d), (1, HEAD), (HEAD,), (1, HEAD), (LANES,), (2, CONV_TAPS, two_f), (2, two_f),
              (LANES,)])
    loss = t_loss[0]
    dm_all = post_all.reshape(NDEV, -1)[:, :nmod]
    dm0 = lax.dynamic_slice_in_dim(dm_all[:, :6 * d], me * n0, n0, axis=1)
    dm1 = lax.dynamic_slice_in_dim(dm_all[:, 6 * d:12 * d], me * n0, n0, axis=1)
    dkv = lax.dynamic_slice_in_dim(dm_all[:, 12 * d:], me * nkv, nkv, axis=1)
    g_ada_w, g_kv_ada_w, g_logits = _ada_bwd(c_act, _pad_rows(dm0, 2 * NDEV), _pad_rows(dm1, 2 * NDEV),
                                              _pad_rows(dkv, 2 * NDEV), lb, t_lb)

    grads.update({
        "ada_w": g_ada_w,
        "ada_b": t_mod[:12 * d].reshape(2, 6 * d),
        "a_lb_logits": lax.dynamic_slice_in_dim(g_logits, me * HEAD, HEAD, axis=1),
        "a_norm_g": t_ang,
        "kv_ada_w": g_kv_ada_w,
        "kv_ada_b": t_mod[12 * d:],
        "kv_b_f": t_bf[:nh],
        "k_norm_g": t_kng,
        "q_norm_g": t_qng,
        "ffn_conv_w": lax.dynamic_slice_in_dim(t_cw, me * ncw, ncw, axis=2),
        "ffn_conv_b": t_cb,
    })

    small_adam = [n for n in order if n not in delta and n not in ("ada_w", "kv_ada_w", "a_w_in")]
    packs = [_pack_small([src[n] for n in small_adam]) for src in (weights, grads, m_in, v_in)]
    outs = _adamw(*packs, "adamw_small", tr=packs[0].shape[0])
    shapes = [weights[n].shape for n in small_adam]
    for dst, o in zip((delta, new_m, new_v), outs):
        for n, a in zip(small_adam, _unpack_small(o.reshape(-1), shapes)):
            dst[n] = a
    adamw_matrix("ada_w")
    adamw_matrix("kv_ada_w")
    (landed,) = _xchg_wait(a_in_flight, new_v["kv_ada_w"], True, ALL_PEERS, "scatter_l0a_wait")
    grads["a_w_in"] = _slab_sum(landed, "rs_slab_sum_a_w_in").reshape(a_w_in.shape)
    adamw_matrix("a_w_in")

    return (loss, grad_x.reshape(x.shape), *[grads[n] for n in order], *[delta[n] for n in order],
            *[new_m[n] for n in order], *[new_v[n] for n in order])
```

```python
import functools

import jax
import jax.numpy as jnp
from jax import lax
from jax.experimental import pallas as pl
from jax.experimental.pallas import tpu as pltpu

F32 = jnp.float32
BF16 = jnp.bfloat16

NDEV = 8
NCHIP = 4
HEAD = 128
A_CHUNK = 64
CONV_TAPS = 3
EPS = 1e-6
NEG_INF = -1e30
LANES = 128
VMEM_LIMIT = 48 * 1024 * 1024

ADAM_LR = 0.001
ADAM_B1 = 0.9
ADAM_B2 = 0.999
ADAM_EPS = 1e-08
ADAM_WD = 0.01
ADAM_STEP = 10

_NN = (((1,), (0,)), ((), ()))
_NT = (((1,), (1,)), ((), ()))
_TN = (((0,), (0,)), ((), ()))
_MESH = pl.DeviceIdType.MESH


def _cparams(**kw):
    return pltpu.CompilerParams(vmem_limit_bytes=VMEM_LIMIT, **kw)


def _divisor_tile(n, pref, unit=LANES):
    if n <= pref:
        return n
    best = None
    for t in range(unit, pref + 1, unit):
        if n % t == 0:
            best = t
    assert best is not None, (n, pref)
    return best


def _round_up(n, unit):
    return -(-n // unit) * unit


def _bdot_raw(a, b, dims):
    return lax.dot_general(a.astype(BF16), b.astype(BF16), dims, preferred_element_type=F32)


@jax.custom_vjp
def _dot_nn(a, b):
    return _bdot_raw(a, b, _NN)


@jax.custom_vjp
def _dot_nt(a, b):
    return _bdot_raw(a, b, _NT)


@jax.custom_vjp
def _dot_tn(a, b):
    return _bdot_raw(a, b, _TN)


_dot_nn.defvjp(lambda a, b: (_bdot_raw(a, b, _NN), (a, b)),
               lambda r, g: (_dot_nt(g, r[1]), _dot_tn(r[0], g)))
_dot_nt.defvjp(lambda a, b: (_bdot_raw(a, b, _NT), (a, b)),
               lambda r, g: (_dot_nn(g, r[1]), _dot_tn(g, r[0])))
_dot_tn.defvjp(lambda a, b: (_bdot_raw(a, b, _TN), (a, b)),
               lambda r, g: (_dot_nt(r[1], g), _dot_nn(r[0], g)))


def _f32dot(a, b):
    return lax.dot_general(a, b, _NN, precision=lax.Precision.HIGHEST, preferred_element_type=F32)


def _sigmoid(x):
    return jax.nn.sigmoid(x)


def _silu(x):
    return x * jax.nn.sigmoid(x)


def _rms(x):
    return x * lax.rsqrt(jnp.mean(x * x, axis=-1, keepdims=True) + EPS)


def _modulate(x, sh, sc):
    return _rms(x) * (1.0 + sc) + sh


def _mm_call(a, b, dims, a_spec, b_spec, o_spec, o_shape, grid, acc_tile, name):
    nk = grid[2]

    def body(a_ref, b_ref, o_ref, *acc):
        p = lax.dot_general(a_ref[...].astype(BF16), b_ref[...].astype(BF16), dims,
                            preferred_element_type=F32)
        if nk == 1:
            o_ref[...] = p.astype(o_ref.dtype)
        else:
            kk = pl.program_id(2)

            @pl.when(kk == 0)
            def _():
                acc[0][...] = p

            @pl.when(kk > 0)
            def _():
                acc[0][...] += p

            @pl.when(kk == nk - 1)
            def _():
                o_ref[...] = acc[0][...].astype(o_ref.dtype)

    return pl.pallas_call(
        body, name=name, grid=grid, in_specs=[a_spec, b_spec], out_specs=o_spec, out_shape=o_shape,
        scratch_shapes=[pltpu.VMEM(acc_tile, F32)] if nk > 1 else [],
        compiler_params=_cparams(dimension_semantics=("parallel", "parallel", "arbitrary")),
    )(a, b)


def _mm(a, b, mode, out_dtype, name, tm=1024, tn=1024, tk=2048):
    if mode == "nn":
        (m, k), (k2, n) = a.shape, b.shape
    elif mode == "nt":
        (m, k), (n, k2) = a.shape, b.shape
    else:
        (k, m), (k2, n) = a.shape, b.shape
    assert k == k2, (a.shape, b.shape, mode)
    tm, tn, tk = _divisor_tile(m, tm), _divisor_tile(n, tn), _divisor_tile(k, tk)
    if mode == "tn":
        a_spec = pl.BlockSpec((tk, tm), lambda i, j, kk: (kk, i))
    else:
        a_spec = pl.BlockSpec((tm, tk), lambda i, j, kk: (i, kk))
    if mode == "nt":
        b_spec = pl.BlockSpec((tn, tk), lambda i, j, kk: (j, kk))
    else:
        b_spec = pl.BlockSpec((tk, tn), lambda i, j, kk: (kk, j))
    return _mm_call(a, b, {"nn": _NN, "nt": _NT, "tn": _TN}[mode], a_spec, b_spec,
                    pl.BlockSpec((tm, tn), lambda i, j, kk: (i, j)), jax.ShapeDtypeStruct((m, n), out_dtype),
                    (m // tm, n // tn, k // tk), (tm, tn), name)


def _wblk_act_spec(rows, gb, nl, split, nb, row_axis, blk_axis):
    if split == 1:
        return pl.BlockSpec((rows, gb * nl), lambda *g: (g[row_axis], g[blk_axis]))
    groups = nb // split // gb
    return pl.BlockSpec((None, rows, gb * nl),
                        lambda *g: (g[blk_axis] // groups, g[row_axis], g[blk_axis] % groups))


def _mm_wblk(a, wb, out_dtype, name, *, gb, row_off=0, split=1, tm=1024):
    m, k = a.shape
    nb, _, nl = wb.shape
    assert (nb // split) % gb == 0
    tm = _divisor_tile(m, tm)

    def body(a_ref, b_ref, o_ref):
        av = a_ref[...].astype(BF16)
        for s in range(gb):
            o_ref[:, s * nl:(s + 1) * nl] = lax.dot_general(
                av, b_ref[s].astype(BF16), _NN, preferred_element_type=F32).astype(o_ref.dtype)

    o_shape = (m, nb * nl) if split == 1 else (split, m, nb // split * nl)
    return pl.pallas_call(
        body, name=name, grid=(nb // gb, m // tm),
        in_specs=[pl.BlockSpec((tm, k), lambda j, i: (i, 0)),
                  pl.BlockSpec((gb, k, nl), lambda j, i: (j, row_off, 0))],
        out_specs=_wblk_act_spec(tm, gb, nl, split, nb, 1, 0),
        out_shape=jax.ShapeDtypeStruct(o_shape, out_dtype),
        compiler_params=_cparams(dimension_semantics=("parallel", "parallel")),
    )(a, wb)


def _mm_wblk_dx(dy, wb, out_dtype, name, *, k, gb, row_off=0, split=1, tm=1024):
    nb, _, nl = wb.shape
    m = dy.shape[-2]
    tm = _divisor_tile(m, tm)
    nk = nb // gb
    per = nb // split
    whole = split > 1 and gb == nb
    assert whole or per % gb == 0

    def body(a_ref, b_ref, o_ref, *acc):
        p = None
        for s in range(gb):
            a_blk = a_ref[s // per, :, (s % per) * nl:(s % per + 1) * nl] if whole else a_ref[:, s * nl:(s + 1) * nl]
            q = lax.dot_general(a_blk.astype(BF16), b_ref[s].astype(BF16), _NT, preferred_element_type=F32)
            p = q if p is None else p + q
        if nk == 1:
            o_ref[...] = p.astype(o_ref.dtype)
        else:
            kk = pl.program_id(1)

            @pl.when(kk == 0)
            def _():
                acc[0][...] = p

            @pl.when(kk > 0)
            def _():
                acc[0][...] += p

            @pl.when(kk == nk - 1)
            def _():
                o_ref[...] = acc[0][...].astype(o_ref.dtype)

    return pl.pallas_call(
        body, name=name, grid=(m // tm, nk),
        in_specs=[pl.BlockSpec((split, tm, per * nl), lambda i, kk: (0, i, 0)) if whole
                  else _wblk_act_spec(tm, gb, nl, split, nb, 0, 1),
                  pl.BlockSpec((gb, k, nl), lambda i, kk: (kk, row_off, 0))],
        out_specs=pl.BlockSpec((tm, k), lambda i, kk: (i, 0)),
        out_shape=jax.ShapeDtypeStruct((m, k), out_dtype),
        scratch_shapes=[pltpu.VMEM((tm, k), F32)] if nk > 1 else [],
        compiler_params=_cparams(dimension_semantics=("parallel", "arbitrary")),
    )(dy, wb)


def _mm_wblk_dw(x, dy, name, *, nb, gb, split=1, tk=1024):
    t, k = x.shape
    assert (nb // split) % gb == 0
    nl = dy.shape[-1] * split // nb
    tk = _divisor_tile(t, tk)
    nk = t // tk

    def body(a_ref, b_ref, o_ref, *acc):
        kk = pl.program_id(1)
        av = a_ref[...].astype(BF16)
        for s in range(gb):
            p = lax.dot_general(av, b_ref[:, s * nl:(s + 1) * nl].astype(BF16), _TN, preferred_element_type=F32)
            if nk == 1:
                o_ref[s] = p.astype(o_ref.dtype)
                continue

            @pl.when(kk == 0)
            def _():
                acc[0][s] = p

            @pl.when(kk > 0)
            def _():
                acc[0][s] += p

        if nk > 1:
            @pl.when(kk == nk - 1)
            def _():
                o_ref[...] = acc[0][...].astype(o_ref.dtype)

    return pl.pallas_call(
        body, name=name, grid=(nb // gb, nk),
        in_specs=[pl.BlockSpec((tk, k), lambda j, kk: (kk, 0)), _wblk_act_spec(tk, gb, nl, split, nb, 1, 0)],
        out_specs=pl.BlockSpec((gb, k, nl), lambda j, kk: (j, 0, 0)),
        out_shape=jax.ShapeDtypeStruct((nb, k, nl), BF16),
        scratch_shapes=[pltpu.VMEM((gb, k, nl), F32)] if nk > 1 else [],
        compiler_params=_cparams(dimension_semantics=("parallel", "arbitrary")),
    )(x, dy)


def _row_specs(rows, tb, nsub):
    return [pl.BlockSpec((tb, nsub * cw), functools.partial(lambda i, off: (i, off), off=off))
            for (_, cw, off) in rows]


def _vec_specs(params):
    return [pl.BlockSpec(p.shape, lambda i: (0, 0)) for p in params]


def _row_fwd(f, rows, params, out_dtypes, *, nsub=1, tb, name):
    t = rows[0][0].shape[0]
    tb = min(tb, t)
    n_r, n_p = len(rows), len(params)
    blk = [jax.ShapeDtypeStruct((tb, cw), F32) for (_, cw, _) in rows]
    blk += [jax.ShapeDtypeStruct(p.shape, F32) for p in params]
    out_avals = jax.eval_shape(f, *blk)

    def body(*refs):
        pv = [r[...] for r in refs[n_r:n_r + n_p]]
        for s in range(nsub):
            vals = [r[:, s * cw:(s + 1) * cw].astype(F32) for r, (_, cw, _) in zip(refs[:n_r], rows)]
            outs = f(*vals, *pv)
            for o_ref, o in zip(refs[n_r + n_p:], outs):
                w = o.shape[1]
                o_ref[:, s * w:(s + 1) * w] = o.astype(o_ref.dtype)

    return pl.pallas_call(
        body, name=name,
        grid=(t // tb,),
        in_specs=_row_specs(rows, tb, nsub) + _vec_specs(params),
        out_specs=[pl.BlockSpec((tb, nsub * av.shape[1]), lambda i: (i, 0)) for av in out_avals],
        out_shape=[jax.ShapeDtypeStruct((t, nsub * av.shape[1]), dt) for av, dt in zip(out_avals, out_dtypes)],
        compiler_params=_cparams(dimension_semantics=("parallel",)),
    )(*[r[0] for r in rows], *params)


def _row_bwd(f, rows, params, cots, row_grad_dtypes, *, nsub=1, tb, name, add_to=None, cot_add=None):
    t = rows[0][0].shape[0]
    tb = min(tb, t)
    n_r, n_p, n_c = len(rows), len(params), len(cots)
    want = [j for j in range(n_r) if row_grad_dtypes[j] is not None]
    cot_add = cot_add or []
    extra = [] if add_to is None else [(add_to[1], rows[add_to[0]][1], 0)]
    n_add_to = len(extra)
    extra += [(arr, cots[ci][1], 0) for ci, arr in cot_add]

    def body(*refs):
        i = pl.program_id(0)
        r_in, p_in = refs[:n_r], refs[n_r:n_r + n_p]
        c_in = refs[n_r + n_p:n_r + n_p + n_c]
        e_in = refs[n_r + n_p + n_c:n_r + n_p + n_c + len(extra)]
        outs = refs[n_r + n_p + n_c + len(extra):]
        pv = [r[...] for r in p_in]
        psum = [None] * n_p
        for s in range(nsub):
            vals = [r[:, s * cw:(s + 1) * cw].astype(F32) for r, (_, cw, _) in zip(r_in, rows)]
            cvals = [r[:, s * cw:(s + 1) * cw].astype(F32) for r, (_, cw, _) in zip(c_in, cots)]
            for (ci, _), e_ref in zip(cot_add, e_in[n_add_to:]):
                cw = cots[ci][1]
                cvals[ci] = cvals[ci] + e_ref[:, s * cw:(s + 1) * cw].astype(F32)
            _, vjp_fn = jax.vjp(f, *vals, *pv)
            grads = vjp_fn(tuple(cvals))
            for o_ref, jr in zip(outs[:len(want)], want):
                cw = rows[jr][1]
                gr = grads[jr]
                if add_to is not None and jr == add_to[0]:
                    gr = gr + e_in[0][:, s * cw:(s + 1) * cw]
                o_ref[:, s * cw:(s + 1) * cw] = gr.astype(o_ref.dtype)
            for jp in range(n_p):
                psum[jp] = grads[n_r + jp] if psum[jp] is None else psum[jp] + grads[n_r + jp]
        for o_ref, g in zip(outs[len(want):], psum):
            @pl.when(i == 0)
            def _():
                o_ref[...] = g

            @pl.when(i > 0)
            def _():
                o_ref[...] += g

    out_specs = [pl.BlockSpec((tb, nsub * rows[jr][1]), lambda i: (i, 0)) for jr in want]
    out_shape = [jax.ShapeDtypeStruct((t, nsub * rows[jr][1]), row_grad_dtypes[jr]) for jr in want]
    out_specs += _vec_specs(params)
    out_shape += [jax.ShapeDtypeStruct(p.shape, F32) for p in params]
    res = pl.pallas_call(
        body, name=name,
        grid=(t // tb,),
        in_specs=_row_specs(rows, tb, nsub) + _vec_specs(params) + _row_specs(cots, tb, nsub)
        + _row_specs(extra, tb, nsub),
        out_specs=out_specs, out_shape=out_shape,
        compiler_params=_cparams(dimension_semantics=("arbitrary",)),
    )(*[r[0] for r in rows], *params, *[c[0] for c in cots], *[e[0] for e in extra])
    return res[:len(want)], res[len(want):]


def _f_mod(x, sh, sc):
    return (_modulate(x, sh, sc),)


def _f_res_mod(x, y, g, sh, sc):
    x1 = x + g * y
    return x1, _modulate(x1, sh, sc)


def _f_res_mod2(x, y, g, sh_a, sc_a, sh_b, sc_b):
    x1 = x + g * y
    return x1, _modulate(x1, sh_a, sc_a), _modulate(x1, sh_b, sc_b)


def _f_qnorm(p, g):
    return (_rms(p) * g * (HEAD ** -0.5),)


def _f_knorm(p, g):
    return (_rms(p) * g,)


def _f_qnorm_aug(p, g):
    lane = lax.broadcasted_iota(jnp.int32, p.shape, 1)
    return (jnp.concatenate([_rms(p) * g * (HEAD ** -0.5), jnp.where(lane < 3, 1.0, 0.0)], axis=1),)


def _f_knorm_aug(p, c0, c1, c2, g):
    lane = lax.broadcasted_iota(jnp.int32, p.shape, 1)
    aug = jnp.where(lane == 0, c0, jnp.where(lane == 1, c1, jnp.where(lane == 2, c2, 0.0)))
    return (jnp.concatenate([_rms(p) * g, aug], axis=1),)


def _split3(a):
    round_bf16 = lambda v: lax.reduce_precision(v, exponent_bits=8, mantissa_bits=7)
    hi = round_bf16(a)
    mid = round_bf16(a - hi)
    lo = round_bf16(a - hi - mid)
    return hi.astype(BF16), mid.astype(BF16), lo.astype(BF16)


def _f_outgate(o, og):
    return (o * _sigmoid(og),)


def _loss_call(x3, f, g2, target, tb):
    t, d = x3.shape
    tb = min(tb, t)

    def body(x_ref, f_ref, g_ref, t_ref, loss_ref, dx_ref, df_ref, dg_ref):
        i = pl.program_id(0)
        fv = f_ref[...]
        g = g_ref[...]
        e = x_ref[...] + g * fv - t_ref[...]
        dx = e * (1.0 / d)
        part = 0.5 * jnp.sum(jnp.sum(e * dx, axis=1, keepdims=True), axis=0, keepdims=True)
        dx_ref[...] = dx
        df_ref[...] = (g * dx).astype(df_ref.dtype)
        dg = jnp.sum(dx * fv, axis=0, keepdims=True)

        @pl.when(i == 0)
        def _():
            loss_ref[...] = jnp.broadcast_to(part, loss_ref.shape)
            dg_ref[...] = dg

        @pl.when(i > 0)
        def _():
            loss_ref[...] += jnp.broadcast_to(part, loss_ref.shape)
            dg_ref[...] += dg

    row = pl.BlockSpec((tb, d), lambda i: (i, 0))
    vec = pl.BlockSpec((1, d), lambda i: (0, 0))
    return pl.pallas_call(
        body, name="loss_head",
        grid=(t // tb,),
        in_specs=[row, row, vec, row],
        out_specs=[pl.BlockSpec((1, LANES), lambda i: (0, 0)), row, row, vec],
        out_shape=[jax.ShapeDtypeStruct((1, LANES), F32), jax.ShapeDtypeStruct((t, d), F32),
                   jax.ShapeDtypeStruct((t, d), BF16), jax.ShapeDtypeStruct((1, d), F32)],
        compiler_params=_cparams(dimension_semantics=("arbitrary",)),
    )(x3, f, g2, target)


def _hg_mask(tb):
    br = lax.broadcasted_iota(jnp.int32, (tb, tb), 0)
    bs = lax.broadcasted_iota(jnp.int32, (tb, tb), 1)
    return jnp.logical_and(br // A_CHUNK == bs // A_CHUNK, bs <= br).astype(F32)


def _hg_consts(mask):
    c = A_CHUNK
    r = lax.broadcasted_iota(jnp.int32, (c, c), 0)
    s = lax.broadcasted_iota(jnp.int32, (c, c), 1)
    return (s <= r).astype(F32), (r <= s).astype(F32), mask > 0.5


def _chunk_apply(mat, x):
    c = mat.shape[0]
    return jnp.concatenate([_f32dot(mat, x[i * c:(i + 1) * c]) for i in range(x.shape[0] // c)], axis=0)


@jax.custom_vjp
def _chunk_cumsum(x, tri, tri_t):
    return _chunk_apply(tri, x)


_chunk_cumsum.defvjp(lambda x, tri, tri_t: (_chunk_apply(tri, x), (tri, tri_t)),
                     lambda r, g: (_chunk_apply(r[1], g), jnp.zeros_like(r[0]), jnp.zeros_like(r[1])))


def _per_chunk(a, b, dims):
    return jnp.stack([_bdot_raw(a[i], b[i], dims) for i in range(a.shape[0])])


@jax.custom_vjp
def _chunk_tn(a, b):
    return _per_chunk(a, b, _TN)


@jax.custom_vjp
def _chunk_nt(a, b):
    return _per_chunk(a, b, _NT)


@jax.custom_vjp
def _chunk_nn(a, b):
    return _per_chunk(a, b, _NN)


_chunk_tn.defvjp(lambda a, b: (_per_chunk(a, b, _TN), (a, b)),
                 lambda r, g: (_chunk_nt(r[1], g), _chunk_nn(r[0], g)))
_chunk_nt.defvjp(lambda a, b: (_per_chunk(a, b, _NT), (a, b)),
                 lambda r, g: (_chunk_nn(g, r[1]), _chunk_tn(g, r[0])))
_chunk_nn.defvjp(lambda a, b: (_per_chunk(a, b, _NN), (a, b)),
                 lambda r, g: (_chunk_nt(g, r[1]), _chunk_tn(r[0], g)))


def _scan_states(decay, m, st):
    sts = []
    for i in range(m.shape[0]):
        sts.append(st)
        st = st * decay[i] + m[i]
    return jnp.stack(sts), st


@jax.custom_vjp
def _state_scan(decay, m, st):
    return _scan_states(decay, m, st)


def _state_scan_fwd(decay, m, st):
    sts, st_out = _scan_states(decay, m, st)
    return (sts, st_out), (decay, sts)


def _state_scan_bwd(res, cts):
    decay, sts = res
    d_sts, g = cts
    d_decay, d_m = [], []
    for i in range(sts.shape[0] - 1, -1, -1):
        d_m.append(g)
        d_decay.append(jnp.sum(g * sts[i], axis=0, keepdims=True))
        g = g * decay[i] + d_sts[i]
    return jnp.stack(d_decay[::-1]), jnp.stack(d_m[::-1]), g


_state_scan.defvjp(_state_scan_fwd, _state_scan_bwd)


def _hg_block(qp, fp, ip, gp, lb, ng, st, tri, tri_t, bd_causal):
    tb = qp.shape[0]
    c = A_CHUNK
    n = tb // c
    q = _silu(qp)
    fg = lb + (1.0 - lb) * _sigmoid(fp)
    logf = jnp.log(fg)
    k = 1.0 - fg
    b3 = _chunk_cumsum(logf, tri, tri_t).reshape(n, c, HEAD)
    pos = lax.broadcasted_iota(jnp.int32, (1, c, 1), 1)
    b_mid = lax.stop_gradient(jnp.sum(jnp.where(pos == c // 2, b3, 0.0), axis=1, keepdims=True))
    b_last = jnp.sum(jnp.where(pos == c - 1, b3, 0.0), axis=1, keepdims=True)
    q3, k3, v3 = q.reshape(n, c, HEAD), k.reshape(n, c, HEAD), ip.reshape(n, c, HEAD)
    scores = _dot_nt((q3 * jnp.exp(b3 - b_mid)).reshape(tb, HEAD), (k3 * jnp.exp(b_mid - b3)).reshape(tb, HEAD))
    o_intra = _dot_nn(jnp.where(bd_causal, scores, 0.0), ip)
    states, st_new = _state_scan(jnp.exp(b_last), _chunk_tn(v3, k3 * jnp.exp(b_last - b3)), st)
    o = o_intra + _chunk_nt(q3 * jnp.exp(b3), states).reshape(tb, HEAD)
    y = _rms(o) * ng * _silu(gp)
    return y, st_new


HG_HEADS = 2


def _hg_specs(tb, nh, rev_nb=None):
    wide = HG_HEADS * HEAD
    per = nh // HG_HEADS

    def row(part):
        if rev_nb is None:
            return pl.BlockSpec((tb, wide), functools.partial(lambda h, i, off: (i, off + h), off=part * per))
        return pl.BlockSpec((tb, wide),
                            functools.partial(lambda h, i, off: (rev_nb - 1 - i, off + h), off=part * per))
    return [row(0), row(1), row(2), row(3),
            pl.BlockSpec((1, wide), lambda h, i: (0, h)), pl.BlockSpec((1, HEAD), lambda h, i: (0, 0)),
            pl.BlockSpec((tb, tb), lambda h, i: (0, 0))]


def _hgrn2_fwd(proj, lb, ng, tb):
    t = proj.shape[0]
    nh = proj.shape[1] // (4 * HEAD)
    tb = min(tb, t)
    nb = t // tb
    wide = HG_HEADS * HEAD

    def body(q_ref, f_ref, i_ref, g_ref, lb_ref, ng_ref, mask_ref, y_ref, s_ref, st_ref):
        i = pl.program_id(1)

        @pl.when(i == 0)
        def _():
            st_ref[...] = jnp.zeros_like(st_ref)

        consts = _hg_consts(mask_ref[...])
        for p in range(HG_HEADS):
            cs = slice(p * HEAD, (p + 1) * HEAD)
            st = st_ref[p]
            s_ref[p, 0] = st
            y, st_new = _hg_block(q_ref[:, cs], f_ref[:, cs], i_ref[:, cs], g_ref[:, cs], lb_ref[:, cs],
                                  ng_ref[...], st, *consts)
            y_ref[:, cs] = y.astype(y_ref.dtype)
            st_ref[p] = st_new

    return pl.pallas_call(
        body, name="hgrn2_fwd",
        grid=(nh // HG_HEADS, nb),
        in_specs=_hg_specs(tb, nh),
        out_specs=[pl.BlockSpec((tb, wide), lambda h, i: (i, h)),
                   pl.BlockSpec((HG_HEADS, 1, HEAD, HEAD), lambda h, i: (h, i, 0, 0))],
        out_shape=[jax.ShapeDtypeStruct((t, nh * HEAD), BF16),
                   jax.ShapeDtypeStruct((nh, nb, HEAD, HEAD), F32)],
        scratch_shapes=[pltpu.VMEM((HG_HEADS, HEAD, HEAD), F32)],
        compiler_params=_cparams(dimension_semantics=("parallel", "arbitrary")),
    )(proj, proj, proj, proj, lb, ng, _hg_mask(tb))


def _hgrn2_bwd(proj, lb, ng, states, dy, tb):
    t = proj.shape[0]
    nh = proj.shape[1] // (4 * HEAD)
    tb = min(tb, t)
    nb = t // tb
    wide = HG_HEADS * HEAD

    def body(q_ref, f_ref, i_ref, g_ref, lb_ref, ng_ref, mask_ref, s_ref, dy_ref,
             dp_ref, dlb_ref, dng_ref, dst_ref):
        h, i = pl.program_id(0), pl.program_id(1)
        consts = _hg_consts(mask_ref[...])

        @pl.when(i == 0)
        def _():
            dst_ref[...] = jnp.zeros_like(dst_ref)
            dlb_ref[...] = jnp.zeros_like(dlb_ref)

        @pl.when(jnp.logical_and(i == 0, h == 0))
        def _():
            dng_ref[...] = jnp.zeros_like(dng_ref)

        def fn(qp, fp, ip, gp, lbx, ngx, stx):
            return _hg_block(qp, fp, ip, gp, lbx, ngx, stx, *consts)

        for p in range(HG_HEADS):
            cs = slice(p * HEAD, (p + 1) * HEAD)
            _, vjp_fn = jax.vjp(fn, q_ref[:, cs], f_ref[:, cs], i_ref[:, cs], g_ref[:, cs], lb_ref[:, cs],
                                ng_ref[...], s_ref[p, 0])
            *gparts, glb, gng, dst = vjp_fn((dy_ref[:, cs].astype(F32), dst_ref[p]))
            for part, gpart in enumerate(gparts):
                dp_ref[part, :, cs] = gpart.astype(dp_ref.dtype)
            dst_ref[p] = dst
            dlb_ref[:, cs] += glb
            dng_ref[...] += gng

    rev = lambda h, i: (nb - 1 - i, h)
    return pl.pallas_call(
        body, name="hgrn2_bwd",
        grid=(nh // HG_HEADS, nb),
        in_specs=_hg_specs(tb, nh, rev_nb=nb) + [
            pl.BlockSpec((HG_HEADS, 1, HEAD, HEAD), lambda h, i: (h, nb - 1 - i, 0, 0)),
            pl.BlockSpec((tb, wide), rev)],
        out_specs=[pl.BlockSpec((4, tb, wide), lambda h, i: (0, nb - 1 - i, h)),
                   pl.BlockSpec((1, wide), lambda h, i: (0, h)), pl.BlockSpec((1, HEAD), lambda h, i: (0, 0))],
        out_shape=[jax.ShapeDtypeStruct((4, t, nh * HEAD), BF16),
                   jax.ShapeDtypeStruct((1, nh * HEAD), F32), jax.ShapeDtypeStruct((1, HEAD), F32)],
        scratch_shapes=[pltpu.VMEM((HG_HEADS, HEAD, HEAD), F32)],
        compiler_params=_cparams(dimension_semantics=("arbitrary", "arbitrary")),
    )(proj, proj, proj, proj, lb, ng, _hg_mask(tb), states, dy)


def _fgate_consts(cb):
    r = lax.broadcasted_iota(jnp.int32, (cb, cb), 0)
    s = lax.broadcasted_iota(jnp.int32, (cb, cb), 1)
    return (r <= s).astype(F32), (r >= s).astype(F32)


def _fgate_fwd(xt, bias, cb=512):
    nh, t = xt.shape
    cb = min(cb, t)

    def body(x_ref, b_ref, o_ref):
        upper, _ = _fgate_consts(cb)
        carry = jnp.zeros((nh, 1), F32)
        for blk in range(t // cb):
            z = x_ref[:, blk * cb:(blk + 1) * cb] + b_ref[...]
            logf = jnp.minimum(z, 0.0) - jnp.log(1.0 + jnp.exp(-jnp.abs(z)))
            cs = _f32dot(logf, upper) + carry
            o_ref[:, blk * cb:(blk + 1) * cb] = cs
            carry = cs[:, cb - 1:cb]

    vm = pl.BlockSpec(memory_space=pltpu.VMEM)
    return pl.pallas_call(
        body, name="fgate_fwd", in_specs=[vm, vm], out_specs=vm,
        out_shape=jax.ShapeDtypeStruct((nh, t), F32), compiler_params=_cparams(),
    )(xt, bias)


def _fgate_bwd(xt, bias, dft, cb=512):
    nh, t = xt.shape
    cb = min(cb, t)
    nblk = t // cb

    def body(x_ref, b_ref, d_ref, dx_ref, db_ref):
        _, lower = _fgate_consts(cb)
        carry = jnp.zeros((nh, 1), F32)
        db = jnp.zeros((nh, 1), F32)
        for blk in range(nblk - 1, -1, -1):
            sl = slice(blk * cb, (blk + 1) * cb)
            dlogf = _f32dot(d_ref[:, sl], lower) + carry
            carry = dlogf[:, 0:1]
            z = x_ref[:, sl] + b_ref[...]
            dz = dlogf * (1.0 - _sigmoid(z))
            dx_ref[:, sl] = dz
            db = db + jnp.sum(dz, axis=1, keepdims=True)
        db_ref[...] = db

    vm = pl.BlockSpec(memory_space=pltpu.VMEM)
    return pl.pallas_call(
        body, name="fgate_bwd", in_specs=[vm, vm, vm], out_specs=[vm, vm],
        out_shape=[jax.ShapeDtypeStruct((nh, t), F32), jax.ShapeDtypeStruct((nh, 1), F32)],
        compiler_params=_cparams(),
    )(xt, bias, dft)


ATTN_GROUPS = 4
ATTN_FWD_HEADS = 2


def _attn_fwd(q, k, v, f_grp, blk):
    t, width = v.shape
    nh = width // HEAD
    nq = t // blk
    hpg = nh // ATTN_GROUPS

    def body(q_ref, k_ref, v_ref, fc_ref, o_ref, lse_ref):
        i = pl.program_id(0)
        tri = (lax.broadcasted_iota(jnp.int32, (blk, blk), 1) <= lax.broadcasted_iota(jnp.int32, (blk, blk), 0))
        for h0 in range(0, nh, ATTN_FWD_HEADS):
            heads = range(h0, min(h0 + ATTN_FWD_HEADS, nh))

            def tile(j, carries, masked):
                rs = pl.ds(pl.multiple_of(j * blk, blk), blk)
                out = []
                for h, (m, l, acc) in zip(heads, carries):
                    cs = slice(h * HEAD, (h + 1) * HEAD)
                    cs2 = slice(2 * h * HEAD, 2 * (h + 1) * HEAD)
                    s = _bdot_raw(q_ref[:, cs2], k_ref[rs, cs2], _NT)
                    if masked:
                        s = jnp.where(tri, s, NEG_INF)
                    m_new = jnp.maximum(m, jnp.max(s, axis=1, keepdims=True))
                    p = jnp.exp(s - m_new)
                    alpha = jnp.exp(m - m_new)
                    l_new = alpha * l + jnp.sum(p, axis=1, keepdims=True)
                    out.append((m_new, l_new, alpha * acc + _bdot_raw(p, v_ref[rs, cs], _NN)))
                return tuple(out)

            init = tuple((jnp.full((blk, 1), NEG_INF, F32), jnp.zeros((blk, 1), F32), jnp.zeros((blk, HEAD), F32))
                         for _ in heads)
            carries = lax.fori_loop(0, i, lambda j, c: tile(j, c, False), init)
            for h, (m, l, acc) in zip(heads, tile(i, carries, True)):
                o_ref[:, h * HEAD:(h + 1) * HEAD] = acc / l
                g, hh = divmod(h, hpg)
                lse_ref[g, :, hh:hh + 1] = m + jnp.log(l) + fc_ref[g, :, hh:hh + 1]

    vm = pl.BlockSpec(memory_space=pltpu.VMEM)
    stat = pl.BlockSpec((ATTN_GROUPS, blk, hpg), lambda i: (0, i, 0))
    return pl.pallas_call(
        body, name="fox_attn_fwd",
        grid=(nq,),
        in_specs=[pl.BlockSpec((blk, 2 * width), lambda i: (i, 0)), vm, vm, stat],
        out_specs=[pl.BlockSpec((blk, width), lambda i: (i, 0)), stat],
        out_shape=[jax.ShapeDtypeStruct((t, width), F32), jax.ShapeDtypeStruct((ATTN_GROUPS, t, hpg), F32)],
        compiler_params=_cparams(dimension_semantics=("parallel",)),
    )(q, k, v, f_grp)


def _attn_delta(do, o, tb):
    t, width = o.shape
    nh = width // HEAD
    hpg = nh // ATTN_GROUPS
    tb = min(tb, t)

    def body(do_ref, o_ref, dl_ref):
        for h in range(nh):
            cs = slice(h * HEAD, (h + 1) * HEAD)
            g, hh = divmod(h, hpg)
            dl_ref[g, :, hh:hh + 1] = jnp.sum(do_ref[:, cs].astype(F32) * o_ref[:, cs], axis=1, keepdims=True)

    wide = pl.BlockSpec((tb, width), lambda i: (i, 0))
    return pl.pallas_call(body, name="fox_attn_delta", grid=(t // tb,), in_specs=[wide, wide],
                          out_specs=pl.BlockSpec((ATTN_GROUPS, tb, hpg), lambda i: (0, i, 0)),
                          out_shape=jax.ShapeDtypeStruct((ATTN_GROUPS, t, hpg), F32),
                          compiler_params=_cparams(dimension_semantics=("parallel",)))(do, o)


def _attn_bwd(q, k, v, f_grp, do, lse, delta, blk):
    t, width = v.shape
    nh = width // HEAD
    nq = t // blk
    hpg = nh // ATTN_GROUPS
    gw = hpg * HEAD

    def body(q_ref, do_ref, k_ref, v_ref, fc_ref, lse_ref, dl_ref,
             dq_ref, dk_ref, dv_ref, dfc_ref, dfr_ref):
        g, j = pl.program_id(0), pl.program_id(1)
        tri = (lax.broadcasted_iota(jnp.int32, (blk, blk), 1) <= lax.broadcasted_iota(jnp.int32, (blk, blk), 0))

        @pl.when(j == 0)
        def _():
            dq_ref[...] = jnp.zeros_like(dq_ref)
            dfc_ref[...] = jnp.zeros_like(dfc_ref)

        def tile(i, carries, masked):
            rs = pl.ds(pl.multiple_of(i * blk, blk), blk)
            out = []
            for h, (dk, dv, dfs) in enumerate(carries):
                cs = slice(h * HEAD, (h + 1) * HEAD)
                cs2 = slice(2 * h * HEAD, 2 * (h + 1) * HEAD)
                csq = slice(2 * h * HEAD, (2 * h + 1) * HEAD)
                qi = q_ref[rs, csq]
                doi = do_ref[rs, cs]
                bias = fc_ref[0, rs, h:h + 1] - lse_ref[0, rs, h:h + 1]
                p = jnp.exp(_bdot_raw(q_ref[rs, cs2], k_ref[:, cs2], _NT) + bias)
                if masked:
                    p = jnp.where(tri, p, 0.0)
                ds = p * (_bdot_raw(doi, v_ref[:, cs], _NT) - dl_ref[0, rs, h:h + 1])
                dsb = ds.astype(BF16)
                dq_ref[rs, cs] += _bdot_raw(dsb, k_ref[:, csq], _NN)
                dfc_ref[0, rs, h:h + 1] += jnp.sum(ds, axis=1, keepdims=True)
                out.append((dk + _bdot_raw(dsb, qi, _TN), dv + _bdot_raw(p, doi, _TN),
                            dfs - jnp.sum(ds, axis=0, keepdims=True)))
            return tuple(out)

        init = tuple((jnp.zeros((blk, HEAD), F32), jnp.zeros((blk, HEAD), F32), jnp.zeros((1, blk), F32))
                     for _ in range(hpg))
        carries = lax.fori_loop(j + 1, nq, lambda i, c: tile(i, c, False), tile(j, init, True))
        for h, (dk, dv, dfs) in enumerate(carries):
            cs = slice(h * HEAD, (h + 1) * HEAD)
            dk_ref[:, cs] = dk
            dv_ref[:, cs] = dv.astype(dv_ref.dtype)
            dfr_ref[0, 0, h:h + 1, :] = dfs

    once = pl.Buffered(1)
    stat = pl.BlockSpec((1, t, hpg), lambda g, j: (g, 0, 0), pipeline_mode=once)
    kv_blk = pl.BlockSpec((blk, gw), lambda g, j: (j, g))
    frow = pl.BlockSpec((1, 1, hpg, blk), lambda g, j: (g, j, 0, 0))
    dq, dk, dv, dfc, dfr = pl.pallas_call(
        body, name="fox_attn_bwd",
        grid=(ATTN_GROUPS, nq),
        in_specs=[pl.BlockSpec((t, 2 * gw), lambda g, j: (0, g), pipeline_mode=once),
                  pl.BlockSpec((t, gw), lambda g, j: (0, g), pipeline_mode=once),
                  pl.BlockSpec((blk, 2 * gw), lambda g, j: (j, g)), kv_blk, stat, stat, stat],
        out_specs=[pl.BlockSpec((t, gw), lambda g, j: (0, g)), kv_blk, kv_blk,
                   pl.BlockSpec((1, t, hpg), lambda g, j: (g, 0, 0)), frow],
        out_shape=[jax.ShapeDtypeStruct((t, width), F32), jax.ShapeDtypeStruct((t, width), F32),
                   jax.ShapeDtypeStruct((t, width), BF16), jax.ShapeDtypeStruct((ATTN_GROUPS, t, hpg), F32),
                   jax.ShapeDtypeStruct((ATTN_GROUPS, nq, hpg, blk), F32)],
        compiler_params=_cparams(dimension_semantics=("parallel", "arbitrary")),
    )(q, do, k, v, f_grp, lse, delta)
    return dq, dk, dv, dfc, dfr


SUBLANES = 8


def _shift_down(u, n):
    r = pltpu.roll(u, n, 0)
    row = lax.broadcasted_iota(jnp.int32, (SUBLANES, u.shape[1]), 0)
    return jnp.concatenate([jnp.where(row < n, 0.0, r[:SUBLANES]), r[SUBLANES:]], axis=0)


def _shift_up(u, n):
    t = u.shape[0]
    r = pltpu.roll(u, t - n, 0)
    row = lax.broadcasted_iota(jnp.int32, (SUBLANES, u.shape[1]), 0)
    return jnp.concatenate([r[:t - SUBLANES], jnp.where(row >= SUBLANES - n, 0.0, r[t - SUBLANES:])], axis=0)


def _convglu_specs(t):
    return [pl.BlockSpec((2, t, LANES), lambda j: (0, 0, j)),
            pl.BlockSpec((2, CONV_TAPS, LANES), lambda j: (0, 0, j)),
            pl.BlockSpec((2, 1, LANES), lambda j: (0, 0, j))]


def _convglu_fwd(u, cw, cb):
    _, t, fp = u.shape

    def body(u_ref, w_ref, b_ref, a_ref, c_ref):
        c = []
        for hf in range(2):
            uv, w = u_ref[hf].astype(F32), w_ref[hf]
            c.append(w[0:1] * _shift_down(uv, 2) + w[1:2] * _shift_down(uv, 1) + w[2:3] * uv + b_ref[hf])
            c_ref[hf] = c[hf].astype(c_ref.dtype)
        a_ref[...] = (_silu(c[0]) * c[1]).astype(a_ref.dtype)

    return pl.pallas_call(
        body, name="convglu_fwd",
        grid=(fp // LANES,),
        in_specs=_convglu_specs(t),
        out_specs=[pl.BlockSpec((t, LANES), lambda j: (0, j)), pl.BlockSpec((2, t, LANES), lambda j: (0, 0, j))],
        out_shape=[jax.ShapeDtypeStruct((t, fp), BF16), jax.ShapeDtypeStruct((2, t, fp), BF16)],
        compiler_params=_cparams(dimension_semantics=("parallel",)),
    )(u, cw, cb)


def _convglu_bwd(u, c, cw, da):
    _, t, fp = u.shape

    def body(u_ref, c_ref, w_ref, da_ref, du_ref, dw_ref, db_ref):
        gc, vc = c_ref[0].astype(F32), c_ref[1].astype(F32)
        sg = _sigmoid(gc)
        dav = da_ref[...].astype(F32)
        dcs = [dav * vc * (sg * (1.0 + gc * (1.0 - sg))), dav * (gc * sg)]
        for hf in range(2):
            dc, w, uv = dcs[hf], w_ref[hf], u_ref[hf].astype(F32)
            dc1, dc2 = _shift_up(dc, 1), _shift_up(dc, 2)
            du_ref[hf] = (w[2:3] * dc + w[1:2] * dc1 + w[0:1] * dc2).astype(du_ref.dtype)
            dw_ref[hf, 0:1, :] = jnp.sum(dc2 * uv, axis=0, keepdims=True)
            dw_ref[hf, 1:2, :] = jnp.sum(dc1 * uv, axis=0, keepdims=True)
            dw_ref[hf, 2:3, :] = jnp.sum(dc * uv, axis=0, keepdims=True)
            db_ref[hf] = jnp.sum(dc, axis=0, keepdims=True)

    pair, taps, bias = _convglu_specs(t)
    return pl.pallas_call(
        body, name="convglu_bwd",
        grid=(fp // LANES,),
        in_specs=[pair, pair, taps, pl.BlockSpec((t, LANES), lambda j: (0, j))],
        out_specs=[pair, taps, bias],
        out_shape=[jax.ShapeDtypeStruct((2, t, fp), BF16), jax.ShapeDtypeStruct((2, CONV_TAPS, fp), F32),
                   jax.ShapeDtypeStruct((2, 1, fp), F32)],
        compiler_params=_cparams(dimension_semantics=("parallel",)),
    )(u, c, cw, da)


def _local_step(x, target, mods, lb, small, pre_w, get_w, put_g, *, tb=512, attn_blk=512):
    t, d = x.shape
    nh = d // HEAD
    nb = NDEV
    wts = {}
    vec = lambda *names: [mods[n] for n in names]

    def ffn_fwd(h2, l):
        u = _mm_wblk(h2, wts[f"up{l}"], BF16, f"ffn{l}_up", gb=nb // 2, split=2, tm=512)
        a, c = _convglu_fwd(u, small[f"conv_w{l}"], small[f"conv_b{l}"])
        f = _mm(a, wts[f"down{l}"], "nn", F32, f"ffn{l}_down", tk=4096)
        return (u, c), a, f

    def ffn_bwd(df, h2, uc, a, l):
        u, c = uc
        da = _mm(df, wts[f"down{l}"], "nt", BF16, f"ffn{l}_down_dx", tn=1536)
        dwd = _mm(a, df, "tn", BF16, f"ffn{l}_down_dw", tm=768, tk=t)
        du, dcw, dcb = _convglu_bwd(u, c, small[f"conv_w{l}"], da)
        dh2 = _mm_wblk_dx(du, wts[f"up{l}"], BF16, f"ffn{l}_up_dx", k=d, gb=nb // 2, split=2, tm=1024)
        dwu = _mm_wblk_dw(h2, du, f"ffn{l}_up_dw", nb=nb, gb=1, split=2, tk=t)
        return dh2, dwu, dwd, dcw, dcb

    (h_a,) = _row_fwd(_f_mod, [(x, d, 0)], vec("sh1_0", "sc1_0"), [BF16], tb=tb, name="l0_mod1")
    wts.update(get_w("l0a", h_a))
    proj_a = _mm_wblk(h_a, wts["a_in"], F32, "a_in", gb=nb // 2)
    ypre, states = _hgrn2_fwd(proj_a, lb, small["a_norm_g"], tb)
    pre_w("l0b", ypre)
    wts.update(get_w("l0b", ypre))
    y_a = _mm(ypre, wts["a_out"], "nn", F32, "a_out")
    x1, h2_0 = _row_fwd(_f_res_mod, [(x, d, 0), (y_a, d, 0)], vec("g1_0", "sh2_0", "sc2_0"), [F32, BF16],
                        tb=tb, name="l0_res_mod2")
    wts.update(get_w("l0b_ffn", h2_0))
    u0, a0, f0 = ffn_fwd(h2_0, 0)
    x2, h_kv, h_q = _row_fwd(_f_res_mod2, [(x1, d, 0), (f0, d, 0)],
                             [mods["g2_0"] + pre_w("l1", f0)] + vec("kv_sh", "kv_sc", "sh1_1", "sc1_1"),
                             [F32, BF16, BF16], tb=tb, name="l0_res_kvmod_qmod")
    wts.update(get_w("l1", h_kv))
    proj_k = _mm(h_kv, wts["kv_k"], "nt", F32, "k_proj")
    v_b = _mm(h_kv, wts["kv_v"], "nt", BF16, "v_proj")
    proj_f = _mm(h_kv, wts["kv_f"], "nt", F32, "kv_fproj")
    f_logit_t = proj_f[:, :nh].T
    f_bias = small["kv_b_f"].reshape(nh, 1)
    f_t = _fgate_fwd(f_logit_t, f_bias)
    f_grp = f_t.reshape(ATTN_GROUPS, nh // ATTN_GROUPS, t).transpose(0, 2, 1)
    (k_n,) = _row_fwd(_f_knorm_aug, [(proj_k, HEAD, 0)] + [(piece, 1, 0) for piece in _split3(-f_t.T)],
                      [small["k_norm_g"]], [BF16], nsub=nh, tb=tb, name="k_norm")
    proj_q = _mm_wblk(h_q, wts["b_q"], F32, "b_q", gb=nb)
    (q_n,) = _row_fwd(_f_qnorm_aug, [(proj_q, HEAD, 0)], [small["q_norm_g"]], [BF16], nsub=nh, tb=tb,
                      name="q_norm")
    o_att, lse = _attn_fwd(q_n, k_n, v_b, f_grp, attn_blk)
    (z,) = _row_fwd(_f_outgate, [(o_att, HEAD, 0), (proj_q, HEAD, 1)], [], [BF16], nsub=nh, tb=tb, name="out_gate")
    y_b = _mm(z, wts["b_out"], "nn", F32, "b_out")
    x3, h2_1 = _row_fwd(_f_res_mod, [(x2, d, 0), (y_b, d, 0)], vec("g1_1", "sh2_1", "sc2_1"), [F32, BF16],
                        tb=tb, name="l1_res_mod2")
    u1, a1, f1 = ffn_fwd(h2_1, 1)
    loss, dx4, df1, dg2_1 = _loss_call(x3, f1, mods["g2_1"], target, tb)

    g = {}
    dmods = {"g2_1": dg2_1}
    dh2, g["up1"], g["down1"], g["conv_w1"], g["conv_b1"] = ffn_bwd(df1, h2_1, u1, a1, 1)
    (dx2, dy_b), (dmods["g1_1"], dmods["sh2_1"], dmods["sc2_1"]) = _row_bwd(
        _f_res_mod, [(x2, d, 0), (y_b, d, 0)], vec("g1_1", "sh2_1", "sc2_1"),
        [(dx4, d, 0), (dh2, d, 0)], [F32, BF16], tb=tb, name="l1_res_mod2_bwd")
    dz = _mm(dy_b, wts["b_out"], "nt", BF16, "b_out_dx")
    g["b_out"] = _mm(z, dy_b, "tn", BF16, "b_out_dw", tk=t)
    (do_att, dog), _ = _row_bwd(_f_outgate, [(o_att, HEAD, 0), (proj_q, HEAD, 1)], [], [(dz, HEAD, 0)],
                                [BF16, BF16], nsub=nh, tb=tb, name="out_gate_bwd")
    delta = _attn_delta(do_att, o_att, tb)
    dq_n, dk_n, dv, dfc_q, dfr_k = _attn_bwd(q_n, k_n, v_b, f_grp, do_att, lse, delta, attn_blk)
    (dpq,), (g["q_norm_g"],) = _row_bwd(_f_qnorm, [(proj_q, HEAD, 0)], [small["q_norm_g"]],
                                        [(dq_n, HEAD, 0)], [BF16], nsub=nh, tb=tb, name="q_norm_bwd")
    dproj_q = jnp.concatenate([dpq, dog], axis=1)
    dh_q = _mm_wblk_dx(dproj_q, wts["b_q"], BF16, "b_q_dx", k=d, gb=nb)
    g["b_q"] = _mm_wblk_dw(h_q, dproj_q, "b_q_dw", nb=nb, gb=nb // 4, tk=t)
    (dpk,), (g["k_norm_g"],) = _row_bwd(_f_knorm, [(proj_k, HEAD, 0)], [small["k_norm_g"]],
                                        [(dk_n, HEAD, 0)], [BF16], nsub=nh, tb=tb, name="k_norm_bwd")
    df_t = dfc_q.transpose(0, 2, 1).reshape(nh, t) + dfr_k.transpose(0, 2, 1, 3).reshape(nh, t)
    dflogit_t, g["kv_b_f"] = _fgate_bwd(f_logit_t, f_bias, df_t)
    dproj_f = jnp.pad(dflogit_t.T, ((0, 0), (0, LANES - nh))).astype(BF16)
    dh_kv = _mm(dpk, wts["kv_k"], "nn", BF16, "k_proj_dx")
    dh_kv_v = _mm(dv, wts["kv_v"], "nn", BF16, "v_proj_dx")
    dh_kv_f = _mm(dproj_f, wts["kv_f"], "nn", BF16, "kv_fproj_dx")
    g["kv_k"] = _mm(dpk, h_kv, "tn", BF16, "k_proj_dw", tk=t)
    g["kv_v"] = _mm(dv, h_kv, "tn", BF16, "v_proj_dw", tk=t)
    g["kv_f"] = _mm(dproj_f, h_kv, "tn", F32, "kv_fproj_dw", tk=1024)
    sent = put_g("l1", {n: g.pop(n) for n in ("b_out", "b_q", "kv_k", "kv_v", "kv_f", "up1", "down1")})
    (dx1, df0), (dmods["g2_0"], dmods["kv_sh"], dmods["kv_sc"], dmods["sh1_1"], dmods["sc1_1"]) = _row_bwd(
        _f_res_mod2, [(x1, d, 0), (f0, d, 0)], [mods["g2_0"] + sent] + vec("kv_sh", "kv_sc", "sh1_1", "sc1_1"),
        [(dx2, d, 0), (dh_kv, d, 0), (dh_q, d, 0)], [F32, BF16], tb=tb, name="l0_res_kvmod_qmod_bwd",
        cot_add=[(1, dh_kv_v), (1, dh_kv_f)])
    dh2, g["up0"], g["down0"], g["conv_w0"], g["conv_b0"] = ffn_bwd(df0, h2_0, u0, a0, 0)
    (dx0, dy_a), (dmods["g1_0"], dmods["sh2_0"], dmods["sc2_0"]) = _row_bwd(
        _f_res_mod, [(x, d, 0), (y_a, d, 0)], vec("g1_0", "sh2_0", "sc2_0"),
        [(dx1, d, 0), (dh2, d, 0)], [F32, BF16], tb=tb, name="l0_res_mod2_bwd")
    dypre = _mm(dy_a, wts["a_out"], "nt", BF16, "a_out_dx")
    g["a_out"] = _mm(ypre, dy_a, "tn", BF16, "a_out_dw", tk=t)
    sent = put_g("l0b", {n: g.pop(n) for n in ("a_out", "up0", "down0")})
    dproj_a, dlb, g["a_norm_g"] = _hgrn2_bwd(proj_a, lb + sent, small["a_norm_g"], states, dypre, tb)
    dh_a = _mm_wblk_dx(dproj_a, wts["a_in"], BF16, "a_in_dx", k=d, gb=nb, split=4, tm=512)
    put_g("l0a", {"a_in": _mm_wblk_dw(h_a, dproj_a, "a_in_dw", nb=nb, gb=1, split=4, tk=t)})
    (grad_x,), (dmods["sh1_0"], dmods["sc1_0"]) = _row_bwd(
        _f_mod, [(x, d, 0)], vec("sh1_0", "sc1_0"), [(dh_a, d, 0)], [F32], tb=tb, name="l0_mod1_bwd",
        add_to=(0, dx0))
    return loss, grad_x, dmods, dlb, g


def _position():
    return lax.axis_index("x"), lax.axis_index("y"), lax.axis_index("c")


_XCHG_EFFECT = pltpu.SideEffectType.DATAFLOW_SIDE_EFFECTING
ALL_PEERS = (1, 2, 3, 4, 5, 6, 7)
SAME_CORE = (2, 4, 6)


def _xchg_copies(src_refs, land_refs, send_sems, recv_sems, local_sems, scatter, rels):
    x, y, cc = _position()
    me = 4 * x + 2 * y + cc
    remote, local = [], []
    for a, (src, land) in enumerate(zip(src_refs, land_refs)):
        local.append(pltpu.make_async_copy(src.at[me] if scatter else src, land.at[me], local_sems.at[a]))
        for idx, rel in enumerate(rels):
            px = 1 - x if rel & 4 else x
            py = 1 - y if rel & 2 else y
            pc = 1 - cc if rel & 1 else cc
            k = len(rels) * a + idx
            remote.append(pltpu.make_async_remote_copy(
                src_ref=src.at[4 * px + 2 * py + pc] if scatter else src, dst_ref=land.at[me],
                send_sem=send_sems.at[k], recv_sem=recv_sems.at[k], device_id=(px, py, pc), device_id_type=_MESH))
    return remote, local


def _xchg_start(srcs, scatter, rels, after, name):
    n = len(srcs)
    lands = [lax.empty(s.shape if scatter else (NDEV, *s.shape), s.dtype) for s in srcs]

    def body(*refs):
        remote, local = _xchg_copies(refs[:n], refs[n:2 * n], *refs[2 * n + 1:2 * n + 4], scatter, rels)
        for cp in local + remote:
            cp.start()
        token = refs[-1]
        token[...] = jnp.zeros_like(token)

    hbm = pl.BlockSpec(memory_space=pltpu.HBM)
    sem = pl.BlockSpec(memory_space=pltpu.SEMAPHORE)
    out = pl.pallas_call(
        body, name=name,
        out_shape=(pltpu.SemaphoreType.DMA((len(rels) * n,)), pltpu.SemaphoreType.DMA((len(rels) * n,)),
                   pltpu.SemaphoreType.DMA((n,)),
                   *[pltpu.HBM(a.shape, a.dtype) for a in srcs + lands], jax.ShapeDtypeStruct((8, LANES), F32)),
        in_specs=[hbm] * (2 * n) + [pl.BlockSpec(memory_space=pl.ANY)],
        out_specs=(sem, sem, sem, *[hbm] * (2 * n), pl.BlockSpec(memory_space=pltpu.VMEM)),
        input_output_aliases={i: 3 + i for i in range(2 * n)},
        compiler_params=pltpu.CompilerParams(has_side_effects=_XCHG_EFFECT),
    )(*[pltpu.with_memory_space_constraint(a, pltpu.HBM) for a in srcs + lands], after)
    return out[:-1], out[-1][0, 0]


def _xchg_wait(handles, after, scatter, rels, name):
    n = (len(handles) - 3) // 2

    def body(*refs):
        remote, local = _xchg_copies(refs[:n], refs[n:2 * n], *refs[2 * n:2 * n + 3], scatter, rels)
        for cp in remote:
            cp.wait_send()
            cp.wait_recv()
        for cp in local:
            cp.wait()

    hbm = pl.BlockSpec(memory_space=pltpu.HBM)
    sem = pl.BlockSpec(memory_space=pltpu.SEMAPHORE)
    thru = list(handles[3:])
    afters = list(after) if isinstance(after, (list, tuple)) else [after]
    out = pl.pallas_call(
        body, name=name,
        out_shape=tuple(pltpu.HBM(a.shape, a.dtype) for a in thru),
        in_specs=[hbm] * (2 * n) + [sem, sem, sem] + [pl.BlockSpec(memory_space=pl.ANY)] * len(afters),
        out_specs=tuple([hbm] * (2 * n)),
        input_output_aliases={i: i for i in range(2 * n)},
        compiler_params=pltpu.CompilerParams(has_side_effects=_XCHG_EFFECT),
    )(*thru, *handles[:3], *afters)
    return list(out[n:])


def _sibling_copies(land_refs, send_sems, recv_sems):
    x, y, cc = _position()

    def copy(a, q, core):
        slot = land_refs[a].at[2 * q + core]
        return pltpu.make_async_remote_copy(
            src_ref=slot, dst_ref=slot, send_sem=send_sems.at[NCHIP * a + q], recv_sem=recv_sems.at[NCHIP * a + q],
            device_id=(x, y, 1 - cc), device_id_type=_MESH)

    pairs = [(a, q) for a in range(len(land_refs)) for q in range(NCHIP)]
    return [copy(a, q, cc) for a, q in pairs], [copy(a, q, 1 - cc) for a, q in pairs]


def _sibling_forward_start(lands, name, after=None):
    n = len(lands)
    deps = [] if after is None else [after]

    def body(*refs):
        sends, _ = _sibling_copies(refs[:n], refs[n + len(deps)], refs[n + len(deps) + 1])
        for cp in sends:
            cp.start()
        refs[-1][...] = jnp.zeros_like(refs[-1])

    hbm = pl.BlockSpec(memory_space=pltpu.HBM)
    sem = pl.BlockSpec(memory_space=pltpu.SEMAPHORE)
    out = pl.pallas_call(
        body, name=name,
        out_shape=(pltpu.SemaphoreType.DMA((NCHIP * n,)), pltpu.SemaphoreType.DMA((NCHIP * n,)),
                   *[pltpu.HBM(a.shape, a.dtype) for a in lands], jax.ShapeDtypeStruct((8, LANES), F32)),
        in_specs=[hbm] * n + [pl.BlockSpec(memory_space=pl.ANY)] * len(deps),
        out_specs=(sem, sem, *[hbm] * n, pl.BlockSpec(memory_space=pltpu.VMEM)),
        input_output_aliases={i: 2 + i for i in range(n)},
        compiler_params=pltpu.CompilerParams(has_side_effects=_XCHG_EFFECT),
    )(*lands, *deps)
    return out[:-1], out[-1][0, 0]


def _sibling_forward_wait(handles, after, name):
    n = len(handles) - 2

    def body(*refs):
        sends, arrivals = _sibling_copies(refs[:n], refs[n], refs[n + 1])
        for cp in sends:
            cp.wait_send()
        for cp in arrivals:
            cp.wait_recv()

    hbm = pl.BlockSpec(memory_space=pltpu.HBM)
    sem = pl.BlockSpec(memory_space=pltpu.SEMAPHORE)
    lands = list(handles[2:])
    return list(pl.pallas_call(
        body, name=name,
        out_shape=tuple(pltpu.HBM(a.shape, a.dtype) for a in lands),
        in_specs=[hbm] * n + [sem, sem, pl.BlockSpec(memory_space=pl.ANY)],
        out_specs=tuple([hbm] * n),
        input_output_aliases={i: i for i in range(n)},
        compiler_params=pltpu.CompilerParams(has_side_effects=_XCHG_EFFECT),
    )(*lands, *handles[:2], after))


def _slab_sum(slabs, name, tr=None):
    n, r, c = slabs.shape
    tr = r if tr is None else tr

    def body(s_ref, o_ref):
        acc = s_ref[0].astype(F32)
        for q in range(1, n):
            acc = acc + s_ref[q].astype(F32)
        o_ref[...] = acc

    return pl.pallas_call(body, name=name, grid=(r // tr,),
                          in_specs=[pl.BlockSpec((n, tr, c), lambda i: (0, i, 0))],
                          out_specs=pl.BlockSpec((tr, c), lambda i: (i, 0)),
                          out_shape=jax.ShapeDtypeStruct((r, c), F32),
                          compiler_params=_cparams(dimension_semantics=("parallel",)))(slabs)


def _ada_fwd(c_all, ada_w, kv_ada_w, logits):
    rows, d = c_all.shape
    n0, nkv = ada_w.shape[2], kv_ada_w.shape[1]

    def body(c_ref, w_ref, kw_ref, lg_ref, part_ref, cact_ref, lb_ref):
        ca = _silu(c_ref[...])
        cact_ref[...] = ca
        part_ref[:, 0:n0] = _bdot_raw(ca, w_ref[0], _NN)
        part_ref[:, n0:2 * n0] = _bdot_raw(ca, w_ref[1], _NN)
        part_ref[:, 2 * n0:2 * n0 + nkv] = _bdot_raw(ca, kw_ref[...], _NN)
        lb_ref[...] = _sigmoid(lg_ref[0:1, :] - lg_ref[1:2, :])

    vm = pl.BlockSpec(memory_space=pltpu.VMEM)
    return pl.pallas_call(
        body, name="ada_fwd", in_specs=[vm, vm, vm, vm], out_specs=[vm, vm, vm],
        out_shape=[jax.ShapeDtypeStruct((rows, 2 * n0 + nkv), F32), jax.ShapeDtypeStruct((rows, d), F32),
                   jax.ShapeDtypeStruct((1, d), F32)],
        compiler_params=_cparams(),
    )(c_all, ada_w, kv_ada_w, logits)


def _ada_bwd(c_act, dm0, dm1, dkv, lb, dlb):
    rows, d = c_act.shape

    def body(c_ref, d0_ref, d1_ref, dk_ref, lb_ref, dlb_ref, dw_ref, dkw_ref, dlg_ref):
        ca = c_ref[...]
        dw_ref[0] = _bdot_raw(ca, d0_ref[...], _TN)
        dw_ref[1] = _bdot_raw(ca, d1_ref[...], _TN)
        dkw_ref[...] = _bdot_raw(ca, dk_ref[...], _TN)
        lbv = lb_ref[...]
        dl0 = dlb_ref[...] * lbv * (1.0 - lbv)
        dlg_ref[0:1, :] = dl0
        dlg_ref[1:2, :] = -dl0

    vm = pl.BlockSpec(memory_space=pltpu.VMEM)
    return pl.pallas_call(
        body, name="ada_bwd", in_specs=[vm] * 6, out_specs=[vm, vm, vm],
        out_shape=[jax.ShapeDtypeStruct((2, d, dm0.shape[1]), F32), jax.ShapeDtypeStruct((d, dkv.shape[1]), F32),
                   jax.ShapeDtypeStruct((2, d), F32)],
        compiler_params=_cparams(),
    )(c_act, dm0, dm1, dkv, lb, dlb)


def _adamw(w, g, m, v, name, tr=512, after=None):
    r, c = w.shape
    tr = _divisor_tile(r, tr, unit=8)
    c1 = 1.0 - ADAM_B1 ** ADAM_STEP
    c2 = 1.0 - ADAM_B2 ** ADAM_STEP
    deps = [] if after is None else [after]

    def body(w_ref, g_ref, m_ref, v_ref, *rest):
        d_ref, mo_ref, vo_ref = rest[len(deps):]
        gv = g_ref[...]
        mn = ADAM_B1 * m_ref[...] + (1.0 - ADAM_B1) * gv
        vn = ADAM_B2 * v_ref[...] + (1.0 - ADAM_B2) * (gv * gv)
        d_ref[...] = -ADAM_LR * ((mn / c1) / (jnp.sqrt(vn / c2) + ADAM_EPS) + ADAM_WD * w_ref[...])
        mo_ref[...] = mn
        vo_ref[...] = vn

    spec = pl.BlockSpec((tr, c), lambda i: (i, 0))
    out = jax.ShapeDtypeStruct((r, c), F32)
    return pl.pallas_call(body, name=name, grid=(r // tr,),
                          in_specs=[spec] * 4 + [pl.BlockSpec(a.shape, lambda i: (0, 0)) for a in deps],
                          out_specs=[spec] * 3, out_shape=[out, out, out],
                          compiler_params=_cparams(dimension_semantics=("parallel",)))(w, g, m, v, *deps)


def _pad_rows(a, rows):
    return jnp.pad(a, ((0, rows - a.shape[0]), (0, 0)))


def _pack_small(parts, lanes=LANES, row_unit=8):
    flat = jnp.concatenate([p.reshape(-1).astype(F32) for p in parts])
    rows = _round_up(-(-flat.shape[0] // lanes), row_unit)
    return jnp.pad(flat, (0, rows * lanes - flat.shape[0])).reshape(rows, lanes)


def _unpack_small(flat, shapes):
    out, off = [], 0
    for s in shapes:
        n = 1
        for k in s:
            n *= k
        out.append(flat[off:off + n].reshape(s))
        off += n
    return out


def _pad_shard_cols(a, n_loc, n_pad):
    lead = a.shape[:-1]
    a = a.reshape(*lead, NDEV, n_loc)
    a = jnp.pad(a, [(0, 0)] * (len(lead) + 1) + [(0, n_pad - n_loc)])
    return a.reshape(*lead, NDEV * n_pad)


def _unpad_shard_cols(a, n_loc, n_pad):
    lead = a.shape[:-1]
    return a.reshape(*lead, NDEV, n_pad)[..., :n_loc].reshape(*lead, NDEV * n_loc)


def kernel(x, c, ada_w, ada_b, a_w_in, a_lb_logits, a_norm_g, a_w_out, kv_ada_w, kv_ada_b, kv_w, kv_b_f, k_norm_g, b_w_q, q_norm_g, b_w_out, ffn_w_up, ffn_conv_w, ffn_conv_b, ffn_w_down, loss_target, m_ada_w, m_ada_b, m_a_w_in, m_a_lb_logits, m_a_norm_g, m_a_w_out, m_kv_ada_w, m_kv_ada_b, m_kv_w, m_kv_b_f, m_k_norm_g, m_b_w_q, m_q_norm_g, m_b_w_out, m_ffn_w_up, m_ffn_conv_w, m_ffn_conv_b, m_ffn_w_down, v_ada_w, v_ada_b, v_a_w_in, v_a_lb_logits, v_a_norm_g, v_a_w_out, v_kv_ada_w, v_kv_ada_b, v_kv_w, v_kv_b_f, v_k_norm_g, v_b_w_q, v_q_norm_g, v_b_w_out, v_ffn_w_up, v_ffn_conv_w, v_ffn_conv_b, v_ffn_w_down):
    t, d = x.shape[1], x.shape[2]
    nh = d // HEAD
    ncw = ffn_w_up.shape[2]
    ncp = _round_up(ncw, LANES)
    two_f = ncw * NDEV
    ff = two_f // 2
    fp = ncp * NDEV // 2
    rd = ffn_w_down.shape[1]
    me = 4 * lax.axis_index("x") + 2 * lax.axis_index("y") + lax.axis_index("c")
    weights = dict(ada_w=ada_w, ada_b=ada_b, a_w_in=a_w_in, a_lb_logits=a_lb_logits, a_norm_g=a_norm_g,
                   a_w_out=a_w_out, kv_ada_w=kv_ada_w, kv_ada_b=kv_ada_b, kv_w=kv_w, kv_b_f=kv_b_f,
                   k_norm_g=k_norm_g, b_w_q=b_w_q, q_norm_g=q_norm_g, b_w_out=b_w_out, ffn_w_up=ffn_w_up,
                   ffn_conv_w=ffn_conv_w, ffn_conv_b=ffn_conv_b, ffn_w_down=ffn_w_down)
    m_in = dict(ada_w=m_ada_w, ada_b=m_ada_b, a_w_in=m_a_w_in, a_lb_logits=m_a_lb_logits, a_norm_g=m_a_norm_g,
                a_w_out=m_a_w_out, kv_ada_w=m_kv_ada_w, kv_ada_b=m_kv_ada_b, kv_w=m_kv_w, kv_b_f=m_kv_b_f,
                k_norm_g=m_k_norm_g, b_w_q=m_b_w_q, q_norm_g=m_q_norm_g, b_w_out=m_b_w_out, ffn_w_up=m_ffn_w_up,
                ffn_conv_w=m_ffn_conv_w, ffn_conv_b=m_ffn_conv_b, ffn_w_down=m_ffn_w_down)
    v_in = dict(ada_w=v_ada_w, ada_b=v_ada_b, a_w_in=v_a_w_in, a_lb_logits=v_a_lb_logits, a_norm_g=v_a_norm_g,
                a_w_out=v_a_w_out, kv_ada_w=v_kv_ada_w, kv_ada_b=v_kv_ada_b, kv_w=v_kv_w, kv_b_f=v_kv_b_f,
                k_norm_g=v_k_norm_g, b_w_q=v_b_w_q, q_norm_g=v_q_norm_g, b_w_out=v_b_w_out, ffn_w_up=v_ffn_w_up,
                ffn_conv_w=v_ffn_conv_w, ffn_conv_b=v_ffn_conv_b, ffn_w_down=v_ffn_w_down)
    order = list(weights)

    up_loc = jnp.pad(ffn_w_up, ((0, 0), (0, 0), (0, ncp - ncw))).astype(BF16)
    down_loc = ffn_w_down.astype(BF16)
    gather_names = {"l0b": ["a_out", "up0", "down0"], "l1": ["kv", "b_q", "b_out", "up1", "down1"]}
    forward_names = {"l0b": ["a_out"], "l0b_ffn": ["up0", "down0"], "l1": gather_names["l1"]}
    shards = {"a_out": a_w_out[0].astype(BF16), "up0": up_loc[0], "down0": down_loc[0], "kv": kv_w.T.astype(BF16),
              "b_q": b_w_q[0].astype(BF16), "b_out": b_w_out[0].astype(BF16), "up1": up_loc[1],
              "down1": down_loc[1]}
    pre = _pack_small([c, a_lb_logits, ffn_conv_w])
    in_flight = {}
    pre_flight, _ = _xchg_start([pre], False, ALL_PEERS, pre, "gather_small_inputs_start")
    (pre_all,) = _xchg_wait(pre_flight, pre, False, ALL_PEERS, "gather_small_inputs_wait")
    pre_all = pre_all.reshape(NDEV, -1)
    c_all = pre_all[:, :d]
    logits = pre_all[:, d:d + 2 * HEAD].reshape(NDEV, 2, HEAD).transpose(1, 0, 2).reshape(2, d)
    conv_w_full = pre_all[:, d + 2 * HEAD:d + 2 * HEAD + 2 * CONV_TAPS * ncw]
    conv_w_full = conv_w_full.reshape(NDEV, 2, CONV_TAPS, ncw).transpose(1, 2, 0, 3).reshape(2, CONV_TAPS, two_f)

    part, c_act, lb = _ada_fwd(_pad_rows(c_all, 2 * NDEV), ada_w, kv_ada_w, logits)
    part_flight, _ = _xchg_start([part[:NDEV]], False, ALL_PEERS, part, "gather_adaln_start")
    in_flight["l0a"], _ = _xchg_start([a_w_in[0].astype(BF16)], False, SAME_CORE, part_flight[-1], "gather_l0a_start")
    (part_all,) = _xchg_wait(part_flight, in_flight["l0a"][-1], False, ALL_PEERS, "gather_adaln_wait")
    forwarding = {}
    mine = lax.dynamic_index_in_dim(part_all, me, axis=1, keepdims=False)
    n0, nkv = ada_w.shape[2], kv_ada_w.shape[1]
    mod_names = ["sh1", "sc1", "g1", "sh2", "sc2", "g2"]
    mods = {}
    for l in range(2):
        row = mine[:, l * n0:(l + 1) * n0].reshape(-1) + ada_b[l]
        for k, nm in enumerate(mod_names):
            mods[f"{nm}_{l}"] = row[k * d:(k + 1) * d].reshape(1, d)
    kvrow = mine[:, 2 * n0:2 * n0 + nkv].reshape(-1) + kv_ada_b
    mods["kv_sh"], mods["kv_sc"] = kvrow[:d].reshape(1, d), kvrow[d:].reshape(1, d)

    def start_gather(grp, dep):
        srcs = [shards[n] for n in gather_names[grp]]
        in_flight[grp], started = _xchg_start(srcs, False, SAME_CORE, dep, f"gather_{grp}_start")
        return started

    zero = start_gather("l0b", part_all)
    mods["sh1_0"] = mods["sh1_0"] + zero

    small = {"a_norm_g": a_norm_g, "k_norm_g": k_norm_g.reshape(1, HEAD), "q_norm_g": q_norm_g, "kv_b_f": kv_b_f}
    for l in range(2):
        small[f"conv_w{l}"] = _pad_shard_cols(conv_w_full[l], ncw, ncp).reshape(CONV_TAPS, 2, fp).transpose(1, 0, 2)
        small[f"conv_b{l}"] = _pad_shard_cols(ffn_conv_b[l], ncw, ncp).reshape(2, 1, fp)

    def pre_w(grp, after):
        arrived = _xchg_wait(in_flight[grp], after, False, SAME_CORE, f"gather_{grp}_wait")
        if grp == "l0b":
            forwarding[grp], _ = _sibling_forward_start(arrived[:1], "gather_l0b_to_sibling_start")
            forwarding["l0b_ffn"], started = _sibling_forward_start(
                arrived[1:], "gather_l0b_ffn_to_sibling_start", after=forwarding[grp][-1])
            return started
        forwarding[grp], started = _sibling_forward_start(arrived, f"gather_{grp}_to_sibling_start")
        return started

    def get_w(grp, after):
        if grp == "l0a":
            arrived = _xchg_wait(in_flight["l0a"], after, False, SAME_CORE, "gather_l0a_wait")
            handles, _ = _sibling_forward_start(arrived, "gather_l0a_to_sibling_start")
            return {"a_in": _sibling_forward_wait(handles, after, "gather_l0a_to_sibling_wait")[0]}
        full = _sibling_forward_wait(forwarding[grp], after, f"gather_{grp}_to_sibling_wait")
        if grp == "l0b":
            started = start_gather("l1", full[0])
            full[0] = full[0] + started.astype(full[0].dtype)
        got = dict(zip(forward_names[grp], full))
        out = {}
        for n, a in got.items():
            if n in ("a_out", "b_out"):
                out[n] = a.reshape(d, d)
            elif n in ("down0", "down1"):
                dn = a.reshape(NCHIP, ff // NCHIP, d)
                out[n] = jnp.pad(dn, ((0, 0), (0, ncp - ncw), (0, 0))).reshape(fp, d)
            elif n == "kv":
                kv_t = a.reshape(NDEV * kv_w.shape[1], d)
                out["kv_k"], out["kv_v"] = kv_t[:d], kv_t[d:2 * d]
                out["kv_f"] = jnp.pad(kv_t[2 * d:], ((0, LANES - nh), (0, 0)))
            else:
                out[n] = a
        return out

    scatter_flight, g_last = {}, {}

    def put_g(grp, gr):
        if grp == "l0a":
            g_last.update(gr)
            return zero
        if grp == "l1":
            g_kvw = jnp.concatenate([gr["kv_k"], gr["kv_v"], gr["kv_f"][:nh].astype(BF16)], axis=0)
            arrs = {"kv_w": g_kvw.reshape(NDEV, kv_w.shape[1], d), "b_w_q": gr["b_q"],
                    "b_w_out": gr["b_out"].reshape(NDEV, d // NDEV, d), "up1": gr["up1"],
                    "down1": gr["down1"].reshape(NCHIP, ncp, d)[:, :ncw].reshape(NDEV, rd, d)}
        else:
            arrs = {"a_w_out": gr["a_out"].reshape(NDEV, d // NDEV, d), "up0": gr["up0"],
                    "down0": gr["down0"].reshape(NCHIP, ncp, d)[:, :ncw].reshape(NDEV, rd, d)}
        srcs = list(arrs.values())
        handles, sent = _xchg_start(srcs, True, ALL_PEERS, srcs[0], f"scatter_{grp}_start")
        scatter_flight[grp] = (list(arrs), handles)
        return sent

    loss_v, grad_x, dmods, dlb, g = _local_step(x[0], loss_target[0], mods, lb, small, pre_w, get_w, put_g)

    g_sum = {}
    for grp in ("l1", "l0b"):
        names, handles = scatter_flight[grp]
        for nm, a in zip(names, _xchg_wait(handles, grad_x, True, ALL_PEERS, f"scatter_{grp}_wait")):
            g_sum[nm] = _slab_sum(a, f"rs_slab_sum_{nm}")

    def conv_w_grad(a):
        return _unpad_shard_cols(a.transpose(1, 0, 2).reshape(CONV_TAPS, 2 * fp), ncw, ncp)

    def conv_b_grad(a):
        return _unpad_shard_cols(a.reshape(2 * fp), ncw, ncp)

    dmod_vec = [dmods[f"{nm}_{l}"] for l in range(2) for nm in mod_names] + [dmods["kv_sh"], dmods["kv_sc"]]
    post = _pack_small(dmod_vec + [dlb, g["a_norm_g"], g["k_norm_g"], g["q_norm_g"],
                                   jnp.pad(g["kv_b_f"].reshape(-1), (0, LANES - nh)),
                                   conv_w_grad(g["conv_w0"]), conv_w_grad(g["conv_w1"]),
                                   conv_b_grad(g["conv_b0"]), conv_b_grad(g["conv_b1"]), loss_v])
    post_flight, _ = _xchg_start([post], False, ALL_PEERS, post, "gather_small_grads_start")
    a_in_flight, a_in_sent = _xchg_start([g_last["a_in"]], True, ALL_PEERS, post_flight[-1], "scatter_l0a_start")
    a_in_sent = a_in_sent.reshape(1, 1)
    grads = {
        "a_w_out": g_sum["a_w_out"].reshape(a_w_out.shape),
        "kv_w": g_sum["kv_w"].T,
        "b_w_q": g_sum["b_w_q"].reshape(b_w_q.shape),
        "b_w_out": g_sum["b_w_out"].reshape(b_w_out.shape),
        "ffn_w_up": jnp.stack([g_sum["up0"][:, :ncw], g_sum["up1"][:, :ncw]]),
        "ffn_w_down": jnp.stack([g_sum["down0"], g_sum["down1"]]),
    }
    delta, new_m, new_v = {}, {}, {}

    def adamw_matrix(n):
        shp = weights[n].shape
        two_d = lambda a: a.reshape(-1, shp[-1])
        dl, mn, vn = _adamw(two_d(weights[n]), two_d(grads[n]), two_d(m_in[n]), two_d(v_in[n]), f"adamw_{n}",
                            after=a_in_sent)
        delta[n], new_m[n], new_v[n] = dl.reshape(shp), mn.reshape(shp), vn.reshape(shp)

    for n in grads:
        adamw_matrix(n)
    (post_all,) = _xchg_wait(post_flight, [new_v[n] for n in grads], False, ALL_PEERS, "gather_small_grads_wait")
    tot = _slab_sum(post_all, "small_grad_sum").reshape(-1)
    nmod = 14 * d
    (t_mod, t_lb, t_ang, t_kng, t_qng, t_bf, t_cw, t_cb, t_loss) = _unpack_small(
        tot, [(nmod,), (1, d), (1, HEAD), (HEAD,), (1, HEAD), (LANES,), (2, CONV_TAPS, two_f), (2, two_f),
              (LANES,)])
    loss = t_loss[0]
    dm_all = post_all.reshape(NDEV, -1)[:, :nmod]
    dm0 = lax.dynamic_slice_in_dim(dm_all[:, :6 * d], me * n0, n0, axis=1)
    dm1 = lax.dynamic_slice_in_dim(dm_all[:, 6 * d:12 * d], me * n0, n0, axis=1)
    dkv = lax.dynamic_slice_in_dim(dm_all[:, 12 * d:], me * nkv, nkv, axis=1)
    g_ada_w, g_kv_ada_w, g_logits = _ada_bwd(c_act, _pad_rows(dm0, 2 * NDEV), _pad_rows(dm1, 2 * NDEV),
                                              _pad_rows(dkv, 2 * NDEV), lb, t_lb)

    grads.update({
        "ada_w": g_ada_w,
        "ada_b": t_mod[:12 * d].reshape(2, 6 * d),
        "a_lb_logits": lax.dynamic_slice_in_dim(g_logits, me * HEAD, HEAD, axis=1),
        "a_norm_g": t_ang,
        "kv_ada_w": g_kv_ada_w,
        "kv_ada_b": t_mod[12 * d:],
        "kv_b_f": t_bf[:nh],
        "k_norm_g": t_kng,
        "q_norm_g": t_qng,
        "ffn_conv_w": lax.dynamic_slice_in_dim(t_cw, me * ncw, ncw, axis=2),
        "ffn_conv_b": t_cb,
    })

    small_adam = [n for n in order if n not in delta and n not in ("ada_w", "kv_ada_w", "a_w_in")]
    packs = [_pack_small([src[n] for n in small_adam]) for src in (weights, grads, m_in, v_in)]
    outs = _adamw(*packs, "adamw_small", tr=packs[0].shape[0])
    shapes = [weights[n].shape for n in small_adam]
    for dst, o in zip((delta, new_m, new_v), outs):
        for n, a in zip(small_adam, _unpack_small(o.reshape(-1), shapes)):
            dst[n] = a
    adamw_matrix("ada_w")
    adamw_matrix("kv_ada_w")
    (landed,) = _xchg_wait(a_in_flight, new_v["kv_ada_w"], True, ALL_PEERS, "scatter_l0a_wait")
    grads["a_w_in"] = _slab_sum(landed, "rs_slab_sum_a_w_in").reshape(a_w_in.shape)
    adamw_matrix("a_w_in")

    return (loss, grad_x.reshape(x.shape), *[grads[n] for n in order], *[delta[n] for n in order],
            *[new_m[n] for n in order], *[new_v[n] for n in order])
```

```python
import functools

import jax
import jax.numpy as jnp
from jax import lax
from jax.experimental import pallas as pl
from jax.experimental.pallas import tpu as pltpu

F32 = jnp.float32
BF16 = jnp.bfloat16

NDEV = 8
NCHIP = 4
HEAD = 128
A_CHUNK = 64
CONV_TAPS = 3
EPS = 1e-6
NEG_INF = -1e30
LANES = 128
VMEM_LIMIT = 48 * 1024 * 1024

ADAM_LR = 0.001
ADAM_B1 = 0.9
ADAM_B2 = 0.999
ADAM_EPS = 1e-08
ADAM_WD = 0.01
ADAM_STEP = 10

_NN = (((1,), (0,)), ((), ()))
_NT = (((1,), (1,)), ((), ()))
_TN = (((0,), (0,)), ((), ()))
_MESH = pl.DeviceIdType.MESH


def _cparams(**kw):
    return pltpu.CompilerParams(vmem_limit_bytes=VMEM_LIMIT, **kw)


def _divisor_tile(n, pref, unit=LANES):
    if n <= pref:
        return n
    best = None
    for t in range(unit, pref + 1, unit):
        if n % t == 0:
            best = t
    assert best is not None, (n, pref)
    return best


def _round_up(n, unit):
    return -(-n // unit) * unit


def _bdot_raw(a, b, dims):
    return lax.dot_general(a.astype(BF16), b.astype(BF16), dims, preferred_element_type=F32)


@jax.custom_vjp
def _dot_nn(a, b):
    return _bdot_raw(a, b, _NN)


@jax.custom_vjp
def _dot_nt(a, b):
    return _bdot_raw(a, b, _NT)


@jax.custom_vjp
def _dot_tn(a, b):
    return _bdot_raw(a, b, _TN)


_dot_nn.defvjp(lambda a, b: (_bdot_raw(a, b, _NN), (a, b)),
               lambda r, g: (_dot_nt(g, r[1]), _dot_tn(r[0], g)))
_dot_nt.defvjp(lambda a, b: (_bdot_raw(a, b, _NT), (a, b)),
               lambda r, g: (_dot_nn(g, r[1]), _dot_tn(g, r[0])))
_dot_tn.defvjp(lambda a, b: (_bdot_raw(a, b, _TN), (a, b)),
               lambda r, g: (_dot_nt(r[1], g), _dot_nn(r[0], g)))


def _f32dot(a, b):
    return lax.dot_general(a, b, _NN, precision=lax.Precision.HIGHEST, preferred_element_type=F32)


def _sigmoid(x):
    return jax.nn.sigmoid(x)


def _silu(x):
    return x * jax.nn.sigmoid(x)


def _rms(x):
    return x * lax.rsqrt(jnp.mean(x * x, axis=-1, keepdims=True) + EPS)


def _modulate(x, sh, sc):
    return _rms(x) * (1.0 + sc) + sh


def _mm_call(a, b, dims, a_spec, b_spec, o_spec, o_shape, grid, acc_tile, name):
    nk = grid[2]

    def body(a_ref, b_ref, o_ref, *acc):
        p = lax.dot_general(a_ref[...].astype(BF16), b_ref[...].astype(BF16), dims,
                            preferred_element_type=F32)
        if nk == 1:
            o_ref[...] = p.astype(o_ref.dtype)
        else:
            kk = pl.program_id(2)

            @pl.when(kk == 0)
            def _():
                acc[0][...] = p

            @pl.when(kk > 0)
            def _():
                acc[0][...] += p

            @pl.when(kk == nk - 1)
            def _():
                o_ref[...] = acc[0][...].astype(o_ref.dtype)

    return pl.pallas_call(
        body, name=name, grid=grid, in_specs=[a_spec, b_spec], out_specs=o_spec, out_shape=o_shape,
        scratch_shapes=[pltpu.VMEM(acc_tile, F32)] if nk > 1 else [],
        compiler_params=_cparams(dimension_semantics=("parallel", "parallel", "arbitrary")),
    )(a, b)


def _mm(a, b, mode, out_dtype, name, tm=1024, tn=1024, tk=2048):
    if mode == "nn":
        (m, k), (k2, n) = a.shape, b.shape
    elif mode == "nt":
        (m, k), (n, k2) = a.shape, b.shape
    else:
        (k, m), (k2, n) = a.shape, b.shape
    assert k == k2, (a.shape, b.shape, mode)
    tm, tn, tk = _divisor_tile(m, tm), _divisor_tile(n, tn), _divisor_tile(k, tk)
    if mode == "tn":
        a_spec = pl.BlockSpec((tk, tm), lambda i, j, kk: (kk, i))
    else:
        a_spec = pl.BlockSpec((tm, tk), lambda i, j, kk: (i, kk))
    if mode == "nt":
        b_spec = pl.BlockSpec((tn, tk), lambda i, j, kk: (j, kk))
    else:
        b_spec = pl.BlockSpec((tk, tn), lambda i, j, kk: (kk, j))
    return _mm_call(a, b, {"nn": _NN, "nt": _NT, "tn": _TN}[mode], a_spec, b_spec,
                    pl.BlockSpec((tm, tn), lambda i, j, kk: (i, j)), jax.ShapeDtypeStruct((m, n), out_dtype),
                    (m // tm, n // tn, k // tk), (tm, tn), name)


def _wblk_act_spec(rows, gb, nl, split, nb, row_axis, blk_axis):
    if split == 1:
        return pl.BlockSpec((rows, gb * nl), lambda *g: (g[row_axis], g[blk_axis]))
    groups = nb // split // gb
    return pl.BlockSpec((None, rows, gb * nl),
                        lambda *g: (g[blk_axis] // groups, g[row_axis], g[blk_axis] % groups))


def _mm_wblk(a, wb, out_dtype, name, *, gb, row_off=0, split=1, tm=1024):
    m, k = a.shape
    nb, _, nl = wb.shape
    assert (nb // split) % gb == 0
    tm = _divisor_tile(m, tm)

    def body(a_ref, b_ref, o_ref):
        av = a_ref[...].astype(BF16)
        for s in range(gb):
            o_ref[:, s * nl:(s + 1) * nl] = lax.dot_general(
                av, b_ref[s].astype(BF16), _NN, preferred_element_type=F32).astype(o_ref.dtype)

    o_shape = (m, nb * nl) if split == 1 else (split, m, nb // split * nl)
    return pl.pallas_call(
        body, name=name, grid=(nb // gb, m // tm),
        in_specs=[pl.BlockSpec((tm, k), lambda j, i: (i, 0)),
                  pl.BlockSpec((gb, k, nl), lambda j, i: (j, row_off, 0))],
        out_specs=_wblk_act_spec(tm, gb, nl, split, nb, 1, 0),
        out_shape=jax.ShapeDtypeStruct(o_shape, out_dtype),
        compiler_params=_cparams(dimension_semantics=("parallel", "parallel")),
    )(a, wb)


def _mm_wblk_dx(dy, wb, out_dtype, name, *, k, gb, row_off=0, split=1, tm=1024):
    nb, _, nl = wb.shape
    m = dy.shape[-2]
    tm = _divisor_tile(m, tm)
    nk = nb // gb
    per = nb // split
    whole = split > 1 and gb == nb
    assert whole or per % gb == 0

    def body(a_ref, b_ref, o_ref, *acc):
        p = None
        for s in range(gb):
            a_blk = a_ref[s // per, :, (s % per) * nl:(s % per + 1) * nl] if whole else a_ref[:, s * nl:(s + 1) * nl]
            q = lax.dot_general(a_blk.astype(BF16), b_ref[s].astype(BF16), _NT, preferred_element_type=F32)
            p = q if p is None else p + q
        if nk == 1:
            o_ref[...] = p.astype(o_ref.dtype)
        else:
            kk = pl.program_id(1)

            @pl.when(kk == 0)
            def _():
                acc[0][...] = p

            @pl.when(kk > 0)
            def _():
                acc[0][...] += p

            @pl.when(kk == nk - 1)
            def _():
                o_ref[...] = acc[0][...].astype(o_ref.dtype)

    return pl.pallas_call(
        body, name=name, grid=(m // tm, nk),
        in_specs=[pl.BlockSpec((split, tm, per * nl), lambda i, kk: (0, i, 0)) if whole
                  else _wblk_act_spec(tm, gb, nl, split, nb, 0, 1),
                  pl.BlockSpec((gb, k, nl), lambda i, kk: (kk, row_off, 0))],
        out_specs=pl.BlockSpec((tm, k), lambda i, kk: (i, 0)),
        out_shape=jax.ShapeDtypeStruct((m, k), out_dtype),
        scratch_shapes=[pltpu.VMEM((tm, k), F32)] if nk > 1 else [],
        compiler_params=_cparams(dimension_semantics=("parallel", "arbitrary")),
    )(dy, wb)


def _mm_wblk_dw(x, dy, name, *, nb, gb, split=1, tk=1024):
    t, k = x.shape
    assert (nb // split) % gb == 0
    nl = dy.shape[-1] * split // nb
    tk = _divisor_tile(t, tk)
    nk = t // tk

    def body(a_ref, b_ref, o_ref, *acc):
        kk = pl.program_id(1)
        av = a_ref[...].astype(BF16)
        for s in range(gb):
            p = lax.dot_general(av, b_ref[:, s * nl:(s + 1) * nl].astype(BF16), _TN, preferred_element_type=F32)
            if nk == 1:
                o_ref[s] = p.astype(o_ref.dtype)
                continue

            @pl.when(kk == 0)
            def _():
                acc[0][s] = p

            @pl.when(kk > 0)
            def _():
                acc[0][s] += p

        if nk > 1:
            @pl.when(kk == nk - 1)
            def _():
                o_ref[...] = acc[0][...].astype(o_ref.dtype)

    return pl.pallas_call(
        body, name=name, grid=(nb // gb, nk),
        in_specs=[pl.BlockSpec((tk, k), lambda j, kk: (kk, 0)), _wblk_act_spec(tk, gb, nl, split, nb, 1, 0)],
        out_specs=pl.BlockSpec((gb, k, nl), lambda j, kk: (j, 0, 0)),
        out_shape=jax.ShapeDtypeStruct((nb, k, nl), BF16),
        scratch_shapes=[pltpu.VMEM((gb, k, nl), F32)] if nk > 1 else [],
        compiler_params=_cparams(dimension_semantics=("parallel", "arbitrary")),
    )(x, dy)


def _row_specs(rows, tb, nsub):
    return [pl.BlockSpec((tb, nsub * cw), functools.partial(lambda i, off: (i, off), off=off))
            for (_, cw, off) in rows]


def _vec_specs(params):
    return [pl.BlockSpec(p.shape, lambda i: (0, 0)) for p in params]


def _row_fwd(f, rows, params, out_dtypes, *, nsub=1, tb, name):
    t = rows[0][0].shape[0]
    tb = min(tb, t)
    n_r, n_p = len(rows), len(params)
    blk = [jax.ShapeDtypeStruct((tb, cw), F32) for (_, cw, _) in rows]
    blk += [jax.ShapeDtypeStruct(p.shape, F32) for p in params]
    out_avals = jax.eval_shape(f, *blk)

    def body(*refs):
        pv = [r[...] for r in refs[n_r:n_r + n_p]]
        for s in range(nsub):
            vals = [r[:, s * cw:(s + 1) * cw].astype(F32) for r, (_, cw, _) in zip(refs[:n_r], rows)]
            outs = f(*vals, *pv)
            for o_ref, o in zip(refs[n_r + n_p:], outs):
                w = o.shape[1]
                o_ref[:, s * w:(s + 1) * w] = o.astype(o_ref.dtype)

    return pl.pallas_call(
        body, name=name,
        grid=(t // tb,),
        in_specs=_row_specs(rows, tb, nsub) + _vec_specs(params),
        out_specs=[pl.BlockSpec((tb, nsub * av.shape[1]), lambda i: (i, 0)) for av in out_avals],
        out_shape=[jax.ShapeDtypeStruct((t, nsub * av.shape[1]), dt) for av, dt in zip(out_avals, out_dtypes)],
        compiler_params=_cparams(dimension_semantics=("parallel",)),
    )(*[r[0] for r in rows], *params)


def _row_bwd(f, rows, params, cots, row_grad_dtypes, *, nsub=1, tb, name, add_to=None, cot_add=None):
    t = rows[0][0].shape[0]
    tb = min(tb, t)
    n_r, n_p, n_c = len(rows), len(params), len(cots)
    want = [j for j in range(n_r) if row_grad_dtypes[j] is not None]
    cot_add = cot_add or []
    extra = [] if add_to is None else [(add_to[1], rows[add_to[0]][1], 0)]
    n_add_to = len(extra)
    extra += [(arr, cots[ci][1], 0) for ci, arr in cot_add]

    def body(*refs):
        i = pl.program_id(0)
        r_in, p_in = refs[:n_r], refs[n_r:n_r + n_p]
        c_in = refs[n_r + n_p:n_r + n_p + n_c]
        e_in = refs[n_r + n_p + n_c:n_r + n_p + n_c + len(extra)]
        outs = refs[n_r + n_p + n_c + len(extra):]
        pv = [r[...] for r in p_in]
        psum = [None] * n_p
        for s in range(nsub):
            vals = [r[:, s * cw:(s + 1) * cw].astype(F32) for r, (_, cw, _) in zip(r_in, rows)]
            cvals = [r[:, s * cw:(s + 1) * cw].astype(F32) for r, (_, cw, _) in zip(c_in, cots)]
            for (ci, _), e_ref in zip(cot_add, e_in[n_add_to:]):
                cw = cots[ci][1]
                cvals[ci] = cvals[ci] + e_ref[:, s * cw:(s + 1) * cw].astype(F32)
            _, vjp_fn = jax.vjp(f, *vals, *pv)
            grads = vjp_fn(tuple(cvals))
            for o_ref, jr in zip(outs[:len(want)], want):
                cw = rows[jr][1]
                gr = grads[jr]
                if add_to is not None and jr == add_to[0]:
                    gr = gr + e_in[0][:, s * cw:(s + 1) * cw]
                o_ref[:, s * cw:(s + 1) * cw] = gr.astype(o_ref.dtype)
            for jp in range(n_p):
                psum[jp] = grads[n_r + jp] if psum[jp] is None else psum[jp] + grads[n_r + jp]
        for o_ref, g in zip(outs[len(want):], psum):
            @pl.when(i == 0)
            def _():
                o_ref[...] = g

            @pl.when(i > 0)
            def _():
                o_ref[...] += g

    out_specs = [pl.BlockSpec((tb, nsub * rows[jr][1]), lambda i: (i, 0)) for jr in want]
    out_shape = [jax.ShapeDtypeStruct((t, nsub * rows[jr][1]), row_grad_dtypes[jr]) for jr in want]
    out_specs += _vec_specs(params)
    out_shape += [jax.ShapeDtypeStruct(p.shape, F32) for p in params]
    res = pl.pallas_call(
        body, name=name,
        grid=(t // tb,),
        in_specs=_row_specs(rows, tb, nsub) + _vec_specs(params) + _row_specs(cots, tb, nsub)
        + _row_specs(extra, tb, nsub),
        out_specs=out_specs, out_shape=out_shape,
        compiler_params=_cparams(dimension_semantics=("arbitrary",)),
    )(*[r[0] for r in rows], *params, *[c[0] for c in cots], *[e[0] for e in extra])
    return res[:len(want)], res[len(want):]


def _f_mod(x, sh, sc):
    return (_modulate(x, sh, sc),)


def _f_res_mod(x, y, g, sh, sc):
    x1 = x + g * y
    return x1, _modulate(x1, sh, sc)


def _f_res_mod2(x, y, g, sh_a, sc_a, sh_b, sc_b):
    x1 = x + g * y
    return x1, _modulate(x1, sh_a, sc_a), _modulate(x1, sh_b, sc_b)


def _f_qnorm(p, g):
    return (_rms(p) * g * (HEAD ** -0.5),)


def _f_knorm(p, g):
    return (_rms(p) * g,)


def _f_qnorm_aug(p, g):
    lane = lax.broadcasted_iota(jnp.int32, p.shape, 1)
    return (jnp.concatenate([_rms(p) * g * (HEAD ** -0.5), jnp.where(lane < 3, 1.0, 0.0)], axis=1),)


def _f_knorm_aug(p, c0, c1, c2, g):
    lane = lax.broadcasted_iota(jnp.int32, p.shape, 1)
    aug = jnp.where(lane == 0, c0, jnp.where(lane == 1, c1, jnp.where(lane == 2, c2, 0.0)))
    return (jnp.concatenate([_rms(p) * g, aug], axis=1),)


def _split3(a):
    round_bf16 = lambda v: lax.reduce_precision(v, exponent_bits=8, mantissa_bits=7)
    hi = round_bf16(a)
    mid = round_bf16(a - hi)
    lo = round_bf16(a - hi - mid)
    return hi.astype(BF16), mid.astype(BF16), lo.astype(BF16)


def _f_outgate(o, og):
    return (o * _sigmoid(og),)


def _loss_call(x3, f, g2, target, tb):
    t, d = x3.shape
    tb = min(tb, t)

    def body(x_ref, f_ref, g_ref, t_ref, loss_ref, dx_ref, df_ref, dg_ref):
        i = pl.program_id(0)
        fv = f_ref[...]
        g = g_ref[...]
        e = x_ref[...] + g * fv - t_ref[...]
        dx = e * (1.0 / d)
        part = 0.5 * jnp.sum(jnp.sum(e * dx, axis=1, keepdims=True), axis=0, keepdims=True)
        dx_ref[...] = dx
        df_ref[...] = (g * dx).astype(df_ref.dtype)
        dg = jnp.sum(dx * fv, axis=0, keepdims=True)

        @pl.when(i == 0)
        def _():
            loss_ref[...] = jnp.broadcast_to(part, loss_ref.shape)
            dg_ref[...] = dg

        @pl.when(i > 0)
        def _():
            loss_ref[...] += jnp.broadcast_to(part, loss_ref.shape)
            dg_ref[...] += dg

    row = pl.BlockSpec((tb, d), lambda i: (i, 0))
    vec = pl.BlockSpec((1, d), lambda i: (0, 0))
    return pl.pallas_call(
        body, name="loss_head",
        grid=(t // tb,),
        in_specs=[row, row, vec, row],
        out_specs=[pl.BlockSpec((1, LANES), lambda i: (0, 0)), row, row, vec],
        out_shape=[jax.ShapeDtypeStruct((1, LANES), F32), jax.ShapeDtypeStruct((t, d), F32),
                   jax.ShapeDtypeStruct((t, d), BF16), jax.ShapeDtypeStruct((1, d), F32)],
        compiler_params=_cparams(dimension_semantics=("arbitrary",)),
    )(x3, f, g2, target)


def _hg_mask(tb):
    br = lax.broadcasted_iota(jnp.int32, (tb, tb), 0)
    bs = lax.broadcasted_iota(jnp.int32, (tb, tb), 1)
    return jnp.logical_and(br // A_CHUNK == bs // A_CHUNK, bs <= br).astype(F32)


def _hg_consts(mask):
    c = A_CHUNK
    r = lax.broadcasted_iota(jnp.int32, (c, c), 0)
    s = lax.broadcasted_iota(jnp.int32, (c, c), 1)
    return (s <= r).astype(F32), (r <= s).astype(F32), mask > 0.5


def _chunk_apply(mat, x):
    c = mat.shape[0]
    return jnp.concatenate([_f32dot(mat, x[i * c:(i + 1) * c]) for i in range(x.shape[0] // c)], axis=0)


@jax.custom_vjp
def _chunk_cumsum(x, tri, tri_t):
    return _chunk_apply(tri, x)


_chunk_cumsum.defvjp(lambda x, tri, tri_t: (_chunk_apply(tri, x), (tri, tri_t)),
                     lambda r, g: (_chunk_apply(r[1], g), jnp.zeros_like(r[0]), jnp.zeros_like(r[1])))


def _per_chunk(a, b, dims):
    return jnp.stack([_bdot_raw(a[i], b[i], dims) for i in range(a.shape[0])])


@jax.custom_vjp
def _chunk_tn(a, b):
    return _per_chunk(a, b, _TN)


@jax.custom_vjp
def _chunk_nt(a, b):
    return _per_chunk(a, b, _NT)


@jax.custom_vjp
def _chunk_nn(a, b):
    return _per_chunk(a, b, _NN)


_chunk_tn.defvjp(lambda a, b: (_per_chunk(a, b, _TN), (a, b)),
                 lambda r, g: (_chunk_nt(r[1], g), _chunk_nn(r[0], g)))
_chunk_nt.defvjp(lambda a, b: (_per_chunk(a, b, _NT), (a, b)),
                 lambda r, g: (_chunk_nn(g, r[1]), _chunk_tn(g, r[0])))
_chunk_nn.defvjp(lambda a, b: (_per_chunk(a, b, _NN), (a, b)),
                 lambda r, g: (_chunk_nt(g, r[1]), _chunk_tn(r[0], g)))


def _scan_states(decay, m, st):
    sts = []
    for i in range(m.shape[0]):
        sts.append(st)
        st = st * decay[i] + m[i]
    return jnp.stack(sts), st


@jax.custom_vjp
def _state_scan(decay, m, st):
    return _scan_states(decay, m, st)


def _state_scan_fwd(decay, m, st):
    sts, st_out = _scan_states(decay, m, st)
    return (sts, st_out), (decay, sts)


def _state_scan_bwd(res, cts):
    decay, sts = res
    d_sts, g = cts
    d_decay, d_m = [], []
    for i in range(sts.shape[0] - 1, -1, -1):
        d_m.append(g)
        d_decay.append(jnp.sum(g * sts[i], axis=0, keepdims=True))
        g = g * decay[i] + d_sts[i]
    return jnp.stack(d_decay[::-1]), jnp.stack(d_m[::-1]), g


_state_scan.defvjp(_state_scan_fwd, _state_scan_bwd)


def _hg_block(qp, fp, ip, gp, lb, ng, st, tri, tri_t, bd_causal):
    tb = qp.shape[0]
    c = A_CHUNK
    n = tb // c
    q = _silu(qp)
    fg = lb + (1.0 - lb) * _sigmoid(fp)
    logf = jnp.log(fg)
    k = 1.0 - fg
    b3 = _chunk_cumsum(logf, tri, tri_t).reshape(n, c, HEAD)
    pos = lax.broadcasted_iota(jnp.int32, (1, c, 1), 1)
    b_mid = lax.stop_gradient(jnp.sum(jnp.where(pos == c // 2, b3, 0.0), axis=1, keepdims=True))
    b_last = jnp.sum(jnp.where(pos == c - 1, b3, 0.0), axis=1, keepdims=True)
    q3, k3, v3 = q.reshape(n, c, HEAD), k.reshape(n, c, HEAD), ip.reshape(n, c, HEAD)
    scores = _dot_nt((q3 * jnp.exp(b3 - b_mid)).reshape(tb, HEAD), (k3 * jnp.exp(b_mid - b3)).reshape(tb, HEAD))
    o_intra = _dot_nn(jnp.where(bd_causal, scores, 0.0), ip)
    states, st_new = _state_scan(jnp.exp(b_last), _chunk_tn(v3, k3 * jnp.exp(b_last - b3)), st)
    o = o_intra + _chunk_nt(q3 * jnp.exp(b3), states).reshape(tb, HEAD)
    y = _rms(o) * ng * _silu(gp)
    return y, st_new


HG_HEADS = 2


def _hg_specs(tb, nh, rev_nb=None):
    wide = HG_HEADS * HEAD
    per = nh // HG_HEADS

    def row(part):
        if rev_nb is None:
            return pl.BlockSpec((tb, wide), functools.partial(lambda h, i, off: (i, off + h), off=part * per))
        return pl.BlockSpec((tb, wide),
                            functools.partial(lambda h, i, off: (rev_nb - 1 - i, off + h), off=part * per))
    return [row(0), row(1), row(2), row(3),
            pl.BlockSpec((1, wide), lambda h, i: (0, h)), pl.BlockSpec((1, HEAD), lambda h, i: (0, 0)),
            pl.BlockSpec((tb, tb), lambda h, i: (0, 0))]


def _hgrn2_fwd(proj, lb, ng, tb):
    t = proj.shape[0]
    nh = proj.shape[1] // (4 * HEAD)
    tb = min(tb, t)
    nb = t // tb
    wide = HG_HEADS * HEAD

    def body(q_ref, f_ref, i_ref, g_ref, lb_ref, ng_ref, mask_ref, y_ref, s_ref, st_ref):
        i = pl.program_id(1)

        @pl.when(i == 0)
        def _():
            st_ref[...] = jnp.zeros_like(st_ref)

        consts = _hg_consts(mask_ref[...])
        for p in range(HG_HEADS):
            cs = slice(p * HEAD, (p + 1) * HEAD)
            st = st_ref[p]
            s_ref[p, 0] = st
            y, st_new = _hg_block(q_ref[:, cs], f_ref[:, cs], i_ref[:, cs], g_ref[:, cs], lb_ref[:, cs],
                                  ng_ref[...], st, *consts)
            y_ref[:, cs] = y.astype(y_ref.dtype)
            st_ref[p] = st_new

    return pl.pallas_call(
        body, name="hgrn2_fwd",
        grid=(nh // HG_HEADS, nb),
        in_specs=_hg_specs(tb, nh),
        out_specs=[pl.BlockSpec((tb, wide), lambda h, i: (i, h)),
                   pl.BlockSpec((HG_HEADS, 1, HEAD, HEAD), lambda h, i: (h, i, 0, 0))],
        out_shape=[jax.ShapeDtypeStruct((t, nh * HEAD), BF16),
                   jax.ShapeDtypeStruct((nh, nb, HEAD, HEAD), F32)],
        scratch_shapes=[pltpu.VMEM((HG_HEADS, HEAD, HEAD), F32)],
        compiler_params=_cparams(dimension_semantics=("parallel", "arbitrary")),
    )(proj, proj, proj, proj, lb, ng, _hg_mask(tb))


def _hgrn2_bwd(proj, lb, ng, states, dy, tb):
    t = proj.shape[0]
    nh = proj.shape[1] // (4 * HEAD)
    tb = min(tb, t)
    nb = t // tb
    wide = HG_HEADS * HEAD

    def body(q_ref, f_ref, i_ref, g_ref, lb_ref, ng_ref, mask_ref, s_ref, dy_ref,
             dp_ref, dlb_ref, dng_ref, dst_ref):
        h, i = pl.program_id(0), pl.program_id(1)
        consts = _hg_consts(mask_ref[...])

        @pl.when(i == 0)
        def _():
            dst_ref[...] = jnp.zeros_like(dst_ref)
            dlb_ref[...] = jnp.zeros_like(dlb_ref)

        @pl.when(jnp.logical_and(i == 0, h == 0))
        def _():
            dng_ref[...] = jnp.zeros_like(dng_ref)

        def fn(qp, fp, ip, gp, lbx, ngx, stx):
            return _hg_block(qp, fp, ip, gp, lbx, ngx, stx, *consts)

        for p in range(HG_HEADS):
            cs = slice(p * HEAD, (p + 1) * HEAD)
            _, vjp_fn = jax.vjp(fn, q_ref[:, cs], f_ref[:, cs], i_ref[:, cs], g_ref[:, cs], lb_ref[:, cs],
                                ng_ref[...], s_ref[p, 0])
            *gparts, glb, gng, dst = vjp_fn((dy_ref[:, cs].astype(F32), dst_ref[p]))
            for part, gpart in enumerate(gparts):
                dp_ref[part, :, cs] = gpart.astype(dp_ref.dtype)
            dst_ref[p] = dst
            dlb_ref[:, cs] += glb
            dng_ref[...] += gng

    rev = lambda h, i: (nb - 1 - i, h)
    return pl.pallas_call(
        body, name="hgrn2_bwd",
        grid=(nh // HG_HEADS, nb),
        in_specs=_hg_specs(tb, nh, rev_nb=nb) + [
            pl.BlockSpec((HG_HEADS, 1, HEAD, HEAD), lambda h, i: (h, nb - 1 - i, 0, 0)),
            pl.BlockSpec((tb, wide), rev)],
        out_specs=[pl.BlockSpec((4, tb, wide), lambda h, i: (0, nb - 1 - i, h)),
                   pl.BlockSpec((1, wide), lambda h, i: (0, h)), pl.BlockSpec((1, HEAD), lambda h, i: (0, 0))],
        out_shape=[jax.ShapeDtypeStruct((4, t, nh * HEAD), BF16),
                   jax.ShapeDtypeStruct((1, nh * HEAD), F32), jax.ShapeDtypeStruct((1, HEAD), F32)],
        scratch_shapes=[pltpu.VMEM((HG_HEADS, HEAD, HEAD), F32)],
        compiler_params=_cparams(dimension_semantics=("arbitrary", "arbitrary")),
    )(proj, proj, proj, proj, lb, ng, _hg_mask(tb), states, dy)


def _fgate_consts(cb):
    r = lax.broadcasted_iota(jnp.int32, (cb, cb), 0)
    s = lax.broadcasted_iota(jnp.int32, (cb, cb), 1)
    return (r <= s).astype(F32), (r >= s).astype(F32)


def _fgate_fwd(xt, bias, cb=512):
    nh, t = xt.shape
    cb = min(cb, t)

    def body(x_ref, b_ref, o_ref):
        upper, _ = _fgate_consts(cb)
        carry = jnp.zeros((nh, 1), F32)
        for blk in range(t // cb):
            z = x_ref[:, blk * cb:(blk + 1) * cb] + b_ref[...]
            logf = jnp.minimum(z, 0.0) - jnp.log(1.0 + jnp.exp(-jnp.abs(z)))
            cs = _f32dot(logf, upper) + carry
            o_ref[:, blk * cb:(blk + 1) * cb] = cs
            carry = cs[:, cb - 1:cb]

    vm = pl.BlockSpec(memory_space=pltpu.VMEM)
    return pl.pallas_call(
        body, name="fgate_fwd", in_specs=[vm, vm], out_specs=vm,
        out_shape=jax.ShapeDtypeStruct((nh, t), F32), compiler_params=_cparams(),
    )(xt, bias)


def _fgate_bwd(xt, bias, dft, cb=512):
    nh, t = xt.shape
    cb = min(cb, t)
    nblk = t // cb

    def body(x_ref, b_ref, d_ref, dx_ref, db_ref):
        _, lower = _fgate_consts(cb)
        carry = jnp.zeros((nh, 1), F32)
        db = jnp.zeros((nh, 1), F32)
        for blk in range(nblk - 1, -1, -1):
            sl = slice(blk * cb, (blk + 1) * cb)
            dlogf = _f32dot(d_ref[:, sl], lower) + carry
            carry = dlogf[:, 0:1]
            z = x_ref[:, sl] + b_ref[...]
            dz = dlogf * (1.0 - _sigmoid(z))
            dx_ref[:, sl] = dz
            db = db + jnp.sum(dz, axis=1, keepdims=True)
        db_ref[...] = db

    vm = pl.BlockSpec(memory_space=pltpu.VMEM)
    return pl.pallas_call(
        body, name="fgate_bwd", in_specs=[vm, vm, vm], out_specs=[vm, vm],
        out_shape=[jax.ShapeDtypeStruct((nh, t), F32), jax.ShapeDtypeStruct((nh, 1), F32)],
        compiler_params=_cparams(),
    )(xt, bias, dft)


ATTN_GROUPS = 4
ATTN_FWD_HEADS = 2


def _attn_fwd(q, k, v, f_grp, blk):
    t, width = v.shape
    nh = width // HEAD
    nq = t // blk
    hpg = nh // ATTN_GROUPS

    def body(q_ref, k_ref, v_ref, fc_ref, o_ref, lse_ref):
        i = pl.program_id(0)
        tri = (lax.broadcasted_iota(jnp.int32, (blk, blk), 1) <= lax.broadcasted_iota(jnp.int32, (blk, blk), 0))
        for h0 in range(0, nh, ATTN_FWD_HEADS):
            heads = range(h0, min(h0 + ATTN_FWD_HEADS, nh))

            def tile(j, carries, masked):
                rs = pl.ds(pl.multiple_of(j * blk, blk), blk)
                out = []
                for h, (m, l, acc) in zip(heads, carries):
                    cs = slice(h * HEAD, (h + 1) * HEAD)
                    cs2 = slice(2 * h * HEAD, 2 * (h + 1) * HEAD)
                    s = _bdot_raw(q_ref[:, cs2], k_ref[rs, cs2], _NT)
                    if masked:
                        s = jnp.where(tri, s, NEG_INF)
                    m_new = jnp.maximum(m, jnp.max(s, axis=1, keepdims=True))
                    p = jnp.exp(s - m_new)
                    alpha = jnp.exp(m - m_new)
                    l_new = alpha * l + jnp.sum(p, axis=1, keepdims=True)
                    out.append((m_new, l_new, alpha * acc + _bdot_raw(p, v_ref[rs, cs], _NN)))
                return tuple(out)

            init = tuple((jnp.full((blk, 1), NEG_INF, F32), jnp.zeros((blk, 1), F32), jnp.zeros((blk, HEAD), F32))
                         for _ in heads)
            carries = lax.fori_loop(0, i, lambda j, c: tile(j, c, False), init)
            for h, (m, l, acc) in zip(heads, tile(i, carries, True)):
                o_ref[:, h * HEAD:(h + 1) * HEAD] = acc / l
                g, hh = divmod(h, hpg)
                lse_ref[g, :, hh:hh + 1] = m + jnp.log(l) + fc_ref[g, :, hh:hh + 1]

    vm = pl.BlockSpec(memory_space=pltpu.VMEM)
    stat = pl.BlockSpec((ATTN_GROUPS, blk, hpg), lambda i: (0, i, 0))
    return pl.pallas_call(
        body, name="fox_attn_fwd",
        grid=(nq,),
        in_specs=[pl.BlockSpec((blk, 2 * width), lambda i: (i, 0)), vm, vm, stat],
        out_specs=[pl.BlockSpec((blk, width), lambda i: (i, 0)), stat],
        out_shape=[jax.ShapeDtypeStruct((t, width), F32), jax.ShapeDtypeStruct((ATTN_GROUPS, t, hpg), F32)],
        compiler_params=_cparams(dimension_semantics=("parallel",)),
    )(q, k, v, f_grp)


def _outgate_bwd(o, proj_q, dz, tb):
    t, width = o.shape
    nh = width // HEAD
    hpg = nh // ATTN_GROUPS
    tb = min(tb, t)

    def body(o_ref, og_ref, dz_ref, do_ref, dog_ref, dl_ref):
        for h in range(nh):
            cs = slice(h * HEAD, (h + 1) * HEAD)
            ov = o_ref[:, cs]
            _, vjp_fn = jax.vjp(_f_outgate, ov, og_ref[:, cs])
            do, dog = vjp_fn((dz_ref[:, cs].astype(F32),))
            do = do.astype(do_ref.dtype)
            do_ref[:, cs] = do
            dog_ref[:, cs] = dog.astype(dog_ref.dtype)
            g, hh = divmod(h, hpg)
            dl_ref[g, :, hh:hh + 1] = jnp.sum(do.astype(F32) * ov, axis=1, keepdims=True)

    wide = pl.BlockSpec((tb, width), lambda i: (i, 0))
    return pl.pallas_call(body, name="out_gate_bwd", grid=(t // tb,),
                          in_specs=[wide, pl.BlockSpec((tb, width), lambda i: (i, 1)), wide],
                          out_specs=[wide, wide, pl.BlockSpec((ATTN_GROUPS, tb, hpg), lambda i: (0, i, 0))],
                          out_shape=[jax.ShapeDtypeStruct((t, width), BF16), jax.ShapeDtypeStruct((t, width), BF16),
                                     jax.ShapeDtypeStruct((ATTN_GROUPS, t, hpg), F32)],
                          compiler_params=_cparams(dimension_semantics=("parallel",)))(o, proj_q, dz)


def _attn_bwd(q, k, v, f_grp, do, lse, delta, blk):
    t, width = v.shape
    nh = width // HEAD
    nq = t // blk
    hpg = nh // ATTN_GROUPS
    gw = hpg * HEAD

    def body(q_ref, do_ref, k_ref, v_ref, fc_ref, lse_ref, dl_ref,
             dq_ref, dk_ref, dv_ref, dfc_ref, dfr_ref):
        g, j = pl.program_id(0), pl.program_id(1)
        tri = (lax.broadcasted_iota(jnp.int32, (blk, blk), 1) <= lax.broadcasted_iota(jnp.int32, (blk, blk), 0))

        @pl.when(j == 0)
        def _():
            dq_ref[...] = jnp.zeros_like(dq_ref)
            dfc_ref[...] = jnp.zeros_like(dfc_ref)

        def tile(i, carries, masked):
            rs = pl.ds(pl.multiple_of(i * blk, blk), blk)
            out = []
            for h, (dk, dv, dfs) in enumerate(carries):
                cs = slice(h * HEAD, (h + 1) * HEAD)
                cs2 = slice(2 * h * HEAD, 2 * (h + 1) * HEAD)
                csq = slice(2 * h * HEAD, (2 * h + 1) * HEAD)
                qi = q_ref[rs, csq]
                doi = do_ref[rs, cs]
                bias = fc_ref[0, rs, h:h + 1] - lse_ref[0, rs, h:h + 1]
                p = jnp.exp(_bdot_raw(q_ref[rs, cs2], k_ref[:, cs2], _NT) + bias)
                if masked:
                    p = jnp.where(tri, p, 0.0)
                ds = p * (_bdot_raw(doi, v_ref[:, cs], _NT) - dl_ref[0, rs, h:h + 1])
                dsb = ds.astype(BF16)
                dq_ref[rs, cs] += _bdot_raw(dsb, k_ref[:, csq], _NN)
                dfc_ref[0, rs, h:h + 1] += jnp.sum(ds, axis=1, keepdims=True)
                out.append((dk + _bdot_raw(dsb, qi, _TN), dv + _bdot_raw(p, doi, _TN),
                            dfs - jnp.sum(ds, axis=0, keepdims=True)))
            return tuple(out)

        init = tuple((jnp.zeros((blk, HEAD), F32), jnp.zeros((blk, HEAD), F32), jnp.zeros((1, blk), F32))
                     for _ in range(hpg))
        carries = lax.fori_loop(j + 1, nq, lambda i, c: tile(i, c, False), tile(j, init, True))
        for h, (dk, dv, dfs) in enumerate(carries):
            cs = slice(h * HEAD, (h + 1) * HEAD)
            dk_ref[:, cs] = dk
            dv_ref[:, cs] = dv.astype(dv_ref.dtype)
            dfr_ref[0, 0, h:h + 1, :] = dfs

    once = pl.Buffered(1)
    stat = pl.BlockSpec((1, t, hpg), lambda g, j: (g, 0, 0), pipeline_mode=once)
    kv_blk = pl.BlockSpec((blk, gw), lambda g, j: (j, g))
    frow = pl.BlockSpec((1, 1, hpg, blk), lambda g, j: (g, j, 0, 0))
    dq, dk, dv, dfc, dfr = pl.pallas_call(
        body, name="fox_attn_bwd",
        grid=(ATTN_GROUPS, nq),
        in_specs=[pl.BlockSpec((t, 2 * gw), lambda g, j: (0, g), pipeline_mode=once),
                  pl.BlockSpec((t, gw), lambda g, j: (0, g), pipeline_mode=once),
                  pl.BlockSpec((blk, 2 * gw), lambda g, j: (j, g)), kv_blk, stat, stat, stat],
        out_specs=[pl.BlockSpec((t, gw), lambda g, j: (0, g)), kv_blk, kv_blk,
                   pl.BlockSpec((1, t, hpg), lambda g, j: (g, 0, 0)), frow],
        out_shape=[jax.ShapeDtypeStruct((t, width), F32), jax.ShapeDtypeStruct((t, width), F32),
                   jax.ShapeDtypeStruct((t, width), BF16), jax.ShapeDtypeStruct((ATTN_GROUPS, t, hpg), F32),
                   jax.ShapeDtypeStruct((ATTN_GROUPS, nq, hpg, blk), F32)],
        compiler_params=_cparams(dimension_semantics=("parallel", "arbitrary")),
    )(q, do, k, v, f_grp, lse, delta)
    return dq, dk, dv, dfc, dfr


SUBLANES = 8


def _shift_down(u, n):
    r = pltpu.roll(u, n, 0)
    row = lax.broadcasted_iota(jnp.int32, (SUBLANES, u.shape[1]), 0)
    return jnp.concatenate([jnp.where(row < n, 0.0, r[:SUBLANES]), r[SUBLANES:]], axis=0)


def _shift_up(u, n):
    t = u.shape[0]
    r = pltpu.roll(u, t - n, 0)
    row = lax.broadcasted_iota(jnp.int32, (SUBLANES, u.shape[1]), 0)
    return jnp.concatenate([r[:t - SUBLANES], jnp.where(row >= SUBLANES - n, 0.0, r[t - SUBLANES:])], axis=0)


def _convglu_specs(t):
    return [pl.BlockSpec((2, t, LANES), lambda j: (0, 0, j)),
            pl.BlockSpec((2, CONV_TAPS, LANES), lambda j: (0, 0, j)),
            pl.BlockSpec((2, 1, LANES), lambda j: (0, 0, j))]


def _convglu_fwd(u, cw, cb):
    _, t, fp = u.shape

    def body(u_ref, w_ref, b_ref, a_ref, c_ref):
        c = []
        for hf in range(2):
            uv, w = u_ref[hf].astype(F32), w_ref[hf]
            c.append(w[0:1] * _shift_down(uv, 2) + w[1:2] * _shift_down(uv, 1) + w[2:3] * uv + b_ref[hf])
            c_ref[hf] = c[hf].astype(c_ref.dtype)
        a_ref[...] = (_silu(c[0]) * c[1]).astype(a_ref.dtype)

    return pl.pallas_call(
        body, name="convglu_fwd",
        grid=(fp // LANES,),
        in_specs=_convglu_specs(t),
        out_specs=[pl.BlockSpec((t, LANES), lambda j: (0, j)), pl.BlockSpec((2, t, LANES), lambda j: (0, 0, j))],
        out_shape=[jax.ShapeDtypeStruct((t, fp), BF16), jax.ShapeDtypeStruct((2, t, fp), BF16)],
        compiler_params=_cparams(dimension_semantics=("parallel",)),
    )(u, cw, cb)


def _convglu_bwd(u, c, cw, da):
    _, t, fp = u.shape

    def body(u_ref, c_ref, w_ref, da_ref, du_ref, dw_ref, db_ref):
        gc, vc = c_ref[0].astype(F32), c_ref[1].astype(F32)
        sg = _sigmoid(gc)
        dav = da_ref[...].astype(F32)
        dcs = [dav * vc * (sg * (1.0 + gc * (1.0 - sg))), dav * (gc * sg)]
        for hf in range(2):
            dc, w, uv = dcs[hf], w_ref[hf], u_ref[hf].astype(F32)
            dc1, dc2 = _shift_up(dc, 1), _shift_up(dc, 2)
            du_ref[hf] = (w[2:3] * dc + w[1:2] * dc1 + w[0:1] * dc2).astype(du_ref.dtype)
            dw_ref[hf, 0:1, :] = jnp.sum(dc2 * uv, axis=0, keepdims=True)
            dw_ref[hf, 1:2, :] = jnp.sum(dc1 * uv, axis=0, keepdims=True)
            dw_ref[hf, 2:3, :] = jnp.sum(dc * uv, axis=0, keepdims=True)
            db_ref[hf] = jnp.sum(dc, axis=0, keepdims=True)

    pair, taps, bias = _convglu_specs(t)
    return pl.pallas_call(
        body, name="convglu_bwd",
        grid=(fp // LANES,),
        in_specs=[pair, pair, taps, pl.BlockSpec((t, LANES), lambda j: (0, j))],
        out_specs=[pair, taps, bias],
        out_shape=[jax.ShapeDtypeStruct((2, t, fp), BF16), jax.ShapeDtypeStruct((2, CONV_TAPS, fp), F32),
                   jax.ShapeDtypeStruct((2, 1, fp), F32)],
        compiler_params=_cparams(dimension_semantics=("parallel",)),
    )(u, c, cw, da)


def _local_step(x, target, mods, lb, small, pre_w, get_w, put_g, *, tb=512, attn_blk=512):
    t, d = x.shape
    nh = d // HEAD
    nb = NDEV
    wts = {}
    vec = lambda *names: [mods[n] for n in names]

    def ffn_fwd(h2, l):
        u = _mm_wblk(h2, wts[f"up{l}"], BF16, f"ffn{l}_up", gb=nb // 2, split=2, tm=512)
        a, c = _convglu_fwd(u, small[f"conv_w{l}"], small[f"conv_b{l}"])
        f = _mm(a, wts[f"down{l}"], "nn", F32, f"ffn{l}_down", tk=4096)
        return (u, c), a, f

    def ffn_bwd(df, h2, uc, a, l):
        u, c = uc
        da = _mm(df, wts[f"down{l}"], "nt", BF16, f"ffn{l}_down_dx", tn=1536)
        dwd = _mm(a, df, "tn", BF16, f"ffn{l}_down_dw", tm=768, tk=t)
        du, dcw, dcb = _convglu_bwd(u, c, small[f"conv_w{l}"], da)
        dh2 = _mm_wblk_dx(du, wts[f"up{l}"], BF16, f"ffn{l}_up_dx", k=d, gb=nb // 2, split=2, tm=1024)
        dwu = _mm_wblk_dw(h2, du, f"ffn{l}_up_dw", nb=nb, gb=1, split=2, tk=t)
        return dh2, dwu, dwd, dcw, dcb

    (h_a,) = _row_fwd(_f_mod, [(x, d, 0)], vec("sh1_0", "sc1_0"), [BF16], tb=tb, name="l0_mod1")
    wts.update(get_w("l0a", h_a))
    proj_a = _mm_wblk(h_a, wts["a_in"], F32, "a_in", gb=nb // 2)
    ypre, states = _hgrn2_fwd(proj_a, lb, small["a_norm_g"], tb)
    pre_w("l0b", ypre)
    wts.update(get_w("l0b", ypre))
    y_a = _mm(ypre, wts["a_out"], "nn", F32, "a_out")
    x1, h2_0 = _row_fwd(_f_res_mod, [(x, d, 0), (y_a, d, 0)], vec("g1_0", "sh2_0", "sc2_0"), [F32, BF16],
                        tb=tb, name="l0_res_mod2")
    wts.update(get_w("l0b_ffn", h2_0))
    u0, a0, f0 = ffn_fwd(h2_0, 0)
    x2, h_kv, h_q = _row_fwd(_f_res_mod2, [(x1, d, 0), (f0, d, 0)],
                             [mods["g2_0"] + pre_w("l1", f0)] + vec("kv_sh", "kv_sc", "sh1_1", "sc1_1"),
                             [F32, BF16, BF16], tb=tb, name="l0_res_kvmod_qmod")
    wts.update(get_w("l1", h_kv))
    proj_k = _mm(h_kv, wts["kv_k"], "nt", F32, "k_proj")
    v_b = _mm(h_kv, wts["kv_v"], "nt", BF16, "v_proj")
    proj_f = _mm(h_kv, wts["kv_f"], "nt", F32, "kv_fproj")
    f_logit_t = proj_f[:, :nh].T
    f_bias = small["kv_b_f"].reshape(nh, 1)
    f_t = _fgate_fwd(f_logit_t, f_bias)
    f_grp = f_t.reshape(ATTN_GROUPS, nh // ATTN_GROUPS, t).transpose(0, 2, 1)
    (k_n,) = _row_fwd(_f_knorm_aug, [(proj_k, HEAD, 0)] + [(piece, 1, 0) for piece in _split3(-f_t.T)],
                      [small["k_norm_g"]], [BF16], nsub=nh, tb=tb, name="k_norm")
    proj_q = _mm_wblk(h_q, wts["b_q"], F32, "b_q", gb=nb)
    (q_n,) = _row_fwd(_f_qnorm_aug, [(proj_q, HEAD, 0)], [small["q_norm_g"]], [BF16], nsub=nh, tb=tb,
                      name="q_norm")
    o_att, lse = _attn_fwd(q_n, k_n, v_b, f_grp, attn_blk)
    (z,) = _row_fwd(_f_outgate, [(o_att, HEAD, 0), (proj_q, HEAD, 1)], [], [BF16], nsub=nh, tb=tb, name="out_gate")
    y_b = _mm(z, wts["b_out"], "nn", F32, "b_out")
    x3, h2_1 = _row_fwd(_f_res_mod, [(x2, d, 0), (y_b, d, 0)], vec("g1_1", "sh2_1", "sc2_1"), [F32, BF16],
                        tb=tb, name="l1_res_mod2")
    u1, a1, f1 = ffn_fwd(h2_1, 1)
    loss, dx4, df1, dg2_1 = _loss_call(x3, f1, mods["g2_1"], target, tb)

    g = {}
    dmods = {"g2_1": dg2_1}
    dh2, g["up1"], g["down1"], g["conv_w1"], g["conv_b1"] = ffn_bwd(df1, h2_1, u1, a1, 1)
    (dx2, dy_b), (dmods["g1_1"], dmods["sh2_1"], dmods["sc2_1"]) = _row_bwd(
        _f_res_mod, [(x2, d, 0), (y_b, d, 0)], vec("g1_1", "sh2_1", "sc2_1"),
        [(dx4, d, 0), (dh2, d, 0)], [F32, BF16], tb=tb, name="l1_res_mod2_bwd")
    dz = _mm(dy_b, wts["b_out"], "nt", BF16, "b_out_dx")
    g["b_out"] = _mm(z, dy_b, "tn", BF16, "b_out_dw", tk=t)
    do_att, dog, delta = _outgate_bwd(o_att, proj_q, dz, tb)
    dq_n, dk_n, dv, dfc_q, dfr_k = _attn_bwd(q_n, k_n, v_b, f_grp, do_att, lse, delta, attn_blk)
    (dpq,), (g["q_norm_g"],) = _row_bwd(_f_qnorm, [(proj_q, HEAD, 0)], [small["q_norm_g"]],
                                        [(dq_n, HEAD, 0)], [BF16], nsub=nh, tb=tb, name="q_norm_bwd")
    dproj_q = jnp.concatenate([dpq, dog], axis=1)
    dh_q = _mm_wblk_dx(dproj_q, wts["b_q"], BF16, "b_q_dx", k=d, gb=nb)
    g["b_q"] = _mm_wblk_dw(h_q, dproj_q, "b_q_dw", nb=nb, gb=nb // 4, tk=t)
    (dpk,), (g["k_norm_g"],) = _row_bwd(_f_knorm, [(proj_k, HEAD, 0)], [small["k_norm_g"]],
                                        [(dk_n, HEAD, 0)], [BF16], nsub=nh, tb=tb, name="k_norm_bwd")
    df_t = dfc_q.transpose(0, 2, 1).reshape(nh, t) + dfr_k.transpose(0, 2, 1, 3).reshape(nh, t)
    dflogit_t, g["kv_b_f"] = _fgate_bwd(f_logit_t, f_bias, df_t)
    dproj_f = jnp.pad(dflogit_t.T, ((0, 0), (0, LANES - nh))).astype(BF16)
    dh_kv = _mm(dpk, wts["kv_k"], "nn", BF16, "k_proj_dx")
    dh_kv_v = _mm(dv, wts["kv_v"], "nn", BF16, "v_proj_dx")
    dh_kv_f = _mm(dproj_f, wts["kv_f"], "nn", BF16, "kv_fproj_dx")
    g["kv_k"] = _mm(dpk, h_kv, "tn", BF16, "k_proj_dw", tk=t)
    g["kv_v"] = _mm(dv, h_kv, "tn", BF16, "v_proj_dw", tk=t)
    g["kv_f"] = _mm(dproj_f, h_kv, "tn", F32, "kv_fproj_dw", tk=1024)
    sent = put_g("l1", {n: g.pop(n) for n in ("b_out", "b_q", "kv_k", "kv_v", "kv_f", "up1", "down1")})
    (dx1, df0), (dmods["g2_0"], dmods["kv_sh"], dmods["kv_sc"], dmods["sh1_1"], dmods["sc1_1"]) = _row_bwd(
        _f_res_mod2, [(x1, d, 0), (f0, d, 0)], [mods["g2_0"] + sent] + vec("kv_sh", "kv_sc", "sh1_1", "sc1_1"),
        [(dx2, d, 0), (dh_kv, d, 0), (dh_q, d, 0)], [F32, BF16], tb=tb, name="l0_res_kvmod_qmod_bwd",
        cot_add=[(1, dh_kv_v), (1, dh_kv_f)])
    dh2, g["up0"], g["down0"], g["conv_w0"], g["conv_b0"] = ffn_bwd(df0, h2_0, u0, a0, 0)
    (dx0, dy_a), (dmods["g1_0"], dmods["sh2_0"], dmods["sc2_0"]) = _row_bwd(
        _f_res_mod, [(x, d, 0), (y_a, d, 0)], vec("g1_0", "sh2_0", "sc2_0"),
        [(dx1, d, 0), (dh2, d, 0)], [F32, BF16], tb=tb, name="l0_res_mod2_bwd")
    dypre = _mm(dy_a, wts["a_out"], "nt", BF16, "a_out_dx")
    g["a_out"] = _mm(ypre, dy_a, "tn", BF16, "a_out_dw", tk=t)
    sent = put_g("l0b", {n: g.pop(n) for n in ("a_out", "up0", "down0")})
    dproj_a, dlb, g["a_norm_g"] = _hgrn2_bwd(proj_a, lb + sent, small["a_norm_g"], states, dypre, tb)
    dh_a = _mm_wblk_dx(dproj_a, wts["a_in"], BF16, "a_in_dx", k=d, gb=nb, split=4, tm=512)
    put_g("l0a", {"a_in": _mm_wblk_dw(h_a, dproj_a, "a_in_dw", nb=nb, gb=1, split=4, tk=t)})
    (grad_x,), (dmods["sh1_0"], dmods["sc1_0"]) = _row_bwd(
        _f_mod, [(x, d, 0)], vec("sh1_0", "sc1_0"), [(dh_a, d, 0)], [F32], tb=tb, name="l0_mod1_bwd",
        add_to=(0, dx0))
    return loss, grad_x, dmods, dlb, g


def _position():
    return lax.axis_index("x"), lax.axis_index("y"), lax.axis_index("c")


_XCHG_EFFECT = pltpu.SideEffectType.DATAFLOW_SIDE_EFFECTING
ALL_PEERS = (1, 2, 3, 4, 5, 6, 7)
SAME_CORE = (2, 4, 6)


def _xchg_copies(src_refs, land_refs, send_sems, recv_sems, local_sems, scatter, rels):
    x, y, cc = _position()
    me = 4 * x + 2 * y + cc
    remote, local = [], []
    for a, (src, land) in enumerate(zip(src_refs, land_refs)):
        local.append(pltpu.make_async_copy(src.at[me] if scatter else src, land.at[me], local_sems.at[a]))
        for idx, rel in enumerate(rels):
            px = 1 - x if rel & 4 else x
            py = 1 - y if rel & 2 else y
            pc = 1 - cc if rel & 1 else cc
            k = len(rels) * a + idx
            remote.append(pltpu.make_async_remote_copy(
                src_ref=src.at[4 * px + 2 * py + pc] if scatter else src, dst_ref=land.at[me],
                send_sem=send_sems.at[k], recv_sem=recv_sems.at[k], device_id=(px, py, pc), device_id_type=_MESH))
    return remote, local


def _xchg_start(srcs, scatter, rels, after, name):
    n = len(srcs)
    lands = [lax.empty(s.shape if scatter else (NDEV, *s.shape), s.dtype) for s in srcs]

    def body(*refs):
        remote, local = _xchg_copies(refs[:n], refs[n:2 * n], *refs[2 * n + 1:2 * n + 4], scatter, rels)
        for cp in local + remote:
            cp.start()
        token = refs[-1]
        token[...] = jnp.zeros_like(token)

    hbm = pl.BlockSpec(memory_space=pltpu.HBM)
    sem = pl.BlockSpec(memory_space=pltpu.SEMAPHORE)
    out = pl.pallas_call(
        body, name=name,
        out_shape=(pltpu.SemaphoreType.DMA((len(rels) * n,)), pltpu.SemaphoreType.DMA((len(rels) * n,)),
                   pltpu.SemaphoreType.DMA((n,)),
                   *[pltpu.HBM(a.shape, a.dtype) for a in srcs + lands], jax.ShapeDtypeStruct((8, LANES), F32)),
        in_specs=[hbm] * (2 * n) + [pl.BlockSpec(memory_space=pl.ANY)],
        out_specs=(sem, sem, sem, *[hbm] * (2 * n), pl.BlockSpec(memory_space=pltpu.VMEM)),
        input_output_aliases={i: 3 + i for i in range(2 * n)},
        compiler_params=pltpu.CompilerParams(has_side_effects=_XCHG_EFFECT),
    )(*[pltpu.with_memory_space_constraint(a, pltpu.HBM) for a in srcs + lands], after)
    return out[:-1], out[-1][0, 0]


def _xchg_wait(handles, after, scatter, rels, name):
    n = (len(handles) - 3) // 2

    def body(*refs):
        remote, local = _xchg_copies(refs[:n], refs[n:2 * n], *refs[2 * n:2 * n + 3], scatter, rels)
        for cp in remote:
            cp.wait_send()
            cp.wait_recv()
        for cp in local:
            cp.wait()

    hbm = pl.BlockSpec(memory_space=pltpu.HBM)
    sem = pl.BlockSpec(memory_space=pltpu.SEMAPHORE)
    thru = list(handles[3:])
    afters = list(after) if isinstance(after, (list, tuple)) else [after]
    out = pl.pallas_call(
        body, name=name,
        out_shape=tuple(pltpu.HBM(a.shape, a.dtype) for a in thru),
        in_specs=[hbm] * (2 * n) + [sem, sem, sem] + [pl.BlockSpec(memory_space=pl.ANY)] * len(afters),
        out_specs=tuple([hbm] * (2 * n)),
        input_output_aliases={i: i for i in range(2 * n)},
        compiler_params=pltpu.CompilerParams(has_side_effects=_XCHG_EFFECT),
    )(*thru, *handles[:3], *afters)
    return list(out[n:])


def _sibling_copies(land_refs, send_sems, recv_sems):
    x, y, cc = _position()

    def copy(a, q, core):
        slot = land_refs[a].at[2 * q + core]
        return pltpu.make_async_remote_copy(
            src_ref=slot, dst_ref=slot, send_sem=send_sems.at[NCHIP * a + q], recv_sem=recv_sems.at[NCHIP * a + q],
            device_id=(x, y, 1 - cc), device_id_type=_MESH)

    pairs = [(a, q) for a in range(len(land_refs)) for q in range(NCHIP)]
    return [copy(a, q, cc) for a, q in pairs], [copy(a, q, 1 - cc) for a, q in pairs]


def _sibling_forward_start(lands, name, after=None):
    n = len(lands)
    deps = [] if after is None else [after]

    def body(*refs):
        sends, _ = _sibling_copies(refs[:n], refs[n + len(deps)], refs[n + len(deps) + 1])
        for cp in sends:
            cp.start()
        refs[-1][...] = jnp.zeros_like(refs[-1])

    hbm = pl.BlockSpec(memory_space=pltpu.HBM)
    sem = pl.BlockSpec(memory_space=pltpu.SEMAPHORE)
    out = pl.pallas_call(
        body, name=name,
        out_shape=(pltpu.SemaphoreType.DMA((NCHIP * n,)), pltpu.SemaphoreType.DMA((NCHIP * n,)),
                   *[pltpu.HBM(a.shape, a.dtype) for a in lands], jax.ShapeDtypeStruct((8, LANES), F32)),
        in_specs=[hbm] * n + [pl.BlockSpec(memory_space=pl.ANY)] * len(deps),
        out_specs=(sem, sem, *[hbm] * n, pl.BlockSpec(memory_space=pltpu.VMEM)),
        input_output_aliases={i: 2 + i for i in range(n)},
        compiler_params=pltpu.CompilerParams(has_side_effects=_XCHG_EFFECT),
    )(*lands, *deps)
    return out[:-1], out[-1][0, 0]


def _sibling_forward_wait(handles, after, name):
    n = len(handles) - 2

    def body(*refs):
        sends, arrivals = _sibling_copies(refs[:n], refs[n], refs[n + 1])
        for cp in sends:
            cp.wait_send()
        for cp in arrivals:
            cp.wait_recv()

    hbm = pl.BlockSpec(memory_space=pltpu.HBM)
    sem = pl.BlockSpec(memory_space=pltpu.SEMAPHORE)
    lands = list(handles[2:])
    return list(pl.pallas_call(
        body, name=name,
        out_shape=tuple(pltpu.HBM(a.shape, a.dtype) for a in lands),
        in_specs=[hbm] * n + [sem, sem, pl.BlockSpec(memory_space=pl.ANY)],
        out_specs=tuple([hbm] * n),
        input_output_aliases={i: i for i in range(n)},
        compiler_params=pltpu.CompilerParams(has_side_effects=_XCHG_EFFECT),
    )(*lands, *handles[:2], after))


def _slab_sum(slabs, name, tr=None):
    n, r, c = slabs.shape
    tr = r if tr is None else tr

    def body(s_ref, o_ref):
        acc = s_ref[0].astype(F32)
        for q in range(1, n):
            acc = acc + s_ref[q].astype(F32)
        o_ref[...] = acc

    return pl.pallas_call(body, name=name, grid=(r // tr,),
                          in_specs=[pl.BlockSpec((n, tr, c), lambda i: (0, i, 0))],
                          out_specs=pl.BlockSpec((tr, c), lambda i: (i, 0)),
                          out_shape=jax.ShapeDtypeStruct((r, c), F32),
                          compiler_params=_cparams(dimension_semantics=("parallel",)))(slabs)


def _ada_fwd(c_all, ada_w, kv_ada_w, logits):
    rows, d = c_all.shape
    n0, nkv = ada_w.shape[2], kv_ada_w.shape[1]

    def body(c_ref, w_ref, kw_ref, lg_ref, part_ref, cact_ref, lb_ref):
        ca = _silu(c_ref[...])
        cact_ref[...] = ca
        part_ref[:, 0:n0] = _bdot_raw(ca, w_ref[0], _NN)
        part_ref[:, n0:2 * n0] = _bdot_raw(ca, w_ref[1], _NN)
        part_ref[:, 2 * n0:2 * n0 + nkv] = _bdot_raw(ca, kw_ref[...], _NN)
        lb_ref[...] = _sigmoid(lg_ref[0:1, :] - lg_ref[1:2, :])

    vm = pl.BlockSpec(memory_space=pltpu.VMEM)
    return pl.pallas_call(
        body, name="ada_fwd", in_specs=[vm, vm, vm, vm], out_specs=[vm, vm, vm],
        out_shape=[jax.ShapeDtypeStruct((rows, 2 * n0 + nkv), F32), jax.ShapeDtypeStruct((rows, d), F32),
                   jax.ShapeDtypeStruct((1, d), F32)],
        compiler_params=_cparams(),
    )(c_all, ada_w, kv_ada_w, logits)


def _ada_bwd(c_act, dm0, dm1, dkv, lb, dlb):
    rows, d = c_act.shape

    def body(c_ref, d0_ref, d1_ref, dk_ref, lb_ref, dlb_ref, dw_ref, dkw_ref, dlg_ref):
        ca = c_ref[...]
        dw_ref[0] = _bdot_raw(ca, d0_ref[...], _TN)
        dw_ref[1] = _bdot_raw(ca, d1_ref[...], _TN)
        dkw_ref[...] = _bdot_raw(ca, dk_ref[...], _TN)
        lbv = lb_ref[...]
        dl0 = dlb_ref[...] * lbv * (1.0 - lbv)
        dlg_ref[0:1, :] = dl0
        dlg_ref[1:2, :] = -dl0

    vm = pl.BlockSpec(memory_space=pltpu.VMEM)
    return pl.pallas_call(
        body, name="ada_bwd", in_specs=[vm] * 6, out_specs=[vm, vm, vm],
        out_shape=[jax.ShapeDtypeStruct((2, d, dm0.shape[1]), F32), jax.ShapeDtypeStruct((d, dkv.shape[1]), F32),
                   jax.ShapeDtypeStruct((2, d), F32)],
        compiler_params=_cparams(),
    )(c_act, dm0, dm1, dkv, lb, dlb)


def _adamw(w, g, m, v, name, tr=512, after=None):
    r, c = w.shape
    tr = _divisor_tile(r, tr, unit=8)
    c1 = 1.0 - ADAM_B1 ** ADAM_STEP
    c2 = 1.0 - ADAM_B2 ** ADAM_STEP
    deps = [] if after is None else [after]

    def body(w_ref, g_ref, m_ref, v_ref, *rest):
        d_ref, mo_ref, vo_ref = rest[len(deps):]
        gv = g_ref[...]
        mn = ADAM_B1 * m_ref[...] + (1.0 - ADAM_B1) * gv
        vn = ADAM_B2 * v_ref[...] + (1.0 - ADAM_B2) * (gv * gv)
        d_ref[...] = -ADAM_LR * ((mn / c1) / (jnp.sqrt(vn / c2) + ADAM_EPS) + ADAM_WD * w_ref[...])
        mo_ref[...] = mn
        vo_ref[...] = vn

    spec = pl.BlockSpec((tr, c), lambda i: (i, 0))
    out = jax.ShapeDtypeStruct((r, c), F32)
    return pl.pallas_call(body, name=name, grid=(r // tr,),
                          in_specs=[spec] * 4 + [pl.BlockSpec(a.shape, lambda i: (0, 0)) for a in deps],
                          out_specs=[spec] * 3, out_shape=[out, out, out],
                          compiler_params=_cparams(dimension_semantics=("parallel",)))(w, g, m, v, *deps)


def _pad_rows(a, rows):
    return jnp.pad(a, ((0, rows - a.shape[0]), (0, 0)))


def _pack_small(parts, lanes=LANES, row_unit=8):
    flat = jnp.concatenate([p.reshape(-1).astype(F32) for p in parts])
    rows = _round_up(-(-flat.shape[0] // lanes), row_unit)
    return jnp.pad(flat, (0, rows * lanes - flat.shape[0])).reshape(rows, lanes)


def _unpack_small(flat, shapes):
    out, off = [], 0
    for s in shapes:
        n = 1
        for k in s:
            n *= k
        out.append(flat[off:off + n].reshape(s))
        off += n
    return out


def _pad_shard_cols(a, n_loc, n_pad):
    lead = a.shape[:-1]
    a = a.reshape(*lead, NDEV, n_loc)
    a = jnp.pad(a, [(0, 0)] * (len(lead) + 1) + [(0, n_pad - n_loc)])
    return a.reshape(*lead, NDEV * n_pad)


def _unpad_shard_cols(a, n_loc, n_pad):
    lead = a.shape[:-1]
    return a.reshape(*lead, NDEV, n_pad)[..., :n_loc].reshape(*lead, NDEV * n_loc)


def kernel(x, c, ada_w, ada_b, a_w_in, a_lb_logits, a_norm_g, a_w_out, kv_ada_w, kv_ada_b, kv_w, kv_b_f, k_norm_g, b_w_q, q_norm_g, b_w_out, ffn_w_up, ffn_conv_w, ffn_conv_b, ffn_w_down, loss_target, m_ada_w, m_ada_b, m_a_w_in, m_a_lb_logits, m_a_norm_g, m_a_w_out, m_kv_ada_w, m_kv_ada_b, m_kv_w, m_kv_b_f, m_k_norm_g, m_b_w_q, m_q_norm_g, m_b_w_out, m_ffn_w_up, m_ffn_conv_w, m_ffn_conv_b, m_ffn_w_down, v_ada_w, v_ada_b, v_a_w_in, v_a_lb_logits, v_a_norm_g, v_a_w_out, v_kv_ada_w, v_kv_ada_b, v_kv_w, v_kv_b_f, v_k_norm_g, v_b_w_q, v_q_norm_g, v_b_w_out, v_ffn_w_up, v_ffn_conv_w, v_ffn_conv_b, v_ffn_w_down):
    t, d = x.shape[1], x.shape[2]
    nh = d // HEAD
    ncw = ffn_w_up.shape[2]
    ncp = _round_up(ncw, LANES)
    two_f = ncw * NDEV
    ff = two_f // 2
    fp = ncp * NDEV // 2
    rd = ffn_w_down.shape[1]
    me = 4 * lax.axis_index("x") + 2 * lax.axis_index("y") + lax.axis_index("c")
    weights = dict(ada_w=ada_w, ada_b=ada_b, a_w_in=a_w_in, a_lb_logits=a_lb_logits, a_norm_g=a_norm_g,
                   a_w_out=a_w_out, kv_ada_w=kv_ada_w, kv_ada_b=kv_ada_b, kv_w=kv_w, kv_b_f=kv_b_f,
                   k_norm_g=k_norm_g, b_w_q=b_w_q, q_norm_g=q_norm_g, b_w_out=b_w_out, ffn_w_up=ffn_w_up,
                   ffn_conv_w=ffn_conv_w, ffn_conv_b=ffn_conv_b, ffn_w_down=ffn_w_down)
    m_in = dict(ada_w=m_ada_w, ada_b=m_ada_b, a_w_in=m_a_w_in, a_lb_logits=m_a_lb_logits, a_norm_g=m_a_norm_g,
                a_w_out=m_a_w_out, kv_ada_w=m_kv_ada_w, kv_ada_b=m_kv_ada_b, kv_w=m_kv_w, kv_b_f=m_kv_b_f,
                k_norm_g=m_k_norm_g, b_w_q=m_b_w_q, q_norm_g=m_q_norm_g, b_w_out=m_b_w_out, ffn_w_up=m_ffn_w_up,
                ffn_conv_w=m_ffn_conv_w, ffn_conv_b=m_ffn_conv_b, ffn_w_down=m_ffn_w_down)
    v_in = dict(ada_w=v_ada_w, ada_b=v_ada_b, a_w_in=v_a_w_in, a_lb_logits=v_a_lb_logits, a_norm_g=v_a_norm_g,
                a_w_out=v_a_w_out, kv_ada_w=v_kv_ada_w, kv_ada_b=v_kv_ada_b, kv_w=v_kv_w, kv_b_f=v_kv_b_f,
                k_norm_g=v_k_norm_g, b_w_q=v_b_w_q, q_norm_g=v_q_norm_g, b_w_out=v_b_w_out, ffn_w_up=v_ffn_w_up,
                ffn_conv_w=v_ffn_conv_w, ffn_conv_b=v_ffn_conv_b, ffn_w_down=v_ffn_w_down)
    order = list(weights)

    up_loc = jnp.pad(ffn_w_up, ((0, 0), (0, 0), (0, ncp - ncw))).astype(BF16)
    down_loc = ffn_w_down.astype(BF16)
    gather_names = {"l0b": ["a_out", "up0", "down0"], "l1": ["kv", "b_q", "b_out", "up1", "down1"]}
    forward_names = {"l0b": ["a_out"], "l0b_ffn": ["up0", "down0"], "l1": gather_names["l1"]}
    shards = {"a_out": a_w_out[0].astype(BF16), "up0": up_loc[0], "down0": down_loc[0], "kv": kv_w.T.astype(BF16),
              "b_q": b_w_q[0].astype(BF16), "b_out": b_w_out[0].astype(BF16), "up1": up_loc[1],
              "down1": down_loc[1]}
    pre = _pack_small([c, a_lb_logits, ffn_conv_w])
    in_flight = {}
    pre_flight, _ = _xchg_start([pre], False, ALL_PEERS, pre, "gather_small_inputs_start")
    (pre_all,) = _xchg_wait(pre_flight, pre, False, ALL_PEERS, "gather_small_inputs_wait")
    pre_all = pre_all.reshape(NDEV, -1)
    c_all = pre_all[:, :d]
    logits = pre_all[:, d:d + 2 * HEAD].reshape(NDEV, 2, HEAD).transpose(1, 0, 2).reshape(2, d)
    conv_w_full = pre_all[:, d + 2 * HEAD:d + 2 * HEAD + 2 * CONV_TAPS * ncw]
    conv_w_full = conv_w_full.reshape(NDEV, 2, CONV_TAPS, ncw).transpose(1, 2, 0, 3).reshape(2, CONV_TAPS, two_f)

    part, c_act, lb = _ada_fwd(_pad_rows(c_all, 2 * NDEV), ada_w, kv_ada_w, logits)
    part_flight, _ = _xchg_start([part[:NDEV]], False, ALL_PEERS, part, "gather_adaln_start")
    in_flight["l0a"], _ = _xchg_start([a_w_in[0].astype(BF16)], False, SAME_CORE, part_flight[-1], "gather_l0a_start")
    (part_all,) = _xchg_wait(part_flight, in_flight["l0a"][-1], False, ALL_PEERS, "gather_adaln_wait")
    forwarding = {}
    mine = lax.dynamic_index_in_dim(part_all, me, axis=1, keepdims=False)
    n0, nkv = ada_w.shape[2], kv_ada_w.shape[1]
    mod_names = ["sh1", "sc1", "g1", "sh2", "sc2", "g2"]
    mods = {}
    for l in range(2):
        row = mine[:, l * n0:(l + 1) * n0].reshape(-1) + ada_b[l]
        for k, nm in enumerate(mod_names):
            mods[f"{nm}_{l}"] = row[k * d:(k + 1) * d].reshape(1, d)
    kvrow = mine[:, 2 * n0:2 * n0 + nkv].reshape(-1) + kv_ada_b
    mods["kv_sh"], mods["kv_sc"] = kvrow[:d].reshape(1, d), kvrow[d:].reshape(1, d)

    def start_gather(grp, dep):
        srcs = [shards[n] for n in gather_names[grp]]
        in_flight[grp], started = _xchg_start(srcs, False, SAME_CORE, dep, f"gather_{grp}_start")
        return started

    zero = start_gather("l0b", part_all)
    mods["sh1_0"] = mods["sh1_0"] + zero

    small = {"a_norm_g": a_norm_g, "k_norm_g": k_norm_g.reshape(1, HEAD), "q_norm_g": q_norm_g, "kv_b_f": kv_b_f}
    for l in range(2):
        small[f"conv_w{l}"] = _pad_shard_cols(conv_w_full[l], ncw, ncp).reshape(CONV_TAPS, 2, fp).transpose(1, 0, 2)
        small[f"conv_b{l}"] = _pad_shard_cols(ffn_conv_b[l], ncw, ncp).reshape(2, 1, fp)

    def pre_w(grp, after):
        arrived = _xchg_wait(in_flight[grp], after, False, SAME_CORE, f"gather_{grp}_wait")
        if grp == "l0b":
            forwarding[grp], _ = _sibling_forward_start(arrived[:1], "gather_l0b_to_sibling_start")
            forwarding["l0b_ffn"], started = _sibling_forward_start(
                arrived[1:], "gather_l0b_ffn_to_sibling_start", after=forwarding[grp][-1])
            return started
        forwarding[grp], started = _sibling_forward_start(arrived, f"gather_{grp}_to_sibling_start")
        return started

    def get_w(grp, after):
        if grp == "l0a":
            arrived = _xchg_wait(in_flight["l0a"], after, False, SAME_CORE, "gather_l0a_wait")
            handles, _ = _sibling_forward_start(arrived, "gather_l0a_to_sibling_start")
            return {"a_in": _sibling_forward_wait(handles, after, "gather_l0a_to_sibling_wait")[0]}
        full = _sibling_forward_wait(forwarding[grp], after, f"gather_{grp}_to_sibling_wait")
        if grp == "l0b":
            started = start_gather("l1", full[0])
            full[0] = full[0] + started.astype(full[0].dtype)
        got = dict(zip(forward_names[grp], full))
        out = {}
        for n, a in got.items():
            if n in ("a_out", "b_out"):
                out[n] = a.reshape(d, d)
            elif n in ("down0", "down1"):
                dn = a.reshape(NCHIP, ff // NCHIP, d)
                out[n] = jnp.pad(dn, ((0, 0), (0, ncp - ncw), (0, 0))).reshape(fp, d)
            elif n == "kv":
                kv_t = a.reshape(NDEV * kv_w.shape[1], d)
                out["kv_k"], out["kv_v"] = kv_t[:d], kv_t[d:2 * d]
                out["kv_f"] = jnp.pad(kv_t[2 * d:], ((0, LANES - nh), (0, 0)))
            else:
                out[n] = a
        return out

    scatter_flight, g_last = {}, {}

    def put_g(grp, gr):
        if grp == "l0a":
            g_last.update(gr)
            return zero
        if grp == "l1":
            g_kvw = jnp.concatenate([gr["kv_k"], gr["kv_v"], gr["kv_f"][:nh].astype(BF16)], axis=0)
            arrs = {"kv_w": g_kvw.reshape(NDEV, kv_w.shape[1], d), "b_w_q": gr["b_q"],
                    "b_w_out": gr["b_out"].reshape(NDEV, d // NDEV, d), "up1": gr["up1"],
                    "down1": gr["down1"].reshape(NCHIP, ncp, d)[:, :ncw].reshape(NDEV, rd, d)}
        else:
            arrs = {"a_w_out": gr["a_out"].reshape(NDEV, d // NDEV, d), "up0": gr["up0"],
                    "down0": gr["down0"].reshape(NCHIP, ncp, d)[:, :ncw].reshape(NDEV, rd, d)}
        srcs = list(arrs.values())
        handles, sent = _xchg_start(srcs, True, ALL_PEERS, srcs[0], f"scatter_{grp}_start")
        scatter_flight[grp] = (list(arrs), handles)
        return sent

    loss_v, grad_x, dmods, dlb, g = _local_step(x[0], loss_target[0], mods, lb, small, pre_w, get_w, put_g)

    g_sum = {}
    for grp in ("l1", "l0b"):
        names, handles = scatter_flight[grp]
        for nm, a in zip(names, _xchg_wait(handles, grad_x, True, ALL_PEERS, f"scatter_{grp}_wait")):
            g_sum[nm] = _slab_sum(a, f"rs_slab_sum_{nm}")

    def conv_w_grad(a):
        return _unpad_shard_cols(a.transpose(1, 0, 2).reshape(CONV_TAPS, 2 * fp), ncw, ncp)

    def conv_b_grad(a):
        return _unpad_shard_cols(a.reshape(2 * fp), ncw, ncp)

    dmod_vec = [dmods[f"{nm}_{l}"] for l in range(2) for nm in mod_names] + [dmods["kv_sh"], dmods["kv_sc"]]
    post = _pack_small(dmod_vec + [dlb, g["a_norm_g"], g["k_norm_g"], g["q_norm_g"],
                                   jnp.pad(g["kv_b_f"].reshape(-1), (0, LANES - nh)),
                                   conv_w_grad(g["conv_w0"]), conv_w_grad(g["conv_w1"]),
                                   conv_b_grad(g["conv_b0"]), conv_b_grad(g["conv_b1"]), loss_v])
    post_flight, _ = _xchg_start([post], False, ALL_PEERS, post, "gather_small_grads_start")
    a_in_flight, a_in_sent = _xchg_start([g_last["a_in"]], True, ALL_PEERS, post_flight[-1], "scatter_l0a_start")
    a_in_sent = a_in_sent.reshape(1, 1)
    grads = {
        "a_w_out": g_sum["a_w_out"].reshape(a_w_out.shape),
        "kv_w": g_sum["kv_w"].T,
        "b_w_q": g_sum["b_w_q"].reshape(b_w_q.shape),
        "b_w_out": g_sum["b_w_out"].reshape(b_w_out.shape),
        "ffn_w_up": jnp.stack([g_sum["up0"][:, :ncw], g_sum["up1"][:, :ncw]]),
        "ffn_w_down": jnp.stack([g_sum["down0"], g_sum["down1"]]),
    }
    delta, new_m, new_v = {}, {}, {}

    def adamw_matrix(n):
        shp = weights[n].shape
        two_d = lambda a: a.reshape(-1, shp[-1])
        dl, mn, vn = _adamw(two_d(weights[n]), two_d(grads[n]), two_d(m_in[n]), two_d(v_in[n]), f"adamw_{n}",
                            after=a_in_sent)
        delta[n], new_m[n], new_v[n] = dl.reshape(shp), mn.reshape(shp), vn.reshape(shp)

    for n in grads:
        adamw_matrix(n)
    (post_all,) = _xchg_wait(post_flight, [new_v[n] for n in grads], False, ALL_PEERS, "gather_small_grads_wait")
    tot = _slab_sum(post_all, "small_grad_sum").reshape(-1)
    nmod = 14 * d
    (t_mod, t_lb, t_ang, t_kng, t_qng, t_bf, t_cw, t_cb, t_loss) = _unpack_small(
        tot, [(nmod,), (1, d), (1, HEAD), (HEAD,), (1, HEAD), (LANES,), (2, CONV_TAPS, two_f), (2, two_f),
              (LANES,)])
    loss = t_loss[0]
    dm_all = post_all.reshape(NDEV, -1)[:, :nmod]
    dm0 = lax.dynamic_slice_in_dim(dm_all[:, :6 * d], me * n0, n0, axis=1)
    dm1 = lax.dynamic_slice_in_dim(dm_all[:, 6 * d:12 * d], me * n0, n0, axis=1)
    dkv = lax.dynamic_slice_in_dim(dm_all[:, 12 * d:], me * nkv, nkv, axis=1)
    g_ada_w, g_kv_ada_w, g_logits = _ada_bwd(c_act, _pad_rows(dm0, 2 * NDEV), _pad_rows(dm1, 2 * NDEV),
                                              _pad_rows(dkv, 2 * NDEV), lb, t_lb)

    grads.update({
        "ada_w": g_ada_w,
        "ada_b": t_mod[:12 * d].reshape(2, 6 * d),
        "a_lb_logits": lax.dynamic_slice_in_dim(g_logits, me * HEAD, HEAD, axis=1),
        "a_norm_g": t_ang,
        "kv_ada_w": g_kv_ada_w,
        "kv_ada_b": t_mod[12 * d:],
        "kv_b_f": t_bf[:nh],
        "k_norm_g": t_kng,
        "q_norm_g": t_qng,
        "ffn_conv_w": lax.dynamic_slice_in_dim(t_cw, me * ncw, ncw, axis=2),
        "ffn_conv_b": t_cb,
    })

    small_adam = [n for n in order if n not in delta and n not in ("ada_w", "kv_ada_w", "a_w_in")]
    packs = [_pack_small([src[n] for n in small_adam]) for src in (weights, grads, m_in, v_in)]
    outs = _adamw(*packs, "adamw_small", tr=packs[0].shape[0])
    shapes = [weights[n].shape for n in small_adam]
    for dst, o in zip((delta, new_m, new_v), outs):
        for n, a in zip(small_adam, _unpack_small(o.reshape(-1), shapes)):
            dst[n] = a
    adamw_matrix("ada_w")
    adamw_matrix("kv_ada_w")
    (landed,) = _xchg_wait(a_in_flight, new_v["kv_ada_w"], True, ALL_PEERS, "scatter_l0a_wait")
    grads["a_w_in"] = _slab_sum(landed, "rs_slab_sum_a_w_in").reshape(a_w_in.shape)
    adamw_matrix("a_w_in")

    return (loss, grad_x.reshape(x.shape), *[grads[n] for n in order], *[delta[n] for n in order],
            *[new_m[n] for n in order], *[new_v[n] for n in order])
```

```python
import functools

import jax
import jax.numpy as jnp
from jax import lax
from jax.experimental import pallas as pl
from jax.experimental.pallas import tpu as pltpu

F32 = jnp.float32
BF16 = jnp.bfloat16

NDEV = 8
NCHIP = 4
HEAD = 128
A_CHUNK = 64
CONV_TAPS = 3
EPS = 1e-6
NEG_INF = -1e30
LANES = 128
VMEM_LIMIT = 48 * 1024 * 1024

ADAM_LR = 0.001
ADAM_B1 = 0.9
ADAM_B2 = 0.999
ADAM_EPS = 1e-08
ADAM_WD = 0.01
ADAM_STEP = 10

_NN = (((1,), (0,)), ((), ()))
_NT = (((1,), (1,)), ((), ()))
_TN = (((0,), (0,)), ((), ()))
_MESH = pl.DeviceIdType.MESH


def _cparams(**kw):
    return pltpu.CompilerParams(vmem_limit_bytes=VMEM_LIMIT, **kw)


def _divisor_tile(n, pref, unit=LANES):
    if n <= pref:
        return n
    best = None
    for t in range(unit, pref + 1, unit):
        if n % t == 0:
            best = t
    assert best is not None, (n, pref)
    return best


def _round_up(n, unit):
    return -(-n // unit) * unit


def _bdot_raw(a, b, dims):
    return lax.dot_general(a.astype(BF16), b.astype(BF16), dims, preferred_element_type=F32)


@jax.custom_vjp
def _dot_nn(a, b):
    return _bdot_raw(a, b, _NN)


@jax.custom_vjp
def _dot_nt(a, b):
    return _bdot_raw(a, b, _NT)


@jax.custom_vjp
def _dot_tn(a, b):
    return _bdot_raw(a, b, _TN)


_dot_nn.defvjp(lambda a, b: (_bdot_raw(a, b, _NN), (a, b)),
               lambda r, g: (_dot_nt(g, r[1]), _dot_tn(r[0], g)))
_dot_nt.defvjp(lambda a, b: (_bdot_raw(a, b, _NT), (a, b)),
               lambda r, g: (_dot_nn(g, r[1]), _dot_tn(g, r[0])))
_dot_tn.defvjp(lambda a, b: (_bdot_raw(a, b, _TN), (a, b)),
               lambda r, g: (_dot_nt(r[1], g), _dot_nn(r[0], g)))


def _f32dot(a, b):
    return lax.dot_general(a, b, _NN, precision=lax.Precision.HIGHEST, preferred_element_type=F32)


def _sigmoid(x):
    return jax.nn.sigmoid(x)


def _silu(x):
    return x * jax.nn.sigmoid(x)


def _rms(x):
    return x * lax.rsqrt(jnp.mean(x * x, axis=-1, keepdims=True) + EPS)


def _modulate(x, sh, sc):
    return _rms(x) * (1.0 + sc) + sh


def _mm_call(a, b, dims, a_spec, b_spec, o_spec, o_shape, grid, acc_tile, name):
    nk = grid[2]

    def body(a_ref, b_ref, o_ref, *acc):
        p = lax.dot_general(a_ref[...].astype(BF16), b_ref[...].astype(BF16), dims,
                            preferred_element_type=F32)
        if nk == 1:
            o_ref[...] = p.astype(o_ref.dtype)
        else:
            kk = pl.program_id(2)

            @pl.when(kk == 0)
            def _():
                acc[0][...] = p

            @pl.when(kk > 0)
            def _():
                acc[0][...] += p

            @pl.when(kk == nk - 1)
            def _():
                o_ref[...] = acc[0][...].astype(o_ref.dtype)

    return pl.pallas_call(
        body, name=name, grid=grid, in_specs=[a_spec, b_spec], out_specs=o_spec, out_shape=o_shape,
        scratch_shapes=[pltpu.VMEM(acc_tile, F32)] if nk > 1 else [],
        compiler_params=_cparams(dimension_semantics=("parallel", "parallel", "arbitrary")),
    )(a, b)


def _mm(a, b, mode, out_dtype, name, tm=1024, tn=1024, tk=2048):
    if mode == "nn":
        (m, k), (k2, n) = a.shape, b.shape
    elif mode == "nt":
        (m, k), (n, k2) = a.shape, b.shape
    else:
        (k, m), (k2, n) = a.shape, b.shape
    assert k == k2, (a.shape, b.shape, mode)
    tm, tn, tk = _divisor_tile(m, tm), _divisor_tile(n, tn), _divisor_tile(k, tk)
    if mode == "tn":
        a_spec = pl.BlockSpec((tk, tm), lambda i, j, kk: (kk, i))
    else:
        a_spec = pl.BlockSpec((tm, tk), lambda i, j, kk: (i, kk))
    if mode == "nt":
        b_spec = pl.BlockSpec((tn, tk), lambda i, j, kk: (j, kk))
    else:
        b_spec = pl.BlockSpec((tk, tn), lambda i, j, kk: (kk, j))
    return _mm_call(a, b, {"nn": _NN, "nt": _NT, "tn": _TN}[mode], a_spec, b_spec,
                    pl.BlockSpec((tm, tn), lambda i, j, kk: (i, j)), jax.ShapeDtypeStruct((m, n), out_dtype),
                    (m // tm, n // tn, k // tk), (tm, tn), name)


def _wblk_act_spec(rows, gb, nl, split, nb, row_axis, blk_axis):
    if split == 1:
        return pl.BlockSpec((rows, gb * nl), lambda *g: (g[row_axis], g[blk_axis]))
    groups = nb // split // gb
    return pl.BlockSpec((None, rows, gb * nl),
                        lambda *g: (g[blk_axis] // groups, g[row_axis], g[blk_axis] % groups))


def _mm_wblk(a, wb, out_dtype, name, *, gb, row_off=0, split=1, tm=1024):
    m, k = a.shape
    nb, _, nl = wb.shape
    assert (nb // split) % gb == 0
    tm = _divisor_tile(m, tm)

    def body(a_ref, b_ref, o_ref):
        av = a_ref[...].astype(BF16)
        for s in range(gb):
            o_ref[:, s * nl:(s + 1) * nl] = lax.dot_general(
                av, b_ref[s].astype(BF16), _NN, preferred_element_type=F32).astype(o_ref.dtype)

    o_shape = (m, nb * nl) if split == 1 else (split, m, nb // split * nl)
    return pl.pallas_call(
        body, name=name, grid=(nb // gb, m // tm),
        in_specs=[pl.BlockSpec((tm, k), lambda j, i: (i, 0)),
                  pl.BlockSpec((gb, k, nl), lambda j, i: (j, row_off, 0))],
        out_specs=_wblk_act_spec(tm, gb, nl, split, nb, 1, 0),
        out_shape=jax.ShapeDtypeStruct(o_shape, out_dtype),
        compiler_params=_cparams(dimension_semantics=("parallel", "parallel")),
    )(a, wb)


def _mm_wblk_dx(dy, wb, out_dtype, name, *, k, gb, row_off=0, split=1, tm=1024):
    nb, _, nl = wb.shape
    m = dy.shape[-2]
    tm = _divisor_tile(m, tm)
    nk = nb // gb
    per = nb // split
    whole = split > 1 and gb == nb
    assert whole or per % gb == 0

    def body(a_ref, b_ref, o_ref, *acc):
        p = None
        for s in range(gb):
            a_blk = a_ref[s // per, :, (s % per) * nl:(s % per + 1) * nl] if whole else a_ref[:, s * nl:(s + 1) * nl]
            q = lax.dot_general(a_blk.astype(BF16), b_ref[s].astype(BF16), _NT, preferred_element_type=F32)
            p = q if p is None else p + q
        if nk == 1:
            o_ref[...] = p.astype(o_ref.dtype)
        else:
            kk = pl.program_id(1)

            @pl.when(kk == 0)
            def _():
                acc[0][...] = p

            @pl.when(kk > 0)
            def _():
                acc[0][...] += p

            @pl.when(kk == nk - 1)
            def _():
                o_ref[...] = acc[0][...].astype(o_ref.dtype)

    return pl.pallas_call(
        body, name=name, grid=(m // tm, nk),
        in_specs=[pl.BlockSpec((split, tm, per * nl), lambda i, kk: (0, i, 0)) if whole
                  else _wblk_act_spec(tm, gb, nl, split, nb, 0, 1),
                  pl.BlockSpec((gb, k, nl), lambda i, kk: (kk, row_off, 0))],
        out_specs=pl.BlockSpec((tm, k), lambda i, kk: (i, 0)),
        out_shape=jax.ShapeDtypeStruct((m, k), out_dtype),
        scratch_shapes=[pltpu.VMEM((tm, k), F32)] if nk > 1 else [],
        compiler_params=_cparams(dimension_semantics=("parallel", "arbitrary")),
    )(dy, wb)


def _mm_wblk_dw(x, dy, name, *, nb, gb, split=1, tk=1024):
    t, k = x.shape
    assert (nb // split) % gb == 0
    nl = dy.shape[-1] * split // nb
    tk = _divisor_tile(t, tk)
    nk = t // tk

    def body(a_ref, b_ref, o_ref, *acc):
        kk = pl.program_id(1)
        av = a_ref[...].astype(BF16)
        for s in range(gb):
            p = lax.dot_general(av, b_ref[:, s * nl:(s + 1) * nl].astype(BF16), _TN, preferred_element_type=F32)
            if nk == 1:
                o_ref[s] = p.astype(o_ref.dtype)
                continue

            @pl.when(kk == 0)
            def _():
                acc[0][s] = p

            @pl.when(kk > 0)
            def _():
                acc[0][s] += p

        if nk > 1:
            @pl.when(kk == nk - 1)
            def _():
                o_ref[...] = acc[0][...].astype(o_ref.dtype)

    return pl.pallas_call(
        body, name=name, grid=(nb // gb, nk),
        in_specs=[pl.BlockSpec((tk, k), lambda j, kk: (kk, 0)), _wblk_act_spec(tk, gb, nl, split, nb, 1, 0)],
        out_specs=pl.BlockSpec((gb, k, nl), lambda j, kk: (j, 0, 0)),
        out_shape=jax.ShapeDtypeStruct((nb, k, nl), BF16),
        scratch_shapes=[pltpu.VMEM((gb, k, nl), F32)] if nk > 1 else [],
        compiler_params=_cparams(dimension_semantics=("parallel", "arbitrary")),
    )(x, dy)


def _row_specs(rows, tb, nsub):
    return [pl.BlockSpec((tb, nsub * cw), functools.partial(lambda i, off: (i, off), off=off))
            for (_, cw, off) in rows]


def _vec_specs(params):
    return [pl.BlockSpec(p.shape, lambda i: (0, 0)) for p in params]


def _row_fwd(f, rows, params, out_dtypes, *, nsub=1, tb, name):
    t = rows[0][0].shape[0]
    tb = min(tb, t)
    n_r, n_p = len(rows), len(params)
    blk = [jax.ShapeDtypeStruct((tb, cw), F32) for (_, cw, _) in rows]
    blk += [jax.ShapeDtypeStruct(p.shape, F32) for p in params]
    out_avals = jax.eval_shape(f, *blk)

    def body(*refs):
        pv = [r[...] for r in refs[n_r:n_r + n_p]]
        for s in range(nsub):
            vals = [r[:, s * cw:(s + 1) * cw].astype(F32) for r, (_, cw, _) in zip(refs[:n_r], rows)]
            outs = f(*vals, *pv)
            for o_ref, o in zip(refs[n_r + n_p:], outs):
                w = o.shape[1]
                o_ref[:, s * w:(s + 1) * w] = o.astype(o_ref.dtype)

    return pl.pallas_call(
        body, name=name,
        grid=(t // tb,),
        in_specs=_row_specs(rows, tb, nsub) + _vec_specs(params),
        out_specs=[pl.BlockSpec((tb, nsub * av.shape[1]), lambda i: (i, 0)) for av in out_avals],
        out_shape=[jax.ShapeDtypeStruct((t, nsub * av.shape[1]), dt) for av, dt in zip(out_avals, out_dtypes)],
        compiler_params=_cparams(dimension_semantics=("parallel",)),
    )(*[r[0] for r in rows], *params)


def _row_bwd(f, rows, params, cots, row_grad_dtypes, *, nsub=1, tb, name, add_to=None, cot_add=None):
    t = rows[0][0].shape[0]
    tb = min(tb, t)
    n_r, n_p, n_c = len(rows), len(params), len(cots)
    want = [j for j in range(n_r) if row_grad_dtypes[j] is not None]
    cot_add = cot_add or []
    extra = [] if add_to is None else [(add_to[1], rows[add_to[0]][1], 0)]
    n_add_to = len(extra)
    extra += [(arr, cots[ci][1], 0) for ci, arr in cot_add]

    def body(*refs):
        i = pl.program_id(0)
        r_in, p_in = refs[:n_r], refs[n_r:n_r + n_p]
        c_in = refs[n_r + n_p:n_r + n_p + n_c]
        e_in = refs[n_r + n_p + n_c:n_r + n_p + n_c + len(extra)]
        outs = refs[n_r + n_p + n_c + len(extra):]
        pv = [r[...] for r in p_in]
        psum = [None] * n_p
        for s in range(nsub):
            vals = [r[:, s * cw:(s + 1) * cw].astype(F32) for r, (_, cw, _) in zip(r_in, rows)]
            cvals = [r[:, s * cw:(s + 1) * cw].astype(F32) for r, (_, cw, _) in zip(c_in, cots)]
            for (ci, _), e_ref in zip(cot_add, e_in[n_add_to:]):
                cw = cots[ci][1]
                cvals[ci] = cvals[ci] + e_ref[:, s * cw:(s + 1) * cw].astype(F32)
            _, vjp_fn = jax.vjp(f, *vals, *pv)
            grads = vjp_fn(tuple(cvals))
            for o_ref, jr in zip(outs[:len(want)], want):
                cw = rows[jr][1]
                gr = grads[jr]
                if add_to is not None and jr == add_to[0]:
                    gr = gr + e_in[0][:, s * cw:(s + 1) * cw]
                o_ref[:, s * cw:(s + 1) * cw] = gr.astype(o_ref.dtype)
            for jp in range(n_p):
                psum[jp] = grads[n_r + jp] if psum[jp] is None else psum[jp] + grads[n_r + jp]
        for o_ref, g in zip(outs[len(want):], psum):
            @pl.when(i == 0)
            def _():
                o_ref[...] = g

            @pl.when(i > 0)
            def _():
                o_ref[...] += g

    out_specs = [pl.BlockSpec((tb, nsub * rows[jr][1]), lambda i: (i, 0)) for jr in want]
    out_shape = [jax.ShapeDtypeStruct((t, nsub * rows[jr][1]), row_grad_dtypes[jr]) for jr in want]
    out_specs += _vec_specs(params)
    out_shape += [jax.ShapeDtypeStruct(p.shape, F32) for p in params]
    res = pl.pallas_call(
        body, name=name,
        grid=(t // tb,),
        in_specs=_row_specs(rows, tb, nsub) + _vec_specs(params) + _row_specs(cots, tb, nsub)
        + _row_specs(extra, tb, nsub),
        out_specs=out_specs, out_shape=out_shape,
        compiler_params=_cparams(dimension_semantics=("arbitrary",)),
    )(*[r[0] for r in rows], *params, *[c[0] for c in cots], *[e[0] for e in extra])
    return res[:len(want)], res[len(want):]


def _f_mod(x, sh, sc):
    return (_modulate(x, sh, sc),)


def _f_res_mod(x, y, g, sh, sc):
    x1 = x + g * y
    return x1, _modulate(x1, sh, sc)


def _f_res_mod2(x, y, g, sh_a, sc_a, sh_b, sc_b):
    x1 = x + g * y
    return x1, _modulate(x1, sh_a, sc_a), _modulate(x1, sh_b, sc_b)


def _f_qnorm(p, g):
    return (_rms(p) * g * (HEAD ** -0.5),)


def _f_knorm(p, g):
    return (_rms(p) * g,)


def _f_qnorm_aug(p, g):
    lane = lax.broadcasted_iota(jnp.int32, p.shape, 1)
    return (jnp.concatenate([_rms(p) * g * (HEAD ** -0.5), jnp.where(lane < 3, 1.0, 0.0)], axis=1),)


def _f_knorm_aug(p, c0, c1, c2, g):
    lane = lax.broadcasted_iota(jnp.int32, p.shape, 1)
    aug = jnp.where(lane == 0, c0, jnp.where(lane == 1, c1, jnp.where(lane == 2, c2, 0.0)))
    return (jnp.concatenate([_rms(p) * g, aug], axis=1),)


def _split3(a):
    round_bf16 = lambda v: lax.reduce_precision(v, exponent_bits=8, mantissa_bits=7)
    hi = round_bf16(a)
    mid = round_bf16(a - hi)
    lo = round_bf16(a - hi - mid)
    return hi.astype(BF16), mid.astype(BF16), lo.astype(BF16)


def _f_outgate(o, og):
    return (o * _sigmoid(og),)


def _loss_call(x3, f, g2, target, tb):
    t, d = x3.shape
    tb = min(tb, t)

    def body(x_ref, f_ref, g_ref, t_ref, loss_ref, dx_ref, df_ref, dg_ref):
        i = pl.program_id(0)
        fv = f_ref[...]
        g = g_ref[...]
        e = x_ref[...] + g * fv - t_ref[...]
        dx = e * (1.0 / d)
        part = 0.5 * jnp.sum(jnp.sum(e * dx, axis=1, keepdims=True), axis=0, keepdims=True)
        dx_ref[...] = dx
        df_ref[...] = (g * dx).astype(df_ref.dtype)
        dg = jnp.sum(dx * fv, axis=0, keepdims=True)

        @pl.when(i == 0)
        def _():
            loss_ref[...] = jnp.broadcast_to(part, loss_ref.shape)
            dg_ref[...] = dg

        @pl.when(i > 0)
        def _():
            loss_ref[...] += jnp.broadcast_to(part, loss_ref.shape)
            dg_ref[...] += dg

    row = pl.BlockSpec((tb, d), lambda i: (i, 0))
    vec = pl.BlockSpec((1, d), lambda i: (0, 0))
    return pl.pallas_call(
        body, name="loss_head",
        grid=(t // tb,),
        in_specs=[row, row, vec, row],
        out_specs=[pl.BlockSpec((1, LANES), lambda i: (0, 0)), row, row, vec],
        out_shape=[jax.ShapeDtypeStruct((1, LANES), F32), jax.ShapeDtypeStruct((t, d), F32),
                   jax.ShapeDtypeStruct((t, d), BF16), jax.ShapeDtypeStruct((1, d), F32)],
        compiler_params=_cparams(dimension_semantics=("arbitrary",)),
    )(x3, f, g2, target)


def _hg_mask(tb):
    br = lax.broadcasted_iota(jnp.int32, (tb, tb), 0)
    bs = lax.broadcasted_iota(jnp.int32, (tb, tb), 1)
    return jnp.logical_and(br // A_CHUNK == bs // A_CHUNK, bs <= br).astype(F32)


def _hg_consts(mask):
    c = A_CHUNK
    r = lax.broadcasted_iota(jnp.int32, (c, c), 0)
    s = lax.broadcasted_iota(jnp.int32, (c, c), 1)
    return (s <= r).astype(F32), (r <= s).astype(F32), mask > 0.5


def _chunk_apply(mat, x):
    c = mat.shape[0]
    return jnp.concatenate([_f32dot(mat, x[i * c:(i + 1) * c]) for i in range(x.shape[0] // c)], axis=0)


@jax.custom_vjp
def _chunk_cumsum(x, tri, tri_t):
    return _chunk_apply(tri, x)


_chunk_cumsum.defvjp(lambda x, tri, tri_t: (_chunk_apply(tri, x), (tri, tri_t)),
                     lambda r, g: (_chunk_apply(r[1], g), jnp.zeros_like(r[0]), jnp.zeros_like(r[1])))


def _per_chunk(a, b, dims):
    return jnp.stack([_bdot_raw(a[i], b[i], dims) for i in range(a.shape[0])])


@jax.custom_vjp
def _chunk_tn(a, b):
    return _per_chunk(a, b, _TN)


@jax.custom_vjp
def _chunk_nt(a, b):
    return _per_chunk(a, b, _NT)


@jax.custom_vjp
def _chunk_nn(a, b):
    return _per_chunk(a, b, _NN)


_chunk_tn.defvjp(lambda a, b: (_per_chunk(a, b, _TN), (a, b)),
                 lambda r, g: (_chunk_nt(r[1], g), _chunk_nn(r[0], g)))
_chunk_nt.defvjp(lambda a, b: (_per_chunk(a, b, _NT), (a, b)),
                 lambda r, g: (_chunk_nn(g, r[1]), _chunk_tn(g, r[0])))
_chunk_nn.defvjp(lambda a, b: (_per_chunk(a, b, _NN), (a, b)),
                 lambda r, g: (_chunk_nt(g, r[1]), _chunk_tn(r[0], g)))


def _scan_states(decay, m, st):
    sts = []
    for i in range(m.shape[0]):
        sts.append(st)
        st = st * decay[i] + m[i]
    return jnp.stack(sts), st


@jax.custom_vjp
def _state_scan(decay, m, st):
    return _scan_states(decay, m, st)


def _state_scan_fwd(decay, m, st):
    sts, st_out = _scan_states(decay, m, st)
    return (sts, st_out), (decay, sts)


def _state_scan_bwd(res, cts):
    decay, sts = res
    d_sts, g = cts
    d_decay, d_m = [], []
    for i in range(sts.shape[0] - 1, -1, -1):
        d_m.append(g)
        d_decay.append(jnp.sum(g * sts[i], axis=0, keepdims=True))
        g = g * decay[i] + d_sts[i]
    return jnp.stack(d_decay[::-1]), jnp.stack(d_m[::-1]), g


_state_scan.defvjp(_state_scan_fwd, _state_scan_bwd)


def _hg_block(qp, fp, ip, gp, lb, ng, st, tri, tri_t, bd_causal):
    tb = qp.shape[0]
    c = A_CHUNK
    n = tb // c
    q = _silu(qp)
    fg = lb + (1.0 - lb) * _sigmoid(fp)
    logf = jnp.log(fg)
    k = 1.0 - fg
    b3 = _chunk_cumsum(logf, tri, tri_t).reshape(n, c, HEAD)
    pos = lax.broadcasted_iota(jnp.int32, (1, c, 1), 1)
    b_mid = lax.stop_gradient(jnp.sum(jnp.where(pos == c // 2, b3, 0.0), axis=1, keepdims=True))
    b_last = jnp.sum(jnp.where(pos == c - 1, b3, 0.0), axis=1, keepdims=True)
    q3, k3, v3 = q.reshape(n, c, HEAD), k.reshape(n, c, HEAD), ip.reshape(n, c, HEAD)
    scores = _dot_nt((q3 * jnp.exp(b3 - b_mid)).reshape(tb, HEAD), (k3 * jnp.exp(b_mid - b3)).reshape(tb, HEAD))
    o_intra = _dot_nn(jnp.where(bd_causal, scores, 0.0), ip)
    states, st_new = _state_scan(jnp.exp(b_last), _chunk_tn(v3, k3 * jnp.exp(b_last - b3)), st)
    o = o_intra + _chunk_nt(q3 * jnp.exp(b3), states).reshape(tb, HEAD)
    y = _rms(o) * ng * _silu(gp)
    return y, st_new


HG_HEADS = 2


def _hg_specs(tb, nh, rev_nb=None):
    wide = HG_HEADS * HEAD
    per = nh // HG_HEADS

    def row(part):
        if rev_nb is None:
            return pl.BlockSpec((tb, wide), functools.partial(lambda h, i, off: (i, off + h), off=part * per))
        return pl.BlockSpec((tb, wide),
                            functools.partial(lambda h, i, off: (rev_nb - 1 - i, off + h), off=part * per))
    return [row(0), row(1), row(2), row(3),
            pl.BlockSpec((1, wide), lambda h, i: (0, h)), pl.BlockSpec((1, HEAD), lambda h, i: (0, 0)),
            pl.BlockSpec((tb, tb), lambda h, i: (0, 0))]


def _hgrn2_fwd(proj, lb, ng, tb):
    t = proj.shape[0]
    nh = proj.shape[1] // (4 * HEAD)
    tb = min(tb, t)
    nb = t // tb
    wide = HG_HEADS * HEAD

    def body(q_ref, f_ref, i_ref, g_ref, lb_ref, ng_ref, mask_ref, y_ref, s_ref, st_ref):
        i = pl.program_id(1)

        @pl.when(i == 0)
        def _():
            st_ref[...] = jnp.zeros_like(st_ref)

        consts = _hg_consts(mask_ref[...])
        for p in range(HG_HEADS):
            cs = slice(p * HEAD, (p + 1) * HEAD)
            st = st_ref[p]
            s_ref[p, 0] = st
            y, st_new = _hg_block(q_ref[:, cs], f_ref[:, cs], i_ref[:, cs], g_ref[:, cs], lb_ref[:, cs],
                                  ng_ref[...], st, *consts)
            y_ref[:, cs] = y.astype(y_ref.dtype)
            st_ref[p] = st_new

    return pl.pallas_call(
        body, name="hgrn2_fwd",
        grid=(nh // HG_HEADS, nb),
        in_specs=_hg_specs(tb, nh),
        out_specs=[pl.BlockSpec((tb, wide), lambda h, i: (i, h)),
                   pl.BlockSpec((HG_HEADS, 1, HEAD, HEAD), lambda h, i: (h, i, 0, 0))],
        out_shape=[jax.ShapeDtypeStruct((t, nh * HEAD), BF16),
                   jax.ShapeDtypeStruct((nh, nb, HEAD, HEAD), F32)],
        scratch_shapes=[pltpu.VMEM((HG_HEADS, HEAD, HEAD), F32)],
        compiler_params=_cparams(dimension_semantics=("parallel", "arbitrary")),
    )(proj, proj, proj, proj, lb, ng, _hg_mask(tb))


def _hgrn2_bwd(proj, lb, ng, states, dy, tb):
    t = proj.shape[0]
    nh = proj.shape[1] // (4 * HEAD)
    tb = min(tb, t)
    nb = t // tb
    wide = HG_HEADS * HEAD

    def body(q_ref, f_ref, i_ref, g_ref, lb_ref, ng_ref, mask_ref, s_ref, dy_ref,
             dp_ref, dlb_ref, dng_ref, dst_ref):
        h, i = pl.program_id(0), pl.program_id(1)
        consts = _hg_consts(mask_ref[...])

        @pl.when(i == 0)
        def _():
            dst_ref[...] = jnp.zeros_like(dst_ref)
            dlb_ref[...] = jnp.zeros_like(dlb_ref)

        @pl.when(jnp.logical_and(i == 0, h == 0))
        def _():
            dng_ref[...] = jnp.zeros_like(dng_ref)

        def fn(qp, fp, ip, gp, lbx, ngx, stx):
            return _hg_block(qp, fp, ip, gp, lbx, ngx, stx, *consts)

        for p in range(HG_HEADS):
            cs = slice(p * HEAD, (p + 1) * HEAD)
            _, vjp_fn = jax.vjp(fn, q_ref[:, cs], f_ref[:, cs], i_ref[:, cs], g_ref[:, cs], lb_ref[:, cs],
                                ng_ref[...], s_ref[p, 0])
            *gparts, glb, gng, dst = vjp_fn((dy_ref[:, cs].astype(F32), dst_ref[p]))
            for part, gpart in enumerate(gparts):
                dp_ref[part, :, cs] = gpart.astype(dp_ref.dtype)
            dst_ref[p] = dst
            dlb_ref[:, cs] += glb
            dng_ref[...] += gng

    rev = lambda h, i: (nb - 1 - i, h)
    return pl.pallas_call(
        body, name="hgrn2_bwd",
        grid=(nh // HG_HEADS, nb),
        in_specs=_hg_specs(tb, nh, rev_nb=nb) + [
            pl.BlockSpec((HG_HEADS, 1, HEAD, HEAD), lambda h, i: (h, nb - 1 - i, 0, 0)),
            pl.BlockSpec((tb, wide), rev)],
        out_specs=[pl.BlockSpec((4, tb, wide), lambda h, i: (0, nb - 1 - i, h)),
                   pl.BlockSpec((1, wide), lambda h, i: (0, h)), pl.BlockSpec((1, HEAD), lambda h, i: (0, 0))],
        out_shape=[jax.ShapeDtypeStruct((4, t, nh * HEAD), BF16),
                   jax.ShapeDtypeStruct((1, nh * HEAD), F32), jax.ShapeDtypeStruct((1, HEAD), F32)],
        scratch_shapes=[pltpu.VMEM((HG_HEADS, HEAD, HEAD), F32)],
        compiler_params=_cparams(dimension_semantics=("arbitrary", "arbitrary")),
    )(proj, proj, proj, proj, lb, ng, _hg_mask(tb), states, dy)


def _fgate_consts(cb):
    r = lax.broadcasted_iota(jnp.int32, (cb, cb), 0)
    s = lax.broadcasted_iota(jnp.int32, (cb, cb), 1)
    return (r <= s).astype(F32), (r >= s).astype(F32)


def _fgate_fwd(xt, bias, cb=512):
    nh, t = xt.shape
    cb = min(cb, t)

    def body(x_ref, b_ref, o_ref):
        upper, _ = _fgate_consts(cb)
        carry = jnp.zeros((nh, 1), F32)
        for blk in range(t // cb):
            z = x_ref[:, blk * cb:(blk + 1) * cb] + b_ref[...]
            logf = jnp.minimum(z, 0.0) - jnp.log(1.0 + jnp.exp(-jnp.abs(z)))
            cs = _f32dot(logf, upper) + carry
            o_ref[:, blk * cb:(blk + 1) * cb] = cs
            carry = cs[:, cb - 1:cb]

    vm = pl.BlockSpec(memory_space=pltpu.VMEM)
    return pl.pallas_call(
        body, name="fgate_fwd", in_specs=[vm, vm], out_specs=vm,
        out_shape=jax.ShapeDtypeStruct((nh, t), F32), compiler_params=_cparams(),
    )(xt, bias)


def _fgate_bwd(xt, bias, dft, cb=512):
    nh, t = xt.shape
    cb = min(cb, t)
    nblk = t // cb

    def body(x_ref, b_ref, d_ref, dx_ref, db_ref):
        _, lower = _fgate_consts(cb)
        carry = jnp.zeros((nh, 1), F32)
        db = jnp.zeros((nh, 1), F32)
        for blk in range(nblk - 1, -1, -1):
            sl = slice(blk * cb, (blk + 1) * cb)
            dlogf = _f32dot(d_ref[:, sl], lower) + carry
            carry = dlogf[:, 0:1]
            z = x_ref[:, sl] + b_ref[...]
            dz = dlogf * (1.0 - _sigmoid(z))
            dx_ref[:, sl] = dz
            db = db + jnp.sum(dz, axis=1, keepdims=True)
        db_ref[...] = db

    vm = pl.BlockSpec(memory_space=pltpu.VMEM)
    return pl.pallas_call(
        body, name="fgate_bwd", in_specs=[vm, vm, vm], out_specs=[vm, vm],
        out_shape=[jax.ShapeDtypeStruct((nh, t), F32), jax.ShapeDtypeStruct((nh, 1), F32)],
        compiler_params=_cparams(),
    )(xt, bias, dft)


ATTN_GROUPS = 4
ATTN_FWD_HEADS = 2


def _attn_fwd(q, k, v, f_grp, blk):
    t, width = v.shape
    nh = width // HEAD
    nq = t // blk
    hpg = nh // ATTN_GROUPS

    def body(q_ref, k_ref, v_ref, fc_ref, o_ref, lse_ref):
        i = pl.program_id(0)
        tri = (lax.broadcasted_iota(jnp.int32, (blk, blk), 1) <= lax.broadcasted_iota(jnp.int32, (blk, blk), 0))
        for h0 in range(0, nh, ATTN_FWD_HEADS):
            heads = range(h0, min(h0 + ATTN_FWD_HEADS, nh))

            def tile(j, carries, masked):
                rs = pl.ds(pl.multiple_of(j * blk, blk), blk)
                out = []
                for h, (m, l, acc) in zip(heads, carries):
                    cs = slice(h * HEAD, (h + 1) * HEAD)
                    cs2 = slice(2 * h * HEAD, 2 * (h + 1) * HEAD)
                    s = _bdot_raw(q_ref[:, cs2], k_ref[rs, cs2], _NT)
                    if masked:
                        s = jnp.where(tri, s, NEG_INF)
                    m_new = jnp.maximum(m, jnp.max(s, axis=1, keepdims=True))
                    p = jnp.exp(s - m_new)
                    alpha = jnp.exp(m - m_new)
                    l_new = alpha * l + jnp.sum(p, axis=1, keepdims=True)
                    out.append((m_new, l_new, alpha * acc + _bdot_raw(p, v_ref[rs, cs], _NN)))
                return tuple(out)

            init = tuple((jnp.full((blk, 1), NEG_INF, F32), jnp.zeros((blk, 1), F32), jnp.zeros((blk, HEAD), F32))
                         for _ in heads)
            carries = lax.fori_loop(0, i, lambda j, c: tile(j, c, False), init)
            for h, (m, l, acc) in zip(heads, tile(i, carries, True)):
                o_ref[:, h * HEAD:(h + 1) * HEAD] = acc / l
                g, hh = divmod(h, hpg)
                lse_ref[g, :, hh:hh + 1] = m + jnp.log(l) + fc_ref[g, :, hh:hh + 1]

    vm = pl.BlockSpec(memory_space=pltpu.VMEM)
    stat = pl.BlockSpec((ATTN_GROUPS, blk, hpg), lambda i: (0, i, 0))
    return pl.pallas_call(
        body, name="fox_attn_fwd",
        grid=(nq,),
        in_specs=[pl.BlockSpec((blk, 2 * width), lambda i: (i, 0)), vm, vm, stat],
        out_specs=[pl.BlockSpec((blk, width), lambda i: (i, 0)), stat],
        out_shape=[jax.ShapeDtypeStruct((t, width), F32), jax.ShapeDtypeStruct((ATTN_GROUPS, t, hpg), F32)],
        compiler_params=_cparams(dimension_semantics=("parallel",)),
    )(q, k, v, f_grp)


def _outgate_bwd(o, proj_q, dz, tb):
    t, width = o.shape
    nh = width // HEAD
    hpg = nh // ATTN_GROUPS
    tb = min(tb, t)

    def body(o_ref, og_ref, dz_ref, do_ref, dog_ref, dl_ref):
        for h in range(nh):
            cs = slice(h * HEAD, (h + 1) * HEAD)
            ov = o_ref[:, cs]
            _, vjp_fn = jax.vjp(_f_outgate, ov, og_ref[:, cs])
            do, dog = vjp_fn((dz_ref[:, cs].astype(F32),))
            do = do.astype(do_ref.dtype)
            do_ref[:, cs] = do
            dog_ref[:, cs] = dog.astype(dog_ref.dtype)
            g, hh = divmod(h, hpg)
            dl_ref[g, :, hh:hh + 1] = jnp.sum(do.astype(F32) * ov, axis=1, keepdims=True)

    wide = pl.BlockSpec((tb, width), lambda i: (i, 0))
    return pl.pallas_call(body, name="out_gate_bwd", grid=(t // tb,),
                          in_specs=[wide, pl.BlockSpec((tb, width), lambda i: (i, 1)), wide],
                          out_specs=[wide, wide, pl.BlockSpec((ATTN_GROUPS, tb, hpg), lambda i: (0, i, 0))],
                          out_shape=[jax.ShapeDtypeStruct((t, width), BF16), jax.ShapeDtypeStruct((t, width), BF16),
                                     jax.ShapeDtypeStruct((ATTN_GROUPS, t, hpg), F32)],
                          compiler_params=_cparams(dimension_semantics=("parallel",)))(o, proj_q, dz)


def _qnorm_bwd(proj_q, gain, dq_n, dog, tb):
    t, width = dq_n.shape
    nh = width // HEAD
    tb = min(tb, t)

    def body(p_ref, g_ref, dq_ref, dog_ref, out_ref, dg_ref):
        i = pl.program_id(0)
        gv = g_ref[...]
        acc = None
        for h in range(nh):
            cs = slice(h * HEAD, (h + 1) * HEAD)
            _, vjp_fn = jax.vjp(_f_qnorm, p_ref[:, cs], gv)
            dp, dg = vjp_fn((dq_ref[:, cs],))
            out_ref[:, cs] = dp.astype(out_ref.dtype)
            acc = dg if acc is None else acc + dg
        out_ref[:, width:] = dog_ref[...]

        @pl.when(i == 0)
        def _():
            dg_ref[...] = acc

        @pl.when(i > 0)
        def _():
            dg_ref[...] += acc

    wide = pl.BlockSpec((tb, width), lambda i: (i, 0))
    vec = pl.BlockSpec(gain.shape, lambda i: (0, 0))
    return pl.pallas_call(body, name="q_norm_bwd", grid=(t // tb,), in_specs=[wide, vec, wide, wide],
                          out_specs=[pl.BlockSpec((tb, 2 * width), lambda i: (i, 0)), vec],
                          out_shape=[jax.ShapeDtypeStruct((t, 2 * width), BF16), jax.ShapeDtypeStruct(gain.shape, F32)],
                          compiler_params=_cparams(dimension_semantics=("arbitrary",)))(proj_q, gain, dq_n, dog)


def _attn_bwd(q, k, v, f_grp, do, lse, delta, blk):
    t, width = v.shape
    nh = width // HEAD
    nq = t // blk
    hpg = nh // ATTN_GROUPS
    gw = hpg * HEAD

    def body(q_ref, do_ref, k_ref, v_ref, fc_ref, lse_ref, dl_ref,
             dq_ref, dk_ref, dv_ref, dfc_ref, dfr_ref):
        g, j = pl.program_id(0), pl.program_id(1)
        tri = (lax.broadcasted_iota(jnp.int32, (blk, blk), 1) <= lax.broadcasted_iota(jnp.int32, (blk, blk), 0))

        @pl.when(j == 0)
        def _():
            dq_ref[...] = jnp.zeros_like(dq_ref)
            dfc_ref[...] = jnp.zeros_like(dfc_ref)

        def tile(i, carries, masked):
            rs = pl.ds(pl.multiple_of(i * blk, blk), blk)
            out = []
            for h, (dk, dv, dfs) in enumerate(carries):
                cs = slice(h * HEAD, (h + 1) * HEAD)
                cs2 = slice(2 * h * HEAD, 2 * (h + 1) * HEAD)
                csq = slice(2 * h * HEAD, (2 * h + 1) * HEAD)
                qi = q_ref[rs, csq]
                doi = do_ref[rs, cs]
                bias = fc_ref[0, rs, h:h + 1] - lse_ref[0, rs, h:h + 1]
                p = jnp.exp(_bdot_raw(q_ref[rs, cs2], k_ref[:, cs2], _NT) + bias)
                if masked:
                    p = jnp.where(tri, p, 0.0)
                ds = p * (_bdot_raw(doi, v_ref[:, cs], _NT) - dl_ref[0, rs, h:h + 1])
                dsb = ds.astype(BF16)
                dq_ref[rs, cs] += _bdot_raw(dsb, k_ref[:, csq], _NN)
                dfc_ref[0, rs, h:h + 1] += jnp.sum(ds, axis=1, keepdims=True)
                out.append((dk + _bdot_raw(dsb, qi, _TN), dv + _bdot_raw(p, doi, _TN),
                            dfs - jnp.sum(ds, axis=0, keepdims=True)))
            return tuple(out)

        init = tuple((jnp.zeros((blk, HEAD), F32), jnp.zeros((blk, HEAD), F32), jnp.zeros((1, blk), F32))
                     for _ in range(hpg))
        carries = lax.fori_loop(j + 1, nq, lambda i, c: tile(i, c, False), tile(j, init, True))
        for h, (dk, dv, dfs) in enumerate(carries):
            cs = slice(h * HEAD, (h + 1) * HEAD)
            dk_ref[:, cs] = dk
            dv_ref[:, cs] = dv.astype(dv_ref.dtype)
            dfr_ref[0, 0, h:h + 1, :] = dfs

    once = pl.Buffered(1)
    stat = pl.BlockSpec((1, t, hpg), lambda g, j: (g, 0, 0), pipeline_mode=once)
    kv_blk = pl.BlockSpec((blk, gw), lambda g, j: (j, g))
    frow = pl.BlockSpec((1, 1, hpg, blk), lambda g, j: (g, j, 0, 0))
    dq, dk, dv, dfc, dfr = pl.pallas_call(
        body, name="fox_attn_bwd",
        grid=(ATTN_GROUPS, nq),
        in_specs=[pl.BlockSpec((t, 2 * gw), lambda g, j: (0, g), pipeline_mode=once),
                  pl.BlockSpec((t, gw), lambda g, j: (0, g), pipeline_mode=once),
                  pl.BlockSpec((blk, 2 * gw), lambda g, j: (j, g)), kv_blk, stat, stat, stat],
        out_specs=[pl.BlockSpec((t, gw), lambda g, j: (0, g)), kv_blk, kv_blk,
                   pl.BlockSpec((1, t, hpg), lambda g, j: (g, 0, 0)), frow],
        out_shape=[jax.ShapeDtypeStruct((t, width), F32), jax.ShapeDtypeStruct((t, width), F32),
                   jax.ShapeDtypeStruct((t, width), BF16), jax.ShapeDtypeStruct((ATTN_GROUPS, t, hpg), F32),
                   jax.ShapeDtypeStruct((ATTN_GROUPS, nq, hpg, blk), F32)],
        compiler_params=_cparams(dimension_semantics=("parallel", "arbitrary")),
    )(q, do, k, v, f_grp, lse, delta)
    return dq, dk, dv, dfc, dfr


SUBLANES = 8


def _shift_down(u, n):
    r = pltpu.roll(u, n, 0)
    row = lax.broadcasted_iota(jnp.int32, (SUBLANES, u.shape[1]), 0)
    return jnp.concatenate([jnp.where(row < n, 0.0, r[:SUBLANES]), r[SUBLANES:]], axis=0)


def _shift_up(u, n):
    t = u.shape[0]
    r = pltpu.roll(u, t - n, 0)
    row = lax.broadcasted_iota(jnp.int32, (SUBLANES, u.shape[1]), 0)
    return jnp.concatenate([r[:t - SUBLANES], jnp.where(row >= SUBLANES - n, 0.0, r[t - SUBLANES:])], axis=0)


def _convglu_specs(t):
    return [pl.BlockSpec((2, t, LANES), lambda j: (0, 0, j)),
            pl.BlockSpec((2, CONV_TAPS, LANES), lambda j: (0, 0, j)),
            pl.BlockSpec((2, 1, LANES), lambda j: (0, 0, j))]


def _convglu_fwd(u, cw, cb):
    _, t, fp = u.shape

    def body(u_ref, w_ref, b_ref, a_ref, c_ref):
        c = []
        for hf in range(2):
            uv, w = u_ref[hf].astype(F32), w_ref[hf]
            c.append(w[0:1] * _shift_down(uv, 2) + w[1:2] * _shift_down(uv, 1) + w[2:3] * uv + b_ref[hf])
            c_ref[hf] = c[hf].astype(c_ref.dtype)
        a_ref[...] = (_silu(c[0]) * c[1]).astype(a_ref.dtype)

    return pl.pallas_call(
        body, name="convglu_fwd",
        grid=(fp // LANES,),
        in_specs=_convglu_specs(t),
        out_specs=[pl.BlockSpec((t, LANES), lambda j: (0, j)), pl.BlockSpec((2, t, LANES), lambda j: (0, 0, j))],
        out_shape=[jax.ShapeDtypeStruct((t, fp), BF16), jax.ShapeDtypeStruct((2, t, fp), BF16)],
        compiler_params=_cparams(dimension_semantics=("parallel",)),
    )(u, cw, cb)


def _convglu_bwd(u, c, cw, da):
    _, t, fp = u.shape

    def body(u_ref, c_ref, w_ref, da_ref, du_ref, dw_ref, db_ref):
        gc, vc = c_ref[0].astype(F32), c_ref[1].astype(F32)
        sg = _sigmoid(gc)
        dav = da_ref[...].astype(F32)
        dcs = [dav * vc * (sg * (1.0 + gc * (1.0 - sg))), dav * (gc * sg)]
        for hf in range(2):
            dc, w, uv = dcs[hf], w_ref[hf], u_ref[hf].astype(F32)
            dc1, dc2 = _shift_up(dc, 1), _shift_up(dc, 2)
            du_ref[hf] = (w[2:3] * dc + w[1:2] * dc1 + w[0:1] * dc2).astype(du_ref.dtype)
            dw_ref[hf, 0:1, :] = jnp.sum(dc2 * uv, axis=0, keepdims=True)
            dw_ref[hf, 1:2, :] = jnp.sum(dc1 * uv, axis=0, keepdims=True)
            dw_ref[hf, 2:3, :] = jnp.sum(dc * uv, axis=0, keepdims=True)
            db_ref[hf] = jnp.sum(dc, axis=0, keepdims=True)

    pair, taps, bias = _convglu_specs(t)
    return pl.pallas_call(
        body, name="convglu_bwd",
        grid=(fp // LANES,),
        in_specs=[pair, pair, taps, pl.BlockSpec((t, LANES), lambda j: (0, j))],
        out_specs=[pair, taps, bias],
        out_shape=[jax.ShapeDtypeStruct((2, t, fp), BF16), jax.ShapeDtypeStruct((2, CONV_TAPS, fp), F32),
                   jax.ShapeDtypeStruct((2, 1, fp), F32)],
        compiler_params=_cparams(dimension_semantics=("parallel",)),
    )(u, c, cw, da)


def _local_step(x, target, mods, lb, small, pre_w, get_w, put_g, *, tb=512, attn_blk=512):
    t, d = x.shape
    nh = d // HEAD
    nb = NDEV
    wts = {}
    vec = lambda *names: [mods[n] for n in names]

    def ffn_fwd(h2, l):
        u = _mm_wblk(h2, wts[f"up{l}"], BF16, f"ffn{l}_up", gb=nb // 2, split=2, tm=512)
        a, c = _convglu_fwd(u, small[f"conv_w{l}"], small[f"conv_b{l}"])
        f = _mm(a, wts[f"down{l}"], "nn", F32, f"ffn{l}_down", tk=4096)
        return (u, c), a, f

    def ffn_bwd(df, h2, uc, a, l):
        u, c = uc
        da = _mm(df, wts[f"down{l}"], "nt", BF16, f"ffn{l}_down_dx", tn=1536)
        dwd = _mm(a, df, "tn", BF16, f"ffn{l}_down_dw", tm=768, tk=t)
        du, dcw, dcb = _convglu_bwd(u, c, small[f"conv_w{l}"], da)
        dh2 = _mm_wblk_dx(du, wts[f"up{l}"], BF16, f"ffn{l}_up_dx", k=d, gb=nb // 2, split=2, tm=1024)
        dwu = _mm_wblk_dw(h2, du, f"ffn{l}_up_dw", nb=nb, gb=1, split=2, tk=t)
        return dh2, dwu, dwd, dcw, dcb

    (h_a,) = _row_fwd(_f_mod, [(x, d, 0)], vec("sh1_0", "sc1_0"), [BF16], tb=tb, name="l0_mod1")
    wts.update(get_w("l0a", h_a))
    proj_a = _mm_wblk(h_a, wts["a_in"], F32, "a_in", gb=nb // 2)
    ypre, states = _hgrn2_fwd(proj_a, lb, small["a_norm_g"], tb)
    pre_w("l0b", ypre)
    wts.update(get_w("l0b", ypre))
    y_a = _mm(ypre, wts["a_out"], "nn", F32, "a_out")
    x1, h2_0 = _row_fwd(_f_res_mod, [(x, d, 0), (y_a, d, 0)], vec("g1_0", "sh2_0", "sc2_0"), [F32, BF16],
                        tb=tb, name="l0_res_mod2")
    wts.update(get_w("l0b_ffn", h2_0))
    u0, a0, f0 = ffn_fwd(h2_0, 0)
    x2, h_kv, h_q = _row_fwd(_f_res_mod2, [(x1, d, 0), (f0, d, 0)],
                             [mods["g2_0"] + pre_w("l1", f0)] + vec("kv_sh", "kv_sc", "sh1_1", "sc1_1"),
                             [F32, BF16, BF16], tb=tb, name="l0_res_kvmod_qmod")
    wts.update(get_w("l1", h_kv))
    proj_k = _mm(h_kv, wts["kv_k"], "nt", F32, "k_proj")
    v_b = _mm(h_kv, wts["kv_v"], "nt", BF16, "v_proj")
    proj_f = _mm(h_kv, wts["kv_f"], "nt", F32, "kv_fproj")
    f_logit_t = proj_f[:, :nh].T
    f_bias = small["kv_b_f"].reshape(nh, 1)
    f_t = _fgate_fwd(f_logit_t, f_bias)
    f_grp = f_t.reshape(ATTN_GROUPS, nh // ATTN_GROUPS, t).transpose(0, 2, 1)
    (k_n,) = _row_fwd(_f_knorm_aug, [(proj_k, HEAD, 0)] + [(piece, 1, 0) for piece in _split3(-f_t.T)],
                      [small["k_norm_g"]], [BF16], nsub=nh, tb=tb, name="k_norm")
    proj_q = _mm_wblk(h_q, wts["b_q"], F32, "b_q", gb=nb)
    (q_n,) = _row_fwd(_f_qnorm_aug, [(proj_q, HEAD, 0)], [small["q_norm_g"]], [BF16], nsub=nh, tb=tb,
                      name="q_norm")
    o_att, lse = _attn_fwd(q_n, k_n, v_b, f_grp, attn_blk)
    (z,) = _row_fwd(_f_outgate, [(o_att, HEAD, 0), (proj_q, HEAD, 1)], [], [BF16], nsub=nh, tb=tb, name="out_gate")
    y_b = _mm(z, wts["b_out"], "nn", F32, "b_out")
    x3, h2_1 = _row_fwd(_f_res_mod, [(x2, d, 0), (y_b, d, 0)], vec("g1_1", "sh2_1", "sc2_1"), [F32, BF16],
                        tb=tb, name="l1_res_mod2")
    u1, a1, f1 = ffn_fwd(h2_1, 1)
    loss, dx4, df1, dg2_1 = _loss_call(x3, f1, mods["g2_1"], target, tb)

    g = {}
    dmods = {"g2_1": dg2_1}
    dh2, g["up1"], g["down1"], g["conv_w1"], g["conv_b1"] = ffn_bwd(df1, h2_1, u1, a1, 1)
    (dx2, dy_b), (dmods["g1_1"], dmods["sh2_1"], dmods["sc2_1"]) = _row_bwd(
        _f_res_mod, [(x2, d, 0), (y_b, d, 0)], vec("g1_1", "sh2_1", "sc2_1"),
        [(dx4, d, 0), (dh2, d, 0)], [F32, BF16], tb=tb, name="l1_res_mod2_bwd")
    dz = _mm(dy_b, wts["b_out"], "nt", BF16, "b_out_dx")
    g["b_out"] = _mm(z, dy_b, "tn", BF16, "b_out_dw", tk=t)
    do_att, dog, delta = _outgate_bwd(o_att, proj_q, dz, tb)
    dq_n, dk_n, dv, dfc_q, dfr_k = _attn_bwd(q_n, k_n, v_b, f_grp, do_att, lse, delta, attn_blk)
    dproj_q, g["q_norm_g"] = _qnorm_bwd(proj_q, small["q_norm_g"], dq_n, dog, tb)
    dh_q = _mm_wblk_dx(dproj_q, wts["b_q"], BF16, "b_q_dx", k=d, gb=nb)
    g["b_q"] = _mm_wblk_dw(h_q, dproj_q, "b_q_dw", nb=nb, gb=nb // 4, tk=t)
    (dpk,), (g["k_norm_g"],) = _row_bwd(_f_knorm, [(proj_k, HEAD, 0)], [small["k_norm_g"]],
                                        [(dk_n, HEAD, 0)], [BF16], nsub=nh, tb=tb, name="k_norm_bwd")
    df_t = dfc_q.transpose(0, 2, 1).reshape(nh, t) + dfr_k.transpose(0, 2, 1, 3).reshape(nh, t)
    dflogit_t, g["kv_b_f"] = _fgate_bwd(f_logit_t, f_bias, df_t)
    dproj_f = jnp.pad(dflogit_t.T, ((0, 0), (0, LANES - nh))).astype(BF16)
    dh_kv = _mm(dpk, wts["kv_k"], "nn", BF16, "k_proj_dx")
    dh_kv_v = _mm(dv, wts["kv_v"], "nn", BF16, "v_proj_dx")
    dh_kv_f = _mm(dproj_f, wts["kv_f"], "nn", BF16, "kv_fproj_dx")
    g["kv_k"] = _mm(dpk, h_kv, "tn", BF16, "k_proj_dw", tk=t)
    g["kv_v"] = _mm(dv, h_kv, "tn", BF16, "v_proj_dw", tk=t)
    g["kv_f"] = _mm(dproj_f, h_kv, "tn", F32, "kv_fproj_dw", tk=1024)
    sent = put_g("l1", {n: g.pop(n) for n in ("b_out", "b_q", "kv_k", "kv_v", "kv_f", "up1", "down1")})
    (dx1, df0), (dmods["g2_0"], dmods["kv_sh"], dmods["kv_sc"], dmods["sh1_1"], dmods["sc1_1"]) = _row_bwd(
        _f_res_mod2, [(x1, d, 0), (f0, d, 0)], [mods["g2_0"] + sent] + vec("kv_sh", "kv_sc", "sh1_1", "sc1_1"),
        [(dx2, d, 0), (dh_kv, d, 0), (dh_q, d, 0)], [F32, BF16], tb=tb, name="l0_res_kvmod_qmod_bwd",
        cot_add=[(1, dh_kv_v), (1, dh_kv_f)])
    dh2, g["up0"], g["down0"], g["conv_w0"], g["conv_b0"] = ffn_bwd(df0, h2_0, u0, a0, 0)
    (dx0, dy_a), (dmods["g1_0"], dmods["sh2_0"], dmods["sc2_0"]) = _row_bwd(
        _f_res_mod, [(x, d, 0), (y_a, d, 0)], vec("g1_0", "sh2_0", "sc2_0"),
        [(dx1, d, 0), (dh2, d, 0)], [F32, BF16], tb=tb, name="l0_res_mod2_bwd")
    dypre = _mm(dy_a, wts["a_out"], "nt", BF16, "a_out_dx")
    g["a_out"] = _mm(ypre, dy_a, "tn", BF16, "a_out_dw", tk=t)
    sent = put_g("l0b", {n: g.pop(n) for n in ("a_out", "up0", "down0")})
    dproj_a, dlb, g["a_norm_g"] = _hgrn2_bwd(proj_a, lb + sent, small["a_norm_g"], states, dypre, tb)
    dh_a = _mm_wblk_dx(dproj_a, wts["a_in"], BF16, "a_in_dx", k=d, gb=nb, split=4, tm=512)
    put_g("l0a", {"a_in": _mm_wblk_dw(h_a, dproj_a, "a_in_dw", nb=nb, gb=1, split=4, tk=t)})
    (grad_x,), (dmods["sh1_0"], dmods["sc1_0"]) = _row_bwd(
        _f_mod, [(x, d, 0)], vec("sh1_0", "sc1_0"), [(dh_a, d, 0)], [F32], tb=tb, name="l0_mod1_bwd",
        add_to=(0, dx0))
    return loss, grad_x, dmods, dlb, g


def _position():
    return lax.axis_index("x"), lax.axis_index("y"), lax.axis_index("c")


_XCHG_EFFECT = pltpu.SideEffectType.DATAFLOW_SIDE_EFFECTING
ALL_PEERS = (1, 2, 3, 4, 5, 6, 7)
SAME_CORE = (2, 4, 6)


def _xchg_copies(src_refs, land_refs, send_sems, recv_sems, local_sems, scatter, rels):
    x, y, cc = _position()
    me = 4 * x + 2 * y + cc
    remote, local = [], []
    for a, (src, land) in enumerate(zip(src_refs, land_refs)):
        local.append(pltpu.make_async_copy(src.at[me] if scatter else src, land.at[me], local_sems.at[a]))
        for idx, rel in enumerate(rels):
            px = 1 - x if rel & 4 else x
            py = 1 - y if rel & 2 else y
            pc = 1 - cc if rel & 1 else cc
            k = len(rels) * a + idx
            remote.append(pltpu.make_async_remote_copy(
                src_ref=src.at[4 * px + 2 * py + pc] if scatter else src, dst_ref=land.at[me],
                send_sem=send_sems.at[k], recv_sem=recv_sems.at[k], device_id=(px, py, pc), device_id_type=_MESH))
    return remote, local


def _xchg_start(srcs, scatter, rels, after, name):
    n = len(srcs)
    lands = [lax.empty(s.shape if scatter else (NDEV, *s.shape), s.dtype) for s in srcs]

    def body(*refs):
        remote, local = _xchg_copies(refs[:n], refs[n:2 * n], *refs[2 * n + 1:2 * n + 4], scatter, rels)
        for cp in local + remote:
            cp.start()
        token = refs[-1]
        token[...] = jnp.zeros_like(token)

    hbm = pl.BlockSpec(memory_space=pltpu.HBM)
    sem = pl.BlockSpec(memory_space=pltpu.SEMAPHORE)
    out = pl.pallas_call(
        body, name=name,
        out_shape=(pltpu.SemaphoreType.DMA((len(rels) * n,)), pltpu.SemaphoreType.DMA((len(rels) * n,)),
                   pltpu.SemaphoreType.DMA((n,)),
                   *[pltpu.HBM(a.shape, a.dtype) for a in srcs + lands], jax.ShapeDtypeStruct((8, LANES), F32)),
        in_specs=[hbm] * (2 * n) + [pl.BlockSpec(memory_space=pl.ANY)],
        out_specs=(sem, sem, sem, *[hbm] * (2 * n), pl.BlockSpec(memory_space=pltpu.VMEM)),
        input_output_aliases={i: 3 + i for i in range(2 * n)},
        compiler_params=pltpu.CompilerParams(has_side_effects=_XCHG_EFFECT),
    )(*[pltpu.with_memory_space_constraint(a, pltpu.HBM) for a in srcs + lands], after)
    return out[:-1], out[-1][0, 0]


def _xchg_wait(handles, after, scatter, rels, name):
    n = (len(handles) - 3) // 2

    def body(*refs):
        remote, local = _xchg_copies(refs[:n], refs[n:2 * n], *refs[2 * n:2 * n + 3], scatter, rels)
        for cp in remote:
            cp.wait_send()
            cp.wait_recv()
        for cp in local:
            cp.wait()

    hbm = pl.BlockSpec(memory_space=pltpu.HBM)
    sem = pl.BlockSpec(memory_space=pltpu.SEMAPHORE)
    thru = list(handles[3:])
    afters = list(after) if isinstance(after, (list, tuple)) else [after]
    out = pl.pallas_call(
        body, name=name,
        out_shape=tuple(pltpu.HBM(a.shape, a.dtype) for a in thru),
        in_specs=[hbm] * (2 * n) + [sem, sem, sem] + [pl.BlockSpec(memory_space=pl.ANY)] * len(afters),
        out_specs=tuple([hbm] * (2 * n)),
        input_output_aliases={i: i for i in range(2 * n)},
        compiler_params=pltpu.CompilerParams(has_side_effects=_XCHG_EFFECT),
    )(*thru, *handles[:3], *afters)
    return list(out[n:])


def _sibling_copies(land_refs, send_sems, recv_sems):
    x, y, cc = _position()

    def copy(a, q, core):
        slot = land_refs[a].at[2 * q + core]
        return pltpu.make_async_remote_copy(
            src_ref=slot, dst_ref=slot, send_sem=send_sems.at[NCHIP * a + q], recv_sem=recv_sems.at[NCHIP * a + q],
            device_id=(x, y, 1 - cc), device_id_type=_MESH)

    pairs = [(a, q) for a in range(len(land_refs)) for q in range(NCHIP)]
    return [copy(a, q, cc) for a, q in pairs], [copy(a, q, 1 - cc) for a, q in pairs]


def _sibling_forward_start(lands, name, after=None):
    n = len(lands)
    deps = [] if after is None else [after]

    def body(*refs):
        sends, _ = _sibling_copies(refs[:n], refs[n + len(deps)], refs[n + len(deps) + 1])
        for cp in sends:
            cp.start()
        refs[-1][...] = jnp.zeros_like(refs[-1])

    hbm = pl.BlockSpec(memory_space=pltpu.HBM)
    sem = pl.BlockSpec(memory_space=pltpu.SEMAPHORE)
    out = pl.pallas_call(
        body, name=name,
        out_shape=(pltpu.SemaphoreType.DMA((NCHIP * n,)), pltpu.SemaphoreType.DMA((NCHIP * n,)),
                   *[pltpu.HBM(a.shape, a.dtype) for a in lands], jax.ShapeDtypeStruct((8, LANES), F32)),
        in_specs=[hbm] * n + [pl.BlockSpec(memory_space=pl.ANY)] * len(deps),
        out_specs=(sem, sem, *[hbm] * n, pl.BlockSpec(memory_space=pltpu.VMEM)),
        input_output_aliases={i: 2 + i for i in range(n)},
        compiler_params=pltpu.CompilerParams(has_side_effects=_XCHG_EFFECT),
    )(*lands, *deps)
    return out[:-1], out[-1][0, 0]


def _sibling_forward_wait(handles, after, name):
    n = len(handles) - 2

    def body(*refs):
        sends, arrivals = _sibling_copies(refs[:n], refs[n], refs[n + 1])
        for cp in sends:
            cp.wait_send()
        for cp in arrivals:
            cp.wait_recv()

    hbm = pl.BlockSpec(memory_space=pltpu.HBM)
    sem = pl.BlockSpec(memory_space=pltpu.SEMAPHORE)
    lands = list(handles[2:])
    return list(pl.pallas_call(
        body, name=name,
        out_shape=tuple(pltpu.HBM(a.shape, a.dtype) for a in lands),
        in_specs=[hbm] * n + [sem, sem, pl.BlockSpec(memory_space=pl.ANY)],
        out_specs=tuple([hbm] * n),
        input_output_aliases={i: i for i in range(n)},
        compiler_params=pltpu.CompilerParams(has_side_effects=_XCHG_EFFECT),
    )(*lands, *handles[:2], after))


def _slab_sum(slabs, name, tr=None):
    n, r, c = slabs.shape
    tr = r if tr is None else tr

    def body(s_ref, o_ref):
        acc = s_ref[0].astype(F32)
        for q in range(1, n):
            acc = acc + s_ref[q].astype(F32)
        o_ref[...] = acc

    return pl.pallas_call(body, name=name, grid=(r // tr,),
                          in_specs=[pl.BlockSpec((n, tr, c), lambda i: (0, i, 0))],
                          out_specs=pl.BlockSpec((tr, c), lambda i: (i, 0)),
                          out_shape=jax.ShapeDtypeStruct((r, c), F32),
                          compiler_params=_cparams(dimension_semantics=("parallel",)))(slabs)


def _ada_fwd(c_all, ada_w, kv_ada_w, logits):
    rows, d = c_all.shape
    n0, nkv = ada_w.shape[2], kv_ada_w.shape[1]

    def body(c_ref, w_ref, kw_ref, lg_ref, part_ref, cact_ref, lb_ref):
        ca = _silu(c_ref[...])
        cact_ref[...] = ca
        part_ref[:, 0:n0] = _bdot_raw(ca, w_ref[0], _NN)
        part_ref[:, n0:2 * n0] = _bdot_raw(ca, w_ref[1], _NN)
        part_ref[:, 2 * n0:2 * n0 + nkv] = _bdot_raw(ca, kw_ref[...], _NN)
        lb_ref[...] = _sigmoid(lg_ref[0:1, :] - lg_ref[1:2, :])

    vm = pl.BlockSpec(memory_space=pltpu.VMEM)
    return pl.pallas_call(
        body, name="ada_fwd", in_specs=[vm, vm, vm, vm], out_specs=[vm, vm, vm],
        out_shape=[jax.ShapeDtypeStruct((rows, 2 * n0 + nkv), F32), jax.ShapeDtypeStruct((rows, d), F32),
                   jax.ShapeDtypeStruct((1, d), F32)],
        compiler_params=_cparams(),
    )(c_all, ada_w, kv_ada_w, logits)


def _ada_bwd(c_act, dm0, dm1, dkv, lb, dlb):
    rows, d = c_act.shape

    def body(c_ref, d0_ref, d1_ref, dk_ref, lb_ref, dlb_ref, dw_ref, dkw_ref, dlg_ref):
        ca = c_ref[...]
        dw_ref[0] = _bdot_raw(ca, d0_ref[...], _TN)
        dw_ref[1] = _bdot_raw(ca, d1_ref[...], _TN)
        dkw_ref[...] = _bdot_raw(ca, dk_ref[...], _TN)
        lbv = lb_ref[...]
        dl0 = dlb_ref[...] * lbv * (1.0 - lbv)
        dlg_ref[0:1, :] = dl0
        dlg_ref[1:2, :] = -dl0

    vm = pl.BlockSpec(memory_space=pltpu.VMEM)
    return pl.pallas_call(
        body, name="ada_bwd", in_specs=[vm] * 6, out_specs=[vm, vm, vm],
        out_shape=[jax.ShapeDtypeStruct((2, d, dm0.shape[1]), F32), jax.ShapeDtypeStruct((d, dkv.shape[1]), F32),
                   jax.ShapeDtypeStruct((2, d), F32)],
        compiler_params=_cparams(),
    )(c_act, dm0, dm1, dkv, lb, dlb)


def _adamw(w, g, m, v, name, tr=512, after=None):
    r, c = w.shape
    tr = _divisor_tile(r, tr, unit=8)
    c1 = 1.0 - ADAM_B1 ** ADAM_STEP
    c2 = 1.0 - ADAM_B2 ** ADAM_STEP
    deps = [] if after is None else [after]

    def body(w_ref, g_ref, m_ref, v_ref, *rest):
        d_ref, mo_ref, vo_ref = rest[len(deps):]
        gv = g_ref[...]
        mn = ADAM_B1 * m_ref[...] + (1.0 - ADAM_B1) * gv
        vn = ADAM_B2 * v_ref[...] + (1.0 - ADAM_B2) * (gv * gv)
        d_ref[...] = -ADAM_LR * ((mn / c1) / (jnp.sqrt(vn / c2) + ADAM_EPS) + ADAM_WD * w_ref[...])
        mo_ref[...] = mn
        vo_ref[...] = vn

    spec = pl.BlockSpec((tr, c), lambda i: (i, 0))
    out = jax.ShapeDtypeStruct((r, c), F32)
    return pl.pallas_call(body, name=name, grid=(r // tr,),
                          in_specs=[spec] * 4 + [pl.BlockSpec(a.shape, lambda i: (0, 0)) for a in deps],
                          out_specs=[spec] * 3, out_shape=[out, out, out],
                          compiler_params=_cparams(dimension_semantics=("parallel",)))(w, g, m, v, *deps)


def _pad_rows(a, rows):
    return jnp.pad(a, ((0, rows - a.shape[0]), (0, 0)))


def _pack_small(parts, lanes=LANES, row_unit=8):
    flat = jnp.concatenate([p.reshape(-1).astype(F32) for p in parts])
    rows = _round_up(-(-flat.shape[0] // lanes), row_unit)
    return jnp.pad(flat, (0, rows * lanes - flat.shape[0])).reshape(rows, lanes)


def _unpack_small(flat, shapes):
    out, off = [], 0
    for s in shapes:
        n = 1
        for k in s:
            n *= k
        out.append(flat[off:off + n].reshape(s))
        off += n
    return out


def _pad_shard_cols(a, n_loc, n_pad):
    lead = a.shape[:-1]
    a = a.reshape(*lead, NDEV, n_loc)
    a = jnp.pad(a, [(0, 0)] * (len(lead) + 1) + [(0, n_pad - n_loc)])
    return a.reshape(*lead, NDEV * n_pad)


def _unpad_shard_cols(a, n_loc, n_pad):
    lead = a.shape[:-1]
    return a.reshape(*lead, NDEV, n_pad)[..., :n_loc].reshape(*lead, NDEV * n_loc)


def kernel(x, c, ada_w, ada_b, a_w_in, a_lb_logits, a_norm_g, a_w_out, kv_ada_w, kv_ada_b, kv_w, kv_b_f, k_norm_g, b_w_q, q_norm_g, b_w_out, ffn_w_up, ffn_conv_w, ffn_conv_b, ffn_w_down, loss_target, m_ada_w, m_ada_b, m_a_w_in, m_a_lb_logits, m_a_norm_g, m_a_w_out, m_kv_ada_w, m_kv_ada_b, m_kv_w, m_kv_b_f, m_k_norm_g, m_b_w_q, m_q_norm_g, m_b_w_out, m_ffn_w_up, m_ffn_conv_w, m_ffn_conv_b, m_ffn_w_down, v_ada_w, v_ada_b, v_a_w_in, v_a_lb_logits, v_a_norm_g, v_a_w_out, v_kv_ada_w, v_kv_ada_b, v_kv_w, v_kv_b_f, v_k_norm_g, v_b_w_q, v_q_norm_g, v_b_w_out, v_ffn_w_up, v_ffn_conv_w, v_ffn_conv_b, v_ffn_w_down):
    t, d = x.shape[1], x.shape[2]
    nh = d // HEAD
    ncw = ffn_w_up.shape[2]
    ncp = _round_up(ncw, LANES)
    two_f = ncw * NDEV
    ff = two_f // 2
    fp = ncp * NDEV // 2
    rd = ffn_w_down.shape[1]
    me = 4 * lax.axis_index("x") + 2 * lax.axis_index("y") + lax.axis_index("c")
    weights = dict(ada_w=ada_w, ada_b=ada_b, a_w_in=a_w_in, a_lb_logits=a_lb_logits, a_norm_g=a_norm_g,
                   a_w_out=a_w_out, kv_ada_w=kv_ada_w, kv_ada_b=kv_ada_b, kv_w=kv_w, kv_b_f=kv_b_f,
                   k_norm_g=k_norm_g, b_w_q=b_w_q, q_norm_g=q_norm_g, b_w_out=b_w_out, ffn_w_up=ffn_w_up,
                   ffn_conv_w=ffn_conv_w, ffn_conv_b=ffn_conv_b, ffn_w_down=ffn_w_down)
    m_in = dict(ada_w=m_ada_w, ada_b=m_ada_b, a_w_in=m_a_w_in, a_lb_logits=m_a_lb_logits, a_norm_g=m_a_norm_g,
                a_w_out=m_a_w_out, kv_ada_w=m_kv_ada_w, kv_ada_b=m_kv_ada_b, kv_w=m_kv_w, kv_b_f=m_kv_b_f,
                k_norm_g=m_k_norm_g, b_w_q=m_b_w_q, q_norm_g=m_q_norm_g, b_w_out=m_b_w_out, ffn_w_up=m_ffn_w_up,
                ffn_conv_w=m_ffn_conv_w, ffn_conv_b=m_ffn_conv_b, ffn_w_down=m_ffn_w_down)
    v_in = dict(ada_w=v_ada_w, ada_b=v_ada_b, a_w_in=v_a_w_in, a_lb_logits=v_a_lb_logits, a_norm_g=v_a_norm_g,
                a_w_out=v_a_w_out, kv_ada_w=v_kv_ada_w, kv_ada_b=v_kv_ada_b, kv_w=v_kv_w, kv_b_f=v_kv_b_f,
                k_norm_g=v_k_norm_g, b_w_q=v_b_w_q, q_norm_g=v_q_norm_g, b_w_out=v_b_w_out, ffn_w_up=v_ffn_w_up,
                ffn_conv_w=v_ffn_conv_w, ffn_conv_b=v_ffn_conv_b, ffn_w_down=v_ffn_w_down)
    order = list(weights)

    up_loc = jnp.pad(ffn_w_up, ((0, 0), (0, 0), (0, ncp - ncw))).astype(BF16)
    down_loc = ffn_w_down.astype(BF16)
    gather_names = {"l0b": ["a_out", "up0", "down0"], "l1": ["kv", "b_q", "b_out", "up1", "down1"]}
    forward_names = {"l0b": ["a_out"], "l0b_ffn": ["up0", "down0"], "l1": gather_names["l1"]}
    shards = {"a_out": a_w_out[0].astype(BF16), "up0": up_loc[0], "down0": down_loc[0], "kv": kv_w.T.astype(BF16),
              "b_q": b_w_q[0].astype(BF16), "b_out": b_w_out[0].astype(BF16), "up1": up_loc[1],
              "down1": down_loc[1]}
    pre = _pack_small([c, a_lb_logits, ffn_conv_w])
    in_flight = {}
    pre_flight, _ = _xchg_start([pre], False, ALL_PEERS, pre, "gather_small_inputs_start")
    (pre_all,) = _xchg_wait(pre_flight, pre, False, ALL_PEERS, "gather_small_inputs_wait")
    pre_all = pre_all.reshape(NDEV, -1)
    c_all = pre_all[:, :d]
    logits = pre_all[:, d:d + 2 * HEAD].reshape(NDEV, 2, HEAD).transpose(1, 0, 2).reshape(2, d)
    conv_w_full = pre_all[:, d + 2 * HEAD:d + 2 * HEAD + 2 * CONV_TAPS * ncw]
    conv_w_full = conv_w_full.reshape(NDEV, 2, CONV_TAPS, ncw).transpose(1, 2, 0, 3).reshape(2, CONV_TAPS, two_f)

    part, c_act, lb = _ada_fwd(_pad_rows(c_all, 2 * NDEV), ada_w, kv_ada_w, logits)
    part_flight, _ = _xchg_start([part[:NDEV]], False, ALL_PEERS, part, "gather_adaln_start")
    in_flight["l0a"], _ = _xchg_start([a_w_in[0].astype(BF16)], False, SAME_CORE, part_flight[-1], "gather_l0a_start")
    (part_all,) = _xchg_wait(part_flight, in_flight["l0a"][-1], False, ALL_PEERS, "gather_adaln_wait")
    forwarding = {}
    mine = lax.dynamic_index_in_dim(part_all, me, axis=1, keepdims=False)
    n0, nkv = ada_w.shape[2], kv_ada_w.shape[1]
    mod_names = ["sh1", "sc1", "g1", "sh2", "sc2", "g2"]
    mods = {}
    for l in range(2):
        row = mine[:, l * n0:(l + 1) * n0].reshape(-1) + ada_b[l]
        for k, nm in enumerate(mod_names):
            mods[f"{nm}_{l}"] = row[k * d:(k + 1) * d].reshape(1, d)
    kvrow = mine[:, 2 * n0:2 * n0 + nkv].reshape(-1) + kv_ada_b
    mods["kv_sh"], mods["kv_sc"] = kvrow[:d].reshape(1, d), kvrow[d:].reshape(1, d)

    def start_gather(grp, dep):
        srcs = [shards[n] for n in gather_names[grp]]
        in_flight[grp], started = _xchg_start(srcs, False, SAME_CORE, dep, f"gather_{grp}_start")
        return started

    zero = start_gather("l0b", part_all)
    mods["sh1_0"] = mods["sh1_0"] + zero

    small = {"a_norm_g": a_norm_g, "k_norm_g": k_norm_g.reshape(1, HEAD), "q_norm_g": q_norm_g, "kv_b_f": kv_b_f}
    for l in range(2):
        small[f"conv_w{l}"] = _pad_shard_cols(conv_w_full[l], ncw, ncp).reshape(CONV_TAPS, 2, fp).transpose(1, 0, 2)
        small[f"conv_b{l}"] = _pad_shard_cols(ffn_conv_b[l], ncw, ncp).reshape(2, 1, fp)

    def pre_w(grp, after):
        arrived = _xchg_wait(in_flight[grp], after, False, SAME_CORE, f"gather_{grp}_wait")
        if grp == "l0b":
            forwarding[grp], _ = _sibling_forward_start(arrived[:1], "gather_l0b_to_sibling_start")
            forwarding["l0b_ffn"], started = _sibling_forward_start(
                arrived[1:], "gather_l0b_ffn_to_sibling_start", after=forwarding[grp][-1])
            return started
        forwarding[grp], started = _sibling_forward_start(arrived, f"gather_{grp}_to_sibling_start")
        return started

    def get_w(grp, after):
        if grp == "l0a":
            arrived = _xchg_wait(in_flight["l0a"], after, False, SAME_CORE, "gather_l0a_wait")
            handles, _ = _sibling_forward_start(arrived, "gather_l0a_to_sibling_start")
            return {"a_in": _sibling_forward_wait(handles, after, "gather_l0a_to_sibling_wait")[0]}
        full = _sibling_forward_wait(forwarding[grp], after, f"gather_{grp}_to_sibling_wait")
        if grp == "l0b":
            started = start_gather("l1", full[0])
            full[0] = full[0] + started.astype(full[0].dtype)
        got = dict(zip(forward_names[grp], full))
        out = {}
        for n, a in got.items():
            if n in ("a_out", "b_out"):
                out[n] = a.reshape(d, d)
            elif n in ("down0", "down1"):
                dn = a.reshape(NCHIP, ff // NCHIP, d)
                out[n] = jnp.pad(dn, ((0, 0), (0, ncp - ncw), (0, 0))).reshape(fp, d)
            elif n == "kv":
                kv_t = a.reshape(NDEV * kv_w.shape[1], d)
                out["kv_k"], out["kv_v"] = kv_t[:d], kv_t[d:2 * d]
                out["kv_f"] = jnp.pad(kv_t[2 * d:], ((0, LANES - nh), (0, 0)))
            else:
                out[n] = a
        return out

    scatter_flight, g_last = {}, {}

    def put_g(grp, gr):
        if grp == "l0a":
            g_last.update(gr)
            return zero
        if grp == "l1":
            g_kvw = jnp.concatenate([gr["kv_k"], gr["kv_v"], gr["kv_f"][:nh].astype(BF16)], axis=0)
            arrs = {"kv_w": g_kvw.reshape(NDEV, kv_w.shape[1], d), "b_w_q": gr["b_q"],
                    "b_w_out": gr["b_out"].reshape(NDEV, d // NDEV, d), "up1": gr["up1"],
                    "down1": gr["down1"].reshape(NCHIP, ncp, d)[:, :ncw].reshape(NDEV, rd, d)}
        else:
            arrs = {"a_w_out": gr["a_out"].reshape(NDEV, d // NDEV, d), "up0": gr["up0"],
                    "down0": gr["down0"].reshape(NCHIP, ncp, d)[:, :ncw].reshape(NDEV, rd, d)}
        srcs = list(arrs.values())
        handles, sent = _xchg_start(srcs, True, ALL_PEERS, srcs[0], f"scatter_{grp}_start")
        scatter_flight[grp] = (list(arrs), handles)
        return sent

    loss_v, grad_x, dmods, dlb, g = _local_step(x[0], loss_target[0], mods, lb, small, pre_w, get_w, put_g)

    g_sum = {}
    for grp in ("l1", "l0b"):
        names, handles = scatter_flight[grp]
        for nm, a in zip(names, _xchg_wait(handles, grad_x, True, ALL_PEERS, f"scatter_{grp}_wait")):
            g_sum[nm] = _slab_sum(a, f"rs_slab_sum_{nm}")

    def conv_w_grad(a):
        return _unpad_shard_cols(a.transpose(1, 0, 2).reshape(CONV_TAPS, 2 * fp), ncw, ncp)

    def conv_b_grad(a):
        return _unpad_shard_cols(a.reshape(2 * fp), ncw, ncp)

    dmod_vec = [dmods[f"{nm}_{l}"] for l in range(2) for nm in mod_names] + [dmods["kv_sh"], dmods["kv_sc"]]
    post = _pack_small(dmod_vec + [dlb, g["a_norm_g"], g["k_norm_g"], g["q_norm_g"],
                                   jnp.pad(g["kv_b_f"].reshape(-1), (0, LANES - nh)),
                                   conv_w_grad(g["conv_w0"]), conv_w_grad(g["conv_w1"]),
                                   conv_b_grad(g["conv_b0"]), conv_b_grad(g["conv_b1"]), loss_v])
    post_flight, _ = _xchg_start([post], False, ALL_PEERS, post, "gather_small_grads_start")
    a_in_flight, a_in_sent = _xchg_start([g_last["a_in"]], True, ALL_PEERS, post_flight[-1], "scatter_l0a_start")
    a_in_sent = a_in_sent.reshape(1, 1)
    grads = {
        "a_w_out": g_sum["a_w_out"].reshape(a_w_out.shape),
        "kv_w": g_sum["kv_w"].T,
        "b_w_q": g_sum["b_w_q"].reshape(b_w_q.shape),
        "b_w_out": g_sum["b_w_out"].reshape(b_w_out.shape),
        "ffn_w_up": jnp.stack([g_sum["up0"][:, :ncw], g_sum["up1"][:, :ncw]]),
        "ffn_w_down": jnp.stack([g_sum["down0"], g_sum["down1"]]),
    }
    delta, new_m, new_v = {}, {}, {}

    def adamw_matrix(n):
        shp = weights[n].shape
        two_d = lambda a: a.reshape(-1, shp[-1])
        dl, mn, vn = _adamw(two_d(weights[n]), two_d(grads[n]), two_d(m_in[n]), two_d(v_in[n]), f"adamw_{n}",
                            after=a_in_sent)
        delta[n], new_m[n], new_v[n] = dl.reshape(shp), mn.reshape(shp), vn.reshape(shp)

    for n in grads:
        adamw_matrix(n)
    (post_all,) = _xchg_wait(post_flight, [new_v[n] for n in grads], False, ALL_PEERS, "gather_small_grads_wait")
    tot = _slab_sum(post_all, "small_grad_sum").reshape(-1)
    nmod = 14 * d
    (t_mod, t_lb, t_ang, t_kng, t_qng, t_bf, t_cw, t_cb, t_loss) = _unpack_small(
        tot, [(nmod,), (1, d), (1, HEAD), (HEAD,), (1, HEAD), (LANES,), (2, CONV_TAPS, two_f), (2, two_f),
              (LANES,)])
    loss = t_loss[0]
    dm_all = post_all.reshape(NDEV, -1)[:, :nmod]
    dm0 = lax.dynamic_slice_in_dim(dm_all[:, :6 * d], me * n0, n0, axis=1)
    dm1 = lax.dynamic_slice_in_dim(dm_all[:, 6 * d:12 * d], me * n0, n0, axis=1)
    dkv = lax.dynamic_slice_in_dim(dm_all[:, 12 * d:], me * nkv, nkv, axis=1)
    g_ada_w, g_kv_ada_w, g_logits = _ada_bwd(c_act, _pad_rows(dm0, 2 * NDEV), _pad_rows(dm1, 2 * NDEV),
                                              _pad_rows(dkv, 2 * NDEV), lb, t_lb)

    grads.update({
        "ada_w": g_ada_w,
        "ada_b": t_mod[:12 * d].reshape(2, 6 * d),
        "a_lb_logits": lax.dynamic_slice_in_dim(g_logits, me * HEAD, HEAD, axis=1),
        "a_norm_g": t_ang,
        "kv_ada_w": g_kv_ada_w,
        "kv_ada_b": t_mod[12 * d:],
        "kv_b_f": t_bf[:nh],
        "k_norm_g": t_kng,
        "q_norm_g": t_qng,
        "ffn_conv_w": lax.dynamic_slice_in_dim(t_cw, me * ncw, ncw, axis=2),
        "ffn_conv_b": t_cb,
    })

    small_adam = [n for n in order if n not in delta and n not in ("ada_w", "kv_ada_w", "a_w_in")]
    packs = [_pack_small([src[n] for n in small_adam]) for src in (weights, grads, m_in, v_in)]
    outs = _adamw(*packs, "adamw_small", tr=packs[0].shape[0])
    shapes = [weights[n].shape for n in small_adam]
    for dst, o in zip((delta, new_m, new_v), outs):
        for n, a in zip(small_adam, _unpack_small(o.reshape(-1), shapes)):
            dst[n] = a
    adamw_matrix("ada_w")
    adamw_matrix("kv_ada_w")
    (landed,) = _xchg_wait(a_in_flight, new_v["kv_ada_w"], True, ALL_PEERS, "scatter_l0a_wait")
    grads["a_w_in"] = _slab_sum(landed, "rs_slab_sum_a_w_in").reshape(a_w_in.shape)
    adamw_matrix("a_w_in")

    return (loss, grad_x.reshape(x.shape), *[grads[n] for n in order], *[delta[n] for n in order],
            *[new_m[n] for n in order], *[new_v[n] for n in order])
```

```python
import functools

import jax
import jax.numpy as jnp
from jax import lax
from jax.experimental import pallas as pl
from jax.experimental.pallas import tpu as pltpu

F32 = jnp.float32
BF16 = jnp.bfloat16

NDEV = 8
NCHIP = 4
HEAD = 128
A_CHUNK = 64
CONV_TAPS = 3
EPS = 1e-6
NEG_INF = -1e30
LANES = 128
VMEM_LIMIT = 48 * 1024 * 1024

ADAM_LR = 0.001
ADAM_B1 = 0.9
ADAM_B2 = 0.999
ADAM_EPS = 1e-08
ADAM_WD = 0.01
ADAM_STEP = 10

_NN = (((1,), (0,)), ((), ()))
_NT = (((1,), (1,)), ((), ()))
_TN = (((0,), (0,)), ((), ()))
_MESH = pl.DeviceIdType.MESH


def _cparams(**kw):
    return pltpu.CompilerParams(vmem_limit_bytes=VMEM_LIMIT, **kw)


def _divisor_tile(n, pref, unit=LANES):
    if n <= pref:
        return n
    best = None
    for t in range(unit, pref + 1, unit):
        if n % t == 0:
            best = t
    assert best is not None, (n, pref)
    return best


def _round_up(n, unit):
    return -(-n // unit) * unit


def _bdot_raw(a, b, dims):
    return lax.dot_general(a.astype(BF16), b.astype(BF16), dims, preferred_element_type=F32)


@jax.custom_vjp
def _dot_nn(a, b):
    return _bdot_raw(a, b, _NN)


@jax.custom_vjp
def _dot_nt(a, b):
    return _bdot_raw(a, b, _NT)


@jax.custom_vjp
def _dot_tn(a, b):
    return _bdot_raw(a, b, _TN)


_dot_nn.defvjp(lambda a, b: (_bdot_raw(a, b, _NN), (a, b)),
               lambda r, g: (_dot_nt(g, r[1]), _dot_tn(r[0], g)))
_dot_nt.defvjp(lambda a, b: (_bdot_raw(a, b, _NT), (a, b)),
               lambda r, g: (_dot_nn(g, r[1]), _dot_tn(g, r[0])))
_dot_tn.defvjp(lambda a, b: (_bdot_raw(a, b, _TN), (a, b)),
               lambda r, g: (_dot_nt(r[1], g), _dot_nn(r[0], g)))


def _f32dot(a, b):
    return lax.dot_general(a, b, _NN, precision=lax.Precision.HIGHEST, preferred_element_type=F32)


def _sigmoid(x):
    return jax.nn.sigmoid(x)


def _silu(x):
    return x * jax.nn.sigmoid(x)


def _rms(x):
    return x * lax.rsqrt(jnp.mean(x * x, axis=-1, keepdims=True) + EPS)


def _modulate(x, sh, sc):
    return _rms(x) * (1.0 + sc) + sh


def _mm_call(a, b, dims, a_spec, b_spec, o_spec, o_shape, grid, acc_tile, name):
    nk = grid[2]

    def body(a_ref, b_ref, o_ref, *acc):
        p = lax.dot_general(a_ref[...].astype(BF16), b_ref[...].astype(BF16), dims,
                            preferred_element_type=F32)
        if nk == 1:
            o_ref[...] = p.astype(o_ref.dtype)
        else:
            kk = pl.program_id(2)

            @pl.when(kk == 0)
            def _():
                acc[0][...] = p

            @pl.when(kk > 0)
            def _():
                acc[0][...] += p

            @pl.when(kk == nk - 1)
            def _():
                o_ref[...] = acc[0][...].astype(o_ref.dtype)

    return pl.pallas_call(
        body, name=name, grid=grid, in_specs=[a_spec, b_spec], out_specs=o_spec, out_shape=o_shape,
        scratch_shapes=[pltpu.VMEM(acc_tile, F32)] if nk > 1 else [],
        compiler_params=_cparams(dimension_semantics=("parallel", "parallel", "arbitrary")),
    )(a, b)


def _mm(a, b, mode, out_dtype, name, tm=1024, tn=1024, tk=2048):
    if mode == "nn":
        (m, k), (k2, n) = a.shape, b.shape
    elif mode == "nt":
        (m, k), (n, k2) = a.shape, b.shape
    else:
        (k, m), (k2, n) = a.shape, b.shape
    assert k == k2, (a.shape, b.shape, mode)
    tm, tn, tk = _divisor_tile(m, tm), _divisor_tile(n, tn), _divisor_tile(k, tk)
    if mode == "tn":
        a_spec = pl.BlockSpec((tk, tm), lambda i, j, kk: (kk, i))
    else:
        a_spec = pl.BlockSpec((tm, tk), lambda i, j, kk: (i, kk))
    if mode == "nt":
        b_spec = pl.BlockSpec((tn, tk), lambda i, j, kk: (j, kk))
    else:
        b_spec = pl.BlockSpec((tk, tn), lambda i, j, kk: (kk, j))
    return _mm_call(a, b, {"nn": _NN, "nt": _NT, "tn": _TN}[mode], a_spec, b_spec,
                    pl.BlockSpec((tm, tn), lambda i, j, kk: (i, j)), jax.ShapeDtypeStruct((m, n), out_dtype),
                    (m // tm, n // tn, k // tk), (tm, tn), name)


def _wblk_act_spec(rows, gb, nl, split, nb, row_axis, blk_axis):
    if split == 1:
        return pl.BlockSpec((rows, gb * nl), lambda *g: (g[row_axis], g[blk_axis]))
    groups = nb // split // gb
    return pl.BlockSpec((None, rows, gb * nl),
                        lambda *g: (g[blk_axis] // groups, g[row_axis], g[blk_axis] % groups))


def _mm_wblk(a, wb, out_dtype, name, *, gb, row_off=0, split=1, tm=1024):
    m, k = a.shape
    nb, _, nl = wb.shape
    assert (nb // split) % gb == 0
    tm = _divisor_tile(m, tm)

    def body(a_ref, b_ref, o_ref):
        av = a_ref[...].astype(BF16)
        for s in range(gb):
            o_ref[:, s * nl:(s + 1) * nl] = lax.dot_general(
                av, b_ref[s].astype(BF16), _NN, preferred_element_type=F32).astype(o_ref.dtype)

    o_shape = (m, nb * nl) if split == 1 else (split, m, nb // split * nl)
    return pl.pallas_call(
        body, name=name, grid=(nb // gb, m // tm),
        in_specs=[pl.BlockSpec((tm, k), lambda j, i: (i, 0)),
                  pl.BlockSpec((gb, k, nl), lambda j, i: (j, row_off, 0))],
        out_specs=_wblk_act_spec(tm, gb, nl, split, nb, 1, 0),
        out_shape=jax.ShapeDtypeStruct(o_shape, out_dtype),
        compiler_params=_cparams(dimension_semantics=("parallel", "parallel")),
    )(a, wb)


def _mm_wblk_dx(dy, wb, out_dtype, name, *, k, gb, row_off=0, split=1, tm=1024):
    nb, _, nl = wb.shape
    m = dy.shape[-2]
    tm = _divisor_tile(m, tm)
    nk = nb // gb
    per = nb // split
    whole = split > 1 and gb == nb
    assert whole or per % gb == 0

    def body(a_ref, b_ref, o_ref, *acc):
        p = None
        for s in range(gb):
            a_blk = a_ref[s // per, :, (s % per) * nl:(s % per + 1) * nl] if whole else a_ref[:, s * nl:(s + 1) * nl]
            q = lax.dot_general(a_blk.astype(BF16), b_ref[s].astype(BF16), _NT, preferred_element_type=F32)
            p = q if p is None else p + q
        if nk == 1:
            o_ref[...] = p.astype(o_ref.dtype)
        else:
            kk = pl.program_id(1)

            @pl.when(kk == 0)
            def _():
                acc[0][...] = p

            @pl.when(kk > 0)
            def _():
                acc[0][...] += p

            @pl.when(kk == nk - 1)
            def _():
                o_ref[...] = acc[0][...].astype(o_ref.dtype)

    return pl.pallas_call(
        body, name=name, grid=(m // tm, nk),
        in_specs=[pl.BlockSpec((split, tm, per * nl), lambda i, kk: (0, i, 0)) if whole
                  else _wblk_act_spec(tm, gb, nl, split, nb, 0, 1),
                  pl.BlockSpec((gb, k, nl), lambda i, kk: (kk, row_off, 0))],
        out_specs=pl.BlockSpec((tm, k), lambda i, kk: (i, 0)),
        out_shape=jax.ShapeDtypeStruct((m, k), out_dtype),
        scratch_shapes=[pltpu.VMEM((tm, k), F32)] if nk > 1 else [],
        compiler_params=_cparams(dimension_semantics=("parallel", "arbitrary")),
    )(dy, wb)


def _mm_wblk_dw(x, dy, name, *, nb, gb, split=1, tk=1024):
    t, k = x.shape
    assert (nb // split) % gb == 0
    nl = dy.shape[-1] * split // nb
    tk = _divisor_tile(t, tk)
    nk = t // tk

    def body(a_ref, b_ref, o_ref, *acc):
        kk = pl.program_id(1)
        av = a_ref[...].astype(BF16)
        for s in range(gb):
            p = lax.dot_general(av, b_ref[:, s * nl:(s + 1) * nl].astype(BF16), _TN, preferred_element_type=F32)
            if nk == 1:
                o_ref[s] = p.astype(o_ref.dtype)
                continue

            @pl.when(kk == 0)
            def _():
                acc[0][s] = p

            @pl.when(kk > 0)
            def _():
                acc[0][s] += p

        if nk > 1:
            @pl.when(kk == nk - 1)
            def _():
                o_ref[...] = acc[0][...].astype(o_ref.dtype)

    return pl.pallas_call(
        body, name=name, grid=(nb // gb, nk),
        in_specs=[pl.BlockSpec((tk, k), lambda j, kk: (kk, 0)), _wblk_act_spec(tk, gb, nl, split, nb, 1, 0)],
        out_specs=pl.BlockSpec((gb, k, nl), lambda j, kk: (j, 0, 0)),
        out_shape=jax.ShapeDtypeStruct((nb, k, nl), BF16),
        scratch_shapes=[pltpu.VMEM((gb, k, nl), F32)] if nk > 1 else [],
        compiler_params=_cparams(dimension_semantics=("parallel", "arbitrary")),
    )(x, dy)


def _row_specs(rows, tb, nsub):
    return [pl.BlockSpec((tb, nsub * cw), functools.partial(lambda i, off: (i, off), off=off))
            for (_, cw, off) in rows]


def _vec_specs(params):
    return [pl.BlockSpec(p.shape, lambda i: (0, 0)) for p in params]


def _row_fwd(f, rows, params, out_dtypes, *, nsub=1, tb, name):
    t = rows[0][0].shape[0]
    tb = min(tb, t)
    n_r, n_p = len(rows), len(params)
    blk = [jax.ShapeDtypeStruct((tb, cw), F32) for (_, cw, _) in rows]
    blk += [jax.ShapeDtypeStruct(p.shape, F32) for p in params]
    out_avals = jax.eval_shape(f, *blk)

    def body(*refs):
        pv = [r[...] for r in refs[n_r:n_r + n_p]]
        for s in range(nsub):
            vals = [r[:, s * cw:(s + 1) * cw].astype(F32) for r, (_, cw, _) in zip(refs[:n_r], rows)]
            outs = f(*vals, *pv)
            for o_ref, o in zip(refs[n_r + n_p:], outs):
                w = o.shape[1]
                o_ref[:, s * w:(s + 1) * w] = o.astype(o_ref.dtype)

    return pl.pallas_call(
        body, name=name,
        grid=(t // tb,),
        in_specs=_row_specs(rows, tb, nsub) + _vec_specs(params),
        out_specs=[pl.BlockSpec((tb, nsub * av.shape[1]), lambda i: (i, 0)) for av in out_avals],
        out_shape=[jax.ShapeDtypeStruct((t, nsub * av.shape[1]), dt) for av, dt in zip(out_avals, out_dtypes)],
        compiler_params=_cparams(dimension_semantics=("parallel",)),
    )(*[r[0] for r in rows], *params)


def _row_bwd(f, rows, params, cots, row_grad_dtypes, *, nsub=1, tb, name, add_to=None, cot_add=None):
    t = rows[0][0].shape[0]
    tb = min(tb, t)
    n_r, n_p, n_c = len(rows), len(params), len(cots)
    want = [j for j in range(n_r) if row_grad_dtypes[j] is not None]
    cot_add = cot_add or []
    extra = [] if add_to is None else [(add_to[1], rows[add_to[0]][1], 0)]
    n_add_to = len(extra)
    extra += [(arr, cots[ci][1], 0) for ci, arr in cot_add]

    def body(*refs):
        i = pl.program_id(0)
        r_in, p_in = refs[:n_r], refs[n_r:n_r + n_p]
        c_in = refs[n_r + n_p:n_r + n_p + n_c]
        e_in = refs[n_r + n_p + n_c:n_r + n_p + n_c + len(extra)]
        outs = refs[n_r + n_p + n_c + len(extra):]
        pv = [r[...] for r in p_in]
        psum = [None] * n_p
        for s in range(nsub):
            vals = [r[:, s * cw:(s + 1) * cw].astype(F32) for r, (_, cw, _) in zip(r_in, rows)]
            cvals = [r[:, s * cw:(s + 1) * cw].astype(F32) for r, (_, cw, _) in zip(c_in, cots)]
            for (ci, _), e_ref in zip(cot_add, e_in[n_add_to:]):
                cw = cots[ci][1]
                cvals[ci] = cvals[ci] + e_ref[:, s * cw:(s + 1) * cw].astype(F32)
            _, vjp_fn = jax.vjp(f, *vals, *pv)
            grads = vjp_fn(tuple(cvals))
            for o_ref, jr in zip(outs[:len(want)], want):
                cw = rows[jr][1]
                gr = grads[jr]
                if add_to is not None and jr == add_to[0]:
                    gr = gr + e_in[0][:, s * cw:(s + 1) * cw]
                o_ref[:, s * cw:(s + 1) * cw] = gr.astype(o_ref.dtype)
            for jp in range(n_p):
                psum[jp] = grads[n_r + jp] if psum[jp] is None else psum[jp] + grads[n_r + jp]
        for o_ref, g in zip(outs[len(want):], psum):
            @pl.when(i == 0)
            def _():
                o_ref[...] = g

            @pl.when(i > 0)
            def _():
                o_ref[...] += g

    out_specs = [pl.BlockSpec((tb, nsub * rows[jr][1]), lambda i: (i, 0)) for jr in want]
    out_shape = [jax.ShapeDtypeStruct((t, nsub * rows[jr][1]), row_grad_dtypes[jr]) for jr in want]
    out_specs += _vec_specs(params)
    out_shape += [jax.ShapeDtypeStruct(p.shape, F32) for p in params]
    res = pl.pallas_call(
        body, name=name,
        grid=(t // tb,),
        in_specs=_row_specs(rows, tb, nsub) + _vec_specs(params) + _row_specs(cots, tb, nsub)
        + _row_specs(extra, tb, nsub),
        out_specs=out_specs, out_shape=out_shape,
        compiler_params=_cparams(dimension_semantics=("arbitrary",)),
    )(*[r[0] for r in rows], *params, *[c[0] for c in cots], *[e[0] for e in extra])
    return res[:len(want)], res[len(want):]


def _f_mod(x, sh, sc):
    return (_modulate(x, sh, sc),)


def _f_res_mod(x, y, g, sh, sc):
    x1 = x + g * y
    return x1, _modulate(x1, sh, sc)


def _f_res_mod2(x, y, g, sh_a, sc_a, sh_b, sc_b):
    x1 = x + g * y
    return x1, _modulate(x1, sh_a, sc_a), _modulate(x1, sh_b, sc_b)


def _f_qnorm(p, g):
    return (_rms(p) * g * (HEAD ** -0.5),)


def _f_knorm(p, g):
    return (_rms(p) * g,)


def _f_qnorm_aug(p, g):
    lane = lax.broadcasted_iota(jnp.int32, p.shape, 1)
    return (jnp.concatenate([_rms(p) * g * (HEAD ** -0.5), jnp.where(lane < 3, 1.0, 0.0)], axis=1),)


def _f_knorm_aug(p, c0, c1, c2, g):
    lane = lax.broadcasted_iota(jnp.int32, p.shape, 1)
    aug = jnp.where(lane == 0, c0, jnp.where(lane == 1, c1, jnp.where(lane == 2, c2, 0.0)))
    return (jnp.concatenate([_rms(p) * g, aug], axis=1),)


def _split3(a):
    round_bf16 = lambda v: lax.reduce_precision(v, exponent_bits=8, mantissa_bits=7)
    hi = round_bf16(a)
    mid = round_bf16(a - hi)
    lo = round_bf16(a - hi - mid)
    return hi.astype(BF16), mid.astype(BF16), lo.astype(BF16)


def _f_outgate(o, og):
    return (o * _sigmoid(og),)


def _loss_call(x3, f, g2, target, tb):
    t, d = x3.shape
    tb = min(tb, t)

    def body(x_ref, f_ref, g_ref, t_ref, loss_ref, dx_ref, df_ref, dg_ref):
        i = pl.program_id(0)
        fv = f_ref[...]
        g = g_ref[...]
        e = x_ref[...] + g * fv - t_ref[...]
        dx = e * (1.0 / d)
        part = 0.5 * jnp.sum(jnp.sum(e * dx, axis=1, keepdims=True), axis=0, keepdims=True)
        dx_ref[...] = dx
        df_ref[...] = (g * dx).astype(df_ref.dtype)
        dg = jnp.sum(dx * fv, axis=0, keepdims=True)

        @pl.when(i == 0)
        def _():
            loss_ref[...] = jnp.broadcast_to(part, loss_ref.shape)
            dg_ref[...] = dg

        @pl.when(i > 0)
        def _():
            loss_ref[...] += jnp.broadcast_to(part, loss_ref.shape)
            dg_ref[...] += dg

    row = pl.BlockSpec((tb, d), lambda i: (i, 0))
    vec = pl.BlockSpec((1, d), lambda i: (0, 0))
    return pl.pallas_call(
        body, name="loss_head",
        grid=(t // tb,),
        in_specs=[row, row, vec, row],
        out_specs=[pl.BlockSpec((1, LANES), lambda i: (0, 0)), row, row, vec],
        out_shape=[jax.ShapeDtypeStruct((1, LANES), F32), jax.ShapeDtypeStruct((t, d), F32),
                   jax.ShapeDtypeStruct((t, d), BF16), jax.ShapeDtypeStruct((1, d), F32)],
        compiler_params=_cparams(dimension_semantics=("arbitrary",)),
    )(x3, f, g2, target)


def _hg_mask(tb):
    br = lax.broadcasted_iota(jnp.int32, (tb, tb), 0)
    bs = lax.broadcasted_iota(jnp.int32, (tb, tb), 1)
    return jnp.logical_and(br // A_CHUNK == bs // A_CHUNK, bs <= br).astype(F32)


def _hg_consts(mask):
    c = A_CHUNK
    r = lax.broadcasted_iota(jnp.int32, (c, c), 0)
    s = lax.broadcasted_iota(jnp.int32, (c, c), 1)
    return (s <= r).astype(F32), (r <= s).astype(F32), mask > 0.5


def _chunk_apply(mat, x):
    c = mat.shape[0]
    return jnp.concatenate([_f32dot(mat, x[i * c:(i + 1) * c]) for i in range(x.shape[0] // c)], axis=0)


@jax.custom_vjp
def _chunk_cumsum(x, tri, tri_t):
    return _chunk_apply(tri, x)


_chunk_cumsum.defvjp(lambda x, tri, tri_t: (_chunk_apply(tri, x), (tri, tri_t)),
                     lambda r, g: (_chunk_apply(r[1], g), jnp.zeros_like(r[0]), jnp.zeros_like(r[1])))


def _per_chunk(a, b, dims):
    return jnp.stack([_bdot_raw(a[i], b[i], dims) for i in range(a.shape[0])])


@jax.custom_vjp
def _chunk_tn(a, b):
    return _per_chunk(a, b, _TN)


@jax.custom_vjp
def _chunk_nt(a, b):
    return _per_chunk(a, b, _NT)


@jax.custom_vjp
def _chunk_nn(a, b):
    return _per_chunk(a, b, _NN)


_chunk_tn.defvjp(lambda a, b: (_per_chunk(a, b, _TN), (a, b)),
                 lambda r, g: (_chunk_nt(r[1], g), _chunk_nn(r[0], g)))
_chunk_nt.defvjp(lambda a, b: (_per_chunk(a, b, _NT), (a, b)),
                 lambda r, g: (_chunk_nn(g, r[1]), _chunk_tn(g, r[0])))
_chunk_nn.defvjp(lambda a, b: (_per_chunk(a, b, _NN), (a, b)),
                 lambda r, g: (_chunk_nt(g, r[1]), _chunk_tn(r[0], g)))


def _scan_states(decay, m, st):
    sts = []
    for i in range(m.shape[0]):
        sts.append(st)
        st = st * decay[i] + m[i]
    return jnp.stack(sts), st


@jax.custom_vjp
def _state_scan(decay, m, st):
    return _scan_states(decay, m, st)


def _state_scan_fwd(decay, m, st):
    sts, st_out = _scan_states(decay, m, st)
    return (sts, st_out), (decay, sts)


def _state_scan_bwd(res, cts):
    decay, sts = res
    d_sts, g = cts
    d_decay, d_m = [], []
    for i in range(sts.shape[0] - 1, -1, -1):
        d_m.append(g)
        d_decay.append(jnp.sum(g * sts[i], axis=0, keepdims=True))
        g = g * decay[i] + d_sts[i]
    return jnp.stack(d_decay[::-1]), jnp.stack(d_m[::-1]), g


_state_scan.defvjp(_state_scan_fwd, _state_scan_bwd)


def _hg_block(qp, fp, ip, gp, lb, ng, st, tri, tri_t, bd_causal):
    tb = qp.shape[0]
    c = A_CHUNK
    n = tb // c
    q = _silu(qp)
    fg = lb + (1.0 - lb) * _sigmoid(fp)
    logf = jnp.log(fg)
    k = 1.0 - fg
    b3 = _chunk_cumsum(logf, tri, tri_t).reshape(n, c, HEAD)
    pos = lax.broadcasted_iota(jnp.int32, (1, c, 1), 1)
    b_mid = lax.stop_gradient(jnp.sum(jnp.where(pos == c // 2, b3, 0.0), axis=1, keepdims=True))
    b_last = jnp.sum(jnp.where(pos == c - 1, b3, 0.0), axis=1, keepdims=True)
    q3, k3, v3 = q.reshape(n, c, HEAD), k.reshape(n, c, HEAD), ip.reshape(n, c, HEAD)
    scores = _dot_nt((q3 * jnp.exp(b3 - b_mid)).reshape(tb, HEAD), (k3 * jnp.exp(b_mid - b3)).reshape(tb, HEAD))
    o_intra = _dot_nn(jnp.where(bd_causal, scores, 0.0), ip)
    states, st_new = _state_scan(jnp.exp(b_last), _chunk_tn(v3, k3 * jnp.exp(b_last - b3)), st)
    o = o_intra + _chunk_nt(q3 * jnp.exp(b3), states).reshape(tb, HEAD)
    y = _rms(o) * ng * _silu(gp)
    return y, st_new


HG_HEADS = 2


def _hg_specs(tb, nh, rev_nb=None):
    wide = HG_HEADS * HEAD
    per = nh // HG_HEADS

    def row(part):
        if rev_nb is None:
            return pl.BlockSpec((tb, wide), functools.partial(lambda h, i, off: (i, off + h), off=part * per))
        return pl.BlockSpec((tb, wide),
                            functools.partial(lambda h, i, off: (rev_nb - 1 - i, off + h), off=part * per))
    return [row(0), row(1), row(2), row(3),
            pl.BlockSpec((1, wide), lambda h, i: (0, h)), pl.BlockSpec((1, HEAD), lambda h, i: (0, 0)),
            pl.BlockSpec((tb, tb), lambda h, i: (0, 0))]


def _hgrn2_fwd(proj, lb, ng, tb):
    t = proj.shape[0]
    nh = proj.shape[1] // (4 * HEAD)
    tb = min(tb, t)
    nb = t // tb
    wide = HG_HEADS * HEAD

    def body(q_ref, f_ref, i_ref, g_ref, lb_ref, ng_ref, mask_ref, y_ref, s_ref, st_ref):
        i = pl.program_id(1)

        @pl.when(i == 0)
        def _():
            st_ref[...] = jnp.zeros_like(st_ref)

        consts = _hg_consts(mask_ref[...])
        for p in range(HG_HEADS):
            cs = slice(p * HEAD, (p + 1) * HEAD)
            st = st_ref[p]
            s_ref[p, 0] = st
            y, st_new = _hg_block(q_ref[:, cs], f_ref[:, cs], i_ref[:, cs], g_ref[:, cs], lb_ref[:, cs],
                                  ng_ref[...], st, *consts)
            y_ref[:, cs] = y.astype(y_ref.dtype)
            st_ref[p] = st_new

    return pl.pallas_call(
        body, name="hgrn2_fwd",
        grid=(nh // HG_HEADS, nb),
        in_specs=_hg_specs(tb, nh),
        out_specs=[pl.BlockSpec((tb, wide), lambda h, i: (i, h)),
                   pl.BlockSpec((HG_HEADS, 1, HEAD, HEAD), lambda h, i: (h, i, 0, 0))],
        out_shape=[jax.ShapeDtypeStruct((t, nh * HEAD), BF16),
                   jax.ShapeDtypeStruct((nh, nb, HEAD, HEAD), F32)],
        scratch_shapes=[pltpu.VMEM((HG_HEADS, HEAD, HEAD), F32)],
        compiler_params=_cparams(dimension_semantics=("parallel", "arbitrary")),
    )(proj, proj, proj, proj, lb, ng, _hg_mask(tb))


def _hgrn2_bwd(proj, lb, ng, states, dy, tb):
    t = proj.shape[0]
    nh = proj.shape[1] // (4 * HEAD)
    tb = min(tb, t)
    nb = t // tb
    wide = HG_HEADS * HEAD

    def body(q_ref, f_ref, i_ref, g_ref, lb_ref, ng_ref, mask_ref, s_ref, dy_ref,
             dp_ref, dlb_ref, dng_ref, dst_ref):
        h, i = pl.program_id(0), pl.program_id(1)
        consts = _hg_consts(mask_ref[...])

        @pl.when(i == 0)
        def _():
            dst_ref[...] = jnp.zeros_like(dst_ref)
            dlb_ref[...] = jnp.zeros_like(dlb_ref)

        @pl.when(jnp.logical_and(i == 0, h == 0))
        def _():
            dng_ref[...] = jnp.zeros_like(dng_ref)

        def fn(qp, fp, ip, gp, lbx, ngx, stx):
            return _hg_block(qp, fp, ip, gp, lbx, ngx, stx, *consts)

        for p in range(HG_HEADS):
            cs = slice(p * HEAD, (p + 1) * HEAD)
            _, vjp_fn = jax.vjp(fn, q_ref[:, cs], f_ref[:, cs], i_ref[:, cs], g_ref[:, cs], lb_ref[:, cs],
                                ng_ref[...], s_ref[p, 0])
            *gparts, glb, gng, dst = vjp_fn((dy_ref[:, cs].astype(F32), dst_ref[p]))
            for part, gpart in enumerate(gparts):
                dp_ref[part, :, cs] = gpart.astype(dp_ref.dtype)
            dst_ref[p] = dst
            dlb_ref[:, cs] += glb
            dng_ref[...] += gng

    rev = lambda h, i: (nb - 1 - i, h)
    return pl.pallas_call(
        body, name="hgrn2_bwd",
        grid=(nh // HG_HEADS, nb),
        in_specs=_hg_specs(tb, nh, rev_nb=nb) + [
            pl.BlockSpec((HG_HEADS, 1, HEAD, HEAD), lambda h, i: (h, nb - 1 - i, 0, 0)),
            pl.BlockSpec((tb, wide), rev)],
        out_specs=[pl.BlockSpec((4, tb, wide), lambda h, i: (0, nb - 1 - i, h)),
                   pl.BlockSpec((1, wide), lambda h, i: (0, h)), pl.BlockSpec((1, HEAD), lambda h, i: (0, 0))],
        out_shape=[jax.ShapeDtypeStruct((4, t, nh * HEAD), BF16),
                   jax.ShapeDtypeStruct((1, nh * HEAD), F32), jax.ShapeDtypeStruct((1, HEAD), F32)],
        scratch_shapes=[pltpu.VMEM((HG_HEADS, HEAD, HEAD), F32)],
        compiler_params=_cparams(dimension_semantics=("arbitrary", "arbitrary")),
    )(proj, proj, proj, proj, lb, ng, _hg_mask(tb), states, dy)


def _fgate_consts(cb):
    r = lax.broadcasted_iota(jnp.int32, (cb, cb), 0)
    s = lax.broadcasted_iota(jnp.int32, (cb, cb), 1)
    return (r <= s).astype(F32), (r >= s).astype(F32)


def _fgate_fwd(xt, bias, cb=512):
    nh, t = xt.shape
    cb = min(cb, t)

    def body(x_ref, b_ref, o_ref):
        upper, _ = _fgate_consts(cb)
        carry = jnp.zeros((nh, 1), F32)
        for blk in range(t // cb):
            z = x_ref[:, blk * cb:(blk + 1) * cb] + b_ref[...]
            logf = jnp.minimum(z, 0.0) - jnp.log(1.0 + jnp.exp(-jnp.abs(z)))
            cs = _f32dot(logf, upper) + carry
            o_ref[:, blk * cb:(blk + 1) * cb] = cs
            carry = cs[:, cb - 1:cb]

    vm = pl.BlockSpec(memory_space=pltpu.VMEM)
    return pl.pallas_call(
        body, name="fgate_fwd", in_specs=[vm, vm], out_specs=vm,
        out_shape=jax.ShapeDtypeStruct((nh, t), F32), compiler_params=_cparams(),
    )(xt, bias)


def _fgate_bwd(xt, bias, dft, cb=512):
    nh, t = xt.shape
    cb = min(cb, t)
    nblk = t // cb

    def body(x_ref, b_ref, d_ref, dx_ref, db_ref):
        _, lower = _fgate_consts(cb)
        carry = jnp.zeros((nh, 1), F32)
        db = jnp.zeros((nh, 1), F32)
        for blk in range(nblk - 1, -1, -1):
            sl = slice(blk * cb, (blk + 1) * cb)
            dlogf = _f32dot(d_ref[:, sl], lower) + carry
            carry = dlogf[:, 0:1]
            z = x_ref[:, sl] + b_ref[...]
            dz = dlogf * (1.0 - _sigmoid(z))
            dx_ref[:, sl] = dz
            db = db + jnp.sum(dz, axis=1, keepdims=True)
        db_ref[...] = db

    vm = pl.BlockSpec(memory_space=pltpu.VMEM)
    return pl.pallas_call(
        body, name="fgate_bwd", in_specs=[vm, vm, vm], out_specs=[vm, vm],
        out_shape=[jax.ShapeDtypeStruct((nh, t), F32), jax.ShapeDtypeStruct((nh, 1), F32)],
        compiler_params=_cparams(),
    )(xt, bias, dft)


ATTN_GROUPS = 4
ATTN_FWD_HEADS = 2


def _attn_fwd(q, k, v, f_grp, blk):
    t, width = v.shape
    nh = width // HEAD
    nq = t // blk
    hpg = nh // ATTN_GROUPS

    def body(q_ref, k_ref, v_ref, fc_ref, o_ref, lse_ref):
        i = pl.program_id(0)
        tri = (lax.broadcasted_iota(jnp.int32, (blk, blk), 1) <= lax.broadcasted_iota(jnp.int32, (blk, blk), 0))
        for h0 in range(0, nh, ATTN_FWD_HEADS):
            heads = range(h0, min(h0 + ATTN_FWD_HEADS, nh))

            def tile(j, carries, masked):
                rs = pl.ds(pl.multiple_of(j * blk, blk), blk)
                out = []
                for h, (m, l, acc) in zip(heads, carries):
                    cs = slice(h * HEAD, (h + 1) * HEAD)
                    cs2 = slice(2 * h * HEAD, 2 * (h + 1) * HEAD)
                    s = _bdot_raw(q_ref[:, cs2], k_ref[rs, cs2], _NT)
                    if masked:
                        s = jnp.where(tri, s, NEG_INF)
                    m_new = jnp.maximum(m, jnp.max(s, axis=1, keepdims=True))
                    p = jnp.exp(s - m_new)
                    alpha = jnp.exp(m - m_new)
                    l_new = alpha * l + jnp.sum(p, axis=1, keepdims=True)
                    out.append((m_new, l_new, alpha * acc + _bdot_raw(p, v_ref[rs, cs], _NN)))
                return tuple(out)

            init = tuple((jnp.full((blk, 1), NEG_INF, F32), jnp.zeros((blk, 1), F32), jnp.zeros((blk, HEAD), F32))
                         for _ in heads)
            carries = lax.fori_loop(0, i, lambda j, c: tile(j, c, False), init)
            for h, (m, l, acc) in zip(heads, tile(i, carries, True)):
                o_ref[:, h * HEAD:(h + 1) * HEAD] = acc / l
                g, hh = divmod(h, hpg)
                lse_ref[g, :, hh:hh + 1] = m + jnp.log(l) + fc_ref[g, :, hh:hh + 1]

    vm = pl.BlockSpec(memory_space=pltpu.VMEM)
    stat = pl.BlockSpec((ATTN_GROUPS, blk, hpg), lambda i: (0, i, 0))
    return pl.pallas_call(
        body, name="fox_attn_fwd",
        grid=(nq,),
        in_specs=[pl.BlockSpec((blk, 2 * width), lambda i: (i, 0)), vm, vm, stat],
        out_specs=[pl.BlockSpec((blk, width), lambda i: (i, 0)), stat],
        out_shape=[jax.ShapeDtypeStruct((t, width), F32), jax.ShapeDtypeStruct((ATTN_GROUPS, t, hpg), F32)],
        compiler_params=_cparams(dimension_semantics=("parallel",)),
    )(q, k, v, f_grp)


def _outgate_bwd(o, proj_q, dz, tb):
    t, width = o.shape
    nh = width // HEAD
    hpg = nh // ATTN_GROUPS
    tb = min(tb, t)

    def body(o_ref, og_ref, dz_ref, do_ref, dog_ref, dl_ref):
        for h in range(nh):
            cs = slice(h * HEAD, (h + 1) * HEAD)
            ov = o_ref[:, cs]
            _, vjp_fn = jax.vjp(_f_outgate, ov, og_ref[:, cs])
            do, dog = vjp_fn((dz_ref[:, cs].astype(F32),))
            do = do.astype(do_ref.dtype)
            do_ref[:, cs] = do
            dog_ref[:, cs] = dog.astype(dog_ref.dtype)
            g, hh = divmod(h, hpg)
            dl_ref[g, :, hh:hh + 1] = jnp.sum(do.astype(F32) * ov, axis=1, keepdims=True)

    wide = pl.BlockSpec((tb, width), lambda i: (i, 0))
    return pl.pallas_call(body, name="out_gate_bwd", grid=(t // tb,),
                          in_specs=[wide, pl.BlockSpec((tb, width), lambda i: (i, 1)), wide],
                          out_specs=[wide, wide, pl.BlockSpec((ATTN_GROUPS, tb, hpg), lambda i: (0, i, 0))],
                          out_shape=[jax.ShapeDtypeStruct((t, width), BF16), jax.ShapeDtypeStruct((t, width), BF16),
                                     jax.ShapeDtypeStruct((ATTN_GROUPS, t, hpg), F32)],
                          compiler_params=_cparams(dimension_semantics=("parallel",)))(o, proj_q, dz)


def _qnorm_bwd(proj_q, gain, dq_n, dog, tb):
    t, width = dq_n.shape
    nh = width // HEAD
    tb = min(tb, t)

    def body(p_ref, g_ref, dq_ref, dog_ref, out_ref, dg_ref):
        i = pl.program_id(0)
        gv = g_ref[...]
        acc = None
        for h in range(nh):
            cs = slice(h * HEAD, (h + 1) * HEAD)
            _, vjp_fn = jax.vjp(_f_qnorm, p_ref[:, cs], gv)
            dp, dg = vjp_fn((dq_ref[:, cs],))
            out_ref[:, cs] = dp.astype(out_ref.dtype)
            acc = dg if acc is None else acc + dg
        out_ref[:, width:] = dog_ref[...]

        @pl.when(i == 0)
        def _():
            dg_ref[...] = acc

        @pl.when(i > 0)
        def _():
            dg_ref[...] += acc

    wide = pl.BlockSpec((tb, width), lambda i: (i, 0))
    vec = pl.BlockSpec(gain.shape, lambda i: (0, 0))
    return pl.pallas_call(body, name="q_norm_bwd", grid=(t // tb,), in_specs=[wide, vec, wide, wide],
                          out_specs=[pl.BlockSpec((tb, 2 * width), lambda i: (i, 0)), vec],
                          out_shape=[jax.ShapeDtypeStruct((t, 2 * width), BF16), jax.ShapeDtypeStruct(gain.shape, F32)],
                          compiler_params=_cparams(dimension_semantics=("arbitrary",)))(proj_q, gain, dq_n, dog)


def _attn_bwd(q, k, v, f_grp, do, lse, delta, blk):
    t, width = v.shape
    nh = width // HEAD
    nq = t // blk
    hpg = nh // ATTN_GROUPS
    gw = hpg * HEAD

    def body(q_ref, do_ref, k_ref, v_ref, fc_ref, lse_ref, dl_ref,
             dq_ref, dk_ref, dv_ref, dfc_ref, dfr_ref):
        g, j = pl.program_id(0), pl.program_id(1)
        tri = (lax.broadcasted_iota(jnp.int32, (blk, blk), 1) <= lax.broadcasted_iota(jnp.int32, (blk, blk), 0))

        @pl.when(j == 0)
        def _():
            dq_ref[...] = jnp.zeros_like(dq_ref)
            dfc_ref[...] = jnp.zeros_like(dfc_ref)

        def tile(i, carries, masked):
            rs = pl.ds(pl.multiple_of(i * blk, blk), blk)
            out = []
            for h, (dk, dv, dfs) in enumerate(carries):
                cs = slice(h * HEAD, (h + 1) * HEAD)
                cs2 = slice(2 * h * HEAD, 2 * (h + 1) * HEAD)
                csq = slice(2 * h * HEAD, (2 * h + 1) * HEAD)
                qi = q_ref[rs, csq]
                doi = do_ref[rs, cs]
                bias = fc_ref[0, rs, h:h + 1] - lse_ref[0, rs, h:h + 1]
                p = jnp.exp(_bdot_raw(q_ref[rs, cs2], k_ref[:, cs2], _NT) + bias)
                if masked:
                    p = jnp.where(tri, p, 0.0)
                ds = p * (_bdot_raw(doi, v_ref[:, cs], _NT) - dl_ref[0, rs, h:h + 1])
                dsb = ds.astype(BF16)
                dq_ref[rs, cs] += _bdot_raw(dsb, k_ref[:, csq], _NN)
                dfc_ref[0, rs, h:h + 1] += jnp.sum(ds, axis=1, keepdims=True)
                out.append((dk + _bdot_raw(dsb, qi, _TN), dv + _bdot_raw(p, doi, _TN),
                            dfs - jnp.sum(ds, axis=0, keepdims=True)))
            return tuple(out)

        init = tuple((jnp.zeros((blk, HEAD), F32), jnp.zeros((blk, HEAD), F32), jnp.zeros((1, blk), F32))
                     for _ in range(hpg))
        carries = lax.fori_loop(j + 1, nq, lambda i, c: tile(i, c, False), tile(j, init, True))
        for h, (dk, dv, dfs) in enumerate(carries):
            cs = slice(h * HEAD, (h + 1) * HEAD)
            dk_ref[:, cs] = dk
            dv_ref[:, cs] = dv.astype(dv_ref.dtype)
            dfr_ref[0, 0, h:h + 1, :] = dfs

    once = pl.Buffered(1)
    stat = pl.BlockSpec((1, t, hpg), lambda g, j: (g, 0, 0), pipeline_mode=once)
    kv_blk = pl.BlockSpec((blk, gw), lambda g, j: (j, g))
    frow = pl.BlockSpec((1, 1, hpg, blk), lambda g, j: (g, j, 0, 0))
    dq, dk, dv, dfc, dfr = pl.pallas_call(
        body, name="fox_attn_bwd",
        grid=(ATTN_GROUPS, nq),
        in_specs=[pl.BlockSpec((t, 2 * gw), lambda g, j: (0, g), pipeline_mode=once),
                  pl.BlockSpec((t, gw), lambda g, j: (0, g), pipeline_mode=once),
                  pl.BlockSpec((blk, 2 * gw), lambda g, j: (j, g)), kv_blk, stat, stat, stat],
        out_specs=[pl.BlockSpec((t, gw), lambda g, j: (0, g)), kv_blk, kv_blk,
                   pl.BlockSpec((1, t, hpg), lambda g, j: (g, 0, 0)), frow],
        out_shape=[jax.ShapeDtypeStruct((t, width), F32), jax.ShapeDtypeStruct((t, width), F32),
                   jax.ShapeDtypeStruct((t, width), BF16), jax.ShapeDtypeStruct((ATTN_GROUPS, t, hpg), F32),
                   jax.ShapeDtypeStruct((ATTN_GROUPS, nq, hpg, blk), F32)],
        compiler_params=_cparams(dimension_semantics=("parallel", "arbitrary")),
    )(q, do, k, v, f_grp, lse, delta)
    return dq, dk, dv, dfc, dfr


SUBLANES = 8


def _shift_down(u, n):
    r = pltpu.roll(u, n, 0)
    row = lax.broadcasted_iota(jnp.int32, (SUBLANES, u.shape[1]), 0)
    return jnp.concatenate([jnp.where(row < n, 0.0, r[:SUBLANES]), r[SUBLANES:]], axis=0)


def _shift_up(u, n):
    t = u.shape[0]
    r = pltpu.roll(u, t - n, 0)
    row = lax.broadcasted_iota(jnp.int32, (SUBLANES, u.shape[1]), 0)
    return jnp.concatenate([r[:t - SUBLANES], jnp.where(row >= SUBLANES - n, 0.0, r[t - SUBLANES:])], axis=0)


def _convglu_specs(t):
    return [pl.BlockSpec((2, t, LANES), lambda j: (0, 0, j)),
            pl.BlockSpec((2, CONV_TAPS, LANES), lambda j: (0, 0, j)),
            pl.BlockSpec((2, 1, LANES), lambda j: (0, 0, j))]


def _convglu_fwd(u, cw, cb):
    _, t, fp = u.shape

    def body(u_ref, w_ref, b_ref, a_ref, c_ref):
        c = []
        for hf in range(2):
            uv, w = u_ref[hf].astype(F32), w_ref[hf]
            c.append(w[0:1] * _shift_down(uv, 2) + w[1:2] * _shift_down(uv, 1) + w[2:3] * uv + b_ref[hf])
            c_ref[hf] = c[hf].astype(c_ref.dtype)
        a_ref[...] = (_silu(c[0]) * c[1]).astype(a_ref.dtype)

    return pl.pallas_call(
        body, name="convglu_fwd",
        grid=(fp // LANES,),
        in_specs=_convglu_specs(t),
        out_specs=[pl.BlockSpec((t, LANES), lambda j: (0, j)), pl.BlockSpec((2, t, LANES), lambda j: (0, 0, j))],
        out_shape=[jax.ShapeDtypeStruct((t, fp), BF16), jax.ShapeDtypeStruct((2, t, fp), BF16)],
        compiler_params=_cparams(dimension_semantics=("parallel",)),
    )(u, cw, cb)


def _convglu_bwd(u, c, cw, da):
    _, t, fp = u.shape

    def body(u_ref, c_ref, w_ref, da_ref, du_ref, dw_ref, db_ref):
        gc, vc = c_ref[0].astype(F32), c_ref[1].astype(F32)
        sg = _sigmoid(gc)
        dav = da_ref[...].astype(F32)
        dcs = [dav * vc * (sg * (1.0 + gc * (1.0 - sg))), dav * (gc * sg)]
        for hf in range(2):
            dc, w, uv = dcs[hf], w_ref[hf], u_ref[hf].astype(F32)
            dc1, dc2 = _shift_up(dc, 1), _shift_up(dc, 2)
            du_ref[hf] = (w[2:3] * dc + w[1:2] * dc1 + w[0:1] * dc2).astype(du_ref.dtype)
            dw_ref[hf, 0:1, :] = jnp.sum(dc2 * uv, axis=0, keepdims=True)
            dw_ref[hf, 1:2, :] = jnp.sum(dc1 * uv, axis=0, keepdims=True)
            dw_ref[hf, 2:3, :] = jnp.sum(dc * uv, axis=0, keepdims=True)
            db_ref[hf] = jnp.sum(dc, axis=0, keepdims=True)

    pair, taps, bias = _convglu_specs(t)
    return pl.pallas_call(
        body, name="convglu_bwd",
        grid=(fp // LANES,),
        in_specs=[pair, pair, taps, pl.BlockSpec((t, LANES), lambda j: (0, j))],
        out_specs=[pair, taps, bias],
        out_shape=[jax.ShapeDtypeStruct((2, t, fp), BF16), jax.ShapeDtypeStruct((2, CONV_TAPS, fp), F32),
                   jax.ShapeDtypeStruct((2, 1, fp), F32)],
        compiler_params=_cparams(dimension_semantics=("parallel",)),
    )(u, c, cw, da)


def _local_step(x, target, mods, lb, small, pre_w, get_w, put_g, *, tb=512, attn_blk=512):
    t, d = x.shape
    nh = d // HEAD
    nb = NDEV
    wts = {}
    vec = lambda *names: [mods[n] for n in names]

    def ffn_fwd(h2, l):
        u = _mm_wblk(h2, wts[f"up{l}"], BF16, f"ffn{l}_up", gb=nb // 2, split=2, tm=512)
        a, c = _convglu_fwd(u, small[f"conv_w{l}"], small[f"conv_b{l}"])
        f = _mm(a, wts[f"down{l}"], "nn", F32, f"ffn{l}_down", tk=4096)
        return (u, c), a, f

    def ffn_bwd(df, h2, uc, a, l):
        u, c = uc
        da = _mm(df, wts[f"down{l}"], "nt", BF16, f"ffn{l}_down_dx", tn=1536)
        dwd = _mm(a, df, "tn", BF16, f"ffn{l}_down_dw", tm=768, tk=t)
        du, dcw, dcb = _convglu_bwd(u, c, small[f"conv_w{l}"], da)
        dh2 = _mm_wblk_dx(du, wts[f"up{l}"], BF16, f"ffn{l}_up_dx", k=d, gb=nb // 2, split=2, tm=1024)
        dwu = _mm_wblk_dw(h2, du, f"ffn{l}_up_dw", nb=nb, gb=1, split=2, tk=t)
        return dh2, dwu, dwd, dcw, dcb

    (h_a,) = _row_fwd(_f_mod, [(x, d, 0)], vec("sh1_0", "sc1_0"), [BF16], tb=tb, name="l0_mod1")
    wts.update(get_w("l0a", h_a))
    proj_a = _mm_wblk(h_a, wts["a_in"], F32, "a_in", gb=nb // 2)
    ypre, states = _hgrn2_fwd(proj_a, lb, small["a_norm_g"], tb)
    pre_w("l0b", ypre)
    wts.update(get_w("l0b", ypre))
    y_a = _mm(ypre, wts["a_out"], "nn", F32, "a_out")
    x1, h2_0 = _row_fwd(_f_res_mod, [(x, d, 0), (y_a, d, 0)], vec("g1_0", "sh2_0", "sc2_0"), [F32, BF16],
                        tb=tb, name="l0_res_mod2")
    wts.update(get_w("l0b_ffn", h2_0))
    u0, a0, f0 = ffn_fwd(h2_0, 0)
    x2, h_kv, h_q = _row_fwd(_f_res_mod2, [(x1, d, 0), (f0, d, 0)],
                             [mods["g2_0"] + pre_w("l1", f0)] + vec("kv_sh", "kv_sc", "sh1_1", "sc1_1"),
                             [F32, BF16, BF16], tb=tb, name="l0_res_kvmod_qmod")
    wts.update(get_w("l1", h_kv))
    proj_k = _mm(h_kv, wts["kv_k"], "nt", F32, "k_proj")
    v_b = _mm(h_kv, wts["kv_v"], "nt", BF16, "v_proj")
    proj_f = _mm(h_kv, wts["kv_f"], "nt", F32, "kv_fproj")
    f_logit_t = proj_f[:, :nh].T
    f_bias = small["kv_b_f"].reshape(nh, 1)
    f_t = _fgate_fwd(f_logit_t, f_bias)
    f_grp = f_t.reshape(ATTN_GROUPS, nh // ATTN_GROUPS, t).transpose(0, 2, 1)
    (k_n,) = _row_fwd(_f_knorm_aug, [(proj_k, HEAD, 0)] + [(piece, 1, 0) for piece in _split3(-f_t.T)],
                      [small["k_norm_g"]], [BF16], nsub=nh, tb=tb, name="k_norm")
    proj_q = _mm_wblk(h_q, wts["b_q"], F32, "b_q", gb=nb)
    (q_n,) = _row_fwd(_f_qnorm_aug, [(proj_q, HEAD, 0)], [small["q_norm_g"]], [BF16], nsub=nh, tb=tb,
                      name="q_norm")
    o_att, lse = _attn_fwd(q_n, k_n, v_b, f_grp, attn_blk)
    (z,) = _row_fwd(_f_outgate, [(o_att, HEAD, 0), (proj_q, HEAD, 1)], [], [BF16], nsub=nh, tb=tb, name="out_gate")
    y_b = _mm(z, wts["b_out"], "nn", F32, "b_out")
    x3, h2_1 = _row_fwd(_f_res_mod, [(x2, d, 0), (y_b, d, 0)], vec("g1_1", "sh2_1", "sc2_1"), [F32, BF16],
                        tb=tb, name="l1_res_mod2")
    u1, a1, f1 = ffn_fwd(h2_1, 1)
    loss, dx4, df1, dg2_1 = _loss_call(x3, f1, mods["g2_1"], target, tb)

    g = {}
    dmods = {"g2_1": dg2_1}
    dh2, g["up1"], g["down1"], g["conv_w1"], g["conv_b1"] = ffn_bwd(df1, h2_1, u1, a1, 1)
    (dx2, dy_b), (dmods["g1_1"], dmods["sh2_1"], dmods["sc2_1"]) = _row_bwd(
        _f_res_mod, [(x2, d, 0), (y_b, d, 0)], vec("g1_1", "sh2_1", "sc2_1"),
        [(dx4, d, 0), (dh2, d, 0)], [F32, BF16], tb=tb, name="l1_res_mod2_bwd")
    dz = _mm(dy_b, wts["b_out"], "nt", BF16, "b_out_dx")
    g["b_out"] = _mm(z, dy_b, "tn", BF16, "b_out_dw", tk=t)
    do_att, dog, delta = _outgate_bwd(o_att, proj_q, dz, tb)
    dq_n, dk_n, dv, dfc_q, dfr_k = _attn_bwd(q_n, k_n, v_b, f_grp, do_att, lse, delta, attn_blk)
    dproj_q, g["q_norm_g"] = _qnorm_bwd(proj_q, small["q_norm_g"], dq_n, dog, tb)
    dh_q = _mm_wblk_dx(dproj_q, wts["b_q"], BF16, "b_q_dx", k=d, gb=nb)
    g["b_q"] = _mm_wblk_dw(h_q, dproj_q, "b_q_dw", nb=nb, gb=nb // 4, tk=t)
    (dpk,), (g["k_norm_g"],) = _row_bwd(_f_knorm, [(proj_k, HEAD, 0)], [small["k_norm_g"]],
                                        [(dk_n, HEAD, 0)], [BF16], nsub=nh, tb=tb, name="k_norm_bwd")
    df_t = dfc_q.transpose(0, 2, 1).reshape(nh, t) + dfr_k.transpose(0, 2, 1, 3).reshape(nh, t)
    dflogit_t, g["kv_b_f"] = _fgate_bwd(f_logit_t, f_bias, df_t)
    dproj_f = jnp.pad(dflogit_t.T, ((0, 0), (0, LANES - nh))).astype(BF16)
    dh_kv = _mm(dpk, wts["kv_k"], "nn", BF16, "k_proj_dx")
    dh_kv_v = _mm(dv, wts["kv_v"], "nn", BF16, "v_proj_dx")
    dh_kv_f = _mm(dproj_f, wts["kv_f"], "nn", BF16, "kv_fproj_dx")
    g["kv_k"] = _mm(dpk, h_kv, "tn", BF16, "k_proj_dw", tk=t)
    g["kv_v"] = _mm(dv, h_kv, "tn", BF16, "v_proj_dw", tk=t)
    g["kv_f"] = _mm(dproj_f, h_kv, "tn", F32, "kv_fproj_dw", tk=1024)
    sent = put_g("l1", {n: g.pop(n) for n in ("b_out", "b_q", "kv_k", "kv_v", "kv_f", "up1", "down1")})
    (dx1, df0), (dmods["g2_0"], dmods["kv_sh"], dmods["kv_sc"], dmods["sh1_1"], dmods["sc1_1"]) = _row_bwd(
        _f_res_mod2, [(x1, d, 0), (f0, d, 0)], [mods["g2_0"] + sent] + vec("kv_sh", "kv_sc", "sh1_1", "sc1_1"),
        [(dx2, d, 0), (dh_kv, d, 0), (dh_q, d, 0)], [F32, BF16], tb=tb, name="l0_res_kvmod_qmod_bwd",
        cot_add=[(1, dh_kv_v), (1, dh_kv_f)])
    dh2, g["up0"], g["down0"], g["conv_w0"], g["conv_b0"] = ffn_bwd(df0, h2_0, u0, a0, 0)
    (dx0, dy_a), (dmods["g1_0"], dmods["sh2_0"], dmods["sc2_0"]) = _row_bwd(
        _f_res_mod, [(x, d, 0), (y_a, d, 0)], vec("g1_0", "sh2_0", "sc2_0"),
        [(dx1, d, 0), (dh2, d, 0)], [F32, BF16], tb=tb, name="l0_res_mod2_bwd")
    dypre = _mm(dy_a, wts["a_out"], "nt", BF16, "a_out_dx")
    g["a_out"] = _mm(ypre, dy_a, "tn", BF16, "a_out_dw", tk=t)
    sent = put_g("l0b", {n: g.pop(n) for n in ("a_out", "up0", "down0")})
    dproj_a, dlb, g["a_norm_g"] = _hgrn2_bwd(proj_a, lb + sent, small["a_norm_g"], states, dypre, tb)
    dh_a = _mm_wblk_dx(dproj_a, wts["a_in"], BF16, "a_in_dx", k=d, gb=nb, split=4, tm=512)
    put_g("l0a", {"a_in": _mm_wblk_dw(h_a, dproj_a, "a_in_dw", nb=nb, gb=1, split=4, tk=t)})
    (grad_x,), (dmods["sh1_0"], dmods["sc1_0"]) = _row_bwd(
        _f_mod, [(x, d, 0)], vec("sh1_0", "sc1_0"), [(dh_a, d, 0)], [F32], tb=tb, name="l0_mod1_bwd",
        add_to=(0, dx0))
    return loss, grad_x, dmods, dlb, g


def _position():
    return lax.axis_index("x"), lax.axis_index("y"), lax.axis_index("c")


_XCHG_EFFECT = pltpu.SideEffectType.DATAFLOW_SIDE_EFFECTING
ALL_PEERS = (1, 2, 3, 4, 5, 6, 7)
SAME_CORE = (2, 4, 6)


def _xchg_copies(src_refs, land_refs, send_sems, recv_sems, local_sems, scatter, rels):
    x, y, cc = _position()
    me = 4 * x + 2 * y + cc
    remote, local = [], []
    for a, (src, land) in enumerate(zip(src_refs, land_refs)):
        local.append(pltpu.make_async_copy(src.at[me] if scatter else src, land.at[me], local_sems.at[a]))
        for idx, rel in enumerate(rels):
            px = 1 - x if rel & 4 else x
            py = 1 - y if rel & 2 else y
            pc = 1 - cc if rel & 1 else cc
            k = len(rels) * a + idx
            remote.append(pltpu.make_async_remote_copy(
                src_ref=src.at[4 * px + 2 * py + pc] if scatter else src, dst_ref=land.at[me],
                send_sem=send_sems.at[k], recv_sem=recv_sems.at[k], device_id=(px, py, pc), device_id_type=_MESH))
    return remote, local


def _xchg_start(srcs, scatter, rels, after, name):
    n = len(srcs)
    lands = [lax.empty(s.shape if scatter else (NDEV, *s.shape), s.dtype) for s in srcs]

    def body(*refs):
        remote, local = _xchg_copies(refs[:n], refs[n:2 * n], *refs[2 * n + 1:2 * n + 4], scatter, rels)
        for cp in local + remote:
            cp.start()
        token = refs[-1]
        token[...] = jnp.zeros_like(token)

    hbm = pl.BlockSpec(memory_space=pltpu.HBM)
    sem = pl.BlockSpec(memory_space=pltpu.SEMAPHORE)
    out = pl.pallas_call(
        body, name=name,
        out_shape=(pltpu.SemaphoreType.DMA((len(rels) * n,)), pltpu.SemaphoreType.DMA((len(rels) * n,)),
                   pltpu.SemaphoreType.DMA((n,)),
                   *[pltpu.HBM(a.shape, a.dtype) for a in srcs + lands], jax.ShapeDtypeStruct((8, LANES), F32)),
        in_specs=[hbm] * (2 * n) + [pl.BlockSpec(memory_space=pl.ANY)],
        out_specs=(sem, sem, sem, *[hbm] * (2 * n), pl.BlockSpec(memory_space=pltpu.VMEM)),
        input_output_aliases={i: 3 + i for i in range(2 * n)},
        compiler_params=pltpu.CompilerParams(has_side_effects=_XCHG_EFFECT),
    )(*[pltpu.with_memory_space_constraint(a, pltpu.HBM) for a in srcs + lands], after)
    return out[:-1], out[-1][0, 0]


def _xchg_wait(handles, after, scatter, rels, name):
    n = (len(handles) - 3) // 2

    def body(*refs):
        remote, local = _xchg_copies(refs[:n], refs[n:2 * n], *refs[2 * n:2 * n + 3], scatter, rels)
        for cp in remote:
            cp.wait_send()
            cp.wait_recv()
        for cp in local:
            cp.wait()

    hbm = pl.BlockSpec(memory_space=pltpu.HBM)
    sem = pl.BlockSpec(memory_space=pltpu.SEMAPHORE)
    thru = list(handles[3:])
    afters = list(after) if isinstance(after, (list, tuple)) else [after]
    out = pl.pallas_call(
        body, name=name,
        out_shape=tuple(pltpu.HBM(a.shape, a.dtype) for a in thru),
        in_specs=[hbm] * (2 * n) + [sem, sem, sem] + [pl.BlockSpec(memory_space=pl.ANY)] * len(afters),
        out_specs=tuple([hbm] * (2 * n)),
        input_output_aliases={i: i for i in range(2 * n)},
        compiler_params=pltpu.CompilerParams(has_side_effects=_XCHG_EFFECT),
    )(*thru, *handles[:3], *afters)
    return list(out[n:])


def _sibling_copies(land_refs, send_sems, recv_sems):
    x, y, cc = _position()

    def copy(a, q, core):
        slot = land_refs[a].at[2 * q + core]
        return pltpu.make_async_remote_copy(
            src_ref=slot, dst_ref=slot, send_sem=send_sems.at[NCHIP * a + q], recv_sem=recv_sems.at[NCHIP * a + q],
            device_id=(x, y, 1 - cc), device_id_type=_MESH)

    pairs = [(a, q) for a in range(len(land_refs)) for q in range(NCHIP)]
    return [copy(a, q, cc) for a, q in pairs], [copy(a, q, 1 - cc) for a, q in pairs]


def _sibling_forward_start(lands, name, after=None):
    n = len(lands)
    deps = [] if after is None else [after]

    def body(*refs):
        sends, _ = _sibling_copies(refs[:n], refs[n + len(deps)], refs[n + len(deps) + 1])
        for cp in sends:
            cp.start()
        refs[-1][...] = jnp.zeros_like(refs[-1])

    hbm = pl.BlockSpec(memory_space=pltpu.HBM)
    sem = pl.BlockSpec(memory_space=pltpu.SEMAPHORE)
    out = pl.pallas_call(
        body, name=name,
        out_shape=(pltpu.SemaphoreType.DMA((NCHIP * n,)), pltpu.SemaphoreType.DMA((NCHIP * n,)),
                   *[pltpu.HBM(a.shape, a.dtype) for a in lands], jax.ShapeDtypeStruct((8, LANES), F32)),
        in_specs=[hbm] * n + [pl.BlockSpec(memory_space=pl.ANY)] * len(deps),
        out_specs=(sem, sem, *[hbm] * n, pl.BlockSpec(memory_space=pltpu.VMEM)),
        input_output_aliases={i: 2 + i for i in range(n)},
        compiler_params=pltpu.CompilerParams(has_side_effects=_XCHG_EFFECT),
    )(*lands, *deps)
    return out[:-1], out[-1][0, 0]


def _sibling_forward_wait(handles, after, name):
    n = len(handles) - 2

    def body(*refs):
        sends, arrivals = _sibling_copies(refs[:n], refs[n], refs[n + 1])
        for cp in sends:
            cp.wait_send()
        for cp in arrivals:
            cp.wait_recv()

    hbm = pl.BlockSpec(memory_space=pltpu.HBM)
    sem = pl.BlockSpec(memory_space=pltpu.SEMAPHORE)
    lands = list(handles[2:])
    return list(pl.pallas_call(
        body, name=name,
        out_shape=tuple(pltpu.HBM(a.shape, a.dtype) for a in lands),
        in_specs=[hbm] * n + [sem, sem, pl.BlockSpec(memory_space=pl.ANY)],
        out_specs=tuple([hbm] * n),
        input_output_aliases={i: i for i in range(n)},
        compiler_params=pltpu.CompilerParams(has_side_effects=_XCHG_EFFECT),
    )(*lands, *handles[:2], after))


def _slab_sum(slabs, name, tr=None):
    n, r, c = slabs.shape
    tr = r if tr is None else tr

    def body(s_ref, o_ref):
        acc = s_ref[0].astype(F32)
        for q in range(1, n):
            acc = acc + s_ref[q].astype(F32)
        o_ref[...] = acc

    return pl.pallas_call(body, name=name, grid=(r // tr,),
                          in_specs=[pl.BlockSpec((n, tr, c), lambda i: (0, i, 0))],
                          out_specs=pl.BlockSpec((tr, c), lambda i: (i, 0)),
                          out_shape=jax.ShapeDtypeStruct((r, c), F32),
                          compiler_params=_cparams(dimension_semantics=("parallel",)))(slabs)


def _ada_fwd(c_all, ada_w, kv_ada_w, logits):
    rows, d = c_all.shape
    n0, nkv = ada_w.shape[2], kv_ada_w.shape[1]

    def body(c_ref, w_ref, kw_ref, lg_ref, part_ref, cact_ref, lb_ref):
        ca = _silu(c_ref[...])
        cact_ref[...] = ca
        part_ref[:, 0:n0] = _bdot_raw(ca, w_ref[0], _NN)
        part_ref[:, n0:2 * n0] = _bdot_raw(ca, w_ref[1], _NN)
        part_ref[:, 2 * n0:2 * n0 + nkv] = _bdot_raw(ca, kw_ref[...], _NN)
        lb_ref[...] = _sigmoid(lg_ref[0:1, :] - lg_ref[1:2, :])

    vm = pl.BlockSpec(memory_space=pltpu.VMEM)
    return pl.pallas_call(
        body, name="ada_fwd", in_specs=[vm, vm, vm, vm], out_specs=[vm, vm, vm],
        out_shape=[jax.ShapeDtypeStruct((rows, 2 * n0 + nkv), F32), jax.ShapeDtypeStruct((rows, d), F32),
                   jax.ShapeDtypeStruct((1, d), F32)],
        compiler_params=_cparams(),
    )(c_all, ada_w, kv_ada_w, logits)


def _ada_bwd(c_act, dm0, dm1, dkv, lb, dlb):
    rows, d = c_act.shape

    def body(c_ref, d0_ref, d1_ref, dk_ref, lb_ref, dlb_ref, dw_ref, dkw_ref, dlg_ref):
        ca = c_ref[...]
        dw_ref[0] = _bdot_raw(ca, d0_ref[...], _TN)
        dw_ref[1] = _bdot_raw(ca, d1_ref[...], _TN)
        dkw_ref[...] = _bdot_raw(ca, dk_ref[...], _TN)
        lbv = lb_ref[...]
        dl0 = dlb_ref[...] * lbv * (1.0 - lbv)
        dlg_ref[0:1, :] = dl0
        dlg_ref[1:2, :] = -dl0

    vm = pl.BlockSpec(memory_space=pltpu.VMEM)
    return pl.pallas_call(
        body, name="ada_bwd", in_specs=[vm] * 6, out_specs=[vm, vm, vm],
        out_shape=[jax.ShapeDtypeStruct((2, d, dm0.shape[1]), F32), jax.ShapeDtypeStruct((d, dkv.shape[1]), F32),
                   jax.ShapeDtypeStruct((2, d), F32)],
        compiler_params=_cparams(),
    )(c_act, dm0, dm1, dkv, lb, dlb)


def _adamw(w, g, m, v, name, tr=512, after=None):
    r, c = w.shape
    tr = _divisor_tile(r, tr, unit=8)
    c1 = 1.0 - ADAM_B1 ** ADAM_STEP
    c2 = 1.0 - ADAM_B2 ** ADAM_STEP
    deps = [] if after is None else [after]

    def body(w_ref, g_ref, m_ref, v_ref, *rest):
        d_ref, mo_ref, vo_ref = rest[len(deps):]
        gv = g_ref[...]
        mn = ADAM_B1 * m_ref[...] + (1.0 - ADAM_B1) * gv
        vn = ADAM_B2 * v_ref[...] + (1.0 - ADAM_B2) * (gv * gv)
        d_ref[...] = -ADAM_LR * ((mn / c1) / (jnp.sqrt(vn / c2) + ADAM_EPS) + ADAM_WD * w_ref[...])
        mo_ref[...] = mn
        vo_ref[...] = vn

    spec = pl.BlockSpec((tr, c), lambda i: (i, 0))
    out = jax.ShapeDtypeStruct((r, c), F32)
    return pl.pallas_call(body, name=name, grid=(r // tr,),
                          in_specs=[spec] * 4 + [pl.BlockSpec(a.shape, lambda i: (0, 0)) for a in deps],
                          out_specs=[spec] * 3, out_shape=[out, out, out],
                          compiler_params=_cparams(dimension_semantics=("parallel",)))(w, g, m, v, *deps)


def _pad_rows(a, rows):
    return jnp.pad(a, ((0, rows - a.shape[0]), (0, 0)))


def _pack_small(parts, lanes=LANES, row_unit=8):
    flat = jnp.concatenate([p.reshape(-1).astype(F32) for p in parts])
    rows = _round_up(-(-flat.shape[0] // lanes), row_unit)
    return jnp.pad(flat, (0, rows * lanes - flat.shape[0])).reshape(rows, lanes)


def _unpack_small(flat, shapes):
    out, off = [], 0
    for s in shapes:
        n = 1
        for k in s:
            n *= k
        out.append(flat[off:off + n].reshape(s))
        off += n
    return out


def _pad_shard_cols(a, n_loc, n_pad):
    lead, runs = a.shape[:-1], a.shape[-1] // n_loc
    a = a.reshape(*lead, runs, n_loc)
    a = jnp.pad(a, [(0, 0)] * (len(lead) + 1) + [(0, n_pad - n_loc)])
    return a.reshape(*lead, runs * n_pad)


def _unpad_shard_cols(a, n_loc, n_pad):
    lead, runs = a.shape[:-1], a.shape[-1] // n_pad
    return a.reshape(*lead, runs, n_pad)[..., :n_loc].reshape(*lead, runs * n_loc)


def kernel(x, c, ada_w, ada_b, a_w_in, a_lb_logits, a_norm_g, a_w_out, kv_ada_w, kv_ada_b, kv_w, kv_b_f, k_norm_g, b_w_q, q_norm_g, b_w_out, ffn_w_up, ffn_conv_w, ffn_conv_b, ffn_w_down, loss_target, m_ada_w, m_ada_b, m_a_w_in, m_a_lb_logits, m_a_norm_g, m_a_w_out, m_kv_ada_w, m_kv_ada_b, m_kv_w, m_kv_b_f, m_k_norm_g, m_b_w_q, m_q_norm_g, m_b_w_out, m_ffn_w_up, m_ffn_conv_w, m_ffn_conv_b, m_ffn_w_down, v_ada_w, v_ada_b, v_a_w_in, v_a_lb_logits, v_a_norm_g, v_a_w_out, v_kv_ada_w, v_kv_ada_b, v_kv_w, v_kv_b_f, v_k_norm_g, v_b_w_q, v_q_norm_g, v_b_w_out, v_ffn_w_up, v_ffn_conv_w, v_ffn_conv_b, v_ffn_w_down):
    t, d = x.shape[1], x.shape[2]
    nh = d // HEAD
    ncw = ffn_w_up.shape[2]
    rd = ffn_w_down.shape[1]
    assert ncw == 2 * rd
    rp = _round_up(rd, LANES)
    ncp = 2 * rp
    two_f = ncw * NDEV
    fp = ncp * NDEV // 2
    me = 4 * lax.axis_index("x") + 2 * lax.axis_index("y") + lax.axis_index("c")
    weights = dict(ada_w=ada_w, ada_b=ada_b, a_w_in=a_w_in, a_lb_logits=a_lb_logits, a_norm_g=a_norm_g,
                   a_w_out=a_w_out, kv_ada_w=kv_ada_w, kv_ada_b=kv_ada_b, kv_w=kv_w, kv_b_f=kv_b_f,
                   k_norm_g=k_norm_g, b_w_q=b_w_q, q_norm_g=q_norm_g, b_w_out=b_w_out, ffn_w_up=ffn_w_up,
                   ffn_conv_w=ffn_conv_w, ffn_conv_b=ffn_conv_b, ffn_w_down=ffn_w_down)
    m_in = dict(ada_w=m_ada_w, ada_b=m_ada_b, a_w_in=m_a_w_in, a_lb_logits=m_a_lb_logits, a_norm_g=m_a_norm_g,
                a_w_out=m_a_w_out, kv_ada_w=m_kv_ada_w, kv_ada_b=m_kv_ada_b, kv_w=m_kv_w, kv_b_f=m_kv_b_f,
                k_norm_g=m_k_norm_g, b_w_q=m_b_w_q, q_norm_g=m_q_norm_g, b_w_out=m_b_w_out, ffn_w_up=m_ffn_w_up,
                ffn_conv_w=m_ffn_conv_w, ffn_conv_b=m_ffn_conv_b, ffn_w_down=m_ffn_w_down)
    v_in = dict(ada_w=v_ada_w, ada_b=v_ada_b, a_w_in=v_a_w_in, a_lb_logits=v_a_lb_logits, a_norm_g=v_a_norm_g,
                a_w_out=v_a_w_out, kv_ada_w=v_kv_ada_w, kv_ada_b=v_kv_ada_b, kv_w=v_kv_w, kv_b_f=v_kv_b_f,
                k_norm_g=v_k_norm_g, b_w_q=v_b_w_q, q_norm_g=v_q_norm_g, b_w_out=v_b_w_out, ffn_w_up=v_ffn_w_up,
                ffn_conv_w=v_ffn_conv_w, ffn_conv_b=v_ffn_conv_b, ffn_w_down=v_ffn_w_down)
    order = list(weights)

    up_loc = _pad_shard_cols(ffn_w_up, rd, rp).astype(BF16)
    down_loc = jnp.pad(ffn_w_down, ((0, 0), (0, rp - rd), (0, 0))).astype(BF16)
    gather_names = {"l0b": ["a_out", "up0", "down0"], "l1": ["kv", "b_q", "b_out", "up1", "down1"]}
    forward_names = {"l0b": ["a_out"], "l0b_ffn": ["up0", "down0"], "l1": gather_names["l1"]}
    shards = {"a_out": a_w_out[0].astype(BF16), "up0": up_loc[0], "down0": down_loc[0], "kv": kv_w.T.astype(BF16),
              "b_q": b_w_q[0].astype(BF16), "b_out": b_w_out[0].astype(BF16), "up1": up_loc[1],
              "down1": down_loc[1]}
    pre = _pack_small([c, a_lb_logits, ffn_conv_w])
    in_flight = {}
    pre_flight, _ = _xchg_start([pre], False, ALL_PEERS, pre, "gather_small_inputs_start")
    (pre_all,) = _xchg_wait(pre_flight, pre, False, ALL_PEERS, "gather_small_inputs_wait")
    pre_all = pre_all.reshape(NDEV, -1)
    c_all = pre_all[:, :d]
    logits = pre_all[:, d:d + 2 * HEAD].reshape(NDEV, 2, HEAD).transpose(1, 0, 2).reshape(2, d)
    conv_w_full = pre_all[:, d + 2 * HEAD:d + 2 * HEAD + 2 * CONV_TAPS * ncw]
    conv_w_full = conv_w_full.reshape(NDEV, 2, CONV_TAPS, ncw).transpose(1, 2, 0, 3).reshape(2, CONV_TAPS, two_f)

    part, c_act, lb = _ada_fwd(_pad_rows(c_all, 2 * NDEV), ada_w, kv_ada_w, logits)
    part_flight, _ = _xchg_start([part[:NDEV]], False, ALL_PEERS, part, "gather_adaln_start")
    in_flight["l0a"], _ = _xchg_start([a_w_in[0].astype(BF16)], False, SAME_CORE, part_flight[-1], "gather_l0a_start")
    (part_all,) = _xchg_wait(part_flight, in_flight["l0a"][-1], False, ALL_PEERS, "gather_adaln_wait")
    forwarding = {}
    mine = lax.dynamic_index_in_dim(part_all, me, axis=1, keepdims=False)
    n0, nkv = ada_w.shape[2], kv_ada_w.shape[1]
    mod_names = ["sh1", "sc1", "g1", "sh2", "sc2", "g2"]
    mods = {}
    for l in range(2):
        row = mine[:, l * n0:(l + 1) * n0].reshape(-1) + ada_b[l]
        for k, nm in enumerate(mod_names):
            mods[f"{nm}_{l}"] = row[k * d:(k + 1) * d].reshape(1, d)
    kvrow = mine[:, 2 * n0:2 * n0 + nkv].reshape(-1) + kv_ada_b
    mods["kv_sh"], mods["kv_sc"] = kvrow[:d].reshape(1, d), kvrow[d:].reshape(1, d)

    def start_gather(grp, dep):
        srcs = [shards[n] for n in gather_names[grp]]
        in_flight[grp], started = _xchg_start(srcs, False, SAME_CORE, dep, f"gather_{grp}_start")
        return started

    zero = start_gather("l0b", part_all)
    mods["sh1_0"] = mods["sh1_0"] + zero

    small = {"a_norm_g": a_norm_g, "k_norm_g": k_norm_g.reshape(1, HEAD), "q_norm_g": q_norm_g, "kv_b_f": kv_b_f}
    for l in range(2):
        small[f"conv_w{l}"] = _pad_shard_cols(conv_w_full[l], rd, rp).reshape(CONV_TAPS, 2, fp).transpose(1, 0, 2)
        small[f"conv_b{l}"] = _pad_shard_cols(ffn_conv_b[l], rd, rp).reshape(2, 1, fp)

    def pre_w(grp, after):
        arrived = _xchg_wait(in_flight[grp], after, False, SAME_CORE, f"gather_{grp}_wait")
        if grp == "l0b":
            forwarding[grp], _ = _sibling_forward_start(arrived[:1], "gather_l0b_to_sibling_start")
            forwarding["l0b_ffn"], started = _sibling_forward_start(
                arrived[1:], "gather_l0b_ffn_to_sibling_start", after=forwarding[grp][-1])
            return started
        forwarding[grp], started = _sibling_forward_start(arrived, f"gather_{grp}_to_sibling_start")
        return started

    def get_w(grp, after):
        if grp == "l0a":
            arrived = _xchg_wait(in_flight["l0a"], after, False, SAME_CORE, "gather_l0a_wait")
            handles, _ = _sibling_forward_start(arrived, "gather_l0a_to_sibling_start")
            return {"a_in": _sibling_forward_wait(handles, after, "gather_l0a_to_sibling_wait")[0]}
        full = _sibling_forward_wait(forwarding[grp], after, f"gather_{grp}_to_sibling_wait")
        if grp == "l0b":
            started = start_gather("l1", full[0])
            full[0] = full[0] + started.astype(full[0].dtype)
        got = dict(zip(forward_names[grp], full))
        out = {}
        for n, a in got.items():
            if n in ("a_out", "b_out"):
                out[n] = a.reshape(d, d)
            elif n in ("down0", "down1"):
                out[n] = a.reshape(fp, d)
            elif n == "kv":
                kv_t = a.reshape(NDEV * kv_w.shape[1], d)
                out["kv_k"], out["kv_v"] = kv_t[:d], kv_t[d:2 * d]
                out["kv_f"] = jnp.pad(kv_t[2 * d:], ((0, LANES - nh), (0, 0)))
            else:
                out[n] = a
        return out

    scatter_flight, g_last = {}, {}

    def put_g(grp, gr):
        if grp == "l0a":
            g_last.update(gr)
            return zero
        if grp == "l1":
            g_kvw = jnp.concatenate([gr["kv_k"], gr["kv_v"], gr["kv_f"][:nh].astype(BF16)], axis=0)
            arrs = {"kv_w": g_kvw.reshape(NDEV, kv_w.shape[1], d), "b_w_q": gr["b_q"],
                    "b_w_out": gr["b_out"].reshape(NDEV, d // NDEV, d), "up1": gr["up1"],
                    "down1": gr["down1"].reshape(NDEV, rp, d)}
        else:
            arrs = {"a_w_out": gr["a_out"].reshape(NDEV, d // NDEV, d), "up0": gr["up0"],
                    "down0": gr["down0"].reshape(NDEV, rp, d)}
        srcs = list(arrs.values())
        handles, sent = _xchg_start(srcs, True, ALL_PEERS, srcs[0], f"scatter_{grp}_start")
        scatter_flight[grp] = (list(arrs), handles)
        return sent

    loss_v, grad_x, dmods, dlb, g = _local_step(x[0], loss_target[0], mods, lb, small, pre_w, get_w, put_g)

    g_sum = {}
    for grp in ("l1", "l0b"):
        names, handles = scatter_flight[grp]
        for nm, a in zip(names, _xchg_wait(handles, grad_x, True, ALL_PEERS, f"scatter_{grp}_wait")):
            g_sum[nm] = _slab_sum(a, f"rs_slab_sum_{nm}")

    def conv_w_grad(a):
        return _unpad_shard_cols(a.transpose(1, 0, 2).reshape(CONV_TAPS, 2 * fp), rd, rp)

    def conv_b_grad(a):
        return _unpad_shard_cols(a.reshape(2 * fp), rd, rp)

    dmod_vec = [dmods[f"{nm}_{l}"] for l in range(2) for nm in mod_names] + [dmods["kv_sh"], dmods["kv_sc"]]
    post = _pack_small(dmod_vec + [dlb, g["a_norm_g"], g["k_norm_g"], g["q_norm_g"],
                                   jnp.pad(g["kv_b_f"].reshape(-1), (0, LANES - nh)),
                                   conv_w_grad(g["conv_w0"]), conv_w_grad(g["conv_w1"]),
                                   conv_b_grad(g["conv_b0"]), conv_b_grad(g["conv_b1"]), loss_v])
    post_flight, _ = _xchg_start([post], False, ALL_PEERS, post, "gather_small_grads_start")
    a_in_flight, a_in_sent = _xchg_start([g_last["a_in"]], True, ALL_PEERS, post_flight[-1], "scatter_l0a_start")
    a_in_sent = a_in_sent.reshape(1, 1)
    grads = {
        "a_w_out": g_sum["a_w_out"].reshape(a_w_out.shape),
        "kv_w": g_sum["kv_w"].T,
        "b_w_q": g_sum["b_w_q"].reshape(b_w_q.shape),
        "b_w_out": g_sum["b_w_out"].reshape(b_w_out.shape),
        "ffn_w_up": jnp.stack([_unpad_shard_cols(g_sum["up0"], rd, rp), _unpad_shard_cols(g_sum["up1"], rd, rp)]),
        "ffn_w_down": jnp.stack([g_sum["down0"][:rd], g_sum["down1"][:rd]]),
    }
    delta, new_m, new_v = {}, {}, {}

    def adamw_matrix(n):
        shp = weights[n].shape
        two_d = lambda a: a.reshape(-1, shp[-1])
        dl, mn, vn = _adamw(two_d(weights[n]), two_d(grads[n]), two_d(m_in[n]), two_d(v_in[n]), f"adamw_{n}",
                            after=a_in_sent)
        delta[n], new_m[n], new_v[n] = dl.reshape(shp), mn.reshape(shp), vn.reshape(shp)

    for n in grads:
        adamw_matrix(n)
    (post_all,) = _xchg_wait(post_flight, [new_v[n] for n in grads], False, ALL_PEERS, "gather_small_grads_wait")
    tot = _slab_sum(post_all, "small_grad_sum").reshape(-1)
    nmod = 14 * d
    (t_mod, t_lb, t_ang, t_kng, t_qng, t_bf, t_cw, t_cb, t_loss) = _unpack_small(
        tot, [(nmod,), (1, d), (1, HEAD), (HEAD,), (1, HEAD), (LANES,), (2, CONV_TAPS, two_f), (2, two_f),
              (LANES,)])
    loss = t_loss[0]
    dm_all = post_all.reshape(NDEV, -1)[:, :nmod]
    dm0 = lax.dynamic_slice_in_dim(dm_all[:, :6 * d], me * n0, n0, axis=1)
    dm1 = lax.dynamic_slice_in_dim(dm_all[:, 6 * d:12 * d], me * n0, n0, axis=1)
    dkv = lax.dynamic_slice_in_dim(dm_all[:, 12 * d:], me * nkv, nkv, axis=1)
    g_ada_w, g_kv_ada_w, g_logits = _ada_bwd(c_act, _pad_rows(dm0, 2 * NDEV), _pad_rows(dm1, 2 * NDEV),
                                              _pad_rows(dkv, 2 * NDEV), lb, t_lb)

    grads.update({
        "ada_w": g_ada_w,
        "ada_b": t_mod[:12 * d].reshape(2, 6 * d),
        "a_lb_logits": lax.dynamic_slice_in_dim(g_logits, me * HEAD, HEAD, axis=1),
        "a_norm_g": t_ang,
        "kv_ada_w": g_kv_ada_w,
        "kv_ada_b": t_mod[12 * d:],
        "kv_b_f": t_bf[:nh],
        "k_norm_g": t_kng,
        "q_norm_g": t_qng,
        "ffn_conv_w": lax.dynamic_slice_in_dim(t_cw, me * ncw, ncw, axis=2),
        "ffn_conv_b": t_cb,
    })

    small_adam = [n for n in order if n not in delta and n not in ("ada_w", "kv_ada_w", "a_w_in")]
    packs = [_pack_small([src[n] for n in small_adam]) for src in (weights, grads, m_in, v_in)]
    outs = _adamw(*packs, "adamw_small", tr=packs[0].shape[0])
    shapes = [weights[n].shape for n in small_adam]
    for dst, o in zip((delta, new_m, new_v), outs):
        for n, a in zip(small_adam, _unpack_small(o.reshape(-1), shapes)):
            dst[n] = a
    adamw_matrix("ada_w")
    adamw_matrix("kv_ada_w")
    (landed,) = _xchg_wait(a_in_flight, new_v["kv_ada_w"], True, ALL_PEERS, "scatter_l0a_wait")
    grads["a_w_in"] = _slab_sum(landed, "rs_slab_sum_a_w_in").reshape(a_w_in.shape)
    adamw_matrix("a_w_in")

    return (loss, grad_x.reshape(x.shape), *[grads[n] for n in order], *[delta[n] for n in order],
            *[new_m[n] for n in order], *[new_v[n] for n in order])
```

```python
import functools

import jax
import jax.numpy as jnp
from jax import lax
from jax.experimental import pallas as pl
from jax.experimental.pallas import tpu as pltpu

F32 = jnp.float32
BF16 = jnp.bfloat16

NDEV = 8
NCHIP = 4
HEAD = 128
A_CHUNK = 64
CONV_TAPS = 3
EPS = 1e-6
NEG_INF = -1e30
LANES = 128
VMEM_LIMIT = 48 * 1024 * 1024

ADAM_LR = 0.001
ADAM_B1 = 0.9
ADAM_B2 = 0.999
ADAM_EPS = 1e-08
ADAM_WD = 0.01
ADAM_STEP = 10

_NN = (((1,), (0,)), ((), ()))
_NT = (((1,), (1,)), ((), ()))
_TN = (((0,), (0,)), ((), ()))
_MESH = pl.DeviceIdType.MESH


def _cparams(**kw):
    return pltpu.CompilerParams(vmem_limit_bytes=VMEM_LIMIT, **kw)


def _divisor_tile(n, pref, unit=LANES):
    if n <= pref:
        return n
    best = None
    for t in range(unit, pref + 1, unit):
        if n % t == 0:
            best = t
    assert best is not None, (n, pref)
    return best


def _round_up(n, unit):
    return -(-n // unit) * unit


def _bdot_raw(a, b, dims):
    return lax.dot_general(a.astype(BF16), b.astype(BF16), dims, preferred_element_type=F32)


@jax.custom_vjp
def _dot_nn(a, b):
    return _bdot_raw(a, b, _NN)


@jax.custom_vjp
def _dot_nt(a, b):
    return _bdot_raw(a, b, _NT)


@jax.custom_vjp
def _dot_tn(a, b):
    return _bdot_raw(a, b, _TN)


_dot_nn.defvjp(lambda a, b: (_bdot_raw(a, b, _NN), (a, b)),
               lambda r, g: (_dot_nt(g, r[1]), _dot_tn(r[0], g)))
_dot_nt.defvjp(lambda a, b: (_bdot_raw(a, b, _NT), (a, b)),
               lambda r, g: (_dot_nn(g, r[1]), _dot_tn(g, r[0])))
_dot_tn.defvjp(lambda a, b: (_bdot_raw(a, b, _TN), (a, b)),
               lambda r, g: (_dot_nt(r[1], g), _dot_nn(r[0], g)))


def _f32dot(a, b):
    return lax.dot_general(a, b, _NN, precision=lax.Precision.HIGHEST, preferred_element_type=F32)


def _sigmoid(x):
    return jax.nn.sigmoid(x)


def _silu(x):
    return x * jax.nn.sigmoid(x)


def _rms(x):
    return x * lax.rsqrt(jnp.mean(x * x, axis=-1, keepdims=True) + EPS)


def _modulate(x, sh, sc):
    return _rms(x) * (1.0 + sc) + sh


def _mm_call(a, b, dims, a_spec, b_spec, o_spec, o_shape, grid, acc_tile, name):
    nk = grid[2]

    def body(a_ref, b_ref, o_ref, *acc):
        p = lax.dot_general(a_ref[...].astype(BF16), b_ref[...].astype(BF16), dims,
                            preferred_element_type=F32)
        if nk == 1:
            o_ref[...] = p.astype(o_ref.dtype)
        else:
            kk = pl.program_id(2)

            @pl.when(kk == 0)
            def _():
                acc[0][...] = p

            @pl.when(kk > 0)
            def _():
                acc[0][...] += p

            @pl.when(kk == nk - 1)
            def _():
                o_ref[...] = acc[0][...].astype(o_ref.dtype)

    return pl.pallas_call(
        body, name=name, grid=grid, in_specs=[a_spec, b_spec], out_specs=o_spec, out_shape=o_shape,
        scratch_shapes=[pltpu.VMEM(acc_tile, F32)] if nk > 1 else [],
        compiler_params=_cparams(dimension_semantics=("parallel", "parallel", "arbitrary")),
    )(a, b)


def _mm(a, b, mode, out_dtype, name, tm=1024, tn=1024, tk=2048):
    if mode == "nn":
        (m, k), (k2, n) = a.shape, b.shape
    elif mode == "nt":
        (m, k), (n, k2) = a.shape, b.shape
    else:
        (k, m), (k2, n) = a.shape, b.shape
    assert k == k2, (a.shape, b.shape, mode)
    tm, tn, tk = _divisor_tile(m, tm), _divisor_tile(n, tn), _divisor_tile(k, tk)
    if mode == "tn":
        a_spec = pl.BlockSpec((tk, tm), lambda i, j, kk: (kk, i))
    else:
        a_spec = pl.BlockSpec((tm, tk), lambda i, j, kk: (i, kk))
    if mode == "nt":
        b_spec = pl.BlockSpec((tn, tk), lambda i, j, kk: (j, kk))
    else:
        b_spec = pl.BlockSpec((tk, tn), lambda i, j, kk: (kk, j))
    return _mm_call(a, b, {"nn": _NN, "nt": _NT, "tn": _TN}[mode], a_spec, b_spec,
                    pl.BlockSpec((tm, tn), lambda i, j, kk: (i, j)), jax.ShapeDtypeStruct((m, n), out_dtype),
                    (m // tm, n // tn, k // tk), (tm, tn), name)


def _wblk_act_spec(rows, gb, nl, split, nb, row_axis, blk_axis):
    if split == 1:
        return pl.BlockSpec((rows, gb * nl), lambda *g: (g[row_axis], g[blk_axis]))
    groups = nb // split // gb
    return pl.BlockSpec((None, rows, gb * nl),
                        lambda *g: (g[blk_axis] // groups, g[row_axis], g[blk_axis] % groups))


def _mm_wblk(a, wb, out_dtype, name, *, gb, row_off=0, split=1, tm=1024):
    m, k = a.shape
    nb, _, nl = wb.shape
    assert (nb // split) % gb == 0
    tm = _divisor_tile(m, tm)

    def body(a_ref, b_ref, o_ref):
        av = a_ref[...].astype(BF16)
        for s in range(gb):
            o_ref[:, s * nl:(s + 1) * nl] = lax.dot_general(
                av, b_ref[s].astype(BF16), _NN, preferred_element_type=F32).astype(o_ref.dtype)

    o_shape = (m, nb * nl) if split == 1 else (split, m, nb // split * nl)
    return pl.pallas_call(
        body, name=name, grid=(nb // gb, m // tm),
        in_specs=[pl.BlockSpec((tm, k), lambda j, i: (i, 0)),
                  pl.BlockSpec((gb, k, nl), lambda j, i: (j, row_off, 0))],
        out_specs=_wblk_act_spec(tm, gb, nl, split, nb, 1, 0),
        out_shape=jax.ShapeDtypeStruct(o_shape, out_dtype),
        compiler_params=_cparams(dimension_semantics=("parallel", "parallel")),
    )(a, wb)


def _mm_wblk_dx(dy, wb, out_dtype, name, *, k, gb, row_off=0, split=1, tm=1024):
    nb, _, nl = wb.shape
    m = dy.shape[-2]
    tm = _divisor_tile(m, tm)
    nk = nb // gb
    per = nb // split
    whole = split > 1 and gb == nb
    assert whole or per % gb == 0

    def body(a_ref, b_ref, o_ref, *acc):
        p = None
        for s in range(gb):
            a_blk = a_ref[s // per, :, (s % per) * nl:(s % per + 1) * nl] if whole else a_ref[:, s * nl:(s + 1) * nl]
            q = lax.dot_general(a_blk.astype(BF16), b_ref[s].astype(BF16), _NT, preferred_element_type=F32)
            p = q if p is None else p + q
        if nk == 1:
            o_ref[...] = p.astype(o_ref.dtype)
        else:
            kk = pl.program_id(1)

            @pl.when(kk == 0)
            def _():
                acc[0][...] = p

            @pl.when(kk > 0)
            def _():
                acc[0][...] += p

            @pl.when(kk == nk - 1)
            def _():
                o_ref[...] = acc[0][...].astype(o_ref.dtype)

    return pl.pallas_call(
        body, name=name, grid=(m // tm, nk),
        in_specs=[pl.BlockSpec((split, tm, per * nl), lambda i, kk: (0, i, 0)) if whole
                  else _wblk_act_spec(tm, gb, nl, split, nb, 0, 1),
                  pl.BlockSpec((gb, k, nl), lambda i, kk: (kk, row_off, 0))],
        out_specs=pl.BlockSpec((tm, k), lambda i, kk: (i, 0)),
        out_shape=jax.ShapeDtypeStruct((m, k), out_dtype),
        scratch_shapes=[pltpu.VMEM((tm, k), F32)] if nk > 1 else [],
        compiler_params=_cparams(dimension_semantics=("parallel", "arbitrary")),
    )(dy, wb)


def _mm_wblk_dw(x, dy, name, *, nb, gb, split=1, tk=1024):
    t, k = x.shape
    assert (nb // split) % gb == 0
    nl = dy.shape[-1] * split // nb
    tk = _divisor_tile(t, tk)
    nk = t // tk

    def body(a_ref, b_ref, o_ref, *acc):
        kk = pl.program_id(1)
        av = a_ref[...].astype(BF16)
        for s in range(gb):
            p = lax.dot_general(av, b_ref[:, s * nl:(s + 1) * nl].astype(BF16), _TN, preferred_element_type=F32)
            if nk == 1:
                o_ref[s] = p.astype(o_ref.dtype)
                continue

            @pl.when(kk == 0)
            def _():
                acc[0][s] = p

            @pl.when(kk > 0)
            def _():
                acc[0][s] += p

        if nk > 1:
            @pl.when(kk == nk - 1)
            def _():
                o_ref[...] = acc[0][...].astype(o_ref.dtype)

    return pl.pallas_call(
        body, name=name, grid=(nb // gb, nk),
        in_specs=[pl.BlockSpec((tk, k), lambda j, kk: (kk, 0)), _wblk_act_spec(tk, gb, nl, split, nb, 1, 0)],
        out_specs=pl.BlockSpec((gb, k, nl), lambda j, kk: (j, 0, 0)),
        out_shape=jax.ShapeDtypeStruct((nb, k, nl), BF16),
        scratch_shapes=[pltpu.VMEM((gb, k, nl), F32)] if nk > 1 else [],
        compiler_params=_cparams(dimension_semantics=("parallel", "arbitrary")),
    )(x, dy)


def _row_specs(rows, tb, nsub):
    return [pl.BlockSpec((tb, nsub * cw), functools.partial(lambda i, off: (i, off), off=off))
            for (_, cw, off) in rows]


def _vec_specs(params):
    return [pl.BlockSpec(p.shape, lambda i: (0, 0)) for p in params]


def _row_fwd(f, rows, params, out_dtypes, *, nsub=1, tb, name):
    t = rows[0][0].shape[0]
    tb = min(tb, t)
    n_r, n_p = len(rows), len(params)
    blk = [jax.ShapeDtypeStruct((tb, cw), F32) for (_, cw, _) in rows]
    blk += [jax.ShapeDtypeStruct(p.shape, F32) for p in params]
    out_avals = jax.eval_shape(f, *blk)

    def body(*refs):
        pv = [r[...] for r in refs[n_r:n_r + n_p]]
        for s in range(nsub):
            vals = [r[:, s * cw:(s + 1) * cw].astype(F32) for r, (_, cw, _) in zip(refs[:n_r], rows)]
            outs = f(*vals, *pv)
            for o_ref, o in zip(refs[n_r + n_p:], outs):
                w = o.shape[1]
                o_ref[:, s * w:(s + 1) * w] = o.astype(o_ref.dtype)

    return pl.pallas_call(
        body, name=name,
        grid=(t // tb,),
        in_specs=_row_specs(rows, tb, nsub) + _vec_specs(params),
        out_specs=[pl.BlockSpec((tb, nsub * av.shape[1]), lambda i: (i, 0)) for av in out_avals],
        out_shape=[jax.ShapeDtypeStruct((t, nsub * av.shape[1]), dt) for av, dt in zip(out_avals, out_dtypes)],
        compiler_params=_cparams(dimension_semantics=("parallel",)),
    )(*[r[0] for r in rows], *params)


def _row_bwd(f, rows, params, cots, row_grad_dtypes, *, nsub=1, tb, name, add_to=None, cot_add=None):
    t = rows[0][0].shape[0]
    tb = min(tb, t)
    n_r, n_p, n_c = len(rows), len(params), len(cots)
    want = [j for j in range(n_r) if row_grad_dtypes[j] is not None]
    cot_add = cot_add or []
    extra = [] if add_to is None else [(add_to[1], rows[add_to[0]][1], 0)]
    n_add_to = len(extra)
    extra += [(arr, cots[ci][1], 0) for ci, arr in cot_add]

    def body(*refs):
        i = pl.program_id(0)
        r_in, p_in = refs[:n_r], refs[n_r:n_r + n_p]
        c_in = refs[n_r + n_p:n_r + n_p + n_c]
        e_in = refs[n_r + n_p + n_c:n_r + n_p + n_c + len(extra)]
        outs = refs[n_r + n_p + n_c + len(extra):]
        pv = [r[...] for r in p_in]
        psum = [None] * n_p
        for s in range(nsub):
            vals = [r[:, s * cw:(s + 1) * cw].astype(F32) for r, (_, cw, _) in zip(r_in, rows)]
            cvals = [r[:, s * cw:(s + 1) * cw].astype(F32) for r, (_, cw, _) in zip(c_in, cots)]
            for (ci, _), e_ref in zip(cot_add, e_in[n_add_to:]):
                cw = cots[ci][1]
                cvals[ci] = cvals[ci] + e_ref[:, s * cw:(s + 1) * cw].astype(F32)
            _, vjp_fn = jax.vjp(f, *vals, *pv)
            grads = vjp_fn(tuple(cvals))
            for o_ref, jr in zip(outs[:len(want)], want):
                cw = rows[jr][1]
                gr = grads[jr]
                if add_to is not None and jr == add_to[0]:
                    gr = gr + e_in[0][:, s * cw:(s + 1) * cw]
                o_ref[:, s * cw:(s + 1) * cw] = gr.astype(o_ref.dtype)
            for jp in range(n_p):
                psum[jp] = grads[n_r + jp] if psum[jp] is None else psum[jp] + grads[n_r + jp]
        for o_ref, g in zip(outs[len(want):], psum):
            @pl.when(i == 0)
            def _():
                o_ref[...] = g

            @pl.when(i > 0)
            def _():
                o_ref[...] += g

    out_specs = [pl.BlockSpec((tb, nsub * rows[jr][1]), lambda i: (i, 0)) for jr in want]
    out_shape = [jax.ShapeDtypeStruct((t, nsub * rows[jr][1]), row_grad_dtypes[jr]) for jr in want]
    out_specs += _vec_specs(params)
    out_shape += [jax.ShapeDtypeStruct(p.shape, F32) for p in params]
    res = pl.pallas_call(
        body, name=name,
        grid=(t // tb,),
        in_specs=_row_specs(rows, tb, nsub) + _vec_specs(params) + _row_specs(cots, tb, nsub)
        + _row_specs(extra, tb, nsub),
        out_specs=out_specs, out_shape=out_shape,
        compiler_params=_cparams(dimension_semantics=("arbitrary",)),
    )(*[r[0] for r in rows], *params, *[c[0] for c in cots], *[e[0] for e in extra])
    return res[:len(want)], res[len(want):]


def _f_mod(x, sh, sc):
    return (_modulate(x, sh, sc),)


def _f_res_mod(x, y, g, sh, sc):
    x1 = x + g * y
    return x1, _modulate(x1, sh, sc)


def _f_res_mod2(x, y, g, sh_a, sc_a, sh_b, sc_b):
    x1 = x + g * y
    return x1, _modulate(x1, sh_a, sc_a), _modulate(x1, sh_b, sc_b)


def _f_qnorm(p, g):
    return (_rms(p) * g * (HEAD ** -0.5),)


def _f_knorm(p, g):
    return (_rms(p) * g,)


def _f_qnorm_aug(p, g):
    lane = lax.broadcasted_iota(jnp.int32, p.shape, 1)
    return (jnp.concatenate([_rms(p) * g * (HEAD ** -0.5), jnp.where(lane < 3, 1.0, 0.0)], axis=1),)


def _f_knorm_aug(p, c0, c1, c2, g):
    lane = lax.broadcasted_iota(jnp.int32, p.shape, 1)
    aug = jnp.where(lane == 0, c0, jnp.where(lane == 1, c1, jnp.where(lane == 2, c2, 0.0)))
    return (jnp.concatenate([_rms(p) * g, aug], axis=1),)


def _split3(a):
    round_bf16 = lambda v: lax.reduce_precision(v, exponent_bits=8, mantissa_bits=7)
    hi = round_bf16(a)
    mid = round_bf16(a - hi)
    lo = round_bf16(a - hi - mid)
    return hi.astype(BF16), mid.astype(BF16), lo.astype(BF16)


def _f_outgate(o, og):
    return (o * _sigmoid(og),)


def _loss_call(x3, f, g2, target, tb):
    t, d = x3.shape
    tb = min(tb, t)

    def body(x_ref, f_ref, g_ref, t_ref, loss_ref, dx_ref, df_ref, dg_ref):
        i = pl.program_id(0)
        fv = f_ref[...]
        g = g_ref[...]
        e = x_ref[...] + g * fv - t_ref[...]
        dx = e * (1.0 / d)
        part = 0.5 * jnp.sum(jnp.sum(e * dx, axis=1, keepdims=True), axis=0, keepdims=True)
        dx_ref[...] = dx
        df_ref[...] = (g * dx).astype(df_ref.dtype)
        dg = jnp.sum(dx * fv, axis=0, keepdims=True)

        @pl.when(i == 0)
        def _():
            loss_ref[...] = jnp.broadcast_to(part, loss_ref.shape)
            dg_ref[...] = dg

        @pl.when(i > 0)
        def _():
            loss_ref[...] += jnp.broadcast_to(part, loss_ref.shape)
            dg_ref[...] += dg

    row = pl.BlockSpec((tb, d), lambda i: (i, 0))
    vec = pl.BlockSpec((1, d), lambda i: (0, 0))
    return pl.pallas_call(
        body, name="loss_head",
        grid=(t // tb,),
        in_specs=[row, row, vec, row],
        out_specs=[pl.BlockSpec((1, LANES), lambda i: (0, 0)), row, row, vec],
        out_shape=[jax.ShapeDtypeStruct((1, LANES), F32), jax.ShapeDtypeStruct((t, d), F32),
                   jax.ShapeDtypeStruct((t, d), BF16), jax.ShapeDtypeStruct((1, d), F32)],
        compiler_params=_cparams(dimension_semantics=("arbitrary",)),
    )(x3, f, g2, target)


def _hg_mask(tb):
    br = lax.broadcasted_iota(jnp.int32, (tb, tb), 0)
    bs = lax.broadcasted_iota(jnp.int32, (tb, tb), 1)
    return jnp.logical_and(br // A_CHUNK == bs // A_CHUNK, bs <= br).astype(F32)


def _hg_consts(mask):
    c = A_CHUNK
    r = lax.broadcasted_iota(jnp.int32, (c, c), 0)
    s = lax.broadcasted_iota(jnp.int32, (c, c), 1)
    return (s <= r).astype(F32), (r <= s).astype(F32), mask > 0.5


def _chunk_apply(mat, x):
    c = mat.shape[0]
    return jnp.concatenate([_f32dot(mat, x[i * c:(i + 1) * c]) for i in range(x.shape[0] // c)], axis=0)


@jax.custom_vjp
def _chunk_cumsum(x, tri, tri_t):
    return _chunk_apply(tri, x)


_chunk_cumsum.defvjp(lambda x, tri, tri_t: (_chunk_apply(tri, x), (tri, tri_t)),
                     lambda r, g: (_chunk_apply(r[1], g), jnp.zeros_like(r[0]), jnp.zeros_like(r[1])))


def _per_chunk(a, b, dims):
    return jnp.stack([_bdot_raw(a[i], b[i], dims) for i in range(a.shape[0])])


@jax.custom_vjp
def _chunk_tn(a, b):
    return _per_chunk(a, b, _TN)


@jax.custom_vjp
def _chunk_nt(a, b):
    return _per_chunk(a, b, _NT)


@jax.custom_vjp
def _chunk_nn(a, b):
    return _per_chunk(a, b, _NN)


_chunk_tn.defvjp(lambda a, b: (_per_chunk(a, b, _TN), (a, b)),
                 lambda r, g: (_chunk_nt(r[1], g), _chunk_nn(r[0], g)))
_chunk_nt.defvjp(lambda a, b: (_per_chunk(a, b, _NT), (a, b)),
                 lambda r, g: (_chunk_nn(g, r[1]), _chunk_tn(g, r[0])))
_chunk_nn.defvjp(lambda a, b: (_per_chunk(a, b, _NN), (a, b)),
                 lambda r, g: (_chunk_nt(g, r[1]), _chunk_tn(r[0], g)))


def _scan_states(decay, m, st):
    sts = []
    for i in range(m.shape[0]):
        sts.append(st)
        st = st * decay[i] + m[i]
    return jnp.stack(sts), st


@jax.custom_vjp
def _state_scan(decay, m, st):
    return _scan_states(decay, m, st)


def _state_scan_fwd(decay, m, st):
    sts, st_out = _scan_states(decay, m, st)
    return (sts, st_out), (decay, sts)


def _state_scan_bwd(res, cts):
    decay, sts = res
    d_sts, g = cts
    d_decay, d_m = [], []
    for i in range(sts.shape[0] - 1, -1, -1):
        d_m.append(g)
        d_decay.append(jnp.sum(g * sts[i], axis=0, keepdims=True))
        g = g * decay[i] + d_sts[i]
    return jnp.stack(d_decay[::-1]), jnp.stack(d_m[::-1]), g


_state_scan.defvjp(_state_scan_fwd, _state_scan_bwd)


def _hg_block(qp, fp, ip, gp, lb, ng, st, tri, tri_t, bd_causal):
    tb = qp.shape[0]
    c = A_CHUNK
    n = tb // c
    q = _silu(qp)
    fg = lb + (1.0 - lb) * _sigmoid(fp)
    logf = jnp.log(fg)
    k = 1.0 - fg
    b3 = _chunk_cumsum(logf, tri, tri_t).reshape(n, c, HEAD)
    pos = lax.broadcasted_iota(jnp.int32, (1, c, 1), 1)
    b_mid = lax.stop_gradient(jnp.sum(jnp.where(pos == c // 2, b3, 0.0), axis=1, keepdims=True))
    b_last = jnp.sum(jnp.where(pos == c - 1, b3, 0.0), axis=1, keepdims=True)
    q3, k3, v3 = q.reshape(n, c, HEAD), k.reshape(n, c, HEAD), ip.reshape(n, c, HEAD)
    scores = _dot_nt((q3 * jnp.exp(b3 - b_mid)).reshape(tb, HEAD), (k3 * jnp.exp(b_mid - b3)).reshape(tb, HEAD))
    o_intra = _dot_nn(jnp.where(bd_causal, scores, 0.0), ip)
    states, st_new = _state_scan(jnp.exp(b_last), _chunk_tn(v3, k3 * jnp.exp(b_last - b3)), st)
    o = o_intra + _chunk_nt(q3 * jnp.exp(b3), states).reshape(tb, HEAD)
    y = _rms(o) * ng * _silu(gp)
    return y, st_new


HG_HEADS = 2


def _hg_specs(tb, nh, rev_nb=None):
    wide = HG_HEADS * HEAD
    per = nh // HG_HEADS

    def row(part):
        if rev_nb is None:
            return pl.BlockSpec((tb, wide), functools.partial(lambda h, i, off: (i, off + h), off=part * per))
        return pl.BlockSpec((tb, wide),
                            functools.partial(lambda h, i, off: (rev_nb - 1 - i, off + h), off=part * per))
    return [row(0), row(1), row(2), row(3),
            pl.BlockSpec((1, wide), lambda h, i: (0, h)), pl.BlockSpec((1, HEAD), lambda h, i: (0, 0)),
            pl.BlockSpec((tb, tb), lambda h, i: (0, 0))]


def _hgrn2_fwd(proj, lb, ng, tb):
    t = proj.shape[0]
    nh = proj.shape[1] // (4 * HEAD)
    tb = min(tb, t)
    nb = t // tb
    wide = HG_HEADS * HEAD

    def body(q_ref, f_ref, i_ref, g_ref, lb_ref, ng_ref, mask_ref, y_ref, s_ref, st_ref):
        i = pl.program_id(1)

        @pl.when(i == 0)
        def _():
            st_ref[...] = jnp.zeros_like(st_ref)

        consts = _hg_consts(mask_ref[...])
        for p in range(HG_HEADS):
            cs = slice(p * HEAD, (p + 1) * HEAD)
            st = st_ref[p]
            s_ref[p, 0] = st
            y, st_new = _hg_block(q_ref[:, cs], f_ref[:, cs], i_ref[:, cs], g_ref[:, cs], lb_ref[:, cs],
                                  ng_ref[...], st, *consts)
            y_ref[:, cs] = y.astype(y_ref.dtype)
            st_ref[p] = st_new

    return pl.pallas_call(
        body, name="hgrn2_fwd",
        grid=(nh // HG_HEADS, nb),
        in_specs=_hg_specs(tb, nh),
        out_specs=[pl.BlockSpec((tb, wide), lambda h, i: (i, h)),
                   pl.BlockSpec((HG_HEADS, 1, HEAD, HEAD), lambda h, i: (h, i, 0, 0))],
        out_shape=[jax.ShapeDtypeStruct((t, nh * HEAD), BF16),
                   jax.ShapeDtypeStruct((nh, nb, HEAD, HEAD), F32)],
        scratch_shapes=[pltpu.VMEM((HG_HEADS, HEAD, HEAD), F32)],
        compiler_params=_cparams(dimension_semantics=("parallel", "arbitrary")),
    )(proj, proj, proj, proj, lb, ng, _hg_mask(tb))


def _hgrn2_bwd(proj, lb, ng, states, dy, tb):
    t = proj.shape[0]
    nh = proj.shape[1] // (4 * HEAD)
    tb = min(tb, t)
    nb = t // tb
    wide = HG_HEADS * HEAD

    def body(q_ref, f_ref, i_ref, g_ref, lb_ref, ng_ref, mask_ref, s_ref, dy_ref,
             dp_ref, dlb_ref, dng_ref, dst_ref):
        h, i = pl.program_id(0), pl.program_id(1)
        consts = _hg_consts(mask_ref[...])

        @pl.when(i == 0)
        def _():
            dst_ref[...] = jnp.zeros_like(dst_ref)
            dlb_ref[...] = jnp.zeros_like(dlb_ref)

        @pl.when(jnp.logical_and(i == 0, h == 0))
        def _():
            dng_ref[...] = jnp.zeros_like(dng_ref)

        def fn(qp, fp, ip, gp, lbx, ngx, stx):
            return _hg_block(qp, fp, ip, gp, lbx, ngx, stx, *consts)

        for p in range(HG_HEADS):
            cs = slice(p * HEAD, (p + 1) * HEAD)
            _, vjp_fn = jax.vjp(fn, q_ref[:, cs], f_ref[:, cs], i_ref[:, cs], g_ref[:, cs], lb_ref[:, cs],
                                ng_ref[...], s_ref[p, 0])
            *gparts, glb, gng, dst = vjp_fn((dy_ref[:, cs].astype(F32), dst_ref[p]))
            for part, gpart in enumerate(gparts):
                dp_ref[part, :, cs] = gpart.astype(dp_ref.dtype)
            dst_ref[p] = dst
            dlb_ref[:, cs] += glb
            dng_ref[...] += gng

    rev = lambda h, i: (nb - 1 - i, h)
    return pl.pallas_call(
        body, name="hgrn2_bwd",
        grid=(nh // HG_HEADS, nb),
        in_specs=_hg_specs(tb, nh, rev_nb=nb) + [
            pl.BlockSpec((HG_HEADS, 1, HEAD, HEAD), lambda h, i: (h, nb - 1 - i, 0, 0)),
            pl.BlockSpec((tb, wide), rev)],
        out_specs=[pl.BlockSpec((4, tb, wide), lambda h, i: (0, nb - 1 - i, h)),
                   pl.BlockSpec((1, wide), lambda h, i: (0, h)), pl.BlockSpec((1, HEAD), lambda h, i: (0, 0))],
        out_shape=[jax.ShapeDtypeStruct((4, t, nh * HEAD), BF16),
                   jax.ShapeDtypeStruct((1, nh * HEAD), F32), jax.ShapeDtypeStruct((1, HEAD), F32)],
        scratch_shapes=[pltpu.VMEM((HG_HEADS, HEAD, HEAD), F32)],
        compiler_params=_cparams(dimension_semantics=("arbitrary", "arbitrary")),
    )(proj, proj, proj, proj, lb, ng, _hg_mask(tb), states, dy)


def _fgate_consts(cb):
    r = lax.broadcasted_iota(jnp.int32, (cb, cb), 0)
    s = lax.broadcasted_iota(jnp.int32, (cb, cb), 1)
    return (r <= s).astype(F32), (r >= s).astype(F32)


def _fgate_fwd(xt, bias, cb=512):
    nh, t = xt.shape
    cb = min(cb, t)

    def body(x_ref, b_ref, o_ref):
        upper, _ = _fgate_consts(cb)
        carry = jnp.zeros((nh, 1), F32)
        for blk in range(t // cb):
            z = x_ref[:, blk * cb:(blk + 1) * cb] + b_ref[...]
            logf = jnp.minimum(z, 0.0) - jnp.log(1.0 + jnp.exp(-jnp.abs(z)))
            cs = _f32dot(logf, upper) + carry
            o_ref[:, blk * cb:(blk + 1) * cb] = cs
            carry = cs[:, cb - 1:cb]

    vm = pl.BlockSpec(memory_space=pltpu.VMEM)
    return pl.pallas_call(
        body, name="fgate_fwd", in_specs=[vm, vm], out_specs=vm,
        out_shape=jax.ShapeDtypeStruct((nh, t), F32), compiler_params=_cparams(),
    )(xt, bias)


def _fgate_bwd(xt, bias, dft, cb=512):
    nh, t = xt.shape
    cb = min(cb, t)
    nblk = t // cb

    def body(x_ref, b_ref, d_ref, dx_ref, db_ref):
        _, lower = _fgate_consts(cb)
        carry = jnp.zeros((nh, 1), F32)
        db = jnp.zeros((nh, 1), F32)
        for blk in range(nblk - 1, -1, -1):
            sl = slice(blk * cb, (blk + 1) * cb)
            dlogf = _f32dot(d_ref[:, sl], lower) + carry
            carry = dlogf[:, 0:1]
            z = x_ref[:, sl] + b_ref[...]
            dz = dlogf * (1.0 - _sigmoid(z))
            dx_ref[:, sl] = dz
            db = db + jnp.sum(dz, axis=1, keepdims=True)
        db_ref[...] = db

    vm = pl.BlockSpec(memory_space=pltpu.VMEM)
    return pl.pallas_call(
        body, name="fgate_bwd", in_specs=[vm, vm, vm], out_specs=[vm, vm],
        out_shape=[jax.ShapeDtypeStruct((nh, t), F32), jax.ShapeDtypeStruct((nh, 1), F32)],
        compiler_params=_cparams(),
    )(xt, bias, dft)


ATTN_GROUPS = 4
ATTN_FWD_HEADS = 2


def _attn_fwd(q, k, v, f_grp, blk):
    t, width = v.shape
    nh = width // HEAD
    nq = t // blk
    hpg = nh // ATTN_GROUPS

    def body(q_ref, k_ref, v_ref, fc_ref, o_ref, lse_ref):
        i = pl.program_id(0)
        tri = (lax.broadcasted_iota(jnp.int32, (blk, blk), 1) <= lax.broadcasted_iota(jnp.int32, (blk, blk), 0))
        for h0 in range(0, nh, ATTN_FWD_HEADS):
            heads = range(h0, min(h0 + ATTN_FWD_HEADS, nh))

            def tile(j, carries, masked):
                rs = pl.ds(pl.multiple_of(j * blk, blk), blk)
                out = []
                for h, (m, l, acc) in zip(heads, carries):
                    cs = slice(h * HEAD, (h + 1) * HEAD)
                    cs2 = slice(2 * h * HEAD, 2 * (h + 1) * HEAD)
                    s = _bdot_raw(q_ref[:, cs2], k_ref[rs, cs2], _NT)
                    if masked:
                        s = jnp.where(tri, s, NEG_INF)
                    m_new = jnp.maximum(m, jnp.max(s, axis=1, keepdims=True))
                    p = jnp.exp(s - m_new)
                    alpha = jnp.exp(m - m_new)
                    l_new = alpha * l + jnp.sum(p, axis=1, keepdims=True)
                    out.append((m_new, l_new, alpha * acc + _bdot_raw(p, v_ref[rs, cs], _NN)))
                return tuple(out)

            init = tuple((jnp.full((blk, 1), NEG_INF, F32), jnp.zeros((blk, 1), F32), jnp.zeros((blk, HEAD), F32))
                         for _ in heads)
            carries = lax.fori_loop(0, i, lambda j, c: tile(j, c, False), init)
            for h, (m, l, acc) in zip(heads, tile(i, carries, True)):
                o_ref[:, h * HEAD:(h + 1) * HEAD] = acc / l
                g, hh = divmod(h, hpg)
                lse_ref[g, :, hh:hh + 1] = m + jnp.log(l) + fc_ref[g, :, hh:hh + 1]

    vm = pl.BlockSpec(memory_space=pltpu.VMEM)
    stat = pl.BlockSpec((ATTN_GROUPS, blk, hpg), lambda i: (0, i, 0))
    return pl.pallas_call(
        body, name="fox_attn_fwd",
        grid=(nq,),
        in_specs=[pl.BlockSpec((blk, 2 * width), lambda i: (i, 0)), vm, vm, stat],
        out_specs=[pl.BlockSpec((blk, width), lambda i: (i, 0)), stat],
        out_shape=[jax.ShapeDtypeStruct((t, width), F32), jax.ShapeDtypeStruct((ATTN_GROUPS, t, hpg), F32)],
        compiler_params=_cparams(dimension_semantics=("parallel",)),
    )(q, k, v, f_grp)


def _outgate_bwd(o, proj_q, dz, tb):
    t, width = o.shape
    nh = width // HEAD
    hpg = nh // ATTN_GROUPS
    tb = min(tb, t)

    def body(o_ref, og_ref, dz_ref, do_ref, dog_ref, dl_ref):
        for h in range(nh):
            cs = slice(h * HEAD, (h + 1) * HEAD)
            ov = o_ref[:, cs]
            _, vjp_fn = jax.vjp(_f_outgate, ov, og_ref[:, cs])
            do, dog = vjp_fn((dz_ref[:, cs].astype(F32),))
            do = do.astype(do_ref.dtype)
            do_ref[:, cs] = do
            dog_ref[:, cs] = dog.astype(dog_ref.dtype)
            g, hh = divmod(h, hpg)
            dl_ref[g, :, hh:hh + 1] = jnp.sum(do.astype(F32) * ov, axis=1, keepdims=True)

    wide = pl.BlockSpec((tb, width), lambda i: (i, 0))
    return pl.pallas_call(body, name="out_gate_bwd", grid=(t // tb,),
                          in_specs=[wide, pl.BlockSpec((tb, width), lambda i: (i, 1)), wide],
                          out_specs=[wide, wide, pl.BlockSpec((ATTN_GROUPS, tb, hpg), lambda i: (0, i, 0))],
                          out_shape=[jax.ShapeDtypeStruct((t, width), BF16), jax.ShapeDtypeStruct((t, width), BF16),
                                     jax.ShapeDtypeStruct((ATTN_GROUPS, t, hpg), F32)],
                          compiler_params=_cparams(dimension_semantics=("parallel",)))(o, proj_q, dz)


def _qnorm_bwd(proj_q, gain, dq_n, dog, tb):
    t, width = dq_n.shape
    nh = width // HEAD
    tb = min(tb, t)

    def body(p_ref, g_ref, dq_ref, dog_ref, out_ref, dg_ref):
        i = pl.program_id(0)
        gv = g_ref[...]
        acc = None
        for h in range(nh):
            cs = slice(h * HEAD, (h + 1) * HEAD)
            _, vjp_fn = jax.vjp(_f_qnorm, p_ref[:, cs], gv)
            dp, dg = vjp_fn((dq_ref[:, cs],))
            out_ref[:, cs] = dp.astype(out_ref.dtype)
            acc = dg if acc is None else acc + dg
        out_ref[:, width:] = dog_ref[...]

        @pl.when(i == 0)
        def _():
            dg_ref[...] = acc

        @pl.when(i > 0)
        def _():
            dg_ref[...] += acc

    wide = pl.BlockSpec((tb, width), lambda i: (i, 0))
    vec = pl.BlockSpec(gain.shape, lambda i: (0, 0))
    return pl.pallas_call(body, name="q_norm_bwd", grid=(t // tb,), in_specs=[wide, vec, wide, wide],
                          out_specs=[pl.BlockSpec((tb, 2 * width), lambda i: (i, 0)), vec],
                          out_shape=[jax.ShapeDtypeStruct((t, 2 * width), BF16), jax.ShapeDtypeStruct(gain.shape, F32)],
                          compiler_params=_cparams(dimension_semantics=("arbitrary",)))(proj_q, gain, dq_n, dog)


def _attn_bwd(q, k, v, f_grp, do, lse, delta, blk):
    t, width = v.shape
    nh = width // HEAD
    nq = t // blk
    hpg = nh // ATTN_GROUPS
    gw = hpg * HEAD

    def body(q_ref, do_ref, k_ref, v_ref, fc_ref, lse_ref, dl_ref,
             dq_ref, dk_ref, dv_ref, dfc_ref, dfr_ref):
        g, j = pl.program_id(0), pl.program_id(1)
        tri = (lax.broadcasted_iota(jnp.int32, (blk, blk), 1) <= lax.broadcasted_iota(jnp.int32, (blk, blk), 0))

        @pl.when(j == 0)
        def _():
            dq_ref[...] = jnp.zeros_like(dq_ref)
            dfc_ref[...] = jnp.zeros_like(dfc_ref)

        def tile(i, carries, masked):
            rs = pl.ds(pl.multiple_of(i * blk, blk), blk)
            out = []
            for h, (dk, dv, dfs) in enumerate(carries):
                cs = slice(h * HEAD, (h + 1) * HEAD)
                cs2 = slice(2 * h * HEAD, 2 * (h + 1) * HEAD)
                csq = slice(2 * h * HEAD, (2 * h + 1) * HEAD)
                qi = q_ref[rs, csq]
                doi = do_ref[rs, cs]
                bias = fc_ref[0, rs, h:h + 1] - lse_ref[0, rs, h:h + 1]
                p = jnp.exp(_bdot_raw(q_ref[rs, cs2], k_ref[:, cs2], _NT) + bias)
                if masked:
                    p = jnp.where(tri, p, 0.0)
                ds = p * (_bdot_raw(doi, v_ref[:, cs], _NT) - dl_ref[0, rs, h:h + 1])
                dsb = ds.astype(BF16)
                dq_ref[rs, cs] += _bdot_raw(dsb, k_ref[:, csq], _NN)
                dfc_ref[0, rs, h:h + 1] += jnp.sum(ds, axis=1, keepdims=True)
                out.append((dk + _bdot_raw(dsb, qi, _TN), dv + _bdot_raw(p, doi, _TN),
                            dfs - jnp.sum(ds, axis=0, keepdims=True)))
            return tuple(out)

        init = tuple((jnp.zeros((blk, HEAD), F32), jnp.zeros((blk, HEAD), F32), jnp.zeros((1, blk), F32))
                     for _ in range(hpg))
        carries = lax.fori_loop(j + 1, nq, lambda i, c: tile(i, c, False), tile(j, init, True))
        for h, (dk, dv, dfs) in enumerate(carries):
            cs = slice(h * HEAD, (h + 1) * HEAD)
            dk_ref[:, cs] = dk
            dv_ref[:, cs] = dv.astype(dv_ref.dtype)
            dfr_ref[0, 0, h:h + 1, :] = dfs

    once = pl.Buffered(1)
    stat = pl.BlockSpec((1, t, hpg), lambda g, j: (g, 0, 0), pipeline_mode=once)
    kv_blk = pl.BlockSpec((blk, gw), lambda g, j: (j, g))
    frow = pl.BlockSpec((1, 1, hpg, blk), lambda g, j: (g, j, 0, 0))
    dq, dk, dv, dfc, dfr = pl.pallas_call(
        body, name="fox_attn_bwd",
        grid=(ATTN_GROUPS, nq),
        in_specs=[pl.BlockSpec((t, 2 * gw), lambda g, j: (0, g), pipeline_mode=once),
                  pl.BlockSpec((t, gw), lambda g, j: (0, g), pipeline_mode=once),
                  pl.BlockSpec((blk, 2 * gw), lambda g, j: (j, g)), kv_blk, stat, stat, stat],
        out_specs=[pl.BlockSpec((t, gw), lambda g, j: (0, g)), kv_blk, kv_blk,
                   pl.BlockSpec((1, t, hpg), lambda g, j: (g, 0, 0)), frow],
        out_shape=[jax.ShapeDtypeStruct((t, width), F32), jax.ShapeDtypeStruct((t, width), F32),
                   jax.ShapeDtypeStruct((t, width), BF16), jax.ShapeDtypeStruct((ATTN_GROUPS, t, hpg), F32),
                   jax.ShapeDtypeStruct((ATTN_GROUPS, nq, hpg, blk), F32)],
        compiler_params=_cparams(dimension_semantics=("parallel", "arbitrary")),
    )(q, do, k, v, f_grp, lse, delta)
    return dq, dk, dv, dfc, dfr


SUBLANES = 8


def _shift_down(u, n):
    r = pltpu.roll(u, n, 0)
    row = lax.broadcasted_iota(jnp.int32, (SUBLANES, u.shape[1]), 0)
    return jnp.concatenate([jnp.where(row < n, 0.0, r[:SUBLANES]), r[SUBLANES:]], axis=0)


def _shift_up(u, n):
    t = u.shape[0]
    r = pltpu.roll(u, t - n, 0)
    row = lax.broadcasted_iota(jnp.int32, (SUBLANES, u.shape[1]), 0)
    return jnp.concatenate([r[:t - SUBLANES], jnp.where(row >= SUBLANES - n, 0.0, r[t - SUBLANES:])], axis=0)


def _convglu_specs(t):
    return [pl.BlockSpec((2, t, LANES), lambda j: (0, 0, j)),
            pl.BlockSpec((2, CONV_TAPS, LANES), lambda j: (0, 0, j)),
            pl.BlockSpec((2, 1, LANES), lambda j: (0, 0, j))]


def _convglu_fwd(u, cw, cb):
    _, t, fp = u.shape

    def body(u_ref, w_ref, b_ref, a_ref, c_ref):
        c = []
        for hf in range(2):
            uv, w = u_ref[hf].astype(F32), w_ref[hf]
            c.append(w[0:1] * _shift_down(uv, 2) + w[1:2] * _shift_down(uv, 1) + w[2:3] * uv + b_ref[hf])
            c_ref[hf] = c[hf].astype(c_ref.dtype)
        a_ref[...] = (_silu(c[0]) * c[1]).astype(a_ref.dtype)

    return pl.pallas_call(
        body, name="convglu_fwd",
        grid=(fp // LANES,),
        in_specs=_convglu_specs(t),
        out_specs=[pl.BlockSpec((t, LANES), lambda j: (0, j)), pl.BlockSpec((2, t, LANES), lambda j: (0, 0, j))],
        out_shape=[jax.ShapeDtypeStruct((t, fp), BF16), jax.ShapeDtypeStruct((2, t, fp), BF16)],
        compiler_params=_cparams(dimension_semantics=("parallel",)),
    )(u, cw, cb)


def _convglu_bwd(u, c, cw, da):
    _, t, fp = u.shape

    def body(u_ref, c_ref, w_ref, da_ref, du_ref, dw_ref, db_ref):
        gc, vc = c_ref[0].astype(F32), c_ref[1].astype(F32)
        sg = _sigmoid(gc)
        dav = da_ref[...].astype(F32)
        dcs = [dav * vc * (sg * (1.0 + gc * (1.0 - sg))), dav * (gc * sg)]
        for hf in range(2):
            dc, w, uv = dcs[hf], w_ref[hf], u_ref[hf].astype(F32)
            dc1, dc2 = _shift_up(dc, 1), _shift_up(dc, 2)
            du_ref[hf] = (w[2:3] * dc + w[1:2] * dc1 + w[0:1] * dc2).astype(du_ref.dtype)
            dw_ref[hf, 0:1, :] = jnp.sum(dc2 * uv, axis=0, keepdims=True)
            dw_ref[hf, 1:2, :] = jnp.sum(dc1 * uv, axis=0, keepdims=True)
            dw_ref[hf, 2:3, :] = jnp.sum(dc * uv, axis=0, keepdims=True)
            db_ref[hf] = jnp.sum(dc, axis=0, keepdims=True)

    pair, taps, bias = _convglu_specs(t)
    return pl.pallas_call(
        body, name="convglu_bwd",
        grid=(fp // LANES,),
        in_specs=[pair, pair, taps, pl.BlockSpec((t, LANES), lambda j: (0, j))],
        out_specs=[pair, taps, bias],
        out_shape=[jax.ShapeDtypeStruct((2, t, fp), BF16), jax.ShapeDtypeStruct((2, CONV_TAPS, fp), F32),
                   jax.ShapeDtypeStruct((2, 1, fp), F32)],
        compiler_params=_cparams(dimension_semantics=("parallel",)),
    )(u, c, cw, da)


def _local_step(x, target, mods, lb, small, pre_w, get_w, put_g, *, tb=512, attn_blk=512):
    t, d = x.shape
    nh = d // HEAD
    nb = NDEV
    wts = {}
    vec = lambda *names: [mods[n] for n in names]

    def ffn_fwd(h2, l):
        u = _mm_wblk(h2, wts[f"up{l}"], BF16, f"ffn{l}_up", gb=nb // 2, split=2, tm=512)
        a, c = _convglu_fwd(u, small[f"conv_w{l}"], small[f"conv_b{l}"])
        f = _mm(a, wts[f"down{l}"], "nn", F32, f"ffn{l}_down", tk=4096)
        return (u, c), a, f

    def ffn_bwd(df, h2, uc, a, l):
        u, c = uc
        da = _mm(df, wts[f"down{l}"], "nt", BF16, f"ffn{l}_down_dx", tn=1536)
        dwd = _mm(a, df, "tn", BF16, f"ffn{l}_down_dw", tm=768, tk=t)
        du, dcw, dcb = _convglu_bwd(u, c, small[f"conv_w{l}"], da)
        dh2 = _mm_wblk_dx(du, wts[f"up{l}"], BF16, f"ffn{l}_up_dx", k=d, gb=nb // 2, split=2, tm=1024)
        dwu = _mm_wblk_dw(h2, du, f"ffn{l}_up_dw", nb=nb, gb=1, split=2, tk=t)
        return dh2, dwu, dwd, dcw, dcb

    (h_a,) = _row_fwd(_f_mod, [(x, d, 0)], vec("sh1_0", "sc1_0"), [BF16], tb=tb, name="l0_mod1")
    wts.update(get_w("l0a", h_a))
    proj_a = _mm_wblk(h_a, wts["a_in"], F32, "a_in", gb=nb // 2)
    ypre, states = _hgrn2_fwd(proj_a, lb, small["a_norm_g"], tb)
    pre_w("l0b", ypre)
    wts.update(get_w("l0b", ypre))
    y_a = _mm(ypre, wts["a_out"], "nn", F32, "a_out")
    x1, h2_0 = _row_fwd(_f_res_mod, [(x, d, 0), (y_a, d, 0)], vec("g1_0", "sh2_0", "sc2_0"), [F32, BF16],
                        tb=tb, name="l0_res_mod2")
    wts.update(get_w("l0b_ffn", h2_0))
    u0, a0, f0 = ffn_fwd(h2_0, 0)
    x2, h_kv, h_q = _row_fwd(_f_res_mod2, [(x1, d, 0), (f0, d, 0)],
                             [mods["g2_0"] + pre_w("l1", f0)] + vec("kv_sh", "kv_sc", "sh1_1", "sc1_1"),
                             [F32, BF16, BF16], tb=tb, name="l0_res_kvmod_qmod")
    wts.update(get_w("l1", h_kv))
    proj_k = _mm(h_kv, wts["kv_k"], "nt", F32, "k_proj")
    v_b = _mm(h_kv, wts["kv_v"], "nt", BF16, "v_proj")
    proj_f = _mm(h_kv, wts["kv_f"], "nt", F32, "kv_fproj")
    f_logit_t = proj_f[:, :nh].T
    f_bias = small["kv_b_f"].reshape(nh, 1)
    f_t = _fgate_fwd(f_logit_t, f_bias)
    f_grp = f_t.reshape(ATTN_GROUPS, nh // ATTN_GROUPS, t).transpose(0, 2, 1)
    (k_n,) = _row_fwd(_f_knorm_aug, [(proj_k, HEAD, 0)] + [(piece, 1, 0) for piece in _split3(-f_t.T)],
                      [small["k_norm_g"]], [BF16], nsub=nh, tb=tb, name="k_norm")
    proj_q = _mm_wblk(h_q, wts["b_q"], F32, "b_q", gb=nb)
    (q_n,) = _row_fwd(_f_qnorm_aug, [(proj_q, HEAD, 0)], [small["q_norm_g"]], [BF16], nsub=nh, tb=tb,
                      name="q_norm")
    o_att, lse = _attn_fwd(q_n, k_n, v_b, f_grp, attn_blk)
    (z,) = _row_fwd(_f_outgate, [(o_att, HEAD, 0), (proj_q, HEAD, 1)], [], [BF16], nsub=nh, tb=tb, name="out_gate")
    y_b = _mm(z, wts["b_out"], "nn", F32, "b_out")
    x3, h2_1 = _row_fwd(_f_res_mod, [(x2, d, 0), (y_b, d, 0)], vec("g1_1", "sh2_1", "sc2_1"), [F32, BF16],
                        tb=tb, name="l1_res_mod2")
    u1, a1, f1 = ffn_fwd(h2_1, 1)
    loss, dx4, df1, dg2_1 = _loss_call(x3, f1, mods["g2_1"], target, tb)

    g = {}
    dmods = {"g2_1": dg2_1}
    dh2, g["up1"], g["down1"], g["conv_w1"], g["conv_b1"] = ffn_bwd(df1, h2_1, u1, a1, 1)
    (dx2, dy_b), (dmods["g1_1"], dmods["sh2_1"], dmods["sc2_1"]) = _row_bwd(
        _f_res_mod, [(x2, d, 0), (y_b, d, 0)], vec("g1_1", "sh2_1", "sc2_1"),
        [(dx4, d, 0), (dh2, d, 0)], [F32, BF16], tb=tb, name="l1_res_mod2_bwd")
    dz = _mm(dy_b, wts["b_out"], "nt", BF16, "b_out_dx")
    g["b_out"] = _mm(z, dy_b, "tn", BF16, "b_out_dw", tk=t)
    do_att, dog, delta = _outgate_bwd(o_att, proj_q, dz, tb)
    dq_n, dk_n, dv, dfc_q, dfr_k = _attn_bwd(q_n, k_n, v_b, f_grp, do_att, lse, delta, attn_blk)
    dproj_q, g["q_norm_g"] = _qnorm_bwd(proj_q, small["q_norm_g"], dq_n, dog, tb)
    dh_q = _mm_wblk_dx(dproj_q, wts["b_q"], BF16, "b_q_dx", k=d, gb=nb)
    g["b_q"] = _mm_wblk_dw(h_q, dproj_q, "b_q_dw", nb=nb, gb=nb // 4, tk=t)
    (dpk,), (g["k_norm_g"],) = _row_bwd(_f_knorm, [(proj_k, HEAD, 0)], [small["k_norm_g"]],
                                        [(dk_n, HEAD, 0)], [BF16], nsub=nh, tb=tb, name="k_norm_bwd")
    df_t = dfc_q.transpose(0, 2, 1).reshape(nh, t) + dfr_k.transpose(0, 2, 1, 3).reshape(nh, t)
    dflogit_t, g["kv_b_f"] = _fgate_bwd(f_logit_t, f_bias, df_t)
    dproj_f = jnp.pad(dflogit_t.T, ((0, 0), (0, LANES - nh))).astype(BF16)
    dh_kv = _mm(dpk, wts["kv_k"], "nn", BF16, "k_proj_dx")
    dh_kv_v = _mm(dv, wts["kv_v"], "nn", BF16, "v_proj_dx")
    dh_kv_f = _mm(dproj_f, wts["kv_f"], "nn", BF16, "kv_fproj_dx")
    g["kv_k"] = _mm(dpk, h_kv, "tn", BF16, "k_proj_dw", tk=t)
    g["kv_v"] = _mm(dv, h_kv, "tn", BF16, "v_proj_dw", tk=t)
    g["kv_f"] = _mm(dproj_f, h_kv, "tn", F32, "kv_fproj_dw", tk=1024)
    sent = put_g("l1", {n: g.pop(n) for n in ("b_out", "b_q", "kv_k", "kv_v", "kv_f", "up1", "down1")})
    (dx1, df0), (dmods["g2_0"], dmods["kv_sh"], dmods["kv_sc"], dmods["sh1_1"], dmods["sc1_1"]) = _row_bwd(
        _f_res_mod2, [(x1, d, 0), (f0, d, 0)], [mods["g2_0"] + sent] + vec("kv_sh", "kv_sc", "sh1_1", "sc1_1"),
        [(dx2, d, 0), (dh_kv, d, 0), (dh_q, d, 0)], [F32, BF16], tb=tb, name="l0_res_kvmod_qmod_bwd",
        cot_add=[(1, dh_kv_v), (1, dh_kv_f)])
    dh2, g["up0"], g["down0"], g["conv_w0"], g["conv_b0"] = ffn_bwd(df0, h2_0, u0, a0, 0)
    (dx0, dy_a), (dmods["g1_0"], dmods["sh2_0"], dmods["sc2_0"]) = _row_bwd(
        _f_res_mod, [(x, d, 0), (y_a, d, 0)], vec("g1_0", "sh2_0", "sc2_0"),
        [(dx1, d, 0), (dh2, d, 0)], [F32, BF16], tb=tb, name="l0_res_mod2_bwd")
    dypre = _mm(dy_a, wts["a_out"], "nt", BF16, "a_out_dx")
    g["a_out"] = _mm(ypre, dy_a, "tn", BF16, "a_out_dw", tk=t)
    sent = put_g("l0b", {n: g.pop(n) for n in ("a_out", "up0", "down0")})
    dproj_a, dlb, g["a_norm_g"] = _hgrn2_bwd(proj_a, lb + sent, small["a_norm_g"], states, dypre, tb)
    dh_a = _mm_wblk_dx(dproj_a, wts["a_in"], BF16, "a_in_dx", k=d, gb=nb, split=4, tm=512)
    put_g("l0a", {"a_in": _mm_wblk_dw(h_a, dproj_a, "a_in_dw", nb=nb, gb=1, split=4, tk=t)})
    (grad_x,), (dmods["sh1_0"], dmods["sc1_0"]) = _row_bwd(
        _f_mod, [(x, d, 0)], vec("sh1_0", "sc1_0"), [(dh_a, d, 0)], [F32], tb=tb, name="l0_mod1_bwd",
        add_to=(0, dx0))
    return loss, grad_x, dmods, dlb, g


def _position():
    return lax.axis_index("x"), lax.axis_index("y"), lax.axis_index("c")


_XCHG_EFFECT = pltpu.SideEffectType.DATAFLOW_SIDE_EFFECTING
ALL_PEERS = (1, 2, 3, 4, 5, 6, 7)
SAME_CORE = (2, 4, 6)


def _xchg_copies(src_refs, land_refs, send_sems, recv_sems, local_sems, scatter, rels):
    x, y, cc = _position()
    me = 4 * x + 2 * y + cc
    remote, local = [], []
    for a, (src, land) in enumerate(zip(src_refs, land_refs)):
        local.append(pltpu.make_async_copy(src.at[me] if scatter else src, land.at[me], local_sems.at[a]))
        for idx, rel in enumerate(rels):
            px = 1 - x if rel & 4 else x
            py = 1 - y if rel & 2 else y
            pc = 1 - cc if rel & 1 else cc
            k = len(rels) * a + idx
            remote.append(pltpu.make_async_remote_copy(
                src_ref=src.at[4 * px + 2 * py + pc] if scatter else src, dst_ref=land.at[me],
                send_sem=send_sems.at[k], recv_sem=recv_sems.at[k], device_id=(px, py, pc), device_id_type=_MESH))
    return remote, local


def _xchg_start(srcs, scatter, rels, after, name):
    n = len(srcs)
    lands = [lax.empty(s.shape if scatter else (NDEV, *s.shape), s.dtype) for s in srcs]

    def body(*refs):
        remote, local = _xchg_copies(refs[:n], refs[n:2 * n], *refs[2 * n + 1:2 * n + 4], scatter, rels)
        for cp in local + remote:
            cp.start()
        token = refs[-1]
        token[...] = jnp.zeros_like(token)

    hbm = pl.BlockSpec(memory_space=pltpu.HBM)
    sem = pl.BlockSpec(memory_space=pltpu.SEMAPHORE)
    out = pl.pallas_call(
        body, name=name,
        out_shape=(pltpu.SemaphoreType.DMA((len(rels) * n,)), pltpu.SemaphoreType.DMA((len(rels) * n,)),
                   pltpu.SemaphoreType.DMA((n,)),
                   *[pltpu.HBM(a.shape, a.dtype) for a in srcs + lands], jax.ShapeDtypeStruct((8, LANES), F32)),
        in_specs=[hbm] * (2 * n) + [pl.BlockSpec(memory_space=pl.ANY)],
        out_specs=(sem, sem, sem, *[hbm] * (2 * n), pl.BlockSpec(memory_space=pltpu.VMEM)),
        input_output_aliases={i: 3 + i for i in range(2 * n)},
        compiler_params=pltpu.CompilerParams(has_side_effects=_XCHG_EFFECT),
    )(*[pltpu.with_memory_space_constraint(a, pltpu.HBM) for a in srcs + lands], after)
    return out[:-1], out[-1][0, 0]


def _xchg_wait(handles, after, scatter, rels, name):
    n = (len(handles) - 3) // 2

    def body(*refs):
        remote, local = _xchg_copies(refs[:n], refs[n:2 * n], *refs[2 * n:2 * n + 3], scatter, rels)
        for cp in remote:
            cp.wait_send()
            cp.wait_recv()
        for cp in local:
            cp.wait()

    hbm = pl.BlockSpec(memory_space=pltpu.HBM)
    sem = pl.BlockSpec(memory_space=pltpu.SEMAPHORE)
    thru = list(handles[3:])
    afters = list(after) if isinstance(after, (list, tuple)) else [after]
    out = pl.pallas_call(
        body, name=name,
        out_shape=tuple(pltpu.HBM(a.shape, a.dtype) for a in thru),
        in_specs=[hbm] * (2 * n) + [sem, sem, sem] + [pl.BlockSpec(memory_space=pl.ANY)] * len(afters),
        out_specs=tuple([hbm] * (2 * n)),
        input_output_aliases={i: i for i in range(2 * n)},
        compiler_params=pltpu.CompilerParams(has_side_effects=_XCHG_EFFECT),
    )(*thru, *handles[:3], *afters)
    return list(out[n:])


def _sibling_copies(land_refs, send_sems, recv_sems):
    x, y, cc = _position()

    def copy(a, q, core):
        slot = land_refs[a].at[2 * q + core]
        return pltpu.make_async_remote_copy(
            src_ref=slot, dst_ref=slot, send_sem=send_sems.at[NCHIP * a + q], recv_sem=recv_sems.at[NCHIP * a + q],
            device_id=(x, y, 1 - cc), device_id_type=_MESH)

    pairs = [(a, q) for a in range(len(land_refs)) for q in range(NCHIP)]
    return [copy(a, q, cc) for a, q in pairs], [copy(a, q, 1 - cc) for a, q in pairs]


def _sibling_forward_start(lands, name, after=None):
    n = len(lands)
    deps = [] if after is None else [after]

    def body(*refs):
        sends, _ = _sibling_copies(refs[:n], refs[n + len(deps)], refs[n + len(deps) + 1])
        for cp in sends:
            cp.start()
        refs[-1][...] = jnp.zeros_like(refs[-1])

    hbm = pl.BlockSpec(memory_space=pltpu.HBM)
    sem = pl.BlockSpec(memory_space=pltpu.SEMAPHORE)
    out = pl.pallas_call(
        body, name=name,
        out_shape=(pltpu.SemaphoreType.DMA((NCHIP * n,)), pltpu.SemaphoreType.DMA((NCHIP * n,)),
                   *[pltpu.HBM(a.shape, a.dtype) for a in lands], jax.ShapeDtypeStruct((8, LANES), F32)),
        in_specs=[hbm] * n + [pl.BlockSpec(memory_space=pl.ANY)] * len(deps),
        out_specs=(sem, sem, *[hbm] * n, pl.BlockSpec(memory_space=pltpu.VMEM)),
        input_output_aliases={i: 2 + i for i in range(n)},
        compiler_params=pltpu.CompilerParams(has_side_effects=_XCHG_EFFECT),
    )(*lands, *deps)
    return out[:-1], out[-1][0, 0]


def _sibling_forward_wait(handles, after, name):
    n = len(handles) - 2

    def body(*refs):
        sends, arrivals = _sibling_copies(refs[:n], refs[n], refs[n + 1])
        for cp in sends:
            cp.wait_send()
        for cp in arrivals:
            cp.wait_recv()

    hbm = pl.BlockSpec(memory_space=pltpu.HBM)
    sem = pl.BlockSpec(memory_space=pltpu.SEMAPHORE)
    lands = list(handles[2:])
    return list(pl.pallas_call(
        body, name=name,
        out_shape=tuple(pltpu.HBM(a.shape, a.dtype) for a in lands),
        in_specs=[hbm] * n + [sem, sem, pl.BlockSpec(memory_space=pl.ANY)],
        out_specs=tuple([hbm] * n),
        input_output_aliases={i: i for i in range(n)},
        compiler_params=pltpu.CompilerParams(has_side_effects=_XCHG_EFFECT),
    )(*lands, *handles[:2], after))


def _slab_sum(slabs, name, tr=None):
    n, r, c = slabs.shape
    tr = r if tr is None else tr

    def body(s_ref, o_ref):
        acc = s_ref[0].astype(F32)
        for q in range(1, n):
            acc = acc + s_ref[q].astype(F32)
        o_ref[...] = acc

    return pl.pallas_call(body, name=name, grid=(r // tr,),
                          in_specs=[pl.BlockSpec((n, tr, c), lambda i: (0, i, 0))],
                          out_specs=pl.BlockSpec((tr, c), lambda i: (i, 0)),
                          out_shape=jax.ShapeDtypeStruct((r, c), F32),
                          compiler_params=_cparams(dimension_semantics=("parallel",)))(slabs)


def _slab_sum_unpad(arrs, n_loc, n_pad, name, tr=256):
    n, r, c = arrs[0].shape
    runs = c // n_pad

    def body(*refs):
        o_ref = refs[-1]
        for k, s_ref in enumerate(refs[:-1]):
            acc = s_ref[0].astype(F32)
            for q in range(1, n):
                acc = acc + s_ref[q].astype(F32)
            for u in range(runs):
                o_ref[k, :, u * n_loc:(u + 1) * n_loc] = acc[:, u * n_pad:u * n_pad + n_loc]

    return pl.pallas_call(body, name=name, grid=(r // tr,),
                          in_specs=[pl.BlockSpec((n, tr, c), lambda i: (0, i, 0))] * len(arrs),
                          out_specs=pl.BlockSpec((len(arrs), tr, runs * n_loc), lambda i: (0, i, 0)),
                          out_shape=jax.ShapeDtypeStruct((len(arrs), r, runs * n_loc), F32),
                          compiler_params=_cparams(dimension_semantics=("parallel",)))(*arrs)


def _ada_fwd(c_all, ada_w, kv_ada_w, logits):
    rows, d = c_all.shape
    n0, nkv = ada_w.shape[2], kv_ada_w.shape[1]

    def body(c_ref, w_ref, kw_ref, lg_ref, part_ref, cact_ref, lb_ref):
        ca = _silu(c_ref[...])
        cact_ref[...] = ca
        part_ref[:, 0:n0] = _bdot_raw(ca, w_ref[0], _NN)
        part_ref[:, n0:2 * n0] = _bdot_raw(ca, w_ref[1], _NN)
        part_ref[:, 2 * n0:2 * n0 + nkv] = _bdot_raw(ca, kw_ref[...], _NN)
        lb_ref[...] = _sigmoid(lg_ref[0:1, :] - lg_ref[1:2, :])

    vm = pl.BlockSpec(memory_space=pltpu.VMEM)
    return pl.pallas_call(
        body, name="ada_fwd", in_specs=[vm, vm, vm, vm], out_specs=[vm, vm, vm],
        out_shape=[jax.ShapeDtypeStruct((rows, 2 * n0 + nkv), F32), jax.ShapeDtypeStruct((rows, d), F32),
                   jax.ShapeDtypeStruct((1, d), F32)],
        compiler_params=_cparams(),
    )(c_all, ada_w, kv_ada_w, logits)


def _ada_bwd(c_act, dm0, dm1, dkv, lb, dlb):
    rows, d = c_act.shape

    def body(c_ref, d0_ref, d1_ref, dk_ref, lb_ref, dlb_ref, dw_ref, dkw_ref, dlg_ref):
        ca = c_ref[...]
        dw_ref[0] = _bdot_raw(ca, d0_ref[...], _TN)
        dw_ref[1] = _bdot_raw(ca, d1_ref[...], _TN)
        dkw_ref[...] = _bdot_raw(ca, dk_ref[...], _TN)
        lbv = lb_ref[...]
        dl0 = dlb_ref[...] * lbv * (1.0 - lbv)
        dlg_ref[0:1, :] = dl0
        dlg_ref[1:2, :] = -dl0

    vm = pl.BlockSpec(memory_space=pltpu.VMEM)
    return pl.pallas_call(
        body, name="ada_bwd", in_specs=[vm] * 6, out_specs=[vm, vm, vm],
        out_shape=[jax.ShapeDtypeStruct((2, d, dm0.shape[1]), F32), jax.ShapeDtypeStruct((d, dkv.shape[1]), F32),
                   jax.ShapeDtypeStruct((2, d), F32)],
        compiler_params=_cparams(),
    )(c_act, dm0, dm1, dkv, lb, dlb)


def _adamw(w, g, m, v, name, tr=512, after=None):
    r, c = w.shape
    tr = _divisor_tile(r, tr, unit=8)
    c1 = 1.0 - ADAM_B1 ** ADAM_STEP
    c2 = 1.0 - ADAM_B2 ** ADAM_STEP
    deps = [] if after is None else [after]

    def body(w_ref, g_ref, m_ref, v_ref, *rest):
        d_ref, mo_ref, vo_ref = rest[len(deps):]
        gv = g_ref[...]
        mn = ADAM_B1 * m_ref[...] + (1.0 - ADAM_B1) * gv
        vn = ADAM_B2 * v_ref[...] + (1.0 - ADAM_B2) * (gv * gv)
        d_ref[...] = -ADAM_LR * ((mn / c1) / (jnp.sqrt(vn / c2) + ADAM_EPS) + ADAM_WD * w_ref[...])
        mo_ref[...] = mn
        vo_ref[...] = vn

    spec = pl.BlockSpec((tr, c), lambda i: (i, 0))
    out = jax.ShapeDtypeStruct((r, c), F32)
    return pl.pallas_call(body, name=name, grid=(r // tr,),
                          in_specs=[spec] * 4 + [pl.BlockSpec(a.shape, lambda i: (0, 0)) for a in deps],
                          out_specs=[spec] * 3, out_shape=[out, out, out],
                          compiler_params=_cparams(dimension_semantics=("parallel",)))(w, g, m, v, *deps)


def _pad_rows(a, rows):
    return jnp.pad(a, ((0, rows - a.shape[0]), (0, 0)))


def _pack_small(parts, lanes=LANES, row_unit=8):
    flat = jnp.concatenate([p.reshape(-1).astype(F32) for p in parts])
    rows = _round_up(-(-flat.shape[0] // lanes), row_unit)
    return jnp.pad(flat, (0, rows * lanes - flat.shape[0])).reshape(rows, lanes)


def _unpack_small(flat, shapes):
    out, off = [], 0
    for s in shapes:
        n = 1
        for k in s:
            n *= k
        out.append(flat[off:off + n].reshape(s))
        off += n
    return out


def _pad_shard_cols(a, n_loc, n_pad):
    lead, runs = a.shape[:-1], a.shape[-1] // n_loc
    a = a.reshape(*lead, runs, n_loc)
    a = jnp.pad(a, [(0, 0)] * (len(lead) + 1) + [(0, n_pad - n_loc)])
    return a.reshape(*lead, runs * n_pad)


def _unpad_shard_cols(a, n_loc, n_pad):
    lead, runs = a.shape[:-1], a.shape[-1] // n_pad
    return a.reshape(*lead, runs, n_pad)[..., :n_loc].reshape(*lead, runs * n_loc)


def kernel(x, c, ada_w, ada_b, a_w_in, a_lb_logits, a_norm_g, a_w_out, kv_ada_w, kv_ada_b, kv_w, kv_b_f, k_norm_g, b_w_q, q_norm_g, b_w_out, ffn_w_up, ffn_conv_w, ffn_conv_b, ffn_w_down, loss_target, m_ada_w, m_ada_b, m_a_w_in, m_a_lb_logits, m_a_norm_g, m_a_w_out, m_kv_ada_w, m_kv_ada_b, m_kv_w, m_kv_b_f, m_k_norm_g, m_b_w_q, m_q_norm_g, m_b_w_out, m_ffn_w_up, m_ffn_conv_w, m_ffn_conv_b, m_ffn_w_down, v_ada_w, v_ada_b, v_a_w_in, v_a_lb_logits, v_a_norm_g, v_a_w_out, v_kv_ada_w, v_kv_ada_b, v_kv_w, v_kv_b_f, v_k_norm_g, v_b_w_q, v_q_norm_g, v_b_w_out, v_ffn_w_up, v_ffn_conv_w, v_ffn_conv_b, v_ffn_w_down):
    t, d = x.shape[1], x.shape[2]
    nh = d // HEAD
    ncw = ffn_w_up.shape[2]
    rd = ffn_w_down.shape[1]
    assert ncw == 2 * rd
    rp = _round_up(rd, LANES)
    ncp = 2 * rp
    two_f = ncw * NDEV
    fp = ncp * NDEV // 2
    me = 4 * lax.axis_index("x") + 2 * lax.axis_index("y") + lax.axis_index("c")
    weights = dict(ada_w=ada_w, ada_b=ada_b, a_w_in=a_w_in, a_lb_logits=a_lb_logits, a_norm_g=a_norm_g,
                   a_w_out=a_w_out, kv_ada_w=kv_ada_w, kv_ada_b=kv_ada_b, kv_w=kv_w, kv_b_f=kv_b_f,
                   k_norm_g=k_norm_g, b_w_q=b_w_q, q_norm_g=q_norm_g, b_w_out=b_w_out, ffn_w_up=ffn_w_up,
                   ffn_conv_w=ffn_conv_w, ffn_conv_b=ffn_conv_b, ffn_w_down=ffn_w_down)
    m_in = dict(ada_w=m_ada_w, ada_b=m_ada_b, a_w_in=m_a_w_in, a_lb_logits=m_a_lb_logits, a_norm_g=m_a_norm_g,
                a_w_out=m_a_w_out, kv_ada_w=m_kv_ada_w, kv_ada_b=m_kv_ada_b, kv_w=m_kv_w, kv_b_f=m_kv_b_f,
                k_norm_g=m_k_norm_g, b_w_q=m_b_w_q, q_norm_g=m_q_norm_g, b_w_out=m_b_w_out, ffn_w_up=m_ffn_w_up,
                ffn_conv_w=m_ffn_conv_w, ffn_conv_b=m_ffn_conv_b, ffn_w_down=m_ffn_w_down)
    v_in = dict(ada_w=v_ada_w, ada_b=v_ada_b, a_w_in=v_a_w_in, a_lb_logits=v_a_lb_logits, a_norm_g=v_a_norm_g,
                a_w_out=v_a_w_out, kv_ada_w=v_kv_ada_w, kv_ada_b=v_kv_ada_b, kv_w=v_kv_w, kv_b_f=v_kv_b_f,
                k_norm_g=v_k_norm_g, b_w_q=v_b_w_q, q_norm_g=v_q_norm_g, b_w_out=v_b_w_out, ffn_w_up=v_ffn_w_up,
                ffn_conv_w=v_ffn_conv_w, ffn_conv_b=v_ffn_conv_b, ffn_w_down=v_ffn_w_down)
    order = list(weights)

    up_loc = _pad_shard_cols(ffn_w_up, rd, rp).astype(BF16)
    down_loc = jnp.pad(ffn_w_down, ((0, 0), (0, rp - rd), (0, 0))).astype(BF16)
    gather_names = {"l0b": ["a_out", "up0", "down0"], "l1": ["kv", "b_q", "b_out", "up1", "down1"]}
    forward_names = {"l0b": ["a_out"], "l0b_ffn": ["up0", "down0"], "l1": gather_names["l1"]}
    shards = {"a_out": a_w_out[0].astype(BF16), "up0": up_loc[0], "down0": down_loc[0], "kv": kv_w.T.astype(BF16),
              "b_q": b_w_q[0].astype(BF16), "b_out": b_w_out[0].astype(BF16), "up1": up_loc[1],
              "down1": down_loc[1]}
    pre = _pack_small([c, a_lb_logits, ffn_conv_w])
    in_flight = {}
    pre_flight, _ = _xchg_start([pre], False, ALL_PEERS, pre, "gather_small_inputs_start")
    (pre_all,) = _xchg_wait(pre_flight, pre, False, ALL_PEERS, "gather_small_inputs_wait")
    pre_all = pre_all.reshape(NDEV, -1)
    c_all = pre_all[:, :d]
    logits = pre_all[:, d:d + 2 * HEAD].reshape(NDEV, 2, HEAD).transpose(1, 0, 2).reshape(2, d)
    conv_w_full = pre_all[:, d + 2 * HEAD:d + 2 * HEAD + 2 * CONV_TAPS * ncw]
    conv_w_full = conv_w_full.reshape(NDEV, 2, CONV_TAPS, ncw).transpose(1, 2, 0, 3).reshape(2, CONV_TAPS, two_f)

    part, c_act, lb = _ada_fwd(_pad_rows(c_all, 2 * NDEV), ada_w, kv_ada_w, logits)
    part_flight, _ = _xchg_start([part[:NDEV]], False, ALL_PEERS, part, "gather_adaln_start")
    in_flight["l0a"], _ = _xchg_start([a_w_in[0].astype(BF16)], False, SAME_CORE, part_flight[-1], "gather_l0a_start")
    (part_all,) = _xchg_wait(part_flight, in_flight["l0a"][-1], False, ALL_PEERS, "gather_adaln_wait")
    forwarding = {}
    mine = lax.dynamic_index_in_dim(part_all, me, axis=1, keepdims=False)
    n0, nkv = ada_w.shape[2], kv_ada_w.shape[1]
    mod_names = ["sh1", "sc1", "g1", "sh2", "sc2", "g2"]
    mods = {}
    for l in range(2):
        row = mine[:, l * n0:(l + 1) * n0].reshape(-1) + ada_b[l]
        for k, nm in enumerate(mod_names):
            mods[f"{nm}_{l}"] = row[k * d:(k + 1) * d].reshape(1, d)
    kvrow = mine[:, 2 * n0:2 * n0 + nkv].reshape(-1) + kv_ada_b
    mods["kv_sh"], mods["kv_sc"] = kvrow[:d].reshape(1, d), kvrow[d:].reshape(1, d)

    def start_gather(grp, dep):
        srcs = [shards[n] for n in gather_names[grp]]
        in_flight[grp], started = _xchg_start(srcs, False, SAME_CORE, dep, f"gather_{grp}_start")
        return started

    zero = start_gather("l0b", part_all)
    mods["sh1_0"] = mods["sh1_0"] + zero

    small = {"a_norm_g": a_norm_g, "k_norm_g": k_norm_g.reshape(1, HEAD), "q_norm_g": q_norm_g, "kv_b_f": kv_b_f}
    for l in range(2):
        small[f"conv_w{l}"] = _pad_shard_cols(conv_w_full[l], rd, rp).reshape(CONV_TAPS, 2, fp).transpose(1, 0, 2)
        small[f"conv_b{l}"] = _pad_shard_cols(ffn_conv_b[l], rd, rp).reshape(2, 1, fp)

    def pre_w(grp, after):
        arrived = _xchg_wait(in_flight[grp], after, False, SAME_CORE, f"gather_{grp}_wait")
        if grp == "l0b":
            forwarding[grp], _ = _sibling_forward_start(arrived[:1], "gather_l0b_to_sibling_start")
            forwarding["l0b_ffn"], started = _sibling_forward_start(
                arrived[1:], "gather_l0b_ffn_to_sibling_start", after=forwarding[grp][-1])
            return started
        forwarding[grp], started = _sibling_forward_start(arrived, f"gather_{grp}_to_sibling_start")
        return started

    def get_w(grp, after):
        if grp == "l0a":
            arrived = _xchg_wait(in_flight["l0a"], after, False, SAME_CORE, "gather_l0a_wait")
            handles, _ = _sibling_forward_start(arrived, "gather_l0a_to_sibling_start")
            return {"a_in": _sibling_forward_wait(handles, after, "gather_l0a_to_sibling_wait")[0]}
        full = _sibling_forward_wait(forwarding[grp], after, f"gather_{grp}_to_sibling_wait")
        if grp == "l0b":
            started = start_gather("l1", full[0])
            full[0] = full[0] + started.astype(full[0].dtype)
        got = dict(zip(forward_names[grp], full))
        out = {}
        for n, a in got.items():
            if n in ("a_out", "b_out"):
                out[n] = a.reshape(d, d)
            elif n in ("down0", "down1"):
                out[n] = a.reshape(fp, d)
            elif n == "kv":
                kv_t = a.reshape(NDEV * kv_w.shape[1], d)
                out["kv_k"], out["kv_v"] = kv_t[:d], kv_t[d:2 * d]
                out["kv_f"] = jnp.pad(kv_t[2 * d:], ((0, LANES - nh), (0, 0)))
            else:
                out[n] = a
        return out

    scatter_flight, g_last = {}, {}

    def put_g(grp, gr):
        if grp == "l0a":
            g_last.update(gr)
            return zero
        if grp == "l1":
            g_kvw = jnp.concatenate([gr["kv_k"], gr["kv_v"], gr["kv_f"][:nh].astype(BF16)], axis=0)
            arrs = {"kv_w": g_kvw.reshape(NDEV, kv_w.shape[1], d), "b_w_q": gr["b_q"],
                    "b_w_out": gr["b_out"].reshape(NDEV, d // NDEV, d), "up1": gr["up1"],
                    "down1": gr["down1"].reshape(NDEV, rp, d)}
        else:
            arrs = {"a_w_out": gr["a_out"].reshape(NDEV, d // NDEV, d), "up0": gr["up0"],
                    "down0": gr["down0"].reshape(NDEV, rp, d)}
        srcs = list(arrs.values())
        handles, sent = _xchg_start(srcs, True, ALL_PEERS, srcs[0], f"scatter_{grp}_start")
        scatter_flight[grp] = (list(arrs), handles)
        return sent

    loss_v, grad_x, dmods, dlb, g = _local_step(x[0], loss_target[0], mods, lb, small, pre_w, get_w, put_g)

    g_sum, landed_up = {}, {}
    for grp in ("l1", "l0b"):
        names, handles = scatter_flight[grp]
        for nm, a in zip(names, _xchg_wait(handles, grad_x, True, ALL_PEERS, f"scatter_{grp}_wait")):
            if nm in ("up0", "up1"):
                landed_up[nm] = a
            else:
                g_sum[nm] = _slab_sum(a, f"rs_slab_sum_{nm}")

    def conv_w_grad(a):
        return _unpad_shard_cols(a.transpose(1, 0, 2).reshape(CONV_TAPS, 2 * fp), rd, rp)

    def conv_b_grad(a):
        return _unpad_shard_cols(a.reshape(2 * fp), rd, rp)

    dmod_vec = [dmods[f"{nm}_{l}"] for l in range(2) for nm in mod_names] + [dmods["kv_sh"], dmods["kv_sc"]]
    post = _pack_small(dmod_vec + [dlb, g["a_norm_g"], g["k_norm_g"], g["q_norm_g"],
                                   jnp.pad(g["kv_b_f"].reshape(-1), (0, LANES - nh)),
                                   conv_w_grad(g["conv_w0"]), conv_w_grad(g["conv_w1"]),
                                   conv_b_grad(g["conv_b0"]), conv_b_grad(g["conv_b1"]), loss_v])
    post_flight, _ = _xchg_start([post], False, ALL_PEERS, post, "gather_small_grads_start")
    a_in_flight, a_in_sent = _xchg_start([g_last["a_in"]], True, ALL_PEERS, post_flight[-1], "scatter_l0a_start")
    a_in_sent = a_in_sent.reshape(1, 1)
    grads = {
        "a_w_out": g_sum["a_w_out"].reshape(a_w_out.shape),
        "kv_w": g_sum["kv_w"].T,
        "b_w_q": g_sum["b_w_q"].reshape(b_w_q.shape),
        "b_w_out": g_sum["b_w_out"].reshape(b_w_out.shape),
        "ffn_w_up": _slab_sum_unpad([landed_up["up0"], landed_up["up1"]], rd, rp, "rs_slab_sum_up"),
        "ffn_w_down": jnp.stack([g_sum["down0"][:rd], g_sum["down1"][:rd]]),
    }
    delta, new_m, new_v = {}, {}, {}

    def adamw_matrix(n):
        shp = weights[n].shape
        two_d = lambda a: a.reshape(-1, shp[-1])
        dl, mn, vn = _adamw(two_d(weights[n]), two_d(grads[n]), two_d(m_in[n]), two_d(v_in[n]), f"adamw_{n}",
                            after=a_in_sent)
        delta[n], new_m[n], new_v[n] = dl.reshape(shp), mn.reshape(shp), vn.reshape(shp)

    for n in grads:
        adamw_matrix(n)
    (post_all,) = _xchg_wait(post_flight, [new_v[n] for n in grads], False, ALL_PEERS, "gather_small_grads_wait")
    tot = _slab_sum(post_all, "small_grad_sum").reshape(-1)
    nmod = 14 * d
    (t_mod, t_lb, t_ang, t_kng, t_qng, t_bf, t_cw, t_cb, t_loss) = _unpack_small(
        tot, [(nmod,), (1, d), (1, HEAD), (HEAD,), (1, HEAD), (LANES,), (2, CONV_TAPS, two_f), (2, two_f),
              (LANES,)])
    loss = t_loss[0]
    dm_all = post_all.reshape(NDEV, -1)[:, :nmod]
    dm0 = lax.dynamic_slice_in_dim(dm_all[:, :6 * d], me * n0, n0, axis=1)
    dm1 = lax.dynamic_slice_in_dim(dm_all[:, 6 * d:12 * d], me * n0, n0, axis=1)
    dkv = lax.dynamic_slice_in_dim(dm_all[:, 12 * d:], me * nkv, nkv, axis=1)
    g_ada_w, g_kv_ada_w, g_logits = _ada_bwd(c_act, _pad_rows(dm0, 2 * NDEV), _pad_rows(dm1, 2 * NDEV),
                                              _pad_rows(dkv, 2 * NDEV), lb, t_lb)

    grads.update({
        "ada_w": g_ada_w,
        "ada_b": t_mod[:12 * d].reshape(2, 6 * d),
        "a_lb_logits": lax.dynamic_slice_in_dim(g_logits, me * HEAD, HEAD, axis=1),
        "a_norm_g": t_ang,
        "kv_ada_w": g_kv_ada_w,
        "kv_ada_b": t_mod[12 * d:],
        "kv_b_f": t_bf[:nh],
        "k_norm_g": t_kng,
        "q_norm_g": t_qng,
        "ffn_conv_w": lax.dynamic_slice_in_dim(t_cw, me * ncw, ncw, axis=2),
        "ffn_conv_b": t_cb,
    })

    small_adam = [n for n in order if n not in delta and n not in ("ada_w", "kv_ada_w", "a_w_in")]
    packs = [_pack_small([src[n] for n in small_adam]) for src in (weights, grads, m_in, v_in)]
    outs = _adamw(*packs, "adamw_small", tr=packs[0].shape[0])
    shapes = [weights[n].shape for n in small_adam]
    for dst, o in zip((delta, new_m, new_v), outs):
        for n, a in zip(small_adam, _unpack_small(o.reshape(-1), shapes)):
            dst[n] = a
    adamw_matrix("ada_w")
    adamw_matrix("kv_ada_w")
    (landed,) = _xchg_wait(a_in_flight, new_v["kv_ada_w"], True, ALL_PEERS, "scatter_l0a_wait")
    grads["a_w_in"] = _slab_sum(landed, "rs_slab_sum_a_w_in").reshape(a_w_in.shape)
    adamw_matrix("a_w_in")

    return (loss, grad_x.reshape(x.shape), *[grads[n] for n in order], *[delta[n] for n in order],
            *[new_m[n] for n in order], *[new_v[n] for n in order])
```

```python
import functools

import jax
import jax.numpy as jnp
from jax import lax
from jax.experimental import pallas as pl
from jax.experimental.pallas import tpu as pltpu

F32 = jnp.float32
BF16 = jnp.bfloat16

NDEV = 8
NCHIP = 4
HEAD = 128
A_CHUNK = 64
CONV_TAPS = 3
EPS = 1e-6
NEG_INF = -1e30
LANES = 128
VMEM_LIMIT = 48 * 1024 * 1024

ADAM_LR = 0.001
ADAM_B1 = 0.9
ADAM_B2 = 0.999
ADAM_EPS = 1e-08
ADAM_WD = 0.01
ADAM_STEP = 10

_NN = (((1,), (0,)), ((), ()))
_NT = (((1,), (1,)), ((), ()))
_TN = (((0,), (0,)), ((), ()))
_MESH = pl.DeviceIdType.MESH


def _cparams(**kw):
    return pltpu.CompilerParams(vmem_limit_bytes=VMEM_LIMIT, **kw)


def _divisor_tile(n, pref, unit=LANES):
    if n <= pref:
        return n
    best = None
    for t in range(unit, pref + 1, unit):
        if n % t == 0:
            best = t
    assert best is not None, (n, pref)
    return best


def _round_up(n, unit):
    return -(-n // unit) * unit


def _bdot_raw(a, b, dims):
    return lax.dot_general(a.astype(BF16), b.astype(BF16), dims, preferred_element_type=F32)


@jax.custom_vjp
def _dot_nn(a, b):
    return _bdot_raw(a, b, _NN)


@jax.custom_vjp
def _dot_nt(a, b):
    return _bdot_raw(a, b, _NT)


@jax.custom_vjp
def _dot_tn(a, b):
    return _bdot_raw(a, b, _TN)


_dot_nn.defvjp(lambda a, b: (_bdot_raw(a, b, _NN), (a, b)),
               lambda r, g: (_dot_nt(g, r[1]), _dot_tn(r[0], g)))
_dot_nt.defvjp(lambda a, b: (_bdot_raw(a, b, _NT), (a, b)),
               lambda r, g: (_dot_nn(g, r[1]), _dot_tn(g, r[0])))
_dot_tn.defvjp(lambda a, b: (_bdot_raw(a, b, _TN), (a, b)),
               lambda r, g: (_dot_nt(r[1], g), _dot_nn(r[0], g)))


def _f32dot(a, b):
    return lax.dot_general(a, b, _NN, precision=lax.Precision.HIGHEST, preferred_element_type=F32)


def _sigmoid(x):
    return jax.nn.sigmoid(x)


def _silu(x):
    return x * jax.nn.sigmoid(x)


def _rms(x):
    return x * lax.rsqrt(jnp.mean(x * x, axis=-1, keepdims=True) + EPS)


def _modulate(x, sh, sc):
    return _rms(x) * (1.0 + sc) + sh


def _mm_call(a, b, dims, a_spec, b_spec, o_spec, o_shape, grid, acc_tile, name):
    nk = grid[2]

    def body(a_ref, b_ref, o_ref, *acc):
        p = lax.dot_general(a_ref[...].astype(BF16), b_ref[...].astype(BF16), dims,
                            preferred_element_type=F32)
        if nk == 1:
            o_ref[...] = p.astype(o_ref.dtype)
        else:
            kk = pl.program_id(2)

            @pl.when(kk == 0)
            def _():
                acc[0][...] = p

            @pl.when(kk > 0)
            def _():
                acc[0][...] += p

            @pl.when(kk == nk - 1)
            def _():
                o_ref[...] = acc[0][...].astype(o_ref.dtype)

    return pl.pallas_call(
        body, name=name, grid=grid, in_specs=[a_spec, b_spec], out_specs=o_spec, out_shape=o_shape,
        scratch_shapes=[pltpu.VMEM(acc_tile, F32)] if nk > 1 else [],
        compiler_params=_cparams(dimension_semantics=("parallel", "parallel", "arbitrary")),
    )(a, b)


def _mm(a, b, mode, out_dtype, name, tm=1024, tn=1024, tk=2048):
    if mode == "nn":
        (m, k), (k2, n) = a.shape, b.shape
    elif mode == "nt":
        (m, k), (n, k2) = a.shape, b.shape
    else:
        (k, m), (k2, n) = a.shape, b.shape
    assert k == k2, (a.shape, b.shape, mode)
    tm, tn, tk = _divisor_tile(m, tm), _divisor_tile(n, tn), _divisor_tile(k, tk)
    if mode == "tn":
        a_spec = pl.BlockSpec((tk, tm), lambda i, j, kk: (kk, i))
    else:
        a_spec = pl.BlockSpec((tm, tk), lambda i, j, kk: (i, kk))
    if mode == "nt":
        b_spec = pl.BlockSpec((tn, tk), lambda i, j, kk: (j, kk))
    else:
        b_spec = pl.BlockSpec((tk, tn), lambda i, j, kk: (kk, j))
    return _mm_call(a, b, {"nn": _NN, "nt": _NT, "tn": _TN}[mode], a_spec, b_spec,
                    pl.BlockSpec((tm, tn), lambda i, j, kk: (i, j)), jax.ShapeDtypeStruct((m, n), out_dtype),
                    (m // tm, n // tn, k // tk), (tm, tn), name)


def _wblk_act_spec(rows, gb, nl, split, nb, row_axis, blk_axis):
    if split == 1:
        return pl.BlockSpec((rows, gb * nl), lambda *g: (g[row_axis], g[blk_axis]))
    groups = nb // split // gb
    return pl.BlockSpec((None, rows, gb * nl),
                        lambda *g: (g[blk_axis] // groups, g[row_axis], g[blk_axis] % groups))


def _mm_wblk(a, wb, out_dtype, name, *, gb, row_off=0, split=1, tm=1024):
    m, k = a.shape
    nb, _, nl = wb.shape
    assert (nb // split) % gb == 0
    tm = _divisor_tile(m, tm)

    def body(a_ref, b_ref, o_ref):
        av = a_ref[...].astype(BF16)
        for s in range(gb):
            o_ref[:, s * nl:(s + 1) * nl] = lax.dot_general(
                av, b_ref[s].astype(BF16), _NN, preferred_element_type=F32).astype(o_ref.dtype)

    o_shape = (m, nb * nl) if split == 1 else (split, m, nb // split * nl)
    return pl.pallas_call(
        body, name=name, grid=(nb // gb, m // tm),
        in_specs=[pl.BlockSpec((tm, k), lambda j, i: (i, 0)),
                  pl.BlockSpec((gb, k, nl), lambda j, i: (j, row_off, 0))],
        out_specs=_wblk_act_spec(tm, gb, nl, split, nb, 1, 0),
        out_shape=jax.ShapeDtypeStruct(o_shape, out_dtype),
        compiler_params=_cparams(dimension_semantics=("parallel", "parallel")),
    )(a, wb)


def _mm_wblk_dx(dy, wb, out_dtype, name, *, k, gb, row_off=0, split=1, tm=1024):
    nb, _, nl = wb.shape
    m = dy.shape[-2]
    tm = _divisor_tile(m, tm)
    nk = nb // gb
    per = nb // split
    whole = split > 1 and gb == nb
    assert whole or per % gb == 0

    def body(a_ref, b_ref, o_ref, *acc):
        p = None
        for s in range(gb):
            a_blk = a_ref[s // per, :, (s % per) * nl:(s % per + 1) * nl] if whole else a_ref[:, s * nl:(s + 1) * nl]
            q = lax.dot_general(a_blk.astype(BF16), b_ref[s].astype(BF16), _NT, preferred_element_type=F32)
            p = q if p is None else p + q
        if nk == 1:
            o_ref[...] = p.astype(o_ref.dtype)
        else:
            kk = pl.program_id(1)

            @pl.when(kk == 0)
            def _():
                acc[0][...] = p

            @pl.when(kk > 0)
            def _():
                acc[0][...] += p

            @pl.when(kk == nk - 1)
            def _():
                o_ref[...] = acc[0][...].astype(o_ref.dtype)

    return pl.pallas_call(
        body, name=name, grid=(m // tm, nk),
        in_specs=[pl.BlockSpec((split, tm, per * nl), lambda i, kk: (0, i, 0)) if whole
                  else _wblk_act_spec(tm, gb, nl, split, nb, 0, 1),
                  pl.BlockSpec((gb, k, nl), lambda i, kk: (kk, row_off, 0))],
        out_specs=pl.BlockSpec((tm, k), lambda i, kk: (i, 0)),
        out_shape=jax.ShapeDtypeStruct((m, k), out_dtype),
        scratch_shapes=[pltpu.VMEM((tm, k), F32)] if nk > 1 else [],
        compiler_params=_cparams(dimension_semantics=("parallel", "arbitrary")),
    )(dy, wb)


def _mm_wblk_dw(x, dy, name, *, nb, gb, split=1, tk=1024):
    t, k = x.shape
    assert (nb // split) % gb == 0
    nl = dy.shape[-1] * split // nb
    tk = _divisor_tile(t, tk)
    nk = t // tk

    def body(a_ref, b_ref, o_ref, *acc):
        kk = pl.program_id(1)
        av = a_ref[...].astype(BF16)
        for s in range(gb):
            p = lax.dot_general(av, b_ref[:, s * nl:(s + 1) * nl].astype(BF16), _TN, preferred_element_type=F32)
            if nk == 1:
                o_ref[s] = p.astype(o_ref.dtype)
                continue

            @pl.when(kk == 0)
            def _():
                acc[0][s] = p

            @pl.when(kk > 0)
            def _():
                acc[0][s] += p

        if nk > 1:
            @pl.when(kk == nk - 1)
            def _():
                o_ref[...] = acc[0][...].astype(o_ref.dtype)

    return pl.pallas_call(
        body, name=name, grid=(nb // gb, nk),
        in_specs=[pl.BlockSpec((tk, k), lambda j, kk: (kk, 0)), _wblk_act_spec(tk, gb, nl, split, nb, 1, 0)],
        out_specs=pl.BlockSpec((gb, k, nl), lambda j, kk: (j, 0, 0)),
        out_shape=jax.ShapeDtypeStruct((nb, k, nl), BF16),
        scratch_shapes=[pltpu.VMEM((gb, k, nl), F32)] if nk > 1 else [],
        compiler_params=_cparams(dimension_semantics=("parallel", "arbitrary")),
    )(x, dy)


def _row_specs(rows, tb, nsub):
    return [pl.BlockSpec((tb, nsub * cw), functools.partial(lambda i, off: (i, off), off=off))
            for (_, cw, off) in rows]


def _vec_specs(params):
    return [pl.BlockSpec(p.shape, lambda i: (0, 0)) for p in params]


def _row_fwd(f, rows, params, out_dtypes, *, nsub=1, tb, name):
    t = rows[0][0].shape[0]
    tb = min(tb, t)
    n_r, n_p = len(rows), len(params)
    blk = [jax.ShapeDtypeStruct((tb, cw), F32) for (_, cw, _) in rows]
    blk += [jax.ShapeDtypeStruct(p.shape, F32) for p in params]
    out_avals = jax.eval_shape(f, *blk)

    def body(*refs):
        pv = [r[...] for r in refs[n_r:n_r + n_p]]
        for s in range(nsub):
            vals = [r[:, s * cw:(s + 1) * cw].astype(F32) for r, (_, cw, _) in zip(refs[:n_r], rows)]
            outs = f(*vals, *pv)
            for o_ref, o in zip(refs[n_r + n_p:], outs):
                w = o.shape[1]
                o_ref[:, s * w:(s + 1) * w] = o.astype(o_ref.dtype)

    return pl.pallas_call(
        body, name=name,
        grid=(t // tb,),
        in_specs=_row_specs(rows, tb, nsub) + _vec_specs(params),
        out_specs=[pl.BlockSpec((tb, nsub * av.shape[1]), lambda i: (i, 0)) for av in out_avals],
        out_shape=[jax.ShapeDtypeStruct((t, nsub * av.shape[1]), dt) for av, dt in zip(out_avals, out_dtypes)],
        compiler_params=_cparams(dimension_semantics=("parallel",)),
    )(*[r[0] for r in rows], *params)


def _row_bwd(f, rows, params, cots, row_grad_dtypes, *, nsub=1, tb, name, add_to=None, cot_add=None):
    t = rows[0][0].shape[0]
    tb = min(tb, t)
    n_r, n_p, n_c = len(rows), len(params), len(cots)
    want = [j for j in range(n_r) if row_grad_dtypes[j] is not None]
    cot_add = cot_add or []
    extra = [] if add_to is None else [(add_to[1], rows[add_to[0]][1], 0)]
    n_add_to = len(extra)
    extra += [(arr, cots[ci][1], 0) for ci, arr in cot_add]

    def body(*refs):
        i = pl.program_id(0)
        r_in, p_in = refs[:n_r], refs[n_r:n_r + n_p]
        c_in = refs[n_r + n_p:n_r + n_p + n_c]
        e_in = refs[n_r + n_p + n_c:n_r + n_p + n_c + len(extra)]
        outs = refs[n_r + n_p + n_c + len(extra):]
        pv = [r[...] for r in p_in]
        psum = [None] * n_p
        for s in range(nsub):
            vals = [r[:, s * cw:(s + 1) * cw].astype(F32) for r, (_, cw, _) in zip(r_in, rows)]
            cvals = [r[:, s * cw:(s + 1) * cw].astype(F32) for r, (_, cw, _) in zip(c_in, cots)]
            for (ci, _), e_ref in zip(cot_add, e_in[n_add_to:]):
                cw = cots[ci][1]
                cvals[ci] = cvals[ci] + e_ref[:, s * cw:(s + 1) * cw].astype(F32)
            _, vjp_fn = jax.vjp(f, *vals, *pv)
            grads = vjp_fn(tuple(cvals))
            for o_ref, jr in zip(outs[:len(want)], want):
                cw = rows[jr][1]
                gr = grads[jr]
                if add_to is not None and jr == add_to[0]:
                    gr = gr + e_in[0][:, s * cw:(s + 1) * cw]
                o_ref[:, s * cw:(s + 1) * cw] = gr.astype(o_ref.dtype)
            for jp in range(n_p):
                psum[jp] = grads[n_r + jp] if psum[jp] is None else psum[jp] + grads[n_r + jp]
        for o_ref, g in zip(outs[len(want):], psum):
            @pl.when(i == 0)
            def _():
                o_ref[...] = g

            @pl.when(i > 0)
            def _():
                o_ref[...] += g

    out_specs = [pl.BlockSpec((tb, nsub * rows[jr][1]), lambda i: (i, 0)) for jr in want]
    out_shape = [jax.ShapeDtypeStruct((t, nsub * rows[jr][1]), row_grad_dtypes[jr]) for jr in want]
    out_specs += _vec_specs(params)
    out_shape += [jax.ShapeDtypeStruct(p.shape, F32) for p in params]
    res = pl.pallas_call(
        body, name=name,
        grid=(t // tb,),
        in_specs=_row_specs(rows, tb, nsub) + _vec_specs(params) + _row_specs(cots, tb, nsub)
        + _row_specs(extra, tb, nsub),
        out_specs=out_specs, out_shape=out_shape,
        compiler_params=_cparams(dimension_semantics=("arbitrary",)),
    )(*[r[0] for r in rows], *params, *[c[0] for c in cots], *[e[0] for e in extra])
    return res[:len(want)], res[len(want):]


def _f_mod(x, sh, sc):
    return (_modulate(x, sh, sc),)


def _f_res_mod(x, y, g, sh, sc):
    x1 = x + g * y
    return x1, _modulate(x1, sh, sc)


def _f_res_mod2(x, y, g, sh_a, sc_a, sh_b, sc_b):
    x1 = x + g * y
    return x1, _modulate(x1, sh_a, sc_a), _modulate(x1, sh_b, sc_b)


def _f_qnorm(p, g):
    return (_rms(p) * g * (HEAD ** -0.5),)


def _f_knorm(p, g):
    return (_rms(p) * g,)


def _f_qnorm_aug(p, g):
    lane = lax.broadcasted_iota(jnp.int32, p.shape, 1)
    return (jnp.concatenate([_rms(p) * g * (HEAD ** -0.5), jnp.where(lane < 3, 1.0, 0.0)], axis=1),)


def _f_knorm_aug(p, c0, c1, c2, g):
    lane = lax.broadcasted_iota(jnp.int32, p.shape, 1)
    aug = jnp.where(lane == 0, c0, jnp.where(lane == 1, c1, jnp.where(lane == 2, c2, 0.0)))
    return (jnp.concatenate([_rms(p) * g, aug], axis=1),)


def _split3(a):
    round_bf16 = lambda v: lax.reduce_precision(v, exponent_bits=8, mantissa_bits=7)
    hi = round_bf16(a)
    mid = round_bf16(a - hi)
    lo = round_bf16(a - hi - mid)
    return hi.astype(BF16), mid.astype(BF16), lo.astype(BF16)


def _f_outgate(o, og):
    return (o * _sigmoid(og),)


def _loss_call(x3, f, g2, target, tb):
    t, d = x3.shape
    tb = min(tb, t)

    def body(x_ref, f_ref, g_ref, t_ref, loss_ref, dx_ref, df_ref, dg_ref):
        i = pl.program_id(0)
        fv = f_ref[...]
        g = g_ref[...]
        e = x_ref[...] + g * fv - t_ref[...]
        dx = e * (1.0 / d)
        part = 0.5 * jnp.sum(jnp.sum(e * dx, axis=1, keepdims=True), axis=0, keepdims=True)
        dx_ref[...] = dx
        df_ref[...] = (g * dx).astype(df_ref.dtype)
        dg = jnp.sum(dx * fv, axis=0, keepdims=True)

        @pl.when(i == 0)
        def _():
            loss_ref[...] = jnp.broadcast_to(part, loss_ref.shape)
            dg_ref[...] = dg

        @pl.when(i > 0)
        def _():
            loss_ref[...] += jnp.broadcast_to(part, loss_ref.shape)
            dg_ref[...] += dg

    row = pl.BlockSpec((tb, d), lambda i: (i, 0))
    vec = pl.BlockSpec((1, d), lambda i: (0, 0))
    return pl.pallas_call(
        body, name="loss_head",
        grid=(t // tb,),
        in_specs=[row, row, vec, row],
        out_specs=[pl.BlockSpec((1, LANES), lambda i: (0, 0)), row, row, vec],
        out_shape=[jax.ShapeDtypeStruct((1, LANES), F32), jax.ShapeDtypeStruct((t, d), F32),
                   jax.ShapeDtypeStruct((t, d), BF16), jax.ShapeDtypeStruct((1, d), F32)],
        compiler_params=_cparams(dimension_semantics=("arbitrary",)),
    )(x3, f, g2, target)


def _hg_mask(tb):
    br = lax.broadcasted_iota(jnp.int32, (tb, tb), 0)
    bs = lax.broadcasted_iota(jnp.int32, (tb, tb), 1)
    return jnp.logical_and(br // A_CHUNK == bs // A_CHUNK, bs <= br).astype(F32)


def _hg_consts(mask):
    c = A_CHUNK
    r = lax.broadcasted_iota(jnp.int32, (c, c), 0)
    s = lax.broadcasted_iota(jnp.int32, (c, c), 1)
    return (s <= r).astype(F32), (r <= s).astype(F32), mask > 0.5


def _chunk_apply(mat, x):
    c = mat.shape[0]
    return jnp.concatenate([_f32dot(mat, x[i * c:(i + 1) * c]) for i in range(x.shape[0] // c)], axis=0)


@jax.custom_vjp
def _chunk_cumsum(x, tri, tri_t):
    return _chunk_apply(tri, x)


_chunk_cumsum.defvjp(lambda x, tri, tri_t: (_chunk_apply(tri, x), (tri, tri_t)),
                     lambda r, g: (_chunk_apply(r[1], g), jnp.zeros_like(r[0]), jnp.zeros_like(r[1])))


def _per_chunk(a, b, dims):
    return jnp.stack([_bdot_raw(a[i], b[i], dims) for i in range(a.shape[0])])


@jax.custom_vjp
def _chunk_tn(a, b):
    return _per_chunk(a, b, _TN)


@jax.custom_vjp
def _chunk_nt(a, b):
    return _per_chunk(a, b, _NT)


@jax.custom_vjp
def _chunk_nn(a, b):
    return _per_chunk(a, b, _NN)


_chunk_tn.defvjp(lambda a, b: (_per_chunk(a, b, _TN), (a, b)),
                 lambda r, g: (_chunk_nt(r[1], g), _chunk_nn(r[0], g)))
_chunk_nt.defvjp(lambda a, b: (_per_chunk(a, b, _NT), (a, b)),
                 lambda r, g: (_chunk_nn(g, r[1]), _chunk_tn(g, r[0])))
_chunk_nn.defvjp(lambda a, b: (_per_chunk(a, b, _NN), (a, b)),
                 lambda r, g: (_chunk_nt(g, r[1]), _chunk_tn(r[0], g)))


def _scan_states(decay, m, st):
    sts = []
    for i in range(m.shape[0]):
        sts.append(st)
        st = st * decay[i] + m[i]
    return jnp.stack(sts), st


@jax.custom_vjp
def _state_scan(decay, m, st):
    return _scan_states(decay, m, st)


def _state_scan_fwd(decay, m, st):
    sts, st_out = _scan_states(decay, m, st)
    return (sts, st_out), (decay, sts)


def _state_scan_bwd(res, cts):
    decay, sts = res
    d_sts, g = cts
    d_decay, d_m = [], []
    for i in range(sts.shape[0] - 1, -1, -1):
        d_m.append(g)
        d_decay.append(jnp.sum(g * sts[i], axis=0, keepdims=True))
        g = g * decay[i] + d_sts[i]
    return jnp.stack(d_decay[::-1]), jnp.stack(d_m[::-1]), g


_state_scan.defvjp(_state_scan_fwd, _state_scan_bwd)


def _hg_block(qp, fp, ip, gp, lb, ng, st, tri, tri_t, bd_causal):
    tb = qp.shape[0]
    c = A_CHUNK
    n = tb // c
    q = _silu(qp)
    fg = lb + (1.0 - lb) * _sigmoid(fp)
    logf = jnp.log(fg)
    k = 1.0 - fg
    b3 = _chunk_cumsum(logf, tri, tri_t).reshape(n, c, HEAD)
    pos = lax.broadcasted_iota(jnp.int32, (1, c, 1), 1)
    b_mid = lax.stop_gradient(jnp.sum(jnp.where(pos == c // 2, b3, 0.0), axis=1, keepdims=True))
    b_last = jnp.sum(jnp.where(pos == c - 1, b3, 0.0), axis=1, keepdims=True)
    q3, k3, v3 = q.reshape(n, c, HEAD), k.reshape(n, c, HEAD), ip.reshape(n, c, HEAD)
    scores = _dot_nt((q3 * jnp.exp(b3 - b_mid)).reshape(tb, HEAD), (k3 * jnp.exp(b_mid - b3)).reshape(tb, HEAD))
    o_intra = _dot_nn(jnp.where(bd_causal, scores, 0.0), ip)
    states, st_new = _state_scan(jnp.exp(b_last), _chunk_tn(v3, k3 * jnp.exp(b_last - b3)), st)
    o = o_intra + _chunk_nt(q3 * jnp.exp(b3), states).reshape(tb, HEAD)
    y = _rms(o) * ng * _silu(gp)
    return y, st_new


HG_HEADS = 2


def _hg_specs(tb, nh, rev_nb=None):
    wide = HG_HEADS * HEAD
    per = nh // HG_HEADS

    def row(part):
        if rev_nb is None:
            return pl.BlockSpec((tb, wide), functools.partial(lambda h, i, off: (i, off + h), off=part * per))
        return pl.BlockSpec((tb, wide),
                            functools.partial(lambda h, i, off: (rev_nb - 1 - i, off + h), off=part * per))
    return [row(0), row(1), row(2), row(3),
            pl.BlockSpec((1, wide), lambda h, i: (0, h)), pl.BlockSpec((1, HEAD), lambda h, i: (0, 0)),
            pl.BlockSpec((tb, tb), lambda h, i: (0, 0))]


def _hgrn2_fwd(proj, lb, ng, tb):
    t = proj.shape[0]
    nh = proj.shape[1] // (4 * HEAD)
    tb = min(tb, t)
    nb = t // tb
    wide = HG_HEADS * HEAD

    def body(q_ref, f_ref, i_ref, g_ref, lb_ref, ng_ref, mask_ref, y_ref, s_ref, st_ref):
        i = pl.program_id(1)

        @pl.when(i == 0)
        def _():
            st_ref[...] = jnp.zeros_like(st_ref)

        consts = _hg_consts(mask_ref[...])
        for p in range(HG_HEADS):
            cs = slice(p * HEAD, (p + 1) * HEAD)
            st = st_ref[p]
            s_ref[p, 0] = st
            y, st_new = _hg_block(q_ref[:, cs], f_ref[:, cs], i_ref[:, cs], g_ref[:, cs], lb_ref[:, cs],
                                  ng_ref[...], st, *consts)
            y_ref[:, cs] = y.astype(y_ref.dtype)
            st_ref[p] = st_new

    return pl.pallas_call(
        body, name="hgrn2_fwd",
        grid=(nh // HG_HEADS, nb),
        in_specs=_hg_specs(tb, nh),
        out_specs=[pl.BlockSpec((tb, wide), lambda h, i: (i, h)),
                   pl.BlockSpec((HG_HEADS, 1, HEAD, HEAD), lambda h, i: (h, i, 0, 0))],
        out_shape=[jax.ShapeDtypeStruct((t, nh * HEAD), BF16),
                   jax.ShapeDtypeStruct((nh, nb, HEAD, HEAD), F32)],
        scratch_shapes=[pltpu.VMEM((HG_HEADS, HEAD, HEAD), F32)],
        compiler_params=_cparams(dimension_semantics=("parallel", "arbitrary")),
    )(proj, proj, proj, proj, lb, ng, _hg_mask(tb))


def _hgrn2_bwd(proj, lb, ng, states, dy, tb):
    t = proj.shape[0]
    nh = proj.shape[1] // (4 * HEAD)
    tb = min(tb, t)
    nb = t // tb
    wide = HG_HEADS * HEAD

    def body(q_ref, f_ref, i_ref, g_ref, lb_ref, ng_ref, mask_ref, s_ref, dy_ref,
             dp_ref, dlb_ref, dng_ref, dst_ref):
        h, i = pl.program_id(0), pl.program_id(1)
        consts = _hg_consts(mask_ref[...])

        @pl.when(i == 0)
        def _():
            dst_ref[...] = jnp.zeros_like(dst_ref)
            dlb_ref[...] = jnp.zeros_like(dlb_ref)

        @pl.when(jnp.logical_and(i == 0, h == 0))
        def _():
            dng_ref[...] = jnp.zeros_like(dng_ref)

        def fn(qp, fp, ip, gp, lbx, ngx, stx):
            return _hg_block(qp, fp, ip, gp, lbx, ngx, stx, *consts)

        for p in range(HG_HEADS):
            cs = slice(p * HEAD, (p + 1) * HEAD)
            _, vjp_fn = jax.vjp(fn, q_ref[:, cs], f_ref[:, cs], i_ref[:, cs], g_ref[:, cs], lb_ref[:, cs],
                                ng_ref[...], s_ref[p, 0])
            *gparts, glb, gng, dst = vjp_fn((dy_ref[:, cs].astype(F32), dst_ref[p]))
            for part, gpart in enumerate(gparts):
                dp_ref[part, :, cs] = gpart.astype(dp_ref.dtype)
            dst_ref[p] = dst
            dlb_ref[:, cs] += glb
            dng_ref[...] += gng

    rev = lambda h, i: (nb - 1 - i, h)
    return pl.pallas_call(
        body, name="hgrn2_bwd",
        grid=(nh // HG_HEADS, nb),
        in_specs=_hg_specs(tb, nh, rev_nb=nb) + [
            pl.BlockSpec((HG_HEADS, 1, HEAD, HEAD), lambda h, i: (h, nb - 1 - i, 0, 0)),
            pl.BlockSpec((tb, wide), rev)],
        out_specs=[pl.BlockSpec((4, tb, wide), lambda h, i: (0, nb - 1 - i, h)),
                   pl.BlockSpec((1, wide), lambda h, i: (0, h)), pl.BlockSpec((1, HEAD), lambda h, i: (0, 0))],
        out_shape=[jax.ShapeDtypeStruct((4, t, nh * HEAD), BF16),
                   jax.ShapeDtypeStruct((1, nh * HEAD), F32), jax.ShapeDtypeStruct((1, HEAD), F32)],
        scratch_shapes=[pltpu.VMEM((HG_HEADS, HEAD, HEAD), F32)],
        compiler_params=_cparams(dimension_semantics=("arbitrary", "arbitrary")),
    )(proj, proj, proj, proj, lb, ng, _hg_mask(tb), states, dy)


def _fgate_consts(cb):
    r = lax.broadcasted_iota(jnp.int32, (cb, cb), 0)
    s = lax.broadcasted_iota(jnp.int32, (cb, cb), 1)
    return (r <= s).astype(F32), (r >= s).astype(F32)


def _fgate_fwd(xt, bias, cb=512):
    nh, t = xt.shape
    cb = min(cb, t)

    def body(x_ref, b_ref, o_ref):
        upper, _ = _fgate_consts(cb)
        carry = jnp.zeros((nh, 1), F32)
        for blk in range(t // cb):
            z = x_ref[:, blk * cb:(blk + 1) * cb] + b_ref[...]
            logf = jnp.minimum(z, 0.0) - jnp.log(1.0 + jnp.exp(-jnp.abs(z)))
            cs = _f32dot(logf, upper) + carry
            o_ref[:, blk * cb:(blk + 1) * cb] = cs
            carry = cs[:, cb - 1:cb]

    vm = pl.BlockSpec(memory_space=pltpu.VMEM)
    return pl.pallas_call(
        body, name="fgate_fwd", in_specs=[vm, vm], out_specs=vm,
        out_shape=jax.ShapeDtypeStruct((nh, t), F32), compiler_params=_cparams(),
    )(xt, bias)


def _fgate_bwd(xt, bias, dft, cb=512):
    nh, t = xt.shape
    cb = min(cb, t)
    nblk = t // cb

    def body(x_ref, b_ref, d_ref, dx_ref, db_ref):
        _, lower = _fgate_consts(cb)
        carry = jnp.zeros((nh, 1), F32)
        db = jnp.zeros((nh, 1), F32)
        for blk in range(nblk - 1, -1, -1):
            sl = slice(blk * cb, (blk + 1) * cb)
            dlogf = _f32dot(d_ref[:, sl], lower) + carry
            carry = dlogf[:, 0:1]
            z = x_ref[:, sl] + b_ref[...]
            dz = dlogf * (1.0 - _sigmoid(z))
            dx_ref[:, sl] = dz
            db = db + jnp.sum(dz, axis=1, keepdims=True)
        db_ref[...] = db

    vm = pl.BlockSpec(memory_space=pltpu.VMEM)
    return pl.pallas_call(
        body, name="fgate_bwd", in_specs=[vm, vm, vm], out_specs=[vm, vm],
        out_shape=[jax.ShapeDtypeStruct((nh, t), F32), jax.ShapeDtypeStruct((nh, 1), F32)],
        compiler_params=_cparams(),
    )(xt, bias, dft)


ATTN_GROUPS = 4
ATTN_FWD_HEADS = 2


def _attn_fwd(q, k, v, f_grp, blk):
    t, width = v.shape
    nh = width // HEAD
    nq = t // blk
    hpg = nh // ATTN_GROUPS

    def body(q_ref, k_ref, v_ref, fc_ref, o_ref, lse_ref):
        i = pl.program_id(0)
        tri = (lax.broadcasted_iota(jnp.int32, (blk, blk), 1) <= lax.broadcasted_iota(jnp.int32, (blk, blk), 0))
        for h0 in range(0, nh, ATTN_FWD_HEADS):
            heads = range(h0, min(h0 + ATTN_FWD_HEADS, nh))

            def tile(j, carries, masked):
                rs = pl.ds(pl.multiple_of(j * blk, blk), blk)
                out = []
                for h, (m, l, acc) in zip(heads, carries):
                    cs = slice(h * HEAD, (h + 1) * HEAD)
                    cs2 = slice(2 * h * HEAD, 2 * (h + 1) * HEAD)
                    s = _bdot_raw(q_ref[:, cs2], k_ref[rs, cs2], _NT)
                    if masked:
                        s = jnp.where(tri, s, NEG_INF)
                    m_new = jnp.maximum(m, jnp.max(s, axis=1, keepdims=True))
                    p = jnp.exp(s - m_new)
                    alpha = jnp.exp(m - m_new)
                    l_new = alpha * l + jnp.sum(p, axis=1, keepdims=True)
                    out.append((m_new, l_new, alpha * acc + _bdot_raw(p, v_ref[rs, cs], _NN)))
                return tuple(out)

            init = tuple((jnp.full((blk, 1), NEG_INF, F32), jnp.zeros((blk, 1), F32), jnp.zeros((blk, HEAD), F32))
                         for _ in heads)
            carries = lax.fori_loop(0, i, lambda j, c: tile(j, c, False), init)
            for h, (m, l, acc) in zip(heads, tile(i, carries, True)):
                o_ref[:, h * HEAD:(h + 1) * HEAD] = acc / l
                g, hh = divmod(h, hpg)
                lse_ref[g, :, hh:hh + 1] = m + jnp.log(l) + fc_ref[g, :, hh:hh + 1]

    vm = pl.BlockSpec(memory_space=pltpu.VMEM)
    stat = pl.BlockSpec((ATTN_GROUPS, blk, hpg), lambda i: (0, i, 0))
    return pl.pallas_call(
        body, name="fox_attn_fwd",
        grid=(nq,),
        in_specs=[pl.BlockSpec((blk, 2 * width), lambda i: (i, 0)), vm, vm, stat],
        out_specs=[pl.BlockSpec((blk, width), lambda i: (i, 0)), stat],
        out_shape=[jax.ShapeDtypeStruct((t, width), F32), jax.ShapeDtypeStruct((ATTN_GROUPS, t, hpg), F32)],
        compiler_params=_cparams(dimension_semantics=("parallel",)),
    )(q, k, v, f_grp)


def _outgate_bwd(o, proj_q, dz, tb):
    t, width = o.shape
    nh = width // HEAD
    hpg = nh // ATTN_GROUPS
    tb = min(tb, t)

    def body(o_ref, og_ref, dz_ref, do_ref, dog_ref, dl_ref):
        for h in range(nh):
            cs = slice(h * HEAD, (h + 1) * HEAD)
            ov = o_ref[:, cs]
            _, vjp_fn = jax.vjp(_f_outgate, ov, og_ref[:, cs])
            do, dog = vjp_fn((dz_ref[:, cs].astype(F32),))
            do = do.astype(do_ref.dtype)
            do_ref[:, cs] = do
            dog_ref[:, cs] = dog.astype(dog_ref.dtype)
            g, hh = divmod(h, hpg)
            dl_ref[g, :, hh:hh + 1] = jnp.sum(do.astype(F32) * ov, axis=1, keepdims=True)

    wide = pl.BlockSpec((tb, width), lambda i: (i, 0))
    return pl.pallas_call(body, name="out_gate_bwd", grid=(t // tb,),
                          in_specs=[wide, pl.BlockSpec((tb, width), lambda i: (i, 1)), wide],
                          out_specs=[wide, wide, pl.BlockSpec((ATTN_GROUPS, tb, hpg), lambda i: (0, i, 0))],
                          out_shape=[jax.ShapeDtypeStruct((t, width), BF16), jax.ShapeDtypeStruct((t, width), BF16),
                                     jax.ShapeDtypeStruct((ATTN_GROUPS, t, hpg), F32)],
                          compiler_params=_cparams(dimension_semantics=("parallel",)))(o, proj_q, dz)


def _qnorm_bwd(proj_q, gain, dq_n, dog, tb):
    t, width = dq_n.shape
    nh = width // HEAD
    tb = min(tb, t)

    def body(p_ref, g_ref, dq_ref, dog_ref, out_ref, dg_ref):
        i = pl.program_id(0)
        gv = g_ref[...]
        acc = None
        for h in range(nh):
            cs = slice(h * HEAD, (h + 1) * HEAD)
            _, vjp_fn = jax.vjp(_f_qnorm, p_ref[:, cs], gv)
            dp, dg = vjp_fn((dq_ref[:, cs],))
            out_ref[:, cs] = dp.astype(out_ref.dtype)
            acc = dg if acc is None else acc + dg
        out_ref[:, width:] = dog_ref[...]

        @pl.when(i == 0)
        def _():
            dg_ref[...] = acc

        @pl.when(i > 0)
        def _():
            dg_ref[...] += acc

    wide = pl.BlockSpec((tb, width), lambda i: (i, 0))
    vec = pl.BlockSpec(gain.shape, lambda i: (0, 0))
    return pl.pallas_call(body, name="q_norm_bwd", grid=(t // tb,), in_specs=[wide, vec, wide, wide],
                          out_specs=[pl.BlockSpec((tb, 2 * width), lambda i: (i, 0)), vec],
                          out_shape=[jax.ShapeDtypeStruct((t, 2 * width), BF16), jax.ShapeDtypeStruct(gain.shape, F32)],
                          compiler_params=_cparams(dimension_semantics=("arbitrary",)))(proj_q, gain, dq_n, dog)


def _attn_bwd(q, k, v, f_grp, do, lse, delta, blk):
    t, width = v.shape
    nh = width // HEAD
    nq = t // blk
    hpg = nh // ATTN_GROUPS
    gw = hpg * HEAD

    def body(q_ref, do_ref, k_ref, v_ref, fc_ref, lse_ref, dl_ref,
             dq_ref, dk_ref, dv_ref, dfc_ref, dfr_ref):
        g, j = pl.program_id(0), pl.program_id(1)
        tri = (lax.broadcasted_iota(jnp.int32, (blk, blk), 1) <= lax.broadcasted_iota(jnp.int32, (blk, blk), 0))

        @pl.when(j == 0)
        def _():
            dq_ref[...] = jnp.zeros_like(dq_ref)
            dfc_ref[...] = jnp.zeros_like(dfc_ref)

        def tile(i, carries, masked):
            rs = pl.ds(pl.multiple_of(i * blk, blk), blk)
            out = []
            for h, (dk, dv, dfs) in enumerate(carries):
                cs = slice(h * HEAD, (h + 1) * HEAD)
                cs2 = slice(2 * h * HEAD, 2 * (h + 1) * HEAD)
                csq = slice(2 * h * HEAD, (2 * h + 1) * HEAD)
                qi = q_ref[rs, csq]
                doi = do_ref[rs, cs]
                bias = fc_ref[0, rs, h:h + 1] - lse_ref[0, rs, h:h + 1]
                p = jnp.exp(_bdot_raw(q_ref[rs, cs2], k_ref[:, cs2], _NT) + bias)
                if masked:
                    p = jnp.where(tri, p, 0.0)
                ds = p * (_bdot_raw(doi, v_ref[:, cs], _NT) - dl_ref[0, rs, h:h + 1])
                dsb = ds.astype(BF16)
                dq_ref[rs, cs] += _bdot_raw(dsb, k_ref[:, csq], _NN)
                dfc_ref[0, rs, h:h + 1] += jnp.sum(ds, axis=1, keepdims=True)
                out.append((dk + _bdot_raw(dsb, qi, _TN), dv + _bdot_raw(p, doi, _TN),
                            dfs - jnp.sum(ds, axis=0, keepdims=True)))
            return tuple(out)

        init = tuple((jnp.zeros((blk, HEAD), F32), jnp.zeros((blk, HEAD), F32), jnp.zeros((1, blk), F32))
                     for _ in range(hpg))
        carries = lax.fori_loop(j + 1, nq, lambda i, c: tile(i, c, False), tile(j, init, True))
        for h, (dk, dv, dfs) in enumerate(carries):
            cs = slice(h * HEAD, (h + 1) * HEAD)
            dk_ref[:, cs] = dk
            dv_ref[:, cs] = dv.astype(dv_ref.dtype)
            dfr_ref[0, 0, h:h + 1, :] = dfs

    once = pl.Buffered(1)
    stat = pl.BlockSpec((1, t, hpg), lambda g, j: (g, 0, 0), pipeline_mode=once)
    kv_blk = pl.BlockSpec((blk, gw), lambda g, j: (j, g))
    frow = pl.BlockSpec((1, 1, hpg, blk), lambda g, j: (g, j, 0, 0))
    dq, dk, dv, dfc, dfr = pl.pallas_call(
        body, name="fox_attn_bwd",
        grid=(ATTN_GROUPS, nq),
        in_specs=[pl.BlockSpec((t, 2 * gw), lambda g, j: (0, g), pipeline_mode=once),
                  pl.BlockSpec((t, gw), lambda g, j: (0, g), pipeline_mode=once),
                  pl.BlockSpec((blk, 2 * gw), lambda g, j: (j, g)), kv_blk, stat, stat, stat],
        out_specs=[pl.BlockSpec((t, gw), lambda g, j: (0, g)), kv_blk, kv_blk,
                   pl.BlockSpec((1, t, hpg), lambda g, j: (g, 0, 0)), frow],
        out_shape=[jax.ShapeDtypeStruct((t, width), F32), jax.ShapeDtypeStruct((t, width), F32),
                   jax.ShapeDtypeStruct((t, width), BF16), jax.ShapeDtypeStruct((ATTN_GROUPS, t, hpg), F32),
                   jax.ShapeDtypeStruct((ATTN_GROUPS, nq, hpg, blk), F32)],
        compiler_params=_cparams(dimension_semantics=("parallel", "arbitrary")),
    )(q, do, k, v, f_grp, lse, delta)
    return dq, dk, dv, dfc, dfr


SUBLANES = 8


def _shift_down(u, n):
    r = pltpu.roll(u, n, 0)
    row = lax.broadcasted_iota(jnp.int32, (SUBLANES, u.shape[1]), 0)
    return jnp.concatenate([jnp.where(row < n, 0.0, r[:SUBLANES]), r[SUBLANES:]], axis=0)


def _shift_up(u, n):
    t = u.shape[0]
    r = pltpu.roll(u, t - n, 0)
    row = lax.broadcasted_iota(jnp.int32, (SUBLANES, u.shape[1]), 0)
    return jnp.concatenate([r[:t - SUBLANES], jnp.where(row >= SUBLANES - n, 0.0, r[t - SUBLANES:])], axis=0)


CONV_COLS = 2 * LANES


def _convglu_specs(t):
    return [pl.BlockSpec((2, t, CONV_COLS), lambda j: (0, 0, j)),
            pl.BlockSpec((2, CONV_TAPS, CONV_COLS), lambda j: (0, 0, j)),
            pl.BlockSpec((2, 1, CONV_COLS), lambda j: (0, 0, j))]


def _lane_blocks():
    return [slice(k * LANES, (k + 1) * LANES) for k in range(CONV_COLS // LANES)]


def _convglu_fwd(u, cw, cb):
    _, t, fp = u.shape

    def body(u_ref, w_ref, b_ref, a_ref, c_ref):
        for cs in _lane_blocks():
            c = []
            for hf in range(2):
                uv, w = u_ref[hf, :, cs].astype(F32), w_ref[hf, :, cs]
                c.append(w[0:1] * _shift_down(uv, 2) + w[1:2] * _shift_down(uv, 1) + w[2:3] * uv + b_ref[hf, :, cs])
                c_ref[hf, :, cs] = c[hf].astype(c_ref.dtype)
            a_ref[:, cs] = (_silu(c[0]) * c[1]).astype(a_ref.dtype)

    return pl.pallas_call(
        body, name="convglu_fwd",
        grid=(fp // CONV_COLS,),
        in_specs=_convglu_specs(t),
        out_specs=[pl.BlockSpec((t, CONV_COLS), lambda j: (0, j)),
                   pl.BlockSpec((2, t, CONV_COLS), lambda j: (0, 0, j))],
        out_shape=[jax.ShapeDtypeStruct((t, fp), BF16), jax.ShapeDtypeStruct((2, t, fp), BF16)],
        compiler_params=_cparams(dimension_semantics=("parallel",)),
    )(u, cw, cb)


def _convglu_bwd(u, c, cw, da):
    _, t, fp = u.shape

    def body(u_ref, c_ref, w_ref, da_ref, du_ref, dw_ref, db_ref):
        for cs in _lane_blocks():
            gc, vc = c_ref[0, :, cs].astype(F32), c_ref[1, :, cs].astype(F32)
            sg = _sigmoid(gc)
            dav = da_ref[:, cs].astype(F32)
            dcs = [dav * vc * (sg * (1.0 + gc * (1.0 - sg))), dav * (gc * sg)]
            for hf in range(2):
                dc, w, uv = dcs[hf], w_ref[hf, :, cs], u_ref[hf, :, cs].astype(F32)
                dc1, dc2 = _shift_up(dc, 1), _shift_up(dc, 2)
                du_ref[hf, :, cs] = (w[2:3] * dc + w[1:2] * dc1 + w[0:1] * dc2).astype(du_ref.dtype)
                dw_ref[hf, 0:1, cs] = jnp.sum(dc2 * uv, axis=0, keepdims=True)
                dw_ref[hf, 1:2, cs] = jnp.sum(dc1 * uv, axis=0, keepdims=True)
                dw_ref[hf, 2:3, cs] = jnp.sum(dc * uv, axis=0, keepdims=True)
                db_ref[hf, :, cs] = jnp.sum(dc, axis=0, keepdims=True)

    pair, taps, bias = _convglu_specs(t)
    return pl.pallas_call(
        body, name="convglu_bwd",
        grid=(fp // CONV_COLS,),
        in_specs=[pair, pair, taps, pl.BlockSpec((t, CONV_COLS), lambda j: (0, j))],
        out_specs=[pair, taps, bias],
        out_shape=[jax.ShapeDtypeStruct((2, t, fp), BF16), jax.ShapeDtypeStruct((2, CONV_TAPS, fp), F32),
                   jax.ShapeDtypeStruct((2, 1, fp), F32)],
        compiler_params=_cparams(dimension_semantics=("parallel",)),
    )(u, c, cw, da)


def _local_step(x, target, mods, lb, small, pre_w, get_w, put_g, *, tb=512, attn_blk=512):
    t, d = x.shape
    nh = d // HEAD
    nb = NDEV
    wts = {}
    vec = lambda *names: [mods[n] for n in names]

    def ffn_fwd(h2, l):
        u = _mm_wblk(h2, wts[f"up{l}"], BF16, f"ffn{l}_up", gb=nb // 2, split=2, tm=512)
        a, c = _convglu_fwd(u, small[f"conv_w{l}"], small[f"conv_b{l}"])
        f = _mm(a, wts[f"down{l}"], "nn", F32, f"ffn{l}_down", tk=4096)
        return (u, c), a, f

    def ffn_bwd(df, h2, uc, a, l):
        u, c = uc
        da = _mm(df, wts[f"down{l}"], "nt", BF16, f"ffn{l}_down_dx", tn=1536)
        dwd = _mm(a, df, "tn", BF16, f"ffn{l}_down_dw", tm=768, tk=t)
        du, dcw, dcb = _convglu_bwd(u, c, small[f"conv_w{l}"], da)
        dh2 = _mm_wblk_dx(du, wts[f"up{l}"], BF16, f"ffn{l}_up_dx", k=d, gb=nb // 2, split=2, tm=1024)
        dwu = _mm_wblk_dw(h2, du, f"ffn{l}_up_dw", nb=nb, gb=1, split=2, tk=t)
        return dh2, dwu, dwd, dcw, dcb

    (h_a,) = _row_fwd(_f_mod, [(x, d, 0)], vec("sh1_0", "sc1_0"), [BF16], tb=tb, name="l0_mod1")
    wts.update(get_w("l0a", h_a))
    proj_a = _mm_wblk(h_a, wts["a_in"], F32, "a_in", gb=nb // 2)
    ypre, states = _hgrn2_fwd(proj_a, lb, small["a_norm_g"], tb)
    pre_w("l0b", ypre)
    wts.update(get_w("l0b", ypre))
    y_a = _mm(ypre, wts["a_out"], "nn", F32, "a_out")
    x1, h2_0 = _row_fwd(_f_res_mod, [(x, d, 0), (y_a, d, 0)], vec("g1_0", "sh2_0", "sc2_0"), [F32, BF16],
                        tb=tb, name="l0_res_mod2")
    wts.update(get_w("l0b_ffn", h2_0))
    u0, a0, f0 = ffn_fwd(h2_0, 0)
    x2, h_kv, h_q = _row_fwd(_f_res_mod2, [(x1, d, 0), (f0, d, 0)],
                             [mods["g2_0"] + pre_w("l1", f0)] + vec("kv_sh", "kv_sc", "sh1_1", "sc1_1"),
                             [F32, BF16, BF16], tb=tb, name="l0_res_kvmod_qmod")
    wts.update(get_w("l1", h_kv))
    proj_k = _mm(h_kv, wts["kv_k"], "nt", F32, "k_proj")
    v_b = _mm(h_kv, wts["kv_v"], "nt", BF16, "v_proj")
    proj_f = _mm(h_kv, wts["kv_f"], "nt", F32, "kv_fproj")
    f_logit_t = proj_f[:, :nh].T
    f_bias = small["kv_b_f"].reshape(nh, 1)
    f_t = _fgate_fwd(f_logit_t, f_bias)
    f_grp = f_t.reshape(ATTN_GROUPS, nh // ATTN_GROUPS, t).transpose(0, 2, 1)
    (k_n,) = _row_fwd(_f_knorm_aug, [(proj_k, HEAD, 0)] + [(piece, 1, 0) for piece in _split3(-f_t.T)],
                      [small["k_norm_g"]], [BF16], nsub=nh, tb=tb, name="k_norm")
    proj_q = _mm_wblk(h_q, wts["b_q"], F32, "b_q", gb=nb)
    (q_n,) = _row_fwd(_f_qnorm_aug, [(proj_q, HEAD, 0)], [small["q_norm_g"]], [BF16], nsub=nh, tb=tb,
                      name="q_norm")
    o_att, lse = _attn_fwd(q_n, k_n, v_b, f_grp, attn_blk)
    (z,) = _row_fwd(_f_outgate, [(o_att, HEAD, 0), (proj_q, HEAD, 1)], [], [BF16], nsub=nh, tb=tb, name="out_gate")
    y_b = _mm(z, wts["b_out"], "nn", F32, "b_out")
    x3, h2_1 = _row_fwd(_f_res_mod, [(x2, d, 0), (y_b, d, 0)], vec("g1_1", "sh2_1", "sc2_1"), [F32, BF16],
                        tb=tb, name="l1_res_mod2")
    u1, a1, f1 = ffn_fwd(h2_1, 1)
    loss, dx4, df1, dg2_1 = _loss_call(x3, f1, mods["g2_1"], target, tb)

    g = {}
    dmods = {"g2_1": dg2_1}
    dh2, g["up1"], g["down1"], g["conv_w1"], g["conv_b1"] = ffn_bwd(df1, h2_1, u1, a1, 1)
    (dx2, dy_b), (dmods["g1_1"], dmods["sh2_1"], dmods["sc2_1"]) = _row_bwd(
        _f_res_mod, [(x2, d, 0), (y_b, d, 0)], vec("g1_1", "sh2_1", "sc2_1"),
        [(dx4, d, 0), (dh2, d, 0)], [F32, BF16], tb=tb, name="l1_res_mod2_bwd")
    dz = _mm(dy_b, wts["b_out"], "nt", BF16, "b_out_dx")
    g["b_out"] = _mm(z, dy_b, "tn", BF16, "b_out_dw", tk=t)
    do_att, dog, delta = _outgate_bwd(o_att, proj_q, dz, tb)
    dq_n, dk_n, dv, dfc_q, dfr_k = _attn_bwd(q_n, k_n, v_b, f_grp, do_att, lse, delta, attn_blk)
    dproj_q, g["q_norm_g"] = _qnorm_bwd(proj_q, small["q_norm_g"], dq_n, dog, tb)
    dh_q = _mm_wblk_dx(dproj_q, wts["b_q"], BF16, "b_q_dx", k=d, gb=nb)
    g["b_q"] = _mm_wblk_dw(h_q, dproj_q, "b_q_dw", nb=nb, gb=nb // 4, tk=t)
    (dpk,), (g["k_norm_g"],) = _row_bwd(_f_knorm, [(proj_k, HEAD, 0)], [small["k_norm_g"]],
                                        [(dk_n, HEAD, 0)], [BF16], nsub=nh, tb=tb, name="k_norm_bwd")
    df_t = dfc_q.transpose(0, 2, 1).reshape(nh, t) + dfr_k.transpose(0, 2, 1, 3).reshape(nh, t)
    dflogit_t, g["kv_b_f"] = _fgate_bwd(f_logit_t, f_bias, df_t)
    dproj_f = jnp.pad(dflogit_t.T, ((0, 0), (0, LANES - nh))).astype(BF16)
    dh_kv = _mm(dpk, wts["kv_k"], "nn", BF16, "k_proj_dx")
    dh_kv_v = _mm(dv, wts["kv_v"], "nn", BF16, "v_proj_dx")
    dh_kv_f = _mm(dproj_f, wts["kv_f"], "nn", BF16, "kv_fproj_dx")
    g["kv_k"] = _mm(dpk, h_kv, "tn", BF16, "k_proj_dw", tk=t)
    g["kv_v"] = _mm(dv, h_kv, "tn", BF16, "v_proj_dw", tk=t)
    g["kv_f"] = _mm(dproj_f, h_kv, "tn", F32, "kv_fproj_dw", tk=1024)
    sent = put_g("l1", {n: g.pop(n) for n in ("b_out", "b_q", "kv_k", "kv_v", "kv_f", "up1", "down1")})
    (dx1, df0), (dmods["g2_0"], dmods["kv_sh"], dmods["kv_sc"], dmods["sh1_1"], dmods["sc1_1"]) = _row_bwd(
        _f_res_mod2, [(x1, d, 0), (f0, d, 0)], [mods["g2_0"] + sent] + vec("kv_sh", "kv_sc", "sh1_1", "sc1_1"),
        [(dx2, d, 0), (dh_kv, d, 0), (dh_q, d, 0)], [F32, BF16], tb=tb, name="l0_res_kvmod_qmod_bwd",
        cot_add=[(1, dh_kv_v), (1, dh_kv_f)])
    dh2, g["up0"], g["down0"], g["conv_w0"], g["conv_b0"] = ffn_bwd(df0, h2_0, u0, a0, 0)
    (dx0, dy_a), (dmods["g1_0"], dmods["sh2_0"], dmods["sc2_0"]) = _row_bwd(
        _f_res_mod, [(x, d, 0), (y_a, d, 0)], vec("g1_0", "sh2_0", "sc2_0"),
        [(dx1, d, 0), (dh2, d, 0)], [F32, BF16], tb=tb, name="l0_res_mod2_bwd")
    dypre = _mm(dy_a, wts["a_out"], "nt", BF16, "a_out_dx")
    g["a_out"] = _mm(ypre, dy_a, "tn", BF16, "a_out_dw", tk=t)
    sent = put_g("l0b", {n: g.pop(n) for n in ("a_out", "up0", "down0")})
    dproj_a, dlb, g["a_norm_g"] = _hgrn2_bwd(proj_a, lb + sent, small["a_norm_g"], states, dypre, tb)
    dh_a = _mm_wblk_dx(dproj_a, wts["a_in"], BF16, "a_in_dx", k=d, gb=nb, split=4, tm=512)
    put_g("l0a", {"a_in": _mm_wblk_dw(h_a, dproj_a, "a_in_dw", nb=nb, gb=1, split=4, tk=t)})
    (grad_x,), (dmods["sh1_0"], dmods["sc1_0"]) = _row_bwd(
        _f_mod, [(x, d, 0)], vec("sh1_0", "sc1_0"), [(dh_a, d, 0)], [F32], tb=tb, name="l0_mod1_bwd",
        add_to=(0, dx0))
    return loss, grad_x, dmods, dlb, g


def _position():
    return lax.axis_index("x"), lax.axis_index("y"), lax.axis_index("c")


_XCHG_EFFECT = pltpu.SideEffectType.DATAFLOW_SIDE_EFFECTING
ALL_PEERS = (1, 2, 3, 4, 5, 6, 7)
SAME_CORE = (2, 4, 6)


def _xchg_copies(src_refs, land_refs, send_sems, recv_sems, local_sems, scatter, rels):
    x, y, cc = _position()
    me = 4 * x + 2 * y + cc
    remote, local = [], []
    for a, (src, land) in enumerate(zip(src_refs, land_refs)):
        local.append(pltpu.make_async_copy(src.at[me] if scatter else src, land.at[me], local_sems.at[a]))
        for idx, rel in enumerate(rels):
            px = 1 - x if rel & 4 else x
            py = 1 - y if rel & 2 else y
            pc = 1 - cc if rel & 1 else cc
            k = len(rels) * a + idx
            remote.append(pltpu.make_async_remote_copy(
                src_ref=src.at[4 * px + 2 * py + pc] if scatter else src, dst_ref=land.at[me],
                send_sem=send_sems.at[k], recv_sem=recv_sems.at[k], device_id=(px, py, pc), device_id_type=_MESH))
    return remote, local


def _xchg_start(srcs, scatter, rels, after, name):
    n = len(srcs)
    lands = [lax.empty(s.shape if scatter else (NDEV, *s.shape), s.dtype) for s in srcs]

    def body(*refs):
        remote, local = _xchg_copies(refs[:n], refs[n:2 * n], *refs[2 * n + 1:2 * n + 4], scatter, rels)
        for cp in local + remote:
            cp.start()
        token = refs[-1]
        token[...] = jnp.zeros_like(token)

    hbm = pl.BlockSpec(memory_space=pltpu.HBM)
    sem = pl.BlockSpec(memory_space=pltpu.SEMAPHORE)
    out = pl.pallas_call(
        body, name=name,
        out_shape=(pltpu.SemaphoreType.DMA((len(rels) * n,)), pltpu.SemaphoreType.DMA((len(rels) * n,)),
                   pltpu.SemaphoreType.DMA((n,)),
                   *[pltpu.HBM(a.shape, a.dtype) for a in srcs + lands], jax.ShapeDtypeStruct((8, LANES), F32)),
        in_specs=[hbm] * (2 * n) + [pl.BlockSpec(memory_space=pl.ANY)],
        out_specs=(sem, sem, sem, *[hbm] * (2 * n), pl.BlockSpec(memory_space=pltpu.VMEM)),
        input_output_aliases={i: 3 + i for i in range(2 * n)},
        compiler_params=pltpu.CompilerParams(has_side_effects=_XCHG_EFFECT),
    )(*[pltpu.with_memory_space_constraint(a, pltpu.HBM) for a in srcs + lands], after)
    return out[:-1], out[-1][0, 0]


def _xchg_wait(handles, after, scatter, rels, name):
    n = (len(handles) - 3) // 2

    def body(*refs):
        remote, local = _xchg_copies(refs[:n], refs[n:2 * n], *refs[2 * n:2 * n + 3], scatter, rels)
        for cp in remote:
            cp.wait_send()
            cp.wait_recv()
        for cp in local:
            cp.wait()

    hbm = pl.BlockSpec(memory_space=pltpu.HBM)
    sem = pl.BlockSpec(memory_space=pltpu.SEMAPHORE)
    thru = list(handles[3:])
    afters = list(after) if isinstance(after, (list, tuple)) else [after]
    out = pl.pallas_call(
        body, name=name,
        out_shape=tuple(pltpu.HBM(a.shape, a.dtype) for a in thru),
        in_specs=[hbm] * (2 * n) + [sem, sem, sem] + [pl.BlockSpec(memory_space=pl.ANY)] * len(afters),
        out_specs=tuple([hbm] * (2 * n)),
        input_output_aliases={i: i for i in range(2 * n)},
        compiler_params=pltpu.CompilerParams(has_side_effects=_XCHG_EFFECT),
    )(*thru, *handles[:3], *afters)
    return list(out[n:])


def _sibling_copies(land_refs, send_sems, recv_sems):
    x, y, cc = _position()

    def copy(a, q, core):
        slot = land_refs[a].at[2 * q + core]
        return pltpu.make_async_remote_copy(
            src_ref=slot, dst_ref=slot, send_sem=send_sems.at[NCHIP * a + q], recv_sem=recv_sems.at[NCHIP * a + q],
            device_id=(x, y, 1 - cc), device_id_type=_MESH)

    pairs = [(a, q) for a in range(len(land_refs)) for q in range(NCHIP)]
    return [copy(a, q, cc) for a, q in pairs], [copy(a, q, 1 - cc) for a, q in pairs]


def _sibling_forward_start(lands, name, after=None):
    n = len(lands)
    deps = [] if after is None else [after]

    def body(*refs):
        sends, _ = _sibling_copies(refs[:n], refs[n + len(deps)], refs[n + len(deps) + 1])
        for cp in sends:
            cp.start()
        refs[-1][...] = jnp.zeros_like(refs[-1])

    hbm = pl.BlockSpec(memory_space=pltpu.HBM)
    sem = pl.BlockSpec(memory_space=pltpu.SEMAPHORE)
    out = pl.pallas_call(
        body, name=name,
        out_shape=(pltpu.SemaphoreType.DMA((NCHIP * n,)), pltpu.SemaphoreType.DMA((NCHIP * n,)),
                   *[pltpu.HBM(a.shape, a.dtype) for a in lands], jax.ShapeDtypeStruct((8, LANES), F32)),
        in_specs=[hbm] * n + [pl.BlockSpec(memory_space=pl.ANY)] * len(deps),
        out_specs=(sem, sem, *[hbm] * n, pl.BlockSpec(memory_space=pltpu.VMEM)),
        input_output_aliases={i: 2 + i for i in range(n)},
        compiler_params=pltpu.CompilerParams(has_side_effects=_XCHG_EFFECT),
    )(*lands, *deps)
    return out[:-1], out[-1][0, 0]


def _sibling_forward_wait(handles, after, name):
    n = len(handles) - 2

    def body(*refs):
        sends, arrivals = _sibling_copies(refs[:n], refs[n], refs[n + 1])
        for cp in sends:
            cp.wait_send()
        for cp in arrivals:
            cp.wait_recv()

    hbm = pl.BlockSpec(memory_space=pltpu.HBM)
    sem = pl.BlockSpec(memory_space=pltpu.SEMAPHORE)
    lands = list(handles[2:])
    return list(pl.pallas_call(
        body, name=name,
        out_shape=tuple(pltpu.HBM(a.shape, a.dtype) for a in lands),
        in_specs=[hbm] * n + [sem, sem, pl.BlockSpec(memory_space=pl.ANY)],
        out_specs=tuple([hbm] * n),
        input_output_aliases={i: i for i in range(n)},
        compiler_params=pltpu.CompilerParams(has_side_effects=_XCHG_EFFECT),
    )(*lands, *handles[:2], after))


def _slab_sum(slabs, name, tr=None):
    n, r, c = slabs.shape
    tr = r if tr is None else tr

    def body(s_ref, o_ref):
        acc = s_ref[0].astype(F32)
        for q in range(1, n):
            acc = acc + s_ref[q].astype(F32)
        o_ref[...] = acc

    return pl.pallas_call(body, name=name, grid=(r // tr,),
                          in_specs=[pl.BlockSpec((n, tr, c), lambda i: (0, i, 0))],
                          out_specs=pl.BlockSpec((tr, c), lambda i: (i, 0)),
                          out_shape=jax.ShapeDtypeStruct((r, c), F32),
                          compiler_params=_cparams(dimension_semantics=("parallel",)))(slabs)


def _slab_sum_unpad(arrs, n_loc, n_pad, name, tr=256):
    n, r, c = arrs[0].shape
    runs = c // n_pad

    def body(*refs):
        o_ref = refs[-1]
        for k, s_ref in enumerate(refs[:-1]):
            acc = s_ref[0].astype(F32)
            for q in range(1, n):
                acc = acc + s_ref[q].astype(F32)
            for u in range(runs):
                o_ref[k, :, u * n_loc:(u + 1) * n_loc] = acc[:, u * n_pad:u * n_pad + n_loc]

    return pl.pallas_call(body, name=name, grid=(r // tr,),
                          in_specs=[pl.BlockSpec((n, tr, c), lambda i: (0, i, 0))] * len(arrs),
                          out_specs=pl.BlockSpec((len(arrs), tr, runs * n_loc), lambda i: (0, i, 0)),
                          out_shape=jax.ShapeDtypeStruct((len(arrs), r, runs * n_loc), F32),
                          compiler_params=_cparams(dimension_semantics=("parallel",)))(*arrs)


def _ada_fwd(c_all, ada_w, kv_ada_w, logits):
    rows, d = c_all.shape
    n0, nkv = ada_w.shape[2], kv_ada_w.shape[1]

    def body(c_ref, w_ref, kw_ref, lg_ref, part_ref, cact_ref, lb_ref):
        ca = _silu(c_ref[...])
        cact_ref[...] = ca
        part_ref[:, 0:n0] = _bdot_raw(ca, w_ref[0], _NN)
        part_ref[:, n0:2 * n0] = _bdot_raw(ca, w_ref[1], _NN)
        part_ref[:, 2 * n0:2 * n0 + nkv] = _bdot_raw(ca, kw_ref[...], _NN)
        lb_ref[...] = _sigmoid(lg_ref[0:1, :] - lg_ref[1:2, :])

    vm = pl.BlockSpec(memory_space=pltpu.VMEM)
    return pl.pallas_call(
        body, name="ada_fwd", in_specs=[vm, vm, vm, vm], out_specs=[vm, vm, vm],
        out_shape=[jax.ShapeDtypeStruct((rows, 2 * n0 + nkv), F32), jax.ShapeDtypeStruct((rows, d), F32),
                   jax.ShapeDtypeStruct((1, d), F32)],
        compiler_params=_cparams(),
    )(c_all, ada_w, kv_ada_w, logits)


def _ada_bwd(c_act, dm0, dm1, dkv, lb, dlb):
    rows, d = c_act.shape

    def body(c_ref, d0_ref, d1_ref, dk_ref, lb_ref, dlb_ref, dw_ref, dkw_ref, dlg_ref):
        ca = c_ref[...]
        dw_ref[0] = _bdot_raw(ca, d0_ref[...], _TN)
        dw_ref[1] = _bdot_raw(ca, d1_ref[...], _TN)
        dkw_ref[...] = _bdot_raw(ca, dk_ref[...], _TN)
        lbv = lb_ref[...]
        dl0 = dlb_ref[...] * lbv * (1.0 - lbv)
        dlg_ref[0:1, :] = dl0
        dlg_ref[1:2, :] = -dl0

    vm = pl.BlockSpec(memory_space=pltpu.VMEM)
    return pl.pallas_call(
        body, name="ada_bwd", in_specs=[vm] * 6, out_specs=[vm, vm, vm],
        out_shape=[jax.ShapeDtypeStruct((2, d, dm0.shape[1]), F32), jax.ShapeDtypeStruct((d, dkv.shape[1]), F32),
                   jax.ShapeDtypeStruct((2, d), F32)],
        compiler_params=_cparams(),
    )(c_act, dm0, dm1, dkv, lb, dlb)


def _adamw(w, g, m, v, name, tr=512, after=None):
    r, c = w.shape
    tr = _divisor_tile(r, tr, unit=8)
    c1 = 1.0 - ADAM_B1 ** ADAM_STEP
    c2 = 1.0 - ADAM_B2 ** ADAM_STEP
    deps = [] if after is None else [after]

    def body(w_ref, g_ref, m_ref, v_ref, *rest):
        d_ref, mo_ref, vo_ref = rest[len(deps):]
        gv = g_ref[...]
        mn = ADAM_B1 * m_ref[...] + (1.0 - ADAM_B1) * gv
        vn = ADAM_B2 * v_ref[...] + (1.0 - ADAM_B2) * (gv * gv)
        d_ref[...] = -ADAM_LR * ((mn / c1) / (jnp.sqrt(vn / c2) + ADAM_EPS) + ADAM_WD * w_ref[...])
        mo_ref[...] = mn
        vo_ref[...] = vn

    spec = pl.BlockSpec((tr, c), lambda i: (i, 0))
    out = jax.ShapeDtypeStruct((r, c), F32)
    return pl.pallas_call(body, name=name, grid=(r // tr,),
                          in_specs=[spec] * 4 + [pl.BlockSpec(a.shape, lambda i: (0, 0)) for a in deps],
                          out_specs=[spec] * 3, out_shape=[out, out, out],
                          compiler_params=_cparams(dimension_semantics=("parallel",)))(w, g, m, v, *deps)


def _pad_rows(a, rows):
    return jnp.pad(a, ((0, rows - a.shape[0]), (0, 0)))


def _pack_small(parts, lanes=LANES, row_unit=8):
    flat = jnp.concatenate([p.reshape(-1).astype(F32) for p in parts])
    rows = _round_up(-(-flat.shape[0] // lanes), row_unit)
    return jnp.pad(flat, (0, rows * lanes - flat.shape[0])).reshape(rows, lanes)


def _unpack_small(flat, shapes):
    out, off = [], 0
    for s in shapes:
        n = 1
        for k in s:
            n *= k
        out.append(flat[off:off + n].reshape(s))
        off += n
    return out


def _pad_shard_cols(a, n_loc, n_pad):
    lead, runs = a.shape[:-1], a.shape[-1] // n_loc
    a = a.reshape(*lead, runs, n_loc)
    a = jnp.pad(a, [(0, 0)] * (len(lead) + 1) + [(0, n_pad - n_loc)])
    return a.reshape(*lead, runs * n_pad)


def _unpad_shard_cols(a, n_loc, n_pad):
    lead, runs = a.shape[:-1], a.shape[-1] // n_pad
    return a.reshape(*lead, runs, n_pad)[..., :n_loc].reshape(*lead, runs * n_loc)


def kernel(x, c, ada_w, ada_b, a_w_in, a_lb_logits, a_norm_g, a_w_out, kv_ada_w, kv_ada_b, kv_w, kv_b_f, k_norm_g, b_w_q, q_norm_g, b_w_out, ffn_w_up, ffn_conv_w, ffn_conv_b, ffn_w_down, loss_target, m_ada_w, m_ada_b, m_a_w_in, m_a_lb_logits, m_a_norm_g, m_a_w_out, m_kv_ada_w, m_kv_ada_b, m_kv_w, m_kv_b_f, m_k_norm_g, m_b_w_q, m_q_norm_g, m_b_w_out, m_ffn_w_up, m_ffn_conv_w, m_ffn_conv_b, m_ffn_w_down, v_ada_w, v_ada_b, v_a_w_in, v_a_lb_logits, v_a_norm_g, v_a_w_out, v_kv_ada_w, v_kv_ada_b, v_kv_w, v_kv_b_f, v_k_norm_g, v_b_w_q, v_q_norm_g, v_b_w_out, v_ffn_w_up, v_ffn_conv_w, v_ffn_conv_b, v_ffn_w_down):
    t, d = x.shape[1], x.shape[2]
    nh = d // HEAD
    ncw = ffn_w_up.shape[2]
    rd = ffn_w_down.shape[1]
    assert ncw == 2 * rd
    rp = _round_up(rd, LANES)
    ncp = 2 * rp
    two_f = ncw * NDEV
    fp = ncp * NDEV // 2
    me = 4 * lax.axis_index("x") + 2 * lax.axis_index("y") + lax.axis_index("c")
    weights = dict(ada_w=ada_w, ada_b=ada_b, a_w_in=a_w_in, a_lb_logits=a_lb_logits, a_norm_g=a_norm_g,
                   a_w_out=a_w_out, kv_ada_w=kv_ada_w, kv_ada_b=kv_ada_b, kv_w=kv_w, kv_b_f=kv_b_f,
                   k_norm_g=k_norm_g, b_w_q=b_w_q, q_norm_g=q_norm_g, b_w_out=b_w_out, ffn_w_up=ffn_w_up,
                   ffn_conv_w=ffn_conv_w, ffn_conv_b=ffn_conv_b, ffn_w_down=ffn_w_down)
    m_in = dict(ada_w=m_ada_w, ada_b=m_ada_b, a_w_in=m_a_w_in, a_lb_logits=m_a_lb_logits, a_norm_g=m_a_norm_g,
                a_w_out=m_a_w_out, kv_ada_w=m_kv_ada_w, kv_ada_b=m_kv_ada_b, kv_w=m_kv_w, kv_b_f=m_kv_b_f,
                k_norm_g=m_k_norm_g, b_w_q=m_b_w_q, q_norm_g=m_q_norm_g, b_w_out=m_b_w_out, ffn_w_up=m_ffn_w_up,
                ffn_conv_w=m_ffn_conv_w, ffn_conv_b=m_ffn_conv_b, ffn_w_down=m_ffn_w_down)
    v_in = dict(ada_w=v_ada_w, ada_b=v_ada_b, a_w_in=v_a_w_in, a_lb_logits=v_a_lb_logits, a_norm_g=v_a_norm_g,
                a_w_out=v_a_w_out, kv_ada_w=v_kv_ada_w, kv_ada_b=v_kv_ada_b, kv_w=v_kv_w, kv_b_f=v_kv_b_f,
                k_norm_g=v_k_norm_g, b_w_q=v_b_w_q, q_norm_g=v_q_norm_g, b_w_out=v_b_w_out, ffn_w_up=v_ffn_w_up,
                ffn_conv_w=v_ffn_conv_w, ffn_conv_b=v_ffn_conv_b, ffn_w_down=v_ffn_w_down)
    order = list(weights)

    up_loc = _pad_shard_cols(ffn_w_up, rd, rp).astype(BF16)
    down_loc = jnp.pad(ffn_w_down, ((0, 0), (0, rp - rd), (0, 0))).astype(BF16)
    gather_names = {"l0b": ["a_out", "up0", "down0"], "l1": ["kv", "b_q", "b_out", "up1", "down1"]}
    forward_names = {"l0b": ["a_out"], "l0b_ffn": ["up0", "down0"], "l1": gather_names["l1"]}
    shards = {"a_out": a_w_out[0].astype(BF16), "up0": up_loc[0], "down0": down_loc[0], "kv": kv_w.T.astype(BF16),
              "b_q": b_w_q[0].astype(BF16), "b_out": b_w_out[0].astype(BF16), "up1": up_loc[1],
              "down1": down_loc[1]}
    pre = _pack_small([c, a_lb_logits, ffn_conv_w])
    in_flight = {}
    pre_flight, _ = _xchg_start([pre], False, ALL_PEERS, pre, "gather_small_inputs_start")
    (pre_all,) = _xchg_wait(pre_flight, pre, False, ALL_PEERS, "gather_small_inputs_wait")
    pre_all = pre_all.reshape(NDEV, -1)
    c_all = pre_all[:, :d]
    logits = pre_all[:, d:d + 2 * HEAD].reshape(NDEV, 2, HEAD).transpose(1, 0, 2).reshape(2, d)
    conv_w_full = pre_all[:, d + 2 * HEAD:d + 2 * HEAD + 2 * CONV_TAPS * ncw]
    conv_w_full = conv_w_full.reshape(NDEV, 2, CONV_TAPS, ncw).transpose(1, 2, 0, 3).reshape(2, CONV_TAPS, two_f)

    part, c_act, lb = _ada_fwd(_pad_rows(c_all, 2 * NDEV), ada_w, kv_ada_w, logits)
    part_flight, _ = _xchg_start([part[:NDEV]], False, ALL_PEERS, part, "gather_adaln_start")
    in_flight["l0a"], _ = _xchg_start([a_w_in[0].astype(BF16)], False, SAME_CORE, part_flight[-1], "gather_l0a_start")
    (part_all,) = _xchg_wait(part_flight, in_flight["l0a"][-1], False, ALL_PEERS, "gather_adaln_wait")
    forwarding = {}
    mine = lax.dynamic_index_in_dim(part_all, me, axis=1, keepdims=False)
    n0, nkv = ada_w.shape[2], kv_ada_w.shape[1]
    mod_names = ["sh1", "sc1", "g1", "sh2", "sc2", "g2"]
    mods = {}
    for l in range(2):
        row = mine[:, l * n0:(l + 1) * n0].reshape(-1) + ada_b[l]
        for k, nm in enumerate(mod_names):
            mods[f"{nm}_{l}"] = row[k * d:(k + 1) * d].reshape(1, d)
    kvrow = mine[:, 2 * n0:2 * n0 + nkv].reshape(-1) + kv_ada_b
    mods["kv_sh"], mods["kv_sc"] = kvrow[:d].reshape(1, d), kvrow[d:].reshape(1, d)

    def start_gather(grp, dep):
        srcs = [shards[n] for n in gather_names[grp]]
        in_flight[grp], started = _xchg_start(srcs, False, SAME_CORE, dep, f"gather_{grp}_start")
        return started

    zero = start_gather("l0b", part_all)
    mods["sh1_0"] = mods["sh1_0"] + zero

    small = {"a_norm_g": a_norm_g, "k_norm_g": k_norm_g.reshape(1, HEAD), "q_norm_g": q_norm_g, "kv_b_f": kv_b_f}
    for l in range(2):
        small[f"conv_w{l}"] = _pad_shard_cols(conv_w_full[l], rd, rp).reshape(CONV_TAPS, 2, fp).transpose(1, 0, 2)
        small[f"conv_b{l}"] = _pad_shard_cols(ffn_conv_b[l], rd, rp).reshape(2, 1, fp)

    def pre_w(grp, after):
        arrived = _xchg_wait(in_flight[grp], after, False, SAME_CORE, f"gather_{grp}_wait")
        if grp == "l0b":
            forwarding[grp], _ = _sibling_forward_start(arrived[:1], "gather_l0b_to_sibling_start")
            forwarding["l0b_ffn"], started = _sibling_forward_start(
                arrived[1:], "gather_l0b_ffn_to_sibling_start", after=forwarding[grp][-1])
            return started
        forwarding[grp], started = _sibling_forward_start(arrived, f"gather_{grp}_to_sibling_start")
        return started

    def get_w(grp, after):
        if grp == "l0a":
            arrived = _xchg_wait(in_flight["l0a"], after, False, SAME_CORE, "gather_l0a_wait")
            handles, _ = _sibling_forward_start(arrived, "gather_l0a_to_sibling_start")
            return {"a_in": _sibling_forward_wait(handles, after, "gather_l0a_to_sibling_wait")[0]}
        full = _sibling_forward_wait(forwarding[grp], after, f"gather_{grp}_to_sibling_wait")
        if grp == "l0b":
            started = start_gather("l1", full[0])
            full[0] = full[0] + started.astype(full[0].dtype)
        got = dict(zip(forward_names[grp], full))
        out = {}
        for n, a in got.items():
            if n in ("a_out", "b_out"):
                out[n] = a.reshape(d, d)
            elif n in ("down0", "down1"):
                out[n] = a.reshape(fp, d)
            elif n == "kv":
                kv_t = a.reshape(NDEV * kv_w.shape[1], d)
                out["kv_k"], out["kv_v"] = kv_t[:d], kv_t[d:2 * d]
                out["kv_f"] = jnp.pad(kv_t[2 * d:], ((0, LANES - nh), (0, 0)))
            else:
                out[n] = a
        return out

    scatter_flight, g_last = {}, {}

    def put_g(grp, gr):
        if grp == "l0a":
            g_last.update(gr)
            return zero
        if grp == "l1":
            g_kvw = jnp.concatenate([gr["kv_k"], gr["kv_v"], gr["kv_f"][:nh].astype(BF16)], axis=0)
            arrs = {"kv_w": g_kvw.reshape(NDEV, kv_w.shape[1], d), "b_w_q": gr["b_q"],
                    "b_w_out": gr["b_out"].reshape(NDEV, d // NDEV, d), "up1": gr["up1"],
                    "down1": gr["down1"].reshape(NDEV, rp, d)}
        else:
            arrs = {"a_w_out": gr["a_out"].reshape(NDEV, d // NDEV, d), "up0": gr["up0"],
                    "down0": gr["down0"].reshape(NDEV, rp, d)}
        srcs = list(arrs.values())
        handles, sent = _xchg_start(srcs, True, ALL_PEERS, srcs[0], f"scatter_{grp}_start")
        scatter_flight[grp] = (list(arrs), handles)
        return sent

    loss_v, grad_x, dmods, dlb, g = _local_step(x[0], loss_target[0], mods, lb, small, pre_w, get_w, put_g)

    g_sum, landed_up = {}, {}
    for grp in ("l1", "l0b"):
        names, handles = scatter_flight[grp]
        for nm, a in zip(names, _xchg_wait(handles, grad_x, True, ALL_PEERS, f"scatter_{grp}_wait")):
            if nm in ("up0", "up1"):
                landed_up[nm] = a
            else:
                g_sum[nm] = _slab_sum(a, f"rs_slab_sum_{nm}")

    def conv_w_grad(a):
        return _unpad_shard_cols(a.transpose(1, 0, 2).reshape(CONV_TAPS, 2 * fp), rd, rp)

    def conv_b_grad(a):
        return _unpad_shard_cols(a.reshape(2 * fp), rd, rp)

    dmod_vec = [dmods[f"{nm}_{l}"] for l in range(2) for nm in mod_names] + [dmods["kv_sh"], dmods["kv_sc"]]
    post = _pack_small(dmod_vec + [dlb, g["a_norm_g"], g["k_norm_g"], g["q_norm_g"],
                                   jnp.pad(g["kv_b_f"].reshape(-1), (0, LANES - nh)),
                                   conv_w_grad(g["conv_w0"]), conv_w_grad(g["conv_w1"]),
                                   conv_b_grad(g["conv_b0"]), conv_b_grad(g["conv_b1"]), loss_v])
    post_flight, _ = _xchg_start([post], False, ALL_PEERS, post, "gather_small_grads_start")
    a_in_flight, a_in_sent = _xchg_start([g_last["a_in"]], True, ALL_PEERS, post_flight[-1], "scatter_l0a_start")
    a_in_sent = a_in_sent.reshape(1, 1)
    grads = {
        "a_w_out": g_sum["a_w_out"].reshape(a_w_out.shape),
        "kv_w": g_sum["kv_w"].T,
        "b_w_q": g_sum["b_w_q"].reshape(b_w_q.shape),
        "b_w_out": g_sum["b_w_out"].reshape(b_w_out.shape),
        "ffn_w_up": _slab_sum_unpad([landed_up["up0"], landed_up["up1"]], rd, rp, "rs_slab_sum_up"),
        "ffn_w_down": jnp.stack([g_sum["down0"][:rd], g_sum["down1"][:rd]]),
    }
    delta, new_m, new_v = {}, {}, {}

    def adamw_matrix(n):
        shp = weights[n].shape
        two_d = lambda a: a.reshape(-1, shp[-1])
        dl, mn, vn = _adamw(two_d(weights[n]), two_d(grads[n]), two_d(m_in[n]), two_d(v_in[n]), f"adamw_{n}",
                            after=a_in_sent)
        delta[n], new_m[n], new_v[n] = dl.reshape(shp), mn.reshape(shp), vn.reshape(shp)

    for n in grads:
        adamw_matrix(n)
    (post_all,) = _xchg_wait(post_flight, [new_v[n] for n in grads], False, ALL_PEERS, "gather_small_grads_wait")
    tot = _slab_sum(post_all, "small_grad_sum").reshape(-1)
    nmod = 14 * d
    (t_mod, t_lb, t_ang, t_kng, t_qng, t_bf, t_cw, t_cb, t_loss) = _unpack_small(
        tot, [(nmod,), (1, d), (1, HEAD), (HEAD,), (1, HEAD), (LANES,), (2, CONV_TAPS, two_f), (2, two_f),
              (LANES,)])
    loss = t_loss[0]
    dm_all = post_all.reshape(NDEV, -1)[:, :nmod]
    dm0 = lax.dynamic_slice_in_dim(dm_all[:, :6 * d], me * n0, n0, axis=1)
    dm1 = lax.dynamic_slice_in_dim(dm_all[:, 6 * d:12 * d], me * n0, n0, axis=1)
    dkv = lax.dynamic_slice_in_dim(dm_all[:, 12 * d:], me * nkv, nkv, axis=1)
    g_ada_w, g_kv_ada_w, g_logits = _ada_bwd(c_act, _pad_rows(dm0, 2 * NDEV), _pad_rows(dm1, 2 * NDEV),
                                              _pad_rows(dkv, 2 * NDEV), lb, t_lb)

    grads.update({
        "ada_w": g_ada_w,
        "ada_b": t_mod[:12 * d].reshape(2, 6 * d),
        "a_lb_logits": lax.dynamic_slice_in_dim(g_logits, me * HEAD, HEAD, axis=1),
        "a_norm_g": t_ang,
        "kv_ada_w": g_kv_ada_w,
        "kv_ada_b": t_mod[12 * d:],
        "kv_b_f": t_bf[:nh],
        "k_norm_g": t_kng,
        "q_norm_g": t_qng,
        "ffn_conv_w": lax.dynamic_slice_in_dim(t_cw, me * ncw, ncw, axis=2),
        "ffn_conv_b": t_cb,
    })

    small_adam = [n for n in order if n not in delta and n not in ("ada_w", "kv_ada_w", "a_w_in")]
    packs = [_pack_small([src[n] for n in small_adam]) for src in (weights, grads, m_in, v_in)]
    outs = _adamw(*packs, "adamw_small", tr=packs[0].shape[0])
    shapes = [weights[n].shape for n in small_adam]
    for dst, o in zip((delta, new_m, new_v), outs):
        for n, a in zip(small_adam, _unpack_small(o.reshape(-1), shapes)):
            dst[n] = a
    adamw_matrix("ada_w")
    adamw_matrix("kv_ada_w")
    (landed,) = _xchg_wait(a_in_flight, new_v["kv_ada_w"], True, ALL_PEERS, "scatter_l0a_wait")
    grads["a_w_in"] = _slab_sum(landed, "rs_slab_sum_a_w_in").reshape(a_w_in.shape)
    adamw_matrix("a_w_in")

    return (loss, grad_x.reshape(x.shape), *[grads[n] for n in order], *[delta[n] for n in order],
            *[new_m[n] for n in order], *[new_v[n] for n in order])
```

```python
import functools

import jax
import jax.numpy as jnp
from jax import lax
from jax.experimental import pallas as pl
from jax.experimental.pallas import tpu as pltpu

F32 = jnp.float32
BF16 = jnp.bfloat16

NDEV = 8
NCHIP = 4
HEAD = 128
A_CHUNK = 64
CONV_TAPS = 3
EPS = 1e-6
NEG_INF = -1e30
LANES = 128
VMEM_LIMIT = 48 * 1024 * 1024

ADAM_LR = 0.001
ADAM_B1 = 0.9
ADAM_B2 = 0.999
ADAM_EPS = 1e-08
ADAM_WD = 0.01
ADAM_STEP = 10

_NN = (((1,), (0,)), ((), ()))
_NT = (((1,), (1,)), ((), ()))
_TN = (((0,), (0,)), ((), ()))
_MESH = pl.DeviceIdType.MESH


def _cparams(**kw):
    return pltpu.CompilerParams(vmem_limit_bytes=VMEM_LIMIT, **kw)


def _divisor_tile(n, pref, unit=LANES):
    if n <= pref:
        return n
    best = None
    for t in range(unit, pref + 1, unit):
        if n % t == 0:
            best = t
    assert best is not None, (n, pref)
    return best


def _round_up(n, unit):
    return -(-n // unit) * unit


def _bdot_raw(a, b, dims):
    return lax.dot_general(a.astype(BF16), b.astype(BF16), dims, preferred_element_type=F32)


@jax.custom_vjp
def _dot_nn(a, b):
    return _bdot_raw(a, b, _NN)


@jax.custom_vjp
def _dot_nt(a, b):
    return _bdot_raw(a, b, _NT)


@jax.custom_vjp
def _dot_tn(a, b):
    return _bdot_raw(a, b, _TN)


_dot_nn.defvjp(lambda a, b: (_bdot_raw(a, b, _NN), (a, b)),
               lambda r, g: (_dot_nt(g, r[1]), _dot_tn(r[0], g)))
_dot_nt.defvjp(lambda a, b: (_bdot_raw(a, b, _NT), (a, b)),
               lambda r, g: (_dot_nn(g, r[1]), _dot_tn(g, r[0])))
_dot_tn.defvjp(lambda a, b: (_bdot_raw(a, b, _TN), (a, b)),
               lambda r, g: (_dot_nt(r[1], g), _dot_nn(r[0], g)))


def _f32dot(a, b):
    return lax.dot_general(a, b, _NN, precision=lax.Precision.HIGHEST, preferred_element_type=F32)


def _sigmoid(x):
    return jax.nn.sigmoid(x)


def _silu(x):
    return x * jax.nn.sigmoid(x)


def _rms(x):
    return x * lax.rsqrt(jnp.mean(x * x, axis=-1, keepdims=True) + EPS)


def _modulate(x, sh, sc):
    return _rms(x) * (1.0 + sc) + sh


def _mm_call(a, b, dims, a_spec, b_spec, o_spec, o_shape, grid, acc_tile, name):
    nk = grid[2]

    def body(a_ref, b_ref, o_ref, *acc):
        p = lax.dot_general(a_ref[...].astype(BF16), b_ref[...].astype(BF16), dims,
                            preferred_element_type=F32)
        if nk == 1:
            o_ref[...] = p.astype(o_ref.dtype)
        else:
            kk = pl.program_id(2)

            @pl.when(kk == 0)
            def _():
                acc[0][...] = p

            @pl.when(kk > 0)
            def _():
                acc[0][...] += p

            @pl.when(kk == nk - 1)
            def _():
                o_ref[...] = acc[0][...].astype(o_ref.dtype)

    return pl.pallas_call(
        body, name=name, grid=grid, in_specs=[a_spec, b_spec], out_specs=o_spec, out_shape=o_shape,
        scratch_shapes=[pltpu.VMEM(acc_tile, F32)] if nk > 1 else [],
        compiler_params=_cparams(dimension_semantics=("parallel", "parallel", "arbitrary")),
    )(a, b)


def _mm(a, b, mode, out_dtype, name, tm=1024, tn=1024, tk=2048):
    if mode == "nn":
        (m, k), (k2, n) = a.shape, b.shape
    elif mode == "nt":
        (m, k), (n, k2) = a.shape, b.shape
    else:
        (k, m), (k2, n) = a.shape, b.shape
    assert k == k2, (a.shape, b.shape, mode)
    tm, tn, tk = _divisor_tile(m, tm), _divisor_tile(n, tn), _divisor_tile(k, tk)
    if mode == "tn":
        a_spec = pl.BlockSpec((tk, tm), lambda i, j, kk: (kk, i))
    else:
        a_spec = pl.BlockSpec((tm, tk), lambda i, j, kk: (i, kk))
    if mode == "nt":
        b_spec = pl.BlockSpec((tn, tk), lambda i, j, kk: (j, kk))
    else:
        b_spec = pl.BlockSpec((tk, tn), lambda i, j, kk: (kk, j))
    return _mm_call(a, b, {"nn": _NN, "nt": _NT, "tn": _TN}[mode], a_spec, b_spec,
                    pl.BlockSpec((tm, tn), lambda i, j, kk: (i, j)), jax.ShapeDtypeStruct((m, n), out_dtype),
                    (m // tm, n // tn, k // tk), (tm, tn), name)


def _wblk_act_spec(rows, gb, nl, split, nb, row_axis, blk_axis):
    if split == 1:
        return pl.BlockSpec((rows, gb * nl), lambda *g: (g[row_axis], g[blk_axis]))
    groups = nb // split // gb
    return pl.BlockSpec((None, rows, gb * nl),
                        lambda *g: (g[blk_axis] // groups, g[row_axis], g[blk_axis] % groups))


def _mm_wblk(a, wb, out_dtype, name, *, gb, row_off=0, split=1, tm=1024):
    m, k = a.shape
    nb, _, nl = wb.shape
    assert (nb // split) % gb == 0
    tm = _divisor_tile(m, tm)

    def body(a_ref, b_ref, o_ref):
        av = a_ref[...].astype(BF16)
        for s in range(gb):
            o_ref[:, s * nl:(s + 1) * nl] = lax.dot_general(
                av, b_ref[s].astype(BF16), _NN, preferred_element_type=F32).astype(o_ref.dtype)

    o_shape = (m, nb * nl) if split == 1 else (split, m, nb // split * nl)
    return pl.pallas_call(
        body, name=name, grid=(nb // gb, m // tm),
        in_specs=[pl.BlockSpec((tm, k), lambda j, i: (i, 0)),
                  pl.BlockSpec((gb, k, nl), lambda j, i: (j, row_off, 0))],
        out_specs=_wblk_act_spec(tm, gb, nl, split, nb, 1, 0),
        out_shape=jax.ShapeDtypeStruct(o_shape, out_dtype),
        compiler_params=_cparams(dimension_semantics=("parallel", "parallel")),
    )(a, wb)


def _mm_wblk_dx(dy, wb, out_dtype, name, *, k, gb, row_off=0, split=1, tm=1024):
    nb, _, nl = wb.shape
    m = dy.shape[-2]
    tm = _divisor_tile(m, tm)
    nk = nb // gb
    per = nb // split
    whole = split > 1 and gb == nb
    assert whole or per % gb == 0

    def body(a_ref, b_ref, o_ref, *acc):
        p = None
        for s in range(gb):
            a_blk = a_ref[s // per, :, (s % per) * nl:(s % per + 1) * nl] if whole else a_ref[:, s * nl:(s + 1) * nl]
            q = lax.dot_general(a_blk.astype(BF16), b_ref[s].astype(BF16), _NT, preferred_element_type=F32)
            p = q if p is None else p + q
        if nk == 1:
            o_ref[...] = p.astype(o_ref.dtype)
        else:
            kk = pl.program_id(1)

            @pl.when(kk == 0)
            def _():
                acc[0][...] = p

            @pl.when(kk > 0)
            def _():
                acc[0][...] += p

            @pl.when(kk == nk - 1)
            def _():
                o_ref[...] = acc[0][...].astype(o_ref.dtype)

    return pl.pallas_call(
        body, name=name, grid=(m // tm, nk),
        in_specs=[pl.BlockSpec((split, tm, per * nl), lambda i, kk: (0, i, 0)) if whole
                  else _wblk_act_spec(tm, gb, nl, split, nb, 0, 1),
                  pl.BlockSpec((gb, k, nl), lambda i, kk: (kk, row_off, 0))],
        out_specs=pl.BlockSpec((tm, k), lambda i, kk: (i, 0)),
        out_shape=jax.ShapeDtypeStruct((m, k), out_dtype),
        scratch_shapes=[pltpu.VMEM((tm, k), F32)] if nk > 1 else [],
        compiler_params=_cparams(dimension_semantics=("parallel", "arbitrary")),
    )(dy, wb)


def _mm_wblk_dw(x, dy, name, *, nb, gb, split=1, tk=1024):
    t, k = x.shape
    assert (nb // split) % gb == 0
    nl = dy.shape[-1] * split // nb
    tk = _divisor_tile(t, tk)
    nk = t // tk

    def body(a_ref, b_ref, o_ref, *acc):
        kk = pl.program_id(1)
        av = a_ref[...].astype(BF16)
        for s in range(gb):
            p = lax.dot_general(av, b_ref[:, s * nl:(s + 1) * nl].astype(BF16), _TN, preferred_element_type=F32)
            if nk == 1:
                o_ref[s] = p.astype(o_ref.dtype)
                continue

            @pl.when(kk == 0)
            def _():
                acc[0][s] = p

            @pl.when(kk > 0)
            def _():
                acc[0][s] += p

        if nk > 1:
            @pl.when(kk == nk - 1)
            def _():
                o_ref[...] = acc[0][...].astype(o_ref.dtype)

    return pl.pallas_call(
        body, name=name, grid=(nb // gb, nk),
        in_specs=[pl.BlockSpec((tk, k), lambda j, kk: (kk, 0)), _wblk_act_spec(tk, gb, nl, split, nb, 1, 0)],
        out_specs=pl.BlockSpec((gb, k, nl), lambda j, kk: (j, 0, 0)),
        out_shape=jax.ShapeDtypeStruct((nb, k, nl), BF16),
        scratch_shapes=[pltpu.VMEM((gb, k, nl), F32)] if nk > 1 else [],
        compiler_params=_cparams(dimension_semantics=("parallel", "arbitrary")),
    )(x, dy)


def _row_specs(rows, tb, nsub):
    return [pl.BlockSpec((tb, nsub * cw), functools.partial(lambda i, off: (i, off), off=off))
            for (_, cw, off) in rows]


def _vec_specs(params):
    return [pl.BlockSpec(p.shape, lambda i: (0, 0)) for p in params]


def _row_fwd(f, rows, params, out_dtypes, *, nsub=1, tb, name):
    t = rows[0][0].shape[0]
    tb = min(tb, t)
    n_r, n_p = len(rows), len(params)
    blk = [jax.ShapeDtypeStruct((tb, cw), F32) for (_, cw, _) in rows]
    blk += [jax.ShapeDtypeStruct(p.shape, F32) for p in params]
    out_avals = jax.eval_shape(f, *blk)

    def body(*refs):
        pv = [r[...] for r in refs[n_r:n_r + n_p]]
        for s in range(nsub):
            vals = [r[:, s * cw:(s + 1) * cw].astype(F32) for r, (_, cw, _) in zip(refs[:n_r], rows)]
            outs = f(*vals, *pv)
            for o_ref, o in zip(refs[n_r + n_p:], outs):
                w = o.shape[1]
                o_ref[:, s * w:(s + 1) * w] = o.astype(o_ref.dtype)

    return pl.pallas_call(
        body, name=name,
        grid=(t // tb,),
        in_specs=_row_specs(rows, tb, nsub) + _vec_specs(params),
        out_specs=[pl.BlockSpec((tb, nsub * av.shape[1]), lambda i: (i, 0)) for av in out_avals],
        out_shape=[jax.ShapeDtypeStruct((t, nsub * av.shape[1]), dt) for av, dt in zip(out_avals, out_dtypes)],
        compiler_params=_cparams(dimension_semantics=("parallel",)),
    )(*[r[0] for r in rows], *params)


def _row_bwd(f, rows, params, cots, row_grad_dtypes, *, nsub=1, tb, name, add_to=None, cot_add=None):
    t = rows[0][0].shape[0]
    tb = min(tb, t)
    n_r, n_p, n_c = len(rows), len(params), len(cots)
    want = [j for j in range(n_r) if row_grad_dtypes[j] is not None]
    cot_add = cot_add or []
    extra = [] if add_to is None else [(add_to[1], rows[add_to[0]][1], 0)]
    n_add_to = len(extra)
    extra += [(arr, cots[ci][1], 0) for ci, arr in cot_add]

    def body(*refs):
        i = pl.program_id(0)
        r_in, p_in = refs[:n_r], refs[n_r:n_r + n_p]
        c_in = refs[n_r + n_p:n_r + n_p + n_c]
        e_in = refs[n_r + n_p + n_c:n_r + n_p + n_c + len(extra)]
        outs = refs[n_r + n_p + n_c + len(extra):]
        pv = [r[...] for r in p_in]
        psum = [None] * n_p
        for s in range(nsub):
            vals = [r[:, s * cw:(s + 1) * cw].astype(F32) for r, (_, cw, _) in zip(r_in, rows)]
            cvals = [r[:, s * cw:(s + 1) * cw].astype(F32) for r, (_, cw, _) in zip(c_in, cots)]
            for (ci, _), e_ref in zip(cot_add, e_in[n_add_to:]):
                cw = cots[ci][1]
                cvals[ci] = cvals[ci] + e_ref[:, s * cw:(s + 1) * cw].astype(F32)
            _, vjp_fn = jax.vjp(f, *vals, *pv)
            grads = vjp_fn(tuple(cvals))
            for o_ref, jr in zip(outs[:len(want)], want):
                cw = rows[jr][1]
                gr = grads[jr]
                if add_to is not None and jr == add_to[0]:
                    gr = gr + e_in[0][:, s * cw:(s + 1) * cw]
                o_ref[:, s * cw:(s + 1) * cw] = gr.astype(o_ref.dtype)
            for jp in range(n_p):
                psum[jp] = grads[n_r + jp] if psum[jp] is None else psum[jp] + grads[n_r + jp]
        for o_ref, g in zip(outs[len(want):], psum):
            @pl.when(i == 0)
            def _():
                o_ref[...] = g

            @pl.when(i > 0)
            def _():
                o_ref[...] += g

    out_specs = [pl.BlockSpec((tb, nsub * rows[jr][1]), lambda i: (i, 0)) for jr in want]
    out_shape = [jax.ShapeDtypeStruct((t, nsub * rows[jr][1]), row_grad_dtypes[jr]) for jr in want]
    out_specs += _vec_specs(params)
    out_shape += [jax.ShapeDtypeStruct(p.shape, F32) for p in params]
    res = pl.pallas_call(
        body, name=name,
        grid=(t // tb,),
        in_specs=_row_specs(rows, tb, nsub) + _vec_specs(params) + _row_specs(cots, tb, nsub)
        + _row_specs(extra, tb, nsub),
        out_specs=out_specs, out_shape=out_shape,
        compiler_params=_cparams(dimension_semantics=("arbitrary",)),
    )(*[r[0] for r in rows], *params, *[c[0] for c in cots], *[e[0] for e in extra])
    return res[:len(want)], res[len(want):]


def _f_mod(x, sh, sc):
    return (_modulate(x, sh, sc),)


def _f_res_mod(x, y, g, sh, sc):
    x1 = x + g * y
    return x1, _modulate(x1, sh, sc)


def _f_res_mod2(x, y, g, sh_a, sc_a, sh_b, sc_b):
    x1 = x + g * y
    return x1, _modulate(x1, sh_a, sc_a), _modulate(x1, sh_b, sc_b)


def _f_qnorm(p, g):
    return (_rms(p) * g * (HEAD ** -0.5),)


def _f_knorm(p, g):
    return (_rms(p) * g,)


def _f_qnorm_aug(p, g):
    lane = lax.broadcasted_iota(jnp.int32, p.shape, 1)
    return (jnp.concatenate([_rms(p) * g * (HEAD ** -0.5), jnp.where(lane < 3, 1.0, 0.0)], axis=1),)


def _f_knorm_aug(p, c0, c1, c2, g):
    lane = lax.broadcasted_iota(jnp.int32, p.shape, 1)
    aug = jnp.where(lane == 0, c0, jnp.where(lane == 1, c1, jnp.where(lane == 2, c2, 0.0)))
    return (jnp.concatenate([_rms(p) * g, aug], axis=1),)


def _split3(a):
    round_bf16 = lambda v: lax.reduce_precision(v, exponent_bits=8, mantissa_bits=7)
    hi = round_bf16(a)
    mid = round_bf16(a - hi)
    lo = round_bf16(a - hi - mid)
    return hi.astype(BF16), mid.astype(BF16), lo.astype(BF16)


def _f_outgate(o, og):
    return (o * _sigmoid(og),)


def _loss_call(x3, f, g2, target, tb):
    t, d = x3.shape
    tb = min(tb, t)

    def body(x_ref, f_ref, g_ref, t_ref, loss_ref, dx_ref, df_ref, dg_ref):
        i = pl.program_id(0)
        fv = f_ref[...]
        g = g_ref[...]
        e = x_ref[...] + g * fv - t_ref[...]
        dx = e * (1.0 / d)
        part = 0.5 * jnp.sum(jnp.sum(e * dx, axis=1, keepdims=True), axis=0, keepdims=True)
        dx_ref[...] = dx
        df_ref[...] = (g * dx).astype(df_ref.dtype)
        dg = jnp.sum(dx * fv, axis=0, keepdims=True)

        @pl.when(i == 0)
        def _():
            loss_ref[...] = jnp.broadcast_to(part, loss_ref.shape)
            dg_ref[...] = dg

        @pl.when(i > 0)
        def _():
            loss_ref[...] += jnp.broadcast_to(part, loss_ref.shape)
            dg_ref[...] += dg

    row = pl.BlockSpec((tb, d), lambda i: (i, 0))
    vec = pl.BlockSpec((1, d), lambda i: (0, 0))
    return pl.pallas_call(
        body, name="loss_head",
        grid=(t // tb,),
        in_specs=[row, row, vec, row],
        out_specs=[pl.BlockSpec((1, LANES), lambda i: (0, 0)), row, row, vec],
        out_shape=[jax.ShapeDtypeStruct((1, LANES), F32), jax.ShapeDtypeStruct((t, d), F32),
                   jax.ShapeDtypeStruct((t, d), BF16), jax.ShapeDtypeStruct((1, d), F32)],
        compiler_params=_cparams(dimension_semantics=("arbitrary",)),
    )(x3, f, g2, target)


def _hg_mask(tb):
    br = lax.broadcasted_iota(jnp.int32, (tb, tb), 0)
    bs = lax.broadcasted_iota(jnp.int32, (tb, tb), 1)
    return jnp.logical_and(br // A_CHUNK == bs // A_CHUNK, bs <= br).astype(F32)


def _hg_consts(mask):
    c = A_CHUNK
    r = lax.broadcasted_iota(jnp.int32, (c, c), 0)
    s = lax.broadcasted_iota(jnp.int32, (c, c), 1)
    return (s <= r).astype(F32), (r <= s).astype(F32), mask > 0.5


def _chunk_apply(mat, x):
    c = mat.shape[0]
    return jnp.concatenate([_f32dot(mat, x[i * c:(i + 1) * c]) for i in range(x.shape[0] // c)], axis=0)


@jax.custom_vjp
def _chunk_cumsum(x, tri, tri_t):
    return _chunk_apply(tri, x)


_chunk_cumsum.defvjp(lambda x, tri, tri_t: (_chunk_apply(tri, x), (tri, tri_t)),
                     lambda r, g: (_chunk_apply(r[1], g), jnp.zeros_like(r[0]), jnp.zeros_like(r[1])))


def _per_chunk(a, b, dims):
    return jnp.stack([_bdot_raw(a[i], b[i], dims) for i in range(a.shape[0])])


@jax.custom_vjp
def _chunk_tn(a, b):
    return _per_chunk(a, b, _TN)


@jax.custom_vjp
def _chunk_nt(a, b):
    return _per_chunk(a, b, _NT)


@jax.custom_vjp
def _chunk_nn(a, b):
    return _per_chunk(a, b, _NN)


_chunk_tn.defvjp(lambda a, b: (_per_chunk(a, b, _TN), (a, b)),
                 lambda r, g: (_chunk_nt(r[1], g), _chunk_nn(r[0], g)))
_chunk_nt.defvjp(lambda a, b: (_per_chunk(a, b, _NT), (a, b)),
                 lambda r, g: (_chunk_nn(g, r[1]), _chunk_tn(g, r[0])))
_chunk_nn.defvjp(lambda a, b: (_per_chunk(a, b, _NN), (a, b)),
                 lambda r, g: (_chunk_nt(g, r[1]), _chunk_tn(r[0], g)))


def _scan_states(decay, m, st):
    sts = []
    for i in range(m.shape[0]):
        sts.append(st)
        st = st * decay[i] + m[i]
    return jnp.stack(sts), st


@jax.custom_vjp
def _state_scan(decay, m, st):
    return _scan_states(decay, m, st)


def _state_scan_fwd(decay, m, st):
    sts, st_out = _scan_states(decay, m, st)
    return (sts, st_out), (decay, sts)


def _state_scan_bwd(res, cts):
    decay, sts = res
    d_sts, g = cts
    d_decay, d_m = [], []
    for i in range(sts.shape[0] - 1, -1, -1):
        d_m.append(g)
        d_decay.append(jnp.sum(g * sts[i], axis=0, keepdims=True))
        g = g * decay[i] + d_sts[i]
    return jnp.stack(d_decay[::-1]), jnp.stack(d_m[::-1]), g


_state_scan.defvjp(_state_scan_fwd, _state_scan_bwd)


def _hg_block(qp, fp, ip, gp, lb, ng, st, tri, tri_t, bd_causal):
    tb = qp.shape[0]
    c = A_CHUNK
    n = tb // c
    q = _silu(qp)
    fg = lb + (1.0 - lb) * _sigmoid(fp)
    logf = jnp.log(fg)
    k = 1.0 - fg
    b3 = _chunk_cumsum(logf, tri, tri_t).reshape(n, c, HEAD)
    pos = lax.broadcasted_iota(jnp.int32, (1, c, 1), 1)
    b_mid = lax.stop_gradient(jnp.sum(jnp.where(pos == c // 2, b3, 0.0), axis=1, keepdims=True))
    b_last = jnp.sum(jnp.where(pos == c - 1, b3, 0.0), axis=1, keepdims=True)
    q3, k3, v3 = q.reshape(n, c, HEAD), k.reshape(n, c, HEAD), ip.reshape(n, c, HEAD)
    scores = _dot_nt((q3 * jnp.exp(b3 - b_mid)).reshape(tb, HEAD), (k3 * jnp.exp(b_mid - b3)).reshape(tb, HEAD))
    o_intra = _dot_nn(jnp.where(bd_causal, scores, 0.0), ip)
    states, st_new = _state_scan(jnp.exp(b_last), _chunk_tn(v3, k3 * jnp.exp(b_last - b3)), st)
    o = o_intra + _chunk_nt(q3 * jnp.exp(b3), states).reshape(tb, HEAD)
    y = _rms(o) * ng * _silu(gp)
    return y, st_new


HG_HEADS = 2


def _hg_specs(tb, nh, rev_nb=None):
    wide = HG_HEADS * HEAD
    per = nh // HG_HEADS

    def row(part):
        if rev_nb is None:
            return pl.BlockSpec((tb, wide), functools.partial(lambda h, i, off: (i, off + h), off=part * per))
        return pl.BlockSpec((tb, wide),
                            functools.partial(lambda h, i, off: (rev_nb - 1 - i, off + h), off=part * per))
    return [row(0), row(1), row(2), row(3),
            pl.BlockSpec((1, wide), lambda h, i: (0, h)), pl.BlockSpec((1, HEAD), lambda h, i: (0, 0)),
            pl.BlockSpec((tb, tb), lambda h, i: (0, 0))]


def _hgrn2_fwd(proj, lb, ng, tb):
    t = proj.shape[0]
    nh = proj.shape[1] // (4 * HEAD)
    tb = min(tb, t)
    nb = t // tb
    wide = HG_HEADS * HEAD

    def body(q_ref, f_ref, i_ref, g_ref, lb_ref, ng_ref, mask_ref, y_ref, s_ref, st_ref):
        i = pl.program_id(1)

        @pl.when(i == 0)
        def _():
            st_ref[...] = jnp.zeros_like(st_ref)

        consts = _hg_consts(mask_ref[...])
        for p in range(HG_HEADS):
            cs = slice(p * HEAD, (p + 1) * HEAD)
            st = st_ref[p]
            s_ref[p, 0] = st
            y, st_new = _hg_block(q_ref[:, cs], f_ref[:, cs], i_ref[:, cs], g_ref[:, cs], lb_ref[:, cs],
                                  ng_ref[...], st, *consts)
            y_ref[:, cs] = y.astype(y_ref.dtype)
            st_ref[p] = st_new

    return pl.pallas_call(
        body, name="hgrn2_fwd",
        grid=(nh // HG_HEADS, nb),
        in_specs=_hg_specs(tb, nh),
        out_specs=[pl.BlockSpec((tb, wide), lambda h, i: (i, h)),
                   pl.BlockSpec((HG_HEADS, 1, HEAD, HEAD), lambda h, i: (h, i, 0, 0))],
        out_shape=[jax.ShapeDtypeStruct((t, nh * HEAD), BF16),
                   jax.ShapeDtypeStruct((nh, nb, HEAD, HEAD), F32)],
        scratch_shapes=[pltpu.VMEM((HG_HEADS, HEAD, HEAD), F32)],
        compiler_params=_cparams(dimension_semantics=("parallel", "arbitrary")),
    )(proj, proj, proj, proj, lb, ng, _hg_mask(tb))


def _hgrn2_bwd(proj, lb, ng, states, dy, tb):
    t = proj.shape[0]
    nh = proj.shape[1] // (4 * HEAD)
    tb = min(tb, t)
    nb = t // tb
    wide = HG_HEADS * HEAD

    def body(q_ref, f_ref, i_ref, g_ref, lb_ref, ng_ref, mask_ref, s_ref, dy_ref,
             dp_ref, dlb_ref, dng_ref, dst_ref):
        h, i = pl.program_id(0), pl.program_id(1)
        consts = _hg_consts(mask_ref[...])

        @pl.when(i == 0)
        def _():
            dst_ref[...] = jnp.zeros_like(dst_ref)
            dlb_ref[...] = jnp.zeros_like(dlb_ref)

        @pl.when(jnp.logical_and(i == 0, h == 0))
        def _():
            dng_ref[...] = jnp.zeros_like(dng_ref)

        def fn(qp, fp, ip, gp, lbx, ngx, stx):
            return _hg_block(qp, fp, ip, gp, lbx, ngx, stx, *consts)

        for p in range(HG_HEADS):
            cs = slice(p * HEAD, (p + 1) * HEAD)
            _, vjp_fn = jax.vjp(fn, q_ref[:, cs], f_ref[:, cs], i_ref[:, cs], g_ref[:, cs], lb_ref[:, cs],
                                ng_ref[...], s_ref[p, 0])
            *gparts, glb, gng, dst = vjp_fn((dy_ref[:, cs].astype(F32), dst_ref[p]))
            for part, gpart in enumerate(gparts):
                dp_ref[part, :, cs] = gpart.astype(dp_ref.dtype)
            dst_ref[p] = dst
            dlb_ref[:, cs] += glb
            dng_ref[...] += gng

    rev = lambda h, i: (nb - 1 - i, h)
    return pl.pallas_call(
        body, name="hgrn2_bwd",
        grid=(nh // HG_HEADS, nb),
        in_specs=_hg_specs(tb, nh, rev_nb=nb) + [
            pl.BlockSpec((HG_HEADS, 1, HEAD, HEAD), lambda h, i: (h, nb - 1 - i, 0, 0)),
            pl.BlockSpec((tb, wide), rev)],
        out_specs=[pl.BlockSpec((4, tb, wide), lambda h, i: (0, nb - 1 - i, h)),
                   pl.BlockSpec((1, wide), lambda h, i: (0, h)), pl.BlockSpec((1, HEAD), lambda h, i: (0, 0))],
        out_shape=[jax.ShapeDtypeStruct((4, t, nh * HEAD), BF16),
                   jax.ShapeDtypeStruct((1, nh * HEAD), F32), jax.ShapeDtypeStruct((1, HEAD), F32)],
        scratch_shapes=[pltpu.VMEM((HG_HEADS, HEAD, HEAD), F32)],
        compiler_params=_cparams(dimension_semantics=("arbitrary", "arbitrary")),
    )(proj, proj, proj, proj, lb, ng, _hg_mask(tb), states, dy)


def _fgate_consts(cb):
    r = lax.broadcasted_iota(jnp.int32, (cb, cb), 0)
    s = lax.broadcasted_iota(jnp.int32, (cb, cb), 1)
    return (r <= s).astype(F32), (r >= s).astype(F32)


def _fgate_fwd(xt, bias, cb=512):
    nh, t = xt.shape
    cb = min(cb, t)

    def body(x_ref, b_ref, o_ref):
        upper, _ = _fgate_consts(cb)
        carry = jnp.zeros((nh, 1), F32)
        for blk in range(t // cb):
            z = x_ref[:, blk * cb:(blk + 1) * cb] + b_ref[...]
            logf = jnp.minimum(z, 0.0) - jnp.log(1.0 + jnp.exp(-jnp.abs(z)))
            cs = _f32dot(logf, upper) + carry
            o_ref[:, blk * cb:(blk + 1) * cb] = cs
            carry = cs[:, cb - 1:cb]

    vm = pl.BlockSpec(memory_space=pltpu.VMEM)
    return pl.pallas_call(
        body, name="fgate_fwd", in_specs=[vm, vm], out_specs=vm,
        out_shape=jax.ShapeDtypeStruct((nh, t), F32), compiler_params=_cparams(),
    )(xt, bias)


def _fgate_bwd(xt, bias, dft, cb=512):
    nh, t = xt.shape
    cb = min(cb, t)
    nblk = t // cb

    def body(x_ref, b_ref, d_ref, dx_ref, db_ref):
        _, lower = _fgate_consts(cb)
        carry = jnp.zeros((nh, 1), F32)
        db = jnp.zeros((nh, 1), F32)
        for blk in range(nblk - 1, -1, -1):
            sl = slice(blk * cb, (blk + 1) * cb)
            dlogf = _f32dot(d_ref[:, sl], lower) + carry
            carry = dlogf[:, 0:1]
            z = x_ref[:, sl] + b_ref[...]
            dz = dlogf * (1.0 - _sigmoid(z))
            dx_ref[:, sl] = dz
            db = db + jnp.sum(dz, axis=1, keepdims=True)
        db_ref[...] = db

    vm = pl.BlockSpec(memory_space=pltpu.VMEM)
    return pl.pallas_call(
        body, name="fgate_bwd", in_specs=[vm, vm, vm], out_specs=[vm, vm],
        out_shape=[jax.ShapeDtypeStruct((nh, t), F32), jax.ShapeDtypeStruct((nh, 1), F32)],
        compiler_params=_cparams(),
    )(xt, bias, dft)


ATTN_GROUPS = 4
ATTN_FWD_HEADS = 2


def _attn_fwd(q, k, v, f_grp, blk):
    t, width = v.shape
    nh = width // HEAD
    nq = t // blk
    hpg = nh // ATTN_GROUPS

    def body(q_ref, k_ref, v_ref, fc_ref, o_ref, lse_ref):
        i = pl.program_id(0)
        tri = (lax.broadcasted_iota(jnp.int32, (blk, blk), 1) <= lax.broadcasted_iota(jnp.int32, (blk, blk), 0))
        for h0 in range(0, nh, ATTN_FWD_HEADS):
            heads = range(h0, min(h0 + ATTN_FWD_HEADS, nh))

            def tile(j, carries, masked):
                rs = pl.ds(pl.multiple_of(j * blk, blk), blk)
                out = []
                for h, (m, l, acc) in zip(heads, carries):
                    cs = slice(h * HEAD, (h + 1) * HEAD)
                    cs2 = slice(2 * h * HEAD, 2 * (h + 1) * HEAD)
                    s = _bdot_raw(q_ref[:, cs2], k_ref[rs, cs2], _NT)
                    if masked:
                        s = jnp.where(tri, s, NEG_INF)
                    m_new = jnp.maximum(m, jnp.max(s, axis=1, keepdims=True))
                    p = jnp.exp(s - m_new)
                    alpha = jnp.exp(m - m_new)
                    l_new = alpha * l + jnp.sum(p, axis=1, keepdims=True)
                    out.append((m_new, l_new, alpha * acc + _bdot_raw(p, v_ref[rs, cs], _NN)))
                return tuple(out)

            init = tuple((jnp.full((blk, 1), NEG_INF, F32), jnp.zeros((blk, 1), F32), jnp.zeros((blk, HEAD), F32))
                         for _ in heads)
            carries = lax.fori_loop(0, i, lambda j, c: tile(j, c, False), init)
            for h, (m, l, acc) in zip(heads, tile(i, carries, True)):
                o_ref[:, h * HEAD:(h + 1) * HEAD] = acc / l
                g, hh = divmod(h, hpg)
                lse_ref[g, :, hh:hh + 1] = m + jnp.log(l) + fc_ref[g, :, hh:hh + 1]

    vm = pl.BlockSpec(memory_space=pltpu.VMEM)
    stat = pl.BlockSpec((ATTN_GROUPS, blk, hpg), lambda i: (0, i, 0))
    return pl.pallas_call(
        body, name="fox_attn_fwd",
        grid=(nq,),
        in_specs=[pl.BlockSpec((blk, 2 * width), lambda i: (i, 0)), vm, vm, stat],
        out_specs=[pl.BlockSpec((blk, width), lambda i: (i, 0)), stat],
        out_shape=[jax.ShapeDtypeStruct((t, width), F32), jax.ShapeDtypeStruct((ATTN_GROUPS, t, hpg), F32)],
        compiler_params=_cparams(dimension_semantics=("parallel",)),
    )(q, k, v, f_grp)


def _outgate_bwd(o, proj_q, dz, tb):
    t, width = o.shape
    nh = width // HEAD
    hpg = nh // ATTN_GROUPS
    tb = min(tb, t)

    def body(o_ref, og_ref, dz_ref, do_ref, dog_ref, dl_ref):
        for h in range(nh):
            cs = slice(h * HEAD, (h + 1) * HEAD)
            ov = o_ref[:, cs]
            _, vjp_fn = jax.vjp(_f_outgate, ov, og_ref[:, cs])
            do, dog = vjp_fn((dz_ref[:, cs].astype(F32),))
            do = do.astype(do_ref.dtype)
            do_ref[:, cs] = do
            dog_ref[:, cs] = dog.astype(dog_ref.dtype)
            g, hh = divmod(h, hpg)
            dl_ref[g, :, hh:hh + 1] = jnp.sum(do.astype(F32) * ov, axis=1, keepdims=True)

    wide = pl.BlockSpec((tb, width), lambda i: (i, 0))
    return pl.pallas_call(body, name="out_gate_bwd", grid=(t // tb,),
                          in_specs=[wide, pl.BlockSpec((tb, width), lambda i: (i, 1)), wide],
                          out_specs=[wide, wide, pl.BlockSpec((ATTN_GROUPS, tb, hpg), lambda i: (0, i, 0))],
                          out_shape=[jax.ShapeDtypeStruct((t, width), BF16), jax.ShapeDtypeStruct((t, width), BF16),
                                     jax.ShapeDtypeStruct((ATTN_GROUPS, t, hpg), F32)],
                          compiler_params=_cparams(dimension_semantics=("parallel",)))(o, proj_q, dz)


def _qnorm_bwd(proj_q, gain, dq_n, dog, tb):
    t, width = dq_n.shape
    nh = width // HEAD
    tb = min(tb, t)

    def body(p_ref, g_ref, dq_ref, dog_ref, out_ref, dg_ref):
        i = pl.program_id(0)
        gv = g_ref[...]
        acc = None
        for h in range(nh):
            cs = slice(h * HEAD, (h + 1) * HEAD)
            _, vjp_fn = jax.vjp(_f_qnorm, p_ref[:, cs], gv)
            dp, dg = vjp_fn((dq_ref[:, cs],))
            out_ref[:, cs] = dp.astype(out_ref.dtype)
            acc = dg if acc is None else acc + dg
        out_ref[:, width:] = dog_ref[...]

        @pl.when(i == 0)
        def _():
            dg_ref[...] = acc

        @pl.when(i > 0)
        def _():
            dg_ref[...] += acc

    wide = pl.BlockSpec((tb, width), lambda i: (i, 0))
    vec = pl.BlockSpec(gain.shape, lambda i: (0, 0))
    return pl.pallas_call(body, name="q_norm_bwd", grid=(t // tb,), in_specs=[wide, vec, wide, wide],
                          out_specs=[pl.BlockSpec((tb, 2 * width), lambda i: (i, 0)), vec],
                          out_shape=[jax.ShapeDtypeStruct((t, 2 * width), BF16), jax.ShapeDtypeStruct(gain.shape, F32)],
                          compiler_params=_cparams(dimension_semantics=("arbitrary",)))(proj_q, gain, dq_n, dog)


def _attn_bwd(q, k, v, f_grp, do, lse, delta, blk):
    t, width = v.shape
    nh = width // HEAD
    nq = t // blk
    hpg = nh // ATTN_GROUPS
    gw = hpg * HEAD

    def body(q_ref, do_ref, k_ref, v_ref, fc_ref, lse_ref, dl_ref,
             dq_ref, dk_ref, dv_ref, dfc_ref, dfr_ref):
        g, j = pl.program_id(0), pl.program_id(1)
        tri = (lax.broadcasted_iota(jnp.int32, (blk, blk), 1) <= lax.broadcasted_iota(jnp.int32, (blk, blk), 0))

        @pl.when(j == 0)
        def _():
            dq_ref[...] = jnp.zeros_like(dq_ref)
            dfc_ref[...] = jnp.zeros_like(dfc_ref)

        def tile(i, carries, masked):
            rs = pl.ds(pl.multiple_of(i * blk, blk), blk)
            out = []
            for h, (dk, dv, dfs) in enumerate(carries):
                cs = slice(h * HEAD, (h + 1) * HEAD)
                cs2 = slice(2 * h * HEAD, 2 * (h + 1) * HEAD)
                csq = slice(2 * h * HEAD, (2 * h + 1) * HEAD)
                qi = q_ref[rs, csq]
                doi = do_ref[rs, cs]
                bias = fc_ref[0, rs, h:h + 1] - lse_ref[0, rs, h:h + 1]
                p = jnp.exp(_bdot_raw(q_ref[rs, cs2], k_ref[:, cs2], _NT) + bias)
                if masked:
                    p = jnp.where(tri, p, 0.0)
                ds = p * (_bdot_raw(doi, v_ref[:, cs], _NT) - dl_ref[0, rs, h:h + 1])
                dsb = ds.astype(BF16)
                dq_ref[rs, cs] += _bdot_raw(dsb, k_ref[:, csq], _NN)
                dfc_ref[0, rs, h:h + 1] += jnp.sum(ds, axis=1, keepdims=True)
                out.append((dk + _bdot_raw(dsb, qi, _TN), dv + _bdot_raw(p, doi, _TN),
                            dfs - jnp.sum(ds, axis=0, keepdims=True)))
            return tuple(out)

        init = tuple((jnp.zeros((blk, HEAD), F32), jnp.zeros((blk, HEAD), F32), jnp.zeros((1, blk), F32))
                     for _ in range(hpg))
        carries = lax.fori_loop(j + 1, nq, lambda i, c: tile(i, c, False), tile(j, init, True))
        for h, (dk, dv, dfs) in enumerate(carries):
            cs = slice(h * HEAD, (h + 1) * HEAD)
            dk_ref[:, cs] = dk
            dv_ref[:, cs] = dv.astype(dv_ref.dtype)
            dfr_ref[0, 0, h:h + 1, :] = dfs

    once = pl.Buffered(1)
    stat = pl.BlockSpec((1, t, hpg), lambda g, j: (g, 0, 0), pipeline_mode=once)
    kv_blk = pl.BlockSpec((blk, gw), lambda g, j: (j, g))
    frow = pl.BlockSpec((1, 1, hpg, blk), lambda g, j: (g, j, 0, 0))
    dq, dk, dv, dfc, dfr = pl.pallas_call(
        body, name="fox_attn_bwd",
        grid=(ATTN_GROUPS, nq),
        in_specs=[pl.BlockSpec((t, 2 * gw), lambda g, j: (0, g), pipeline_mode=once),
                  pl.BlockSpec((t, gw), lambda g, j: (0, g), pipeline_mode=once),
                  pl.BlockSpec((blk, 2 * gw), lambda g, j: (j, g)), kv_blk, stat, stat, stat],
        out_specs=[pl.BlockSpec((t, gw), lambda g, j: (0, g)), kv_blk, kv_blk,
                   pl.BlockSpec((1, t, hpg), lambda g, j: (g, 0, 0)), frow],
        out_shape=[jax.ShapeDtypeStruct((t, width), F32), jax.ShapeDtypeStruct((t, width), F32),
                   jax.ShapeDtypeStruct((t, width), BF16), jax.ShapeDtypeStruct((ATTN_GROUPS, t, hpg), F32),
                   jax.ShapeDtypeStruct((ATTN_GROUPS, nq, hpg, blk), F32)],
        compiler_params=_cparams(dimension_semantics=("parallel", "arbitrary")),
    )(q, do, k, v, f_grp, lse, delta)
    return dq, dk, dv, dfc, dfr


SUBLANES = 8


def _shift_down(u, n):
    r = pltpu.roll(u, n, 0)
    row = lax.broadcasted_iota(jnp.int32, (SUBLANES, u.shape[1]), 0)
    return jnp.concatenate([jnp.where(row < n, 0.0, r[:SUBLANES]), r[SUBLANES:]], axis=0)


def _shift_up(u, n):
    t = u.shape[0]
    r = pltpu.roll(u, t - n, 0)
    row = lax.broadcasted_iota(jnp.int32, (SUBLANES, u.shape[1]), 0)
    return jnp.concatenate([r[:t - SUBLANES], jnp.where(row >= SUBLANES - n, 0.0, r[t - SUBLANES:])], axis=0)


CONV_FWD_COLS = 2 * LANES
CONV_BWD_COLS = LANES


def _convglu_specs(t, cols):
    return [pl.BlockSpec((2, t, cols), lambda j: (0, 0, j)),
            pl.BlockSpec((2, CONV_TAPS, cols), lambda j: (0, 0, j)),
            pl.BlockSpec((2, 1, cols), lambda j: (0, 0, j))]


def _lane_blocks(cols):
    return [slice(k * LANES, (k + 1) * LANES) for k in range(cols // LANES)]


def _convglu_fwd(u, cw, cb):
    _, t, fp = u.shape

    def body(u_ref, w_ref, b_ref, a_ref, c_ref):
        for cs in _lane_blocks(CONV_FWD_COLS):
            c = []
            for hf in range(2):
                uv, w = u_ref[hf, :, cs].astype(F32), w_ref[hf, :, cs]
                c.append(w[0:1] * _shift_down(uv, 2) + w[1:2] * _shift_down(uv, 1) + w[2:3] * uv + b_ref[hf, :, cs])
                c_ref[hf, :, cs] = c[hf].astype(c_ref.dtype)
            a_ref[:, cs] = (_silu(c[0]) * c[1]).astype(a_ref.dtype)

    return pl.pallas_call(
        body, name="convglu_fwd",
        grid=(fp // CONV_FWD_COLS,),
        in_specs=_convglu_specs(t, CONV_FWD_COLS),
        out_specs=[pl.BlockSpec((t, CONV_FWD_COLS), lambda j: (0, j)),
                   pl.BlockSpec((2, t, CONV_FWD_COLS), lambda j: (0, 0, j))],
        out_shape=[jax.ShapeDtypeStruct((t, fp), BF16), jax.ShapeDtypeStruct((2, t, fp), BF16)],
        compiler_params=_cparams(dimension_semantics=("parallel",)),
    )(u, cw, cb)


def _convglu_bwd(u, c, cw, da):
    _, t, fp = u.shape

    def body(u_ref, c_ref, w_ref, da_ref, du_ref, dw_ref, db_ref):
        for cs in _lane_blocks(CONV_BWD_COLS):
            gc, vc = c_ref[0, :, cs].astype(F32), c_ref[1, :, cs].astype(F32)
            sg = _sigmoid(gc)
            dav = da_ref[:, cs].astype(F32)
            dcs = [dav * vc * (sg * (1.0 + gc * (1.0 - sg))), dav * (gc * sg)]
            for hf in range(2):
                dc, w, uv = dcs[hf], w_ref[hf, :, cs], u_ref[hf, :, cs].astype(F32)
                dc1, dc2 = _shift_up(dc, 1), _shift_up(dc, 2)
                du_ref[hf, :, cs] = (w[2:3] * dc + w[1:2] * dc1 + w[0:1] * dc2).astype(du_ref.dtype)
                dw_ref[hf, 0:1, cs] = jnp.sum(dc2 * uv, axis=0, keepdims=True)
                dw_ref[hf, 1:2, cs] = jnp.sum(dc1 * uv, axis=0, keepdims=True)
                dw_ref[hf, 2:3, cs] = jnp.sum(dc * uv, axis=0, keepdims=True)
                db_ref[hf, :, cs] = jnp.sum(dc, axis=0, keepdims=True)

    pair, taps, bias = _convglu_specs(t, CONV_BWD_COLS)
    return pl.pallas_call(
        body, name="convglu_bwd",
        grid=(fp // CONV_BWD_COLS,),
        in_specs=[pair, pair, taps, pl.BlockSpec((t, CONV_BWD_COLS), lambda j: (0, j))],
        out_specs=[pair, taps, bias],
        out_shape=[jax.ShapeDtypeStruct((2, t, fp), BF16), jax.ShapeDtypeStruct((2, CONV_TAPS, fp), F32),
                   jax.ShapeDtypeStruct((2, 1, fp), F32)],
        compiler_params=_cparams(dimension_semantics=("parallel",)),
    )(u, c, cw, da)


def _local_step(x, target, mods, lb, small, pre_w, get_w, put_g, *, tb=512, attn_blk=512):
    t, d = x.shape
    nh = d // HEAD
    nb = NDEV
    wts = {}
    vec = lambda *names: [mods[n] for n in names]

    def ffn_fwd(h2, l):
        u = _mm_wblk(h2, wts[f"up{l}"], BF16, f"ffn{l}_up", gb=nb // 2, split=2, tm=512)
        a, c = _convglu_fwd(u, small[f"conv_w{l}"], small[f"conv_b{l}"])
        f = _mm(a, wts[f"down{l}"], "nn", F32, f"ffn{l}_down", tk=4096)
        return (u, c), a, f

    def ffn_bwd(df, h2, uc, a, l):
        u, c = uc
        da = _mm(df, wts[f"down{l}"], "nt", BF16, f"ffn{l}_down_dx", tn=1536)
        dwd = _mm(a, df, "tn", BF16, f"ffn{l}_down_dw", tm=768, tk=t)
        du, dcw, dcb = _convglu_bwd(u, c, small[f"conv_w{l}"], da)
        dh2 = _mm_wblk_dx(du, wts[f"up{l}"], BF16, f"ffn{l}_up_dx", k=d, gb=nb // 2, split=2, tm=1024)
        dwu = _mm_wblk_dw(h2, du, f"ffn{l}_up_dw", nb=nb, gb=1, split=2, tk=t)
        return dh2, dwu, dwd, dcw, dcb

    (h_a,) = _row_fwd(_f_mod, [(x, d, 0)], vec("sh1_0", "sc1_0"), [BF16], tb=tb, name="l0_mod1")
    wts.update(get_w("l0a", h_a))
    proj_a = _mm_wblk(h_a, wts["a_in"], F32, "a_in", gb=nb // 2)
    ypre, states = _hgrn2_fwd(proj_a, lb, small["a_norm_g"], tb)
    pre_w("l0b", ypre)
    wts.update(get_w("l0b", ypre))
    y_a = _mm(ypre, wts["a_out"], "nn", F32, "a_out")
    x1, h2_0 = _row_fwd(_f_res_mod, [(x, d, 0), (y_a, d, 0)], vec("g1_0", "sh2_0", "sc2_0"), [F32, BF16],
                        tb=tb, name="l0_res_mod2")
    wts.update(get_w("l0b_ffn", h2_0))
    u0, a0, f0 = ffn_fwd(h2_0, 0)
    x2, h_kv, h_q = _row_fwd(_f_res_mod2, [(x1, d, 0), (f0, d, 0)],
                             [mods["g2_0"] + pre_w("l1", f0)] + vec("kv_sh", "kv_sc", "sh1_1", "sc1_1"),
                             [F32, BF16, BF16], tb=tb, name="l0_res_kvmod_qmod")
    wts.update(get_w("l1", h_kv))
    proj_k = _mm(h_kv, wts["kv_k"], "nt", F32, "k_proj")
    v_b = _mm(h_kv, wts["kv_v"], "nt", BF16, "v_proj")
    proj_f = _mm(h_kv, wts["kv_f"], "nt", F32, "kv_fproj")
    f_logit_t = proj_f[:, :nh].T
    f_bias = small["kv_b_f"].reshape(nh, 1)
    f_t = _fgate_fwd(f_logit_t, f_bias)
    f_grp = f_t.reshape(ATTN_GROUPS, nh // ATTN_GROUPS, t).transpose(0, 2, 1)
    (k_n,) = _row_fwd(_f_knorm_aug, [(proj_k, HEAD, 0)] + [(piece, 1, 0) for piece in _split3(-f_t.T)],
                      [small["k_norm_g"]], [BF16], nsub=nh, tb=tb, name="k_norm")
    proj_q = _mm_wblk(h_q, wts["b_q"], F32, "b_q", gb=nb)
    (q_n,) = _row_fwd(_f_qnorm_aug, [(proj_q, HEAD, 0)], [small["q_norm_g"]], [BF16], nsub=nh, tb=tb,
                      name="q_norm")
    o_att, lse = _attn_fwd(q_n, k_n, v_b, f_grp, attn_blk)
    (z,) = _row_fwd(_f_outgate, [(o_att, HEAD, 0), (proj_q, HEAD, 1)], [], [BF16], nsub=nh, tb=tb, name="out_gate")
    y_b = _mm(z, wts["b_out"], "nn", F32, "b_out")
    x3, h2_1 = _row_fwd(_f_res_mod, [(x2, d, 0), (y_b, d, 0)], vec("g1_1", "sh2_1", "sc2_1"), [F32, BF16],
                        tb=tb, name="l1_res_mod2")
    u1, a1, f1 = ffn_fwd(h2_1, 1)
    loss, dx4, df1, dg2_1 = _loss_call(x3, f1, mods["g2_1"], target, tb)

    g = {}
    dmods = {"g2_1": dg2_1}
    dh2, g["up1"], g["down1"], g["conv_w1"], g["conv_b1"] = ffn_bwd(df1, h2_1, u1, a1, 1)
    (dx2, dy_b), (dmods["g1_1"], dmods["sh2_1"], dmods["sc2_1"]) = _row_bwd(
        _f_res_mod, [(x2, d, 0), (y_b, d, 0)], vec("g1_1", "sh2_1", "sc2_1"),
        [(dx4, d, 0), (dh2, d, 0)], [F32, BF16], tb=tb, name="l1_res_mod2_bwd")
    dz = _mm(dy_b, wts["b_out"], "nt", BF16, "b_out_dx")
    g["b_out"] = _mm(z, dy_b, "tn", BF16, "b_out_dw", tk=t)
    do_att, dog, delta = _outgate_bwd(o_att, proj_q, dz, tb)
    dq_n, dk_n, dv, dfc_q, dfr_k = _attn_bwd(q_n, k_n, v_b, f_grp, do_att, lse, delta, attn_blk)
    dproj_q, g["q_norm_g"] = _qnorm_bwd(proj_q, small["q_norm_g"], dq_n, dog, tb)
    dh_q = _mm_wblk_dx(dproj_q, wts["b_q"], BF16, "b_q_dx", k=d, gb=nb)
    g["b_q"] = _mm_wblk_dw(h_q, dproj_q, "b_q_dw", nb=nb, gb=nb // 4, tk=t)
    (dpk,), (g["k_norm_g"],) = _row_bwd(_f_knorm, [(proj_k, HEAD, 0)], [small["k_norm_g"]],
                                        [(dk_n, HEAD, 0)], [BF16], nsub=nh, tb=tb, name="k_norm_bwd")
    df_t = dfc_q.transpose(0, 2, 1).reshape(nh, t) + dfr_k.transpose(0, 2, 1, 3).reshape(nh, t)
    dflogit_t, g["kv_b_f"] = _fgate_bwd(f_logit_t, f_bias, df_t)
    dproj_f = jnp.pad(dflogit_t.T, ((0, 0), (0, LANES - nh))).astype(BF16)
    dh_kv = _mm(dpk, wts["kv_k"], "nn", BF16, "k_proj_dx")
    dh_kv_v = _mm(dv, wts["kv_v"], "nn", BF16, "v_proj_dx")
    dh_kv_f = _mm(dproj_f, wts["kv_f"], "nn", BF16, "kv_fproj_dx")
    g["kv_k"] = _mm(dpk, h_kv, "tn", BF16, "k_proj_dw", tk=t)
    g["kv_v"] = _mm(dv, h_kv, "tn", BF16, "v_proj_dw", tk=t)
    g["kv_f"] = _mm(dproj_f, h_kv, "tn", F32, "kv_fproj_dw", tk=1024)
    sent = put_g("l1", {n: g.pop(n) for n in ("b_out", "b_q", "kv_k", "kv_v", "kv_f", "up1", "down1")})
    (dx1, df0), (dmods["g2_0"], dmods["kv_sh"], dmods["kv_sc"], dmods["sh1_1"], dmods["sc1_1"]) = _row_bwd(
        _f_res_mod2, [(x1, d, 0), (f0, d, 0)], [mods["g2_0"] + sent] + vec("kv_sh", "kv_sc", "sh1_1", "sc1_1"),
        [(dx2, d, 0), (dh_kv, d, 0), (dh_q, d, 0)], [F32, BF16], tb=tb, name="l0_res_kvmod_qmod_bwd",
        cot_add=[(1, dh_kv_v), (1, dh_kv_f)])
    dh2, g["up0"], g["down0"], g["conv_w0"], g["conv_b0"] = ffn_bwd(df0, h2_0, u0, a0, 0)
    (dx0, dy_a), (dmods["g1_0"], dmods["sh2_0"], dmods["sc2_0"]) = _row_bwd(
        _f_res_mod, [(x, d, 0), (y_a, d, 0)], vec("g1_0", "sh2_0", "sc2_0"),
        [(dx1, d, 0), (dh2, d, 0)], [F32, BF16], tb=tb, name="l0_res_mod2_bwd")
    dypre = _mm(dy_a, wts["a_out"], "nt", BF16, "a_out_dx")
    g["a_out"] = _mm(ypre, dy_a, "tn", BF16, "a_out_dw", tk=t)
    sent = put_g("l0b", {n: g.pop(n) for n in ("a_out", "up0", "down0")})
    dproj_a, dlb, g["a_norm_g"] = _hgrn2_bwd(proj_a, lb + sent, small["a_norm_g"], states, dypre, tb)
    dh_a = _mm_wblk_dx(dproj_a, wts["a_in"], BF16, "a_in_dx", k=d, gb=nb, split=4, tm=512)
    put_g("l0a", {"a_in": _mm_wblk_dw(h_a, dproj_a, "a_in_dw", nb=nb, gb=1, split=4, tk=t)})
    (grad_x,), (dmods["sh1_0"], dmods["sc1_0"]) = _row_bwd(
        _f_mod, [(x, d, 0)], vec("sh1_0", "sc1_0"), [(dh_a, d, 0)], [F32], tb=tb, name="l0_mod1_bwd",
        add_to=(0, dx0))
    return loss, grad_x, dmods, dlb, g


def _position():
    return lax.axis_index("x"), lax.axis_index("y"), lax.axis_index("c")


_XCHG_EFFECT = pltpu.SideEffectType.DATAFLOW_SIDE_EFFECTING
ALL_PEERS = (1, 2, 3, 4, 5, 6, 7)
SAME_CORE = (2, 4, 6)


def _xchg_copies(src_refs, land_refs, send_sems, recv_sems, local_sems, scatter, rels):
    x, y, cc = _position()
    me = 4 * x + 2 * y + cc
    remote, local = [], []
    for a, (src, land) in enumerate(zip(src_refs, land_refs)):
        local.append(pltpu.make_async_copy(src.at[me] if scatter else src, land.at[me], local_sems.at[a]))
        for idx, rel in enumerate(rels):
            px = 1 - x if rel & 4 else x
            py = 1 - y if rel & 2 else y
            pc = 1 - cc if rel & 1 else cc
            k = len(rels) * a + idx
            remote.append(pltpu.make_async_remote_copy(
                src_ref=src.at[4 * px + 2 * py + pc] if scatter else src, dst_ref=land.at[me],
                send_sem=send_sems.at[k], recv_sem=recv_sems.at[k], device_id=(px, py, pc), device_id_type=_MESH))
    return remote, local


def _xchg_start(srcs, scatter, rels, after, name):
    n = len(srcs)
    lands = [lax.empty(s.shape if scatter else (NDEV, *s.shape), s.dtype) for s in srcs]

    def body(*refs):
        remote, local = _xchg_copies(refs[:n], refs[n:2 * n], *refs[2 * n + 1:2 * n + 4], scatter, rels)
        for cp in local + remote:
            cp.start()
        token = refs[-1]
        token[...] = jnp.zeros_like(token)

    hbm = pl.BlockSpec(memory_space=pltpu.HBM)
    sem = pl.BlockSpec(memory_space=pltpu.SEMAPHORE)
    out = pl.pallas_call(
        body, name=name,
        out_shape=(pltpu.SemaphoreType.DMA((len(rels) * n,)), pltpu.SemaphoreType.DMA((len(rels) * n,)),
                   pltpu.SemaphoreType.DMA((n,)),
                   *[pltpu.HBM(a.shape, a.dtype) for a in srcs + lands], jax.ShapeDtypeStruct((8, LANES), F32)),
        in_specs=[hbm] * (2 * n) + [pl.BlockSpec(memory_space=pl.ANY)],
        out_specs=(sem, sem, sem, *[hbm] * (2 * n), pl.BlockSpec(memory_space=pltpu.VMEM)),
        input_output_aliases={i: 3 + i for i in range(2 * n)},
        compiler_params=pltpu.CompilerParams(has_side_effects=_XCHG_EFFECT),
    )(*[pltpu.with_memory_space_constraint(a, pltpu.HBM) for a in srcs + lands], after)
    return out[:-1], out[-1][0, 0]


def _xchg_wait(handles, after, scatter, rels, name):
    n = (len(handles) - 3) // 2

    def body(*refs):
        remote, local = _xchg_copies(refs[:n], refs[n:2 * n], *refs[2 * n:2 * n + 3], scatter, rels)
        for cp in remote:
            cp.wait_send()
            cp.wait_recv()
        for cp in local:
            cp.wait()

    hbm = pl.BlockSpec(memory_space=pltpu.HBM)
    sem = pl.BlockSpec(memory_space=pltpu.SEMAPHORE)
    thru = list(handles[3:])
    afters = list(after) if isinstance(after, (list, tuple)) else [after]
    out = pl.pallas_call(
        body, name=name,
        out_shape=tuple(pltpu.HBM(a.shape, a.dtype) for a in thru),
        in_specs=[hbm] * (2 * n) + [sem, sem, sem] + [pl.BlockSpec(memory_space=pl.ANY)] * len(afters),
        out_specs=tuple([hbm] * (2 * n)),
        input_output_aliases={i: i for i in range(2 * n)},
        compiler_params=pltpu.CompilerParams(has_side_effects=_XCHG_EFFECT),
    )(*thru, *handles[:3], *afters)
    return list(out[n:])


def _sibling_copies(land_refs, send_sems, recv_sems):
    x, y, cc = _position()

    def copy(a, q, core):
        slot = land_refs[a].at[2 * q + core]
        return pltpu.make_async_remote_copy(
            src_ref=slot, dst_ref=slot, send_sem=send_sems.at[NCHIP * a + q], recv_sem=recv_sems.at[NCHIP * a + q],
            device_id=(x, y, 1 - cc), device_id_type=_MESH)

    pairs = [(a, q) for a in range(len(land_refs)) for q in range(NCHIP)]
    return [copy(a, q, cc) for a, q in pairs], [copy(a, q, 1 - cc) for a, q in pairs]


def _sibling_forward_start(lands, name, after=None):
    n = len(lands)
    deps = [] if after is None else [after]

    def body(*refs):
        sends, _ = _sibling_copies(refs[:n], refs[n + len(deps)], refs[n + len(deps) + 1])
        for cp in sends:
            cp.start()
        refs[-1][...] = jnp.zeros_like(refs[-1])

    hbm = pl.BlockSpec(memory_space=pltpu.HBM)
    sem = pl.BlockSpec(memory_space=pltpu.SEMAPHORE)
    out = pl.pallas_call(
        body, name=name,
        out_shape=(pltpu.SemaphoreType.DMA((NCHIP * n,)), pltpu.SemaphoreType.DMA((NCHIP * n,)),
                   *[pltpu.HBM(a.shape, a.dtype) for a in lands], jax.ShapeDtypeStruct((8, LANES), F32)),
        in_specs=[hbm] * n + [pl.BlockSpec(memory_space=pl.ANY)] * len(deps),
        out_specs=(sem, sem, *[hbm] * n, pl.BlockSpec(memory_space=pltpu.VMEM)),
        input_output_aliases={i: 2 + i for i in range(n)},
        compiler_params=pltpu.CompilerParams(has_side_effects=_XCHG_EFFECT),
    )(*lands, *deps)
    return out[:-1], out[-1][0, 0]


def _sibling_forward_wait(handles, after, name):
    n = len(handles) - 2

    def body(*refs):
        sends, arrivals = _sibling_copies(refs[:n], refs[n], refs[n + 1])
        for cp in sends:
            cp.wait_send()
        for cp in arrivals:
            cp.wait_recv()

    hbm = pl.BlockSpec(memory_space=pltpu.HBM)
    sem = pl.BlockSpec(memory_space=pltpu.SEMAPHORE)
    lands = list(handles[2:])
    return list(pl.pallas_call(
        body, name=name,
        out_shape=tuple(pltpu.HBM(a.shape, a.dtype) for a in lands),
        in_specs=[hbm] * n + [sem, sem, pl.BlockSpec(memory_space=pl.ANY)],
        out_specs=tuple([hbm] * n),
        input_output_aliases={i: i for i in range(n)},
        compiler_params=pltpu.CompilerParams(has_side_effects=_XCHG_EFFECT),
    )(*lands, *handles[:2], after))


def _slab_sum(slabs, name, tr=None):
    n, r, c = slabs.shape
    tr = r if tr is None else tr

    def body(s_ref, o_ref):
        acc = s_ref[0].astype(F32)
        for q in range(1, n):
            acc = acc + s_ref[q].astype(F32)
        o_ref[...] = acc

    return pl.pallas_call(body, name=name, grid=(r // tr,),
                          in_specs=[pl.BlockSpec((n, tr, c), lambda i: (0, i, 0))],
                          out_specs=pl.BlockSpec((tr, c), lambda i: (i, 0)),
                          out_shape=jax.ShapeDtypeStruct((r, c), F32),
                          compiler_params=_cparams(dimension_semantics=("parallel",)))(slabs)


def _slab_sum_unpad(arrs, n_loc, n_pad, name, tr=256):
    n, r, c = arrs[0].shape
    runs = c // n_pad

    def body(*refs):
        o_ref = refs[-1]
        for k, s_ref in enumerate(refs[:-1]):
            acc = s_ref[0].astype(F32)
            for q in range(1, n):
                acc = acc + s_ref[q].astype(F32)
            for u in range(runs):
                o_ref[k, :, u * n_loc:(u + 1) * n_loc] = acc[:, u * n_pad:u * n_pad + n_loc]

    return pl.pallas_call(body, name=name, grid=(r // tr,),
                          in_specs=[pl.BlockSpec((n, tr, c), lambda i: (0, i, 0))] * len(arrs),
                          out_specs=pl.BlockSpec((len(arrs), tr, runs * n_loc), lambda i: (0, i, 0)),
                          out_shape=jax.ShapeDtypeStruct((len(arrs), r, runs * n_loc), F32),
                          compiler_params=_cparams(dimension_semantics=("parallel",)))(*arrs)


def _ada_fwd(c_all, ada_w, kv_ada_w, logits):
    rows, d = c_all.shape
    n0, nkv = ada_w.shape[2], kv_ada_w.shape[1]

    def body(c_ref, w_ref, kw_ref, lg_ref, part_ref, cact_ref, lb_ref):
        ca = _silu(c_ref[...])
        cact_ref[...] = ca
        part_ref[:, 0:n0] = _bdot_raw(ca, w_ref[0], _NN)
        part_ref[:, n0:2 * n0] = _bdot_raw(ca, w_ref[1], _NN)
        part_ref[:, 2 * n0:2 * n0 + nkv] = _bdot_raw(ca, kw_ref[...], _NN)
        lb_ref[...] = _sigmoid(lg_ref[0:1, :] - lg_ref[1:2, :])

    vm = pl.BlockSpec(memory_space=pltpu.VMEM)
    return pl.pallas_call(
        body, name="ada_fwd", in_specs=[vm, vm, vm, vm], out_specs=[vm, vm, vm],
        out_shape=[jax.ShapeDtypeStruct((rows, 2 * n0 + nkv), F32), jax.ShapeDtypeStruct((rows, d), F32),
                   jax.ShapeDtypeStruct((1, d), F32)],
        compiler_params=_cparams(),
    )(c_all, ada_w, kv_ada_w, logits)


def _ada_bwd(c_act, dm0, dm1, dkv, lb, dlb):
    rows, d = c_act.shape

    def body(c_ref, d0_ref, d1_ref, dk_ref, lb_ref, dlb_ref, dw_ref, dkw_ref, dlg_ref):
        ca = c_ref[...]
        dw_ref[0] = _bdot_raw(ca, d0_ref[...], _TN)
        dw_ref[1] = _bdot_raw(ca, d1_ref[...], _TN)
        dkw_ref[...] = _bdot_raw(ca, dk_ref[...], _TN)
        lbv = lb_ref[...]
        dl0 = dlb_ref[...] * lbv * (1.0 - lbv)
        dlg_ref[0:1, :] = dl0
        dlg_ref[1:2, :] = -dl0

    vm = pl.BlockSpec(memory_space=pltpu.VMEM)
    return pl.pallas_call(
        body, name="ada_bwd", in_specs=[vm] * 6, out_specs=[vm, vm, vm],
        out_shape=[jax.ShapeDtypeStruct((2, d, dm0.shape[1]), F32), jax.ShapeDtypeStruct((d, dkv.shape[1]), F32),
                   jax.ShapeDtypeStruct((2, d), F32)],
        compiler_params=_cparams(),
    )(c_act, dm0, dm1, dkv, lb, dlb)


def _adamw(w, g, m, v, name, tr=512, after=None):
    r, c = w.shape
    tr = _divisor_tile(r, tr, unit=8)
    c1 = 1.0 - ADAM_B1 ** ADAM_STEP
    c2 = 1.0 - ADAM_B2 ** ADAM_STEP
    deps = [] if after is None else [after]

    def body(w_ref, g_ref, m_ref, v_ref, *rest):
        d_ref, mo_ref, vo_ref = rest[len(deps):]
        gv = g_ref[...]
        mn = ADAM_B1 * m_ref[...] + (1.0 - ADAM_B1) * gv
        vn = ADAM_B2 * v_ref[...] + (1.0 - ADAM_B2) * (gv * gv)
        d_ref[...] = -ADAM_LR * ((mn / c1) / (jnp.sqrt(vn / c2) + ADAM_EPS) + ADAM_WD * w_ref[...])
        mo_ref[...] = mn
        vo_ref[...] = vn

    spec = pl.BlockSpec((tr, c), lambda i: (i, 0))
    out = jax.ShapeDtypeStruct((r, c), F32)
    return pl.pallas_call(body, name=name, grid=(r // tr,),
                          in_specs=[spec] * 4 + [pl.BlockSpec(a.shape, lambda i: (0, 0)) for a in deps],
                          out_specs=[spec] * 3, out_shape=[out, out, out],
                          compiler_params=_cparams(dimension_semantics=("parallel",)))(w, g, m, v, *deps)


def _pad_rows(a, rows):
    return jnp.pad(a, ((0, rows - a.shape[0]), (0, 0)))


def _pack_small(parts, lanes=LANES, row_unit=8):
    flat = jnp.concatenate([p.reshape(-1).astype(F32) for p in parts])
    rows = _round_up(-(-flat.shape[0] // lanes), row_unit)
    return jnp.pad(flat, (0, rows * lanes - flat.shape[0])).reshape(rows, lanes)


def _unpack_small(flat, shapes):
    out, off = [], 0
    for s in shapes:
        n = 1
        for k in s:
            n *= k
        out.append(flat[off:off + n].reshape(s))
        off += n
    return out


def _pad_shard_cols(a, n_loc, n_pad):
    lead, runs = a.shape[:-1], a.shape[-1] // n_loc
    a = a.reshape(*lead, runs, n_loc)
    a = jnp.pad(a, [(0, 0)] * (len(lead) + 1) + [(0, n_pad - n_loc)])
    return a.reshape(*lead, runs * n_pad)


def _unpad_shard_cols(a, n_loc, n_pad):
    lead, runs = a.shape[:-1], a.shape[-1] // n_pad
    return a.reshape(*lead, runs, n_pad)[..., :n_loc].reshape(*lead, runs * n_loc)


def kernel(x, c, ada_w, ada_b, a_w_in, a_lb_logits, a_norm_g, a_w_out, kv_ada_w, kv_ada_b, kv_w, kv_b_f, k_norm_g, b_w_q, q_norm_g, b_w_out, ffn_w_up, ffn_conv_w, ffn_conv_b, ffn_w_down, loss_target, m_ada_w, m_ada_b, m_a_w_in, m_a_lb_logits, m_a_norm_g, m_a_w_out, m_kv_ada_w, m_kv_ada_b, m_kv_w, m_kv_b_f, m_k_norm_g, m_b_w_q, m_q_norm_g, m_b_w_out, m_ffn_w_up, m_ffn_conv_w, m_ffn_conv_b, m_ffn_w_down, v_ada_w, v_ada_b, v_a_w_in, v_a_lb_logits, v_a_norm_g, v_a_w_out, v_kv_ada_w, v_kv_ada_b, v_kv_w, v_kv_b_f, v_k_norm_g, v_b_w_q, v_q_norm_g, v_b_w_out, v_ffn_w_up, v_ffn_conv_w, v_ffn_conv_b, v_ffn_w_down):
    t, d = x.shape[1], x.shape[2]
    nh = d // HEAD
    ncw = ffn_w_up.shape[2]
    rd = ffn_w_down.shape[1]
    assert ncw == 2 * rd
    rp = _round_up(rd, LANES)
    ncp = 2 * rp
    two_f = ncw * NDEV
    fp = ncp * NDEV // 2
    me = 4 * lax.axis_index("x") + 2 * lax.axis_index("y") + lax.axis_index("c")
    weights = dict(ada_w=ada_w, ada_b=ada_b, a_w_in=a_w_in, a_lb_logits=a_lb_logits, a_norm_g=a_norm_g,
                   a_w_out=a_w_out, kv_ada_w=kv_ada_w, kv_ada_b=kv_ada_b, kv_w=kv_w, kv_b_f=kv_b_f,
                   k_norm_g=k_norm_g, b_w_q=b_w_q, q_norm_g=q_norm_g, b_w_out=b_w_out, ffn_w_up=ffn_w_up,
                   ffn_conv_w=ffn_conv_w, ffn_conv_b=ffn_conv_b, ffn_w_down=ffn_w_down)
    m_in = dict(ada_w=m_ada_w, ada_b=m_ada_b, a_w_in=m_a_w_in, a_lb_logits=m_a_lb_logits, a_norm_g=m_a_norm_g,
                a_w_out=m_a_w_out, kv_ada_w=m_kv_ada_w, kv_ada_b=m_kv_ada_b, kv_w=m_kv_w, kv_b_f=m_kv_b_f,
                k_norm_g=m_k_norm_g, b_w_q=m_b_w_q, q_norm_g=m_q_norm_g, b_w_out=m_b_w_out, ffn_w_up=m_ffn_w_up,
                ffn_conv_w=m_ffn_conv_w, ffn_conv_b=m_ffn_conv_b, ffn_w_down=m_ffn_w_down)
    v_in = dict(ada_w=v_ada_w, ada_b=v_ada_b, a_w_in=v_a_w_in, a_lb_logits=v_a_lb_logits, a_norm_g=v_a_norm_g,
                a_w_out=v_a_w_out, kv_ada_w=v_kv_ada_w, kv_ada_b=v_kv_ada_b, kv_w=v_kv_w, kv_b_f=v_kv_b_f,
                k_norm_g=v_k_norm_g, b_w_q=v_b_w_q, q_norm_g=v_q_norm_g, b_w_out=v_b_w_out, ffn_w_up=v_ffn_w_up,
                ffn_conv_w=v_ffn_conv_w, ffn_conv_b=v_ffn_conv_b, ffn_w_down=v_ffn_w_down)
    order = list(weights)

    up_loc = _pad_shard_cols(ffn_w_up, rd, rp).astype(BF16)
    down_loc = jnp.pad(ffn_w_down, ((0, 0), (0, rp - rd), (0, 0))).astype(BF16)
    gather_names = {"l0b": ["a_out", "up0", "down0"], "l1": ["kv", "b_q", "b_out", "up1", "down1"]}
    forward_names = {"l0b": ["a_out"], "l0b_ffn": ["up0", "down0"], "l1": gather_names["l1"]}
    shards = {"a_out": a_w_out[0].astype(BF16), "up0": up_loc[0], "down0": down_loc[0], "kv": kv_w.T.astype(BF16),
              "b_q": b_w_q[0].astype(BF16), "b_out": b_w_out[0].astype(BF16), "up1": up_loc[1],
              "down1": down_loc[1]}
    pre = _pack_small([c, a_lb_logits, ffn_conv_w])
    in_flight = {}
    pre_flight, _ = _xchg_start([pre], False, ALL_PEERS, pre, "gather_small_inputs_start")
    (pre_all,) = _xchg_wait(pre_flight, pre, False, ALL_PEERS, "gather_small_inputs_wait")
    pre_all = pre_all.reshape(NDEV, -1)
    c_all = pre_all[:, :d]
    logits = pre_all[:, d:d + 2 * HEAD].reshape(NDEV, 2, HEAD).transpose(1, 0, 2).reshape(2, d)
    conv_w_full = pre_all[:, d + 2 * HEAD:d + 2 * HEAD + 2 * CONV_TAPS * ncw]
    conv_w_full = conv_w_full.reshape(NDEV, 2, CONV_TAPS, ncw).transpose(1, 2, 0, 3).reshape(2, CONV_TAPS, two_f)

    part, c_act, lb = _ada_fwd(_pad_rows(c_all, 2 * NDEV), ada_w, kv_ada_w, logits)
    part_flight, _ = _xchg_start([part[:NDEV]], False, ALL_PEERS, part, "gather_adaln_start")
    in_flight["l0a"], _ = _xchg_start([a_w_in[0].astype(BF16)], False, SAME_CORE, part_flight[-1], "gather_l0a_start")
    (part_all,) = _xchg_wait(part_flight, in_flight["l0a"][-1], False, ALL_PEERS, "gather_adaln_wait")
    forwarding = {}
    mine = lax.dynamic_index_in_dim(part_all, me, axis=1, keepdims=False)
    n0, nkv = ada_w.shape[2], kv_ada_w.shape[1]
    mod_names = ["sh1", "sc1", "g1", "sh2", "sc2", "g2"]
    mods = {}
    for l in range(2):
        row = mine[:, l * n0:(l + 1) * n0].reshape(-1) + ada_b[l]
        for k, nm in enumerate(mod_names):
            mods[f"{nm}_{l}"] = row[k * d:(k + 1) * d].reshape(1, d)
    kvrow = mine[:, 2 * n0:2 * n0 + nkv].reshape(-1) + kv_ada_b
    mods["kv_sh"], mods["kv_sc"] = kvrow[:d].reshape(1, d), kvrow[d:].reshape(1, d)

    def start_gather(grp, dep):
        srcs = [shards[n] for n in gather_names[grp]]
        in_flight[grp], started = _xchg_start(srcs, False, SAME_CORE, dep, f"gather_{grp}_start")
        return started

    zero = start_gather("l0b", part_all)
    mods["sh1_0"] = mods["sh1_0"] + zero

    small = {"a_norm_g": a_norm_g, "k_norm_g": k_norm_g.reshape(1, HEAD), "q_norm_g": q_norm_g, "kv_b_f": kv_b_f}
    for l in range(2):
        small[f"conv_w{l}"] = _pad_shard_cols(conv_w_full[l], rd, rp).reshape(CONV_TAPS, 2, fp).transpose(1, 0, 2)
        small[f"conv_b{l}"] = _pad_shard_cols(ffn_conv_b[l], rd, rp).reshape(2, 1, fp)

    def pre_w(grp, after):
        arrived = _xchg_wait(in_flight[grp], after, False, SAME_CORE, f"gather_{grp}_wait")
        if grp == "l0b":
            forwarding[grp], _ = _sibling_forward_start(arrived[:1], "gather_l0b_to_sibling_start")
            forwarding["l0b_ffn"], started = _sibling_forward_start(
                arrived[1:], "gather_l0b_ffn_to_sibling_start", after=forwarding[grp][-1])
            return started
        forwarding[grp], started = _sibling_forward_start(arrived, f"gather_{grp}_to_sibling_start")
        return started

    def get_w(grp, after):
        if grp == "l0a":
            arrived = _xchg_wait(in_flight["l0a"], after, False, SAME_CORE, "gather_l0a_wait")
            handles, _ = _sibling_forward_start(arrived, "gather_l0a_to_sibling_start")
            return {"a_in": _sibling_forward_wait(handles, after, "gather_l0a_to_sibling_wait")[0]}
        full = _sibling_forward_wait(forwarding[grp], after, f"gather_{grp}_to_sibling_wait")
        if grp == "l0b":
            started = start_gather("l1", full[0])
            full[0] = full[0] + started.astype(full[0].dtype)
        got = dict(zip(forward_names[grp], full))
        out = {}
        for n, a in got.items():
            if n in ("a_out", "b_out"):
                out[n] = a.reshape(d, d)
            elif n in ("down0", "down1"):
                out[n] = a.reshape(fp, d)
            elif n == "kv":
                kv_t = a.reshape(NDEV * kv_w.shape[1], d)
                out["kv_k"], out["kv_v"] = kv_t[:d], kv_t[d:2 * d]
                out["kv_f"] = jnp.pad(kv_t[2 * d:], ((0, LANES - nh), (0, 0)))
            else:
                out[n] = a
        return out

    scatter_flight, g_last = {}, {}

    def put_g(grp, gr):
        if grp == "l0a":
            g_last.update(gr)
            return zero
        if grp == "l1":
            g_kvw = jnp.concatenate([gr["kv_k"], gr["kv_v"], gr["kv_f"][:nh].astype(BF16)], axis=0)
            arrs = {"kv_w": g_kvw.reshape(NDEV, kv_w.shape[1], d), "b_w_q": gr["b_q"],
                    "b_w_out": gr["b_out"].reshape(NDEV, d // NDEV, d), "up1": gr["up1"],
                    "down1": gr["down1"].reshape(NDEV, rp, d)}
        else:
            arrs = {"a_w_out": gr["a_out"].reshape(NDEV, d // NDEV, d), "up0": gr["up0"],
                    "down0": gr["down0"].reshape(NDEV, rp, d)}
        srcs = list(arrs.values())
        handles, sent = _xchg_start(srcs, True, ALL_PEERS, srcs[0], f"scatter_{grp}_start")
        scatter_flight[grp] = (list(arrs), handles)
        return sent

    loss_v, grad_x, dmods, dlb, g = _local_step(x[0], loss_target[0], mods, lb, small, pre_w, get_w, put_g)

    g_sum, landed_up = {}, {}
    for grp in ("l1", "l0b"):
        names, handles = scatter_flight[grp]
        for nm, a in zip(names, _xchg_wait(handles, grad_x, True, ALL_PEERS, f"scatter_{grp}_wait")):
            if nm in ("up0", "up1"):
                landed_up[nm] = a
            else:
                g_sum[nm] = _slab_sum(a, f"rs_slab_sum_{nm}")

    def conv_w_grad(a):
        return _unpad_shard_cols(a.transpose(1, 0, 2).reshape(CONV_TAPS, 2 * fp), rd, rp)

    def conv_b_grad(a):
        return _unpad_shard_cols(a.reshape(2 * fp), rd, rp)

    dmod_vec = [dmods[f"{nm}_{l}"] for l in range(2) for nm in mod_names] + [dmods["kv_sh"], dmods["kv_sc"]]
    post = _pack_small(dmod_vec + [dlb, g["a_norm_g"], g["k_norm_g"], g["q_norm_g"],
                                   jnp.pad(g["kv_b_f"].reshape(-1), (0, LANES - nh)),
                                   conv_w_grad(g["conv_w0"]), conv_w_grad(g["conv_w1"]),
                                   conv_b_grad(g["conv_b0"]), conv_b_grad(g["conv_b1"]), loss_v])
    post_flight, _ = _xchg_start([post], False, ALL_PEERS, post, "gather_small_grads_start")
    a_in_flight, a_in_sent = _xchg_start([g_last["a_in"]], True, ALL_PEERS, post_flight[-1], "scatter_l0a_start")
    a_in_sent = a_in_sent.reshape(1, 1)
    grads = {
        "a_w_out": g_sum["a_w_out"].reshape(a_w_out.shape),
        "kv_w": g_sum["kv_w"].T,
        "b_w_q": g_sum["b_w_q"].reshape(b_w_q.shape),
        "b_w_out": g_sum["b_w_out"].reshape(b_w_out.shape),
        "ffn_w_up": _slab_sum_unpad([landed_up["up0"], landed_up["up1"]], rd, rp, "rs_slab_sum_up"),
        "ffn_w_down": jnp.stack([g_sum["down0"][:rd], g_sum["down1"][:rd]]),
    }
    delta, new_m, new_v = {}, {}, {}

    def adamw_matrix(n):
        shp = weights[n].shape
        two_d = lambda a: a.reshape(-1, shp[-1])
        dl, mn, vn = _adamw(two_d(weights[n]), two_d(grads[n]), two_d(m_in[n]), two_d(v_in[n]), f"adamw_{n}",
                            after=a_in_sent)
        delta[n], new_m[n], new_v[n] = dl.reshape(shp), mn.reshape(shp), vn.reshape(shp)

    for n in grads:
        adamw_matrix(n)
    (post_all,) = _xchg_wait(post_flight, [new_v[n] for n in grads], False, ALL_PEERS, "gather_small_grads_wait")
    tot = _slab_sum(post_all, "small_grad_sum").reshape(-1)
    nmod = 14 * d
    (t_mod, t_lb, t_ang, t_kng, t_qng, t_bf, t_cw, t_cb, t_loss) = _unpack_small(
        tot, [(nmod,), (1, d), (1, HEAD), (HEAD,), (1, HEAD), (LANES,), (2, CONV_TAPS, two_f), (2, two_f),
              (LANES,)])
    loss = t_loss[0]
    dm_all = post_all.reshape(NDEV, -1)[:, :nmod]
    dm0 = lax.dynamic_slice_in_dim(dm_all[:, :6 * d], me * n0, n0, axis=1)
    dm1 = lax.dynamic_slice_in_dim(dm_all[:, 6 * d:12 * d], me * n0, n0, axis=1)
    dkv = lax.dynamic_slice_in_dim(dm_all[:, 12 * d:], me * nkv, nkv, axis=1)
    g_ada_w, g_kv_ada_w, g_logits = _ada_bwd(c_act, _pad_rows(dm0, 2 * NDEV), _pad_rows(dm1, 2 * NDEV),
                                              _pad_rows(dkv, 2 * NDEV), lb, t_lb)

    grads.update({
        "ada_w": g_ada_w,
        "ada_b": t_mod[:12 * d].reshape(2, 6 * d),
        "a_lb_logits": lax.dynamic_slice_in_dim(g_logits, me * HEAD, HEAD, axis=1),
        "a_norm_g": t_ang,
        "kv_ada_w": g_kv_ada_w,
        "kv_ada_b": t_mod[12 * d:],
        "kv_b_f": t_bf[:nh],
        "k_norm_g": t_kng,
        "q_norm_g": t_qng,
        "ffn_conv_w": lax.dynamic_slice_in_dim(t_cw, me * ncw, ncw, axis=2),
        "ffn_conv_b": t_cb,
    })

    small_adam = [n for n in order if n not in delta and n not in ("ada_w", "kv_ada_w", "a_w_in")]
    packs = [_pack_small([src[n] for n in small_adam]) for src in (weights, grads, m_in, v_in)]
    outs = _adamw(*packs, "adamw_small", tr=packs[0].shape[0])
    shapes = [weights[n].shape for n in small_adam]
    for dst, o in zip((delta, new_m, new_v), outs):
        for n, a in zip(small_adam, _unpack_small(o.reshape(-1), shapes)):
            dst[n] = a
    adamw_matrix("ada_w")
    adamw_matrix("kv_ada_w")
    (landed,) = _xchg_wait(a_in_flight, new_v["kv_ada_w"], True, ALL_PEERS, "scatter_l0a_wait")
    grads["a_w_in"] = _slab_sum(landed, "rs_slab_sum_a_w_in").reshape(a_w_in.shape)
    adamw_matrix("a_w_in")

    return (loss, grad_x.reshape(x.shape), *[grads[n] for n in order], *[delta[n] for n in order],
            *[new_m[n] for n in order], *[new_v[n] for n in order])
```

```python
import functools

import jax
import jax.numpy as jnp
from jax import lax
from jax.experimental import pallas as pl
from jax.experimental.pallas import tpu as pltpu

F32 = jnp.float32
BF16 = jnp.bfloat16

NDEV = 8
NCHIP = 4
HEAD = 128
A_CHUNK = 64
CONV_TAPS = 3
EPS = 1e-6
NEG_INF = -1e30
LANES = 128
VMEM_LIMIT = 48 * 1024 * 1024

ADAM_LR = 0.001
ADAM_B1 = 0.9
ADAM_B2 = 0.999
ADAM_EPS = 1e-08
ADAM_WD = 0.01
ADAM_STEP = 10

_NN = (((1,), (0,)), ((), ()))
_NT = (((1,), (1,)), ((), ()))
_TN = (((0,), (0,)), ((), ()))
_MESH = pl.DeviceIdType.MESH


def _cparams(**kw):
    return pltpu.CompilerParams(vmem_limit_bytes=VMEM_LIMIT, **kw)


def _divisor_tile(n, pref, unit=LANES):
    if n <= pref:
        return n
    best = None
    for t in range(unit, pref + 1, unit):
        if n % t == 0:
            best = t
    assert best is not None, (n, pref)
    return best


def _round_up(n, unit):
    return -(-n // unit) * unit


def _bdot_raw(a, b, dims):
    return lax.dot_general(a.astype(BF16), b.astype(BF16), dims, preferred_element_type=F32)


@jax.custom_vjp
def _dot_nn(a, b):
    return _bdot_raw(a, b, _NN)


@jax.custom_vjp
def _dot_nt(a, b):
    return _bdot_raw(a, b, _NT)


@jax.custom_vjp
def _dot_tn(a, b):
    return _bdot_raw(a, b, _TN)


_dot_nn.defvjp(lambda a, b: (_bdot_raw(a, b, _NN), (a, b)),
               lambda r, g: (_dot_nt(g, r[1]), _dot_tn(r[0], g)))
_dot_nt.defvjp(lambda a, b: (_bdot_raw(a, b, _NT), (a, b)),
               lambda r, g: (_dot_nn(g, r[1]), _dot_tn(g, r[0])))
_dot_tn.defvjp(lambda a, b: (_bdot_raw(a, b, _TN), (a, b)),
               lambda r, g: (_dot_nt(r[1], g), _dot_nn(r[0], g)))


def _f32dot(a, b):
    return lax.dot_general(a, b, _NN, precision=lax.Precision.HIGHEST, preferred_element_type=F32)


def _sigmoid(x):
    return jax.nn.sigmoid(x)


def _silu(x):
    return x * jax.nn.sigmoid(x)


def _rms(x):
    return x * lax.rsqrt(jnp.mean(x * x, axis=-1, keepdims=True) + EPS)


def _modulate(x, sh, sc):
    return _rms(x) * (1.0 + sc) + sh


def _mm_call(a, b, dims, a_spec, b_spec, o_spec, o_shape, grid, acc_tile, name):
    nk = grid[2]

    def body(a_ref, b_ref, o_ref, *acc):
        p = lax.dot_general(a_ref[...].astype(BF16), b_ref[...].astype(BF16), dims,
                            preferred_element_type=F32)
        if nk == 1:
            o_ref[...] = p.astype(o_ref.dtype)
        else:
            kk = pl.program_id(2)

            @pl.when(kk == 0)
            def _():
                acc[0][...] = p

            @pl.when(kk > 0)
            def _():
                acc[0][...] += p

            @pl.when(kk == nk - 1)
            def _():
                o_ref[...] = acc[0][...].astype(o_ref.dtype)

    return pl.pallas_call(
        body, name=name, grid=grid, in_specs=[a_spec, b_spec], out_specs=o_spec, out_shape=o_shape,
        scratch_shapes=[pltpu.VMEM(acc_tile, F32)] if nk > 1 else [],
        compiler_params=_cparams(dimension_semantics=("parallel", "parallel", "arbitrary")),
    )(a, b)


def _mm(a, b, mode, out_dtype, name, tm=1024, tn=1024, tk=2048):
    if mode == "nn":
        (m, k), (k2, n) = a.shape, b.shape
    elif mode == "nt":
        (m, k), (n, k2) = a.shape, b.shape
    else:
        (k, m), (k2, n) = a.shape, b.shape
    assert k == k2, (a.shape, b.shape, mode)
    tm, tn, tk = _divisor_tile(m, tm), _divisor_tile(n, tn), _divisor_tile(k, tk)
    if mode == "tn":
        a_spec = pl.BlockSpec((tk, tm), lambda i, j, kk: (kk, i))
    else:
        a_spec = pl.BlockSpec((tm, tk), lambda i, j, kk: (i, kk))
    if mode == "nt":
        b_spec = pl.BlockSpec((tn, tk), lambda i, j, kk: (j, kk))
    else:
        b_spec = pl.BlockSpec((tk, tn), lambda i, j, kk: (kk, j))
    return _mm_call(a, b, {"nn": _NN, "nt": _NT, "tn": _TN}[mode], a_spec, b_spec,
                    pl.BlockSpec((tm, tn), lambda i, j, kk: (i, j)), jax.ShapeDtypeStruct((m, n), out_dtype),
                    (m // tm, n // tn, k // tk), (tm, tn), name)


def _wblk_act_spec(rows, gb, nl, split, nb, row_axis, blk_axis):
    if split == 1:
        return pl.BlockSpec((rows, gb * nl), lambda *g: (g[row_axis], g[blk_axis]))
    groups = nb // split // gb
    return pl.BlockSpec((None, rows, gb * nl),
                        lambda *g: (g[blk_axis] // groups, g[row_axis], g[blk_axis] % groups))


def _mm_wblk(a, wb, out_dtype, name, *, gb, row_off=0, split=1, tm=1024):
    m, k = a.shape
    nb, _, nl = wb.shape
    assert (nb // split) % gb == 0
    tm = _divisor_tile(m, tm)

    def body(a_ref, b_ref, o_ref):
        av = a_ref[...].astype(BF16)
        for s in range(gb):
            o_ref[:, s * nl:(s + 1) * nl] = lax.dot_general(
                av, b_ref[s].astype(BF16), _NN, preferred_element_type=F32).astype(o_ref.dtype)

    o_shape = (m, nb * nl) if split == 1 else (split, m, nb // split * nl)
    return pl.pallas_call(
        body, name=name, grid=(nb // gb, m // tm),
        in_specs=[pl.BlockSpec((tm, k), lambda j, i: (i, 0)),
                  pl.BlockSpec((gb, k, nl), lambda j, i: (j, row_off, 0))],
        out_specs=_wblk_act_spec(tm, gb, nl, split, nb, 1, 0),
        out_shape=jax.ShapeDtypeStruct(o_shape, out_dtype),
        compiler_params=_cparams(dimension_semantics=("parallel", "parallel")),
    )(a, wb)


def _mm_wblk_dx(dy, wb, out_dtype, name, *, k, gb, row_off=0, split=1, tm=1024):
    nb, _, nl = wb.shape
    m = dy.shape[-2]
    tm = _divisor_tile(m, tm)
    nk = nb // gb
    per = nb // split
    whole = split > 1 and gb == nb
    assert whole or per % gb == 0

    def body(a_ref, b_ref, o_ref, *acc):
        p = None
        for s in range(gb):
            a_blk = a_ref[s // per, :, (s % per) * nl:(s % per + 1) * nl] if whole else a_ref[:, s * nl:(s + 1) * nl]
            q = lax.dot_general(a_blk.astype(BF16), b_ref[s].astype(BF16), _NT, preferred_element_type=F32)
            p = q if p is None else p + q
        if nk == 1:
            o_ref[...] = p.astype(o_ref.dtype)
        else:
            kk = pl.program_id(1)

            @pl.when(kk == 0)
            def _():
                acc[0][...] = p

            @pl.when(kk > 0)
            def _():
                acc[0][...] += p

            @pl.when(kk == nk - 1)
            def _():
                o_ref[...] = acc[0][...].astype(o_ref.dtype)

    return pl.pallas_call(
        body, name=name, grid=(m // tm, nk),
        in_specs=[pl.BlockSpec((split, tm, per * nl), lambda i, kk: (0, i, 0)) if whole
                  else _wblk_act_spec(tm, gb, nl, split, nb, 0, 1),
                  pl.BlockSpec((gb, k, nl), lambda i, kk: (kk, row_off, 0))],
        out_specs=pl.BlockSpec((tm, k), lambda i, kk: (i, 0)),
        out_shape=jax.ShapeDtypeStruct((m, k), out_dtype),
        scratch_shapes=[pltpu.VMEM((tm, k), F32)] if nk > 1 else [],
        compiler_params=_cparams(dimension_semantics=("parallel", "arbitrary")),
    )(dy, wb)


def _mm_wblk_dw(x, dy, name, *, nb, gb, split=1, tk=1024):
    t, k = x.shape
    assert (nb // split) % gb == 0
    nl = dy.shape[-1] * split // nb
    tk = _divisor_tile(t, tk)
    nk = t // tk

    def body(a_ref, b_ref, o_ref, *acc):
        kk = pl.program_id(1)
        av = a_ref[...].astype(BF16)
        for s in range(gb):
            p = lax.dot_general(av, b_ref[:, s * nl:(s + 1) * nl].astype(BF16), _TN, preferred_element_type=F32)
            if nk == 1:
                o_ref[s] = p.astype(o_ref.dtype)
                continue

            @pl.when(kk == 0)
            def _():
                acc[0][s] = p

            @pl.when(kk > 0)
            def _():
                acc[0][s] += p

        if nk > 1:
            @pl.when(kk == nk - 1)
            def _():
                o_ref[...] = acc[0][...].astype(o_ref.dtype)

    return pl.pallas_call(
        body, name=name, grid=(nb // gb, nk),
        in_specs=[pl.BlockSpec((tk, k), lambda j, kk: (kk, 0)), _wblk_act_spec(tk, gb, nl, split, nb, 1, 0)],
        out_specs=pl.BlockSpec((gb, k, nl), lambda j, kk: (j, 0, 0)),
        out_shape=jax.ShapeDtypeStruct((nb, k, nl), BF16),
        scratch_shapes=[pltpu.VMEM((gb, k, nl), F32)] if nk > 1 else [],
        compiler_params=_cparams(dimension_semantics=("parallel", "arbitrary")),
    )(x, dy)


def _row_specs(rows, tb, nsub):
    return [pl.BlockSpec((tb, nsub * cw), functools.partial(lambda i, off: (i, off), off=off))
            for (_, cw, off) in rows]


def _vec_specs(params):
    return [pl.BlockSpec(p.shape, lambda i: (0, 0)) for p in params]


def _row_fwd(f, rows, params, out_dtypes, *, nsub=1, tb, name):
    t = rows[0][0].shape[0]
    tb = min(tb, t)
    n_r, n_p = len(rows), len(params)
    blk = [jax.ShapeDtypeStruct((tb, cw), F32) for (_, cw, _) in rows]
    blk += [jax.ShapeDtypeStruct(p.shape, F32) for p in params]
    out_avals = jax.eval_shape(f, *blk)

    def body(*refs):
        pv = [r[...] for r in refs[n_r:n_r + n_p]]
        for s in range(nsub):
            vals = [r[:, s * cw:(s + 1) * cw].astype(F32) for r, (_, cw, _) in zip(refs[:n_r], rows)]
            outs = f(*vals, *pv)
            for o_ref, o in zip(refs[n_r + n_p:], outs):
                w = o.shape[1]
                o_ref[:, s * w:(s + 1) * w] = o.astype(o_ref.dtype)

    return pl.pallas_call(
        body, name=name,
        grid=(t // tb,),
        in_specs=_row_specs(rows, tb, nsub) + _vec_specs(params),
        out_specs=[pl.BlockSpec((tb, nsub * av.shape[1]), lambda i: (i, 0)) for av in out_avals],
        out_shape=[jax.ShapeDtypeStruct((t, nsub * av.shape[1]), dt) for av, dt in zip(out_avals, out_dtypes)],
        compiler_params=_cparams(dimension_semantics=("parallel",)),
    )(*[r[0] for r in rows], *params)


def _row_bwd(f, rows, params, cots, row_grad_dtypes, *, nsub=1, tb, name, add_to=None, cot_add=None):
    t = rows[0][0].shape[0]
    tb = min(tb, t)
    n_r, n_p, n_c = len(rows), len(params), len(cots)
    want = [j for j in range(n_r) if row_grad_dtypes[j] is not None]
    cot_add = cot_add or []
    extra = [] if add_to is None else [(add_to[1], rows[add_to[0]][1], 0)]
    n_add_to = len(extra)
    extra += [(arr, cots[ci][1], 0) for ci, arr in cot_add]

    def body(*refs):
        i = pl.program_id(0)
        r_in, p_in = refs[:n_r], refs[n_r:n_r + n_p]
        c_in = refs[n_r + n_p:n_r + n_p + n_c]
        e_in = refs[n_r + n_p + n_c:n_r + n_p + n_c + len(extra)]
        outs = refs[n_r + n_p + n_c + len(extra):]
        pv = [r[...] for r in p_in]
        psum = [None] * n_p
        for s in range(nsub):
            vals = [r[:, s * cw:(s + 1) * cw].astype(F32) for r, (_, cw, _) in zip(r_in, rows)]
            cvals = [r[:, s * cw:(s + 1) * cw].astype(F32) for r, (_, cw, _) in zip(c_in, cots)]
            for (ci, _), e_ref in zip(cot_add, e_in[n_add_to:]):
                cw = cots[ci][1]
                cvals[ci] = cvals[ci] + e_ref[:, s * cw:(s + 1) * cw].astype(F32)
            _, vjp_fn = jax.vjp(f, *vals, *pv)
            grads = vjp_fn(tuple(cvals))
            for o_ref, jr in zip(outs[:len(want)], want):
                cw = rows[jr][1]
                gr = grads[jr]
                if add_to is not None and jr == add_to[0]:
                    gr = gr + e_in[0][:, s * cw:(s + 1) * cw]
                o_ref[:, s * cw:(s + 1) * cw] = gr.astype(o_ref.dtype)
            for jp in range(n_p):
                psum[jp] = grads[n_r + jp] if psum[jp] is None else psum[jp] + grads[n_r + jp]
        for o_ref, g in zip(outs[len(want):], psum):
            @pl.when(i == 0)
            def _():
                o_ref[...] = g

            @pl.when(i > 0)
            def _():
                o_ref[...] += g

    out_specs = [pl.BlockSpec((tb, nsub * rows[jr][1]), lambda i: (i, 0)) for jr in want]
    out_shape = [jax.ShapeDtypeStruct((t, nsub * rows[jr][1]), row_grad_dtypes[jr]) for jr in want]
    out_specs += _vec_specs(params)
    out_shape += [jax.ShapeDtypeStruct(p.shape, F32) for p in params]
    res = pl.pallas_call(
        body, name=name,
        grid=(t // tb,),
        in_specs=_row_specs(rows, tb, nsub) + _vec_specs(params) + _row_specs(cots, tb, nsub)
        + _row_specs(extra, tb, nsub),
        out_specs=out_specs, out_shape=out_shape,
        compiler_params=_cparams(dimension_semantics=("arbitrary",)),
    )(*[r[0] for r in rows], *params, *[c[0] for c in cots], *[e[0] for e in extra])
    return res[:len(want)], res[len(want):]


def _f_mod(x, sh, sc):
    return (_modulate(x, sh, sc),)


def _f_res_mod(x, y, g, sh, sc):
    x1 = x + g * y
    return x1, _modulate(x1, sh, sc)


def _f_res_mod2(x, y, g, sh_a, sc_a, sh_b, sc_b):
    x1 = x + g * y
    return x1, _modulate(x1, sh_a, sc_a), _modulate(x1, sh_b, sc_b)


def _f_qnorm(p, g):
    return (_rms(p) * g * (HEAD ** -0.5),)


def _f_knorm(p, g):
    return (_rms(p) * g,)


def _f_qnorm_aug(p, g):
    lane = lax.broadcasted_iota(jnp.int32, p.shape, 1)
    return (jnp.concatenate([_rms(p) * g * (HEAD ** -0.5), jnp.where(lane < 3, 1.0, 0.0)], axis=1),)


def _f_knorm_aug(p, c0, c1, c2, g):
    lane = lax.broadcasted_iota(jnp.int32, p.shape, 1)
    aug = jnp.where(lane == 0, c0, jnp.where(lane == 1, c1, jnp.where(lane == 2, c2, 0.0)))
    return (jnp.concatenate([_rms(p) * g, aug], axis=1),)


def _split3(a):
    round_bf16 = lambda v: lax.reduce_precision(v, exponent_bits=8, mantissa_bits=7)
    hi = round_bf16(a)
    mid = round_bf16(a - hi)
    lo = round_bf16(a - hi - mid)
    return hi.astype(BF16), mid.astype(BF16), lo.astype(BF16)


def _f_outgate(o, og):
    return (o * _sigmoid(og),)


def _loss_call(x3, f, g2, target, tb):
    t, d = x3.shape
    tb = min(tb, t)

    def body(x_ref, f_ref, g_ref, t_ref, loss_ref, dx_ref, df_ref, dg_ref):
        i = pl.program_id(0)
        fv = f_ref[...]
        g = g_ref[...]
        e = x_ref[...] + g * fv - t_ref[...]
        dx = e * (1.0 / d)
        part = 0.5 * jnp.sum(jnp.sum(e * dx, axis=1, keepdims=True), axis=0, keepdims=True)
        dx_ref[...] = dx
        df_ref[...] = (g * dx).astype(df_ref.dtype)
        dg = jnp.sum(dx * fv, axis=0, keepdims=True)

        @pl.when(i == 0)
        def _():
            loss_ref[...] = jnp.broadcast_to(part, loss_ref.shape)
            dg_ref[...] = dg

        @pl.when(i > 0)
        def _():
            loss_ref[...] += jnp.broadcast_to(part, loss_ref.shape)
            dg_ref[...] += dg

    row = pl.BlockSpec((tb, d), lambda i: (i, 0))
    vec = pl.BlockSpec((1, d), lambda i: (0, 0))
    return pl.pallas_call(
        body, name="loss_head",
        grid=(t // tb,),
        in_specs=[row, row, vec, row],
        out_specs=[pl.BlockSpec((1, LANES), lambda i: (0, 0)), row, row, vec],
        out_shape=[jax.ShapeDtypeStruct((1, LANES), F32), jax.ShapeDtypeStruct((t, d), F32),
                   jax.ShapeDtypeStruct((t, d), BF16), jax.ShapeDtypeStruct((1, d), F32)],
        compiler_params=_cparams(dimension_semantics=("arbitrary",)),
    )(x3, f, g2, target)


def _hg_mask(tb):
    br = lax.broadcasted_iota(jnp.int32, (tb, tb), 0)
    bs = lax.broadcasted_iota(jnp.int32, (tb, tb), 1)
    return jnp.logical_and(br // A_CHUNK == bs // A_CHUNK, bs <= br).astype(F32)


def _hg_consts(mask):
    c = A_CHUNK
    r = lax.broadcasted_iota(jnp.int32, (c, c), 0)
    s = lax.broadcasted_iota(jnp.int32, (c, c), 1)
    return (s <= r).astype(F32), (r <= s).astype(F32), mask > 0.5


def _chunk_apply(mat, x):
    c = mat.shape[0]
    return jnp.concatenate([_f32dot(mat, x[i * c:(i + 1) * c]) for i in range(x.shape[0] // c)], axis=0)


@jax.custom_vjp
def _chunk_cumsum(x, tri, tri_t):
    return _chunk_apply(tri, x)


_chunk_cumsum.defvjp(lambda x, tri, tri_t: (_chunk_apply(tri, x), (tri, tri_t)),
                     lambda r, g: (_chunk_apply(r[1], g), jnp.zeros_like(r[0]), jnp.zeros_like(r[1])))


def _per_chunk(a, b, dims):
    return jnp.stack([_bdot_raw(a[i], b[i], dims) for i in range(a.shape[0])])


@jax.custom_vjp
def _chunk_tn(a, b):
    return _per_chunk(a, b, _TN)


@jax.custom_vjp
def _chunk_nt(a, b):
    return _per_chunk(a, b, _NT)


@jax.custom_vjp
def _chunk_nn(a, b):
    return _per_chunk(a, b, _NN)


_chunk_tn.defvjp(lambda a, b: (_per_chunk(a, b, _TN), (a, b)),
                 lambda r, g: (_chunk_nt(r[1], g), _chunk_nn(r[0], g)))
_chunk_nt.defvjp(lambda a, b: (_per_chunk(a, b, _NT), (a, b)),
                 lambda r, g: (_chunk_nn(g, r[1]), _chunk_tn(g, r[0])))
_chunk_nn.defvjp(lambda a, b: (_per_chunk(a, b, _NN), (a, b)),
                 lambda r, g: (_chunk_nt(g, r[1]), _chunk_tn(r[0], g)))


def _scan_states(decay, m, st):
    sts = []
    for i in range(m.shape[0]):
        sts.append(st)
        st = st * decay[i] + m[i]
    return jnp.stack(sts), st


@jax.custom_vjp
def _state_scan(decay, m, st):
    return _scan_states(decay, m, st)


def _state_scan_fwd(decay, m, st):
    sts, st_out = _scan_states(decay, m, st)
    return (sts, st_out), (decay, sts)


def _state_scan_bwd(res, cts):
    decay, sts = res
    d_sts, g = cts
    d_decay, d_m = [], []
    for i in range(sts.shape[0] - 1, -1, -1):
        d_m.append(g)
        d_decay.append(jnp.sum(g * sts[i], axis=0, keepdims=True))
        g = g * decay[i] + d_sts[i]
    return jnp.stack(d_decay[::-1]), jnp.stack(d_m[::-1]), g


_state_scan.defvjp(_state_scan_fwd, _state_scan_bwd)


def _hg_block(qp, fp, ip, gp, lb, ng, st, tri, tri_t, bd_causal):
    tb = qp.shape[0]
    c = A_CHUNK
    n = tb // c
    q = _silu(qp)
    fg = lb + (1.0 - lb) * _sigmoid(fp)
    logf = jnp.log(fg)
    k = 1.0 - fg
    b3 = _chunk_cumsum(logf, tri, tri_t).reshape(n, c, HEAD)
    pos = lax.broadcasted_iota(jnp.int32, (1, c, 1), 1)
    b_mid = lax.stop_gradient(jnp.sum(jnp.where(pos == c // 2, b3, 0.0), axis=1, keepdims=True))
    b_last = jnp.sum(jnp.where(pos == c - 1, b3, 0.0), axis=1, keepdims=True)
    q3, k3, v3 = q.reshape(n, c, HEAD), k.reshape(n, c, HEAD), ip.reshape(n, c, HEAD)
    scores = _dot_nt((q3 * jnp.exp(b3 - b_mid)).reshape(tb, HEAD), (k3 * jnp.exp(b_mid - b3)).reshape(tb, HEAD))
    o_intra = _dot_nn(jnp.where(bd_causal, scores, 0.0), ip)
    states, st_new = _state_scan(jnp.exp(b_last), _chunk_tn(v3, k3 * jnp.exp(b_last - b3)), st)
    o = o_intra + _chunk_nt(q3 * jnp.exp(b3), states).reshape(tb, HEAD)
    y = _rms(o) * ng * _silu(gp)
    return y, st_new


HG_HEADS = 2


def _hg_specs(tb, nh, rev_nb=None):
    wide = HG_HEADS * HEAD
    per = nh // HG_HEADS

    def row(part):
        if rev_nb is None:
            return pl.BlockSpec((tb, wide), functools.partial(lambda h, i, off: (i, off + h), off=part * per))
        return pl.BlockSpec((tb, wide),
                            functools.partial(lambda h, i, off: (rev_nb - 1 - i, off + h), off=part * per))
    return [row(0), row(1), row(2), row(3),
            pl.BlockSpec((1, wide), lambda h, i: (0, h)), pl.BlockSpec((1, HEAD), lambda h, i: (0, 0)),
            pl.BlockSpec((tb, tb), lambda h, i: (0, 0))]


def _hgrn2_fwd(proj, lb, ng, tb):
    t = proj.shape[0]
    nh = proj.shape[1] // (4 * HEAD)
    tb = min(tb, t)
    nb = t // tb
    wide = HG_HEADS * HEAD

    def body(q_ref, f_ref, i_ref, g_ref, lb_ref, ng_ref, mask_ref, y_ref, s_ref, st_ref):
        i = pl.program_id(1)

        @pl.when(i == 0)
        def _():
            st_ref[...] = jnp.zeros_like(st_ref)

        consts = _hg_consts(mask_ref[...])
        for p in range(HG_HEADS):
            cs = slice(p * HEAD, (p + 1) * HEAD)
            st = st_ref[p]
            s_ref[p, 0] = st
            y, st_new = _hg_block(q_ref[:, cs], f_ref[:, cs], i_ref[:, cs], g_ref[:, cs], lb_ref[:, cs],
                                  ng_ref[...], st, *consts)
            y_ref[:, cs] = y.astype(y_ref.dtype)
            st_ref[p] = st_new

    return pl.pallas_call(
        body, name="hgrn2_fwd",
        grid=(nh // HG_HEADS, nb),
        in_specs=_hg_specs(tb, nh),
        out_specs=[pl.BlockSpec((tb, wide), lambda h, i: (i, h)),
                   pl.BlockSpec((HG_HEADS, 1, HEAD, HEAD), lambda h, i: (h, i, 0, 0))],
        out_shape=[jax.ShapeDtypeStruct((t, nh * HEAD), BF16),
                   jax.ShapeDtypeStruct((nh, nb, HEAD, HEAD), F32)],
        scratch_shapes=[pltpu.VMEM((HG_HEADS, HEAD, HEAD), F32)],
        compiler_params=_cparams(dimension_semantics=("parallel", "arbitrary")),
    )(proj, proj, proj, proj, lb, ng, _hg_mask(tb))


def _hgrn2_bwd(proj, lb, ng, states, dy, tb):
    t = proj.shape[0]
    nh = proj.shape[1] // (4 * HEAD)
    tb = min(tb, t)
    nb = t // tb
    wide = HG_HEADS * HEAD

    def body(q_ref, f_ref, i_ref, g_ref, lb_ref, ng_ref, mask_ref, s_ref, dy_ref,
             dp_ref, dlb_ref, dng_ref, dst_ref):
        h, i = pl.program_id(0), pl.program_id(1)
        consts = _hg_consts(mask_ref[...])

        @pl.when(i == 0)
        def _():
            dst_ref[...] = jnp.zeros_like(dst_ref)
            dlb_ref[...] = jnp.zeros_like(dlb_ref)

        @pl.when(jnp.logical_and(i == 0, h == 0))
        def _():
            dng_ref[...] = jnp.zeros_like(dng_ref)

        def fn(qp, fp, ip, gp, lbx, ngx, stx):
            return _hg_block(qp, fp, ip, gp, lbx, ngx, stx, *consts)

        for p in range(HG_HEADS):
            cs = slice(p * HEAD, (p + 1) * HEAD)
            _, vjp_fn = jax.vjp(fn, q_ref[:, cs], f_ref[:, cs], i_ref[:, cs], g_ref[:, cs], lb_ref[:, cs],
                                ng_ref[...], s_ref[p, 0])
            *gparts, glb, gng, dst = vjp_fn((dy_ref[:, cs].astype(F32), dst_ref[p]))
            for part, gpart in enumerate(gparts):
                dp_ref[part, :, cs] = gpart.astype(dp_ref.dtype)
            dst_ref[p] = dst
            dlb_ref[:, cs] += glb
            dng_ref[...] += gng

    rev = lambda h, i: (nb - 1 - i, h)
    return pl.pallas_call(
        body, name="hgrn2_bwd",
        grid=(nh // HG_HEADS, nb),
        in_specs=_hg_specs(tb, nh, rev_nb=nb) + [
            pl.BlockSpec((HG_HEADS, 1, HEAD, HEAD), lambda h, i: (h, nb - 1 - i, 0, 0)),
            pl.BlockSpec((tb, wide), rev)],
        out_specs=[pl.BlockSpec((4, tb, wide), lambda h, i: (0, nb - 1 - i, h)),
                   pl.BlockSpec((1, wide), lambda h, i: (0, h)), pl.BlockSpec((1, HEAD), lambda h, i: (0, 0))],
        out_shape=[jax.ShapeDtypeStruct((4, t, nh * HEAD), BF16),
                   jax.ShapeDtypeStruct((1, nh * HEAD), F32), jax.ShapeDtypeStruct((1, HEAD), F32)],
        scratch_shapes=[pltpu.VMEM((HG_HEADS, HEAD, HEAD), F32)],
        compiler_params=_cparams(dimension_semantics=("arbitrary", "arbitrary")),
    )(proj, proj, proj, proj, lb, ng, _hg_mask(tb), states, dy)


def _fgate_consts(cb):
    r = lax.broadcasted_iota(jnp.int32, (cb, cb), 0)
    s = lax.broadcasted_iota(jnp.int32, (cb, cb), 1)
    return (r <= s).astype(F32), (r >= s).astype(F32)


def _fgate_fwd(xt, bias, cb=512):
    nh, t = xt.shape
    cb = min(cb, t)

    def body(x_ref, b_ref, o_ref):
        upper, _ = _fgate_consts(cb)
        carry = jnp.zeros((nh, 1), F32)
        for blk in range(t // cb):
            z = x_ref[:, blk * cb:(blk + 1) * cb] + b_ref[...]
            logf = jnp.minimum(z, 0.0) - jnp.log(1.0 + jnp.exp(-jnp.abs(z)))
            cs = _f32dot(logf, upper) + carry
            o_ref[:, blk * cb:(blk + 1) * cb] = cs
            carry = cs[:, cb - 1:cb]

    vm = pl.BlockSpec(memory_space=pltpu.VMEM)
    return pl.pallas_call(
        body, name="fgate_fwd", in_specs=[vm, vm], out_specs=vm,
        out_shape=jax.ShapeDtypeStruct((nh, t), F32), compiler_params=_cparams(),
    )(xt, bias)


def _fgate_bwd(xt, bias, dft, cb=512):
    nh, t = xt.shape
    cb = min(cb, t)
    nblk = t // cb

    def body(x_ref, b_ref, d_ref, dx_ref, db_ref):
        _, lower = _fgate_consts(cb)
        carry = jnp.zeros((nh, 1), F32)
        db = jnp.zeros((nh, 1), F32)
        for blk in range(nblk - 1, -1, -1):
            sl = slice(blk * cb, (blk + 1) * cb)
            dlogf = _f32dot(d_ref[:, sl], lower) + carry
            carry = dlogf[:, 0:1]
            z = x_ref[:, sl] + b_ref[...]
            dz = dlogf * (1.0 - _sigmoid(z))
            dx_ref[:, sl] = dz
            db = db + jnp.sum(dz, axis=1, keepdims=True)
        db_ref[...] = db

    vm = pl.BlockSpec(memory_space=pltpu.VMEM)
    return pl.pallas_call(
        body, name="fgate_bwd", in_specs=[vm, vm, vm], out_specs=[vm, vm],
        out_shape=[jax.ShapeDtypeStruct((nh, t), F32), jax.ShapeDtypeStruct((nh, 1), F32)],
        compiler_params=_cparams(),
    )(xt, bias, dft)


ATTN_GROUPS = 4
ATTN_FWD_HEADS = 2


def _attn_fwd(q, k, v, f_grp, blk):
    t, width = v.shape
    nh = width // HEAD
    nq = t // blk
    hpg = nh // ATTN_GROUPS

    def body(q_ref, k_ref, v_ref, fc_ref, o_ref, lse_ref):
        i = pl.program_id(0)
        tri = (lax.broadcasted_iota(jnp.int32, (blk, blk), 1) <= lax.broadcasted_iota(jnp.int32, (blk, blk), 0))
        for h0 in range(0, nh, ATTN_FWD_HEADS):
            heads = range(h0, min(h0 + ATTN_FWD_HEADS, nh))

            def tile(j, carries, masked):
                rs = pl.ds(pl.multiple_of(j * blk, blk), blk)
                out = []
                for h, (m, l, acc) in zip(heads, carries):
                    cs = slice(h * HEAD, (h + 1) * HEAD)
                    cs2 = slice(2 * h * HEAD, 2 * (h + 1) * HEAD)
                    s = _bdot_raw(q_ref[:, cs2], k_ref[rs, cs2], _NT)
                    if masked:
                        s = jnp.where(tri, s, NEG_INF)
                    m_new = jnp.maximum(m, jnp.max(s, axis=1, keepdims=True))
                    p = jnp.exp(s - m_new)
                    alpha = jnp.exp(m - m_new)
                    l_new = alpha * l + jnp.sum(p, axis=1, keepdims=True)
                    out.append((m_new, l_new, alpha * acc + _bdot_raw(p, v_ref[rs, cs], _NN)))
                return tuple(out)

            init = tuple((jnp.full((blk, 1), NEG_INF, F32), jnp.zeros((blk, 1), F32), jnp.zeros((blk, HEAD), F32))
                         for _ in heads)
            carries = lax.fori_loop(0, i, lambda j, c: tile(j, c, False), init)
            for h, (m, l, acc) in zip(heads, tile(i, carries, True)):
                o_ref[:, h * HEAD:(h + 1) * HEAD] = acc / l
                g, hh = divmod(h, hpg)
                lse_ref[g, :, hh:hh + 1] = m + jnp.log(l) + fc_ref[g, :, hh:hh + 1]

    vm = pl.BlockSpec(memory_space=pltpu.VMEM)
    stat = pl.BlockSpec((ATTN_GROUPS, blk, hpg), lambda i: (0, i, 0))
    return pl.pallas_call(
        body, name="fox_attn_fwd",
        grid=(nq,),
        in_specs=[pl.BlockSpec((blk, 2 * width), lambda i: (i, 0)), vm, vm, stat],
        out_specs=[pl.BlockSpec((blk, width), lambda i: (i, 0)), stat],
        out_shape=[jax.ShapeDtypeStruct((t, width), F32), jax.ShapeDtypeStruct((ATTN_GROUPS, t, hpg), F32)],
        compiler_params=_cparams(dimension_semantics=("parallel",)),
    )(q, k, v, f_grp)


def _outgate_bwd(o, proj_q, dz, tb):
    t, width = o.shape
    nh = width // HEAD
    hpg = nh // ATTN_GROUPS
    tb = min(tb, t)

    def body(o_ref, og_ref, dz_ref, do_ref, dog_ref, dl_ref):
        for h in range(nh):
            cs = slice(h * HEAD, (h + 1) * HEAD)
            ov = o_ref[:, cs]
            _, vjp_fn = jax.vjp(_f_outgate, ov, og_ref[:, cs])
            do, dog = vjp_fn((dz_ref[:, cs].astype(F32),))
            do = do.astype(do_ref.dtype)
            do_ref[:, cs] = do
            dog_ref[:, cs] = dog.astype(dog_ref.dtype)
            g, hh = divmod(h, hpg)
            dl_ref[g, :, hh:hh + 1] = jnp.sum(do.astype(F32) * ov, axis=1, keepdims=True)

    wide = pl.BlockSpec((tb, width), lambda i: (i, 0))
    return pl.pallas_call(body, name="out_gate_bwd", grid=(t // tb,),
                          in_specs=[wide, pl.BlockSpec((tb, width), lambda i: (i, 1)), wide],
                          out_specs=[wide, wide, pl.BlockSpec((ATTN_GROUPS, tb, hpg), lambda i: (0, i, 0))],
                          out_shape=[jax.ShapeDtypeStruct((t, width), BF16), jax.ShapeDtypeStruct((t, width), BF16),
                                     jax.ShapeDtypeStruct((ATTN_GROUPS, t, hpg), F32)],
                          compiler_params=_cparams(dimension_semantics=("parallel",)))(o, proj_q, dz)


def _qnorm_bwd(proj_q, gain, dq_n, dog, tb):
    t, width = dq_n.shape
    nh = width // HEAD
    tb = min(tb, t)

    def body(p_ref, g_ref, dq_ref, dog_ref, out_ref, dg_ref):
        i = pl.program_id(0)
        gv = g_ref[...]
        acc = None
        for h in range(nh):
            cs = slice(h * HEAD, (h + 1) * HEAD)
            _, vjp_fn = jax.vjp(_f_qnorm, p_ref[:, cs], gv)
            dp, dg = vjp_fn((dq_ref[:, cs],))
            out_ref[:, cs] = dp.astype(out_ref.dtype)
            acc = dg if acc is None else acc + dg
        out_ref[:, width:] = dog_ref[...]

        @pl.when(i == 0)
        def _():
            dg_ref[...] = acc

        @pl.when(i > 0)
        def _():
            dg_ref[...] += acc

    wide = pl.BlockSpec((tb, width), lambda i: (i, 0))
    vec = pl.BlockSpec(gain.shape, lambda i: (0, 0))
    return pl.pallas_call(body, name="q_norm_bwd", grid=(t // tb,), in_specs=[wide, vec, wide, wide],
                          out_specs=[pl.BlockSpec((tb, 2 * width), lambda i: (i, 0)), vec],
                          out_shape=[jax.ShapeDtypeStruct((t, 2 * width), BF16), jax.ShapeDtypeStruct(gain.shape, F32)],
                          compiler_params=_cparams(dimension_semantics=("arbitrary",)))(proj_q, gain, dq_n, dog)


def _attn_bwd(q, k, v, f_grp, do, lse, delta, blk):
    t, width = v.shape
    nh = width // HEAD
    nq = t // blk
    hpg = nh // ATTN_GROUPS
    gw = hpg * HEAD

    def body(q_ref, do_ref, k_ref, v_ref, fc_ref, lse_ref, dl_ref,
             dq_ref, dk_ref, dv_ref, dfc_ref, dfr_ref):
        g, j = pl.program_id(0), pl.program_id(1)
        tri = (lax.broadcasted_iota(jnp.int32, (blk, blk), 1) <= lax.broadcasted_iota(jnp.int32, (blk, blk), 0))

        @pl.when(j == 0)
        def _():
            dq_ref[...] = jnp.zeros_like(dq_ref)
            dfc_ref[...] = jnp.zeros_like(dfc_ref)

        def tile(i, carries, masked):
            rs = pl.ds(pl.multiple_of(i * blk, blk), blk)
            out = []
            for h, (dk, dv, dfs) in enumerate(carries):
                cs = slice(h * HEAD, (h + 1) * HEAD)
                cs2 = slice(2 * h * HEAD, 2 * (h + 1) * HEAD)
                csq = slice(2 * h * HEAD, (2 * h + 1) * HEAD)
                qi = q_ref[rs, csq]
                doi = do_ref[rs, cs]
                bias = fc_ref[0, rs, h:h + 1] - lse_ref[0, rs, h:h + 1]
                p = jnp.exp(_bdot_raw(q_ref[rs, cs2], k_ref[:, cs2], _NT) + bias)
                if masked:
                    p = jnp.where(tri, p, 0.0)
                ds = p * (_bdot_raw(doi, v_ref[:, cs], _NT) - dl_ref[0, rs, h:h + 1])
                dsb = ds.astype(BF16)
                dq_ref[rs, cs] += _bdot_raw(dsb, k_ref[:, csq], _NN)
                dfc_ref[0, rs, h:h + 1] += jnp.sum(ds, axis=1, keepdims=True)
                out.append((dk + _bdot_raw(dsb, qi, _TN), dv + _bdot_raw(p, doi, _TN),
                            dfs - jnp.sum(ds, axis=0, keepdims=True)))
            return tuple(out)

        init = tuple((jnp.zeros((blk, HEAD), F32), jnp.zeros((blk, HEAD), F32), jnp.zeros((1, blk), F32))
                     for _ in range(hpg))
        carries = lax.fori_loop(j + 1, nq, lambda i, c: tile(i, c, False), tile(j, init, True))
        for h, (dk, dv, dfs) in enumerate(carries):
            cs = slice(h * HEAD, (h + 1) * HEAD)
            dk_ref[:, cs] = dk
            dv_ref[:, cs] = dv.astype(dv_ref.dtype)
            dfr_ref[0, 0, h:h + 1, :] = dfs

    once = pl.Buffered(1)
    stat = pl.BlockSpec((1, t, hpg), lambda g, j: (g, 0, 0), pipeline_mode=once)
    kv_blk = pl.BlockSpec((blk, gw), lambda g, j: (j, g))
    frow = pl.BlockSpec((1, 1, hpg, blk), lambda g, j: (g, j, 0, 0))
    dq, dk, dv, dfc, dfr = pl.pallas_call(
        body, name="fox_attn_bwd",
        grid=(ATTN_GROUPS, nq),
        in_specs=[pl.BlockSpec((t, 2 * gw), lambda g, j: (0, g), pipeline_mode=once),
                  pl.BlockSpec((t, gw), lambda g, j: (0, g), pipeline_mode=once),
                  pl.BlockSpec((blk, 2 * gw), lambda g, j: (j, g)), kv_blk, stat, stat, stat],
        out_specs=[pl.BlockSpec((t, gw), lambda g, j: (0, g)), kv_blk, kv_blk,
                   pl.BlockSpec((1, t, hpg), lambda g, j: (g, 0, 0)), frow],
        out_shape=[jax.ShapeDtypeStruct((t, width), F32), jax.ShapeDtypeStruct((t, width), F32),
                   jax.ShapeDtypeStruct((t, width), BF16), jax.ShapeDtypeStruct((ATTN_GROUPS, t, hpg), F32),
                   jax.ShapeDtypeStruct((ATTN_GROUPS, nq, hpg, blk), F32)],
        compiler_params=_cparams(dimension_semantics=("parallel", "arbitrary")),
    )(q, do, k, v, f_grp, lse, delta)
    return dq, dk, dv, dfc, dfr


SUBLANES = 8


def _shift_down(u, n):
    r = pltpu.roll(u, n, 0)
    row = lax.broadcasted_iota(jnp.int32, (SUBLANES, u.shape[1]), 0)
    return jnp.concatenate([jnp.where(row < n, 0.0, r[:SUBLANES]), r[SUBLANES:]], axis=0)


def _shift_up(u, n):
    t = u.shape[0]
    r = pltpu.roll(u, t - n, 0)
    row = lax.broadcasted_iota(jnp.int32, (SUBLANES, u.shape[1]), 0)
    return jnp.concatenate([r[:t - SUBLANES], jnp.where(row >= SUBLANES - n, 0.0, r[t - SUBLANES:])], axis=0)


CONV_FWD_COLS = 2 * LANES
CONV_BWD_COLS = LANES


def _convglu_specs(t, cols):
    return [pl.BlockSpec((2, t, cols), lambda j: (0, 0, j)),
            pl.BlockSpec((2, CONV_TAPS, cols), lambda j: (0, 0, j)),
            pl.BlockSpec((2, 1, cols), lambda j: (0, 0, j))]


def _lane_blocks(cols):
    return [slice(k * LANES, (k + 1) * LANES) for k in range(cols // LANES)]


CONV_RING = 3


def _convglu_fwd(u, cw, cb):
    _, t, fp = u.shape
    cols = CONV_FWD_COLS
    n = fp // cols
    ahead = CONV_RING - 1
    assert n >= ahead

    def body(u_hbm, w_ref, b_ref, a_ref, c_ref, ring, sems):
        j = pl.program_id(0)

        def fetch(step):
            start = step * cols if isinstance(step, int) else pl.multiple_of(step * cols, cols)
            slot = step % CONV_RING if isinstance(step, int) else lax.rem(step, CONV_RING)
            return pltpu.make_async_copy(u_hbm.at[:, :, pl.ds(start, cols)], ring.at[slot], sems.at[slot])

        @pl.when(j == 0)
        def _():
            for s in range(ahead):
                fetch(s).start()

        @pl.when(j + ahead < n)
        def _():
            fetch(j + ahead).start()

        fetch(j).wait()
        slot = lax.rem(j, CONV_RING)
        for cs in _lane_blocks(cols):
            c = []
            for hf in range(2):
                uv, w = ring[slot, hf, :, cs].astype(F32), w_ref[hf, :, cs]
                c.append(w[0:1] * _shift_down(uv, 2) + w[1:2] * _shift_down(uv, 1) + w[2:3] * uv + b_ref[hf, :, cs])
                c_ref[hf, :, cs] = c[hf].astype(c_ref.dtype)
            a_ref[:, cs] = (_silu(c[0]) * c[1]).astype(a_ref.dtype)

    _, taps, bias = _convglu_specs(t, cols)
    return pl.pallas_call(
        body, name="convglu_fwd",
        grid=(n,),
        in_specs=[pl.BlockSpec(memory_space=pl.ANY), taps, bias],
        out_specs=[pl.BlockSpec((t, cols), lambda j: (0, j)), pl.BlockSpec((2, t, cols), lambda j: (0, 0, j))],
        out_shape=[jax.ShapeDtypeStruct((t, fp), BF16), jax.ShapeDtypeStruct((2, t, fp), BF16)],
        scratch_shapes=[pltpu.VMEM((CONV_RING, 2, t, cols), BF16), pltpu.SemaphoreType.DMA((CONV_RING,))],
        compiler_params=_cparams(dimension_semantics=("arbitrary",)),
    )(u, cw, cb)


def _convglu_bwd(u, c, cw, da):
    _, t, fp = u.shape

    def body(u_ref, c_ref, w_ref, da_ref, du_ref, dw_ref, db_ref):
        for cs in _lane_blocks(CONV_BWD_COLS):
            gc, vc = c_ref[0, :, cs].astype(F32), c_ref[1, :, cs].astype(F32)
            sg = _sigmoid(gc)
            dav = da_ref[:, cs].astype(F32)
            dcs = [dav * vc * (sg * (1.0 + gc * (1.0 - sg))), dav * (gc * sg)]
            for hf in range(2):
                dc, w, uv = dcs[hf], w_ref[hf, :, cs], u_ref[hf, :, cs].astype(F32)
                dc1, dc2 = _shift_up(dc, 1), _shift_up(dc, 2)
                du_ref[hf, :, cs] = (w[2:3] * dc + w[1:2] * dc1 + w[0:1] * dc2).astype(du_ref.dtype)
                dw_ref[hf, 0:1, cs] = jnp.sum(dc2 * uv, axis=0, keepdims=True)
                dw_ref[hf, 1:2, cs] = jnp.sum(dc1 * uv, axis=0, keepdims=True)
                dw_ref[hf, 2:3, cs] = jnp.sum(dc * uv, axis=0, keepdims=True)
                db_ref[hf, :, cs] = jnp.sum(dc, axis=0, keepdims=True)

    pair, taps, bias = _convglu_specs(t, CONV_BWD_COLS)
    return pl.pallas_call(
        body, name="convglu_bwd",
        grid=(fp // CONV_BWD_COLS,),
        in_specs=[pair, pair, taps, pl.BlockSpec((t, CONV_BWD_COLS), lambda j: (0, j))],
        out_specs=[pair, taps, bias],
        out_shape=[jax.ShapeDtypeStruct((2, t, fp), BF16), jax.ShapeDtypeStruct((2, CONV_TAPS, fp), F32),
                   jax.ShapeDtypeStruct((2, 1, fp), F32)],
        compiler_params=_cparams(dimension_semantics=("parallel",)),
    )(u, c, cw, da)


def _local_step(x, target, mods, lb, small, pre_w, get_w, put_g, *, tb=512, attn_blk=512):
    t, d = x.shape
    nh = d // HEAD
    nb = NDEV
    wts = {}
    vec = lambda *names: [mods[n] for n in names]

    def ffn_fwd(h2, l):
        u = _mm_wblk(h2, wts[f"up{l}"], BF16, f"ffn{l}_up", gb=nb // 2, split=2, tm=512)
        a, c = _convglu_fwd(u, small[f"conv_w{l}"], small[f"conv_b{l}"])
        f = _mm(a, wts[f"down{l}"], "nn", F32, f"ffn{l}_down", tk=4096)
        return (u, c), a, f

    def ffn_bwd(df, h2, uc, a, l):
        u, c = uc
        da = _mm(df, wts[f"down{l}"], "nt", BF16, f"ffn{l}_down_dx", tn=1536)
        dwd = _mm(a, df, "tn", BF16, f"ffn{l}_down_dw", tm=768, tk=t)
        du, dcw, dcb = _convglu_bwd(u, c, small[f"conv_w{l}"], da)
        dh2 = _mm_wblk_dx(du, wts[f"up{l}"], BF16, f"ffn{l}_up_dx", k=d, gb=nb // 2, split=2, tm=1024)
        dwu = _mm_wblk_dw(h2, du, f"ffn{l}_up_dw", nb=nb, gb=1, split=2, tk=t)
        return dh2, dwu, dwd, dcw, dcb

    (h_a,) = _row_fwd(_f_mod, [(x, d, 0)], vec("sh1_0", "sc1_0"), [BF16], tb=tb, name="l0_mod1")
    wts.update(get_w("l0a", h_a))
    proj_a = _mm_wblk(h_a, wts["a_in"], F32, "a_in", gb=nb // 2)
    ypre, states = _hgrn2_fwd(proj_a, lb, small["a_norm_g"], tb)
    pre_w("l0b", ypre)
    wts.update(get_w("l0b", ypre))
    y_a = _mm(ypre, wts["a_out"], "nn", F32, "a_out")
    x1, h2_0 = _row_fwd(_f_res_mod, [(x, d, 0), (y_a, d, 0)], vec("g1_0", "sh2_0", "sc2_0"), [F32, BF16],
                        tb=tb, name="l0_res_mod2")
    wts.update(get_w("l0b_ffn", h2_0))
    u0, a0, f0 = ffn_fwd(h2_0, 0)
    x2, h_kv, h_q = _row_fwd(_f_res_mod2, [(x1, d, 0), (f0, d, 0)],
                             [mods["g2_0"] + pre_w("l1", f0)] + vec("kv_sh", "kv_sc", "sh1_1", "sc1_1"),
                             [F32, BF16, BF16], tb=tb, name="l0_res_kvmod_qmod")
    wts.update(get_w("l1", h_kv))
    proj_k = _mm(h_kv, wts["kv_k"], "nt", F32, "k_proj")
    v_b = _mm(h_kv, wts["kv_v"], "nt", BF16, "v_proj")
    proj_f = _mm(h_kv, wts["kv_f"], "nt", F32, "kv_fproj")
    f_logit_t = proj_f[:, :nh].T
    f_bias = small["kv_b_f"].reshape(nh, 1)
    f_t = _fgate_fwd(f_logit_t, f_bias)
    f_grp = f_t.reshape(ATTN_GROUPS, nh // ATTN_GROUPS, t).transpose(0, 2, 1)
    (k_n,) = _row_fwd(_f_knorm_aug, [(proj_k, HEAD, 0)] + [(piece, 1, 0) for piece in _split3(-f_t.T)],
                      [small["k_norm_g"]], [BF16], nsub=nh, tb=tb, name="k_norm")
    proj_q = _mm_wblk(h_q, wts["b_q"], F32, "b_q", gb=nb)
    (q_n,) = _row_fwd(_f_qnorm_aug, [(proj_q, HEAD, 0)], [small["q_norm_g"]], [BF16], nsub=nh, tb=tb,
                      name="q_norm")
    o_att, lse = _attn_fwd(q_n, k_n, v_b, f_grp, attn_blk)
    (z,) = _row_fwd(_f_outgate, [(o_att, HEAD, 0), (proj_q, HEAD, 1)], [], [BF16], nsub=nh, tb=tb, name="out_gate")
    y_b = _mm(z, wts["b_out"], "nn", F32, "b_out")
    x3, h2_1 = _row_fwd(_f_res_mod, [(x2, d, 0), (y_b, d, 0)], vec("g1_1", "sh2_1", "sc2_1"), [F32, BF16],
                        tb=tb, name="l1_res_mod2")
    u1, a1, f1 = ffn_fwd(h2_1, 1)
    loss, dx4, df1, dg2_1 = _loss_call(x3, f1, mods["g2_1"], target, tb)

    g = {}
    dmods = {"g2_1": dg2_1}
    dh2, g["up1"], g["down1"], g["conv_w1"], g["conv_b1"] = ffn_bwd(df1, h2_1, u1, a1, 1)
    (dx2, dy_b), (dmods["g1_1"], dmods["sh2_1"], dmods["sc2_1"]) = _row_bwd(
        _f_res_mod, [(x2, d, 0), (y_b, d, 0)], vec("g1_1", "sh2_1", "sc2_1"),
        [(dx4, d, 0), (dh2, d, 0)], [F32, BF16], tb=tb, name="l1_res_mod2_bwd")
    dz = _mm(dy_b, wts["b_out"], "nt", BF16, "b_out_dx")
    g["b_out"] = _mm(z, dy_b, "tn", BF16, "b_out_dw", tk=t)
    do_att, dog, delta = _outgate_bwd(o_att, proj_q, dz, tb)
    dq_n, dk_n, dv, dfc_q, dfr_k = _attn_bwd(q_n, k_n, v_b, f_grp, do_att, lse, delta, attn_blk)
    dproj_q, g["q_norm_g"] = _qnorm_bwd(proj_q, small["q_norm_g"], dq_n, dog, tb)
    dh_q = _mm_wblk_dx(dproj_q, wts["b_q"], BF16, "b_q_dx", k=d, gb=nb)
    g["b_q"] = _mm_wblk_dw(h_q, dproj_q, "b_q_dw", nb=nb, gb=nb // 4, tk=t)
    (dpk,), (g["k_norm_g"],) = _row_bwd(_f_knorm, [(proj_k, HEAD, 0)], [small["k_norm_g"]],
                                        [(dk_n, HEAD, 0)], [BF16], nsub=nh, tb=tb, name="k_norm_bwd")
    df_t = dfc_q.transpose(0, 2, 1).reshape(nh, t) + dfr_k.transpose(0, 2, 1, 3).reshape(nh, t)
    dflogit_t, g["kv_b_f"] = _fgate_bwd(f_logit_t, f_bias, df_t)
    dproj_f = jnp.pad(dflogit_t.T, ((0, 0), (0, LANES - nh))).astype(BF16)
    dh_kv = _mm(dpk, wts["kv_k"], "nn", BF16, "k_proj_dx")
    dh_kv_v = _mm(dv, wts["kv_v"], "nn", BF16, "v_proj_dx")
    dh_kv_f = _mm(dproj_f, wts["kv_f"], "nn", BF16, "kv_fproj_dx")
    g["kv_k"] = _mm(dpk, h_kv, "tn", BF16, "k_proj_dw", tk=t)
    g["kv_v"] = _mm(dv, h_kv, "tn", BF16, "v_proj_dw", tk=t)
    g["kv_f"] = _mm(dproj_f, h_kv, "tn", F32, "kv_fproj_dw", tk=1024)
    sent = put_g("l1", {n: g.pop(n) for n in ("b_out", "b_q", "kv_k", "kv_v", "kv_f", "up1", "down1")})
    (dx1, df0), (dmods["g2_0"], dmods["kv_sh"], dmods["kv_sc"], dmods["sh1_1"], dmods["sc1_1"]) = _row_bwd(
        _f_res_mod2, [(x1, d, 0), (f0, d, 0)], [mods["g2_0"] + sent] + vec("kv_sh", "kv_sc", "sh1_1", "sc1_1"),
        [(dx2, d, 0), (dh_kv, d, 0), (dh_q, d, 0)], [F32, BF16], tb=tb, name="l0_res_kvmod_qmod_bwd",
        cot_add=[(1, dh_kv_v), (1, dh_kv_f)])
    dh2, g["up0"], g["down0"], g["conv_w0"], g["conv_b0"] = ffn_bwd(df0, h2_0, u0, a0, 0)
    (dx0, dy_a), (dmods["g1_0"], dmods["sh2_0"], dmods["sc2_0"]) = _row_bwd(
        _f_res_mod, [(x, d, 0), (y_a, d, 0)], vec("g1_0", "sh2_0", "sc2_0"),
        [(dx1, d, 0), (dh2, d, 0)], [F32, BF16], tb=tb, name="l0_res_mod2_bwd")
    dypre = _mm(dy_a, wts["a_out"], "nt", BF16, "a_out_dx")
    g["a_out"] = _mm(ypre, dy_a, "tn", BF16, "a_out_dw", tk=t)
    sent = put_g("l0b", {n: g.pop(n) for n in ("a_out", "up0", "down0")})
    dproj_a, dlb, g["a_norm_g"] = _hgrn2_bwd(proj_a, lb + sent, small["a_norm_g"], states, dypre, tb)
    dh_a = _mm_wblk_dx(dproj_a, wts["a_in"], BF16, "a_in_dx", k=d, gb=nb, split=4, tm=512)
    put_g("l0a", {"a_in": _mm_wblk_dw(h_a, dproj_a, "a_in_dw", nb=nb, gb=1, split=4, tk=t)})
    (grad_x,), (dmods["sh1_0"], dmods["sc1_0"]) = _row_bwd(
        _f_mod, [(x, d, 0)], vec("sh1_0", "sc1_0"), [(dh_a, d, 0)], [F32], tb=tb, name="l0_mod1_bwd",
        add_to=(0, dx0))
    return loss, grad_x, dmods, dlb, g


def _position():
    return lax.axis_index("x"), lax.axis_index("y"), lax.axis_index("c")


_XCHG_EFFECT = pltpu.SideEffectType.DATAFLOW_SIDE_EFFECTING
ALL_PEERS = (1, 2, 3, 4, 5, 6, 7)
SAME_CORE = (2, 4, 6)


def _xchg_copies(src_refs, land_refs, send_sems, recv_sems, local_sems, scatter, rels):
    x, y, cc = _position()
    me = 4 * x + 2 * y + cc
    remote, local = [], []
    for a, (src, land) in enumerate(zip(src_refs, land_refs)):
        local.append(pltpu.make_async_copy(src.at[me] if scatter else src, land.at[me], local_sems.at[a]))
        for idx, rel in enumerate(rels):
            px = 1 - x if rel & 4 else x
            py = 1 - y if rel & 2 else y
            pc = 1 - cc if rel & 1 else cc
            k = len(rels) * a + idx
            remote.append(pltpu.make_async_remote_copy(
                src_ref=src.at[4 * px + 2 * py + pc] if scatter else src, dst_ref=land.at[me],
                send_sem=send_sems.at[k], recv_sem=recv_sems.at[k], device_id=(px, py, pc), device_id_type=_MESH))
    return remote, local


def _xchg_start(srcs, scatter, rels, after, name):
    n = len(srcs)
    lands = [lax.empty(s.shape if scatter else (NDEV, *s.shape), s.dtype) for s in srcs]

    def body(*refs):
        remote, local = _xchg_copies(refs[:n], refs[n:2 * n], *refs[2 * n + 1:2 * n + 4], scatter, rels)
        for cp in local + remote:
            cp.start()
        token = refs[-1]
        token[...] = jnp.zeros_like(token)

    hbm = pl.BlockSpec(memory_space=pltpu.HBM)
    sem = pl.BlockSpec(memory_space=pltpu.SEMAPHORE)
    out = pl.pallas_call(
        body, name=name,
        out_shape=(pltpu.SemaphoreType.DMA((len(rels) * n,)), pltpu.SemaphoreType.DMA((len(rels) * n,)),
                   pltpu.SemaphoreType.DMA((n,)),
                   *[pltpu.HBM(a.shape, a.dtype) for a in srcs + lands], jax.ShapeDtypeStruct((8, LANES), F32)),
        in_specs=[hbm] * (2 * n) + [pl.BlockSpec(memory_space=pl.ANY)],
        out_specs=(sem, sem, sem, *[hbm] * (2 * n), pl.BlockSpec(memory_space=pltpu.VMEM)),
        input_output_aliases={i: 3 + i for i in range(2 * n)},
        compiler_params=pltpu.CompilerParams(has_side_effects=_XCHG_EFFECT),
    )(*[pltpu.with_memory_space_constraint(a, pltpu.HBM) for a in srcs + lands], after)
    return out[:-1], out[-1][0, 0]


def _xchg_wait(handles, after, scatter, rels, name):
    n = (len(handles) - 3) // 2

    def body(*refs):
        remote, local = _xchg_copies(refs[:n], refs[n:2 * n], *refs[2 * n:2 * n + 3], scatter, rels)
        for cp in remote:
            cp.wait_send()
            cp.wait_recv()
        for cp in local:
            cp.wait()

    hbm = pl.BlockSpec(memory_space=pltpu.HBM)
    sem = pl.BlockSpec(memory_space=pltpu.SEMAPHORE)
    thru = list(handles[3:])
    afters = list(after) if isinstance(after, (list, tuple)) else [after]
    out = pl.pallas_call(
        body, name=name,
        out_shape=tuple(pltpu.HBM(a.shape, a.dtype) for a in thru),
        in_specs=[hbm] * (2 * n) + [sem, sem, sem] + [pl.BlockSpec(memory_space=pl.ANY)] * len(afters),
        out_specs=tuple([hbm] * (2 * n)),
        input_output_aliases={i: i for i in range(2 * n)},
        compiler_params=pltpu.CompilerParams(has_side_effects=_XCHG_EFFECT),
    )(*thru, *handles[:3], *afters)
    return list(out[n:])


def _sibling_copies(land_refs, send_sems, recv_sems):
    x, y, cc = _position()

    def copy(a, q, core):
        slot = land_refs[a].at[2 * q + core]
        return pltpu.make_async_remote_copy(
            src_ref=slot, dst_ref=slot, send_sem=send_sems.at[NCHIP * a + q], recv_sem=recv_sems.at[NCHIP * a + q],
            device_id=(x, y, 1 - cc), device_id_type=_MESH)

    pairs = [(a, q) for a in range(len(land_refs)) for q in range(NCHIP)]
    return [copy(a, q, cc) for a, q in pairs], [copy(a, q, 1 - cc) for a, q in pairs]


def _sibling_forward_start(lands, name, after=None):
    n = len(lands)
    deps = [] if after is None else [after]

    def body(*refs):
        sends, _ = _sibling_copies(refs[:n], refs[n + len(deps)], refs[n + len(deps) + 1])
        for cp in sends:
            cp.start()
        refs[-1][...] = jnp.zeros_like(refs[-1])

    hbm = pl.BlockSpec(memory_space=pltpu.HBM)
    sem = pl.BlockSpec(memory_space=pltpu.SEMAPHORE)
    out = pl.pallas_call(
        body, name=name,
        out_shape=(pltpu.SemaphoreType.DMA((NCHIP * n,)), pltpu.SemaphoreType.DMA((NCHIP * n,)),
                   *[pltpu.HBM(a.shape, a.dtype) for a in lands], jax.ShapeDtypeStruct((8, LANES), F32)),
        in_specs=[hbm] * n + [pl.BlockSpec(memory_space=pl.ANY)] * len(deps),
        out_specs=(sem, sem, *[hbm] * n, pl.BlockSpec(memory_space=pltpu.VMEM)),
        input_output_aliases={i: 2 + i for i in range(n)},
        compiler_params=pltpu.CompilerParams(has_side_effects=_XCHG_EFFECT),
    )(*lands, *deps)
    return out[:-1], out[-1][0, 0]


def _sibling_forward_wait(handles, after, name):
    n = len(handles) - 2

    def body(*refs):
        sends, arrivals = _sibling_copies(refs[:n], refs[n], refs[n + 1])
        for cp in sends:
            cp.wait_send()
        for cp in arrivals:
            cp.wait_recv()

    hbm = pl.BlockSpec(memory_space=pltpu.HBM)
    sem = pl.BlockSpec(memory_space=pltpu.SEMAPHORE)
    lands = list(handles[2:])
    return list(pl.pallas_call(
        body, name=name,
        out_shape=tuple(pltpu.HBM(a.shape, a.dtype) for a in lands),
        in_specs=[hbm] * n + [sem, sem, pl.BlockSpec(memory_space=pl.ANY)],
        out_specs=tuple([hbm] * n),
        input_output_aliases={i: i for i in range(n)},
        compiler_params=pltpu.CompilerParams(has_side_effects=_XCHG_EFFECT),
    )(*lands, *handles[:2], after))


def _slab_sum(slabs, name, tr=None):
    n, r, c = slabs.shape
    tr = r if tr is None else tr

    def body(s_ref, o_ref):
        acc = s_ref[0].astype(F32)
        for q in range(1, n):
            acc = acc + s_ref[q].astype(F32)
        o_ref[...] = acc

    return pl.pallas_call(body, name=name, grid=(r // tr,),
                          in_specs=[pl.BlockSpec((n, tr, c), lambda i: (0, i, 0))],
                          out_specs=pl.BlockSpec((tr, c), lambda i: (i, 0)),
                          out_shape=jax.ShapeDtypeStruct((r, c), F32),
                          compiler_params=_cparams(dimension_semantics=("parallel",)))(slabs)


def _slab_sum_unpad(arrs, n_loc, n_pad, name, tr=256):
    n, r, c = arrs[0].shape
    runs = c // n_pad

    def body(*refs):
        o_ref = refs[-1]
        for k, s_ref in enumerate(refs[:-1]):
            acc = s_ref[0].astype(F32)
            for q in range(1, n):
                acc = acc + s_ref[q].astype(F32)
            for u in range(runs):
                o_ref[k, :, u * n_loc:(u + 1) * n_loc] = acc[:, u * n_pad:u * n_pad + n_loc]

    return pl.pallas_call(body, name=name, grid=(r // tr,),
                          in_specs=[pl.BlockSpec((n, tr, c), lambda i: (0, i, 0))] * len(arrs),
                          out_specs=pl.BlockSpec((len(arrs), tr, runs * n_loc), lambda i: (0, i, 0)),
                          out_shape=jax.ShapeDtypeStruct((len(arrs), r, runs * n_loc), F32),
                          compiler_params=_cparams(dimension_semantics=("parallel",)))(*arrs)


def _ada_fwd(c_all, ada_w, kv_ada_w, logits):
    rows, d = c_all.shape
    n0, nkv = ada_w.shape[2], kv_ada_w.shape[1]

    def body(c_ref, w_ref, kw_ref, lg_ref, part_ref, cact_ref, lb_ref):
        ca = _silu(c_ref[...])
        cact_ref[...] = ca
        part_ref[:, 0:n0] = _bdot_raw(ca, w_ref[0], _NN)
        part_ref[:, n0:2 * n0] = _bdot_raw(ca, w_ref[1], _NN)
        part_ref[:, 2 * n0:2 * n0 + nkv] = _bdot_raw(ca, kw_ref[...], _NN)
        lb_ref[...] = _sigmoid(lg_ref[0:1, :] - lg_ref[1:2, :])

    vm = pl.BlockSpec(memory_space=pltpu.VMEM)
    return pl.pallas_call(
        body, name="ada_fwd", in_specs=[vm, vm, vm, vm], out_specs=[vm, vm, vm],
        out_shape=[jax.ShapeDtypeStruct((rows, 2 * n0 + nkv), F32), jax.ShapeDtypeStruct((rows, d), F32),
                   jax.ShapeDtypeStruct((1, d), F32)],
        compiler_params=_cparams(),
    )(c_all, ada_w, kv_ada_w, logits)


def _ada_bwd(c_act, dm0, dm1, dkv, lb, dlb):
    rows, d = c_act.shape

    def body(c_ref, d0_ref, d1_ref, dk_ref, lb_ref, dlb_ref, dw_ref, dkw_ref, dlg_ref):
        ca = c_ref[...]
        dw_ref[0] = _bdot_raw(ca, d0_ref[...], _TN)
        dw_ref[1] = _bdot_raw(ca, d1_ref[...], _TN)
        dkw_ref[...] = _bdot_raw(ca, dk_ref[...], _TN)
        lbv = lb_ref[...]
        dl0 = dlb_ref[...] * lbv * (1.0 - lbv)
        dlg_ref[0:1, :] = dl0
        dlg_ref[1:2, :] = -dl0

    vm = pl.BlockSpec(memory_space=pltpu.VMEM)
    return pl.pallas_call(
        body, name="ada_bwd", in_specs=[vm] * 6, out_specs=[vm, vm, vm],
        out_shape=[jax.ShapeDtypeStruct((2, d, dm0.shape[1]), F32), jax.ShapeDtypeStruct((d, dkv.shape[1]), F32),
                   jax.ShapeDtypeStruct((2, d), F32)],
        compiler_params=_cparams(),
    )(c_act, dm0, dm1, dkv, lb, dlb)


def _adamw(w, g, m, v, name, tr=512, after=None):
    r, c = w.shape
    tr = _divisor_tile(r, tr, unit=8)
    c1 = 1.0 - ADAM_B1 ** ADAM_STEP
    c2 = 1.0 - ADAM_B2 ** ADAM_STEP
    deps = [] if after is None else [after]

    def body(w_ref, g_ref, m_ref, v_ref, *rest):
        d_ref, mo_ref, vo_ref = rest[len(deps):]
        gv = g_ref[...]
        mn = ADAM_B1 * m_ref[...] + (1.0 - ADAM_B1) * gv
        vn = ADAM_B2 * v_ref[...] + (1.0 - ADAM_B2) * (gv * gv)
        d_ref[...] = -ADAM_LR * ((mn / c1) / (jnp.sqrt(vn / c2) + ADAM_EPS) + ADAM_WD * w_ref[...])
        mo_ref[...] = mn
        vo_ref[...] = vn

    spec = pl.BlockSpec((tr, c), lambda i: (i, 0))
    out = jax.ShapeDtypeStruct((r, c), F32)
    return pl.pallas_call(body, name=name, grid=(r // tr,),
                          in_specs=[spec] * 4 + [pl.BlockSpec(a.shape, lambda i: (0, 0)) for a in deps],
                          out_specs=[spec] * 3, out_shape=[out, out, out],
                          compiler_params=_cparams(dimension_semantics=("parallel",)))(w, g, m, v, *deps)


def _pad_rows(a, rows):
    return jnp.pad(a, ((0, rows - a.shape[0]), (0, 0)))


def _pack_small(parts, lanes=LANES, row_unit=8):
    flat = jnp.concatenate([p.reshape(-1).astype(F32) for p in parts])
    rows = _round_up(-(-flat.shape[0] // lanes), row_unit)
    return jnp.pad(flat, (0, rows * lanes - flat.shape[0])).reshape(rows, lanes)


def _unpack_small(flat, shapes):
    out, off = [], 0
    for s in shapes:
        n = 1
        for k in s:
            n *= k
        out.append(flat[off:off + n].reshape(s))
        off += n
    return out


def _pad_shard_cols(a, n_loc, n_pad):
    lead, runs = a.shape[:-1], a.shape[-1] // n_loc
    a = a.reshape(*lead, runs, n_loc)
    a = jnp.pad(a, [(0, 0)] * (len(lead) + 1) + [(0, n_pad - n_loc)])
    return a.reshape(*lead, runs * n_pad)


def _unpad_shard_cols(a, n_loc, n_pad):
    lead, runs = a.shape[:-1], a.shape[-1] // n_pad
    return a.reshape(*lead, runs, n_pad)[..., :n_loc].reshape(*lead, runs * n_loc)


def kernel(x, c, ada_w, ada_b, a_w_in, a_lb_logits, a_norm_g, a_w_out, kv_ada_w, kv_ada_b, kv_w, kv_b_f, k_norm_g, b_w_q, q_norm_g, b_w_out, ffn_w_up, ffn_conv_w, ffn_conv_b, ffn_w_down, loss_target, m_ada_w, m_ada_b, m_a_w_in, m_a_lb_logits, m_a_norm_g, m_a_w_out, m_kv_ada_w, m_kv_ada_b, m_kv_w, m_kv_b_f, m_k_norm_g, m_b_w_q, m_q_norm_g, m_b_w_out, m_ffn_w_up, m_ffn_conv_w, m_ffn_conv_b, m_ffn_w_down, v_ada_w, v_ada_b, v_a_w_in, v_a_lb_logits, v_a_norm_g, v_a_w_out, v_kv_ada_w, v_kv_ada_b, v_kv_w, v_kv_b_f, v_k_norm_g, v_b_w_q, v_q_norm_g, v_b_w_out, v_ffn_w_up, v_ffn_conv_w, v_ffn_conv_b, v_ffn_w_down):
    t, d = x.shape[1], x.shape[2]
    nh = d // HEAD
    ncw = ffn_w_up.shape[2]
    rd = ffn_w_down.shape[1]
    assert ncw == 2 * rd
    rp = _round_up(rd, LANES)
    ncp = 2 * rp
    two_f = ncw * NDEV
    fp = ncp * NDEV // 2
    me = 4 * lax.axis_index("x") + 2 * lax.axis_index("y") + lax.axis_index("c")
    weights = dict(ada_w=ada_w, ada_b=ada_b, a_w_in=a_w_in, a_lb_logits=a_lb_logits, a_norm_g=a_norm_g,
                   a_w_out=a_w_out, kv_ada_w=kv_ada_w, kv_ada_b=kv_ada_b, kv_w=kv_w, kv_b_f=kv_b_f,
                   k_norm_g=k_norm_g, b_w_q=b_w_q, q_norm_g=q_norm_g, b_w_out=b_w_out, ffn_w_up=ffn_w_up,
                   ffn_conv_w=ffn_conv_w, ffn_conv_b=ffn_conv_b, ffn_w_down=ffn_w_down)
    m_in = dict(ada_w=m_ada_w, ada_b=m_ada_b, a_w_in=m_a_w_in, a_lb_logits=m_a_lb_logits, a_norm_g=m_a_norm_g,
                a_w_out=m_a_w_out, kv_ada_w=m_kv_ada_w, kv_ada_b=m_kv_ada_b, kv_w=m_kv_w, kv_b_f=m_kv_b_f,
                k_norm_g=m_k_norm_g, b_w_q=m_b_w_q, q_norm_g=m_q_norm_g, b_w_out=m_b_w_out, ffn_w_up=m_ffn_w_up,
                ffn_conv_w=m_ffn_conv_w, ffn_conv_b=m_ffn_conv_b, ffn_w_down=m_ffn_w_down)
    v_in = dict(ada_w=v_ada_w, ada_b=v_ada_b, a_w_in=v_a_w_in, a_lb_logits=v_a_lb_logits, a_norm_g=v_a_norm_g,
                a_w_out=v_a_w_out, kv_ada_w=v_kv_ada_w, kv_ada_b=v_kv_ada_b, kv_w=v_kv_w, kv_b_f=v_kv_b_f,
                k_norm_g=v_k_norm_g, b_w_q=v_b_w_q, q_norm_g=v_q_norm_g, b_w_out=v_b_w_out, ffn_w_up=v_ffn_w_up,
                ffn_conv_w=v_ffn_conv_w, ffn_conv_b=v_ffn_conv_b, ffn_w_down=v_ffn_w_down)
    order = list(weights)

    up_loc = _pad_shard_cols(ffn_w_up, rd, rp).astype(BF16)
    down_loc = jnp.pad(ffn_w_down, ((0, 0), (0, rp - rd), (0, 0))).astype(BF16)
    gather_names = {"l0b": ["a_out", "up0", "down0"], "l1": ["kv", "b_q", "b_out", "up1", "down1"]}
    forward_names = {"l0b": ["a_out"], "l0b_ffn": ["up0", "down0"], "l1": gather_names["l1"]}
    shards = {"a_out": a_w_out[0].astype(BF16), "up0": up_loc[0], "down0": down_loc[0], "kv": kv_w.T.astype(BF16),
              "b_q": b_w_q[0].astype(BF16), "b_out": b_w_out[0].astype(BF16), "up1": up_loc[1],
              "down1": down_loc[1]}
    pre = _pack_small([c, a_lb_logits, ffn_conv_w])
    in_flight = {}
    pre_flight, _ = _xchg_start([pre], False, ALL_PEERS, pre, "gather_small_inputs_start")
    (pre_all,) = _xchg_wait(pre_flight, pre, False, ALL_PEERS, "gather_small_inputs_wait")
    pre_all = pre_all.reshape(NDEV, -1)
    c_all = pre_all[:, :d]
    logits = pre_all[:, d:d + 2 * HEAD].reshape(NDEV, 2, HEAD).transpose(1, 0, 2).reshape(2, d)
    conv_w_full = pre_all[:, d + 2 * HEAD:d + 2 * HEAD + 2 * CONV_TAPS * ncw]
    conv_w_full = conv_w_full.reshape(NDEV, 2, CONV_TAPS, ncw).transpose(1, 2, 0, 3).reshape(2, CONV_TAPS, two_f)

    part, c_act, lb = _ada_fwd(_pad_rows(c_all, 2 * NDEV), ada_w, kv_ada_w, logits)
    part_flight, _ = _xchg_start([part[:NDEV]], False, ALL_PEERS, part, "gather_adaln_start")
    in_flight["l0a"], _ = _xchg_start([a_w_in[0].astype(BF16)], False, SAME_CORE, part_flight[-1], "gather_l0a_start")
    (part_all,) = _xchg_wait(part_flight, in_flight["l0a"][-1], False, ALL_PEERS, "gather_adaln_wait")
    forwarding = {}
    mine = lax.dynamic_index_in_dim(part_all, me, axis=1, keepdims=False)
    n0, nkv = ada_w.shape[2], kv_ada_w.shape[1]
    mod_names = ["sh1", "sc1", "g1", "sh2", "sc2", "g2"]
    mods = {}
    for l in range(2):
        row = mine[:, l * n0:(l + 1) * n0].reshape(-1) + ada_b[l]
        for k, nm in enumerate(mod_names):
            mods[f"{nm}_{l}"] = row[k * d:(k + 1) * d].reshape(1, d)
    kvrow = mine[:, 2 * n0:2 * n0 + nkv].reshape(-1) + kv_ada_b
    mods["kv_sh"], mods["kv_sc"] = kvrow[:d].reshape(1, d), kvrow[d:].reshape(1, d)

    def start_gather(grp, dep):
        srcs = [shards[n] for n in gather_names[grp]]
        in_flight[grp], started = _xchg_start(srcs, False, SAME_CORE, dep, f"gather_{grp}_start")
        return started

    zero = start_gather("l0b", part_all)
    mods["sh1_0"] = mods["sh1_0"] + zero

    small = {"a_norm_g": a_norm_g, "k_norm_g": k_norm_g.reshape(1, HEAD), "q_norm_g": q_norm_g, "kv_b_f": kv_b_f}
    for l in range(2):
        small[f"conv_w{l}"] = _pad_shard_cols(conv_w_full[l], rd, rp).reshape(CONV_TAPS, 2, fp).transpose(1, 0, 2)
        small[f"conv_b{l}"] = _pad_shard_cols(ffn_conv_b[l], rd, rp).reshape(2, 1, fp)

    def pre_w(grp, after):
        arrived = _xchg_wait(in_flight[grp], after, False, SAME_CORE, f"gather_{grp}_wait")
        if grp == "l0b":
            forwarding[grp], _ = _sibling_forward_start(arrived[:1], "gather_l0b_to_sibling_start")
            forwarding["l0b_ffn"], started = _sibling_forward_start(
                arrived[1:], "gather_l0b_ffn_to_sibling_start", after=forwarding[grp][-1])
            return started
        forwarding[grp], started = _sibling_forward_start(arrived, f"gather_{grp}_to_sibling_start")
        return started

    def get_w(grp, after):
        if grp == "l0a":
            arrived = _xchg_wait(in_flight["l0a"], after, False, SAME_CORE, "gather_l0a_wait")
            handles, _ = _sibling_forward_start(arrived, "gather_l0a_to_sibling_start")
            return {"a_in": _sibling_forward_wait(handles, after, "gather_l0a_to_sibling_wait")[0]}
        full = _sibling_forward_wait(forwarding[grp], after, f"gather_{grp}_to_sibling_wait")
        if grp == "l0b":
            started = start_gather("l1", full[0])
            full[0] = full[0] + started.astype(full[0].dtype)
        got = dict(zip(forward_names[grp], full))
        out = {}
        for n, a in got.items():
            if n in ("a_out", "b_out"):
                out[n] = a.reshape(d, d)
            elif n in ("down0", "down1"):
                out[n] = a.reshape(fp, d)
            elif n == "kv":
                kv_t = a.reshape(NDEV * kv_w.shape[1], d)
                out["kv_k"], out["kv_v"] = kv_t[:d], kv_t[d:2 * d]
                out["kv_f"] = jnp.pad(kv_t[2 * d:], ((0, LANES - nh), (0, 0)))
            else:
                out[n] = a
        return out

    scatter_flight, g_last = {}, {}

    def put_g(grp, gr):
        if grp == "l0a":
            g_last.update(gr)
            return zero
        if grp == "l1":
            g_kvw = jnp.concatenate([gr["kv_k"], gr["kv_v"], gr["kv_f"][:nh].astype(BF16)], axis=0)
            arrs = {"kv_w": g_kvw.reshape(NDEV, kv_w.shape[1], d), "b_w_q": gr["b_q"],
                    "b_w_out": gr["b_out"].reshape(NDEV, d // NDEV, d), "up1": gr["up1"],
                    "down1": gr["down1"].reshape(NDEV, rp, d)}
        else:
            arrs = {"a_w_out": gr["a_out"].reshape(NDEV, d // NDEV, d), "up0": gr["up0"],
                    "down0": gr["down0"].reshape(NDEV, rp, d)}
        srcs = list(arrs.values())
        handles, sent = _xchg_start(srcs, True, ALL_PEERS, srcs[0], f"scatter_{grp}_start")
        scatter_flight[grp] = (list(arrs), handles)
        return sent

    loss_v, grad_x, dmods, dlb, g = _local_step(x[0], loss_target[0], mods, lb, small, pre_w, get_w, put_g)

    g_sum, landed_up = {}, {}
    for grp in ("l1", "l0b"):
        names, handles = scatter_flight[grp]
        for nm, a in zip(names, _xchg_wait(handles, grad_x, True, ALL_PEERS, f"scatter_{grp}_wait")):
            if nm in ("up0", "up1"):
                landed_up[nm] = a
            else:
                g_sum[nm] = _slab_sum(a, f"rs_slab_sum_{nm}")

    def conv_w_grad(a):
        return _unpad_shard_cols(a.transpose(1, 0, 2).reshape(CONV_TAPS, 2 * fp), rd, rp)

    def conv_b_grad(a):
        return _unpad_shard_cols(a.reshape(2 * fp), rd, rp)

    dmod_vec = [dmods[f"{nm}_{l}"] for l in range(2) for nm in mod_names] + [dmods["kv_sh"], dmods["kv_sc"]]
    post = _pack_small(dmod_vec + [dlb, g["a_norm_g"], g["k_norm_g"], g["q_norm_g"],
                                   jnp.pad(g["kv_b_f"].reshape(-1), (0, LANES - nh)),
                                   conv_w_grad(g["conv_w0"]), conv_w_grad(g["conv_w1"]),
                                   conv_b_grad(g["conv_b0"]), conv_b_grad(g["conv_b1"]), loss_v])
    post_flight, _ = _xchg_start([post], False, ALL_PEERS, post, "gather_small_grads_start")
    a_in_flight, a_in_sent = _xchg_start([g_last["a_in"]], True, ALL_PEERS, post_flight[-1], "scatter_l0a_start")
    a_in_sent = a_in_sent.reshape(1, 1)
    grads = {
        "a_w_out": g_sum["a_w_out"].reshape(a_w_out.shape),
        "kv_w": g_sum["kv_w"].T,
        "b_w_q": g_sum["b_w_q"].reshape(b_w_q.shape),
        "b_w_out": g_sum["b_w_out"].reshape(b_w_out.shape),
        "ffn_w_up": _slab_sum_unpad([landed_up["up0"], landed_up["up1"]], rd, rp, "rs_slab_sum_up"),
        "ffn_w_down": jnp.stack([g_sum["down0"][:rd], g_sum["down1"][:rd]]),
    }
    delta, new_m, new_v = {}, {}, {}

    def adamw_matrix(n):
        shp = weights[n].shape
        two_d = lambda a: a.reshape(-1, shp[-1])
        dl, mn, vn = _adamw(two_d(weights[n]), two_d(grads[n]), two_d(m_in[n]), two_d(v_in[n]), f"adamw_{n}",
                            after=a_in_sent)
        delta[n], new_m[n], new_v[n] = dl.reshape(shp), mn.reshape(shp), vn.reshape(shp)

    for n in grads:
        adamw_matrix(n)
    (post_all,) = _xchg_wait(post_flight, [new_v[n] for n in grads], False, ALL_PEERS, "gather_small_grads_wait")
    tot = _slab_sum(post_all, "small_grad_sum").reshape(-1)
    nmod = 14 * d
    (t_mod, t_lb, t_ang, t_kng, t_qng, t_bf, t_cw, t_cb, t_loss) = _unpack_small(
        tot, [(nmod,), (1, d), (1, HEAD), (HEAD,), (1, HEAD), (LANES,), (2, CONV_TAPS, two_f), (2, two_f),
              (LANES,)])
    loss = t_loss[0]
    dm_all = post_all.reshape(NDEV, -1)[:, :nmod]
    dm0 = lax.dynamic_slice_in_dim(dm_all[:, :6 * d], me * n0, n0, axis=1)
    dm1 = lax.dynamic_slice_in_dim(dm_all[:, 6 * d:12 * d], me * n0, n0, axis=1)
    dkv = lax.dynamic_slice_in_dim(dm_all[:, 12 * d:], me * nkv, nkv, axis=1)
    g_ada_w, g_kv_ada_w, g_logits = _ada_bwd(c_act, _pad_rows(dm0, 2 * NDEV), _pad_rows(dm1, 2 * NDEV),
                                              _pad_rows(dkv, 2 * NDEV), lb, t_lb)

    grads.update({
        "ada_w": g_ada_w,
        "ada_b": t_mod[:12 * d].reshape(2, 6 * d),
        "a_lb_logits": lax.dynamic_slice_in_dim(g_logits, me * HEAD, HEAD, axis=1),
        "a_norm_g": t_ang,
        "kv_ada_w": g_kv_ada_w,
        "kv_ada_b": t_mod[12 * d:],
        "kv_b_f": t_bf[:nh],
        "k_norm_g": t_kng,
        "q_norm_g": t_qng,
        "ffn_conv_w": lax.dynamic_slice_in_dim(t_cw, me * ncw, ncw, axis=2),
        "ffn_conv_b": t_cb,
    })

    small_adam = [n for n in order if n not in delta and n not in ("ada_w", "kv_ada_w", "a_w_in")]
    packs = [_pack_small([src[n] for n in small_adam]) for src in (weights, grads, m_in, v_in)]
    outs = _adamw(*packs, "adamw_small", tr=packs[0].shape[0])
    shapes = [weights[n].shape for n in small_adam]
    for dst, o in zip((delta, new_m, new_v), outs):
        for n, a in zip(small_adam, _unpack_small(o.reshape(-1), shapes)):
            dst[n] = a
    adamw_matrix("ada_w")
    adamw_matrix("kv_ada_w")
    (landed,) = _xchg_wait(a_in_flight, new_v["kv_ada_w"], True, ALL_PEERS, "scatter_l0a_wait")
    grads["a_w_in"] = _slab_sum(landed, "rs_slab_sum_a_w_in").reshape(a_w_in.shape)
    adamw_matrix("a_w_in")

    return (loss, grad_x.reshape(x.shape), *[grads[n] for n in order], *[delta[n] for n in order],
            *[new_m[n] for n in order], *[new_v[n] for n in order])
```
